```python
import math
import jax, jax.numpy as jnp
from jax import lax
import numpy as np

D_MODEL = 1024
BATCH = 8
SEQ = 4096
DEPTH = 1

HEAD_DIM = 64
ATT_WIDTH = D_MODEL
CONV_WIDTH = D_MODEL
MIX_WIDTH = ATT_WIDTH + CONV_WIDTH
N_Q_HEADS = ATT_WIDTH // HEAD_DIM
N_KV_HEADS = 4
Q_PER_KV = N_Q_HEADS // N_KV_HEADS
KV_WIDTH = N_KV_HEADS * HEAD_DIM
N_CONV_GROUPS = CONV_WIDTH // HEAD_DIM
CONV_K = 31
DILATED_PATTERNS = ((128, 1), (512, 4), (2048, 16))
BLK = 128
NORM_EPS = 1e-6
LN_EPS = 1e-5
SPLIT_SIZES = (ATT_WIDTH, KV_WIDTH, KV_WIDTH, ATT_WIDTH, CONV_WIDTH, CONV_WIDTH, CONV_WIDTH)
IN_COLS = sum(SPLIT_SIZES)

kernel_name = "hybrid_dilated_attn_conformer_conv"


def rmsnorm(x, g):
    xf = x.astype(jnp.float32)
    y = xf * lax.rsqrt(jnp.mean(xf * xf, axis=-1, keepdims=True) + NORM_EPS)
    return (y * g.astype(jnp.float32)).astype(x.dtype)


def layernorm(x, g, b):
    xf = x.astype(jnp.float32)
    mu = jnp.mean(xf, axis=-1, keepdims=True)
    var = jnp.mean(jnp.square(xf - mu), axis=-1, keepdims=True)
    y = (xf - mu) * lax.rsqrt(var + LN_EPS)
    return (y * g.astype(jnp.float32) + b.astype(jnp.float32)).astype(x.dtype)


def alibi_slopes(n):
    return jnp.exp2(-8.0 * (jnp.arange(n, dtype=jnp.float32) + 1.0) / n)


def _to_blocks(t, dilation, n_blocks):
    b, s = t.shape[:2]
    rest = t.shape[2:]
    sub_len = s // dilation
    t = t.reshape((b, sub_len, dilation) + rest)
    t = jnp.moveaxis(t, 2, 1)
    t = jnp.pad(t, [(0, 0), (0, 0), (0, n_blocks * BLK - sub_len)] + [(0, 0)] * len(rest))
    return t.reshape((b, dilation, n_blocks, BLK) + rest)


def _from_blocks(t, seq):
    b, d, nb = t.shape[:3]
    rest = t.shape[4:]
    sub_len = seq // d
    t = t.reshape((b, d, nb * BLK) + rest)[:, :, :sub_len]
    t = jnp.moveaxis(t, 1, 2)
    return t.reshape((b, seq) + rest)


def _with_prev_block(t):
    prev = jnp.pad(t, [(0, 0), (0, 0), (1, 0)] + [(0, 0)] * (t.ndim - 3))[:, :, :-1]
    return jnp.concatenate([prev, t], axis=3)


def dilated_window_attention(q, k, v, slopes, window, dilation):
    seq = q.shape[1]
    sub_len = seq // dilation
    w = window // dilation
    nb = -(-sub_len // BLK)
    qb = _to_blocks(q, dilation, nb)
    kw = _with_prev_block(_to_blocks(k, dilation, nb))
    vw = _with_prev_block(_to_blocks(v, dilation, nb))
    s = jnp.einsum('brnqhgc,brnkhc->brnhgqk', qb, kw).astype(jnp.float32)
    qi = jnp.arange(BLK)[:, None]
    kj = jnp.arange(2 * BLK)[None, :]
    dist = BLK + qi - kj
    kpos = (jnp.arange(nb)[:, None, None] - 1) * BLK + kj
    valid = (dist >= 0) & (dist <= w) & (kpos >= 0)
    bias = -slopes[:, :, None, None] * (dist * dilation).astype(jnp.float32)
    s = jnp.where(valid[:, None, None], s + bias, -jnp.inf)
    m = jnp.max(s, axis=-1, keepdims=True)
    p = jnp.exp(s - m)
    l = jnp.sum(p, axis=-1)
    o = jnp.einsum('brnhgqk,brnkhc->brnqhgc', p, vw.astype(jnp.float32))
    l_q = jnp.moveaxis(l, -1, 3)
    o = o / l_q[..., None]
    lse = jnp.moveaxis(m[..., 0], -1, 3) + jnp.log(l_q)
    return _from_blocks(o, seq), _from_blocks(lse, seq)


def attention_branch(q, k, v, gate):
    b, s, _ = q.shape
    q = q.reshape(b, s, N_KV_HEADS, Q_PER_KV, HEAD_DIM) * (HEAD_DIM ** -0.5)
    k = k.reshape(b, s, N_KV_HEADS, HEAD_DIM)
    v = v.reshape(b, s, N_KV_HEADS, HEAD_DIM)
    slopes = alibi_slopes(N_Q_HEADS).reshape(N_KV_HEADS, Q_PER_KV)
    outs, lses = [], []
    for window, dilation in DILATED_PATTERNS:
        o, lse = dilated_window_attention(q, k, v, slopes, window, dilation)
        outs.append(o)
        lses.append(lse)
    wts = jax.nn.softmax(jnp.stack(lses, axis=0), axis=0)
    o = jnp.sum(wts[..., None] * jnp.stack(outs, axis=0), axis=0)
    o = o.reshape(b, s, ATT_WIDTH).astype(gate.dtype)
    return o * jax.nn.silu(gate)


def conv_branch(val, glu_gate, gate, conv_w, conv_b, ln_g, ln_b):
    h = val * jax.nn.sigmoid(glu_gate)
    h = lax.conv_general_dilated(
        h, conv_w.astype(h.dtype)[:, None, :], window_strides=(1,),
        padding=[(CONV_K - 1, 0)], dimension_numbers=('NWC', 'WIO', 'NWC'),
        feature_group_count=CONV_WIDTH) + conv_b.astype(h.dtype)
    h = layernorm(h, ln_g, ln_b)
    h = jax.nn.silu(h)
    return h * jax.nn.silu(gate)


def _fwd_setup_inputs(seed: int = 0) -> dict:
    key = jax.random.key(seed)
    ks = jax.random.split(key, 9)
    f32 = jnp.float32
    x = jax.random.normal(ks[0], (BATCH, SEQ, D_MODEL), f32)
    norm_g = 1.0 + 0.02 * jax.random.normal(ks[1], (DEPTH, D_MODEL), f32)
    w_in = jax.random.normal(ks[2], (DEPTH, D_MODEL, IN_COLS), f32) * D_MODEL ** -0.5
    conv_w = jax.random.normal(ks[3], (DEPTH, CONV_K, CONV_WIDTH), f32) * CONV_K ** -0.5
    conv_b = 0.02 * jax.random.normal(ks[4], (DEPTH, CONV_WIDTH), f32)
    conv_ln_g = 1.0 + 0.02 * jax.random.normal(ks[5], (DEPTH, CONV_WIDTH), f32)
    conv_ln_b = 0.02 * jax.random.normal(ks[6], (DEPTH, CONV_WIDTH), f32)
    w_out = jax.random.normal(ks[7], (DEPTH, MIX_WIDTH, D_MODEL), f32) * MIX_WIDTH ** -0.5
    final_norm_g = 1.0 + 0.02 * jax.random.normal(ks[8], (D_MODEL,), f32)
    return {"x": x, "norm_g": norm_g, "w_in": w_in, "conv_w": conv_w, "conv_b": conv_b,
            "conv_ln_g": conv_ln_g, "conv_ln_b": conv_ln_b, "w_out": w_out,
            "final_norm_g": final_norm_g}


def _fwd_reference(x, norm_g, w_in, conv_w, conv_b, conv_ln_g, conv_ln_b, w_out, final_norm_g):
    split_idx = list(np.cumsum(SPLIT_SIZES)[:-1])
    for layer in range(DEPTH):
        h = rmsnorm(x, norm_g[layer])
        proj = jnp.einsum('bsd,de->bse', h, w_in[layer])
        q, k, v, a_gate, c_val, c_glu, c_gate = jnp.split(proj, split_idx, axis=-1)
        y_att = attention_branch(q, k, v, a_gate)
        y_conv = conv_branch(c_val, c_glu, c_gate, conv_w[layer], conv_b[layer],
                             conv_ln_g[layer], conv_ln_b[layer])
        y = jnp.concatenate([y_att, y_conv], axis=-1)
        x = x + jnp.einsum('bse,ed->bsd', y, w_out[layer])
    return rmsnorm(x, final_norm_g)


import jax as _jax
import jax.numpy as _jnp

TWIN_FORMAT = 'train_step'
FWD_PARAMS = ['x', 'norm_g', 'w_in', 'conv_w', 'conv_b', 'conv_ln_g', 'conv_ln_b', 'w_out', 'final_norm_g']
TWIN_WEIGHTS = ['norm_g', 'w_in', 'conv_w', 'conv_b', 'conv_ln_g', 'conv_ln_b', 'w_out', 'final_norm_g']
TWIN_DIFF_INPUT = 'x'
TWIN_INPUTS = ['x', 'norm_g', 'w_in', 'conv_w', 'conv_b', 'conv_ln_g', 'conv_ln_b', 'w_out', 'final_norm_g', 'loss_target', 'm_norm_g', 'm_w_in', 'm_conv_w', 'm_conv_b', 'm_conv_ln_g', 'm_conv_ln_b', 'm_w_out', 'm_final_norm_g', 'v_norm_g', 'v_w_in', 'v_conv_w', 'v_conv_b', 'v_conv_ln_g', 'v_conv_ln_b', 'v_w_out', 'v_final_norm_g']
TWIN_OUTPUTS = ['loss', 'grad_x', 'grad_norm_g', 'grad_w_in', 'grad_conv_w', 'grad_conv_b', 'grad_conv_ln_g', 'grad_conv_ln_b', 'grad_w_out', 'grad_final_norm_g', 'delta_norm_g', 'delta_w_in', 'delta_conv_w', 'delta_conv_b', 'delta_conv_ln_g', 'delta_conv_ln_b', 'delta_w_out', 'delta_final_norm_g', 'new_m_norm_g', 'new_m_w_in', 'new_m_conv_w', 'new_m_conv_b', 'new_m_conv_ln_g', 'new_m_conv_ln_b', 'new_m_w_out', 'new_m_final_norm_g', 'new_v_norm_g', 'new_v_w_in', 'new_v_conv_w', 'new_v_conv_b', 'new_v_conv_ln_g', 'new_v_conv_ln_b', 'new_v_w_out', 'new_v_final_norm_g']
TWIN_LEAF_KINDS = {'loss': 'loss', 'grad_x': 'grad_x', 'grad_norm_g': 'grad_w', 'grad_w_in': 'grad_w', 'grad_conv_w': 'grad_w', 'grad_conv_b': 'grad_w', 'grad_conv_ln_g': 'grad_w', 'grad_conv_ln_b': 'grad_w', 'grad_w_out': 'grad_w', 'grad_final_norm_g': 'grad_w', 'delta_norm_g': 'delta_w', 'delta_w_in': 'delta_w', 'delta_conv_w': 'delta_w', 'delta_conv_b': 'delta_w', 'delta_conv_ln_g': 'delta_w', 'delta_conv_ln_b': 'delta_w', 'delta_w_out': 'delta_w', 'delta_final_norm_g': 'delta_w', 'new_m_norm_g': 'new_m', 'new_m_w_in': 'new_m', 'new_m_conv_w': 'new_m', 'new_m_conv_b': 'new_m', 'new_m_conv_ln_g': 'new_m', 'new_m_conv_ln_b': 'new_m', 'new_m_w_out': 'new_m', 'new_m_final_norm_g': 'new_m', 'new_v_norm_g': 'new_v', 'new_v_w_in': 'new_v', 'new_v_conv_w': 'new_v', 'new_v_conv_b': 'new_v', 'new_v_conv_ln_g': 'new_v', 'new_v_conv_ln_b': 'new_v', 'new_v_w_out': 'new_v', 'new_v_final_norm_g': 'new_v'}


def _forward(args):
    return _fwd_reference(*[args[k] for k in FWD_PARAMS])


def _output_shape():
    out = _jax.eval_shape(lambda: _forward(_fwd_setup_inputs(0)))
    return out.shape, out.dtype

N_MICROBATCH = 1
ADAM_LR = 0.001
ADAM_B1 = 0.9
ADAM_B2 = 0.999
ADAM_EPS = 1e-08
ADAM_WD = 0.01
ADAM_STEP = 10
PER_EXAMPLE_BATCH_AXIS = {'x': 0, 'loss_target': 0}
SHARED_INPUTS = []
_WEIGHT_DTYPES = {'norm_g': _jnp.float32, 'w_in': _jnp.float32, 'conv_w': _jnp.float32, 'conv_b': _jnp.float32, 'conv_ln_g': _jnp.float32, 'conv_ln_b': _jnp.float32, 'w_out': _jnp.float32, 'final_norm_g': _jnp.float32}
MOMENT_SCALE = {'norm_g': 8.767826e-02, 'w_in': 3.497028e-02, 'conv_w': 4.465014e-02, 'conv_b': 9.484781e-02, 'conv_ln_g': 5.279977e-02, 'conv_ln_b': 4.460821e-02, 'w_out': 5.196214e-02, 'final_norm_g': 3.196487e+01}


def _to_microbatches(a, axis):
    t = _jnp.moveaxis(a, axis, 0)
    t = t.reshape((N_MICROBATCH, t.shape[0] // N_MICROBATCH) + t.shape[1:])
    return _jnp.moveaxis(t, 1, axis + 1)


def setup_inputs(seed: int = 0) -> dict:
    inp = _fwd_setup_inputs(seed)
    key = _jax.random.fold_in(_jax.random.key(seed), 7919)
    shape, _ = _output_shape()
    out = dict(inp)
    out["loss_target"] = _jax.random.normal(_jax.random.fold_in(key, 0), shape, _jnp.float32)
    for i, name in enumerate(TWIN_WEIGHTS):
        w = inp[name].astype(_jnp.float32)
        if MOMENT_SCALE is None:
            s = _jnp.sqrt(_jnp.mean(_jnp.square(w)) + 1e-30)
        else:
            s = MOMENT_SCALE[name]
        km, kv = _jax.random.split(_jax.random.fold_in(key, i + 1))
        out[name] = w
        out["m_" + name] = s * _jax.random.normal(km, w.shape, _jnp.float32)
        out["v_" + name] = (s * s) * _jax.random.uniform(kv, w.shape, _jnp.float32, 0.5, 1.5)
    if N_MICROBATCH > 1:
        for name, axis in PER_EXAMPLE_BATCH_AXIS.items():
            out[name] = _to_microbatches(out[name], axis)
    return {'x': out['x'], 'norm_g': out['norm_g'], 'w_in': out['w_in'], 'conv_w': out['conv_w'], 'conv_b': out['conv_b'], 'conv_ln_g': out['conv_ln_g'], 'conv_ln_b': out['conv_ln_b'], 'w_out': out['w_out'], 'final_norm_g': out['final_norm_g'], 'loss_target': out['loss_target'], 'm_norm_g': out['m_norm_g'], 'm_w_in': out['m_w_in'], 'm_conv_w': out['m_conv_w'], 'm_conv_b': out['m_conv_b'], 'm_conv_ln_g': out['m_conv_ln_g'], 'm_conv_ln_b': out['m_conv_ln_b'], 'm_w_out': out['m_w_out'], 'm_final_norm_g': out['m_final_norm_g'], 'v_norm_g': out['v_norm_g'], 'v_w_in': out['v_w_in'], 'v_conv_w': out['v_conv_w'], 'v_conv_b': out['v_conv_b'], 'v_conv_ln_g': out['v_conv_ln_g'], 'v_conv_ln_b': out['v_conv_ln_b'], 'v_w_out': out['v_w_out'], 'v_final_norm_g': out['v_final_norm_g']}


def _loss(weights, diff, rest, loss_target):
    with _jax.named_scope("forward"):
        args = {**rest, TWIN_DIFF_INPUT: diff, **{k: w.astype(_WEIGHT_DTYPES[k]) for k, w in weights.items()}}
        y = _forward(args)
    with _jax.named_scope("loss_head"):
        err = _jnp.square(y.astype(_jnp.float32) - loss_target)
        return 0.5 * _jnp.sum(_jnp.mean(err, axis=-1)) if err.ndim else 0.5 * err


def _adamw(w, g, m, v):
    m = ADAM_B1 * m + (1.0 - ADAM_B1) * g
    v = ADAM_B2 * v + (1.0 - ADAM_B2) * _jnp.square(g)
    m_hat = m / (1.0 - ADAM_B1 ** ADAM_STEP)
    v_hat = v / (1.0 - ADAM_B2 ** ADAM_STEP)
    delta = -ADAM_LR * (m_hat / (_jnp.sqrt(v_hat) + ADAM_EPS) + ADAM_WD * w)
    return delta, m, v


def reference(x, norm_g, w_in, conv_w, conv_b, conv_ln_g, conv_ln_b, w_out, final_norm_g, loss_target, m_norm_g, m_w_in, m_conv_w, m_conv_b, m_conv_ln_g, m_conv_ln_b, m_w_out, m_final_norm_g, v_norm_g, v_w_in, v_conv_w, v_conv_b, v_conv_ln_g, v_conv_ln_b, v_w_out, v_final_norm_g):
    given = dict(x=x, norm_g=norm_g, w_in=w_in, conv_w=conv_w, conv_b=conv_b, conv_ln_g=conv_ln_g, conv_ln_b=conv_ln_b, w_out=w_out, final_norm_g=final_norm_g, loss_target=loss_target, m_norm_g=m_norm_g, m_w_in=m_w_in, m_conv_w=m_conv_w, m_conv_b=m_conv_b, m_conv_ln_g=m_conv_ln_g, m_conv_ln_b=m_conv_ln_b, m_w_out=m_w_out, m_final_norm_g=m_final_norm_g, v_norm_g=v_norm_g, v_w_in=v_w_in, v_conv_w=v_conv_w, v_conv_b=v_conv_b, v_conv_ln_g=v_conv_ln_g, v_conv_ln_b=v_conv_ln_b, v_w_out=v_w_out, v_final_norm_g=v_final_norm_g)
    weights = {n: given[n] for n in TWIN_WEIGHTS}
    shared = {n: given[n] for n in SHARED_INPUTS}
    per_example = {n: given[n] for n in ['x']}
    grad_fn = _jax.value_and_grad(_loss, argnums=(0, 1))

    def one_microbatch(ex, loss_target):
        ex = dict(ex)
        diff = ex.pop(TWIN_DIFF_INPUT)
        return grad_fn(weights, diff, {**shared, **ex}, loss_target)

    if N_MICROBATCH == 1:
        loss, (grad_w, grad_x) = one_microbatch(per_example, given["loss_target"])
    else:
        def body(carry, xs):
            loss_sum, grad_sum = carry
            l_k, (gw_k, gx_k) = one_microbatch(xs[0], xs[1])
            with _jax.named_scope("update"):
                return (loss_sum + l_k, _jax.tree.map(_jnp.add, grad_sum, gw_k)), gx_k

        init = (_jnp.zeros((), _jnp.float32), _jax.tree.map(_jnp.zeros_like, weights))
        (loss, grad_w), grad_x = _jax.lax.scan(body, init, (per_example, given["loss_target"]))
    with _jax.named_scope("update"):
        delta_w, new_m, new_v = {}, {}, {}
        for n in TWIN_WEIGHTS:
            delta_w[n], new_m[n], new_v[n] = _adamw(weights[n], grad_w[n], given["m_" + n], given["v_" + n])
    return (loss, grad_x, *[grad_w[n] for n in TWIN_WEIGHTS], *[delta_w[n] for n in TWIN_WEIGHTS],
            *[new_m[n] for n in TWIN_WEIGHTS], *[new_v[n] for n in TWIN_WEIGHTS])
```

```python
import jax
import jax.numpy as jnp
from jax import lax
from jax.experimental import pallas as pl
from jax.experimental.pallas import tpu as pltpu

F32 = jnp.float32
BF16 = jnp.bfloat16

S = 4096
D = 1024
HD = 64
NKV = 4
GQ = 4
KVW = NKV * HD
NCOL = 5632
CONV_K = 31
HALO = 32
BLK = 128
PATTERNS = (1, 4, 16)
NORM_EPS = 1e-6
LN_EPS = 1e-5
NEG = -1e30
OFF_Q, OFF_K, OFF_V, OFF_AG, OFF_CV, OFF_CG, OFF_CGATE = 0, 1024, 1280, 1536, 2560, 3584, 4608
NCHIP = 4
CHUNK = NCOL // NCHIP
WOUT_ROWS = 2 * D
WOUT_SHARD = WOUT_ROWS // NCHIP
CONVW_SHARD = D // NCHIP

ADAM_LR, ADAM_B1, ADAM_B2, ADAM_EPS, ADAM_WD, ADAM_STEP = 0.001, 0.9, 0.999, 1e-08, 0.01, 10

VMEM_LIMIT = 56 * 1024 * 1024


def _params(sem=None, vmem=VMEM_LIMIT):
    return pltpu.CompilerParams(dimension_semantics=sem, vmem_limit_bytes=vmem)


def _sigmoid(a):
    return 1.0 / (1.0 + jnp.exp(-a))


def _rows(tm, width):
    return pl.BlockSpec((tm, width), lambda i: (i, 0))


def _resident(shape):
    return pl.BlockSpec(shape, lambda *_: (0,) * len(shape), pipeline_mode=pl.Buffered(1))


def _dot(a, b):
    return jnp.dot(a, b, preferred_element_type=F32)


def _dot_nt(a, b):
    return lax.dot_general(a, b, (((1,), (1,)), ((), ())), preferred_element_type=F32)


def _dot_tn(a, b):
    return lax.dot_general(a, b, (((0,), (0,)), ((), ())), preferred_element_type=F32)


def _inproj_fwd(x, g1, w_bf):
    tm = 256

    def body(x_ref, g_ref, w_ref, h_ref, q_ref, k_ref, v_ref, ag_ref, cv_ref, cg_ref, cgate_ref):
        xt = x_ref[...]
        r = lax.rsqrt(jnp.mean(xt * xt, axis=-1, keepdims=True) + NORM_EPS)
        h = (xt * r * g_ref[...]).astype(BF16)
        h_ref[...] = h
        q_ref[...] = (_dot(h, w_ref[:, OFF_Q:OFF_Q + D]) * (HD ** -0.5)).astype(BF16)
        k_ref[...] = _dot(h, w_ref[:, OFF_K:OFF_K + KVW]).astype(BF16)
        v_ref[...] = _dot(h, w_ref[:, OFF_V:OFF_V + KVW]).astype(BF16)
        ag_ref[...] = _dot(h, w_ref[:, OFF_AG:OFF_AG + D])
        cv_ref[...] = _dot(h, w_ref[:, OFF_CV:OFF_CV + D])
        cg_ref[...] = _dot(h, w_ref[:, OFF_CG:OFF_CG + D])
        cgate_ref[...] = _dot(h, w_ref[:, OFF_CGATE:OFF_CGATE + D])

    big = jax.ShapeDtypeStruct((S, D), F32)
    return pl.pallas_call(
        body, grid=(S // tm,), name="inproj_fwd",
        in_specs=[_rows(tm, D), _resident((1, D)), _resident((D, NCOL))],
        out_specs=[_rows(tm, D), _rows(tm, D), _rows(tm, KVW), _rows(tm, KVW),
                   _rows(tm, D), _rows(tm, D), _rows(tm, D), _rows(tm, D)],
        out_shape=[jax.ShapeDtypeStruct((S, D), BF16), jax.ShapeDtypeStruct((S, D), BF16),
                   jax.ShapeDtypeStruct((S, KVW), BF16), jax.ShapeDtypeStruct((S, KVW), BF16),
                   big, big, big, big],
        compiler_params=_params(("arbitrary",)),
    )(x, g1, w_bf)


def _bias_table(d):
    h = jnp.arange(NKV * GQ, dtype=F32)
    slopes = jnp.exp2(-8.0 * (h + 1.0) / (NKV * GQ))
    qi = jnp.arange(BLK)[:, None]
    kj = jnp.arange(2 * BLK)[None, :]
    dist = BLK + qi - kj
    window = (dist >= 0) & (dist <= BLK)
    bias = -slopes[:, None, None] * (dist * d).astype(F32)[None]
    has_prev = jnp.stack([jnp.broadcast_to(kj >= BLK, (BLK, 2 * BLK)), jnp.ones((BLK, 2 * BLK), bool)])
    valid = window[None] & has_prev
    tab = jnp.where(valid[:, None], bias[None], NEG)
    return tab.reshape(2, NKV, GQ * BLK, 2 * BLK)


def _stack_heads(ref, j, width=HD):
    return jnp.concatenate(
        [ref[:, (j * GQ + g) * HD:(j * GQ + g) * HD + width] for g in range(GQ)], axis=0)


def _attn_fwd(q, k, v, bias, d):
    sub = S // d
    nb = sub // BLK
    qv, kv, vv = q.reshape(sub, d * D), k.reshape(sub, d * KVW), v.reshape(sub, d * KVW)

    def body(q_ref, kc_ref, kp_ref, vc_ref, vp_ref, b_ref, o_ref, l_ref):
        for j in range(NKV):
            cols = slice(j * HD, (j + 1) * HD)
            qs = _stack_heads(q_ref, j)
            kw = jnp.concatenate([kp_ref[:, cols], kc_ref[:, cols]], axis=0)
            vw = jnp.concatenate([vp_ref[:, cols], vc_ref[:, cols]], axis=0)
            s = _dot_nt(qs, kw) + b_ref[0, j]
            m = jnp.max(s, axis=1, keepdims=True)
            p = jnp.exp(s - m)
            l = jnp.sum(p, axis=1, keepdims=True)
            o = _dot(p.astype(BF16), vw) / l
            lse = m + jnp.log(l)
            for g in range(GQ):
                hc = slice((j * GQ + g) * HD, (j * GQ + g + 1) * HD)
                o_ref[:, hc] = o[g * BLK:(g + 1) * BLK, :]
                l_ref[:, hc] = jnp.broadcast_to(lse[g * BLK:(g + 1) * BLK, :], (BLK, HD))

    cur = lambda r, n: (n, r)
    prev = lambda r, n: (jnp.maximum(n - 1, 0), r)
    o, lse = pl.pallas_call(
        body, grid=(d, nb), name=f"attn_fwd_d{d}",
        in_specs=[pl.BlockSpec((BLK, D), cur),
                  pl.BlockSpec((BLK, KVW), cur), pl.BlockSpec((BLK, KVW), prev),
                  pl.BlockSpec((BLK, KVW), cur), pl.BlockSpec((BLK, KVW), prev),
                  pl.BlockSpec((1, NKV, GQ * BLK, 2 * BLK), lambda r, n: (jnp.minimum(n, 1), 0, 0, 0))],
        out_specs=[pl.BlockSpec((BLK, D), cur), pl.BlockSpec((BLK, D), cur)],
        out_shape=[jax.ShapeDtypeStruct((sub, d * D), F32), jax.ShapeDtypeStruct((sub, d * D), F32)],
        compiler_params=_params(("arbitrary", "arbitrary")),
    )(qv, kv, kv, vv, vv, bias)
    return o.reshape(S, D), lse.reshape(S, D)


def _attn_combine(outs, lses, a_gate):
    tm = 256

    def body(o1, o2, o3, l1, l2, l3, ag_ref, o_ref, lse_ref, y_ref):
        a, b, c = l1[...], l2[...], l3[...]
        m = jnp.maximum(jnp.maximum(a, b), c)
        ea, eb, ec = jnp.exp(a - m), jnp.exp(b - m), jnp.exp(c - m)
        den = ea + eb + ec
        o = (ea * o1[...] + eb * o2[...] + ec * o3[...]) / den
        o_ref[...] = o
        lse_ref[...] = m + jnp.log(den)
        ag = ag_ref[...]
        y_ref[...] = (o * (ag * _sigmoid(ag))).astype(BF16)

    big = jax.ShapeDtypeStruct((S, D), F32)
    return pl.pallas_call(
        body, grid=(S // tm,), name="attn_combine",
        in_specs=[_rows(tm, D)] * 7,
        out_specs=[_rows(tm, D)] * 3,
        out_shape=[big, big, jax.ShapeDtypeStruct((S, D), BF16)],
        compiler_params=_params(("arbitrary",)),
    )(*outs, *lses, a_gate)


def _attn_gate_bwd(dy_att, o, a_gate, ones_bd):
    tm = 256

    def body(dy_ref, o_ref, ag_ref, e_ref, do_ref, dag_ref, delta_ref):
        dy, o_, ag = dy_ref[...], o_ref[...], ag_ref[...]
        sg = _sigmoid(ag)
        do = dy * (ag * sg)
        do_ref[...] = do.astype(BF16)
        dag_ref[...] = dy * o_ * (sg * (1.0 + ag * (1.0 - sg)))
        prod = do * o_
        hi = prod.astype(BF16)
        lo = (prod - hi.astype(F32)).astype(BF16)
        delta_ref[...] = _dot(hi, e_ref[...]) + _dot(lo, e_ref[...])

    big = jax.ShapeDtypeStruct((S, D), F32)
    return pl.pallas_call(
        body, grid=(S // tm,), name="attn_gate_bwd",
        in_specs=[_rows(tm, D), _rows(tm, D), _rows(tm, D), _resident((D, D))],
        out_specs=[_rows(tm, D)] * 3,
        out_shape=[jax.ShapeDtypeStruct((S, D), BF16), big, big],
        compiler_params=_params(("arbitrary",)),
    )(dy_att, o, a_gate, ones_bd)


def _attn_bwd(q, k, v, do, lse, delta, bias, d):
    sub = S // d
    nb = sub // BLK
    qv, dov = q.reshape(sub, d * D), do.reshape(sub, d * D)
    lv, dlv = lse.reshape(sub, d * D), delta.reshape(sub, d * D)
    kv, vv = k.reshape(sub, d * KVW), v.reshape(sub, d * KVW)

    def body(q_ref, do_ref, l_ref, dl_ref, kc_ref, kp_ref, vc_ref, vp_ref, b_ref,
             dq_ref, dk_ref, dv_ref, carry_k, carry_v):
        n = pl.program_id(1)

        @pl.when(n < nb)
        def _():
            for j in range(NKV):
                cols = slice(j * HD, (j + 1) * HD)
                qs = _stack_heads(q_ref, j)
                dos = _stack_heads(do_ref, j)
                lse_r = _stack_heads(l_ref, j, 1)
                delta_r = _stack_heads(dl_ref, j, 1)
                kw = jnp.concatenate([kp_ref[:, cols], kc_ref[:, cols]], axis=0)
                vw = jnp.concatenate([vp_ref[:, cols], vc_ref[:, cols]], axis=0)
                s = _dot_nt(qs, kw) + b_ref[0, j]
                p = jnp.exp(s - lse_r)
                dvw = _dot_tn(p.astype(BF16), dos)
                dp = _dot_nt(dos, vw)
                ds = (p * (dp - delta_r)).astype(BF16)
                dqs = _dot(ds, kw)
                dkw = _dot_tn(ds, qs)
                for g in range(GQ):
                    hc = slice((j * GQ + g) * HD, (j * GQ + g + 1) * HD)
                    dq_ref[:, hc] = dqs[g * BLK:(g + 1) * BLK, :]

                @pl.when(n >= 1)
                def _():
                    dk_ref[:, cols] = carry_k[:, cols] + dkw[0:BLK, :]
                    dv_ref[:, cols] = carry_v[:, cols] + dvw[0:BLK, :]

                carry_k[:, cols] = dkw[BLK:2 * BLK, :]
                carry_v[:, cols] = dvw[BLK:2 * BLK, :]

        @pl.when(n == nb)
        def _():
            dk_ref[...] = carry_k[...]
            dv_ref[...] = carry_v[...]

    qidx = lambda r, n: (jnp.minimum(n, nb - 1), r)
    pidx = lambda r, n: (jnp.maximum(jnp.minimum(n, nb - 1) - 1, 0), r)
    oidx = lambda r, n: (jnp.maximum(n - 1, 0), r)
    dq, dk, dv = pl.pallas_call(
        body, grid=(d, nb + 1), name=f"attn_bwd_d{d}",
        in_specs=[pl.BlockSpec((BLK, D), qidx)] * 4
        + [pl.BlockSpec((BLK, KVW), qidx), pl.BlockSpec((BLK, KVW), pidx),
           pl.BlockSpec((BLK, KVW), qidx), pl.BlockSpec((BLK, KVW), pidx),
           pl.BlockSpec((1, NKV, GQ * BLK, 2 * BLK), lambda r, n: (jnp.minimum(n, 1), 0, 0, 0))],
        out_specs=[pl.BlockSpec((BLK, D), qidx), pl.BlockSpec((BLK, KVW), oidx), pl.BlockSpec((BLK, KVW), oidx)],
        out_shape=[jax.ShapeDtypeStruct((sub, d * D), F32), jax.ShapeDtypeStruct((sub, d * KVW), F32),
                   jax.ShapeDtypeStruct((sub, d * KVW), F32)],
        scratch_shapes=[pltpu.VMEM((BLK, KVW), F32), pltpu.VMEM((BLK, KVW), F32)],
        compiler_params=_params(("arbitrary", "arbitrary")),
    )(qv, dov, lv, dlv, kv, kv, vv, vv, bias)
    return dq.reshape(S, D), dk.reshape(S, KVW), dv.reshape(S, KVW)


CONV_T = 128


def _halo_before(i):
    return (jnp.maximum(i * (CONV_T // HALO) - 1, 0), 0)


def _halo_after(i):
    return (jnp.minimum((i + 1) * (CONV_T // HALO), S // HALO - 1), 0)


def _conv_fwd(c_val, c_glu, c_gate, conv_w, conv_b, ln_g, ln_b):
    T = CONV_T

    def body(cv_ref, cg_ref, cvh_ref, cgh_ref, gate_ref, w_ref, b_ref, lg_ref, lb_ref, u_ref, y_ref, win):
        i = pl.program_id(0)
        win[HALO:HALO + T, :] = cv_ref[...] * _sigmoid(cg_ref[...])
        win[0:HALO, :] = jnp.where(i > 0, cvh_ref[...] * _sigmoid(cgh_ref[...]), 0.0)
        u = jnp.broadcast_to(b_ref[...], (T, D))
        for j in range(CONV_K):
            lo = HALO - (CONV_K - 1) + j
            u = u + w_ref[j:j + 1, :] * win[lo:lo + T, :]
        u_ref[...] = u
        mu = jnp.mean(u, axis=-1, keepdims=True)
        uc = u - mu
        rstd = lax.rsqrt(jnp.mean(uc * uc, axis=-1, keepdims=True) + LN_EPS)
        nrm = uc * rstd * lg_ref[...] + lb_ref[...]
        gate = gate_ref[...]
        y_ref[...] = (nrm * _sigmoid(nrm) * (gate * _sigmoid(gate))).astype(BF16)

    halo = pl.BlockSpec((HALO, D), _halo_before)
    return pl.pallas_call(
        body, grid=(S // T,), name="conv_fwd",
        in_specs=[_rows(T, D), _rows(T, D), halo, halo, _rows(T, D),
                  _resident((HALO, D)), _resident((1, D)), _resident((1, D)), _resident((1, D))],
        out_specs=[_rows(T, D), _rows(T, D)],
        out_shape=[jax.ShapeDtypeStruct((S, D), F32), jax.ShapeDtypeStruct((S, D), BF16)],
        scratch_shapes=[pltpu.VMEM((T + HALO, D), F32)],
        compiler_params=_params(("arbitrary",)),
    )(c_val, c_glu, c_val, c_glu, c_gate, conv_w, conv_b, ln_g, ln_b)


def _conv_bwd_rows(u, c_gate, dy_conv, ln_g, ln_b):
    tm = 256

    def body(u_ref, gate_ref, dy_ref, lg_ref, lb_ref, du_ref, dgate_ref, st_ref):
        @pl.when(pl.program_id(0) == 0)
        def _():
            st_ref[...] = jnp.zeros_like(st_ref)

        u, gate, dy = u_ref[...], gate_ref[...], dy_ref[...]
        mu = jnp.mean(u, axis=-1, keepdims=True)
        uc = u - mu
        rstd = lax.rsqrt(jnp.mean(uc * uc, axis=-1, keepdims=True) + LN_EPS)
        z = uc * rstd
        nrm = z * lg_ref[...] + lb_ref[...]
        sn, sg = _sigmoid(nrm), _sigmoid(gate)
        dgate_ref[...] = dy * (nrm * sn) * (sg * (1.0 + gate * (1.0 - sg)))
        dn = dy * (gate * sg) * (sn * (1.0 + nrm * (1.0 - sn)))
        dz = dn * lg_ref[...]
        du = rstd * (dz - jnp.mean(dz, axis=-1, keepdims=True) - z * jnp.mean(dz * z, axis=-1, keepdims=True))
        du_ref[...] = du
        st_ref[0:1, :] += jnp.sum(dn * z, axis=0, keepdims=True)
        st_ref[1:2, :] += jnp.sum(dn, axis=0, keepdims=True)
        st_ref[2:3, :] += jnp.sum(du, axis=0, keepdims=True)

    big = jax.ShapeDtypeStruct((S, D), F32)
    return pl.pallas_call(
        body, grid=(S // tm,), name="conv_bwd_rows",
        in_specs=[_rows(tm, D)] * 3 + [_resident((1, D)), _resident((1, D))],
        out_specs=[_rows(tm, D), _rows(tm, D), pl.BlockSpec((8, D), lambda i: (0, 0))],
        out_shape=[big, big, jax.ShapeDtypeStruct((8, D), F32)],
        compiler_params=_params(("arbitrary",)),
    )(u, c_gate, dy_conv, ln_g, ln_b)


def _conv_bwd_taps(du, c_val, c_glu, conv_w):
    T = CONV_T
    last = S // T - 1

    def body(du_ref, dua_ref, cv_ref, cg_ref, cvh_ref, cgh_ref, w_ref, dcv_ref, dcg_ref, dw_ref, hwin, dwin):
        i = pl.program_id(0)

        @pl.when(i == 0)
        def _():
            dw_ref[...] = jnp.zeros_like(dw_ref)

        cv, sg = cv_ref[...], _sigmoid(cg_ref[...])
        hwin[HALO:HALO + T, :] = cv * sg
        hwin[0:HALO, :] = jnp.where(i > 0, cvh_ref[...] * _sigmoid(cgh_ref[...]), 0.0)
        du = du_ref[...]
        dwin[0:T, :] = du
        dwin[T:T + HALO, :] = jnp.where(i < last, dua_ref[...], 0.0)
        dh = jnp.zeros((T, D), F32)
        for j in range(CONV_K):
            lo = HALO - (CONV_K - 1) + j
            dw_ref[j:j + 1, :] += jnp.sum(du * hwin[lo:lo + T, :], axis=0, keepdims=True)
            hi = CONV_K - 1 - j
            dh = dh + w_ref[j:j + 1, :] * dwin[hi:hi + T, :]
        dcv_ref[...] = dh * sg
        dcg_ref[...] = dh * cv * (sg * (1.0 - sg))

    before = pl.BlockSpec((HALO, D), _halo_before)
    after = pl.BlockSpec((HALO, D), _halo_after)
    big = jax.ShapeDtypeStruct((S, D), F32)
    return pl.pallas_call(
        body, grid=(S // T,), name="conv_bwd_taps",
        in_specs=[_rows(T, D), after, _rows(T, D), _rows(T, D), before, before, _resident((HALO, D))],
        out_specs=[_rows(T, D), _rows(T, D), pl.BlockSpec((HALO, D), lambda i: (0, 0))],
        out_shape=[big, big, jax.ShapeDtypeStruct((HALO, D), F32)],
        scratch_shapes=[pltpu.VMEM((T + HALO, D), F32), pltpu.VMEM((T + HALO, D), F32)],
        compiler_params=_params(("arbitrary",)),
    )(du, du, c_val, c_glu, c_val, c_glu, conv_w)


def _outproj_loss(y_att, y_conv, w_out_bf, x, target, gf):
    tm = 256

    def body(ya_ref, yc_ref, w_ref, x_ref, t_ref, gf_ref, dx2_ref, dya_ref, dyc_ref, dw_ref, st_ref, acc):
        @pl.when(pl.program_id(0) == 0)
        def _():
            acc[...] = jnp.zeros_like(acc)
            st_ref[...] = jnp.zeros_like(st_ref)

        ya, yc = ya_ref[...], yc_ref[...]
        x2 = x_ref[...] + _dot(ya, w_ref[0:D, :]) + _dot(yc, w_ref[D:2 * D, :])
        r = lax.rsqrt(jnp.mean(x2 * x2, axis=-1, keepdims=True) + NORM_EPS)
        xn = x2 * r
        err = xn * gf_ref[...] - t_ref[...]
        dout = err * (1.0 / D)
        dxn = dout * gf_ref[...]
        dx2 = r * (dxn - xn * jnp.mean(dxn * xn, axis=-1, keepdims=True))
        dx2_ref[...] = dx2
        dx2b = dx2.astype(BF16)
        dya_ref[...] = _dot_nt(dx2b, w_ref[0:D, :])
        dyc_ref[...] = _dot_nt(dx2b, w_ref[D:2 * D, :])
        acc[0:D, :] += _dot_tn(ya, dx2b)
        acc[D:2 * D, :] += _dot_tn(yc, dx2b)
        st_ref[0:1, :] += jnp.sum(dout * xn, axis=0, keepdims=True)
        st_ref[1:2, :] += jnp.sum(err * err, axis=0, keepdims=True) * (0.5 / D)

        @pl.when(pl.program_id(0) == S // tm - 1)
        def _():
            dw_ref[...] = acc[...].astype(BF16)

    big = jax.ShapeDtypeStruct((S, D), F32)
    return pl.pallas_call(
        body, grid=(S // tm,), name="outproj_loss",
        in_specs=[_rows(tm, D), _rows(tm, D), _resident((WOUT_ROWS, D)), _rows(tm, D), _rows(tm, D), _resident((1, D))],
        out_specs=[_rows(tm, D), _rows(tm, D), _rows(tm, D),
                   pl.BlockSpec((WOUT_ROWS, D), lambda i: (0, 0)), pl.BlockSpec((8, D), lambda i: (0, 0))],
        out_shape=[big, big, big, jax.ShapeDtypeStruct((WOUT_ROWS, D), BF16), jax.ShapeDtypeStruct((8, D), F32)],
        scratch_shapes=[pltpu.VMEM((WOUT_ROWS, D), F32)],
        compiler_params=_params(("arbitrary",)),
    )(y_att, y_conv, w_out_bf, x, target, gf)


def _inproj_bwd_x(dqs, dks, dvs, dag, dcv, dcg, dcgate, w_bf, x, g1, dx2):
    tm = 256

    def body(dq1, dq2, dq3, dk1, dk2, dk3, dv1, dv2, dv3, dag_ref, dcv_ref, dcg_ref, dcgate_ref,
             w_ref, x_ref, g_ref, dx2_ref, dp_ref, gx_ref, st_ref):
        @pl.when(pl.program_id(0) == 0)
        def _():
            st_ref[...] = jnp.zeros_like(st_ref)

        pieces = (
            (OFF_Q, D, (dq1[...] + dq2[...] + dq3[...]) * (HD ** -0.5)),
            (OFF_K, KVW, dk1[...] + dk2[...] + dk3[...]),
            (OFF_V, KVW, dv1[...] + dv2[...] + dv3[...]),
            (OFF_AG, D, dag_ref[...]), (OFF_CV, D, dcv_ref[...]), (OFF_CG, D, dcg_ref[...]),
            (OFF_CGATE, D, dcgate_ref[...]),
        )
        dh = jnp.zeros((tm, D), F32)
        for off, width, val in pieces:
            vb = val.astype(BF16)
            dp_ref[:, off:off + width] = vb
            dh = dh + _dot_nt(vb, w_ref[:, off:off + width])
        xt = x_ref[...]
        r = lax.rsqrt(jnp.mean(xt * xt, axis=-1, keepdims=True) + NORM_EPS)
        xn = xt * r
        dxn = dh * g_ref[...]
        gx_ref[...] = dx2_ref[...] + r * (dxn - xn * jnp.mean(dxn * xn, axis=-1, keepdims=True))
        st_ref[0:1, :] += jnp.sum(dh * xn, axis=0, keepdims=True)

    return pl.pallas_call(
        body, grid=(S // tm,), name="inproj_bwd_x",
        in_specs=[_rows(tm, D)] * 3 + [_rows(tm, KVW)] * 6 + [_rows(tm, D)] * 4
        + [_resident((D, NCOL)), _rows(tm, D), _resident((1, D)), _rows(tm, D)],
        out_specs=[_rows(tm, NCOL), _rows(tm, D), pl.BlockSpec((8, D), lambda i: (0, 0))],
        out_shape=[jax.ShapeDtypeStruct((S, NCOL), BF16), jax.ShapeDtypeStruct((S, D), F32),
                   jax.ShapeDtypeStruct((8, D), F32)],
        compiler_params=_params(("arbitrary",)),
    )(*dqs, *dks, *dvs, dag, dcv, dcg, dcgate, w_bf, x, g1, dx2)


def _inproj_bwd_w(h, dproj):
    tk = 1024
    nk = S // tk

    def body(h_ref, dp_ref, o_ref, acc):
        i = pl.program_id(1)

        @pl.when(i == 0)
        def _():
            acc[...] = jnp.zeros_like(acc)

        acc[...] += _dot_tn(h_ref[...], dp_ref[...])

        @pl.when(i == nk - 1)
        def _():
            o_ref[0] = acc[...].astype(BF16)

    return pl.pallas_call(
        body, grid=(NCHIP, nk), name="inproj_bwd_w",
        in_specs=[pl.BlockSpec((tk, D), lambda c, i: (i, 0)), pl.BlockSpec((tk, CHUNK), lambda c, i: (i, c))],
        out_specs=pl.BlockSpec((1, D, CHUNK), lambda c, i: (c, 0, 0)),
        out_shape=jax.ShapeDtypeStruct((NCHIP, D, CHUNK), BF16),
        scratch_shapes=[pltpu.VMEM((D, CHUNK), F32)],
        compiler_params=_params(("arbitrary", "arbitrary")),
    )(h, dproj)


def _local_step(x, target, g1, w_in_bf, conv_w, conv_b, ln_g, ln_b, w_out_bf, gf):
    h, q, k, v, a_gate, c_val, c_glu, c_gate = _inproj_fwd(x, g1, w_in_bf)
    tables = [_bias_table(d) for d in PATTERNS]
    outs, lses = zip(*[_attn_fwd(q, k, v, t, d) for t, d in zip(tables, PATTERNS)])
    o, lse, y_att = _attn_combine(outs, lses, a_gate)
    u, y_conv = _conv_fwd(c_val, c_glu, c_gate, conv_w, conv_b, ln_g, ln_b)
    dx2, dy_att, dy_conv, dw_out, st_out = _outproj_loss(y_att, y_conv, w_out_bf, x, target, gf)

    head = jnp.arange(D) // HD
    ones_bd = (head[:, None] == head[None, :]).astype(BF16)
    do, da_gate, delta = _attn_gate_bwd(dy_att, o, a_gate, ones_bd)
    dqs, dks, dvs = zip(*[_attn_bwd(q, k, v, do, lse, delta, t, d) for t, d in zip(tables, PATTERNS)])

    du, dc_gate, st_conv = _conv_bwd_rows(u, c_gate, dy_conv, ln_g, ln_b)
    dc_val, dc_glu, dconv_w = _conv_bwd_taps(du, c_val, c_glu, conv_w)

    dproj, grad_x, st_in = _inproj_bwd_x(dqs, dks, dvs, da_gate, dc_val, dc_glu, dc_gate, w_in_bf, x, g1, dx2)
    dw_in = _inproj_bwd_w(h, dproj)
    small = jnp.concatenate([st_in, st_conv, st_out, dconv_w], axis=0)
    return grad_x, dw_in, dw_out, small


ROW_NORM_G, ROW_LN_G, ROW_LN_B, ROW_CONV_B, ROW_FINAL_G, ROW_LOSS, ROW_TAPS = 0, 8, 9, 10, 16, 17, 24
SMALL_ROWS = 24 + HALO


MESH = pl.DeviceIdType.MESH
ANY = pl.BlockSpec(memory_space=pl.ANY)
CHIP_FLIPS = ((1, 0), (0, 1), (1, 1))


def _pos():
    return lax.axis_index("x"), lax.axis_index("y"), lax.axis_index("c")


def _flip(v, f):
    return 1 - v if f else v


def _ds(start, size, align=None):
    return pl.ds(pl.multiple_of(start, align or size), size)


def _gather_weights(wi, wo, cw):
    halves = (D // 2, WOUT_SHARD // 2, HALO // 2)
    n_ici = 3 * len(CHIP_FLIPS)

    def body(wi_ref, wo_ref, cw_ref, wi_full, wo_full, cw_full, send, recv, lsem):
        x, y, c = _pos()
        shards = (wi_ref, wo_ref, cw_ref)

        def region(a, px, py, half):
            chip = 2 * px + py
            n = halves[a]
            if a == 0:
                return wi_full.at[_ds(half * n, n), _ds(chip * CHUNK, CHUNK, 128)]
            if a == 1:
                return wo_full.at[_ds(chip * WOUT_SHARD + half * n, n), :]
            return cw_full.at[_ds(half * n, n), _ds(chip * CONVW_SHARD, CONVW_SHARD, 128)]

        def half_shard(a, half):
            n = halves[a]
            return shards[a].at[_ds(half * n, n), :]

        def remote(k, src, dst, dev):
            return pltpu.make_async_remote_copy(src_ref=src, dst_ref=dst, send_sem=send.at[k], recv_sem=recv.at[k],
                                                device_id=dev, device_id_type=MESH)

        local = [pltpu.make_async_copy(half_shard(a, hf), region(a, x, y, hf), lsem.at[2 * a + hf])
                 for a in range(3) for hf in range(2)]
        for cp in local:
            cp.start()
        sends = []
        for a in range(3):
            for j, (fx, fy) in enumerate(CHIP_FLIPS):
                cp = remote(3 * a + j, half_shard(a, c), region(a, x, y, c), (_flip(x, fx), _flip(y, fy), c))
                cp.start()
                sends.append(cp)
        for a in range(3):
            for j, (fx, fy) in enumerate(CHIP_FLIPS):
                px, py = _flip(x, fx), _flip(y, fy)
                got = region(a, px, py, c)
                remote(3 * a + j, got, got, (px, py, c)).wait_recv()
                cp = remote(n_ici + 3 * a + j, got, got, (x, y, 1 - c))
                cp.start()
                sends.append(cp)
        for a in range(3):
            for j, (fx, fy) in enumerate(CHIP_FLIPS):
                got = region(a, _flip(x, fx), _flip(y, fy), 1 - c)
                remote(n_ici + 3 * a + j, got, got, (x, y, 1 - c)).wait_recv()
        for cp in sends:
            cp.wait_send()
        for cp in local:
            cp.wait()

    return pl.pallas_call(
        body, name="gather_weights",
        in_specs=[ANY, ANY, ANY], out_specs=[ANY, ANY, ANY],
        out_shape=[jax.ShapeDtypeStruct((D, NCOL), BF16), jax.ShapeDtypeStruct((WOUT_ROWS, D), BF16),
                   jax.ShapeDtypeStruct((HALO, D), F32)],
        scratch_shapes=[pltpu.SemaphoreType.DMA((2 * n_ici,)), pltpu.SemaphoreType.DMA((2 * n_ici,)),
                        pltpu.SemaphoreType.DMA((6,))],
    )(wi, wo, cw)


def _exchange_halves(gi4, go4, small):
    def body(gi_ref, go_ref, sm_ref, ri_ref, ro_ref, rs_ref, send, recv):
        x, y, c = _pos()
        sib = (x, y, 1 - c)
        copies = [
            (gi_ref.at[:, _ds((1 - c) * (D // 2), D // 2), :], ri_ref),
            (go_ref.at[:, _ds((1 - c) * (WOUT_SHARD // 2), WOUT_SHARD // 2), :], ro_ref),
            (sm_ref, rs_ref),
        ]
        cps = [pltpu.make_async_remote_copy(src_ref=s_, dst_ref=d_, send_sem=send.at[k], recv_sem=recv.at[k],
                                            device_id=sib, device_id_type=MESH) for k, (s_, d_) in enumerate(copies)]
        for cp in cps:
            cp.start()
        for cp in cps:
            cp.wait()

    return pl.pallas_call(
        body, name="exchange_halves",
        in_specs=[ANY, ANY, ANY], out_specs=[ANY, ANY, ANY],
        out_shape=[jax.ShapeDtypeStruct((NCHIP, D // 2, CHUNK), BF16),
                   jax.ShapeDtypeStruct((NCHIP, WOUT_SHARD // 2, D), BF16),
                   jax.ShapeDtypeStruct((SMALL_ROWS, D), F32)],
        scratch_shapes=[pltpu.SemaphoreType.DMA((3,)), pltpu.SemaphoreType.DMA((3,))],
    )(gi4, go4, small)


def _add_halves(gi4, ri, go4, ro, small, rs):
    hi, ho = D // 2, WOUT_SHARD // 2

    def body(gi_ref, ri_ref, go_ref, ro_ref, sm_ref, rs_ref, pi_ref, po_ref, ps_ref):
        c = lax.axis_index("c")
        pi_ref[0] = (gi_ref[0, _ds(c * hi, hi), :].astype(F32) + ri_ref[0].astype(F32)).astype(BF16)
        po_ref[0] = (go_ref[0, _ds(c * ho, ho), :].astype(F32) + ro_ref[0].astype(F32)).astype(BF16)
        ps_ref[...] = sm_ref[...] + rs_ref[...]

    blk = lambda n, w: pl.BlockSpec((1, n, w), lambda k: (k, 0, 0))
    whole = pl.BlockSpec((SMALL_ROWS, D), lambda k: (0, 0))
    return pl.pallas_call(
        body, grid=(NCHIP,), name="add_halves",
        in_specs=[blk(D, CHUNK), blk(hi, CHUNK), blk(WOUT_SHARD, D), blk(ho, D), whole, whole],
        out_specs=[blk(hi, CHUNK), blk(ho, D), whole],
        out_shape=[jax.ShapeDtypeStruct((NCHIP, hi, CHUNK), BF16), jax.ShapeDtypeStruct((NCHIP, ho, D), BF16),
                   jax.ShapeDtypeStruct((SMALL_ROWS, D), F32)],
        compiler_params=_params(("arbitrary",)),
    )(gi4, ri, go4, ro, small, rs)


def _exchange_chips(pi, po, ps):
    def body(pi_ref, po_ref, ps_ref, ri_ref, ro_ref, rs_ref, send, recv, lsem):
        x, y, c = _pos()
        me = 2 * x + y
        srcs = (pi_ref, po_ref, ps_ref)
        dsts = (ri_ref, ro_ref, rs_ref)

        def piece(a, chip):
            return srcs[a] if a == 2 else srcs[a].at[chip]

        local = [pltpu.make_async_copy(piece(a, me), dsts[a].at[me], lsem.at[a]) for a in range(3)]
        for cp in local:
            cp.start()
        sends = []
        for a in range(3):
            for j, (fx, fy) in enumerate(CHIP_FLIPS):
                px, py = _flip(x, fx), _flip(y, fy)
                cp = pltpu.make_async_remote_copy(
                    src_ref=piece(a, 2 * px + py), dst_ref=dsts[a].at[me], send_sem=send.at[3 * a + j],
                    recv_sem=recv.at[3 * a + j], device_id=(px, py, c), device_id_type=MESH)
                cp.start()
                sends.append(cp)
        for a in range(3):
            for j, (fx, fy) in enumerate(CHIP_FLIPS):
                px, py = _flip(x, fx), _flip(y, fy)
                got = dsts[a].at[2 * px + py]
                pltpu.make_async_remote_copy(src_ref=got, dst_ref=got, send_sem=send.at[3 * a + j],
                                             recv_sem=recv.at[3 * a + j], device_id=(px, py, c),
                                             device_id_type=MESH).wait_recv()
        for cp in sends:
            cp.wait_send()
        for cp in local:
            cp.wait()

    return pl.pallas_call(
        body, name="exchange_chips",
        in_specs=[ANY, ANY, ANY], out_specs=[ANY, ANY, ANY],
        out_shape=[jax.ShapeDtypeStruct((NCHIP, D // 2, CHUNK), BF16),
                   jax.ShapeDtypeStruct((NCHIP, WOUT_SHARD // 2, D), BF16),
                   jax.ShapeDtypeStruct((NCHIP, SMALL_ROWS, D), F32)],
        scratch_shapes=[pltpu.SemaphoreType.DMA((9,)), pltpu.SemaphoreType.DMA((9,)), pltpu.SemaphoreType.DMA((3,))],
    )(pi, po, ps)


def _sum_chips(ri, ro, rs):
    def body(ri_ref, ro_ref, rs_ref, gi_ref, go_ref, gs_ref, g5_ref, loss_ref):
        def total(ref):
            acc = ref[0].astype(F32)
            for k in range(1, NCHIP):
                acc = acc + ref[k].astype(F32)
            return acc

        gi_ref[...] = total(ri_ref)
        go_ref[...] = total(ro_ref)
        gs_ref[...] = total(rs_ref)
        g5_ref[...] = jnp.zeros_like(g5_ref)
        for i, row in enumerate((ROW_NORM_G, ROW_CONV_B, ROW_LN_G, ROW_LN_B, ROW_FINAL_G)):
            g5_ref[i:i + 1, :] = gs_ref[row:row + 1, :]
        loss = jnp.sum(gs_ref[ROW_LOSS:ROW_LOSS + 1, :], axis=1, keepdims=True)
        loss_ref[...] = jnp.broadcast_to(loss, loss_ref.shape)

    return pl.pallas_call(
        body, name="sum_chips",
        out_shape=[jax.ShapeDtypeStruct((D // 2, CHUNK), F32), jax.ShapeDtypeStruct((WOUT_SHARD // 2, D), F32),
                   jax.ShapeDtypeStruct((SMALL_ROWS, D), F32), jax.ShapeDtypeStruct((8, D), F32),
                   jax.ShapeDtypeStruct((8, 128), F32)],
        compiler_params=_params(),
    )(ri, ro, rs)


def _exchange_results(gi, go):
    def body(gi_ref, go_ref, fi_ref, fo_ref, send, recv, lsem):
        x, y, c = _pos()
        pairs = ((gi_ref, fi_ref), (go_ref, fo_ref))
        local = [pltpu.make_async_copy(s_, d_.at[c], lsem.at[k]) for k, (s_, d_) in enumerate(pairs)]
        remote = [pltpu.make_async_remote_copy(src_ref=s_, dst_ref=d_.at[c], send_sem=send.at[k], recv_sem=recv.at[k],
                                               device_id=(x, y, 1 - c), device_id_type=MESH)
                  for k, (s_, d_) in enumerate(pairs)]
        for cp in local + remote:
            cp.start()
        for k, (s_, d_) in enumerate(pairs):
            got = d_.at[1 - c]
            pltpu.make_async_remote_copy(src_ref=got, dst_ref=got, send_sem=send.at[k], recv_sem=recv.at[k],
                                         device_id=(x, y, 1 - c), device_id_type=MESH).wait_recv()
        for cp in remote:
            cp.wait_send()
        for cp in local:
            cp.wait()

    return pl.pallas_call(
        body, name="exchange_results",
        in_specs=[ANY, ANY], out_specs=[ANY, ANY],
        out_shape=[jax.ShapeDtypeStruct((2, D // 2, CHUNK), F32), jax.ShapeDtypeStruct((2, WOUT_SHARD // 2, D), F32)],
        scratch_shapes=[pltpu.SemaphoreType.DMA((2,)), pltpu.SemaphoreType.DMA((2,)), pltpu.SemaphoreType.DMA((2,))],
    )(gi, go)


def _adamw_math(w, g, m, v):
    m2 = ADAM_B1 * m + (1.0 - ADAM_B1) * g
    v2 = ADAM_B2 * v + (1.0 - ADAM_B2) * (g * g)
    m_hat = m2 / (1.0 - ADAM_B1 ** ADAM_STEP)
    v_hat = v2 / (1.0 - ADAM_B2 ** ADAM_STEP)
    delta = -ADAM_LR * (m_hat / (jnp.sqrt(v_hat) + ADAM_EPS) + ADAM_WD * w)
    return delta, m2, v2


def _adamw(w, g, m, v, name):
    rows, cols = w.shape
    tm = 256 if rows % 256 == 0 else rows

    def body(w_ref, g_ref, m_ref, v_ref, d_ref, m2_ref, v2_ref):
        d_ref[...], m2_ref[...], v2_ref[...] = _adamw_math(w_ref[...], g_ref[...], m_ref[...], v_ref[...])

    shape = jax.ShapeDtypeStruct(w.shape, F32)
    return pl.pallas_call(
        body, grid=(rows // tm,), name=name,
        in_specs=[_rows(tm, cols)] * 4, out_specs=[_rows(tm, cols)] * 3, out_shape=[shape] * 3,
        compiler_params=_params(("arbitrary",)),
    )(w, g, m, v)


def _adamw_vectors(g5, ws, ms, vs):
    n = len(ws)

    def body(g_ref, *refs):
        ins, outs = refs[:3 * n], refs[3 * n:]
        for i in range(n):
            res = _adamw_math(ins[i][...], g_ref[i:i + 1, :], ins[n + i][...], ins[2 * n + i][...])
            for kind in range(3):
                outs[kind * n + i][...] = res[kind]

    shape = jax.ShapeDtypeStruct((1, D), F32)
    return pl.pallas_call(body, name="adamw_vectors", out_shape=[shape] * (3 * n), compiler_params=_params())(
        g5, *ws, *ms, *vs)


def kernel(x, norm_g, w_in, conv_w, conv_b, conv_ln_g, conv_ln_b, w_out, final_norm_g, loss_target, m_norm_g, m_w_in, m_conv_w, m_conv_b, m_conv_ln_g, m_conv_ln_b, m_w_out, m_final_norm_g, v_norm_g, v_w_in, v_conv_w, v_conv_b, v_conv_ln_g, v_conv_ln_b, v_w_out, v_final_norm_g):
    chip = 2 * lax.axis_index("x") + lax.axis_index("y")
    taps_shard = jnp.pad(conv_w[0], ((0, HALO - CONV_K), (0, 0)))
    wi_full, wo_full, cw_full = _gather_weights(w_in[0].astype(BF16), w_out[0].astype(BF16), taps_shard)

    gf = final_norm_g[None]
    grad_x, dw_in4, dw_out, small = _local_step(
        x[0], loss_target[0], norm_g, wi_full, cw_full, conv_b, conv_ln_g, conv_ln_b, wo_full, gf)
    dw_out4 = dw_out.reshape(NCHIP, WOUT_SHARD, D)

    ri, ro, rs = _exchange_halves(dw_in4, dw_out4, small)
    pi, po, ps = _add_halves(dw_in4, ri, dw_out4, ro, small, rs)
    ri, ro, rs = _exchange_chips(pi, po, ps)
    gi_half, go_half, g_small, g5, loss8 = _sum_chips(ri, ro, rs)
    gi2, go2 = _exchange_results(gi_half, go_half)
    g_w_in = gi2.reshape(D, CHUNK)
    g_w_out = go2.reshape(WOUT_SHARD, D)
    g_taps = lax.dynamic_slice(g_small, (ROW_TAPS, chip * CONVW_SHARD), (CONV_K, CONVW_SHARD))

    d_w_in, m2_w_in, v2_w_in = _adamw(w_in[0], g_w_in, m_w_in[0], v_w_in[0], "adamw_w_in")
    d_w_out, m2_w_out, v2_w_out = _adamw(w_out[0], g_w_out, m_w_out[0], v_w_out[0], "adamw_w_out")
    d_taps, m2_taps, v2_taps = _adamw(conv_w[0], g_taps, m_conv_w[0], v_conv_w[0], "adamw_conv_w")
    vec = _adamw_vectors(
        g5,
        (norm_g, conv_b, conv_ln_g, conv_ln_b, gf),
        (m_norm_g, m_conv_b, m_conv_ln_g, m_conv_ln_b, m_final_norm_g[None]),
        (v_norm_g, v_conv_b, v_conv_ln_g, v_conv_ln_b, v_final_norm_g[None]))
    d_vec, m2_vec, v2_vec = vec[0:5], vec[5:10], vec[10:15]

    def weight_order(ng, wi, cw, cb, lg, lb, wo, fg):
        return (ng, wi[None], cw[None], cb, lg, lb, wo[None], fg[0])

    grads = weight_order(g5[0:1], g_w_in, g_taps, g5[1:2], g5[2:3], g5[3:4], g_w_out, g5[4:5])
    deltas = weight_order(d_vec[0], d_w_in, d_taps, d_vec[1], d_vec[2], d_vec[3], d_w_out, d_vec[4])
    new_m = weight_order(m2_vec[0], m2_w_in, m2_taps, m2_vec[1], m2_vec[2], m2_vec[3], m2_w_out, m2_vec[4])
    new_v = weight_order(v2_vec[0], v2_w_in, v2_taps, v2_vec[1], v2_vec[2], v2_vec[3], v2_w_out, v2_vec[4])
    return (loss8[0, 0], grad_x[None], *grads, *deltas, *new_m, *new_v)
```

```python
import jax
import jax.numpy as jnp
from jax import lax
from jax.experimental import pallas as pl
from jax.experimental.pallas import tpu as pltpu

F32 = jnp.float32
BF16 = jnp.bfloat16

S = 4096
D = 1024
LANES = 128
HD = 64
NKV = 4
GQ = 4
KVW = NKV * HD
NCOL = 5632
CONV_K = 31
HALO = 32
BLK = 128
PATTERNS = (1, 4, 16)
NORM_EPS = 1e-6
LN_EPS = 1e-5
NEG = -1e30
OFF_Q, OFF_K, OFF_V, OFF_AG, OFF_CV, OFF_CG, OFF_CGATE = 0, 1024, 1280, 1536, 2560, 3584, 4608
NCHIP = 4
CHUNK = NCOL // NCHIP
WOUT_ROWS = 2 * D
WOUT_SHARD = WOUT_ROWS // NCHIP
CONVW_SHARD = D // NCHIP

ADAM_LR, ADAM_B1, ADAM_B2, ADAM_EPS, ADAM_WD, ADAM_STEP = 0.001, 0.9, 0.999, 1e-08, 0.01, 10

VMEM_LIMIT = 56 * 1024 * 1024


def _params(sem=None, vmem=VMEM_LIMIT):
    return pltpu.CompilerParams(dimension_semantics=sem, vmem_limit_bytes=vmem)


def _sigmoid(a):
    return 1.0 / (1.0 + jnp.exp(-a))


def _rows(tm, width):
    return pl.BlockSpec((tm, width), lambda i: (i, 0))


def _slabs(n):
    return jax.ShapeDtypeStruct((n, S, LANES), F32)


def _slab_rows(n, tm):
    return pl.BlockSpec((n, tm, LANES), lambda i: (0, i, 0))


def _resident(shape):
    return pl.BlockSpec(shape, lambda *_: (0,) * len(shape), pipeline_mode=pl.Buffered(1))


def _dot(a, b):
    return jnp.dot(a, b, preferred_element_type=F32)


def _dot_nt(a, b):
    return lax.dot_general(a, b, (((1,), (1,)), ((), ())), preferred_element_type=F32)


def _dot_tn(a, b):
    return lax.dot_general(a, b, (((0,), (0,)), ((), ())), preferred_element_type=F32)


def _inproj_fwd(x, g1, w_bf):
    tm = 256

    def body(x_ref, g_ref, w_ref, h_ref, q_ref, k_ref, v_ref, ag_ref, cv_ref, cg_ref, cgate_ref):
        xt = x_ref[...]
        r = lax.rsqrt(jnp.mean(xt * xt, axis=-1, keepdims=True) + NORM_EPS)
        h = (xt * r * g_ref[...]).astype(BF16)
        h_ref[...] = h
        q = _dot(h, w_ref[:, OFF_Q:OFF_Q + D]) * (HD ** -0.5)
        kv = _dot(h, w_ref[:, OFF_K:OFF_K + 2 * KVW])
        for sl in range(D // LANES):
            q_ref[sl] = q[:, sl * LANES:(sl + 1) * LANES]
        for sl in range(KVW // LANES):
            k_ref[sl] = kv[:, sl * LANES:(sl + 1) * LANES]
            v_ref[sl] = kv[:, KVW + sl * LANES:KVW + (sl + 1) * LANES]
        ag_ref[...] = _dot(h, w_ref[:, OFF_AG:OFF_AG + D])
        cv_ref[...] = _dot(h, w_ref[:, OFF_CV:OFF_CV + D])
        cg_ref[...] = _dot(h, w_ref[:, OFF_CG:OFF_CG + D])
        cgate_ref[...] = _dot(h, w_ref[:, OFF_CGATE:OFF_CGATE + D])

    big = jax.ShapeDtypeStruct((S, D), F32)
    return pl.pallas_call(
        body, grid=(S // tm,), name="inproj_fwd",
        in_specs=[_rows(tm, D), _resident((1, D)), _resident((D, NCOL))],
        out_specs=[_rows(tm, D), _slab_rows(D // LANES, tm), _slab_rows(KVW // LANES, tm), _slab_rows(KVW // LANES, tm),
                   _rows(tm, D), _rows(tm, D), _rows(tm, D), _rows(tm, D)],
        out_shape=[jax.ShapeDtypeStruct((S, D), BF16), _slabs(D // LANES), _slabs(KVW // LANES), _slabs(KVW // LANES),
                   big, big, big, big],
        compiler_params=_params(("arbitrary",)),
    )(x, g1, w_bf)


def _bias_table(d):
    h = jnp.arange(NKV * GQ, dtype=F32)
    slopes = jnp.exp2(-8.0 * (h + 1.0) / (NKV * GQ))
    qi = jnp.arange(BLK)[:, None]
    kj = jnp.arange(2 * BLK)[None, :]
    dist = BLK + qi - kj
    window = (dist >= 0) & (dist <= BLK)
    bias = -slopes[:, None, None] * (dist * d).astype(F32)[None]
    has_prev = jnp.stack([jnp.broadcast_to(kj >= BLK, (BLK, 2 * BLK)), jnp.ones((BLK, 2 * BLK), bool)])
    valid = window[None] & has_prev
    tab = jnp.where(valid[:, None], bias[None], NEG)
    return tab.reshape(2, NKV, GQ * BLK, 2 * BLK)


def _sub_rows(start, d):
    if d == 1:
        return pl.ds(pl.multiple_of(start, BLK), BLK)
    return pl.ds(start, BLK, stride=d)


def _block_rows(idx, d):
    b, r = idx // d, idx % d
    start = b * (BLK * d) + r
    return b, _sub_rows(start, d), _sub_rows(jnp.maximum(start - BLK * d, r), d)


def _stack_q(ref, rows):
    t0, t1 = ref[0, rows, :], ref[1, rows, :]
    return jnp.concatenate([t0[:, :HD], t0[:, HD:], t1[:, :HD], t1[:, HD:]], axis=0).astype(BF16)


def _unstack_q(val):
    return (jnp.concatenate([val[0:BLK], val[BLK:2 * BLK]], axis=1),
            jnp.concatenate([val[2 * BLK:3 * BLK], val[3 * BLK:4 * BLK]], axis=1))


def _kv_window(ref, prow, rows, odd):
    t = jnp.concatenate([ref[0, prow, :], ref[0, rows, :]], axis=0)
    return jnp.where(odd, t[:, HD:], t[:, :HD]).astype(BF16)


def _lane_columns(vals):
    lane = lax.broadcasted_iota(jnp.int32, (BLK, LANES), 1)
    tile = jnp.zeros((BLK, LANES), F32)
    for g in range(GQ):
        tile = jnp.where(lane == g, vals[g * BLK:(g + 1) * BLK], tile)
    return tile


def _attn_specs():
    q_like = pl.BlockSpec((2, S, LANES), lambda j: (j, 0, 0))
    kv = pl.BlockSpec((1, S, LANES), lambda j: (j // 2, 0, 0))
    per_head = pl.BlockSpec((1, S, LANES), lambda j: (j, 0, 0))
    bias = pl.BlockSpec((2, 1, GQ * BLK, 2 * BLK), lambda j: (0, j, 0, 0))
    return q_like, kv, per_head, bias


def _attn_fwd(q, k, v, bias, d):
    def body(q_ref, k_ref, v_ref, b_ref, o_ref, l_ref):
        odd = pl.program_id(0) % 2 == 1

        def block(idx, carry):
            b, rows, prow = _block_rows(idx, d)
            qs = _stack_q(q_ref, rows)
            kw = _kv_window(k_ref, prow, rows, odd)
            vw = _kv_window(v_ref, prow, rows, odd)
            s = _dot_nt(qs, kw) + b_ref[jnp.minimum(b, 1), 0]
            m = jnp.max(s, axis=1, keepdims=True)
            p = jnp.exp(s - m)
            l = jnp.sum(p, axis=1, keepdims=True)
            o = _dot(p.astype(BF16), vw) / l
            o_ref[0, rows, :], o_ref[1, rows, :] = _unstack_q(o)
            l_ref[0, rows, :] = _lane_columns(m + jnp.log(l))
            return carry

        lax.fori_loop(0, S // BLK, block, 0)

    q_like, kv, per_head, bias_spec = _attn_specs()
    return pl.pallas_call(
        body, grid=(NKV,), name=f"attn_fwd_d{d}",
        in_specs=[q_like, kv, kv, bias_spec],
        out_specs=[q_like, per_head],
        out_shape=[_slabs(D // LANES), _slabs(NKV)],
        compiler_params=_params(("arbitrary",)),
    )(q, k, v, bias)


def _attn_combine(outs, lses, a_gate):
    tm = 256

    def body(o1, o2, o3, l1, l2, l3, ag_ref, o_ref, lse_ref, y_ref):
        lane = lax.broadcasted_iota(jnp.int32, (tm, LANES), 1)
        for j in range(NKV):
            a, b, c = l1[j], l2[j], l3[j]
            m = jnp.maximum(jnp.maximum(a, b), c)
            ea, eb, ec = jnp.exp(a - m), jnp.exp(b - m), jnp.exp(c - m)
            den = ea + eb + ec
            lse_ref[j] = jnp.where(lane < GQ, m + jnp.log(den), 0.0)
            inv = 1.0 / den
            for half in range(2):
                sl = 2 * j + half

                def spread(w):
                    return jnp.where(lane < HD, w[:, 2 * half:2 * half + 1], w[:, 2 * half + 1:2 * half + 2])

                o = spread(ea * inv) * o1[sl] + spread(eb * inv) * o2[sl] + spread(ec * inv) * o3[sl]
                o_ref[sl] = o
                cols = slice(sl * LANES, (sl + 1) * LANES)
                ag = ag_ref[:, cols]
                y_ref[:, cols] = (o * (ag * _sigmoid(ag))).astype(BF16)

    wide, narrow = _slab_rows(D // LANES, tm), _slab_rows(NKV, tm)
    return pl.pallas_call(
        body, grid=(S // tm,), name="attn_combine",
        in_specs=[wide] * 3 + [narrow] * 3 + [_rows(tm, D)],
        out_specs=[wide, narrow, _rows(tm, D)],
        out_shape=[_slabs(D // LANES), _slabs(NKV), jax.ShapeDtypeStruct((S, D), BF16)],
        compiler_params=_params(("arbitrary",)),
    )(*outs, *lses, a_gate)


def _delta_selector():
    sl = jnp.arange(D // LANES)[:, None]
    lane_in = jnp.arange(LANES)[None, :]
    target = GQ + 2 * (sl % 2) + lane_in // HD
    return (jnp.arange(LANES)[None, None, :] == target[:, :, None]).astype(BF16)


def _attn_gate_bwd(dy_att, o, a_gate, lse, selector):
    tm = 256

    def body(dy_ref, o_ref, ag_ref, l_ref, e_ref, do_ref, dag_ref, ld_ref):
        for j in range(NKV):
            ld = l_ref[j]
            for sl in (2 * j, 2 * j + 1):
                cols = slice(sl * LANES, (sl + 1) * LANES)
                dy, ag, o_ = dy_ref[:, cols], ag_ref[:, cols], o_ref[sl]
                sg = _sigmoid(ag)
                do = dy * (ag * sg)
                do_ref[sl] = do
                dag_ref[:, cols] = dy * o_ * (sg * (1.0 + ag * (1.0 - sg)))
                prod = do * o_
                hi = prod.astype(BF16)
                lo = (prod - hi.astype(F32)).astype(BF16)
                ld = ld + _dot(hi, e_ref[sl]) + _dot(lo, e_ref[sl])
            ld_ref[j] = ld

    return pl.pallas_call(
        body, grid=(S // tm,), name="attn_gate_bwd",
        in_specs=[_rows(tm, D), _slab_rows(D // LANES, tm), _rows(tm, D), _slab_rows(NKV, tm),
                  _resident((D // LANES, LANES, LANES))],
        out_specs=[_slab_rows(D // LANES, tm), _rows(tm, D), _slab_rows(NKV, tm)],
        out_shape=[_slabs(D // LANES), jax.ShapeDtypeStruct((S, D), F32), _slabs(NKV)],
        compiler_params=_params(("arbitrary",)),
    )(dy_att, o, a_gate, lse, selector)


def _attn_bwd(q, k, v, do, ld, bias, d):
    def body(q_ref, do_ref, ld_ref, k_ref, v_ref, b_ref, dq_ref, dkv_ref):
        odd = pl.program_id(0) % 2 == 1
        dkv_ref[...] = jnp.zeros_like(dkv_ref)

        def block(idx, carry):
            b, rows, prow = _block_rows(idx, d)
            qs = _stack_q(q_ref, rows)
            dos = _stack_q(do_ref, rows)
            lt = ld_ref[0, rows, :]
            lse = jnp.concatenate([lt[:, g:g + 1] for g in range(GQ)], axis=0)
            delta = jnp.concatenate([lt[:, GQ + g:GQ + g + 1] for g in range(GQ)], axis=0)
            kw = _kv_window(k_ref, prow, rows, odd)
            vw = _kv_window(v_ref, prow, rows, odd)
            s = _dot_nt(qs, kw) + b_ref[jnp.minimum(b, 1), 0]
            p = jnp.exp(s - lse)
            dvw = _dot_tn(p.astype(BF16), dos)
            dp = _dot_nt(dos, vw)
            ds = (p * (dp - delta)).astype(BF16)
            dq_ref[0, rows, :], dq_ref[1, rows, :] = _unstack_q(_dot(ds, kw))
            dkw = _dot_tn(ds, qs)
            dkv_ref[0, rows, :] = dkv_ref[0, rows, :] + jnp.concatenate([dkw[BLK:], dvw[BLK:]], axis=1)

            @pl.when(b >= 1)
            def _():
                dkv_ref[0, prow, :] = dkv_ref[0, prow, :] + jnp.concatenate([dkw[:BLK], dvw[:BLK]], axis=1)

            return carry

        lax.fori_loop(0, S // BLK, block, 0)

    q_like, kv, per_head, bias_spec = _attn_specs()
    return pl.pallas_call(
        body, grid=(NKV,), name=f"attn_bwd_d{d}",
        in_specs=[q_like, q_like, per_head, kv, kv, bias_spec],
        out_specs=[q_like, per_head],
        out_shape=[_slabs(D // LANES), _slabs(NKV)],
        compiler_params=_params(("arbitrary",)),
    )(q, do, ld, k, v, bias)


CONV_T = 128


def _halo_before(i):
    return (jnp.maximum(i * (CONV_T // HALO) - 1, 0), 0)


def _halo_after(i):
    return (jnp.minimum((i + 1) * (CONV_T // HALO), S // HALO - 1), 0)


def _conv_fwd(c_val, c_glu, c_gate, conv_w, conv_b, ln_g, ln_b):
    T = CONV_T

    def body(cv_ref, cg_ref, cvh_ref, cgh_ref, gate_ref, w_ref, b_ref, lg_ref, lb_ref, u_ref, y_ref, win):
        i = pl.program_id(0)
        win[HALO:HALO + T, :] = cv_ref[...] * _sigmoid(cg_ref[...])
        win[0:HALO, :] = jnp.where(i > 0, cvh_ref[...] * _sigmoid(cgh_ref[...]), 0.0)
        u = jnp.broadcast_to(b_ref[...], (T, D))
        for j in range(CONV_K):
            lo = HALO - (CONV_K - 1) + j
            u = u + w_ref[j:j + 1, :] * win[lo:lo + T, :]
        u_ref[...] = u
        mu = jnp.mean(u, axis=-1, keepdims=True)
        uc = u - mu
        rstd = lax.rsqrt(jnp.mean(uc * uc, axis=-1, keepdims=True) + LN_EPS)
        nrm = uc * rstd * lg_ref[...] + lb_ref[...]
        gate = gate_ref[...]
        y_ref[...] = (nrm * _sigmoid(nrm) * (gate * _sigmoid(gate))).astype(BF16)

    halo = pl.BlockSpec((HALO, D), _halo_before)
    return pl.pallas_call(
        body, grid=(S // T,), name="conv_fwd",
        in_specs=[_rows(T, D), _rows(T, D), halo, halo, _rows(T, D),
                  _resident((HALO, D)), _resident((1, D)), _resident((1, D)), _resident((1, D))],
        out_specs=[_rows(T, D), _rows(T, D)],
        out_shape=[jax.ShapeDtypeStruct((S, D), F32), jax.ShapeDtypeStruct((S, D), BF16)],
        scratch_shapes=[pltpu.VMEM((T + HALO, D), F32)],
        compiler_params=_params(("arbitrary",)),
    )(c_val, c_glu, c_val, c_glu, c_gate, conv_w, conv_b, ln_g, ln_b)


def _conv_bwd_rows(u, c_gate, dy_conv, ln_g, ln_b):
    tm = 256

    def body(u_ref, gate_ref, dy_ref, lg_ref, lb_ref, du_ref, dgate_ref, st_ref):
        @pl.when(pl.program_id(0) == 0)
        def _():
            st_ref[...] = jnp.zeros_like(st_ref)

        u, gate, dy = u_ref[...], gate_ref[...], dy_ref[...]
        mu = jnp.mean(u, axis=-1, keepdims=True)
        uc = u - mu
        rstd = lax.rsqrt(jnp.mean(uc * uc, axis=-1, keepdims=True) + LN_EPS)
        z = uc * rstd
        nrm = z * lg_ref[...] + lb_ref[...]
        sn, sg = _sigmoid(nrm), _sigmoid(gate)
        dgate_ref[...] = dy * (nrm * sn) * (sg * (1.0 + gate * (1.0 - sg)))
        dn = dy * (gate * sg) * (sn * (1.0 + nrm * (1.0 - sn)))
        dz = dn * lg_ref[...]
        du = rstd * (dz - jnp.mean(dz, axis=-1, keepdims=True) - z * jnp.mean(dz * z, axis=-1, keepdims=True))
        du_ref[...] = du
        st_ref[0:1, :] += jnp.sum(dn * z, axis=0, keepdims=True)
        st_ref[1:2, :] += jnp.sum(dn, axis=0, keepdims=True)
        st_ref[2:3, :] += jnp.sum(du, axis=0, keepdims=True)

    big = jax.ShapeDtypeStruct((S, D), F32)
    return pl.pallas_call(
        body, grid=(S // tm,), name="conv_bwd_rows",
        in_specs=[_rows(tm, D)] * 3 + [_resident((1, D)), _resident((1, D))],
        out_specs=[_rows(tm, D), _rows(tm, D), pl.BlockSpec((8, D), lambda i: (0, 0))],
        out_shape=[big, big, jax.ShapeDtypeStruct((8, D), F32)],
        compiler_params=_params(("arbitrary",)),
    )(u, c_gate, dy_conv, ln_g, ln_b)


def _conv_bwd_taps(du, c_val, c_glu, conv_w):
    T = CONV_T
    last = S // T - 1

    def body(du_ref, dua_ref, cv_ref, cg_ref, cvh_ref, cgh_ref, w_ref, dcv_ref, dcg_ref, dw_ref, hwin, dwin):
        i = pl.program_id(0)

        @pl.when(i == 0)
        def _():
            dw_ref[...] = jnp.zeros_like(dw_ref)

        cv, sg = cv_ref[...], _sigmoid(cg_ref[...])
        hwin[HALO:HALO + T, :] = cv * sg
        hwin[0:HALO, :] = jnp.where(i > 0, cvh_ref[...] * _sigmoid(cgh_ref[...]), 0.0)
        du = du_ref[...]
        dwin[0:T, :] = du
        dwin[T:T + HALO, :] = jnp.where(i < last, dua_ref[...], 0.0)
        dh = jnp.zeros((T, D), F32)
        for j in range(CONV_K):
            lo = HALO - (CONV_K - 1) + j
            dw_ref[j:j + 1, :] += jnp.sum(du * hwin[lo:lo + T, :], axis=0, keepdims=True)
            hi = CONV_K - 1 - j
            dh = dh + w_ref[j:j + 1, :] * dwin[hi:hi + T, :]
        dcv_ref[...] = dh * sg
        dcg_ref[...] = dh * cv * (sg * (1.0 - sg))

    before = pl.BlockSpec((HALO, D), _halo_before)
    after = pl.BlockSpec((HALO, D), _halo_after)
    big = jax.ShapeDtypeStruct((S, D), F32)
    return pl.pallas_call(
        body, grid=(S // T,), name="conv_bwd_taps",
        in_specs=[_rows(T, D), after, _rows(T, D), _rows(T, D), before, before, _resident((HALO, D))],
        out_specs=[_rows(T, D), _rows(T, D), pl.BlockSpec((HALO, D), lambda i: (0, 0))],
        out_shape=[big, big, jax.ShapeDtypeStruct((HALO, D), F32)],
        scratch_shapes=[pltpu.VMEM((T + HALO, D), F32), pltpu.VMEM((T + HALO, D), F32)],
        compiler_params=_params(("arbitrary",)),
    )(du, du, c_val, c_glu, c_val, c_glu, conv_w)


def _outproj_loss(y_att, y_conv, w_out_bf, x, target, gf):
    tm = 256

    def body(ya_ref, yc_ref, w_ref, x_ref, t_ref, gf_ref, dx2_ref, dya_ref, dyc_ref, dw_ref, st_ref, acc):
        @pl.when(pl.program_id(0) == 0)
        def _():
            acc[...] = jnp.zeros_like(acc)
            st_ref[...] = jnp.zeros_like(st_ref)

        ya, yc = ya_ref[...], yc_ref[...]
        x2 = x_ref[...] + _dot(ya, w_ref[0:D, :]) + _dot(yc, w_ref[D:2 * D, :])
        r = lax.rsqrt(jnp.mean(x2 * x2, axis=-1, keepdims=True) + NORM_EPS)
        xn = x2 * r
        err = xn * gf_ref[...] - t_ref[...]
        dout = err * (1.0 / D)
        dxn = dout * gf_ref[...]
        dx2 = r * (dxn - xn * jnp.mean(dxn * xn, axis=-1, keepdims=True))
        dx2_ref[...] = dx2
        dx2b = dx2.astype(BF16)
        dya_ref[...] = _dot_nt(dx2b, w_ref[0:D, :])
        dyc_ref[...] = _dot_nt(dx2b, w_ref[D:2 * D, :])
        acc[0:D, :] += _dot_tn(ya, dx2b)
        acc[D:2 * D, :] += _dot_tn(yc, dx2b)
        st_ref[0:1, :] += jnp.sum(dout * xn, axis=0, keepdims=True)
        st_ref[1:2, :] += jnp.sum(err * err, axis=0, keepdims=True) * (0.5 / D)

        @pl.when(pl.program_id(0) == S // tm - 1)
        def _():
            dw_ref[...] = acc[...].astype(BF16)

    big = jax.ShapeDtypeStruct((S, D), F32)
    return pl.pallas_call(
        body, grid=(S // tm,), name="outproj_loss",
        in_specs=[_rows(tm, D), _rows(tm, D), _resident((WOUT_ROWS, D)), _rows(tm, D), _rows(tm, D), _resident((1, D))],
        out_specs=[_rows(tm, D), _rows(tm, D), _rows(tm, D),
                   pl.BlockSpec((WOUT_ROWS, D), lambda i: (0, 0)), pl.BlockSpec((8, D), lambda i: (0, 0))],
        out_shape=[big, big, big, jax.ShapeDtypeStruct((WOUT_ROWS, D), BF16), jax.ShapeDtypeStruct((8, D), F32)],
        scratch_shapes=[pltpu.VMEM((WOUT_ROWS, D), F32)],
        compiler_params=_params(("arbitrary",)),
    )(y_att, y_conv, w_out_bf, x, target, gf)


def _inproj_bwd_x(dqs, dkvs, dag, dcv, dcg, dcgate, w_bf, x, g1, dx2):
    tm = 256

    def body(dq1, dq2, dq3, dkv1, dkv2, dkv3, dag_ref, dcv_ref, dcg_ref, dcgate_ref,
             w_ref, x_ref, g_ref, dx2_ref, dp_ref, gx_ref, st_ref):
        @pl.when(pl.program_id(0) == 0)
        def _():
            st_ref[...] = jnp.zeros_like(st_ref)

        for sl in range(D // LANES):
            dq = (dq1[sl] + dq2[sl] + dq3[sl]) * (HD ** -0.5)
            dp_ref[:, OFF_Q + sl * LANES:OFF_Q + (sl + 1) * LANES] = dq.astype(BF16)
        for j in range(NKV):
            dkv = (dkv1[j] + dkv2[j] + dkv3[j]).astype(BF16)
            dp_ref[:, OFF_K + j * HD:OFF_K + (j + 1) * HD] = dkv[:, :HD]
            dp_ref[:, OFF_V + j * HD:OFF_V + (j + 1) * HD] = dkv[:, HD:]
        for off, ref in ((OFF_AG, dag_ref), (OFF_CV, dcv_ref), (OFF_CG, dcg_ref), (OFF_CGATE, dcgate_ref)):
            dp_ref[:, off:off + D] = ref[...].astype(BF16)
        dh = jnp.zeros((tm, D), F32)
        for off, width in ((OFF_Q, D), (OFF_K, 2 * KVW), (OFF_AG, D), (OFF_CV, D), (OFF_CG, D), (OFF_CGATE, D)):
            dh = dh + _dot_nt(dp_ref[:, off:off + width], w_ref[:, off:off + width])
        xt = x_ref[...]
        r = lax.rsqrt(jnp.mean(xt * xt, axis=-1, keepdims=True) + NORM_EPS)
        xn = xt * r
        dxn = dh * g_ref[...]
        gx_ref[...] = dx2_ref[...] + r * (dxn - xn * jnp.mean(dxn * xn, axis=-1, keepdims=True))
        st_ref[0:1, :] += jnp.sum(dh * xn, axis=0, keepdims=True)

    return pl.pallas_call(
        body, grid=(S // tm,), name="inproj_bwd_x",
        in_specs=[_slab_rows(D // LANES, tm)] * 3 + [_slab_rows(NKV, tm)] * 3 + [_rows(tm, D)] * 4
        + [_resident((D, NCOL)), _rows(tm, D), _resident((1, D)), _rows(tm, D)],
        out_specs=[_rows(tm, NCOL), _rows(tm, D), pl.BlockSpec((8, D), lambda i: (0, 0))],
        out_shape=[jax.ShapeDtypeStruct((S, NCOL), BF16), jax.ShapeDtypeStruct((S, D), F32),
                   jax.ShapeDtypeStruct((8, D), F32)],
        compiler_params=_params(("arbitrary",)),
    )(*dqs, *dkvs, dag, dcv, dcg, dcgate, w_bf, x, g1, dx2)


def _inproj_bwd_w(h, dproj):
    tk = 1024
    nk = S // tk

    def body(h_ref, dp_ref, o_ref, acc):
        i = pl.program_id(1)

        @pl.when(i == 0)
        def _():
            acc[...] = jnp.zeros_like(acc)

        acc[...] += _dot_tn(h_ref[...], dp_ref[...])

        @pl.when(i == nk - 1)
        def _():
            o_ref[0] = acc[...].astype(BF16)

    return pl.pallas_call(
        body, grid=(NCHIP, nk), name="inproj_bwd_w",
        in_specs=[pl.BlockSpec((tk, D), lambda c, i: (i, 0)), pl.BlockSpec((tk, CHUNK), lambda c, i: (i, c))],
        out_specs=pl.BlockSpec((1, D, CHUNK), lambda c, i: (c, 0, 0)),
        out_shape=jax.ShapeDtypeStruct((NCHIP, D, CHUNK), BF16),
        scratch_shapes=[pltpu.VMEM((D, CHUNK), F32)],
        compiler_params=_params(("arbitrary", "arbitrary")),
    )(h, dproj)


def _local_step(x, target, g1, w_in_bf, conv_w, conv_b, ln_g, ln_b, w_out_bf, gf):
    h, q, k, v, a_gate, c_val, c_glu, c_gate = _inproj_fwd(x, g1, w_in_bf)
    tables = [_bias_table(d) for d in PATTERNS]
    outs, lses = zip(*[_attn_fwd(q, k, v, t, d) for t, d in zip(tables, PATTERNS)])
    o, lse, y_att = _attn_combine(outs, lses, a_gate)
    u, y_conv = _conv_fwd(c_val, c_glu, c_gate, conv_w, conv_b, ln_g, ln_b)
    dx2, dy_att, dy_conv, dw_out, st_out = _outproj_loss(y_att, y_conv, w_out_bf, x, target, gf)

    do, da_gate, ld = _attn_gate_bwd(dy_att, o, a_gate, lse, _delta_selector())
    dqs, dkvs = zip(*[_attn_bwd(q, k, v, do, ld, t, d) for t, d in zip(tables, PATTERNS)])

    du, dc_gate, st_conv = _conv_bwd_rows(u, c_gate, dy_conv, ln_g, ln_b)
    dc_val, dc_glu, dconv_w = _conv_bwd_taps(du, c_val, c_glu, conv_w)

    dproj, grad_x, st_in = _inproj_bwd_x(dqs, dkvs, da_gate, dc_val, dc_glu, dc_gate, w_in_bf, x, g1, dx2)
    dw_in = _inproj_bwd_w(h, dproj)
    small = jnp.concatenate([st_in, st_conv, st_out, dconv_w], axis=0)
    return grad_x, dw_in, dw_out, small


ROW_NORM_G, ROW_LN_G, ROW_LN_B, ROW_CONV_B, ROW_FINAL_G, ROW_LOSS, ROW_TAPS = 0, 8, 9, 10, 16, 17, 24
SMALL_ROWS = 24 + HALO


MESH = pl.DeviceIdType.MESH
ANY = pl.BlockSpec(memory_space=pl.ANY)
CHIP_FLIPS = ((1, 0), (0, 1), (1, 1))


def _pos():
    return lax.axis_index("x"), lax.axis_index("y"), lax.axis_index("c")


def _flip(v, f):
    return 1 - v if f else v


def _ds(start, size, align=None):
    return pl.ds(pl.multiple_of(start, align or size), size)


def _place_shards(wi, wo, cw, where):
    steps = 4

    def body(where_ref, wi_ref, wo_ref, cw_ref, wi_full, wo_full, cw_full):
        wi_full[...] = wi_ref[...].astype(BF16)
        wo_full[...] = wo_ref[...].astype(BF16)
        cw_full[...] = cw_ref[...]

    grid_spec = pltpu.PrefetchScalarGridSpec(
        num_scalar_prefetch=1, grid=(steps,),
        in_specs=[pl.BlockSpec((D // steps, CHUNK), lambda i, w: (i, 0)),
                  pl.BlockSpec((WOUT_SHARD // steps, D), lambda i, w: (i, 0)),
                  pl.BlockSpec((HALO, CONVW_SHARD), lambda i, w: (0, 0))],
        out_specs=[pl.BlockSpec((D // steps, CHUNK), lambda i, w: (i, w[0])),
                   pl.BlockSpec((WOUT_SHARD // steps, D), lambda i, w: (w[0] * steps + i, 0)),
                   pl.BlockSpec((HALO, CONVW_SHARD), lambda i, w: (0, w[0]))])
    return pl.pallas_call(
        body, grid_spec=grid_spec, name="place_shards",
        out_shape=[jax.ShapeDtypeStruct((D, NCOL), BF16), jax.ShapeDtypeStruct((WOUT_ROWS, D), BF16),
                   jax.ShapeDtypeStruct((HALO, D), F32)],
        compiler_params=_params(("arbitrary",)),
    )(where, wi, wo, cw)


def _gather_weights(wi_full, wo_full, cw_full):
    halves = (D // 2, WOUT_SHARD // 2, HALO // 2)
    n_ici = 3 * len(CHIP_FLIPS)

    def body(_wi, _wo, _cw, wi_full, wo_full, cw_full, send, recv):
        x, y, c = _pos()

        def region(a, px, py, half):
            chip = 2 * px + py
            n = halves[a]
            if a == 0:
                return wi_full.at[_ds(half * n, n), _ds(chip * CHUNK, CHUNK, 128)]
            if a == 1:
                return wo_full.at[_ds(chip * WOUT_SHARD + half * n, n), :]
            return cw_full.at[_ds(half * n, n), _ds(chip * CONVW_SHARD, CONVW_SHARD, 128)]

        def remote(k, src, dst, dev):
            return pltpu.make_async_remote_copy(src_ref=src, dst_ref=dst, send_sem=send.at[k], recv_sem=recv.at[k],
                                                device_id=dev, device_id_type=MESH)

        sends = []
        for a in range(3):
            for j, (fx, fy) in enumerate(CHIP_FLIPS):
                mine = region(a, x, y, c)
                cp = remote(3 * a + j, mine, mine, (_flip(x, fx), _flip(y, fy), c))
                cp.start()
                sends.append(cp)
        for a in range(3):
            for j, (fx, fy) in enumerate(CHIP_FLIPS):
                px, py = _flip(x, fx), _flip(y, fy)
                got = region(a, px, py, c)
                remote(3 * a + j, got, got, (px, py, c)).wait_recv()
                cp = remote(n_ici + 3 * a + j, got, got, (x, y, 1 - c))
                cp.start()
                sends.append(cp)
        for a in range(3):
            for j, (fx, fy) in enumerate(CHIP_FLIPS):
                got = region(a, _flip(x, fx), _flip(y, fy), 1 - c)
                remote(n_ici + 3 * a + j, got, got, (x, y, 1 - c)).wait_recv()
        for cp in sends:
            cp.wait_send()

    return pl.pallas_call(
        body, name="gather_weights",
        in_specs=[ANY, ANY, ANY], out_specs=[ANY, ANY, ANY], input_output_aliases={0: 0, 1: 1, 2: 2},
        out_shape=[jax.ShapeDtypeStruct((D, NCOL), BF16), jax.ShapeDtypeStruct((WOUT_ROWS, D), BF16),
                   jax.ShapeDtypeStruct((HALO, D), F32)],
        scratch_shapes=[pltpu.SemaphoreType.DMA((2 * n_ici,)), pltpu.SemaphoreType.DMA((2 * n_ici,))],
    )(wi_full, wo_full, cw_full)


def _exchange_halves(gi4, go4, small):
    def body(gi_ref, go_ref, sm_ref, ri_ref, ro_ref, rs_ref, send, recv):
        x, y, c = _pos()
        sib = (x, y, 1 - c)
        copies = [
            (gi_ref.at[:, _ds((1 - c) * (D // 2), D // 2), :], ri_ref),
            (go_ref.at[:, _ds((1 - c) * (WOUT_SHARD // 2), WOUT_SHARD // 2), :], ro_ref),
            (sm_ref, rs_ref),
        ]
        cps = [pltpu.make_async_remote_copy(src_ref=s_, dst_ref=d_, send_sem=send.at[k], recv_sem=recv.at[k],
                                            device_id=sib, device_id_type=MESH) for k, (s_, d_) in enumerate(copies)]
        for cp in cps:
            cp.start()
        for cp in cps:
            cp.wait()

    return pl.pallas_call(
        body, name="exchange_halves",
        in_specs=[ANY, ANY, ANY], out_specs=[ANY, ANY, ANY],
        out_shape=[jax.ShapeDtypeStruct((NCHIP, D // 2, CHUNK), BF16),
                   jax.ShapeDtypeStruct((NCHIP, WOUT_SHARD // 2, D), BF16),
                   jax.ShapeDtypeStruct((SMALL_ROWS, D), F32)],
        scratch_shapes=[pltpu.SemaphoreType.DMA((3,)), pltpu.SemaphoreType.DMA((3,))],
    )(gi4, go4, small)


def _add_halves(gi4, ri, go4, ro, small, rs):
    hi, ho = D // 2, WOUT_SHARD // 2

    def body(gi_ref, ri_ref, go_ref, ro_ref, sm_ref, rs_ref, pi_ref, po_ref, ps_ref):
        c = lax.axis_index("c")
        pi_ref[0] = (gi_ref[0, _ds(c * hi, hi), :].astype(F32) + ri_ref[0].astype(F32)).astype(BF16)
        po_ref[0] = (go_ref[0, _ds(c * ho, ho), :].astype(F32) + ro_ref[0].astype(F32)).astype(BF16)
        ps_ref[...] = sm_ref[...] + rs_ref[...]

    blk = lambda n, w: pl.BlockSpec((1, n, w), lambda k: (k, 0, 0))
    whole = pl.BlockSpec((SMALL_ROWS, D), lambda k: (0, 0))
    return pl.pallas_call(
        body, grid=(NCHIP,), name="add_halves",
        in_specs=[blk(D, CHUNK), blk(hi, CHUNK), blk(WOUT_SHARD, D), blk(ho, D), whole, whole],
        out_specs=[blk(hi, CHUNK), blk(ho, D), whole],
        out_shape=[jax.ShapeDtypeStruct((NCHIP, hi, CHUNK), BF16), jax.ShapeDtypeStruct((NCHIP, ho, D), BF16),
                   jax.ShapeDtypeStruct((SMALL_ROWS, D), F32)],
        compiler_params=_params(("arbitrary",)),
    )(gi4, ri, go4, ro, small, rs)


def _exchange_chips(pi, po, ps):
    def body(pi_ref, po_ref, ps_ref, ri_ref, ro_ref, rs_ref, send, recv):
        x, y, c = _pos()
        me = 2 * x + y
        srcs = (pi_ref, po_ref, ps_ref)
        dsts = (ri_ref, ro_ref, rs_ref)

        def piece(a, chip):
            return srcs[a] if a == 2 else srcs[a].at[chip]

        sends = []
        for a in range(3):
            for j, (fx, fy) in enumerate(CHIP_FLIPS):
                px, py = _flip(x, fx), _flip(y, fy)
                cp = pltpu.make_async_remote_copy(
                    src_ref=piece(a, 2 * px + py), dst_ref=dsts[a].at[me], send_sem=send.at[3 * a + j],
                    recv_sem=recv.at[3 * a + j], device_id=(px, py, c), device_id_type=MESH)
                cp.start()
                sends.append(cp)
        for a in range(3):
            for j, (fx, fy) in enumerate(CHIP_FLIPS):
                px, py = _flip(x, fx), _flip(y, fy)
                got = dsts[a].at[2 * px + py]
                pltpu.make_async_remote_copy(src_ref=got, dst_ref=got, send_sem=send.at[3 * a + j],
                                             recv_sem=recv.at[3 * a + j], device_id=(px, py, c),
                                             device_id_type=MESH).wait_recv()
        for cp in sends:
            cp.wait_send()

    return pl.pallas_call(
        body, name="exchange_chips",
        in_specs=[ANY, ANY, ANY], out_specs=[ANY, ANY, ANY],
        out_shape=[jax.ShapeDtypeStruct((NCHIP, D // 2, CHUNK), BF16),
                   jax.ShapeDtypeStruct((NCHIP, WOUT_SHARD // 2, D), BF16),
                   jax.ShapeDtypeStruct((NCHIP, SMALL_ROWS, D), F32)],
        scratch_shapes=[pltpu.SemaphoreType.DMA((9,)), pltpu.SemaphoreType.DMA((9,))],
    )(pi, po, ps)


def _sum_chips(ri, ro, rs, pi, po, ps, where):
    def body(w_ref, ri_ref, ro_ref, rs_ref, pi_ref, po_ref, ps_ref, gi_ref, go_ref, gs_ref, g5_ref, loss_ref,
             acc_i, acc_o, acc_s):
        k = pl.program_id(0)
        accs = (acc_i, acc_o, acc_s)

        @pl.when(k == 0)
        def _():
            for acc in accs:
                acc[...] = jnp.zeros_like(acc)

        @pl.when(k == w_ref[0])
        def _():
            for acc, val in zip(accs, (pi_ref[0], po_ref[0], ps_ref[...])):
                acc[...] += val.astype(F32)

        @pl.when(k != w_ref[0])
        def _():
            for acc, ref in zip(accs, (ri_ref, ro_ref, rs_ref)):
                acc[...] += ref[0].astype(F32)

        @pl.when(k == NCHIP - 1)
        def _():
            gi_ref[0] = acc_i[...]
            go_ref[0] = acc_o[...]
            gs_ref[...] = acc_s[...]
            g5_ref[...] = jnp.zeros_like(g5_ref)
            for i, row in enumerate((ROW_NORM_G, ROW_CONV_B, ROW_LN_G, ROW_LN_B, ROW_FINAL_G)):
                g5_ref[i:i + 1, :] = acc_s[row:row + 1, :]
            loss = jnp.sum(acc_s[ROW_LOSS:ROW_LOSS + 1, :], axis=1, keepdims=True)
            loss_ref[...] = jnp.broadcast_to(loss, loss_ref.shape)

    def sent(k, w):
        return jnp.where(k == w[0], (k + 1) % NCHIP, k)

    hi, ho = D // 2, WOUT_SHARD // 2
    const = lambda shape: pl.BlockSpec(shape, lambda k, w: (0,) * len(shape))
    grid_spec = pltpu.PrefetchScalarGridSpec(
        num_scalar_prefetch=1, grid=(NCHIP,),
        in_specs=[pl.BlockSpec((1, hi, CHUNK), lambda k, w: (sent(k, w), 0, 0)),
                  pl.BlockSpec((1, ho, D), lambda k, w: (sent(k, w), 0, 0)),
                  pl.BlockSpec((1, SMALL_ROWS, D), lambda k, w: (sent(k, w), 0, 0)),
                  pl.BlockSpec((1, hi, CHUNK), lambda k, w: (w[0], 0, 0)),
                  pl.BlockSpec((1, ho, D), lambda k, w: (w[0], 0, 0)),
                  const((SMALL_ROWS, D))],
        out_specs=[pl.BlockSpec((1, hi, CHUNK), lambda k, w: (w[1], 0, 0)),
                   pl.BlockSpec((1, ho, D), lambda k, w: (w[1], 0, 0)),
                   const((SMALL_ROWS, D)), const((8, D)), const((8, LANES))],
        scratch_shapes=[pltpu.VMEM((hi, CHUNK), F32), pltpu.VMEM((ho, D), F32), pltpu.VMEM((SMALL_ROWS, D), F32)])
    return pl.pallas_call(
        body, grid_spec=grid_spec, name="sum_chips",
        out_shape=[jax.ShapeDtypeStruct((2, hi, CHUNK), F32), jax.ShapeDtypeStruct((2, ho, D), F32),
                   jax.ShapeDtypeStruct((SMALL_ROWS, D), F32), jax.ShapeDtypeStruct((8, D), F32),
                   jax.ShapeDtypeStruct((8, LANES), F32)],
        compiler_params=_params(("arbitrary",)),
    )(where, ri, ro, rs, pi, po, ps)


def _exchange_results(gi2, go2):
    def body(_gi, _go, gi_ref, go_ref, send, recv):
        x, y, c = _pos()
        sib = (x, y, 1 - c)

        def copy(k, ref, slot):
            return pltpu.make_async_remote_copy(src_ref=ref.at[slot], dst_ref=ref.at[slot], send_sem=send.at[k],
                                                recv_sem=recv.at[k], device_id=sib, device_id_type=MESH)

        sends = [copy(k, ref, c) for k, ref in enumerate((gi_ref, go_ref))]
        for cp in sends:
            cp.start()
        for k, ref in enumerate((gi_ref, go_ref)):
            copy(k, ref, 1 - c).wait_recv()
        for cp in sends:
            cp.wait_send()

    return pl.pallas_call(
        body, name="exchange_results",
        in_specs=[ANY, ANY], out_specs=[ANY, ANY], input_output_aliases={0: 0, 1: 1},
        out_shape=[jax.ShapeDtypeStruct((2, D // 2, CHUNK), F32), jax.ShapeDtypeStruct((2, WOUT_SHARD // 2, D), F32)],
        scratch_shapes=[pltpu.SemaphoreType.DMA((2,)), pltpu.SemaphoreType.DMA((2,))],
    )(gi2, go2)


def _adamw_math(w, g, m, v):
    m2 = ADAM_B1 * m + (1.0 - ADAM_B1) * g
    v2 = ADAM_B2 * v + (1.0 - ADAM_B2) * (g * g)
    m_hat = m2 / (1.0 - ADAM_B1 ** ADAM_STEP)
    v_hat = v2 / (1.0 - ADAM_B2 ** ADAM_STEP)
    delta = -ADAM_LR * (m_hat / (jnp.sqrt(v_hat) + ADAM_EPS) + ADAM_WD * w)
    return delta, m2, v2


def _adamw(w, g, m, v, name):
    rows, cols = w.shape
    tm = 256 if rows % 256 == 0 else rows

    def body(w_ref, g_ref, m_ref, v_ref, d_ref, m2_ref, v2_ref):
        d_ref[...], m2_ref[...], v2_ref[...] = _adamw_math(w_ref[...], g_ref[...], m_ref[...], v_ref[...])

    shape = jax.ShapeDtypeStruct(w.shape, F32)
    return pl.pallas_call(
        body, grid=(rows // tm,), name=name,
        in_specs=[_rows(tm, cols)] * 4, out_specs=[_rows(tm, cols)] * 3, out_shape=[shape] * 3,
        compiler_params=_params(("arbitrary",)),
    )(w, g, m, v)


def _adamw_vectors(g5, ws, ms, vs):
    n = len(ws)

    def body(g_ref, *refs):
        ins, outs = refs[:3 * n], refs[3 * n:]
        for i in range(n):
            res = _adamw_math(ins[i][...], g_ref[i:i + 1, :], ins[n + i][...], ins[2 * n + i][...])
            for kind in range(3):
                outs[kind * n + i][...] = res[kind]

    shape = jax.ShapeDtypeStruct((1, D), F32)
    return pl.pallas_call(body, name="adamw_vectors", out_shape=[shape] * (3 * n), compiler_params=_params())(
        g5, *ws, *ms, *vs)


def kernel(x, norm_g, w_in, conv_w, conv_b, conv_ln_g, conv_ln_b, w_out, final_norm_g, loss_target, m_norm_g, m_w_in, m_conv_w, m_conv_b, m_conv_ln_g, m_conv_ln_b, m_w_out, m_final_norm_g, v_norm_g, v_w_in, v_conv_w, v_conv_b, v_conv_ln_g, v_conv_ln_b, v_w_out, v_final_norm_g):
    chip = 2 * lax.axis_index("x") + lax.axis_index("y")
    where = jnp.stack([chip, lax.axis_index("c")]).astype(jnp.int32)
    taps_shard = jnp.pad(conv_w[0], ((0, HALO - CONV_K), (0, 0)))
    wi_full, wo_full, cw_full = _gather_weights(*_place_shards(w_in[0], w_out[0], taps_shard, where))

    gf = final_norm_g[None]
    grad_x, dw_in4, dw_out, small = _local_step(
        x[0], loss_target[0], norm_g, wi_full, cw_full, conv_b, conv_ln_g, conv_ln_b, wo_full, gf)
    dw_out4 = dw_out.reshape(NCHIP, WOUT_SHARD, D)

    ri, ro, rs = _exchange_halves(dw_in4, dw_out4, small)
    pi, po, ps = _add_halves(dw_in4, ri, dw_out4, ro, small, rs)
    ri, ro, rs = _exchange_chips(pi, po, ps)
    gi2, go2, g_small, g5, loss8 = _sum_chips(ri, ro, rs, pi, po, ps, where)
    gi2, go2 = _exchange_results(gi2, go2)
    g_w_in = gi2.reshape(D, CHUNK)
    g_w_out = go2.reshape(WOUT_SHARD, D)
    g_taps = lax.dynamic_slice(g_small, (ROW_TAPS, chip * CONVW_SHARD), (CONV_K, CONVW_SHARD))

    d_w_in, m2_w_in, v2_w_in = _adamw(w_in[0], g_w_in, m_w_in[0], v_w_in[0], "adamw_w_in")
    d_w_out, m2_w_out, v2_w_out = _adamw(w_out[0], g_w_out, m_w_out[0], v_w_out[0], "adamw_w_out")
    d_taps, m2_taps, v2_taps = _adamw(conv_w[0], g_taps, m_conv_w[0], v_conv_w[0], "adamw_conv_w")
    vec = _adamw_vectors(
        g5,
        (norm_g, conv_b, conv_ln_g, conv_ln_b, gf),
        (m_norm_g, m_conv_b, m_conv_ln_g, m_conv_ln_b, m_final_norm_g[None]),
        (v_norm_g, v_conv_b, v_conv_ln_g, v_conv_ln_b, v_final_norm_g[None]))
    d_vec, m2_vec, v2_vec = vec[0:5], vec[5:10], vec[10:15]

    def weight_order(ng, wi, cw, cb, lg, lb, wo, fg):
        return (ng, wi[None], cw[None], cb, lg, lb, wo[None], fg[0])

    grads = weight_order(g5[0:1], g_w_in, g_taps, g5[1:2], g5[2:3], g5[3:4], g_w_out, g5[4:5])
    deltas = weight_order(d_vec[0], d_w_in, d_taps, d_vec[1], d_vec[2], d_vec[3], d_w_out, d_vec[4])
    new_m = weight_order(m2_vec[0], m2_w_in, m2_taps, m2_vec[1], m2_vec[2], m2_vec[3], m2_w_out, m2_vec[4])
    new_v = weight_order(v2_vec[0], v2_w_in, v2_taps, v2_vec[1], v2_vec[2], v2_vec[3], v2_w_out, v2_vec[4])
    return (loss8[0, 0], grad_x[None], *grads, *deltas, *new_m, *new_v)
```

```python
import jax
import jax.numpy as jnp
from jax import lax
from jax.experimental import pallas as pl
from jax.experimental.pallas import tpu as pltpu

F32 = jnp.float32
BF16 = jnp.bfloat16

S = 4096
D = 1024
LANES = 128
HD = 64
NKV = 4
GQ = 4
KVW = NKV * HD
NCOL = 5632
CONV_K = 31
HALO = 32
BLK = 128
PATTERNS = (1, 4, 16)
NORM_EPS = 1e-6
LN_EPS = 1e-5
NEG = -1e30
OFF_Q, OFF_K, OFF_V, OFF_AG, OFF_CV, OFF_CG, OFF_CGATE = 0, 1024, 1280, 1536, 2560, 3584, 4608
NCHIP = 4
CHUNK = NCOL // NCHIP
WOUT_ROWS = 2 * D
WOUT_SHARD = WOUT_ROWS // NCHIP
CONVW_SHARD = D // NCHIP

ADAM_LR, ADAM_B1, ADAM_B2, ADAM_EPS, ADAM_WD, ADAM_STEP = 0.001, 0.9, 0.999, 1e-08, 0.01, 10

VMEM_LIMIT = 56 * 1024 * 1024


def _params(sem=None, vmem=VMEM_LIMIT):
    return pltpu.CompilerParams(dimension_semantics=sem, vmem_limit_bytes=vmem)


def _sigmoid(a):
    return 1.0 / (1.0 + jnp.exp(-a))


def _rows(tm, width):
    return pl.BlockSpec((tm, width), lambda i: (i, 0))


def _slabs(n):
    return jax.ShapeDtypeStruct((n, S, LANES), F32)


def _slab_rows(n, tm):
    return pl.BlockSpec((n, tm, LANES), lambda i: (0, i, 0))


def _resident(shape):
    return pl.BlockSpec(shape, lambda *_: (0,) * len(shape), pipeline_mode=pl.Buffered(1))


def _dot(a, b):
    return jnp.dot(a, b, preferred_element_type=F32)


def _dot_nt(a, b):
    return lax.dot_general(a, b, (((1,), (1,)), ((), ())), preferred_element_type=F32)


def _dot_tn(a, b):
    return lax.dot_general(a, b, (((0,), (0,)), ((), ())), preferred_element_type=F32)


def _inproj_fwd(x, g1, w_bf):
    tm = 256

    def body(x_ref, g_ref, w_ref, h_ref, q_ref, k_ref, v_ref, ag_ref, cv_ref, cg_ref, cgate_ref):
        xt = x_ref[...]
        r = lax.rsqrt(jnp.mean(xt * xt, axis=-1, keepdims=True) + NORM_EPS)
        h = (xt * r * g_ref[...]).astype(BF16)
        h_ref[...] = h
        q = _dot(h, w_ref[:, OFF_Q:OFF_Q + D]) * (HD ** -0.5)
        kv = _dot(h, w_ref[:, OFF_K:OFF_K + 2 * KVW])
        for sl in range(D // LANES):
            q_ref[sl] = q[:, sl * LANES:(sl + 1) * LANES]
        for sl in range(KVW // LANES):
            k_ref[sl] = kv[:, sl * LANES:(sl + 1) * LANES]
            v_ref[sl] = kv[:, KVW + sl * LANES:KVW + (sl + 1) * LANES]
        ag_ref[...] = _dot(h, w_ref[:, OFF_AG:OFF_AG + D])
        cv_ref[...] = _dot(h, w_ref[:, OFF_CV:OFF_CV + D])
        cg_ref[...] = _dot(h, w_ref[:, OFF_CG:OFF_CG + D])
        cgate_ref[...] = _dot(h, w_ref[:, OFF_CGATE:OFF_CGATE + D])

    big = jax.ShapeDtypeStruct((S, D), F32)
    return pl.pallas_call(
        body, grid=(S // tm,), name="inproj_fwd",
        in_specs=[_rows(tm, D), _resident((1, D)), _resident((D, NCOL))],
        out_specs=[_rows(tm, D), _slab_rows(D // LANES, tm), _slab_rows(KVW // LANES, tm), _slab_rows(KVW // LANES, tm),
                   _rows(tm, D), _rows(tm, D), _rows(tm, D), _rows(tm, D)],
        out_shape=[jax.ShapeDtypeStruct((S, D), BF16), _slabs(D // LANES), _slabs(KVW // LANES), _slabs(KVW // LANES),
                   big, big, big, big],
        compiler_params=_params(("arbitrary",)),
    )(x, g1, w_bf)


def _bias_table(d):
    h = jnp.arange(NKV * GQ, dtype=F32)
    slopes = jnp.exp2(-8.0 * (h + 1.0) / (NKV * GQ))
    qi = jnp.arange(BLK)[:, None]
    kj = jnp.arange(2 * BLK)[None, :]
    dist = BLK + qi - kj
    window = (dist >= 0) & (dist <= BLK)
    bias = -slopes[:, None, None] * (dist * d).astype(F32)[None]
    has_prev = jnp.stack([jnp.broadcast_to(kj >= BLK, (BLK, 2 * BLK)), jnp.ones((BLK, 2 * BLK), bool)])
    valid = window[None] & has_prev
    tab = jnp.where(valid[:, None], bias[None], NEG)
    return tab.reshape(2, NKV, GQ * BLK, 2 * BLK)


def _sub_rows(start, d):
    if d == 1:
        return pl.ds(pl.multiple_of(start, BLK), BLK)
    return pl.ds(start, BLK, stride=d)


def _block_rows(idx, d):
    shift = d.bit_length() - 1
    b, r = lax.shift_right_logical(idx, shift), lax.bitwise_and(idx, d - 1)
    start = b * (BLK * d) + r
    return b, _sub_rows(start, d), _sub_rows(jnp.maximum(start - BLK * d, r), d)


def _stack_q(ref, rows):
    t0, t1 = ref[0, rows, :], ref[1, rows, :]
    return jnp.concatenate([t0[:, :HD], t0[:, HD:], t1[:, :HD], t1[:, HD:]], axis=0).astype(BF16)


def _unstack_q(val):
    return (jnp.concatenate([val[0:BLK], val[BLK:2 * BLK]], axis=1),
            jnp.concatenate([val[2 * BLK:3 * BLK], val[3 * BLK:4 * BLK]], axis=1))


def _kv_window(ref, prow, rows, odd):
    t = jnp.concatenate([ref[0, prow, :], ref[0, rows, :]], axis=0)
    return jnp.where(odd, t[:, HD:], t[:, :HD]).astype(BF16)


def _lane_columns(vals):
    lane = lax.broadcasted_iota(jnp.int32, (BLK, LANES), 1)
    tile = jnp.zeros((BLK, LANES), F32)
    for g in range(GQ):
        tile = jnp.where(lane == g, vals[g * BLK:(g + 1) * BLK], tile)
    return tile


def _attn_specs():
    q_like = pl.BlockSpec((2, S, LANES), lambda j: (j, 0, 0))
    kv = pl.BlockSpec((1, S, LANES), lambda j: (j // 2, 0, 0))
    per_head = pl.BlockSpec((1, S, LANES), lambda j: (j, 0, 0))
    bias = pl.BlockSpec((2, 1, GQ * BLK, 2 * BLK), lambda j: (0, j, 0, 0))
    return q_like, kv, per_head, bias


def _attn_fwd(q, k, v, bias, d):
    def body(q_ref, k_ref, v_ref, b_ref, o_ref, l_ref):
        odd = pl.program_id(0) % 2 == 1

        def block(idx, carry):
            b, rows, prow = _block_rows(idx, d)
            qs = _stack_q(q_ref, rows)
            kw = _kv_window(k_ref, prow, rows, odd)
            vw = _kv_window(v_ref, prow, rows, odd)
            s = _dot_nt(qs, kw) + b_ref[jnp.minimum(b, 1), 0]
            m = jnp.max(s, axis=1, keepdims=True)
            p = jnp.exp(s - m)
            l = jnp.sum(p, axis=1, keepdims=True)
            o = _dot(p.astype(BF16), vw) / l
            o_ref[0, rows, :], o_ref[1, rows, :] = _unstack_q(o)
            l_ref[0, rows, :] = _lane_columns(m + jnp.log(l))
            return carry

        lax.fori_loop(0, S // BLK, block, 0, unroll=2)

    q_like, kv, per_head, bias_spec = _attn_specs()
    return pl.pallas_call(
        body, grid=(NKV,), name=f"attn_fwd_d{d}",
        in_specs=[q_like, kv, kv, bias_spec],
        out_specs=[q_like, per_head],
        out_shape=[_slabs(D // LANES), _slabs(NKV)],
        compiler_params=_params(("arbitrary",)),
    )(q, k, v, bias)


def _attn_combine(outs, lses, a_gate):
    tm = 256

    def body(o1, o2, o3, l1, l2, l3, ag_ref, o_ref, lse_ref, y_ref):
        lane = lax.broadcasted_iota(jnp.int32, (tm, LANES), 1)
        for j in range(NKV):
            a, b, c = l1[j], l2[j], l3[j]
            m = jnp.maximum(jnp.maximum(a, b), c)
            ea, eb, ec = jnp.exp(a - m), jnp.exp(b - m), jnp.exp(c - m)
            den = ea + eb + ec
            lse_ref[j] = jnp.where(lane < GQ, m + jnp.log(den), 0.0)
            inv = 1.0 / den
            for half in range(2):
                sl = 2 * j + half

                def spread(w):
                    return jnp.where(lane < HD, w[:, 2 * half:2 * half + 1], w[:, 2 * half + 1:2 * half + 2])

                o = spread(ea * inv) * o1[sl] + spread(eb * inv) * o2[sl] + spread(ec * inv) * o3[sl]
                o_ref[sl] = o
                cols = slice(sl * LANES, (sl + 1) * LANES)
                ag = ag_ref[:, cols]
                y_ref[:, cols] = (o * (ag * _sigmoid(ag))).astype(BF16)

    wide, narrow = _slab_rows(D // LANES, tm), _slab_rows(NKV, tm)
    return pl.pallas_call(
        body, grid=(S // tm,), name="attn_combine",
        in_specs=[wide] * 3 + [narrow] * 3 + [_rows(tm, D)],
        out_specs=[wide, narrow, _rows(tm, D)],
        out_shape=[_slabs(D // LANES), _slabs(NKV), jax.ShapeDtypeStruct((S, D), BF16)],
        compiler_params=_params(("arbitrary",)),
    )(*outs, *lses, a_gate)


def _delta_selector():
    sl = jnp.arange(D // LANES)[:, None]
    lane_in = jnp.arange(LANES)[None, :]
    target = GQ + 2 * (sl % 2) + lane_in // HD
    return (jnp.arange(LANES)[None, None, :] == target[:, :, None]).astype(BF16)


def _attn_gate_bwd(dy_att, o, a_gate, lse, selector):
    tm = 256

    def body(dy_ref, o_ref, ag_ref, l_ref, e_ref, do_ref, dag_ref, ld_ref):
        for j in range(NKV):
            ld = l_ref[j]
            for sl in (2 * j, 2 * j + 1):
                cols = slice(sl * LANES, (sl + 1) * LANES)
                dy, ag, o_ = dy_ref[:, cols], ag_ref[:, cols], o_ref[sl]
                sg = _sigmoid(ag)
                do = dy * (ag * sg)
                do_ref[sl] = do
                dag_ref[:, cols] = dy * o_ * (sg * (1.0 + ag * (1.0 - sg)))
                prod = do * o_
                hi = prod.astype(BF16)
                lo = (prod - hi.astype(F32)).astype(BF16)
                ld = ld + _dot(hi, e_ref[sl]) + _dot(lo, e_ref[sl])
            ld_ref[j] = ld

    return pl.pallas_call(
        body, grid=(S // tm,), name="attn_gate_bwd",
        in_specs=[_rows(tm, D), _slab_rows(D // LANES, tm), _rows(tm, D), _slab_rows(NKV, tm),
                  _resident((D // LANES, LANES, LANES))],
        out_specs=[_slab_rows(D // LANES, tm), _rows(tm, D), _slab_rows(NKV, tm)],
        out_shape=[_slabs(D // LANES), jax.ShapeDtypeStruct((S, D), F32), _slabs(NKV)],
        compiler_params=_params(("arbitrary",)),
    )(dy_att, o, a_gate, lse, selector)


def _attn_bwd(q, k, v, do, ld, bias, d):
    def body(q_ref, do_ref, ld_ref, k_ref, v_ref, b_ref, dq_ref, dkv_ref):
        odd = pl.program_id(0) % 2 == 1
        dkv_ref[...] = jnp.zeros_like(dkv_ref)

        def block(idx, carry):
            b, rows, prow = _block_rows(idx, d)
            qs = _stack_q(q_ref, rows)
            dos = _stack_q(do_ref, rows)
            lt = ld_ref[0, rows, :]
            lse = jnp.concatenate([lt[:, g:g + 1] for g in range(GQ)], axis=0)
            delta = jnp.concatenate([lt[:, GQ + g:GQ + g + 1] for g in range(GQ)], axis=0)
            kw = _kv_window(k_ref, prow, rows, odd)
            vw = _kv_window(v_ref, prow, rows, odd)
            s = _dot_nt(qs, kw) + b_ref[jnp.minimum(b, 1), 0]
            p = jnp.exp(s - lse)
            dvw = _dot_tn(p.astype(BF16), dos)
            dp = _dot_nt(dos, vw)
            ds = (p * (dp - delta)).astype(BF16)
            dq_ref[0, rows, :], dq_ref[1, rows, :] = _unstack_q(_dot(ds, kw))
            dkw = _dot_tn(ds, qs)
            dkv_ref[0, rows, :] = dkv_ref[0, rows, :] + jnp.concatenate([dkw[BLK:], dvw[BLK:]], axis=1)

            @pl.when(b >= 1)
            def _():
                dkv_ref[0, prow, :] = dkv_ref[0, prow, :] + jnp.concatenate([dkw[:BLK], dvw[:BLK]], axis=1)

            return carry

        lax.fori_loop(0, S // BLK, block, 0, unroll=2)

    q_like, kv, per_head, bias_spec = _attn_specs()
    return pl.pallas_call(
        body, grid=(NKV,), name=f"attn_bwd_d{d}",
        in_specs=[q_like, q_like, per_head, kv, kv, bias_spec],
        out_specs=[q_like, per_head],
        out_shape=[_slabs(D // LANES), _slabs(NKV)],
        compiler_params=_params(("arbitrary",)),
    )(q, do, ld, k, v, bias)


CONV_T = 128


def _halo_before(i):
    return (jnp.maximum(i * (CONV_T // HALO) - 1, 0), 0)


def _halo_after(i):
    return (jnp.minimum((i + 1) * (CONV_T // HALO), S // HALO - 1), 0)


SUBLANES = 8
NCH = D // LANES
GROUP = SUBLANES * SUBLANES


def _comb(ref, cb, base):
    return ref[cb, pl.ds(base, SUBLANES, stride=SUBLANES), :]


def _taps(w_ref, cols):
    return [jnp.broadcast_to(w_ref[j:j + 1, cols], (SUBLANES, LANES)) for j in range(CONV_K)]


def _conv_fwd(c_val, c_glu, c_gate, conv_w, conv_b, ln_g, ln_b):
    T = CONV_T

    def body(cv_ref, cg_ref, cvh_ref, cgh_ref, gate_ref, w_ref, b_ref, lg_ref, lb_ref, u_ref, y_ref, win, us):
        i = pl.program_id(0)
        for cb in range(NCH):
            cols = slice(cb * LANES, (cb + 1) * LANES)
            win[cb, HALO:HALO + T, :] = cv_ref[:, cols] * _sigmoid(cg_ref[:, cols])
            win[cb, 0:HALO, :] = jnp.where(i > 0, cvh_ref[:, cols] * _sigmoid(cgh_ref[:, cols]), 0.0)
        for cb in range(NCH):
            cols = slice(cb * LANES, (cb + 1) * LANES)
            taps = _taps(w_ref, cols)
            bias = jnp.broadcast_to(b_ref[:, cols], (SUBLANES, LANES))

            def group(g, carry):
                for b in range(SUBLANES):
                    base = g * GROUP + b
                    acc = bias
                    for j in range(CONV_K):
                        acc = acc + taps[j] * _comb(win, cb, base + (HALO - (CONV_K - 1) + j))
                    us[cb, pl.ds(base, SUBLANES, stride=SUBLANES), :] = acc
                return carry

            lax.fori_loop(0, T // GROUP, group, 0)
        total = us[0]
        for cb in range(1, NCH):
            total = total + us[cb]
        mu = jnp.sum(total, axis=-1, keepdims=True) * (1.0 / D)
        sq = jnp.zeros((T, LANES), F32)
        for cb in range(NCH):
            uc = us[cb] - mu
            sq = sq + uc * uc
        rstd = lax.rsqrt(jnp.sum(sq, axis=-1, keepdims=True) * (1.0 / D) + LN_EPS)
        for cb in range(NCH):
            cols = slice(cb * LANES, (cb + 1) * LANES)
            u = us[cb]
            u_ref[:, cols] = u
            nrm = (u - mu) * rstd * lg_ref[:, cols] + lb_ref[:, cols]
            gate = gate_ref[:, cols]
            y_ref[:, cols] = (nrm * _sigmoid(nrm) * (gate * _sigmoid(gate))).astype(BF16)

    halo = pl.BlockSpec((HALO, D), _halo_before)
    return pl.pallas_call(
        body, grid=(S // T,), name="conv_fwd",
        in_specs=[_rows(T, D), _rows(T, D), halo, halo, _rows(T, D),
                  _resident((HALO, D)), _resident((1, D)), _resident((1, D)), _resident((1, D))],
        out_specs=[_rows(T, D), _rows(T, D)],
        out_shape=[jax.ShapeDtypeStruct((S, D), F32), jax.ShapeDtypeStruct((S, D), BF16)],
        scratch_shapes=[pltpu.VMEM((NCH, T + HALO, LANES), F32), pltpu.VMEM((NCH, T, LANES), F32)],
        compiler_params=_params(("arbitrary",)),
    )(c_val, c_glu, c_val, c_glu, c_gate, conv_w, conv_b, ln_g, ln_b)


def _conv_bwd_rows(u, c_gate, dy_conv, ln_g, ln_b):
    tm = 256

    def body(u_ref, gate_ref, dy_ref, lg_ref, lb_ref, du_ref, dgate_ref, st_ref):
        @pl.when(pl.program_id(0) == 0)
        def _():
            st_ref[...] = jnp.zeros_like(st_ref)

        u, gate, dy = u_ref[...], gate_ref[...], dy_ref[...]
        mu = jnp.mean(u, axis=-1, keepdims=True)
        uc = u - mu
        rstd = lax.rsqrt(jnp.mean(uc * uc, axis=-1, keepdims=True) + LN_EPS)
        z = uc * rstd
        nrm = z * lg_ref[...] + lb_ref[...]
        sn, sg = _sigmoid(nrm), _sigmoid(gate)
        dgate_ref[...] = dy * (nrm * sn) * (sg * (1.0 + gate * (1.0 - sg)))
        dn = dy * (gate * sg) * (sn * (1.0 + nrm * (1.0 - sn)))
        dz = dn * lg_ref[...]
        du = rstd * (dz - jnp.mean(dz, axis=-1, keepdims=True) - z * jnp.mean(dz * z, axis=-1, keepdims=True))
        du_ref[...] = du
        st_ref[0:1, :] += jnp.sum(dn * z, axis=0, keepdims=True)
        st_ref[1:2, :] += jnp.sum(dn, axis=0, keepdims=True)
        st_ref[2:3, :] += jnp.sum(du, axis=0, keepdims=True)

    big = jax.ShapeDtypeStruct((S, D), F32)
    return pl.pallas_call(
        body, grid=(S // tm,), name="conv_bwd_rows",
        in_specs=[_rows(tm, D)] * 3 + [_resident((1, D)), _resident((1, D))],
        out_specs=[_rows(tm, D), _rows(tm, D), pl.BlockSpec((8, D), lambda i: (0, 0))],
        out_shape=[big, big, jax.ShapeDtypeStruct((8, D), F32)],
        compiler_params=_params(("arbitrary",)),
    )(u, c_gate, dy_conv, ln_g, ln_b)


def _conv_bwd_taps(du, c_val, c_glu, conv_w):
    T = CONV_T
    last = S // T - 1

    def body(du_ref, dua_ref, cv_ref, cg_ref, cvh_ref, cgh_ref, w_ref, dcv_ref, dcg_ref, dw_ref,
             hwin, dwin, dhs, dw_acc):
        i = pl.program_id(0)

        @pl.when(i == 0)
        def _():
            dw_acc[...] = jnp.zeros_like(dw_acc)

        for cb in range(NCH):
            cols = slice(cb * LANES, (cb + 1) * LANES)
            hwin[cb, HALO:HALO + T, :] = cv_ref[:, cols] * _sigmoid(cg_ref[:, cols])
            hwin[cb, 0:HALO, :] = jnp.where(i > 0, cvh_ref[:, cols] * _sigmoid(cgh_ref[:, cols]), 0.0)
            dwin[cb, 0:T, :] = du_ref[:, cols]
            dwin[cb, T:T + HALO, :] = jnp.where(i < last, dua_ref[:, cols], 0.0)
        for cb in range(NCH):
            cols = slice(cb * LANES, (cb + 1) * LANES)
            taps = _taps(w_ref, cols)

            def group_dh(g, carry):
                for b in range(SUBLANES):
                    base = g * GROUP + b
                    acc = jnp.zeros((SUBLANES, LANES), F32)
                    for j in range(CONV_K):
                        acc = acc + taps[j] * _comb(dwin, cb, base + (CONV_K - 1 - j))
                    dhs[cb, pl.ds(base, SUBLANES, stride=SUBLANES), :] = acc
                return carry

            lax.fori_loop(0, T // GROUP, group_dh, 0)

            def group_dw(g, sums):
                for b in range(SUBLANES):
                    base = g * GROUP + b
                    d = _comb(dwin, cb, base)
                    sums = tuple(sums[j] + d * _comb(hwin, cb, base + (HALO - (CONV_K - 1) + j))
                                 for j in range(CONV_K))
                return sums

            sums = lax.fori_loop(0, T // GROUP, group_dw, tuple(dw_acc[j, :, cols] for j in range(CONV_K)))
            for j in range(CONV_K):
                dw_acc[j, :, cols] = sums[j]
            dh = dhs[cb]
            cv, sg = cv_ref[:, cols], _sigmoid(cg_ref[:, cols])
            dcv_ref[:, cols] = dh * sg
            dcg_ref[:, cols] = dh * cv * (sg * (1.0 - sg))

        @pl.when(i == last)
        def _():
            dw_ref[...] = jnp.zeros_like(dw_ref)
            for j in range(CONV_K):
                dw_ref[j:j + 1, :] = jnp.sum(dw_acc[j], axis=0, keepdims=True)

    before = pl.BlockSpec((HALO, D), _halo_before)
    after = pl.BlockSpec((HALO, D), _halo_after)
    big = jax.ShapeDtypeStruct((S, D), F32)
    return pl.pallas_call(
        body, grid=(S // T,), name="conv_bwd_taps",
        in_specs=[_rows(T, D), after, _rows(T, D), _rows(T, D), before, before, _resident((HALO, D))],
        out_specs=[_rows(T, D), _rows(T, D), pl.BlockSpec((HALO, D), lambda i: (0, 0))],
        out_shape=[big, big, jax.ShapeDtypeStruct((HALO, D), F32)],
        scratch_shapes=[pltpu.VMEM((NCH, T + HALO, LANES), F32), pltpu.VMEM((NCH, T + HALO, LANES), F32),
                        pltpu.VMEM((NCH, T, LANES), F32), pltpu.VMEM((CONV_K, SUBLANES, D), F32)],
        compiler_params=_params(("arbitrary",)),
    )(du, du, c_val, c_glu, c_val, c_glu, conv_w)


def _outproj_loss(y_att, y_conv, w_out_bf, x, target, gf):
    tm = 256

    def body(ya_ref, yc_ref, w_ref, x_ref, t_ref, gf_ref, dx2_ref, dya_ref, dyc_ref, dw_ref, st_ref, acc):
        @pl.when(pl.program_id(0) == 0)
        def _():
            acc[...] = jnp.zeros_like(acc)
            st_ref[...] = jnp.zeros_like(st_ref)

        ya, yc = ya_ref[...], yc_ref[...]
        x2 = x_ref[...] + _dot(ya, w_ref[0:D, :]) + _dot(yc, w_ref[D:2 * D, :])
        r = lax.rsqrt(jnp.mean(x2 * x2, axis=-1, keepdims=True) + NORM_EPS)
        xn = x2 * r
        err = xn * gf_ref[...] - t_ref[...]
        dout = err * (1.0 / D)
        dxn = dout * gf_ref[...]
        dx2 = r * (dxn - xn * jnp.mean(dxn * xn, axis=-1, keepdims=True))
        dx2_ref[...] = dx2
        dx2b = dx2.astype(BF16)
        dya_ref[...] = _dot_nt(dx2b, w_ref[0:D, :])
        dyc_ref[...] = _dot_nt(dx2b, w_ref[D:2 * D, :])
        acc[0:D, :] += _dot_tn(ya, dx2b)
        acc[D:2 * D, :] += _dot_tn(yc, dx2b)
        st_ref[0:1, :] += jnp.sum(dout * xn, axis=0, keepdims=True)
        st_ref[1:2, :] += jnp.sum(err * err, axis=0, keepdims=True) * (0.5 / D)

        @pl.when(pl.program_id(0) == S // tm - 1)
        def _():
            dw_ref[...] = acc[...].astype(BF16)

    big = jax.ShapeDtypeStruct((S, D), F32)
    return pl.pallas_call(
        body, grid=(S // tm,), name="outproj_loss",
        in_specs=[_rows(tm, D), _rows(tm, D), _resident((WOUT_ROWS, D)), _rows(tm, D), _rows(tm, D), _resident((1, D))],
        out_specs=[_rows(tm, D), _rows(tm, D), _rows(tm, D),
                   pl.BlockSpec((WOUT_ROWS, D), lambda i: (0, 0)), pl.BlockSpec((8, D), lambda i: (0, 0))],
        out_shape=[big, big, big, jax.ShapeDtypeStruct((WOUT_ROWS, D), BF16), jax.ShapeDtypeStruct((8, D), F32)],
        scratch_shapes=[pltpu.VMEM((WOUT_ROWS, D), F32)],
        compiler_params=_params(("arbitrary",)),
    )(y_att, y_conv, w_out_bf, x, target, gf)


def _inproj_bwd_x(dqs, dkvs, dag, dcv, dcg, dcgate, w_bf, x, g1, dx2):
    tm = 256

    def body(dq1, dq2, dq3, dkv1, dkv2, dkv3, dag_ref, dcv_ref, dcg_ref, dcgate_ref,
             w_ref, x_ref, g_ref, dx2_ref, dp_ref, gx_ref, st_ref):
        @pl.when(pl.program_id(0) == 0)
        def _():
            st_ref[...] = jnp.zeros_like(st_ref)

        for sl in range(D // LANES):
            dq = (dq1[sl] + dq2[sl] + dq3[sl]) * (HD ** -0.5)
            dp_ref[:, OFF_Q + sl * LANES:OFF_Q + (sl + 1) * LANES] = dq.astype(BF16)
        for j in range(NKV):
            dkv = (dkv1[j] + dkv2[j] + dkv3[j]).astype(BF16)
            dp_ref[:, OFF_K + j * HD:OFF_K + (j + 1) * HD] = dkv[:, :HD]
            dp_ref[:, OFF_V + j * HD:OFF_V + (j + 1) * HD] = dkv[:, HD:]
        for off, ref in ((OFF_AG, dag_ref), (OFF_CV, dcv_ref), (OFF_CG, dcg_ref), (OFF_CGATE, dcgate_ref)):
            dp_ref[:, off:off + D] = ref[...].astype(BF16)
        dh = jnp.zeros((tm, D), F32)
        for off, width in ((OFF_Q, D), (OFF_K, 2 * KVW), (OFF_AG, D), (OFF_CV, D), (OFF_CG, D), (OFF_CGATE, D)):
            dh = dh + _dot_nt(dp_ref[:, off:off + width], w_ref[:, off:off + width])
        xt = x_ref[...]
        r = lax.rsqrt(jnp.mean(xt * xt, axis=-1, keepdims=True) + NORM_EPS)
        xn = xt * r
        dxn = dh * g_ref[...]
        gx_ref[...] = dx2_ref[...] + r * (dxn - xn * jnp.mean(dxn * xn, axis=-1, keepdims=True))
        st_ref[0:1, :] += jnp.sum(dh * xn, axis=0, keepdims=True)

    return pl.pallas_call(
        body, grid=(S // tm,), name="inproj_bwd_x",
        in_specs=[_slab_rows(D // LANES, tm)] * 3 + [_slab_rows(NKV, tm)] * 3 + [_rows(tm, D)] * 4
        + [_resident((D, NCOL)), _rows(tm, D), _resident((1, D)), _rows(tm, D)],
        out_specs=[_rows(tm, NCOL), _rows(tm, D), pl.BlockSpec((8, D), lambda i: (0, 0))],
        out_shape=[jax.ShapeDtypeStruct((S, NCOL), BF16), jax.ShapeDtypeStruct((S, D), F32),
                   jax.ShapeDtypeStruct((8, D), F32)],
        compiler_params=_params(("arbitrary",)),
    )(*dqs, *dkvs, dag, dcv, dcg, dcgate, w_bf, x, g1, dx2)


def _inproj_bwd_w(h, dproj):
    tk = 1024
    nk = S // tk

    def body(h_ref, dp_ref, o_ref, acc):
        i = pl.program_id(1)

        @pl.when(i == 0)
        def _():
            acc[...] = jnp.zeros_like(acc)

        acc[...] += _dot_tn(h_ref[...], dp_ref[...])

        @pl.when(i == nk - 1)
        def _():
            o_ref[0] = acc[...].astype(BF16)

    return pl.pallas_call(
        body, grid=(NCHIP, nk), name="inproj_bwd_w",
        in_specs=[pl.BlockSpec((tk, D), lambda c, i: (i, 0)), pl.BlockSpec((tk, CHUNK), lambda c, i: (i, c))],
        out_specs=pl.BlockSpec((1, D, CHUNK), lambda c, i: (c, 0, 0)),
        out_shape=jax.ShapeDtypeStruct((NCHIP, D, CHUNK), BF16),
        scratch_shapes=[pltpu.VMEM((D, CHUNK), F32)],
        compiler_params=_params(("arbitrary", "arbitrary")),
    )(h, dproj)


def _local_step(x, target, g1, w_in_bf, conv_w, conv_b, ln_g, ln_b, w_out_bf, gf):
    h, q, k, v, a_gate, c_val, c_glu, c_gate = _inproj_fwd(x, g1, w_in_bf)
    tables = [_bias_table(d) for d in PATTERNS]
    outs, lses = zip(*[_attn_fwd(q, k, v, t, d) for t, d in zip(tables, PATTERNS)])
    o, lse, y_att = _attn_combine(outs, lses, a_gate)
    u, y_conv = _conv_fwd(c_val, c_glu, c_gate, conv_w, conv_b, ln_g, ln_b)
    dx2, dy_att, dy_conv, dw_out, st_out = _outproj_loss(y_att, y_conv, w_out_bf, x, target, gf)

    do, da_gate, ld = _attn_gate_bwd(dy_att, o, a_gate, lse, _delta_selector())
    dqs, dkvs = zip(*[_attn_bwd(q, k, v, do, ld, t, d) for t, d in zip(tables, PATTERNS)])

    du, dc_gate, st_conv = _conv_bwd_rows(u, c_gate, dy_conv, ln_g, ln_b)
    dc_val, dc_glu, dconv_w = _conv_bwd_taps(du, c_val, c_glu, conv_w)

    dproj, grad_x, st_in = _inproj_bwd_x(dqs, dkvs, da_gate, dc_val, dc_glu, dc_gate, w_in_bf, x, g1, dx2)
    dw_in = _inproj_bwd_w(h, dproj)
    small = jnp.concatenate([st_in, st_conv, st_out, dconv_w], axis=0)
    return grad_x, dw_in, dw_out, small


ROW_NORM_G, ROW_LN_G, ROW_LN_B, ROW_CONV_B, ROW_FINAL_G, ROW_LOSS, ROW_TAPS = 0, 8, 9, 10, 16, 17, 24
SMALL_ROWS = 24 + HALO


MESH = pl.DeviceIdType.MESH
ANY = pl.BlockSpec(memory_space=pl.ANY)
CHIP_FLIPS = ((1, 0), (0, 1), (1, 1))


def _pos():
    return lax.axis_index("x"), lax.axis_index("y"), lax.axis_index("c")


def _flip(v, f):
    return 1 - v if f else v


def _ds(start, size, align=None):
    return pl.ds(pl.multiple_of(start, align or size), size)


def _place_shards(wi, wo, cw, where):
    steps = 4

    def body(where_ref, wi_ref, wo_ref, cw_ref, wi_full, wo_full, cw_full):
        wi_full[...] = wi_ref[...].astype(BF16)
        wo_full[...] = wo_ref[...].astype(BF16)
        cw_full[...] = cw_ref[...]

    grid_spec = pltpu.PrefetchScalarGridSpec(
        num_scalar_prefetch=1, grid=(steps,),
        in_specs=[pl.BlockSpec((D // steps, CHUNK), lambda i, w: (i, 0)),
                  pl.BlockSpec((WOUT_SHARD // steps, D), lambda i, w: (i, 0)),
                  pl.BlockSpec((HALO, CONVW_SHARD), lambda i, w: (0, 0))],
        out_specs=[pl.BlockSpec((D // steps, CHUNK), lambda i, w: (i, w[0])),
                   pl.BlockSpec((WOUT_SHARD // steps, D), lambda i, w: (w[0] * steps + i, 0)),
                   pl.BlockSpec((HALO, CONVW_SHARD), lambda i, w: (0, w[0]))])
    return pl.pallas_call(
        body, grid_spec=grid_spec, name="place_shards",
        out_shape=[jax.ShapeDtypeStruct((D, NCOL), BF16), jax.ShapeDtypeStruct((WOUT_ROWS, D), BF16),
                   jax.ShapeDtypeStruct((HALO, D), F32)],
        compiler_params=_params(("arbitrary",)),
    )(where, wi, wo, cw)


def _gather_weights(wi_full, wo_full, cw_full):
    halves = (D // 2, WOUT_SHARD // 2, HALO // 2)
    n_ici = 3 * len(CHIP_FLIPS)

    def body(_wi, _wo, _cw, wi_full, wo_full, cw_full, send, recv):
        x, y, c = _pos()

        def region(a, px, py, half):
            chip = 2 * px + py
            n = halves[a]
            if a == 0:
                return wi_full.at[_ds(half * n, n), _ds(chip * CHUNK, CHUNK, 128)]
            if a == 1:
                return wo_full.at[_ds(chip * WOUT_SHARD + half * n, n), :]
            return cw_full.at[_ds(half * n, n), _ds(chip * CONVW_SHARD, CONVW_SHARD, 128)]

        def remote(k, src, dst, dev):
            return pltpu.make_async_remote_copy(src_ref=src, dst_ref=dst, send_sem=send.at[k], recv_sem=recv.at[k],
                                                device_id=dev, device_id_type=MESH)

        sends = []
        for a in range(3):
            for j, (fx, fy) in enumerate(CHIP_FLIPS):
                mine = region(a, x, y, c)
                cp = remote(3 * a + j, mine, mine, (_flip(x, fx), _flip(y, fy), c))
                cp.start()
                sends.append(cp)
        for a in range(3):
            for j, (fx, fy) in enumerate(CHIP_FLIPS):
                px, py = _flip(x, fx), _flip(y, fy)
                got = region(a, px, py, c)
                remote(3 * a + j, got, got, (px, py, c)).wait_recv()
                cp = remote(n_ici + 3 * a + j, got, got, (x, y, 1 - c))
                cp.start()
                sends.append(cp)
        for a in range(3):
            for j, (fx, fy) in enumerate(CHIP_FLIPS):
                got = region(a, _flip(x, fx), _flip(y, fy), 1 - c)
                remote(n_ici + 3 * a + j, got, got, (x, y, 1 - c)).wait_recv()
        for cp in sends:
            cp.wait_send()

    return pl.pallas_call(
        body, name="gather_weights",
        in_specs=[ANY, ANY, ANY], out_specs=[ANY, ANY, ANY], input_output_aliases={0: 0, 1: 1, 2: 2},
        out_shape=[jax.ShapeDtypeStruct((D, NCOL), BF16), jax.ShapeDtypeStruct((WOUT_ROWS, D), BF16),
                   jax.ShapeDtypeStruct((HALO, D), F32)],
        scratch_shapes=[pltpu.SemaphoreType.DMA((2 * n_ici,)), pltpu.SemaphoreType.DMA((2 * n_ici,))],
    )(wi_full, wo_full, cw_full)


def _exchange_halves(gi4, go4, small):
    def body(gi_ref, go_ref, sm_ref, ri_ref, ro_ref, rs_ref, send, recv):
        x, y, c = _pos()
        sib = (x, y, 1 - c)
        copies = [
            (gi_ref.at[:, _ds((1 - c) * (D // 2), D // 2), :], ri_ref),
            (go_ref.at[:, _ds((1 - c) * (WOUT_SHARD // 2), WOUT_SHARD // 2), :], ro_ref),
            (sm_ref, rs_ref),
        ]
        cps = [pltpu.make_async_remote_copy(src_ref=s_, dst_ref=d_, send_sem=send.at[k], recv_sem=recv.at[k],
                                            device_id=sib, device_id_type=MESH) for k, (s_, d_) in enumerate(copies)]
        for cp in cps:
            cp.start()
        for cp in cps:
            cp.wait()

    return pl.pallas_call(
        body, name="exchange_halves",
        in_specs=[ANY, ANY, ANY], out_specs=[ANY, ANY, ANY],
        out_shape=[jax.ShapeDtypeStruct((NCHIP, D // 2, CHUNK), BF16),
                   jax.ShapeDtypeStruct((NCHIP, WOUT_SHARD // 2, D), BF16),
                   jax.ShapeDtypeStruct((SMALL_ROWS, D), F32)],
        scratch_shapes=[pltpu.SemaphoreType.DMA((3,)), pltpu.SemaphoreType.DMA((3,))],
    )(gi4, go4, small)


def _add_halves(gi4, ri, go4, ro, small, rs):
    hi, ho = D // 2, WOUT_SHARD // 2

    def body(gi_ref, ri_ref, go_ref, ro_ref, sm_ref, rs_ref, pi_ref, po_ref, ps_ref):
        c = lax.axis_index("c")
        pi_ref[0] = (gi_ref[0, _ds(c * hi, hi), :].astype(F32) + ri_ref[0].astype(F32)).astype(BF16)
        po_ref[0] = (go_ref[0, _ds(c * ho, ho), :].astype(F32) + ro_ref[0].astype(F32)).astype(BF16)
        ps_ref[...] = sm_ref[...] + rs_ref[...]

    blk = lambda n, w: pl.BlockSpec((1, n, w), lambda k: (k, 0, 0))
    whole = pl.BlockSpec((SMALL_ROWS, D), lambda k: (0, 0))
    return pl.pallas_call(
        body, grid=(NCHIP,), name="add_halves",
        in_specs=[blk(D, CHUNK), blk(hi, CHUNK), blk(WOUT_SHARD, D), blk(ho, D), whole, whole],
        out_specs=[blk(hi, CHUNK), blk(ho, D), whole],
        out_shape=[jax.ShapeDtypeStruct((NCHIP, hi, CHUNK), BF16), jax.ShapeDtypeStruct((NCHIP, ho, D), BF16),
                   jax.ShapeDtypeStruct((SMALL_ROWS, D), F32)],
        compiler_params=_params(("arbitrary",)),
    )(gi4, ri, go4, ro, small, rs)


def _exchange_chips(pi, po, ps):
    def body(pi_ref, po_ref, ps_ref, ri_ref, ro_ref, rs_ref, send, recv):
        x, y, c = _pos()
        me = 2 * x + y
        srcs = (pi_ref, po_ref, ps_ref)
        dsts = (ri_ref, ro_ref, rs_ref)

        def piece(a, chip):
            return srcs[a] if a == 2 else srcs[a].at[chip]

        sends = []
        for a in range(3):
            for j, (fx, fy) in enumerate(CHIP_FLIPS):
                px, py = _flip(x, fx), _flip(y, fy)
                cp = pltpu.make_async_remote_copy(
                    src_ref=piece(a, 2 * px + py), dst_ref=dsts[a].at[me], send_sem=send.at[3 * a + j],
                    recv_sem=recv.at[3 * a + j], device_id=(px, py, c), device_id_type=MESH)
                cp.start()
                sends.append(cp)
        for a in range(3):
            for j, (fx, fy) in enumerate(CHIP_FLIPS):
                px, py = _flip(x, fx), _flip(y, fy)
                got = dsts[a].at[2 * px + py]
                pltpu.make_async_remote_copy(src_ref=got, dst_ref=got, send_sem=send.at[3 * a + j],
                                             recv_sem=recv.at[3 * a + j], device_id=(px, py, c),
                                             device_id_type=MESH).wait_recv()
        for cp in sends:
            cp.wait_send()

    return pl.pallas_call(
        body, name="exchange_chips",
        in_specs=[ANY, ANY, ANY], out_specs=[ANY, ANY, ANY],
        out_shape=[jax.ShapeDtypeStruct((NCHIP, D // 2, CHUNK), BF16),
                   jax.ShapeDtypeStruct((NCHIP, WOUT_SHARD // 2, D), BF16),
                   jax.ShapeDtypeStruct((NCHIP, SMALL_ROWS, D), F32)],
        scratch_shapes=[pltpu.SemaphoreType.DMA((9,)), pltpu.SemaphoreType.DMA((9,))],
    )(pi, po, ps)


def _sum_chips(ri, ro, rs, pi, po, ps, where):
    def body(w_ref, ri_ref, ro_ref, rs_ref, pi_ref, po_ref, ps_ref, gi_ref, go_ref, gs_ref, g5_ref, loss_ref,
             acc_i, acc_o, acc_s):
        k = pl.program_id(0)
        accs = (acc_i, acc_o, acc_s)

        @pl.when(k == 0)
        def _():
            for acc in accs:
                acc[...] = jnp.zeros_like(acc)

        @pl.when(k == w_ref[0])
        def _():
            for acc, val in zip(accs, (pi_ref[0], po_ref[0], ps_ref[...])):
                acc[...] += val.astype(F32)

        @pl.when(k != w_ref[0])
        def _():
            for acc, ref in zip(accs, (ri_ref, ro_ref, rs_ref)):
                acc[...] += ref[0].astype(F32)

        @pl.when(k == NCHIP - 1)
        def _():
            gi_ref[0] = acc_i[...]
            go_ref[0] = acc_o[...]
            gs_ref[...] = acc_s[...]
            g5_ref[...] = jnp.zeros_like(g5_ref)
            for i, row in enumerate((ROW_NORM_G, ROW_CONV_B, ROW_LN_G, ROW_LN_B, ROW_FINAL_G)):
                g5_ref[i:i + 1, :] = acc_s[row:row + 1, :]
            loss = jnp.sum(acc_s[ROW_LOSS:ROW_LOSS + 1, :], axis=1, keepdims=True)
            loss_ref[...] = jnp.broadcast_to(loss, loss_ref.shape)

    def sent(k, w):
        return jnp.where(k == w[0], (k + 1) % NCHIP, k)

    hi, ho = D // 2, WOUT_SHARD // 2
    const = lambda shape: pl.BlockSpec(shape, lambda k, w: (0,) * len(shape))
    grid_spec = pltpu.PrefetchScalarGridSpec(
        num_scalar_prefetch=1, grid=(NCHIP,),
        in_specs=[pl.BlockSpec((1, hi, CHUNK), lambda k, w: (sent(k, w), 0, 0)),
                  pl.BlockSpec((1, ho, D), lambda k, w: (sent(k, w), 0, 0)),
                  pl.BlockSpec((1, SMALL_ROWS, D), lambda k, w: (sent(k, w), 0, 0)),
                  pl.BlockSpec((1, hi, CHUNK), lambda k, w: (w[0], 0, 0)),
                  pl.BlockSpec((1, ho, D), lambda k, w: (w[0], 0, 0)),
                  const((SMALL_ROWS, D))],
        out_specs=[pl.BlockSpec((1, hi, CHUNK), lambda k, w: (w[1], 0, 0)),
                   pl.BlockSpec((1, ho, D), lambda k, w: (w[1], 0, 0)),
                   const((SMALL_ROWS, D)), const((8, D)), const((8, LANES))],
        scratch_shapes=[pltpu.VMEM((hi, CHUNK), F32), pltpu.VMEM((ho, D), F32), pltpu.VMEM((SMALL_ROWS, D), F32)])
    return pl.pallas_call(
        body, grid_spec=grid_spec, name="sum_chips",
        out_shape=[jax.ShapeDtypeStruct((2, hi, CHUNK), F32), jax.ShapeDtypeStruct((2, ho, D), F32),
                   jax.ShapeDtypeStruct((SMALL_ROWS, D), F32), jax.ShapeDtypeStruct((8, D), F32),
                   jax.ShapeDtypeStruct((8, LANES), F32)],
        compiler_params=_params(("arbitrary",)),
    )(where, ri, ro, rs, pi, po, ps)


def _exchange_results(gi2, go2):
    def body(_gi, _go, gi_ref, go_ref, send, recv):
        x, y, c = _pos()
        sib = (x, y, 1 - c)

        def copy(k, ref, slot):
            return pltpu.make_async_remote_copy(src_ref=ref.at[slot], dst_ref=ref.at[slot], send_sem=send.at[k],
                                                recv_sem=recv.at[k], device_id=sib, device_id_type=MESH)

        sends = [copy(k, ref, c) for k, ref in enumerate((gi_ref, go_ref))]
        for cp in sends:
            cp.start()
        for k, ref in enumerate((gi_ref, go_ref)):
            copy(k, ref, 1 - c).wait_recv()
        for cp in sends:
            cp.wait_send()

    return pl.pallas_call(
        body, name="exchange_results",
        in_specs=[ANY, ANY], out_specs=[ANY, ANY], input_output_aliases={0: 0, 1: 1},
        out_shape=[jax.ShapeDtypeStruct((2, D // 2, CHUNK), F32), jax.ShapeDtypeStruct((2, WOUT_SHARD // 2, D), F32)],
        scratch_shapes=[pltpu.SemaphoreType.DMA((2,)), pltpu.SemaphoreType.DMA((2,))],
    )(gi2, go2)


def _adamw_math(w, g, m, v):
    m2 = ADAM_B1 * m + (1.0 - ADAM_B1) * g
    v2 = ADAM_B2 * v + (1.0 - ADAM_B2) * (g * g)
    m_hat = m2 / (1.0 - ADAM_B1 ** ADAM_STEP)
    v_hat = v2 / (1.0 - ADAM_B2 ** ADAM_STEP)
    delta = -ADAM_LR * (m_hat / (jnp.sqrt(v_hat) + ADAM_EPS) + ADAM_WD * w)
    return delta, m2, v2


def _adamw(w, g, m, v, name):
    rows, cols = w.shape
    tm = 256 if rows % 256 == 0 else rows

    def body(w_ref, g_ref, m_ref, v_ref, d_ref, m2_ref, v2_ref):
        d_ref[...], m2_ref[...], v2_ref[...] = _adamw_math(w_ref[...], g_ref[...], m_ref[...], v_ref[...])

    shape = jax.ShapeDtypeStruct(w.shape, F32)
    return pl.pallas_call(
        body, grid=(rows // tm,), name=name,
        in_specs=[_rows(tm, cols)] * 4, out_specs=[_rows(tm, cols)] * 3, out_shape=[shape] * 3,
        compiler_params=_params(("arbitrary",)),
    )(w, g, m, v)


def _adamw_vectors(g5, ws, ms, vs):
    n = len(ws)

    def body(g_ref, *refs):
        ins, outs = refs[:3 * n], refs[3 * n:]
        for i in range(n):
            res = _adamw_math(ins[i][...], g_ref[i:i + 1, :], ins[n + i][...], ins[2 * n + i][...])
            for kind in range(3):
                outs[kind * n + i][...] = res[kind]

    shape = jax.ShapeDtypeStruct((1, D), F32)
    return pl.pallas_call(body, name="adamw_vectors", out_shape=[shape] * (3 * n), compiler_params=_params())(
        g5, *ws, *ms, *vs)


def kernel(x, norm_g, w_in, conv_w, conv_b, conv_ln_g, conv_ln_b, w_out, final_norm_g, loss_target, m_norm_g, m_w_in, m_conv_w, m_conv_b, m_conv_ln_g, m_conv_ln_b, m_w_out, m_final_norm_g, v_norm_g, v_w_in, v_conv_w, v_conv_b, v_conv_ln_g, v_conv_ln_b, v_w_out, v_final_norm_g):
    chip = 2 * lax.axis_index("x") + lax.axis_index("y")
    where = jnp.stack([chip, lax.axis_index("c")]).astype(jnp.int32)
    taps_shard = jnp.pad(conv_w[0], ((0, HALO - CONV_K), (0, 0)))
    wi_full, wo_full, cw_full = _gather_weights(*_place_shards(w_in[0], w_out[0], taps_shard, where))

    gf = final_norm_g[None]
    grad_x, dw_in4, dw_out, small = _local_step(
        x[0], loss_target[0], norm_g, wi_full, cw_full, conv_b, conv_ln_g, conv_ln_b, wo_full, gf)
    dw_out4 = dw_out.reshape(NCHIP, WOUT_SHARD, D)

    ri, ro, rs = _exchange_halves(dw_in4, dw_out4, small)
    pi, po, ps = _add_halves(dw_in4, ri, dw_out4, ro, small, rs)
    ri, ro, rs = _exchange_chips(pi, po, ps)
    gi2, go2, g_small, g5, loss8 = _sum_chips(ri, ro, rs, pi, po, ps, where)
    gi2, go2 = _exchange_results(gi2, go2)
    g_w_in = gi2.reshape(D, CHUNK)
    g_w_out = go2.reshape(WOUT_SHARD, D)
    g_taps = lax.dynamic_slice(g_small, (ROW_TAPS, chip * CONVW_SHARD), (CONV_K, CONVW_SHARD))

    d_w_in, m2_w_in, v2_w_in = _adamw(w_in[0], g_w_in, m_w_in[0], v_w_in[0], "adamw_w_in")
    d_w_out, m2_w_out, v2_w_out = _adamw(w_out[0], g_w_out, m_w_out[0], v_w_out[0], "adamw_w_out")
    d_taps, m2_taps, v2_taps = _adamw(conv_w[0], g_taps, m_conv_w[0], v_conv_w[0], "adamw_conv_w")
    vec = _adamw_vectors(
        g5,
        (norm_g, conv_b, conv_ln_g, conv_ln_b, gf),
        (m_norm_g, m_conv_b, m_conv_ln_g, m_conv_ln_b, m_final_norm_g[None]),
        (v_norm_g, v_conv_b, v_conv_ln_g, v_conv_ln_b, v_final_norm_g[None]))
    d_vec, m2_vec, v2_vec = vec[0:5], vec[5:10], vec[10:15]

    def weight_order(ng, wi, cw, cb, lg, lb, wo, fg):
        return (ng, wi[None], cw[None], cb, lg, lb, wo[None], fg[0])

    grads = weight_order(g5[0:1], g_w_in, g_taps, g5[1:2], g5[2:3], g5[3:4], g_w_out, g5[4:5])
    deltas = weight_order(d_vec[0], d_w_in, d_taps, d_vec[1], d_vec[2], d_vec[3], d_w_out, d_vec[4])
    new_m = weight_order(m2_vec[0], m2_w_in, m2_taps, m2_vec[1], m2_vec[2], m2_vec[3], m2_w_out, m2_vec[4])
    new_v = weight_order(v2_vec[0], v2_w_in, v2_taps, v2_vec[1], v2_vec[2], v2_vec[3], v2_w_out, v2_vec[4])
    return (loss8[0, 0], grad_x[None], *grads, *deltas, *new_m, *new_v)
```

```python
import jax
import jax.numpy as jnp
from jax import lax
from jax.experimental import pallas as pl
from jax.experimental.pallas import tpu as pltpu

F32 = jnp.float32
BF16 = jnp.bfloat16

S = 4096
D = 1024
LANES = 128
HD = 64
NKV = 4
GQ = 4
KVW = NKV * HD
NCOL = 5632
CONV_K = 31
HALO = 32
BLK = 128
PATTERNS = (1, 4, 16)
NORM_EPS = 1e-6
LN_EPS = 1e-5
NEG = -1e30
OFF_Q, OFF_K, OFF_V, OFF_AG, OFF_CV, OFF_CG, OFF_CGATE = 0, 1024, 1280, 1536, 2560, 3584, 4608
NCHIP = 4
CHUNK = NCOL // NCHIP
WOUT_ROWS = 2 * D
WOUT_SHARD = WOUT_ROWS // NCHIP
CONVW_SHARD = D // NCHIP

ADAM_LR, ADAM_B1, ADAM_B2, ADAM_EPS, ADAM_WD, ADAM_STEP = 0.001, 0.9, 0.999, 1e-08, 0.01, 10

VMEM_LIMIT = 56 * 1024 * 1024


def _params(sem=None, vmem=VMEM_LIMIT):
    return pltpu.CompilerParams(dimension_semantics=sem, vmem_limit_bytes=vmem)


def _sigmoid(a):
    return 1.0 / (1.0 + jnp.exp(-a))


def _rows(tm, width):
    return pl.BlockSpec((tm, width), lambda i: (i, 0))


def _slabs(n):
    return jax.ShapeDtypeStruct((n, S, LANES), F32)


def _slab_rows(n, tm):
    return pl.BlockSpec((n, tm, LANES), lambda i: (0, i, 0))


def _resident(shape):
    return pl.BlockSpec(shape, lambda *_: (0,) * len(shape), pipeline_mode=pl.Buffered(1))


def _dot(a, b):
    return jnp.dot(a, b, preferred_element_type=F32)


def _dot_nt(a, b):
    return lax.dot_general(a, b, (((1,), (1,)), ((), ())), preferred_element_type=F32)


def _dot_tn(a, b):
    return lax.dot_general(a, b, (((0,), (0,)), ((), ())), preferred_element_type=F32)


def _inproj_fwd(x, g1, w_bf):
    tm = 256

    def body(x_ref, g_ref, w_ref, h_ref, q_ref, k_ref, v_ref, ag_ref, cv_ref, cg_ref, cgate_ref):
        xt = x_ref[...]
        r = lax.rsqrt(jnp.mean(xt * xt, axis=-1, keepdims=True) + NORM_EPS)
        h = (xt * r * g_ref[...]).astype(BF16)
        h_ref[...] = h
        q = _dot(h, w_ref[:, OFF_Q:OFF_Q + D]) * (HD ** -0.5)
        kv = _dot(h, w_ref[:, OFF_K:OFF_K + 2 * KVW])
        for sl in range(D // LANES):
            q_ref[sl] = q[:, sl * LANES:(sl + 1) * LANES]
        for sl in range(KVW // LANES):
            k_ref[sl] = kv[:, sl * LANES:(sl + 1) * LANES]
            v_ref[sl] = kv[:, KVW + sl * LANES:KVW + (sl + 1) * LANES]
        ag_ref[...] = _dot(h, w_ref[:, OFF_AG:OFF_AG + D])
        cv_ref[...] = _dot(h, w_ref[:, OFF_CV:OFF_CV + D])
        cg_ref[...] = _dot(h, w_ref[:, OFF_CG:OFF_CG + D])
        cgate_ref[...] = _dot(h, w_ref[:, OFF_CGATE:OFF_CGATE + D])

    big = jax.ShapeDtypeStruct((S, D), F32)
    return pl.pallas_call(
        body, grid=(S // tm,), name="inproj_fwd",
        in_specs=[_rows(tm, D), _resident((1, D)), _resident((D, NCOL))],
        out_specs=[_rows(tm, D), _slab_rows(D // LANES, tm), _slab_rows(KVW // LANES, tm), _slab_rows(KVW // LANES, tm),
                   _rows(tm, D), _rows(tm, D), _rows(tm, D), _rows(tm, D)],
        out_shape=[jax.ShapeDtypeStruct((S, D), BF16), _slabs(D // LANES), _slabs(KVW // LANES), _slabs(KVW // LANES),
                   big, big, big, big],
        compiler_params=_params(("arbitrary",)),
    )(x, g1, w_bf)


def _bias_table(d):
    h = jnp.arange(NKV * GQ, dtype=F32)
    slopes = jnp.exp2(-8.0 * (h + 1.0) / (NKV * GQ))
    qi = jnp.arange(BLK)[:, None]
    kj = jnp.arange(2 * BLK)[None, :]
    dist = BLK + qi - kj
    window = (dist >= 0) & (dist <= BLK)
    bias = -slopes[:, None, None] * (dist * d).astype(F32)[None]
    has_prev = jnp.stack([jnp.broadcast_to(kj >= BLK, (BLK, 2 * BLK)), jnp.ones((BLK, 2 * BLK), bool)])
    valid = window[None] & has_prev
    tab = jnp.where(valid[:, None], bias[None], NEG)
    return tab.reshape(2, NKV, GQ * BLK, 2 * BLK)


def _sub_rows(start, d):
    if d == 1:
        return pl.ds(pl.multiple_of(start, BLK), BLK)
    return pl.ds(start, BLK, stride=d)


NHEAD = NKV * GQ
CHUNK_ROWS = 2048
BLOCKS_PER_CHUNK = CHUNK_ROWS // BLK


def _low_lanes(rows=BLK):
    return lax.broadcasted_iota(jnp.int32, (rows, LANES), 1) < HD


def _block_start(idx, d):
    shift = d.bit_length() - 1
    b, r = lax.shift_right_logical(idx, shift), lax.bitwise_and(idx, d - 1)
    start = b * (BLK * d) + r
    return b, start, jnp.maximum(start - BLK * d, r)


def _stack_heads(ref, rows):
    low = _low_lanes()
    t0, t1 = ref[0, rows, :], ref[1, rows, :]
    return jnp.concatenate([jnp.where(low, t0, 0.0), jnp.where(low, 0.0, t0),
                            jnp.where(low, t1, 0.0), jnp.where(low, 0.0, t1)], axis=0).astype(BF16)


def _unstack_heads(dup):
    low = _low_lanes()
    return (jnp.where(low, dup[0:BLK], dup[BLK:2 * BLK]), jnp.where(low, dup[2 * BLK:3 * BLK], dup[3 * BLK:4 * BLK]))


def _kv_dup(ref, prow, rows, odd):
    t = jnp.concatenate([ref[0, prow, :], ref[0, rows, :]], axis=0)
    swapped = pltpu.roll(t, HD, axis=1)
    keep = jnp.logical_xor(_low_lanes(2 * BLK), odd)
    return jnp.where(keep, t, swapped).astype(BF16)


def _attn_fwd(q, k, v, bias, d):
    def body(q_ref, k_ref, v_ref, b_ref, o_ref, l_ref):
        odd = pl.program_id(0) % 2 == 1
        ones = jnp.ones((2 * BLK, LANES), BF16)

        def block(idx, carry):
            b, start, pstart = _block_start(idx, d)
            rows, prow = _sub_rows(start, d), _sub_rows(pstart, d)
            qs = _stack_heads(q_ref, rows)
            kw = _kv_dup(k_ref, prow, rows, odd)
            vw = _kv_dup(v_ref, prow, rows, odd)
            s = _dot_nt(qs, kw) + b_ref[jnp.minimum(b, 1), 0]
            m = jnp.max(s, axis=1, keepdims=True)
            p = jnp.exp(s - m).astype(BF16)
            ol = _dot(p, jnp.concatenate([vw, ones], axis=1))
            l = ol[:, LANES:]
            o_ref[0, rows, :], o_ref[1, rows, :] = _unstack_heads(ol[:, :LANES] / l)
            lse = m + jnp.log(l)
            for g in range(GQ):
                l_ref[g, rows, :] = lse[g * BLK:(g + 1) * BLK]
            return carry

        lax.fori_loop(0, S // BLK, block, 0, unroll=2)

    q_like = pl.BlockSpec((2, S, LANES), lambda j: (j, 0, 0))
    kv = pl.BlockSpec((1, S, LANES), lambda j: (j // 2, 0, 0))
    heads = pl.BlockSpec((GQ, S, LANES), lambda j: (j, 0, 0))
    bias_spec = pl.BlockSpec((2, 1, GQ * BLK, 2 * BLK), lambda j: (0, j, 0, 0))
    return pl.pallas_call(
        body, grid=(NKV,), name=f"attn_fwd_d{d}",
        in_specs=[q_like, kv, kv, bias_spec],
        out_specs=[q_like, heads],
        out_shape=[_slabs(D // LANES), _slabs(NHEAD)],
        compiler_params=_params(("arbitrary",)),
    )(q, k, v, bias)


def _attn_combine(outs, lses, a_gate):
    tm = 256

    def body(o1, o2, o3, l1, l2, l3, ag_ref, o_ref, lse_ref, y_ref):
        low = _low_lanes(tm)
        for sl in range(D // LANES):
            w = []
            for h in (2 * sl, 2 * sl + 1):
                a, b, c = l1[h], l2[h], l3[h]
                m = jnp.maximum(jnp.maximum(a, b), c)
                ea, eb, ec = jnp.exp(a - m), jnp.exp(b - m), jnp.exp(c - m)
                den = ea + eb + ec
                lse_ref[h] = m + jnp.log(den)
                inv = 1.0 / den
                w.append((ea * inv, eb * inv, ec * inv))
            wa, wb, wc = (jnp.where(low, w[0][i], w[1][i]) for i in range(3))
            o = wa * o1[sl] + wb * o2[sl] + wc * o3[sl]
            o_ref[sl] = o
            cols = slice(sl * LANES, (sl + 1) * LANES)
            ag = ag_ref[:, cols]
            y_ref[:, cols] = (o * (ag * _sigmoid(ag))).astype(BF16)

    wide, per_head = _slab_rows(D // LANES, tm), _slab_rows(NHEAD, tm)
    return pl.pallas_call(
        body, grid=(S // tm,), name="attn_combine",
        in_specs=[wide] * 3 + [per_head] * 3 + [_rows(tm, D)],
        out_specs=[wide, per_head, _rows(tm, D)],
        out_shape=[_slabs(D // LANES), _slabs(NHEAD), jax.ShapeDtypeStruct((S, D), BF16)],
        compiler_params=_params(("arbitrary",)),
    )(*outs, *lses, a_gate)


def _head_sum_selectors():
    lane_in = jnp.arange(LANES)[:, None] // HD
    return jnp.stack([jnp.broadcast_to(lane_in == h, (LANES, LANES)) for h in range(2)]).astype(BF16)


def _attn_gate_bwd(dy_att, o, a_gate, selectors):
    tm = 256

    def body(dy_ref, o_ref, ag_ref, e_ref, do_ref, dag_ref, delta_ref):
        for sl in range(D // LANES):
            cols = slice(sl * LANES, (sl + 1) * LANES)
            dy, ag, o_ = dy_ref[:, cols], ag_ref[:, cols], o_ref[sl]
            sg = _sigmoid(ag)
            do = dy * (ag * sg)
            do_ref[sl] = do
            dag_ref[:, cols] = dy * o_ * (sg * (1.0 + ag * (1.0 - sg)))
            prod = do * o_
            hi = prod.astype(BF16)
            lo = (prod - hi.astype(F32)).astype(BF16)
            for h in range(2):
                delta_ref[2 * sl + h] = _dot(hi, e_ref[h]) + _dot(lo, e_ref[h])

    return pl.pallas_call(
        body, grid=(S // tm,), name="attn_gate_bwd",
        in_specs=[_rows(tm, D), _slab_rows(D // LANES, tm), _rows(tm, D), _resident((2, LANES, LANES))],
        out_specs=[_slab_rows(D // LANES, tm), _rows(tm, D), _slab_rows(NHEAD, tm)],
        out_shape=[_slabs(D // LANES), jax.ShapeDtypeStruct((S, D), F32), _slabs(NHEAD)],
        compiler_params=_params(("arbitrary",)),
    )(dy_att, o, a_gate, selectors)


def _attn_bwd(q, k, v, do, lse, delta, bias, d):
    def body(q_ref, do_ref, l_ref, dl_ref, k_ref, v_ref, b_ref, dq_ref, dkv_ref):
        odd = pl.program_id(0) % 2 == 1
        chunk = pl.program_id(1)

        @pl.when(chunk == 0)
        def _():
            dkv_ref[...] = jnp.zeros_like(dkv_ref)

        def block(idx, carry):
            b, start, pstart = _block_start(chunk * BLOCKS_PER_CHUNK + idx, d)
            rows, prow = _sub_rows(start, d), _sub_rows(pstart, d)
            mine = _sub_rows(start - chunk * CHUNK_ROWS, d)
            qs = _stack_heads(q_ref, mine)
            dos = _stack_heads(do_ref, mine)
            lse_t = jnp.concatenate([l_ref[g, mine, :] for g in range(GQ)], axis=0)
            delta_t = jnp.concatenate([dl_ref[g, mine, :] for g in range(GQ)], axis=0)
            kw = _kv_dup(k_ref, prow, rows, odd)
            vw = _kv_dup(v_ref, prow, rows, odd)
            s = _dot_nt(qs, kw) + b_ref[jnp.minimum(b, 1), 0]
            p = jnp.exp(s - jnp.concatenate([lse_t, lse_t], axis=1))
            dv2 = _dot_tn(p.astype(BF16), dos)
            dp = _dot_nt(dos, vw)
            ds = (p * (dp - jnp.concatenate([delta_t, delta_t], axis=1))).astype(BF16)
            dq_ref[0, mine, :], dq_ref[1, mine, :] = _unstack_heads(_dot(ds, kw))
            dk2 = _dot_tn(ds, qs)
            dkv = jnp.where(_low_lanes(2 * BLK), dk2 + pltpu.roll(dk2, HD, axis=1), dv2 + pltpu.roll(dv2, HD, axis=1))
            dkv_ref[0, rows, :] = dkv_ref[0, rows, :] + dkv[BLK:]

            @pl.when(b >= 1)
            def _():
                dkv_ref[0, prow, :] = dkv_ref[0, prow, :] + dkv[:BLK]

            return carry

        lax.fori_loop(0, BLOCKS_PER_CHUNK, block, 0, unroll=2)

    q_like = pl.BlockSpec((2, CHUNK_ROWS, LANES), lambda j, c: (j, c, 0))
    heads = pl.BlockSpec((GQ, CHUNK_ROWS, LANES), lambda j, c: (j, c, 0))
    kv = pl.BlockSpec((1, S, LANES), lambda j, c: (j // 2, 0, 0))
    per_kv = pl.BlockSpec((1, S, LANES), lambda j, c: (j, 0, 0))
    bias_spec = pl.BlockSpec((2, 1, GQ * BLK, 2 * BLK), lambda j, c: (0, j, 0, 0))
    return pl.pallas_call(
        body, grid=(NKV, S // CHUNK_ROWS), name=f"attn_bwd_d{d}",
        in_specs=[q_like, q_like, heads, heads, kv, kv, bias_spec],
        out_specs=[q_like, per_kv],
        out_shape=[_slabs(D // LANES), _slabs(NKV)],
        compiler_params=_params(("arbitrary", "arbitrary")),
    )(q, do, lse, delta, k, v, bias)


CONV_T = 128


def _halo_before(i):
    return (jnp.maximum(i * (CONV_T // HALO) - 1, 0), 0)


def _halo_after(i):
    return (jnp.minimum((i + 1) * (CONV_T // HALO), S // HALO - 1), 0)


SUBLANES = 8
NCH = D // LANES
GROUP = SUBLANES * SUBLANES


def _comb(ref, cb, base):
    return ref[cb, pl.ds(base, SUBLANES, stride=SUBLANES), :]


def _taps(w_ref, cols):
    return [jnp.broadcast_to(w_ref[j:j + 1, cols], (SUBLANES, LANES)) for j in range(CONV_K)]


def _conv_fwd(c_val, c_glu, c_gate, conv_w, conv_b, ln_g, ln_b):
    T = CONV_T

    def body(cv_ref, cg_ref, cvh_ref, cgh_ref, gate_ref, w_ref, b_ref, lg_ref, lb_ref, u_ref, y_ref, win, us):
        i = pl.program_id(0)
        for cb in range(NCH):
            cols = slice(cb * LANES, (cb + 1) * LANES)
            win[cb, HALO:HALO + T, :] = cv_ref[:, cols] * _sigmoid(cg_ref[:, cols])
            win[cb, 0:HALO, :] = jnp.where(i > 0, cvh_ref[:, cols] * _sigmoid(cgh_ref[:, cols]), 0.0)
        for cb in range(NCH):
            cols = slice(cb * LANES, (cb + 1) * LANES)
            taps = _taps(w_ref, cols)
            bias = jnp.broadcast_to(b_ref[:, cols], (SUBLANES, LANES))

            def group(g, carry):
                for b in range(SUBLANES):
                    base = g * GROUP + b
                    acc = bias
                    for j in range(CONV_K):
                        acc = acc + taps[j] * _comb(win, cb, base + (HALO - (CONV_K - 1) + j))
                    us[cb, pl.ds(base, SUBLANES, stride=SUBLANES), :] = acc
                return carry

            lax.fori_loop(0, T // GROUP, group, 0)
        total = us[0]
        for cb in range(1, NCH):
            total = total + us[cb]
        mu = jnp.sum(total, axis=-1, keepdims=True) * (1.0 / D)
        sq = jnp.zeros((T, LANES), F32)
        for cb in range(NCH):
            uc = us[cb] - mu
            sq = sq + uc * uc
        rstd = lax.rsqrt(jnp.sum(sq, axis=-1, keepdims=True) * (1.0 / D) + LN_EPS)
        for cb in range(NCH):
            cols = slice(cb * LANES, (cb + 1) * LANES)
            u = us[cb]
            u_ref[:, cols] = u
            nrm = (u - mu) * rstd * lg_ref[:, cols] + lb_ref[:, cols]
            gate = gate_ref[:, cols]
            y_ref[:, cols] = (nrm * _sigmoid(nrm) * (gate * _sigmoid(gate))).astype(BF16)

    halo = pl.BlockSpec((HALO, D), _halo_before)
    return pl.pallas_call(
        body, grid=(S // T,), name="conv_fwd",
        in_specs=[_rows(T, D), _rows(T, D), halo, halo, _rows(T, D),
                  _resident((HALO, D)), _resident((1, D)), _resident((1, D)), _resident((1, D))],
        out_specs=[_rows(T, D), _rows(T, D)],
        out_shape=[jax.ShapeDtypeStruct((S, D), F32), jax.ShapeDtypeStruct((S, D), BF16)],
        scratch_shapes=[pltpu.VMEM((NCH, T + HALO, LANES), F32), pltpu.VMEM((NCH, T, LANES), F32)],
        compiler_params=_params(("arbitrary",)),
    )(c_val, c_glu, c_val, c_glu, c_gate, conv_w, conv_b, ln_g, ln_b)


def _conv_bwd_rows(u, c_gate, dy_conv, ln_g, ln_b):
    tm = 256

    def body(u_ref, gate_ref, dy_ref, lg_ref, lb_ref, du_ref, dgate_ref, st_ref):
        @pl.when(pl.program_id(0) == 0)
        def _():
            st_ref[...] = jnp.zeros_like(st_ref)

        u, gate, dy = u_ref[...], gate_ref[...], dy_ref[...]
        mu = jnp.mean(u, axis=-1, keepdims=True)
        uc = u - mu
        rstd = lax.rsqrt(jnp.mean(uc * uc, axis=-1, keepdims=True) + LN_EPS)
        z = uc * rstd
        nrm = z * lg_ref[...] + lb_ref[...]
        sn, sg = _sigmoid(nrm), _sigmoid(gate)
        dgate_ref[...] = dy * (nrm * sn) * (sg * (1.0 + gate * (1.0 - sg)))
        dn = dy * (gate * sg) * (sn * (1.0 + nrm * (1.0 - sn)))
        dz = dn * lg_ref[...]
        du = rstd * (dz - jnp.mean(dz, axis=-1, keepdims=True) - z * jnp.mean(dz * z, axis=-1, keepdims=True))
        du_ref[...] = du
        st_ref[0:1, :] += jnp.sum(dn * z, axis=0, keepdims=True)
        st_ref[1:2, :] += jnp.sum(dn, axis=0, keepdims=True)
        st_ref[2:3, :] += jnp.sum(du, axis=0, keepdims=True)

    big = jax.ShapeDtypeStruct((S, D), F32)
    return pl.pallas_call(
        body, grid=(S // tm,), name="conv_bwd_rows",
        in_specs=[_rows(tm, D)] * 3 + [_resident((1, D)), _resident((1, D))],
        out_specs=[_rows(tm, D), _rows(tm, D), pl.BlockSpec((8, D), lambda i: (0, 0))],
        out_shape=[big, big, jax.ShapeDtypeStruct((8, D), F32)],
        compiler_params=_params(("arbitrary",)),
    )(u, c_gate, dy_conv, ln_g, ln_b)


def _conv_bwd_taps(du, c_val, c_glu, conv_w):
    T = CONV_T
    last = S // T - 1

    def body(du_ref, dua_ref, cv_ref, cg_ref, cvh_ref, cgh_ref, w_ref, dcv_ref, dcg_ref, dw_ref,
             hwin, dwin, dhs, dw_acc):
        i = pl.program_id(0)

        @pl.when(i == 0)
        def _():
            dw_acc[...] = jnp.zeros_like(dw_acc)

        for cb in range(NCH):
            cols = slice(cb * LANES, (cb + 1) * LANES)
            hwin[cb, HALO:HALO + T, :] = cv_ref[:, cols] * _sigmoid(cg_ref[:, cols])
            hwin[cb, 0:HALO, :] = jnp.where(i > 0, cvh_ref[:, cols] * _sigmoid(cgh_ref[:, cols]), 0.0)
            dwin[cb, 0:T, :] = du_ref[:, cols]
            dwin[cb, T:T + HALO, :] = jnp.where(i < last, dua_ref[:, cols], 0.0)
        for cb in range(NCH):
            cols = slice(cb * LANES, (cb + 1) * LANES)
            taps = _taps(w_ref, cols)

            def group_dh(g, carry):
                for b in range(SUBLANES):
                    base = g * GROUP + b
                    acc = jnp.zeros((SUBLANES, LANES), F32)
                    for j in range(CONV_K):
                        acc = acc + taps[j] * _comb(dwin, cb, base + (CONV_K - 1 - j))
                    dhs[cb, pl.ds(base, SUBLANES, stride=SUBLANES), :] = acc
                return carry

            lax.fori_loop(0, T // GROUP, group_dh, 0)

            def group_dw(g, sums):
                for b in range(SUBLANES):
                    base = g * GROUP + b
                    d = _comb(dwin, cb, base)
                    sums = tuple(sums[j] + d * _comb(hwin, cb, base + (HALO - (CONV_K - 1) + j))
                                 for j in range(CONV_K))
                return sums

            sums = lax.fori_loop(0, T // GROUP, group_dw, tuple(dw_acc[j, :, cols] for j in range(CONV_K)))
            for j in range(CONV_K):
                dw_acc[j, :, cols] = sums[j]
            dh = dhs[cb]
            cv, sg = cv_ref[:, cols], _sigmoid(cg_ref[:, cols])
            dcv_ref[:, cols] = dh * sg
            dcg_ref[:, cols] = dh * cv * (sg * (1.0 - sg))

        @pl.when(i == last)
        def _():
            dw_ref[...] = jnp.zeros_like(dw_ref)
            for j in range(CONV_K):
                dw_ref[j:j + 1, :] = jnp.sum(dw_acc[j], axis=0, keepdims=True)

    before = pl.BlockSpec((HALO, D), _halo_before)
    after = pl.BlockSpec((HALO, D), _halo_after)
    big = jax.ShapeDtypeStruct((S, D), F32)
    return pl.pallas_call(
        body, grid=(S // T,), name="conv_bwd_taps",
        in_specs=[_rows(T, D), after, _rows(T, D), _rows(T, D), before, before, _resident((HALO, D))],
        out_specs=[_rows(T, D), _rows(T, D), pl.BlockSpec((HALO, D), lambda i: (0, 0))],
        out_shape=[big, big, jax.ShapeDtypeStruct((HALO, D), F32)],
        scratch_shapes=[pltpu.VMEM((NCH, T + HALO, LANES), F32), pltpu.VMEM((NCH, T + HALO, LANES), F32),
                        pltpu.VMEM((NCH, T, LANES), F32), pltpu.VMEM((CONV_K, SUBLANES, D), F32)],
        compiler_params=_params(("arbitrary",)),
    )(du, du, c_val, c_glu, c_val, c_glu, conv_w)


def _outproj_loss(y_att, y_conv, w_out_bf, x, target, gf):
    tm = 256

    def body(ya_ref, yc_ref, w_ref, x_ref, t_ref, gf_ref, dx2_ref, dya_ref, dyc_ref, dw_ref, st_ref, acc):
        @pl.when(pl.program_id(0) == 0)
        def _():
            acc[...] = jnp.zeros_like(acc)
            st_ref[...] = jnp.zeros_like(st_ref)

        ya, yc = ya_ref[...], yc_ref[...]
        x2 = x_ref[...] + _dot(ya, w_ref[0:D, :]) + _dot(yc, w_ref[D:2 * D, :])
        r = lax.rsqrt(jnp.mean(x2 * x2, axis=-1, keepdims=True) + NORM_EPS)
        xn = x2 * r
        err = xn * gf_ref[...] - t_ref[...]
        dout = err * (1.0 / D)
        dxn = dout * gf_ref[...]
        dx2 = r * (dxn - xn * jnp.mean(dxn * xn, axis=-1, keepdims=True))
        dx2_ref[...] = dx2
        dx2b = dx2.astype(BF16)
        dya_ref[...] = _dot_nt(dx2b, w_ref[0:D, :])
        dyc_ref[...] = _dot_nt(dx2b, w_ref[D:2 * D, :])
        acc[0:D, :] += _dot_tn(ya, dx2b)
        acc[D:2 * D, :] += _dot_tn(yc, dx2b)
        st_ref[0:1, :] += jnp.sum(dout * xn, axis=0, keepdims=True)
        st_ref[1:2, :] += jnp.sum(err * err, axis=0, keepdims=True) * (0.5 / D)

        @pl.when(pl.program_id(0) == S // tm - 1)
        def _():
            dw_ref[...] = acc[...].astype(BF16)

    big = jax.ShapeDtypeStruct((S, D), F32)
    return pl.pallas_call(
        body, grid=(S // tm,), name="outproj_loss",
        in_specs=[_rows(tm, D), _rows(tm, D), _resident((WOUT_ROWS, D)), _rows(tm, D), _rows(tm, D), _resident((1, D))],
        out_specs=[_rows(tm, D), _rows(tm, D), _rows(tm, D),
                   pl.BlockSpec((WOUT_ROWS, D), lambda i: (0, 0)), pl.BlockSpec((8, D), lambda i: (0, 0))],
        out_shape=[big, big, big, jax.ShapeDtypeStruct((WOUT_ROWS, D), BF16), jax.ShapeDtypeStruct((8, D), F32)],
        scratch_shapes=[pltpu.VMEM((WOUT_ROWS, D), F32)],
        compiler_params=_params(("arbitrary",)),
    )(y_att, y_conv, w_out_bf, x, target, gf)


def _inproj_bwd_x(dqs, dkvs, dag, dcv, dcg, dcgate, w_bf, x, g1, dx2):
    tm = 256

    def body(dq1, dq2, dq3, dkv1, dkv2, dkv3, dag_ref, dcv_ref, dcg_ref, dcgate_ref,
             w_ref, x_ref, g_ref, dx2_ref, dp_ref, gx_ref, st_ref):
        @pl.when(pl.program_id(0) == 0)
        def _():
            st_ref[...] = jnp.zeros_like(st_ref)

        for sl in range(D // LANES):
            dq = (dq1[sl] + dq2[sl] + dq3[sl]) * (HD ** -0.5)
            dp_ref[:, OFF_Q + sl * LANES:OFF_Q + (sl + 1) * LANES] = dq.astype(BF16)
        for j in range(NKV):
            dkv = (dkv1[j] + dkv2[j] + dkv3[j]).astype(BF16)
            dp_ref[:, OFF_K + j * HD:OFF_K + (j + 1) * HD] = dkv[:, :HD]
            dp_ref[:, OFF_V + j * HD:OFF_V + (j + 1) * HD] = dkv[:, HD:]
        for off, ref in ((OFF_AG, dag_ref), (OFF_CV, dcv_ref), (OFF_CG, dcg_ref), (OFF_CGATE, dcgate_ref)):
            dp_ref[:, off:off + D] = ref[...].astype(BF16)
        dh = jnp.zeros((tm, D), F32)
        for off, width in ((OFF_Q, D), (OFF_K, 2 * KVW), (OFF_AG, D), (OFF_CV, D), (OFF_CG, D), (OFF_CGATE, D)):
            dh = dh + _dot_nt(dp_ref[:, off:off + width], w_ref[:, off:off + width])
        xt = x_ref[...]
        r = lax.rsqrt(jnp.mean(xt * xt, axis=-1, keepdims=True) + NORM_EPS)
        xn = xt * r
        dxn = dh * g_ref[...]
        gx_ref[...] = dx2_ref[...] + r * (dxn - xn * jnp.mean(dxn * xn, axis=-1, keepdims=True))
        st_ref[0:1, :] += jnp.sum(dh * xn, axis=0, keepdims=True)

    return pl.pallas_call(
        body, grid=(S // tm,), name="inproj_bwd_x",
        in_specs=[_slab_rows(D // LANES, tm)] * 3 + [_slab_rows(NKV, tm)] * 3 + [_rows(tm, D)] * 4
        + [_resident((D, NCOL)), _rows(tm, D), _resident((1, D)), _rows(tm, D)],
        out_specs=[_rows(tm, NCOL), _rows(tm, D), pl.BlockSpec((8, D), lambda i: (0, 0))],
        out_shape=[jax.ShapeDtypeStruct((S, NCOL), BF16), jax.ShapeDtypeStruct((S, D), F32),
                   jax.ShapeDtypeStruct((8, D), F32)],
        compiler_params=_params(("arbitrary",)),
    )(*dqs, *dkvs, dag, dcv, dcg, dcgate, w_bf, x, g1, dx2)


def _inproj_bwd_w(h, dproj):
    tk = 1024
    nk = S // tk

    def body(h_ref, dp_ref, o_ref, acc):
        i = pl.program_id(1)

        @pl.when(i == 0)
        def _():
            acc[...] = jnp.zeros_like(acc)

        acc[...] += _dot_tn(h_ref[...], dp_ref[...])

        @pl.when(i == nk - 1)
        def _():
            o_ref[0] = acc[...].astype(BF16)

    return pl.pallas_call(
        body, grid=(NCHIP, nk), name="inproj_bwd_w",
        in_specs=[pl.BlockSpec((tk, D), lambda c, i: (i, 0)), pl.BlockSpec((tk, CHUNK), lambda c, i: (i, c))],
        out_specs=pl.BlockSpec((1, D, CHUNK), lambda c, i: (c, 0, 0)),
        out_shape=jax.ShapeDtypeStruct((NCHIP, D, CHUNK), BF16),
        scratch_shapes=[pltpu.VMEM((D, CHUNK), F32)],
        compiler_params=_params(("arbitrary", "arbitrary")),
    )(h, dproj)


def _local_step(x, target, g1, w_in_bf, conv_w, conv_b, ln_g, ln_b, w_out_bf, gf):
    h, q, k, v, a_gate, c_val, c_glu, c_gate = _inproj_fwd(x, g1, w_in_bf)
    tables = [_bias_table(d) for d in PATTERNS]
    outs, lses = zip(*[_attn_fwd(q, k, v, t, d) for t, d in zip(tables, PATTERNS)])
    o, lse, y_att = _attn_combine(outs, lses, a_gate)
    u, y_conv = _conv_fwd(c_val, c_glu, c_gate, conv_w, conv_b, ln_g, ln_b)
    dx2, dy_att, dy_conv, dw_out, st_out = _outproj_loss(y_att, y_conv, w_out_bf, x, target, gf)

    do, da_gate, delta = _attn_gate_bwd(dy_att, o, a_gate, _head_sum_selectors())
    dqs, dkvs = zip(*[_attn_bwd(q, k, v, do, lse, delta, t, d) for t, d in zip(tables, PATTERNS)])

    du, dc_gate, st_conv = _conv_bwd_rows(u, c_gate, dy_conv, ln_g, ln_b)
    dc_val, dc_glu, dconv_w = _conv_bwd_taps(du, c_val, c_glu, conv_w)

    dproj, grad_x, st_in = _inproj_bwd_x(dqs, dkvs, da_gate, dc_val, dc_glu, dc_gate, w_in_bf, x, g1, dx2)
    dw_in = _inproj_bwd_w(h, dproj)
    small = jnp.concatenate([st_in, st_conv, st_out, dconv_w], axis=0)
    return grad_x, dw_in, dw_out, small


ROW_NORM_G, ROW_LN_G, ROW_LN_B, ROW_CONV_B, ROW_FINAL_G, ROW_LOSS, ROW_TAPS = 0, 8, 9, 10, 16, 17, 24
SMALL_ROWS = 24 + HALO


MESH = pl.DeviceIdType.MESH
ANY = pl.BlockSpec(memory_space=pl.ANY)
CHIP_FLIPS = ((1, 0), (0, 1), (1, 1))


def _pos():
    return lax.axis_index("x"), lax.axis_index("y"), lax.axis_index("c")


def _flip(v, f):
    return 1 - v if f else v


def _ds(start, size, align=None):
    return pl.ds(pl.multiple_of(start, align or size), size)


def _place_shards(wi, wo, cw, where):
    steps = 4

    def body(where_ref, wi_ref, wo_ref, cw_ref, wi_full, wo_full, cw_full):
        wi_full[...] = wi_ref[...].astype(BF16)
        wo_full[...] = wo_ref[...].astype(BF16)
        cw_full[...] = cw_ref[...]

    grid_spec = pltpu.PrefetchScalarGridSpec(
        num_scalar_prefetch=1, grid=(steps,),
        in_specs=[pl.BlockSpec((D // steps, CHUNK), lambda i, w: (i, 0)),
                  pl.BlockSpec((WOUT_SHARD // steps, D), lambda i, w: (i, 0)),
                  pl.BlockSpec((HALO, CONVW_SHARD), lambda i, w: (0, 0))],
        out_specs=[pl.BlockSpec((D // steps, CHUNK), lambda i, w: (i, w[0])),
                   pl.BlockSpec((WOUT_SHARD // steps, D), lambda i, w: (w[0] * steps + i, 0)),
                   pl.BlockSpec((HALO, CONVW_SHARD), lambda i, w: (0, w[0]))])
    return pl.pallas_call(
        body, grid_spec=grid_spec, name="place_shards",
        out_shape=[jax.ShapeDtypeStruct((D, NCOL), BF16), jax.ShapeDtypeStruct((WOUT_ROWS, D), BF16),
                   jax.ShapeDtypeStruct((HALO, D), F32)],
        compiler_params=_params(("arbitrary",)),
    )(where, wi, wo, cw)


def _gather_weights(wi_full, wo_full, cw_full):
    halves = (D // 2, WOUT_SHARD // 2, HALO // 2)
    n_ici = 3 * len(CHIP_FLIPS)

    def body(_wi, _wo, _cw, wi_full, wo_full, cw_full, send, recv):
        x, y, c = _pos()

        def region(a, px, py, half):
            chip = 2 * px + py
            n = halves[a]
            if a == 0:
                return wi_full.at[_ds(half * n, n), _ds(chip * CHUNK, CHUNK, 128)]
            if a == 1:
                return wo_full.at[_ds(chip * WOUT_SHARD + half * n, n), :]
            return cw_full.at[_ds(half * n, n), _ds(chip * CONVW_SHARD, CONVW_SHARD, 128)]

        def remote(k, src, dst, dev):
            return pltpu.make_async_remote_copy(src_ref=src, dst_ref=dst, send_sem=send.at[k], recv_sem=recv.at[k],
                                                device_id=dev, device_id_type=MESH)

        sends = []
        for a in range(3):
            for j, (fx, fy) in enumerate(CHIP_FLIPS):
                mine = region(a, x, y, c)
                cp = remote(3 * a + j, mine, mine, (_flip(x, fx), _flip(y, fy), c))
                cp.start()
                sends.append(cp)
        for a in range(3):
            for j, (fx, fy) in enumerate(CHIP_FLIPS):
                px, py = _flip(x, fx), _flip(y, fy)
                got = region(a, px, py, c)
                remote(3 * a + j, got, got, (px, py, c)).wait_recv()
                cp = remote(n_ici + 3 * a + j, got, got, (x, y, 1 - c))
                cp.start()
                sends.append(cp)
        for a in range(3):
            for j, (fx, fy) in enumerate(CHIP_FLIPS):
                got = region(a, _flip(x, fx), _flip(y, fy), 1 - c)
                remote(n_ici + 3 * a + j, got, got, (x, y, 1 - c)).wait_recv()
        for cp in sends:
            cp.wait_send()

    return pl.pallas_call(
        body, name="gather_weights",
        in_specs=[ANY, ANY, ANY], out_specs=[ANY, ANY, ANY], input_output_aliases={0: 0, 1: 1, 2: 2},
        out_shape=[jax.ShapeDtypeStruct((D, NCOL), BF16), jax.ShapeDtypeStruct((WOUT_ROWS, D), BF16),
                   jax.ShapeDtypeStruct((HALO, D), F32)],
        scratch_shapes=[pltpu.SemaphoreType.DMA((2 * n_ici,)), pltpu.SemaphoreType.DMA((2 * n_ici,))],
    )(wi_full, wo_full, cw_full)


def _exchange_halves(gi4, go4, small):
    def body(gi_ref, go_ref, sm_ref, ri_ref, ro_ref, rs_ref, send, recv):
        x, y, c = _pos()
        sib = (x, y, 1 - c)
        copies = [
            (gi_ref.at[:, _ds((1 - c) * (D // 2), D // 2), :], ri_ref),
            (go_ref.at[:, _ds((1 - c) * (WOUT_SHARD // 2), WOUT_SHARD // 2), :], ro_ref),
            (sm_ref, rs_ref),
        ]
        cps = [pltpu.make_async_remote_copy(src_ref=s_, dst_ref=d_, send_sem=send.at[k], recv_sem=recv.at[k],
                                            device_id=sib, device_id_type=MESH) for k, (s_, d_) in enumerate(copies)]
        for cp in cps:
            cp.start()
        for cp in cps:
            cp.wait()

    return pl.pallas_call(
        body, name="exchange_halves",
        in_specs=[ANY, ANY, ANY], out_specs=[ANY, ANY, ANY],
        out_shape=[jax.ShapeDtypeStruct((NCHIP, D // 2, CHUNK), BF16),
                   jax.ShapeDtypeStruct((NCHIP, WOUT_SHARD // 2, D), BF16),
                   jax.ShapeDtypeStruct((SMALL_ROWS, D), F32)],
        scratch_shapes=[pltpu.SemaphoreType.DMA((3,)), pltpu.SemaphoreType.DMA((3,))],
    )(gi4, go4, small)


def _add_halves(gi4, ri, go4, ro, small, rs):
    hi, ho = D // 2, WOUT_SHARD // 2

    def body(gi_ref, ri_ref, go_ref, ro_ref, sm_ref, rs_ref, pi_ref, po_ref, ps_ref):
        c = lax.axis_index("c")
        pi_ref[0] = (gi_ref[0, _ds(c * hi, hi), :].astype(F32) + ri_ref[0].astype(F32)).astype(BF16)
        po_ref[0] = (go_ref[0, _ds(c * ho, ho), :].astype(F32) + ro_ref[0].astype(F32)).astype(BF16)
        ps_ref[...] = sm_ref[...] + rs_ref[...]

    blk = lambda n, w: pl.BlockSpec((1, n, w), lambda k: (k, 0, 0))
    whole = pl.BlockSpec((SMALL_ROWS, D), lambda k: (0, 0))
    return pl.pallas_call(
        body, grid=(NCHIP,), name="add_halves",
        in_specs=[blk(D, CHUNK), blk(hi, CHUNK), blk(WOUT_SHARD, D), blk(ho, D), whole, whole],
        out_specs=[blk(hi, CHUNK), blk(ho, D), whole],
        out_shape=[jax.ShapeDtypeStruct((NCHIP, hi, CHUNK), BF16), jax.ShapeDtypeStruct((NCHIP, ho, D), BF16),
                   jax.ShapeDtypeStruct((SMALL_ROWS, D), F32)],
        compiler_params=_params(("arbitrary",)),
    )(gi4, ri, go4, ro, small, rs)


def _exchange_chips(pi, po, ps):
    def body(pi_ref, po_ref, ps_ref, ri_ref, ro_ref, rs_ref, send, recv):
        x, y, c = _pos()
        me = 2 * x + y
        srcs = (pi_ref, po_ref, ps_ref)
        dsts = (ri_ref, ro_ref, rs_ref)

        def piece(a, chip):
            return srcs[a] if a == 2 else srcs[a].at[chip]

        sends = []
        for a in range(3):
            for j, (fx, fy) in enumerate(CHIP_FLIPS):
                px, py = _flip(x, fx), _flip(y, fy)
                cp = pltpu.make_async_remote_copy(
                    src_ref=piece(a, 2 * px + py), dst_ref=dsts[a].at[me], send_sem=send.at[3 * a + j],
                    recv_sem=recv.at[3 * a + j], device_id=(px, py, c), device_id_type=MESH)
                cp.start()
                sends.append(cp)
        for a in range(3):
            for j, (fx, fy) in enumerate(CHIP_FLIPS):
                px, py = _flip(x, fx), _flip(y, fy)
                got = dsts[a].at[2 * px + py]
                pltpu.make_async_remote_copy(src_ref=got, dst_ref=got, send_sem=send.at[3 * a + j],
                                             recv_sem=recv.at[3 * a + j], device_id=(px, py, c),
                                             device_id_type=MESH).wait_recv()
        for cp in sends:
            cp.wait_send()

    return pl.pallas_call(
        body, name="exchange_chips",
        in_specs=[ANY, ANY, ANY], out_specs=[ANY, ANY, ANY],
        out_shape=[jax.ShapeDtypeStruct((NCHIP, D // 2, CHUNK), BF16),
                   jax.ShapeDtypeStruct((NCHIP, WOUT_SHARD // 2, D), BF16),
                   jax.ShapeDtypeStruct((NCHIP, SMALL_ROWS, D), F32)],
        scratch_shapes=[pltpu.SemaphoreType.DMA((9,)), pltpu.SemaphoreType.DMA((9,))],
    )(pi, po, ps)


def _sum_chips(ri, ro, rs, pi, po, ps, where):
    def body(w_ref, ri_ref, ro_ref, rs_ref, pi_ref, po_ref, ps_ref, gi_ref, go_ref, gs_ref, g5_ref, loss_ref,
             acc_i, acc_o, acc_s):
        k = pl.program_id(0)
        accs = (acc_i, acc_o, acc_s)

        @pl.when(k == 0)
        def _():
            for acc in accs:
                acc[...] = jnp.zeros_like(acc)

        @pl.when(k == w_ref[0])
        def _():
            for acc, val in zip(accs, (pi_ref[0], po_ref[0], ps_ref[...])):
                acc[...] += val.astype(F32)

        @pl.when(k != w_ref[0])
        def _():
            for acc, ref in zip(accs, (ri_ref, ro_ref, rs_ref)):
                acc[...] += ref[0].astype(F32)

        @pl.when(k == NCHIP - 1)
        def _():
            gi_ref[0] = acc_i[...]
            go_ref[0] = acc_o[...]
            gs_ref[...] = acc_s[...]
            g5_ref[...] = jnp.zeros_like(g5_ref)
            for i, row in enumerate((ROW_NORM_G, ROW_CONV_B, ROW_LN_G, ROW_LN_B, ROW_FINAL_G)):
                g5_ref[i:i + 1, :] = acc_s[row:row + 1, :]
            loss = jnp.sum(acc_s[ROW_LOSS:ROW_LOSS + 1, :], axis=1, keepdims=True)
            loss_ref[...] = jnp.broadcast_to(loss, loss_ref.shape)

    def sent(k, w):
        return jnp.where(k == w[0], (k + 1) % NCHIP, k)

    hi, ho = D // 2, WOUT_SHARD // 2
    const = lambda shape: pl.BlockSpec(shape, lambda k, w: (0,) * len(shape))
    grid_spec = pltpu.PrefetchScalarGridSpec(
        num_scalar_prefetch=1, grid=(NCHIP,),
        in_specs=[pl.BlockSpec((1, hi, CHUNK), lambda k, w: (sent(k, w), 0, 0)),
                  pl.BlockSpec((1, ho, D), lambda k, w: (sent(k, w), 0, 0)),
                  pl.BlockSpec((1, SMALL_ROWS, D), lambda k, w: (sent(k, w), 0, 0)),
                  pl.BlockSpec((1, hi, CHUNK), lambda k, w: (w[0], 0, 0)),
                  pl.BlockSpec((1, ho, D), lambda k, w: (w[0], 0, 0)),
                  const((SMALL_ROWS, D))],
        out_specs=[pl.BlockSpec((1, hi, CHUNK), lambda k, w: (w[1], 0, 0)),
                   pl.BlockSpec((1, ho, D), lambda k, w: (w[1], 0, 0)),
                   const((SMALL_ROWS, D)), const((8, D)), const((8, LANES))],
        scratch_shapes=[pltpu.VMEM((hi, CHUNK), F32), pltpu.VMEM((ho, D), F32), pltpu.VMEM((SMALL_ROWS, D), F32)])
    return pl.pallas_call(
        body, grid_spec=grid_spec, name="sum_chips",
        out_shape=[jax.ShapeDtypeStruct((2, hi, CHUNK), F32), jax.ShapeDtypeStruct((2, ho, D), F32),
                   jax.ShapeDtypeStruct((SMALL_ROWS, D), F32), jax.ShapeDtypeStruct((8, D), F32),
                   jax.ShapeDtypeStruct((8, LANES), F32)],
        compiler_params=_params(("arbitrary",)),
    )(where, ri, ro, rs, pi, po, ps)


def _exchange_results(gi2, go2):
    def body(_gi, _go, gi_ref, go_ref, send, recv):
        x, y, c = _pos()
        sib = (x, y, 1 - c)

        def copy(k, ref, slot):
            return pltpu.make_async_remote_copy(src_ref=ref.at[slot], dst_ref=ref.at[slot], send_sem=send.at[k],
                                                recv_sem=recv.at[k], device_id=sib, device_id_type=MESH)

        sends = [copy(k, ref, c) for k, ref in enumerate((gi_ref, go_ref))]
        for cp in sends:
            cp.start()
        for k, ref in enumerate((gi_ref, go_ref)):
            copy(k, ref, 1 - c).wait_recv()
        for cp in sends:
            cp.wait_send()

    return pl.pallas_call(
        body, name="exchange_results",
        in_specs=[ANY, ANY], out_specs=[ANY, ANY], input_output_aliases={0: 0, 1: 1},
        out_shape=[jax.ShapeDtypeStruct((2, D // 2, CHUNK), F32), jax.ShapeDtypeStruct((2, WOUT_SHARD // 2, D), F32)],
        scratch_shapes=[pltpu.SemaphoreType.DMA((2,)), pltpu.SemaphoreType.DMA((2,))],
    )(gi2, go2)


def _adamw_math(w, g, m, v):
    m2 = ADAM_B1 * m + (1.0 - ADAM_B1) * g
    v2 = ADAM_B2 * v + (1.0 - ADAM_B2) * (g * g)
    m_hat = m2 / (1.0 - ADAM_B1 ** ADAM_STEP)
    v_hat = v2 / (1.0 - ADAM_B2 ** ADAM_STEP)
    delta = -ADAM_LR * (m_hat / (jnp.sqrt(v_hat) + ADAM_EPS) + ADAM_WD * w)
    return delta, m2, v2


def _adamw(w, g, m, v, name):
    rows, cols = w.shape
    tm = 256 if rows % 256 == 0 else rows

    def body(w_ref, g_ref, m_ref, v_ref, d_ref, m2_ref, v2_ref):
        d_ref[...], m2_ref[...], v2_ref[...] = _adamw_math(w_ref[...], g_ref[...], m_ref[...], v_ref[...])

    shape = jax.ShapeDtypeStruct(w.shape, F32)
    return pl.pallas_call(
        body, grid=(rows // tm,), name=name,
        in_specs=[_rows(tm, cols)] * 4, out_specs=[_rows(tm, cols)] * 3, out_shape=[shape] * 3,
        compiler_params=_params(("arbitrary",)),
    )(w, g, m, v)


def _adamw_vectors(g5, ws, ms, vs):
    n = len(ws)

    def body(g_ref, *refs):
        ins, outs = refs[:3 * n], refs[3 * n:]
        for i in range(n):
            res = _adamw_math(ins[i][...], g_ref[i:i + 1, :], ins[n + i][...], ins[2 * n + i][...])
            for kind in range(3):
                outs[kind * n + i][...] = res[kind]

    shape = jax.ShapeDtypeStruct((1, D), F32)
    return pl.pallas_call(body, name="adamw_vectors", out_shape=[shape] * (3 * n), compiler_params=_params())(
        g5, *ws, *ms, *vs)


def kernel(x, norm_g, w_in, conv_w, conv_b, conv_ln_g, conv_ln_b, w_out, final_norm_g, loss_target, m_norm_g, m_w_in, m_conv_w, m_conv_b, m_conv_ln_g, m_conv_ln_b, m_w_out, m_final_norm_g, v_norm_g, v_w_in, v_conv_w, v_conv_b, v_conv_ln_g, v_conv_ln_b, v_w_out, v_final_norm_g):
    chip = 2 * lax.axis_index("x") + lax.axis_index("y")
    where = jnp.stack([chip, lax.axis_index("c")]).astype(jnp.int32)
    taps_shard = jnp.pad(conv_w[0], ((0, HALO - CONV_K), (0, 0)))
    wi_full, wo_full, cw_full = _gather_weights(*_place_shards(w_in[0], w_out[0], taps_shard, where))

    gf = final_norm_g[None]
    grad_x, dw_in4, dw_out, small = _local_step(
        x[0], loss_target[0], norm_g, wi_full, cw_full, conv_b, conv_ln_g, conv_ln_b, wo_full, gf)
    dw_out4 = dw_out.reshape(NCHIP, WOUT_SHARD, D)

    ri, ro, rs = _exchange_halves(dw_in4, dw_out4, small)
    pi, po, ps = _add_halves(dw_in4, ri, dw_out4, ro, small, rs)
    ri, ro, rs = _exchange_chips(pi, po, ps)
    gi2, go2, g_small, g5, loss8 = _sum_chips(ri, ro, rs, pi, po, ps, where)
    gi2, go2 = _exchange_results(gi2, go2)
    g_w_in = gi2.reshape(D, CHUNK)
    g_w_out = go2.reshape(WOUT_SHARD, D)
    g_taps = lax.dynamic_slice(g_small, (ROW_TAPS, chip * CONVW_SHARD), (CONV_K, CONVW_SHARD))

    d_w_in, m2_w_in, v2_w_in = _adamw(w_in[0], g_w_in, m_w_in[0], v_w_in[0], "adamw_w_in")
    d_w_out, m2_w_out, v2_w_out = _adamw(w_out[0], g_w_out, m_w_out[0], v_w_out[0], "adamw_w_out")
    d_taps, m2_taps, v2_taps = _adamw(conv_w[0], g_taps, m_conv_w[0], v_conv_w[0], "adamw_conv_w")
    vec = _adamw_vectors(
        g5,
        (norm_g, conv_b, conv_ln_g, conv_ln_b, gf),
        (m_norm_g, m_conv_b, m_conv_ln_g, m_conv_ln_b, m_final_norm_g[None]),
        (v_norm_g, v_conv_b, v_conv_ln_g, v_conv_ln_b, v_final_norm_g[None]))
    d_vec, m2_vec, v2_vec = vec[0:5], vec[5:10], vec[10:15]

    def weight_order(ng, wi, cw, cb, lg, lb, wo, fg):
        return (ng, wi[None], cw[None], cb, lg, lb, wo[None], fg[0])

    grads = weight_order(g5[0:1], g_w_in, g_taps, g5[1:2], g5[2:3], g5[3:4], g_w_out, g5[4:5])
    deltas = weight_order(d_vec[0], d_w_in, d_taps, d_vec[1], d_vec[2], d_vec[3], d_w_out, d_vec[4])
    new_m = weight_order(m2_vec[0], m2_w_in, m2_taps, m2_vec[1], m2_vec[2], m2_vec[3], m2_w_out, m2_vec[4])
    new_v = weight_order(v2_vec[0], v2_w_in, v2_taps, v2_vec[1], v2_vec[2], v2_vec[3], v2_w_out, v2_vec[4])
    return (loss8[0, 0], grad_x[None], *grads, *deltas, *new_m, *new_v)
```

```python
import jax
import jax.numpy as jnp
from jax import lax
from jax.experimental import pallas as pl
from jax.experimental.pallas import tpu as pltpu

F32 = jnp.float32
BF16 = jnp.bfloat16

S = 4096
D = 1024
LANES = 128
HD = 64
NKV = 4
GQ = 4
KVW = NKV * HD
NCOL = 5632
CONV_K = 31
HALO = 32
BLK = 128
PATTERNS = (1, 4, 16)
NORM_EPS = 1e-6
LN_EPS = 1e-5
NEG = -1e30
OFF_Q, OFF_K, OFF_V, OFF_AG, OFF_CV, OFF_CG, OFF_CGATE = 0, 1024, 1280, 1536, 2560, 3584, 4608
NCHIP = 4
CHUNK = NCOL // NCHIP
WOUT_ROWS = 2 * D
WOUT_SHARD = WOUT_ROWS // NCHIP
CONVW_SHARD = D // NCHIP

ADAM_LR, ADAM_B1, ADAM_B2, ADAM_EPS, ADAM_WD, ADAM_STEP = 0.001, 0.9, 0.999, 1e-08, 0.01, 10

VMEM_LIMIT = 56 * 1024 * 1024


def _params(sem=None, vmem=VMEM_LIMIT):
    return pltpu.CompilerParams(dimension_semantics=sem, vmem_limit_bytes=vmem)


def _sigmoid(a):
    return 0.5 * jnp.tanh(0.5 * a) + 0.5


def _rows(tm, width):
    return pl.BlockSpec((tm, width), lambda i: (i, 0))


def _slabs(n):
    return jax.ShapeDtypeStruct((n, S, LANES), F32)


def _slab_rows(n, tm):
    return pl.BlockSpec((n, tm, LANES), lambda i: (0, i, 0))


def _resident(shape):
    return pl.BlockSpec(shape, lambda *_: (0,) * len(shape), pipeline_mode=pl.Buffered(1))


def _dot(a, b):
    return jnp.dot(a, b, preferred_element_type=F32)


def _dot_nt(a, b):
    return lax.dot_general(a, b, (((1,), (1,)), ((), ())), preferred_element_type=F32)


def _dot_tn(a, b):
    return lax.dot_general(a, b, (((0,), (0,)), ((), ())), preferred_element_type=F32)


def _inproj_fwd(x, g1, w_bf):
    tm = 256

    def body(x_ref, g_ref, w_ref, h_ref, q_ref, k_ref, v_ref, ag_ref, cv_ref, cg_ref, cgate_ref):
        xt = x_ref[...]
        r = lax.rsqrt(jnp.mean(xt * xt, axis=-1, keepdims=True) + NORM_EPS)
        h = (xt * r * g_ref[...]).astype(BF16)
        h_ref[...] = h
        q = _dot(h, w_ref[:, OFF_Q:OFF_Q + D]) * (HD ** -0.5)
        kv = _dot(h, w_ref[:, OFF_K:OFF_K + 2 * KVW])
        for sl in range(D // LANES):
            q_ref[sl] = q[:, sl * LANES:(sl + 1) * LANES]
        for sl in range(KVW // LANES):
            k_ref[sl] = kv[:, sl * LANES:(sl + 1) * LANES]
            v_ref[sl] = kv[:, KVW + sl * LANES:KVW + (sl + 1) * LANES]
        ag_ref[...] = _dot(h, w_ref[:, OFF_AG:OFF_AG + D])
        cv_ref[...] = _dot(h, w_ref[:, OFF_CV:OFF_CV + D])
        cg_ref[...] = _dot(h, w_ref[:, OFF_CG:OFF_CG + D])
        cgate_ref[...] = _dot(h, w_ref[:, OFF_CGATE:OFF_CGATE + D])

    big = jax.ShapeDtypeStruct((S, D), F32)
    return pl.pallas_call(
        body, grid=(S // tm,), name="inproj_fwd",
        in_specs=[_rows(tm, D), _resident((1, D)), _resident((D, NCOL))],
        out_specs=[_rows(tm, D), _slab_rows(D // LANES, tm), _slab_rows(KVW // LANES, tm), _slab_rows(KVW // LANES, tm),
                   _rows(tm, D), _rows(tm, D), _rows(tm, D), _rows(tm, D)],
        out_shape=[jax.ShapeDtypeStruct((S, D), BF16), _slabs(D // LANES), _slabs(KVW // LANES), _slabs(KVW // LANES),
                   big, big, big, big],
        compiler_params=_params(("arbitrary",)),
    )(x, g1, w_bf)


def _bias_table(d):
    h = jnp.arange(NKV * GQ, dtype=F32)
    slopes = jnp.exp2(-8.0 * (h + 1.0) / (NKV * GQ))
    qi = jnp.arange(BLK)[:, None]
    kj = jnp.arange(2 * BLK)[None, :]
    dist = BLK + qi - kj
    window = (dist >= 0) & (dist <= BLK)
    bias = -slopes[:, None, None] * (dist * d).astype(F32)[None]
    has_prev = jnp.stack([jnp.broadcast_to(kj >= BLK, (BLK, 2 * BLK)), jnp.ones((BLK, 2 * BLK), bool)])
    valid = window[None] & has_prev
    tab = jnp.where(valid[:, None], bias[None], NEG)
    return tab.reshape(2, NKV, GQ * BLK, 2 * BLK)


def _sub_rows(start, d):
    if d == 1:
        return pl.ds(pl.multiple_of(start, BLK), BLK)
    return pl.ds(start, BLK, stride=d)


NHEAD = NKV * GQ
CHUNK_ROWS = 2048
BLOCKS_PER_CHUNK = CHUNK_ROWS // BLK


def _low_lanes(rows=BLK):
    return lax.broadcasted_iota(jnp.int32, (rows, LANES), 1) < HD


def _block_start(idx, d):
    shift = d.bit_length() - 1
    b, r = lax.shift_right_logical(idx, shift), lax.bitwise_and(idx, d - 1)
    start = b * (BLK * d) + r
    return b, start, jnp.maximum(start - BLK * d, r)


def _stack_heads(ref, rows):
    low = _low_lanes()
    t0, t1 = ref[0, rows, :], ref[1, rows, :]
    return jnp.concatenate([jnp.where(low, t0, 0.0), jnp.where(low, 0.0, t0),
                            jnp.where(low, t1, 0.0), jnp.where(low, 0.0, t1)], axis=0).astype(BF16)


def _unstack_heads(dup):
    low = _low_lanes()
    return (jnp.where(low, dup[0:BLK], dup[BLK:2 * BLK]), jnp.where(low, dup[2 * BLK:3 * BLK], dup[3 * BLK:4 * BLK]))


def _kv_dup(ref, prow, rows, odd):
    t = jnp.concatenate([ref[0, prow, :], ref[0, rows, :]], axis=0)
    swapped = pltpu.roll(t, HD, axis=1)
    keep = jnp.logical_xor(_low_lanes(2 * BLK), odd)
    return jnp.where(keep, t, swapped).astype(BF16)


def _attn_fwd(q, k, v, bias, d):
    def body(q_ref, k_ref, v_ref, b_ref, o_ref, l_ref):
        odd = pl.program_id(0) % 2 == 1
        ones = jnp.ones((2 * BLK, LANES), BF16)

        def block(idx, carry):
            b, start, pstart = _block_start(idx, d)
            rows, prow = _sub_rows(start, d), _sub_rows(pstart, d)
            qs = _stack_heads(q_ref, rows)
            kw = _kv_dup(k_ref, prow, rows, odd)
            vw = _kv_dup(v_ref, prow, rows, odd)
            s = _dot_nt(qs, kw) + b_ref[jnp.minimum(b, 1), 0]
            m = jnp.max(s, axis=1, keepdims=True)
            p = jnp.exp(s - m).astype(BF16)
            ol = _dot(p, jnp.concatenate([vw, ones], axis=1))
            l = ol[:, LANES:]
            o_ref[0, rows, :], o_ref[1, rows, :] = _unstack_heads(ol[:, :LANES] / l)
            lse = m + jnp.log(l)
            for g in range(GQ):
                l_ref[g, rows, :] = lse[g * BLK:(g + 1) * BLK]
            return carry

        lax.fori_loop(0, S // BLK, block, 0, unroll=2)

    q_like = pl.BlockSpec((2, S, LANES), lambda j: (j, 0, 0))
    kv = pl.BlockSpec((1, S, LANES), lambda j: (j // 2, 0, 0))
    heads = pl.BlockSpec((GQ, S, LANES), lambda j: (j, 0, 0))
    bias_spec = pl.BlockSpec((2, 1, GQ * BLK, 2 * BLK), lambda j: (0, j, 0, 0))
    return pl.pallas_call(
        body, grid=(NKV,), name=f"attn_fwd_d{d}",
        in_specs=[q_like, kv, kv, bias_spec],
        out_specs=[q_like, heads],
        out_shape=[_slabs(D // LANES), _slabs(NHEAD)],
        compiler_params=_params(("arbitrary",)),
    )(q, k, v, bias)


def _attn_combine(outs, lses, a_gate):
    tm = 256

    def body(o1, o2, o3, l1, l2, l3, ag_ref, o_ref, lse_ref, y_ref):
        low = _low_lanes(tm)
        for sl in range(D // LANES):
            w = []
            for h in (2 * sl, 2 * sl + 1):
                a, b, c = l1[h], l2[h], l3[h]
                m = jnp.maximum(jnp.maximum(a, b), c)
                ea, eb, ec = jnp.exp(a - m), jnp.exp(b - m), jnp.exp(c - m)
                den = ea + eb + ec
                lse_ref[h] = m + jnp.log(den)
                inv = 1.0 / den
                w.append((ea * inv, eb * inv, ec * inv))
            wa, wb, wc = (jnp.where(low, w[0][i], w[1][i]) for i in range(3))
            o = wa * o1[sl] + wb * o2[sl] + wc * o3[sl]
            o_ref[sl] = o
            cols = slice(sl * LANES, (sl + 1) * LANES)
            ag = ag_ref[:, cols]
            y_ref[:, cols] = (o * (ag * _sigmoid(ag))).astype(BF16)

    wide, per_head = _slab_rows(D // LANES, tm), _slab_rows(NHEAD, tm)
    return pl.pallas_call(
        body, grid=(S // tm,), name="attn_combine",
        in_specs=[wide] * 3 + [per_head] * 3 + [_rows(tm, D)],
        out_specs=[wide, per_head, _rows(tm, D)],
        out_shape=[_slabs(D // LANES), _slabs(NHEAD), jax.ShapeDtypeStruct((S, D), BF16)],
        compiler_params=_params(("arbitrary",)),
    )(*outs, *lses, a_gate)


def _head_sum_selectors():
    lane_in = jnp.arange(LANES)[:, None] // HD
    return jnp.stack([jnp.broadcast_to(lane_in == h, (LANES, LANES)) for h in range(2)]).astype(BF16)


def _attn_gate_bwd(dy_att, o, a_gate, selectors):
    tm = 256

    def body(dy_ref, o_ref, ag_ref, e_ref, do_ref, dag_ref, delta_ref):
        for sl in range(D // LANES):
            cols = slice(sl * LANES, (sl + 1) * LANES)
            dy, ag, o_ = dy_ref[:, cols], ag_ref[:, cols], o_ref[sl]
            sg = _sigmoid(ag)
            do = dy * (ag * sg)
            do_ref[sl] = do
            dag_ref[:, cols] = dy * o_ * (sg * (1.0 + ag * (1.0 - sg)))
            prod = do * o_
            hi = prod.astype(BF16)
            lo = (prod - hi.astype(F32)).astype(BF16)
            for h in range(2):
                delta_ref[2 * sl + h] = _dot(hi, e_ref[h]) + _dot(lo, e_ref[h])

    return pl.pallas_call(
        body, grid=(S // tm,), name="attn_gate_bwd",
        in_specs=[_rows(tm, D), _slab_rows(D // LANES, tm), _rows(tm, D), _resident((2, LANES, LANES))],
        out_specs=[_slab_rows(D // LANES, tm), _rows(tm, D), _slab_rows(NHEAD, tm)],
        out_shape=[_slabs(D // LANES), jax.ShapeDtypeStruct((S, D), F32), _slabs(NHEAD)],
        compiler_params=_params(("arbitrary",)),
    )(dy_att, o, a_gate, selectors)


def _attn_bwd(q, k, v, do, lse, delta, bias, d):
    def body(q_ref, do_ref, l_ref, dl_ref, k_ref, v_ref, b_ref, dq_ref, dkv_ref):
        odd = pl.program_id(0) % 2 == 1
        chunk = pl.program_id(1)

        @pl.when(chunk == 0)
        def _():
            dkv_ref[...] = jnp.zeros_like(dkv_ref)

        def block(idx, carry):
            b, start, pstart = _block_start(chunk * BLOCKS_PER_CHUNK + idx, d)
            rows, prow = _sub_rows(start, d), _sub_rows(pstart, d)
            mine = _sub_rows(start - chunk * CHUNK_ROWS, d)
            qs = _stack_heads(q_ref, mine)
            dos = _stack_heads(do_ref, mine)
            lse_t = jnp.concatenate([l_ref[g, mine, :] for g in range(GQ)], axis=0)
            delta_t = jnp.concatenate([dl_ref[g, mine, :] for g in range(GQ)], axis=0)
            kw = _kv_dup(k_ref, prow, rows, odd)
            vw = _kv_dup(v_ref, prow, rows, odd)
            s = _dot_nt(qs, kw) + b_ref[jnp.minimum(b, 1), 0]
            p = jnp.exp(s - jnp.concatenate([lse_t, lse_t], axis=1))
            dv2 = _dot_tn(p.astype(BF16), dos)
            dp = _dot_nt(dos, vw)
            ds = (p * (dp - jnp.concatenate([delta_t, delta_t], axis=1))).astype(BF16)
            dq_ref[0, mine, :], dq_ref[1, mine, :] = _unstack_heads(_dot(ds, kw))
            dk2 = _dot_tn(ds, qs)
            dkv = jnp.where(_low_lanes(2 * BLK), dk2 + pltpu.roll(dk2, HD, axis=1), dv2 + pltpu.roll(dv2, HD, axis=1))
            dkv_ref[0, rows, :] = dkv_ref[0, rows, :] + dkv[BLK:]
            dkv_ref[0, prow, :] = dkv_ref[0, prow, :] + dkv[:BLK]
            return carry

        lax.fori_loop(0, BLOCKS_PER_CHUNK, block, 0, unroll=4)

    q_like = pl.BlockSpec((2, CHUNK_ROWS, LANES), lambda j, c: (j, c, 0))
    heads = pl.BlockSpec((GQ, CHUNK_ROWS, LANES), lambda j, c: (j, c, 0))
    kv = pl.BlockSpec((1, S, LANES), lambda j, c: (j // 2, 0, 0))
    per_kv = pl.BlockSpec((1, S, LANES), lambda j, c: (j, 0, 0))
    bias_spec = pl.BlockSpec((2, 1, GQ * BLK, 2 * BLK), lambda j, c: (0, j, 0, 0))
    return pl.pallas_call(
        body, grid=(NKV, S // CHUNK_ROWS), name=f"attn_bwd_d{d}",
        in_specs=[q_like, q_like, heads, heads, kv, kv, bias_spec],
        out_specs=[q_like, per_kv],
        out_shape=[_slabs(D // LANES), _slabs(NKV)],
        compiler_params=_params(("arbitrary", "arbitrary")),
    )(q, do, lse, delta, k, v, bias)


CONV_T = 128


def _halo_before(i):
    return (jnp.maximum(i * (CONV_T // HALO) - 1, 0), 0)


def _halo_after(i):
    return (jnp.minimum((i + 1) * (CONV_T // HALO), S // HALO - 1), 0)


SUBLANES = 8
NCH = D // LANES
GROUP = SUBLANES * SUBLANES


def _comb(ref, cb, base):
    return ref[cb, pl.ds(base, SUBLANES, stride=SUBLANES), :]


def _taps(w_ref, cols):
    return [jnp.broadcast_to(w_ref[j:j + 1, cols], (SUBLANES, LANES)) for j in range(CONV_K)]


def _conv_fwd(c_val, c_glu, c_gate, conv_w, conv_b, ln_g, ln_b):
    T = CONV_T

    def body(cv_ref, cg_ref, cvh_ref, cgh_ref, gate_ref, w_ref, b_ref, lg_ref, lb_ref, u_ref, y_ref, win, us):
        i = pl.program_id(0)
        for cb in range(NCH):
            cols = slice(cb * LANES, (cb + 1) * LANES)
            win[cb, HALO:HALO + T, :] = cv_ref[:, cols] * _sigmoid(cg_ref[:, cols])
            win[cb, 0:HALO, :] = jnp.where(i > 0, cvh_ref[:, cols] * _sigmoid(cgh_ref[:, cols]), 0.0)
        for cb in range(NCH):
            cols = slice(cb * LANES, (cb + 1) * LANES)
            taps = _taps(w_ref, cols)
            bias = jnp.broadcast_to(b_ref[:, cols], (SUBLANES, LANES))

            def group(g, carry):
                for b in range(SUBLANES):
                    base = g * GROUP + b
                    acc = bias
                    for j in range(CONV_K):
                        acc = acc + taps[j] * _comb(win, cb, base + (HALO - (CONV_K - 1) + j))
                    us[cb, pl.ds(base, SUBLANES, stride=SUBLANES), :] = acc
                return carry

            lax.fori_loop(0, T // GROUP, group, 0)
        total = us[0]
        for cb in range(1, NCH):
            total = total + us[cb]
        mu = jnp.sum(total, axis=-1, keepdims=True) * (1.0 / D)
        sq = jnp.zeros((T, LANES), F32)
        for cb in range(NCH):
            uc = us[cb] - mu
            sq = sq + uc * uc
        rstd = lax.rsqrt(jnp.sum(sq, axis=-1, keepdims=True) * (1.0 / D) + LN_EPS)
        for cb in range(NCH):
            cols = slice(cb * LANES, (cb + 1) * LANES)
            u = us[cb]
            u_ref[:, cols] = u
            nrm = (u - mu) * rstd * lg_ref[:, cols] + lb_ref[:, cols]
            gate = gate_ref[:, cols]
            y_ref[:, cols] = (nrm * _sigmoid(nrm) * (gate * _sigmoid(gate))).astype(BF16)

    halo = pl.BlockSpec((HALO, D), _halo_before)
    return pl.pallas_call(
        body, grid=(S // T,), name="conv_fwd",
        in_specs=[_rows(T, D), _rows(T, D), halo, halo, _rows(T, D),
                  _resident((HALO, D)), _resident((1, D)), _resident((1, D)), _resident((1, D))],
        out_specs=[_rows(T, D), _rows(T, D)],
        out_shape=[jax.ShapeDtypeStruct((S, D), F32), jax.ShapeDtypeStruct((S, D), BF16)],
        scratch_shapes=[pltpu.VMEM((NCH, T + HALO, LANES), F32), pltpu.VMEM((NCH, T, LANES), F32)],
        compiler_params=_params(("arbitrary",)),
    )(c_val, c_glu, c_val, c_glu, c_gate, conv_w, conv_b, ln_g, ln_b)


def _conv_bwd_rows(u, c_gate, dy_conv, ln_g, ln_b):
    tm = 256

    def body(u_ref, gate_ref, dy_ref, lg_ref, lb_ref, du_ref, dgate_ref, st_ref):
        @pl.when(pl.program_id(0) == 0)
        def _():
            st_ref[...] = jnp.zeros_like(st_ref)

        u, gate, dy = u_ref[...], gate_ref[...], dy_ref[...]
        mu = jnp.mean(u, axis=-1, keepdims=True)
        uc = u - mu
        rstd = lax.rsqrt(jnp.mean(uc * uc, axis=-1, keepdims=True) + LN_EPS)
        z = uc * rstd
        nrm = z * lg_ref[...] + lb_ref[...]
        sn, sg = _sigmoid(nrm), _sigmoid(gate)
        dgate_ref[...] = dy * (nrm * sn) * (sg * (1.0 + gate * (1.0 - sg)))
        dn = dy * (gate * sg) * (sn * (1.0 + nrm * (1.0 - sn)))
        dz = dn * lg_ref[...]
        du = rstd * (dz - jnp.mean(dz, axis=-1, keepdims=True) - z * jnp.mean(dz * z, axis=-1, keepdims=True))
        du_ref[...] = du
        st_ref[0:1, :] += jnp.sum(dn * z, axis=0, keepdims=True)
        st_ref[1:2, :] += jnp.sum(dn, axis=0, keepdims=True)
        st_ref[2:3, :] += jnp.sum(du, axis=0, keepdims=True)

    big = jax.ShapeDtypeStruct((S, D), F32)
    return pl.pallas_call(
        body, grid=(S // tm,), name="conv_bwd_rows",
        in_specs=[_rows(tm, D)] * 3 + [_resident((1, D)), _resident((1, D))],
        out_specs=[_rows(tm, D), _rows(tm, D), pl.BlockSpec((8, D), lambda i: (0, 0))],
        out_shape=[big, big, jax.ShapeDtypeStruct((8, D), F32)],
        compiler_params=_params(("arbitrary",)),
    )(u, c_gate, dy_conv, ln_g, ln_b)


def _conv_bwd_taps(du, c_val, c_glu, conv_w):
    T = CONV_T
    last = S // T - 1

    def body(du_ref, dua_ref, cv_ref, cg_ref, cvh_ref, cgh_ref, w_ref, dcv_ref, dcg_ref, dw_ref,
             hwin, dwin, dhs, dw_acc):
        i = pl.program_id(0)

        @pl.when(i == 0)
        def _():
            dw_acc[...] = jnp.zeros_like(dw_acc)

        for cb in range(NCH):
            cols = slice(cb * LANES, (cb + 1) * LANES)
            hwin[cb, HALO:HALO + T, :] = cv_ref[:, cols] * _sigmoid(cg_ref[:, cols])
            hwin[cb, 0:HALO, :] = jnp.where(i > 0, cvh_ref[:, cols] * _sigmoid(cgh_ref[:, cols]), 0.0)
            dwin[cb, 0:T, :] = du_ref[:, cols]
            dwin[cb, T:T + HALO, :] = jnp.where(i < last, dua_ref[:, cols], 0.0)
        for cb in range(NCH):
            cols = slice(cb * LANES, (cb + 1) * LANES)
            taps = _taps(w_ref, cols)

            def group_dh(g, carry):
                for b in range(SUBLANES):
                    base = g * GROUP + b
                    acc = jnp.zeros((SUBLANES, LANES), F32)
                    for j in range(CONV_K):
                        acc = acc + taps[j] * _comb(dwin, cb, base + (CONV_K - 1 - j))
                    dhs[cb, pl.ds(base, SUBLANES, stride=SUBLANES), :] = acc
                return carry

            lax.fori_loop(0, T // GROUP, group_dh, 0)

            def group_dw(g, sums):
                for b in range(SUBLANES):
                    base = g * GROUP + b
                    d = _comb(dwin, cb, base)
                    sums = tuple(sums[j] + d * _comb(hwin, cb, base + (HALO - (CONV_K - 1) + j))
                                 for j in range(CONV_K))
                return sums

            sums = lax.fori_loop(0, T // GROUP, group_dw, tuple(dw_acc[j, :, cols] for j in range(CONV_K)))
            for j in range(CONV_K):
                dw_acc[j, :, cols] = sums[j]
            dh = dhs[cb]
            cv, sg = cv_ref[:, cols], _sigmoid(cg_ref[:, cols])
            dcv_ref[:, cols] = dh * sg
            dcg_ref[:, cols] = dh * cv * (sg * (1.0 - sg))

        @pl.when(i == last)
        def _():
            dw_ref[...] = jnp.zeros_like(dw_ref)
            for j in range(CONV_K):
                dw_ref[j:j + 1, :] = jnp.sum(dw_acc[j], axis=0, keepdims=True)

    before = pl.BlockSpec((HALO, D), _halo_before)
    after = pl.BlockSpec((HALO, D), _halo_after)
    big = jax.ShapeDtypeStruct((S, D), F32)
    return pl.pallas_call(
        body, grid=(S // T,), name="conv_bwd_taps",
        in_specs=[_rows(T, D), after, _rows(T, D), _rows(T, D), before, before, _resident((HALO, D))],
        out_specs=[_rows(T, D), _rows(T, D), pl.BlockSpec((HALO, D), lambda i: (0, 0))],
        out_shape=[big, big, jax.ShapeDtypeStruct((HALO, D), F32)],
        scratch_shapes=[pltpu.VMEM((NCH, T + HALO, LANES), F32), pltpu.VMEM((NCH, T + HALO, LANES), F32),
                        pltpu.VMEM((NCH, T, LANES), F32), pltpu.VMEM((CONV_K, SUBLANES, D), F32)],
        compiler_params=_params(("arbitrary",)),
    )(du, du, c_val, c_glu, c_val, c_glu, conv_w)


def _outproj_loss(y_att, y_conv, w_out_bf, x, target, gf):
    tm = 256

    def body(ya_ref, yc_ref, w_ref, x_ref, t_ref, gf_ref, dx2_ref, dya_ref, dyc_ref, dw_ref, st_ref, acc):
        @pl.when(pl.program_id(0) == 0)
        def _():
            acc[...] = jnp.zeros_like(acc)
            st_ref[...] = jnp.zeros_like(st_ref)

        ya, yc = ya_ref[...], yc_ref[...]
        x2 = x_ref[...] + _dot(ya, w_ref[0:D, :]) + _dot(yc, w_ref[D:2 * D, :])
        r = lax.rsqrt(jnp.mean(x2 * x2, axis=-1, keepdims=True) + NORM_EPS)
        xn = x2 * r
        err = xn * gf_ref[...] - t_ref[...]
        dout = err * (1.0 / D)
        dxn = dout * gf_ref[...]
        dx2 = r * (dxn - xn * jnp.mean(dxn * xn, axis=-1, keepdims=True))
        dx2_ref[...] = dx2
        dx2b = dx2.astype(BF16)
        dya_ref[...] = _dot_nt(dx2b, w_ref[0:D, :])
        dyc_ref[...] = _dot_nt(dx2b, w_ref[D:2 * D, :])
        acc[0:D, :] += _dot_tn(ya, dx2b)
        acc[D:2 * D, :] += _dot_tn(yc, dx2b)
        st_ref[0:1, :] += jnp.sum(dout * xn, axis=0, keepdims=True)
        st_ref[1:2, :] += jnp.sum(err * err, axis=0, keepdims=True) * (0.5 / D)

        @pl.when(pl.program_id(0) == S // tm - 1)
        def _():
            dw_ref[...] = acc[...].astype(BF16)

    big = jax.ShapeDtypeStruct((S, D), F32)
    return pl.pallas_call(
        body, grid=(S // tm,), name="outproj_loss",
        in_specs=[_rows(tm, D), _rows(tm, D), _resident((WOUT_ROWS, D)), _rows(tm, D), _rows(tm, D), _resident((1, D))],
        out_specs=[_rows(tm, D), _rows(tm, D), _rows(tm, D),
                   pl.BlockSpec((WOUT_ROWS, D), lambda i: (0, 0)), pl.BlockSpec((8, D), lambda i: (0, 0))],
        out_shape=[big, big, big, jax.ShapeDtypeStruct((WOUT_ROWS, D), BF16), jax.ShapeDtypeStruct((8, D), F32)],
        scratch_shapes=[pltpu.VMEM((WOUT_ROWS, D), F32)],
        compiler_params=_params(("arbitrary",)),
    )(y_att, y_conv, w_out_bf, x, target, gf)


def _inproj_bwd_x(dqs, dkvs, dag, dcv, dcg, dcgate, w_bf, x, g1, dx2):
    tm = 256

    def body(dq1, dq2, dq3, dkv1, dkv2, dkv3, dag_ref, dcv_ref, dcg_ref, dcgate_ref,
             w_ref, x_ref, g_ref, dx2_ref, dp_ref, gx_ref, st_ref):
        @pl.when(pl.program_id(0) == 0)
        def _():
            st_ref[...] = jnp.zeros_like(st_ref)

        for sl in range(D // LANES):
            dq = (dq1[sl] + dq2[sl] + dq3[sl]) * (HD ** -0.5)
            dp_ref[:, OFF_Q + sl * LANES:OFF_Q + (sl + 1) * LANES] = dq.astype(BF16)
        for j in range(NKV):
            dkv = (dkv1[j] + dkv2[j] + dkv3[j]).astype(BF16)
            dp_ref[:, OFF_K + j * HD:OFF_K + (j + 1) * HD] = dkv[:, :HD]
            dp_ref[:, OFF_V + j * HD:OFF_V + (j + 1) * HD] = dkv[:, HD:]
        for off, ref in ((OFF_AG, dag_ref), (OFF_CV, dcv_ref), (OFF_CG, dcg_ref), (OFF_CGATE, dcgate_ref)):
            dp_ref[:, off:off + D] = ref[...].astype(BF16)
        dh = jnp.zeros((tm, D), F32)
        for off, width in ((OFF_Q, D), (OFF_K, 2 * KVW), (OFF_AG, D), (OFF_CV, D), (OFF_CG, D), (OFF_CGATE, D)):
            dh = dh + _dot_nt(dp_ref[:, off:off + width], w_ref[:, off:off + width])
        xt = x_ref[...]
        r = lax.rsqrt(jnp.mean(xt * xt, axis=-1, keepdims=True) + NORM_EPS)
        xn = xt * r
        dxn = dh * g_ref[...]
        gx_ref[...] = dx2_ref[...] + r * (dxn - xn * jnp.mean(dxn * xn, axis=-1, keepdims=True))
        st_ref[0:1, :] += jnp.sum(dh * xn, axis=0, keepdims=True)

    return pl.pallas_call(
        body, grid=(S // tm,), name="inproj_bwd_x",
        in_specs=[_slab_rows(D // LANES, tm)] * 3 + [_slab_rows(NKV, tm)] * 3 + [_rows(tm, D)] * 4
        + [_resident((D, NCOL)), _rows(tm, D), _resident((1, D)), _rows(tm, D)],
        out_specs=[_rows(tm, NCOL), _rows(tm, D), pl.BlockSpec((8, D), lambda i: (0, 0))],
        out_shape=[jax.ShapeDtypeStruct((S, NCOL), BF16), jax.ShapeDtypeStruct((S, D), F32),
                   jax.ShapeDtypeStruct((8, D), F32)],
        compiler_params=_params(("arbitrary",)),
    )(*dqs, *dkvs, dag, dcv, dcg, dcgate, w_bf, x, g1, dx2)


def _inproj_bwd_w(h, dproj):
    tk = 1024
    nk = S // tk

    def body(h_ref, dp_ref, o_ref, acc):
        i = pl.program_id(1)

        @pl.when(i == 0)
        def _():
            acc[...] = jnp.zeros_like(acc)

        acc[...] += _dot_tn(h_ref[...], dp_ref[...])

        @pl.when(i == nk - 1)
        def _():
            o_ref[0] = acc[...].astype(BF16)

    return pl.pallas_call(
        body, grid=(NCHIP, nk), name="inproj_bwd_w",
        in_specs=[pl.BlockSpec((tk, D), lambda c, i: (i, 0)), pl.BlockSpec((tk, CHUNK), lambda c, i: (i, c))],
        out_specs=pl.BlockSpec((1, D, CHUNK), lambda c, i: (c, 0, 0)),
        out_shape=jax.ShapeDtypeStruct((NCHIP, D, CHUNK), BF16),
        scratch_shapes=[pltpu.VMEM((D, CHUNK), F32)],
        compiler_params=_params(("arbitrary", "arbitrary")),
    )(h, dproj)


def _local_step(x, target, g1, w_in_bf, conv_w, conv_b, ln_g, ln_b, w_out_bf, gf):
    h, q, k, v, a_gate, c_val, c_glu, c_gate = _inproj_fwd(x, g1, w_in_bf)
    tables = [_bias_table(d) for d in PATTERNS]
    outs, lses = zip(*[_attn_fwd(q, k, v, t, d) for t, d in zip(tables, PATTERNS)])
    o, lse, y_att = _attn_combine(outs, lses, a_gate)
    u, y_conv = _conv_fwd(c_val, c_glu, c_gate, conv_w, conv_b, ln_g, ln_b)
    dx2, dy_att, dy_conv, dw_out, st_out = _outproj_loss(y_att, y_conv, w_out_bf, x, target, gf)

    do, da_gate, delta = _attn_gate_bwd(dy_att, o, a_gate, _head_sum_selectors())
    dqs, dkvs = zip(*[_attn_bwd(q, k, v, do, lse, delta, t, d) for t, d in zip(tables, PATTERNS)])

    du, dc_gate, st_conv = _conv_bwd_rows(u, c_gate, dy_conv, ln_g, ln_b)
    dc_val, dc_glu, dconv_w = _conv_bwd_taps(du, c_val, c_glu, conv_w)

    dproj, grad_x, st_in = _inproj_bwd_x(dqs, dkvs, da_gate, dc_val, dc_glu, dc_gate, w_in_bf, x, g1, dx2)
    dw_in = _inproj_bwd_w(h, dproj)
    small = jnp.concatenate([st_in, st_conv, st_out, dconv_w], axis=0)
    return grad_x, dw_in, dw_out, small


ROW_NORM_G, ROW_LN_G, ROW_LN_B, ROW_CONV_B, ROW_FINAL_G, ROW_LOSS, ROW_TAPS = 0, 8, 9, 10, 16, 17, 24
SMALL_ROWS = 24 + HALO


MESH = pl.DeviceIdType.MESH
ANY = pl.BlockSpec(memory_space=pl.ANY)
CHIP_FLIPS = ((1, 0), (0, 1), (1, 1))


def _pos():
    return lax.axis_index("x"), lax.axis_index("y"), lax.axis_index("c")


def _flip(v, f):
    return 1 - v if f else v


def _ds(start, size, align=None):
    return pl.ds(pl.multiple_of(start, align or size), size)


def _place_shards(wi, wo, cw, where):
    steps = 4

    def body(where_ref, wi_ref, wo_ref, cw_ref, wi_full, wo_full, cw_full):
        wi_full[...] = wi_ref[...].astype(BF16)
        wo_full[...] = wo_ref[...].astype(BF16)
        cw_full[...] = cw_ref[...]

    grid_spec = pltpu.PrefetchScalarGridSpec(
        num_scalar_prefetch=1, grid=(steps,),
        in_specs=[pl.BlockSpec((D // steps, CHUNK), lambda i, w: (i, 0)),
                  pl.BlockSpec((WOUT_SHARD // steps, D), lambda i, w: (i, 0)),
                  pl.BlockSpec((HALO, CONVW_SHARD), lambda i, w: (0, 0))],
        out_specs=[pl.BlockSpec((D // steps, CHUNK), lambda i, w: (i, w[0])),
                   pl.BlockSpec((WOUT_SHARD // steps, D), lambda i, w: (w[0] * steps + i, 0)),
                   pl.BlockSpec((HALO, CONVW_SHARD), lambda i, w: (0, w[0]))])
    return pl.pallas_call(
        body, grid_spec=grid_spec, name="place_shards",
        out_shape=[jax.ShapeDtypeStruct((D, NCOL), BF16), jax.ShapeDtypeStruct((WOUT_ROWS, D), BF16),
                   jax.ShapeDtypeStruct((HALO, D), F32)],
        compiler_params=_params(("arbitrary",)),
    )(where, wi, wo, cw)


def _gather_weights(wi_full, wo_full, cw_full):
    halves = (D // 2, WOUT_SHARD // 2, HALO // 2)
    n_ici = 3 * len(CHIP_FLIPS)

    def body(_wi, _wo, _cw, wi_full, wo_full, cw_full, send, recv):
        x, y, c = _pos()

        def region(a, px, py, half):
            chip = 2 * px + py
            n = halves[a]
            if a == 0:
                return wi_full.at[_ds(half * n, n), _ds(chip * CHUNK, CHUNK, 128)]
            if a == 1:
                return wo_full.at[_ds(chip * WOUT_SHARD + half * n, n), :]
            return cw_full.at[_ds(half * n, n), _ds(chip * CONVW_SHARD, CONVW_SHARD, 128)]

        def remote(k, src, dst, dev):
            return pltpu.make_async_remote_copy(src_ref=src, dst_ref=dst, send_sem=send.at[k], recv_sem=recv.at[k],
                                                device_id=dev, device_id_type=MESH)

        sends = []
        for a in range(3):
            for j, (fx, fy) in enumerate(CHIP_FLIPS):
                mine = region(a, x, y, c)
                cp = remote(3 * a + j, mine, mine, (_flip(x, fx), _flip(y, fy), c))
                cp.start()
                sends.append(cp)
        for a in range(3):
            for j, (fx, fy) in enumerate(CHIP_FLIPS):
                px, py = _flip(x, fx), _flip(y, fy)
                got = region(a, px, py, c)
                remote(3 * a + j, got, got, (px, py, c)).wait_recv()
                cp = remote(n_ici + 3 * a + j, got, got, (x, y, 1 - c))
                cp.start()
                sends.append(cp)
        for a in range(3):
            for j, (fx, fy) in enumerate(CHIP_FLIPS):
                got = region(a, _flip(x, fx), _flip(y, fy), 1 - c)
                remote(n_ici + 3 * a + j, got, got, (x, y, 1 - c)).wait_recv()
        for cp in sends:
            cp.wait_send()

    return pl.pallas_call(
        body, name="gather_weights",
        in_specs=[ANY, ANY, ANY], out_specs=[ANY, ANY, ANY], input_output_aliases={0: 0, 1: 1, 2: 2},
        out_shape=[jax.ShapeDtypeStruct((D, NCOL), BF16), jax.ShapeDtypeStruct((WOUT_ROWS, D), BF16),
                   jax.ShapeDtypeStruct((HALO, D), F32)],
        scratch_shapes=[pltpu.SemaphoreType.DMA((2 * n_ici,)), pltpu.SemaphoreType.DMA((2 * n_ici,))],
    )(wi_full, wo_full, cw_full)


def _exchange_halves(gi4, go4, small):
    def body(gi_ref, go_ref, sm_ref, ri_ref, ro_ref, rs_ref, send, recv):
        x, y, c = _pos()
        sib = (x, y, 1 - c)
        copies = [
            (gi_ref.at[:, _ds((1 - c) * (D // 2), D // 2), :], ri_ref),
            (go_ref.at[:, _ds((1 - c) * (WOUT_SHARD // 2), WOUT_SHARD // 2), :], ro_ref),
            (sm_ref, rs_ref),
        ]
        cps = [pltpu.make_async_remote_copy(src_ref=s_, dst_ref=d_, send_sem=send.at[k], recv_sem=recv.at[k],
                                            device_id=sib, device_id_type=MESH) for k, (s_, d_) in enumerate(copies)]
        for cp in cps:
            cp.start()
        for cp in cps:
            cp.wait()

    return pl.pallas_call(
        body, name="exchange_halves",
        in_specs=[ANY, ANY, ANY], out_specs=[ANY, ANY, ANY],
        out_shape=[jax.ShapeDtypeStruct((NCHIP, D // 2, CHUNK), BF16),
                   jax.ShapeDtypeStruct((NCHIP, WOUT_SHARD // 2, D), BF16),
                   jax.ShapeDtypeStruct((SMALL_ROWS, D), F32)],
        scratch_shapes=[pltpu.SemaphoreType.DMA((3,)), pltpu.SemaphoreType.DMA((3,))],
    )(gi4, go4, small)


def _add_halves(gi4, ri, go4, ro, small, rs):
    hi, ho = D // 2, WOUT_SHARD // 2

    def body(gi_ref, ri_ref, go_ref, ro_ref, sm_ref, rs_ref, pi_ref, po_ref, ps_ref):
        c = lax.axis_index("c")
        pi_ref[0] = (gi_ref[0, _ds(c * hi, hi), :].astype(F32) + ri_ref[0].astype(F32)).astype(BF16)
        po_ref[0] = (go_ref[0, _ds(c * ho, ho), :].astype(F32) + ro_ref[0].astype(F32)).astype(BF16)
        ps_ref[...] = sm_ref[...] + rs_ref[...]

    blk = lambda n, w: pl.BlockSpec((1, n, w), lambda k: (k, 0, 0))
    whole = pl.BlockSpec((SMALL_ROWS, D), lambda k: (0, 0))
    return pl.pallas_call(
        body, grid=(NCHIP,), name="add_halves",
        in_specs=[blk(D, CHUNK), blk(hi, CHUNK), blk(WOUT_SHARD, D), blk(ho, D), whole, whole],
        out_specs=[blk(hi, CHUNK), blk(ho, D), whole],
        out_shape=[jax.ShapeDtypeStruct((NCHIP, hi, CHUNK), BF16), jax.ShapeDtypeStruct((NCHIP, ho, D), BF16),
                   jax.ShapeDtypeStruct((SMALL_ROWS, D), F32)],
        compiler_params=_params(("arbitrary",)),
    )(gi4, ri, go4, ro, small, rs)


def _exchange_chips(pi, po, ps):
    def body(pi_ref, po_ref, ps_ref, ri_ref, ro_ref, rs_ref, send, recv):
        x, y, c = _pos()
        me = 2 * x + y
        srcs = (pi_ref, po_ref, ps_ref)
        dsts = (ri_ref, ro_ref, rs_ref)

        def piece(a, chip):
            return srcs[a] if a == 2 else srcs[a].at[chip]

        sends = []
        for a in range(3):
            for j, (fx, fy) in enumerate(CHIP_FLIPS):
                px, py = _flip(x, fx), _flip(y, fy)
                cp = pltpu.make_async_remote_copy(
                    src_ref=piece(a, 2 * px + py), dst_ref=dsts[a].at[me], send_sem=send.at[3 * a + j],
                    recv_sem=recv.at[3 * a + j], device_id=(px, py, c), device_id_type=MESH)
                cp.start()
                sends.append(cp)
        for a in range(3):
            for j, (fx, fy) in enumerate(CHIP_FLIPS):
                px, py = _flip(x, fx), _flip(y, fy)
                got = dsts[a].at[2 * px + py]
                pltpu.make_async_remote_copy(src_ref=got, dst_ref=got, send_sem=send.at[3 * a + j],
                                             recv_sem=recv.at[3 * a + j], device_id=(px, py, c),
                                             device_id_type=MESH).wait_recv()
        for cp in sends:
            cp.wait_send()

    return pl.pallas_call(
        body, name="exchange_chips",
        in_specs=[ANY, ANY, ANY], out_specs=[ANY, ANY, ANY],
        out_shape=[jax.ShapeDtypeStruct((NCHIP, D // 2, CHUNK), BF16),
                   jax.ShapeDtypeStruct((NCHIP, WOUT_SHARD // 2, D), BF16),
                   jax.ShapeDtypeStruct((NCHIP, SMALL_ROWS, D), F32)],
        scratch_shapes=[pltpu.SemaphoreType.DMA((9,)), pltpu.SemaphoreType.DMA((9,))],
    )(pi, po, ps)


def _sum_chips(ri, ro, rs, pi, po, ps, where):
    def body(w_ref, ri_ref, ro_ref, rs_ref, pi_ref, po_ref, ps_ref, gi_ref, go_ref, gs_ref, g5_ref, loss_ref,
             acc_i, acc_o, acc_s):
        k = pl.program_id(0)
        accs = (acc_i, acc_o, acc_s)

        @pl.when(k == 0)
        def _():
            for acc in accs:
                acc[...] = jnp.zeros_like(acc)

        @pl.when(k == w_ref[0])
        def _():
            for acc, val in zip(accs, (pi_ref[0], po_ref[0], ps_ref[...])):
                acc[...] += val.astype(F32)

        @pl.when(k != w_ref[0])
        def _():
            for acc, ref in zip(accs, (ri_ref, ro_ref, rs_ref)):
                acc[...] += ref[0].astype(F32)

        @pl.when(k == NCHIP - 1)
        def _():
            gi_ref[0] = acc_i[...]
            go_ref[0] = acc_o[...]
            gs_ref[...] = acc_s[...]
            g5_ref[...] = jnp.zeros_like(g5_ref)
            for i, row in enumerate((ROW_NORM_G, ROW_CONV_B, ROW_LN_G, ROW_LN_B, ROW_FINAL_G)):
                g5_ref[i:i + 1, :] = acc_s[row:row + 1, :]
            loss = jnp.sum(acc_s[ROW_LOSS:ROW_LOSS + 1, :], axis=1, keepdims=True)
            loss_ref[...] = jnp.broadcast_to(loss, loss_ref.shape)

    def sent(k, w):
        return jnp.where(k == w[0], (k + 1) % NCHIP, k)

    hi, ho = D // 2, WOUT_SHARD // 2
    const = lambda shape: pl.BlockSpec(shape, lambda k, w: (0,) * len(shape))
    grid_spec = pltpu.PrefetchScalarGridSpec(
        num_scalar_prefetch=1, grid=(NCHIP,),
        in_specs=[pl.BlockSpec((1, hi, CHUNK), lambda k, w: (sent(k, w), 0, 0)),
                  pl.BlockSpec((1, ho, D), lambda k, w: (sent(k, w), 0, 0)),
                  pl.BlockSpec((1, SMALL_ROWS, D), lambda k, w: (sent(k, w), 0, 0)),
                  pl.BlockSpec((1, hi, CHUNK), lambda k, w: (w[0], 0, 0)),
                  pl.BlockSpec((1, ho, D), lambda k, w: (w[0], 0, 0)),
                  const((SMALL_ROWS, D))],
        out_specs=[pl.BlockSpec((1, hi, CHUNK), lambda k, w: (w[1], 0, 0)),
                   pl.BlockSpec((1, ho, D), lambda k, w: (w[1], 0, 0)),
                   const((SMALL_ROWS, D)), const((8, D)), const((8, LANES))],
        scratch_shapes=[pltpu.VMEM((hi, CHUNK), F32), pltpu.VMEM((ho, D), F32), pltpu.VMEM((SMALL_ROWS, D), F32)])
    return pl.pallas_call(
        body, grid_spec=grid_spec, name="sum_chips",
        out_shape=[jax.ShapeDtypeStruct((2, hi, CHUNK), F32), jax.ShapeDtypeStruct((2, ho, D), F32),
                   jax.ShapeDtypeStruct((SMALL_ROWS, D), F32), jax.ShapeDtypeStruct((8, D), F32),
                   jax.ShapeDtypeStruct((8, LANES), F32)],
        compiler_params=_params(("arbitrary",)),
    )(where, ri, ro, rs, pi, po, ps)


def _exchange_results(gi2, go2):
    def body(_gi, _go, gi_ref, go_ref, send, recv):
        x, y, c = _pos()
        sib = (x, y, 1 - c)

        def copy(k, ref, slot):
            return pltpu.make_async_remote_copy(src_ref=ref.at[slot], dst_ref=ref.at[slot], send_sem=send.at[k],
                                                recv_sem=recv.at[k], device_id=sib, device_id_type=MESH)

        sends = [copy(k, ref, c) for k, ref in enumerate((gi_ref, go_ref))]
        for cp in sends:
            cp.start()
        for k, ref in enumerate((gi_ref, go_ref)):
            copy(k, ref, 1 - c).wait_recv()
        for cp in sends:
            cp.wait_send()

    return pl.pallas_call(
        body, name="exchange_results",
        in_specs=[ANY, ANY], out_specs=[ANY, ANY], input_output_aliases={0: 0, 1: 1},
        out_shape=[jax.ShapeDtypeStruct((2, D // 2, CHUNK), F32), jax.ShapeDtypeStruct((2, WOUT_SHARD // 2, D), F32)],
        scratch_shapes=[pltpu.SemaphoreType.DMA((2,)), pltpu.SemaphoreType.DMA((2,))],
    )(gi2, go2)


def _adamw_math(w, g, m, v):
    m2 = ADAM_B1 * m + (1.0 - ADAM_B1) * g
    v2 = ADAM_B2 * v + (1.0 - ADAM_B2) * (g * g)
    m_hat = m2 / (1.0 - ADAM_B1 ** ADAM_STEP)
    v_hat = v2 / (1.0 - ADAM_B2 ** ADAM_STEP)
    delta = -ADAM_LR * (m_hat / (jnp.sqrt(v_hat) + ADAM_EPS) + ADAM_WD * w)
    return delta, m2, v2


def _adamw(w, g, m, v, name):
    rows, cols = w.shape
    tm = 256 if rows % 256 == 0 else rows

    def body(w_ref, g_ref, m_ref, v_ref, d_ref, m2_ref, v2_ref):
        d_ref[...], m2_ref[...], v2_ref[...] = _adamw_math(w_ref[...], g_ref[...], m_ref[...], v_ref[...])

    shape = jax.ShapeDtypeStruct(w.shape, F32)
    return pl.pallas_call(
        body, grid=(rows // tm,), name=name,
        in_specs=[_rows(tm, cols)] * 4, out_specs=[_rows(tm, cols)] * 3, out_shape=[shape] * 3,
        compiler_params=_params(("arbitrary",)),
    )(w, g, m, v)


def _adamw_vectors(g5, ws, ms, vs):
    n = len(ws)

    def body(g_ref, *refs):
        ins, outs = refs[:3 * n], refs[3 * n:]
        for i in range(n):
            res = _adamw_math(ins[i][...], g_ref[i:i + 1, :], ins[n + i][...], ins[2 * n + i][...])
            for kind in range(3):
                outs[kind * n + i][...] = res[kind]

    shape = jax.ShapeDtypeStruct((1, D), F32)
    return pl.pallas_call(body, name="adamw_vectors", out_shape=[shape] * (3 * n), compiler_params=_params())(
        g5, *ws, *ms, *vs)


def kernel(x, norm_g, w_in, conv_w, conv_b, conv_ln_g, conv_ln_b, w_out, final_norm_g, loss_target, m_norm_g, m_w_in, m_conv_w, m_conv_b, m_conv_ln_g, m_conv_ln_b, m_w_out, m_final_norm_g, v_norm_g, v_w_in, v_conv_w, v_conv_b, v_conv_ln_g, v_conv_ln_b, v_w_out, v_final_norm_g):
    chip = 2 * lax.axis_index("x") + lax.axis_index("y")
    where = jnp.stack([chip, lax.axis_index("c")]).astype(jnp.int32)
    taps_shard = jnp.pad(conv_w[0], ((0, HALO - CONV_K), (0, 0)))
    wi_full, wo_full, cw_full = _gather_weights(*_place_shards(w_in[0], w_out[0], taps_shard, where))

    gf = final_norm_g[None]
    grad_x, dw_in4, dw_out, small = _local_step(
        x[0], loss_target[0], norm_g, wi_full, cw_full, conv_b, conv_ln_g, conv_ln_b, wo_full, gf)
    dw_out4 = dw_out.reshape(NCHIP, WOUT_SHARD, D)

    ri, ro, rs = _exchange_halves(dw_in4, dw_out4, small)
    pi, po, ps = _add_halves(dw_in4, ri, dw_out4, ro, small, rs)
    ri, ro, rs = _exchange_chips(pi, po, ps)
    gi2, go2, g_small, g5, loss8 = _sum_chips(ri, ro, rs, pi, po, ps, where)
    gi2, go2 = _exchange_results(gi2, go2)
    g_w_in = gi2.reshape(D, CHUNK)
    g_w_out = go2.reshape(WOUT_SHARD, D)
    g_taps = lax.dynamic_slice(g_small, (ROW_TAPS, chip * CONVW_SHARD), (CONV_K, CONVW_SHARD))

    d_w_in, m2_w_in, v2_w_in = _adamw(w_in[0], g_w_in, m_w_in[0], v_w_in[0], "adamw_w_in")
    d_w_out, m2_w_out, v2_w_out = _adamw(w_out[0], g_w_out, m_w_out[0], v_w_out[0], "adamw_w_out")
    d_taps, m2_taps, v2_taps = _adamw(conv_w[0], g_taps, m_conv_w[0], v_conv_w[0], "adamw_conv_w")
    vec = _adamw_vectors(
        g5,
        (norm_g, conv_b, conv_ln_g, conv_ln_b, gf),
        (m_norm_g, m_conv_b, m_conv_ln_g, m_conv_ln_b, m_final_norm_g[None]),
        (v_norm_g, v_conv_b, v_conv_ln_g, v_conv_ln_b, v_final_norm_g[None]))
    d_vec, m2_vec, v2_vec = vec[0:5], vec[5:10], vec[10:15]

    def weight_order(ng, wi, cw, cb, lg, lb, wo, fg):
        return (ng, wi[None], cw[None], cb, lg, lb, wo[None], fg[0])

    grads = weight_order(g5[0:1], g_w_in, g_taps, g5[1:2], g5[2:3], g5[3:4], g_w_out, g5[4:5])
    deltas = weight_order(d_vec[0], d_w_in, d_taps, d_vec[1], d_vec[2], d_vec[3], d_w_out, d_vec[4])
    new_m = weight_order(m2_vec[0], m2_w_in, m2_taps, m2_vec[1], m2_vec[2], m2_vec[3], m2_w_out, m2_vec[4])
    new_v = weight_order(v2_vec[0], v2_w_in, v2_taps, v2_vec[1], v2_vec[2], v2_vec[3], v2_w_out, v2_vec[4])
    return (loss8[0, 0], grad_x[None], *grads, *deltas, *new_m, *new_v)
```

```python
import jax
import jax.numpy as jnp
from jax import lax
from jax.experimental import pallas as pl
from jax.experimental.pallas import tpu as pltpu

F32 = jnp.float32
BF16 = jnp.bfloat16

S = 4096
D = 1024
LANES = 128
HD = 64
NKV = 4
GQ = 4
KVW = NKV * HD
NCOL = 5632
CONV_K = 31
HALO = 32
BLK = 128
PATTERNS = (1, 4, 16)
NORM_EPS = 1e-6
LN_EPS = 1e-5
NEG = -1e30
OFF_Q, OFF_K, OFF_V, OFF_AG, OFF_CV, OFF_CG, OFF_CGATE = 0, 1024, 1280, 1536, 2560, 3584, 4608
NCHIP = 4
CHUNK = NCOL // NCHIP
WOUT_ROWS = 2 * D
WOUT_SHARD = WOUT_ROWS // NCHIP
CONVW_SHARD = D // NCHIP

ADAM_LR, ADAM_B1, ADAM_B2, ADAM_EPS, ADAM_WD, ADAM_STEP = 0.001, 0.9, 0.999, 1e-08, 0.01, 10

VMEM_LIMIT = 56 * 1024 * 1024


def _params(sem=None, vmem=VMEM_LIMIT):
    return pltpu.CompilerParams(dimension_semantics=sem, vmem_limit_bytes=vmem)


def _sigmoid(a):
    return 0.5 * jnp.tanh(0.5 * a) + 0.5


def _rows(tm, width):
    return pl.BlockSpec((tm, width), lambda i: (i, 0))


def _slabs(n):
    return jax.ShapeDtypeStruct((n, S, LANES), F32)


def _slab_rows(n, tm):
    return pl.BlockSpec((n, tm, LANES), lambda i: (0, i, 0))


def _resident(shape):
    return pl.BlockSpec(shape, lambda *_: (0,) * len(shape), pipeline_mode=pl.Buffered(1))


def _dot(a, b):
    return jnp.dot(a, b, preferred_element_type=F32)


def _dot_nt(a, b):
    return lax.dot_general(a, b, (((1,), (1,)), ((), ())), preferred_element_type=F32)


def _dot_tn(a, b):
    return lax.dot_general(a, b, (((0,), (0,)), ((), ())), preferred_element_type=F32)


def _inproj_fwd(x, g1, w_bf):
    tm = 512

    def body(x_ref, g_ref, w_ref, h_ref, q_ref, k_ref, v_ref, ag_ref, cv_ref, cg_ref, cgate_ref):
        xt = x_ref[...]
        r = lax.rsqrt(jnp.mean(xt * xt, axis=-1, keepdims=True) + NORM_EPS)
        h = (xt * r * g_ref[...]).astype(BF16)
        h_ref[...] = h
        q = _dot(h, w_ref[:, OFF_Q:OFF_Q + D]) * (HD ** -0.5)
        kv = _dot(h, w_ref[:, OFF_K:OFF_K + 2 * KVW])
        for sl in range(D // LANES):
            q_ref[sl] = q[:, sl * LANES:(sl + 1) * LANES]
        for sl in range(KVW // LANES):
            k_ref[sl] = kv[:, sl * LANES:(sl + 1) * LANES]
            v_ref[sl] = kv[:, KVW + sl * LANES:KVW + (sl + 1) * LANES]
        ag_ref[...] = _dot(h, w_ref[:, OFF_AG:OFF_AG + D])
        cv_ref[...] = _dot(h, w_ref[:, OFF_CV:OFF_CV + D])
        cg_ref[...] = _dot(h, w_ref[:, OFF_CG:OFF_CG + D])
        cgate_ref[...] = _dot(h, w_ref[:, OFF_CGATE:OFF_CGATE + D])

    big = jax.ShapeDtypeStruct((S, D), F32)
    return pl.pallas_call(
        body, grid=(S // tm,), name="inproj_fwd",
        in_specs=[_rows(tm, D), _resident((1, D)), _resident((D, NCOL))],
        out_specs=[_rows(tm, D), _slab_rows(D // LANES, tm), _slab_rows(KVW // LANES, tm), _slab_rows(KVW // LANES, tm),
                   _rows(tm, D), _rows(tm, D), _rows(tm, D), _rows(tm, D)],
        out_shape=[jax.ShapeDtypeStruct((S, D), BF16), _slabs(D // LANES), _slabs(KVW // LANES), _slabs(KVW // LANES),
                   big, big, big, big],
        compiler_params=_params(("arbitrary",)),
    )(x, g1, w_bf)


def _bias_table(d):
    h = jnp.arange(NKV * GQ, dtype=F32)
    slopes = jnp.exp2(-8.0 * (h + 1.0) / (NKV * GQ))
    qi = jnp.arange(BLK)[:, None]
    kj = jnp.arange(2 * BLK)[None, :]
    dist = BLK + qi - kj
    window = (dist >= 0) & (dist <= BLK)
    bias = -slopes[:, None, None] * (dist * d).astype(F32)[None]
    has_prev = jnp.stack([jnp.broadcast_to(kj >= BLK, (BLK, 2 * BLK)), jnp.ones((BLK, 2 * BLK), bool)])
    valid = window[None] & has_prev
    tab = jnp.where(valid[:, None], bias[None], NEG)
    return tab.reshape(2, NKV, GQ * BLK, 2 * BLK)


def _sub_rows(start, d):
    if d == 1:
        return pl.ds(pl.multiple_of(start, BLK), BLK)
    return pl.ds(start, BLK, stride=d)


NHEAD = NKV * GQ
CHUNK_ROWS = 2048
BLOCKS_PER_CHUNK = CHUNK_ROWS // BLK


def _low_lanes(rows=BLK):
    return lax.broadcasted_iota(jnp.int32, (rows, LANES), 1) < HD


def _block_start(idx, d):
    shift = d.bit_length() - 1
    b, r = lax.shift_right_logical(idx, shift), lax.bitwise_and(idx, d - 1)
    start = b * (BLK * d) + r
    return b, start, jnp.maximum(start - BLK * d, r)


def _stack_heads(ref, rows):
    low = _low_lanes()
    t0, t1 = ref[0, rows, :], ref[1, rows, :]
    return jnp.concatenate([jnp.where(low, t0, 0.0), jnp.where(low, 0.0, t0),
                            jnp.where(low, t1, 0.0), jnp.where(low, 0.0, t1)], axis=0).astype(BF16)


def _unstack_heads(dup):
    low = _low_lanes()
    return (jnp.where(low, dup[0:BLK], dup[BLK:2 * BLK]), jnp.where(low, dup[2 * BLK:3 * BLK], dup[3 * BLK:4 * BLK]))


def _kv_dup(ref, prow, rows, odd):
    t = jnp.concatenate([ref[0, prow, :], ref[0, rows, :]], axis=0)
    swapped = pltpu.roll(t, HD, axis=1)
    keep = jnp.logical_xor(_low_lanes(2 * BLK), odd)
    return jnp.where(keep, t, swapped).astype(BF16)


def _attn_fwd(q, k, v, bias, d):
    def body(q_ref, k_ref, v_ref, b_ref, o_ref, l_ref):
        odd = pl.program_id(0) % 2 == 1
        ones = jnp.ones((2 * BLK, LANES), BF16)

        def block(idx, carry):
            b, start, pstart = _block_start(idx, d)
            rows, prow = _sub_rows(start, d), _sub_rows(pstart, d)
            qs = _stack_heads(q_ref, rows)
            kw = _kv_dup(k_ref, prow, rows, odd)
            vw = _kv_dup(v_ref, prow, rows, odd)
            s = _dot_nt(qs, kw) + b_ref[jnp.minimum(b, 1), 0]
            m = jnp.max(s, axis=1, keepdims=True)
            p = jnp.exp(s - m).astype(BF16)
            ol = _dot(p, jnp.concatenate([vw, ones], axis=1))
            l = ol[:, LANES:]
            o_ref[0, rows, :], o_ref[1, rows, :] = _unstack_heads(ol[:, :LANES] / l)
            lse = m + jnp.log(l)
            for g in range(GQ):
                l_ref[g, rows, :] = lse[g * BLK:(g + 1) * BLK]
            return carry

        lax.fori_loop(0, S // BLK, block, 0, unroll=2)

    q_like = pl.BlockSpec((2, S, LANES), lambda j: (j, 0, 0))
    kv = pl.BlockSpec((1, S, LANES), lambda j: (j // 2, 0, 0))
    heads = pl.BlockSpec((GQ, S, LANES), lambda j: (j, 0, 0))
    bias_spec = pl.BlockSpec((2, 1, GQ * BLK, 2 * BLK), lambda j: (0, j, 0, 0))
    return pl.pallas_call(
        body, grid=(NKV,), name=f"attn_fwd_d{d}",
        in_specs=[q_like, kv, kv, bias_spec],
        out_specs=[q_like, heads],
        out_shape=[_slabs(D // LANES), _slabs(NHEAD)],
        compiler_params=_params(("arbitrary",)),
    )(q, k, v, bias)


def _attn_combine(outs, lses, a_gate):
    tm = 256

    def body(o1, o2, o3, l1, l2, l3, ag_ref, o_ref, lse_ref, y_ref):
        low = _low_lanes(tm)
        for sl in range(D // LANES):
            w = []
            for h in (2 * sl, 2 * sl + 1):
                a, b, c = l1[h], l2[h], l3[h]
                m = jnp.maximum(jnp.maximum(a, b), c)
                ea, eb, ec = jnp.exp(a - m), jnp.exp(b - m), jnp.exp(c - m)
                den = ea + eb + ec
                lse_ref[h] = m + jnp.log(den)
                inv = 1.0 / den
                w.append((ea * inv, eb * inv, ec * inv))
            wa, wb, wc = (jnp.where(low, w[0][i], w[1][i]) for i in range(3))
            o = wa * o1[sl] + wb * o2[sl] + wc * o3[sl]
            o_ref[sl] = o
            cols = slice(sl * LANES, (sl + 1) * LANES)
            ag = ag_ref[:, cols]
            y_ref[:, cols] = (o * (ag * _sigmoid(ag))).astype(BF16)

    wide, per_head = _slab_rows(D // LANES, tm), _slab_rows(NHEAD, tm)
    return pl.pallas_call(
        body, grid=(S // tm,), name="attn_combine",
        in_specs=[wide] * 3 + [per_head] * 3 + [_rows(tm, D)],
        out_specs=[wide, per_head, _rows(tm, D)],
        out_shape=[_slabs(D // LANES), _slabs(NHEAD), jax.ShapeDtypeStruct((S, D), BF16)],
        compiler_params=_params(("arbitrary",)),
    )(*outs, *lses, a_gate)


def _head_sum_selectors():
    lane_in = jnp.arange(LANES)[:, None] // HD
    return jnp.stack([jnp.broadcast_to(lane_in == h, (LANES, LANES)) for h in range(2)]).astype(BF16)


def _attn_gate_bwd(dy_att, o, a_gate, selectors):
    tm = 256

    def body(dy_ref, o_ref, ag_ref, e_ref, do_ref, dag_ref, delta_ref):
        for sl in range(D // LANES):
            cols = slice(sl * LANES, (sl + 1) * LANES)
            dy, ag, o_ = dy_ref[:, cols], ag_ref[:, cols], o_ref[sl]
            sg = _sigmoid(ag)
            do = dy * (ag * sg)
            do_ref[sl] = do
            dag_ref[:, cols] = dy * o_ * (sg * (1.0 + ag * (1.0 - sg)))
            prod = do * o_
            hi = prod.astype(BF16)
            lo = (prod - hi.astype(F32)).astype(BF16)
            for h in range(2):
                delta_ref[2 * sl + h] = _dot(hi, e_ref[h]) + _dot(lo, e_ref[h])

    return pl.pallas_call(
        body, grid=(S // tm,), name="attn_gate_bwd",
        in_specs=[_rows(tm, D), _slab_rows(D // LANES, tm), _rows(tm, D), _resident((2, LANES, LANES))],
        out_specs=[_slab_rows(D // LANES, tm), _rows(tm, D), _slab_rows(NHEAD, tm)],
        out_shape=[_slabs(D // LANES), jax.ShapeDtypeStruct((S, D), F32), _slabs(NHEAD)],
        compiler_params=_params(("arbitrary",)),
    )(dy_att, o, a_gate, selectors)


def _attn_bwd(q, k, v, do, lse, delta, bias, d):
    def body(q_ref, do_ref, l_ref, dl_ref, k_ref, v_ref, b_ref, dq_ref, dkv_ref):
        odd = pl.program_id(0) % 2 == 1
        chunk = pl.program_id(1)

        @pl.when(chunk == 0)
        def _():
            dkv_ref[...] = jnp.zeros_like(dkv_ref)

        def block(idx, carry):
            b, start, pstart = _block_start(chunk * BLOCKS_PER_CHUNK + idx, d)
            rows, prow = _sub_rows(start, d), _sub_rows(pstart, d)
            mine = _sub_rows(start - chunk * CHUNK_ROWS, d)
            qs = _stack_heads(q_ref, mine)
            dos = _stack_heads(do_ref, mine)
            lse_t = jnp.concatenate([l_ref[g, mine, :] for g in range(GQ)], axis=0)
            delta_t = jnp.concatenate([dl_ref[g, mine, :] for g in range(GQ)], axis=0)
            kw = _kv_dup(k_ref, prow, rows, odd)
            vw = _kv_dup(v_ref, prow, rows, odd)
            s = _dot_nt(qs, kw) + b_ref[jnp.minimum(b, 1), 0]
            p = jnp.exp(s - jnp.concatenate([lse_t, lse_t], axis=1))
            dv2 = _dot_tn(p.astype(BF16), dos)
            dp = _dot_nt(dos, vw)
            ds = (p * (dp - jnp.concatenate([delta_t, delta_t], axis=1))).astype(BF16)
            dq_ref[0, mine, :], dq_ref[1, mine, :] = _unstack_heads(_dot(ds, kw))
            dk2 = _dot_tn(ds, qs)
            dkv = jnp.where(_low_lanes(2 * BLK), dk2 + pltpu.roll(dk2, HD, axis=1), dv2 + pltpu.roll(dv2, HD, axis=1))
            dkv_ref[0, rows, :] = dkv_ref[0, rows, :] + dkv[BLK:]
            dkv_ref[0, prow, :] = dkv_ref[0, prow, :] + dkv[:BLK]
            return carry

        lax.fori_loop(0, BLOCKS_PER_CHUNK, block, 0, unroll=8)

    q_like = pl.BlockSpec((2, CHUNK_ROWS, LANES), lambda j, c: (j, c, 0))
    heads = pl.BlockSpec((GQ, CHUNK_ROWS, LANES), lambda j, c: (j, c, 0))
    kv = pl.BlockSpec((1, S, LANES), lambda j, c: (j // 2, 0, 0))
    per_kv = pl.BlockSpec((1, S, LANES), lambda j, c: (j, 0, 0))
    bias_spec = pl.BlockSpec((2, 1, GQ * BLK, 2 * BLK), lambda j, c: (0, j, 0, 0))
    return pl.pallas_call(
        body, grid=(NKV, S // CHUNK_ROWS), name=f"attn_bwd_d{d}",
        in_specs=[q_like, q_like, heads, heads, kv, kv, bias_spec],
        out_specs=[q_like, per_kv],
        out_shape=[_slabs(D // LANES), _slabs(NKV)],
        compiler_params=_params(("arbitrary", "arbitrary")),
    )(q, do, lse, delta, k, v, bias)


CONV_T = 128


def _halo_before(i):
    return (jnp.maximum(i * (CONV_T // HALO) - 1, 0), 0)


def _halo_after(i):
    return (jnp.minimum((i + 1) * (CONV_T // HALO), S // HALO - 1), 0)


SUBLANES = 8
NCH = D // LANES
GROUP = SUBLANES * SUBLANES


def _comb(ref, cb, base):
    return ref[cb, pl.ds(base, SUBLANES, stride=SUBLANES), :]


def _taps(w_ref, cols):
    return [jnp.broadcast_to(w_ref[j:j + 1, cols], (SUBLANES, LANES)) for j in range(CONV_K)]


def _conv_fwd(c_val, c_glu, c_gate, conv_w, conv_b, ln_g, ln_b):
    T = CONV_T

    def body(cv_ref, cg_ref, cvh_ref, cgh_ref, gate_ref, w_ref, b_ref, lg_ref, lb_ref, u_ref, y_ref, win, us):
        i = pl.program_id(0)
        for cb in range(NCH):
            cols = slice(cb * LANES, (cb + 1) * LANES)
            win[cb, HALO:HALO + T, :] = cv_ref[:, cols] * _sigmoid(cg_ref[:, cols])
            win[cb, 0:HALO, :] = jnp.where(i > 0, cvh_ref[:, cols] * _sigmoid(cgh_ref[:, cols]), 0.0)
        for cb in range(NCH):
            cols = slice(cb * LANES, (cb + 1) * LANES)
            taps = _taps(w_ref, cols)
            bias = jnp.broadcast_to(b_ref[:, cols], (SUBLANES, LANES))

            def group(g, carry):
                for b in range(SUBLANES):
                    base = g * GROUP + b
                    acc = bias
                    for j in range(CONV_K):
                        acc = acc + taps[j] * _comb(win, cb, base + (HALO - (CONV_K - 1) + j))
                    us[cb, pl.ds(base, SUBLANES, stride=SUBLANES), :] = acc
                return carry

            lax.fori_loop(0, T // GROUP, group, 0)
        total = us[0]
        for cb in range(1, NCH):
            total = total + us[cb]
        mu = jnp.sum(total, axis=-1, keepdims=True) * (1.0 / D)
        sq = jnp.zeros((T, LANES), F32)
        for cb in range(NCH):
            uc = us[cb] - mu
            sq = sq + uc * uc
        rstd = lax.rsqrt(jnp.sum(sq, axis=-1, keepdims=True) * (1.0 / D) + LN_EPS)
        for cb in range(NCH):
            cols = slice(cb * LANES, (cb + 1) * LANES)
            u = us[cb]
            u_ref[:, cols] = u
            nrm = (u - mu) * rstd * lg_ref[:, cols] + lb_ref[:, cols]
            gate = gate_ref[:, cols]
            y_ref[:, cols] = (nrm * _sigmoid(nrm) * (gate * _sigmoid(gate))).astype(BF16)

    halo = pl.BlockSpec((HALO, D), _halo_before)
    return pl.pallas_call(
        body, grid=(S // T,), name="conv_fwd",
        in_specs=[_rows(T, D), _rows(T, D), halo, halo, _rows(T, D),
                  _resident((HALO, D)), _resident((1, D)), _resident((1, D)), _resident((1, D))],
        out_specs=[_rows(T, D), _rows(T, D)],
        out_shape=[jax.ShapeDtypeStruct((S, D), F32), jax.ShapeDtypeStruct((S, D), BF16)],
        scratch_shapes=[pltpu.VMEM((NCH, T + HALO, LANES), F32), pltpu.VMEM((NCH, T, LANES), F32)],
        compiler_params=_params(("arbitrary",)),
    )(c_val, c_glu, c_val, c_glu, c_gate, conv_w, conv_b, ln_g, ln_b)


def _conv_bwd_rows(u, c_gate, dy_conv, ln_g, ln_b):
    tm = 256

    def body(u_ref, gate_ref, dy_ref, lg_ref, lb_ref, du_ref, dgate_ref, st_ref):
        @pl.when(pl.program_id(0) == 0)
        def _():
            st_ref[...] = jnp.zeros_like(st_ref)

        u, gate, dy = u_ref[...], gate_ref[...], dy_ref[...]
        mu = jnp.mean(u, axis=-1, keepdims=True)
        uc = u - mu
        rstd = lax.rsqrt(jnp.mean(uc * uc, axis=-1, keepdims=True) + LN_EPS)
        z = uc * rstd
        nrm = z * lg_ref[...] + lb_ref[...]
        sn, sg = _sigmoid(nrm), _sigmoid(gate)
        dgate_ref[...] = dy * (nrm * sn) * (sg * (1.0 + gate * (1.0 - sg)))
        dn = dy * (gate * sg) * (sn * (1.0 + nrm * (1.0 - sn)))
        dz = dn * lg_ref[...]
        du = rstd * (dz - jnp.mean(dz, axis=-1, keepdims=True) - z * jnp.mean(dz * z, axis=-1, keepdims=True))
        du_ref[...] = du
        st_ref[0:1, :] += jnp.sum(dn * z, axis=0, keepdims=True)
        st_ref[1:2, :] += jnp.sum(dn, axis=0, keepdims=True)
        st_ref[2:3, :] += jnp.sum(du, axis=0, keepdims=True)

    big = jax.ShapeDtypeStruct((S, D), F32)
    return pl.pallas_call(
        body, grid=(S // tm,), name="conv_bwd_rows",
        in_specs=[_rows(tm, D)] * 3 + [_resident((1, D)), _resident((1, D))],
        out_specs=[_rows(tm, D), _rows(tm, D), pl.BlockSpec((8, D), lambda i: (0, 0))],
        out_shape=[big, big, jax.ShapeDtypeStruct((8, D), F32)],
        compiler_params=_params(("arbitrary",)),
    )(u, c_gate, dy_conv, ln_g, ln_b)


def _conv_bwd_taps(du, c_val, c_glu, conv_w):
    T = CONV_T
    last = S // T - 1

    def body(du_ref, dua_ref, cv_ref, cg_ref, cvh_ref, cgh_ref, w_ref, dcv_ref, dcg_ref, dw_ref,
             hwin, dwin, dhs, dw_acc):
        i = pl.program_id(0)

        @pl.when(i == 0)
        def _():
            dw_acc[...] = jnp.zeros_like(dw_acc)

        for cb in range(NCH):
            cols = slice(cb * LANES, (cb + 1) * LANES)
            hwin[cb, HALO:HALO + T, :] = cv_ref[:, cols] * _sigmoid(cg_ref[:, cols])
            hwin[cb, 0:HALO, :] = jnp.where(i > 0, cvh_ref[:, cols] * _sigmoid(cgh_ref[:, cols]), 0.0)
            dwin[cb, 0:T, :] = du_ref[:, cols]
            dwin[cb, T:T + HALO, :] = jnp.where(i < last, dua_ref[:, cols], 0.0)
        for cb in range(NCH):
            cols = slice(cb * LANES, (cb + 1) * LANES)
            taps = _taps(w_ref, cols)

            def group_dh(g, carry):
                for b in range(SUBLANES):
                    base = g * GROUP + b
                    acc = jnp.zeros((SUBLANES, LANES), F32)
                    for j in range(CONV_K):
                        acc = acc + taps[j] * _comb(dwin, cb, base + (CONV_K - 1 - j))
                    dhs[cb, pl.ds(base, SUBLANES, stride=SUBLANES), :] = acc
                return carry

            lax.fori_loop(0, T // GROUP, group_dh, 0)

            def group_dw(g, sums):
                for b in range(SUBLANES):
                    base = g * GROUP + b
                    d = _comb(dwin, cb, base)
                    sums = tuple(sums[j] + d * _comb(hwin, cb, base + (HALO - (CONV_K - 1) + j))
                                 for j in range(CONV_K))
                return sums

            sums = lax.fori_loop(0, T // GROUP, group_dw, tuple(dw_acc[j, :, cols] for j in range(CONV_K)))
            for j in range(CONV_K):
                dw_acc[j, :, cols] = sums[j]
            dh = dhs[cb]
            cv, sg = cv_ref[:, cols], _sigmoid(cg_ref[:, cols])
            dcv_ref[:, cols] = dh * sg
            dcg_ref[:, cols] = dh * cv * (sg * (1.0 - sg))

        @pl.when(i == last)
        def _():
            dw_ref[...] = jnp.zeros_like(dw_ref)
            for j in range(CONV_K):
                dw_ref[j:j + 1, :] = jnp.sum(dw_acc[j], axis=0, keepdims=True)

    before = pl.BlockSpec((HALO, D), _halo_before)
    after = pl.BlockSpec((HALO, D), _halo_after)
    big = jax.ShapeDtypeStruct((S, D), F32)
    return pl.pallas_call(
        body, grid=(S // T,), name="conv_bwd_taps",
        in_specs=[_rows(T, D), after, _rows(T, D), _rows(T, D), before, before, _resident((HALO, D))],
        out_specs=[_rows(T, D), _rows(T, D), pl.BlockSpec((HALO, D), lambda i: (0, 0))],
        out_shape=[big, big, jax.ShapeDtypeStruct((HALO, D), F32)],
        scratch_shapes=[pltpu.VMEM((NCH, T + HALO, LANES), F32), pltpu.VMEM((NCH, T + HALO, LANES), F32),
                        pltpu.VMEM((NCH, T, LANES), F32), pltpu.VMEM((CONV_K, SUBLANES, D), F32)],
        compiler_params=_params(("arbitrary",)),
    )(du, du, c_val, c_glu, c_val, c_glu, conv_w)


def _outproj_loss(y_att, y_conv, w_out_bf, x, target, gf):
    tm = 256

    def body(ya_ref, yc_ref, w_ref, x_ref, t_ref, gf_ref, dx2_ref, dya_ref, dyc_ref, dw_ref, st_ref, acc):
        @pl.when(pl.program_id(0) == 0)
        def _():
            acc[...] = jnp.zeros_like(acc)
            st_ref[...] = jnp.zeros_like(st_ref)

        ya, yc = ya_ref[...], yc_ref[...]
        x2 = x_ref[...] + _dot(ya, w_ref[0:D, :]) + _dot(yc, w_ref[D:2 * D, :])
        r = lax.rsqrt(jnp.mean(x2 * x2, axis=-1, keepdims=True) + NORM_EPS)
        xn = x2 * r
        err = xn * gf_ref[...] - t_ref[...]
        dout = err * (1.0 / D)
        dxn = dout * gf_ref[...]
        dx2 = r * (dxn - xn * jnp.mean(dxn * xn, axis=-1, keepdims=True))
        dx2_ref[...] = dx2
        dx2b = dx2.astype(BF16)
        dya_ref[...] = _dot_nt(dx2b, w_ref[0:D, :])
        dyc_ref[...] = _dot_nt(dx2b, w_ref[D:2 * D, :])
        acc[0:D, :] += _dot_tn(ya, dx2b)
        acc[D:2 * D, :] += _dot_tn(yc, dx2b)
        st_ref[0:1, :] += jnp.sum(dout * xn, axis=0, keepdims=True)
        st_ref[1:2, :] += jnp.sum(err * err, axis=0, keepdims=True) * (0.5 / D)

        @pl.when(pl.program_id(0) == S // tm - 1)
        def _():
            dw_ref[...] = acc[...].astype(BF16)

    big = jax.ShapeDtypeStruct((S, D), F32)
    return pl.pallas_call(
        body, grid=(S // tm,), name="outproj_loss",
        in_specs=[_rows(tm, D), _rows(tm, D), _resident((WOUT_ROWS, D)), _rows(tm, D), _rows(tm, D), _resident((1, D))],
        out_specs=[_rows(tm, D), _rows(tm, D), _rows(tm, D),
                   pl.BlockSpec((WOUT_ROWS, D), lambda i: (0, 0)), pl.BlockSpec((8, D), lambda i: (0, 0))],
        out_shape=[big, big, big, jax.ShapeDtypeStruct((WOUT_ROWS, D), BF16), jax.ShapeDtypeStruct((8, D), F32)],
        scratch_shapes=[pltpu.VMEM((WOUT_ROWS, D), F32)],
        compiler_params=_params(("arbitrary",)),
    )(y_att, y_conv, w_out_bf, x, target, gf)


def _inproj_bwd_x(dqs, dkvs, dag, dcv, dcg, dcgate, w_bf, x, g1, dx2):
    tm = 256

    def body(dq1, dq2, dq3, dkv1, dkv2, dkv3, dag_ref, dcv_ref, dcg_ref, dcgate_ref,
             w_ref, x_ref, g_ref, dx2_ref, dp_ref, gx_ref, st_ref):
        @pl.when(pl.program_id(0) == 0)
        def _():
            st_ref[...] = jnp.zeros_like(st_ref)

        for sl in range(D // LANES):
            dq = (dq1[sl] + dq2[sl] + dq3[sl]) * (HD ** -0.5)
            dp_ref[:, OFF_Q + sl * LANES:OFF_Q + (sl + 1) * LANES] = dq.astype(BF16)
        for j in range(NKV):
            dkv = (dkv1[j] + dkv2[j] + dkv3[j]).astype(BF16)
            dp_ref[:, OFF_K + j * HD:OFF_K + (j + 1) * HD] = dkv[:, :HD]
            dp_ref[:, OFF_V + j * HD:OFF_V + (j + 1) * HD] = dkv[:, HD:]
        for off, ref in ((OFF_AG, dag_ref), (OFF_CV, dcv_ref), (OFF_CG, dcg_ref), (OFF_CGATE, dcgate_ref)):
            dp_ref[:, off:off + D] = ref[...].astype(BF16)
        dh = jnp.zeros((tm, D), F32)
        for off, width in ((OFF_Q, D), (OFF_K, 2 * KVW), (OFF_AG, D), (OFF_CV, D), (OFF_CG, D), (OFF_CGATE, D)):
            dh = dh + _dot_nt(dp_ref[:, off:off + width], w_ref[:, off:off + width])
        xt = x_ref[...]
        r = lax.rsqrt(jnp.mean(xt * xt, axis=-1, keepdims=True) + NORM_EPS)
        xn = xt * r
        dxn = dh * g_ref[...]
        gx_ref[...] = dx2_ref[...] + r * (dxn - xn * jnp.mean(dxn * xn, axis=-1, keepdims=True))
        st_ref[0:1, :] += jnp.sum(dh * xn, axis=0, keepdims=True)

    return pl.pallas_call(
        body, grid=(S // tm,), name="inproj_bwd_x",
        in_specs=[_slab_rows(D // LANES, tm)] * 3 + [_slab_rows(NKV, tm)] * 3 + [_rows(tm, D)] * 4
        + [_resident((D, NCOL)), _rows(tm, D), _resident((1, D)), _rows(tm, D)],
        out_specs=[_rows(tm, NCOL), _rows(tm, D), pl.BlockSpec((8, D), lambda i: (0, 0))],
        out_shape=[jax.ShapeDtypeStruct((S, NCOL), BF16), jax.ShapeDtypeStruct((S, D), F32),
                   jax.ShapeDtypeStruct((8, D), F32)],
        compiler_params=_params(("arbitrary",)),
    )(*dqs, *dkvs, dag, dcv, dcg, dcgate, w_bf, x, g1, dx2)


def _inproj_bwd_w(h, dproj):
    tk = 1024
    nk = S // tk

    def body(h_ref, dp_ref, o_ref, acc):
        i = pl.program_id(1)

        @pl.when(i == 0)
        def _():
            acc[...] = jnp.zeros_like(acc)

        acc[...] += _dot_tn(h_ref[...], dp_ref[...])

        @pl.when(i == nk - 1)
        def _():
            o_ref[0] = acc[...].astype(BF16)

    return pl.pallas_call(
        body, grid=(NCHIP, nk), name="inproj_bwd_w",
        in_specs=[pl.BlockSpec((tk, D), lambda c, i: (i, 0)), pl.BlockSpec((tk, CHUNK), lambda c, i: (i, c))],
        out_specs=pl.BlockSpec((1, D, CHUNK), lambda c, i: (c, 0, 0)),
        out_shape=jax.ShapeDtypeStruct((NCHIP, D, CHUNK), BF16),
        scratch_shapes=[pltpu.VMEM((D, CHUNK), F32)],
        compiler_params=_params(("arbitrary", "arbitrary")),
    )(h, dproj)


def _local_step(x, target, g1, w_in_bf, conv_w, conv_b, ln_g, ln_b, w_out_bf, gf):
    h, q, k, v, a_gate, c_val, c_glu, c_gate = _inproj_fwd(x, g1, w_in_bf)
    tables = [_bias_table(d) for d in PATTERNS]
    outs, lses = zip(*[_attn_fwd(q, k, v, t, d) for t, d in zip(tables, PATTERNS)])
    o, lse, y_att = _attn_combine(outs, lses, a_gate)
    u, y_conv = _conv_fwd(c_val, c_glu, c_gate, conv_w, conv_b, ln_g, ln_b)
    dx2, dy_att, dy_conv, dw_out, st_out = _outproj_loss(y_att, y_conv, w_out_bf, x, target, gf)

    do, da_gate, delta = _attn_gate_bwd(dy_att, o, a_gate, _head_sum_selectors())
    dqs, dkvs = zip(*[_attn_bwd(q, k, v, do, lse, delta, t, d) for t, d in zip(tables, PATTERNS)])

    du, dc_gate, st_conv = _conv_bwd_rows(u, c_gate, dy_conv, ln_g, ln_b)
    dc_val, dc_glu, dconv_w = _conv_bwd_taps(du, c_val, c_glu, conv_w)

    dproj, grad_x, st_in = _inproj_bwd_x(dqs, dkvs, da_gate, dc_val, dc_glu, dc_gate, w_in_bf, x, g1, dx2)
    dw_in = _inproj_bwd_w(h, dproj)
    small = jnp.concatenate([st_in, st_conv, st_out, dconv_w], axis=0)
    return grad_x, dw_in, dw_out, small


ROW_NORM_G, ROW_LN_G, ROW_LN_B, ROW_CONV_B, ROW_FINAL_G, ROW_LOSS, ROW_TAPS = 0, 8, 9, 10, 16, 17, 24
SMALL_ROWS = 24 + HALO


MESH = pl.DeviceIdType.MESH
ANY = pl.BlockSpec(memory_space=pl.ANY)
CHIP_FLIPS = ((1, 0), (0, 1), (1, 1))


def _pos():
    return lax.axis_index("x"), lax.axis_index("y"), lax.axis_index("c")


def _flip(v, f):
    return 1 - v if f else v


def _ds(start, size, align=None):
    return pl.ds(pl.multiple_of(start, align or size), size)


def _place_shards(wi, wo, cw, where):
    steps = 4

    def body(where_ref, wi_ref, wo_ref, cw_ref, wi_full, wo_full, cw_full):
        wi_full[...] = wi_ref[...].astype(BF16)
        wo_full[...] = wo_ref[...].astype(BF16)
        cw_full[...] = cw_ref[...]

    grid_spec = pltpu.PrefetchScalarGridSpec(
        num_scalar_prefetch=1, grid=(steps,),
        in_specs=[pl.BlockSpec((D // steps, CHUNK), lambda i, w: (i, 0)),
                  pl.BlockSpec((WOUT_SHARD // steps, D), lambda i, w: (i, 0)),
                  pl.BlockSpec((HALO, CONVW_SHARD), lambda i, w: (0, 0))],
        out_specs=[pl.BlockSpec((D // steps, CHUNK), lambda i, w: (i, w[0])),
                   pl.BlockSpec((WOUT_SHARD // steps, D), lambda i, w: (w[0] * steps + i, 0)),
                   pl.BlockSpec((HALO, CONVW_SHARD), lambda i, w: (0, w[0]))])
    return pl.pallas_call(
        body, grid_spec=grid_spec, name="place_shards",
        out_shape=[jax.ShapeDtypeStruct((D, NCOL), BF16), jax.ShapeDtypeStruct((WOUT_ROWS, D), BF16),
                   jax.ShapeDtypeStruct((HALO, D), F32)],
        compiler_params=_params(("arbitrary",)),
    )(where, wi, wo, cw)


def _gather_weights(wi_full, wo_full, cw_full):
    halves = (D // 2, WOUT_SHARD // 2, HALO // 2)
    n_ici = 3 * len(CHIP_FLIPS)

    def body(_wi, _wo, _cw, wi_full, wo_full, cw_full, send, recv):
        x, y, c = _pos()

        def region(a, px, py, half):
            chip = 2 * px + py
            n = halves[a]
            if a == 0:
                return wi_full.at[_ds(half * n, n), _ds(chip * CHUNK, CHUNK, 128)]
            if a == 1:
                return wo_full.at[_ds(chip * WOUT_SHARD + half * n, n), :]
            return cw_full.at[_ds(half * n, n), _ds(chip * CONVW_SHARD, CONVW_SHARD, 128)]

        def remote(k, src, dst, dev):
            return pltpu.make_async_remote_copy(src_ref=src, dst_ref=dst, send_sem=send.at[k], recv_sem=recv.at[k],
                                                device_id=dev, device_id_type=MESH)

        sends = []
        for a in range(3):
            for j, (fx, fy) in enumerate(CHIP_FLIPS):
                mine = region(a, x, y, c)
                cp = remote(3 * a + j, mine, mine, (_flip(x, fx), _flip(y, fy), c))
                cp.start()
                sends.append(cp)
        for a in range(3):
            for j, (fx, fy) in enumerate(CHIP_FLIPS):
                px, py = _flip(x, fx), _flip(y, fy)
                got = region(a, px, py, c)
                remote(3 * a + j, got, got, (px, py, c)).wait_recv()
                cp = remote(n_ici + 3 * a + j, got, got, (x, y, 1 - c))
                cp.start()
                sends.append(cp)
        for a in range(3):
            for j, (fx, fy) in enumerate(CHIP_FLIPS):
                got = region(a, _flip(x, fx), _flip(y, fy), 1 - c)
                remote(n_ici + 3 * a + j, got, got, (x, y, 1 - c)).wait_recv()
        for cp in sends:
            cp.wait_send()

    return pl.pallas_call(
        body, name="gather_weights",
        in_specs=[ANY, ANY, ANY], out_specs=[ANY, ANY, ANY], input_output_aliases={0: 0, 1: 1, 2: 2},
        out_shape=[jax.ShapeDtypeStruct((D, NCOL), BF16), jax.ShapeDtypeStruct((WOUT_ROWS, D), BF16),
                   jax.ShapeDtypeStruct((HALO, D), F32)],
        scratch_shapes=[pltpu.SemaphoreType.DMA((2 * n_ici,)), pltpu.SemaphoreType.DMA((2 * n_ici,))],
    )(wi_full, wo_full, cw_full)


def _exchange_halves(gi4, go4, small):
    def body(gi_ref, go_ref, sm_ref, ri_ref, ro_ref, rs_ref, send, recv):
        x, y, c = _pos()
        sib = (x, y, 1 - c)
        copies = [
            (gi_ref.at[:, _ds((1 - c) * (D // 2), D // 2), :], ri_ref),
            (go_ref.at[:, _ds((1 - c) * (WOUT_SHARD // 2), WOUT_SHARD // 2), :], ro_ref),
            (sm_ref, rs_ref),
        ]
        cps = [pltpu.make_async_remote_copy(src_ref=s_, dst_ref=d_, send_sem=send.at[k], recv_sem=recv.at[k],
                                            device_id=sib, device_id_type=MESH) for k, (s_, d_) in enumerate(copies)]
        for cp in cps:
            cp.start()
        for cp in cps:
            cp.wait()

    return pl.pallas_call(
        body, name="exchange_halves",
        in_specs=[ANY, ANY, ANY], out_specs=[ANY, ANY, ANY],
        out_shape=[jax.ShapeDtypeStruct((NCHIP, D // 2, CHUNK), BF16),
                   jax.ShapeDtypeStruct((NCHIP, WOUT_SHARD // 2, D), BF16),
                   jax.ShapeDtypeStruct((SMALL_ROWS, D), F32)],
        scratch_shapes=[pltpu.SemaphoreType.DMA((3,)), pltpu.SemaphoreType.DMA((3,))],
    )(gi4, go4, small)


def _add_halves(gi4, ri, go4, ro, small, rs):
    hi, ho = D // 2, WOUT_SHARD // 2

    def body(gi_ref, ri_ref, go_ref, ro_ref, sm_ref, rs_ref, pi_ref, po_ref, ps_ref):
        c = lax.axis_index("c")
        pi_ref[0] = (gi_ref[0, _ds(c * hi, hi), :].astype(F32) + ri_ref[0].astype(F32)).astype(BF16)
        po_ref[0] = (go_ref[0, _ds(c * ho, ho), :].astype(F32) + ro_ref[0].astype(F32)).astype(BF16)
        ps_ref[...] = sm_ref[...] + rs_ref[...]

    blk = lambda n, w: pl.BlockSpec((1, n, w), lambda k: (k, 0, 0))
    whole = pl.BlockSpec((SMALL_ROWS, D), lambda k: (0, 0))
    return pl.pallas_call(
        body, grid=(NCHIP,), name="add_halves",
        in_specs=[blk(D, CHUNK), blk(hi, CHUNK), blk(WOUT_SHARD, D), blk(ho, D), whole, whole],
        out_specs=[blk(hi, CHUNK), blk(ho, D), whole],
        out_shape=[jax.ShapeDtypeStruct((NCHIP, hi, CHUNK), BF16), jax.ShapeDtypeStruct((NCHIP, ho, D), BF16),
                   jax.ShapeDtypeStruct((SMALL_ROWS, D), F32)],
        compiler_params=_params(("arbitrary",)),
    )(gi4, ri, go4, ro, small, rs)


def _exchange_chips(pi, po, ps):
    def body(pi_ref, po_ref, ps_ref, ri_ref, ro_ref, rs_ref, send, recv):
        x, y, c = _pos()
        me = 2 * x + y
        srcs = (pi_ref, po_ref, ps_ref)
        dsts = (ri_ref, ro_ref, rs_ref)

        def piece(a, chip):
            return srcs[a] if a == 2 else srcs[a].at[chip]

        sends = []
        for a in range(3):
            for j, (fx, fy) in enumerate(CHIP_FLIPS):
                px, py = _flip(x, fx), _flip(y, fy)
                cp = pltpu.make_async_remote_copy(
                    src_ref=piece(a, 2 * px + py), dst_ref=dsts[a].at[me], send_sem=send.at[3 * a + j],
                    recv_sem=recv.at[3 * a + j], device_id=(px, py, c), device_id_type=MESH)
                cp.start()
                sends.append(cp)
        for a in range(3):
            for j, (fx, fy) in enumerate(CHIP_FLIPS):
                px, py = _flip(x, fx), _flip(y, fy)
                got = dsts[a].at[2 * px + py]
                pltpu.make_async_remote_copy(src_ref=got, dst_ref=got, send_sem=send.at[3 * a + j],
                                             recv_sem=recv.at[3 * a + j], device_id=(px, py, c),
                                             device_id_type=MESH).wait_recv()
        for cp in sends:
            cp.wait_send()

    return pl.pallas_call(
        body, name="exchange_chips",
        in_specs=[ANY, ANY, ANY], out_specs=[ANY, ANY, ANY],
        out_shape=[jax.ShapeDtypeStruct((NCHIP, D // 2, CHUNK), BF16),
                   jax.ShapeDtypeStruct((NCHIP, WOUT_SHARD // 2, D), BF16),
                   jax.ShapeDtypeStruct((NCHIP, SMALL_ROWS, D), F32)],
        scratch_shapes=[pltpu.SemaphoreType.DMA((9,)), pltpu.SemaphoreType.DMA((9,))],
    )(pi, po, ps)


def _sum_chips(ri, ro, rs, pi, po, ps, where):
    def body(w_ref, ri_ref, ro_ref, rs_ref, pi_ref, po_ref, ps_ref, gi_ref, go_ref, gs_ref, g5_ref, loss_ref,
             acc_i, acc_o, acc_s):
        k = pl.program_id(0)
        accs = (acc_i, acc_o, acc_s)

        @pl.when(k == 0)
        def _():
            for acc in accs:
                acc[...] = jnp.zeros_like(acc)

        @pl.when(k == w_ref[0])
        def _():
            for acc, val in zip(accs, (pi_ref[0], po_ref[0], ps_ref[...])):
                acc[...] += val.astype(F32)

        @pl.when(k != w_ref[0])
        def _():
            for acc, ref in zip(accs, (ri_ref, ro_ref, rs_ref)):
                acc[...] += ref[0].astype(F32)

        @pl.when(k == NCHIP - 1)
        def _():
            gi_ref[0] = acc_i[...]
            go_ref[0] = acc_o[...]
            gs_ref[...] = acc_s[...]
            g5_ref[...] = jnp.zeros_like(g5_ref)
            for i, row in enumerate((ROW_NORM_G, ROW_CONV_B, ROW_LN_G, ROW_LN_B, ROW_FINAL_G)):
                g5_ref[i:i + 1, :] = acc_s[row:row + 1, :]
            loss = jnp.sum(acc_s[ROW_LOSS:ROW_LOSS + 1, :], axis=1, keepdims=True)
            loss_ref[...] = jnp.broadcast_to(loss, loss_ref.shape)

    def sent(k, w):
        return jnp.where(k == w[0], (k + 1) % NCHIP, k)

    hi, ho = D // 2, WOUT_SHARD // 2
    const = lambda shape: pl.BlockSpec(shape, lambda k, w: (0,) * len(shape))
    grid_spec = pltpu.PrefetchScalarGridSpec(
        num_scalar_prefetch=1, grid=(NCHIP,),
        in_specs=[pl.BlockSpec((1, hi, CHUNK), lambda k, w: (sent(k, w), 0, 0)),
                  pl.BlockSpec((1, ho, D), lambda k, w: (sent(k, w), 0, 0)),
                  pl.BlockSpec((1, SMALL_ROWS, D), lambda k, w: (sent(k, w), 0, 0)),
                  pl.BlockSpec((1, hi, CHUNK), lambda k, w: (w[0], 0, 0)),
                  pl.BlockSpec((1, ho, D), lambda k, w: (w[0], 0, 0)),
                  const((SMALL_ROWS, D))],
        out_specs=[pl.BlockSpec((1, hi, CHUNK), lambda k, w: (w[1], 0, 0)),
                   pl.BlockSpec((1, ho, D), lambda k, w: (w[1], 0, 0)),
                   const((SMALL_ROWS, D)), const((8, D)), const((8, LANES))],
        scratch_shapes=[pltpu.VMEM((hi, CHUNK), F32), pltpu.VMEM((ho, D), F32), pltpu.VMEM((SMALL_ROWS, D), F32)])
    return pl.pallas_call(
        body, grid_spec=grid_spec, name="sum_chips",
        out_shape=[jax.ShapeDtypeStruct((2, hi, CHUNK), F32), jax.ShapeDtypeStruct((2, ho, D), F32),
                   jax.ShapeDtypeStruct((SMALL_ROWS, D), F32), jax.ShapeDtypeStruct((8, D), F32),
                   jax.ShapeDtypeStruct((8, LANES), F32)],
        compiler_params=_params(("arbitrary",)),
    )(where, ri, ro, rs, pi, po, ps)


def _exchange_results(gi2, go2):
    def body(_gi, _go, gi_ref, go_ref, send, recv):
        x, y, c = _pos()
        sib = (x, y, 1 - c)

        def copy(k, ref, slot):
            return pltpu.make_async_remote_copy(src_ref=ref.at[slot], dst_ref=ref.at[slot], send_sem=send.at[k],
                                                recv_sem=recv.at[k], device_id=sib, device_id_type=MESH)

        sends = [copy(k, ref, c) for k, ref in enumerate((gi_ref, go_ref))]
        for cp in sends:
            cp.start()
        for k, ref in enumerate((gi_ref, go_ref)):
            copy(k, ref, 1 - c).wait_recv()
        for cp in sends:
            cp.wait_send()

    return pl.pallas_call(
        body, name="exchange_results",
        in_specs=[ANY, ANY], out_specs=[ANY, ANY], input_output_aliases={0: 0, 1: 1},
        out_shape=[jax.ShapeDtypeStruct((2, D // 2, CHUNK), F32), jax.ShapeDtypeStruct((2, WOUT_SHARD // 2, D), F32)],
        scratch_shapes=[pltpu.SemaphoreType.DMA((2,)), pltpu.SemaphoreType.DMA((2,))],
    )(gi2, go2)


def _adamw_math(w, g, m, v):
    m2 = ADAM_B1 * m + (1.0 - ADAM_B1) * g
    v2 = ADAM_B2 * v + (1.0 - ADAM_B2) * (g * g)
    m_hat = m2 / (1.0 - ADAM_B1 ** ADAM_STEP)
    v_hat = v2 / (1.0 - ADAM_B2 ** ADAM_STEP)
    delta = -ADAM_LR * (m_hat / (jnp.sqrt(v_hat) + ADAM_EPS) + ADAM_WD * w)
    return delta, m2, v2


def _adamw(w, g, m, v, name):
    rows, cols = w.shape
    tm = 256 if rows % 256 == 0 else rows

    def body(w_ref, g_ref, m_ref, v_ref, d_ref, m2_ref, v2_ref):
        d_ref[...], m2_ref[...], v2_ref[...] = _adamw_math(w_ref[...], g_ref[...], m_ref[...], v_ref[...])

    shape = jax.ShapeDtypeStruct(w.shape, F32)
    return pl.pallas_call(
        body, grid=(rows // tm,), name=name,
        in_specs=[_rows(tm, cols)] * 4, out_specs=[_rows(tm, cols)] * 3, out_shape=[shape] * 3,
        compiler_params=_params(("arbitrary",)),
    )(w, g, m, v)


def _adamw_vectors(g5, ws, ms, vs):
    n = len(ws)

    def body(g_ref, *refs):
        ins, outs = refs[:3 * n], refs[3 * n:]
        for i in range(n):
            res = _adamw_math(ins[i][...], g_ref[i:i + 1, :], ins[n + i][...], ins[2 * n + i][...])
            for kind in range(3):
                outs[kind * n + i][...] = res[kind]

    shape = jax.ShapeDtypeStruct((1, D), F32)
    return pl.pallas_call(body, name="adamw_vectors", out_shape=[shape] * (3 * n), compiler_params=_params())(
        g5, *ws, *ms, *vs)


def kernel(x, norm_g, w_in, conv_w, conv_b, conv_ln_g, conv_ln_b, w_out, final_norm_g, loss_target, m_norm_g, m_w_in, m_conv_w, m_conv_b, m_conv_ln_g, m_conv_ln_b, m_w_out, m_final_norm_g, v_norm_g, v_w_in, v_conv_w, v_conv_b, v_conv_ln_g, v_conv_ln_b, v_w_out, v_final_norm_g):
    chip = 2 * lax.axis_index("x") + lax.axis_index("y")
    where = jnp.stack([chip, lax.axis_index("c")]).astype(jnp.int32)
    taps_shard = jnp.pad(conv_w[0], ((0, HALO - CONV_K), (0, 0)))
    wi_full, wo_full, cw_full = _gather_weights(*_place_shards(w_in[0], w_out[0], taps_shard, where))

    gf = final_norm_g[None]
    grad_x, dw_in4, dw_out, small = _local_step(
        x[0], loss_target[0], norm_g, wi_full, cw_full, conv_b, conv_ln_g, conv_ln_b, wo_full, gf)
    dw_out4 = dw_out.reshape(NCHIP, WOUT_SHARD, D)

    ri, ro, rs = _exchange_halves(dw_in4, dw_out4, small)
    pi, po, ps = _add_halves(dw_in4, ri, dw_out4, ro, small, rs)
    ri, ro, rs = _exchange_chips(pi, po, ps)
    gi2, go2, g_small, g5, loss8 = _sum_chips(ri, ro, rs, pi, po, ps, where)
    gi2, go2 = _exchange_results(gi2, go2)
    g_w_in = gi2.reshape(D, CHUNK)
    g_w_out = go2.reshape(WOUT_SHARD, D)
    g_taps = lax.dynamic_slice(g_small, (ROW_TAPS, chip * CONVW_SHARD), (CONV_K, CONVW_SHARD))

    d_w_in, m2_w_in, v2_w_in = _adamw(w_in[0], g_w_in, m_w_in[0], v_w_in[0], "adamw_w_in")
    d_w_out, m2_w_out, v2_w_out = _adamw(w_out[0], g_w_out, m_w_out[0], v_w_out[0], "adamw_w_out")
    d_taps, m2_taps, v2_taps = _adamw(conv_w[0], g_taps, m_conv_w[0], v_conv_w[0], "adamw_conv_w")
    vec = _adamw_vectors(
        g5,
        (norm_g, conv_b, conv_ln_g, conv_ln_b, gf),
        (m_norm_g, m_conv_b, m_conv_ln_g, m_conv_ln_b, m_final_norm_g[None]),
        (v_norm_g, v_conv_b, v_conv_ln_g, v_conv_ln_b, v_final_norm_g[None]))
    d_vec, m2_vec, v2_vec = vec[0:5], vec[5:10], vec[10:15]

    def weight_order(ng, wi, cw, cb, lg, lb, wo, fg):
        return (ng, wi[None], cw[None], cb, lg, lb, wo[None], fg[0])

    grads = weight_order(g5[0:1], g_w_in, g_taps, g5[1:2], g5[2:3], g5[3:4], g_w_out, g5[4:5])
    deltas = weight_order(d_vec[0], d_w_in, d_taps, d_vec[1], d_vec[2], d_vec[3], d_w_out, d_vec[4])
    new_m = weight_order(m2_vec[0], m2_w_in, m2_taps, m2_vec[1], m2_vec[2], m2_vec[3], m2_w_out, m2_vec[4])
    new_v = weight_order(v2_vec[0], v2_w_in, v2_taps, v2_vec[1], v2_vec[2], v2_vec[3], v2_w_out, v2_vec[4])
    return (loss8[0, 0], grad_x[None], *grads, *deltas, *new_m, *new_v)
```

```python
import jax
import jax.numpy as jnp
from jax import lax
from jax.experimental import pallas as pl
from jax.experimental.pallas import tpu as pltpu

F32 = jnp.float32
BF16 = jnp.bfloat16

S = 4096
D = 1024
LANES = 128
HD = 64
NKV = 4
GQ = 4
KVW = NKV * HD
NCOL = 5632
CONV_K = 31
HALO = 32
BLK = 128
PATTERNS = (1, 4, 16)
NORM_EPS = 1e-6
LN_EPS = 1e-5
NEG = -1e30
OFF_Q, OFF_K, OFF_V, OFF_AG, OFF_CV, OFF_CG, OFF_CGATE = 0, 1024, 1280, 1536, 2560, 3584, 4608
NCHIP = 4
CHUNK = NCOL // NCHIP
WOUT_ROWS = 2 * D
WOUT_SHARD = WOUT_ROWS // NCHIP
CONVW_SHARD = D // NCHIP

ADAM_LR, ADAM_B1, ADAM_B2, ADAM_EPS, ADAM_WD, ADAM_STEP = 0.001, 0.9, 0.999, 1e-08, 0.01, 10

VMEM_LIMIT = 56 * 1024 * 1024


def _params(sem=None, vmem=VMEM_LIMIT):
    return pltpu.CompilerParams(dimension_semantics=sem, vmem_limit_bytes=vmem)


def _sigmoid(a):
    return 0.5 * jnp.tanh(0.5 * a) + 0.5


def _rows(tm, width):
    return pl.BlockSpec((tm, width), lambda i: (i, 0))


def _slabs(n):
    return jax.ShapeDtypeStruct((n, S, LANES), F32)


def _slab_rows(n, tm):
    return pl.BlockSpec((n, tm, LANES), lambda i: (0, i, 0))


def _resident(shape):
    return pl.BlockSpec(shape, lambda *_: (0,) * len(shape), pipeline_mode=pl.Buffered(1))


def _dot(a, b):
    return jnp.dot(a, b, preferred_element_type=F32)


def _dot_nt(a, b):
    return lax.dot_general(a, b, (((1,), (1,)), ((), ())), preferred_element_type=F32)


def _dot_tn(a, b):
    return lax.dot_general(a, b, (((0,), (0,)), ((), ())), preferred_element_type=F32)


def _inproj_fwd(x, g1, w_bf):
    tm = 512

    def body(x_ref, g_ref, w_ref, h_ref, q_ref, k_ref, v_ref, ag_ref, cv_ref, cg_ref, cgate_ref):
        xt = x_ref[...]
        r = lax.rsqrt(jnp.mean(xt * xt, axis=-1, keepdims=True) + NORM_EPS)
        h = (xt * r * g_ref[...]).astype(BF16)
        h_ref[...] = h
        q = _dot(h, w_ref[:, OFF_Q:OFF_Q + D]) * (HD ** -0.5)
        kv = _dot(h, w_ref[:, OFF_K:OFF_K + 2 * KVW])
        for sl in range(D // LANES):
            q_ref[sl] = q[:, sl * LANES:(sl + 1) * LANES]
        for sl in range(KVW // LANES):
            k_ref[sl] = kv[:, sl * LANES:(sl + 1) * LANES]
            v_ref[sl] = kv[:, KVW + sl * LANES:KVW + (sl + 1) * LANES]
        ag_ref[...] = _dot(h, w_ref[:, OFF_AG:OFF_AG + D])
        cv_ref[...] = _dot(h, w_ref[:, OFF_CV:OFF_CV + D])
        cg_ref[...] = _dot(h, w_ref[:, OFF_CG:OFF_CG + D])
        cgate_ref[...] = _dot(h, w_ref[:, OFF_CGATE:OFF_CGATE + D])

    big = jax.ShapeDtypeStruct((S, D), F32)
    return pl.pallas_call(
        body, grid=(S // tm,), name="inproj_fwd",
        in_specs=[_rows(tm, D), _resident((1, D)), _resident((D, NCOL))],
        out_specs=[_rows(tm, D), _slab_rows(D // LANES, tm), _slab_rows(KVW // LANES, tm), _slab_rows(KVW // LANES, tm),
                   _rows(tm, D), _rows(tm, D), _rows(tm, D), _rows(tm, D)],
        out_shape=[jax.ShapeDtypeStruct((S, D), BF16), _slabs(D // LANES), _slabs(KVW // LANES), _slabs(KVW // LANES),
                   big, big, big, big],
        compiler_params=_params(("arbitrary",)),
    )(x, g1, w_bf)


def _bias_table(d):
    h = jnp.arange(NKV * GQ, dtype=F32)
    slopes = jnp.exp2(-8.0 * (h + 1.0) / (NKV * GQ))
    qi = jnp.arange(BLK)[:, None]
    kj = jnp.arange(2 * BLK)[None, :]
    dist = BLK + qi - kj
    window = (dist >= 0) & (dist <= BLK)
    bias = -slopes[:, None, None] * (dist * d).astype(F32)[None]
    has_prev = jnp.stack([jnp.broadcast_to(kj >= BLK, (BLK, 2 * BLK)), jnp.ones((BLK, 2 * BLK), bool)])
    valid = window[None] & has_prev
    tab = jnp.where(valid[:, None], bias[None], NEG)
    return tab.reshape(2, NKV, GQ * BLK, 2 * BLK)


def _sub_rows(start, d):
    if d == 1:
        return pl.ds(pl.multiple_of(start, BLK), BLK)
    return pl.ds(start, BLK, stride=d)


NHEAD = NKV * GQ
CHUNK_ROWS = 2048
BLOCKS_PER_CHUNK = CHUNK_ROWS // BLK


def _low_lanes(rows=BLK):
    return lax.broadcasted_iota(jnp.int32, (rows, LANES), 1) < HD


def _block_start(idx, d):
    shift = d.bit_length() - 1
    b, r = lax.shift_right_logical(idx, shift), lax.bitwise_and(idx, d - 1)
    start = b * (BLK * d) + r
    return b, start, jnp.maximum(start - BLK * d, r)


def _stack_heads(ref, rows):
    low = _low_lanes()
    t0, t1 = ref[0, rows, :], ref[1, rows, :]
    return jnp.concatenate([jnp.where(low, t0, 0.0), jnp.where(low, 0.0, t0),
                            jnp.where(low, t1, 0.0), jnp.where(low, 0.0, t1)], axis=0).astype(BF16)


def _unstack_heads(dup):
    low = _low_lanes()
    return (jnp.where(low, dup[0:BLK], dup[BLK:2 * BLK]), jnp.where(low, dup[2 * BLK:3 * BLK], dup[3 * BLK:4 * BLK]))


def _kv_dup(ref, prow, rows, odd):
    t = jnp.concatenate([ref[0, prow, :], ref[0, rows, :]], axis=0)
    swapped = pltpu.roll(t, HD, axis=1)
    keep = jnp.logical_xor(_low_lanes(2 * BLK), odd)
    return jnp.where(keep, t, swapped).astype(BF16)


def _attn_fwd(q, k, v, bias, d):
    def body(q_ref, k_ref, v_ref, b_ref, o_ref, l_ref):
        odd = pl.program_id(0) % 2 == 1
        ones = jnp.ones((2 * BLK, LANES), BF16)

        def block(idx, carry):
            b, start, pstart = _block_start(idx, d)
            rows, prow = _sub_rows(start, d), _sub_rows(pstart, d)
            qs = _stack_heads(q_ref, rows)
            kw = _kv_dup(k_ref, prow, rows, odd)
            vw = _kv_dup(v_ref, prow, rows, odd)
            s = _dot_nt(qs, kw) + b_ref[jnp.minimum(b, 1), 0]
            m = jnp.max(s, axis=1, keepdims=True)
            p = jnp.exp(s - m).astype(BF16)
            ol = _dot(p, jnp.concatenate([vw, ones], axis=1))
            l = ol[:, LANES:]
            o_ref[0, rows, :], o_ref[1, rows, :] = _unstack_heads(ol[:, :LANES] / l)
            lse = m + jnp.log(l)
            for g in range(GQ):
                l_ref[g, rows, :] = lse[g * BLK:(g + 1) * BLK]
            return carry

        lax.fori_loop(0, S // BLK, block, 0, unroll=2)

    q_like = pl.BlockSpec((2, S, LANES), lambda j: (j, 0, 0))
    kv = pl.BlockSpec((1, S, LANES), lambda j: (j // 2, 0, 0))
    heads = pl.BlockSpec((GQ, S, LANES), lambda j: (j, 0, 0))
    bias_spec = pl.BlockSpec((2, 1, GQ * BLK, 2 * BLK), lambda j: (0, j, 0, 0))
    return pl.pallas_call(
        body, grid=(NKV,), name=f"attn_fwd_d{d}",
        in_specs=[q_like, kv, kv, bias_spec],
        out_specs=[q_like, heads],
        out_shape=[_slabs(D // LANES), _slabs(NHEAD)],
        compiler_params=_params(("arbitrary",)),
    )(q, k, v, bias)


def _attn_combine(outs, lses, a_gate):
    tm = 256

    def body(o1, o2, o3, l1, l2, l3, ag_ref, o_ref, lse_ref, y_ref):
        low = _low_lanes(tm)
        for sl in range(D // LANES):
            w = []
            for h in (2 * sl, 2 * sl + 1):
                a, b, c = l1[h], l2[h], l3[h]
                m = jnp.maximum(jnp.maximum(a, b), c)
                ea, eb, ec = jnp.exp(a - m), jnp.exp(b - m), jnp.exp(c - m)
                den = ea + eb + ec
                lse_ref[h] = m + jnp.log(den)
                inv = 1.0 / den
                w.append((ea * inv, eb * inv, ec * inv))
            wa, wb, wc = (jnp.where(low, w[0][i], w[1][i]) for i in range(3))
            o = wa * o1[sl] + wb * o2[sl] + wc * o3[sl]
            o_ref[sl] = o
            cols = slice(sl * LANES, (sl + 1) * LANES)
            ag = ag_ref[:, cols]
            y_ref[:, cols] = (o * (ag * _sigmoid(ag))).astype(BF16)

    wide, per_head = _slab_rows(D // LANES, tm), _slab_rows(NHEAD, tm)
    return pl.pallas_call(
        body, grid=(S // tm,), name="attn_combine",
        in_specs=[wide] * 3 + [per_head] * 3 + [_rows(tm, D)],
        out_specs=[wide, per_head, _rows(tm, D)],
        out_shape=[_slabs(D // LANES), _slabs(NHEAD), jax.ShapeDtypeStruct((S, D), BF16)],
        compiler_params=_params(("arbitrary",)),
    )(*outs, *lses, a_gate)


def _head_sum_selectors():
    lane_in = jnp.arange(LANES)[:, None] // HD
    return jnp.stack([jnp.broadcast_to(lane_in == h, (LANES, LANES)) for h in range(2)]).astype(BF16)


def _attn_gate_bwd(dy_att, o, a_gate, selectors):
    tm = 256

    def body(dy_ref, o_ref, ag_ref, e_ref, do_ref, dag_ref, delta_ref):
        for sl in range(D // LANES):
            cols = slice(sl * LANES, (sl + 1) * LANES)
            dy, ag, o_ = dy_ref[:, cols], ag_ref[:, cols], o_ref[sl]
            sg = _sigmoid(ag)
            do = dy * (ag * sg)
            do_ref[sl] = do
            dag_ref[:, cols] = (dy * o_ * (sg * (1.0 + ag * (1.0 - sg)))).astype(BF16)
            prod = do * o_
            hi = prod.astype(BF16)
            lo = (prod - hi.astype(F32)).astype(BF16)
            for h in range(2):
                delta_ref[2 * sl + h] = _dot(hi, e_ref[h]) + _dot(lo, e_ref[h])

    return pl.pallas_call(
        body, grid=(S // tm,), name="attn_gate_bwd",
        in_specs=[_rows(tm, D), _slab_rows(D // LANES, tm), _rows(tm, D), _resident((2, LANES, LANES))],
        out_specs=[_slab_rows(D // LANES, tm), _rows(tm, D), _slab_rows(NHEAD, tm)],
        out_shape=[_slabs(D // LANES), jax.ShapeDtypeStruct((S, D), BF16), _slabs(NHEAD)],
        compiler_params=_params(("arbitrary",)),
    )(dy_att, o, a_gate, selectors)


def _attn_bwd(q, k, v, do, lse, delta, bias, d):
    def body(q_ref, do_ref, l_ref, dl_ref, k_ref, v_ref, b_ref, dq_ref, dkv_ref):
        odd = pl.program_id(0) % 2 == 1
        chunk = pl.program_id(1)

        @pl.when(chunk == 0)
        def _():
            dkv_ref[...] = jnp.zeros_like(dkv_ref)

        def block(idx, carry):
            b, start, pstart = _block_start(chunk * BLOCKS_PER_CHUNK + idx, d)
            rows, prow = _sub_rows(start, d), _sub_rows(pstart, d)
            mine = _sub_rows(start - chunk * CHUNK_ROWS, d)
            qs = _stack_heads(q_ref, mine)
            dos = _stack_heads(do_ref, mine)
            lse_t = jnp.concatenate([l_ref[g, mine, :] for g in range(GQ)], axis=0)
            delta_t = jnp.concatenate([dl_ref[g, mine, :] for g in range(GQ)], axis=0)
            kw = _kv_dup(k_ref, prow, rows, odd)
            vw = _kv_dup(v_ref, prow, rows, odd)
            s = _dot_nt(qs, kw) + b_ref[jnp.minimum(b, 1), 0]
            p = jnp.exp(s - jnp.concatenate([lse_t, lse_t], axis=1))
            dv2 = _dot_tn(p.astype(BF16), dos)
            dp = _dot_nt(dos, vw)
            ds = (p * (dp - jnp.concatenate([delta_t, delta_t], axis=1))).astype(BF16)
            dq_ref[0, mine, :], dq_ref[1, mine, :] = _unstack_heads(_dot(ds, kw))
            dk2 = _dot_tn(ds, qs)
            dkv = jnp.where(_low_lanes(2 * BLK), dk2 + pltpu.roll(dk2, HD, axis=1), dv2 + pltpu.roll(dv2, HD, axis=1))
            dkv_ref[0, rows, :] = dkv_ref[0, rows, :] + dkv[BLK:]
            dkv_ref[0, prow, :] = dkv_ref[0, prow, :] + dkv[:BLK]
            return carry

        lax.fori_loop(0, BLOCKS_PER_CHUNK, block, 0, unroll=8)

    q_like = pl.BlockSpec((2, CHUNK_ROWS, LANES), lambda j, c: (j, c, 0))
    heads = pl.BlockSpec((GQ, CHUNK_ROWS, LANES), lambda j, c: (j, c, 0))
    kv = pl.BlockSpec((1, S, LANES), lambda j, c: (j // 2, 0, 0))
    per_kv = pl.BlockSpec((1, S, LANES), lambda j, c: (j, 0, 0))
    bias_spec = pl.BlockSpec((2, 1, GQ * BLK, 2 * BLK), lambda j, c: (0, j, 0, 0))
    return pl.pallas_call(
        body, grid=(NKV, S // CHUNK_ROWS), name=f"attn_bwd_d{d}",
        in_specs=[q_like, q_like, heads, heads, kv, kv, bias_spec],
        out_specs=[q_like, per_kv],
        out_shape=[_slabs(D // LANES), _slabs(NKV)],
        compiler_params=_params(("arbitrary", "arbitrary")),
    )(q, do, lse, delta, k, v, bias)


CONV_T = 128


def _halo_before(i):
    return (jnp.maximum(i * (CONV_T // HALO) - 1, 0), 0)


def _halo_after(i):
    return (jnp.minimum((i + 1) * (CONV_T // HALO), S // HALO - 1), 0)


SUBLANES = 8
NCH = D // LANES
GROUP = SUBLANES * SUBLANES


def _comb(ref, cb, base):
    return ref[cb, pl.ds(base, SUBLANES, stride=SUBLANES), :]


def _taps(w_ref, cols):
    return [jnp.broadcast_to(w_ref[j:j + 1, cols], (SUBLANES, LANES)) for j in range(CONV_K)]


def _conv_fwd(c_val, c_glu, c_gate, conv_w, conv_b, ln_g, ln_b):
    T = CONV_T

    def body(cv_ref, cg_ref, cvh_ref, cgh_ref, gate_ref, w_ref, b_ref, lg_ref, lb_ref, u_ref, y_ref, win, us):
        i = pl.program_id(0)
        for cb in range(NCH):
            cols = slice(cb * LANES, (cb + 1) * LANES)
            win[cb, HALO:HALO + T, :] = cv_ref[:, cols] * _sigmoid(cg_ref[:, cols])
            win[cb, 0:HALO, :] = jnp.where(i > 0, cvh_ref[:, cols] * _sigmoid(cgh_ref[:, cols]), 0.0)
        for cb in range(NCH):
            cols = slice(cb * LANES, (cb + 1) * LANES)
            taps = _taps(w_ref, cols)
            bias = jnp.broadcast_to(b_ref[:, cols], (SUBLANES, LANES))

            def group(g, carry):
                for b in range(SUBLANES):
                    base = g * GROUP + b
                    acc = bias
                    for j in range(CONV_K):
                        acc = acc + taps[j] * _comb(win, cb, base + (HALO - (CONV_K - 1) + j))
                    us[cb, pl.ds(base, SUBLANES, stride=SUBLANES), :] = acc
                return carry

            lax.fori_loop(0, T // GROUP, group, 0)
        total = us[0]
        for cb in range(1, NCH):
            total = total + us[cb]
        mu = jnp.sum(total, axis=-1, keepdims=True) * (1.0 / D)
        sq = jnp.zeros((T, LANES), F32)
        for cb in range(NCH):
            uc = us[cb] - mu
            sq = sq + uc * uc
        rstd = lax.rsqrt(jnp.sum(sq, axis=-1, keepdims=True) * (1.0 / D) + LN_EPS)
        for cb in range(NCH):
            cols = slice(cb * LANES, (cb + 1) * LANES)
            u = us[cb]
            u_ref[:, cols] = u
            nrm = (u - mu) * rstd * lg_ref[:, cols] + lb_ref[:, cols]
            gate = gate_ref[:, cols]
            y_ref[:, cols] = (nrm * _sigmoid(nrm) * (gate * _sigmoid(gate))).astype(BF16)

    halo = pl.BlockSpec((HALO, D), _halo_before)
    return pl.pallas_call(
        body, grid=(S // T,), name="conv_fwd",
        in_specs=[_rows(T, D), _rows(T, D), halo, halo, _rows(T, D),
                  _resident((HALO, D)), _resident((1, D)), _resident((1, D)), _resident((1, D))],
        out_specs=[_rows(T, D), _rows(T, D)],
        out_shape=[jax.ShapeDtypeStruct((S, D), F32), jax.ShapeDtypeStruct((S, D), BF16)],
        scratch_shapes=[pltpu.VMEM((NCH, T + HALO, LANES), F32), pltpu.VMEM((NCH, T, LANES), F32)],
        compiler_params=_params(("arbitrary",)),
    )(c_val, c_glu, c_val, c_glu, c_gate, conv_w, conv_b, ln_g, ln_b)


def _conv_bwd_rows(u, c_gate, dy_conv, ln_g, ln_b):
    tm = 256

    def body(u_ref, gate_ref, dy_ref, lg_ref, lb_ref, du_ref, dgate_ref, st_ref):
        @pl.when(pl.program_id(0) == 0)
        def _():
            st_ref[...] = jnp.zeros_like(st_ref)

        u, gate, dy = u_ref[...], gate_ref[...], dy_ref[...]
        mu = jnp.mean(u, axis=-1, keepdims=True)
        uc = u - mu
        rstd = lax.rsqrt(jnp.mean(uc * uc, axis=-1, keepdims=True) + LN_EPS)
        z = uc * rstd
        nrm = z * lg_ref[...] + lb_ref[...]
        sn, sg = _sigmoid(nrm), _sigmoid(gate)
        dgate_ref[...] = (dy * (nrm * sn) * (sg * (1.0 + gate * (1.0 - sg)))).astype(BF16)
        dn = dy * (gate * sg) * (sn * (1.0 + nrm * (1.0 - sn)))
        dz = dn * lg_ref[...]
        du = rstd * (dz - jnp.mean(dz, axis=-1, keepdims=True) - z * jnp.mean(dz * z, axis=-1, keepdims=True))
        du_ref[...] = du
        st_ref[0:1, :] += jnp.sum(dn * z, axis=0, keepdims=True)
        st_ref[1:2, :] += jnp.sum(dn, axis=0, keepdims=True)
        st_ref[2:3, :] += jnp.sum(du, axis=0, keepdims=True)

    big = jax.ShapeDtypeStruct((S, D), F32)
    return pl.pallas_call(
        body, grid=(S // tm,), name="conv_bwd_rows",
        in_specs=[_rows(tm, D)] * 3 + [_resident((1, D)), _resident((1, D))],
        out_specs=[_rows(tm, D), _rows(tm, D), pl.BlockSpec((8, D), lambda i: (0, 0))],
        out_shape=[big, jax.ShapeDtypeStruct((S, D), BF16), jax.ShapeDtypeStruct((8, D), F32)],
        compiler_params=_params(("arbitrary",)),
    )(u, c_gate, dy_conv, ln_g, ln_b)


def _conv_bwd_taps(du, c_val, c_glu, conv_w):
    T = CONV_T
    last = S // T - 1

    def body(du_ref, dua_ref, cv_ref, cg_ref, cvh_ref, cgh_ref, w_ref, dcv_ref, dcg_ref, dw_ref,
             hwin, dwin, dhs, dw_acc):
        i = pl.program_id(0)

        @pl.when(i == 0)
        def _():
            dw_acc[...] = jnp.zeros_like(dw_acc)

        for cb in range(NCH):
            cols = slice(cb * LANES, (cb + 1) * LANES)
            hwin[cb, HALO:HALO + T, :] = cv_ref[:, cols] * _sigmoid(cg_ref[:, cols])
            hwin[cb, 0:HALO, :] = jnp.where(i > 0, cvh_ref[:, cols] * _sigmoid(cgh_ref[:, cols]), 0.0)
            dwin[cb, 0:T, :] = du_ref[:, cols]
            dwin[cb, T:T + HALO, :] = jnp.where(i < last, dua_ref[:, cols], 0.0)
        for cb in range(NCH):
            cols = slice(cb * LANES, (cb + 1) * LANES)
            taps = _taps(w_ref, cols)

            def group_dh(g, carry):
                for b in range(SUBLANES):
                    base = g * GROUP + b
                    acc = jnp.zeros((SUBLANES, LANES), F32)
                    for j in range(CONV_K):
                        acc = acc + taps[j] * _comb(dwin, cb, base + (CONV_K - 1 - j))
                    dhs[cb, pl.ds(base, SUBLANES, stride=SUBLANES), :] = acc
                return carry

            lax.fori_loop(0, T // GROUP, group_dh, 0)

            def group_dw(g, sums):
                for b in range(SUBLANES):
                    base = g * GROUP + b
                    d = _comb(dwin, cb, base)
                    sums = tuple(sums[j] + d * _comb(hwin, cb, base + (HALO - (CONV_K - 1) + j))
                                 for j in range(CONV_K))
                return sums

            sums = lax.fori_loop(0, T // GROUP, group_dw, tuple(dw_acc[j, :, cols] for j in range(CONV_K)))
            for j in range(CONV_K):
                dw_acc[j, :, cols] = sums[j]
            dh = dhs[cb]
            cv, sg = cv_ref[:, cols], _sigmoid(cg_ref[:, cols])
            dcv_ref[:, cols] = (dh * sg).astype(BF16)
            dcg_ref[:, cols] = (dh * cv * (sg * (1.0 - sg))).astype(BF16)

        @pl.when(i == last)
        def _():
            dw_ref[...] = jnp.zeros_like(dw_ref)
            for j in range(CONV_K):
                dw_ref[j:j + 1, :] = jnp.sum(dw_acc[j], axis=0, keepdims=True)

    before = pl.BlockSpec((HALO, D), _halo_before)
    after = pl.BlockSpec((HALO, D), _halo_after)
    big = jax.ShapeDtypeStruct((S, D), BF16)
    return pl.pallas_call(
        body, grid=(S // T,), name="conv_bwd_taps",
        in_specs=[_rows(T, D), after, _rows(T, D), _rows(T, D), before, before, _resident((HALO, D))],
        out_specs=[_rows(T, D), _rows(T, D), pl.BlockSpec((HALO, D), lambda i: (0, 0))],
        out_shape=[big, big, jax.ShapeDtypeStruct((HALO, D), F32)],
        scratch_shapes=[pltpu.VMEM((NCH, T + HALO, LANES), F32), pltpu.VMEM((NCH, T + HALO, LANES), F32),
                        pltpu.VMEM((NCH, T, LANES), F32), pltpu.VMEM((CONV_K, SUBLANES, D), F32)],
        compiler_params=_params(("arbitrary",)),
    )(du, du, c_val, c_glu, c_val, c_glu, conv_w)


def _outproj_loss(y_att, y_conv, w_out_bf, x, target, gf):
    tm = 256

    def body(ya_ref, yc_ref, w_ref, x_ref, t_ref, gf_ref, dx2_ref, dya_ref, dyc_ref, dw_ref, st_ref, acc):
        @pl.when(pl.program_id(0) == 0)
        def _():
            acc[...] = jnp.zeros_like(acc)
            st_ref[...] = jnp.zeros_like(st_ref)

        ya, yc = ya_ref[...], yc_ref[...]
        x2 = x_ref[...] + _dot(ya, w_ref[0:D, :]) + _dot(yc, w_ref[D:2 * D, :])
        r = lax.rsqrt(jnp.mean(x2 * x2, axis=-1, keepdims=True) + NORM_EPS)
        xn = x2 * r
        err = xn * gf_ref[...] - t_ref[...]
        dout = err * (1.0 / D)
        dxn = dout * gf_ref[...]
        dx2 = r * (dxn - xn * jnp.mean(dxn * xn, axis=-1, keepdims=True))
        dx2_ref[...] = dx2
        dx2b = dx2.astype(BF16)
        dya_ref[...] = _dot_nt(dx2b, w_ref[0:D, :])
        dyc_ref[...] = _dot_nt(dx2b, w_ref[D:2 * D, :])
        acc[0:D, :] += _dot_tn(ya, dx2b)
        acc[D:2 * D, :] += _dot_tn(yc, dx2b)
        st_ref[0:1, :] += jnp.sum(dout * xn, axis=0, keepdims=True)
        st_ref[1:2, :] += jnp.sum(err * err, axis=0, keepdims=True) * (0.5 / D)

        @pl.when(pl.program_id(0) == S // tm - 1)
        def _():
            dw_ref[...] = acc[...].astype(BF16)

    big = jax.ShapeDtypeStruct((S, D), F32)
    return pl.pallas_call(
        body, grid=(S // tm,), name="outproj_loss",
        in_specs=[_rows(tm, D), _rows(tm, D), _resident((WOUT_ROWS, D)), _rows(tm, D), _rows(tm, D), _resident((1, D))],
        out_specs=[_rows(tm, D), _rows(tm, D), _rows(tm, D),
                   pl.BlockSpec((WOUT_ROWS, D), lambda i: (0, 0)), pl.BlockSpec((8, D), lambda i: (0, 0))],
        out_shape=[big, big, big, jax.ShapeDtypeStruct((WOUT_ROWS, D), BF16), jax.ShapeDtypeStruct((8, D), F32)],
        scratch_shapes=[pltpu.VMEM((WOUT_ROWS, D), F32)],
        compiler_params=_params(("arbitrary",)),
    )(y_att, y_conv, w_out_bf, x, target, gf)


def _assemble_dproj(dqs, dkvs, dag, dcv, dcg, dcgate):
    tm = 256

    def body(dq1, dq2, dq3, dkv1, dkv2, dkv3, dag_ref, dcv_ref, dcg_ref, dcgate_ref, dp_ref):
        for sl in range(D // LANES):
            dq = (dq1[sl] + dq2[sl] + dq3[sl]) * (HD ** -0.5)
            dp_ref[:, OFF_Q + sl * LANES:OFF_Q + (sl + 1) * LANES] = dq.astype(BF16)
        for j in range(NKV):
            dkv = (dkv1[j] + dkv2[j] + dkv3[j]).astype(BF16)
            dp_ref[:, OFF_K + j * HD:OFF_K + (j + 1) * HD] = dkv[:, :HD]
            dp_ref[:, OFF_V + j * HD:OFF_V + (j + 1) * HD] = dkv[:, HD:]
        for off, ref in ((OFF_AG, dag_ref), (OFF_CV, dcv_ref), (OFF_CG, dcg_ref), (OFF_CGATE, dcgate_ref)):
            dp_ref[:, off:off + D] = ref[...]

    return pl.pallas_call(
        body, grid=(S // tm,), name="assemble_dproj",
        in_specs=[_slab_rows(D // LANES, tm)] * 3 + [_slab_rows(NKV, tm)] * 3 + [_rows(tm, D)] * 4,
        out_specs=_rows(tm, NCOL),
        out_shape=jax.ShapeDtypeStruct((S, NCOL), BF16),
        compiler_params=_params(("arbitrary",)),
    )(*dqs, *dkvs, dag, dcv, dcg, dcgate)


def _inproj_bwd_x(dproj, w_bf, x, g1, dx2, pi, po, ps):
    tm = 256
    last = S // tm - 1

    def body(dp_ref, w_ref, x_ref, g_ref, dx2_ref, pi_ref, po_ref, ps_ref,
             gx_ref, st_ref, ri_ref, ro_ref, rs_ref, send, recv):
        i = pl.program_id(0)
        copies = _chip_exchange_copies((pi_ref, po_ref, ps_ref), (ri_ref, ro_ref, rs_ref), send, recv)

        @pl.when(i == 0)
        def _():
            st_ref[...] = jnp.zeros_like(st_ref)
            for out, _ in copies:
                out.start()

        dh = _dot_nt(dp_ref[...], w_ref[...])
        xt = x_ref[...]
        r = lax.rsqrt(jnp.mean(xt * xt, axis=-1, keepdims=True) + NORM_EPS)
        xn = xt * r
        dxn = dh * g_ref[...]
        gx_ref[...] = dx2_ref[...] + r * (dxn - xn * jnp.mean(dxn * xn, axis=-1, keepdims=True))
        st_ref[0:1, :] += jnp.sum(dh * xn, axis=0, keepdims=True)

        @pl.when(i == last)
        def _():
            for _, arrival in copies:
                arrival.wait_recv()
            for out, _ in copies:
                out.wait_send()

    n = 3 * len(CHIP_FLIPS)
    return pl.pallas_call(
        body, grid=(S // tm,), name="inproj_bwd_x",
        in_specs=[_rows(tm, NCOL), _resident((D, NCOL)), _rows(tm, D), _resident((1, D)), _rows(tm, D), ANY, ANY, ANY],
        out_specs=[_rows(tm, D), pl.BlockSpec((8, D), lambda i: (0, 0)), ANY, ANY, ANY],
        out_shape=[jax.ShapeDtypeStruct((S, D), F32), jax.ShapeDtypeStruct((8, D), F32),
                   jax.ShapeDtypeStruct((NCHIP, D // 2, CHUNK), BF16),
                   jax.ShapeDtypeStruct((NCHIP, WOUT_SHARD // 2, D), BF16),
                   jax.ShapeDtypeStruct((NCHIP, SMALL_ROWS, D), F32)],
        scratch_shapes=[pltpu.SemaphoreType.DMA((n,)), pltpu.SemaphoreType.DMA((n,))],
        compiler_params=_params(("arbitrary",)),
    )(dproj, w_bf, x, g1, dx2, pi, po, ps)


def _inproj_bwd_w(h, dproj):
    tk = 1024
    nk = S // tk

    def body(h_ref, dp_ref, o_ref, acc):
        i = pl.program_id(1)

        @pl.when(i == 0)
        def _():
            acc[...] = jnp.zeros_like(acc)

        acc[...] += _dot_tn(h_ref[...], dp_ref[...])

        @pl.when(i == nk - 1)
        def _():
            o_ref[0] = acc[...].astype(BF16)

    return pl.pallas_call(
        body, grid=(NCHIP, nk), name="inproj_bwd_w",
        in_specs=[pl.BlockSpec((tk, D), lambda c, i: (i, 0)), pl.BlockSpec((tk, CHUNK), lambda c, i: (i, c))],
        out_specs=pl.BlockSpec((1, D, CHUNK), lambda c, i: (c, 0, 0)),
        out_shape=jax.ShapeDtypeStruct((NCHIP, D, CHUNK), BF16),
        scratch_shapes=[pltpu.VMEM((D, CHUNK), F32)],
        compiler_params=_params(("arbitrary", "arbitrary")),
    )(h, dproj)


def _local_step(x, target, g1, w_in_bf, conv_w, conv_b, ln_g, ln_b, w_out_bf, gf):
    h, q, k, v, a_gate, c_val, c_glu, c_gate = _inproj_fwd(x, g1, w_in_bf)
    tables = [_bias_table(d) for d in PATTERNS]
    outs, lses = zip(*[_attn_fwd(q, k, v, t, d) for t, d in zip(tables, PATTERNS)])
    o, lse, y_att = _attn_combine(outs, lses, a_gate)
    u, y_conv = _conv_fwd(c_val, c_glu, c_gate, conv_w, conv_b, ln_g, ln_b)
    dx2, dy_att, dy_conv, dw_out, st_out = _outproj_loss(y_att, y_conv, w_out_bf, x, target, gf)

    do, da_gate, delta = _attn_gate_bwd(dy_att, o, a_gate, _head_sum_selectors())
    dqs, dkvs = zip(*[_attn_bwd(q, k, v, do, lse, delta, t, d) for t, d in zip(tables, PATTERNS)])

    du, dc_gate, st_conv = _conv_bwd_rows(u, c_gate, dy_conv, ln_g, ln_b)
    dc_val, dc_glu, dconv_w = _conv_bwd_taps(du, c_val, c_glu, conv_w)

    dproj = _assemble_dproj(dqs, dkvs, da_gate, dc_val, dc_glu, dc_gate)
    dw_in = _inproj_bwd_w(h, dproj)
    small = jnp.concatenate([st_conv, st_out, dconv_w], axis=0)
    return dw_in, dw_out, small, dproj, dx2


ROW_LN_G, ROW_LN_B, ROW_CONV_B, ROW_FINAL_G, ROW_LOSS, ROW_TAPS = 0, 1, 2, 8, 9, 16
SMALL_ROWS = 16 + HALO
NDEV = 8


MESH = pl.DeviceIdType.MESH
ANY = pl.BlockSpec(memory_space=pl.ANY)
CHIP_FLIPS = ((1, 0), (0, 1), (1, 1))


def _pos():
    return lax.axis_index("x"), lax.axis_index("y"), lax.axis_index("c")


def _flip(v, f):
    return 1 - v if f else v


def _ds(start, size, align=None):
    return pl.ds(pl.multiple_of(start, align or size), size)


def _place_shards(wi, wo, cw, where):
    steps = 4

    def body(where_ref, wi_ref, wo_ref, cw_ref, wi_full, wo_full, cw_full):
        wi_full[...] = wi_ref[...].astype(BF16)
        wo_full[...] = wo_ref[...].astype(BF16)
        cw_full[...] = cw_ref[...]

    grid_spec = pltpu.PrefetchScalarGridSpec(
        num_scalar_prefetch=1, grid=(steps,),
        in_specs=[pl.BlockSpec((D // steps, CHUNK), lambda i, w: (i, 0)),
                  pl.BlockSpec((WOUT_SHARD // steps, D), lambda i, w: (i, 0)),
                  pl.BlockSpec((HALO, CONVW_SHARD), lambda i, w: (0, 0))],
        out_specs=[pl.BlockSpec((D // steps, CHUNK), lambda i, w: (i, w[0])),
                   pl.BlockSpec((WOUT_SHARD // steps, D), lambda i, w: (w[0] * steps + i, 0)),
                   pl.BlockSpec((HALO, CONVW_SHARD), lambda i, w: (0, w[0]))])
    return pl.pallas_call(
        body, grid_spec=grid_spec, name="place_shards",
        out_shape=[jax.ShapeDtypeStruct((D, NCOL), BF16), jax.ShapeDtypeStruct((WOUT_ROWS, D), BF16),
                   jax.ShapeDtypeStruct((HALO, D), F32)],
        compiler_params=_params(("arbitrary",)),
    )(where, wi, wo, cw)


def _gather_weights(wi_full, wo_full, cw_full):
    halves = (D // 2, WOUT_SHARD // 2, HALO // 2)
    n_ici = 3 * len(CHIP_FLIPS)

    def body(_wi, _wo, _cw, wi_full, wo_full, cw_full, send, recv):
        x, y, c = _pos()

        def region(a, px, py, half):
            chip = 2 * px + py
            n = halves[a]
            if a == 0:
                return wi_full.at[_ds(half * n, n), _ds(chip * CHUNK, CHUNK, 128)]
            if a == 1:
                return wo_full.at[_ds(chip * WOUT_SHARD + half * n, n), :]
            return cw_full.at[_ds(half * n, n), _ds(chip * CONVW_SHARD, CONVW_SHARD, 128)]

        def remote(k, src, dst, dev):
            return pltpu.make_async_remote_copy(src_ref=src, dst_ref=dst, send_sem=send.at[k], recv_sem=recv.at[k],
                                                device_id=dev, device_id_type=MESH)

        sends = []
        for a in range(3):
            for j, (fx, fy) in enumerate(CHIP_FLIPS):
                mine = region(a, x, y, c)
                cp = remote(3 * a + j, mine, mine, (_flip(x, fx), _flip(y, fy), c))
                cp.start()
                sends.append(cp)
        for a in range(3):
            for j, (fx, fy) in enumerate(CHIP_FLIPS):
                px, py = _flip(x, fx), _flip(y, fy)
                got = region(a, px, py, c)
                remote(3 * a + j, got, got, (px, py, c)).wait_recv()
                cp = remote(n_ici + 3 * a + j, got, got, (x, y, 1 - c))
                cp.start()
                sends.append(cp)
        for a in range(3):
            for j, (fx, fy) in enumerate(CHIP_FLIPS):
                got = region(a, _flip(x, fx), _flip(y, fy), 1 - c)
                remote(n_ici + 3 * a + j, got, got, (x, y, 1 - c)).wait_recv()
        for cp in sends:
            cp.wait_send()

    return pl.pallas_call(
        body, name="gather_weights",
        in_specs=[ANY, ANY, ANY], out_specs=[ANY, ANY, ANY], input_output_aliases={0: 0, 1: 1, 2: 2},
        out_shape=[jax.ShapeDtypeStruct((D, NCOL), BF16), jax.ShapeDtypeStruct((WOUT_ROWS, D), BF16),
                   jax.ShapeDtypeStruct((HALO, D), F32)],
        scratch_shapes=[pltpu.SemaphoreType.DMA((2 * n_ici,)), pltpu.SemaphoreType.DMA((2 * n_ici,))],
    )(wi_full, wo_full, cw_full)


def _exchange_halves(gi4, go4, small):
    def body(gi_ref, go_ref, sm_ref, ri_ref, ro_ref, rs_ref, send, recv):
        x, y, c = _pos()
        sib = (x, y, 1 - c)
        copies = [
            (gi_ref.at[:, _ds((1 - c) * (D // 2), D // 2), :], ri_ref),
            (go_ref.at[:, _ds((1 - c) * (WOUT_SHARD // 2), WOUT_SHARD // 2), :], ro_ref),
            (sm_ref, rs_ref),
        ]
        cps = [pltpu.make_async_remote_copy(src_ref=s_, dst_ref=d_, send_sem=send.at[k], recv_sem=recv.at[k],
                                            device_id=sib, device_id_type=MESH) for k, (s_, d_) in enumerate(copies)]
        for cp in cps:
            cp.start()
        for cp in cps:
            cp.wait()

    return pl.pallas_call(
        body, name="exchange_halves",
        in_specs=[ANY, ANY, ANY], out_specs=[ANY, ANY, ANY],
        out_shape=[jax.ShapeDtypeStruct((NCHIP, D // 2, CHUNK), BF16),
                   jax.ShapeDtypeStruct((NCHIP, WOUT_SHARD // 2, D), BF16),
                   jax.ShapeDtypeStruct((SMALL_ROWS, D), F32)],
        scratch_shapes=[pltpu.SemaphoreType.DMA((3,)), pltpu.SemaphoreType.DMA((3,))],
    )(gi4, go4, small)


def _add_halves(gi4, ri, go4, ro, small, rs):
    hi, ho = D // 2, WOUT_SHARD // 2

    def body(gi_ref, ri_ref, go_ref, ro_ref, sm_ref, rs_ref, pi_ref, po_ref, ps_ref):
        c = lax.axis_index("c")
        pi_ref[0] = (gi_ref[0, _ds(c * hi, hi), :].astype(F32) + ri_ref[0].astype(F32)).astype(BF16)
        po_ref[0] = (go_ref[0, _ds(c * ho, ho), :].astype(F32) + ro_ref[0].astype(F32)).astype(BF16)
        ps_ref[...] = sm_ref[...] + rs_ref[...]

    blk = lambda n, w: pl.BlockSpec((1, n, w), lambda k: (k, 0, 0))
    whole = pl.BlockSpec((SMALL_ROWS, D), lambda k: (0, 0))
    return pl.pallas_call(
        body, grid=(NCHIP,), name="add_halves",
        in_specs=[blk(D, CHUNK), blk(hi, CHUNK), blk(WOUT_SHARD, D), blk(ho, D), whole, whole],
        out_specs=[blk(hi, CHUNK), blk(ho, D), whole],
        out_shape=[jax.ShapeDtypeStruct((NCHIP, hi, CHUNK), BF16), jax.ShapeDtypeStruct((NCHIP, ho, D), BF16),
                   jax.ShapeDtypeStruct((SMALL_ROWS, D), F32)],
        compiler_params=_params(("arbitrary",)),
    )(gi4, ri, go4, ro, small, rs)


def _chip_exchange_copies(srcs, dsts, send, recv):
    x, y, c = _pos()
    me = 2 * x + y
    pairs = []
    for a in range(3):
        for j, (fx, fy) in enumerate(CHIP_FLIPS):
            px, py = _flip(x, fx), _flip(y, fy)
            peer = 2 * px + py
            k = 3 * a + j
            out = pltpu.make_async_remote_copy(
                src_ref=srcs[a] if a == 2 else srcs[a].at[peer], dst_ref=dsts[a].at[me],
                send_sem=send.at[k], recv_sem=recv.at[k], device_id=(px, py, c), device_id_type=MESH)
            got = dsts[a].at[peer]
            arrival = pltpu.make_async_remote_copy(
                src_ref=got, dst_ref=got, send_sem=send.at[k], recv_sem=recv.at[k],
                device_id=(px, py, c), device_id_type=MESH)
            pairs.append((out, arrival))
    return pairs


def _sum_chips(ri, ro, rs, pi, po, ps, where):
    def body(w_ref, ri_ref, ro_ref, rs_ref, pi_ref, po_ref, ps_ref, gi_ref, go_ref, gs_ref, g5_ref, loss_ref,
             acc_i, acc_o, acc_s):
        k = pl.program_id(0)
        accs = (acc_i, acc_o, acc_s)

        @pl.when(k == 0)
        def _():
            for acc in accs:
                acc[...] = jnp.zeros_like(acc)

        @pl.when(k == w_ref[0])
        def _():
            for acc, val in zip(accs, (pi_ref[0], po_ref[0], ps_ref[...])):
                acc[...] += val.astype(F32)

        @pl.when(k != w_ref[0])
        def _():
            for acc, ref in zip(accs, (ri_ref, ro_ref, rs_ref)):
                acc[...] += ref[0].astype(F32)

        @pl.when(k == NCHIP - 1)
        def _():
            gi_ref[0] = acc_i[...]
            go_ref[0] = acc_o[...]
            gs_ref[...] = acc_s[...]
            g5_ref[...] = jnp.zeros_like(g5_ref)
            for i, row in enumerate((ROW_CONV_B, ROW_LN_G, ROW_LN_B, ROW_FINAL_G)):
                g5_ref[i + 1:i + 2, :] = acc_s[row:row + 1, :]
            loss = jnp.sum(acc_s[ROW_LOSS:ROW_LOSS + 1, :], axis=1, keepdims=True)
            loss_ref[...] = jnp.broadcast_to(loss, loss_ref.shape)

    def sent(k, w):
        return jnp.where(k == w[0], (k + 1) % NCHIP, k)

    hi, ho = D // 2, WOUT_SHARD // 2
    const = lambda shape: pl.BlockSpec(shape, lambda k, w: (0,) * len(shape))
    grid_spec = pltpu.PrefetchScalarGridSpec(
        num_scalar_prefetch=1, grid=(NCHIP,),
        in_specs=[pl.BlockSpec((1, hi, CHUNK), lambda k, w: (sent(k, w), 0, 0)),
                  pl.BlockSpec((1, ho, D), lambda k, w: (sent(k, w), 0, 0)),
                  pl.BlockSpec((1, SMALL_ROWS, D), lambda k, w: (sent(k, w), 0, 0)),
                  pl.BlockSpec((1, hi, CHUNK), lambda k, w: (w[0], 0, 0)),
                  pl.BlockSpec((1, ho, D), lambda k, w: (w[0], 0, 0)),
                  const((SMALL_ROWS, D))],
        out_specs=[pl.BlockSpec((1, hi, CHUNK), lambda k, w: (w[1], 0, 0)),
                   pl.BlockSpec((1, ho, D), lambda k, w: (w[1], 0, 0)),
                   const((SMALL_ROWS, D)), const((8, D)), const((8, LANES))],
        scratch_shapes=[pltpu.VMEM((hi, CHUNK), F32), pltpu.VMEM((ho, D), F32), pltpu.VMEM((SMALL_ROWS, D), F32)])
    return pl.pallas_call(
        body, grid_spec=grid_spec, name="sum_chips",
        out_shape=[jax.ShapeDtypeStruct((2, hi, CHUNK), F32), jax.ShapeDtypeStruct((2, ho, D), F32),
                   jax.ShapeDtypeStruct((SMALL_ROWS, D), F32), jax.ShapeDtypeStruct((8, D), F32),
                   jax.ShapeDtypeStruct((8, LANES), F32)],
        compiler_params=_params(("arbitrary",)),
    )(where, ri, ro, rs, pi, po, ps)


def _exchange_results(gi2, go2, st):
    flips = [(fx, fy, fc) for fx in (0, 1) for fy in (0, 1) for fc in (0, 1)][1:]

    def body(_gi, _go, st_ref, gi_ref, go_ref, all_ref, send, recv, lsem):
        x, y, c = _pos()
        sib = (x, y, 1 - c)

        def half(k, ref, slot):
            return pltpu.make_async_remote_copy(src_ref=ref.at[slot], dst_ref=ref.at[slot], send_sem=send.at[k],
                                                recv_sem=recv.at[k], device_id=sib, device_id_type=MESH)

        def stat(k, src, slot, dev):
            return pltpu.make_async_remote_copy(src_ref=src, dst_ref=all_ref.at[slot], send_sem=send.at[k],
                                                recv_sem=recv.at[k], device_id=dev, device_id_type=MESH)

        mine = pltpu.make_async_copy(st_ref, all_ref.at[4 * x + 2 * y + c], lsem)
        mine.start()
        sends = [half(k, ref, c) for k, ref in enumerate((gi_ref, go_ref))]
        peers = [(_flip(x, fx), _flip(y, fy), _flip(c, fc)) for fx, fy, fc in flips]
        sends += [stat(2 + k, st_ref, 4 * x + 2 * y + c, dev) for k, dev in enumerate(peers)]
        for cp in sends:
            cp.start()
        for k, ref in enumerate((gi_ref, go_ref)):
            half(k, ref, 1 - c).wait_recv()
        for k, (px, py, pc) in enumerate(peers):
            slot = 4 * px + 2 * py + pc
            stat(2 + k, all_ref.at[slot], slot, (px, py, pc)).wait_recv()
        for cp in sends:
            cp.wait_send()
        mine.wait()

    n = 2 + len(flips)
    return pl.pallas_call(
        body, name="exchange_results",
        in_specs=[ANY, ANY, ANY], out_specs=[ANY, ANY, ANY], input_output_aliases={0: 0, 1: 1},
        out_shape=[jax.ShapeDtypeStruct((2, D // 2, CHUNK), F32), jax.ShapeDtypeStruct((2, WOUT_SHARD // 2, D), F32),
                   jax.ShapeDtypeStruct((NDEV, 8, D), F32)],
        scratch_shapes=[pltpu.SemaphoreType.DMA((n,)), pltpu.SemaphoreType.DMA((n,)), pltpu.SemaphoreType.DMA],
    )(gi2, go2, st)


def _adamw_math(w, g, m, v):
    m2 = ADAM_B1 * m + (1.0 - ADAM_B1) * g
    v2 = ADAM_B2 * v + (1.0 - ADAM_B2) * (g * g)
    m_hat = m2 / (1.0 - ADAM_B1 ** ADAM_STEP)
    v_hat = v2 / (1.0 - ADAM_B2 ** ADAM_STEP)
    delta = -ADAM_LR * (m_hat / (jnp.sqrt(v_hat) + ADAM_EPS) + ADAM_WD * w)
    return delta, m2, v2


def _adamw(w, g, m, v, name):
    rows, cols = w.shape
    tm = 256 if rows % 256 == 0 else rows

    def body(w_ref, g_ref, m_ref, v_ref, d_ref, m2_ref, v2_ref):
        d_ref[...], m2_ref[...], v2_ref[...] = _adamw_math(w_ref[...], g_ref[...], m_ref[...], v_ref[...])

    shape = jax.ShapeDtypeStruct(w.shape, F32)
    return pl.pallas_call(
        body, grid=(rows // tm,), name=name,
        in_specs=[_rows(tm, cols)] * 4, out_specs=[_rows(tm, cols)] * 3, out_shape=[shape] * 3,
        compiler_params=_params(("arbitrary",)),
    )(w, g, m, v)


def _adamw_vectors(g5, first_parts, ws, ms, vs):
    n = len(ws)

    def body(g_ref, parts_ref, *refs):
        ins, g0_ref, outs = refs[:3 * n], refs[3 * n], refs[3 * n + 1:]
        g0 = parts_ref[0, 0:1, :]
        for dev in range(1, NDEV):
            g0 = g0 + parts_ref[dev, 0:1, :]
        g0_ref[...] = g0
        for i in range(n):
            g = g0 if i == 0 else g_ref[i:i + 1, :]
            res = _adamw_math(ins[i][...], g, ins[n + i][...], ins[2 * n + i][...])
            for kind in range(3):
                outs[kind * n + i][...] = res[kind]

    shape = jax.ShapeDtypeStruct((1, D), F32)
    return pl.pallas_call(body, name="adamw_vectors", out_shape=[shape] * (1 + 3 * n), compiler_params=_params())(
        g5, first_parts, *ws, *ms, *vs)


def kernel(x, norm_g, w_in, conv_w, conv_b, conv_ln_g, conv_ln_b, w_out, final_norm_g, loss_target, m_norm_g, m_w_in, m_conv_w, m_conv_b, m_conv_ln_g, m_conv_ln_b, m_w_out, m_final_norm_g, v_norm_g, v_w_in, v_conv_w, v_conv_b, v_conv_ln_g, v_conv_ln_b, v_w_out, v_final_norm_g):
    chip = 2 * lax.axis_index("x") + lax.axis_index("y")
    where = jnp.stack([chip, lax.axis_index("c")]).astype(jnp.int32)
    taps_shard = jnp.pad(conv_w[0], ((0, HALO - CONV_K), (0, 0)))
    wi_full, wo_full, cw_full = _gather_weights(*_place_shards(w_in[0], w_out[0], taps_shard, where))

    gf = final_norm_g[None]
    dw_in4, dw_out, small, dproj, dx2 = _local_step(
        x[0], loss_target[0], norm_g, wi_full, cw_full, conv_b, conv_ln_g, conv_ln_b, wo_full, gf)
    dw_out4 = dw_out.reshape(NCHIP, WOUT_SHARD, D)

    ri, ro, rs = _exchange_halves(dw_in4, dw_out4, small)
    pi, po, ps = _add_halves(dw_in4, ri, dw_out4, ro, small, rs)
    grad_x, st_in, ri, ro, rs = _inproj_bwd_x(dproj, wi_full, x[0], norm_g, dx2, pi, po, ps)
    gi2, go2, g_small, g5, loss8 = _sum_chips(ri, ro, rs, pi, po, ps, where)
    gi2, go2, norm_g_parts = _exchange_results(gi2, go2, st_in)
    g_w_in = gi2.reshape(D, CHUNK)
    g_w_out = go2.reshape(WOUT_SHARD, D)
    g_taps = lax.dynamic_slice(g_small, (ROW_TAPS, chip * CONVW_SHARD), (CONV_K, CONVW_SHARD))

    d_w_in, m2_w_in, v2_w_in = _adamw(w_in[0], g_w_in, m_w_in[0], v_w_in[0], "adamw_w_in")
    d_w_out, m2_w_out, v2_w_out = _adamw(w_out[0], g_w_out, m_w_out[0], v_w_out[0], "adamw_w_out")
    d_taps, m2_taps, v2_taps = _adamw(conv_w[0], g_taps, m_conv_w[0], v_conv_w[0], "adamw_conv_w")
    g_norm, *vec = _adamw_vectors(
        g5, norm_g_parts,
        (norm_g, conv_b, conv_ln_g, conv_ln_b, gf),
        (m_norm_g, m_conv_b, m_conv_ln_g, m_conv_ln_b, m_final_norm_g[None]),
        (v_norm_g, v_conv_b, v_conv_ln_g, v_conv_ln_b, v_final_norm_g[None]))
    d_vec, m2_vec, v2_vec = vec[0:5], vec[5:10], vec[10:15]

    def weight_order(ng, wi, cw, cb, lg, lb, wo, fg):
        return (ng, wi[None], cw[None], cb, lg, lb, wo[None], fg[0])

    grads = weight_order(g_norm, g_w_in, g_taps, g5[1:2], g5[2:3], g5[3:4], g_w_out, g5[4:5])
    deltas = weight_order(d_vec[0], d_w_in, d_taps, d_vec[1], d_vec[2], d_vec[3], d_w_out, d_vec[4])
    new_m = weight_order(m2_vec[0], m2_w_in, m2_taps, m2_vec[1], m2_vec[2], m2_vec[3], m2_w_out, m2_vec[4])
    new_v = weight_order(v2_vec[0], v2_w_in, v2_taps, v2_vec[1], v2_vec[2], v2_vec[3], v2_w_out, v2_vec[4])
    return (loss8[0, 0], grad_x[None], *grads, *deltas, *new_m, *new_v)
```

```python
import jax
import jax.numpy as jnp
from jax import lax
from jax.experimental import pallas as pl
from jax.experimental.pallas import tpu as pltpu

F32 = jnp.float32
BF16 = jnp.bfloat16

S = 4096
D = 1024
LANES = 128
HD = 64
NKV = 4
GQ = 4
KVW = NKV * HD
NCOL = 5632
CONV_K = 31
HALO = 32
BLK = 128
PATTERNS = (1, 4, 16)
NORM_EPS = 1e-6
LN_EPS = 1e-5
NEG = -1e30
OFF_Q, OFF_K, OFF_V, OFF_AG, OFF_CV, OFF_CG, OFF_CGATE = 0, 1024, 1280, 1536, 2560, 3584, 4608
NCHIP = 4
CHUNK = NCOL // NCHIP
WOUT_ROWS = 2 * D
WOUT_SHARD = WOUT_ROWS // NCHIP
CONVW_SHARD = D // NCHIP

ADAM_LR, ADAM_B1, ADAM_B2, ADAM_EPS, ADAM_WD, ADAM_STEP = 0.001, 0.9, 0.999, 1e-08, 0.01, 10

VMEM_LIMIT = 56 * 1024 * 1024


def _params(sem=None, vmem=VMEM_LIMIT):
    return pltpu.CompilerParams(dimension_semantics=sem, vmem_limit_bytes=vmem)


def _sigmoid(a):
    return 0.5 * jnp.tanh(0.5 * a) + 0.5


def _rows(tm, width):
    return pl.BlockSpec((tm, width), lambda i: (i, 0))


def _slabs(n):
    return jax.ShapeDtypeStruct((n, S, LANES), F32)


def _slab_rows(n, tm):
    return pl.BlockSpec((n, tm, LANES), lambda i: (0, i, 0))


def _resident(shape):
    return pl.BlockSpec(shape, lambda *_: (0,) * len(shape), pipeline_mode=pl.Buffered(1))


def _dot(a, b):
    return jnp.dot(a, b, preferred_element_type=F32)


def _dot_nt(a, b):
    return lax.dot_general(a, b, (((1,), (1,)), ((), ())), preferred_element_type=F32)


def _dot_tn(a, b):
    return lax.dot_general(a, b, (((0,), (0,)), ((), ())), preferred_element_type=F32)


def _inproj_fwd(x, g1, w_bf):
    tm = 512

    def body(x_ref, g_ref, w_ref, h_ref, q_ref, k_ref, v_ref, ag_ref, cv_ref, cg_ref, cgate_ref):
        xt = x_ref[...]
        r = lax.rsqrt(jnp.mean(xt * xt, axis=-1, keepdims=True) + NORM_EPS)
        h = (xt * r * g_ref[...]).astype(BF16)
        h_ref[...] = h
        q = _dot(h, w_ref[:, OFF_Q:OFF_Q + D]) * (HD ** -0.5)
        kv = _dot(h, w_ref[:, OFF_K:OFF_K + 2 * KVW])
        for sl in range(D // LANES):
            q_ref[sl] = q[:, sl * LANES:(sl + 1) * LANES]
        for sl in range(KVW // LANES):
            k_ref[sl] = kv[:, sl * LANES:(sl + 1) * LANES]
            v_ref[sl] = kv[:, KVW + sl * LANES:KVW + (sl + 1) * LANES]
        ag_ref[...] = _dot(h, w_ref[:, OFF_AG:OFF_AG + D])
        cv_ref[...] = _dot(h, w_ref[:, OFF_CV:OFF_CV + D])
        cg_ref[...] = _dot(h, w_ref[:, OFF_CG:OFF_CG + D])
        cgate_ref[...] = _dot(h, w_ref[:, OFF_CGATE:OFF_CGATE + D])

    big = jax.ShapeDtypeStruct((S, D), F32)
    return pl.pallas_call(
        body, grid=(S // tm,), name="inproj_fwd",
        in_specs=[_rows(tm, D), _resident((1, D)), _resident((D, NCOL))],
        out_specs=[_rows(tm, D), _slab_rows(D // LANES, tm), _slab_rows(KVW // LANES, tm), _slab_rows(KVW // LANES, tm),
                   _rows(tm, D), _rows(tm, D), _rows(tm, D), _rows(tm, D)],
        out_shape=[jax.ShapeDtypeStruct((S, D), BF16), _slabs(D // LANES), _slabs(KVW // LANES), _slabs(KVW // LANES),
                   big, big, big, big],
        compiler_params=_params(("arbitrary",)),
    )(x, g1, w_bf)


def _bias_table(d):
    h = jnp.arange(NKV * GQ, dtype=F32)
    slopes = jnp.exp2(-8.0 * (h + 1.0) / (NKV * GQ))
    qi = jnp.arange(BLK)[:, None]
    kj = jnp.arange(2 * BLK)[None, :]
    dist = BLK + qi - kj
    window = (dist >= 0) & (dist <= BLK)
    bias = -slopes[:, None, None] * (dist * d).astype(F32)[None]
    has_prev = jnp.stack([jnp.broadcast_to(kj >= BLK, (BLK, 2 * BLK)), jnp.ones((BLK, 2 * BLK), bool)])
    valid = window[None] & has_prev
    tab = jnp.where(valid[:, None], bias[None], NEG)
    return tab.reshape(2, NKV, GQ * BLK, 2 * BLK)


def _sub_rows(start, d):
    if d == 1:
        return pl.ds(pl.multiple_of(start, BLK), BLK)
    return pl.ds(start, BLK, stride=d)


NHEAD = NKV * GQ
CHUNK_ROWS = 2048
BLOCKS_PER_CHUNK = CHUNK_ROWS // BLK


def _low_lanes(rows=BLK):
    return lax.broadcasted_iota(jnp.int32, (rows, LANES), 1) < HD


def _block_start(idx, d):
    shift = d.bit_length() - 1
    b, r = lax.shift_right_logical(idx, shift), lax.bitwise_and(idx, d - 1)
    start = b * (BLK * d) + r
    return b, start, jnp.maximum(start - BLK * d, r)


def _stack_heads(ref, rows):
    low = _low_lanes()
    t0, t1 = ref[0, rows, :], ref[1, rows, :]
    return jnp.concatenate([jnp.where(low, t0, 0.0), jnp.where(low, 0.0, t0),
                            jnp.where(low, t1, 0.0), jnp.where(low, 0.0, t1)], axis=0).astype(BF16)


def _unstack_heads(dup):
    low = _low_lanes()
    return (jnp.where(low, dup[0:BLK], dup[BLK:2 * BLK]), jnp.where(low, dup[2 * BLK:3 * BLK], dup[3 * BLK:4 * BLK]))


def _kv_dup(ref, prow, rows, odd):
    t = jnp.concatenate([ref[0, prow, :], ref[0, rows, :]], axis=0)
    swapped = pltpu.roll(t, HD, axis=1)
    keep = jnp.logical_xor(_low_lanes(2 * BLK), odd)
    return jnp.where(keep, t, swapped).astype(BF16)


def _attn_fwd(q, k, v, bias, d):
    def body(q_ref, k_ref, v_ref, b_ref, o_ref, l_ref):
        odd = pl.program_id(0) % 2 == 1
        ones = jnp.ones((2 * BLK, LANES), BF16)

        def block(idx, carry):
            b, start, pstart = _block_start(idx, d)
            rows, prow = _sub_rows(start, d), _sub_rows(pstart, d)
            qs = _stack_heads(q_ref, rows)
            kw = _kv_dup(k_ref, prow, rows, odd)
            vw = _kv_dup(v_ref, prow, rows, odd)
            s = _dot_nt(qs, kw) + b_ref[jnp.minimum(b, 1), 0]
            m = jnp.max(s, axis=1, keepdims=True)
            p = jnp.exp(s - m).astype(BF16)
            ol = _dot(p, jnp.concatenate([vw, ones], axis=1))
            l = ol[:, LANES:]
            o_ref[0, rows, :], o_ref[1, rows, :] = _unstack_heads(ol[:, :LANES] / l)
            lse = m + jnp.log(l)
            for g in range(GQ):
                l_ref[g, rows, :] = lse[g * BLK:(g + 1) * BLK]
            return carry

        lax.fori_loop(0, S // BLK, block, 0, unroll=2)

    q_like = pl.BlockSpec((2, S, LANES), lambda j: (j, 0, 0))
    kv = pl.BlockSpec((1, S, LANES), lambda j: (j // 2, 0, 0))
    heads = pl.BlockSpec((GQ, S, LANES), lambda j: (j, 0, 0))
    bias_spec = pl.BlockSpec((2, 1, GQ * BLK, 2 * BLK), lambda j: (0, j, 0, 0))
    return pl.pallas_call(
        body, grid=(NKV,), name=f"attn_fwd_d{d}",
        in_specs=[q_like, kv, kv, bias_spec],
        out_specs=[q_like, heads],
        out_shape=[_slabs(D // LANES), _slabs(NHEAD)],
        compiler_params=_params(("arbitrary",)),
    )(q, k, v, bias)


def _attn_combine(outs, lses, a_gate):
    tm = 256

    def body(o1, o2, o3, l1, l2, l3, ag_ref, o_ref, lse_ref, y_ref):
        low = _low_lanes(tm)
        for sl in range(D // LANES):
            w = []
            for h in (2 * sl, 2 * sl + 1):
                a, b, c = l1[h], l2[h], l3[h]
                m = jnp.maximum(jnp.maximum(a, b), c)
                ea, eb, ec = jnp.exp(a - m), jnp.exp(b - m), jnp.exp(c - m)
                den = ea + eb + ec
                lse_ref[h] = m + jnp.log(den)
                inv = 1.0 / den
                w.append((ea * inv, eb * inv, ec * inv))
            wa, wb, wc = (jnp.where(low, w[0][i], w[1][i]) for i in range(3))
            o = wa * o1[sl] + wb * o2[sl] + wc * o3[sl]
            o_ref[sl] = o
            cols = slice(sl * LANES, (sl + 1) * LANES)
            ag = ag_ref[:, cols]
            y_ref[:, cols] = (o * (ag * _sigmoid(ag))).astype(BF16)

    wide, per_head = _slab_rows(D // LANES, tm), _slab_rows(NHEAD, tm)
    return pl.pallas_call(
        body, grid=(S // tm,), name="attn_combine",
        in_specs=[wide] * 3 + [per_head] * 3 + [_rows(tm, D)],
        out_specs=[wide, per_head, _rows(tm, D)],
        out_shape=[_slabs(D // LANES), _slabs(NHEAD), jax.ShapeDtypeStruct((S, D), BF16)],
        compiler_params=_params(("arbitrary",)),
    )(*outs, *lses, a_gate)


def _head_sum_selectors():
    lane_in = jnp.arange(LANES)[:, None] // HD
    return jnp.stack([jnp.broadcast_to(lane_in == h, (LANES, LANES)) for h in range(2)]).astype(BF16)


def _attn_gate_bwd(dy_att, o, a_gate, selectors):
    tm = 256

    def body(dy_ref, o_ref, ag_ref, e_ref, do_ref, dag_ref, delta_ref):
        for sl in range(D // LANES):
            cols = slice(sl * LANES, (sl + 1) * LANES)
            dy, ag, o_ = dy_ref[:, cols], ag_ref[:, cols], o_ref[sl]
            sg = _sigmoid(ag)
            do = dy * (ag * sg)
            do_ref[sl] = do
            dag_ref[:, cols] = (dy * o_ * (sg * (1.0 + ag * (1.0 - sg)))).astype(BF16)
            prod = do * o_
            hi = prod.astype(BF16)
            lo = (prod - hi.astype(F32)).astype(BF16)
            for h in range(2):
                delta_ref[2 * sl + h] = _dot(hi, e_ref[h]) + _dot(lo, e_ref[h])

    return pl.pallas_call(
        body, grid=(S // tm,), name="attn_gate_bwd",
        in_specs=[_rows(tm, D), _slab_rows(D // LANES, tm), _rows(tm, D), _resident((2, LANES, LANES))],
        out_specs=[_slab_rows(D // LANES, tm), _rows(tm, D), _slab_rows(NHEAD, tm)],
        out_shape=[_slabs(D // LANES), jax.ShapeDtypeStruct((S, D), BF16), _slabs(NHEAD)],
        compiler_params=_params(("arbitrary",)),
    )(dy_att, o, a_gate, selectors)


def _attn_bwd(q, k, v, do, lse, delta, bias, d, so_far=None):
    def body(q_ref, do_ref, l_ref, dl_ref, k_ref, v_ref, b_ref, *rest):
        if so_far is None:
            dq_ref, dkv_ref = rest
        else:
            dq0_ref, dkv0_ref, dq_ref, dkv_ref = rest
        j = pl.program_id(0)
        odd = j % 2 == 1
        chunk = pl.program_id(1)

        @pl.when(chunk == 0)
        def _():
            if so_far is None:
                dkv_ref[...] = jnp.zeros_like(dkv_ref)
            else:
                pltpu.sync_copy(dkv0_ref.at[pl.ds(j, 1)], dkv_ref)

        def block(idx, carry):
            b, start, pstart = _block_start(chunk * BLOCKS_PER_CHUNK + idx, d)
            rows, prow = _sub_rows(start, d), _sub_rows(pstart, d)
            mine = _sub_rows(start - chunk * CHUNK_ROWS, d)
            qs = _stack_heads(q_ref, mine)
            dos = _stack_heads(do_ref, mine)
            lse_t = jnp.concatenate([l_ref[g, mine, :] for g in range(GQ)], axis=0)
            delta_t = jnp.concatenate([dl_ref[g, mine, :] for g in range(GQ)], axis=0)
            kw = _kv_dup(k_ref, prow, rows, odd)
            vw = _kv_dup(v_ref, prow, rows, odd)
            s = _dot_nt(qs, kw) + b_ref[jnp.minimum(b, 1), 0]
            p = jnp.exp(s - jnp.concatenate([lse_t, lse_t], axis=1))
            dv2 = _dot_tn(p.astype(BF16), dos)
            dp = _dot_nt(dos, vw)
            ds = (p * (dp - jnp.concatenate([delta_t, delta_t], axis=1))).astype(BF16)
            dq0, dq1 = _unstack_heads(_dot(ds, kw))
            if so_far is not None:
                dq0, dq1 = dq0 + dq0_ref[0, mine, :], dq1 + dq0_ref[1, mine, :]
            dq_ref[0, mine, :], dq_ref[1, mine, :] = dq0, dq1
            dk2 = _dot_tn(ds, qs)
            dkv = jnp.where(_low_lanes(2 * BLK), dk2 + pltpu.roll(dk2, HD, axis=1), dv2 + pltpu.roll(dv2, HD, axis=1))
            dkv_ref[0, rows, :] = dkv_ref[0, rows, :] + dkv[BLK:]
            dkv_ref[0, prow, :] = dkv_ref[0, prow, :] + dkv[:BLK]
            return carry

        lax.fori_loop(0, BLOCKS_PER_CHUNK, block, 0, unroll=8)

    q_like = pl.BlockSpec((2, CHUNK_ROWS, LANES), lambda j, c: (j, c, 0))
    heads = pl.BlockSpec((GQ, CHUNK_ROWS, LANES), lambda j, c: (j, c, 0))
    kv = pl.BlockSpec((1, S, LANES), lambda j, c: (j // 2, 0, 0))
    per_kv = pl.BlockSpec((1, S, LANES), lambda j, c: (j, 0, 0))
    bias_spec = pl.BlockSpec((2, 1, GQ * BLK, 2 * BLK), lambda j, c: (0, j, 0, 0))
    earlier = () if so_far is None else (q_like, ANY)
    return pl.pallas_call(
        body, grid=(NKV, S // CHUNK_ROWS), name=f"attn_bwd_d{d}",
        in_specs=[q_like, q_like, heads, heads, kv, kv, bias_spec, *earlier],
        out_specs=[q_like, per_kv],
        out_shape=[_slabs(D // LANES), _slabs(NKV)],
        compiler_params=_params(("arbitrary", "arbitrary")),
    )(q, do, lse, delta, k, v, bias, *(so_far or ()))


CONV_T = 128


def _halo_before(i):
    return (jnp.maximum(i * (CONV_T // HALO) - 1, 0), 0)


def _halo_after(i):
    return (jnp.minimum((i + 1) * (CONV_T // HALO), S // HALO - 1), 0)


SUBLANES = 8
NCH = D // LANES
GROUP = SUBLANES * SUBLANES


def _comb(ref, cb, base):
    return ref[cb, pl.ds(base, SUBLANES, stride=SUBLANES), :]


def _taps(w_ref, cols):
    return [jnp.broadcast_to(w_ref[j:j + 1, cols], (SUBLANES, LANES)) for j in range(CONV_K)]


def _conv_fwd(c_val, c_glu, c_gate, conv_w, conv_b, ln_g, ln_b):
    T = CONV_T

    def body(cv_ref, cg_ref, cvh_ref, cgh_ref, gate_ref, w_ref, b_ref, lg_ref, lb_ref, u_ref, y_ref, win, us):
        i = pl.program_id(0)
        for cb in range(NCH):
            cols = slice(cb * LANES, (cb + 1) * LANES)
            win[cb, HALO:HALO + T, :] = cv_ref[:, cols] * _sigmoid(cg_ref[:, cols])
            win[cb, 0:HALO, :] = jnp.where(i > 0, cvh_ref[:, cols] * _sigmoid(cgh_ref[:, cols]), 0.0)
        for cb in range(NCH):
            cols = slice(cb * LANES, (cb + 1) * LANES)
            taps = _taps(w_ref, cols)
            bias = jnp.broadcast_to(b_ref[:, cols], (SUBLANES, LANES))

            def group(g, carry):
                for b in range(SUBLANES):
                    base = g * GROUP + b
                    acc = bias
                    for j in range(CONV_K):
                        acc = acc + taps[j] * _comb(win, cb, base + (HALO - (CONV_K - 1) + j))
                    us[cb, pl.ds(base, SUBLANES, stride=SUBLANES), :] = acc
                return carry

            lax.fori_loop(0, T // GROUP, group, 0)
        total = us[0]
        for cb in range(1, NCH):
            total = total + us[cb]
        mu = jnp.sum(total, axis=-1, keepdims=True) * (1.0 / D)
        sq = jnp.zeros((T, LANES), F32)
        for cb in range(NCH):
            uc = us[cb] - mu
            sq = sq + uc * uc
        rstd = lax.rsqrt(jnp.sum(sq, axis=-1, keepdims=True) * (1.0 / D) + LN_EPS)
        for cb in range(NCH):
            cols = slice(cb * LANES, (cb + 1) * LANES)
            u = us[cb]
            u_ref[:, cols] = u
            nrm = (u - mu) * rstd * lg_ref[:, cols] + lb_ref[:, cols]
            gate = gate_ref[:, cols]
            y_ref[:, cols] = (nrm * _sigmoid(nrm) * (gate * _sigmoid(gate))).astype(BF16)

    halo = pl.BlockSpec((HALO, D), _halo_before)
    return pl.pallas_call(
        body, grid=(S // T,), name="conv_fwd",
        in_specs=[_rows(T, D), _rows(T, D), halo, halo, _rows(T, D),
                  _resident((HALO, D)), _resident((1, D)), _resident((1, D)), _resident((1, D))],
        out_specs=[_rows(T, D), _rows(T, D)],
        out_shape=[jax.ShapeDtypeStruct((S, D), F32), jax.ShapeDtypeStruct((S, D), BF16)],
        scratch_shapes=[pltpu.VMEM((NCH, T + HALO, LANES), F32), pltpu.VMEM((NCH, T, LANES), F32)],
        compiler_params=_params(("arbitrary",)),
    )(c_val, c_glu, c_val, c_glu, c_gate, conv_w, conv_b, ln_g, ln_b)


def _conv_bwd_rows(u, c_gate, dy_conv, ln_g, ln_b):
    tm = 256

    def body(u_ref, gate_ref, dy_ref, lg_ref, lb_ref, du_ref, dgate_ref, st_ref):
        @pl.when(pl.program_id(0) == 0)
        def _():
            st_ref[...] = jnp.zeros_like(st_ref)

        u, gate, dy = u_ref[...], gate_ref[...], dy_ref[...]
        mu = jnp.mean(u, axis=-1, keepdims=True)
        uc = u - mu
        rstd = lax.rsqrt(jnp.mean(uc * uc, axis=-1, keepdims=True) + LN_EPS)
        z = uc * rstd
        nrm = z * lg_ref[...] + lb_ref[...]
        sn, sg = _sigmoid(nrm), _sigmoid(gate)
        dgate_ref[...] = (dy * (nrm * sn) * (sg * (1.0 + gate * (1.0 - sg)))).astype(BF16)
        dn = dy * (gate * sg) * (sn * (1.0 + nrm * (1.0 - sn)))
        dz = dn * lg_ref[...]
        du = rstd * (dz - jnp.mean(dz, axis=-1, keepdims=True) - z * jnp.mean(dz * z, axis=-1, keepdims=True))
        du_ref[...] = du
        st_ref[0:1, :] += jnp.sum(dn * z, axis=0, keepdims=True)
        st_ref[1:2, :] += jnp.sum(dn, axis=0, keepdims=True)
        st_ref[2:3, :] += jnp.sum(du, axis=0, keepdims=True)

    big = jax.ShapeDtypeStruct((S, D), F32)
    return pl.pallas_call(
        body, grid=(S // tm,), name="conv_bwd_rows",
        in_specs=[_rows(tm, D)] * 3 + [_resident((1, D)), _resident((1, D))],
        out_specs=[_rows(tm, D), _rows(tm, D), pl.BlockSpec((8, D), lambda i: (0, 0))],
        out_shape=[big, jax.ShapeDtypeStruct((S, D), BF16), jax.ShapeDtypeStruct((8, D), F32)],
        compiler_params=_params(("arbitrary",)),
    )(u, c_gate, dy_conv, ln_g, ln_b)


def _conv_bwd_taps(du, c_val, c_glu, conv_w):
    T = CONV_T
    last = S // T - 1

    def body(du_ref, dua_ref, cv_ref, cg_ref, cvh_ref, cgh_ref, w_ref, dcv_ref, dcg_ref, dw_ref,
             hwin, dwin, dhs, dw_acc):
        i = pl.program_id(0)

        @pl.when(i == 0)
        def _():
            dw_acc[...] = jnp.zeros_like(dw_acc)

        for cb in range(NCH):
            cols = slice(cb * LANES, (cb + 1) * LANES)
            hwin[cb, HALO:HALO + T, :] = cv_ref[:, cols] * _sigmoid(cg_ref[:, cols])
            hwin[cb, 0:HALO, :] = jnp.where(i > 0, cvh_ref[:, cols] * _sigmoid(cgh_ref[:, cols]), 0.0)
            dwin[cb, 0:T, :] = du_ref[:, cols]
            dwin[cb, T:T + HALO, :] = jnp.where(i < last, dua_ref[:, cols], 0.0)
        for cb in range(NCH):
            cols = slice(cb * LANES, (cb + 1) * LANES)
            taps = _taps(w_ref, cols)

            def group_dh(g, carry):
                for b in range(SUBLANES):
                    base = g * GROUP + b
                    acc = jnp.zeros((SUBLANES, LANES), F32)
                    for j in range(CONV_K):
                        acc = acc + taps[j] * _comb(dwin, cb, base + (CONV_K - 1 - j))
                    dhs[cb, pl.ds(base, SUBLANES, stride=SUBLANES), :] = acc
                return carry

            lax.fori_loop(0, T // GROUP, group_dh, 0)

            def group_dw(g, sums):
                for b in range(SUBLANES):
                    base = g * GROUP + b
                    d = _comb(dwin, cb, base)
                    sums = tuple(sums[j] + d * _comb(hwin, cb, base + (HALO - (CONV_K - 1) + j))
                                 for j in range(CONV_K))
                return sums

            sums = lax.fori_loop(0, T // GROUP, group_dw, tuple(dw_acc[j, :, cols] for j in range(CONV_K)))
            for j in range(CONV_K):
                dw_acc[j, :, cols] = sums[j]
            dh = dhs[cb]
            cv, sg = cv_ref[:, cols], _sigmoid(cg_ref[:, cols])
            dcv_ref[:, cols] = (dh * sg).astype(BF16)
            dcg_ref[:, cols] = (dh * cv * (sg * (1.0 - sg))).astype(BF16)

        @pl.when(i == last)
        def _():
            dw_ref[...] = jnp.zeros_like(dw_ref)
            for j in range(CONV_K):
                dw_ref[j:j + 1, :] = jnp.sum(dw_acc[j], axis=0, keepdims=True)

    before = pl.BlockSpec((HALO, D), _halo_before)
    after = pl.BlockSpec((HALO, D), _halo_after)
    big = jax.ShapeDtypeStruct((S, D), BF16)
    return pl.pallas_call(
        body, grid=(S // T,), name="conv_bwd_taps",
        in_specs=[_rows(T, D), after, _rows(T, D), _rows(T, D), before, before, _resident((HALO, D))],
        out_specs=[_rows(T, D), _rows(T, D), pl.BlockSpec((HALO, D), lambda i: (0, 0))],
        out_shape=[big, big, jax.ShapeDtypeStruct((HALO, D), F32)],
        scratch_shapes=[pltpu.VMEM((NCH, T + HALO, LANES), F32), pltpu.VMEM((NCH, T + HALO, LANES), F32),
                        pltpu.VMEM((NCH, T, LANES), F32), pltpu.VMEM((CONV_K, SUBLANES, D), F32)],
        compiler_params=_params(("arbitrary",)),
    )(du, du, c_val, c_glu, c_val, c_glu, conv_w)


def _outproj_loss(y_att, y_conv, w_out_bf, x, target, gf):
    tm = 256

    def body(ya_ref, yc_ref, w_ref, x_ref, t_ref, gf_ref, dx2_ref, dya_ref, dyc_ref, dw_ref, st_ref, acc):
        @pl.when(pl.program_id(0) == 0)
        def _():
            acc[...] = jnp.zeros_like(acc)
            st_ref[...] = jnp.zeros_like(st_ref)

        ya, yc = ya_ref[...], yc_ref[...]
        x2 = x_ref[...] + _dot(ya, w_ref[0:D, :]) + _dot(yc, w_ref[D:2 * D, :])
        r = lax.rsqrt(jnp.mean(x2 * x2, axis=-1, keepdims=True) + NORM_EPS)
        xn = x2 * r
        err = xn * gf_ref[...] - t_ref[...]
        dout = err * (1.0 / D)
        dxn = dout * gf_ref[...]
        dx2 = r * (dxn - xn * jnp.mean(dxn * xn, axis=-1, keepdims=True))
        dx2_ref[...] = dx2
        dx2b = dx2.astype(BF16)
        dya_ref[...] = _dot_nt(dx2b, w_ref[0:D, :])
        dyc_ref[...] = _dot_nt(dx2b, w_ref[D:2 * D, :])
        acc[0:D, :] += _dot_tn(ya, dx2b)
        acc[D:2 * D, :] += _dot_tn(yc, dx2b)
        st_ref[0:1, :] += jnp.sum(dout * xn, axis=0, keepdims=True)
        st_ref[1:2, :] += jnp.sum(err * err, axis=0, keepdims=True) * (0.5 / D)

        @pl.when(pl.program_id(0) == S // tm - 1)
        def _():
            dw_ref[...] = acc[...].astype(BF16)

    big = jax.ShapeDtypeStruct((S, D), F32)
    return pl.pallas_call(
        body, grid=(S // tm,), name="outproj_loss",
        in_specs=[_rows(tm, D), _rows(tm, D), _resident((WOUT_ROWS, D)), _rows(tm, D), _rows(tm, D), _resident((1, D))],
        out_specs=[_rows(tm, D), _rows(tm, D), _rows(tm, D),
                   pl.BlockSpec((WOUT_ROWS, D), lambda i: (0, 0)), pl.BlockSpec((8, D), lambda i: (0, 0))],
        out_shape=[big, big, big, jax.ShapeDtypeStruct((WOUT_ROWS, D), BF16), jax.ShapeDtypeStruct((8, D), F32)],
        scratch_shapes=[pltpu.VMEM((WOUT_ROWS, D), F32)],
        compiler_params=_params(("arbitrary",)),
    )(y_att, y_conv, w_out_bf, x, target, gf)


def _assemble_dproj(dq, dkv, dag, dcv, dcg, dcgate):
    tm = 512

    def body(dq_ref, dkv_ref, dag_ref, dcv_ref, dcg_ref, dcgate_ref, dp_ref):
        for sl in range(D // LANES):
            dp_ref[:, OFF_Q + sl * LANES:OFF_Q + (sl + 1) * LANES] = (dq_ref[sl] * (HD ** -0.5)).astype(BF16)
        for j in range(NKV):
            dkv_j = dkv_ref[j].astype(BF16)
            dp_ref[:, OFF_K + j * HD:OFF_K + (j + 1) * HD] = dkv_j[:, :HD]
            dp_ref[:, OFF_V + j * HD:OFF_V + (j + 1) * HD] = dkv_j[:, HD:]
        for off, ref in ((OFF_AG, dag_ref), (OFF_CV, dcv_ref), (OFF_CG, dcg_ref), (OFF_CGATE, dcgate_ref)):
            dp_ref[:, off:off + D] = ref[...]

    return pl.pallas_call(
        body, grid=(S // tm,), name="assemble_dproj",
        in_specs=[_slab_rows(D // LANES, tm), _slab_rows(NKV, tm)] + [_rows(tm, D)] * 4,
        out_specs=_rows(tm, NCOL),
        out_shape=jax.ShapeDtypeStruct((S, NCOL), BF16),
        compiler_params=_params(("arbitrary",)),
    )(dq, dkv, dag, dcv, dcg, dcgate)


def _inproj_bwd_x(dproj, w_bf, x, g1, dx2, pi, po, ps):
    tm = 256
    last = S // tm - 1

    def body(dp_ref, w_ref, x_ref, g_ref, dx2_ref, pi_ref, po_ref, ps_ref,
             gx_ref, st_ref, ri_ref, ro_ref, rs_ref, send, recv):
        i = pl.program_id(0)
        copies = _chip_exchange_copies((pi_ref, po_ref, ps_ref), (ri_ref, ro_ref, rs_ref), send, recv)

        @pl.when(i == 0)
        def _():
            st_ref[...] = jnp.zeros_like(st_ref)
            for out, _ in copies:
                out.start()

        dh = _dot_nt(dp_ref[...], w_ref[...])
        xt = x_ref[...]
        r = lax.rsqrt(jnp.mean(xt * xt, axis=-1, keepdims=True) + NORM_EPS)
        xn = xt * r
        dxn = dh * g_ref[...]
        gx_ref[...] = dx2_ref[...] + r * (dxn - xn * jnp.mean(dxn * xn, axis=-1, keepdims=True))
        st_ref[0:1, :] += jnp.sum(dh * xn, axis=0, keepdims=True)

        @pl.when(i == last)
        def _():
            for _, arrival in copies:
                arrival.wait_recv()
            for out, _ in copies:
                out.wait_send()

    n = 3 * len(CHIP_FLIPS)
    return pl.pallas_call(
        body, grid=(S // tm,), name="inproj_bwd_x",
        in_specs=[_rows(tm, NCOL), _resident((D, NCOL)), _rows(tm, D), _resident((1, D)), _rows(tm, D), ANY, ANY, ANY],
        out_specs=[_rows(tm, D), pl.BlockSpec((8, D), lambda i: (0, 0)), ANY, ANY, ANY],
        out_shape=[jax.ShapeDtypeStruct((S, D), F32), jax.ShapeDtypeStruct((8, D), F32),
                   jax.ShapeDtypeStruct((NCHIP, D // 2, CHUNK), BF16),
                   jax.ShapeDtypeStruct((NCHIP, WOUT_SHARD // 2, D), BF16),
                   jax.ShapeDtypeStruct((NCHIP, SMALL_ROWS, D), F32)],
        scratch_shapes=[pltpu.SemaphoreType.DMA((n,)), pltpu.SemaphoreType.DMA((n,))],
        compiler_params=_params(("arbitrary",)),
    )(dproj, w_bf, x, g1, dx2, pi, po, ps)


def _inproj_bwd_w(h, dproj):
    tk = 1024
    nk = S // tk

    def body(h_ref, dp_ref, o_ref, acc):
        i = pl.program_id(1)

        @pl.when(i == 0)
        def _():
            acc[...] = jnp.zeros_like(acc)

        acc[...] += _dot_tn(h_ref[...], dp_ref[...])

        @pl.when(i == nk - 1)
        def _():
            o_ref[0] = acc[...].astype(BF16)

    return pl.pallas_call(
        body, grid=(NCHIP, nk), name="inproj_bwd_w",
        in_specs=[pl.BlockSpec((tk, D), lambda c, i: (i, 0)), pl.BlockSpec((tk, CHUNK), lambda c, i: (i, c))],
        out_specs=pl.BlockSpec((1, D, CHUNK), lambda c, i: (c, 0, 0)),
        out_shape=jax.ShapeDtypeStruct((NCHIP, D, CHUNK), BF16),
        scratch_shapes=[pltpu.VMEM((D, CHUNK), F32)],
        compiler_params=_params(("arbitrary", "arbitrary")),
    )(h, dproj)


def _local_step(x, target, g1, w_in_bf, conv_w, conv_b, ln_g, ln_b, w_out_bf, gf):
    h, q, k, v, a_gate, c_val, c_glu, c_gate = _inproj_fwd(x, g1, w_in_bf)
    tables = [_bias_table(d) for d in PATTERNS]
    outs, lses = zip(*[_attn_fwd(q, k, v, t, d) for t, d in zip(tables, PATTERNS)])
    o, lse, y_att = _attn_combine(outs, lses, a_gate)
    u, y_conv = _conv_fwd(c_val, c_glu, c_gate, conv_w, conv_b, ln_g, ln_b)
    dx2, dy_att, dy_conv, dw_out, st_out = _outproj_loss(y_att, y_conv, w_out_bf, x, target, gf)

    do, da_gate, delta = _attn_gate_bwd(dy_att, o, a_gate, _head_sum_selectors())
    attn_grads = None
    for t, d in zip(tables, PATTERNS):
        attn_grads = _attn_bwd(q, k, v, do, lse, delta, t, d, attn_grads)

    du, dc_gate, st_conv = _conv_bwd_rows(u, c_gate, dy_conv, ln_g, ln_b)
    dc_val, dc_glu, dconv_w = _conv_bwd_taps(du, c_val, c_glu, conv_w)

    dproj = _assemble_dproj(*attn_grads, da_gate, dc_val, dc_glu, dc_gate)
    dw_in = _inproj_bwd_w(h, dproj)
    small = jnp.concatenate([st_conv, st_out, dconv_w], axis=0)
    return dw_in, dw_out, small, dproj, dx2


ROW_LN_G, ROW_LN_B, ROW_CONV_B, ROW_FINAL_G, ROW_LOSS, ROW_TAPS = 0, 1, 2, 8, 9, 16
SMALL_ROWS = 16 + HALO
NDEV = 8


MESH = pl.DeviceIdType.MESH
ANY = pl.BlockSpec(memory_space=pl.ANY)
CHIP_FLIPS = ((1, 0), (0, 1), (1, 1))


def _pos():
    return lax.axis_index("x"), lax.axis_index("y"), lax.axis_index("c")


def _flip(v, f):
    return 1 - v if f else v


def _ds(start, size, align=None):
    return pl.ds(pl.multiple_of(start, align or size), size)


def _place_shards(wi, wo, cw, where):
    steps = 4

    def body(where_ref, wi_ref, wo_ref, cw_ref, wi_full, wo_full, cw_full):
        wi_full[...] = wi_ref[...].astype(BF16)
        wo_full[...] = wo_ref[...].astype(BF16)
        cw_full[...] = cw_ref[...]

    grid_spec = pltpu.PrefetchScalarGridSpec(
        num_scalar_prefetch=1, grid=(steps,),
        in_specs=[pl.BlockSpec((D // steps, CHUNK), lambda i, w: (i, 0)),
                  pl.BlockSpec((WOUT_SHARD // steps, D), lambda i, w: (i, 0)),
                  pl.BlockSpec((HALO, CONVW_SHARD), lambda i, w: (0, 0))],
        out_specs=[pl.BlockSpec((D // steps, CHUNK), lambda i, w: (i, w[0])),
                   pl.BlockSpec((WOUT_SHARD // steps, D), lambda i, w: (w[0] * steps + i, 0)),
                   pl.BlockSpec((HALO, CONVW_SHARD), lambda i, w: (0, w[0]))])
    return pl.pallas_call(
        body, grid_spec=grid_spec, name="place_shards",
        out_shape=[jax.ShapeDtypeStruct((D, NCOL), BF16), jax.ShapeDtypeStruct((WOUT_ROWS, D), BF16),
                   jax.ShapeDtypeStruct((HALO, D), F32)],
        compiler_params=_params(("arbitrary",)),
    )(where, wi, wo, cw)


def _gather_weights(wi_full, wo_full, cw_full):
    halves = (D // 2, WOUT_SHARD // 2, HALO // 2)
    OWN_X, OWN_Y, VIA_Y, VIA_X = range(4)

    def body(_wi, _wo, _cw, wi_full, wo_full, cw_full, send, recv):
        x, y, c = _pos()
        x_nbr, y_nbr, diag = (1 - x, y), (x, 1 - y), (1 - x, 1 - y)

        def region(a, chip_xy, half, part=None):
            chip = 2 * chip_xy[0] + chip_xy[1]
            n, row = halves[a], half * halves[a]
            if part is not None:
                n = n // 2
                row = row + part * n
            if a == 0:
                return wi_full.at[_ds(row, n), _ds(chip * CHUNK, CHUNK, 128)]
            if a == 1:
                return wo_full.at[_ds(chip * WOUT_SHARD + row, n), :]
            return cw_full.at[_ds(row, n), _ds(chip * CONVW_SHARD, CONVW_SHARD, 128)]

        def copy(a, kind, piece, dev):
            k = 8 * a + kind
            return pltpu.make_async_remote_copy(src_ref=piece, dst_ref=piece, send_sem=send.at[k], recv_sem=recv.at[k],
                                                device_id=dev, device_id_type=MESH)

        def to_sibling(a, kind, piece):
            cp = copy(a, 4 + kind, piece, (x, y, 1 - c))
            cp.start()
            return cp

        sends = []
        for a in range(3):
            for kind, nbr in ((OWN_X, x_nbr), (OWN_Y, y_nbr)):
                cp = copy(a, kind, region(a, (x, y), c), (*nbr, c))
                cp.start()
                sends.append(cp)
        for a in range(3):
            got = region(a, x_nbr, c)
            copy(a, OWN_X, got, (*x_nbr, c)).wait_recv()
            onward = copy(a, VIA_Y, region(a, x_nbr, c, 0), (*y_nbr, c))
            onward.start()
            sends += [onward, to_sibling(a, OWN_X, got)]
            got = region(a, y_nbr, c)
            copy(a, OWN_Y, got, (*y_nbr, c)).wait_recv()
            onward = copy(a, VIA_X, region(a, y_nbr, c, 1), (*x_nbr, c))
            onward.start()
            sends += [onward, to_sibling(a, OWN_Y, got)]
        for a in range(3):
            got = region(a, diag, c, 0)
            copy(a, VIA_Y, got, (*y_nbr, c)).wait_recv()
            sends.append(to_sibling(a, VIA_Y, got))
            got = region(a, diag, c, 1)
            copy(a, VIA_X, got, (*x_nbr, c)).wait_recv()
            sends.append(to_sibling(a, VIA_X, got))
        for a in range(3):
            for kind, piece in ((OWN_X, region(a, x_nbr, 1 - c)), (OWN_Y, region(a, y_nbr, 1 - c)),
                                (VIA_Y, region(a, diag, 1 - c, 0)), (VIA_X, region(a, diag, 1 - c, 1))):
                copy(a, 4 + kind, piece, (x, y, 1 - c)).wait_recv()
        for cp in sends:
            cp.wait_send()

    n_sems = 3 * 8
    return pl.pallas_call(
        body, name="gather_weights",
        in_specs=[ANY, ANY, ANY], out_specs=[ANY, ANY, ANY], input_output_aliases={0: 0, 1: 1, 2: 2},
        out_shape=[jax.ShapeDtypeStruct((D, NCOL), BF16), jax.ShapeDtypeStruct((WOUT_ROWS, D), BF16),
                   jax.ShapeDtypeStruct((HALO, D), F32)],
        scratch_shapes=[pltpu.SemaphoreType.DMA((n_sems,)), pltpu.SemaphoreType.DMA((n_sems,))],
    )(wi_full, wo_full, cw_full)


def _exchange_halves(gi4, go4, small):
    def body(gi_ref, go_ref, sm_ref, ri_ref, ro_ref, rs_ref, send, recv):
        x, y, c = _pos()
        sib = (x, y, 1 - c)
        copies = [
            (gi_ref.at[:, _ds((1 - c) * (D // 2), D // 2), :], ri_ref),
            (go_ref.at[:, _ds((1 - c) * (WOUT_SHARD // 2), WOUT_SHARD // 2), :], ro_ref),
            (sm_ref, rs_ref),
        ]
        cps = [pltpu.make_async_remote_copy(src_ref=s_, dst_ref=d_, send_sem=send.at[k], recv_sem=recv.at[k],
                                            device_id=sib, device_id_type=MESH) for k, (s_, d_) in enumerate(copies)]
        for cp in cps:
            cp.start()
        for cp in cps:
            cp.wait()

    return pl.pallas_call(
        body, name="exchange_halves",
        in_specs=[ANY, ANY, ANY], out_specs=[ANY, ANY, ANY],
        out_shape=[jax.ShapeDtypeStruct((NCHIP, D // 2, CHUNK), BF16),
                   jax.ShapeDtypeStruct((NCHIP, WOUT_SHARD // 2, D), BF16),
                   jax.ShapeDtypeStruct((SMALL_ROWS, D), F32)],
        scratch_shapes=[pltpu.SemaphoreType.DMA((3,)), pltpu.SemaphoreType.DMA((3,))],
    )(gi4, go4, small)


def _add_halves(gi4, ri, go4, ro, small, rs):
    hi, ho = D // 2, WOUT_SHARD // 2

    def body(gi_ref, ri_ref, go_ref, ro_ref, sm_ref, rs_ref, pi_ref, po_ref, ps_ref):
        c = lax.axis_index("c")
        pi_ref[0] = (gi_ref[0, _ds(c * hi, hi), :].astype(F32) + ri_ref[0].astype(F32)).astype(BF16)
        po_ref[0] = (go_ref[0, _ds(c * ho, ho), :].astype(F32) + ro_ref[0].astype(F32)).astype(BF16)
        ps_ref[...] = sm_ref[...] + rs_ref[...]

    blk = lambda n, w: pl.BlockSpec((1, n, w), lambda k: (k, 0, 0))
    whole = pl.BlockSpec((SMALL_ROWS, D), lambda k: (0, 0))
    return pl.pallas_call(
        body, grid=(NCHIP,), name="add_halves",
        in_specs=[blk(D, CHUNK), blk(hi, CHUNK), blk(WOUT_SHARD, D), blk(ho, D), whole, whole],
        out_specs=[blk(hi, CHUNK), blk(ho, D), whole],
        out_shape=[jax.ShapeDtypeStruct((NCHIP, hi, CHUNK), BF16), jax.ShapeDtypeStruct((NCHIP, ho, D), BF16),
                   jax.ShapeDtypeStruct((SMALL_ROWS, D), F32)],
        compiler_params=_params(("arbitrary",)),
    )(gi4, ri, go4, ro, small, rs)


def _chip_exchange_copies(srcs, dsts, send, recv):
    x, y, c = _pos()
    me = 2 * x + y
    pairs = []
    for a in range(3):
        for j, (fx, fy) in enumerate(CHIP_FLIPS):
            px, py = _flip(x, fx), _flip(y, fy)
            peer = 2 * px + py
            k = 3 * a + j
            out = pltpu.make_async_remote_copy(
                src_ref=srcs[a] if a == 2 else srcs[a].at[peer], dst_ref=dsts[a].at[me],
                send_sem=send.at[k], recv_sem=recv.at[k], device_id=(px, py, c), device_id_type=MESH)
            got = dsts[a].at[peer]
            arrival = pltpu.make_async_remote_copy(
                src_ref=got, dst_ref=got, send_sem=send.at[k], recv_sem=recv.at[k],
                device_id=(px, py, c), device_id_type=MESH)
            pairs.append((out, arrival))
    return pairs


def _sum_chips(ri, ro, rs, pi, po, ps, where):
    def body(w_ref, ri_ref, ro_ref, rs_ref, pi_ref, po_ref, ps_ref, gi_ref, go_ref, gs_ref, g5_ref, loss_ref,
             acc_i, acc_o, acc_s):
        k = pl.program_id(0)
        accs = (acc_i, acc_o, acc_s)

        @pl.when(k == 0)
        def _():
            for acc in accs:
                acc[...] = jnp.zeros_like(acc)

        @pl.when(k == w_ref[0])
        def _():
            for acc, val in zip(accs, (pi_ref[0], po_ref[0], ps_ref[...])):
                acc[...] += val.astype(F32)

        @pl.when(k != w_ref[0])
        def _():
            for acc, ref in zip(accs, (ri_ref, ro_ref, rs_ref)):
                acc[...] += ref[0].astype(F32)

        @pl.when(k == NCHIP - 1)
        def _():
            gi_ref[0] = acc_i[...]
            go_ref[0] = acc_o[...]
            gs_ref[...] = acc_s[...]
            g5_ref[...] = jnp.zeros_like(g5_ref)
            for i, row in enumerate((ROW_CONV_B, ROW_LN_G, ROW_LN_B, ROW_FINAL_G)):
                g5_ref[i + 1:i + 2, :] = acc_s[row:row + 1, :]
            loss = jnp.sum(acc_s[ROW_LOSS:ROW_LOSS + 1, :], axis=1, keepdims=True)
            loss_ref[...] = jnp.broadcast_to(loss, loss_ref.shape)

    def sent(k, w):
        return jnp.where(k == w[0], (k + 1) % NCHIP, k)

    hi, ho = D // 2, WOUT_SHARD // 2
    const = lambda shape: pl.BlockSpec(shape, lambda k, w: (0,) * len(shape))
    grid_spec = pltpu.PrefetchScalarGridSpec(
        num_scalar_prefetch=1, grid=(NCHIP,),
        in_specs=[pl.BlockSpec((1, hi, CHUNK), lambda k, w: (sent(k, w), 0, 0)),
                  pl.BlockSpec((1, ho, D), lambda k, w: (sent(k, w), 0, 0)),
                  pl.BlockSpec((1, SMALL_ROWS, D), lambda k, w: (sent(k, w), 0, 0)),
                  pl.BlockSpec((1, hi, CHUNK), lambda k, w: (w[0], 0, 0)),
                  pl.BlockSpec((1, ho, D), lambda k, w: (w[0], 0, 0)),
                  const((SMALL_ROWS, D))],
        out_specs=[pl.BlockSpec((1, hi, CHUNK), lambda k, w: (w[1], 0, 0)),
                   pl.BlockSpec((1, ho, D), lambda k, w: (w[1], 0, 0)),
                   const((SMALL_ROWS, D)), const((8, D)), const((8, LANES))],
        scratch_shapes=[pltpu.VMEM((hi, CHUNK), F32), pltpu.VMEM((ho, D), F32), pltpu.VMEM((SMALL_ROWS, D), F32)])
    return pl.pallas_call(
        body, grid_spec=grid_spec, name="sum_chips",
        out_shape=[jax.ShapeDtypeStruct((2, hi, CHUNK), F32), jax.ShapeDtypeStruct((2, ho, D), F32),
                   jax.ShapeDtypeStruct((SMALL_ROWS, D), F32), jax.ShapeDtypeStruct((8, D), F32),
                   jax.ShapeDtypeStruct((8, LANES), F32)],
        compiler_params=_params(("arbitrary",)),
    )(where, ri, ro, rs, pi, po, ps)


def _in_place(a):
    def body(a_ref, o_ref):
        del a_ref, o_ref

    return pl.pallas_call(body, name="in_place", in_specs=[ANY], out_specs=ANY, input_output_aliases={0: 0},
                          out_shape=jax.ShapeDtypeStruct(a.shape, a.dtype))(a)


def _exchange_results(gi2, go2, st):
    flips = [(fx, fy, fc) for fx in (0, 1) for fy in (0, 1) for fc in (0, 1)][1:]

    def body(_gi, _go, st_ref, gi_ref, go_ref, all_ref, send, recv, lsem):
        x, y, c = _pos()
        sib = (x, y, 1 - c)

        def half(k, ref, slot):
            return pltpu.make_async_remote_copy(src_ref=ref.at[slot], dst_ref=ref.at[slot], send_sem=send.at[k],
                                                recv_sem=recv.at[k], device_id=sib, device_id_type=MESH)

        def stat(k, src, slot, dev):
            return pltpu.make_async_remote_copy(src_ref=src, dst_ref=all_ref.at[slot], send_sem=send.at[k],
                                                recv_sem=recv.at[k], device_id=dev, device_id_type=MESH)

        mine = pltpu.make_async_copy(st_ref, all_ref.at[4 * x + 2 * y + c], lsem)
        mine.start()
        sends = [half(k, ref, c) for k, ref in enumerate((gi_ref, go_ref))]
        peers = [(_flip(x, fx), _flip(y, fy), _flip(c, fc)) for fx, fy, fc in flips]
        sends += [stat(2 + k, st_ref, 4 * x + 2 * y + c, dev) for k, dev in enumerate(peers)]
        for cp in sends:
            cp.start()
        for k, ref in enumerate((gi_ref, go_ref)):
            half(k, ref, 1 - c).wait_recv()
        for k, (px, py, pc) in enumerate(peers):
            slot = 4 * px + 2 * py + pc
            stat(2 + k, all_ref.at[slot], slot, (px, py, pc)).wait_recv()
        for cp in sends:
            cp.wait_send()
        mine.wait()

    n = 2 + len(flips)
    return pl.pallas_call(
        body, name="exchange_results",
        in_specs=[ANY, ANY, ANY], out_specs=[ANY, ANY, ANY], input_output_aliases={0: 0, 1: 1},
        out_shape=[jax.ShapeDtypeStruct((2, D // 2, CHUNK), F32), jax.ShapeDtypeStruct((2, WOUT_SHARD // 2, D), F32),
                   jax.ShapeDtypeStruct((NDEV, 8, D), F32)],
        scratch_shapes=[pltpu.SemaphoreType.DMA((n,)), pltpu.SemaphoreType.DMA((n,)), pltpu.SemaphoreType.DMA],
    )(gi2, go2, st)


def _adamw_math(w, g, m, v):
    m2 = ADAM_B1 * m + (1.0 - ADAM_B1) * g
    v2 = ADAM_B2 * v + (1.0 - ADAM_B2) * (g * g)
    m_hat = m2 / (1.0 - ADAM_B1 ** ADAM_STEP)
    v_hat = v2 / (1.0 - ADAM_B2 ** ADAM_STEP)
    delta = -ADAM_LR * (m_hat / (jnp.sqrt(v_hat) + ADAM_EPS) + ADAM_WD * w)
    return delta, m2, v2


def _adamw(w, g, m, v, name):
    rows, cols = w.shape
    tm = 256 if rows % 256 == 0 else rows

    def body(w_ref, g_ref, m_ref, v_ref, d_ref, m2_ref, v2_ref):
        d_ref[...], m2_ref[...], v2_ref[...] = _adamw_math(w_ref[...], g_ref[...], m_ref[...], v_ref[...])

    shape = jax.ShapeDtypeStruct(w.shape, F32)
    return pl.pallas_call(
        body, grid=(rows // tm,), name=name,
        in_specs=[_rows(tm, cols)] * 4, out_specs=[_rows(tm, cols)] * 3, out_shape=[shape] * 3,
        compiler_params=_params(("arbitrary",)),
    )(w, g, m, v)


def _adamw_vectors(g5, first_parts, ws, ms, vs):
    n = len(ws)

    def body(g_ref, parts_ref, *refs):
        ins, g0_ref, outs = refs[:3 * n], refs[3 * n], refs[3 * n + 1:]
        g0 = parts_ref[0, 0:1, :]
        for dev in range(1, NDEV):
            g0 = g0 + parts_ref[dev, 0:1, :]
        g0_ref[...] = g0
        for i in range(n):
            g = g0 if i == 0 else g_ref[i:i + 1, :]
            res = _adamw_math(ins[i][...], g, ins[n + i][...], ins[2 * n + i][...])
            for kind in range(3):
                outs[kind * n + i][...] = res[kind]

    shape = jax.ShapeDtypeStruct((1, D), F32)
    return pl.pallas_call(body, name="adamw_vectors", out_shape=[shape] * (1 + 3 * n), compiler_params=_params())(
        g5, first_parts, *ws, *ms, *vs)


def kernel(x, norm_g, w_in, conv_w, conv_b, conv_ln_g, conv_ln_b, w_out, final_norm_g, loss_target, m_norm_g, m_w_in, m_conv_w, m_conv_b, m_conv_ln_g, m_conv_ln_b, m_w_out, m_final_norm_g, v_norm_g, v_w_in, v_conv_w, v_conv_b, v_conv_ln_g, v_conv_ln_b, v_w_out, v_final_norm_g):
    chip = 2 * lax.axis_index("x") + lax.axis_index("y")
    where = jnp.stack([chip, lax.axis_index("c")]).astype(jnp.int32)
    taps_shard = jnp.pad(conv_w[0], ((0, HALO - CONV_K), (0, 0)))
    wi_full, wo_full, cw_full = _gather_weights(*_place_shards(w_in[0], w_out[0], taps_shard, where))

    gf = final_norm_g[None]
    dw_in4, dw_out, small, dproj, dx2 = _local_step(
        x[0], loss_target[0], norm_g, wi_full, cw_full, conv_b, conv_ln_g, conv_ln_b, wo_full, gf)
    dw_out4 = dw_out.reshape(NCHIP, WOUT_SHARD, D)

    ri, ro, rs = _exchange_halves(dw_in4, dw_out4, small)
    pi, po, ps = _add_halves(dw_in4, ri, dw_out4, ro, small, rs)
    grad_x, st_in, ri, ro, rs = _inproj_bwd_x(dproj, wi_full, x[0], norm_g, dx2, pi, po, ps)
    grad_x = _in_place(grad_x)
    gi2, go2, g_small, g5, loss8 = _sum_chips(ri, ro, rs, pi, po, ps, where)
    gi2, go2, norm_g_parts = _exchange_results(gi2, go2, st_in)
    g_w_in = gi2.reshape(D, CHUNK)
    g_w_out = go2.reshape(WOUT_SHARD, D)
    g_taps = lax.dynamic_slice(g_small, (ROW_TAPS, chip * CONVW_SHARD), (CONV_K, CONVW_SHARD))

    d_w_in, m2_w_in, v2_w_in = _adamw(w_in[0], g_w_in, m_w_in[0], v_w_in[0], "adamw_w_in")
    d_w_out, m2_w_out, v2_w_out = _adamw(w_out[0], g_w_out, m_w_out[0], v_w_out[0], "adamw_w_out")
    d_taps, m2_taps, v2_taps = _adamw(conv_w[0], g_taps, m_conv_w[0], v_conv_w[0], "adamw_conv_w")
    g_norm, *vec = _adamw_vectors(
        g5, norm_g_parts,
        (norm_g, conv_b, conv_ln_g, conv_ln_b, gf),
        (m_norm_g, m_conv_b, m_conv_ln_g, m_conv_ln_b, m_final_norm_g[None]),
        (v_norm_g, v_conv_b, v_conv_ln_g, v_conv_ln_b, v_final_norm_g[None]))
    d_vec, m2_vec, v2_vec = vec[0:5], vec[5:10], vec[10:15]

    def weight_order(ng, wi, cw, cb, lg, lb, wo, fg):
        return (ng, wi[None], cw[None], cb, lg, lb, wo[None], fg[0])

    grads = weight_order(g_norm, g_w_in, g_taps, g5[1:2], g5[2:3], g5[3:4], g_w_out, g5[4:5])
    deltas = weight_order(d_vec[0], d_w_in, d_taps, d_vec[1], d_vec[2], d_vec[3], d_w_out, d_vec[4])
    new_m = weight_order(m2_vec[0], m2_w_in, m2_taps, m2_vec[1], m2_vec[2], m2_vec[3], m2_w_out, m2_vec[4])
    new_v = weight_order(v2_vec[0], v2_w_in, v2_taps, v2_vec[1], v2_vec[2], v2_vec[3], v2_w_out, v2_vec[4])
    return (loss8[0, 0], grad_x[None], *grads, *deltas, *new_m, *new_v)
```

```python
import jax
import jax.numpy as jnp
from jax import lax
from jax.experimental import pallas as pl
from jax.experimental.pallas import tpu as pltpu

F32 = jnp.float32
BF16 = jnp.bfloat16

S = 4096
D = 1024
LANES = 128
HD = 64
NKV = 4
GQ = 4
KVW = NKV * HD
NCOL = 5632
CONV_K = 31
HALO = 32
BLK = 128
PATTERNS = (1, 4, 16)
NORM_EPS = 1e-6
LN_EPS = 1e-5
NEG = -1e30
OFF_Q, OFF_K, OFF_V, OFF_AG, OFF_CV, OFF_CG, OFF_CGATE = 0, 1024, 1280, 1536, 2560, 3584, 4608
NCHIP = 4
CHUNK = NCOL // NCHIP
WOUT_ROWS = 2 * D
WOUT_SHARD = WOUT_ROWS // NCHIP
CONVW_SHARD = D // NCHIP

ADAM_LR, ADAM_B1, ADAM_B2, ADAM_EPS, ADAM_WD, ADAM_STEP = 0.001, 0.9, 0.999, 1e-08, 0.01, 10

VMEM_LIMIT = 56 * 1024 * 1024


def _params(sem=None, vmem=VMEM_LIMIT):
    return pltpu.CompilerParams(dimension_semantics=sem, vmem_limit_bytes=vmem)


def _sigmoid(a):
    return 0.5 * jnp.tanh(0.5 * a) + 0.5


def _rows(tm, width):
    return pl.BlockSpec((tm, width), lambda i: (i, 0))


def _slabs(n):
    return jax.ShapeDtypeStruct((n, S, LANES), F32)


def _slab_rows(n, tm):
    return pl.BlockSpec((n, tm, LANES), lambda i: (0, i, 0))


def _resident(shape):
    return pl.BlockSpec(shape, lambda *_: (0,) * len(shape), pipeline_mode=pl.Buffered(1))


def _dot(a, b):
    return jnp.dot(a, b, preferred_element_type=F32)


def _dot_nt(a, b):
    return lax.dot_general(a, b, (((1,), (1,)), ((), ())), preferred_element_type=F32)


def _dot_tn(a, b):
    return lax.dot_general(a, b, (((0,), (0,)), ((), ())), preferred_element_type=F32)


def _inproj_fwd(x, g1, w_bf):
    tm = 512

    def body(x_ref, g_ref, w_ref, h_ref, q_ref, k_ref, v_ref, ag_ref, cv_ref, cg_ref, cgate_ref):
        xt = x_ref[...]
        r = lax.rsqrt(jnp.mean(xt * xt, axis=-1, keepdims=True) + NORM_EPS)
        h = (xt * r * g_ref[...]).astype(BF16)
        h_ref[...] = h
        q = _dot(h, w_ref[:, OFF_Q:OFF_Q + D]) * (HD ** -0.5)
        kv = _dot(h, w_ref[:, OFF_K:OFF_K + 2 * KVW])
        for sl in range(D // LANES):
            q_ref[sl] = q[:, sl * LANES:(sl + 1) * LANES]
        for sl in range(KVW // LANES):
            k_ref[sl] = kv[:, sl * LANES:(sl + 1) * LANES]
            v_ref[sl] = kv[:, KVW + sl * LANES:KVW + (sl + 1) * LANES]
        ag_ref[...] = _dot(h, w_ref[:, OFF_AG:OFF_AG + D])
        cv_ref[...] = _dot(h, w_ref[:, OFF_CV:OFF_CV + D])
        cg_ref[...] = _dot(h, w_ref[:, OFF_CG:OFF_CG + D])
        cgate_ref[...] = _dot(h, w_ref[:, OFF_CGATE:OFF_CGATE + D])

    big = jax.ShapeDtypeStruct((S, D), F32)
    return pl.pallas_call(
        body, grid=(S // tm,), name="inproj_fwd",
        in_specs=[_rows(tm, D), _resident((1, D)), _resident((D, NCOL))],
        out_specs=[_rows(tm, D), _slab_rows(D // LANES, tm), _slab_rows(KVW // LANES, tm), _slab_rows(KVW // LANES, tm),
                   _rows(tm, D), _rows(tm, D), _rows(tm, D), _rows(tm, D)],
        out_shape=[jax.ShapeDtypeStruct((S, D), BF16), _slabs(D // LANES), _slabs(KVW // LANES), _slabs(KVW // LANES),
                   big, big, big, big],
        compiler_params=_params(("arbitrary",)),
    )(x, g1, w_bf)


def _bias_table(d):
    h = jnp.arange(NKV * GQ, dtype=F32)
    slopes = jnp.exp2(-8.0 * (h + 1.0) / (NKV * GQ))
    qi = jnp.arange(BLK)[:, None]
    kj = jnp.arange(2 * BLK)[None, :]
    dist = BLK + qi - kj
    window = (dist >= 0) & (dist <= BLK)
    bias = -slopes[:, None, None] * (dist * d).astype(F32)[None]
    has_prev = jnp.stack([jnp.broadcast_to(kj >= BLK, (BLK, 2 * BLK)), jnp.ones((BLK, 2 * BLK), bool)])
    valid = window[None] & has_prev
    tab = jnp.where(valid[:, None], bias[None], NEG)
    return tab.reshape(2, NKV, GQ * BLK, 2 * BLK)


def _sub_rows(start, d):
    if d == 1:
        return pl.ds(pl.multiple_of(start, BLK), BLK)
    return pl.ds(start, BLK, stride=d)


NHEAD = NKV * GQ
CHUNK_ROWS = 2048
BLOCKS_PER_CHUNK = CHUNK_ROWS // BLK


def _low_lanes(rows=BLK):
    return lax.broadcasted_iota(jnp.int32, (rows, LANES), 1) < HD


def _block_start(idx, d):
    shift = d.bit_length() - 1
    b, r = lax.shift_right_logical(idx, shift), lax.bitwise_and(idx, d - 1)
    start = b * (BLK * d) + r
    return b, start, jnp.maximum(start - BLK * d, r)


def _stack_heads(ref, rows):
    low = _low_lanes()
    t0, t1 = ref[0, rows, :], ref[1, rows, :]
    return jnp.concatenate([jnp.where(low, t0, 0.0), jnp.where(low, 0.0, t0),
                            jnp.where(low, t1, 0.0), jnp.where(low, 0.0, t1)], axis=0).astype(BF16)


def _unstack_heads(dup):
    low = _low_lanes()
    return (jnp.where(low, dup[0:BLK], dup[BLK:2 * BLK]), jnp.where(low, dup[2 * BLK:3 * BLK], dup[3 * BLK:4 * BLK]))


def _kv_dup(ref, prow, rows, odd):
    t = jnp.concatenate([ref[0, prow, :], ref[0, rows, :]], axis=0)
    swapped = pltpu.roll(t, HD, axis=1)
    keep = jnp.logical_xor(_low_lanes(2 * BLK), odd)
    return jnp.where(keep, t, swapped).astype(BF16)


def _attn_fwd(q, k, v, bias, d):
    def body(q_ref, k_ref, v_ref, b_ref, o_ref, l_ref):
        odd = pl.program_id(0) % 2 == 1
        ones = jnp.ones((2 * BLK, LANES), BF16)

        def block(idx, carry):
            b, start, pstart = _block_start(idx, d)
            rows, prow = _sub_rows(start, d), _sub_rows(pstart, d)
            qs = _stack_heads(q_ref, rows)
            kw = _kv_dup(k_ref, prow, rows, odd)
            vw = _kv_dup(v_ref, prow, rows, odd)
            s = _dot_nt(qs, kw) + b_ref[jnp.minimum(b, 1), 0]
            m = jnp.max(s, axis=1, keepdims=True)
            p = jnp.exp(s - m).astype(BF16)
            ol = _dot(p, jnp.concatenate([vw, ones], axis=1))
            l = ol[:, LANES:]
            o_ref[0, rows, :], o_ref[1, rows, :] = _unstack_heads(ol[:, :LANES] / l)
            lse = m + jnp.log(l)
            for g in range(GQ):
                l_ref[g, rows, :] = lse[g * BLK:(g + 1) * BLK]
            return carry

        lax.fori_loop(0, S // BLK, block, 0, unroll=2)

    q_like = pl.BlockSpec((2, S, LANES), lambda j: (j, 0, 0))
    kv = pl.BlockSpec((1, S, LANES), lambda j: (j // 2, 0, 0))
    heads = pl.BlockSpec((GQ, S, LANES), lambda j: (j, 0, 0))
    bias_spec = pl.BlockSpec((2, 1, GQ * BLK, 2 * BLK), lambda j: (0, j, 0, 0))
    return pl.pallas_call(
        body, grid=(NKV,), name=f"attn_fwd_d{d}",
        in_specs=[q_like, kv, kv, bias_spec],
        out_specs=[q_like, heads],
        out_shape=[_slabs(D // LANES), _slabs(NHEAD)],
        compiler_params=_params(("arbitrary",)),
    )(q, k, v, bias)


def _attn_combine(outs, lses, a_gate):
    tm = 256

    def body(o1, o2, o3, l1, l2, l3, ag_ref, o_ref, lse_ref, y_ref):
        low = _low_lanes(tm)
        for sl in range(D // LANES):
            w = []
            for h in (2 * sl, 2 * sl + 1):
                a, b, c = l1[h], l2[h], l3[h]
                m = jnp.maximum(jnp.maximum(a, b), c)
                ea, eb, ec = jnp.exp(a - m), jnp.exp(b - m), jnp.exp(c - m)
                den = ea + eb + ec
                lse_ref[h] = m + jnp.log(den)
                inv = 1.0 / den
                w.append((ea * inv, eb * inv, ec * inv))
            wa, wb, wc = (jnp.where(low, w[0][i], w[1][i]) for i in range(3))
            o = wa * o1[sl] + wb * o2[sl] + wc * o3[sl]
            o_ref[sl] = o
            cols = slice(sl * LANES, (sl + 1) * LANES)
            ag = ag_ref[:, cols]
            y_ref[:, cols] = (o * (ag * _sigmoid(ag))).astype(BF16)

    wide, per_head = _slab_rows(D // LANES, tm), _slab_rows(NHEAD, tm)
    return pl.pallas_call(
        body, grid=(S // tm,), name="attn_combine",
        in_specs=[wide] * 3 + [per_head] * 3 + [_rows(tm, D)],
        out_specs=[wide, per_head, _rows(tm, D)],
        out_shape=[_slabs(D // LANES), _slabs(NHEAD), jax.ShapeDtypeStruct((S, D), BF16)],
        compiler_params=_params(("arbitrary",)),
    )(*outs, *lses, a_gate)


def _head_sum_selectors():
    lane_in = jnp.arange(LANES)[:, None] // HD
    return jnp.stack([jnp.broadcast_to(lane_in == h, (LANES, LANES)) for h in range(2)]).astype(BF16)


def _attn_gate_bwd(dy_att, o, a_gate, selectors):
    tm = 256

    def body(dy_ref, o_ref, ag_ref, e_ref, do_ref, dag_ref, delta_ref):
        for sl in range(D // LANES):
            cols = slice(sl * LANES, (sl + 1) * LANES)
            dy, ag, o_ = dy_ref[:, cols], ag_ref[:, cols], o_ref[sl]
            sg = _sigmoid(ag)
            do = dy * (ag * sg)
            do_ref[sl] = do
            dag_ref[:, cols] = (dy * o_ * (sg * (1.0 + ag * (1.0 - sg)))).astype(BF16)
            prod = do * o_
            hi = prod.astype(BF16)
            lo = (prod - hi.astype(F32)).astype(BF16)
            for h in range(2):
                delta_ref[2 * sl + h] = _dot(hi, e_ref[h]) + _dot(lo, e_ref[h])

    return pl.pallas_call(
        body, grid=(S // tm,), name="attn_gate_bwd",
        in_specs=[_rows(tm, D), _slab_rows(D // LANES, tm), _rows(tm, D), _resident((2, LANES, LANES))],
        out_specs=[_slab_rows(D // LANES, tm), _rows(tm, D), _slab_rows(NHEAD, tm)],
        out_shape=[_slabs(D // LANES), jax.ShapeDtypeStruct((S, D), BF16), _slabs(NHEAD)],
        compiler_params=_params(("arbitrary",)),
    )(dy_att, o, a_gate, selectors)


def _attn_bwd(q, k, v, do, lse, delta, bias, d, dkv_so_far=None):
    def body(q_ref, do_ref, l_ref, dl_ref, k_ref, v_ref, b_ref, *rest):
        if dkv_so_far is None:
            dq_ref, dkv_ref = rest
        else:
            dkv0_ref, dq_ref, dkv_ref = rest
        j = pl.program_id(0)
        odd = j % 2 == 1
        chunk = pl.program_id(1)

        @pl.when(chunk == 0)
        def _():
            if dkv_so_far is None:
                dkv_ref[...] = jnp.zeros_like(dkv_ref)
            else:
                pltpu.sync_copy(dkv0_ref.at[pl.ds(j, 1)], dkv_ref)

        def block(idx, carry):
            b, start, pstart = _block_start(chunk * BLOCKS_PER_CHUNK + idx, d)
            rows, prow = _sub_rows(start, d), _sub_rows(pstart, d)
            mine = _sub_rows(start - chunk * CHUNK_ROWS, d)
            qs = _stack_heads(q_ref, mine)
            dos = _stack_heads(do_ref, mine)
            lse_t = jnp.concatenate([l_ref[g, mine, :] for g in range(GQ)], axis=0)
            delta_t = jnp.concatenate([dl_ref[g, mine, :] for g in range(GQ)], axis=0)
            kw = _kv_dup(k_ref, prow, rows, odd)
            vw = _kv_dup(v_ref, prow, rows, odd)
            s = _dot_nt(qs, kw) + b_ref[jnp.minimum(b, 1), 0]
            p = jnp.exp(s - jnp.concatenate([lse_t, lse_t], axis=1))
            dv2 = _dot_tn(p.astype(BF16), dos)
            dp = _dot_nt(dos, vw)
            ds = (p * (dp - jnp.concatenate([delta_t, delta_t], axis=1))).astype(BF16)
            dq_ref[0, mine, :], dq_ref[1, mine, :] = _unstack_heads(_dot(ds, kw))
            dk2 = _dot_tn(ds, qs)
            dkv = jnp.where(_low_lanes(2 * BLK), dk2 + pltpu.roll(dk2, HD, axis=1), dv2 + pltpu.roll(dv2, HD, axis=1))
            dkv_ref[0, rows, :] = dkv_ref[0, rows, :] + dkv[BLK:]
            dkv_ref[0, prow, :] = dkv_ref[0, prow, :] + dkv[:BLK]
            return carry

        lax.fori_loop(0, BLOCKS_PER_CHUNK, block, 0, unroll=8)

    q_like = pl.BlockSpec((2, CHUNK_ROWS, LANES), lambda j, c: (j, c, 0))
    heads = pl.BlockSpec((GQ, CHUNK_ROWS, LANES), lambda j, c: (j, c, 0))
    kv = pl.BlockSpec((1, S, LANES), lambda j, c: (j // 2, 0, 0))
    per_kv = pl.BlockSpec((1, S, LANES), lambda j, c: (j, 0, 0))
    bias_spec = pl.BlockSpec((2, 1, GQ * BLK, 2 * BLK), lambda j, c: (0, j, 0, 0))
    earlier = () if dkv_so_far is None else (dkv_so_far,)
    return pl.pallas_call(
        body, grid=(NKV, S // CHUNK_ROWS), name=f"attn_bwd_d{d}",
        in_specs=[q_like, q_like, heads, heads, kv, kv, bias_spec] + [ANY] * len(earlier),
        out_specs=[q_like, per_kv],
        out_shape=[_slabs(D // LANES), _slabs(NKV)],
        compiler_params=_params(("arbitrary", "arbitrary")),
    )(q, do, lse, delta, k, v, bias, *earlier)


CONV_T = 128


def _halo_before(i):
    return (jnp.maximum(i * (CONV_T // HALO) - 1, 0), 0)


def _halo_after(i):
    return (jnp.minimum((i + 1) * (CONV_T // HALO), S // HALO - 1), 0)


SUBLANES = 8
NCH = D // LANES
GROUP = SUBLANES * SUBLANES


def _comb(ref, cb, base):
    return ref[cb, pl.ds(base, SUBLANES, stride=SUBLANES), :]


def _taps(w_ref, cols):
    return [jnp.broadcast_to(w_ref[j:j + 1, cols], (SUBLANES, LANES)) for j in range(CONV_K)]


def _conv_fwd(c_val, c_glu, c_gate, conv_w, conv_b, ln_g, ln_b):
    T = CONV_T

    def body(cv_ref, cg_ref, cvh_ref, cgh_ref, gate_ref, w_ref, b_ref, lg_ref, lb_ref, u_ref, y_ref, win, us):
        i = pl.program_id(0)
        for cb in range(NCH):
            cols = slice(cb * LANES, (cb + 1) * LANES)
            win[cb, HALO:HALO + T, :] = cv_ref[:, cols] * _sigmoid(cg_ref[:, cols])
            win[cb, 0:HALO, :] = jnp.where(i > 0, cvh_ref[:, cols] * _sigmoid(cgh_ref[:, cols]), 0.0)
        for cb in range(NCH):
            cols = slice(cb * LANES, (cb + 1) * LANES)
            taps = _taps(w_ref, cols)
            bias = jnp.broadcast_to(b_ref[:, cols], (SUBLANES, LANES))

            def group(g, carry):
                for b in range(SUBLANES):
                    base = g * GROUP + b
                    acc = bias
                    for j in range(CONV_K):
                        acc = acc + taps[j] * _comb(win, cb, base + (HALO - (CONV_K - 1) + j))
                    us[cb, pl.ds(base, SUBLANES, stride=SUBLANES), :] = acc
                return carry

            lax.fori_loop(0, T // GROUP, group, 0)
        total = us[0]
        for cb in range(1, NCH):
            total = total + us[cb]
        mu = jnp.sum(total, axis=-1, keepdims=True) * (1.0 / D)
        sq = jnp.zeros((T, LANES), F32)
        for cb in range(NCH):
            uc = us[cb] - mu
            sq = sq + uc * uc
        rstd = lax.rsqrt(jnp.sum(sq, axis=-1, keepdims=True) * (1.0 / D) + LN_EPS)
        for cb in range(NCH):
            cols = slice(cb * LANES, (cb + 1) * LANES)
            u = us[cb]
            u_ref[:, cols] = u
            nrm = (u - mu) * rstd * lg_ref[:, cols] + lb_ref[:, cols]
            gate = gate_ref[:, cols]
            y_ref[:, cols] = (nrm * _sigmoid(nrm) * (gate * _sigmoid(gate))).astype(BF16)

    halo = pl.BlockSpec((HALO, D), _halo_before)
    return pl.pallas_call(
        body, grid=(S // T,), name="conv_fwd",
        in_specs=[_rows(T, D), _rows(T, D), halo, halo, _rows(T, D),
                  _resident((HALO, D)), _resident((1, D)), _resident((1, D)), _resident((1, D))],
        out_specs=[_rows(T, D), _rows(T, D)],
        out_shape=[jax.ShapeDtypeStruct((S, D), F32), jax.ShapeDtypeStruct((S, D), BF16)],
        scratch_shapes=[pltpu.VMEM((NCH, T + HALO, LANES), F32), pltpu.VMEM((NCH, T, LANES), F32)],
        compiler_params=_params(("arbitrary",)),
    )(c_val, c_glu, c_val, c_glu, c_gate, conv_w, conv_b, ln_g, ln_b)


def _conv_bwd_rows(u, c_gate, dy_conv, ln_g, ln_b):
    tm = 256

    def body(u_ref, gate_ref, dy_ref, lg_ref, lb_ref, du_ref, dgate_ref, st_ref):
        @pl.when(pl.program_id(0) == 0)
        def _():
            st_ref[...] = jnp.zeros_like(st_ref)

        u, gate, dy = u_ref[...], gate_ref[...], dy_ref[...]
        mu = jnp.mean(u, axis=-1, keepdims=True)
        uc = u - mu
        rstd = lax.rsqrt(jnp.mean(uc * uc, axis=-1, keepdims=True) + LN_EPS)
        z = uc * rstd
        nrm = z * lg_ref[...] + lb_ref[...]
        sn, sg = _sigmoid(nrm), _sigmoid(gate)
        dgate_ref[...] = (dy * (nrm * sn) * (sg * (1.0 + gate * (1.0 - sg)))).astype(BF16)
        dn = dy * (gate * sg) * (sn * (1.0 + nrm * (1.0 - sn)))
        dz = dn * lg_ref[...]
        du = rstd * (dz - jnp.mean(dz, axis=-1, keepdims=True) - z * jnp.mean(dz * z, axis=-1, keepdims=True))
        du_ref[...] = du
        st_ref[0:1, :] += jnp.sum(dn * z, axis=0, keepdims=True)
        st_ref[1:2, :] += jnp.sum(dn, axis=0, keepdims=True)
        st_ref[2:3, :] += jnp.sum(du, axis=0, keepdims=True)

    big = jax.ShapeDtypeStruct((S, D), F32)
    return pl.pallas_call(
        body, grid=(S // tm,), name="conv_bwd_rows",
        in_specs=[_rows(tm, D)] * 3 + [_resident((1, D)), _resident((1, D))],
        out_specs=[_rows(tm, D), _rows(tm, D), pl.BlockSpec((8, D), lambda i: (0, 0))],
        out_shape=[big, jax.ShapeDtypeStruct((S, D), BF16), jax.ShapeDtypeStruct((8, D), F32)],
        compiler_params=_params(("arbitrary",)),
    )(u, c_gate, dy_conv, ln_g, ln_b)


def _conv_bwd_taps(du, c_val, c_glu, conv_w):
    T = CONV_T
    last = S // T - 1

    def body(du_ref, dua_ref, cv_ref, cg_ref, cvh_ref, cgh_ref, w_ref, dcv_ref, dcg_ref, dw_ref,
             hwin, dwin, dhs, dw_acc):
        i = pl.program_id(0)

        @pl.when(i == 0)
        def _():
            dw_acc[...] = jnp.zeros_like(dw_acc)

        for cb in range(NCH):
            cols = slice(cb * LANES, (cb + 1) * LANES)
            hwin[cb, HALO:HALO + T, :] = cv_ref[:, cols] * _sigmoid(cg_ref[:, cols])
            hwin[cb, 0:HALO, :] = jnp.where(i > 0, cvh_ref[:, cols] * _sigmoid(cgh_ref[:, cols]), 0.0)
            dwin[cb, 0:T, :] = du_ref[:, cols]
            dwin[cb, T:T + HALO, :] = jnp.where(i < last, dua_ref[:, cols], 0.0)
        for cb in range(NCH):
            cols = slice(cb * LANES, (cb + 1) * LANES)
            taps = _taps(w_ref, cols)

            def group_dh(g, carry):
                for b in range(SUBLANES):
                    base = g * GROUP + b
                    acc = jnp.zeros((SUBLANES, LANES), F32)
                    for j in range(CONV_K):
                        acc = acc + taps[j] * _comb(dwin, cb, base + (CONV_K - 1 - j))
                    dhs[cb, pl.ds(base, SUBLANES, stride=SUBLANES), :] = acc
                return carry

            lax.fori_loop(0, T // GROUP, group_dh, 0)

            def group_dw(g, sums):
                for b in range(SUBLANES):
                    base = g * GROUP + b
                    d = _comb(dwin, cb, base)
                    sums = tuple(sums[j] + d * _comb(hwin, cb, base + (HALO - (CONV_K - 1) + j))
                                 for j in range(CONV_K))
                return sums

            sums = lax.fori_loop(0, T // GROUP, group_dw, tuple(dw_acc[j, :, cols] for j in range(CONV_K)))
            for j in range(CONV_K):
                dw_acc[j, :, cols] = sums[j]
            dh = dhs[cb]
            cv, sg = cv_ref[:, cols], _sigmoid(cg_ref[:, cols])
            dcv_ref[:, cols] = (dh * sg).astype(BF16)
            dcg_ref[:, cols] = (dh * cv * (sg * (1.0 - sg))).astype(BF16)

        @pl.when(i == last)
        def _():
            dw_ref[...] = jnp.zeros_like(dw_ref)
            for j in range(CONV_K):
                dw_ref[j:j + 1, :] = jnp.sum(dw_acc[j], axis=0, keepdims=True)

    before = pl.BlockSpec((HALO, D), _halo_before)
    after = pl.BlockSpec((HALO, D), _halo_after)
    big = jax.ShapeDtypeStruct((S, D), BF16)
    return pl.pallas_call(
        body, grid=(S // T,), name="conv_bwd_taps",
        in_specs=[_rows(T, D), after, _rows(T, D), _rows(T, D), before, before, _resident((HALO, D))],
        out_specs=[_rows(T, D), _rows(T, D), pl.BlockSpec((HALO, D), lambda i: (0, 0))],
        out_shape=[big, big, jax.ShapeDtypeStruct((HALO, D), F32)],
        scratch_shapes=[pltpu.VMEM((NCH, T + HALO, LANES), F32), pltpu.VMEM((NCH, T + HALO, LANES), F32),
                        pltpu.VMEM((NCH, T, LANES), F32), pltpu.VMEM((CONV_K, SUBLANES, D), F32)],
        compiler_params=_params(("arbitrary",)),
    )(du, du, c_val, c_glu, c_val, c_glu, conv_w)


def _outproj_loss(y_att, y_conv, w_out_bf, x, target, gf):
    tm = 256

    def body(ya_ref, yc_ref, w_ref, x_ref, t_ref, gf_ref, dx2_ref, dya_ref, dyc_ref, dw_ref, st_ref, acc):
        @pl.when(pl.program_id(0) == 0)
        def _():
            acc[...] = jnp.zeros_like(acc)
            st_ref[...] = jnp.zeros_like(st_ref)

        ya, yc = ya_ref[...], yc_ref[...]
        x2 = x_ref[...] + _dot(ya, w_ref[0:D, :]) + _dot(yc, w_ref[D:2 * D, :])
        r = lax.rsqrt(jnp.mean(x2 * x2, axis=-1, keepdims=True) + NORM_EPS)
        xn = x2 * r
        err = xn * gf_ref[...] - t_ref[...]
        dout = err * (1.0 / D)
        dxn = dout * gf_ref[...]
        dx2 = r * (dxn - xn * jnp.mean(dxn * xn, axis=-1, keepdims=True))
        dx2_ref[...] = dx2
        dx2b = dx2.astype(BF16)
        dya_ref[...] = _dot_nt(dx2b, w_ref[0:D, :])
        dyc_ref[...] = _dot_nt(dx2b, w_ref[D:2 * D, :])
        acc[0:D, :] += _dot_tn(ya, dx2b)
        acc[D:2 * D, :] += _dot_tn(yc, dx2b)
        st_ref[0:1, :] += jnp.sum(dout * xn, axis=0, keepdims=True)
        st_ref[1:2, :] += jnp.sum(err * err, axis=0, keepdims=True) * (0.5 / D)

        @pl.when(pl.program_id(0) == S // tm - 1)
        def _():
            dw_ref[...] = acc[...].astype(BF16)

    big = jax.ShapeDtypeStruct((S, D), F32)
    return pl.pallas_call(
        body, grid=(S // tm,), name="outproj_loss",
        in_specs=[_rows(tm, D), _rows(tm, D), _resident((WOUT_ROWS, D)), _rows(tm, D), _rows(tm, D), _resident((1, D))],
        out_specs=[_rows(tm, D), _rows(tm, D), _rows(tm, D),
                   pl.BlockSpec((WOUT_ROWS, D), lambda i: (0, 0)), pl.BlockSpec((8, D), lambda i: (0, 0))],
        out_shape=[big, big, big, jax.ShapeDtypeStruct((WOUT_ROWS, D), BF16), jax.ShapeDtypeStruct((8, D), F32)],
        scratch_shapes=[pltpu.VMEM((WOUT_ROWS, D), F32)],
        compiler_params=_params(("arbitrary",)),
    )(y_att, y_conv, w_out_bf, x, target, gf)


def _assemble_dproj(dqs, dkv, dag, dcv, dcg, dcgate):
    tm = 512

    def body(dq1, dq2, dq3, dkv_ref, dag_ref, dcv_ref, dcg_ref, dcgate_ref, dp_ref):
        for sl in range(D // LANES):
            dq = (dq1[sl] + dq2[sl] + dq3[sl]) * (HD ** -0.5)
            dp_ref[:, OFF_Q + sl * LANES:OFF_Q + (sl + 1) * LANES] = dq.astype(BF16)
        for j in range(NKV):
            dkv_j = dkv_ref[j].astype(BF16)
            dp_ref[:, OFF_K + j * HD:OFF_K + (j + 1) * HD] = dkv_j[:, :HD]
            dp_ref[:, OFF_V + j * HD:OFF_V + (j + 1) * HD] = dkv_j[:, HD:]
        for off, ref in ((OFF_AG, dag_ref), (OFF_CV, dcv_ref), (OFF_CG, dcg_ref), (OFF_CGATE, dcgate_ref)):
            dp_ref[:, off:off + D] = ref[...]

    return pl.pallas_call(
        body, grid=(S // tm,), name="assemble_dproj",
        in_specs=[_slab_rows(D // LANES, tm)] * 3 + [_slab_rows(NKV, tm)] + [_rows(tm, D)] * 4,
        out_specs=_rows(tm, NCOL),
        out_shape=jax.ShapeDtypeStruct((S, NCOL), BF16),
        compiler_params=_params(("arbitrary",)),
    )(*dqs, dkv, dag, dcv, dcg, dcgate)


def _inproj_bwd_x(dproj, w_bf, x, g1, dx2, pi, po, ps):
    tm = 256
    last = S // tm - 1

    def body(dp_ref, w_ref, x_ref, g_ref, dx2_ref, pi_ref, po_ref, ps_ref,
             gx_ref, st_ref, ri_ref, ro_ref, rs_ref, send, recv):
        i = pl.program_id(0)
        copies = _chip_exchange_copies((pi_ref, po_ref, ps_ref), (ri_ref, ro_ref, rs_ref), send, recv)

        @pl.when(i == 0)
        def _():
            st_ref[...] = jnp.zeros_like(st_ref)
            for out, _ in copies:
                out.start()

        dh = _dot_nt(dp_ref[...], w_ref[...])
        xt = x_ref[...]
        r = lax.rsqrt(jnp.mean(xt * xt, axis=-1, keepdims=True) + NORM_EPS)
        xn = xt * r
        dxn = dh * g_ref[...]
        gx_ref[...] = dx2_ref[...] + r * (dxn - xn * jnp.mean(dxn * xn, axis=-1, keepdims=True))
        st_ref[0:1, :] += jnp.sum(dh * xn, axis=0, keepdims=True)

        @pl.when(i == last)
        def _():
            for _, arrival in copies:
                arrival.wait_recv()
            for out, _ in copies:
                out.wait_send()

    n = 3 * len(CHIP_FLIPS)
    return pl.pallas_call(
        body, grid=(S // tm,), name="inproj_bwd_x",
        in_specs=[_rows(tm, NCOL), _resident((D, NCOL)), _rows(tm, D), _resident((1, D)), _rows(tm, D), ANY, ANY, ANY],
        out_specs=[_rows(tm, D), pl.BlockSpec((8, D), lambda i: (0, 0)), ANY, ANY, ANY],
        out_shape=[jax.ShapeDtypeStruct((S, D), F32), jax.ShapeDtypeStruct((8, D), F32),
                   jax.ShapeDtypeStruct((NCHIP, D // 2, CHUNK), BF16),
                   jax.ShapeDtypeStruct((NCHIP, WOUT_SHARD // 2, D), BF16),
                   jax.ShapeDtypeStruct((NCHIP, SMALL_ROWS, D), F32)],
        scratch_shapes=[pltpu.SemaphoreType.DMA((n,)), pltpu.SemaphoreType.DMA((n,))],
        compiler_params=_params(("arbitrary",)),
    )(dproj, w_bf, x, g1, dx2, pi, po, ps)


def _inproj_bwd_w(h, dproj):
    tk = 1024
    nk = S // tk

    def body(h_ref, dp_ref, o_ref, acc):
        i = pl.program_id(1)

        @pl.when(i == 0)
        def _():
            acc[...] = jnp.zeros_like(acc)

        acc[...] += _dot_tn(h_ref[...], dp_ref[...])

        @pl.when(i == nk - 1)
        def _():
            o_ref[0] = acc[...].astype(BF16)

    return pl.pallas_call(
        body, grid=(NCHIP, nk), name="inproj_bwd_w",
        in_specs=[pl.BlockSpec((tk, D), lambda c, i: (i, 0)), pl.BlockSpec((tk, CHUNK), lambda c, i: (i, c))],
        out_specs=pl.BlockSpec((1, D, CHUNK), lambda c, i: (c, 0, 0)),
        out_shape=jax.ShapeDtypeStruct((NCHIP, D, CHUNK), BF16),
        scratch_shapes=[pltpu.VMEM((D, CHUNK), F32)],
        compiler_params=_params(("arbitrary", "arbitrary")),
    )(h, dproj)


def _local_step(x, target, g1, w_in_bf, conv_w, conv_b, ln_g, ln_b, w_out_bf, gf):
    h, q, k, v, a_gate, c_val, c_glu, c_gate = _inproj_fwd(x, g1, w_in_bf)
    tables = [_bias_table(d) for d in PATTERNS]
    outs, lses = zip(*[_attn_fwd(q, k, v, t, d) for t, d in zip(tables, PATTERNS)])
    o, lse, y_att = _attn_combine(outs, lses, a_gate)
    u, y_conv = _conv_fwd(c_val, c_glu, c_gate, conv_w, conv_b, ln_g, ln_b)
    dx2, dy_att, dy_conv, dw_out, st_out = _outproj_loss(y_att, y_conv, w_out_bf, x, target, gf)

    do, da_gate, delta = _attn_gate_bwd(dy_att, o, a_gate, _head_sum_selectors())
    dqs, dkv = [], None
    for t, d in zip(tables, PATTERNS):
        dq, dkv = _attn_bwd(q, k, v, do, lse, delta, t, d, dkv)
        dqs.append(dq)

    du, dc_gate, st_conv = _conv_bwd_rows(u, c_gate, dy_conv, ln_g, ln_b)
    dc_val, dc_glu, dconv_w = _conv_bwd_taps(du, c_val, c_glu, conv_w)

    dproj = _assemble_dproj(dqs, dkv, da_gate, dc_val, dc_glu, dc_gate)
    dw_in = _inproj_bwd_w(h, dproj)
    small = jnp.concatenate([st_conv, st_out, dconv_w], axis=0)
    return dw_in, dw_out, small, dproj, dx2


ROW_LN_G, ROW_LN_B, ROW_CONV_B, ROW_FINAL_G, ROW_LOSS, ROW_TAPS = 0, 1, 2, 8, 9, 16
SMALL_ROWS = 16 + HALO
NDEV = 8


MESH = pl.DeviceIdType.MESH
ANY = pl.BlockSpec(memory_space=pl.ANY)
CHIP_FLIPS = ((1, 0), (0, 1), (1, 1))


def _pos():
    return lax.axis_index("x"), lax.axis_index("y"), lax.axis_index("c")


def _flip(v, f):
    return 1 - v if f else v


def _ds(start, size, align=None):
    return pl.ds(pl.multiple_of(start, align or size), size)


def _place_shards(wi, wo, cw, where):
    steps = 4

    def body(where_ref, wi_ref, wo_ref, cw_ref, wi_full, wo_full, cw_full):
        wi_full[...] = wi_ref[...].astype(BF16)
        wo_full[...] = wo_ref[...].astype(BF16)
        cw_full[...] = cw_ref[...]

    grid_spec = pltpu.PrefetchScalarGridSpec(
        num_scalar_prefetch=1, grid=(steps,),
        in_specs=[pl.BlockSpec((D // steps, CHUNK), lambda i, w: (i, 0)),
                  pl.BlockSpec((WOUT_SHARD // steps, D), lambda i, w: (i, 0)),
                  pl.BlockSpec((HALO, CONVW_SHARD), lambda i, w: (0, 0))],
        out_specs=[pl.BlockSpec((D // steps, CHUNK), lambda i, w: (i, w[0])),
                   pl.BlockSpec((WOUT_SHARD // steps, D), lambda i, w: (w[0] * steps + i, 0)),
                   pl.BlockSpec((HALO, CONVW_SHARD), lambda i, w: (0, w[0]))])
    return pl.pallas_call(
        body, grid_spec=grid_spec, name="place_shards",
        out_shape=[jax.ShapeDtypeStruct((D, NCOL), BF16), jax.ShapeDtypeStruct((WOUT_ROWS, D), BF16),
                   jax.ShapeDtypeStruct((HALO, D), F32)],
        compiler_params=_params(("arbitrary",)),
    )(where, wi, wo, cw)


def _gather_weights(wi_full, wo_full, cw_full):
    halves = (D // 2, WOUT_SHARD // 2, HALO // 2)
    OWN_X, OWN_Y, VIA_Y, VIA_X = range(4)

    def body(_wi, _wo, _cw, wi_full, wo_full, cw_full, send, recv):
        x, y, c = _pos()
        x_nbr, y_nbr, diag = (1 - x, y), (x, 1 - y), (1 - x, 1 - y)

        def region(a, chip_xy, half, part=None):
            chip = 2 * chip_xy[0] + chip_xy[1]
            n, row = halves[a], half * halves[a]
            if part is not None:
                n = n // 2
                row = row + part * n
            if a == 0:
                return wi_full.at[_ds(row, n), _ds(chip * CHUNK, CHUNK, 128)]
            if a == 1:
                return wo_full.at[_ds(chip * WOUT_SHARD + row, n), :]
            return cw_full.at[_ds(row, n), _ds(chip * CONVW_SHARD, CONVW_SHARD, 128)]

        def copy(a, kind, piece, dev):
            k = 8 * a + kind
            return pltpu.make_async_remote_copy(src_ref=piece, dst_ref=piece, send_sem=send.at[k], recv_sem=recv.at[k],
                                                device_id=dev, device_id_type=MESH)

        def to_sibling(a, kind, piece):
            cp = copy(a, 4 + kind, piece, (x, y, 1 - c))
            cp.start()
            return cp

        sends = []
        for a in range(3):
            for kind, nbr in ((OWN_X, x_nbr), (OWN_Y, y_nbr)):
                cp = copy(a, kind, region(a, (x, y), c), (*nbr, c))
                cp.start()
                sends.append(cp)
        for a in range(3):
            got = region(a, x_nbr, c)
            copy(a, OWN_X, got, (*x_nbr, c)).wait_recv()
            onward = copy(a, VIA_Y, region(a, x_nbr, c, 0), (*y_nbr, c))
            onward.start()
            sends += [onward, to_sibling(a, OWN_X, got)]
            got = region(a, y_nbr, c)
            copy(a, OWN_Y, got, (*y_nbr, c)).wait_recv()
            onward = copy(a, VIA_X, region(a, y_nbr, c, 1), (*x_nbr, c))
            onward.start()
            sends += [onward, to_sibling(a, OWN_Y, got)]
        for a in range(3):
            got = region(a, diag, c, 0)
            copy(a, VIA_Y, got, (*y_nbr, c)).wait_recv()
            sends.append(to_sibling(a, VIA_Y, got))
            got = region(a, diag, c, 1)
            copy(a, VIA_X, got, (*x_nbr, c)).wait_recv()
            sends.append(to_sibling(a, VIA_X, got))
        for a in range(3):
            for kind, piece in ((OWN_X, region(a, x_nbr, 1 - c)), (OWN_Y, region(a, y_nbr, 1 - c)),
                                (VIA_Y, region(a, diag, 1 - c, 0)), (VIA_X, region(a, diag, 1 - c, 1))):
                copy(a, 4 + kind, piece, (x, y, 1 - c)).wait_recv()
        for cp in sends:
            cp.wait_send()

    n_sems = 3 * 8
    return pl.pallas_call(
        body, name="gather_weights",
        in_specs=[ANY, ANY, ANY], out_specs=[ANY, ANY, ANY], input_output_aliases={0: 0, 1: 1, 2: 2},
        out_shape=[jax.ShapeDtypeStruct((D, NCOL), BF16), jax.ShapeDtypeStruct((WOUT_ROWS, D), BF16),
                   jax.ShapeDtypeStruct((HALO, D), F32)],
        scratch_shapes=[pltpu.SemaphoreType.DMA((n_sems,)), pltpu.SemaphoreType.DMA((n_sems,))],
    )(wi_full, wo_full, cw_full)


def _exchange_halves(gi4, go4, small):
    def body(gi_ref, go_ref, sm_ref, ri_ref, ro_ref, rs_ref, send, recv):
        x, y, c = _pos()
        sib = (x, y, 1 - c)
        copies = [
            (gi_ref.at[:, _ds((1 - c) * (D // 2), D // 2), :], ri_ref),
            (go_ref.at[:, _ds((1 - c) * (WOUT_SHARD // 2), WOUT_SHARD // 2), :], ro_ref),
            (sm_ref, rs_ref),
        ]
        cps = [pltpu.make_async_remote_copy(src_ref=s_, dst_ref=d_, send_sem=send.at[k], recv_sem=recv.at[k],
                                            device_id=sib, device_id_type=MESH) for k, (s_, d_) in enumerate(copies)]
        for cp in cps:
            cp.start()
        for cp in cps:
            cp.wait()

    return pl.pallas_call(
        body, name="exchange_halves",
        in_specs=[ANY, ANY, ANY], out_specs=[ANY, ANY, ANY],
        out_shape=[jax.ShapeDtypeStruct((NCHIP, D // 2, CHUNK), BF16),
                   jax.ShapeDtypeStruct((NCHIP, WOUT_SHARD // 2, D), BF16),
                   jax.ShapeDtypeStruct((SMALL_ROWS, D), F32)],
        scratch_shapes=[pltpu.SemaphoreType.DMA((3,)), pltpu.SemaphoreType.DMA((3,))],
    )(gi4, go4, small)


def _add_halves(gi4, ri, go4, ro, small, rs):
    hi, ho = D // 2, WOUT_SHARD // 2

    def body(gi_ref, ri_ref, go_ref, ro_ref, sm_ref, rs_ref, pi_ref, po_ref, ps_ref):
        c = lax.axis_index("c")
        pi_ref[0] = (gi_ref[0, _ds(c * hi, hi), :].astype(F32) + ri_ref[0].astype(F32)).astype(BF16)
        po_ref[0] = (go_ref[0, _ds(c * ho, ho), :].astype(F32) + ro_ref[0].astype(F32)).astype(BF16)
        ps_ref[...] = sm_ref[...] + rs_ref[...]

    blk = lambda n, w: pl.BlockSpec((1, n, w), lambda k: (k, 0, 0))
    whole = pl.BlockSpec((SMALL_ROWS, D), lambda k: (0, 0))
    return pl.pallas_call(
        body, grid=(NCHIP,), name="add_halves",
        in_specs=[blk(D, CHUNK), blk(hi, CHUNK), blk(WOUT_SHARD, D), blk(ho, D), whole, whole],
        out_specs=[blk(hi, CHUNK), blk(ho, D), whole],
        out_shape=[jax.ShapeDtypeStruct((NCHIP, hi, CHUNK), BF16), jax.ShapeDtypeStruct((NCHIP, ho, D), BF16),
                   jax.ShapeDtypeStruct((SMALL_ROWS, D), F32)],
        compiler_params=_params(("arbitrary",)),
    )(gi4, ri, go4, ro, small, rs)


def _chip_exchange_copies(srcs, dsts, send, recv):
    x, y, c = _pos()
    me = 2 * x + y
    pairs = []
    for a in range(3):
        for j, (fx, fy) in enumerate(CHIP_FLIPS):
            px, py = _flip(x, fx), _flip(y, fy)
            peer = 2 * px + py
            k = 3 * a + j
            out = pltpu.make_async_remote_copy(
                src_ref=srcs[a] if a == 2 else srcs[a].at[peer], dst_ref=dsts[a].at[me],
                send_sem=send.at[k], recv_sem=recv.at[k], device_id=(px, py, c), device_id_type=MESH)
            got = dsts[a].at[peer]
            arrival = pltpu.make_async_remote_copy(
                src_ref=got, dst_ref=got, send_sem=send.at[k], recv_sem=recv.at[k],
                device_id=(px, py, c), device_id_type=MESH)
            pairs.append((out, arrival))
    return pairs


def _sum_chips(ri, ro, rs, pi, po, ps, where):
    def body(w_ref, ri_ref, ro_ref, rs_ref, pi_ref, po_ref, ps_ref, gi_ref, go_ref, gs_ref, g5_ref, loss_ref,
             acc_i, acc_o, acc_s):
        k = pl.program_id(0)
        accs = (acc_i, acc_o, acc_s)

        @pl.when(k == 0)
        def _():
            for acc in accs:
                acc[...] = jnp.zeros_like(acc)

        @pl.when(k == w_ref[0])
        def _():
            for acc, val in zip(accs, (pi_ref[0], po_ref[0], ps_ref[...])):
                acc[...] += val.astype(F32)

        @pl.when(k != w_ref[0])
        def _():
            for acc, ref in zip(accs, (ri_ref, ro_ref, rs_ref)):
                acc[...] += ref[0].astype(F32)

        @pl.when(k == NCHIP - 1)
        def _():
            gi_ref[0] = acc_i[...]
            go_ref[0] = acc_o[...]
            gs_ref[...] = acc_s[...]
            g5_ref[...] = jnp.zeros_like(g5_ref)
            for i, row in enumerate((ROW_CONV_B, ROW_LN_G, ROW_LN_B, ROW_FINAL_G)):
                g5_ref[i + 1:i + 2, :] = acc_s[row:row + 1, :]
            loss = jnp.sum(acc_s[ROW_LOSS:ROW_LOSS + 1, :], axis=1, keepdims=True)
            loss_ref[...] = jnp.broadcast_to(loss, loss_ref.shape)

    def sent(k, w):
        return jnp.where(k == w[0], (k + 1) % NCHIP, k)

    hi, ho = D // 2, WOUT_SHARD // 2
    const = lambda shape: pl.BlockSpec(shape, lambda k, w: (0,) * len(shape))
    grid_spec = pltpu.PrefetchScalarGridSpec(
        num_scalar_prefetch=1, grid=(NCHIP,),
        in_specs=[pl.BlockSpec((1, hi, CHUNK), lambda k, w: (sent(k, w), 0, 0)),
                  pl.BlockSpec((1, ho, D), lambda k, w: (sent(k, w), 0, 0)),
                  pl.BlockSpec((1, SMALL_ROWS, D), lambda k, w: (sent(k, w), 0, 0)),
                  pl.BlockSpec((1, hi, CHUNK), lambda k, w: (w[0], 0, 0)),
                  pl.BlockSpec((1, ho, D), lambda k, w: (w[0], 0, 0)),
                  const((SMALL_ROWS, D))],
        out_specs=[pl.BlockSpec((1, hi, CHUNK), lambda k, w: (w[1], 0, 0)),
                   pl.BlockSpec((1, ho, D), lambda k, w: (w[1], 0, 0)),
                   const((SMALL_ROWS, D)), const((8, D)), const((8, LANES))],
        scratch_shapes=[pltpu.VMEM((hi, CHUNK), F32), pltpu.VMEM((ho, D), F32), pltpu.VMEM((SMALL_ROWS, D), F32)])
    return pl.pallas_call(
        body, grid_spec=grid_spec, name="sum_chips",
        out_shape=[jax.ShapeDtypeStruct((2, hi, CHUNK), F32), jax.ShapeDtypeStruct((2, ho, D), F32),
                   jax.ShapeDtypeStruct((SMALL_ROWS, D), F32), jax.ShapeDtypeStruct((8, D), F32),
                   jax.ShapeDtypeStruct((8, LANES), F32)],
        compiler_params=_params(("arbitrary",)),
    )(where, ri, ro, rs, pi, po, ps)


def _exchange_results(gi2, go2, st):
    flips = [(fx, fy, fc) for fx in (0, 1) for fy in (0, 1) for fc in (0, 1)][1:]

    def body(_gi, _go, st_ref, gi_ref, go_ref, all_ref, send, recv, lsem):
        x, y, c = _pos()
        sib = (x, y, 1 - c)

        def half(k, ref, slot):
            return pltpu.make_async_remote_copy(src_ref=ref.at[slot], dst_ref=ref.at[slot], send_sem=send.at[k],
                                                recv_sem=recv.at[k], device_id=sib, device_id_type=MESH)

        def stat(k, src, slot, dev):
            return pltpu.make_async_remote_copy(src_ref=src, dst_ref=all_ref.at[slot], send_sem=send.at[k],
                                                recv_sem=recv.at[k], device_id=dev, device_id_type=MESH)

        mine = pltpu.make_async_copy(st_ref, all_ref.at[4 * x + 2 * y + c], lsem)
        mine.start()
        sends = [half(k, ref, c) for k, ref in enumerate((gi_ref, go_ref))]
        peers = [(_flip(x, fx), _flip(y, fy), _flip(c, fc)) for fx, fy, fc in flips]
        sends += [stat(2 + k, st_ref, 4 * x + 2 * y + c, dev) for k, dev in enumerate(peers)]
        for cp in sends:
            cp.start()
        for k, ref in enumerate((gi_ref, go_ref)):
            half(k, ref, 1 - c).wait_recv()
        for k, (px, py, pc) in enumerate(peers):
            slot = 4 * px + 2 * py + pc
            stat(2 + k, all_ref.at[slot], slot, (px, py, pc)).wait_recv()
        for cp in sends:
            cp.wait_send()
        mine.wait()

    n = 2 + len(flips)
    return pl.pallas_call(
        body, name="exchange_results",
        in_specs=[ANY, ANY, ANY], out_specs=[ANY, ANY, ANY], input_output_aliases={0: 0, 1: 1},
        out_shape=[jax.ShapeDtypeStruct((2, D // 2, CHUNK), F32), jax.ShapeDtypeStruct((2, WOUT_SHARD // 2, D), F32),
                   jax.ShapeDtypeStruct((NDEV, 8, D), F32)],
        scratch_shapes=[pltpu.SemaphoreType.DMA((n,)), pltpu.SemaphoreType.DMA((n,)), pltpu.SemaphoreType.DMA],
    )(gi2, go2, st)


def _adamw_math(w, g, m, v):
    m2 = ADAM_B1 * m + (1.0 - ADAM_B1) * g
    v2 = ADAM_B2 * v + (1.0 - ADAM_B2) * (g * g)
    m_hat = m2 / (1.0 - ADAM_B1 ** ADAM_STEP)
    v_hat = v2 / (1.0 - ADAM_B2 ** ADAM_STEP)
    delta = -ADAM_LR * (m_hat / (jnp.sqrt(v_hat) + ADAM_EPS) + ADAM_WD * w)
    return delta, m2, v2


def _adamw(w, g, m, v, name):
    rows, cols = w.shape
    tm = 256 if rows % 256 == 0 else rows

    def body(w_ref, g_ref, m_ref, v_ref, d_ref, m2_ref, v2_ref):
        d_ref[...], m2_ref[...], v2_ref[...] = _adamw_math(w_ref[...], g_ref[...], m_ref[...], v_ref[...])

    shape = jax.ShapeDtypeStruct(w.shape, F32)
    return pl.pallas_call(
        body, grid=(rows // tm,), name=name,
        in_specs=[_rows(tm, cols)] * 4, out_specs=[_rows(tm, cols)] * 3, out_shape=[shape] * 3,
        compiler_params=_params(("arbitrary",)),
    )(w, g, m, v)


def _adamw_vectors(g5, first_parts, ws, ms, vs):
    n = len(ws)

    def body(g_ref, parts_ref, *refs):
        ins, g0_ref, outs = refs[:3 * n], refs[3 * n], refs[3 * n + 1:]
        g0 = parts_ref[0, 0:1, :]
        for dev in range(1, NDEV):
            g0 = g0 + parts_ref[dev, 0:1, :]
        g0_ref[...] = g0
        for i in range(n):
            g = g0 if i == 0 else g_ref[i:i + 1, :]
            res = _adamw_math(ins[i][...], g, ins[n + i][...], ins[2 * n + i][...])
            for kind in range(3):
                outs[kind * n + i][...] = res[kind]

    shape = jax.ShapeDtypeStruct((1, D), F32)
    return pl.pallas_call(body, name="adamw_vectors", out_shape=[shape] * (1 + 3 * n), compiler_params=_params())(
        g5, first_parts, *ws, *ms, *vs)


def kernel(x, norm_g, w_in, conv_w, conv_b, conv_ln_g, conv_ln_b, w_out, final_norm_g, loss_target, m_norm_g, m_w_in, m_conv_w, m_conv_b, m_conv_ln_g, m_conv_ln_b, m_w_out, m_final_norm_g, v_norm_g, v_w_in, v_conv_w, v_conv_b, v_conv_ln_g, v_conv_ln_b, v_w_out, v_final_norm_g):
    chip = 2 * lax.axis_index("x") + lax.axis_index("y")
    where = jnp.stack([chip, lax.axis_index("c")]).astype(jnp.int32)
    taps_shard = jnp.pad(conv_w[0], ((0, HALO - CONV_K), (0, 0)))
    wi_full, wo_full, cw_full = _gather_weights(*_place_shards(w_in[0], w_out[0], taps_shard, where))

    gf = final_norm_g[None]
    dw_in4, dw_out, small, dproj, dx2 = _local_step(
        x[0], loss_target[0], norm_g, wi_full, cw_full, conv_b, conv_ln_g, conv_ln_b, wo_full, gf)
    dw_out4 = dw_out.reshape(NCHIP, WOUT_SHARD, D)

    ri, ro, rs = _exchange_halves(dw_in4, dw_out4, small)
    pi, po, ps = _add_halves(dw_in4, ri, dw_out4, ro, small, rs)
    grad_x, st_in, ri, ro, rs = _inproj_bwd_x(dproj, wi_full, x[0], norm_g, dx2, pi, po, ps)
    gi2, go2, g_small, g5, loss8 = _sum_chips(ri, ro, rs, pi, po, ps, where)
    gi2, go2, norm_g_parts = _exchange_results(gi2, go2, st_in)
    g_w_in = gi2.reshape(D, CHUNK)
    g_w_out = go2.reshape(WOUT_SHARD, D)
    g_taps = lax.dynamic_slice(g_small, (ROW_TAPS, chip * CONVW_SHARD), (CONV_K, CONVW_SHARD))

    d_w_in, m2_w_in, v2_w_in = _adamw(w_in[0], g_w_in, m_w_in[0], v_w_in[0], "adamw_w_in")
    d_w_out, m2_w_out, v2_w_out = _adamw(w_out[0], g_w_out, m_w_out[0], v_w_out[0], "adamw_w_out")
    d_taps, m2_taps, v2_taps = _adamw(conv_w[0], g_taps, m_conv_w[0], v_conv_w[0], "adamw_conv_w")
    g_norm, *vec = _adamw_vectors(
        g5, norm_g_parts,
        (norm_g, conv_b, conv_ln_g, conv_ln_b, gf),
        (m_norm_g, m_conv_b, m_conv_ln_g, m_conv_ln_b, m_final_norm_g[None]),
        (v_norm_g, v_conv_b, v_conv_ln_g, v_conv_ln_b, v_final_norm_g[None]))
    d_vec, m2_vec, v2_vec = vec[0:5], vec[5:10], vec[10:15]

    def weight_order(ng, wi, cw, cb, lg, lb, wo, fg):
        return (ng, wi[None], cw[None], cb, lg, lb, wo[None], fg[0])

    grads = weight_order(g_norm, g_w_in, g_taps, g5[1:2], g5[2:3], g5[3:4], g_w_out, g5[4:5])
    deltas = weight_order(d_vec[0], d_w_in, d_taps, d_vec[1], d_vec[2], d_vec[3], d_w_out, d_vec[4])
    new_m = weight_order(m2_vec[0], m2_w_in, m2_taps, m2_vec[1], m2_vec[2], m2_vec[3], m2_w_out, m2_vec[4])
    new_v = weight_order(v2_vec[0], v2_w_in, v2_taps, v2_vec[1], v2_vec[2], v2_vec[3], v2_w_out, v2_vec[4])
    return (loss8[0, 0], grad_x[None], *grads, *deltas, *new_m, *new_v)
```

```python
import jax
import jax.numpy as jnp
from jax import lax
from jax.experimental import pallas as pl
from jax.experimental.pallas import tpu as pltpu

F32 = jnp.float32
BF16 = jnp.bfloat16

S = 4096
D = 1024
LANES = 128
HD = 64
NKV = 4
GQ = 4
KVW = NKV * HD
NCOL = 5632
CONV_K = 31
HALO = 32
BLK = 128
PATTERNS = (1, 4, 16)
NORM_EPS = 1e-6
LN_EPS = 1e-5
NEG = -1e30
OFF_Q, OFF_K, OFF_V, OFF_AG, OFF_CV, OFF_CG, OFF_CGATE = 0, 1024, 1280, 1536, 2560, 3584, 4608
NCHIP = 4
CHUNK = NCOL // NCHIP
WOUT_ROWS = 2 * D
WOUT_SHARD = WOUT_ROWS // NCHIP
CONVW_SHARD = D // NCHIP

ADAM_LR, ADAM_B1, ADAM_B2, ADAM_EPS, ADAM_WD, ADAM_STEP = 0.001, 0.9, 0.999, 1e-08, 0.01, 10

VMEM_LIMIT = 56 * 1024 * 1024


def _params(sem=None, vmem=VMEM_LIMIT):
    return pltpu.CompilerParams(dimension_semantics=sem, vmem_limit_bytes=vmem)


def _sigmoid(a):
    return 0.5 * jnp.tanh(0.5 * a) + 0.5


def _rows(tm, width):
    return pl.BlockSpec((tm, width), lambda i: (i, 0))


def _slabs(n):
    return jax.ShapeDtypeStruct((n, S, LANES), F32)


def _slab_rows(n, tm):
    return pl.BlockSpec((n, tm, LANES), lambda i: (0, i, 0))


def _resident(shape):
    return pl.BlockSpec(shape, lambda *_: (0,) * len(shape), pipeline_mode=pl.Buffered(1))


def _dot(a, b):
    return jnp.dot(a, b, preferred_element_type=F32)


def _dot_nt(a, b):
    return lax.dot_general(a, b, (((1,), (1,)), ((), ())), preferred_element_type=F32)


def _dot_tn(a, b):
    return lax.dot_general(a, b, (((0,), (0,)), ((), ())), preferred_element_type=F32)


def _inproj_fwd(x, g1, w_bf):
    tm = 512

    def body(x_ref, g_ref, w_ref, h_ref, q_ref, k_ref, v_ref, ag_ref, cv_ref, cg_ref, cgate_ref):
        xt = x_ref[...]
        r = lax.rsqrt(jnp.mean(xt * xt, axis=-1, keepdims=True) + NORM_EPS)
        h = (xt * r * g_ref[...]).astype(BF16)
        h_ref[...] = h
        q = _dot(h, w_ref[:, OFF_Q:OFF_Q + D]) * (HD ** -0.5)
        kv = _dot(h, w_ref[:, OFF_K:OFF_K + 2 * KVW])
        for sl in range(D // LANES):
            q_ref[sl] = q[:, sl * LANES:(sl + 1) * LANES]
        for sl in range(KVW // LANES):
            k_ref[sl] = kv[:, sl * LANES:(sl + 1) * LANES]
            v_ref[sl] = kv[:, KVW + sl * LANES:KVW + (sl + 1) * LANES]
        ag_ref[...] = _dot(h, w_ref[:, OFF_AG:OFF_AG + D])
        cv_ref[...] = _dot(h, w_ref[:, OFF_CV:OFF_CV + D])
        cg_ref[...] = _dot(h, w_ref[:, OFF_CG:OFF_CG + D])
        cgate_ref[...] = _dot(h, w_ref[:, OFF_CGATE:OFF_CGATE + D])

    big = jax.ShapeDtypeStruct((S, D), F32)
    return pl.pallas_call(
        body, grid=(S // tm,), name="inproj_fwd",
        in_specs=[_rows(tm, D), _resident((1, D)), _resident((D, NCOL))],
        out_specs=[_rows(tm, D), _slab_rows(D // LANES, tm), _slab_rows(KVW // LANES, tm), _slab_rows(KVW // LANES, tm),
                   _rows(tm, D), _rows(tm, D), _rows(tm, D), _rows(tm, D)],
        out_shape=[jax.ShapeDtypeStruct((S, D), BF16), _slabs(D // LANES), _slabs(KVW // LANES), _slabs(KVW // LANES),
                   big, big, big, big],
        compiler_params=_params(("arbitrary",)),
    )(x, g1, w_bf)


def _bias_table(d):
    h = jnp.arange(NKV * GQ, dtype=F32)
    slopes = jnp.exp2(-8.0 * (h + 1.0) / (NKV * GQ))
    qi = jnp.arange(BLK)[:, None]
    kj = jnp.arange(2 * BLK)[None, :]
    dist = BLK + qi - kj
    window = (dist >= 0) & (dist <= BLK)
    bias = -slopes[:, None, None] * (dist * d).astype(F32)[None]
    has_prev = jnp.stack([jnp.broadcast_to(kj >= BLK, (BLK, 2 * BLK)), jnp.ones((BLK, 2 * BLK), bool)])
    valid = window[None] & has_prev
    tab = jnp.where(valid[:, None], bias[None], NEG)
    return tab.reshape(2, NKV, GQ * BLK, 2 * BLK)


def _sub_rows(start, d):
    if d == 1:
        return pl.ds(pl.multiple_of(start, BLK), BLK)
    return pl.ds(start, BLK, stride=d)


NHEAD = NKV * GQ
CHUNK_ROWS = 2048
BLOCKS_PER_CHUNK = CHUNK_ROWS // BLK


def _low_lanes(rows=BLK):
    return lax.broadcasted_iota(jnp.int32, (rows, LANES), 1) < HD


def _block_start(idx, d):
    shift = d.bit_length() - 1
    b, r = lax.shift_right_logical(idx, shift), lax.bitwise_and(idx, d - 1)
    start = b * (BLK * d) + r
    return b, start, jnp.maximum(start - BLK * d, r)


def _stack_heads(ref, rows):
    low = _low_lanes()
    t0, t1 = ref[0, rows, :], ref[1, rows, :]
    return jnp.concatenate([jnp.where(low, t0, 0.0), jnp.where(low, 0.0, t0),
                            jnp.where(low, t1, 0.0), jnp.where(low, 0.0, t1)], axis=0).astype(BF16)


def _unstack_heads(dup):
    low = _low_lanes()
    return (jnp.where(low, dup[0:BLK], dup[BLK:2 * BLK]), jnp.where(low, dup[2 * BLK:3 * BLK], dup[3 * BLK:4 * BLK]))


def _kv_dup(ref, prow, rows, odd):
    t = jnp.concatenate([ref[0, prow, :], ref[0, rows, :]], axis=0)
    swapped = pltpu.roll(t, HD, axis=1)
    keep = jnp.logical_xor(_low_lanes(2 * BLK), odd)
    return jnp.where(keep, t, swapped).astype(BF16)


def _attn_fwd(q, k, v, bias, d):
    def body(q_ref, k_ref, v_ref, b_ref, o_ref, l_ref):
        odd = pl.program_id(0) % 2 == 1
        ones = jnp.ones((2 * BLK, LANES), BF16)

        def block(idx, carry):
            b, start, pstart = _block_start(idx, d)
            rows, prow = _sub_rows(start, d), _sub_rows(pstart, d)
            qs = _stack_heads(q_ref, rows)
            kw = _kv_dup(k_ref, prow, rows, odd)
            vw = _kv_dup(v_ref, prow, rows, odd)
            s = _dot_nt(qs, kw) + b_ref[jnp.minimum(b, 1), 0]
            m = jnp.max(s, axis=1, keepdims=True)
            p = jnp.exp(s - m).astype(BF16)
            ol = _dot(p, jnp.concatenate([vw, ones], axis=1))
            l = ol[:, LANES:]
            o_ref[0, rows, :], o_ref[1, rows, :] = _unstack_heads(ol[:, :LANES] / l)
            lse = m + jnp.log(l)
            for g in range(GQ):
                l_ref[g, rows, :] = lse[g * BLK:(g + 1) * BLK]
            return carry

        lax.fori_loop(0, S // BLK, block, 0, unroll=2)

    q_like = pl.BlockSpec((2, S, LANES), lambda j: (j, 0, 0))
    kv = pl.BlockSpec((1, S, LANES), lambda j: (j // 2, 0, 0))
    heads = pl.BlockSpec((GQ, S, LANES), lambda j: (j, 0, 0))
    bias_spec = pl.BlockSpec((2, 1, GQ * BLK, 2 * BLK), lambda j: (0, j, 0, 0))
    return pl.pallas_call(
        body, grid=(NKV,), name=f"attn_fwd_d{d}",
        in_specs=[q_like, kv, kv, bias_spec],
        out_specs=[q_like, heads],
        out_shape=[_slabs(D // LANES), _slabs(NHEAD)],
        compiler_params=_params(("arbitrary",)),
    )(q, k, v, bias)


def _attn_combine(outs, lses, a_gate):
    tm = 256

    def body(o1, o2, o3, l1, l2, l3, ag_ref, o_ref, lse_ref, y_ref):
        low = _low_lanes(tm)
        for sl in range(D // LANES):
            w = []
            for h in (2 * sl, 2 * sl + 1):
                a, b, c = l1[h], l2[h], l3[h]
                m = jnp.maximum(jnp.maximum(a, b), c)
                ea, eb, ec = jnp.exp(a - m), jnp.exp(b - m), jnp.exp(c - m)
                den = ea + eb + ec
                lse_ref[h] = m + jnp.log(den)
                inv = 1.0 / den
                w.append((ea * inv, eb * inv, ec * inv))
            wa, wb, wc = (jnp.where(low, w[0][i], w[1][i]) for i in range(3))
            o = wa * o1[sl] + wb * o2[sl] + wc * o3[sl]
            o_ref[sl] = o
            cols = slice(sl * LANES, (sl + 1) * LANES)
            ag = ag_ref[:, cols]
            y_ref[:, cols] = (o * (ag * _sigmoid(ag))).astype(BF16)

    wide, per_head = _slab_rows(D // LANES, tm), _slab_rows(NHEAD, tm)
    return pl.pallas_call(
        body, grid=(S // tm,), name="attn_combine",
        in_specs=[wide] * 3 + [per_head] * 3 + [_rows(tm, D)],
        out_specs=[wide, per_head, _rows(tm, D)],
        out_shape=[_slabs(D // LANES), _slabs(NHEAD), jax.ShapeDtypeStruct((S, D), BF16)],
        compiler_params=_params(("arbitrary",)),
    )(*outs, *lses, a_gate)


def _head_sum_selectors():
    lane_in = jnp.arange(LANES)[:, None] // HD
    return jnp.stack([jnp.broadcast_to(lane_in == h, (LANES, LANES)) for h in range(2)]).astype(BF16)


def _attn_gate_bwd(dy_att, o, a_gate, selectors):
    tm = 256

    def body(dy_ref, o_ref, ag_ref, e_ref, do_ref, dag_ref, delta_ref):
        for sl in range(D // LANES):
            cols = slice(sl * LANES, (sl + 1) * LANES)
            dy, ag, o_ = dy_ref[:, cols], ag_ref[:, cols], o_ref[sl]
            sg = _sigmoid(ag)
            do = dy * (ag * sg)
            do_ref[sl] = do
            dag_ref[:, cols] = (dy * o_ * (sg * (1.0 + ag * (1.0 - sg)))).astype(BF16)
            prod = do * o_
            hi = prod.astype(BF16)
            lo = (prod - hi.astype(F32)).astype(BF16)
            for h in range(2):
                delta_ref[2 * sl + h] = _dot(hi, e_ref[h]) + _dot(lo, e_ref[h])

    return pl.pallas_call(
        body, grid=(S // tm,), name="attn_gate_bwd",
        in_specs=[_rows(tm, D), _slab_rows(D // LANES, tm), _rows(tm, D), _resident((2, LANES, LANES))],
        out_specs=[_slab_rows(D // LANES, tm), _rows(tm, D), _slab_rows(NHEAD, tm)],
        out_shape=[_slabs(D // LANES), jax.ShapeDtypeStruct((S, D), BF16), _slabs(NHEAD)],
        compiler_params=_params(("arbitrary",)),
    )(dy_att, o, a_gate, selectors)


def _attn_bwd(q, k, v, do, lse, delta, bias, d):
    def body(q_ref, do_ref, l_ref, dl_ref, k_ref, v_ref, b_ref, dq_ref, dkv_ref):
        odd = pl.program_id(0) % 2 == 1
        chunk = pl.program_id(1)

        @pl.when(chunk == 0)
        def _():
            dkv_ref[...] = jnp.zeros_like(dkv_ref)

        def block(idx, carry):
            b, start, pstart = _block_start(chunk * BLOCKS_PER_CHUNK + idx, d)
            rows, prow = _sub_rows(start, d), _sub_rows(pstart, d)
            mine = _sub_rows(start - chunk * CHUNK_ROWS, d)
            qs = _stack_heads(q_ref, mine)
            dos = _stack_heads(do_ref, mine)
            lse_t = jnp.concatenate([l_ref[g, mine, :] for g in range(GQ)], axis=0)
            delta_t = jnp.concatenate([dl_ref[g, mine, :] for g in range(GQ)], axis=0)
            kw = _kv_dup(k_ref, prow, rows, odd)
            vw = _kv_dup(v_ref, prow, rows, odd)
            s = _dot_nt(qs, kw) + b_ref[jnp.minimum(b, 1), 0]
            p = jnp.exp(s - jnp.concatenate([lse_t, lse_t], axis=1))
            dv2 = _dot_tn(p.astype(BF16), dos)
            dp = _dot_nt(dos, vw)
            ds = (p * (dp - jnp.concatenate([delta_t, delta_t], axis=1))).astype(BF16)
            dq_ref[0, mine, :], dq_ref[1, mine, :] = _unstack_heads(_dot(ds, kw))
            dk2 = _dot_tn(ds, qs)
            dkv = jnp.where(_low_lanes(2 * BLK), dk2 + pltpu.roll(dk2, HD, axis=1), dv2 + pltpu.roll(dv2, HD, axis=1))
            dkv_ref[0, rows, :] = dkv_ref[0, rows, :] + dkv[BLK:]
            dkv_ref[0, prow, :] = dkv_ref[0, prow, :] + dkv[:BLK]
            return carry

        lax.fori_loop(0, BLOCKS_PER_CHUNK, block, 0, unroll=8)

    q_like = pl.BlockSpec((2, CHUNK_ROWS, LANES), lambda j, c: (j, c, 0))
    heads = pl.BlockSpec((GQ, CHUNK_ROWS, LANES), lambda j, c: (j, c, 0))
    kv = pl.BlockSpec((1, S, LANES), lambda j, c: (j // 2, 0, 0))
    per_kv = pl.BlockSpec((1, S, LANES), lambda j, c: (j, 0, 0))
    bias_spec = pl.BlockSpec((2, 1, GQ * BLK, 2 * BLK), lambda j, c: (0, j, 0, 0))
    return pl.pallas_call(
        body, grid=(NKV, S // CHUNK_ROWS), name=f"attn_bwd_d{d}",
        in_specs=[q_like, q_like, heads, heads, kv, kv, bias_spec],
        out_specs=[q_like, per_kv],
        out_shape=[_slabs(D // LANES), _slabs(NKV)],
        compiler_params=_params(("arbitrary", "arbitrary")),
    )(q, do, lse, delta, k, v, bias)


CONV_T = 128


def _halo_before(i):
    return (jnp.maximum(i * (CONV_T // HALO) - 1, 0), 0)


def _halo_after(i):
    return (jnp.minimum((i + 1) * (CONV_T // HALO), S // HALO - 1), 0)


SUBLANES = 8
NCH = D // LANES
GROUP = SUBLANES * SUBLANES


def _comb(ref, cb, base):
    return ref[cb, pl.ds(base, SUBLANES, stride=SUBLANES), :]


def _taps(w_ref, cols):
    return [jnp.broadcast_to(w_ref[j:j + 1, cols], (SUBLANES, LANES)) for j in range(CONV_K)]


def _conv_fwd(c_val, c_glu, c_gate, conv_w, conv_b, ln_g, ln_b):
    T = CONV_T

    def body(cv_ref, cg_ref, cvh_ref, cgh_ref, gate_ref, w_ref, b_ref, lg_ref, lb_ref, u_ref, y_ref, win, us):
        i = pl.program_id(0)
        for cb in range(NCH):
            cols = slice(cb * LANES, (cb + 1) * LANES)
            win[cb, HALO:HALO + T, :] = cv_ref[:, cols] * _sigmoid(cg_ref[:, cols])
            win[cb, 0:HALO, :] = jnp.where(i > 0, cvh_ref[:, cols] * _sigmoid(cgh_ref[:, cols]), 0.0)
        for cb in range(NCH):
            cols = slice(cb * LANES, (cb + 1) * LANES)
            taps = _taps(w_ref, cols)
            bias = jnp.broadcast_to(b_ref[:, cols], (SUBLANES, LANES))

            def group(g, carry):
                for b in range(SUBLANES):
                    base = g * GROUP + b
                    acc = bias
                    for j in range(CONV_K):
                        acc = acc + taps[j] * _comb(win, cb, base + (HALO - (CONV_K - 1) + j))
                    us[cb, pl.ds(base, SUBLANES, stride=SUBLANES), :] = acc
                return carry

            lax.fori_loop(0, T // GROUP, group, 0)
        total = us[0]
        for cb in range(1, NCH):
            total = total + us[cb]
        mu = jnp.sum(total, axis=-1, keepdims=True) * (1.0 / D)
        sq = jnp.zeros((T, LANES), F32)
        for cb in range(NCH):
            uc = us[cb] - mu
            sq = sq + uc * uc
        rstd = lax.rsqrt(jnp.sum(sq, axis=-1, keepdims=True) * (1.0 / D) + LN_EPS)
        for cb in range(NCH):
            cols = slice(cb * LANES, (cb + 1) * LANES)
            u = us[cb]
            u_ref[:, cols] = u
            nrm = (u - mu) * rstd * lg_ref[:, cols] + lb_ref[:, cols]
            gate = gate_ref[:, cols]
            y_ref[:, cols] = (nrm * _sigmoid(nrm) * (gate * _sigmoid(gate))).astype(BF16)

    halo = pl.BlockSpec((HALO, D), _halo_before)
    return pl.pallas_call(
        body, grid=(S // T,), name="conv_fwd",
        in_specs=[_rows(T, D), _rows(T, D), halo, halo, _rows(T, D),
                  _resident((HALO, D)), _resident((1, D)), _resident((1, D)), _resident((1, D))],
        out_specs=[_rows(T, D), _rows(T, D)],
        out_shape=[jax.ShapeDtypeStruct((S, D), F32), jax.ShapeDtypeStruct((S, D), BF16)],
        scratch_shapes=[pltpu.VMEM((NCH, T + HALO, LANES), F32), pltpu.VMEM((NCH, T, LANES), F32)],
        compiler_params=_params(("arbitrary",)),
    )(c_val, c_glu, c_val, c_glu, c_gate, conv_w, conv_b, ln_g, ln_b)


def _conv_bwd_rows(u, c_gate, dy_conv, ln_g, ln_b):
    tm = 256

    def body(u_ref, gate_ref, dy_ref, lg_ref, lb_ref, du_ref, dgate_ref, st_ref):
        @pl.when(pl.program_id(0) == 0)
        def _():
            st_ref[...] = jnp.zeros_like(st_ref)

        u, gate, dy = u_ref[...], gate_ref[...], dy_ref[...]
        mu = jnp.mean(u, axis=-1, keepdims=True)
        uc = u - mu
        rstd = lax.rsqrt(jnp.mean(uc * uc, axis=-1, keepdims=True) + LN_EPS)
        z = uc * rstd
        nrm = z * lg_ref[...] + lb_ref[...]
        sn, sg = _sigmoid(nrm), _sigmoid(gate)
        dgate_ref[...] = (dy * (nrm * sn) * (sg * (1.0 + gate * (1.0 - sg)))).astype(BF16)
        dn = dy * (gate * sg) * (sn * (1.0 + nrm * (1.0 - sn)))
        dz = dn * lg_ref[...]
        du = rstd * (dz - jnp.mean(dz, axis=-1, keepdims=True) - z * jnp.mean(dz * z, axis=-1, keepdims=True))
        du_ref[...] = du
        st_ref[0:1, :] += jnp.sum(dn * z, axis=0, keepdims=True)
        st_ref[1:2, :] += jnp.sum(dn, axis=0, keepdims=True)
        st_ref[2:3, :] += jnp.sum(du, axis=0, keepdims=True)

    big = jax.ShapeDtypeStruct((S, D), F32)
    return pl.pallas_call(
        body, grid=(S // tm,), name="conv_bwd_rows",
        in_specs=[_rows(tm, D)] * 3 + [_resident((1, D)), _resident((1, D))],
        out_specs=[_rows(tm, D), _rows(tm, D), pl.BlockSpec((8, D), lambda i: (0, 0))],
        out_shape=[big, jax.ShapeDtypeStruct((S, D), BF16), jax.ShapeDtypeStruct((8, D), F32)],
        compiler_params=_params(("arbitrary",)),
    )(u, c_gate, dy_conv, ln_g, ln_b)


def _conv_bwd_taps(du, c_val, c_glu, conv_w):
    T = CONV_T
    last = S // T - 1

    def body(du_ref, dua_ref, cv_ref, cg_ref, cvh_ref, cgh_ref, w_ref, dcv_ref, dcg_ref, dw_ref,
             hwin, dwin, dhs, dw_acc):
        i = pl.program_id(0)

        @pl.when(i == 0)
        def _():
            dw_acc[...] = jnp.zeros_like(dw_acc)

        for cb in range(NCH):
            cols = slice(cb * LANES, (cb + 1) * LANES)
            hwin[cb, HALO:HALO + T, :] = cv_ref[:, cols] * _sigmoid(cg_ref[:, cols])
            hwin[cb, 0:HALO, :] = jnp.where(i > 0, cvh_ref[:, cols] * _sigmoid(cgh_ref[:, cols]), 0.0)
            dwin[cb, 0:T, :] = du_ref[:, cols]
            dwin[cb, T:T + HALO, :] = jnp.where(i < last, dua_ref[:, cols], 0.0)
        for cb in range(NCH):
            cols = slice(cb * LANES, (cb + 1) * LANES)
            taps = _taps(w_ref, cols)

            def group_dh(g, carry):
                for b in range(SUBLANES):
                    base = g * GROUP + b
                    acc = jnp.zeros((SUBLANES, LANES), F32)
                    for j in range(CONV_K):
                        acc = acc + taps[j] * _comb(dwin, cb, base + (CONV_K - 1 - j))
                    dhs[cb, pl.ds(base, SUBLANES, stride=SUBLANES), :] = acc
                return carry

            lax.fori_loop(0, T // GROUP, group_dh, 0)

            def group_dw(g, sums):
                for b in range(SUBLANES):
                    base = g * GROUP + b
                    d = _comb(dwin, cb, base)
                    sums = tuple(sums[j] + d * _comb(hwin, cb, base + (HALO - (CONV_K - 1) + j))
                                 for j in range(CONV_K))
                return sums

            sums = lax.fori_loop(0, T // GROUP, group_dw, tuple(dw_acc[j, :, cols] for j in range(CONV_K)))
            for j in range(CONV_K):
                dw_acc[j, :, cols] = sums[j]
            dh = dhs[cb]
            cv, sg = cv_ref[:, cols], _sigmoid(cg_ref[:, cols])
            dcv_ref[:, cols] = (dh * sg).astype(BF16)
            dcg_ref[:, cols] = (dh * cv * (sg * (1.0 - sg))).astype(BF16)

        @pl.when(i == last)
        def _():
            dw_ref[...] = jnp.zeros_like(dw_ref)
            for j in range(CONV_K):
                dw_ref[j:j + 1, :] = jnp.sum(dw_acc[j], axis=0, keepdims=True)

    before = pl.BlockSpec((HALO, D), _halo_before)
    after = pl.BlockSpec((HALO, D), _halo_after)
    big = jax.ShapeDtypeStruct((S, D), BF16)
    return pl.pallas_call(
        body, grid=(S // T,), name="conv_bwd_taps",
        in_specs=[_rows(T, D), after, _rows(T, D), _rows(T, D), before, before, _resident((HALO, D))],
        out_specs=[_rows(T, D), _rows(T, D), pl.BlockSpec((HALO, D), lambda i: (0, 0))],
        out_shape=[big, big, jax.ShapeDtypeStruct((HALO, D), F32)],
        scratch_shapes=[pltpu.VMEM((NCH, T + HALO, LANES), F32), pltpu.VMEM((NCH, T + HALO, LANES), F32),
                        pltpu.VMEM((NCH, T, LANES), F32), pltpu.VMEM((CONV_K, SUBLANES, D), F32)],
        compiler_params=_params(("arbitrary",)),
    )(du, du, c_val, c_glu, c_val, c_glu, conv_w)


def _outproj_loss(y_att, y_conv, w_out_bf, x, target, gf):
    tm = 256

    def body(ya_ref, yc_ref, w_ref, x_ref, t_ref, gf_ref, dx2_ref, dya_ref, dyc_ref, dw_ref, st_ref, acc):
        @pl.when(pl.program_id(0) == 0)
        def _():
            acc[...] = jnp.zeros_like(acc)
            st_ref[...] = jnp.zeros_like(st_ref)

        ya, yc = ya_ref[...], yc_ref[...]
        x2 = x_ref[...] + _dot(ya, w_ref[0:D, :]) + _dot(yc, w_ref[D:2 * D, :])
        r = lax.rsqrt(jnp.mean(x2 * x2, axis=-1, keepdims=True) + NORM_EPS)
        xn = x2 * r
        err = xn * gf_ref[...] - t_ref[...]
        dout = err * (1.0 / D)
        dxn = dout * gf_ref[...]
        dx2 = r * (dxn - xn * jnp.mean(dxn * xn, axis=-1, keepdims=True))
        dx2_ref[...] = dx2
        dx2b = dx2.astype(BF16)
        dya_ref[...] = _dot_nt(dx2b, w_ref[0:D, :])
        dyc_ref[...] = _dot_nt(dx2b, w_ref[D:2 * D, :])
        acc[0:D, :] += _dot_tn(ya, dx2b)
        acc[D:2 * D, :] += _dot_tn(yc, dx2b)
        st_ref[0:1, :] += jnp.sum(dout * xn, axis=0, keepdims=True)
        st_ref[1:2, :] += jnp.sum(err * err, axis=0, keepdims=True) * (0.5 / D)

        @pl.when(pl.program_id(0) == S // tm - 1)
        def _():
            dw_ref[...] = acc[...].astype(BF16)

    big = jax.ShapeDtypeStruct((S, D), F32)
    return pl.pallas_call(
        body, grid=(S // tm,), name="outproj_loss",
        in_specs=[_rows(tm, D), _rows(tm, D), _resident((WOUT_ROWS, D)), _rows(tm, D), _rows(tm, D), _resident((1, D))],
        out_specs=[_rows(tm, D), _rows(tm, D), _rows(tm, D),
                   pl.BlockSpec((WOUT_ROWS, D), lambda i: (0, 0)), pl.BlockSpec((8, D), lambda i: (0, 0))],
        out_shape=[big, big, big, jax.ShapeDtypeStruct((WOUT_ROWS, D), BF16), jax.ShapeDtypeStruct((8, D), F32)],
        scratch_shapes=[pltpu.VMEM((WOUT_ROWS, D), F32)],
        compiler_params=_params(("arbitrary",)),
    )(y_att, y_conv, w_out_bf, x, target, gf)


def _assemble_dproj(dqs, dkvs, dag, dcv, dcg, dcgate):
    tm = 512

    def body(dq1, dq2, dq3, dkv1, dkv2, dkv3, dag_ref, dcv_ref, dcg_ref, dcgate_ref, dp_ref):
        for sl in range(D // LANES):
            dq = (dq1[sl] + dq2[sl] + dq3[sl]) * (HD ** -0.5)
            dp_ref[:, OFF_Q + sl * LANES:OFF_Q + (sl + 1) * LANES] = dq.astype(BF16)
        for j in range(NKV):
            dkv_j = (dkv1[j] + dkv2[j] + dkv3[j]).astype(BF16)
            dp_ref[:, OFF_K + j * HD:OFF_K + (j + 1) * HD] = dkv_j[:, :HD]
            dp_ref[:, OFF_V + j * HD:OFF_V + (j + 1) * HD] = dkv_j[:, HD:]
        for off, ref in ((OFF_AG, dag_ref), (OFF_CV, dcv_ref), (OFF_CG, dcg_ref), (OFF_CGATE, dcgate_ref)):
            dp_ref[:, off:off + D] = ref[...]

    return pl.pallas_call(
        body, grid=(S // tm,), name="assemble_dproj",
        in_specs=[_slab_rows(D // LANES, tm)] * 3 + [_slab_rows(NKV, tm)] * 3 + [_rows(tm, D)] * 4,
        out_specs=_rows(tm, NCOL),
        out_shape=jax.ShapeDtypeStruct((S, NCOL), BF16),
        compiler_params=_params(("arbitrary",)),
    )(*dqs, *dkvs, dag, dcv, dcg, dcgate)


def _inproj_bwd_x(dproj, w_bf, x, g1, dx2, pi, po, ps):
    tm = 256
    last = S // tm - 1

    def body(dp_ref, w_ref, x_ref, g_ref, dx2_ref, pi_ref, po_ref, ps_ref,
             gx_ref, st_ref, ri_ref, ro_ref, rs_ref, send, recv):
        i = pl.program_id(0)
        copies = _chip_exchange_copies((pi_ref, po_ref, ps_ref), (ri_ref, ro_ref, rs_ref), send, recv)

        @pl.when(i == 0)
        def _():
            st_ref[...] = jnp.zeros_like(st_ref)
            for out, _ in copies:
                out.start()

        dh = _dot_nt(dp_ref[...], w_ref[...])
        xt = x_ref[...]
        r = lax.rsqrt(jnp.mean(xt * xt, axis=-1, keepdims=True) + NORM_EPS)
        xn = xt * r
        dxn = dh * g_ref[...]
        gx_ref[...] = dx2_ref[...] + r * (dxn - xn * jnp.mean(dxn * xn, axis=-1, keepdims=True))
        st_ref[0:1, :] += jnp.sum(dh * xn, axis=0, keepdims=True)

        @pl.when(i == last)
        def _():
            for _, arrival in copies:
                arrival.wait_recv()
            for out, _ in copies:
                out.wait_send()

    n = 3 * len(CHIP_FLIPS)
    return pl.pallas_call(
        body, grid=(S // tm,), name="inproj_bwd_x",
        in_specs=[_rows(tm, NCOL), _resident((D, NCOL)), _rows(tm, D), _resident((1, D)), _rows(tm, D), ANY, ANY, ANY],
        out_specs=[_rows(tm, D), pl.BlockSpec((8, D), lambda i: (0, 0)), ANY, ANY, ANY],
        out_shape=[jax.ShapeDtypeStruct((S, D), F32), jax.ShapeDtypeStruct((8, D), F32),
                   jax.ShapeDtypeStruct((NCHIP, D // 2, CHUNK), BF16),
                   jax.ShapeDtypeStruct((NCHIP, WOUT_SHARD // 2, D), BF16),
                   jax.ShapeDtypeStruct((NCHIP, SMALL_ROWS, D), F32)],
        scratch_shapes=[pltpu.SemaphoreType.DMA((n,)), pltpu.SemaphoreType.DMA((n,))],
        compiler_params=_params(("arbitrary",)),
    )(dproj, w_bf, x, g1, dx2, pi, po, ps)


def _inproj_bwd_w(h, dproj):
    tk = 1024
    nk = S // tk

    def body(h_ref, dp_ref, o_ref, acc):
        i = pl.program_id(1)

        @pl.when(i == 0)
        def _():
            acc[...] = jnp.zeros_like(acc)

        acc[...] += _dot_tn(h_ref[...], dp_ref[...])

        @pl.when(i == nk - 1)
        def _():
            o_ref[0] = acc[...].astype(BF16)

    return pl.pallas_call(
        body, grid=(NCHIP, nk), name="inproj_bwd_w",
        in_specs=[pl.BlockSpec((tk, D), lambda c, i: (i, 0)), pl.BlockSpec((tk, CHUNK), lambda c, i: (i, c))],
        out_specs=pl.BlockSpec((1, D, CHUNK), lambda c, i: (c, 0, 0)),
        out_shape=jax.ShapeDtypeStruct((NCHIP, D, CHUNK), BF16),
        scratch_shapes=[pltpu.VMEM((D, CHUNK), F32)],
        compiler_params=_params(("arbitrary", "arbitrary")),
    )(h, dproj)


def _local_step(x, target, g1, w_in_bf, conv_w, conv_b, ln_g, ln_b, w_out_bf, gf):
    h, q, k, v, a_gate, c_val, c_glu, c_gate = _inproj_fwd(x, g1, w_in_bf)
    tables = [_bias_table(d) for d in PATTERNS]
    outs, lses = zip(*[_attn_fwd(q, k, v, t, d) for t, d in zip(tables, PATTERNS)])
    o, lse, y_att = _attn_combine(outs, lses, a_gate)
    u, y_conv = _conv_fwd(c_val, c_glu, c_gate, conv_w, conv_b, ln_g, ln_b)
    dx2, dy_att, dy_conv, dw_out, st_out = _outproj_loss(y_att, y_conv, w_out_bf, x, target, gf)

    do, da_gate, delta = _attn_gate_bwd(dy_att, o, a_gate, _head_sum_selectors())
    dqs, dkvs = zip(*[_attn_bwd(q, k, v, do, lse, delta, t, d) for t, d in zip(tables, PATTERNS)])

    du, dc_gate, st_conv = _conv_bwd_rows(u, c_gate, dy_conv, ln_g, ln_b)
    dc_val, dc_glu, dconv_w = _conv_bwd_taps(du, c_val, c_glu, conv_w)

    dproj = _assemble_dproj(dqs, dkvs, da_gate, dc_val, dc_glu, dc_gate)
    dw_in = _inproj_bwd_w(h, dproj)
    small = jnp.concatenate([st_conv, st_out, dconv_w], axis=0)
    return dw_in, dw_out, small, dproj, dx2


ROW_LN_G, ROW_LN_B, ROW_CONV_B, ROW_FINAL_G, ROW_LOSS, ROW_TAPS = 0, 1, 2, 8, 9, 16
SMALL_ROWS = 16 + HALO
NDEV = 8


MESH = pl.DeviceIdType.MESH
ANY = pl.BlockSpec(memory_space=pl.ANY)
CHIP_FLIPS = ((1, 0), (0, 1), (1, 1))


def _pos():
    return lax.axis_index("x"), lax.axis_index("y"), lax.axis_index("c")


def _flip(v, f):
    return 1 - v if f else v


def _ds(start, size, align=None):
    return pl.ds(pl.multiple_of(start, align or size), size)


def _place_shards(wi, wo, cw, where):
    steps = 4

    def body(where_ref, wi_ref, wo_ref, cw_ref, wi_full, wo_full, cw_full):
        wi_full[...] = wi_ref[...].astype(BF16)
        wo_full[...] = wo_ref[...].astype(BF16)
        cw_full[...] = cw_ref[...]

    grid_spec = pltpu.PrefetchScalarGridSpec(
        num_scalar_prefetch=1, grid=(steps,),
        in_specs=[pl.BlockSpec((D // steps, CHUNK), lambda i, w: (i, 0)),
                  pl.BlockSpec((WOUT_SHARD // steps, D), lambda i, w: (i, 0)),
                  pl.BlockSpec((HALO, CONVW_SHARD), lambda i, w: (0, 0))],
        out_specs=[pl.BlockSpec((D // steps, CHUNK), lambda i, w: (i, w[0])),
                   pl.BlockSpec((WOUT_SHARD // steps, D), lambda i, w: (w[0] * steps + i, 0)),
                   pl.BlockSpec((HALO, CONVW_SHARD), lambda i, w: (0, w[0]))])
    return pl.pallas_call(
        body, grid_spec=grid_spec, name="place_shards",
        out_shape=[jax.ShapeDtypeStruct((D, NCOL), BF16), jax.ShapeDtypeStruct((WOUT_ROWS, D), BF16),
                   jax.ShapeDtypeStruct((HALO, D), F32)],
        compiler_params=_params(("arbitrary",)),
    )(where, wi, wo, cw)


def _gather_weights(wi_full, wo_full, cw_full):
    halves = (D // 2, WOUT_SHARD // 2, HALO // 2)
    OWN_X, OWN_Y, VIA_Y, VIA_X = range(4)

    def body(_wi, _wo, _cw, wi_full, wo_full, cw_full, send, recv):
        x, y, c = _pos()
        x_nbr, y_nbr, diag = (1 - x, y), (x, 1 - y), (1 - x, 1 - y)

        def region(a, chip_xy, half, part=None):
            chip = 2 * chip_xy[0] + chip_xy[1]
            n, row = halves[a], half * halves[a]
            if part is not None:
                n = n // 2
                row = row + part * n
            if a == 0:
                return wi_full.at[_ds(row, n), _ds(chip * CHUNK, CHUNK, 128)]
            if a == 1:
                return wo_full.at[_ds(chip * WOUT_SHARD + row, n), :]
            return cw_full.at[_ds(row, n), _ds(chip * CONVW_SHARD, CONVW_SHARD, 128)]

        def copy(a, kind, piece, dev):
            k = 8 * a + kind
            return pltpu.make_async_remote_copy(src_ref=piece, dst_ref=piece, send_sem=send.at[k], recv_sem=recv.at[k],
                                                device_id=dev, device_id_type=MESH)

        def to_sibling(a, kind, piece):
            cp = copy(a, 4 + kind, piece, (x, y, 1 - c))
            cp.start()
            return cp

        sends = []
        for a in range(3):
            for kind, nbr in ((OWN_X, x_nbr), (OWN_Y, y_nbr)):
                cp = copy(a, kind, region(a, (x, y), c), (*nbr, c))
                cp.start()
                sends.append(cp)
        for a in range(3):
            got = region(a, x_nbr, c)
            copy(a, OWN_X, got, (*x_nbr, c)).wait_recv()
            onward = copy(a, VIA_Y, region(a, x_nbr, c, 0), (*y_nbr, c))
            onward.start()
            sends += [onward, to_sibling(a, OWN_X, got)]
            got = region(a, y_nbr, c)
            copy(a, OWN_Y, got, (*y_nbr, c)).wait_recv()
            onward = copy(a, VIA_X, region(a, y_nbr, c, 1), (*x_nbr, c))
            onward.start()
            sends += [onward, to_sibling(a, OWN_Y, got)]
        for a in range(3):
            got = region(a, diag, c, 0)
            copy(a, VIA_Y, got, (*y_nbr, c)).wait_recv()
            sends.append(to_sibling(a, VIA_Y, got))
            got = region(a, diag, c, 1)
            copy(a, VIA_X, got, (*x_nbr, c)).wait_recv()
            sends.append(to_sibling(a, VIA_X, got))
        for a in range(3):
            for kind, piece in ((OWN_X, region(a, x_nbr, 1 - c)), (OWN_Y, region(a, y_nbr, 1 - c)),
                                (VIA_Y, region(a, diag, 1 - c, 0)), (VIA_X, region(a, diag, 1 - c, 1))):
                copy(a, 4 + kind, piece, (x, y, 1 - c)).wait_recv()
        for cp in sends:
            cp.wait_send()

    n_sems = 3 * 8
    return pl.pallas_call(
        body, name="gather_weights",
        in_specs=[ANY, ANY, ANY], out_specs=[ANY, ANY, ANY], input_output_aliases={0: 0, 1: 1, 2: 2},
        out_shape=[jax.ShapeDtypeStruct((D, NCOL), BF16), jax.ShapeDtypeStruct((WOUT_ROWS, D), BF16),
                   jax.ShapeDtypeStruct((HALO, D), F32)],
        scratch_shapes=[pltpu.SemaphoreType.DMA((n_sems,)), pltpu.SemaphoreType.DMA((n_sems,))],
    )(wi_full, wo_full, cw_full)


def _exchange_halves(gi4, go4, small):
    def body(gi_ref, go_ref, sm_ref, ri_ref, ro_ref, rs_ref, send, recv):
        x, y, c = _pos()
        sib = (x, y, 1 - c)
        copies = [
            (gi_ref.at[:, _ds((1 - c) * (D // 2), D // 2), :], ri_ref),
            (go_ref.at[:, _ds((1 - c) * (WOUT_SHARD // 2), WOUT_SHARD // 2), :], ro_ref),
            (sm_ref, rs_ref),
        ]
        cps = [pltpu.make_async_remote_copy(src_ref=s_, dst_ref=d_, send_sem=send.at[k], recv_sem=recv.at[k],
                                            device_id=sib, device_id_type=MESH) for k, (s_, d_) in enumerate(copies)]
        for cp in cps:
            cp.start()
        for cp in cps:
            cp.wait()

    return pl.pallas_call(
        body, name="exchange_halves",
        in_specs=[ANY, ANY, ANY], out_specs=[ANY, ANY, ANY],
        out_shape=[jax.ShapeDtypeStruct((NCHIP, D // 2, CHUNK), BF16),
                   jax.ShapeDtypeStruct((NCHIP, WOUT_SHARD // 2, D), BF16),
                   jax.ShapeDtypeStruct((SMALL_ROWS, D), F32)],
        scratch_shapes=[pltpu.SemaphoreType.DMA((3,)), pltpu.SemaphoreType.DMA((3,))],
    )(gi4, go4, small)


def _add_halves(gi4, ri, go4, ro, small, rs):
    hi, ho = D // 2, WOUT_SHARD // 2

    def body(gi_ref, ri_ref, go_ref, ro_ref, sm_ref, rs_ref, pi_ref, po_ref, ps_ref):
        c = lax.axis_index("c")
        pi_ref[0] = (gi_ref[0, _ds(c * hi, hi), :].astype(F32) + ri_ref[0].astype(F32)).astype(BF16)
        po_ref[0] = (go_ref[0, _ds(c * ho, ho), :].astype(F32) + ro_ref[0].astype(F32)).astype(BF16)
        ps_ref[...] = sm_ref[...] + rs_ref[...]

    blk = lambda n, w: pl.BlockSpec((1, n, w), lambda k: (k, 0, 0))
    whole = pl.BlockSpec((SMALL_ROWS, D), lambda k: (0, 0))
    return pl.pallas_call(
        body, grid=(NCHIP,), name="add_halves",
        in_specs=[blk(D, CHUNK), blk(hi, CHUNK), blk(WOUT_SHARD, D), blk(ho, D), whole, whole],
        out_specs=[blk(hi, CHUNK), blk(ho, D), whole],
        out_shape=[jax.ShapeDtypeStruct((NCHIP, hi, CHUNK), BF16), jax.ShapeDtypeStruct((NCHIP, ho, D), BF16),
                   jax.ShapeDtypeStruct((SMALL_ROWS, D), F32)],
        compiler_params=_params(("arbitrary",)),
    )(gi4, ri, go4, ro, small, rs)


def _chip_exchange_copies(srcs, dsts, send, recv):
    x, y, c = _pos()
    me = 2 * x + y
    pairs = []
    for a in range(3):
        for j, (fx, fy) in enumerate(CHIP_FLIPS):
            px, py = _flip(x, fx), _flip(y, fy)
            peer = 2 * px + py
            k = 3 * a + j
            out = pltpu.make_async_remote_copy(
                src_ref=srcs[a] if a == 2 else srcs[a].at[peer], dst_ref=dsts[a].at[me],
                send_sem=send.at[k], recv_sem=recv.at[k], device_id=(px, py, c), device_id_type=MESH)
            got = dsts[a].at[peer]
            arrival = pltpu.make_async_remote_copy(
                src_ref=got, dst_ref=got, send_sem=send.at[k], recv_sem=recv.at[k],
                device_id=(px, py, c), device_id_type=MESH)
            pairs.append((out, arrival))
    return pairs


def _sum_chips(ri, ro, rs, pi, po, ps, where):
    def body(w_ref, ri_ref, ro_ref, rs_ref, pi_ref, po_ref, ps_ref, gi_ref, go_ref, gs_ref, g5_ref, loss_ref,
             acc_i, acc_o, acc_s):
        k = pl.program_id(0)
        accs = (acc_i, acc_o, acc_s)

        @pl.when(k == 0)
        def _():
            for acc in accs:
                acc[...] = jnp.zeros_like(acc)

        @pl.when(k == w_ref[0])
        def _():
            for acc, val in zip(accs, (pi_ref[0], po_ref[0], ps_ref[...])):
                acc[...] += val.astype(F32)

        @pl.when(k != w_ref[0])
        def _():
            for acc, ref in zip(accs, (ri_ref, ro_ref, rs_ref)):
                acc[...] += ref[0].astype(F32)

        @pl.when(k == NCHIP - 1)
        def _():
            gi_ref[0] = acc_i[...]
            go_ref[0] = acc_o[...]
            gs_ref[...] = acc_s[...]
            g5_ref[...] = jnp.zeros_like(g5_ref)
            for i, row in enumerate((ROW_CONV_B, ROW_LN_G, ROW_LN_B, ROW_FINAL_G)):
                g5_ref[i + 1:i + 2, :] = acc_s[row:row + 1, :]
            loss = jnp.sum(acc_s[ROW_LOSS:ROW_LOSS + 1, :], axis=1, keepdims=True)
            loss_ref[...] = jnp.broadcast_to(loss, loss_ref.shape)

    def sent(k, w):
        return jnp.where(k == w[0], (k + 1) % NCHIP, k)

    hi, ho = D // 2, WOUT_SHARD // 2
    const = lambda shape: pl.BlockSpec(shape, lambda k, w: (0,) * len(shape))
    grid_spec = pltpu.PrefetchScalarGridSpec(
        num_scalar_prefetch=1, grid=(NCHIP,),
        in_specs=[pl.BlockSpec((1, hi, CHUNK), lambda k, w: (sent(k, w), 0, 0)),
                  pl.BlockSpec((1, ho, D), lambda k, w: (sent(k, w), 0, 0)),
                  pl.BlockSpec((1, SMALL_ROWS, D), lambda k, w: (sent(k, w), 0, 0)),
                  pl.BlockSpec((1, hi, CHUNK), lambda k, w: (w[0], 0, 0)),
                  pl.BlockSpec((1, ho, D), lambda k, w: (w[0], 0, 0)),
                  const((SMALL_ROWS, D))],
        out_specs=[pl.BlockSpec((1, hi, CHUNK), lambda k, w: (w[1], 0, 0)),
                   pl.BlockSpec((1, ho, D), lambda k, w: (w[1], 0, 0)),
                   const((SMALL_ROWS, D)), const((8, D)), const((8, LANES))],
        scratch_shapes=[pltpu.VMEM((hi, CHUNK), F32), pltpu.VMEM((ho, D), F32), pltpu.VMEM((SMALL_ROWS, D), F32)])
    return pl.pallas_call(
        body, grid_spec=grid_spec, name="sum_chips",
        out_shape=[jax.ShapeDtypeStruct((2, hi, CHUNK), F32), jax.ShapeDtypeStruct((2, ho, D), F32),
                   jax.ShapeDtypeStruct((SMALL_ROWS, D), F32), jax.ShapeDtypeStruct((8, D), F32),
                   jax.ShapeDtypeStruct((8, LANES), F32)],
        compiler_params=_params(("arbitrary",)),
    )(where, ri, ro, rs, pi, po, ps)


def _exchange_results(gi2, go2, st):
    flips = [(fx, fy, fc) for fx in (0, 1) for fy in (0, 1) for fc in (0, 1)][1:]

    def body(_gi, _go, st_ref, gi_ref, go_ref, all_ref, send, recv, lsem):
        x, y, c = _pos()
        sib = (x, y, 1 - c)

        def half(k, ref, slot):
            return pltpu.make_async_remote_copy(src_ref=ref.at[slot], dst_ref=ref.at[slot], send_sem=send.at[k],
                                                recv_sem=recv.at[k], device_id=sib, device_id_type=MESH)

        def stat(k, src, slot, dev):
            return pltpu.make_async_remote_copy(src_ref=src, dst_ref=all_ref.at[slot], send_sem=send.at[k],
                                                recv_sem=recv.at[k], device_id=dev, device_id_type=MESH)

        mine = pltpu.make_async_copy(st_ref, all_ref.at[4 * x + 2 * y + c], lsem)
        mine.start()
        sends = [half(k, ref, c) for k, ref in enumerate((gi_ref, go_ref))]
        peers = [(_flip(x, fx), _flip(y, fy), _flip(c, fc)) for fx, fy, fc in flips]
        sends += [stat(2 + k, st_ref, 4 * x + 2 * y + c, dev) for k, dev in enumerate(peers)]
        for cp in sends:
            cp.start()
        for k, ref in enumerate((gi_ref, go_ref)):
            half(k, ref, 1 - c).wait_recv()
        for k, (px, py, pc) in enumerate(peers):
            slot = 4 * px + 2 * py + pc
            stat(2 + k, all_ref.at[slot], slot, (px, py, pc)).wait_recv()
        for cp in sends:
            cp.wait_send()
        mine.wait()

    n = 2 + len(flips)
    return pl.pallas_call(
        body, name="exchange_results",
        in_specs=[ANY, ANY, ANY], out_specs=[ANY, ANY, ANY], input_output_aliases={0: 0, 1: 1},
        out_shape=[jax.ShapeDtypeStruct((2, D // 2, CHUNK), F32), jax.ShapeDtypeStruct((2, WOUT_SHARD // 2, D), F32),
                   jax.ShapeDtypeStruct((NDEV, 8, D), F32)],
        scratch_shapes=[pltpu.SemaphoreType.DMA((n,)), pltpu.SemaphoreType.DMA((n,)), pltpu.SemaphoreType.DMA],
    )(gi2, go2, st)


def _adamw_math(w, g, m, v):
    m2 = ADAM_B1 * m + (1.0 - ADAM_B1) * g
    v2 = ADAM_B2 * v + (1.0 - ADAM_B2) * (g * g)
    m_hat = m2 / (1.0 - ADAM_B1 ** ADAM_STEP)
    v_hat = v2 / (1.0 - ADAM_B2 ** ADAM_STEP)
    delta = -ADAM_LR * (m_hat / (jnp.sqrt(v_hat) + ADAM_EPS) + ADAM_WD * w)
    return delta, m2, v2


def _adamw(w, g, m, v, name):
    rows, cols = w.shape
    tm = 256 if rows % 256 == 0 else rows

    def body(w_ref, g_ref, m_ref, v_ref, d_ref, m2_ref, v2_ref):
        d_ref[...], m2_ref[...], v2_ref[...] = _adamw_math(w_ref[...], g_ref[...], m_ref[...], v_ref[...])

    shape = jax.ShapeDtypeStruct(w.shape, F32)
    return pl.pallas_call(
        body, grid=(rows // tm,), name=name,
        in_specs=[_rows(tm, cols)] * 4, out_specs=[_rows(tm, cols)] * 3, out_shape=[shape] * 3,
        compiler_params=_params(("arbitrary",)),
    )(w, g, m, v)


def _adamw_vectors(g5, first_parts, ws, ms, vs):
    n = len(ws)

    def body(g_ref, parts_ref, *refs):
        ins, g0_ref, outs = refs[:3 * n], refs[3 * n], refs[3 * n + 1:]
        g0 = parts_ref[0, 0:1, :]
        for dev in range(1, NDEV):
            g0 = g0 + parts_ref[dev, 0:1, :]
        g0_ref[...] = g0
        for i in range(n):
            g = g0 if i == 0 else g_ref[i:i + 1, :]
            res = _adamw_math(ins[i][...], g, ins[n + i][...], ins[2 * n + i][...])
            for kind in range(3):
                outs[kind * n + i][...] = res[kind]

    shape = jax.ShapeDtypeStruct((1, D), F32)
    return pl.pallas_call(body, name="adamw_vectors", out_shape=[shape] * (1 + 3 * n), compiler_params=_params())(
        g5, first_parts, *ws, *ms, *vs)


def kernel(x, norm_g, w_in, conv_w, conv_b, conv_ln_g, conv_ln_b, w_out, final_norm_g, loss_target, m_norm_g, m_w_in, m_conv_w, m_conv_b, m_conv_ln_g, m_conv_ln_b, m_w_out, m_final_norm_g, v_norm_g, v_w_in, v_conv_w, v_conv_b, v_conv_ln_g, v_conv_ln_b, v_w_out, v_final_norm_g):
    chip = 2 * lax.axis_index("x") + lax.axis_index("y")
    where = jnp.stack([chip, lax.axis_index("c")]).astype(jnp.int32)
    taps_shard = jnp.pad(conv_w[0], ((0, HALO - CONV_K), (0, 0)))
    wi_full, wo_full, cw_full = _gather_weights(*_place_shards(w_in[0], w_out[0], taps_shard, where))

    gf = final_norm_g[None]
    dw_in4, dw_out, small, dproj, dx2 = _local_step(
        x[0], loss_target[0], norm_g, wi_full, cw_full, conv_b, conv_ln_g, conv_ln_b, wo_full, gf)
    dw_out4 = dw_out.reshape(NCHIP, WOUT_SHARD, D)

    ri, ro, rs = _exchange_halves(dw_in4, dw_out4, small)
    pi, po, ps = _add_halves(dw_in4, ri, dw_out4, ro, small, rs)
    grad_x, st_in, ri, ro, rs = _inproj_bwd_x(dproj, wi_full, x[0], norm_g, dx2, pi, po, ps)
    gi2, go2, g_small, g5, loss8 = _sum_chips(ri, ro, rs, pi, po, ps, where)
    gi2, go2, norm_g_parts = _exchange_results(gi2, go2, st_in)
    g_w_in = gi2.reshape(D, CHUNK)
    g_w_out = go2.reshape(WOUT_SHARD, D)
    g_taps = lax.dynamic_slice(g_small, (ROW_TAPS, chip * CONVW_SHARD), (CONV_K, CONVW_SHARD))

    d_w_in, m2_w_in, v2_w_in = _adamw(w_in[0], g_w_in, m_w_in[0], v_w_in[0], "adamw_w_in")
    d_w_out, m2_w_out, v2_w_out = _adamw(w_out[0], g_w_out, m_w_out[0], v_w_out[0], "adamw_w_out")
    d_taps, m2_taps, v2_taps = _adamw(conv_w[0], g_taps, m_conv_w[0], v_conv_w[0], "adamw_conv_w")
    g_norm, *vec = _adamw_vectors(
        g5, norm_g_parts,
        (norm_g, conv_b, conv_ln_g, conv_ln_b, gf),
        (m_norm_g, m_conv_b, m_conv_ln_g, m_conv_ln_b, m_final_norm_g[None]),
        (v_norm_g, v_conv_b, v_conv_ln_g, v_conv_ln_b, v_final_norm_g[None]))
    d_vec, m2_vec, v2_vec = vec[0:5], vec[5:10], vec[10:15]

    def weight_order(ng, wi, cw, cb, lg, lb, wo, fg):
        return (ng, wi[None], cw[None], cb, lg, lb, wo[None], fg[0])

    grads = weight_order(g_norm, g_w_in, g_taps, g5[1:2], g5[2:3], g5[3:4], g_w_out, g5[4:5])
    deltas = weight_order(d_vec[0], d_w_in, d_taps, d_vec[1], d_vec[2], d_vec[3], d_w_out, d_vec[4])
    new_m = weight_order(m2_vec[0], m2_w_in, m2_taps, m2_vec[1], m2_vec[2], m2_vec[3], m2_w_out, m2_vec[4])
    new_v = weight_order(v2_vec[0], v2_w_in, v2_taps, v2_vec[1], v2_vec[2], v2_vec[3], v2_w_out, v2_vec[4])
    return (loss8[0, 0], grad_x[None], *grads, *deltas, *new_m, *new_v)
```

```python
import jax
import jax.numpy as jnp
from jax import lax
from jax.experimental import pallas as pl
from jax.experimental.pallas import tpu as pltpu

F32 = jnp.float32
BF16 = jnp.bfloat16

S = 4096
D = 1024
LANES = 128
HD = 64
NKV = 4
GQ = 4
KVW = NKV * HD
NCOL = 5632
CONV_K = 31
HALO = 32
BLK = 128
PATTERNS = (1, 4, 16)
NORM_EPS = 1e-6
LN_EPS = 1e-5
NEG = -1e30
OFF_Q, OFF_K, OFF_V, OFF_AG, OFF_CV, OFF_CG, OFF_CGATE = 0, 1024, 1280, 1536, 2560, 3584, 4608
NCHIP = 4
CHUNK = NCOL // NCHIP
WOUT_ROWS = 2 * D
WOUT_SHARD = WOUT_ROWS // NCHIP
CONVW_SHARD = D // NCHIP

ADAM_LR, ADAM_B1, ADAM_B2, ADAM_EPS, ADAM_WD, ADAM_STEP = 0.001, 0.9, 0.999, 1e-08, 0.01, 10

VMEM_LIMIT = 56 * 1024 * 1024


def _params(sem=None, vmem=VMEM_LIMIT):
    return pltpu.CompilerParams(dimension_semantics=sem, vmem_limit_bytes=vmem)


def _sigmoid(a):
    return 0.5 * jnp.tanh(0.5 * a) + 0.5


def _rows(tm, width):
    return pl.BlockSpec((tm, width), lambda i: (i, 0))


def _slabs(n):
    return jax.ShapeDtypeStruct((n, S, LANES), F32)


def _slab_rows(n, tm):
    return pl.BlockSpec((n, tm, LANES), lambda i: (0, i, 0))


def _resident(shape):
    return pl.BlockSpec(shape, lambda *_: (0,) * len(shape), pipeline_mode=pl.Buffered(1))


def _dot(a, b):
    return jnp.dot(a, b, preferred_element_type=F32)


def _dot_nt(a, b):
    return lax.dot_general(a, b, (((1,), (1,)), ((), ())), preferred_element_type=F32)


def _dot_tn(a, b):
    return lax.dot_general(a, b, (((0,), (0,)), ((), ())), preferred_element_type=F32)


def _inproj_fwd(x, g1, w_bf):
    tm = 512

    def body(x_ref, g_ref, w_ref, h_ref, q_ref, k_ref, v_ref, ag_ref, cv_ref, cg_ref, cgate_ref):
        xt = x_ref[...]
        r = lax.rsqrt(jnp.mean(xt * xt, axis=-1, keepdims=True) + NORM_EPS)
        h = (xt * r * g_ref[...]).astype(BF16)
        h_ref[...] = h
        q = _dot(h, w_ref[:, OFF_Q:OFF_Q + D]) * (HD ** -0.5)
        kv = _dot(h, w_ref[:, OFF_K:OFF_K + 2 * KVW])
        for sl in range(D // LANES):
            q_ref[sl] = q[:, sl * LANES:(sl + 1) * LANES]
        for sl in range(KVW // LANES):
            k_ref[sl] = kv[:, sl * LANES:(sl + 1) * LANES]
            v_ref[sl] = kv[:, KVW + sl * LANES:KVW + (sl + 1) * LANES]
        ag_ref[...] = _dot(h, w_ref[:, OFF_AG:OFF_AG + D])
        cv_ref[...] = _dot(h, w_ref[:, OFF_CV:OFF_CV + D])
        cg_ref[...] = _dot(h, w_ref[:, OFF_CG:OFF_CG + D])
        cgate_ref[...] = _dot(h, w_ref[:, OFF_CGATE:OFF_CGATE + D])

    big = jax.ShapeDtypeStruct((S, D), F32)
    return pl.pallas_call(
        body, grid=(S // tm,), name="inproj_fwd",
        in_specs=[_rows(tm, D), _resident((1, D)), _resident((D, NCOL))],
        out_specs=[_rows(tm, D), _slab_rows(D // LANES, tm), _slab_rows(KVW // LANES, tm), _slab_rows(KVW // LANES, tm),
                   _rows(tm, D), _rows(tm, D), _rows(tm, D), _rows(tm, D)],
        out_shape=[jax.ShapeDtypeStruct((S, D), BF16), _slabs(D // LANES), _slabs(KVW // LANES), _slabs(KVW // LANES),
                   big, big, big, big],
        compiler_params=_params(("arbitrary",)),
    )(x, g1, w_bf)


def _bias_table(d):
    h = jnp.arange(NKV * GQ, dtype=F32)
    slopes = jnp.exp2(-8.0 * (h + 1.0) / (NKV * GQ))
    qi = jnp.arange(BLK)[:, None]
    kj = jnp.arange(2 * BLK)[None, :]
    dist = BLK + qi - kj
    window = (dist >= 0) & (dist <= BLK)
    bias = -slopes[:, None, None] * (dist * d).astype(F32)[None]
    has_prev = jnp.stack([jnp.broadcast_to(kj >= BLK, (BLK, 2 * BLK)), jnp.ones((BLK, 2 * BLK), bool)])
    valid = window[None] & has_prev
    tab = jnp.where(valid[:, None], bias[None], NEG)
    return tab.reshape(2, NKV, GQ * BLK, 2 * BLK)


def _sub_rows(start, d):
    if d == 1:
        return pl.ds(pl.multiple_of(start, BLK), BLK)
    return pl.ds(start, BLK, stride=d)


NHEAD = NKV * GQ
CHUNK_ROWS = 2048
BLOCKS_PER_CHUNK = CHUNK_ROWS // BLK


def _low_lanes(rows=BLK):
    return lax.broadcasted_iota(jnp.int32, (rows, LANES), 1) < HD


def _block_start(idx, d):
    shift = d.bit_length() - 1
    b, r = lax.shift_right_logical(idx, shift), lax.bitwise_and(idx, d - 1)
    start = b * (BLK * d) + r
    return b, start, jnp.maximum(start - BLK * d, r)


def _stack_heads(ref, rows):
    low = _low_lanes()
    t0, t1 = ref[0, rows, :], ref[1, rows, :]
    return jnp.concatenate([jnp.where(low, t0, 0.0), jnp.where(low, 0.0, t0),
                            jnp.where(low, t1, 0.0), jnp.where(low, 0.0, t1)], axis=0).astype(BF16)


def _unstack_heads(dup):
    low = _low_lanes()
    return (jnp.where(low, dup[0:BLK], dup[BLK:2 * BLK]), jnp.where(low, dup[2 * BLK:3 * BLK], dup[3 * BLK:4 * BLK]))


def _kv_dup(ref, prow, rows, odd):
    t = jnp.concatenate([ref[0, prow, :], ref[0, rows, :]], axis=0)
    swapped = pltpu.roll(t, HD, axis=1)
    keep = jnp.logical_xor(_low_lanes(2 * BLK), odd)
    return jnp.where(keep, t, swapped).astype(BF16)


def _attn_fwd(q, k, v, bias, d):
    def body(q_ref, k_ref, v_ref, b_ref, o_ref, l_ref):
        odd = pl.program_id(0) % 2 == 1
        ones = jnp.ones((2 * BLK, LANES), BF16)

        def block(idx, carry):
            b, start, pstart = _block_start(idx, d)
            rows, prow = _sub_rows(start, d), _sub_rows(pstart, d)
            qs = _stack_heads(q_ref, rows)
            kw = _kv_dup(k_ref, prow, rows, odd)
            vw = _kv_dup(v_ref, prow, rows, odd)
            s = _dot_nt(qs, kw) + b_ref[jnp.minimum(b, 1), 0]
            m = jnp.max(s, axis=1, keepdims=True)
            p = jnp.exp(s - m).astype(BF16)
            ol = _dot(p, jnp.concatenate([vw, ones], axis=1))
            l = ol[:, LANES:]
            o_ref[0, rows, :], o_ref[1, rows, :] = _unstack_heads(ol[:, :LANES] / l)
            lse = m + jnp.log(l)
            for g in range(GQ):
                l_ref[g, rows, :] = lse[g * BLK:(g + 1) * BLK]
            return carry

        lax.fori_loop(0, S // BLK, block, 0, unroll=2)

    q_like = pl.BlockSpec((2, S, LANES), lambda j: (j, 0, 0))
    kv = pl.BlockSpec((1, S, LANES), lambda j: (j // 2, 0, 0))
    heads = pl.BlockSpec((GQ, S, LANES), lambda j: (j, 0, 0))
    bias_spec = pl.BlockSpec((2, 1, GQ * BLK, 2 * BLK), lambda j: (0, j, 0, 0))
    return pl.pallas_call(
        body, grid=(NKV,), name=f"attn_fwd_d{d}",
        in_specs=[q_like, kv, kv, bias_spec],
        out_specs=[q_like, heads],
        out_shape=[_slabs(D // LANES), _slabs(NHEAD)],
        compiler_params=_params(("arbitrary",)),
    )(q, k, v, bias)


def _attn_combine(outs, lses, a_gate):
    tm = 256

    def body(o1, o2, o3, l1, l2, l3, ag_ref, o_ref, lse_ref, y_ref):
        low = _low_lanes(tm)
        for sl in range(D // LANES):
            w = []
            for h in (2 * sl, 2 * sl + 1):
                a, b, c = l1[h], l2[h], l3[h]
                m = jnp.maximum(jnp.maximum(a, b), c)
                ea, eb, ec = jnp.exp(a - m), jnp.exp(b - m), jnp.exp(c - m)
                den = ea + eb + ec
                lse_ref[h] = m + jnp.log(den)
                inv = 1.0 / den
                w.append((ea * inv, eb * inv, ec * inv))
            wa, wb, wc = (jnp.where(low, w[0][i], w[1][i]) for i in range(3))
            o = wa * o1[sl] + wb * o2[sl] + wc * o3[sl]
            o_ref[sl] = o
            cols = slice(sl * LANES, (sl + 1) * LANES)
            ag = ag_ref[:, cols]
            y_ref[:, cols] = (o * (ag * _sigmoid(ag))).astype(BF16)

    wide, per_head = _slab_rows(D // LANES, tm), _slab_rows(NHEAD, tm)
    return pl.pallas_call(
        body, grid=(S // tm,), name="attn_combine",
        in_specs=[wide] * 3 + [per_head] * 3 + [_rows(tm, D)],
        out_specs=[wide, per_head, _rows(tm, D)],
        out_shape=[_slabs(D // LANES), _slabs(NHEAD), jax.ShapeDtypeStruct((S, D), BF16)],
        compiler_params=_params(("arbitrary",)),
    )(*outs, *lses, a_gate)


def _head_sum_selectors():
    lane_in = jnp.arange(LANES)[:, None] // HD
    return jnp.stack([jnp.broadcast_to(lane_in == h, (LANES, LANES)) for h in range(2)]).astype(BF16)


def _attn_gate_bwd(dy_att, o, a_gate, selectors):
    tm = 256

    def body(dy_ref, o_ref, ag_ref, e_ref, do_ref, dag_ref, delta_ref):
        for sl in range(D // LANES):
            cols = slice(sl * LANES, (sl + 1) * LANES)
            dy, ag, o_ = dy_ref[:, cols], ag_ref[:, cols], o_ref[sl]
            sg = _sigmoid(ag)
            do = dy * (ag * sg)
            do_ref[sl] = do
            dag_ref[:, cols] = (dy * o_ * (sg * (1.0 + ag * (1.0 - sg)))).astype(BF16)
            prod = do * o_
            hi = prod.astype(BF16)
            lo = (prod - hi.astype(F32)).astype(BF16)
            for h in range(2):
                delta_ref[2 * sl + h] = _dot(hi, e_ref[h]) + _dot(lo, e_ref[h])

    return pl.pallas_call(
        body, grid=(S // tm,), name="attn_gate_bwd",
        in_specs=[_rows(tm, D), _slab_rows(D // LANES, tm), _rows(tm, D), _resident((2, LANES, LANES))],
        out_specs=[_slab_rows(D // LANES, tm), _rows(tm, D), _slab_rows(NHEAD, tm)],
        out_shape=[_slabs(D // LANES), jax.ShapeDtypeStruct((S, D), BF16), _slabs(NHEAD)],
        compiler_params=_params(("arbitrary",)),
    )(dy_att, o, a_gate, selectors)


def _attn_bwd(q, k, v, do, lse, delta, bias, d):
    def body(q_ref, do_ref, l_ref, dl_ref, k_ref, v_ref, b_ref, dq_ref, dkv_ref):
        odd = pl.program_id(0) % 2 == 1
        chunk = pl.program_id(1)

        @pl.when(chunk == 0)
        def _():
            dkv_ref[...] = jnp.zeros_like(dkv_ref)

        def block(idx, carry):
            b, start, pstart = _block_start(chunk * BLOCKS_PER_CHUNK + idx, d)
            rows, prow = _sub_rows(start, d), _sub_rows(pstart, d)
            mine = _sub_rows(start - chunk * CHUNK_ROWS, d)
            qs = _stack_heads(q_ref, mine)
            dos = _stack_heads(do_ref, mine)
            lse_t = jnp.concatenate([l_ref[g, mine, :] for g in range(GQ)], axis=0)
            delta_t = jnp.concatenate([dl_ref[g, mine, :] for g in range(GQ)], axis=0)
            kw = _kv_dup(k_ref, prow, rows, odd)
            vw = _kv_dup(v_ref, prow, rows, odd)
            s = _dot_nt(qs, kw) + b_ref[jnp.minimum(b, 1), 0]
            p = jnp.exp(s - jnp.concatenate([lse_t, lse_t], axis=1))
            dv2 = _dot_tn(p.astype(BF16), dos)
            dp = _dot_nt(dos, vw)
            ds = (p * (dp - jnp.concatenate([delta_t, delta_t], axis=1))).astype(BF16)
            dq_ref[0, mine, :], dq_ref[1, mine, :] = _unstack_heads(_dot(ds, kw))
            dk2 = _dot_tn(ds, qs)
            dkv = jnp.where(_low_lanes(2 * BLK), dk2 + pltpu.roll(dk2, HD, axis=1), dv2 + pltpu.roll(dv2, HD, axis=1))
            dkv_ref[0, rows, :] = dkv_ref[0, rows, :] + dkv[BLK:]
            dkv_ref[0, prow, :] = dkv_ref[0, prow, :] + dkv[:BLK]
            return carry

        lax.fori_loop(0, BLOCKS_PER_CHUNK, block, 0, unroll=8)

    q_like = pl.BlockSpec((2, CHUNK_ROWS, LANES), lambda j, c: (j, c, 0))
    heads = pl.BlockSpec((GQ, CHUNK_ROWS, LANES), lambda j, c: (j, c, 0))
    kv = pl.BlockSpec((1, S, LANES), lambda j, c: (j // 2, 0, 0))
    per_kv = pl.BlockSpec((1, S, LANES), lambda j, c: (j, 0, 0))
    bias_spec = pl.BlockSpec((2, 1, GQ * BLK, 2 * BLK), lambda j, c: (0, j, 0, 0))
    return pl.pallas_call(
        body, grid=(NKV, S // CHUNK_ROWS), name=f"attn_bwd_d{d}",
        in_specs=[q_like, q_like, heads, heads, kv, kv, bias_spec],
        out_specs=[q_like, per_kv],
        out_shape=[_slabs(D // LANES), _slabs(NKV)],
        compiler_params=_params(("arbitrary", "arbitrary")),
    )(q, do, lse, delta, k, v, bias)


CONV_T = 128


def _halo_before(i):
    return (jnp.maximum(i * (CONV_T // HALO) - 1, 0), 0)


def _halo_after(i):
    return (jnp.minimum((i + 1) * (CONV_T // HALO), S // HALO - 1), 0)


SUBLANES = 8
NCH = D // LANES
GROUP = SUBLANES * SUBLANES


def _comb(ref, cb, base):
    return ref[cb, pl.ds(base, SUBLANES, stride=SUBLANES), :]


def _taps(w_ref, cols):
    return [jnp.broadcast_to(w_ref[j:j + 1, cols], (SUBLANES, LANES)) for j in range(CONV_K)]


def _conv_fwd(c_val, c_glu, c_gate, conv_w, conv_b, ln_g, ln_b):
    T = CONV_T

    def body(cv_ref, cg_ref, cvh_ref, cgh_ref, gate_ref, w_ref, b_ref, lg_ref, lb_ref, u_ref, y_ref, win, us):
        i = pl.program_id(0)
        for cb in range(NCH):
            cols = slice(cb * LANES, (cb + 1) * LANES)
            win[cb, HALO:HALO + T, :] = cv_ref[:, cols] * _sigmoid(cg_ref[:, cols])
            win[cb, 0:HALO, :] = jnp.where(i > 0, cvh_ref[:, cols] * _sigmoid(cgh_ref[:, cols]), 0.0)
        for cb in range(NCH):
            cols = slice(cb * LANES, (cb + 1) * LANES)
            taps = _taps(w_ref, cols)
            bias = jnp.broadcast_to(b_ref[:, cols], (SUBLANES, LANES))

            def group(g, carry):
                for b in range(SUBLANES):
                    base = g * GROUP + b
                    acc = bias
                    for j in range(CONV_K):
                        acc = acc + taps[j] * _comb(win, cb, base + (HALO - (CONV_K - 1) + j))
                    us[cb, pl.ds(base, SUBLANES, stride=SUBLANES), :] = acc
                return carry

            lax.fori_loop(0, T // GROUP, group, 0)
        total = us[0]
        for cb in range(1, NCH):
            total = total + us[cb]
        mu = jnp.sum(total, axis=-1, keepdims=True) * (1.0 / D)
        sq = jnp.zeros((T, LANES), F32)
        for cb in range(NCH):
            uc = us[cb] - mu
            sq = sq + uc * uc
        rstd = lax.rsqrt(jnp.sum(sq, axis=-1, keepdims=True) * (1.0 / D) + LN_EPS)
        for cb in range(NCH):
            cols = slice(cb * LANES, (cb + 1) * LANES)
            u = us[cb]
            u_ref[:, cols] = u
            nrm = (u - mu) * rstd * lg_ref[:, cols] + lb_ref[:, cols]
            gate = gate_ref[:, cols]
            y_ref[:, cols] = (nrm * _sigmoid(nrm) * (gate * _sigmoid(gate))).astype(BF16)

    halo = pl.BlockSpec((HALO, D), _halo_before)
    return pl.pallas_call(
        body, grid=(S // T,), name="conv_fwd",
        in_specs=[_rows(T, D), _rows(T, D), halo, halo, _rows(T, D),
                  _resident((HALO, D)), _resident((1, D)), _resident((1, D)), _resident((1, D))],
        out_specs=[_rows(T, D), _rows(T, D)],
        out_shape=[jax.ShapeDtypeStruct((S, D), F32), jax.ShapeDtypeStruct((S, D), BF16)],
        scratch_shapes=[pltpu.VMEM((NCH, T + HALO, LANES), F32), pltpu.VMEM((NCH, T, LANES), F32)],
        compiler_params=_params(("arbitrary",)),
    )(c_val, c_glu, c_val, c_glu, c_gate, conv_w, conv_b, ln_g, ln_b)


def _conv_bwd_rows(u, c_gate, dy_conv, ln_g, ln_b):
    tm = 256

    def body(u_ref, gate_ref, dy_ref, lg_ref, lb_ref, du_ref, dgate_ref, st_ref):
        @pl.when(pl.program_id(0) == 0)
        def _():
            st_ref[...] = jnp.zeros_like(st_ref)

        u, gate, dy = u_ref[...], gate_ref[...], dy_ref[...]
        mu = jnp.mean(u, axis=-1, keepdims=True)
        uc = u - mu
        rstd = lax.rsqrt(jnp.mean(uc * uc, axis=-1, keepdims=True) + LN_EPS)
        z = uc * rstd
        nrm = z * lg_ref[...] + lb_ref[...]
        sn, sg = _sigmoid(nrm), _sigmoid(gate)
        dgate_ref[...] = (dy * (nrm * sn) * (sg * (1.0 + gate * (1.0 - sg)))).astype(BF16)
        dn = dy * (gate * sg) * (sn * (1.0 + nrm * (1.0 - sn)))
        dz = dn * lg_ref[...]
        du = rstd * (dz - jnp.mean(dz, axis=-1, keepdims=True) - z * jnp.mean(dz * z, axis=-1, keepdims=True))
        du_ref[...] = du
        st_ref[0:1, :] += jnp.sum(dn * z, axis=0, keepdims=True)
        st_ref[1:2, :] += jnp.sum(dn, axis=0, keepdims=True)
        st_ref[2:3, :] += jnp.sum(du, axis=0, keepdims=True)

    big = jax.ShapeDtypeStruct((S, D), F32)
    return pl.pallas_call(
        body, grid=(S // tm,), name="conv_bwd_rows",
        in_specs=[_rows(tm, D)] * 3 + [_resident((1, D)), _resident((1, D))],
        out_specs=[_rows(tm, D), _rows(tm, D), pl.BlockSpec((8, D), lambda i: (0, 0))],
        out_shape=[big, jax.ShapeDtypeStruct((S, D), BF16), jax.ShapeDtypeStruct((8, D), F32)],
        compiler_params=_params(("arbitrary",)),
    )(u, c_gate, dy_conv, ln_g, ln_b)


def _conv_bwd_taps(du, c_val, c_glu, conv_w):
    T = CONV_T
    last = S // T - 1

    def body(du_ref, dua_ref, cv_ref, cg_ref, cvh_ref, cgh_ref, w_ref, dcv_ref, dcg_ref, dw_ref,
             hwin, dwin, dhs, dw_acc):
        i = pl.program_id(0)

        @pl.when(i == 0)
        def _():
            dw_acc[...] = jnp.zeros_like(dw_acc)

        for cb in range(NCH):
            cols = slice(cb * LANES, (cb + 1) * LANES)
            hwin[cb, HALO:HALO + T, :] = cv_ref[:, cols] * _sigmoid(cg_ref[:, cols])
            hwin[cb, 0:HALO, :] = jnp.where(i > 0, cvh_ref[:, cols] * _sigmoid(cgh_ref[:, cols]), 0.0)
            dwin[cb, 0:T, :] = du_ref[:, cols]
            dwin[cb, T:T + HALO, :] = jnp.where(i < last, dua_ref[:, cols], 0.0)
        for cb in range(NCH):
            cols = slice(cb * LANES, (cb + 1) * LANES)
            taps = _taps(w_ref, cols)

            def group_dh(g, carry):
                for b in range(SUBLANES):
                    base = g * GROUP + b
                    acc = jnp.zeros((SUBLANES, LANES), F32)
                    for j in range(CONV_K):
                        acc = acc + taps[j] * _comb(dwin, cb, base + (CONV_K - 1 - j))
                    dhs[cb, pl.ds(base, SUBLANES, stride=SUBLANES), :] = acc
                return carry

            lax.fori_loop(0, T // GROUP, group_dh, 0)

            def group_dw(g, sums):
                for b in range(SUBLANES):
                    base = g * GROUP + b
                    d = _comb(dwin, cb, base)
                    sums = tuple(sums[j] + d * _comb(hwin, cb, base + (HALO - (CONV_K - 1) + j))
                                 for j in range(CONV_K))
                return sums

            sums = lax.fori_loop(0, T // GROUP, group_dw, tuple(dw_acc[j, :, cols] for j in range(CONV_K)))
            for j in range(CONV_K):
                dw_acc[j, :, cols] = sums[j]
            dh = dhs[cb]
            cv, sg = cv_ref[:, cols], _sigmoid(cg_ref[:, cols])
            dcv_ref[:, cols] = (dh * sg).astype(BF16)
            dcg_ref[:, cols] = (dh * cv * (sg * (1.0 - sg))).astype(BF16)

        @pl.when(i == last)
        def _():
            dw_ref[...] = jnp.zeros_like(dw_ref)
            for j in range(CONV_K):
                dw_ref[j:j + 1, :] = jnp.sum(dw_acc[j], axis=0, keepdims=True)

    before = pl.BlockSpec((HALO, D), _halo_before)
    after = pl.BlockSpec((HALO, D), _halo_after)
    big = jax.ShapeDtypeStruct((S, D), BF16)
    return pl.pallas_call(
        body, grid=(S // T,), name="conv_bwd_taps",
        in_specs=[_rows(T, D), after, _rows(T, D), _rows(T, D), before, before, _resident((HALO, D))],
        out_specs=[_rows(T, D), _rows(T, D), pl.BlockSpec((HALO, D), lambda i: (0, 0))],
        out_shape=[big, big, jax.ShapeDtypeStruct((HALO, D), F32)],
        scratch_shapes=[pltpu.VMEM((NCH, T + HALO, LANES), F32), pltpu.VMEM((NCH, T + HALO, LANES), F32),
                        pltpu.VMEM((NCH, T, LANES), F32), pltpu.VMEM((CONV_K, SUBLANES, D), F32)],
        compiler_params=_params(("arbitrary",)),
    )(du, du, c_val, c_glu, c_val, c_glu, conv_w)


def _outproj_loss(y_att, y_conv, w_out_bf, x, target, gf):
    tm = 256

    def body(ya_ref, yc_ref, w_ref, x_ref, t_ref, gf_ref, dx2_ref, dya_ref, dyc_ref, dw_ref, st_ref, acc):
        @pl.when(pl.program_id(0) == 0)
        def _():
            acc[...] = jnp.zeros_like(acc)
            st_ref[...] = jnp.zeros_like(st_ref)

        ya, yc = ya_ref[...], yc_ref[...]
        x2 = x_ref[...] + _dot(ya, w_ref[0:D, :]) + _dot(yc, w_ref[D:2 * D, :])
        r = lax.rsqrt(jnp.mean(x2 * x2, axis=-1, keepdims=True) + NORM_EPS)
        xn = x2 * r
        err = xn * gf_ref[...] - t_ref[...]
        dout = err * (1.0 / D)
        dxn = dout * gf_ref[...]
        dx2 = r * (dxn - xn * jnp.mean(dxn * xn, axis=-1, keepdims=True))
        dx2_ref[...] = dx2
        dx2b = dx2.astype(BF16)
        dya_ref[...] = _dot_nt(dx2b, w_ref[0:D, :])
        dyc_ref[...] = _dot_nt(dx2b, w_ref[D:2 * D, :])
        acc[0:D, :] += _dot_tn(ya, dx2b)
        acc[D:2 * D, :] += _dot_tn(yc, dx2b)
        st_ref[0:1, :] += jnp.sum(dout * xn, axis=0, keepdims=True)
        st_ref[1:2, :] += jnp.sum(err * err, axis=0, keepdims=True) * (0.5 / D)

        @pl.when(pl.program_id(0) == S // tm - 1)
        def _():
            dw_ref[...] = acc[...].astype(BF16)

    big = jax.ShapeDtypeStruct((S, D), F32)
    return pl.pallas_call(
        body, grid=(S // tm,), name="outproj_loss",
        in_specs=[_rows(tm, D), _rows(tm, D), _resident((WOUT_ROWS, D)), _rows(tm, D), _rows(tm, D), _resident((1, D))],
        out_specs=[_rows(tm, D), _rows(tm, D), _rows(tm, D),
                   pl.BlockSpec((WOUT_ROWS, D), lambda i: (0, 0)), pl.BlockSpec((8, D), lambda i: (0, 0))],
        out_shape=[big, big, big, jax.ShapeDtypeStruct((WOUT_ROWS, D), BF16), jax.ShapeDtypeStruct((8, D), F32)],
        scratch_shapes=[pltpu.VMEM((WOUT_ROWS, D), F32)],
        compiler_params=_params(("arbitrary",)),
    )(y_att, y_conv, w_out_bf, x, target, gf)


UNITS_PER_CHUNK = CHUNK // LANES


def _dproj_unit(u, dqs, dkvs, gates, rows):
    if u < OFF_K // LANES:
        return ((dqs[0][u] + dqs[1][u] + dqs[2][u]) * (HD ** -0.5)).astype(BF16)
    if u < OFF_AG // LANES:
        w = u - OFF_K // LANES
        ta, tb = (dkvs[0][j] + dkvs[1][j] + dkvs[2][j] for j in (2 * (w % 2), 2 * (w % 2) + 1))
        low = _low_lanes(rows)
        if w < 2:
            return jnp.where(low, ta, pltpu.roll(tb, HD, axis=1)).astype(BF16)
        return jnp.where(low, pltpu.roll(ta, HD, axis=1), tb).astype(BF16)
    g, sl = divmod(u - OFF_AG // LANES, D // LANES)
    return gates[g][:, sl * LANES:(sl + 1) * LANES]


def _dproj_sources(units, dqs, dkvs, gates, rows):
    use_q = any(u < OFF_K // LANES for u in units)
    use_kv = any(OFF_K // LANES <= u < OFF_AG // LANES for u in units)
    use_g = sorted({(u - OFF_AG // LANES) // (D // LANES) for u in units if u >= OFF_AG // LANES})
    args = (list(dqs) if use_q else []) + (list(dkvs) if use_kv else []) + [gates[g] for g in use_g]
    specs = ([_slab_rows(D // LANES, rows)] * 3 if use_q else []) + ([_slab_rows(NKV, rows)] * 3 if use_kv else []) \
        + [_rows(rows, D)] * len(use_g)

    def pick(refs):
        refs = list(refs)
        q_refs = [refs.pop(0) for _ in range(3)] if use_q else None
        kv_refs = [refs.pop(0) for _ in range(3)] if use_kv else None
        return q_refs, kv_refs, {g: refs.pop(0) for g in use_g}

    return args, specs, pick


def _inproj_bwd_x(dqs, dkvs, gates, w_bf, x, g1, dx2, pi, po, ps):
    tm = 256
    last = S // tm - 1
    units = range(NCOL // LANES)
    pieces, piece_specs, pick = _dproj_sources(units, dqs, dkvs, gates, tm)

    def body(*refs):
        piece_refs, refs = refs[:len(pieces)], refs[len(pieces):]
        (w_ref, x_ref, g_ref, dx2_ref, pi_ref, po_ref, ps_ref,
         gx_ref, st_ref, ri_ref, ro_ref, rs_ref, dp_ref, send, recv) = refs
        i = pl.program_id(0)
        copies = _chip_exchange_copies((pi_ref, po_ref, ps_ref), (ri_ref, ro_ref, rs_ref), send, recv)

        @pl.when(i == 0)
        def _():
            st_ref[...] = jnp.zeros_like(st_ref)
            for out, _ in copies:
                out.start()

        sources = pick(piece_refs)
        for u in units:
            dp_ref[:, u * LANES:(u + 1) * LANES] = _dproj_unit(u, *sources, tm)
        dh = _dot_nt(dp_ref[...], w_ref[...])
        xt = x_ref[...]
        r = lax.rsqrt(jnp.mean(xt * xt, axis=-1, keepdims=True) + NORM_EPS)
        xn = xt * r
        dxn = dh * g_ref[...]
        gx_ref[...] = dx2_ref[...] + r * (dxn - xn * jnp.mean(dxn * xn, axis=-1, keepdims=True))
        st_ref[0:1, :] += jnp.sum(dh * xn, axis=0, keepdims=True)

        @pl.when(i == last)
        def _():
            for _, arrival in copies:
                arrival.wait_recv()
            for out, _ in copies:
                out.wait_send()

    n = 3 * len(CHIP_FLIPS)
    return pl.pallas_call(
        body, grid=(S // tm,), name="inproj_bwd_x",
        in_specs=piece_specs + [_resident((D, NCOL)), _rows(tm, D), _resident((1, D)), _rows(tm, D), ANY, ANY, ANY],
        out_specs=[_rows(tm, D), pl.BlockSpec((8, D), lambda i: (0, 0)), ANY, ANY, ANY],
        out_shape=[jax.ShapeDtypeStruct((S, D), F32), jax.ShapeDtypeStruct((8, D), F32),
                   jax.ShapeDtypeStruct((NCHIP, D // 2, CHUNK), BF16),
                   jax.ShapeDtypeStruct((NCHIP, WOUT_SHARD // 2, D), BF16),
                   jax.ShapeDtypeStruct((NCHIP, SMALL_ROWS, D), F32)],
        scratch_shapes=[pltpu.VMEM((tm, NCOL), BF16), pltpu.SemaphoreType.DMA((n,)), pltpu.SemaphoreType.DMA((n,))],
        compiler_params=_params(("arbitrary",)),
    )(*pieces, w_bf, x, g1, dx2, pi, po, ps)


def _inproj_bwd_w(h, dqs, dkvs, gates):
    tk = 512
    nk = S // tk
    out = None
    for k in range(NCHIP):
        units = range(k * UNITS_PER_CHUNK, (k + 1) * UNITS_PER_CHUNK)
        pieces, piece_specs, pick = _dproj_sources(units, dqs, dkvs, gates, tk)
        handed_on = [] if out is None else [out]

        def body(*refs, units=units, pick=pick, n_pieces=len(pieces), n_in=1 + len(pieces) + len(handed_on)):
            h_ref, piece_refs = refs[0], refs[1:1 + n_pieces]
            o_ref, tile, acc = refs[n_in:]
            i = pl.program_id(0)

            @pl.when(i == 0)
            def _():
                acc[...] = jnp.zeros_like(acc)

            sources = pick(piece_refs)
            for n, u in enumerate(units):
                tile[:, n * LANES:(n + 1) * LANES] = _dproj_unit(u, *sources, tk)
            acc[...] += _dot_tn(h_ref[...], tile[...])

            @pl.when(i == nk - 1)
            def _():
                o_ref[0] = acc[...].astype(BF16)

        out = pl.pallas_call(
            body, grid=(nk,), name=f"inproj_bwd_w{k}",
            in_specs=[_rows(tk, D)] + piece_specs + [ANY] * len(handed_on),
            out_specs=pl.BlockSpec((1, D, CHUNK), lambda i, k=k: (k, 0, 0)),
            out_shape=jax.ShapeDtypeStruct((NCHIP, D, CHUNK), BF16),
            input_output_aliases={1 + len(pieces): 0} if handed_on else {},
            scratch_shapes=[pltpu.VMEM((tk, CHUNK), BF16), pltpu.VMEM((D, CHUNK), F32)],
            compiler_params=_params(("arbitrary",)),
        )(h, *pieces, *handed_on)
    return out


def _local_step(x, target, g1, w_in_bf, conv_w, conv_b, ln_g, ln_b, w_out_bf, gf):
    h, q, k, v, a_gate, c_val, c_glu, c_gate = _inproj_fwd(x, g1, w_in_bf)
    tables = [_bias_table(d) for d in PATTERNS]
    outs, lses = zip(*[_attn_fwd(q, k, v, t, d) for t, d in zip(tables, PATTERNS)])
    o, lse, y_att = _attn_combine(outs, lses, a_gate)
    u, y_conv = _conv_fwd(c_val, c_glu, c_gate, conv_w, conv_b, ln_g, ln_b)
    dx2, dy_att, dy_conv, dw_out, st_out = _outproj_loss(y_att, y_conv, w_out_bf, x, target, gf)

    do, da_gate, delta = _attn_gate_bwd(dy_att, o, a_gate, _head_sum_selectors())
    dqs, dkvs = zip(*[_attn_bwd(q, k, v, do, lse, delta, t, d) for t, d in zip(tables, PATTERNS)])

    du, dc_gate, st_conv = _conv_bwd_rows(u, c_gate, dy_conv, ln_g, ln_b)
    dc_val, dc_glu, dconv_w = _conv_bwd_taps(du, c_val, c_glu, conv_w)

    dproj_pieces = (dqs, dkvs, (da_gate, dc_val, dc_glu, dc_gate))
    dw_in = _inproj_bwd_w(h, *dproj_pieces)
    small = jnp.concatenate([st_conv, st_out, dconv_w], axis=0)
    return dw_in, dw_out, small, dproj_pieces, dx2


ROW_LN_G, ROW_LN_B, ROW_CONV_B, ROW_FINAL_G, ROW_LOSS, ROW_TAPS = 0, 1, 2, 8, 9, 16
SMALL_ROWS = 16 + HALO
NDEV = 8


MESH = pl.DeviceIdType.MESH
ANY = pl.BlockSpec(memory_space=pl.ANY)
CHIP_FLIPS = ((1, 0), (0, 1), (1, 1))


def _pos():
    return lax.axis_index("x"), lax.axis_index("y"), lax.axis_index("c")


def _flip(v, f):
    return 1 - v if f else v


def _ds(start, size, align=None):
    return pl.ds(pl.multiple_of(start, align or size), size)


def _place_shards(wi, wo, cw, where):
    steps = 4

    def body(where_ref, wi_ref, wo_ref, cw_ref, wi_full, wo_full, cw_full):
        wi_full[...] = wi_ref[...].astype(BF16)
        wo_full[...] = wo_ref[...].astype(BF16)
        cw_full[...] = cw_ref[...]

    grid_spec = pltpu.PrefetchScalarGridSpec(
        num_scalar_prefetch=1, grid=(steps,),
        in_specs=[pl.BlockSpec((D // steps, CHUNK), lambda i, w: (i, 0)),
                  pl.BlockSpec((WOUT_SHARD // steps, D), lambda i, w: (i, 0)),
                  pl.BlockSpec((HALO, CONVW_SHARD), lambda i, w: (0, 0))],
        out_specs=[pl.BlockSpec((D // steps, CHUNK), lambda i, w: (i, w[0])),
                   pl.BlockSpec((WOUT_SHARD // steps, D), lambda i, w: (w[0] * steps + i, 0)),
                   pl.BlockSpec((HALO, CONVW_SHARD), lambda i, w: (0, w[0]))])
    return pl.pallas_call(
        body, grid_spec=grid_spec, name="place_shards",
        out_shape=[jax.ShapeDtypeStruct((D, NCOL), BF16), jax.ShapeDtypeStruct((WOUT_ROWS, D), BF16),
                   jax.ShapeDtypeStruct((HALO, D), F32)],
        compiler_params=_params(("arbitrary",)),
    )(where, wi, wo, cw)


def _gather_weights(wi_full, wo_full, cw_full):
    halves = (D // 2, WOUT_SHARD // 2, HALO // 2)
    OWN_X, OWN_Y, VIA_Y, VIA_X = range(4)

    def body(_wi, _wo, _cw, wi_full, wo_full, cw_full, send, recv):
        x, y, c = _pos()
        x_nbr, y_nbr, diag = (1 - x, y), (x, 1 - y), (1 - x, 1 - y)

        def region(a, chip_xy, half, part=None):
            chip = 2 * chip_xy[0] + chip_xy[1]
            n, row = halves[a], half * halves[a]
            if part is not None:
                n = n // 2
                row = row + part * n
            if a == 0:
                return wi_full.at[_ds(row, n), _ds(chip * CHUNK, CHUNK, 128)]
            if a == 1:
                return wo_full.at[_ds(chip * WOUT_SHARD + row, n), :]
            return cw_full.at[_ds(row, n), _ds(chip * CONVW_SHARD, CONVW_SHARD, 128)]

        def copy(a, kind, piece, dev):
            k = 8 * a + kind
            return pltpu.make_async_remote_copy(src_ref=piece, dst_ref=piece, send_sem=send.at[k], recv_sem=recv.at[k],
                                                device_id=dev, device_id_type=MESH)

        def to_sibling(a, kind, piece):
            cp = copy(a, 4 + kind, piece, (x, y, 1 - c))
            cp.start()
            return cp

        sends = []
        for a in range(3):
            for kind, nbr in ((OWN_X, x_nbr), (OWN_Y, y_nbr)):
                cp = copy(a, kind, region(a, (x, y), c), (*nbr, c))
                cp.start()
                sends.append(cp)
        for a in range(3):
            got = region(a, x_nbr, c)
            copy(a, OWN_X, got, (*x_nbr, c)).wait_recv()
            onward = copy(a, VIA_Y, region(a, x_nbr, c, 0), (*y_nbr, c))
            onward.start()
            sends += [onward, to_sibling(a, OWN_X, got)]
            got = region(a, y_nbr, c)
            copy(a, OWN_Y, got, (*y_nbr, c)).wait_recv()
            onward = copy(a, VIA_X, region(a, y_nbr, c, 1), (*x_nbr, c))
            onward.start()
            sends += [onward, to_sibling(a, OWN_Y, got)]
        for a in range(3):
            got = region(a, diag, c, 0)
            copy(a, VIA_Y, got, (*y_nbr, c)).wait_recv()
            sends.append(to_sibling(a, VIA_Y, got))
            got = region(a, diag, c, 1)
            copy(a, VIA_X, got, (*x_nbr, c)).wait_recv()
            sends.append(to_sibling(a, VIA_X, got))
        for a in range(3):
            for kind, piece in ((OWN_X, region(a, x_nbr, 1 - c)), (OWN_Y, region(a, y_nbr, 1 - c)),
                                (VIA_Y, region(a, diag, 1 - c, 0)), (VIA_X, region(a, diag, 1 - c, 1))):
                copy(a, 4 + kind, piece, (x, y, 1 - c)).wait_recv()
        for cp in sends:
            cp.wait_send()

    n_sems = 3 * 8
    return pl.pallas_call(
        body, name="gather_weights",
        in_specs=[ANY, ANY, ANY], out_specs=[ANY, ANY, ANY], input_output_aliases={0: 0, 1: 1, 2: 2},
        out_shape=[jax.ShapeDtypeStruct((D, NCOL), BF16), jax.ShapeDtypeStruct((WOUT_ROWS, D), BF16),
                   jax.ShapeDtypeStruct((HALO, D), F32)],
        scratch_shapes=[pltpu.SemaphoreType.DMA((n_sems,)), pltpu.SemaphoreType.DMA((n_sems,))],
    )(wi_full, wo_full, cw_full)


def _exchange_halves(gi4, go4, small):
    def body(gi_ref, go_ref, sm_ref, ri_ref, ro_ref, rs_ref, send, recv):
        x, y, c = _pos()
        sib = (x, y, 1 - c)
        copies = [
            (gi_ref.at[:, _ds((1 - c) * (D // 2), D // 2), :], ri_ref),
            (go_ref.at[:, _ds((1 - c) * (WOUT_SHARD // 2), WOUT_SHARD // 2), :], ro_ref),
            (sm_ref, rs_ref),
        ]
        cps = [pltpu.make_async_remote_copy(src_ref=s_, dst_ref=d_, send_sem=send.at[k], recv_sem=recv.at[k],
                                            device_id=sib, device_id_type=MESH) for k, (s_, d_) in enumerate(copies)]
        for cp in cps:
            cp.start()
        for cp in cps:
            cp.wait()

    return pl.pallas_call(
        body, name="exchange_halves",
        in_specs=[ANY, ANY, ANY], out_specs=[ANY, ANY, ANY],
        out_shape=[jax.ShapeDtypeStruct((NCHIP, D // 2, CHUNK), BF16),
                   jax.ShapeDtypeStruct((NCHIP, WOUT_SHARD // 2, D), BF16),
                   jax.ShapeDtypeStruct((SMALL_ROWS, D), F32)],
        scratch_shapes=[pltpu.SemaphoreType.DMA((3,)), pltpu.SemaphoreType.DMA((3,))],
    )(gi4, go4, small)


def _add_halves(gi4, ri, go4, ro, small, rs):
    hi, ho = D // 2, WOUT_SHARD // 2

    def body(gi_ref, ri_ref, go_ref, ro_ref, sm_ref, rs_ref, pi_ref, po_ref, ps_ref):
        c = lax.axis_index("c")
        pi_ref[0] = (gi_ref[0, _ds(c * hi, hi), :].astype(F32) + ri_ref[0].astype(F32)).astype(BF16)
        po_ref[0] = (go_ref[0, _ds(c * ho, ho), :].astype(F32) + ro_ref[0].astype(F32)).astype(BF16)
        ps_ref[...] = sm_ref[...] + rs_ref[...]

    blk = lambda n, w: pl.BlockSpec((1, n, w), lambda k: (k, 0, 0))
    whole = pl.BlockSpec((SMALL_ROWS, D), lambda k: (0, 0))
    return pl.pallas_call(
        body, grid=(NCHIP,), name="add_halves",
        in_specs=[blk(D, CHUNK), blk(hi, CHUNK), blk(WOUT_SHARD, D), blk(ho, D), whole, whole],
        out_specs=[blk(hi, CHUNK), blk(ho, D), whole],
        out_shape=[jax.ShapeDtypeStruct((NCHIP, hi, CHUNK), BF16), jax.ShapeDtypeStruct((NCHIP, ho, D), BF16),
                   jax.ShapeDtypeStruct((SMALL_ROWS, D), F32)],
        compiler_params=_params(("arbitrary",)),
    )(gi4, ri, go4, ro, small, rs)


def _chip_exchange_copies(srcs, dsts, send, recv):
    x, y, c = _pos()
    me = 2 * x + y
    pairs = []
    for a in range(3):
        for j, (fx, fy) in enumerate(CHIP_FLIPS):
            px, py = _flip(x, fx), _flip(y, fy)
            peer = 2 * px + py
            k = 3 * a + j
            out = pltpu.make_async_remote_copy(
                src_ref=srcs[a] if a == 2 else srcs[a].at[peer], dst_ref=dsts[a].at[me],
                send_sem=send.at[k], recv_sem=recv.at[k], device_id=(px, py, c), device_id_type=MESH)
            got = dsts[a].at[peer]
            arrival = pltpu.make_async_remote_copy(
                src_ref=got, dst_ref=got, send_sem=send.at[k], recv_sem=recv.at[k],
                device_id=(px, py, c), device_id_type=MESH)
            pairs.append((out, arrival))
    return pairs


def _sum_chips(ri, ro, rs, pi, po, ps, where):
    def body(w_ref, ri_ref, ro_ref, rs_ref, pi_ref, po_ref, ps_ref, gi_ref, go_ref, gs_ref, g5_ref, loss_ref,
             acc_i, acc_o, acc_s):
        k = pl.program_id(0)
        accs = (acc_i, acc_o, acc_s)

        @pl.when(k == 0)
        def _():
            for acc in accs:
                acc[...] = jnp.zeros_like(acc)

        @pl.when(k == w_ref[0])
        def _():
            for acc, val in zip(accs, (pi_ref[0], po_ref[0], ps_ref[...])):
                acc[...] += val.astype(F32)

        @pl.when(k != w_ref[0])
        def _():
            for acc, ref in zip(accs, (ri_ref, ro_ref, rs_ref)):
                acc[...] += ref[0].astype(F32)

        @pl.when(k == NCHIP - 1)
        def _():
            gi_ref[0] = acc_i[...]
            go_ref[0] = acc_o[...]
            gs_ref[...] = acc_s[...]
            g5_ref[...] = jnp.zeros_like(g5_ref)
            for i, row in enumerate((ROW_CONV_B, ROW_LN_G, ROW_LN_B, ROW_FINAL_G)):
                g5_ref[i + 1:i + 2, :] = acc_s[row:row + 1, :]
            loss = jnp.sum(acc_s[ROW_LOSS:ROW_LOSS + 1, :], axis=1, keepdims=True)
            loss_ref[...] = jnp.broadcast_to(loss, loss_ref.shape)

    def sent(k, w):
        return jnp.where(k == w[0], (k + 1) % NCHIP, k)

    hi, ho = D // 2, WOUT_SHARD // 2
    const = lambda shape: pl.BlockSpec(shape, lambda k, w: (0,) * len(shape))
    grid_spec = pltpu.PrefetchScalarGridSpec(
        num_scalar_prefetch=1, grid=(NCHIP,),
        in_specs=[pl.BlockSpec((1, hi, CHUNK), lambda k, w: (sent(k, w), 0, 0)),
                  pl.BlockSpec((1, ho, D), lambda k, w: (sent(k, w), 0, 0)),
                  pl.BlockSpec((1, SMALL_ROWS, D), lambda k, w: (sent(k, w), 0, 0)),
                  pl.BlockSpec((1, hi, CHUNK), lambda k, w: (w[0], 0, 0)),
                  pl.BlockSpec((1, ho, D), lambda k, w: (w[0], 0, 0)),
                  const((SMALL_ROWS, D))],
        out_specs=[pl.BlockSpec((1, hi, CHUNK), lambda k, w: (w[1], 0, 0)),
                   pl.BlockSpec((1, ho, D), lambda k, w: (w[1], 0, 0)),
                   const((SMALL_ROWS, D)), const((8, D)), const((8, LANES))],
        scratch_shapes=[pltpu.VMEM((hi, CHUNK), F32), pltpu.VMEM((ho, D), F32), pltpu.VMEM((SMALL_ROWS, D), F32)])
    return pl.pallas_call(
        body, grid_spec=grid_spec, name="sum_chips",
        out_shape=[jax.ShapeDtypeStruct((2, hi, CHUNK), F32), jax.ShapeDtypeStruct((2, ho, D), F32),
                   jax.ShapeDtypeStruct((SMALL_ROWS, D), F32), jax.ShapeDtypeStruct((8, D), F32),
                   jax.ShapeDtypeStruct((8, LANES), F32)],
        compiler_params=_params(("arbitrary",)),
    )(where, ri, ro, rs, pi, po, ps)


def _exchange_results(gi2, go2, st):
    flips = [(fx, fy, fc) for fx in (0, 1) for fy in (0, 1) for fc in (0, 1)][1:]

    def body(_gi, _go, st_ref, gi_ref, go_ref, all_ref, send, recv, lsem):
        x, y, c = _pos()
        sib = (x, y, 1 - c)

        def half(k, ref, slot):
            return pltpu.make_async_remote_copy(src_ref=ref.at[slot], dst_ref=ref.at[slot], send_sem=send.at[k],
                                                recv_sem=recv.at[k], device_id=sib, device_id_type=MESH)

        def stat(k, src, slot, dev):
            return pltpu.make_async_remote_copy(src_ref=src, dst_ref=all_ref.at[slot], send_sem=send.at[k],
                                                recv_sem=recv.at[k], device_id=dev, device_id_type=MESH)

        mine = pltpu.make_async_copy(st_ref, all_ref.at[4 * x + 2 * y + c], lsem)
        mine.start()
        sends = [half(k, ref, c) for k, ref in enumerate((gi_ref, go_ref))]
        peers = [(_flip(x, fx), _flip(y, fy), _flip(c, fc)) for fx, fy, fc in flips]
        sends += [stat(2 + k, st_ref, 4 * x + 2 * y + c, dev) for k, dev in enumerate(peers)]
        for cp in sends:
            cp.start()
        for k, ref in enumerate((gi_ref, go_ref)):
            half(k, ref, 1 - c).wait_recv()
        for k, (px, py, pc) in enumerate(peers):
            slot = 4 * px + 2 * py + pc
            stat(2 + k, all_ref.at[slot], slot, (px, py, pc)).wait_recv()
        for cp in sends:
            cp.wait_send()
        mine.wait()

    n = 2 + len(flips)
    return pl.pallas_call(
        body, name="exchange_results",
        in_specs=[ANY, ANY, ANY], out_specs=[ANY, ANY, ANY], input_output_aliases={0: 0, 1: 1},
        out_shape=[jax.ShapeDtypeStruct((2, D // 2, CHUNK), F32), jax.ShapeDtypeStruct((2, WOUT_SHARD // 2, D), F32),
                   jax.ShapeDtypeStruct((NDEV, 8, D), F32)],
        scratch_shapes=[pltpu.SemaphoreType.DMA((n,)), pltpu.SemaphoreType.DMA((n,)), pltpu.SemaphoreType.DMA],
    )(gi2, go2, st)


def _adamw_math(w, g, m, v):
    m2 = ADAM_B1 * m + (1.0 - ADAM_B1) * g
    v2 = ADAM_B2 * v + (1.0 - ADAM_B2) * (g * g)
    m_hat = m2 / (1.0 - ADAM_B1 ** ADAM_STEP)
    v_hat = v2 / (1.0 - ADAM_B2 ** ADAM_STEP)
    delta = -ADAM_LR * (m_hat / (jnp.sqrt(v_hat) + ADAM_EPS) + ADAM_WD * w)
    return delta, m2, v2


def _adamw(w, g, m, v, name):
    rows, cols = w.shape
    tm = 256 if rows % 256 == 0 else rows

    def body(w_ref, g_ref, m_ref, v_ref, d_ref, m2_ref, v2_ref):
        d_ref[...], m2_ref[...], v2_ref[...] = _adamw_math(w_ref[...], g_ref[...], m_ref[...], v_ref[...])

    shape = jax.ShapeDtypeStruct(w.shape, F32)
    return pl.pallas_call(
        body, grid=(rows // tm,), name=name,
        in_specs=[_rows(tm, cols)] * 4, out_specs=[_rows(tm, cols)] * 3, out_shape=[shape] * 3,
        compiler_params=_params(("arbitrary",)),
    )(w, g, m, v)


def _adamw_vectors(g5, first_parts, ws, ms, vs):
    n = len(ws)

    def body(g_ref, parts_ref, *refs):
        ins, g0_ref, outs = refs[:3 * n], refs[3 * n], refs[3 * n + 1:]
        g0 = parts_ref[0, 0:1, :]
        for dev in range(1, NDEV):
            g0 = g0 + parts_ref[dev, 0:1, :]
        g0_ref[...] = g0
        for i in range(n):
            g = g0 if i == 0 else g_ref[i:i + 1, :]
            res = _adamw_math(ins[i][...], g, ins[n + i][...], ins[2 * n + i][...])
            for kind in range(3):
                outs[kind * n + i][...] = res[kind]

    shape = jax.ShapeDtypeStruct((1, D), F32)
    return pl.pallas_call(body, name="adamw_vectors", out_shape=[shape] * (1 + 3 * n), compiler_params=_params())(
        g5, first_parts, *ws, *ms, *vs)


def kernel(x, norm_g, w_in, conv_w, conv_b, conv_ln_g, conv_ln_b, w_out, final_norm_g, loss_target, m_norm_g, m_w_in, m_conv_w, m_conv_b, m_conv_ln_g, m_conv_ln_b, m_w_out, m_final_norm_g, v_norm_g, v_w_in, v_conv_w, v_conv_b, v_conv_ln_g, v_conv_ln_b, v_w_out, v_final_norm_g):
    chip = 2 * lax.axis_index("x") + lax.axis_index("y")
    where = jnp.stack([chip, lax.axis_index("c")]).astype(jnp.int32)
    taps_shard = jnp.pad(conv_w[0], ((0, HALO - CONV_K), (0, 0)))
    wi_full, wo_full, cw_full = _gather_weights(*_place_shards(w_in[0], w_out[0], taps_shard, where))

    gf = final_norm_g[None]
    dw_in4, dw_out, small, dproj_pieces, dx2 = _local_step(
        x[0], loss_target[0], norm_g, wi_full, cw_full, conv_b, conv_ln_g, conv_ln_b, wo_full, gf)
    dw_out4 = dw_out.reshape(NCHIP, WOUT_SHARD, D)

    ri, ro, rs = _exchange_halves(dw_in4, dw_out4, small)
    pi, po, ps = _add_halves(dw_in4, ri, dw_out4, ro, small, rs)
    grad_x, st_in, ri, ro, rs = _inproj_bwd_x(*dproj_pieces, wi_full, x[0], norm_g, dx2, pi, po, ps)
    gi2, go2, g_small, g5, loss8 = _sum_chips(ri, ro, rs, pi, po, ps, where)
    gi2, go2, norm_g_parts = _exchange_results(gi2, go2, st_in)
    g_w_in = gi2.reshape(D, CHUNK)
    g_w_out = go2.reshape(WOUT_SHARD, D)
    g_taps = lax.dynamic_slice(g_small, (ROW_TAPS, chip * CONVW_SHARD), (CONV_K, CONVW_SHARD))

    d_w_in, m2_w_in, v2_w_in = _adamw(w_in[0], g_w_in, m_w_in[0], v_w_in[0], "adamw_w_in")
    d_w_out, m2_w_out, v2_w_out = _adamw(w_out[0], g_w_out, m_w_out[0], v_w_out[0], "adamw_w_out")
    d_taps, m2_taps, v2_taps = _adamw(conv_w[0], g_taps, m_conv_w[0], v_conv_w[0], "adamw_conv_w")
    g_norm, *vec = _adamw_vectors(
        g5, norm_g_parts,
        (norm_g, conv_b, conv_ln_g, conv_ln_b, gf),
        (m_norm_g, m_conv_b, m_conv_ln_g, m_conv_ln_b, m_final_norm_g[None]),
        (v_norm_g, v_conv_b, v_conv_ln_g, v_conv_ln_b, v_final_norm_g[None]))
    d_vec, m2_vec, v2_vec = vec[0:5], vec[5:10], vec[10:15]

    def weight_order(ng, wi, cw, cb, lg, lb, wo, fg):
        return (ng, wi[None], cw[None], cb, lg, lb, wo[None], fg[0])

    grads = weight_order(g_norm, g_w_in, g_taps, g5[1:2], g5[2:3], g5[3:4], g_w_out, g5[4:5])
    deltas = weight_order(d_vec[0], d_w_in, d_taps, d_vec[1], d_vec[2], d_vec[3], d_w_out, d_vec[4])
    new_m = weight_order(m2_vec[0], m2_w_in, m2_taps, m2_vec[1], m2_vec[2], m2_vec[3], m2_w_out, m2_vec[4])
    new_v = weight_order(v2_vec[0], v2_w_in, v2_taps, v2_vec[1], v2_vec[2], v2_vec[3], v2_w_out, v2_vec[4])
    return (loss8[0, 0], grad_x[None], *grads, *deltas, *new_m, *new_v)
```

```python
import jax
import jax.numpy as jnp
from jax import lax
from jax.experimental import pallas as pl
from jax.experimental.pallas import tpu as pltpu

F32 = jnp.float32
BF16 = jnp.bfloat16

S = 4096
D = 1024
LANES = 128
HD = 64
NKV = 4
GQ = 4
KVW = NKV * HD
NCOL = 5632
CONV_K = 31
HALO = 32
BLK = 128
PATTERNS = (1, 4, 16)
NORM_EPS = 1e-6
LN_EPS = 1e-5
NEG = -1e30
OFF_Q, OFF_K, OFF_V, OFF_AG, OFF_CV, OFF_CG, OFF_CGATE = 0, 1024, 1280, 1536, 2560, 3584, 4608
NCHIP = 4
CHUNK = NCOL // NCHIP
WOUT_ROWS = 2 * D
WOUT_SHARD = WOUT_ROWS // NCHIP
CONVW_SHARD = D // NCHIP

ADAM_LR, ADAM_B1, ADAM_B2, ADAM_EPS, ADAM_WD, ADAM_STEP = 0.001, 0.9, 0.999, 1e-08, 0.01, 10

VMEM_LIMIT = 56 * 1024 * 1024


def _params(sem=None, vmem=VMEM_LIMIT):
    return pltpu.CompilerParams(dimension_semantics=sem, vmem_limit_bytes=vmem)


def _sigmoid(a):
    return 0.5 * jnp.tanh(0.5 * a) + 0.5


def _rows(tm, width):
    return pl.BlockSpec((tm, width), lambda i: (i, 0))


def _slabs(n):
    return jax.ShapeDtypeStruct((n, S, LANES), F32)


def _slab_rows(n, tm):
    return pl.BlockSpec((n, tm, LANES), lambda i: (0, i, 0))


def _resident(shape):
    return pl.BlockSpec(shape, lambda *_: (0,) * len(shape), pipeline_mode=pl.Buffered(1))


def _dot(a, b):
    return jnp.dot(a, b, preferred_element_type=F32)


def _dot_nt(a, b):
    return lax.dot_general(a, b, (((1,), (1,)), ((), ())), preferred_element_type=F32)


def _dot_tn(a, b):
    return lax.dot_general(a, b, (((0,), (0,)), ((), ())), preferred_element_type=F32)


def _inproj_fwd(x, g1, w_bf):
    tm = 512

    def body(x_ref, g_ref, w_ref, h_ref, q_ref, k_ref, v_ref, ag_ref, cv_ref, cg_ref, cgate_ref):
        xt = x_ref[...]
        r = lax.rsqrt(jnp.mean(xt * xt, axis=-1, keepdims=True) + NORM_EPS)
        h = (xt * r * g_ref[...]).astype(BF16)
        h_ref[...] = h
        q = _dot(h, w_ref[:, OFF_Q:OFF_Q + D]) * (HD ** -0.5)
        kv = _dot(h, w_ref[:, OFF_K:OFF_K + 2 * KVW])
        for sl in range(D // LANES):
            q_ref[sl] = q[:, sl * LANES:(sl + 1) * LANES]
        for sl in range(KVW // LANES):
            k_ref[sl] = kv[:, sl * LANES:(sl + 1) * LANES]
            v_ref[sl] = kv[:, KVW + sl * LANES:KVW + (sl + 1) * LANES]
        ag_ref[...] = _dot(h, w_ref[:, OFF_AG:OFF_AG + D])
        cv_ref[...] = _dot(h, w_ref[:, OFF_CV:OFF_CV + D])
        cg_ref[...] = _dot(h, w_ref[:, OFF_CG:OFF_CG + D])
        cgate_ref[...] = _dot(h, w_ref[:, OFF_CGATE:OFF_CGATE + D])

    big = jax.ShapeDtypeStruct((S, D), F32)
    return pl.pallas_call(
        body, grid=(S // tm,), name="inproj_fwd",
        in_specs=[_rows(tm, D), _resident((1, D)), _resident((D, NCOL))],
        out_specs=[_rows(tm, D), _slab_rows(D // LANES, tm), _slab_rows(KVW // LANES, tm), _slab_rows(KVW // LANES, tm),
                   _rows(tm, D), _rows(tm, D), _rows(tm, D), _rows(tm, D)],
        out_shape=[jax.ShapeDtypeStruct((S, D), BF16), _slabs(D // LANES), _slabs(KVW // LANES), _slabs(KVW // LANES),
                   big, big, big, big],
        compiler_params=_params(("arbitrary",)),
    )(x, g1, w_bf)


def _bias_table(d):
    h = jnp.arange(NKV * GQ, dtype=F32)
    slopes = jnp.exp2(-8.0 * (h + 1.0) / (NKV * GQ))
    qi = jnp.arange(BLK)[:, None]
    kj = jnp.arange(2 * BLK)[None, :]
    dist = BLK + qi - kj
    window = (dist >= 0) & (dist <= BLK)
    bias = -slopes[:, None, None] * (dist * d).astype(F32)[None]
    has_prev = jnp.stack([jnp.broadcast_to(kj >= BLK, (BLK, 2 * BLK)), jnp.ones((BLK, 2 * BLK), bool)])
    valid = window[None] & has_prev
    tab = jnp.where(valid[:, None], bias[None], NEG)
    return tab.reshape(2, NKV, GQ * BLK, 2 * BLK)


def _sub_rows(start, d):
    if d == 1:
        return pl.ds(pl.multiple_of(start, BLK), BLK)
    return pl.ds(start, BLK, stride=d)


NHEAD = NKV * GQ
CHUNK_ROWS = 2048
BLOCKS_PER_CHUNK = CHUNK_ROWS // BLK


def _low_lanes(rows=BLK):
    return lax.broadcasted_iota(jnp.int32, (rows, LANES), 1) < HD


def _block_start(idx, d):
    shift = d.bit_length() - 1
    b, r = lax.shift_right_logical(idx, shift), lax.bitwise_and(idx, d - 1)
    start = b * (BLK * d) + r
    return b, start, jnp.maximum(start - BLK * d, r)


def _stack_heads(ref, rows):
    low = _low_lanes()
    t0, t1 = ref[0, rows, :], ref[1, rows, :]
    return jnp.concatenate([jnp.where(low, t0, 0.0), jnp.where(low, 0.0, t0),
                            jnp.where(low, t1, 0.0), jnp.where(low, 0.0, t1)], axis=0).astype(BF16)


def _unstack_heads(dup):
    low = _low_lanes()
    return (jnp.where(low, dup[0:BLK], dup[BLK:2 * BLK]), jnp.where(low, dup[2 * BLK:3 * BLK], dup[3 * BLK:4 * BLK]))


def _kv_dup(ref, prow, rows, odd):
    t = jnp.concatenate([ref[0, prow, :], ref[0, rows, :]], axis=0)
    swapped = pltpu.roll(t, HD, axis=1)
    keep = jnp.logical_xor(_low_lanes(2 * BLK), odd)
    return jnp.where(keep, t, swapped).astype(BF16)


def _attn_fwd(q, k, v, bias, d):
    def body(q_ref, k_ref, v_ref, b_ref, o_ref, l_ref):
        odd = pl.program_id(0) % 2 == 1
        ones = jnp.ones((2 * BLK, LANES), BF16)

        def block(idx, carry):
            b, start, pstart = _block_start(idx, d)
            rows, prow = _sub_rows(start, d), _sub_rows(pstart, d)
            qs = _stack_heads(q_ref, rows)
            kw = _kv_dup(k_ref, prow, rows, odd)
            vw = _kv_dup(v_ref, prow, rows, odd)
            s = _dot_nt(qs, kw) + b_ref[jnp.minimum(b, 1), 0]
            m = jnp.max(s, axis=1, keepdims=True)
            p = jnp.exp(s - m).astype(BF16)
            ol = _dot(p, jnp.concatenate([vw, ones], axis=1))
            l = ol[:, LANES:]
            o_ref[0, rows, :], o_ref[1, rows, :] = _unstack_heads(ol[:, :LANES] / l)
            l_ref[0, rows, :] = _by_head([(m + jnp.log(l))[g * BLK:(g + 1) * BLK] for g in range(GQ)])
            return carry

        lax.fori_loop(0, S // BLK, block, 0, unroll=2)

    q_like = pl.BlockSpec((2, S, LANES), lambda j: (j, 0, 0))
    kv = pl.BlockSpec((1, S, LANES), lambda j: (j // 2, 0, 0))
    per_kv = pl.BlockSpec((1, S, LANES), lambda j: (j, 0, 0))
    bias_spec = pl.BlockSpec((2, 1, GQ * BLK, 2 * BLK), lambda j: (0, j, 0, 0))
    return pl.pallas_call(
        body, grid=(NKV,), name=f"attn_fwd_d{d}",
        in_specs=[q_like, kv, kv, bias_spec],
        out_specs=[q_like, per_kv],
        out_shape=[_slabs(D // LANES), _slabs(NKV)],
        compiler_params=_params(("arbitrary",)),
    )(q, k, v, bias)


PIECES = 3


def _by_head(tiles):
    lane = lax.broadcasted_iota(jnp.int32, tiles[0].shape, 1)
    out = tiles[0]
    for g in range(1, GQ):
        out = jnp.where(lax.bitwise_and(lane, GQ - 1) == g, tiles[g], out)
    return out


def _minus_in_pieces(x):
    lane = lax.broadcasted_iota(jnp.int32, x.shape, 1)
    hi = (-x).astype(BF16).astype(F32)
    rest = -x - hi
    mid = rest.astype(BF16).astype(F32)
    lo = (rest - mid).astype(BF16).astype(F32)
    return jnp.where(lane < GQ, hi, jnp.where(lane < 2 * GQ, mid, jnp.where(lane < PIECES * GQ, lo, 0.0)))


def _attn_combine(outs, lses, a_gate):
    tm = 256

    def body(o1, o2, o3, l1, l2, l3, ag_ref, o_ref, lse_ref, y_ref):
        low = _low_lanes(tm)
        for j in range(NKV):
            a, b, c = l1[j], l2[j], l3[j]
            m = jnp.maximum(jnp.maximum(a, b), c)
            ea, eb, ec = jnp.exp(a - m), jnp.exp(b - m), jnp.exp(c - m)
            den = ea + eb + ec
            lse_ref[j] = _minus_in_pieces(m + jnp.log(den))
            inv = 1.0 / den
            for half in range(2):
                sl = 2 * j + half

                def spread(w):
                    return jnp.where(low, w[:, 2 * half:2 * half + 1], w[:, 2 * half + 1:2 * half + 2])

                o = spread(ea * inv) * o1[sl] + spread(eb * inv) * o2[sl] + spread(ec * inv) * o3[sl]
                o_ref[sl] = o
                cols = slice(sl * LANES, (sl + 1) * LANES)
                ag = ag_ref[:, cols]
                y_ref[:, cols] = (o * (ag * _sigmoid(ag))).astype(BF16)

    wide, per_kv = _slab_rows(D // LANES, tm), _slab_rows(NKV, tm)
    return pl.pallas_call(
        body, grid=(S // tm,), name="attn_combine",
        in_specs=[wide] * 3 + [per_kv] * 3 + [_rows(tm, D)],
        out_specs=[wide, per_kv, _rows(tm, D)],
        out_shape=[_slabs(D // LANES), _slabs(NKV), jax.ShapeDtypeStruct((S, D), BF16)],
        compiler_params=_params(("arbitrary",)),
    )(*outs, *lses, a_gate)


def _head_sum_selectors():
    lane_in = jnp.arange(LANES)[:, None] // HD
    return jnp.stack([jnp.broadcast_to(lane_in == h, (LANES, LANES)) for h in range(2)]).astype(BF16)


def _attn_gate_bwd(dy_att, o, a_gate, selectors):
    tm = 256

    def body(dy_ref, o_ref, ag_ref, e_ref, do_ref, dag_ref, delta_ref):
        for j in range(NKV):
            deltas = []
            for sl in (2 * j, 2 * j + 1):
                cols = slice(sl * LANES, (sl + 1) * LANES)
                dy, ag, o_ = dy_ref[:, cols], ag_ref[:, cols], o_ref[sl]
                sg = _sigmoid(ag)
                do = dy * (ag * sg)
                do_ref[sl] = do
                dag_ref[:, cols] = (dy * o_ * (sg * (1.0 + ag * (1.0 - sg)))).astype(BF16)
                prod = do * o_
                hi = prod.astype(BF16)
                lo = (prod - hi.astype(F32)).astype(BF16)
                deltas += [_dot(hi, e_ref[h]) + _dot(lo, e_ref[h]) for h in range(2)]
            delta_ref[j] = _minus_in_pieces(_by_head(deltas))

    return pl.pallas_call(
        body, grid=(S // tm,), name="attn_gate_bwd",
        in_specs=[_rows(tm, D), _slab_rows(D // LANES, tm), _rows(tm, D), _resident((2, LANES, LANES))],
        out_specs=[_slab_rows(D // LANES, tm), _rows(tm, D), _slab_rows(NKV, tm)],
        out_shape=[_slabs(D // LANES), jax.ShapeDtypeStruct((S, D), BF16), _slabs(NKV)],
        compiler_params=_params(("arbitrary",)),
    )(dy_att, o, a_gate, selectors)


def _own_pieces(tile):
    lane = lax.broadcasted_iota(jnp.int32, tile.shape, 1)
    head = jnp.where(lane < PIECES * GQ, lax.bitwise_and(lane, GQ - 1), -1)
    return jnp.concatenate([jnp.where(head == g, tile, 0.0) for g in range(GQ)], axis=0).astype(BF16)


def _attn_bwd(q, k, v, do, lse, delta, bias, d):
    def body(q_ref, do_ref, l_ref, dl_ref, k_ref, v_ref, b_ref, dq_ref, dkv_ref):
        odd = pl.program_id(0) % 2 == 1
        chunk = pl.program_id(1)
        ones = (lax.broadcasted_iota(jnp.int32, (2 * BLK, LANES), 1) < PIECES * GQ).astype(BF16)

        @pl.when(chunk == 0)
        def _():
            dkv_ref[...] = jnp.zeros_like(dkv_ref)

        def block(idx, carry):
            b, start, pstart = _block_start(chunk * BLOCKS_PER_CHUNK + idx, d)
            rows, prow = _sub_rows(start, d), _sub_rows(pstart, d)
            mine = _sub_rows(start - chunk * CHUNK_ROWS, d)
            qs = _stack_heads(q_ref, mine)
            dos = _stack_heads(do_ref, mine)
            kw = _kv_dup(k_ref, prow, rows, odd)
            vw = _kv_dup(v_ref, prow, rows, odd)
            s = _dot_nt(jnp.concatenate([qs, _own_pieces(l_ref[0, mine, :])], axis=1),
                        jnp.concatenate([kw, ones], axis=1)) + b_ref[jnp.minimum(b, 1), 0]
            p = jnp.exp(s)
            dv2 = _dot_tn(p.astype(BF16), dos)
            dp = _dot_nt(jnp.concatenate([dos, _own_pieces(dl_ref[0, mine, :])], axis=1),
                         jnp.concatenate([vw, ones], axis=1))
            ds = (p * dp).astype(BF16)
            dq_ref[0, mine, :], dq_ref[1, mine, :] = _unstack_heads(_dot(ds, kw))
            dk2 = _dot_tn(ds, qs)
            dkv = jnp.where(_low_lanes(2 * BLK), dk2 + pltpu.roll(dk2, HD, axis=1), dv2 + pltpu.roll(dv2, HD, axis=1))
            dkv_ref[0, rows, :] = dkv_ref[0, rows, :] + dkv[BLK:]
            dkv_ref[0, prow, :] = dkv_ref[0, prow, :] + dkv[:BLK]
            return carry

        lax.fori_loop(0, BLOCKS_PER_CHUNK, block, 0, unroll=8)

    q_like = pl.BlockSpec((2, CHUNK_ROWS, LANES), lambda j, c: (j, c, 0))
    pieces = pl.BlockSpec((1, CHUNK_ROWS, LANES), lambda j, c: (j, c, 0))
    kv = pl.BlockSpec((1, S, LANES), lambda j, c: (j // 2, 0, 0))
    per_kv = pl.BlockSpec((1, S, LANES), lambda j, c: (j, 0, 0))
    bias_spec = pl.BlockSpec((2, 1, GQ * BLK, 2 * BLK), lambda j, c: (0, j, 0, 0))
    return pl.pallas_call(
        body, grid=(NKV, S // CHUNK_ROWS), name=f"attn_bwd_d{d}",
        in_specs=[q_like, q_like, pieces, pieces, kv, kv, bias_spec],
        out_specs=[q_like, per_kv],
        out_shape=[_slabs(D // LANES), _slabs(NKV)],
        compiler_params=_params(("arbitrary", "arbitrary")),
    )(q, do, lse, delta, k, v, bias)


CONV_T = 128


def _halo_before(i):
    return (jnp.maximum(i * (CONV_T // HALO) - 1, 0), 0)


def _halo_after(i):
    return (jnp.minimum((i + 1) * (CONV_T // HALO), S // HALO - 1), 0)


SUBLANES = 8
NCH = D // LANES
GROUP = SUBLANES * SUBLANES


def _comb(ref, cb, base):
    return ref[cb, pl.ds(base, SUBLANES, stride=SUBLANES), :]


def _taps(w_ref, cols):
    return [jnp.broadcast_to(w_ref[j:j + 1, cols], (SUBLANES, LANES)) for j in range(CONV_K)]


def _conv_fwd(c_val, c_glu, c_gate, conv_w, conv_b, ln_g, ln_b):
    T = CONV_T

    def body(cv_ref, cg_ref, cvh_ref, cgh_ref, gate_ref, w_ref, b_ref, lg_ref, lb_ref, u_ref, y_ref, win, us):
        i = pl.program_id(0)
        for cb in range(NCH):
            cols = slice(cb * LANES, (cb + 1) * LANES)
            win[cb, HALO:HALO + T, :] = cv_ref[:, cols] * _sigmoid(cg_ref[:, cols])
            win[cb, 0:HALO, :] = jnp.where(i > 0, cvh_ref[:, cols] * _sigmoid(cgh_ref[:, cols]), 0.0)
        for cb in range(NCH):
            cols = slice(cb * LANES, (cb + 1) * LANES)
            taps = _taps(w_ref, cols)
            bias = jnp.broadcast_to(b_ref[:, cols], (SUBLANES, LANES))

            def group(g, carry):
                for b in range(SUBLANES):
                    base = g * GROUP + b
                    acc = bias
                    for j in range(CONV_K):
                        acc = acc + taps[j] * _comb(win, cb, base + (HALO - (CONV_K - 1) + j))
                    us[cb, pl.ds(base, SUBLANES, stride=SUBLANES), :] = acc
                return carry

            lax.fori_loop(0, T // GROUP, group, 0)
        total = us[0]
        for cb in range(1, NCH):
            total = total + us[cb]
        mu = jnp.sum(total, axis=-1, keepdims=True) * (1.0 / D)
        sq = jnp.zeros((T, LANES), F32)
        for cb in range(NCH):
            uc = us[cb] - mu
            sq = sq + uc * uc
        rstd = lax.rsqrt(jnp.sum(sq, axis=-1, keepdims=True) * (1.0 / D) + LN_EPS)
        for cb in range(NCH):
            cols = slice(cb * LANES, (cb + 1) * LANES)
            u = us[cb]
            u_ref[:, cols] = u
            nrm = (u - mu) * rstd * lg_ref[:, cols] + lb_ref[:, cols]
            gate = gate_ref[:, cols]
            y_ref[:, cols] = (nrm * _sigmoid(nrm) * (gate * _sigmoid(gate))).astype(BF16)

    halo = pl.BlockSpec((HALO, D), _halo_before)
    return pl.pallas_call(
        body, grid=(S // T,), name="conv_fwd",
        in_specs=[_rows(T, D), _rows(T, D), halo, halo, _rows(T, D),
                  _resident((HALO, D)), _resident((1, D)), _resident((1, D)), _resident((1, D))],
        out_specs=[_rows(T, D), _rows(T, D)],
        out_shape=[jax.ShapeDtypeStruct((S, D), F32), jax.ShapeDtypeStruct((S, D), BF16)],
        scratch_shapes=[pltpu.VMEM((NCH, T + HALO, LANES), F32), pltpu.VMEM((NCH, T, LANES), F32)],
        compiler_params=_params(("arbitrary",)),
    )(c_val, c_glu, c_val, c_glu, c_gate, conv_w, conv_b, ln_g, ln_b)


def _conv_bwd_rows(u, c_gate, dy_conv, ln_g, ln_b):
    tm = 256

    def body(u_ref, gate_ref, dy_ref, lg_ref, lb_ref, du_ref, dgate_ref, st_ref):
        @pl.when(pl.program_id(0) == 0)
        def _():
            st_ref[...] = jnp.zeros_like(st_ref)

        u, gate, dy = u_ref[...], gate_ref[...], dy_ref[...]
        mu = jnp.mean(u, axis=-1, keepdims=True)
        uc = u - mu
        rstd = lax.rsqrt(jnp.mean(uc * uc, axis=-1, keepdims=True) + LN_EPS)
        z = uc * rstd
        nrm = z * lg_ref[...] + lb_ref[...]
        sn, sg = _sigmoid(nrm), _sigmoid(gate)
        dgate_ref[...] = (dy * (nrm * sn) * (sg * (1.0 + gate * (1.0 - sg)))).astype(BF16)
        dn = dy * (gate * sg) * (sn * (1.0 + nrm * (1.0 - sn)))
        dz = dn * lg_ref[...]
        du = rstd * (dz - jnp.mean(dz, axis=-1, keepdims=True) - z * jnp.mean(dz * z, axis=-1, keepdims=True))
        du_ref[...] = du
        st_ref[0:1, :] += jnp.sum(dn * z, axis=0, keepdims=True)
        st_ref[1:2, :] += jnp.sum(dn, axis=0, keepdims=True)
        st_ref[2:3, :] += jnp.sum(du, axis=0, keepdims=True)

    big = jax.ShapeDtypeStruct((S, D), F32)
    return pl.pallas_call(
        body, grid=(S // tm,), name="conv_bwd_rows",
        in_specs=[_rows(tm, D)] * 3 + [_resident((1, D)), _resident((1, D))],
        out_specs=[_rows(tm, D), _rows(tm, D), pl.BlockSpec((8, D), lambda i: (0, 0))],
        out_shape=[big, jax.ShapeDtypeStruct((S, D), BF16), jax.ShapeDtypeStruct((8, D), F32)],
        compiler_params=_params(("arbitrary",)),
    )(u, c_gate, dy_conv, ln_g, ln_b)


def _conv_bwd_taps(du, c_val, c_glu, conv_w):
    T = CONV_T
    last = S // T - 1

    def body(du_ref, dua_ref, cv_ref, cg_ref, cvh_ref, cgh_ref, w_ref, dcv_ref, dcg_ref, dw_ref,
             hwin, dwin, dhs, dw_acc):
        i = pl.program_id(0)

        @pl.when(i == 0)
        def _():
            dw_acc[...] = jnp.zeros_like(dw_acc)

        for cb in range(NCH):
            cols = slice(cb * LANES, (cb + 1) * LANES)
            hwin[cb, HALO:HALO + T, :] = cv_ref[:, cols] * _sigmoid(cg_ref[:, cols])
            hwin[cb, 0:HALO, :] = jnp.where(i > 0, cvh_ref[:, cols] * _sigmoid(cgh_ref[:, cols]), 0.0)
            dwin[cb, 0:T, :] = du_ref[:, cols]
            dwin[cb, T:T + HALO, :] = jnp.where(i < last, dua_ref[:, cols], 0.0)
        for cb in range(NCH):
            cols = slice(cb * LANES, (cb + 1) * LANES)
            taps = _taps(w_ref, cols)

            def group_dh(g, carry):
                for b in range(SUBLANES):
                    base = g * GROUP + b
                    acc = jnp.zeros((SUBLANES, LANES), F32)
                    for j in range(CONV_K):
                        acc = acc + taps[j] * _comb(dwin, cb, base + (CONV_K - 1 - j))
                    dhs[cb, pl.ds(base, SUBLANES, stride=SUBLANES), :] = acc
                return carry

            lax.fori_loop(0, T // GROUP, group_dh, 0)

            def group_dw(g, sums):
                for b in range(SUBLANES):
                    base = g * GROUP + b
                    d = _comb(dwin, cb, base)
                    sums = tuple(sums[j] + d * _comb(hwin, cb, base + (HALO - (CONV_K - 1) + j))
                                 for j in range(CONV_K))
                return sums

            sums = lax.fori_loop(0, T // GROUP, group_dw, tuple(dw_acc[j, :, cols] for j in range(CONV_K)))
            for j in range(CONV_K):
                dw_acc[j, :, cols] = sums[j]
            dh = dhs[cb]
            cv, sg = cv_ref[:, cols], _sigmoid(cg_ref[:, cols])
            dcv_ref[:, cols] = (dh * sg).astype(BF16)
            dcg_ref[:, cols] = (dh * cv * (sg * (1.0 - sg))).astype(BF16)

        @pl.when(i == last)
        def _():
            dw_ref[...] = jnp.zeros_like(dw_ref)
            for j in range(CONV_K):
                dw_ref[j:j + 1, :] = jnp.sum(dw_acc[j], axis=0, keepdims=True)

    before = pl.BlockSpec((HALO, D), _halo_before)
    after = pl.BlockSpec((HALO, D), _halo_after)
    big = jax.ShapeDtypeStruct((S, D), BF16)
    return pl.pallas_call(
        body, grid=(S // T,), name="conv_bwd_taps",
        in_specs=[_rows(T, D), after, _rows(T, D), _rows(T, D), before, before, _resident((HALO, D))],
        out_specs=[_rows(T, D), _rows(T, D), pl.BlockSpec((HALO, D), lambda i: (0, 0))],
        out_shape=[big, big, jax.ShapeDtypeStruct((HALO, D), F32)],
        scratch_shapes=[pltpu.VMEM((NCH, T + HALO, LANES), F32), pltpu.VMEM((NCH, T + HALO, LANES), F32),
                        pltpu.VMEM((NCH, T, LANES), F32), pltpu.VMEM((CONV_K, SUBLANES, D), F32)],
        compiler_params=_params(("arbitrary",)),
    )(du, du, c_val, c_glu, c_val, c_glu, conv_w)


def _outproj_loss(y_att, y_conv, w_out_bf, x, target, gf):
    tm = 256

    def body(ya_ref, yc_ref, w_ref, x_ref, t_ref, gf_ref, dx2_ref, dya_ref, dyc_ref, dw_ref, st_ref, acc):
        @pl.when(pl.program_id(0) == 0)
        def _():
            acc[...] = jnp.zeros_like(acc)
            st_ref[...] = jnp.zeros_like(st_ref)

        ya, yc = ya_ref[...], yc_ref[...]
        x2 = x_ref[...] + _dot(ya, w_ref[0:D, :]) + _dot(yc, w_ref[D:2 * D, :])
        r = lax.rsqrt(jnp.mean(x2 * x2, axis=-1, keepdims=True) + NORM_EPS)
        xn = x2 * r
        err = xn * gf_ref[...] - t_ref[...]
        dout = err * (1.0 / D)
        dxn = dout * gf_ref[...]
        dx2 = r * (dxn - xn * jnp.mean(dxn * xn, axis=-1, keepdims=True))
        dx2_ref[...] = dx2
        dx2b = dx2.astype(BF16)
        dya_ref[...] = _dot_nt(dx2b, w_ref[0:D, :])
        dyc_ref[...] = _dot_nt(dx2b, w_ref[D:2 * D, :])
        acc[0:D, :] += _dot_tn(ya, dx2b)
        acc[D:2 * D, :] += _dot_tn(yc, dx2b)
        st_ref[0:1, :] += jnp.sum(dout * xn, axis=0, keepdims=True)
        st_ref[1:2, :] += jnp.sum(err * err, axis=0, keepdims=True) * (0.5 / D)

        @pl.when(pl.program_id(0) == S // tm - 1)
        def _():
            dw_ref[...] = acc[...].astype(BF16)

    big = jax.ShapeDtypeStruct((S, D), F32)
    return pl.pallas_call(
        body, grid=(S // tm,), name="outproj_loss",
        in_specs=[_rows(tm, D), _rows(tm, D), _resident((WOUT_ROWS, D)), _rows(tm, D), _rows(tm, D), _resident((1, D))],
        out_specs=[_rows(tm, D), _rows(tm, D), _rows(tm, D),
                   pl.BlockSpec((WOUT_ROWS, D), lambda i: (0, 0)), pl.BlockSpec((8, D), lambda i: (0, 0))],
        out_shape=[big, big, big, jax.ShapeDtypeStruct((WOUT_ROWS, D), BF16), jax.ShapeDtypeStruct((8, D), F32)],
        scratch_shapes=[pltpu.VMEM((WOUT_ROWS, D), F32)],
        compiler_params=_params(("arbitrary",)),
    )(y_att, y_conv, w_out_bf, x, target, gf)


UNITS_PER_CHUNK = CHUNK // LANES


def _dproj_unit(u, dqs, dkvs, gates, rows):
    if u < OFF_K // LANES:
        return ((dqs[0][u] + dqs[1][u] + dqs[2][u]) * (HD ** -0.5)).astype(BF16)
    if u < OFF_AG // LANES:
        w = u - OFF_K // LANES
        ta, tb = (dkvs[0][j] + dkvs[1][j] + dkvs[2][j] for j in (2 * (w % 2), 2 * (w % 2) + 1))
        low = _low_lanes(rows)
        if w < 2:
            return jnp.where(low, ta, pltpu.roll(tb, HD, axis=1)).astype(BF16)
        return jnp.where(low, pltpu.roll(ta, HD, axis=1), tb).astype(BF16)
    g, sl = divmod(u - OFF_AG // LANES, D // LANES)
    return gates[g][:, sl * LANES:(sl + 1) * LANES]


def _dproj_sources(units, dqs, dkvs, gates, rows):
    use_q = any(u < OFF_K // LANES for u in units)
    use_kv = any(OFF_K // LANES <= u < OFF_AG // LANES for u in units)
    use_g = sorted({(u - OFF_AG // LANES) // (D // LANES) for u in units if u >= OFF_AG // LANES})
    args = (list(dqs) if use_q else []) + (list(dkvs) if use_kv else []) + [gates[g] for g in use_g]
    specs = ([_slab_rows(D // LANES, rows)] * 3 if use_q else []) + ([_slab_rows(NKV, rows)] * 3 if use_kv else []) \
        + [_rows(rows, D)] * len(use_g)

    def pick(refs):
        refs = list(refs)
        q_refs = [refs.pop(0) for _ in range(3)] if use_q else None
        kv_refs = [refs.pop(0) for _ in range(3)] if use_kv else None
        return q_refs, kv_refs, {g: refs.pop(0) for g in use_g}

    return args, specs, pick


def _inproj_bwd_x(dqs, dkvs, gates, w_bf, x, g1, dx2, pi, po, ps):
    tm = 256
    last = S // tm - 1
    units = range(NCOL // LANES)
    pieces, piece_specs, pick = _dproj_sources(units, dqs, dkvs, gates, tm)

    def body(*refs):
        piece_refs, refs = refs[:len(pieces)], refs[len(pieces):]
        (w_ref, x_ref, g_ref, dx2_ref, pi_ref, po_ref, ps_ref,
         gx_ref, st_ref, ri_ref, ro_ref, rs_ref, dp_ref, send, recv) = refs
        i = pl.program_id(0)
        copies = _chip_exchange_copies((pi_ref, po_ref, ps_ref), (ri_ref, ro_ref, rs_ref), send, recv)

        @pl.when(i == 0)
        def _():
            st_ref[...] = jnp.zeros_like(st_ref)
            for out, _ in copies:
                out.start()

        sources = pick(piece_refs)
        for u in units:
            dp_ref[:, u * LANES:(u + 1) * LANES] = _dproj_unit(u, *sources, tm)
        dh = _dot_nt(dp_ref[...], w_ref[...])
        xt = x_ref[...]
        r = lax.rsqrt(jnp.mean(xt * xt, axis=-1, keepdims=True) + NORM_EPS)
        xn = xt * r
        dxn = dh * g_ref[...]
        gx_ref[...] = dx2_ref[...] + r * (dxn - xn * jnp.mean(dxn * xn, axis=-1, keepdims=True))
        st_ref[0:1, :] += jnp.sum(dh * xn, axis=0, keepdims=True)

        @pl.when(i == last)
        def _():
            for _, arrival in copies:
                arrival.wait_recv()
            for out, _ in copies:
                out.wait_send()

    n = 3 * len(CHIP_FLIPS)
    return pl.pallas_call(
        body, grid=(S // tm,), name="inproj_bwd_x",
        in_specs=piece_specs + [_resident((D, NCOL)), _rows(tm, D), _resident((1, D)), _rows(tm, D), ANY, ANY, ANY],
        out_specs=[_rows(tm, D), pl.BlockSpec((8, D), lambda i: (0, 0)), ANY, ANY, ANY],
        out_shape=[jax.ShapeDtypeStruct((S, D), F32), jax.ShapeDtypeStruct((8, D), F32),
                   jax.ShapeDtypeStruct((NCHIP, D // 2, CHUNK), BF16),
                   jax.ShapeDtypeStruct((NCHIP, WOUT_SHARD // 2, D), BF16),
                   jax.ShapeDtypeStruct((NCHIP, SMALL_ROWS, D), F32)],
        scratch_shapes=[pltpu.VMEM((tm, NCOL), BF16), pltpu.SemaphoreType.DMA((n,)), pltpu.SemaphoreType.DMA((n,))],
        compiler_params=_params(("arbitrary",)),
    )(*pieces, w_bf, x, g1, dx2, pi, po, ps)


def _inproj_bwd_w(h, dqs, dkvs, gates):
    tk = 512
    nk = S // tk
    out = None
    for k in range(NCHIP):
        units = range(k * UNITS_PER_CHUNK, (k + 1) * UNITS_PER_CHUNK)
        pieces, piece_specs, pick = _dproj_sources(units, dqs, dkvs, gates, tk)
        handed_on = [] if out is None else [out]

        def body(*refs, units=units, pick=pick, n_pieces=len(pieces), n_in=1 + len(pieces) + len(handed_on)):
            h_ref, piece_refs = refs[0], refs[1:1 + n_pieces]
            o_ref, tile, acc = refs[n_in:]
            i = pl.program_id(0)

            @pl.when(i == 0)
            def _():
                acc[...] = jnp.zeros_like(acc)

            sources = pick(piece_refs)
            for n, u in enumerate(units):
                tile[:, n * LANES:(n + 1) * LANES] = _dproj_unit(u, *sources, tk)
            acc[...] += _dot_tn(h_ref[...], tile[...])

            @pl.when(i == nk - 1)
            def _():
                o_ref[0] = acc[...].astype(BF16)

        out = pl.pallas_call(
            body, grid=(nk,), name=f"inproj_bwd_w{k}",
            in_specs=[_rows(tk, D)] + piece_specs + [ANY] * len(handed_on),
            out_specs=pl.BlockSpec((1, D, CHUNK), lambda i, k=k: (k, 0, 0)),
            out_shape=jax.ShapeDtypeStruct((NCHIP, D, CHUNK), BF16),
            input_output_aliases={1 + len(pieces): 0} if handed_on else {},
            scratch_shapes=[pltpu.VMEM((tk, CHUNK), BF16), pltpu.VMEM((D, CHUNK), F32)],
            compiler_params=_params(("arbitrary",)),
        )(h, *pieces, *handed_on)
    return out


def _local_step(x, target, g1, w_in_bf, conv_w, conv_b, ln_g, ln_b, w_out_bf, gf):
    h, q, k, v, a_gate, c_val, c_glu, c_gate = _inproj_fwd(x, g1, w_in_bf)
    tables = [_bias_table(d) for d in PATTERNS]
    outs, lses = zip(*[_attn_fwd(q, k, v, t, d) for t, d in zip(tables, PATTERNS)])
    o, lse, y_att = _attn_combine(outs, lses, a_gate)
    u, y_conv = _conv_fwd(c_val, c_glu, c_gate, conv_w, conv_b, ln_g, ln_b)
    dx2, dy_att, dy_conv, dw_out, st_out = _outproj_loss(y_att, y_conv, w_out_bf, x, target, gf)

    do, da_gate, delta = _attn_gate_bwd(dy_att, o, a_gate, _head_sum_selectors())
    dqs, dkvs = zip(*[_attn_bwd(q, k, v, do, lse, delta, t, d) for t, d in zip(tables, PATTERNS)])

    du, dc_gate, st_conv = _conv_bwd_rows(u, c_gate, dy_conv, ln_g, ln_b)
    dc_val, dc_glu, dconv_w = _conv_bwd_taps(du, c_val, c_glu, conv_w)

    dproj_pieces = (dqs, dkvs, (da_gate, dc_val, dc_glu, dc_gate))
    dw_in = _inproj_bwd_w(h, *dproj_pieces)
    small = jnp.concatenate([st_conv, st_out, dconv_w], axis=0)
    return dw_in, dw_out, small, dproj_pieces, dx2


ROW_LN_G, ROW_LN_B, ROW_CONV_B, ROW_FINAL_G, ROW_LOSS, ROW_TAPS = 0, 1, 2, 8, 9, 16
SMALL_ROWS = 16 + HALO
NDEV = 8


MESH = pl.DeviceIdType.MESH
ANY = pl.BlockSpec(memory_space=pl.ANY)
CHIP_FLIPS = ((1, 0), (0, 1), (1, 1))


def _pos():
    return lax.axis_index("x"), lax.axis_index("y"), lax.axis_index("c")


def _flip(v, f):
    return 1 - v if f else v


def _ds(start, size, align=None):
    return pl.ds(pl.multiple_of(start, align or size), size)


def _place_shards(wi, wo, cw, where):
    steps = 4

    def body(where_ref, wi_ref, wo_ref, cw_ref, wi_full, wo_full, cw_full):
        wi_full[...] = wi_ref[...].astype(BF16)
        wo_full[...] = wo_ref[...].astype(BF16)
        cw_full[...] = cw_ref[...]

    grid_spec = pltpu.PrefetchScalarGridSpec(
        num_scalar_prefetch=1, grid=(steps,),
        in_specs=[pl.BlockSpec((D // steps, CHUNK), lambda i, w: (i, 0)),
                  pl.BlockSpec((WOUT_SHARD // steps, D), lambda i, w: (i, 0)),
                  pl.BlockSpec((HALO, CONVW_SHARD), lambda i, w: (0, 0))],
        out_specs=[pl.BlockSpec((D // steps, CHUNK), lambda i, w: (i, w[0])),
                   pl.BlockSpec((WOUT_SHARD // steps, D), lambda i, w: (w[0] * steps + i, 0)),
                   pl.BlockSpec((HALO, CONVW_SHARD), lambda i, w: (0, w[0]))])
    return pl.pallas_call(
        body, grid_spec=grid_spec, name="place_shards",
        out_shape=[jax.ShapeDtypeStruct((D, NCOL), BF16), jax.ShapeDtypeStruct((WOUT_ROWS, D), BF16),
                   jax.ShapeDtypeStruct((HALO, D), F32)],
        compiler_params=_params(("arbitrary",)),
    )(where, wi, wo, cw)


def _gather_weights(wi_full, wo_full, cw_full):
    halves = (D // 2, WOUT_SHARD // 2, HALO // 2)
    OWN_X, OWN_Y, VIA_Y, VIA_X = range(4)

    def body(_wi, _wo, _cw, wi_full, wo_full, cw_full, send, recv):
        x, y, c = _pos()
        x_nbr, y_nbr, diag = (1 - x, y), (x, 1 - y), (1 - x, 1 - y)

        def region(a, chip_xy, half, part=None):
            chip = 2 * chip_xy[0] + chip_xy[1]
            n, row = halves[a], half * halves[a]
            if part is not None:
                n = n // 2
                row = row + part * n
            if a == 0:
                return wi_full.at[_ds(row, n), _ds(chip * CHUNK, CHUNK, 128)]
            if a == 1:
                return wo_full.at[_ds(chip * WOUT_SHARD + row, n), :]
            return cw_full.at[_ds(row, n), _ds(chip * CONVW_SHARD, CONVW_SHARD, 128)]

        def copy(a, kind, piece, dev):
            k = 8 * a + kind
            return pltpu.make_async_remote_copy(src_ref=piece, dst_ref=piece, send_sem=send.at[k], recv_sem=recv.at[k],
                                                device_id=dev, device_id_type=MESH)

        def to_sibling(a, kind, piece):
            cp = copy(a, 4 + kind, piece, (x, y, 1 - c))
            cp.start()
            return cp

        sends = []
        for a in range(3):
            for kind, nbr in ((OWN_X, x_nbr), (OWN_Y, y_nbr)):
                cp = copy(a, kind, region(a, (x, y), c), (*nbr, c))
                cp.start()
                sends.append(cp)
        for a in range(3):
            got = region(a, x_nbr, c)
            copy(a, OWN_X, got, (*x_nbr, c)).wait_recv()
            onward = copy(a, VIA_Y, region(a, x_nbr, c, 0), (*y_nbr, c))
            onward.start()
            sends += [onward, to_sibling(a, OWN_X, got)]
            got = region(a, y_nbr, c)
            copy(a, OWN_Y, got, (*y_nbr, c)).wait_recv()
            onward = copy(a, VIA_X, region(a, y_nbr, c, 1), (*x_nbr, c))
            onward.start()
            sends += [onward, to_sibling(a, OWN_Y, got)]
        for a in range(3):
            got = region(a, diag, c, 0)
            copy(a, VIA_Y, got, (*y_nbr, c)).wait_recv()
            sends.append(to_sibling(a, VIA_Y, got))
            got = region(a, diag, c, 1)
            copy(a, VIA_X, got, (*x_nbr, c)).wait_recv()
            sends.append(to_sibling(a, VIA_X, got))
        for a in range(3):
            for kind, piece in ((OWN_X, region(a, x_nbr, 1 - c)), (OWN_Y, region(a, y_nbr, 1 - c)),
                                (VIA_Y, region(a, diag, 1 - c, 0)), (VIA_X, region(a, diag, 1 - c, 1))):
                copy(a, 4 + kind, piece, (x, y, 1 - c)).wait_recv()
        for cp in sends:
            cp.wait_send()

    n_sems = 3 * 8
    return pl.pallas_call(
        body, name="gather_weights",
        in_specs=[ANY, ANY, ANY], out_specs=[ANY, ANY, ANY], input_output_aliases={0: 0, 1: 1, 2: 2},
        out_shape=[jax.ShapeDtypeStruct((D, NCOL), BF16), jax.ShapeDtypeStruct((WOUT_ROWS, D), BF16),
                   jax.ShapeDtypeStruct((HALO, D), F32)],
        scratch_shapes=[pltpu.SemaphoreType.DMA((n_sems,)), pltpu.SemaphoreType.DMA((n_sems,))],
    )(wi_full, wo_full, cw_full)


def _exchange_halves(gi4, go4, small):
    def body(gi_ref, go_ref, sm_ref, ri_ref, ro_ref, rs_ref, send, recv):
        x, y, c = _pos()
        sib = (x, y, 1 - c)
        copies = [
            (gi_ref.at[:, _ds((1 - c) * (D // 2), D // 2), :], ri_ref),
            (go_ref.at[:, _ds((1 - c) * (WOUT_SHARD // 2), WOUT_SHARD // 2), :], ro_ref),
            (sm_ref, rs_ref),
        ]
        cps = [pltpu.make_async_remote_copy(src_ref=s_, dst_ref=d_, send_sem=send.at[k], recv_sem=recv.at[k],
                                            device_id=sib, device_id_type=MESH) for k, (s_, d_) in enumerate(copies)]
        for cp in cps:
            cp.start()
        for cp in cps:
            cp.wait()

    return pl.pallas_call(
        body, name="exchange_halves",
        in_specs=[ANY, ANY, ANY], out_specs=[ANY, ANY, ANY],
        out_shape=[jax.ShapeDtypeStruct((NCHIP, D // 2, CHUNK), BF16),
                   jax.ShapeDtypeStruct((NCHIP, WOUT_SHARD // 2, D), BF16),
                   jax.ShapeDtypeStruct((SMALL_ROWS, D), F32)],
        scratch_shapes=[pltpu.SemaphoreType.DMA((3,)), pltpu.SemaphoreType.DMA((3,))],
    )(gi4, go4, small)


def _add_halves(gi4, ri, go4, ro, small, rs):
    hi, ho = D // 2, WOUT_SHARD // 2

    def body(gi_ref, ri_ref, go_ref, ro_ref, sm_ref, rs_ref, pi_ref, po_ref, ps_ref):
        c = lax.axis_index("c")
        pi_ref[0] = (gi_ref[0, _ds(c * hi, hi), :].astype(F32) + ri_ref[0].astype(F32)).astype(BF16)
        po_ref[0] = (go_ref[0, _ds(c * ho, ho), :].astype(F32) + ro_ref[0].astype(F32)).astype(BF16)
        ps_ref[...] = sm_ref[...] + rs_ref[...]

    blk = lambda n, w: pl.BlockSpec((1, n, w), lambda k: (k, 0, 0))
    whole = pl.BlockSpec((SMALL_ROWS, D), lambda k: (0, 0))
    return pl.pallas_call(
        body, grid=(NCHIP,), name="add_halves",
        in_specs=[blk(D, CHUNK), blk(hi, CHUNK), blk(WOUT_SHARD, D), blk(ho, D), whole, whole],
        out_specs=[blk(hi, CHUNK), blk(ho, D), whole],
        out_shape=[jax.ShapeDtypeStruct((NCHIP, hi, CHUNK), BF16), jax.ShapeDtypeStruct((NCHIP, ho, D), BF16),
                   jax.ShapeDtypeStruct((SMALL_ROWS, D), F32)],
        compiler_params=_params(("arbitrary",)),
    )(gi4, ri, go4, ro, small, rs)


def _chip_exchange_copies(srcs, dsts, send, recv):
    x, y, c = _pos()
    me = 2 * x + y
    pairs = []
    for a in range(3):
        for j, (fx, fy) in enumerate(CHIP_FLIPS):
            px, py = _flip(x, fx), _flip(y, fy)
            peer = 2 * px + py
            k = 3 * a + j
            out = pltpu.make_async_remote_copy(
                src_ref=srcs[a] if a == 2 else srcs[a].at[peer], dst_ref=dsts[a].at[me],
                send_sem=send.at[k], recv_sem=recv.at[k], device_id=(px, py, c), device_id_type=MESH)
            got = dsts[a].at[peer]
            arrival = pltpu.make_async_remote_copy(
                src_ref=got, dst_ref=got, send_sem=send.at[k], recv_sem=recv.at[k],
                device_id=(px, py, c), device_id_type=MESH)
            pairs.append((out, arrival))
    return pairs


def _sum_chips(ri, ro, rs, pi, po, ps, where):
    def body(w_ref, ri_ref, ro_ref, rs_ref, pi_ref, po_ref, ps_ref, gi_ref, go_ref, gs_ref, g5_ref, loss_ref,
             acc_i, acc_o, acc_s):
        k = pl.program_id(0)
        accs = (acc_i, acc_o, acc_s)

        @pl.when(k == 0)
        def _():
            for acc in accs:
                acc[...] = jnp.zeros_like(acc)

        @pl.when(k == w_ref[0])
        def _():
            for acc, val in zip(accs, (pi_ref[0], po_ref[0], ps_ref[...])):
                acc[...] += val.astype(F32)

        @pl.when(k != w_ref[0])
        def _():
            for acc, ref in zip(accs, (ri_ref, ro_ref, rs_ref)):
                acc[...] += ref[0].astype(F32)

        @pl.when(k == NCHIP - 1)
        def _():
            gi_ref[0] = acc_i[...]
            go_ref[0] = acc_o[...]
            gs_ref[...] = acc_s[...]
            g5_ref[...] = jnp.zeros_like(g5_ref)
            for i, row in enumerate((ROW_CONV_B, ROW_LN_G, ROW_LN_B, ROW_FINAL_G)):
                g5_ref[i + 1:i + 2, :] = acc_s[row:row + 1, :]
            loss = jnp.sum(acc_s[ROW_LOSS:ROW_LOSS + 1, :], axis=1, keepdims=True)
            loss_ref[...] = jnp.broadcast_to(loss, loss_ref.shape)

    def sent(k, w):
        return jnp.where(k == w[0], (k + 1) % NCHIP, k)

    hi, ho = D // 2, WOUT_SHARD // 2
    const = lambda shape: pl.BlockSpec(shape, lambda k, w: (0,) * len(shape))
    grid_spec = pltpu.PrefetchScalarGridSpec(
        num_scalar_prefetch=1, grid=(NCHIP,),
        in_specs=[pl.BlockSpec((1, hi, CHUNK), lambda k, w: (sent(k, w), 0, 0)),
                  pl.BlockSpec((1, ho, D), lambda k, w: (sent(k, w), 0, 0)),
                  pl.BlockSpec((1, SMALL_ROWS, D), lambda k, w: (sent(k, w), 0, 0)),
                  pl.BlockSpec((1, hi, CHUNK), lambda k, w: (w[0], 0, 0)),
                  pl.BlockSpec((1, ho, D), lambda k, w: (w[0], 0, 0)),
                  const((SMALL_ROWS, D))],
        out_specs=[pl.BlockSpec((1, hi, CHUNK), lambda k, w: (w[1], 0, 0)),
                   pl.BlockSpec((1, ho, D), lambda k, w: (w[1], 0, 0)),
                   const((SMALL_ROWS, D)), const((8, D)), const((8, LANES))],
        scratch_shapes=[pltpu.VMEM((hi, CHUNK), F32), pltpu.VMEM((ho, D), F32), pltpu.VMEM((SMALL_ROWS, D), F32)])
    return pl.pallas_call(
        body, grid_spec=grid_spec, name="sum_chips",
        out_shape=[jax.ShapeDtypeStruct((2, hi, CHUNK), F32), jax.ShapeDtypeStruct((2, ho, D), F32),
                   jax.ShapeDtypeStruct((SMALL_ROWS, D), F32), jax.ShapeDtypeStruct((8, D), F32),
                   jax.ShapeDtypeStruct((8, LANES), F32)],
        compiler_params=_params(("arbitrary",)),
    )(where, ri, ro, rs, pi, po, ps)


def _exchange_results(gi2, go2, st):
    flips = [(fx, fy, fc) for fx in (0, 1) for fy in (0, 1) for fc in (0, 1)][1:]

    def body(_gi, _go, st_ref, gi_ref, go_ref, all_ref, send, recv, lsem):
        x, y, c = _pos()
        sib = (x, y, 1 - c)

        def half(k, ref, slot):
            return pltpu.make_async_remote_copy(src_ref=ref.at[slot], dst_ref=ref.at[slot], send_sem=send.at[k],
                                                recv_sem=recv.at[k], device_id=sib, device_id_type=MESH)

        def stat(k, src, slot, dev):
            return pltpu.make_async_remote_copy(src_ref=src, dst_ref=all_ref.at[slot], send_sem=send.at[k],
                                                recv_sem=recv.at[k], device_id=dev, device_id_type=MESH)

        mine = pltpu.make_async_copy(st_ref, all_ref.at[4 * x + 2 * y + c], lsem)
        mine.start()
        sends = [half(k, ref, c) for k, ref in enumerate((gi_ref, go_ref))]
        peers = [(_flip(x, fx), _flip(y, fy), _flip(c, fc)) for fx, fy, fc in flips]
        sends += [stat(2 + k, st_ref, 4 * x + 2 * y + c, dev) for k, dev in enumerate(peers)]
        for cp in sends:
            cp.start()
        for k, ref in enumerate((gi_ref, go_ref)):
            half(k, ref, 1 - c).wait_recv()
        for k, (px, py, pc) in enumerate(peers):
            slot = 4 * px + 2 * py + pc
            stat(2 + k, all_ref.at[slot], slot, (px, py, pc)).wait_recv()
        for cp in sends:
            cp.wait_send()
        mine.wait()

    n = 2 + len(flips)
    return pl.pallas_call(
        body, name="exchange_results",
        in_specs=[ANY, ANY, ANY], out_specs=[ANY, ANY, ANY], input_output_aliases={0: 0, 1: 1},
        out_shape=[jax.ShapeDtypeStruct((2, D // 2, CHUNK), F32), jax.ShapeDtypeStruct((2, WOUT_SHARD // 2, D), F32),
                   jax.ShapeDtypeStruct((NDEV, 8, D), F32)],
        scratch_shapes=[pltpu.SemaphoreType.DMA((n,)), pltpu.SemaphoreType.DMA((n,)), pltpu.SemaphoreType.DMA],
    )(gi2, go2, st)


def _adamw_math(w, g, m, v):
    m2 = ADAM_B1 * m + (1.0 - ADAM_B1) * g
    v2 = ADAM_B2 * v + (1.0 - ADAM_B2) * (g * g)
    m_hat = m2 / (1.0 - ADAM_B1 ** ADAM_STEP)
    v_hat = v2 / (1.0 - ADAM_B2 ** ADAM_STEP)
    delta = -ADAM_LR * (m_hat / (jnp.sqrt(v_hat) + ADAM_EPS) + ADAM_WD * w)
    return delta, m2, v2


def _adamw(w, g, m, v, name):
    rows, cols = w.shape
    tm = 256 if rows % 256 == 0 else rows

    def body(w_ref, g_ref, m_ref, v_ref, d_ref, m2_ref, v2_ref):
        d_ref[...], m2_ref[...], v2_ref[...] = _adamw_math(w_ref[...], g_ref[...], m_ref[...], v_ref[...])

    shape = jax.ShapeDtypeStruct(w.shape, F32)
    return pl.pallas_call(
        body, grid=(rows // tm,), name=name,
        in_specs=[_rows(tm, cols)] * 4, out_specs=[_rows(tm, cols)] * 3, out_shape=[shape] * 3,
        compiler_params=_params(("arbitrary",)),
    )(w, g, m, v)


def _adamw_vectors(g5, first_parts, ws, ms, vs):
    n = len(ws)

    def body(g_ref, parts_ref, *refs):
        ins, g0_ref, outs = refs[:3 * n], refs[3 * n], refs[3 * n + 1:]
        g0 = parts_ref[0, 0:1, :]
        for dev in range(1, NDEV):
            g0 = g0 + parts_ref[dev, 0:1, :]
        g0_ref[...] = g0
        for i in range(n):
            g = g0 if i == 0 else g_ref[i:i + 1, :]
            res = _adamw_math(ins[i][...], g, ins[n + i][...], ins[2 * n + i][...])
            for kind in range(3):
                outs[kind * n + i][...] = res[kind]

    shape = jax.ShapeDtypeStruct((1, D), F32)
    return pl.pallas_call(body, name="adamw_vectors", out_shape=[shape] * (1 + 3 * n), compiler_params=_params())(
        g5, first_parts, *ws, *ms, *vs)


def kernel(x, norm_g, w_in, conv_w, conv_b, conv_ln_g, conv_ln_b, w_out, final_norm_g, loss_target, m_norm_g, m_w_in, m_conv_w, m_conv_b, m_conv_ln_g, m_conv_ln_b, m_w_out, m_final_norm_g, v_norm_g, v_w_in, v_conv_w, v_conv_b, v_conv_ln_g, v_conv_ln_b, v_w_out, v_final_norm_g):
    chip = 2 * lax.axis_index("x") + lax.axis_index("y")
    where = jnp.stack([chip, lax.axis_index("c")]).astype(jnp.int32)
    taps_shard = jnp.pad(conv_w[0], ((0, HALO - CONV_K), (0, 0)))
    wi_full, wo_full, cw_full = _gather_weights(*_place_shards(w_in[0], w_out[0], taps_shard, where))

    gf = final_norm_g[None]
    dw_in4, dw_out, small, dproj_pieces, dx2 = _local_step(
        x[0], loss_target[0], norm_g, wi_full, cw_full, conv_b, conv_ln_g, conv_ln_b, wo_full, gf)
    dw_out4 = dw_out.reshape(NCHIP, WOUT_SHARD, D)

    ri, ro, rs = _exchange_halves(dw_in4, dw_out4, small)
    pi, po, ps = _add_halves(dw_in4, ri, dw_out4, ro, small, rs)
    grad_x, st_in, ri, ro, rs = _inproj_bwd_x(*dproj_pieces, wi_full, x[0], norm_g, dx2, pi, po, ps)
    gi2, go2, g_small, g5, loss8 = _sum_chips(ri, ro, rs, pi, po, ps, where)
    gi2, go2, norm_g_parts = _exchange_results(gi2, go2, st_in)
    g_w_in = gi2.reshape(D, CHUNK)
    g_w_out = go2.reshape(WOUT_SHARD, D)
    g_taps = lax.dynamic_slice(g_small, (ROW_TAPS, chip * CONVW_SHARD), (CONV_K, CONVW_SHARD))

    d_w_in, m2_w_in, v2_w_in = _adamw(w_in[0], g_w_in, m_w_in[0], v_w_in[0], "adamw_w_in")
    d_w_out, m2_w_out, v2_w_out = _adamw(w_out[0], g_w_out, m_w_out[0], v_w_out[0], "adamw_w_out")
    d_taps, m2_taps, v2_taps = _adamw(conv_w[0], g_taps, m_conv_w[0], v_conv_w[0], "adamw_conv_w")
    g_norm, *vec = _adamw_vectors(
        g5, norm_g_parts,
        (norm_g, conv_b, conv_ln_g, conv_ln_b, gf),
        (m_norm_g, m_conv_b, m_conv_ln_g, m_conv_ln_b, m_final_norm_g[None]),
        (v_norm_g, v_conv_b, v_conv_ln_g, v_conv_ln_b, v_final_norm_g[None]))
    d_vec, m2_vec, v2_vec = vec[0:5], vec[5:10], vec[10:15]

    def weight_order(ng, wi, cw, cb, lg, lb, wo, fg):
        return (ng, wi[None], cw[None], cb, lg, lb, wo[None], fg[0])

    grads = weight_order(g_norm, g_w_in, g_taps, g5[1:2], g5[2:3], g5[3:4], g_w_out, g5[4:5])
    deltas = weight_order(d_vec[0], d_w_in, d_taps, d_vec[1], d_vec[2], d_vec[3], d_w_out, d_vec[4])
    new_m = weight_order(m2_vec[0], m2_w_in, m2_taps, m2_vec[1], m2_vec[2], m2_vec[3], m2_w_out, m2_vec[4])
    new_v = weight_order(v2_vec[0], v2_w_in, v2_taps, v2_vec[1], v2_vec[2], v2_vec[3], v2_w_out, v2_vec[4])
    return (loss8[0, 0], grad_x[None], *grads, *deltas, *new_m, *new_v)
```

```python
import jax
import jax.numpy as jnp
from jax import lax
from jax.experimental import pallas as pl
from jax.experimental.pallas import tpu as pltpu

F32 = jnp.float32
BF16 = jnp.bfloat16

S = 4096
D = 1024
LANES = 128
HD = 64
NKV = 4
GQ = 4
KVW = NKV * HD
NCOL = 5632
CONV_K = 31
HALO = 32
BLK = 128
PATTERNS = (1, 4, 16)
NORM_EPS = 1e-6
LN_EPS = 1e-5
NEG = -1e30
OFF_Q, OFF_K, OFF_V, OFF_AG, OFF_CV, OFF_CG, OFF_CGATE = 0, 1024, 1280, 1536, 2560, 3584, 4608
NCHIP = 4
CHUNK = NCOL // NCHIP
WOUT_ROWS = 2 * D
WOUT_SHARD = WOUT_ROWS // NCHIP
CONVW_SHARD = D // NCHIP

ADAM_LR, ADAM_B1, ADAM_B2, ADAM_EPS, ADAM_WD, ADAM_STEP = 0.001, 0.9, 0.999, 1e-08, 0.01, 10

VMEM_LIMIT = 56 * 1024 * 1024


def _params(sem=None, vmem=VMEM_LIMIT):
    return pltpu.CompilerParams(dimension_semantics=sem, vmem_limit_bytes=vmem)


def _sigmoid(a):
    return 0.5 * jnp.tanh(0.5 * a) + 0.5


def _rows(tm, width):
    return pl.BlockSpec((tm, width), lambda i: (i, 0))


def _slabs(n):
    return jax.ShapeDtypeStruct((n, S, LANES), F32)


def _slab_rows(n, tm):
    return pl.BlockSpec((n, tm, LANES), lambda i: (0, i, 0))


def _resident(shape):
    return pl.BlockSpec(shape, lambda *_: (0,) * len(shape), pipeline_mode=pl.Buffered(1))


def _dot(a, b):
    return jnp.dot(a, b, preferred_element_type=F32)


def _dot_nt(a, b):
    return lax.dot_general(a, b, (((1,), (1,)), ((), ())), preferred_element_type=F32)


def _dot_tn(a, b):
    return lax.dot_general(a, b, (((0,), (0,)), ((), ())), preferred_element_type=F32)


def _inproj_fwd(x, g1, w_bf):
    tm = 512

    def body(x_ref, g_ref, w_ref, h_ref, q_ref, k_ref, v_ref, ag_ref, cv_ref, cg_ref, cgate_ref):
        xt = x_ref[...]
        r = lax.rsqrt(jnp.mean(xt * xt, axis=-1, keepdims=True) + NORM_EPS)
        h = (xt * r * g_ref[...]).astype(BF16)
        h_ref[...] = h
        q = _dot(h, w_ref[:, OFF_Q:OFF_Q + D]) * (HD ** -0.5)
        kv = _dot(h, w_ref[:, OFF_K:OFF_K + 2 * KVW])
        for sl in range(D // LANES):
            q_ref[sl] = q[:, sl * LANES:(sl + 1) * LANES]
        for sl in range(KVW // LANES):
            k_ref[sl] = kv[:, sl * LANES:(sl + 1) * LANES]
            v_ref[sl] = kv[:, KVW + sl * LANES:KVW + (sl + 1) * LANES]
        ag_ref[...] = _dot(h, w_ref[:, OFF_AG:OFF_AG + D])
        cv_ref[...] = _dot(h, w_ref[:, OFF_CV:OFF_CV + D])
        cg_ref[...] = _dot(h, w_ref[:, OFF_CG:OFF_CG + D])
        cgate_ref[...] = _dot(h, w_ref[:, OFF_CGATE:OFF_CGATE + D])

    big = jax.ShapeDtypeStruct((S, D), F32)
    return pl.pallas_call(
        body, grid=(S // tm,), name="inproj_fwd",
        in_specs=[_rows(tm, D), _resident((1, D)), _resident((D, NCOL))],
        out_specs=[_rows(tm, D), _slab_rows(D // LANES, tm), _slab_rows(KVW // LANES, tm), _slab_rows(KVW // LANES, tm),
                   _rows(tm, D), _rows(tm, D), _rows(tm, D), _rows(tm, D)],
        out_shape=[jax.ShapeDtypeStruct((S, D), BF16), _slabs(D // LANES), _slabs(KVW // LANES), _slabs(KVW // LANES),
                   big, big, big, big],
        compiler_params=_params(("arbitrary",)),
    )(x, g1, w_bf)


def _bias_table(d):
    h = jnp.arange(NKV * GQ, dtype=F32)
    slopes = jnp.exp2(-8.0 * (h + 1.0) / (NKV * GQ))
    qi = jnp.arange(BLK)[:, None]
    kj = jnp.arange(2 * BLK)[None, :]
    dist = BLK + qi - kj
    window = (dist >= 0) & (dist <= BLK)
    bias = -slopes[:, None, None] * (dist * d).astype(F32)[None]
    has_prev = jnp.stack([jnp.broadcast_to(kj >= BLK, (BLK, 2 * BLK)), jnp.ones((BLK, 2 * BLK), bool)])
    valid = window[None] & has_prev
    tab = jnp.where(valid[:, None], bias[None], NEG)
    return tab.reshape(2, NKV, GQ * BLK, 2 * BLK)


def _sub_rows(start, d):
    if d == 1:
        return pl.ds(pl.multiple_of(start, BLK), BLK)
    return pl.ds(start, BLK, stride=d)


NHEAD = NKV * GQ
CHUNK_ROWS = 2048
BLOCKS_PER_CHUNK = CHUNK_ROWS // BLK


def _low_lanes(rows=BLK):
    return lax.broadcasted_iota(jnp.int32, (rows, LANES), 1) < HD


def _block_start(idx, d):
    shift = d.bit_length() - 1
    b, r = lax.shift_right_logical(idx, shift), lax.bitwise_and(idx, d - 1)
    start = b * (BLK * d) + r
    return b, start, jnp.maximum(start - BLK * d, r)


def _stack_heads(ref, rows):
    low = _low_lanes()
    t0, t1 = ref[0, rows, :], ref[1, rows, :]
    return jnp.concatenate([jnp.where(low, t0, 0.0), jnp.where(low, 0.0, t0),
                            jnp.where(low, t1, 0.0), jnp.where(low, 0.0, t1)], axis=0).astype(BF16)


def _unstack_heads(dup):
    low = _low_lanes()
    return (jnp.where(low, dup[0:BLK], dup[BLK:2 * BLK]), jnp.where(low, dup[2 * BLK:3 * BLK], dup[3 * BLK:4 * BLK]))


def _kv_dup(ref, prow, rows, odd):
    t = jnp.concatenate([ref[0, prow, :], ref[0, rows, :]], axis=0)
    swapped = pltpu.roll(t, HD, axis=1)
    keep = jnp.logical_xor(_low_lanes(2 * BLK), odd)
    return jnp.where(keep, t, swapped).astype(BF16)


def _attn_fwd(q, k, v, bias, d):
    def body(q_ref, k_ref, v_ref, b_ref, o_ref, l_ref):
        odd = pl.program_id(0) % 2 == 1
        ones = jnp.ones((2 * BLK, LANES), BF16)

        def block(idx, carry):
            b, start, pstart = _block_start(idx, d)
            rows, prow = _sub_rows(start, d), _sub_rows(pstart, d)
            qs = _stack_heads(q_ref, rows)
            kw = _kv_dup(k_ref, prow, rows, odd)
            vw = _kv_dup(v_ref, prow, rows, odd)
            s = _dot_nt(qs, kw) + b_ref[jnp.minimum(b, 1), 0]
            m = jnp.max(s, axis=1, keepdims=True)
            p = jnp.exp(s - m).astype(BF16)
            ol = _dot(p, jnp.concatenate([vw, ones], axis=1))
            l = ol[:, LANES:]
            o_ref[0, rows, :], o_ref[1, rows, :] = _unstack_heads(ol[:, :LANES] / l)
            l_ref[0, rows, :] = _by_head([(m + jnp.log(l))[g * BLK:(g + 1) * BLK] for g in range(GQ)])
            return carry

        lax.fori_loop(0, S // BLK, block, 0, unroll=2)

    q_like = pl.BlockSpec((2, S, LANES), lambda j: (j, 0, 0))
    kv = pl.BlockSpec((1, S, LANES), lambda j: (j // 2, 0, 0))
    per_kv = pl.BlockSpec((1, S, LANES), lambda j: (j, 0, 0))
    bias_spec = pl.BlockSpec((2, 1, GQ * BLK, 2 * BLK), lambda j: (0, j, 0, 0))
    return pl.pallas_call(
        body, grid=(NKV,), name=f"attn_fwd_d{d}",
        in_specs=[q_like, kv, kv, bias_spec],
        out_specs=[q_like, per_kv],
        out_shape=[_slabs(D // LANES), _slabs(NKV)],
        compiler_params=_params(("arbitrary",)),
    )(q, k, v, bias)


PIECES = 3


def _by_head(tiles):
    lane = lax.broadcasted_iota(jnp.int32, tiles[0].shape, 1)
    out = tiles[0]
    for g in range(1, GQ):
        out = jnp.where(lax.bitwise_and(lane, GQ - 1) == g, tiles[g], out)
    return out


def _minus_in_pieces(x):
    lane = lax.broadcasted_iota(jnp.int32, x.shape, 1)
    hi = (-x).astype(BF16).astype(F32)
    rest = -x - hi
    mid = rest.astype(BF16).astype(F32)
    lo = (rest - mid).astype(BF16).astype(F32)
    return jnp.where(lane < GQ, hi, jnp.where(lane < 2 * GQ, mid, jnp.where(lane < PIECES * GQ, lo, 0.0)))


def _attn_combine(outs, lses, a_gate):
    tm = 256

    def body(o1, o2, o3, l1, l2, l3, ag_ref, o_ref, lse_ref, y_ref):
        low = _low_lanes(tm)
        for j in range(NKV):
            a, b, c = l1[j], l2[j], l3[j]
            m = jnp.maximum(jnp.maximum(a, b), c)
            ea, eb, ec = jnp.exp(a - m), jnp.exp(b - m), jnp.exp(c - m)
            den = ea + eb + ec
            lse_ref[j] = _minus_in_pieces(m + jnp.log(den))
            inv = 1.0 / den
            for half in range(2):
                sl = 2 * j + half

                def spread(w):
                    return jnp.where(low, w[:, 2 * half:2 * half + 1], w[:, 2 * half + 1:2 * half + 2])

                o = spread(ea * inv) * o1[sl] + spread(eb * inv) * o2[sl] + spread(ec * inv) * o3[sl]
                o_ref[sl] = o
                cols = slice(sl * LANES, (sl + 1) * LANES)
                ag = ag_ref[:, cols]
                y_ref[:, cols] = (o * (ag * _sigmoid(ag))).astype(BF16)

    wide, per_kv = _slab_rows(D // LANES, tm), _slab_rows(NKV, tm)
    return pl.pallas_call(
        body, grid=(S // tm,), name="attn_combine",
        in_specs=[wide] * 3 + [per_kv] * 3 + [_rows(tm, D)],
        out_specs=[wide, per_kv, _rows(tm, D)],
        out_shape=[_slabs(D // LANES), _slabs(NKV), jax.ShapeDtypeStruct((S, D), BF16)],
        compiler_params=_params(("arbitrary",)),
    )(*outs, *lses, a_gate)


def _head_sum_selectors():
    lane_in = jnp.arange(LANES)[:, None] // HD
    return jnp.stack([jnp.broadcast_to(lane_in == h, (LANES, LANES)) for h in range(2)]).astype(BF16)


def _attn_gate_bwd(dy_att, o, a_gate, selectors):
    tm = 256

    def body(dy_ref, o_ref, ag_ref, e_ref, do_ref, dag_ref, delta_ref):
        for j in range(NKV):
            deltas = []
            for sl in (2 * j, 2 * j + 1):
                cols = slice(sl * LANES, (sl + 1) * LANES)
                dy, ag, o_ = dy_ref[:, cols], ag_ref[:, cols], o_ref[sl]
                sg = _sigmoid(ag)
                do = dy * (ag * sg)
                do_ref[sl] = do
                dag_ref[:, cols] = (dy * o_ * (sg * (1.0 + ag * (1.0 - sg)))).astype(BF16)
                prod = do * o_
                hi = prod.astype(BF16)
                lo = (prod - hi.astype(F32)).astype(BF16)
                deltas += [_dot(hi, e_ref[h]) + _dot(lo, e_ref[h]) for h in range(2)]
            delta_ref[j] = _minus_in_pieces(_by_head(deltas))

    return pl.pallas_call(
        body, grid=(S // tm,), name="attn_gate_bwd",
        in_specs=[_rows(tm, D), _slab_rows(D // LANES, tm), _rows(tm, D), _resident((2, LANES, LANES))],
        out_specs=[_slab_rows(D // LANES, tm), _rows(tm, D), _slab_rows(NKV, tm)],
        out_shape=[_slabs(D // LANES), jax.ShapeDtypeStruct((S, D), BF16), _slabs(NKV)],
        compiler_params=_params(("arbitrary",)),
    )(dy_att, o, a_gate, selectors)


def _own_pieces(tile):
    lane = lax.broadcasted_iota(jnp.int32, tile.shape, 1)
    head = jnp.where(lane < PIECES * GQ, lax.bitwise_and(lane, GQ - 1), -1)
    return jnp.concatenate([jnp.where(head == g, tile, 0.0) for g in range(GQ)], axis=0).astype(BF16)


def _attn_bwd(q, k, v, do, lse, delta, bias, d):
    def body(q_ref, do_ref, l_ref, dl_ref, k_ref, v_ref, b_ref, dq_ref, dkv_ref):
        odd = pl.program_id(0) % 2 == 1
        chunk = pl.program_id(1)
        ones = (lax.broadcasted_iota(jnp.int32, (2 * BLK, LANES), 1) < PIECES * GQ).astype(BF16)

        @pl.when(chunk == 0)
        def _():
            dkv_ref[...] = jnp.zeros_like(dkv_ref)

        def block(idx, carry):
            b, start, pstart = _block_start(chunk * BLOCKS_PER_CHUNK + idx, d)
            rows, prow = _sub_rows(start, d), _sub_rows(pstart, d)
            mine = _sub_rows(start - chunk * CHUNK_ROWS, d)
            qs = _stack_heads(q_ref, mine)
            dos = _stack_heads(do_ref, mine)
            kw = _kv_dup(k_ref, prow, rows, odd)
            vw = _kv_dup(v_ref, prow, rows, odd)
            s = _dot_nt(jnp.concatenate([qs, _own_pieces(l_ref[0, mine, :])], axis=1),
                        jnp.concatenate([kw, ones], axis=1)) + b_ref[jnp.minimum(b, 1), 0]
            p = jnp.exp(s)
            dv2 = _dot_tn(p.astype(BF16), dos)
            dp = _dot_nt(jnp.concatenate([dos, _own_pieces(dl_ref[0, mine, :])], axis=1),
                         jnp.concatenate([vw, ones], axis=1))
            ds = (p * dp).astype(BF16)
            dq_ref[0, mine, :], dq_ref[1, mine, :] = _unstack_heads(_dot(ds, kw))
            dk2 = _dot_tn(ds, qs)
            dkv = jnp.where(_low_lanes(2 * BLK), dk2 + pltpu.roll(dk2, HD, axis=1), dv2 + pltpu.roll(dv2, HD, axis=1))
            dkv_ref[0, rows, :] = dkv_ref[0, rows, :] + dkv[BLK:]
            dkv_ref[0, prow, :] = dkv_ref[0, prow, :] + dkv[:BLK]
            return carry

        lax.fori_loop(0, BLOCKS_PER_CHUNK, block, 0, unroll=8)

    q_like = pl.BlockSpec((2, CHUNK_ROWS, LANES), lambda j, c: (j, c, 0))
    pieces = pl.BlockSpec((1, CHUNK_ROWS, LANES), lambda j, c: (j, c, 0))
    kv = pl.BlockSpec((1, S, LANES), lambda j, c: (j // 2, 0, 0))
    per_kv = pl.BlockSpec((1, S, LANES), lambda j, c: (j, 0, 0))
    bias_spec = pl.BlockSpec((2, 1, GQ * BLK, 2 * BLK), lambda j, c: (0, j, 0, 0))
    return pl.pallas_call(
        body, grid=(NKV, S // CHUNK_ROWS), name=f"attn_bwd_d{d}",
        in_specs=[q_like, q_like, pieces, pieces, kv, kv, bias_spec],
        out_specs=[q_like, per_kv],
        out_shape=[_slabs(D // LANES), _slabs(NKV)],
        compiler_params=_params(("arbitrary", "arbitrary")),
    )(q, do, lse, delta, k, v, bias)


CONV_T = 256


def _halo_before(i):
    return (jnp.maximum(i * (CONV_T // HALO) - 1, 0), 0)


def _halo_after(i):
    return (jnp.minimum((i + 1) * (CONV_T // HALO), S // HALO - 1), 0)


SUBLANES = 8
NCH = D // LANES
GROUP = SUBLANES * SUBLANES


def _comb(ref, cb, base):
    return ref[cb, pl.ds(base, SUBLANES, stride=SUBLANES), :]


def _taps(w_ref, cols):
    return [jnp.broadcast_to(w_ref[j:j + 1, cols], (SUBLANES, LANES)) for j in range(CONV_K)]


def _conv_fwd(c_val, c_glu, c_gate, conv_w, conv_b, ln_g, ln_b):
    T = CONV_T

    def body(cv_ref, cg_ref, cvh_ref, cgh_ref, gate_ref, w_ref, b_ref, lg_ref, lb_ref, u_ref, y_ref, win, us):
        i = pl.program_id(0)
        for cb in range(NCH):
            cols = slice(cb * LANES, (cb + 1) * LANES)
            win[cb, HALO:HALO + T, :] = cv_ref[:, cols] * _sigmoid(cg_ref[:, cols])
            win[cb, 0:HALO, :] = jnp.where(i > 0, cvh_ref[:, cols] * _sigmoid(cgh_ref[:, cols]), 0.0)
        for cb in range(NCH):
            cols = slice(cb * LANES, (cb + 1) * LANES)
            taps = _taps(w_ref, cols)
            bias = jnp.broadcast_to(b_ref[:, cols], (SUBLANES, LANES))

            def group(g, carry):
                for b in range(SUBLANES):
                    base = g * GROUP + b
                    acc = bias
                    for j in range(CONV_K):
                        acc = acc + taps[j] * _comb(win, cb, base + (HALO - (CONV_K - 1) + j))
                    us[cb, pl.ds(base, SUBLANES, stride=SUBLANES), :] = acc
                return carry

            lax.fori_loop(0, T // GROUP, group, 0)
        total = us[0]
        for cb in range(1, NCH):
            total = total + us[cb]
        mu = jnp.sum(total, axis=-1, keepdims=True) * (1.0 / D)
        sq = jnp.zeros((T, LANES), F32)
        for cb in range(NCH):
            uc = us[cb] - mu
            sq = sq + uc * uc
        rstd = lax.rsqrt(jnp.sum(sq, axis=-1, keepdims=True) * (1.0 / D) + LN_EPS)
        for cb in range(NCH):
            cols = slice(cb * LANES, (cb + 1) * LANES)
            u = us[cb]
            u_ref[:, cols] = u
            nrm = (u - mu) * rstd * lg_ref[:, cols] + lb_ref[:, cols]
            gate = gate_ref[:, cols]
            y_ref[:, cols] = (nrm * _sigmoid(nrm) * (gate * _sigmoid(gate))).astype(BF16)

    halo = pl.BlockSpec((HALO, D), _halo_before)
    return pl.pallas_call(
        body, grid=(S // T,), name="conv_fwd",
        in_specs=[_rows(T, D), _rows(T, D), halo, halo, _rows(T, D),
                  _resident((HALO, D)), _resident((1, D)), _resident((1, D)), _resident((1, D))],
        out_specs=[_rows(T, D), _rows(T, D)],
        out_shape=[jax.ShapeDtypeStruct((S, D), F32), jax.ShapeDtypeStruct((S, D), BF16)],
        scratch_shapes=[pltpu.VMEM((NCH, T + HALO, LANES), F32), pltpu.VMEM((NCH, T, LANES), F32)],
        compiler_params=_params(("arbitrary",)),
    )(c_val, c_glu, c_val, c_glu, c_gate, conv_w, conv_b, ln_g, ln_b)


def _conv_bwd_rows(u, c_gate, dy_conv, ln_g, ln_b):
    tm = 256

    def body(u_ref, gate_ref, dy_ref, lg_ref, lb_ref, du_ref, dgate_ref, st_ref):
        @pl.when(pl.program_id(0) == 0)
        def _():
            st_ref[...] = jnp.zeros_like(st_ref)

        u, gate, dy = u_ref[...], gate_ref[...], dy_ref[...]
        mu = jnp.mean(u, axis=-1, keepdims=True)
        uc = u - mu
        rstd = lax.rsqrt(jnp.mean(uc * uc, axis=-1, keepdims=True) + LN_EPS)
        z = uc * rstd
        nrm = z * lg_ref[...] + lb_ref[...]
        sn, sg = _sigmoid(nrm), _sigmoid(gate)
        dgate_ref[...] = (dy * (nrm * sn) * (sg * (1.0 + gate * (1.0 - sg)))).astype(BF16)
        dn = dy * (gate * sg) * (sn * (1.0 + nrm * (1.0 - sn)))
        dz = dn * lg_ref[...]
        du = rstd * (dz - jnp.mean(dz, axis=-1, keepdims=True) - z * jnp.mean(dz * z, axis=-1, keepdims=True))
        du_ref[...] = du
        st_ref[0:1, :] += jnp.sum(dn * z, axis=0, keepdims=True)
        st_ref[1:2, :] += jnp.sum(dn, axis=0, keepdims=True)
        st_ref[2:3, :] += jnp.sum(du, axis=0, keepdims=True)

    big = jax.ShapeDtypeStruct((S, D), F32)
    return pl.pallas_call(
        body, grid=(S // tm,), name="conv_bwd_rows",
        in_specs=[_rows(tm, D)] * 3 + [_resident((1, D)), _resident((1, D))],
        out_specs=[_rows(tm, D), _rows(tm, D), pl.BlockSpec((8, D), lambda i: (0, 0))],
        out_shape=[big, jax.ShapeDtypeStruct((S, D), BF16), jax.ShapeDtypeStruct((8, D), F32)],
        compiler_params=_params(("arbitrary",)),
    )(u, c_gate, dy_conv, ln_g, ln_b)


def _conv_bwd_taps(du, c_val, c_glu, conv_w):
    T = CONV_T
    last = S // T - 1

    def body(du_ref, dua_ref, cv_ref, cg_ref, cvh_ref, cgh_ref, w_ref, dcv_ref, dcg_ref, dw_ref,
             hwin, dwin, dhs, dw_acc):
        i = pl.program_id(0)

        @pl.when(i == 0)
        def _():
            dw_acc[...] = jnp.zeros_like(dw_acc)

        for cb in range(NCH):
            cols = slice(cb * LANES, (cb + 1) * LANES)
            hwin[cb, HALO:HALO + T, :] = cv_ref[:, cols] * _sigmoid(cg_ref[:, cols])
            hwin[cb, 0:HALO, :] = jnp.where(i > 0, cvh_ref[:, cols] * _sigmoid(cgh_ref[:, cols]), 0.0)
            dwin[cb, 0:T, :] = du_ref[:, cols]
            dwin[cb, T:T + HALO, :] = jnp.where(i < last, dua_ref[:, cols], 0.0)
        for cb in range(NCH):
            cols = slice(cb * LANES, (cb + 1) * LANES)
            taps = _taps(w_ref, cols)

            def group_dh(g, carry):
                for b in range(SUBLANES):
                    base = g * GROUP + b
                    acc = jnp.zeros((SUBLANES, LANES), F32)
                    for j in range(CONV_K):
                        acc = acc + taps[j] * _comb(dwin, cb, base + (CONV_K - 1 - j))
                    dhs[cb, pl.ds(base, SUBLANES, stride=SUBLANES), :] = acc
                return carry

            lax.fori_loop(0, T // GROUP, group_dh, 0)

            def group_dw(g, sums):
                for b in range(SUBLANES):
                    base = g * GROUP + b
                    d = _comb(dwin, cb, base)
                    sums = tuple(sums[j] + d * _comb(hwin, cb, base + (HALO - (CONV_K - 1) + j))
                                 for j in range(CONV_K))
                return sums

            sums = lax.fori_loop(0, T // GROUP, group_dw, tuple(dw_acc[j, :, cols] for j in range(CONV_K)))
            for j in range(CONV_K):
                dw_acc[j, :, cols] = sums[j]
            dh = dhs[cb]
            cv, sg = cv_ref[:, cols], _sigmoid(cg_ref[:, cols])
            dcv_ref[:, cols] = (dh * sg).astype(BF16)
            dcg_ref[:, cols] = (dh * cv * (sg * (1.0 - sg))).astype(BF16)

        @pl.when(i == last)
        def _():
            dw_ref[...] = jnp.zeros_like(dw_ref)
            for j in range(CONV_K):
                dw_ref[j:j + 1, :] = jnp.sum(dw_acc[j], axis=0, keepdims=True)

    before = pl.BlockSpec((HALO, D), _halo_before)
    after = pl.BlockSpec((HALO, D), _halo_after)
    big = jax.ShapeDtypeStruct((S, D), BF16)
    return pl.pallas_call(
        body, grid=(S // T,), name="conv_bwd_taps",
        in_specs=[_rows(T, D), after, _rows(T, D), _rows(T, D), before, before, _resident((HALO, D))],
        out_specs=[_rows(T, D), _rows(T, D), pl.BlockSpec((HALO, D), lambda i: (0, 0))],
        out_shape=[big, big, jax.ShapeDtypeStruct((HALO, D), F32)],
        scratch_shapes=[pltpu.VMEM((NCH, T + HALO, LANES), F32), pltpu.VMEM((NCH, T + HALO, LANES), F32),
                        pltpu.VMEM((NCH, T, LANES), F32), pltpu.VMEM((CONV_K, SUBLANES, D), F32)],
        compiler_params=_params(("arbitrary",)),
    )(du, du, c_val, c_glu, c_val, c_glu, conv_w)


def _outproj_loss(y_att, y_conv, w_out_bf, x, target, gf):
    tm = 256

    def body(ya_ref, yc_ref, w_ref, x_ref, t_ref, gf_ref, dx2_ref, dya_ref, dyc_ref, dw_ref, st_ref, acc):
        @pl.when(pl.program_id(0) == 0)
        def _():
            acc[...] = jnp.zeros_like(acc)
            st_ref[...] = jnp.zeros_like(st_ref)

        ya, yc = ya_ref[...], yc_ref[...]
        x2 = x_ref[...] + _dot(ya, w_ref[0:D, :]) + _dot(yc, w_ref[D:2 * D, :])
        r = lax.rsqrt(jnp.mean(x2 * x2, axis=-1, keepdims=True) + NORM_EPS)
        xn = x2 * r
        err = xn * gf_ref[...] - t_ref[...]
        dout = err * (1.0 / D)
        dxn = dout * gf_ref[...]
        dx2 = r * (dxn - xn * jnp.mean(dxn * xn, axis=-1, keepdims=True))
        dx2_ref[...] = dx2
        dx2b = dx2.astype(BF16)
        dya_ref[...] = _dot_nt(dx2b, w_ref[0:D, :])
        dyc_ref[...] = _dot_nt(dx2b, w_ref[D:2 * D, :])
        acc[0:D, :] += _dot_tn(ya, dx2b)
        acc[D:2 * D, :] += _dot_tn(yc, dx2b)
        st_ref[0:1, :] += jnp.sum(dout * xn, axis=0, keepdims=True)
        st_ref[1:2, :] += jnp.sum(err * err, axis=0, keepdims=True) * (0.5 / D)

        @pl.when(pl.program_id(0) == S // tm - 1)
        def _():
            dw_ref[...] = acc[...].astype(BF16)

    big = jax.ShapeDtypeStruct((S, D), F32)
    return pl.pallas_call(
        body, grid=(S // tm,), name="outproj_loss",
        in_specs=[_rows(tm, D), _rows(tm, D), _resident((WOUT_ROWS, D)), _rows(tm, D), _rows(tm, D), _resident((1, D))],
        out_specs=[_rows(tm, D), _rows(tm, D), _rows(tm, D),
                   pl.BlockSpec((WOUT_ROWS, D), lambda i: (0, 0)), pl.BlockSpec((8, D), lambda i: (0, 0))],
        out_shape=[big, big, big, jax.ShapeDtypeStruct((WOUT_ROWS, D), BF16), jax.ShapeDtypeStruct((8, D), F32)],
        scratch_shapes=[pltpu.VMEM((WOUT_ROWS, D), F32)],
        compiler_params=_params(("arbitrary",)),
    )(y_att, y_conv, w_out_bf, x, target, gf)


UNITS_PER_CHUNK = CHUNK // LANES


def _dproj_unit(u, dqs, dkvs, gates, rows):
    if u < OFF_K // LANES:
        return ((dqs[0][u] + dqs[1][u] + dqs[2][u]) * (HD ** -0.5)).astype(BF16)
    if u < OFF_AG // LANES:
        w = u - OFF_K // LANES
        ta, tb = (dkvs[0][j] + dkvs[1][j] + dkvs[2][j] for j in (2 * (w % 2), 2 * (w % 2) + 1))
        low = _low_lanes(rows)
        if w < 2:
            return jnp.where(low, ta, pltpu.roll(tb, HD, axis=1)).astype(BF16)
        return jnp.where(low, pltpu.roll(ta, HD, axis=1), tb).astype(BF16)
    g, sl = divmod(u - OFF_AG // LANES, D // LANES)
    return gates[g][:, sl * LANES:(sl + 1) * LANES]


def _dproj_sources(units, dqs, dkvs, gates, rows):
    use_q = any(u < OFF_K // LANES for u in units)
    use_kv = any(OFF_K // LANES <= u < OFF_AG // LANES for u in units)
    use_g = sorted({(u - OFF_AG // LANES) // (D // LANES) for u in units if u >= OFF_AG // LANES})
    args = (list(dqs) if use_q else []) + (list(dkvs) if use_kv else []) + [gates[g] for g in use_g]
    specs = ([_slab_rows(D // LANES, rows)] * 3 if use_q else []) + ([_slab_rows(NKV, rows)] * 3 if use_kv else []) \
        + [_rows(rows, D)] * len(use_g)

    def pick(refs):
        refs = list(refs)
        q_refs = [refs.pop(0) for _ in range(3)] if use_q else None
        kv_refs = [refs.pop(0) for _ in range(3)] if use_kv else None
        return q_refs, kv_refs, {g: refs.pop(0) for g in use_g}

    return args, specs, pick


def _inproj_bwd_x(dqs, dkvs, gates, w_bf, x, g1, dx2, pi, po, ps):
    tm = 256
    last = S // tm - 1
    units = range(NCOL // LANES)
    pieces, piece_specs, pick = _dproj_sources(units, dqs, dkvs, gates, tm)

    def body(*refs):
        piece_refs, refs = refs[:len(pieces)], refs[len(pieces):]
        (w_ref, x_ref, g_ref, dx2_ref, pi_ref, po_ref, ps_ref,
         gx_ref, st_ref, ri_ref, ro_ref, rs_ref, dp_ref, send, recv) = refs
        i = pl.program_id(0)
        copies = _chip_exchange_copies((pi_ref, po_ref, ps_ref), (ri_ref, ro_ref, rs_ref), send, recv)

        @pl.when(i == 0)
        def _():
            st_ref[...] = jnp.zeros_like(st_ref)
            for out, _ in copies:
                out.start()

        sources = pick(piece_refs)
        for u in units:
            dp_ref[:, u * LANES:(u + 1) * LANES] = _dproj_unit(u, *sources, tm)
        dh = _dot_nt(dp_ref[...], w_ref[...])
        xt = x_ref[...]
        r = lax.rsqrt(jnp.mean(xt * xt, axis=-1, keepdims=True) + NORM_EPS)
        xn = xt * r
        dxn = dh * g_ref[...]
        gx_ref[...] = dx2_ref[...] + r * (dxn - xn * jnp.mean(dxn * xn, axis=-1, keepdims=True))
        st_ref[0:1, :] += jnp.sum(dh * xn, axis=0, keepdims=True)

        @pl.when(i == last)
        def _():
            for _, arrival in copies:
                arrival.wait_recv()
            for out, _ in copies:
                out.wait_send()

    n = 3 * len(CHIP_FLIPS)
    return pl.pallas_call(
        body, grid=(S // tm,), name="inproj_bwd_x",
        in_specs=piece_specs + [_resident((D, NCOL)), _rows(tm, D), _resident((1, D)), _rows(tm, D), ANY, ANY, ANY],
        out_specs=[_rows(tm, D), pl.BlockSpec((8, D), lambda i: (0, 0)), ANY, ANY, ANY],
        out_shape=[jax.ShapeDtypeStruct((S, D), F32), jax.ShapeDtypeStruct((8, D), F32),
                   jax.ShapeDtypeStruct((NCHIP, D // 2, CHUNK), BF16),
                   jax.ShapeDtypeStruct((NCHIP, WOUT_SHARD // 2, D), BF16),
                   jax.ShapeDtypeStruct((NCHIP, SMALL_ROWS, D), F32)],
        scratch_shapes=[pltpu.VMEM((tm, NCOL), BF16), pltpu.SemaphoreType.DMA((n,)), pltpu.SemaphoreType.DMA((n,))],
        compiler_params=_params(("arbitrary",)),
    )(*pieces, w_bf, x, g1, dx2, pi, po, ps)


def _inproj_bwd_w(h, dqs, dkvs, gates):
    tk = 512
    nk = S // tk
    out = None
    for k in range(NCHIP):
        units = range(k * UNITS_PER_CHUNK, (k + 1) * UNITS_PER_CHUNK)
        pieces, piece_specs, pick = _dproj_sources(units, dqs, dkvs, gates, tk)
        handed_on = [] if out is None else [out]

        def body(*refs, units=units, pick=pick, n_pieces=len(pieces), n_in=1 + len(pieces) + len(handed_on)):
            h_ref, piece_refs = refs[0], refs[1:1 + n_pieces]
            o_ref, tile, acc = refs[n_in:]
            i = pl.program_id(0)

            @pl.when(i == 0)
            def _():
                acc[...] = jnp.zeros_like(acc)

            sources = pick(piece_refs)
            for n, u in enumerate(units):
                tile[:, n * LANES:(n + 1) * LANES] = _dproj_unit(u, *sources, tk)
            acc[...] += _dot_tn(h_ref[...], tile[...])

            @pl.when(i == nk - 1)
            def _():
                o_ref[0] = acc[...].astype(BF16)

        out = pl.pallas_call(
            body, grid=(nk,), name=f"inproj_bwd_w{k}",
            in_specs=[_rows(tk, D)] + piece_specs + [ANY] * len(handed_on),
            out_specs=pl.BlockSpec((1, D, CHUNK), lambda i, k=k: (k, 0, 0)),
            out_shape=jax.ShapeDtypeStruct((NCHIP, D, CHUNK), BF16),
            input_output_aliases={1 + len(pieces): 0} if handed_on else {},
            scratch_shapes=[pltpu.VMEM((tk, CHUNK), BF16), pltpu.VMEM((D, CHUNK), F32)],
            compiler_params=_params(("arbitrary",)),
        )(h, *pieces, *handed_on)
    return out


def _local_step(x, target, g1, w_in_bf, conv_w, conv_b, ln_g, ln_b, w_out_bf, gf):
    h, q, k, v, a_gate, c_val, c_glu, c_gate = _inproj_fwd(x, g1, w_in_bf)
    tables = [_bias_table(d) for d in PATTERNS]
    outs, lses = zip(*[_attn_fwd(q, k, v, t, d) for t, d in zip(tables, PATTERNS)])
    o, lse, y_att = _attn_combine(outs, lses, a_gate)
    u, y_conv = _conv_fwd(c_val, c_glu, c_gate, conv_w, conv_b, ln_g, ln_b)
    dx2, dy_att, dy_conv, dw_out, st_out = _outproj_loss(y_att, y_conv, w_out_bf, x, target, gf)

    do, da_gate, delta = _attn_gate_bwd(dy_att, o, a_gate, _head_sum_selectors())
    dqs, dkvs = zip(*[_attn_bwd(q, k, v, do, lse, delta, t, d) for t, d in zip(tables, PATTERNS)])

    du, dc_gate, st_conv = _conv_bwd_rows(u, c_gate, dy_conv, ln_g, ln_b)
    dc_val, dc_glu, dconv_w = _conv_bwd_taps(du, c_val, c_glu, conv_w)

    dproj_pieces = (dqs, dkvs, (da_gate, dc_val, dc_glu, dc_gate))
    dw_in = _inproj_bwd_w(h, *dproj_pieces)
    small = jnp.concatenate([st_conv, st_out, dconv_w], axis=0)
    return dw_in, dw_out, small, dproj_pieces, dx2


ROW_LN_G, ROW_LN_B, ROW_CONV_B, ROW_FINAL_G, ROW_LOSS, ROW_TAPS = 0, 1, 2, 8, 9, 16
SMALL_ROWS = 16 + HALO
NDEV = 8


MESH = pl.DeviceIdType.MESH
ANY = pl.BlockSpec(memory_space=pl.ANY)
CHIP_FLIPS = ((1, 0), (0, 1), (1, 1))


def _pos():
    return lax.axis_index("x"), lax.axis_index("y"), lax.axis_index("c")


def _flip(v, f):
    return 1 - v if f else v


def _ds(start, size, align=None):
    return pl.ds(pl.multiple_of(start, align or size), size)


def _place_shards(wi, wo, cw, where):
    steps = 4

    def body(where_ref, wi_ref, wo_ref, cw_ref, wi_full, wo_full, cw_full):
        wi_full[...] = wi_ref[...].astype(BF16)
        wo_full[...] = wo_ref[...].astype(BF16)
        cw_full[...] = cw_ref[...]

    grid_spec = pltpu.PrefetchScalarGridSpec(
        num_scalar_prefetch=1, grid=(steps,),
        in_specs=[pl.BlockSpec((D // steps, CHUNK), lambda i, w: (i, 0)),
                  pl.BlockSpec((WOUT_SHARD // steps, D), lambda i, w: (i, 0)),
                  pl.BlockSpec((HALO, CONVW_SHARD), lambda i, w: (0, 0))],
        out_specs=[pl.BlockSpec((D // steps, CHUNK), lambda i, w: (i, w[0])),
                   pl.BlockSpec((WOUT_SHARD // steps, D), lambda i, w: (w[0] * steps + i, 0)),
                   pl.BlockSpec((HALO, CONVW_SHARD), lambda i, w: (0, w[0]))])
    return pl.pallas_call(
        body, grid_spec=grid_spec, name="place_shards",
        out_shape=[jax.ShapeDtypeStruct((D, NCOL), BF16), jax.ShapeDtypeStruct((WOUT_ROWS, D), BF16),
                   jax.ShapeDtypeStruct((HALO, D), F32)],
        compiler_params=_params(("arbitrary",)),
    )(where, wi, wo, cw)


def _gather_weights(wi_full, wo_full, cw_full):
    halves = (D // 2, WOUT_SHARD // 2, HALO // 2)
    OWN_X, OWN_Y, VIA_Y, VIA_X = range(4)

    def body(_wi, _wo, _cw, wi_full, wo_full, cw_full, send, recv):
        x, y, c = _pos()
        x_nbr, y_nbr, diag = (1 - x, y), (x, 1 - y), (1 - x, 1 - y)

        def region(a, chip_xy, half, part=None):
            chip = 2 * chip_xy[0] + chip_xy[1]
            n, row = halves[a], half * halves[a]
            if part is not None:
                n = n // 2
                row = row + part * n
            if a == 0:
                return wi_full.at[_ds(row, n), _ds(chip * CHUNK, CHUNK, 128)]
            if a == 1:
                return wo_full.at[_ds(chip * WOUT_SHARD + row, n), :]
            return cw_full.at[_ds(row, n), _ds(chip * CONVW_SHARD, CONVW_SHARD, 128)]

        def copy(a, kind, piece, dev):
            k = 8 * a + kind
            return pltpu.make_async_remote_copy(src_ref=piece, dst_ref=piece, send_sem=send.at[k], recv_sem=recv.at[k],
                                                device_id=dev, device_id_type=MESH)

        def to_sibling(a, kind, piece):
            cp = copy(a, 4 + kind, piece, (x, y, 1 - c))
            cp.start()
            return cp

        sends = []
        for a in range(3):
            for kind, nbr in ((OWN_X, x_nbr), (OWN_Y, y_nbr)):
                cp = copy(a, kind, region(a, (x, y), c), (*nbr, c))
                cp.start()
                sends.append(cp)
        for a in range(3):
            got = region(a, x_nbr, c)
            copy(a, OWN_X, got, (*x_nbr, c)).wait_recv()
            onward = copy(a, VIA_Y, region(a, x_nbr, c, 0), (*y_nbr, c))
            onward.start()
            sends += [onward, to_sibling(a, OWN_X, got)]
            got = region(a, y_nbr, c)
            copy(a, OWN_Y, got, (*y_nbr, c)).wait_recv()
            onward = copy(a, VIA_X, region(a, y_nbr, c, 1), (*x_nbr, c))
            onward.start()
            sends += [onward, to_sibling(a, OWN_Y, got)]
        for a in range(3):
            got = region(a, diag, c, 0)
            copy(a, VIA_Y, got, (*y_nbr, c)).wait_recv()
            sends.append(to_sibling(a, VIA_Y, got))
            got = region(a, diag, c, 1)
            copy(a, VIA_X, got, (*x_nbr, c)).wait_recv()
            sends.append(to_sibling(a, VIA_X, got))
        for a in range(3):
            for kind, piece in ((OWN_X, region(a, x_nbr, 1 - c)), (OWN_Y, region(a, y_nbr, 1 - c)),
                                (VIA_Y, region(a, diag, 1 - c, 0)), (VIA_X, region(a, diag, 1 - c, 1))):
                copy(a, 4 + kind, piece, (x, y, 1 - c)).wait_recv()
        for cp in sends:
            cp.wait_send()

    n_sems = 3 * 8
    return pl.pallas_call(
        body, name="gather_weights",
        in_specs=[ANY, ANY, ANY], out_specs=[ANY, ANY, ANY], input_output_aliases={0: 0, 1: 1, 2: 2},
        out_shape=[jax.ShapeDtypeStruct((D, NCOL), BF16), jax.ShapeDtypeStruct((WOUT_ROWS, D), BF16),
                   jax.ShapeDtypeStruct((HALO, D), F32)],
        scratch_shapes=[pltpu.SemaphoreType.DMA((n_sems,)), pltpu.SemaphoreType.DMA((n_sems,))],
    )(wi_full, wo_full, cw_full)


def _exchange_halves(gi4, go4, small):
    def body(gi_ref, go_ref, sm_ref, ri_ref, ro_ref, rs_ref, send, recv):
        x, y, c = _pos()
        sib = (x, y, 1 - c)
        copies = [
            (gi_ref.at[:, _ds((1 - c) * (D // 2), D // 2), :], ri_ref),
            (go_ref.at[:, _ds((1 - c) * (WOUT_SHARD // 2), WOUT_SHARD // 2), :], ro_ref),
            (sm_ref, rs_ref),
        ]
        cps = [pltpu.make_async_remote_copy(src_ref=s_, dst_ref=d_, send_sem=send.at[k], recv_sem=recv.at[k],
                                            device_id=sib, device_id_type=MESH) for k, (s_, d_) in enumerate(copies)]
        for cp in cps:
            cp.start()
        for cp in cps:
            cp.wait()

    return pl.pallas_call(
        body, name="exchange_halves",
        in_specs=[ANY, ANY, ANY], out_specs=[ANY, ANY, ANY],
        out_shape=[jax.ShapeDtypeStruct((NCHIP, D // 2, CHUNK), BF16),
                   jax.ShapeDtypeStruct((NCHIP, WOUT_SHARD // 2, D), BF16),
                   jax.ShapeDtypeStruct((SMALL_ROWS, D), F32)],
        scratch_shapes=[pltpu.SemaphoreType.DMA((3,)), pltpu.SemaphoreType.DMA((3,))],
    )(gi4, go4, small)


def _add_halves(gi4, ri, go4, ro, small, rs):
    hi, ho = D // 2, WOUT_SHARD // 2

    def body(gi_ref, ri_ref, go_ref, ro_ref, sm_ref, rs_ref, pi_ref, po_ref, ps_ref):
        c = lax.axis_index("c")
        pi_ref[0] = (gi_ref[0, _ds(c * hi, hi), :].astype(F32) + ri_ref[0].astype(F32)).astype(BF16)
        po_ref[0] = (go_ref[0, _ds(c * ho, ho), :].astype(F32) + ro_ref[0].astype(F32)).astype(BF16)
        ps_ref[...] = sm_ref[...] + rs_ref[...]

    blk = lambda n, w: pl.BlockSpec((1, n, w), lambda k: (k, 0, 0))
    whole = pl.BlockSpec((SMALL_ROWS, D), lambda k: (0, 0))
    return pl.pallas_call(
        body, grid=(NCHIP,), name="add_halves",
        in_specs=[blk(D, CHUNK), blk(hi, CHUNK), blk(WOUT_SHARD, D), blk(ho, D), whole, whole],
        out_specs=[blk(hi, CHUNK), blk(ho, D), whole],
        out_shape=[jax.ShapeDtypeStruct((NCHIP, hi, CHUNK), BF16), jax.ShapeDtypeStruct((NCHIP, ho, D), BF16),
                   jax.ShapeDtypeStruct((SMALL_ROWS, D), F32)],
        compiler_params=_params(("arbitrary",)),
    )(gi4, ri, go4, ro, small, rs)


def _chip_exchange_copies(srcs, dsts, send, recv):
    x, y, c = _pos()
    me = 2 * x + y
    pairs = []
    for a in range(3):
        for j, (fx, fy) in enumerate(CHIP_FLIPS):
            px, py = _flip(x, fx), _flip(y, fy)
            peer = 2 * px + py
            k = 3 * a + j
            out = pltpu.make_async_remote_copy(
                src_ref=srcs[a] if a == 2 else srcs[a].at[peer], dst_ref=dsts[a].at[me],
                send_sem=send.at[k], recv_sem=recv.at[k], device_id=(px, py, c), device_id_type=MESH)
            got = dsts[a].at[peer]
            arrival = pltpu.make_async_remote_copy(
                src_ref=got, dst_ref=got, send_sem=send.at[k], recv_sem=recv.at[k],
                device_id=(px, py, c), device_id_type=MESH)
            pairs.append((out, arrival))
    return pairs


def _sum_chips(ri, ro, rs, pi, po, ps, where):
    def body(w_ref, ri_ref, ro_ref, rs_ref, pi_ref, po_ref, ps_ref, gi_ref, go_ref, gs_ref, g5_ref, loss_ref,
             acc_i, acc_o, acc_s):
        k = pl.program_id(0)
        accs = (acc_i, acc_o, acc_s)

        @pl.when(k == 0)
        def _():
            for acc in accs:
                acc[...] = jnp.zeros_like(acc)

        @pl.when(k == w_ref[0])
        def _():
            for acc, val in zip(accs, (pi_ref[0], po_ref[0], ps_ref[...])):
                acc[...] += val.astype(F32)

        @pl.when(k != w_ref[0])
        def _():
            for acc, ref in zip(accs, (ri_ref, ro_ref, rs_ref)):
                acc[...] += ref[0].astype(F32)

        @pl.when(k == NCHIP - 1)
        def _():
            gi_ref[0] = acc_i[...]
            go_ref[0] = acc_o[...]
            gs_ref[...] = acc_s[...]
            g5_ref[...] = jnp.zeros_like(g5_ref)
            for i, row in enumerate((ROW_CONV_B, ROW_LN_G, ROW_LN_B, ROW_FINAL_G)):
                g5_ref[i + 1:i + 2, :] = acc_s[row:row + 1, :]
            loss = jnp.sum(acc_s[ROW_LOSS:ROW_LOSS + 1, :], axis=1, keepdims=True)
            loss_ref[...] = jnp.broadcast_to(loss, loss_ref.shape)

    def sent(k, w):
        return jnp.where(k == w[0], (k + 1) % NCHIP, k)

    hi, ho = D // 2, WOUT_SHARD // 2
    const = lambda shape: pl.BlockSpec(shape, lambda k, w: (0,) * len(shape))
    grid_spec = pltpu.PrefetchScalarGridSpec(
        num_scalar_prefetch=1, grid=(NCHIP,),
        in_specs=[pl.BlockSpec((1, hi, CHUNK), lambda k, w: (sent(k, w), 0, 0)),
                  pl.BlockSpec((1, ho, D), lambda k, w: (sent(k, w), 0, 0)),
                  pl.BlockSpec((1, SMALL_ROWS, D), lambda k, w: (sent(k, w), 0, 0)),
                  pl.BlockSpec((1, hi, CHUNK), lambda k, w: (w[0], 0, 0)),
                  pl.BlockSpec((1, ho, D), lambda k, w: (w[0], 0, 0)),
                  const((SMALL_ROWS, D))],
        out_specs=[pl.BlockSpec((1, hi, CHUNK), lambda k, w: (w[1], 0, 0)),
                   pl.BlockSpec((1, ho, D), lambda k, w: (w[1], 0, 0)),
                   const((SMALL_ROWS, D)), const((8, D)), const((8, LANES))],
        scratch_shapes=[pltpu.VMEM((hi, CHUNK), F32), pltpu.VMEM((ho, D), F32), pltpu.VMEM((SMALL_ROWS, D), F32)])
    return pl.pallas_call(
        body, grid_spec=grid_spec, name="sum_chips",
        out_shape=[jax.ShapeDtypeStruct((2, hi, CHUNK), F32), jax.ShapeDtypeStruct((2, ho, D), F32),
                   jax.ShapeDtypeStruct((SMALL_ROWS, D), F32), jax.ShapeDtypeStruct((8, D), F32),
                   jax.ShapeDtypeStruct((8, LANES), F32)],
        compiler_params=_params(("arbitrary",)),
    )(where, ri, ro, rs, pi, po, ps)


def _exchange_results(gi2, go2, st):
    flips = [(fx, fy, fc) for fx in (0, 1) for fy in (0, 1) for fc in (0, 1)][1:]

    def body(_gi, _go, st_ref, gi_ref, go_ref, all_ref, send, recv, lsem):
        x, y, c = _pos()
        sib = (x, y, 1 - c)

        def half(k, ref, slot):
            return pltpu.make_async_remote_copy(src_ref=ref.at[slot], dst_ref=ref.at[slot], send_sem=send.at[k],
                                                recv_sem=recv.at[k], device_id=sib, device_id_type=MESH)

        def stat(k, src, slot, dev):
            return pltpu.make_async_remote_copy(src_ref=src, dst_ref=all_ref.at[slot], send_sem=send.at[k],
                                                recv_sem=recv.at[k], device_id=dev, device_id_type=MESH)

        mine = pltpu.make_async_copy(st_ref, all_ref.at[4 * x + 2 * y + c], lsem)
        mine.start()
        sends = [half(k, ref, c) for k, ref in enumerate((gi_ref, go_ref))]
        peers = [(_flip(x, fx), _flip(y, fy), _flip(c, fc)) for fx, fy, fc in flips]
        sends += [stat(2 + k, st_ref, 4 * x + 2 * y + c, dev) for k, dev in enumerate(peers)]
        for cp in sends:
            cp.start()
        for k, ref in enumerate((gi_ref, go_ref)):
            half(k, ref, 1 - c).wait_recv()
        for k, (px, py, pc) in enumerate(peers):
            slot = 4 * px + 2 * py + pc
            stat(2 + k, all_ref.at[slot], slot, (px, py, pc)).wait_recv()
        for cp in sends:
            cp.wait_send()
        mine.wait()

    n = 2 + len(flips)
    return pl.pallas_call(
        body, name="exchange_results",
        in_specs=[ANY, ANY, ANY], out_specs=[ANY, ANY, ANY], input_output_aliases={0: 0, 1: 1},
        out_shape=[jax.ShapeDtypeStruct((2, D // 2, CHUNK), F32), jax.ShapeDtypeStruct((2, WOUT_SHARD // 2, D), F32),
                   jax.ShapeDtypeStruct((NDEV, 8, D), F32)],
        scratch_shapes=[pltpu.SemaphoreType.DMA((n,)), pltpu.SemaphoreType.DMA((n,)), pltpu.SemaphoreType.DMA],
    )(gi2, go2, st)


def _adamw_math(w, g, m, v):
    m2 = ADAM_B1 * m + (1.0 - ADAM_B1) * g
    v2 = ADAM_B2 * v + (1.0 - ADAM_B2) * (g * g)
    m_hat = m2 / (1.0 - ADAM_B1 ** ADAM_STEP)
    v_hat = v2 / (1.0 - ADAM_B2 ** ADAM_STEP)
    delta = -ADAM_LR * (m_hat / (jnp.sqrt(v_hat) + ADAM_EPS) + ADAM_WD * w)
    return delta, m2, v2


def _adamw(w, g, m, v, name):
    rows, cols = w.shape
    tm = 256 if rows % 256 == 0 else rows

    def body(w_ref, g_ref, m_ref, v_ref, d_ref, m2_ref, v2_ref):
        d_ref[...], m2_ref[...], v2_ref[...] = _adamw_math(w_ref[...], g_ref[...], m_ref[...], v_ref[...])

    shape = jax.ShapeDtypeStruct(w.shape, F32)
    return pl.pallas_call(
        body, grid=(rows // tm,), name=name,
        in_specs=[_rows(tm, cols)] * 4, out_specs=[_rows(tm, cols)] * 3, out_shape=[shape] * 3,
        compiler_params=_params(("arbitrary",)),
    )(w, g, m, v)


def _adamw_vectors(g5, first_parts, ws, ms, vs):
    n = len(ws)

    def body(g_ref, parts_ref, *refs):
        ins, g0_ref, outs = refs[:3 * n], refs[3 * n], refs[3 * n + 1:]
        g0 = parts_ref[0, 0:1, :]
        for dev in range(1, NDEV):
            g0 = g0 + parts_ref[dev, 0:1, :]
        g0_ref[...] = g0
        for i in range(n):
            g = g0 if i == 0 else g_ref[i:i + 1, :]
            res = _adamw_math(ins[i][...], g, ins[n + i][...], ins[2 * n + i][...])
            for kind in range(3):
                outs[kind * n + i][...] = res[kind]

    shape = jax.ShapeDtypeStruct((1, D), F32)
    return pl.pallas_call(body, name="adamw_vectors", out_shape=[shape] * (1 + 3 * n), compiler_params=_params())(
        g5, first_parts, *ws, *ms, *vs)


def kernel(x, norm_g, w_in, conv_w, conv_b, conv_ln_g, conv_ln_b, w_out, final_norm_g, loss_target, m_norm_g, m_w_in, m_conv_w, m_conv_b, m_conv_ln_g, m_conv_ln_b, m_w_out, m_final_norm_g, v_norm_g, v_w_in, v_conv_w, v_conv_b, v_conv_ln_g, v_conv_ln_b, v_w_out, v_final_norm_g):
    chip = 2 * lax.axis_index("x") + lax.axis_index("y")
    where = jnp.stack([chip, lax.axis_index("c")]).astype(jnp.int32)
    taps_shard = jnp.pad(conv_w[0], ((0, HALO - CONV_K), (0, 0)))
    wi_full, wo_full, cw_full = _gather_weights(*_place_shards(w_in[0], w_out[0], taps_shard, where))

    gf = final_norm_g[None]
    dw_in4, dw_out, small, dproj_pieces, dx2 = _local_step(
        x[0], loss_target[0], norm_g, wi_full, cw_full, conv_b, conv_ln_g, conv_ln_b, wo_full, gf)
    dw_out4 = dw_out.reshape(NCHIP, WOUT_SHARD, D)

    ri, ro, rs = _exchange_halves(dw_in4, dw_out4, small)
    pi, po, ps = _add_halves(dw_in4, ri, dw_out4, ro, small, rs)
    grad_x, st_in, ri, ro, rs = _inproj_bwd_x(*dproj_pieces, wi_full, x[0], norm_g, dx2, pi, po, ps)
    gi2, go2, g_small, g5, loss8 = _sum_chips(ri, ro, rs, pi, po, ps, where)
    gi2, go2, norm_g_parts = _exchange_results(gi2, go2, st_in)
    g_w_in = gi2.reshape(D, CHUNK)
    g_w_out = go2.reshape(WOUT_SHARD, D)
    g_taps = lax.dynamic_slice(g_small, (ROW_TAPS, chip * CONVW_SHARD), (CONV_K, CONVW_SHARD))

    d_w_in, m2_w_in, v2_w_in = _adamw(w_in[0], g_w_in, m_w_in[0], v_w_in[0], "adamw_w_in")
    d_w_out, m2_w_out, v2_w_out = _adamw(w_out[0], g_w_out, m_w_out[0], v_w_out[0], "adamw_w_out")
    d_taps, m2_taps, v2_taps = _adamw(conv_w[0], g_taps, m_conv_w[0], v_conv_w[0], "adamw_conv_w")
    g_norm, *vec = _adamw_vectors(
        g5, norm_g_parts,
        (norm_g, conv_b, conv_ln_g, conv_ln_b, gf),
        (m_norm_g, m_conv_b, m_conv_ln_g, m_conv_ln_b, m_final_norm_g[None]),
        (v_norm_g, v_conv_b, v_conv_ln_g, v_conv_ln_b, v_final_norm_g[None]))
    d_vec, m2_vec, v2_vec = vec[0:5], vec[5:10], vec[10:15]

    def weight_order(ng, wi, cw, cb, lg, lb, wo, fg):
        return (ng, wi[None], cw[None], cb, lg, lb, wo[None], fg[0])

    grads = weight_order(g_norm, g_w_in, g_taps, g5[1:2], g5[2:3], g5[3:4], g_w_out, g5[4:5])
    deltas = weight_order(d_vec[0], d_w_in, d_taps, d_vec[1], d_vec[2], d_vec[3], d_w_out, d_vec[4])
    new_m = weight_order(m2_vec[0], m2_w_in, m2_taps, m2_vec[1], m2_vec[2], m2_vec[3], m2_w_out, m2_vec[4])
    new_v = weight_order(v2_vec[0], v2_w_in, v2_taps, v2_vec[1], v2_vec[2], v2_vec[3], v2_w_out, v2_vec[4])
    return (loss8[0, 0], grad_x[None], *grads, *deltas, *new_m, *new_v)
```

```python
import jax
import jax.numpy as jnp
from jax import lax
from jax.experimental import pallas as pl
from jax.experimental.pallas import tpu as pltpu

F32 = jnp.float32
BF16 = jnp.bfloat16

S = 4096
D = 1024
LANES = 128
HD = 64
NKV = 4
GQ = 4
KVW = NKV * HD
NCOL = 5632
CONV_K = 31
HALO = 32
BLK = 128
PATTERNS = (1, 4, 16)
NORM_EPS = 1e-6
LN_EPS = 1e-5
NEG = -1e30
OFF_Q, OFF_K, OFF_V, OFF_AG, OFF_CV, OFF_CG, OFF_CGATE = 0, 1024, 1280, 1536, 2560, 3584, 4608
NCHIP = 4
CHUNK = NCOL // NCHIP
WOUT_ROWS = 2 * D
WOUT_SHARD = WOUT_ROWS // NCHIP
CONVW_SHARD = D // NCHIP

ADAM_LR, ADAM_B1, ADAM_B2, ADAM_EPS, ADAM_WD, ADAM_STEP = 0.001, 0.9, 0.999, 1e-08, 0.01, 10

VMEM_LIMIT = 56 * 1024 * 1024


def _params(sem=None, vmem=VMEM_LIMIT):
    return pltpu.CompilerParams(dimension_semantics=sem, vmem_limit_bytes=vmem)


def _sigmoid(a):
    return 0.5 * jnp.tanh(0.5 * a) + 0.5


def _rows(tm, width):
    return pl.BlockSpec((tm, width), lambda i: (i, 0))


def _slabs(n):
    return jax.ShapeDtypeStruct((n, S, LANES), F32)


def _slab_rows(n, tm):
    return pl.BlockSpec((n, tm, LANES), lambda i: (0, i, 0))


def _resident(shape):
    return pl.BlockSpec(shape, lambda *_: (0,) * len(shape), pipeline_mode=pl.Buffered(1))


def _dot(a, b):
    return jnp.dot(a, b, preferred_element_type=F32)


def _dot_nt(a, b):
    return lax.dot_general(a, b, (((1,), (1,)), ((), ())), preferred_element_type=F32)


def _dot_tn(a, b):
    return lax.dot_general(a, b, (((0,), (0,)), ((), ())), preferred_element_type=F32)


def _inproj_fwd(x, g1, w_bf):
    tm = 512

    def body(x_ref, g_ref, w_ref, h_ref, q_ref, k_ref, v_ref, ag_ref, cv_ref, cg_ref, cgate_ref):
        xt = x_ref[...]
        r = lax.rsqrt(jnp.mean(xt * xt, axis=-1, keepdims=True) + NORM_EPS)
        h = (xt * r * g_ref[...]).astype(BF16)
        h_ref[...] = h
        q = _dot(h, w_ref[:, OFF_Q:OFF_Q + D]) * (HD ** -0.5)
        kv = _dot(h, w_ref[:, OFF_K:OFF_K + 2 * KVW])
        for sl in range(D // LANES):
            q_ref[sl] = q[:, sl * LANES:(sl + 1) * LANES]
        for sl in range(KVW // LANES):
            k_ref[sl] = kv[:, sl * LANES:(sl + 1) * LANES]
            v_ref[sl] = kv[:, KVW + sl * LANES:KVW + (sl + 1) * LANES]
        ag_ref[...] = _dot(h, w_ref[:, OFF_AG:OFF_AG + D])
        cv_ref[...] = _dot(h, w_ref[:, OFF_CV:OFF_CV + D])
        cg_ref[...] = _dot(h, w_ref[:, OFF_CG:OFF_CG + D])
        cgate_ref[...] = _dot(h, w_ref[:, OFF_CGATE:OFF_CGATE + D])

    big = jax.ShapeDtypeStruct((S, D), F32)
    return pl.pallas_call(
        body, grid=(S // tm,), name="inproj_fwd",
        in_specs=[_rows(tm, D), _resident((1, D)), _resident((D, NCOL))],
        out_specs=[_rows(tm, D), _slab_rows(D // LANES, tm), _slab_rows(KVW // LANES, tm), _slab_rows(KVW // LANES, tm),
                   _rows(tm, D), _rows(tm, D), _rows(tm, D), _rows(tm, D)],
        out_shape=[jax.ShapeDtypeStruct((S, D), BF16), _slabs(D // LANES), _slabs(KVW // LANES), _slabs(KVW // LANES),
                   big, big, big, big],
        compiler_params=_params(("arbitrary",)),
    )(x, g1, w_bf)


def _bias_table(d):
    h = jnp.arange(NKV * GQ, dtype=F32)
    slopes = jnp.exp2(-8.0 * (h + 1.0) / (NKV * GQ))
    qi = jnp.arange(BLK)[:, None]
    kj = jnp.arange(2 * BLK)[None, :]
    dist = BLK + qi - kj
    window = (dist >= 0) & (dist <= BLK)
    bias = -slopes[:, None, None] * (dist * d).astype(F32)[None]
    has_prev = jnp.stack([jnp.broadcast_to(kj >= BLK, (BLK, 2 * BLK)), jnp.ones((BLK, 2 * BLK), bool)])
    valid = window[None] & has_prev
    tab = jnp.where(valid[:, None], bias[None], NEG)
    return tab.reshape(2, NKV, GQ * BLK, 2 * BLK)


def _sub_rows(start, d):
    if d == 1:
        return pl.ds(pl.multiple_of(start, BLK), BLK)
    return pl.ds(start, BLK, stride=d)


NHEAD = NKV * GQ
CHUNK_ROWS = 2048
BLOCKS_PER_CHUNK = CHUNK_ROWS // BLK


def _low_lanes(rows=BLK):
    return lax.broadcasted_iota(jnp.int32, (rows, LANES), 1) < HD


def _block_start(idx, d):
    shift = d.bit_length() - 1
    b, r = lax.shift_right_logical(idx, shift), lax.bitwise_and(idx, d - 1)
    start = b * (BLK * d) + r
    return b, start, jnp.maximum(start - BLK * d, r)


def _stack_heads(ref, rows):
    low = _low_lanes()
    t0, t1 = ref[0, rows, :], ref[1, rows, :]
    return jnp.concatenate([jnp.where(low, t0, 0.0), jnp.where(low, 0.0, t0),
                            jnp.where(low, t1, 0.0), jnp.where(low, 0.0, t1)], axis=0).astype(BF16)


def _unstack_heads(dup):
    low = _low_lanes()
    return (jnp.where(low, dup[0:BLK], dup[BLK:2 * BLK]), jnp.where(low, dup[2 * BLK:3 * BLK], dup[3 * BLK:4 * BLK]))


def _kv_dup(ref, prow, rows, odd):
    t = jnp.concatenate([ref[0, prow, :], ref[0, rows, :]], axis=0)
    swapped = pltpu.roll(t, HD, axis=1)
    keep = jnp.logical_xor(_low_lanes(2 * BLK), odd)
    return jnp.where(keep, t, swapped).astype(BF16)


def _attn_fwd(q, k, v, bias, d):
    def body(q_ref, k_ref, v_ref, b_ref, o_ref, l_ref):
        odd = pl.program_id(0) % 2 == 1
        ones = jnp.ones((2 * BLK, LANES), BF16)

        def block(idx, carry):
            b, start, pstart = _block_start(idx, d)
            rows, prow = _sub_rows(start, d), _sub_rows(pstart, d)
            qs = _stack_heads(q_ref, rows)
            kw = _kv_dup(k_ref, prow, rows, odd)
            vw = _kv_dup(v_ref, prow, rows, odd)
            s = _dot_nt(qs, kw) + b_ref[jnp.minimum(b, 1), 0]
            m = jnp.max(s, axis=1, keepdims=True)
            p = jnp.exp(s - m).astype(BF16)
            ol = _dot(p, jnp.concatenate([vw, ones], axis=1))
            l = ol[:, LANES:]
            o_ref[0, rows, :], o_ref[1, rows, :] = _unstack_heads(ol[:, :LANES] / l)
            l_ref[0, rows, :] = _by_head([(m + jnp.log(l))[g * BLK:(g + 1) * BLK] for g in range(GQ)])
            return carry

        lax.fori_loop(0, S // BLK, block, 0, unroll=2)

    q_like = pl.BlockSpec((2, S, LANES), lambda j: (j, 0, 0))
    kv = pl.BlockSpec((1, S, LANES), lambda j: (j // 2, 0, 0))
    per_kv = pl.BlockSpec((1, S, LANES), lambda j: (j, 0, 0))
    bias_spec = pl.BlockSpec((2, 1, GQ * BLK, 2 * BLK), lambda j: (0, j, 0, 0))
    return pl.pallas_call(
        body, grid=(NKV,), name=f"attn_fwd_d{d}",
        in_specs=[q_like, kv, kv, bias_spec],
        out_specs=[q_like, per_kv],
        out_shape=[_slabs(D // LANES), _slabs(NKV)],
        compiler_params=_params(("arbitrary",)),
    )(q, k, v, bias)


PIECES = 3


def _by_head(tiles):
    lane = lax.broadcasted_iota(jnp.int32, tiles[0].shape, 1)
    out = tiles[0]
    for g in range(1, GQ):
        out = jnp.where(lax.bitwise_and(lane, GQ - 1) == g, tiles[g], out)
    return out


def _minus_in_pieces(x):
    lane = lax.broadcasted_iota(jnp.int32, x.shape, 1)
    hi = (-x).astype(BF16).astype(F32)
    rest = -x - hi
    mid = rest.astype(BF16).astype(F32)
    lo = (rest - mid).astype(BF16).astype(F32)
    return jnp.where(lane < GQ, hi, jnp.where(lane < 2 * GQ, mid, jnp.where(lane < PIECES * GQ, lo, 0.0)))


def _attn_combine(outs, lses, a_gate):
    tm = 256

    def body(o1, o2, o3, l1, l2, l3, ag_ref, o_ref, lse_ref, y_ref):
        low = _low_lanes(tm)
        for j in range(NKV):
            a, b, c = l1[j], l2[j], l3[j]
            m = jnp.maximum(jnp.maximum(a, b), c)
            ea, eb, ec = jnp.exp(a - m), jnp.exp(b - m), jnp.exp(c - m)
            den = ea + eb + ec
            lse_ref[j] = _minus_in_pieces(m + jnp.log(den))
            inv = 1.0 / den
            for half in range(2):
                sl = 2 * j + half

                def spread(w):
                    return jnp.where(low, w[:, 2 * half:2 * half + 1], w[:, 2 * half + 1:2 * half + 2])

                o = spread(ea * inv) * o1[sl] + spread(eb * inv) * o2[sl] + spread(ec * inv) * o3[sl]
                o_ref[sl] = o
                cols = slice(sl * LANES, (sl + 1) * LANES)
                ag = ag_ref[:, cols]
                y_ref[:, cols] = (o * (ag * _sigmoid(ag))).astype(BF16)

    wide, per_kv = _slab_rows(D // LANES, tm), _slab_rows(NKV, tm)
    return pl.pallas_call(
        body, grid=(S // tm,), name="attn_combine",
        in_specs=[wide] * 3 + [per_kv] * 3 + [_rows(tm, D)],
        out_specs=[wide, per_kv, _rows(tm, D)],
        out_shape=[_slabs(D // LANES), _slabs(NKV), jax.ShapeDtypeStruct((S, D), BF16)],
        compiler_params=_params(("arbitrary",)),
    )(*outs, *lses, a_gate)


def _head_sum_selectors():
    lane_in = jnp.arange(LANES)[:, None] // HD
    return jnp.stack([jnp.broadcast_to(lane_in == h, (LANES, LANES)) for h in range(2)]).astype(BF16)


def _attn_gate_bwd(dy_att, o, a_gate, selectors):
    tm = 256

    def body(dy_ref, o_ref, ag_ref, e_ref, do_ref, dag_ref, delta_ref):
        for j in range(NKV):
            deltas = []
            for sl in (2 * j, 2 * j + 1):
                cols = slice(sl * LANES, (sl + 1) * LANES)
                dy, ag, o_ = dy_ref[:, cols], ag_ref[:, cols], o_ref[sl]
                sg = _sigmoid(ag)
                do = dy * (ag * sg)
                do_ref[sl] = do
                dag_ref[:, cols] = (dy * o_ * (sg * (1.0 + ag * (1.0 - sg)))).astype(BF16)
                prod = do * o_
                hi = prod.astype(BF16)
                lo = (prod - hi.astype(F32)).astype(BF16)
                deltas += [_dot(hi, e_ref[h]) + _dot(lo, e_ref[h]) for h in range(2)]
            delta_ref[j] = _minus_in_pieces(_by_head(deltas))

    return pl.pallas_call(
        body, grid=(S // tm,), name="attn_gate_bwd",
        in_specs=[_rows(tm, D), _slab_rows(D // LANES, tm), _rows(tm, D), _resident((2, LANES, LANES))],
        out_specs=[_slab_rows(D // LANES, tm), _rows(tm, D), _slab_rows(NKV, tm)],
        out_shape=[_slabs(D // LANES), jax.ShapeDtypeStruct((S, D), BF16), _slabs(NKV)],
        compiler_params=_params(("arbitrary",)),
    )(dy_att, o, a_gate, selectors)


def _own_pieces(tile):
    lane = lax.broadcasted_iota(jnp.int32, tile.shape, 1)
    head = jnp.where(lane < PIECES * GQ, lax.bitwise_and(lane, GQ - 1), -1)
    return jnp.concatenate([jnp.where(head == g, tile, 0.0) for g in range(GQ)], axis=0).astype(BF16)


def _attn_bwd(q, k, v, do, lse, delta, bias, d):
    def body(q_ref, do_ref, l_ref, dl_ref, k_ref, v_ref, b_ref, dq_ref, dkv_ref, acc):
        odd = pl.program_id(0) % 2 == 1
        chunk = pl.program_id(1)
        ones = (lax.broadcasted_iota(jnp.int32, (2 * BLK, LANES), 1) < PIECES * GQ).astype(BF16)

        def in_acc(block_idx):
            return pl.ds(pl.multiple_of(block_idx * BLK, BLK), BLK)

        @pl.when(chunk == 0)
        def _():
            acc[...] = jnp.zeros_like(acc)

        def block(idx, carry):
            idx = chunk * BLOCKS_PER_CHUNK + idx
            b, start, pstart = _block_start(idx, d)
            rows, prow = _sub_rows(start, d), _sub_rows(pstart, d)
            mine = _sub_rows(start - chunk * CHUNK_ROWS, d)
            qs = _stack_heads(q_ref, mine)
            dos = _stack_heads(do_ref, mine)
            kw = _kv_dup(k_ref, prow, rows, odd)
            vw = _kv_dup(v_ref, prow, rows, odd)
            s = _dot_nt(jnp.concatenate([qs, _own_pieces(l_ref[0, mine, :])], axis=1),
                        jnp.concatenate([kw, ones], axis=1)) + b_ref[jnp.minimum(b, 1), 0]
            p = jnp.exp(s)
            dv2 = _dot_tn(p.astype(BF16), dos)
            dp = _dot_nt(jnp.concatenate([dos, _own_pieces(dl_ref[0, mine, :])], axis=1),
                         jnp.concatenate([vw, ones], axis=1))
            ds = (p * dp).astype(BF16)
            dq_ref[0, mine, :], dq_ref[1, mine, :] = _unstack_heads(_dot(ds, kw))
            dk2 = _dot_tn(ds, qs)
            dkv = jnp.where(_low_lanes(2 * BLK), dk2 + pltpu.roll(dk2, HD, axis=1), dv2 + pltpu.roll(dv2, HD, axis=1))
            acc[in_acc(idx), :] = acc[in_acc(idx), :] + dkv[BLK:]
            before = jnp.where(b >= 1, idx - d, idx)
            acc[in_acc(before), :] = acc[in_acc(before), :] + dkv[:BLK]
            return carry

        lax.fori_loop(0, BLOCKS_PER_CHUNK, block, 0, unroll=8)

        @pl.when(chunk == S // CHUNK_ROWS - 1)
        def _():
            def place(idx, carry):
                _, start, _ = _block_start(idx, d)
                dkv_ref[0, _sub_rows(start, d), :] = acc[in_acc(idx), :]
                return carry

            lax.fori_loop(0, S // BLK, place, 0, unroll=4)

    q_like = pl.BlockSpec((2, CHUNK_ROWS, LANES), lambda j, c: (j, c, 0))
    pieces = pl.BlockSpec((1, CHUNK_ROWS, LANES), lambda j, c: (j, c, 0))
    kv = pl.BlockSpec((1, S, LANES), lambda j, c: (j // 2, 0, 0))
    per_kv = pl.BlockSpec((1, S, LANES), lambda j, c: (j, 0, 0))
    bias_spec = pl.BlockSpec((2, 1, GQ * BLK, 2 * BLK), lambda j, c: (0, j, 0, 0))
    return pl.pallas_call(
        body, grid=(NKV, S // CHUNK_ROWS), name=f"attn_bwd_d{d}",
        in_specs=[q_like, q_like, pieces, pieces, kv, kv, bias_spec],
        out_specs=[q_like, per_kv],
        out_shape=[_slabs(D // LANES), _slabs(NKV)],
        scratch_shapes=[pltpu.VMEM((S, LANES), F32)],
        compiler_params=_params(("arbitrary", "arbitrary")),
    )(q, do, lse, delta, k, v, bias)


CONV_T = 256


def _halo_before(i):
    return (jnp.maximum(i * (CONV_T // HALO) - 1, 0), 0)


def _halo_after(i):
    return (jnp.minimum((i + 1) * (CONV_T // HALO), S // HALO - 1), 0)


SUBLANES = 8
NCH = D // LANES
GROUP = SUBLANES * SUBLANES


def _comb(ref, cb, base):
    return ref[cb, pl.ds(base, SUBLANES, stride=SUBLANES), :]


def _taps(w_ref, cols):
    return [jnp.broadcast_to(w_ref[j:j + 1, cols], (SUBLANES, LANES)) for j in range(CONV_K)]


def _conv_fwd(c_val, c_glu, c_gate, conv_w, conv_b, ln_g, ln_b):
    T = CONV_T

    def body(cv_ref, cg_ref, cvh_ref, cgh_ref, gate_ref, w_ref, b_ref, lg_ref, lb_ref, u_ref, y_ref, win, us):
        i = pl.program_id(0)
        for cb in range(NCH):
            cols = slice(cb * LANES, (cb + 1) * LANES)
            win[cb, HALO:HALO + T, :] = cv_ref[:, cols] * _sigmoid(cg_ref[:, cols])
            win[cb, 0:HALO, :] = jnp.where(i > 0, cvh_ref[:, cols] * _sigmoid(cgh_ref[:, cols]), 0.0)
        for cb in range(NCH):
            cols = slice(cb * LANES, (cb + 1) * LANES)
            taps = _taps(w_ref, cols)
            bias = jnp.broadcast_to(b_ref[:, cols], (SUBLANES, LANES))

            def group(g, carry):
                for b in range(SUBLANES):
                    base = g * GROUP + b
                    acc = bias
                    for j in range(CONV_K):
                        acc = acc + taps[j] * _comb(win, cb, base + (HALO - (CONV_K - 1) + j))
                    us[cb, pl.ds(base, SUBLANES, stride=SUBLANES), :] = acc
                return carry

            lax.fori_loop(0, T // GROUP, group, 0)
        total = us[0]
        for cb in range(1, NCH):
            total = total + us[cb]
        mu = jnp.sum(total, axis=-1, keepdims=True) * (1.0 / D)
        sq = jnp.zeros((T, LANES), F32)
        for cb in range(NCH):
            uc = us[cb] - mu
            sq = sq + uc * uc
        rstd = lax.rsqrt(jnp.sum(sq, axis=-1, keepdims=True) * (1.0 / D) + LN_EPS)
        for cb in range(NCH):
            cols = slice(cb * LANES, (cb + 1) * LANES)
            u = us[cb]
            u_ref[:, cols] = u
            nrm = (u - mu) * rstd * lg_ref[:, cols] + lb_ref[:, cols]
            gate = gate_ref[:, cols]
            y_ref[:, cols] = (nrm * _sigmoid(nrm) * (gate * _sigmoid(gate))).astype(BF16)

    halo = pl.BlockSpec((HALO, D), _halo_before)
    return pl.pallas_call(
        body, grid=(S // T,), name="conv_fwd",
        in_specs=[_rows(T, D), _rows(T, D), halo, halo, _rows(T, D),
                  _resident((HALO, D)), _resident((1, D)), _resident((1, D)), _resident((1, D))],
        out_specs=[_rows(T, D), _rows(T, D)],
        out_shape=[jax.ShapeDtypeStruct((S, D), F32), jax.ShapeDtypeStruct((S, D), BF16)],
        scratch_shapes=[pltpu.VMEM((NCH, T + HALO, LANES), F32), pltpu.VMEM((NCH, T, LANES), F32)],
        compiler_params=_params(("arbitrary",)),
    )(c_val, c_glu, c_val, c_glu, c_gate, conv_w, conv_b, ln_g, ln_b)


def _conv_bwd_rows(u, c_gate, dy_conv, ln_g, ln_b):
    tm = 256

    def body(u_ref, gate_ref, dy_ref, lg_ref, lb_ref, du_ref, dgate_ref, st_ref):
        @pl.when(pl.program_id(0) == 0)
        def _():
            st_ref[...] = jnp.zeros_like(st_ref)

        u, gate, dy = u_ref[...], gate_ref[...], dy_ref[...]
        mu = jnp.mean(u, axis=-1, keepdims=True)
        uc = u - mu
        rstd = lax.rsqrt(jnp.mean(uc * uc, axis=-1, keepdims=True) + LN_EPS)
        z = uc * rstd
        nrm = z * lg_ref[...] + lb_ref[...]
        sn, sg = _sigmoid(nrm), _sigmoid(gate)
        dgate_ref[...] = (dy * (nrm * sn) * (sg * (1.0 + gate * (1.0 - sg)))).astype(BF16)
        dn = dy * (gate * sg) * (sn * (1.0 + nrm * (1.0 - sn)))
        dz = dn * lg_ref[...]
        du = rstd * (dz - jnp.mean(dz, axis=-1, keepdims=True) - z * jnp.mean(dz * z, axis=-1, keepdims=True))
        du_ref[...] = du
        st_ref[0:1, :] += jnp.sum(dn * z, axis=0, keepdims=True)
        st_ref[1:2, :] += jnp.sum(dn, axis=0, keepdims=True)
        st_ref[2:3, :] += jnp.sum(du, axis=0, keepdims=True)

    big = jax.ShapeDtypeStruct((S, D), F32)
    return pl.pallas_call(
        body, grid=(S // tm,), name="conv_bwd_rows",
        in_specs=[_rows(tm, D)] * 3 + [_resident((1, D)), _resident((1, D))],
        out_specs=[_rows(tm, D), _rows(tm, D), pl.BlockSpec((8, D), lambda i: (0, 0))],
        out_shape=[big, jax.ShapeDtypeStruct((S, D), BF16), jax.ShapeDtypeStruct((8, D), F32)],
        compiler_params=_params(("arbitrary",)),
    )(u, c_gate, dy_conv, ln_g, ln_b)


def _conv_bwd_taps(du, c_val, c_glu, conv_w):
    T = CONV_T
    last = S // T - 1

    def body(du_ref, dua_ref, cv_ref, cg_ref, cvh_ref, cgh_ref, w_ref, dcv_ref, dcg_ref, dw_ref,
             hwin, dwin, dhs, dw_acc):
        i = pl.program_id(0)

        @pl.when(i == 0)
        def _():
            dw_acc[...] = jnp.zeros_like(dw_acc)

        for cb in range(NCH):
            cols = slice(cb * LANES, (cb + 1) * LANES)
            hwin[cb, HALO:HALO + T, :] = cv_ref[:, cols] * _sigmoid(cg_ref[:, cols])
            hwin[cb, 0:HALO, :] = jnp.where(i > 0, cvh_ref[:, cols] * _sigmoid(cgh_ref[:, cols]), 0.0)
            dwin[cb, 0:T, :] = du_ref[:, cols]
            dwin[cb, T:T + HALO, :] = jnp.where(i < last, dua_ref[:, cols], 0.0)
        for cb in range(NCH):
            cols = slice(cb * LANES, (cb + 1) * LANES)
            taps = _taps(w_ref, cols)

            def group_dh(g, carry):
                for b in range(SUBLANES):
                    base = g * GROUP + b
                    acc = jnp.zeros((SUBLANES, LANES), F32)
                    for j in range(CONV_K):
                        acc = acc + taps[j] * _comb(dwin, cb, base + (CONV_K - 1 - j))
                    dhs[cb, pl.ds(base, SUBLANES, stride=SUBLANES), :] = acc
                return carry

            lax.fori_loop(0, T // GROUP, group_dh, 0)

            def group_dw(g, sums):
                for b in range(SUBLANES):
                    base = g * GROUP + b
                    d = _comb(dwin, cb, base)
                    sums = tuple(sums[j] + d * _comb(hwin, cb, base + (HALO - (CONV_K - 1) + j))
                                 for j in range(CONV_K))
                return sums

            sums = lax.fori_loop(0, T // GROUP, group_dw, tuple(dw_acc[j, :, cols] for j in range(CONV_K)))
            for j in range(CONV_K):
                dw_acc[j, :, cols] = sums[j]
            dh = dhs[cb]
            cv, sg = cv_ref[:, cols], _sigmoid(cg_ref[:, cols])
            dcv_ref[:, cols] = (dh * sg).astype(BF16)
            dcg_ref[:, cols] = (dh * cv * (sg * (1.0 - sg))).astype(BF16)

        @pl.when(i == last)
        def _():
            dw_ref[...] = jnp.zeros_like(dw_ref)
            for j in range(CONV_K):
                dw_ref[j:j + 1, :] = jnp.sum(dw_acc[j], axis=0, keepdims=True)

    before = pl.BlockSpec((HALO, D), _halo_before)
    after = pl.BlockSpec((HALO, D), _halo_after)
    big = jax.ShapeDtypeStruct((S, D), BF16)
    return pl.pallas_call(
        body, grid=(S // T,), name="conv_bwd_taps",
        in_specs=[_rows(T, D), after, _rows(T, D), _rows(T, D), before, before, _resident((HALO, D))],
        out_specs=[_rows(T, D), _rows(T, D), pl.BlockSpec((HALO, D), lambda i: (0, 0))],
        out_shape=[big, big, jax.ShapeDtypeStruct((HALO, D), F32)],
        scratch_shapes=[pltpu.VMEM((NCH, T + HALO, LANES), F32), pltpu.VMEM((NCH, T + HALO, LANES), F32),
                        pltpu.VMEM((NCH, T, LANES), F32), pltpu.VMEM((CONV_K, SUBLANES, D), F32)],
        compiler_params=_params(("arbitrary",)),
    )(du, du, c_val, c_glu, c_val, c_glu, conv_w)


def _outproj_loss(y_att, y_conv, w_out_bf, x, target, gf):
    tm = 256

    def body(ya_ref, yc_ref, w_ref, x_ref, t_ref, gf_ref, dx2_ref, dya_ref, dyc_ref, dw_ref, st_ref, acc):
        @pl.when(pl.program_id(0) == 0)
        def _():
            acc[...] = jnp.zeros_like(acc)
            st_ref[...] = jnp.zeros_like(st_ref)

        ya, yc = ya_ref[...], yc_ref[...]
        x2 = x_ref[...] + _dot(ya, w_ref[0:D, :]) + _dot(yc, w_ref[D:2 * D, :])
        r = lax.rsqrt(jnp.mean(x2 * x2, axis=-1, keepdims=True) + NORM_EPS)
        xn = x2 * r
        err = xn * gf_ref[...] - t_ref[...]
        dout = err * (1.0 / D)
        dxn = dout * gf_ref[...]
        dx2 = r * (dxn - xn * jnp.mean(dxn * xn, axis=-1, keepdims=True))
        dx2_ref[...] = dx2
        dx2b = dx2.astype(BF16)
        dya_ref[...] = _dot_nt(dx2b, w_ref[0:D, :])
        dyc_ref[...] = _dot_nt(dx2b, w_ref[D:2 * D, :])
        acc[0:D, :] += _dot_tn(ya, dx2b)
        acc[D:2 * D, :] += _dot_tn(yc, dx2b)
        st_ref[0:1, :] += jnp.sum(dout * xn, axis=0, keepdims=True)
        st_ref[1:2, :] += jnp.sum(err * err, axis=0, keepdims=True) * (0.5 / D)

        @pl.when(pl.program_id(0) == S // tm - 1)
        def _():
            dw_ref[...] = acc[...].astype(BF16)

    big = jax.ShapeDtypeStruct((S, D), F32)
    return pl.pallas_call(
        body, grid=(S // tm,), name="outproj_loss",
        in_specs=[_rows(tm, D), _rows(tm, D), _resident((WOUT_ROWS, D)), _rows(tm, D), _rows(tm, D), _resident((1, D))],
        out_specs=[_rows(tm, D), _rows(tm, D), _rows(tm, D),
                   pl.BlockSpec((WOUT_ROWS, D), lambda i: (0, 0)), pl.BlockSpec((8, D), lambda i: (0, 0))],
        out_shape=[big, big, big, jax.ShapeDtypeStruct((WOUT_ROWS, D), BF16), jax.ShapeDtypeStruct((8, D), F32)],
        scratch_shapes=[pltpu.VMEM((WOUT_ROWS, D), F32)],
        compiler_params=_params(("arbitrary",)),
    )(y_att, y_conv, w_out_bf, x, target, gf)


UNITS_PER_CHUNK = CHUNK // LANES


def _dproj_unit(u, dqs, dkvs, gates, rows):
    if u < OFF_K // LANES:
        return ((dqs[0][u] + dqs[1][u] + dqs[2][u]) * (HD ** -0.5)).astype(BF16)
    if u < OFF_AG // LANES:
        w = u - OFF_K // LANES
        ta, tb = (dkvs[0][j] + dkvs[1][j] + dkvs[2][j] for j in (2 * (w % 2), 2 * (w % 2) + 1))
        low = _low_lanes(rows)
        if w < 2:
            return jnp.where(low, ta, pltpu.roll(tb, HD, axis=1)).astype(BF16)
        return jnp.where(low, pltpu.roll(ta, HD, axis=1), tb).astype(BF16)
    g, sl = divmod(u - OFF_AG // LANES, D // LANES)
    return gates[g][:, sl * LANES:(sl + 1) * LANES]


def _dproj_sources(units, dqs, dkvs, gates, rows):
    use_q = any(u < OFF_K // LANES for u in units)
    use_kv = any(OFF_K // LANES <= u < OFF_AG // LANES for u in units)
    use_g = sorted({(u - OFF_AG // LANES) // (D // LANES) for u in units if u >= OFF_AG // LANES})
    args = (list(dqs) if use_q else []) + (list(dkvs) if use_kv else []) + [gates[g] for g in use_g]
    specs = ([_slab_rows(D // LANES, rows)] * 3 if use_q else []) + ([_slab_rows(NKV, rows)] * 3 if use_kv else []) \
        + [_rows(rows, D)] * len(use_g)

    def pick(refs):
        refs = list(refs)
        q_refs = [refs.pop(0) for _ in range(3)] if use_q else None
        kv_refs = [refs.pop(0) for _ in range(3)] if use_kv else None
        return q_refs, kv_refs, {g: refs.pop(0) for g in use_g}

    return args, specs, pick


def _inproj_bwd_x(dqs, dkvs, gates, w_bf, x, g1, dx2, pi, po, ps):
    tm = 256
    last = S // tm - 1
    units = range(NCOL // LANES)
    pieces, piece_specs, pick = _dproj_sources(units, dqs, dkvs, gates, tm)

    def body(*refs):
        piece_refs, refs = refs[:len(pieces)], refs[len(pieces):]
        (w_ref, x_ref, g_ref, dx2_ref, pi_ref, po_ref, ps_ref,
         gx_ref, st_ref, ri_ref, ro_ref, rs_ref, dp_ref, send, recv) = refs
        i = pl.program_id(0)
        copies = _chip_exchange_copies((pi_ref, po_ref, ps_ref), (ri_ref, ro_ref, rs_ref), send, recv)

        @pl.when(i == 0)
        def _():
            st_ref[...] = jnp.zeros_like(st_ref)
            for out, _ in copies:
                out.start()

        sources = pick(piece_refs)
        for u in units:
            dp_ref[:, u * LANES:(u + 1) * LANES] = _dproj_unit(u, *sources, tm)
        dh = _dot_nt(dp_ref[...], w_ref[...])
        xt = x_ref[...]
        r = lax.rsqrt(jnp.mean(xt * xt, axis=-1, keepdims=True) + NORM_EPS)
        xn = xt * r
        dxn = dh * g_ref[...]
        gx_ref[...] = dx2_ref[...] + r * (dxn - xn * jnp.mean(dxn * xn, axis=-1, keepdims=True))
        st_ref[0:1, :] += jnp.sum(dh * xn, axis=0, keepdims=True)

        @pl.when(i == last)
        def _():
            for _, arrival in copies:
                arrival.wait_recv()
            for out, _ in copies:
                out.wait_send()

    n = 3 * len(CHIP_FLIPS)
    return pl.pallas_call(
        body, grid=(S // tm,), name="inproj_bwd_x",
        in_specs=piece_specs + [_resident((D, NCOL)), _rows(tm, D), _resident((1, D)), _rows(tm, D), ANY, ANY, ANY],
        out_specs=[_rows(tm, D), pl.BlockSpec((8, D), lambda i: (0, 0)), ANY, ANY, ANY],
        out_shape=[jax.ShapeDtypeStruct((S, D), F32), jax.ShapeDtypeStruct((8, D), F32),
                   jax.ShapeDtypeStruct((NCHIP, D // 2, CHUNK), BF16),
                   jax.ShapeDtypeStruct((NCHIP, WOUT_SHARD // 2, D), BF16),
                   jax.ShapeDtypeStruct((NCHIP, SMALL_ROWS, D), F32)],
        scratch_shapes=[pltpu.VMEM((tm, NCOL), BF16), pltpu.SemaphoreType.DMA((n,)), pltpu.SemaphoreType.DMA((n,))],
        compiler_params=_params(("arbitrary",)),
    )(*pieces, w_bf, x, g1, dx2, pi, po, ps)


def _inproj_bwd_w(h, dqs, dkvs, gates):
    out = None
    for k in range(NCHIP):
        units = range(k * UNITS_PER_CHUNK, (k + 1) * UNITS_PER_CHUNK)
        tk = 512 if units[0] < OFF_K // LANES else 1024
        nk = S // tk
        pieces, piece_specs, pick = _dproj_sources(units, dqs, dkvs, gates, tk)
        handed_on = [] if out is None else [out]

        def body(*refs, units=units, pick=pick, n_pieces=len(pieces), n_in=1 + len(pieces) + len(handed_on)):
            h_ref, piece_refs = refs[0], refs[1:1 + n_pieces]
            o_ref, tile, acc = refs[n_in:]
            i = pl.program_id(0)

            @pl.when(i == 0)
            def _():
                acc[...] = jnp.zeros_like(acc)

            sources = pick(piece_refs)
            for n, u in enumerate(units):
                tile[:, n * LANES:(n + 1) * LANES] = _dproj_unit(u, *sources, tk)
            acc[...] += _dot_tn(h_ref[...], tile[...])

            @pl.when(i == nk - 1)
            def _():
                o_ref[0] = acc[...].astype(BF16)

        out = pl.pallas_call(
            body, grid=(nk,), name=f"inproj_bwd_w{k}",
            in_specs=[_rows(tk, D)] + piece_specs + [ANY] * len(handed_on),
            out_specs=pl.BlockSpec((1, D, CHUNK), lambda i, k=k: (k, 0, 0)),
            out_shape=jax.ShapeDtypeStruct((NCHIP, D, CHUNK), BF16),
            input_output_aliases={1 + len(pieces): 0} if handed_on else {},
            scratch_shapes=[pltpu.VMEM((tk, CHUNK), BF16), pltpu.VMEM((D, CHUNK), F32)],
            compiler_params=_params(("arbitrary",)),
        )(h, *pieces, *handed_on)
    return out


def _local_step(x, target, g1, w_in_bf, conv_w, conv_b, ln_g, ln_b, w_out_bf, gf):
    h, q, k, v, a_gate, c_val, c_glu, c_gate = _inproj_fwd(x, g1, w_in_bf)
    tables = [_bias_table(d) for d in PATTERNS]
    outs, lses = zip(*[_attn_fwd(q, k, v, t, d) for t, d in zip(tables, PATTERNS)])
    o, lse, y_att = _attn_combine(outs, lses, a_gate)
    u, y_conv = _conv_fwd(c_val, c_glu, c_gate, conv_w, conv_b, ln_g, ln_b)
    dx2, dy_att, dy_conv, dw_out, st_out = _outproj_loss(y_att, y_conv, w_out_bf, x, target, gf)

    do, da_gate, delta = _attn_gate_bwd(dy_att, o, a_gate, _head_sum_selectors())
    dqs, dkvs = zip(*[_attn_bwd(q, k, v, do, lse, delta, t, d) for t, d in zip(tables, PATTERNS)])

    du, dc_gate, st_conv = _conv_bwd_rows(u, c_gate, dy_conv, ln_g, ln_b)
    dc_val, dc_glu, dconv_w = _conv_bwd_taps(du, c_val, c_glu, conv_w)

    dproj_pieces = (dqs, dkvs, (da_gate, dc_val, dc_glu, dc_gate))
    dw_in = _inproj_bwd_w(h, *dproj_pieces)
    small = jnp.concatenate([st_conv, st_out, dconv_w], axis=0)
    return dw_in, dw_out, small, dproj_pieces, dx2


ROW_LN_G, ROW_LN_B, ROW_CONV_B, ROW_FINAL_G, ROW_LOSS, ROW_TAPS = 0, 1, 2, 8, 9, 16
SMALL_ROWS = 16 + HALO
NDEV = 8


MESH = pl.DeviceIdType.MESH
ANY = pl.BlockSpec(memory_space=pl.ANY)
CHIP_FLIPS = ((1, 0), (0, 1), (1, 1))


def _pos():
    return lax.axis_index("x"), lax.axis_index("y"), lax.axis_index("c")


def _flip(v, f):
    return 1 - v if f else v


def _ds(start, size, align=None):
    return pl.ds(pl.multiple_of(start, align or size), size)


def _place_shards(wi, wo, cw, where):
    steps = 4

    def body(where_ref, wi_ref, wo_ref, cw_ref, wi_full, wo_full, cw_full):
        wi_full[...] = wi_ref[...].astype(BF16)
        wo_full[...] = wo_ref[...].astype(BF16)
        cw_full[...] = cw_ref[...]

    grid_spec = pltpu.PrefetchScalarGridSpec(
        num_scalar_prefetch=1, grid=(steps,),
        in_specs=[pl.BlockSpec((D // steps, CHUNK), lambda i, w: (i, 0)),
                  pl.BlockSpec((WOUT_SHARD // steps, D), lambda i, w: (i, 0)),
                  pl.BlockSpec((HALO, CONVW_SHARD), lambda i, w: (0, 0))],
        out_specs=[pl.BlockSpec((D // steps, CHUNK), lambda i, w: (i, w[0])),
                   pl.BlockSpec((WOUT_SHARD // steps, D), lambda i, w: (w[0] * steps + i, 0)),
                   pl.BlockSpec((HALO, CONVW_SHARD), lambda i, w: (0, w[0]))])
    return pl.pallas_call(
        body, grid_spec=grid_spec, name="place_shards",
        out_shape=[jax.ShapeDtypeStruct((D, NCOL), BF16), jax.ShapeDtypeStruct((WOUT_ROWS, D), BF16),
                   jax.ShapeDtypeStruct((HALO, D), F32)],
        compiler_params=_params(("arbitrary",)),
    )(where, wi, wo, cw)


def _gather_weights(wi_full, wo_full, cw_full):
    halves = (D // 2, WOUT_SHARD // 2, HALO // 2)
    OWN_X, OWN_Y, VIA_Y, VIA_X = range(4)

    def body(_wi, _wo, _cw, wi_full, wo_full, cw_full, send, recv):
        x, y, c = _pos()
        x_nbr, y_nbr, diag = (1 - x, y), (x, 1 - y), (1 - x, 1 - y)

        def region(a, chip_xy, half, part=None):
            chip = 2 * chip_xy[0] + chip_xy[1]
            n, row = halves[a], half * halves[a]
            if part is not None:
                n = n // 2
                row = row + part * n
            if a == 0:
                return wi_full.at[_ds(row, n), _ds(chip * CHUNK, CHUNK, 128)]
            if a == 1:
                return wo_full.at[_ds(chip * WOUT_SHARD + row, n), :]
            return cw_full.at[_ds(row, n), _ds(chip * CONVW_SHARD, CONVW_SHARD, 128)]

        def copy(a, kind, piece, dev):
            k = 8 * a + kind
            return pltpu.make_async_remote_copy(src_ref=piece, dst_ref=piece, send_sem=send.at[k], recv_sem=recv.at[k],
                                                device_id=dev, device_id_type=MESH)

        def to_sibling(a, kind, piece):
            cp = copy(a, 4 + kind, piece, (x, y, 1 - c))
            cp.start()
            return cp

        sends = []
        for a in range(3):
            for kind, nbr in ((OWN_X, x_nbr), (OWN_Y, y_nbr)):
                cp = copy(a, kind, region(a, (x, y), c), (*nbr, c))
                cp.start()
                sends.append(cp)
        for a in range(3):
            got = region(a, x_nbr, c)
            copy(a, OWN_X, got, (*x_nbr, c)).wait_recv()
            onward = copy(a, VIA_Y, region(a, x_nbr, c, 0), (*y_nbr, c))
            onward.start()
            sends += [onward, to_sibling(a, OWN_X, got)]
            got = region(a, y_nbr, c)
            copy(a, OWN_Y, got, (*y_nbr, c)).wait_recv()
            onward = copy(a, VIA_X, region(a, y_nbr, c, 1), (*x_nbr, c))
            onward.start()
            sends += [onward, to_sibling(a, OWN_Y, got)]
        for a in range(3):
            got = region(a, diag, c, 0)
            copy(a, VIA_Y, got, (*y_nbr, c)).wait_recv()
            sends.append(to_sibling(a, VIA_Y, got))
            got = region(a, diag, c, 1)
            copy(a, VIA_X, got, (*x_nbr, c)).wait_recv()
            sends.append(to_sibling(a, VIA_X, got))
        for a in range(3):
            for kind, piece in ((OWN_X, region(a, x_nbr, 1 - c)), (OWN_Y, region(a, y_nbr, 1 - c)),
                                (VIA_Y, region(a, diag, 1 - c, 0)), (VIA_X, region(a, diag, 1 - c, 1))):
                copy(a, 4 + kind, piece, (x, y, 1 - c)).wait_recv()
        for cp in sends:
            cp.wait_send()

    n_sems = 3 * 8
    return pl.pallas_call(
        body, name="gather_weights",
        in_specs=[ANY, ANY, ANY], out_specs=[ANY, ANY, ANY], input_output_aliases={0: 0, 1: 1, 2: 2},
        out_shape=[jax.ShapeDtypeStruct((D, NCOL), BF16), jax.ShapeDtypeStruct((WOUT_ROWS, D), BF16),
                   jax.ShapeDtypeStruct((HALO, D), F32)],
        scratch_shapes=[pltpu.SemaphoreType.DMA((n_sems,)), pltpu.SemaphoreType.DMA((n_sems,))],
    )(wi_full, wo_full, cw_full)


def _exchange_halves(gi4, go4, small):
    def body(gi_ref, go_ref, sm_ref, ri_ref, ro_ref, rs_ref, send, recv):
        x, y, c = _pos()
        sib = (x, y, 1 - c)
        copies = [
            (gi_ref.at[:, _ds((1 - c) * (D // 2), D // 2), :], ri_ref),
            (go_ref.at[:, _ds((1 - c) * (WOUT_SHARD // 2), WOUT_SHARD // 2), :], ro_ref),
            (sm_ref, rs_ref),
        ]
        cps = [pltpu.make_async_remote_copy(src_ref=s_, dst_ref=d_, send_sem=send.at[k], recv_sem=recv.at[k],
                                            device_id=sib, device_id_type=MESH) for k, (s_, d_) in enumerate(copies)]
        for cp in cps:
            cp.start()
        for cp in cps:
            cp.wait()

    return pl.pallas_call(
        body, name="exchange_halves",
        in_specs=[ANY, ANY, ANY], out_specs=[ANY, ANY, ANY],
        out_shape=[jax.ShapeDtypeStruct((NCHIP, D // 2, CHUNK), BF16),
                   jax.ShapeDtypeStruct((NCHIP, WOUT_SHARD // 2, D), BF16),
                   jax.ShapeDtypeStruct((SMALL_ROWS, D), F32)],
        scratch_shapes=[pltpu.SemaphoreType.DMA((3,)), pltpu.SemaphoreType.DMA((3,))],
    )(gi4, go4, small)


def _add_halves(gi4, ri, go4, ro, small, rs):
    hi, ho = D // 2, WOUT_SHARD // 2

    def body(gi_ref, ri_ref, go_ref, ro_ref, sm_ref, rs_ref, pi_ref, po_ref, ps_ref):
        c = lax.axis_index("c")
        pi_ref[0] = (gi_ref[0, _ds(c * hi, hi), :].astype(F32) + ri_ref[0].astype(F32)).astype(BF16)
        po_ref[0] = (go_ref[0, _ds(c * ho, ho), :].astype(F32) + ro_ref[0].astype(F32)).astype(BF16)
        ps_ref[...] = sm_ref[...] + rs_ref[...]

    blk = lambda n, w: pl.BlockSpec((1, n, w), lambda k: (k, 0, 0))
    whole = pl.BlockSpec((SMALL_ROWS, D), lambda k: (0, 0))
    return pl.pallas_call(
        body, grid=(NCHIP,), name="add_halves",
        in_specs=[blk(D, CHUNK), blk(hi, CHUNK), blk(WOUT_SHARD, D), blk(ho, D), whole, whole],
        out_specs=[blk(hi, CHUNK), blk(ho, D), whole],
        out_shape=[jax.ShapeDtypeStruct((NCHIP, hi, CHUNK), BF16), jax.ShapeDtypeStruct((NCHIP, ho, D), BF16),
                   jax.ShapeDtypeStruct((SMALL_ROWS, D), F32)],
        compiler_params=_params(("arbitrary",)),
    )(gi4, ri, go4, ro, small, rs)


def _chip_exchange_copies(srcs, dsts, send, recv):
    x, y, c = _pos()
    me = 2 * x + y
    pairs = []
    for a in range(3):
        for j, (fx, fy) in enumerate(CHIP_FLIPS):
            px, py = _flip(x, fx), _flip(y, fy)
            peer = 2 * px + py
            k = 3 * a + j
            out = pltpu.make_async_remote_copy(
                src_ref=srcs[a] if a == 2 else srcs[a].at[peer], dst_ref=dsts[a].at[me],
                send_sem=send.at[k], recv_sem=recv.at[k], device_id=(px, py, c), device_id_type=MESH)
            got = dsts[a].at[peer]
            arrival = pltpu.make_async_remote_copy(
                src_ref=got, dst_ref=got, send_sem=send.at[k], recv_sem=recv.at[k],
                device_id=(px, py, c), device_id_type=MESH)
            pairs.append((out, arrival))
    return pairs


def _sum_chips(ri, ro, rs, pi, po, ps, where):
    def body(w_ref, ri_ref, ro_ref, rs_ref, pi_ref, po_ref, ps_ref, gi_ref, go_ref, gs_ref, g5_ref, loss_ref,
             acc_i, acc_o, acc_s):
        k = pl.program_id(0)
        accs = (acc_i, acc_o, acc_s)

        @pl.when(k == 0)
        def _():
            for acc in accs:
                acc[...] = jnp.zeros_like(acc)

        @pl.when(k == w_ref[0])
        def _():
            for acc, val in zip(accs, (pi_ref[0], po_ref[0], ps_ref[...])):
                acc[...] += val.astype(F32)

        @pl.when(k != w_ref[0])
        def _():
            for acc, ref in zip(accs, (ri_ref, ro_ref, rs_ref)):
                acc[...] += ref[0].astype(F32)

        @pl.when(k == NCHIP - 1)
        def _():
            gi_ref[0] = acc_i[...]
            go_ref[0] = acc_o[...]
            gs_ref[...] = acc_s[...]
            g5_ref[...] = jnp.zeros_like(g5_ref)
            for i, row in enumerate((ROW_CONV_B, ROW_LN_G, ROW_LN_B, ROW_FINAL_G)):
                g5_ref[i + 1:i + 2, :] = acc_s[row:row + 1, :]
            loss = jnp.sum(acc_s[ROW_LOSS:ROW_LOSS + 1, :], axis=1, keepdims=True)
            loss_ref[...] = jnp.broadcast_to(loss, loss_ref.shape)

    def sent(k, w):
        return jnp.where(k == w[0], (k + 1) % NCHIP, k)

    hi, ho = D // 2, WOUT_SHARD // 2
    const = lambda shape: pl.BlockSpec(shape, lambda k, w: (0,) * len(shape))
    grid_spec = pltpu.PrefetchScalarGridSpec(
        num_scalar_prefetch=1, grid=(NCHIP,),
        in_specs=[pl.BlockSpec((1, hi, CHUNK), lambda k, w: (sent(k, w), 0, 0)),
                  pl.BlockSpec((1, ho, D), lambda k, w: (sent(k, w), 0, 0)),
                  pl.BlockSpec((1, SMALL_ROWS, D), lambda k, w: (sent(k, w), 0, 0)),
                  pl.BlockSpec((1, hi, CHUNK), lambda k, w: (w[0], 0, 0)),
                  pl.BlockSpec((1, ho, D), lambda k, w: (w[0], 0, 0)),
                  const((SMALL_ROWS, D))],
        out_specs=[pl.BlockSpec((1, hi, CHUNK), lambda k, w: (w[1], 0, 0)),
                   pl.BlockSpec((1, ho, D), lambda k, w: (w[1], 0, 0)),
                   const((SMALL_ROWS, D)), const((8, D)), const((8, LANES))],
        scratch_shapes=[pltpu.VMEM((hi, CHUNK), F32), pltpu.VMEM((ho, D), F32), pltpu.VMEM((SMALL_ROWS, D), F32)])
    return pl.pallas_call(
        body, grid_spec=grid_spec, name="sum_chips",
        out_shape=[jax.ShapeDtypeStruct((2, hi, CHUNK), F32), jax.ShapeDtypeStruct((2, ho, D), F32),
                   jax.ShapeDtypeStruct((SMALL_ROWS, D), F32), jax.ShapeDtypeStruct((8, D), F32),
                   jax.ShapeDtypeStruct((8, LANES), F32)],
        compiler_params=_params(("arbitrary",)),
    )(where, ri, ro, rs, pi, po, ps)


def _exchange_results(gi2, go2, st):
    flips = [(fx, fy, fc) for fx in (0, 1) for fy in (0, 1) for fc in (0, 1)][1:]

    def body(_gi, _go, st_ref, gi_ref, go_ref, all_ref, send, recv, lsem):
        x, y, c = _pos()
        sib = (x, y, 1 - c)

        def half(k, ref, slot):
            return pltpu.make_async_remote_copy(src_ref=ref.at[slot], dst_ref=ref.at[slot], send_sem=send.at[k],
                                                recv_sem=recv.at[k], device_id=sib, device_id_type=MESH)

        def stat(k, src, slot, dev):
            return pltpu.make_async_remote_copy(src_ref=src, dst_ref=all_ref.at[slot], send_sem=send.at[k],
                                                recv_sem=recv.at[k], device_id=dev, device_id_type=MESH)

        mine = pltpu.make_async_copy(st_ref, all_ref.at[4 * x + 2 * y + c], lsem)
        mine.start()
        sends = [half(k, ref, c) for k, ref in enumerate((gi_ref, go_ref))]
        peers = [(_flip(x, fx), _flip(y, fy), _flip(c, fc)) for fx, fy, fc in flips]
        sends += [stat(2 + k, st_ref, 4 * x + 2 * y + c, dev) for k, dev in enumerate(peers)]
        for cp in sends:
            cp.start()
        for k, ref in enumerate((gi_ref, go_ref)):
            half(k, ref, 1 - c).wait_recv()
        for k, (px, py, pc) in enumerate(peers):
            slot = 4 * px + 2 * py + pc
            stat(2 + k, all_ref.at[slot], slot, (px, py, pc)).wait_recv()
        for cp in sends:
            cp.wait_send()
        mine.wait()

    n = 2 + len(flips)
    return pl.pallas_call(
        body, name="exchange_results",
        in_specs=[ANY, ANY, ANY], out_specs=[ANY, ANY, ANY], input_output_aliases={0: 0, 1: 1},
        out_shape=[jax.ShapeDtypeStruct((2, D // 2, CHUNK), F32), jax.ShapeDtypeStruct((2, WOUT_SHARD // 2, D), F32),
                   jax.ShapeDtypeStruct((NDEV, 8, D), F32)],
        scratch_shapes=[pltpu.SemaphoreType.DMA((n,)), pltpu.SemaphoreType.DMA((n,)), pltpu.SemaphoreType.DMA],
    )(gi2, go2, st)


def _adamw_math(w, g, m, v):
    m2 = ADAM_B1 * m + (1.0 - ADAM_B1) * g
    v2 = ADAM_B2 * v + (1.0 - ADAM_B2) * (g * g)
    m_hat = m2 / (1.0 - ADAM_B1 ** ADAM_STEP)
    v_hat = v2 / (1.0 - ADAM_B2 ** ADAM_STEP)
    delta = -ADAM_LR * (m_hat / (jnp.sqrt(v_hat) + ADAM_EPS) + ADAM_WD * w)
    return delta, m2, v2


def _adamw(w, g, m, v, name):
    rows, cols = w.shape
    tm = 256 if rows % 256 == 0 else rows

    def body(w_ref, g_ref, m_ref, v_ref, d_ref, m2_ref, v2_ref):
        d_ref[...], m2_ref[...], v2_ref[...] = _adamw_math(w_ref[...], g_ref[...], m_ref[...], v_ref[...])

    shape = jax.ShapeDtypeStruct(w.shape, F32)
    return pl.pallas_call(
        body, grid=(rows // tm,), name=name,
        in_specs=[_rows(tm, cols)] * 4, out_specs=[_rows(tm, cols)] * 3, out_shape=[shape] * 3,
        compiler_params=_params(("arbitrary",)),
    )(w, g, m, v)


def _adamw_vectors(g5, first_parts, ws, ms, vs):
    n = len(ws)

    def body(g_ref, parts_ref, *refs):
        ins, g0_ref, outs = refs[:3 * n], refs[3 * n], refs[3 * n + 1:]
        g0 = parts_ref[0, 0:1, :]
        for dev in range(1, NDEV):
            g0 = g0 + parts_ref[dev, 0:1, :]
        g0_ref[...] = g0
        for i in range(n):
            g = g0 if i == 0 else g_ref[i:i + 1, :]
            res = _adamw_math(ins[i][...], g, ins[n + i][...], ins[2 * n + i][...])
            for kind in range(3):
                outs[kind * n + i][...] = res[kind]

    shape = jax.ShapeDtypeStruct((1, D), F32)
    return pl.pallas_call(body, name="adamw_vectors", out_shape=[shape] * (1 + 3 * n), compiler_params=_params())(
        g5, first_parts, *ws, *ms, *vs)


def kernel(x, norm_g, w_in, conv_w, conv_b, conv_ln_g, conv_ln_b, w_out, final_norm_g, loss_target, m_norm_g, m_w_in, m_conv_w, m_conv_b, m_conv_ln_g, m_conv_ln_b, m_w_out, m_final_norm_g, v_norm_g, v_w_in, v_conv_w, v_conv_b, v_conv_ln_g, v_conv_ln_b, v_w_out, v_final_norm_g):
    chip = 2 * lax.axis_index("x") + lax.axis_index("y")
    where = jnp.stack([chip, lax.axis_index("c")]).astype(jnp.int32)
    taps_shard = jnp.pad(conv_w[0], ((0, HALO - CONV_K), (0, 0)))
    wi_full, wo_full, cw_full = _gather_weights(*_place_shards(w_in[0], w_out[0], taps_shard, where))

    gf = final_norm_g[None]
    dw_in4, dw_out, small, dproj_pieces, dx2 = _local_step(
        x[0], loss_target[0], norm_g, wi_full, cw_full, conv_b, conv_ln_g, conv_ln_b, wo_full, gf)
    dw_out4 = dw_out.reshape(NCHIP, WOUT_SHARD, D)

    ri, ro, rs = _exchange_halves(dw_in4, dw_out4, small)
    pi, po, ps = _add_halves(dw_in4, ri, dw_out4, ro, small, rs)
    grad_x, st_in, ri, ro, rs = _inproj_bwd_x(*dproj_pieces, wi_full, x[0], norm_g, dx2, pi, po, ps)
    gi2, go2, g_small, g5, loss8 = _sum_chips(ri, ro, rs, pi, po, ps, where)
    gi2, go2, norm_g_parts = _exchange_results(gi2, go2, st_in)
    g_w_in = gi2.reshape(D, CHUNK)
    g_w_out = go2.reshape(WOUT_SHARD, D)
    g_taps = lax.dynamic_slice(g_small, (ROW_TAPS, chip * CONVW_SHARD), (CONV_K, CONVW_SHARD))

    d_w_in, m2_w_in, v2_w_in = _adamw(w_in[0], g_w_in, m_w_in[0], v_w_in[0], "adamw_w_in")
    d_w_out, m2_w_out, v2_w_out = _adamw(w_out[0], g_w_out, m_w_out[0], v_w_out[0], "adamw_w_out")
    d_taps, m2_taps, v2_taps = _adamw(conv_w[0], g_taps, m_conv_w[0], v_conv_w[0], "adamw_conv_w")
    g_norm, *vec = _adamw_vectors(
        g5, norm_g_parts,
        (norm_g, conv_b, conv_ln_g, conv_ln_b, gf),
        (m_norm_g, m_conv_b, m_conv_ln_g, m_conv_ln_b, m_final_norm_g[None]),
        (v_norm_g, v_conv_b, v_conv_ln_g, v_conv_ln_b, v_final_norm_g[None]))
    d_vec, m2_vec, v2_vec = vec[0:5], vec[5:10], vec[10:15]

    def weight_order(ng, wi, cw, cb, lg, lb, wo, fg):
        return (ng, wi[None], cw[None], cb, lg, lb, wo[None], fg[0])

    grads = weight_order(g_norm, g_w_in, g_taps, g5[1:2], g5[2:3], g5[3:4], g_w_out, g5[4:5])
    deltas = weight_order(d_vec[0], d_w_in, d_taps, d_vec[1], d_vec[2], d_vec[3], d_w_out, d_vec[4])
    new_m = weight_order(m2_vec[0], m2_w_in, m2_taps, m2_vec[1], m2_vec[2], m2_vec[3], m2_w_out, m2_vec[4])
    new_v = weight_order(v2_vec[0], v2_w_in, v2_taps, v2_vec[1], v2_vec[2], v2_vec[3], v2_w_out, v2_vec[4])
    return (loss8[0, 0], grad_x[None], *grads, *deltas, *new_m, *new_v)
```

```python
import jax
import jax.numpy as jnp
from jax import lax
from jax.experimental import pallas as pl
from jax.experimental.pallas import tpu as pltpu

F32 = jnp.float32
BF16 = jnp.bfloat16

S = 4096
D = 1024
LANES = 128
HD = 64
NKV = 4
GQ = 4
KVW = NKV * HD
NCOL = 5632
CONV_K = 31
HALO = 32
BLK = 128
PATTERNS = (1, 4, 16)
NORM_EPS = 1e-6
LN_EPS = 1e-5
NEG = -1e30
OFF_Q, OFF_K, OFF_V, OFF_AG, OFF_CV, OFF_CG, OFF_CGATE = 0, 1024, 1280, 1536, 2560, 3584, 4608
NCHIP = 4
CHUNK = NCOL // NCHIP
WOUT_ROWS = 2 * D
WOUT_SHARD = WOUT_ROWS // NCHIP
CONVW_SHARD = D // NCHIP

ADAM_LR, ADAM_B1, ADAM_B2, ADAM_EPS, ADAM_WD, ADAM_STEP = 0.001, 0.9, 0.999, 1e-08, 0.01, 10

VMEM_LIMIT = 56 * 1024 * 1024


def _params(sem=None, vmem=VMEM_LIMIT):
    return pltpu.CompilerParams(dimension_semantics=sem, vmem_limit_bytes=vmem)


def _sigmoid(a):
    return 0.5 * jnp.tanh(0.5 * a) + 0.5


def _rows(tm, width):
    return pl.BlockSpec((tm, width), lambda i: (i, 0))


def _slabs(n):
    return jax.ShapeDtypeStruct((n, S, LANES), F32)


def _slab_rows(n, tm):
    return pl.BlockSpec((n, tm, LANES), lambda i: (0, i, 0))


def _resident(shape):
    return pl.BlockSpec(shape, lambda *_: (0,) * len(shape), pipeline_mode=pl.Buffered(1))


def _dot(a, b):
    return jnp.dot(a, b, preferred_element_type=F32)


def _dot_nt(a, b):
    return lax.dot_general(a, b, (((1,), (1,)), ((), ())), preferred_element_type=F32)


def _dot_tn(a, b):
    return lax.dot_general(a, b, (((0,), (0,)), ((), ())), preferred_element_type=F32)


def _inproj_fwd(x, g1, w_bf):
    tm = 512

    def body(x_ref, g_ref, w_ref, h_ref, q_ref, k_ref, v_ref, ag_ref, cv_ref, cg_ref, cgate_ref):
        xt = x_ref[...]
        r = lax.rsqrt(jnp.mean(xt * xt, axis=-1, keepdims=True) + NORM_EPS)
        h = (xt * r * g_ref[...]).astype(BF16)
        h_ref[...] = h
        q = _dot(h, w_ref[:, OFF_Q:OFF_Q + D]) * (HD ** -0.5)
        kv = _dot(h, w_ref[:, OFF_K:OFF_K + 2 * KVW])
        for sl in range(D // LANES):
            q_ref[sl] = q[:, sl * LANES:(sl + 1) * LANES]
        for sl in range(KVW // LANES):
            k_ref[sl] = kv[:, sl * LANES:(sl + 1) * LANES]
            v_ref[sl] = kv[:, KVW + sl * LANES:KVW + (sl + 1) * LANES]
        ag_ref[...] = _dot(h, w_ref[:, OFF_AG:OFF_AG + D])
        cv_ref[...] = _dot(h, w_ref[:, OFF_CV:OFF_CV + D])
        cg_ref[...] = _dot(h, w_ref[:, OFF_CG:OFF_CG + D])
        cgate_ref[...] = _dot(h, w_ref[:, OFF_CGATE:OFF_CGATE + D])

    big = jax.ShapeDtypeStruct((S, D), F32)
    return pl.pallas_call(
        body, grid=(S // tm,), name="inproj_fwd",
        in_specs=[_rows(tm, D), _resident((1, D)), _resident((D, NCOL))],
        out_specs=[_rows(tm, D), _slab_rows(D // LANES, tm), _slab_rows(KVW // LANES, tm), _slab_rows(KVW // LANES, tm),
                   _rows(tm, D), _rows(tm, D), _rows(tm, D), _rows(tm, D)],
        out_shape=[jax.ShapeDtypeStruct((S, D), BF16), _slabs(D // LANES), _slabs(KVW // LANES), _slabs(KVW // LANES),
                   big, big, big, big],
        compiler_params=_params(("arbitrary",)),
    )(x, g1, w_bf)


def _bias_table(d):
    h = jnp.arange(NKV * GQ, dtype=F32)
    slopes = jnp.exp2(-8.0 * (h + 1.0) / (NKV * GQ))
    qi = jnp.arange(BLK)[:, None]
    kj = jnp.arange(2 * BLK)[None, :]
    dist = BLK + qi - kj
    window = (dist >= 0) & (dist <= BLK)
    bias = -slopes[:, None, None] * (dist * d).astype(F32)[None]
    has_prev = jnp.stack([jnp.broadcast_to(kj >= BLK, (BLK, 2 * BLK)), jnp.ones((BLK, 2 * BLK), bool)])
    valid = window[None] & has_prev
    tab = jnp.where(valid[:, None], bias[None], NEG)
    return tab.reshape(2, NKV, GQ * BLK, 2 * BLK)


def _sub_rows(start, d):
    if d == 1:
        return pl.ds(pl.multiple_of(start, BLK), BLK)
    return pl.ds(start, BLK, stride=d)


NHEAD = NKV * GQ
CHUNK_ROWS = 2048
BLOCKS_PER_CHUNK = CHUNK_ROWS // BLK


def _low_lanes(rows=BLK):
    return lax.broadcasted_iota(jnp.int32, (rows, LANES), 1) < HD


def _block_start(idx, d):
    shift = d.bit_length() - 1
    b, r = lax.shift_right_logical(idx, shift), lax.bitwise_and(idx, d - 1)
    start = b * (BLK * d) + r
    return b, start, jnp.maximum(start - BLK * d, r)


def _stack_heads(ref, rows):
    low = _low_lanes()
    t0, t1 = ref[0, rows, :], ref[1, rows, :]
    return jnp.concatenate([jnp.where(low, t0, 0.0), jnp.where(low, 0.0, t0),
                            jnp.where(low, t1, 0.0), jnp.where(low, 0.0, t1)], axis=0).astype(BF16)


def _unstack_heads(dup):
    low = _low_lanes()
    return (jnp.where(low, dup[0:BLK], dup[BLK:2 * BLK]), jnp.where(low, dup[2 * BLK:3 * BLK], dup[3 * BLK:4 * BLK]))


def _kv_dup(ref, prow, rows, odd):
    t = jnp.concatenate([ref[0, prow, :], ref[0, rows, :]], axis=0)
    swapped = pltpu.roll(t, HD, axis=1)
    keep = jnp.logical_xor(_low_lanes(2 * BLK), odd)
    return jnp.where(keep, t, swapped).astype(BF16)


PIECES = 3


def _by_head(tiles):
    lane = lax.broadcasted_iota(jnp.int32, tiles[0].shape, 1)
    out = tiles[0]
    for g in range(1, GQ):
        out = jnp.where(lax.bitwise_and(lane, GQ - 1) == g, tiles[g], out)
    return out


def _minus_in_pieces(x):
    lane = lax.broadcasted_iota(jnp.int32, x.shape, 1)
    hi = (-x).astype(BF16).astype(F32)
    rest = -x - hi
    mid = rest.astype(BF16).astype(F32)
    lo = (rest - mid).astype(BF16).astype(F32)
    return jnp.where(lane < GQ, hi, jnp.where(lane < 2 * GQ, mid, jnp.where(lane < PIECES * GQ, lo, 0.0)))


def _attn_fwd(q, k, v, tables, a_gate):
    tm = 256
    width = GQ * HD

    lane_out = jnp.arange(LANES)[None, :] // HD
    spread_sel = jnp.stack([jnp.arange(LANES)[:, None] == 2 * half + lane_out for half in range(2)]).astype(BF16)

    def body(q_ref, k_ref, v_ref, b1_ref, b2_ref, b3_ref, ag_ref, sel_ref, o_ref, lse_ref, y_ref, op, lp):
        odd = pl.program_id(0) % 2 == 1
        chunk = pl.program_id(1)
        ones = jnp.ones((2 * BLK, LANES), BF16)

        for pat, (d, b_ref) in enumerate(zip(PATTERNS, (b1_ref, b2_ref, b3_ref))):
            def block(idx, carry, pat=pat, d=d, b_ref=b_ref):
                b, start, pstart = _block_start(chunk * BLOCKS_PER_CHUNK + idx, d)
                rows, prow = _sub_rows(start, d), _sub_rows(pstart, d)
                mine = _sub_rows(start - chunk * CHUNK_ROWS, d)
                qs = _stack_heads(q_ref, mine)
                kw = _kv_dup(k_ref, prow, rows, odd)
                vw = _kv_dup(v_ref, prow, rows, odd)
                s = _dot_nt(qs, kw) + b_ref[jnp.minimum(b, 1), 0]
                m = jnp.max(s, axis=1, keepdims=True)
                p = jnp.exp(s - m).astype(BF16)
                ol = _dot(p, jnp.concatenate([vw, ones], axis=1))
                l = ol[:, LANES:]
                op[pat, 0, mine, :], op[pat, 1, mine, :] = _unstack_heads(ol[:, :LANES] / l)
                lp[pat, mine, :] = _by_head([(m + jnp.log(l))[g * BLK:(g + 1) * BLK] for g in range(GQ)])
                return carry

            lax.fori_loop(0, BLOCKS_PER_CHUNK, block, 0, unroll=2)

        def mix(t, carry):
            r = pl.ds(pl.multiple_of(t * tm, tm), tm)
            low = _low_lanes(tm)
            a, b, c = lp[0, r, :], lp[1, r, :], lp[2, r, :]
            m = jnp.maximum(jnp.maximum(a, b), c)
            ea, eb, ec = jnp.exp(a - m), jnp.exp(b - m), jnp.exp(c - m)
            den = ea + eb + ec
            lse_ref[0, r, :] = _minus_in_pieces(m + jnp.log(den))
            inv = 1.0 / den
            for half in range(2):
                def spread(w):
                    hi = w.astype(BF16)
                    lo = (w - hi.astype(F32)).astype(BF16)
                    return _dot(hi, sel_ref[half]) + _dot(lo, sel_ref[half])

                o = (spread(ea * inv) * op[0, half, r, :] + spread(eb * inv) * op[1, half, r, :]
                     + spread(ec * inv) * op[2, half, r, :])
                o_ref[half, r, :] = o
                cols = slice(half * LANES, (half + 1) * LANES)
                ag = ag_ref[r, cols]
                y_ref[r, cols] = (o * (ag * _sigmoid(ag))).astype(BF16)
            return carry

        lax.fori_loop(0, CHUNK_ROWS // tm, mix, 0)

    q_like = pl.BlockSpec((2, CHUNK_ROWS, LANES), lambda j, c: (j, c, 0))
    per_kv = pl.BlockSpec((1, CHUNK_ROWS, LANES), lambda j, c: (j, c, 0))
    kv = pl.BlockSpec((1, S, LANES), lambda j, c: (j // 2, 0, 0))
    bias_spec = pl.BlockSpec((2, 1, GQ * BLK, 2 * BLK), lambda j, c: (0, j, 0, 0))
    group_cols = pl.BlockSpec((CHUNK_ROWS, width), lambda j, c: (c, j))
    return pl.pallas_call(
        body, grid=(NKV, S // CHUNK_ROWS), name="attn_fwd",
        in_specs=[q_like, kv, kv, bias_spec, bias_spec, bias_spec, group_cols,
                  pl.BlockSpec((2, LANES, LANES), lambda j, c: (0, 0, 0))],
        out_specs=[q_like, per_kv, group_cols],
        out_shape=[_slabs(D // LANES), _slabs(NKV), jax.ShapeDtypeStruct((S, D), BF16)],
        scratch_shapes=[pltpu.VMEM((len(PATTERNS), 2, CHUNK_ROWS, LANES), F32),
                        pltpu.VMEM((len(PATTERNS), CHUNK_ROWS, LANES), F32)],
        compiler_params=_params(("arbitrary", "arbitrary")),
    )(q, k, v, *tables, a_gate, spread_sel)


def _head_sum_selectors():
    lane_in = jnp.arange(LANES)[:, None] // HD
    return jnp.stack([jnp.broadcast_to(lane_in == h, (LANES, LANES)) for h in range(2)]).astype(BF16)


def _attn_gate_bwd(dy_att, o, a_gate, selectors):
    tm = 256

    def body(dy_ref, o_ref, ag_ref, e_ref, do_ref, dag_ref, delta_ref):
        for j in range(NKV):
            deltas = []
            for sl in (2 * j, 2 * j + 1):
                cols = slice(sl * LANES, (sl + 1) * LANES)
                dy, ag, o_ = dy_ref[:, cols], ag_ref[:, cols], o_ref[sl]
                sg = _sigmoid(ag)
                do = dy * (ag * sg)
                do_ref[sl] = do
                dag_ref[:, cols] = (dy * o_ * (sg * (1.0 + ag * (1.0 - sg)))).astype(BF16)
                prod = do * o_
                hi = prod.astype(BF16)
                lo = (prod - hi.astype(F32)).astype(BF16)
                deltas += [_dot(hi, e_ref[h]) + _dot(lo, e_ref[h]) for h in range(2)]
            delta_ref[j] = _minus_in_pieces(_by_head(deltas))

    return pl.pallas_call(
        body, grid=(S // tm,), name="attn_gate_bwd",
        in_specs=[_rows(tm, D), _slab_rows(D // LANES, tm), _rows(tm, D), _resident((2, LANES, LANES))],
        out_specs=[_slab_rows(D // LANES, tm), _rows(tm, D), _slab_rows(NKV, tm)],
        out_shape=[_slabs(D // LANES), jax.ShapeDtypeStruct((S, D), BF16), _slabs(NKV)],
        compiler_params=_params(("arbitrary",)),
    )(dy_att, o, a_gate, selectors)


def _own_pieces(tile):
    lane = lax.broadcasted_iota(jnp.int32, tile.shape, 1)
    head = jnp.where(lane < PIECES * GQ, lax.bitwise_and(lane, GQ - 1), -1)
    return jnp.concatenate([jnp.where(head == g, tile, 0.0) for g in range(GQ)], axis=0).astype(BF16)


def _attn_bwd(q, k, v, do, lse, delta, bias, d):
    def body(q_ref, do_ref, l_ref, dl_ref, k_ref, v_ref, b_ref, dq_ref, dkv_ref, acc):
        odd = pl.program_id(0) % 2 == 1
        chunk = pl.program_id(1)
        ones = (lax.broadcasted_iota(jnp.int32, (2 * BLK, LANES), 1) < PIECES * GQ).astype(BF16)

        def in_acc(block_idx):
            return pl.ds(pl.multiple_of(block_idx * BLK, BLK), BLK)

        @pl.when(chunk == 0)
        def _():
            acc[...] = jnp.zeros_like(acc)

        def block(idx, carry):
            idx = chunk * BLOCKS_PER_CHUNK + idx
            b, start, pstart = _block_start(idx, d)
            rows, prow = _sub_rows(start, d), _sub_rows(pstart, d)
            mine = _sub_rows(start - chunk * CHUNK_ROWS, d)
            qs = _stack_heads(q_ref, mine)
            dos = _stack_heads(do_ref, mine)
            kw = _kv_dup(k_ref, prow, rows, odd)
            vw = _kv_dup(v_ref, prow, rows, odd)
            s = _dot_nt(jnp.concatenate([qs, _own_pieces(l_ref[0, mine, :])], axis=1),
                        jnp.concatenate([kw, ones], axis=1)) + b_ref[jnp.minimum(b, 1), 0]
            p = jnp.exp(s)
            dv2 = _dot_tn(p.astype(BF16), dos)
            dp = _dot_nt(jnp.concatenate([dos, _own_pieces(dl_ref[0, mine, :])], axis=1),
                         jnp.concatenate([vw, ones], axis=1))
            ds = (p * dp).astype(BF16)
            dq_ref[0, mine, :], dq_ref[1, mine, :] = _unstack_heads(_dot(ds, kw))
            dk2 = _dot_tn(ds, qs)
            dkv = jnp.where(_low_lanes(2 * BLK), dk2 + pltpu.roll(dk2, HD, axis=1), dv2 + pltpu.roll(dv2, HD, axis=1))
            acc[in_acc(idx), :] = acc[in_acc(idx), :] + dkv[BLK:]
            before = jnp.where(b >= 1, idx - d, idx)
            acc[in_acc(before), :] = acc[in_acc(before), :] + dkv[:BLK]
            return carry

        lax.fori_loop(0, BLOCKS_PER_CHUNK, block, 0, unroll=8)

        @pl.when(chunk == S // CHUNK_ROWS - 1)
        def _():
            def place(idx, carry):
                _, start, _ = _block_start(idx, d)
                dkv_ref[0, _sub_rows(start, d), :] = acc[in_acc(idx), :]
                return carry

            lax.fori_loop(0, S // BLK, place, 0, unroll=4)

    q_like = pl.BlockSpec((2, CHUNK_ROWS, LANES), lambda j, c: (j, c, 0))
    pieces = pl.BlockSpec((1, CHUNK_ROWS, LANES), lambda j, c: (j, c, 0))
    kv = pl.BlockSpec((1, S, LANES), lambda j, c: (j // 2, 0, 0))
    per_kv = pl.BlockSpec((1, S, LANES), lambda j, c: (j, 0, 0))
    bias_spec = pl.BlockSpec((2, 1, GQ * BLK, 2 * BLK), lambda j, c: (0, j, 0, 0))
    return pl.pallas_call(
        body, grid=(NKV, S // CHUNK_ROWS), name=f"attn_bwd_d{d}",
        in_specs=[q_like, q_like, pieces, pieces, kv, kv, bias_spec],
        out_specs=[q_like, per_kv],
        out_shape=[_slabs(D // LANES), _slabs(NKV)],
        scratch_shapes=[pltpu.VMEM((S, LANES), F32)],
        compiler_params=_params(("arbitrary", "arbitrary")),
    )(q, do, lse, delta, k, v, bias)


CONV_T = 256


def _halo_before(i):
    return (jnp.maximum(i * (CONV_T // HALO) - 1, 0), 0)


def _halo_after(i):
    return (jnp.minimum((i + 1) * (CONV_T // HALO), S // HALO - 1), 0)


SUBLANES = 8
NCH = D // LANES
GROUP = SUBLANES * SUBLANES


def _comb(ref, cb, base):
    return ref[cb, pl.ds(base, SUBLANES, stride=SUBLANES), :]


def _taps(w_ref, cols):
    return [jnp.broadcast_to(w_ref[j:j + 1, cols], (SUBLANES, LANES)) for j in range(CONV_K)]


def _conv_fwd(c_val, c_glu, c_gate, conv_w, conv_b, ln_g, ln_b):
    T = CONV_T

    def body(cv_ref, cg_ref, cvh_ref, cgh_ref, gate_ref, w_ref, b_ref, lg_ref, lb_ref, u_ref, y_ref, win, us):
        i = pl.program_id(0)
        for cb in range(NCH):
            cols = slice(cb * LANES, (cb + 1) * LANES)
            win[cb, HALO:HALO + T, :] = cv_ref[:, cols] * _sigmoid(cg_ref[:, cols])
            win[cb, 0:HALO, :] = jnp.where(i > 0, cvh_ref[:, cols] * _sigmoid(cgh_ref[:, cols]), 0.0)
        for cb in range(NCH):
            cols = slice(cb * LANES, (cb + 1) * LANES)
            taps = _taps(w_ref, cols)
            bias = jnp.broadcast_to(b_ref[:, cols], (SUBLANES, LANES))

            def group(g, carry):
                for b in range(SUBLANES):
                    base = g * GROUP + b
                    acc = bias
                    for j in range(CONV_K):
                        acc = acc + taps[j] * _comb(win, cb, base + (HALO - (CONV_K - 1) + j))
                    us[cb, pl.ds(base, SUBLANES, stride=SUBLANES), :] = acc
                return carry

            lax.fori_loop(0, T // GROUP, group, 0)
        total = us[0]
        for cb in range(1, NCH):
            total = total + us[cb]
        mu = jnp.sum(total, axis=-1, keepdims=True) * (1.0 / D)
        sq = jnp.zeros((T, LANES), F32)
        for cb in range(NCH):
            uc = us[cb] - mu
            sq = sq + uc * uc
        rstd = lax.rsqrt(jnp.sum(sq, axis=-1, keepdims=True) * (1.0 / D) + LN_EPS)
        for cb in range(NCH):
            cols = slice(cb * LANES, (cb + 1) * LANES)
            u = us[cb]
            u_ref[:, cols] = u
            nrm = (u - mu) * rstd * lg_ref[:, cols] + lb_ref[:, cols]
            gate = gate_ref[:, cols]
            y_ref[:, cols] = (nrm * _sigmoid(nrm) * (gate * _sigmoid(gate))).astype(BF16)

    halo = pl.BlockSpec((HALO, D), _halo_before)
    return pl.pallas_call(
        body, grid=(S // T,), name="conv_fwd",
        in_specs=[_rows(T, D), _rows(T, D), halo, halo, _rows(T, D),
                  _resident((HALO, D)), _resident((1, D)), _resident((1, D)), _resident((1, D))],
        out_specs=[_rows(T, D), _rows(T, D)],
        out_shape=[jax.ShapeDtypeStruct((S, D), F32), jax.ShapeDtypeStruct((S, D), BF16)],
        scratch_shapes=[pltpu.VMEM((NCH, T + HALO, LANES), F32), pltpu.VMEM((NCH, T, LANES), F32)],
        compiler_params=_params(("arbitrary",)),
    )(c_val, c_glu, c_val, c_glu, c_gate, conv_w, conv_b, ln_g, ln_b)


def _conv_bwd_rows(u, c_gate, dy_conv, ln_g, ln_b):
    tm = 256

    def body(u_ref, gate_ref, dy_ref, lg_ref, lb_ref, du_ref, dgate_ref, st_ref):
        @pl.when(pl.program_id(0) == 0)
        def _():
            st_ref[...] = jnp.zeros_like(st_ref)

        u, gate, dy = u_ref[...], gate_ref[...], dy_ref[...]
        mu = jnp.mean(u, axis=-1, keepdims=True)
        uc = u - mu
        rstd = lax.rsqrt(jnp.mean(uc * uc, axis=-1, keepdims=True) + LN_EPS)
        z = uc * rstd
        nrm = z * lg_ref[...] + lb_ref[...]
        sn, sg = _sigmoid(nrm), _sigmoid(gate)
        dgate_ref[...] = (dy * (nrm * sn) * (sg * (1.0 + gate * (1.0 - sg)))).astype(BF16)
        dn = dy * (gate * sg) * (sn * (1.0 + nrm * (1.0 - sn)))
        dz = dn * lg_ref[...]
        du = rstd * (dz - jnp.mean(dz, axis=-1, keepdims=True) - z * jnp.mean(dz * z, axis=-1, keepdims=True))
        du_ref[...] = du
        st_ref[0:1, :] += jnp.sum(dn * z, axis=0, keepdims=True)
        st_ref[1:2, :] += jnp.sum(dn, axis=0, keepdims=True)
        st_ref[2:3, :] += jnp.sum(du, axis=0, keepdims=True)

    big = jax.ShapeDtypeStruct((S, D), F32)
    return pl.pallas_call(
        body, grid=(S // tm,), name="conv_bwd_rows",
        in_specs=[_rows(tm, D)] * 3 + [_resident((1, D)), _resident((1, D))],
        out_specs=[_rows(tm, D), _rows(tm, D), pl.BlockSpec((8, D), lambda i: (0, 0))],
        out_shape=[big, jax.ShapeDtypeStruct((S, D), BF16), jax.ShapeDtypeStruct((8, D), F32)],
        compiler_params=_params(("arbitrary",)),
    )(u, c_gate, dy_conv, ln_g, ln_b)


def _conv_bwd_taps(du, c_val, c_glu, conv_w):
    T = CONV_T
    last = S // T - 1

    def body(du_ref, dua_ref, cv_ref, cg_ref, cvh_ref, cgh_ref, w_ref, dcv_ref, dcg_ref, dw_ref,
             hwin, dwin, dhs, dw_acc):
        i = pl.program_id(0)

        @pl.when(i == 0)
        def _():
            dw_acc[...] = jnp.zeros_like(dw_acc)

        for cb in range(NCH):
            cols = slice(cb * LANES, (cb + 1) * LANES)
            hwin[cb, HALO:HALO + T, :] = cv_ref[:, cols] * _sigmoid(cg_ref[:, cols])
            hwin[cb, 0:HALO, :] = jnp.where(i > 0, cvh_ref[:, cols] * _sigmoid(cgh_ref[:, cols]), 0.0)
            dwin[cb, 0:T, :] = du_ref[:, cols]
            dwin[cb, T:T + HALO, :] = jnp.where(i < last, dua_ref[:, cols], 0.0)
        for cb in range(NCH):
            cols = slice(cb * LANES, (cb + 1) * LANES)
            taps = _taps(w_ref, cols)

            def group_dh(g, carry):
                for b in range(SUBLANES):
                    base = g * GROUP + b
                    acc = jnp.zeros((SUBLANES, LANES), F32)
                    for j in range(CONV_K):
                        acc = acc + taps[j] * _comb(dwin, cb, base + (CONV_K - 1 - j))
                    dhs[cb, pl.ds(base, SUBLANES, stride=SUBLANES), :] = acc
                return carry

            lax.fori_loop(0, T // GROUP, group_dh, 0)

            def group_dw(g, sums):
                for b in range(SUBLANES):
                    base = g * GROUP + b
                    d = _comb(dwin, cb, base)
                    sums = tuple(sums[j] + d * _comb(hwin, cb, base + (HALO - (CONV_K - 1) + j))
                                 for j in range(CONV_K))
                return sums

            sums = lax.fori_loop(0, T // GROUP, group_dw, tuple(dw_acc[j, :, cols] for j in range(CONV_K)))
            for j in range(CONV_K):
                dw_acc[j, :, cols] = sums[j]
            dh = dhs[cb]
            cv, sg = cv_ref[:, cols], _sigmoid(cg_ref[:, cols])
            dcv_ref[:, cols] = (dh * sg).astype(BF16)
            dcg_ref[:, cols] = (dh * cv * (sg * (1.0 - sg))).astype(BF16)

        @pl.when(i == last)
        def _():
            dw_ref[...] = jnp.zeros_like(dw_ref)
            for j in range(CONV_K):
                dw_ref[j:j + 1, :] = jnp.sum(dw_acc[j], axis=0, keepdims=True)

    before = pl.BlockSpec((HALO, D), _halo_before)
    after = pl.BlockSpec((HALO, D), _halo_after)
    big = jax.ShapeDtypeStruct((S, D), BF16)
    return pl.pallas_call(
        body, grid=(S // T,), name="conv_bwd_taps",
        in_specs=[_rows(T, D), after, _rows(T, D), _rows(T, D), before, before, _resident((HALO, D))],
        out_specs=[_rows(T, D), _rows(T, D), pl.BlockSpec((HALO, D), lambda i: (0, 0))],
        out_shape=[big, big, jax.ShapeDtypeStruct((HALO, D), F32)],
        scratch_shapes=[pltpu.VMEM((NCH, T + HALO, LANES), F32), pltpu.VMEM((NCH, T + HALO, LANES), F32),
                        pltpu.VMEM((NCH, T, LANES), F32), pltpu.VMEM((CONV_K, SUBLANES, D), F32)],
        compiler_params=_params(("arbitrary",)),
    )(du, du, c_val, c_glu, c_val, c_glu, conv_w)


def _outproj_loss(y_att, y_conv, w_out_bf, x, target, gf):
    tm = 256

    def body(ya_ref, yc_ref, w_ref, x_ref, t_ref, gf_ref, dx2_ref, dya_ref, dyc_ref, dw_ref, st_ref, acc):
        @pl.when(pl.program_id(0) == 0)
        def _():
            acc[...] = jnp.zeros_like(acc)
            st_ref[...] = jnp.zeros_like(st_ref)

        ya, yc = ya_ref[...], yc_ref[...]
        x2 = x_ref[...] + _dot(ya, w_ref[0:D, :]) + _dot(yc, w_ref[D:2 * D, :])
        r = lax.rsqrt(jnp.mean(x2 * x2, axis=-1, keepdims=True) + NORM_EPS)
        xn = x2 * r
        err = xn * gf_ref[...] - t_ref[...]
        dout = err * (1.0 / D)
        dxn = dout * gf_ref[...]
        dx2 = r * (dxn - xn * jnp.mean(dxn * xn, axis=-1, keepdims=True))
        dx2_ref[...] = dx2
        dx2b = dx2.astype(BF16)
        dya_ref[...] = _dot_nt(dx2b, w_ref[0:D, :])
        dyc_ref[...] = _dot_nt(dx2b, w_ref[D:2 * D, :])
        acc[0:D, :] += _dot_tn(ya, dx2b)
        acc[D:2 * D, :] += _dot_tn(yc, dx2b)
        st_ref[0:1, :] += jnp.sum(dout * xn, axis=0, keepdims=True)
        st_ref[1:2, :] += jnp.sum(err * err, axis=0, keepdims=True) * (0.5 / D)

        @pl.when(pl.program_id(0) == S // tm - 1)
        def _():
            dw_ref[...] = acc[...].astype(BF16)

    big = jax.ShapeDtypeStruct((S, D), F32)
    return pl.pallas_call(
        body, grid=(S // tm,), name="outproj_loss",
        in_specs=[_rows(tm, D), _rows(tm, D), _resident((WOUT_ROWS, D)), _rows(tm, D), _rows(tm, D), _resident((1, D))],
        out_specs=[_rows(tm, D), _rows(tm, D), _rows(tm, D),
                   pl.BlockSpec((WOUT_ROWS, D), lambda i: (0, 0)), pl.BlockSpec((8, D), lambda i: (0, 0))],
        out_shape=[big, big, big, jax.ShapeDtypeStruct((WOUT_ROWS, D), BF16), jax.ShapeDtypeStruct((8, D), F32)],
        scratch_shapes=[pltpu.VMEM((WOUT_ROWS, D), F32)],
        compiler_params=_params(("arbitrary",)),
    )(y_att, y_conv, w_out_bf, x, target, gf)


UNITS_PER_CHUNK = CHUNK // LANES


def _dproj_unit(u, dqs, dkvs, gates, rows):
    if u < OFF_K // LANES:
        return ((dqs[0][u] + dqs[1][u] + dqs[2][u]) * (HD ** -0.5)).astype(BF16)
    if u < OFF_AG // LANES:
        w = u - OFF_K // LANES
        ta, tb = (dkvs[0][j] + dkvs[1][j] + dkvs[2][j] for j in (2 * (w % 2), 2 * (w % 2) + 1))
        low = _low_lanes(rows)
        if w < 2:
            return jnp.where(low, ta, pltpu.roll(tb, HD, axis=1)).astype(BF16)
        return jnp.where(low, pltpu.roll(ta, HD, axis=1), tb).astype(BF16)
    g, sl = divmod(u - OFF_AG // LANES, D // LANES)
    return gates[g][:, sl * LANES:(sl + 1) * LANES]


def _dproj_sources(units, dqs, dkvs, gates, rows):
    use_q = any(u < OFF_K // LANES for u in units)
    use_kv = any(OFF_K // LANES <= u < OFF_AG // LANES for u in units)
    use_g = sorted({(u - OFF_AG // LANES) // (D // LANES) for u in units if u >= OFF_AG // LANES})
    args = (list(dqs) if use_q else []) + (list(dkvs) if use_kv else []) + [gates[g] for g in use_g]
    specs = ([_slab_rows(D // LANES, rows)] * 3 if use_q else []) + ([_slab_rows(NKV, rows)] * 3 if use_kv else []) \
        + [_rows(rows, D)] * len(use_g)

    def pick(refs):
        refs = list(refs)
        q_refs = [refs.pop(0) for _ in range(3)] if use_q else None
        kv_refs = [refs.pop(0) for _ in range(3)] if use_kv else None
        return q_refs, kv_refs, {g: refs.pop(0) for g in use_g}

    return args, specs, pick


def _inproj_bwd_x(dqs, dkvs, gates, w_bf, x, g1, dx2, pi, po, ps):
    tm = 256
    last = S // tm - 1
    units = range(NCOL // LANES)
    pieces, piece_specs, pick = _dproj_sources(units, dqs, dkvs, gates, tm)

    def body(*refs):
        piece_refs, refs = refs[:len(pieces)], refs[len(pieces):]
        (w_ref, x_ref, g_ref, dx2_ref, pi_ref, po_ref, ps_ref,
         gx_ref, st_ref, ri_ref, ro_ref, rs_ref, dp_ref, send, recv) = refs
        i = pl.program_id(0)
        copies = _chip_exchange_copies((pi_ref, po_ref, ps_ref), (ri_ref, ro_ref, rs_ref), send, recv)

        @pl.when(i == 0)
        def _():
            st_ref[...] = jnp.zeros_like(st_ref)
            for out, _ in copies:
                out.start()

        sources = pick(piece_refs)
        for u in units:
            dp_ref[:, u * LANES:(u + 1) * LANES] = _dproj_unit(u, *sources, tm)
        dh = _dot_nt(dp_ref[...], w_ref[...])
        xt = x_ref[...]
        r = lax.rsqrt(jnp.mean(xt * xt, axis=-1, keepdims=True) + NORM_EPS)
        xn = xt * r
        dxn = dh * g_ref[...]
        gx_ref[...] = dx2_ref[...] + r * (dxn - xn * jnp.mean(dxn * xn, axis=-1, keepdims=True))
        st_ref[0:1, :] += jnp.sum(dh * xn, axis=0, keepdims=True)

        @pl.when(i == last)
        def _():
            for _, arrival in copies:
                arrival.wait_recv()
            for out, _ in copies:
                out.wait_send()

    n = 3 * len(CHIP_FLIPS)
    return pl.pallas_call(
        body, grid=(S // tm,), name="inproj_bwd_x",
        in_specs=piece_specs + [_resident((D, NCOL)), _rows(tm, D), _resident((1, D)), _rows(tm, D), ANY, ANY, ANY],
        out_specs=[_rows(tm, D), pl.BlockSpec((8, D), lambda i: (0, 0)), ANY, ANY, ANY],
        out_shape=[jax.ShapeDtypeStruct((S, D), F32), jax.ShapeDtypeStruct((8, D), F32),
                   jax.ShapeDtypeStruct((NCHIP, D // 2, CHUNK), BF16),
                   jax.ShapeDtypeStruct((NCHIP, WOUT_SHARD // 2, D), BF16),
                   jax.ShapeDtypeStruct((NCHIP, SMALL_ROWS, D), F32)],
        scratch_shapes=[pltpu.VMEM((tm, NCOL), BF16), pltpu.SemaphoreType.DMA((n,)), pltpu.SemaphoreType.DMA((n,))],
        compiler_params=_params(("arbitrary",)),
    )(*pieces, w_bf, x, g1, dx2, pi, po, ps)


def _inproj_bwd_w(h, dqs, dkvs, gates):
    out = None
    for k in range(NCHIP):
        units = range(k * UNITS_PER_CHUNK, (k + 1) * UNITS_PER_CHUNK)
        tk = 512 if units[0] < OFF_K // LANES else 1024
        nk = S // tk
        pieces, piece_specs, pick = _dproj_sources(units, dqs, dkvs, gates, tk)
        handed_on = [] if out is None else [out]

        def body(*refs, units=units, pick=pick, n_pieces=len(pieces), n_in=1 + len(pieces) + len(handed_on)):
            h_ref, piece_refs = refs[0], refs[1:1 + n_pieces]
            o_ref, tile, acc = refs[n_in:]
            i = pl.program_id(0)

            @pl.when(i == 0)
            def _():
                acc[...] = jnp.zeros_like(acc)

            sources = pick(piece_refs)
            for n, u in enumerate(units):
                tile[:, n * LANES:(n + 1) * LANES] = _dproj_unit(u, *sources, tk)
            acc[...] += _dot_tn(h_ref[...], tile[...])

            @pl.when(i == nk - 1)
            def _():
                o_ref[0] = acc[...].astype(BF16)

        out = pl.pallas_call(
            body, grid=(nk,), name=f"inproj_bwd_w{k}",
            in_specs=[_rows(tk, D)] + piece_specs + [ANY] * len(handed_on),
            out_specs=pl.BlockSpec((1, D, CHUNK), lambda i, k=k: (k, 0, 0)),
            out_shape=jax.ShapeDtypeStruct((NCHIP, D, CHUNK), BF16),
            input_output_aliases={1 + len(pieces): 0} if handed_on else {},
            scratch_shapes=[pltpu.VMEM((tk, CHUNK), BF16), pltpu.VMEM((D, CHUNK), F32)],
            compiler_params=_params(("arbitrary",)),
        )(h, *pieces, *handed_on)
    return out


def _local_step(x, target, g1, w_in_bf, conv_w, conv_b, ln_g, ln_b, w_out_bf, gf):
    h, q, k, v, a_gate, c_val, c_glu, c_gate = _inproj_fwd(x, g1, w_in_bf)
    tables = [_bias_table(d) for d in PATTERNS]
    o, lse, y_att = _attn_fwd(q, k, v, tables, a_gate)
    u, y_conv = _conv_fwd(c_val, c_glu, c_gate, conv_w, conv_b, ln_g, ln_b)
    dx2, dy_att, dy_conv, dw_out, st_out = _outproj_loss(y_att, y_conv, w_out_bf, x, target, gf)

    do, da_gate, delta = _attn_gate_bwd(dy_att, o, a_gate, _head_sum_selectors())
    dqs, dkvs = zip(*[_attn_bwd(q, k, v, do, lse, delta, t, d) for t, d in zip(tables, PATTERNS)])

    du, dc_gate, st_conv = _conv_bwd_rows(u, c_gate, dy_conv, ln_g, ln_b)
    dc_val, dc_glu, dconv_w = _conv_bwd_taps(du, c_val, c_glu, conv_w)

    dproj_pieces = (dqs, dkvs, (da_gate, dc_val, dc_glu, dc_gate))
    dw_in = _inproj_bwd_w(h, *dproj_pieces)
    small = jnp.concatenate([st_conv, st_out, dconv_w], axis=0)
    return dw_in, dw_out, small, dproj_pieces, dx2


ROW_LN_G, ROW_LN_B, ROW_CONV_B, ROW_FINAL_G, ROW_LOSS, ROW_TAPS = 0, 1, 2, 8, 9, 16
SMALL_ROWS = 16 + HALO
NDEV = 8


MESH = pl.DeviceIdType.MESH
ANY = pl.BlockSpec(memory_space=pl.ANY)
CHIP_FLIPS = ((1, 0), (0, 1), (1, 1))


def _pos():
    return lax.axis_index("x"), lax.axis_index("y"), lax.axis_index("c")


def _flip(v, f):
    return 1 - v if f else v


def _ds(start, size, align=None):
    return pl.ds(pl.multiple_of(start, align or size), size)


def _place_shards(wi, wo, cw, where):
    steps = 4

    def body(where_ref, wi_ref, wo_ref, cw_ref, wi_full, wo_full, cw_full):
        wi_full[...] = wi_ref[...].astype(BF16)
        wo_full[...] = wo_ref[...].astype(BF16)
        cw_full[...] = cw_ref[...]

    grid_spec = pltpu.PrefetchScalarGridSpec(
        num_scalar_prefetch=1, grid=(steps,),
        in_specs=[pl.BlockSpec((D // steps, CHUNK), lambda i, w: (i, 0)),
                  pl.BlockSpec((WOUT_SHARD // steps, D), lambda i, w: (i, 0)),
                  pl.BlockSpec((HALO, CONVW_SHARD), lambda i, w: (0, 0))],
        out_specs=[pl.BlockSpec((D // steps, CHUNK), lambda i, w: (i, w[0])),
                   pl.BlockSpec((WOUT_SHARD // steps, D), lambda i, w: (w[0] * steps + i, 0)),
                   pl.BlockSpec((HALO, CONVW_SHARD), lambda i, w: (0, w[0]))])
    return pl.pallas_call(
        body, grid_spec=grid_spec, name="place_shards",
        out_shape=[jax.ShapeDtypeStruct((D, NCOL), BF16), jax.ShapeDtypeStruct((WOUT_ROWS, D), BF16),
                   jax.ShapeDtypeStruct((HALO, D), F32)],
        compiler_params=_params(("arbitrary",)),
    )(where, wi, wo, cw)


def _gather_weights(wi_full, wo_full, cw_full):
    halves = (D // 2, WOUT_SHARD // 2, HALO // 2)
    OWN_X, OWN_Y, VIA_Y, VIA_X = range(4)

    def body(_wi, _wo, _cw, wi_full, wo_full, cw_full, send, recv):
        x, y, c = _pos()
        x_nbr, y_nbr, diag = (1 - x, y), (x, 1 - y), (1 - x, 1 - y)

        def region(a, chip_xy, half, part=None):
            chip = 2 * chip_xy[0] + chip_xy[1]
            n, row = halves[a], half * halves[a]
            if part is not None:
                n = n // 2
                row = row + part * n
            if a == 0:
                return wi_full.at[_ds(row, n), _ds(chip * CHUNK, CHUNK, 128)]
            if a == 1:
                return wo_full.at[_ds(chip * WOUT_SHARD + row, n), :]
            return cw_full.at[_ds(row, n), _ds(chip * CONVW_SHARD, CONVW_SHARD, 128)]

        def copy(a, kind, piece, dev):
            k = 8 * a + kind
            return pltpu.make_async_remote_copy(src_ref=piece, dst_ref=piece, send_sem=send.at[k], recv_sem=recv.at[k],
                                                device_id=dev, device_id_type=MESH)

        def to_sibling(a, kind, piece):
            cp = copy(a, 4 + kind, piece, (x, y, 1 - c))
            cp.start()
            return cp

        sends = []
        for a in range(3):
            for kind, nbr in ((OWN_X, x_nbr), (OWN_Y, y_nbr)):
                cp = copy(a, kind, region(a, (x, y), c), (*nbr, c))
                cp.start()
                sends.append(cp)
        for a in range(3):
            got = region(a, x_nbr, c)
            copy(a, OWN_X, got, (*x_nbr, c)).wait_recv()
            onward = copy(a, VIA_Y, region(a, x_nbr, c, 0), (*y_nbr, c))
            onward.start()
            sends += [onward, to_sibling(a, OWN_X, got)]
            got = region(a, y_nbr, c)
            copy(a, OWN_Y, got, (*y_nbr, c)).wait_recv()
            onward = copy(a, VIA_X, region(a, y_nbr, c, 1), (*x_nbr, c))
            onward.start()
            sends += [onward, to_sibling(a, OWN_Y, got)]
        for a in range(3):
            got = region(a, diag, c, 0)
            copy(a, VIA_Y, got, (*y_nbr, c)).wait_recv()
            sends.append(to_sibling(a, VIA_Y, got))
            got = region(a, diag, c, 1)
            copy(a, VIA_X, got, (*x_nbr, c)).wait_recv()
            sends.append(to_sibling(a, VIA_X, got))
        for a in range(3):
            for kind, piece in ((OWN_X, region(a, x_nbr, 1 - c)), (OWN_Y, region(a, y_nbr, 1 - c)),
                                (VIA_Y, region(a, diag, 1 - c, 0)), (VIA_X, region(a, diag, 1 - c, 1))):
                copy(a, 4 + kind, piece, (x, y, 1 - c)).wait_recv()
        for cp in sends:
            cp.wait_send()

    n_sems = 3 * 8
    return pl.pallas_call(
        body, name="gather_weights",
        in_specs=[ANY, ANY, ANY], out_specs=[ANY, ANY, ANY], input_output_aliases={0: 0, 1: 1, 2: 2},
        out_shape=[jax.ShapeDtypeStruct((D, NCOL), BF16), jax.ShapeDtypeStruct((WOUT_ROWS, D), BF16),
                   jax.ShapeDtypeStruct((HALO, D), F32)],
        scratch_shapes=[pltpu.SemaphoreType.DMA((n_sems,)), pltpu.SemaphoreType.DMA((n_sems,))],
    )(wi_full, wo_full, cw_full)


def _exchange_halves(gi4, go4, small):
    def body(gi_ref, go_ref, sm_ref, ri_ref, ro_ref, rs_ref, send, recv):
        x, y, c = _pos()
        sib = (x, y, 1 - c)
        copies = [
            (gi_ref.at[:, _ds((1 - c) * (D // 2), D // 2), :], ri_ref),
            (go_ref.at[:, _ds((1 - c) * (WOUT_SHARD // 2), WOUT_SHARD // 2), :], ro_ref),
            (sm_ref, rs_ref),
        ]
        cps = [pltpu.make_async_remote_copy(src_ref=s_, dst_ref=d_, send_sem=send.at[k], recv_sem=recv.at[k],
                                            device_id=sib, device_id_type=MESH) for k, (s_, d_) in enumerate(copies)]
        for cp in cps:
            cp.start()
        for cp in cps:
            cp.wait()

    return pl.pallas_call(
        body, name="exchange_halves",
        in_specs=[ANY, ANY, ANY], out_specs=[ANY, ANY, ANY],
        out_shape=[jax.ShapeDtypeStruct((NCHIP, D // 2, CHUNK), BF16),
                   jax.ShapeDtypeStruct((NCHIP, WOUT_SHARD // 2, D), BF16),
                   jax.ShapeDtypeStruct((SMALL_ROWS, D), F32)],
        scratch_shapes=[pltpu.SemaphoreType.DMA((3,)), pltpu.SemaphoreType.DMA((3,))],
    )(gi4, go4, small)


def _add_halves(gi4, ri, go4, ro, small, rs):
    hi, ho = D // 2, WOUT_SHARD // 2

    def body(gi_ref, ri_ref, go_ref, ro_ref, sm_ref, rs_ref, pi_ref, po_ref, ps_ref):
        c = lax.axis_index("c")
        pi_ref[0] = (gi_ref[0, _ds(c * hi, hi), :].astype(F32) + ri_ref[0].astype(F32)).astype(BF16)
        po_ref[0] = (go_ref[0, _ds(c * ho, ho), :].astype(F32) + ro_ref[0].astype(F32)).astype(BF16)
        ps_ref[...] = sm_ref[...] + rs_ref[...]

    blk = lambda n, w: pl.BlockSpec((1, n, w), lambda k: (k, 0, 0))
    whole = pl.BlockSpec((SMALL_ROWS, D), lambda k: (0, 0))
    return pl.pallas_call(
        body, grid=(NCHIP,), name="add_halves",
        in_specs=[blk(D, CHUNK), blk(hi, CHUNK), blk(WOUT_SHARD, D), blk(ho, D), whole, whole],
        out_specs=[blk(hi, CHUNK), blk(ho, D), whole],
        out_shape=[jax.ShapeDtypeStruct((NCHIP, hi, CHUNK), BF16), jax.ShapeDtypeStruct((NCHIP, ho, D), BF16),
                   jax.ShapeDtypeStruct((SMALL_ROWS, D), F32)],
        compiler_params=_params(("arbitrary",)),
    )(gi4, ri, go4, ro, small, rs)


def _chip_exchange_copies(srcs, dsts, send, recv):
    x, y, c = _pos()
    me = 2 * x + y
    pairs = []
    for a in range(3):
        for j, (fx, fy) in enumerate(CHIP_FLIPS):
            px, py = _flip(x, fx), _flip(y, fy)
            peer = 2 * px + py
            k = 3 * a + j
            out = pltpu.make_async_remote_copy(
                src_ref=srcs[a] if a == 2 else srcs[a].at[peer], dst_ref=dsts[a].at[me],
                send_sem=send.at[k], recv_sem=recv.at[k], device_id=(px, py, c), device_id_type=MESH)
            got = dsts[a].at[peer]
            arrival = pltpu.make_async_remote_copy(
                src_ref=got, dst_ref=got, send_sem=send.at[k], recv_sem=recv.at[k],
                device_id=(px, py, c), device_id_type=MESH)
            pairs.append((out, arrival))
    return pairs


def _sum_chips(ri, ro, rs, pi, po, ps, where):
    def body(w_ref, ri_ref, ro_ref, rs_ref, pi_ref, po_ref, ps_ref, gi_ref, go_ref, gs_ref, g5_ref, loss_ref,
             acc_i, acc_o, acc_s):
        k = pl.program_id(0)
        accs = (acc_i, acc_o, acc_s)

        @pl.when(k == 0)
        def _():
            for acc in accs:
                acc[...] = jnp.zeros_like(acc)

        @pl.when(k == w_ref[0])
        def _():
            for acc, val in zip(accs, (pi_ref[0], po_ref[0], ps_ref[...])):
                acc[...] += val.astype(F32)

        @pl.when(k != w_ref[0])
        def _():
            for acc, ref in zip(accs, (ri_ref, ro_ref, rs_ref)):
                acc[...] += ref[0].astype(F32)

        @pl.when(k == NCHIP - 1)
        def _():
            gi_ref[0] = acc_i[...]
            go_ref[0] = acc_o[...]
            gs_ref[...] = acc_s[...]
            g5_ref[...] = jnp.zeros_like(g5_ref)
            for i, row in enumerate((ROW_CONV_B, ROW_LN_G, ROW_LN_B, ROW_FINAL_G)):
                g5_ref[i + 1:i + 2, :] = acc_s[row:row + 1, :]
            loss = jnp.sum(acc_s[ROW_LOSS:ROW_LOSS + 1, :], axis=1, keepdims=True)
            loss_ref[...] = jnp.broadcast_to(loss, loss_ref.shape)

    def sent(k, w):
        return jnp.where(k == w[0], (k + 1) % NCHIP, k)

    hi, ho = D // 2, WOUT_SHARD // 2
    const = lambda shape: pl.BlockSpec(shape, lambda k, w: (0,) * len(shape))
    grid_spec = pltpu.PrefetchScalarGridSpec(
        num_scalar_prefetch=1, grid=(NCHIP,),
        in_specs=[pl.BlockSpec((1, hi, CHUNK), lambda k, w: (sent(k, w), 0, 0)),
                  pl.BlockSpec((1, ho, D), lambda k, w: (sent(k, w), 0, 0)),
                  pl.BlockSpec((1, SMALL_ROWS, D), lambda k, w: (sent(k, w), 0, 0)),
                  pl.BlockSpec((1, hi, CHUNK), lambda k, w: (w[0], 0, 0)),
                  pl.BlockSpec((1, ho, D), lambda k, w: (w[0], 0, 0)),
                  const((SMALL_ROWS, D))],
        out_specs=[pl.BlockSpec((1, hi, CHUNK), lambda k, w: (w[1], 0, 0)),
                   pl.BlockSpec((1, ho, D), lambda k, w: (w[1], 0, 0)),
                   const((SMALL_ROWS, D)), const((8, D)), const((8, LANES))],
        scratch_shapes=[pltpu.VMEM((hi, CHUNK), F32), pltpu.VMEM((ho, D), F32), pltpu.VMEM((SMALL_ROWS, D), F32)])
    return pl.pallas_call(
        body, grid_spec=grid_spec, name="sum_chips",
        out_shape=[jax.ShapeDtypeStruct((2, hi, CHUNK), F32), jax.ShapeDtypeStruct((2, ho, D), F32),
                   jax.ShapeDtypeStruct((SMALL_ROWS, D), F32), jax.ShapeDtypeStruct((8, D), F32),
                   jax.ShapeDtypeStruct((8, LANES), F32)],
        compiler_params=_params(("arbitrary",)),
    )(where, ri, ro, rs, pi, po, ps)


def _exchange_results(gi2, go2, st):
    flips = [(fx, fy, fc) for fx in (0, 1) for fy in (0, 1) for fc in (0, 1)][1:]

    def body(_gi, _go, st_ref, gi_ref, go_ref, all_ref, send, recv, lsem):
        x, y, c = _pos()
        sib = (x, y, 1 - c)

        def half(k, ref, slot):
            return pltpu.make_async_remote_copy(src_ref=ref.at[slot], dst_ref=ref.at[slot], send_sem=send.at[k],
                                                recv_sem=recv.at[k], device_id=sib, device_id_type=MESH)

        def stat(k, src, slot, dev):
            return pltpu.make_async_remote_copy(src_ref=src, dst_ref=all_ref.at[slot], send_sem=send.at[k],
                                                recv_sem=recv.at[k], device_id=dev, device_id_type=MESH)

        mine = pltpu.make_async_copy(st_ref, all_ref.at[4 * x + 2 * y + c], lsem)
        mine.start()
        sends = [half(k, ref, c) for k, ref in enumerate((gi_ref, go_ref))]
        peers = [(_flip(x, fx), _flip(y, fy), _flip(c, fc)) for fx, fy, fc in flips]
        sends += [stat(2 + k, st_ref, 4 * x + 2 * y + c, dev) for k, dev in enumerate(peers)]
        for cp in sends:
            cp.start()
        for k, ref in enumerate((gi_ref, go_ref)):
            half(k, ref, 1 - c).wait_recv()
        for k, (px, py, pc) in enumerate(peers):
            slot = 4 * px + 2 * py + pc
            stat(2 + k, all_ref.at[slot], slot, (px, py, pc)).wait_recv()
        for cp in sends:
            cp.wait_send()
        mine.wait()

    n = 2 + len(flips)
    return pl.pallas_call(
        body, name="exchange_results",
        in_specs=[ANY, ANY, ANY], out_specs=[ANY, ANY, ANY], input_output_aliases={0: 0, 1: 1},
        out_shape=[jax.ShapeDtypeStruct((2, D // 2, CHUNK), F32), jax.ShapeDtypeStruct((2, WOUT_SHARD // 2, D), F32),
                   jax.ShapeDtypeStruct((NDEV, 8, D), F32)],
        scratch_shapes=[pltpu.SemaphoreType.DMA((n,)), pltpu.SemaphoreType.DMA((n,)), pltpu.SemaphoreType.DMA],
    )(gi2, go2, st)


def _adamw_math(w, g, m, v):
    m2 = ADAM_B1 * m + (1.0 - ADAM_B1) * g
    v2 = ADAM_B2 * v + (1.0 - ADAM_B2) * (g * g)
    m_hat = m2 / (1.0 - ADAM_B1 ** ADAM_STEP)
    v_hat = v2 / (1.0 - ADAM_B2 ** ADAM_STEP)
    delta = -ADAM_LR * (m_hat / (jnp.sqrt(v_hat) + ADAM_EPS) + ADAM_WD * w)
    return delta, m2, v2


def _adamw(w, g, m, v, name):
    rows, cols = w.shape
    tm = 256 if rows % 256 == 0 else rows

    def body(w_ref, g_ref, m_ref, v_ref, d_ref, m2_ref, v2_ref):
        d_ref[...], m2_ref[...], v2_ref[...] = _adamw_math(w_ref[...], g_ref[...], m_ref[...], v_ref[...])

    shape = jax.ShapeDtypeStruct(w.shape, F32)
    return pl.pallas_call(
        body, grid=(rows // tm,), name=name,
        in_specs=[_rows(tm, cols)] * 4, out_specs=[_rows(tm, cols)] * 3, out_shape=[shape] * 3,
        compiler_params=_params(("arbitrary",)),
    )(w, g, m, v)


def _adamw_vectors(g5, first_parts, ws, ms, vs):
    n = len(ws)

    def body(g_ref, parts_ref, *refs):
        ins, g0_ref, outs = refs[:3 * n], refs[3 * n], refs[3 * n + 1:]
        g0 = parts_ref[0, 0:1, :]
        for dev in range(1, NDEV):
            g0 = g0 + parts_ref[dev, 0:1, :]
        g0_ref[...] = g0
        for i in range(n):
            g = g0 if i == 0 else g_ref[i:i + 1, :]
            res = _adamw_math(ins[i][...], g, ins[n + i][...], ins[2 * n + i][...])
            for kind in range(3):
                outs[kind * n + i][...] = res[kind]

    shape = jax.ShapeDtypeStruct((1, D), F32)
    return pl.pallas_call(body, name="adamw_vectors", out_shape=[shape] * (1 + 3 * n), compiler_params=_params())(
        g5, first_parts, *ws, *ms, *vs)


def kernel(x, norm_g, w_in, conv_w, conv_b, conv_ln_g, conv_ln_b, w_out, final_norm_g, loss_target, m_norm_g, m_w_in, m_conv_w, m_conv_b, m_conv_ln_g, m_conv_ln_b, m_w_out, m_final_norm_g, v_norm_g, v_w_in, v_conv_w, v_conv_b, v_conv_ln_g, v_conv_ln_b, v_w_out, v_final_norm_g):
    chip = 2 * lax.axis_index("x") + lax.axis_index("y")
    where = jnp.stack([chip, lax.axis_index("c")]).astype(jnp.int32)
    taps_shard = jnp.pad(conv_w[0], ((0, HALO - CONV_K), (0, 0)))
    wi_full, wo_full, cw_full = _gather_weights(*_place_shards(w_in[0], w_out[0], taps_shard, where))

    gf = final_norm_g[None]
    dw_in4, dw_out, small, dproj_pieces, dx2 = _local_step(
        x[0], loss_target[0], norm_g, wi_full, cw_full, conv_b, conv_ln_g, conv_ln_b, wo_full, gf)
    dw_out4 = dw_out.reshape(NCHIP, WOUT_SHARD, D)

    ri, ro, rs = _exchange_halves(dw_in4, dw_out4, small)
    pi, po, ps = _add_halves(dw_in4, ri, dw_out4, ro, small, rs)
    grad_x, st_in, ri, ro, rs = _inproj_bwd_x(*dproj_pieces, wi_full, x[0], norm_g, dx2, pi, po, ps)
    gi2, go2, g_small, g5, loss8 = _sum_chips(ri, ro, rs, pi, po, ps, where)
    gi2, go2, norm_g_parts = _exchange_results(gi2, go2, st_in)
    g_w_in = gi2.reshape(D, CHUNK)
    g_w_out = go2.reshape(WOUT_SHARD, D)
    g_taps = lax.dynamic_slice(g_small, (ROW_TAPS, chip * CONVW_SHARD), (CONV_K, CONVW_SHARD))

    d_w_in, m2_w_in, v2_w_in = _adamw(w_in[0], g_w_in, m_w_in[0], v_w_in[0], "adamw_w_in")
    d_w_out, m2_w_out, v2_w_out = _adamw(w_out[0], g_w_out, m_w_out[0], v_w_out[0], "adamw_w_out")
    d_taps, m2_taps, v2_taps = _adamw(conv_w[0], g_taps, m_conv_w[0], v_conv_w[0], "adamw_conv_w")
    g_norm, *vec = _adamw_vectors(
        g5, norm_g_parts,
        (norm_g, conv_b, conv_ln_g, conv_ln_b, gf),
        (m_norm_g, m_conv_b, m_conv_ln_g, m_conv_ln_b, m_final_norm_g[None]),
        (v_norm_g, v_conv_b, v_conv_ln_g, v_conv_ln_b, v_final_norm_g[None]))
    d_vec, m2_vec, v2_vec = vec[0:5], vec[5:10], vec[10:15]

    def weight_order(ng, wi, cw, cb, lg, lb, wo, fg):
        return (ng, wi[None], cw[None], cb, lg, lb, wo[None], fg[0])

    grads = weight_order(g_norm, g_w_in, g_taps, g5[1:2], g5[2:3], g5[3:4], g_w_out, g5[4:5])
    deltas = weight_order(d_vec[0], d_w_in, d_taps, d_vec[1], d_vec[2], d_vec[3], d_w_out, d_vec[4])
    new_m = weight_order(m2_vec[0], m2_w_in, m2_taps, m2_vec[1], m2_vec[2], m2_vec[3], m2_w_out, m2_vec[4])
    new_v = weight_order(v2_vec[0], v2_w_in, v2_taps, v2_vec[1], v2_vec[2], v2_vec[3], v2_w_out, v2_vec[4])
    return (loss8[0, 0], grad_x[None], *grads, *deltas, *new_m, *new_v)
```

```python
import jax
import jax.numpy as jnp
from jax import lax
from jax.experimental import pallas as pl
from jax.experimental.pallas import tpu as pltpu

F32 = jnp.float32
BF16 = jnp.bfloat16

S = 4096
D = 1024
LANES = 128
HD = 64
NKV = 4
GQ = 4
KVW = NKV * HD
NCOL = 5632
CONV_K = 31
HALO = 32
BLK = 128
PATTERNS = (1, 4, 16)
NORM_EPS = 1e-6
LN_EPS = 1e-5
NEG = -1e30
OFF_Q, OFF_K, OFF_V, OFF_AG, OFF_CV, OFF_CG, OFF_CGATE = 0, 1024, 1280, 1536, 2560, 3584, 4608
NCHIP = 4
CHUNK = NCOL // NCHIP
WOUT_ROWS = 2 * D
WOUT_SHARD = WOUT_ROWS // NCHIP
CONVW_SHARD = D // NCHIP

ADAM_LR, ADAM_B1, ADAM_B2, ADAM_EPS, ADAM_WD, ADAM_STEP = 0.001, 0.9, 0.999, 1e-08, 0.01, 10

VMEM_LIMIT = 56 * 1024 * 1024


def _params(sem=None, vmem=VMEM_LIMIT):
    return pltpu.CompilerParams(dimension_semantics=sem, vmem_limit_bytes=vmem)


def _sigmoid(a):
    return 0.5 * jnp.tanh(0.5 * a) + 0.5


def _rows(tm, width):
    return pl.BlockSpec((tm, width), lambda i: (i, 0))


def _slabs(n):
    return jax.ShapeDtypeStruct((n, S, LANES), F32)


def _slab_rows(n, tm):
    return pl.BlockSpec((n, tm, LANES), lambda i: (0, i, 0))


def _resident(shape):
    return pl.BlockSpec(shape, lambda *_: (0,) * len(shape), pipeline_mode=pl.Buffered(1))


def _dot(a, b):
    return jnp.dot(a, b, preferred_element_type=F32)


def _dot_nt(a, b):
    return lax.dot_general(a, b, (((1,), (1,)), ((), ())), preferred_element_type=F32)


def _dot_tn(a, b):
    return lax.dot_general(a, b, (((0,), (0,)), ((), ())), preferred_element_type=F32)


def _inproj_fwd(x, g1, w_bf):
    tm = 512

    def body(x_ref, g_ref, w_ref, h_ref, q_ref, k_ref, v_ref, ag_ref, cv_ref, cg_ref, cgate_ref):
        xt = x_ref[...]
        r = lax.rsqrt(jnp.mean(xt * xt, axis=-1, keepdims=True) + NORM_EPS)
        h = (xt * r * g_ref[...]).astype(BF16)
        h_ref[...] = h
        q = _dot(h, w_ref[:, OFF_Q:OFF_Q + D]) * (HD ** -0.5)
        kv = _dot(h, w_ref[:, OFF_K:OFF_K + 2 * KVW])
        for sl in range(D // LANES):
            q_ref[sl] = q[:, sl * LANES:(sl + 1) * LANES]
        for sl in range(KVW // LANES):
            k_ref[sl] = kv[:, sl * LANES:(sl + 1) * LANES]
            v_ref[sl] = kv[:, KVW + sl * LANES:KVW + (sl + 1) * LANES]
        ag_ref[...] = _dot(h, w_ref[:, OFF_AG:OFF_AG + D])
        cv_ref[...] = _dot(h, w_ref[:, OFF_CV:OFF_CV + D])
        cg_ref[...] = _dot(h, w_ref[:, OFF_CG:OFF_CG + D])
        cgate_ref[...] = _dot(h, w_ref[:, OFF_CGATE:OFF_CGATE + D])

    big = jax.ShapeDtypeStruct((S, D), F32)
    return pl.pallas_call(
        body, grid=(S // tm,), name="inproj_fwd",
        in_specs=[_rows(tm, D), _resident((1, D)), _resident((D, NCOL))],
        out_specs=[_rows(tm, D), _slab_rows(D // LANES, tm), _slab_rows(KVW // LANES, tm), _slab_rows(KVW // LANES, tm),
                   _rows(tm, D), _rows(tm, D), _rows(tm, D), _rows(tm, D)],
        out_shape=[jax.ShapeDtypeStruct((S, D), BF16), _slabs(D // LANES), _slabs(KVW // LANES), _slabs(KVW // LANES),
                   big, big, big, big],
        compiler_params=_params(("arbitrary",)),
    )(x, g1, w_bf)


def _bias_table(d):
    h = jnp.arange(NKV * GQ, dtype=F32)
    slopes = jnp.exp2(-8.0 * (h + 1.0) / (NKV * GQ))
    qi = jnp.arange(BLK)[:, None]
    kj = jnp.arange(2 * BLK)[None, :]
    dist = BLK + qi - kj
    window = (dist >= 0) & (dist <= BLK)
    bias = -slopes[:, None, None] * (dist * d).astype(F32)[None]
    has_prev = jnp.stack([jnp.broadcast_to(kj >= BLK, (BLK, 2 * BLK)), jnp.ones((BLK, 2 * BLK), bool)])
    valid = window[None] & has_prev
    tab = jnp.where(valid[:, None], bias[None], NEG)
    return tab.reshape(2, NKV, GQ * BLK, 2 * BLK)


def _sub_rows(start, d):
    if d == 1:
        return pl.ds(pl.multiple_of(start, BLK), BLK)
    return pl.ds(start, BLK, stride=d)


NHEAD = NKV * GQ
CHUNK_ROWS = 2048
BLOCKS_PER_CHUNK = CHUNK_ROWS // BLK


def _low_lanes(rows=BLK):
    return lax.broadcasted_iota(jnp.int32, (rows, LANES), 1) < HD


def _block_start(idx, d):
    shift = d.bit_length() - 1
    b, r = lax.shift_right_logical(idx, shift), lax.bitwise_and(idx, d - 1)
    start = b * (BLK * d) + r
    return b, start, jnp.maximum(start - BLK * d, r)


def _stack_heads(ref, rows):
    low = _low_lanes()
    t0, t1 = ref[0, rows, :], ref[1, rows, :]
    return jnp.concatenate([jnp.where(low, t0, 0.0), jnp.where(low, 0.0, t0),
                            jnp.where(low, t1, 0.0), jnp.where(low, 0.0, t1)], axis=0).astype(BF16)


def _unstack_heads(dup):
    low = _low_lanes()
    return (jnp.where(low, dup[0:BLK], dup[BLK:2 * BLK]), jnp.where(low, dup[2 * BLK:3 * BLK], dup[3 * BLK:4 * BLK]))


def _kv_dup(ref, prow, rows, odd):
    t = jnp.concatenate([ref[0, prow, :], ref[0, rows, :]], axis=0)
    swapped = pltpu.roll(t, HD, axis=1)
    keep = jnp.logical_xor(_low_lanes(2 * BLK), odd)
    return jnp.where(keep, t, swapped).astype(BF16)


PIECES = 3


def _by_head(tiles):
    lane = lax.broadcasted_iota(jnp.int32, tiles[0].shape, 1)
    out = tiles[0]
    for g in range(1, GQ):
        out = jnp.where(lax.bitwise_and(lane, GQ - 1) == g, tiles[g], out)
    return out


def _minus_in_pieces(x):
    lane = lax.broadcasted_iota(jnp.int32, x.shape, 1)
    hi = (-x).astype(BF16).astype(F32)
    rest = -x - hi
    mid = rest.astype(BF16).astype(F32)
    lo = (rest - mid).astype(BF16).astype(F32)
    return jnp.where(lane < GQ, hi, jnp.where(lane < 2 * GQ, mid, jnp.where(lane < PIECES * GQ, lo, 0.0)))


def _attn_fwd(q, k, v, tables, a_gate):
    tm = 256
    width = GQ * HD

    lane_out = jnp.arange(LANES)[None, :] // HD
    spread_sel = jnp.stack([jnp.arange(LANES)[:, None] == 2 * half + lane_out for half in range(2)]).astype(BF16)

    def body(q_ref, k_ref, v_ref, b1_ref, b2_ref, b3_ref, ag_ref, sel_ref, o_ref, lse_ref, y_ref, op, lp):
        odd = pl.program_id(0) % 2 == 1
        chunk = pl.program_id(1)
        ones = jnp.ones((2 * BLK, LANES), BF16)

        for pat, (d, b_ref) in enumerate(zip(PATTERNS, (b1_ref, b2_ref, b3_ref))):
            def block(idx, carry, pat=pat, d=d, b_ref=b_ref):
                b, start, pstart = _block_start(chunk * BLOCKS_PER_CHUNK + idx, d)
                rows, prow = _sub_rows(start, d), _sub_rows(pstart, d)
                mine = _sub_rows(start - chunk * CHUNK_ROWS, d)
                qs = _stack_heads(q_ref, mine)
                kw = _kv_dup(k_ref, prow, rows, odd)
                vw = _kv_dup(v_ref, prow, rows, odd)
                s = _dot_nt(qs, kw) + b_ref[jnp.minimum(b, 1), 0]
                m = jnp.max(s, axis=1, keepdims=True)
                p = jnp.exp(s - m).astype(BF16)
                ol = _dot(p, jnp.concatenate([vw, ones], axis=1))
                l = ol[:, LANES:]
                op[pat, 0, mine, :], op[pat, 1, mine, :] = _unstack_heads(ol[:, :LANES] / l)
                lp[pat, mine, :] = _by_head([(m + jnp.log(l))[g * BLK:(g + 1) * BLK] for g in range(GQ)])
                return carry

            lax.fori_loop(0, BLOCKS_PER_CHUNK, block, 0, unroll=2)

        def mix(t, carry):
            r = pl.ds(pl.multiple_of(t * tm, tm), tm)
            low = _low_lanes(tm)
            a, b, c = lp[0, r, :], lp[1, r, :], lp[2, r, :]
            m = jnp.maximum(jnp.maximum(a, b), c)
            ea, eb, ec = jnp.exp(a - m), jnp.exp(b - m), jnp.exp(c - m)
            den = ea + eb + ec
            lse_ref[0, r, :] = _minus_in_pieces(m + jnp.log(den))
            inv = 1.0 / den
            for half in range(2):
                def spread(w):
                    hi = w.astype(BF16)
                    lo = (w - hi.astype(F32)).astype(BF16)
                    return _dot(hi, sel_ref[half]) + _dot(lo, sel_ref[half])

                o = (spread(ea * inv) * op[0, half, r, :] + spread(eb * inv) * op[1, half, r, :]
                     + spread(ec * inv) * op[2, half, r, :])
                o_ref[half, r, :] = o
                cols = slice(half * LANES, (half + 1) * LANES)
                ag = ag_ref[r, cols]
                y_ref[r, cols] = (o * (ag * _sigmoid(ag))).astype(BF16)
            return carry

        lax.fori_loop(0, CHUNK_ROWS // tm, mix, 0)

    q_like = pl.BlockSpec((2, CHUNK_ROWS, LANES), lambda j, c: (j, c, 0))
    per_kv = pl.BlockSpec((1, CHUNK_ROWS, LANES), lambda j, c: (j, c, 0))
    kv = pl.BlockSpec((1, S, LANES), lambda j, c: (j // 2, 0, 0))
    bias_spec = pl.BlockSpec((2, 1, GQ * BLK, 2 * BLK), lambda j, c: (0, j, 0, 0))
    group_cols = pl.BlockSpec((CHUNK_ROWS, width), lambda j, c: (c, j))
    return pl.pallas_call(
        body, grid=(NKV, S // CHUNK_ROWS), name="attn_fwd",
        in_specs=[q_like, kv, kv, bias_spec, bias_spec, bias_spec, group_cols,
                  pl.BlockSpec((2, LANES, LANES), lambda j, c: (0, 0, 0))],
        out_specs=[q_like, per_kv, group_cols],
        out_shape=[_slabs(D // LANES), _slabs(NKV), jax.ShapeDtypeStruct((S, D), BF16)],
        scratch_shapes=[pltpu.VMEM((len(PATTERNS), 2, CHUNK_ROWS, LANES), F32),
                        pltpu.VMEM((len(PATTERNS), CHUNK_ROWS, LANES), F32)],
        compiler_params=_params(("arbitrary", "arbitrary")),
    )(q, k, v, *tables, a_gate, spread_sel)


def _head_sum_selectors():
    lane_in = jnp.arange(LANES)[:, None] // HD
    return jnp.stack([jnp.broadcast_to(lane_in == h, (LANES, LANES)) for h in range(2)]).astype(BF16)


def _attn_gate_bwd(dy_att, o, a_gate, selectors):
    tm = 256

    def body(dy_ref, o_ref, ag_ref, e_ref, do_ref, dag_ref, delta_ref):
        for j in range(NKV):
            deltas = []
            for sl in (2 * j, 2 * j + 1):
                cols = slice(sl * LANES, (sl + 1) * LANES)
                dy, ag, o_ = dy_ref[:, cols], ag_ref[:, cols], o_ref[sl]
                sg = _sigmoid(ag)
                do = dy * (ag * sg)
                do_ref[sl] = do
                dag_ref[:, cols] = (dy * o_ * (sg * (1.0 + ag * (1.0 - sg)))).astype(BF16)
                prod = do * o_
                hi = prod.astype(BF16)
                lo = (prod - hi.astype(F32)).astype(BF16)
                deltas += [_dot(hi, e_ref[h]) + _dot(lo, e_ref[h]) for h in range(2)]
            delta_ref[j] = _minus_in_pieces(_by_head(deltas))

    return pl.pallas_call(
        body, grid=(S // tm,), name="attn_gate_bwd",
        in_specs=[_rows(tm, D), _slab_rows(D // LANES, tm), _rows(tm, D), _resident((2, LANES, LANES))],
        out_specs=[_slab_rows(D // LANES, tm), _rows(tm, D), _slab_rows(NKV, tm)],
        out_shape=[_slabs(D // LANES), jax.ShapeDtypeStruct((S, D), BF16), _slabs(NKV)],
        compiler_params=_params(("arbitrary",)),
    )(dy_att, o, a_gate, selectors)


def _own_pieces(tile):
    lane = lax.broadcasted_iota(jnp.int32, tile.shape, 1)
    head = jnp.where(lane < PIECES * GQ, lax.bitwise_and(lane, GQ - 1), -1)
    return jnp.concatenate([jnp.where(head == g, tile, 0.0) for g in range(GQ)], axis=0).astype(BF16)


def _attn_bwd(q, k, v, do, lse, delta, bias, d):
    def body(q_ref, do_ref, l_ref, dl_ref, k_ref, v_ref, b_ref, dq_ref, dkv_ref, acc):
        odd = pl.program_id(0) % 2 == 1
        chunk = pl.program_id(1)
        ones = (lax.broadcasted_iota(jnp.int32, (2 * BLK, LANES), 1) < PIECES * GQ).astype(BF16)

        def in_acc(block_idx):
            return pl.ds(pl.multiple_of(block_idx * BLK, BLK), BLK)

        @pl.when(chunk == 0)
        def _():
            acc[...] = jnp.zeros_like(acc)

        def block(idx, carry):
            idx = chunk * BLOCKS_PER_CHUNK + idx
            b, start, pstart = _block_start(idx, d)
            rows, prow = _sub_rows(start, d), _sub_rows(pstart, d)
            mine = _sub_rows(start - chunk * CHUNK_ROWS, d)
            qs = _stack_heads(q_ref, mine)
            dos = _stack_heads(do_ref, mine)
            kw = _kv_dup(k_ref, prow, rows, odd)
            vw = _kv_dup(v_ref, prow, rows, odd)
            s = _dot_nt(jnp.concatenate([qs, _own_pieces(l_ref[0, mine, :])], axis=1),
                        jnp.concatenate([kw, ones], axis=1)) + b_ref[jnp.minimum(b, 1), 0]
            p = jnp.exp(s)
            dv2 = _dot_tn(p.astype(BF16), dos)
            dp = _dot_nt(jnp.concatenate([dos, _own_pieces(dl_ref[0, mine, :])], axis=1),
                         jnp.concatenate([vw, ones], axis=1))
            ds = (p * dp).astype(BF16)
            dq_ref[0, mine, :], dq_ref[1, mine, :] = _unstack_heads(_dot(ds, kw))
            dk2 = _dot_tn(ds, qs)
            dkv = jnp.where(_low_lanes(2 * BLK), dk2 + pltpu.roll(dk2, HD, axis=1), dv2 + pltpu.roll(dv2, HD, axis=1))
            acc[in_acc(idx), :] = acc[in_acc(idx), :] + dkv[BLK:]
            before = jnp.where(b >= 1, idx - d, idx)
            acc[in_acc(before), :] = acc[in_acc(before), :] + dkv[:BLK]
            return carry

        lax.fori_loop(0, BLOCKS_PER_CHUNK, block, 0, unroll=8)

        @pl.when(chunk == S // CHUNK_ROWS - 1)
        def _():
            def place(idx, carry):
                _, start, _ = _block_start(idx, d)
                dkv_ref[0, _sub_rows(start, d), :] = acc[in_acc(idx), :]
                return carry

            lax.fori_loop(0, S // BLK, place, 0, unroll=4)

    q_like = pl.BlockSpec((2, CHUNK_ROWS, LANES), lambda j, c: (j, c, 0))
    pieces = pl.BlockSpec((1, CHUNK_ROWS, LANES), lambda j, c: (j, c, 0))
    kv = pl.BlockSpec((1, S, LANES), lambda j, c: (j // 2, 0, 0))
    per_kv = pl.BlockSpec((1, S, LANES), lambda j, c: (j, 0, 0))
    bias_spec = pl.BlockSpec((2, 1, GQ * BLK, 2 * BLK), lambda j, c: (0, j, 0, 0))
    return pl.pallas_call(
        body, grid=(NKV, S // CHUNK_ROWS), name=f"attn_bwd_d{d}",
        in_specs=[q_like, q_like, pieces, pieces, kv, kv, bias_spec],
        out_specs=[q_like, per_kv],
        out_shape=[_slabs(D // LANES), _slabs(NKV)],
        scratch_shapes=[pltpu.VMEM((S, LANES), F32)],
        compiler_params=_params(("arbitrary", "arbitrary")),
    )(q, do, lse, delta, k, v, bias)


CONV_T = 256


def _halo_before(i):
    return (jnp.maximum(i * (CONV_T // HALO) - 1, 0), 0)


def _halo_after(i):
    return (jnp.minimum((i + 1) * (CONV_T // HALO), S // HALO - 1), 0)


SUBLANES = 8
NCH = D // LANES
GROUP = SUBLANES * SUBLANES


def _comb(ref, cb, base):
    return ref[cb, pl.ds(base, SUBLANES, stride=SUBLANES), :]


def _taps(w_ref, cols):
    return [jnp.broadcast_to(w_ref[j:j + 1, cols], (SUBLANES, LANES)) for j in range(CONV_K)]


def _conv_fwd(c_val, c_glu, c_gate, conv_w, conv_b, ln_g, ln_b):
    T = CONV_T

    def body(cv_ref, cg_ref, cvh_ref, cgh_ref, gate_ref, w_ref, b_ref, lg_ref, lb_ref, u_ref, y_ref, win, us):
        i = pl.program_id(0)
        for cb in range(NCH):
            cols = slice(cb * LANES, (cb + 1) * LANES)
            win[cb, HALO:HALO + T, :] = cv_ref[:, cols] * _sigmoid(cg_ref[:, cols])
            win[cb, 0:HALO, :] = jnp.where(i > 0, cvh_ref[:, cols] * _sigmoid(cgh_ref[:, cols]), 0.0)
        for cb in range(NCH):
            cols = slice(cb * LANES, (cb + 1) * LANES)
            taps = _taps(w_ref, cols)
            bias = jnp.broadcast_to(b_ref[:, cols], (SUBLANES, LANES))

            def group(g, carry):
                for b in range(SUBLANES):
                    base = g * GROUP + b
                    acc = bias
                    for j in range(CONV_K):
                        acc = acc + taps[j] * _comb(win, cb, base + (HALO - (CONV_K - 1) + j))
                    us[cb, pl.ds(base, SUBLANES, stride=SUBLANES), :] = acc
                return carry

            lax.fori_loop(0, T // GROUP, group, 0)
        total = us[0]
        for cb in range(1, NCH):
            total = total + us[cb]
        mu = jnp.sum(total, axis=-1, keepdims=True) * (1.0 / D)
        sq = jnp.zeros((T, LANES), F32)
        for cb in range(NCH):
            uc = us[cb] - mu
            sq = sq + uc * uc
        rstd = lax.rsqrt(jnp.sum(sq, axis=-1, keepdims=True) * (1.0 / D) + LN_EPS)
        for cb in range(NCH):
            cols = slice(cb * LANES, (cb + 1) * LANES)
            u = us[cb]
            u_ref[:, cols] = u
            nrm = (u - mu) * rstd * lg_ref[:, cols] + lb_ref[:, cols]
            gate = gate_ref[:, cols]
            y_ref[:, cols] = (nrm * _sigmoid(nrm) * (gate * _sigmoid(gate))).astype(BF16)

    halo = pl.BlockSpec((HALO, D), _halo_before)
    return pl.pallas_call(
        body, grid=(S // T,), name="conv_fwd",
        in_specs=[_rows(T, D), _rows(T, D), halo, halo, _rows(T, D),
                  _resident((HALO, D)), _resident((1, D)), _resident((1, D)), _resident((1, D))],
        out_specs=[_rows(T, D), _rows(T, D)],
        out_shape=[jax.ShapeDtypeStruct((S, D), F32), jax.ShapeDtypeStruct((S, D), BF16)],
        scratch_shapes=[pltpu.VMEM((NCH, T + HALO, LANES), F32), pltpu.VMEM((NCH, T, LANES), F32)],
        compiler_params=_params(("arbitrary",)),
    )(c_val, c_glu, c_val, c_glu, c_gate, conv_w, conv_b, ln_g, ln_b)


def _conv_bwd_rows(u, c_gate, dy_conv, ln_g, ln_b):
    tm = 256

    def body(u_ref, gate_ref, dy_ref, lg_ref, lb_ref, du_ref, dgate_ref, st_ref):
        @pl.when(pl.program_id(0) == 0)
        def _():
            st_ref[...] = jnp.zeros_like(st_ref)

        u, gate, dy = u_ref[...], gate_ref[...], dy_ref[...]
        mu = jnp.mean(u, axis=-1, keepdims=True)
        uc = u - mu
        rstd = lax.rsqrt(jnp.mean(uc * uc, axis=-1, keepdims=True) + LN_EPS)
        z = uc * rstd
        nrm = z * lg_ref[...] + lb_ref[...]
        sn, sg = _sigmoid(nrm), _sigmoid(gate)
        dgate_ref[...] = (dy * (nrm * sn) * (sg * (1.0 + gate * (1.0 - sg)))).astype(BF16)
        dn = dy * (gate * sg) * (sn * (1.0 + nrm * (1.0 - sn)))
        dz = dn * lg_ref[...]
        du = rstd * (dz - jnp.mean(dz, axis=-1, keepdims=True) - z * jnp.mean(dz * z, axis=-1, keepdims=True))
        du_ref[...] = du
        st_ref[0:1, :] += jnp.sum(dn * z, axis=0, keepdims=True)
        st_ref[1:2, :] += jnp.sum(dn, axis=0, keepdims=True)
        st_ref[2:3, :] += jnp.sum(du, axis=0, keepdims=True)

    big = jax.ShapeDtypeStruct((S, D), F32)
    return pl.pallas_call(
        body, grid=(S // tm,), name="conv_bwd_rows",
        in_specs=[_rows(tm, D)] * 3 + [_resident((1, D)), _resident((1, D))],
        out_specs=[_rows(tm, D), _rows(tm, D), pl.BlockSpec((8, D), lambda i: (0, 0))],
        out_shape=[big, jax.ShapeDtypeStruct((S, D), BF16), jax.ShapeDtypeStruct((8, D), F32)],
        compiler_params=_params(("arbitrary",)),
    )(u, c_gate, dy_conv, ln_g, ln_b)


def _conv_bwd_taps(du, c_val, c_glu, conv_w):
    T = CONV_T
    last = S // T - 1

    def body(du_ref, dua_ref, cv_ref, cg_ref, cvh_ref, cgh_ref, w_ref, dcv_ref, dcg_ref, dw_ref,
             hwin, dwin, dhs, dw_acc):
        i = pl.program_id(0)

        @pl.when(i == 0)
        def _():
            dw_acc[...] = jnp.zeros_like(dw_acc)

        for cb in range(NCH):
            cols = slice(cb * LANES, (cb + 1) * LANES)
            hwin[cb, HALO:HALO + T, :] = cv_ref[:, cols] * _sigmoid(cg_ref[:, cols])
            hwin[cb, 0:HALO, :] = jnp.where(i > 0, cvh_ref[:, cols] * _sigmoid(cgh_ref[:, cols]), 0.0)
            dwin[cb, 0:T, :] = du_ref[:, cols]
            dwin[cb, T:T + HALO, :] = jnp.where(i < last, dua_ref[:, cols], 0.0)
        for cb in range(NCH):
            cols = slice(cb * LANES, (cb + 1) * LANES)
            taps = _taps(w_ref, cols)

            def group_dh(g, carry):
                for b in range(SUBLANES):
                    base = g * GROUP + b
                    acc = jnp.zeros((SUBLANES, LANES), F32)
                    for j in range(CONV_K):
                        acc = acc + taps[j] * _comb(dwin, cb, base + (CONV_K - 1 - j))
                    dhs[cb, pl.ds(base, SUBLANES, stride=SUBLANES), :] = acc
                return carry

            lax.fori_loop(0, T // GROUP, group_dh, 0)

            def group_dw(g, sums):
                for b in range(SUBLANES):
                    base = g * GROUP + b
                    d = _comb(dwin, cb, base)
                    sums = tuple(sums[j] + d * _comb(hwin, cb, base + (HALO - (CONV_K - 1) + j))
                                 for j in range(CONV_K))
                return sums

            sums = lax.fori_loop(0, T // GROUP, group_dw, tuple(dw_acc[j, :, cols] for j in range(CONV_K)))
            for j in range(CONV_K):
                dw_acc[j, :, cols] = sums[j]
            dh = dhs[cb]
            cv, sg = cv_ref[:, cols], _sigmoid(cg_ref[:, cols])
            dcv_ref[:, cols] = (dh * sg).astype(BF16)
            dcg_ref[:, cols] = (dh * cv * (sg * (1.0 - sg))).astype(BF16)

        @pl.when(i == last)
        def _():
            dw_ref[...] = jnp.zeros_like(dw_ref)
            for j in range(CONV_K):
                dw_ref[j:j + 1, :] = jnp.sum(dw_acc[j], axis=0, keepdims=True)

    before = pl.BlockSpec((HALO, D), _halo_before)
    after = pl.BlockSpec((HALO, D), _halo_after)
    big = jax.ShapeDtypeStruct((S, D), BF16)
    return pl.pallas_call(
        body, grid=(S // T,), name="conv_bwd_taps",
        in_specs=[_rows(T, D), after, _rows(T, D), _rows(T, D), before, before, _resident((HALO, D))],
        out_specs=[_rows(T, D), _rows(T, D), pl.BlockSpec((HALO, D), lambda i: (0, 0))],
        out_shape=[big, big, jax.ShapeDtypeStruct((HALO, D), F32)],
        scratch_shapes=[pltpu.VMEM((NCH, T + HALO, LANES), F32), pltpu.VMEM((NCH, T + HALO, LANES), F32),
                        pltpu.VMEM((NCH, T, LANES), F32), pltpu.VMEM((CONV_K, SUBLANES, D), F32)],
        compiler_params=_params(("arbitrary",)),
    )(du, du, c_val, c_glu, c_val, c_glu, conv_w)


def _outproj_loss(y_att, y_conv, w_out_bf, x, target, gf):
    tm = 512

    def body(ya_ref, yc_ref, w_ref, x_ref, t_ref, gf_ref, dx2_ref, dya_ref, dyc_ref, dw_ref, st_ref, acc):
        @pl.when(pl.program_id(0) == 0)
        def _():
            acc[...] = jnp.zeros_like(acc)
            st_ref[...] = jnp.zeros_like(st_ref)

        ya, yc = ya_ref[...], yc_ref[...]
        x2 = x_ref[...] + _dot(ya, w_ref[0:D, :]) + _dot(yc, w_ref[D:2 * D, :])
        r = lax.rsqrt(jnp.mean(x2 * x2, axis=-1, keepdims=True) + NORM_EPS)
        xn = x2 * r
        err = xn * gf_ref[...] - t_ref[...]
        dout = err * (1.0 / D)
        dxn = dout * gf_ref[...]
        dx2 = r * (dxn - xn * jnp.mean(dxn * xn, axis=-1, keepdims=True))
        dx2_ref[...] = dx2
        dx2b = dx2.astype(BF16)
        dya_ref[...] = _dot_nt(dx2b, w_ref[0:D, :])
        dyc_ref[...] = _dot_nt(dx2b, w_ref[D:2 * D, :])
        acc[0:D, :] += _dot_tn(ya, dx2b)
        acc[D:2 * D, :] += _dot_tn(yc, dx2b)
        st_ref[0:1, :] += jnp.sum(dout * xn, axis=0, keepdims=True)
        st_ref[1:2, :] += jnp.sum(err * err, axis=0, keepdims=True) * (0.5 / D)

        @pl.when(pl.program_id(0) == S // tm - 1)
        def _():
            dw_ref[...] = acc[...].astype(BF16)

    big = jax.ShapeDtypeStruct((S, D), F32)
    return pl.pallas_call(
        body, grid=(S // tm,), name="outproj_loss",
        in_specs=[_rows(tm, D), _rows(tm, D), _resident((WOUT_ROWS, D)), _rows(tm, D), _rows(tm, D), _resident((1, D))],
        out_specs=[_rows(tm, D), _rows(tm, D), _rows(tm, D),
                   pl.BlockSpec((WOUT_ROWS, D), lambda i: (0, 0)), pl.BlockSpec((8, D), lambda i: (0, 0))],
        out_shape=[big, big, big, jax.ShapeDtypeStruct((WOUT_ROWS, D), BF16), jax.ShapeDtypeStruct((8, D), F32)],
        scratch_shapes=[pltpu.VMEM((WOUT_ROWS, D), F32)],
        compiler_params=_params(("arbitrary",)),
    )(y_att, y_conv, w_out_bf, x, target, gf)


UNITS_PER_CHUNK = CHUNK // LANES


def _dproj_unit(u, dqs, dkvs, gates, rows):
    if u < OFF_K // LANES:
        return ((dqs[0][u] + dqs[1][u] + dqs[2][u]) * (HD ** -0.5)).astype(BF16)
    if u < OFF_AG // LANES:
        w = u - OFF_K // LANES
        ta, tb = (dkvs[0][j] + dkvs[1][j] + dkvs[2][j] for j in (2 * (w % 2), 2 * (w % 2) + 1))
        low = _low_lanes(rows)
        if w < 2:
            return jnp.where(low, ta, pltpu.roll(tb, HD, axis=1)).astype(BF16)
        return jnp.where(low, pltpu.roll(ta, HD, axis=1), tb).astype(BF16)
    g, sl = divmod(u - OFF_AG // LANES, D // LANES)
    return gates[g][:, sl * LANES:(sl + 1) * LANES]


def _dproj_sources(units, dqs, dkvs, gates, rows):
    use_q = any(u < OFF_K // LANES for u in units)
    use_kv = any(OFF_K // LANES <= u < OFF_AG // LANES for u in units)
    use_g = sorted({(u - OFF_AG // LANES) // (D // LANES) for u in units if u >= OFF_AG // LANES})
    args = (list(dqs) if use_q else []) + (list(dkvs) if use_kv else []) + [gates[g] for g in use_g]
    specs = ([_slab_rows(D // LANES, rows)] * 3 if use_q else []) + ([_slab_rows(NKV, rows)] * 3 if use_kv else []) \
        + [_rows(rows, D)] * len(use_g)

    def pick(refs):
        refs = list(refs)
        q_refs = [refs.pop(0) for _ in range(3)] if use_q else None
        kv_refs = [refs.pop(0) for _ in range(3)] if use_kv else None
        return q_refs, kv_refs, {g: refs.pop(0) for g in use_g}

    return args, specs, pick


def _inproj_bwd_x(dqs, dkvs, gates, w_bf, x, g1, dx2, pi, po, ps):
    tm = 256
    last = S // tm - 1
    units = range(NCOL // LANES)
    pieces, piece_specs, pick = _dproj_sources(units, dqs, dkvs, gates, tm)

    def body(*refs):
        piece_refs, refs = refs[:len(pieces)], refs[len(pieces):]
        (w_ref, x_ref, g_ref, dx2_ref, pi_ref, po_ref, ps_ref,
         gx_ref, st_ref, ri_ref, ro_ref, rs_ref, dp_ref, send, recv) = refs
        i = pl.program_id(0)
        copies = _chip_exchange_copies((pi_ref, po_ref, ps_ref), (ri_ref, ro_ref, rs_ref), send, recv)

        @pl.when(i == 0)
        def _():
            st_ref[...] = jnp.zeros_like(st_ref)
            for out, _ in copies:
                out.start()

        sources = pick(piece_refs)
        for u in units:
            dp_ref[:, u * LANES:(u + 1) * LANES] = _dproj_unit(u, *sources, tm)
        dh = _dot_nt(dp_ref[...], w_ref[...])
        xt = x_ref[...]
        r = lax.rsqrt(jnp.mean(xt * xt, axis=-1, keepdims=True) + NORM_EPS)
        xn = xt * r
        dxn = dh * g_ref[...]
        gx_ref[...] = dx2_ref[...] + r * (dxn - xn * jnp.mean(dxn * xn, axis=-1, keepdims=True))
        st_ref[0:1, :] += jnp.sum(dh * xn, axis=0, keepdims=True)

        @pl.when(i == last)
        def _():
            for _, arrival in copies:
                arrival.wait_recv()
            for out, _ in copies:
                out.wait_send()

    n = 3 * len(CHIP_FLIPS)
    return pl.pallas_call(
        body, grid=(S // tm,), name="inproj_bwd_x",
        in_specs=piece_specs + [_resident((D, NCOL)), _rows(tm, D), _resident((1, D)), _rows(tm, D), ANY, ANY, ANY],
        out_specs=[_rows(tm, D), pl.BlockSpec((8, D), lambda i: (0, 0)), ANY, ANY, ANY],
        out_shape=[jax.ShapeDtypeStruct((S, D), F32), jax.ShapeDtypeStruct((8, D), F32),
                   jax.ShapeDtypeStruct((NCHIP, D // 2, CHUNK), BF16),
                   jax.ShapeDtypeStruct((NCHIP, WOUT_SHARD // 2, D), BF16),
                   jax.ShapeDtypeStruct((NCHIP, SMALL_ROWS, D), F32)],
        scratch_shapes=[pltpu.VMEM((tm, NCOL), BF16), pltpu.SemaphoreType.DMA((n,)), pltpu.SemaphoreType.DMA((n,))],
        compiler_params=_params(("arbitrary",)),
    )(*pieces, w_bf, x, g1, dx2, pi, po, ps)


def _inproj_bwd_w(h, dqs, dkvs, gates):
    out = None
    for k in range(NCHIP):
        units = range(k * UNITS_PER_CHUNK, (k + 1) * UNITS_PER_CHUNK)
        tk = 512 if units[0] < OFF_K // LANES else 1024
        nk = S // tk
        pieces, piece_specs, pick = _dproj_sources(units, dqs, dkvs, gates, tk)
        handed_on = [] if out is None else [out]

        def body(*refs, units=units, pick=pick, n_pieces=len(pieces), n_in=1 + len(pieces) + len(handed_on)):
            h_ref, piece_refs = refs[0], refs[1:1 + n_pieces]
            o_ref, tile, acc = refs[n_in:]
            i = pl.program_id(0)

            @pl.when(i == 0)
            def _():
                acc[...] = jnp.zeros_like(acc)

            sources = pick(piece_refs)
            for n, u in enumerate(units):
                tile[:, n * LANES:(n + 1) * LANES] = _dproj_unit(u, *sources, tk)
            acc[...] += _dot_tn(h_ref[...], tile[...])

            @pl.when(i == nk - 1)
            def _():
                o_ref[0] = acc[...].astype(BF16)

        out = pl.pallas_call(
            body, grid=(nk,), name=f"inproj_bwd_w{k}",
            in_specs=[_rows(tk, D)] + piece_specs + [ANY] * len(handed_on),
            out_specs=pl.BlockSpec((1, D, CHUNK), lambda i, k=k: (k, 0, 0)),
            out_shape=jax.ShapeDtypeStruct((NCHIP, D, CHUNK), BF16),
            input_output_aliases={1 + len(pieces): 0} if handed_on else {},
            scratch_shapes=[pltpu.VMEM((tk, CHUNK), BF16), pltpu.VMEM((D, CHUNK), F32)],
            compiler_params=_params(("arbitrary",)),
        )(h, *pieces, *handed_on)
    return out


def _local_step(x, target, g1, w_in_bf, conv_w, conv_b, ln_g, ln_b, w_out_bf, gf):
    h, q, k, v, a_gate, c_val, c_glu, c_gate = _inproj_fwd(x, g1, w_in_bf)
    tables = [_bias_table(d) for d in PATTERNS]
    o, lse, y_att = _attn_fwd(q, k, v, tables, a_gate)
    u, y_conv = _conv_fwd(c_val, c_glu, c_gate, conv_w, conv_b, ln_g, ln_b)
    dx2, dy_att, dy_conv, dw_out, st_out = _outproj_loss(y_att, y_conv, w_out_bf, x, target, gf)

    do, da_gate, delta = _attn_gate_bwd(dy_att, o, a_gate, _head_sum_selectors())
    dqs, dkvs = zip(*[_attn_bwd(q, k, v, do, lse, delta, t, d) for t, d in zip(tables, PATTERNS)])

    du, dc_gate, st_conv = _conv_bwd_rows(u, c_gate, dy_conv, ln_g, ln_b)
    dc_val, dc_glu, dconv_w = _conv_bwd_taps(du, c_val, c_glu, conv_w)

    dproj_pieces = (dqs, dkvs, (da_gate, dc_val, dc_glu, dc_gate))
    dw_in = _inproj_bwd_w(h, *dproj_pieces)
    small = jnp.concatenate([st_conv, st_out, dconv_w], axis=0)
    return dw_in, dw_out, small, dproj_pieces, dx2


ROW_LN_G, ROW_LN_B, ROW_CONV_B, ROW_FINAL_G, ROW_LOSS, ROW_TAPS = 0, 1, 2, 8, 9, 16
SMALL_ROWS = 16 + HALO
NDEV = 8


MESH = pl.DeviceIdType.MESH
ANY = pl.BlockSpec(memory_space=pl.ANY)
CHIP_FLIPS = ((1, 0), (0, 1), (1, 1))


def _pos():
    return lax.axis_index("x"), lax.axis_index("y"), lax.axis_index("c")


def _flip(v, f):
    return 1 - v if f else v


def _ds(start, size, align=None):
    return pl.ds(pl.multiple_of(start, align or size), size)


def _place_shards(wi, wo, cw, where):
    steps = 4

    def body(where_ref, wi_ref, wo_ref, cw_ref, wi_full, wo_full, cw_full):
        wi_full[...] = wi_ref[...].astype(BF16)
        wo_full[...] = wo_ref[...].astype(BF16)
        cw_full[...] = cw_ref[...]

    grid_spec = pltpu.PrefetchScalarGridSpec(
        num_scalar_prefetch=1, grid=(steps,),
        in_specs=[pl.BlockSpec((D // steps, CHUNK), lambda i, w: (i, 0)),
                  pl.BlockSpec((WOUT_SHARD // steps, D), lambda i, w: (i, 0)),
                  pl.BlockSpec((HALO, CONVW_SHARD), lambda i, w: (0, 0))],
        out_specs=[pl.BlockSpec((D // steps, CHUNK), lambda i, w: (i, w[0])),
                   pl.BlockSpec((WOUT_SHARD // steps, D), lambda i, w: (w[0] * steps + i, 0)),
                   pl.BlockSpec((HALO, CONVW_SHARD), lambda i, w: (0, w[0]))])
    return pl.pallas_call(
        body, grid_spec=grid_spec, name="place_shards",
        out_shape=[jax.ShapeDtypeStruct((D, NCOL), BF16), jax.ShapeDtypeStruct((WOUT_ROWS, D), BF16),
                   jax.ShapeDtypeStruct((HALO, D), F32)],
        compiler_params=_params(("arbitrary",)),
    )(where, wi, wo, cw)


def _gather_weights(wi_full, wo_full, cw_full):
    halves = (D // 2, WOUT_SHARD // 2, HALO // 2)
    OWN_X, OWN_Y, VIA_Y, VIA_X = range(4)

    def body(_wi, _wo, _cw, wi_full, wo_full, cw_full, send, recv):
        x, y, c = _pos()
        x_nbr, y_nbr, diag = (1 - x, y), (x, 1 - y), (1 - x, 1 - y)

        def region(a, chip_xy, half, part=None):
            chip = 2 * chip_xy[0] + chip_xy[1]
            n, row = halves[a], half * halves[a]
            if part is not None:
                n = n // 2
                row = row + part * n
            if a == 0:
                return wi_full.at[_ds(row, n), _ds(chip * CHUNK, CHUNK, 128)]
            if a == 1:
                return wo_full.at[_ds(chip * WOUT_SHARD + row, n), :]
            return cw_full.at[_ds(row, n), _ds(chip * CONVW_SHARD, CONVW_SHARD, 128)]

        def copy(a, kind, piece, dev):
            k = 8 * a + kind
            return pltpu.make_async_remote_copy(src_ref=piece, dst_ref=piece, send_sem=send.at[k], recv_sem=recv.at[k],
                                                device_id=dev, device_id_type=MESH)

        def to_sibling(a, kind, piece):
            cp = copy(a, 4 + kind, piece, (x, y, 1 - c))
            cp.start()
            return cp

        sends = []
        for a in range(3):
            for kind, nbr in ((OWN_X, x_nbr), (OWN_Y, y_nbr)):
                cp = copy(a, kind, region(a, (x, y), c), (*nbr, c))
                cp.start()
                sends.append(cp)
        for a in range(3):
            got = region(a, x_nbr, c)
            copy(a, OWN_X, got, (*x_nbr, c)).wait_recv()
            onward = copy(a, VIA_Y, region(a, x_nbr, c, 0), (*y_nbr, c))
            onward.start()
            sends += [onward, to_sibling(a, OWN_X, got)]
            got = region(a, y_nbr, c)
            copy(a, OWN_Y, got, (*y_nbr, c)).wait_recv()
            onward = copy(a, VIA_X, region(a, y_nbr, c, 1), (*x_nbr, c))
            onward.start()
            sends += [onward, to_sibling(a, OWN_Y, got)]
        for a in range(3):
            got = region(a, diag, c, 0)
            copy(a, VIA_Y, got, (*y_nbr, c)).wait_recv()
            sends.append(to_sibling(a, VIA_Y, got))
            got = region(a, diag, c, 1)
            copy(a, VIA_X, got, (*x_nbr, c)).wait_recv()
            sends.append(to_sibling(a, VIA_X, got))
        for a in range(3):
            for kind, piece in ((OWN_X, region(a, x_nbr, 1 - c)), (OWN_Y, region(a, y_nbr, 1 - c)),
                                (VIA_Y, region(a, diag, 1 - c, 0)), (VIA_X, region(a, diag, 1 - c, 1))):
                copy(a, 4 + kind, piece, (x, y, 1 - c)).wait_recv()
        for cp in sends:
            cp.wait_send()

    n_sems = 3 * 8
    return pl.pallas_call(
        body, name="gather_weights",
        in_specs=[ANY, ANY, ANY], out_specs=[ANY, ANY, ANY], input_output_aliases={0: 0, 1: 1, 2: 2},
        out_shape=[jax.ShapeDtypeStruct((D, NCOL), BF16), jax.ShapeDtypeStruct((WOUT_ROWS, D), BF16),
                   jax.ShapeDtypeStruct((HALO, D), F32)],
        scratch_shapes=[pltpu.SemaphoreType.DMA((n_sems,)), pltpu.SemaphoreType.DMA((n_sems,))],
    )(wi_full, wo_full, cw_full)


def _exchange_halves(gi4, go4, small):
    def body(gi_ref, go_ref, sm_ref, ri_ref, ro_ref, rs_ref, send, recv):
        x, y, c = _pos()
        sib = (x, y, 1 - c)
        copies = [
            (gi_ref.at[:, _ds((1 - c) * (D // 2), D // 2), :], ri_ref),
            (go_ref.at[:, _ds((1 - c) * (WOUT_SHARD // 2), WOUT_SHARD // 2), :], ro_ref),
            (sm_ref, rs_ref),
        ]
        cps = [pltpu.make_async_remote_copy(src_ref=s_, dst_ref=d_, send_sem=send.at[k], recv_sem=recv.at[k],
                                            device_id=sib, device_id_type=MESH) for k, (s_, d_) in enumerate(copies)]
        for cp in cps:
            cp.start()
        for cp in cps:
            cp.wait()

    return pl.pallas_call(
        body, name="exchange_halves",
        in_specs=[ANY, ANY, ANY], out_specs=[ANY, ANY, ANY],
        out_shape=[jax.ShapeDtypeStruct((NCHIP, D // 2, CHUNK), BF16),
                   jax.ShapeDtypeStruct((NCHIP, WOUT_SHARD // 2, D), BF16),
                   jax.ShapeDtypeStruct((SMALL_ROWS, D), F32)],
        scratch_shapes=[pltpu.SemaphoreType.DMA((3,)), pltpu.SemaphoreType.DMA((3,))],
    )(gi4, go4, small)


def _add_halves(gi4, ri, go4, ro, small, rs):
    hi, ho = D // 2, WOUT_SHARD // 2

    def body(gi_ref, ri_ref, go_ref, ro_ref, sm_ref, rs_ref, pi_ref, po_ref, ps_ref):
        c = lax.axis_index("c")
        pi_ref[0] = (gi_ref[0, _ds(c * hi, hi), :].astype(F32) + ri_ref[0].astype(F32)).astype(BF16)
        po_ref[0] = (go_ref[0, _ds(c * ho, ho), :].astype(F32) + ro_ref[0].astype(F32)).astype(BF16)
        ps_ref[...] = sm_ref[...] + rs_ref[...]

    blk = lambda n, w: pl.BlockSpec((1, n, w), lambda k: (k, 0, 0))
    whole = pl.BlockSpec((SMALL_ROWS, D), lambda k: (0, 0))
    return pl.pallas_call(
        body, grid=(NCHIP,), name="add_halves",
        in_specs=[blk(D, CHUNK), blk(hi, CHUNK), blk(WOUT_SHARD, D), blk(ho, D), whole, whole],
        out_specs=[blk(hi, CHUNK), blk(ho, D), whole],
        out_shape=[jax.ShapeDtypeStruct((NCHIP, hi, CHUNK), BF16), jax.ShapeDtypeStruct((NCHIP, ho, D), BF16),
                   jax.ShapeDtypeStruct((SMALL_ROWS, D), F32)],
        compiler_params=_params(("arbitrary",)),
    )(gi4, ri, go4, ro, small, rs)


def _chip_exchange_copies(srcs, dsts, send, recv):
    x, y, c = _pos()
    me = 2 * x + y
    pairs = []
    for a in range(3):
        for j, (fx, fy) in enumerate(CHIP_FLIPS):
            px, py = _flip(x, fx), _flip(y, fy)
            peer = 2 * px + py
            k = 3 * a + j
            out = pltpu.make_async_remote_copy(
                src_ref=srcs[a] if a == 2 else srcs[a].at[peer], dst_ref=dsts[a].at[me],
                send_sem=send.at[k], recv_sem=recv.at[k], device_id=(px, py, c), device_id_type=MESH)
            got = dsts[a].at[peer]
            arrival = pltpu.make_async_remote_copy(
                src_ref=got, dst_ref=got, send_sem=send.at[k], recv_sem=recv.at[k],
                device_id=(px, py, c), device_id_type=MESH)
            pairs.append((out, arrival))
    return pairs


def _sum_chips(ri, ro, rs, pi, po, ps, where):
    def body(w_ref, ri_ref, ro_ref, rs_ref, pi_ref, po_ref, ps_ref, gi_ref, go_ref, gs_ref, g5_ref, loss_ref,
             acc_i, acc_o, acc_s):
        k = pl.program_id(0)
        accs = (acc_i, acc_o, acc_s)

        @pl.when(k == 0)
        def _():
            for acc in accs:
                acc[...] = jnp.zeros_like(acc)

        @pl.when(k == w_ref[0])
        def _():
            for acc, val in zip(accs, (pi_ref[0], po_ref[0], ps_ref[...])):
                acc[...] += val.astype(F32)

        @pl.when(k != w_ref[0])
        def _():
            for acc, ref in zip(accs, (ri_ref, ro_ref, rs_ref)):
                acc[...] += ref[0].astype(F32)

        @pl.when(k == NCHIP - 1)
        def _():
            gi_ref[0] = acc_i[...]
            go_ref[0] = acc_o[...]
            gs_ref[...] = acc_s[...]
            g5_ref[...] = jnp.zeros_like(g5_ref)
            for i, row in enumerate((ROW_CONV_B, ROW_LN_G, ROW_LN_B, ROW_FINAL_G)):
                g5_ref[i + 1:i + 2, :] = acc_s[row:row + 1, :]
            loss = jnp.sum(acc_s[ROW_LOSS:ROW_LOSS + 1, :], axis=1, keepdims=True)
            loss_ref[...] = jnp.broadcast_to(loss, loss_ref.shape)

    def sent(k, w):
        return jnp.where(k == w[0], (k + 1) % NCHIP, k)

    hi, ho = D // 2, WOUT_SHARD // 2
    const = lambda shape: pl.BlockSpec(shape, lambda k, w: (0,) * len(shape))
    grid_spec = pltpu.PrefetchScalarGridSpec(
        num_scalar_prefetch=1, grid=(NCHIP,),
        in_specs=[pl.BlockSpec((1, hi, CHUNK), lambda k, w: (sent(k, w), 0, 0)),
                  pl.BlockSpec((1, ho, D), lambda k, w: (sent(k, w), 0, 0)),
                  pl.BlockSpec((1, SMALL_ROWS, D), lambda k, w: (sent(k, w), 0, 0)),
                  pl.BlockSpec((1, hi, CHUNK), lambda k, w: (w[0], 0, 0)),
                  pl.BlockSpec((1, ho, D), lambda k, w: (w[0], 0, 0)),
                  const((SMALL_ROWS, D))],
        out_specs=[pl.BlockSpec((1, hi, CHUNK), lambda k, w: (w[1], 0, 0)),
                   pl.BlockSpec((1, ho, D), lambda k, w: (w[1], 0, 0)),
                   const((SMALL_ROWS, D)), const((8, D)), const((8, LANES))],
        scratch_shapes=[pltpu.VMEM((hi, CHUNK), F32), pltpu.VMEM((ho, D), F32), pltpu.VMEM((SMALL_ROWS, D), F32)])
    return pl.pallas_call(
        body, grid_spec=grid_spec, name="sum_chips",
        out_shape=[jax.ShapeDtypeStruct((2, hi, CHUNK), F32), jax.ShapeDtypeStruct((2, ho, D), F32),
                   jax.ShapeDtypeStruct((SMALL_ROWS, D), F32), jax.ShapeDtypeStruct((8, D), F32),
                   jax.ShapeDtypeStruct((8, LANES), F32)],
        compiler_params=_params(("arbitrary",)),
    )(where, ri, ro, rs, pi, po, ps)


def _exchange_results(gi2, go2, st):
    flips = [(fx, fy, fc) for fx in (0, 1) for fy in (0, 1) for fc in (0, 1)][1:]

    def body(_gi, _go, st_ref, gi_ref, go_ref, all_ref, send, recv, lsem):
        x, y, c = _pos()
        sib = (x, y, 1 - c)

        def half(k, ref, slot):
            return pltpu.make_async_remote_copy(src_ref=ref.at[slot], dst_ref=ref.at[slot], send_sem=send.at[k],
                                                recv_sem=recv.at[k], device_id=sib, device_id_type=MESH)

        def stat(k, src, slot, dev):
            return pltpu.make_async_remote_copy(src_ref=src, dst_ref=all_ref.at[slot], send_sem=send.at[k],
                                                recv_sem=recv.at[k], device_id=dev, device_id_type=MESH)

        mine = pltpu.make_async_copy(st_ref, all_ref.at[4 * x + 2 * y + c], lsem)
        mine.start()
        sends = [half(k, ref, c) for k, ref in enumerate((gi_ref, go_ref))]
        peers = [(_flip(x, fx), _flip(y, fy), _flip(c, fc)) for fx, fy, fc in flips]
        sends += [stat(2 + k, st_ref, 4 * x + 2 * y + c, dev) for k, dev in enumerate(peers)]
        for cp in sends:
            cp.start()
        for k, ref in enumerate((gi_ref, go_ref)):
            half(k, ref, 1 - c).wait_recv()
        for k, (px, py, pc) in enumerate(peers):
            slot = 4 * px + 2 * py + pc
            stat(2 + k, all_ref.at[slot], slot, (px, py, pc)).wait_recv()
        for cp in sends:
            cp.wait_send()
        mine.wait()

    n = 2 + len(flips)
    return pl.pallas_call(
        body, name="exchange_results",
        in_specs=[ANY, ANY, ANY], out_specs=[ANY, ANY, ANY], input_output_aliases={0: 0, 1: 1},
        out_shape=[jax.ShapeDtypeStruct((2, D // 2, CHUNK), F32), jax.ShapeDtypeStruct((2, WOUT_SHARD // 2, D), F32),
                   jax.ShapeDtypeStruct((NDEV, 8, D), F32)],
        scratch_shapes=[pltpu.SemaphoreType.DMA((n,)), pltpu.SemaphoreType.DMA((n,)), pltpu.SemaphoreType.DMA],
    )(gi2, go2, st)


def _adamw_math(w, g, m, v):
    m2 = ADAM_B1 * m + (1.0 - ADAM_B1) * g
    v2 = ADAM_B2 * v + (1.0 - ADAM_B2) * (g * g)
    m_hat = m2 / (1.0 - ADAM_B1 ** ADAM_STEP)
    v_hat = v2 / (1.0 - ADAM_B2 ** ADAM_STEP)
    delta = -ADAM_LR * (m_hat / (jnp.sqrt(v_hat) + ADAM_EPS) + ADAM_WD * w)
    return delta, m2, v2


def _adamw(w, g, m, v, name):
    rows, cols = w.shape
    tm = 256 if rows % 256 == 0 else rows

    def body(w_ref, g_ref, m_ref, v_ref, d_ref, m2_ref, v2_ref):
        d_ref[...], m2_ref[...], v2_ref[...] = _adamw_math(w_ref[...], g_ref[...], m_ref[...], v_ref[...])

    shape = jax.ShapeDtypeStruct(w.shape, F32)
    return pl.pallas_call(
        body, grid=(rows // tm,), name=name,
        in_specs=[_rows(tm, cols)] * 4, out_specs=[_rows(tm, cols)] * 3, out_shape=[shape] * 3,
        compiler_params=_params(("arbitrary",)),
    )(w, g, m, v)


def _adamw_vectors(g5, first_parts, ws, ms, vs):
    n = len(ws)

    def body(g_ref, parts_ref, *refs):
        ins, g0_ref, outs = refs[:3 * n], refs[3 * n], refs[3 * n + 1:]
        g0 = parts_ref[0, 0:1, :]
        for dev in range(1, NDEV):
            g0 = g0 + parts_ref[dev, 0:1, :]
        g0_ref[...] = g0
        for i in range(n):
            g = g0 if i == 0 else g_ref[i:i + 1, :]
            res = _adamw_math(ins[i][...], g, ins[n + i][...], ins[2 * n + i][...])
            for kind in range(3):
                outs[kind * n + i][...] = res[kind]

    shape = jax.ShapeDtypeStruct((1, D), F32)
    return pl.pallas_call(body, name="adamw_vectors", out_shape=[shape] * (1 + 3 * n), compiler_params=_params())(
        g5, first_parts, *ws, *ms, *vs)


def kernel(x, norm_g, w_in, conv_w, conv_b, conv_ln_g, conv_ln_b, w_out, final_norm_g, loss_target, m_norm_g, m_w_in, m_conv_w, m_conv_b, m_conv_ln_g, m_conv_ln_b, m_w_out, m_final_norm_g, v_norm_g, v_w_in, v_conv_w, v_conv_b, v_conv_ln_g, v_conv_ln_b, v_w_out, v_final_norm_g):
    chip = 2 * lax.axis_index("x") + lax.axis_index("y")
    where = jnp.stack([chip, lax.axis_index("c")]).astype(jnp.int32)
    taps_shard = jnp.pad(conv_w[0], ((0, HALO - CONV_K), (0, 0)))
    wi_full, wo_full, cw_full = _gather_weights(*_place_shards(w_in[0], w_out[0], taps_shard, where))

    gf = final_norm_g[None]
    dw_in4, dw_out, small, dproj_pieces, dx2 = _local_step(
        x[0], loss_target[0], norm_g, wi_full, cw_full, conv_b, conv_ln_g, conv_ln_b, wo_full, gf)
    dw_out4 = dw_out.reshape(NCHIP, WOUT_SHARD, D)

    ri, ro, rs = _exchange_halves(dw_in4, dw_out4, small)
    pi, po, ps = _add_halves(dw_in4, ri, dw_out4, ro, small, rs)
    grad_x, st_in, ri, ro, rs = _inproj_bwd_x(*dproj_pieces, wi_full, x[0], norm_g, dx2, pi, po, ps)
    gi2, go2, g_small, g5, loss8 = _sum_chips(ri, ro, rs, pi, po, ps, where)
    gi2, go2, norm_g_parts = _exchange_results(gi2, go2, st_in)
    g_w_in = gi2.reshape(D, CHUNK)
    g_w_out = go2.reshape(WOUT_SHARD, D)
    g_taps = lax.dynamic_slice(g_small, (ROW_TAPS, chip * CONVW_SHARD), (CONV_K, CONVW_SHARD))

    d_w_in, m2_w_in, v2_w_in = _adamw(w_in[0], g_w_in, m_w_in[0], v_w_in[0], "adamw_w_in")
    d_w_out, m2_w_out, v2_w_out = _adamw(w_out[0], g_w_out, m_w_out[0], v_w_out[0], "adamw_w_out")
    d_taps, m2_taps, v2_taps = _adamw(conv_w[0], g_taps, m_conv_w[0], v_conv_w[0], "adamw_conv_w")
    g_norm, *vec = _adamw_vectors(
        g5, norm_g_parts,
        (norm_g, conv_b, conv_ln_g, conv_ln_b, gf),
        (m_norm_g, m_conv_b, m_conv_ln_g, m_conv_ln_b, m_final_norm_g[None]),
        (v_norm_g, v_conv_b, v_conv_ln_g, v_conv_ln_b, v_final_norm_g[None]))
    d_vec, m2_vec, v2_vec = vec[0:5], vec[5:10], vec[10:15]

    def weight_order(ng, wi, cw, cb, lg, lb, wo, fg):
        return (ng, wi[None], cw[None], cb, lg, lb, wo[None], fg[0])

    grads = weight_order(g_norm, g_w_in, g_taps, g5[1:2], g5[2:3], g5[3:4], g_w_out, g5[4:5])
    deltas = weight_order(d_vec[0], d_w_in, d_taps, d_vec[1], d_vec[2], d_vec[3], d_w_out, d_vec[4])
    new_m = weight_order(m2_vec[0], m2_w_in, m2_taps, m2_vec[1], m2_vec[2], m2_vec[3], m2_w_out, m2_vec[4])
    new_v = weight_order(v2_vec[0], v2_w_in, v2_taps, v2_vec[1], v2_vec[2], v2_vec[3], v2_w_out, v2_vec[4])
    return (loss8[0, 0], grad_x[None], *grads, *deltas, *new_m, *new_v)
```

```python
import jax
import jax.numpy as jnp
from jax import lax
from jax.experimental import pallas as pl
from jax.experimental.pallas import tpu as pltpu

F32 = jnp.float32
BF16 = jnp.bfloat16

S = 4096
D = 1024
LANES = 128
HD = 64
NKV = 4
GQ = 4
KVW = NKV * HD
NCOL = 5632
CONV_K = 31
HALO = 32
BLK = 128
PATTERNS = (1, 4, 16)
NORM_EPS = 1e-6
LN_EPS = 1e-5
NEG = -1e30
OFF_Q, OFF_K, OFF_V, OFF_AG, OFF_CV, OFF_CG, OFF_CGATE = 0, 1024, 1280, 1536, 2560, 3584, 4608
NCHIP = 4
CHUNK = NCOL // NCHIP
WOUT_ROWS = 2 * D
WOUT_SHARD = WOUT_ROWS // NCHIP
CONVW_SHARD = D // NCHIP

ADAM_LR, ADAM_B1, ADAM_B2, ADAM_EPS, ADAM_WD, ADAM_STEP = 0.001, 0.9, 0.999, 1e-08, 0.01, 10

VMEM_LIMIT = 56 * 1024 * 1024


def _params(sem=None, vmem=VMEM_LIMIT):
    return pltpu.CompilerParams(dimension_semantics=sem, vmem_limit_bytes=vmem)


def _sigmoid(a):
    return 0.5 * jnp.tanh(0.5 * a) + 0.5


def _rows(tm, width):
    return pl.BlockSpec((tm, width), lambda i: (i, 0))


def _slabs(n):
    return jax.ShapeDtypeStruct((n, S, LANES), F32)


def _slab_rows(n, tm):
    return pl.BlockSpec((n, tm, LANES), lambda i: (0, i, 0))


def _resident(shape):
    return pl.BlockSpec(shape, lambda *_: (0,) * len(shape), pipeline_mode=pl.Buffered(1))


def _dot(a, b):
    return jnp.dot(a, b, preferred_element_type=F32)


def _dot_nt(a, b):
    return lax.dot_general(a, b, (((1,), (1,)), ((), ())), preferred_element_type=F32)


def _dot_tn(a, b):
    return lax.dot_general(a, b, (((0,), (0,)), ((), ())), preferred_element_type=F32)


def _inproj_fwd(x, g1, w_bf):
    tm = 512

    def body(x_ref, g_ref, w_ref, h_ref, q_ref, k_ref, v_ref, ag_ref, cv_ref, cg_ref, cgate_ref):
        xt = x_ref[...]
        r = lax.rsqrt(jnp.mean(xt * xt, axis=-1, keepdims=True) + NORM_EPS)
        h = (xt * r * g_ref[...]).astype(BF16)
        h_ref[...] = h
        q = _dot(h, w_ref[:, OFF_Q:OFF_Q + D]) * (HD ** -0.5)
        kv = _dot(h, w_ref[:, OFF_K:OFF_K + 2 * KVW])
        for sl in range(D // LANES):
            q_ref[sl] = q[:, sl * LANES:(sl + 1) * LANES]
        for sl in range(KVW // LANES):
            k_ref[sl] = kv[:, sl * LANES:(sl + 1) * LANES]
            v_ref[sl] = kv[:, KVW + sl * LANES:KVW + (sl + 1) * LANES]
        ag_ref[...] = _dot(h, w_ref[:, OFF_AG:OFF_AG + D])
        cv_ref[...] = _dot(h, w_ref[:, OFF_CV:OFF_CV + D])
        cg_ref[...] = _dot(h, w_ref[:, OFF_CG:OFF_CG + D])
        cgate_ref[...] = _dot(h, w_ref[:, OFF_CGATE:OFF_CGATE + D])

    big = jax.ShapeDtypeStruct((S, D), F32)
    return pl.pallas_call(
        body, grid=(S // tm,), name="inproj_fwd",
        in_specs=[_rows(tm, D), _resident((1, D)), _resident((D, NCOL))],
        out_specs=[_rows(tm, D), _slab_rows(D // LANES, tm), _slab_rows(KVW // LANES, tm), _slab_rows(KVW // LANES, tm),
                   _rows(tm, D), _rows(tm, D), _rows(tm, D), _rows(tm, D)],
        out_shape=[jax.ShapeDtypeStruct((S, D), BF16), _slabs(D // LANES), _slabs(KVW // LANES), _slabs(KVW // LANES),
                   big, big, big, big],
        compiler_params=_params(("arbitrary",)),
    )(x, g1, w_bf)


def _bias_table(d):
    h = jnp.arange(NKV * GQ, dtype=F32)
    slopes = jnp.exp2(-8.0 * (h + 1.0) / (NKV * GQ))
    qi = jnp.arange(BLK)[:, None]
    kj = jnp.arange(2 * BLK)[None, :]
    dist = BLK + qi - kj
    window = (dist >= 0) & (dist <= BLK)
    bias = -slopes[:, None, None] * (dist * d).astype(F32)[None]
    has_prev = jnp.stack([jnp.broadcast_to(kj >= BLK, (BLK, 2 * BLK)), jnp.ones((BLK, 2 * BLK), bool)])
    valid = window[None] & has_prev
    tab = jnp.where(valid[:, None], bias[None], NEG)
    return tab.reshape(2, NKV, GQ * BLK, 2 * BLK)


def _sub_rows(start, d):
    if d == 1:
        return pl.ds(pl.multiple_of(start, BLK), BLK)
    return pl.ds(start, BLK, stride=d)


NHEAD = NKV * GQ
CHUNK_ROWS = 2048
BLOCKS_PER_CHUNK = CHUNK_ROWS // BLK


def _low_lanes(rows=BLK):
    return lax.broadcasted_iota(jnp.int32, (rows, LANES), 1) < HD


def _block_start(idx, d):
    shift = d.bit_length() - 1
    b, r = lax.shift_right_logical(idx, shift), lax.bitwise_and(idx, d - 1)
    start = b * (BLK * d) + r
    return b, start, jnp.maximum(start - BLK * d, r)


def _stack_heads(ref, rows):
    low = _low_lanes()
    t0, t1 = ref[0, rows, :], ref[1, rows, :]
    return jnp.concatenate([jnp.where(low, t0, 0.0), jnp.where(low, 0.0, t0),
                            jnp.where(low, t1, 0.0), jnp.where(low, 0.0, t1)], axis=0).astype(BF16)


def _unstack_heads(dup):
    low = _low_lanes()
    return (jnp.where(low, dup[0:BLK], dup[BLK:2 * BLK]), jnp.where(low, dup[2 * BLK:3 * BLK], dup[3 * BLK:4 * BLK]))


def _kv_dup(ref, prow, rows, odd):
    t = jnp.concatenate([ref[0, prow, :], ref[0, rows, :]], axis=0)
    swapped = pltpu.roll(t, HD, axis=1)
    keep = jnp.logical_xor(_low_lanes(2 * BLK), odd)
    return jnp.where(keep, t, swapped).astype(BF16)


PIECES = 3


def _by_head(tiles):
    lane = lax.broadcasted_iota(jnp.int32, tiles[0].shape, 1)
    out = tiles[0]
    for g in range(1, GQ):
        out = jnp.where(lax.bitwise_and(lane, GQ - 1) == g, tiles[g], out)
    return out


def _minus_in_pieces(x):
    lane = lax.broadcasted_iota(jnp.int32, x.shape, 1)
    hi = (-x).astype(BF16).astype(F32)
    rest = -x - hi
    mid = rest.astype(BF16).astype(F32)
    lo = (rest - mid).astype(BF16).astype(F32)
    return jnp.where(lane < GQ, hi, jnp.where(lane < 2 * GQ, mid, jnp.where(lane < PIECES * GQ, lo, 0.0)))


def _attn_fwd(q, k, v, tables, a_gate):
    tm = 256
    width = GQ * HD

    lane_out = jnp.arange(LANES)[None, :] // HD
    spread_sel = jnp.stack([jnp.arange(LANES)[:, None] == 2 * half + lane_out for half in range(2)]).astype(BF16)

    def body(q_ref, k_ref, v_ref, b1_ref, b2_ref, b3_ref, ag_ref, sel_ref, o_ref, lse_ref, y_ref, op, lp):
        odd = pl.program_id(0) % 2 == 1
        chunk = pl.program_id(1)
        ones = jnp.ones((2 * BLK, LANES), BF16)

        for pat, (d, b_ref) in enumerate(zip(PATTERNS, (b1_ref, b2_ref, b3_ref))):
            def block(idx, carry, pat=pat, d=d, b_ref=b_ref):
                b, start, pstart = _block_start(chunk * BLOCKS_PER_CHUNK + idx, d)
                rows, prow = _sub_rows(start, d), _sub_rows(pstart, d)
                mine = _sub_rows(start - chunk * CHUNK_ROWS, d)
                qs = _stack_heads(q_ref, mine)
                kw = _kv_dup(k_ref, prow, rows, odd)
                vw = _kv_dup(v_ref, prow, rows, odd)
                s = _dot_nt(qs, kw) + b_ref[jnp.minimum(b, 1), 0]
                m = jnp.max(s, axis=1, keepdims=True)
                p = jnp.exp(s - m).astype(BF16)
                ol = _dot(p, jnp.concatenate([vw, ones], axis=1))
                l = ol[:, LANES:]
                op[pat, 0, mine, :], op[pat, 1, mine, :] = _unstack_heads(ol[:, :LANES] / l)
                lp[pat, mine, :] = _by_head([(m + jnp.log(l))[g * BLK:(g + 1) * BLK] for g in range(GQ)])
                return carry

            lax.fori_loop(0, BLOCKS_PER_CHUNK, block, 0, unroll=2)

        def mix(t, carry):
            r = pl.ds(pl.multiple_of(t * tm, tm), tm)
            low = _low_lanes(tm)
            a, b, c = lp[0, r, :], lp[1, r, :], lp[2, r, :]
            m = jnp.maximum(jnp.maximum(a, b), c)
            ea, eb, ec = jnp.exp(a - m), jnp.exp(b - m), jnp.exp(c - m)
            den = ea + eb + ec
            lse_ref[0, r, :] = _minus_in_pieces(m + jnp.log(den))
            inv = 1.0 / den
            for half in range(2):
                def spread(w):
                    hi = w.astype(BF16)
                    lo = (w - hi.astype(F32)).astype(BF16)
                    return _dot(hi, sel_ref[half]) + _dot(lo, sel_ref[half])

                o = (spread(ea * inv) * op[0, half, r, :] + spread(eb * inv) * op[1, half, r, :]
                     + spread(ec * inv) * op[2, half, r, :])
                o_ref[half, r, :] = o
                cols = slice(half * LANES, (half + 1) * LANES)
                ag = ag_ref[r, cols]
                y_ref[r, cols] = (o * (ag * _sigmoid(ag))).astype(BF16)
            return carry

        lax.fori_loop(0, CHUNK_ROWS // tm, mix, 0)

    q_like = pl.BlockSpec((2, CHUNK_ROWS, LANES), lambda j, c: (j, c, 0))
    per_kv = pl.BlockSpec((1, CHUNK_ROWS, LANES), lambda j, c: (j, c, 0))
    kv = pl.BlockSpec((1, S, LANES), lambda j, c: (j // 2, 0, 0))
    bias_spec = pl.BlockSpec((2, 1, GQ * BLK, 2 * BLK), lambda j, c: (0, j, 0, 0))
    group_cols = pl.BlockSpec((CHUNK_ROWS, width), lambda j, c: (c, j))
    return pl.pallas_call(
        body, grid=(NKV, S // CHUNK_ROWS), name="attn_fwd",
        in_specs=[q_like, kv, kv, bias_spec, bias_spec, bias_spec, group_cols,
                  pl.BlockSpec((2, LANES, LANES), lambda j, c: (0, 0, 0))],
        out_specs=[q_like, per_kv, group_cols],
        out_shape=[_slabs(D // LANES), _slabs(NKV), jax.ShapeDtypeStruct((S, D), BF16)],
        scratch_shapes=[pltpu.VMEM((len(PATTERNS), 2, CHUNK_ROWS, LANES), F32),
                        pltpu.VMEM((len(PATTERNS), CHUNK_ROWS, LANES), F32)],
        compiler_params=_params(("arbitrary", "arbitrary")),
    )(q, k, v, *tables, a_gate, spread_sel)


def _head_sum_selectors():
    lane_in = jnp.arange(LANES)[:, None] // HD
    return jnp.stack([jnp.broadcast_to(lane_in == h, (LANES, LANES)) for h in range(2)]).astype(BF16)


def _attn_gate_bwd(dy_att, o, a_gate, selectors, chip_sums):
    tm = 256
    last = S // tm - 1
    landing, sems = _exchange_results_of(chip_sums)
    n_sums = len(chip_sums)

    def body(dy_ref, o_ref, ag_ref, e_ref, *refs):
        sums, (do_ref, dag_ref, delta_ref), refs = refs[:n_sums], refs[n_sums:n_sums + 3], refs[n_sums + 3:]
        landed, (send, recv) = refs[:n_sums], refs[n_sums:]
        i = pl.program_id(0)
        copies = _chip_exchange_copies(sums, landed, send, recv)
        _start_exchange(copies, i == 0)
        for j in range(NKV):
            deltas = []
            for sl in (2 * j, 2 * j + 1):
                cols = slice(sl * LANES, (sl + 1) * LANES)
                dy, ag, o_ = dy_ref[:, cols], ag_ref[:, cols], o_ref[sl]
                sg = _sigmoid(ag)
                do = dy * (ag * sg)
                do_ref[sl] = do
                dag_ref[:, cols] = (dy * o_ * (sg * (1.0 + ag * (1.0 - sg)))).astype(BF16)
                prod = do * o_
                hi = prod.astype(BF16)
                lo = (prod - hi.astype(F32)).astype(BF16)
                deltas += [_dot(hi, e_ref[h]) + _dot(lo, e_ref[h]) for h in range(2)]
            delta_ref[j] = _minus_in_pieces(_by_head(deltas))
        _finish_exchange(copies, i == last)

    return pl.pallas_call(
        body, grid=(S // tm,), name="attn_gate_bwd",
        in_specs=[_rows(tm, D), _slab_rows(D // LANES, tm), _rows(tm, D), _resident((2, LANES, LANES))] + [ANY] * n_sums,
        out_specs=[_slab_rows(D // LANES, tm), _rows(tm, D), _slab_rows(NKV, tm)] + [ANY] * n_sums,
        out_shape=[_slabs(D // LANES), jax.ShapeDtypeStruct((S, D), BF16), _slabs(NKV)] + landing,
        scratch_shapes=sems,
        compiler_params=_params(("arbitrary",)),
    )(dy_att, o, a_gate, selectors, *chip_sums)


def _own_pieces(tile):
    lane = lax.broadcasted_iota(jnp.int32, tile.shape, 1)
    head = jnp.where(lane < PIECES * GQ, lax.bitwise_and(lane, GQ - 1), -1)
    return jnp.concatenate([jnp.where(head == g, tile, 0.0) for g in range(GQ)], axis=0).astype(BF16)


def _attn_bwd(q, k, v, do, lse, delta, bias, d):
    def body(q_ref, do_ref, l_ref, dl_ref, k_ref, v_ref, b_ref, dq_ref, dkv_ref, acc):
        odd = pl.program_id(0) % 2 == 1
        chunk = pl.program_id(1)
        ones = (lax.broadcasted_iota(jnp.int32, (2 * BLK, LANES), 1) < PIECES * GQ).astype(BF16)

        def in_acc(block_idx):
            return pl.ds(pl.multiple_of(block_idx * BLK, BLK), BLK)

        @pl.when(chunk == 0)
        def _():
            acc[...] = jnp.zeros_like(acc)

        def block(idx, carry):
            idx = chunk * BLOCKS_PER_CHUNK + idx
            b, start, pstart = _block_start(idx, d)
            rows, prow = _sub_rows(start, d), _sub_rows(pstart, d)
            mine = _sub_rows(start - chunk * CHUNK_ROWS, d)
            qs = _stack_heads(q_ref, mine)
            dos = _stack_heads(do_ref, mine)
            kw = _kv_dup(k_ref, prow, rows, odd)
            vw = _kv_dup(v_ref, prow, rows, odd)
            s = _dot_nt(jnp.concatenate([qs, _own_pieces(l_ref[0, mine, :])], axis=1),
                        jnp.concatenate([kw, ones], axis=1)) + b_ref[jnp.minimum(b, 1), 0]
            p = jnp.exp(s)
            dv2 = _dot_tn(p.astype(BF16), dos)
            dp = _dot_nt(jnp.concatenate([dos, _own_pieces(dl_ref[0, mine, :])], axis=1),
                         jnp.concatenate([vw, ones], axis=1))
            ds = (p * dp).astype(BF16)
            dq_ref[0, mine, :], dq_ref[1, mine, :] = _unstack_heads(_dot(ds, kw))
            dk2 = _dot_tn(ds, qs)
            dkv = jnp.where(_low_lanes(2 * BLK), dk2 + pltpu.roll(dk2, HD, axis=1), dv2 + pltpu.roll(dv2, HD, axis=1))
            acc[in_acc(idx), :] = acc[in_acc(idx), :] + dkv[BLK:]
            before = jnp.where(b >= 1, idx - d, idx)
            acc[in_acc(before), :] = acc[in_acc(before), :] + dkv[:BLK]
            return carry

        lax.fori_loop(0, BLOCKS_PER_CHUNK, block, 0, unroll=8)

        @pl.when(chunk == S // CHUNK_ROWS - 1)
        def _():
            def place(idx, carry):
                _, start, _ = _block_start(idx, d)
                dkv_ref[0, _sub_rows(start, d), :] = acc[in_acc(idx), :]
                return carry

            lax.fori_loop(0, S // BLK, place, 0, unroll=4)

    q_like = pl.BlockSpec((2, CHUNK_ROWS, LANES), lambda j, c: (j, c, 0))
    pieces = pl.BlockSpec((1, CHUNK_ROWS, LANES), lambda j, c: (j, c, 0))
    kv = pl.BlockSpec((1, S, LANES), lambda j, c: (j // 2, 0, 0))
    per_kv = pl.BlockSpec((1, S, LANES), lambda j, c: (j, 0, 0))
    bias_spec = pl.BlockSpec((2, 1, GQ * BLK, 2 * BLK), lambda j, c: (0, j, 0, 0))
    return pl.pallas_call(
        body, grid=(NKV, S // CHUNK_ROWS), name=f"attn_bwd_d{d}",
        in_specs=[q_like, q_like, pieces, pieces, kv, kv, bias_spec],
        out_specs=[q_like, per_kv],
        out_shape=[_slabs(D // LANES), _slabs(NKV)],
        scratch_shapes=[pltpu.VMEM((S, LANES), F32)],
        compiler_params=_params(("arbitrary", "arbitrary")),
    )(q, do, lse, delta, k, v, bias)


CONV_T = 256


def _halo_before(i):
    return (jnp.maximum(i * (CONV_T // HALO) - 1, 0), 0)


def _halo_after(i):
    return (jnp.minimum((i + 1) * (CONV_T // HALO), S // HALO - 1), 0)


SUBLANES = 8
NCH = D // LANES
GROUP = SUBLANES * SUBLANES


def _comb(ref, cb, base):
    return ref[cb, pl.ds(base, SUBLANES, stride=SUBLANES), :]


def _taps(w_ref, cols):
    return [jnp.broadcast_to(w_ref[j:j + 1, cols], (SUBLANES, LANES)) for j in range(CONV_K)]


def _conv_fwd(c_val, c_glu, c_gate, conv_w, conv_b, ln_g, ln_b):
    T = CONV_T

    def body(cv_ref, cg_ref, cvh_ref, cgh_ref, gate_ref, w_ref, b_ref, lg_ref, lb_ref, u_ref, y_ref, win, us):
        i = pl.program_id(0)
        for cb in range(NCH):
            cols = slice(cb * LANES, (cb + 1) * LANES)
            win[cb, HALO:HALO + T, :] = cv_ref[:, cols] * _sigmoid(cg_ref[:, cols])
            win[cb, 0:HALO, :] = jnp.where(i > 0, cvh_ref[:, cols] * _sigmoid(cgh_ref[:, cols]), 0.0)
        for cb in range(NCH):
            cols = slice(cb * LANES, (cb + 1) * LANES)
            taps = _taps(w_ref, cols)
            bias = jnp.broadcast_to(b_ref[:, cols], (SUBLANES, LANES))

            def group(g, carry):
                for b in range(SUBLANES):
                    base = g * GROUP + b
                    acc = bias
                    for j in range(CONV_K):
                        acc = acc + taps[j] * _comb(win, cb, base + (HALO - (CONV_K - 1) + j))
                    us[cb, pl.ds(base, SUBLANES, stride=SUBLANES), :] = acc
                return carry

            lax.fori_loop(0, T // GROUP, group, 0)
        total = us[0]
        for cb in range(1, NCH):
            total = total + us[cb]
        mu = jnp.sum(total, axis=-1, keepdims=True) * (1.0 / D)
        sq = jnp.zeros((T, LANES), F32)
        for cb in range(NCH):
            uc = us[cb] - mu
            sq = sq + uc * uc
        rstd = lax.rsqrt(jnp.sum(sq, axis=-1, keepdims=True) * (1.0 / D) + LN_EPS)
        for cb in range(NCH):
            cols = slice(cb * LANES, (cb + 1) * LANES)
            u = us[cb]
            u_ref[:, cols] = u
            nrm = (u - mu) * rstd * lg_ref[:, cols] + lb_ref[:, cols]
            gate = gate_ref[:, cols]
            y_ref[:, cols] = (nrm * _sigmoid(nrm) * (gate * _sigmoid(gate))).astype(BF16)

    halo = pl.BlockSpec((HALO, D), _halo_before)
    return pl.pallas_call(
        body, grid=(S // T,), name="conv_fwd",
        in_specs=[_rows(T, D), _rows(T, D), halo, halo, _rows(T, D),
                  _resident((HALO, D)), _resident((1, D)), _resident((1, D)), _resident((1, D))],
        out_specs=[_rows(T, D), _rows(T, D)],
        out_shape=[jax.ShapeDtypeStruct((S, D), F32), jax.ShapeDtypeStruct((S, D), BF16)],
        scratch_shapes=[pltpu.VMEM((NCH, T + HALO, LANES), F32), pltpu.VMEM((NCH, T, LANES), F32)],
        compiler_params=_params(("arbitrary",)),
    )(c_val, c_glu, c_val, c_glu, c_gate, conv_w, conv_b, ln_g, ln_b)


def _conv_bwd_rows(u, c_gate, dy_conv, ln_g, ln_b):
    tm = 256

    def body(u_ref, gate_ref, dy_ref, lg_ref, lb_ref, du_ref, dgate_ref, st_ref):
        @pl.when(pl.program_id(0) == 0)
        def _():
            st_ref[...] = jnp.zeros_like(st_ref)

        u, gate, dy = u_ref[...], gate_ref[...], dy_ref[...]
        mu = jnp.mean(u, axis=-1, keepdims=True)
        uc = u - mu
        rstd = lax.rsqrt(jnp.mean(uc * uc, axis=-1, keepdims=True) + LN_EPS)
        z = uc * rstd
        nrm = z * lg_ref[...] + lb_ref[...]
        sn, sg = _sigmoid(nrm), _sigmoid(gate)
        dgate_ref[...] = (dy * (nrm * sn) * (sg * (1.0 + gate * (1.0 - sg)))).astype(BF16)
        dn = dy * (gate * sg) * (sn * (1.0 + nrm * (1.0 - sn)))
        dz = dn * lg_ref[...]
        du = rstd * (dz - jnp.mean(dz, axis=-1, keepdims=True) - z * jnp.mean(dz * z, axis=-1, keepdims=True))
        du_ref[...] = du
        st_ref[0:1, :] += jnp.sum(dn * z, axis=0, keepdims=True)
        st_ref[1:2, :] += jnp.sum(dn, axis=0, keepdims=True)
        st_ref[2:3, :] += jnp.sum(du, axis=0, keepdims=True)

    big = jax.ShapeDtypeStruct((S, D), F32)
    return pl.pallas_call(
        body, grid=(S // tm,), name="conv_bwd_rows",
        in_specs=[_rows(tm, D)] * 3 + [_resident((1, D)), _resident((1, D))],
        out_specs=[_rows(tm, D), _rows(tm, D), pl.BlockSpec((8, D), lambda i: (0, 0))],
        out_shape=[big, jax.ShapeDtypeStruct((S, D), BF16), jax.ShapeDtypeStruct((8, D), F32)],
        compiler_params=_params(("arbitrary",)),
    )(u, c_gate, dy_conv, ln_g, ln_b)


def _conv_bwd_taps(du, c_val, c_glu, conv_w):
    T = CONV_T
    last = S // T - 1

    def body(du_ref, dua_ref, cv_ref, cg_ref, cvh_ref, cgh_ref, w_ref, dcv_ref, dcg_ref, dw_ref,
             hwin, dwin, dhs, dw_acc):
        i = pl.program_id(0)

        @pl.when(i == 0)
        def _():
            dw_acc[...] = jnp.zeros_like(dw_acc)

        for cb in range(NCH):
            cols = slice(cb * LANES, (cb + 1) * LANES)
            hwin[cb, HALO:HALO + T, :] = cv_ref[:, cols] * _sigmoid(cg_ref[:, cols])
            hwin[cb, 0:HALO, :] = jnp.where(i > 0, cvh_ref[:, cols] * _sigmoid(cgh_ref[:, cols]), 0.0)
            dwin[cb, 0:T, :] = du_ref[:, cols]
            dwin[cb, T:T + HALO, :] = jnp.where(i < last, dua_ref[:, cols], 0.0)
        for cb in range(NCH):
            cols = slice(cb * LANES, (cb + 1) * LANES)
            taps = _taps(w_ref, cols)

            def group_dh(g, carry):
                for b in range(SUBLANES):
                    base = g * GROUP + b
                    acc = jnp.zeros((SUBLANES, LANES), F32)
                    for j in range(CONV_K):
                        acc = acc + taps[j] * _comb(dwin, cb, base + (CONV_K - 1 - j))
                    dhs[cb, pl.ds(base, SUBLANES, stride=SUBLANES), :] = acc
                return carry

            lax.fori_loop(0, T // GROUP, group_dh, 0)

            def group_dw(g, sums):
                for b in range(SUBLANES):
                    base = g * GROUP + b
                    d = _comb(dwin, cb, base)
                    sums = tuple(sums[j] + d * _comb(hwin, cb, base + (HALO - (CONV_K - 1) + j))
                                 for j in range(CONV_K))
                return sums

            sums = lax.fori_loop(0, T // GROUP, group_dw, tuple(dw_acc[j, :, cols] for j in range(CONV_K)))
            for j in range(CONV_K):
                dw_acc[j, :, cols] = sums[j]
            dh = dhs[cb]
            cv, sg = cv_ref[:, cols], _sigmoid(cg_ref[:, cols])
            dcv_ref[:, cols] = (dh * sg).astype(BF16)
            dcg_ref[:, cols] = (dh * cv * (sg * (1.0 - sg))).astype(BF16)

        @pl.when(i == last)
        def _():
            dw_ref[...] = jnp.zeros_like(dw_ref)
            for j in range(CONV_K):
                dw_ref[j:j + 1, :] = jnp.sum(dw_acc[j], axis=0, keepdims=True)

    before = pl.BlockSpec((HALO, D), _halo_before)
    after = pl.BlockSpec((HALO, D), _halo_after)
    big = jax.ShapeDtypeStruct((S, D), BF16)
    return pl.pallas_call(
        body, grid=(S // T,), name="conv_bwd_taps",
        in_specs=[_rows(T, D), after, _rows(T, D), _rows(T, D), before, before, _resident((HALO, D))],
        out_specs=[_rows(T, D), _rows(T, D), pl.BlockSpec((HALO, D), lambda i: (0, 0))],
        out_shape=[big, big, jax.ShapeDtypeStruct((HALO, D), F32)],
        scratch_shapes=[pltpu.VMEM((NCH, T + HALO, LANES), F32), pltpu.VMEM((NCH, T + HALO, LANES), F32),
                        pltpu.VMEM((NCH, T, LANES), F32), pltpu.VMEM((CONV_K, SUBLANES, D), F32)],
        compiler_params=_params(("arbitrary",)),
    )(du, du, c_val, c_glu, c_val, c_glu, conv_w)


def _outproj_loss(y_att, y_conv, w_out_bf, x, target, gf):
    tm = 512

    def body(ya_ref, yc_ref, w_ref, x_ref, t_ref, gf_ref, dx2_ref, dya_ref, dyc_ref, dw_ref, st_ref, acc):
        @pl.when(pl.program_id(0) == 0)
        def _():
            acc[...] = jnp.zeros_like(acc)
            st_ref[...] = jnp.zeros_like(st_ref)

        ya, yc = ya_ref[...], yc_ref[...]
        x2 = x_ref[...] + _dot(ya, w_ref[0:D, :]) + _dot(yc, w_ref[D:2 * D, :])
        r = lax.rsqrt(jnp.mean(x2 * x2, axis=-1, keepdims=True) + NORM_EPS)
        xn = x2 * r
        err = xn * gf_ref[...] - t_ref[...]
        dout = err * (1.0 / D)
        dxn = dout * gf_ref[...]
        dx2 = r * (dxn - xn * jnp.mean(dxn * xn, axis=-1, keepdims=True))
        dx2_ref[...] = dx2
        dx2b = dx2.astype(BF16)
        dya_ref[...] = _dot_nt(dx2b, w_ref[0:D, :])
        dyc_ref[...] = _dot_nt(dx2b, w_ref[D:2 * D, :])
        acc[0:D, :] += _dot_tn(ya, dx2b)
        acc[D:2 * D, :] += _dot_tn(yc, dx2b)
        st_ref[0:1, :] += jnp.sum(dout * xn, axis=0, keepdims=True)
        st_ref[1:2, :] += jnp.sum(err * err, axis=0, keepdims=True) * (0.5 / D)

        @pl.when(pl.program_id(0) == S // tm - 1)
        def _():
            dw_ref[...] = acc[...].astype(BF16)

    big = jax.ShapeDtypeStruct((S, D), F32)
    return pl.pallas_call(
        body, grid=(S // tm,), name="outproj_loss",
        in_specs=[_rows(tm, D), _rows(tm, D), _resident((WOUT_ROWS, D)), _rows(tm, D), _rows(tm, D), _resident((1, D))],
        out_specs=[_rows(tm, D), _rows(tm, D), _rows(tm, D),
                   pl.BlockSpec((WOUT_ROWS, D), lambda i: (0, 0)), pl.BlockSpec((8, D), lambda i: (0, 0))],
        out_shape=[big, big, big, jax.ShapeDtypeStruct((WOUT_ROWS, D), BF16), jax.ShapeDtypeStruct((8, D), F32)],
        scratch_shapes=[pltpu.VMEM((WOUT_ROWS, D), F32)],
        compiler_params=_params(("arbitrary",)),
    )(y_att, y_conv, w_out_bf, x, target, gf)


UNITS_PER_CHUNK = CHUNK // LANES


def _dproj_unit(u, dqs, dkvs, gates, rows):
    if u < OFF_K // LANES:
        return ((dqs[0][u] + dqs[1][u] + dqs[2][u]) * (HD ** -0.5)).astype(BF16)
    if u < OFF_AG // LANES:
        w = u - OFF_K // LANES
        ta, tb = (dkvs[0][j] + dkvs[1][j] + dkvs[2][j] for j in (2 * (w % 2), 2 * (w % 2) + 1))
        low = _low_lanes(rows)
        if w < 2:
            return jnp.where(low, ta, pltpu.roll(tb, HD, axis=1)).astype(BF16)
        return jnp.where(low, pltpu.roll(ta, HD, axis=1), tb).astype(BF16)
    g, sl = divmod(u - OFF_AG // LANES, D // LANES)
    return gates[g][:, sl * LANES:(sl + 1) * LANES]


def _dproj_sources(units, dqs, dkvs, gates, rows):
    use_q = any(u < OFF_K // LANES for u in units)
    use_kv = any(OFF_K // LANES <= u < OFF_AG // LANES for u in units)
    use_g = sorted({(u - OFF_AG // LANES) // (D // LANES) for u in units if u >= OFF_AG // LANES})
    args = (list(dqs) if use_q else []) + (list(dkvs) if use_kv else []) + [gates[g] for g in use_g]
    specs = ([_slab_rows(D // LANES, rows)] * 3 if use_q else []) + ([_slab_rows(NKV, rows)] * 3 if use_kv else []) \
        + [_rows(rows, D)] * len(use_g)

    def pick(refs):
        refs = list(refs)
        q_refs = [refs.pop(0) for _ in range(3)] if use_q else None
        kv_refs = [refs.pop(0) for _ in range(3)] if use_kv else None
        return q_refs, kv_refs, {g: refs.pop(0) for g in use_g}

    return args, specs, pick


def _exchange_results_of(chip_sums):
    n = len(chip_sums) * len(CHIP_FLIPS)
    shapes = [jax.ShapeDtypeStruct((NCHIP,) + tuple(a.shape[1:] if a.ndim == 3 else a.shape), a.dtype)
              for a in chip_sums]
    return shapes, [pltpu.SemaphoreType.DMA((n,)), pltpu.SemaphoreType.DMA((n,))]


def _start_exchange(copies, first_step):
    @pl.when(first_step)
    def _():
        for out, _ in copies:
            out.start()


def _finish_exchange(copies, last_step):
    @pl.when(last_step)
    def _():
        for _, arrival in copies:
            arrival.wait_recv()
        for out, _ in copies:
            out.wait_send()


def _inproj_bwd_x(dqs, dkvs, gates, w_bf, x, g1, dx2, pi):
    tm = 256
    last = S // tm - 1
    units = range(NCOL // LANES)
    pieces, piece_specs, pick = _dproj_sources(units, dqs, dkvs, gates, tm)
    landing, sems = _exchange_results_of([pi])

    def body(*refs):
        piece_refs, refs = refs[:len(pieces)], refs[len(pieces):]
        w_ref, x_ref, g_ref, dx2_ref, pi_ref, gx_ref, st_ref, ri_ref, dp_ref, send, recv = refs
        i = pl.program_id(0)
        copies = _chip_exchange_copies([pi_ref], [ri_ref], send, recv)
        _start_exchange(copies, i == 0)

        @pl.when(i == 0)
        def _():
            st_ref[...] = jnp.zeros_like(st_ref)

        sources = pick(piece_refs)
        for u in units:
            dp_ref[:, u * LANES:(u + 1) * LANES] = _dproj_unit(u, *sources, tm)
        dh = _dot_nt(dp_ref[...], w_ref[...])
        xt = x_ref[...]
        r = lax.rsqrt(jnp.mean(xt * xt, axis=-1, keepdims=True) + NORM_EPS)
        xn = xt * r
        dxn = dh * g_ref[...]
        gx_ref[...] = dx2_ref[...] + r * (dxn - xn * jnp.mean(dxn * xn, axis=-1, keepdims=True))
        st_ref[0:1, :] += jnp.sum(dh * xn, axis=0, keepdims=True)
        _finish_exchange(copies, i == last)

    return pl.pallas_call(
        body, grid=(S // tm,), name="inproj_bwd_x",
        in_specs=piece_specs + [_resident((D, NCOL)), _rows(tm, D), _resident((1, D)), _rows(tm, D), ANY],
        out_specs=[_rows(tm, D), pl.BlockSpec((8, D), lambda i: (0, 0)), ANY],
        out_shape=[jax.ShapeDtypeStruct((S, D), F32), jax.ShapeDtypeStruct((8, D), F32)] + landing,
        scratch_shapes=[pltpu.VMEM((tm, NCOL), BF16)] + sems,
        compiler_params=_params(("arbitrary",)),
    )(*pieces, w_bf, x, g1, dx2, pi)


def _inproj_bwd_w(h, dqs, dkvs, gates):
    out = None
    for k in range(NCHIP):
        units = range(k * UNITS_PER_CHUNK, (k + 1) * UNITS_PER_CHUNK)
        tk = 512 if units[0] < OFF_K // LANES else 1024
        nk = S // tk
        pieces, piece_specs, pick = _dproj_sources(units, dqs, dkvs, gates, tk)
        handed_on = [] if out is None else [out]

        def body(*refs, units=units, pick=pick, n_pieces=len(pieces), n_in=1 + len(pieces) + len(handed_on)):
            h_ref, piece_refs = refs[0], refs[1:1 + n_pieces]
            o_ref, tile, acc = refs[n_in:]
            i = pl.program_id(0)

            @pl.when(i == 0)
            def _():
                acc[...] = jnp.zeros_like(acc)

            sources = pick(piece_refs)
            for n, u in enumerate(units):
                tile[:, n * LANES:(n + 1) * LANES] = _dproj_unit(u, *sources, tk)
            acc[...] += _dot_tn(h_ref[...], tile[...])

            @pl.when(i == nk - 1)
            def _():
                o_ref[0] = acc[...].astype(BF16)

        out = pl.pallas_call(
            body, grid=(nk,), name=f"inproj_bwd_w{k}",
            in_specs=[_rows(tk, D)] + piece_specs + [ANY] * len(handed_on),
            out_specs=pl.BlockSpec((1, D, CHUNK), lambda i, k=k: (k, 0, 0)),
            out_shape=jax.ShapeDtypeStruct((NCHIP, D, CHUNK), BF16),
            input_output_aliases={1 + len(pieces): 0} if handed_on else {},
            scratch_shapes=[pltpu.VMEM((tk, CHUNK), BF16), pltpu.VMEM((D, CHUNK), F32)],
            compiler_params=_params(("arbitrary",)),
        )(h, *pieces, *handed_on)
    return out


ROW_LN_G, ROW_LN_B, ROW_CONV_B, ROW_FINAL_G, ROW_LOSS, ROW_TAPS = 0, 1, 2, 8, 9, 16
SMALL_ROWS = 16 + HALO
NDEV = 8


MESH = pl.DeviceIdType.MESH
ANY = pl.BlockSpec(memory_space=pl.ANY)
CHIP_FLIPS = ((1, 0), (0, 1), (1, 1))


def _pos():
    return lax.axis_index("x"), lax.axis_index("y"), lax.axis_index("c")


def _flip(v, f):
    return 1 - v if f else v


def _ds(start, size, align=None):
    return pl.ds(pl.multiple_of(start, align or size), size)


def _place_shards(wi, wo, cw, where):
    steps = 4

    def body(where_ref, wi_ref, wo_ref, cw_ref, wi_full, wo_full, cw_full):
        wi_full[...] = wi_ref[...].astype(BF16)
        wo_full[...] = wo_ref[...].astype(BF16)
        cw_full[...] = cw_ref[...]

    grid_spec = pltpu.PrefetchScalarGridSpec(
        num_scalar_prefetch=1, grid=(steps,),
        in_specs=[pl.BlockSpec((D // steps, CHUNK), lambda i, w: (i, 0)),
                  pl.BlockSpec((WOUT_SHARD // steps, D), lambda i, w: (i, 0)),
                  pl.BlockSpec((HALO, CONVW_SHARD), lambda i, w: (0, 0))],
        out_specs=[pl.BlockSpec((D // steps, CHUNK), lambda i, w: (i, w[0])),
                   pl.BlockSpec((WOUT_SHARD // steps, D), lambda i, w: (w[0] * steps + i, 0)),
                   pl.BlockSpec((HALO, CONVW_SHARD), lambda i, w: (0, w[0]))])
    return pl.pallas_call(
        body, grid_spec=grid_spec, name="place_shards",
        out_shape=[jax.ShapeDtypeStruct((D, NCOL), BF16), jax.ShapeDtypeStruct((WOUT_ROWS, D), BF16),
                   jax.ShapeDtypeStruct((HALO, D), F32)],
        compiler_params=_params(("arbitrary",)),
    )(where, wi, wo, cw)


def _gather_weights(wi_full, wo_full, cw_full):
    halves = (D // 2, WOUT_SHARD // 2, HALO // 2)
    OWN_X, OWN_Y, VIA_Y, VIA_X = range(4)

    def body(_wi, _wo, _cw, wi_full, wo_full, cw_full, send, recv):
        x, y, c = _pos()
        x_nbr, y_nbr, diag = (1 - x, y), (x, 1 - y), (1 - x, 1 - y)

        def region(a, chip_xy, half, part=None):
            chip = 2 * chip_xy[0] + chip_xy[1]
            n, row = halves[a], half * halves[a]
            if part is not None:
                n = n // 2
                row = row + part * n
            if a == 0:
                return wi_full.at[_ds(row, n), _ds(chip * CHUNK, CHUNK, 128)]
            if a == 1:
                return wo_full.at[_ds(chip * WOUT_SHARD + row, n), :]
            return cw_full.at[_ds(row, n), _ds(chip * CONVW_SHARD, CONVW_SHARD, 128)]

        def copy(a, kind, piece, dev):
            k = 8 * a + kind
            return pltpu.make_async_remote_copy(src_ref=piece, dst_ref=piece, send_sem=send.at[k], recv_sem=recv.at[k],
                                                device_id=dev, device_id_type=MESH)

        def to_sibling(a, kind, piece):
            cp = copy(a, 4 + kind, piece, (x, y, 1 - c))
            cp.start()
            return cp

        sends = []
        for a in range(3):
            for kind, nbr in ((OWN_X, x_nbr), (OWN_Y, y_nbr)):
                cp = copy(a, kind, region(a, (x, y), c), (*nbr, c))
                cp.start()
                sends.append(cp)
        for a in range(3):
            got = region(a, x_nbr, c)
            copy(a, OWN_X, got, (*x_nbr, c)).wait_recv()
            onward = copy(a, VIA_Y, region(a, x_nbr, c, 0), (*y_nbr, c))
            onward.start()
            sends += [onward, to_sibling(a, OWN_X, got)]
            got = region(a, y_nbr, c)
            copy(a, OWN_Y, got, (*y_nbr, c)).wait_recv()
            onward = copy(a, VIA_X, region(a, y_nbr, c, 1), (*x_nbr, c))
            onward.start()
            sends += [onward, to_sibling(a, OWN_Y, got)]
        for a in range(3):
            got = region(a, diag, c, 0)
            copy(a, VIA_Y, got, (*y_nbr, c)).wait_recv()
            sends.append(to_sibling(a, VIA_Y, got))
            got = region(a, diag, c, 1)
            copy(a, VIA_X, got, (*x_nbr, c)).wait_recv()
            sends.append(to_sibling(a, VIA_X, got))
        for a in range(3):
            for kind, piece in ((OWN_X, region(a, x_nbr, 1 - c)), (OWN_Y, region(a, y_nbr, 1 - c)),
                                (VIA_Y, region(a, diag, 1 - c, 0)), (VIA_X, region(a, diag, 1 - c, 1))):
                copy(a, 4 + kind, piece, (x, y, 1 - c)).wait_recv()
        for cp in sends:
            cp.wait_send()

    n_sems = 3 * 8
    return pl.pallas_call(
        body, name="gather_weights",
        in_specs=[ANY, ANY, ANY], out_specs=[ANY, ANY, ANY], input_output_aliases={0: 0, 1: 1, 2: 2},
        out_shape=[jax.ShapeDtypeStruct((D, NCOL), BF16), jax.ShapeDtypeStruct((WOUT_ROWS, D), BF16),
                   jax.ShapeDtypeStruct((HALO, D), F32)],
        scratch_shapes=[pltpu.SemaphoreType.DMA((n_sems,)), pltpu.SemaphoreType.DMA((n_sems,))],
    )(wi_full, wo_full, cw_full)


def _half_shape(a):
    return jax.ShapeDtypeStruct((NCHIP, a.shape[1] // 2, a.shape[2]) if a.ndim == 3 else a.shape, a.dtype)


def _exchange_halves(arrays, name):
    n = len(arrays)

    def body(*refs):
        srcs, dsts, (send, recv) = refs[:n], refs[n:2 * n], refs[2 * n:]
        x, y, c = _pos()
        cps = []
        for k, (s_, d_) in enumerate(zip(srcs, dsts)):
            if len(s_.shape) == 3:
                h = s_.shape[1] // 2
                s_ = s_.at[:, _ds((1 - c) * h, h), :]
            cps.append(pltpu.make_async_remote_copy(src_ref=s_, dst_ref=d_, send_sem=send.at[k], recv_sem=recv.at[k],
                                                    device_id=(x, y, 1 - c), device_id_type=MESH))
        for cp in cps:
            cp.start()
        for cp in cps:
            cp.wait()

    return pl.pallas_call(
        body, name=name, in_specs=[ANY] * n, out_specs=[ANY] * n, out_shape=[_half_shape(a) for a in arrays],
        scratch_shapes=[pltpu.SemaphoreType.DMA((n,)), pltpu.SemaphoreType.DMA((n,))],
    )(*arrays)


def _add_halves(arrays, received, name):
    n = len(arrays)

    def body(*refs):
        mine, theirs, outs = refs[:n], refs[n:2 * n], refs[2 * n:]
        c = lax.axis_index("c")
        for m_, t_, o_ in zip(mine, theirs, outs):
            if len(m_.shape) == 3:
                h = m_.shape[1] // 2
                o_[0] = (m_[0, _ds(c * h, h), :].astype(F32) + t_[0].astype(F32)).astype(o_.dtype)
            else:
                o_[...] = m_[...] + t_[...]

    def spec(shape):
        if len(shape) == 3:
            return pl.BlockSpec((1,) + tuple(shape[1:]), lambda k: (k, 0, 0))
        return pl.BlockSpec(tuple(shape), lambda k: (0, 0))

    halves = [_half_shape(a) for a in arrays]
    return pl.pallas_call(
        body, grid=(NCHIP,), name=name,
        in_specs=[spec(a.shape) for a in arrays] + [spec(h.shape) for h in halves],
        out_specs=[spec(h.shape) for h in halves], out_shape=halves,
        compiler_params=_params(("arbitrary",)),
    )(*arrays, *received)


def _chip_exchange_copies(srcs, dsts, send, recv):
    x, y, c = _pos()
    me = 2 * x + y
    pairs = []
    for a in range(len(srcs)):
        for j, (fx, fy) in enumerate(CHIP_FLIPS):
            px, py = _flip(x, fx), _flip(y, fy)
            peer = 2 * px + py
            k = len(CHIP_FLIPS) * a + j
            out = pltpu.make_async_remote_copy(
                src_ref=srcs[a].at[peer] if len(srcs[a].shape) == 3 else srcs[a], dst_ref=dsts[a].at[me],
                send_sem=send.at[k], recv_sem=recv.at[k], device_id=(px, py, c), device_id_type=MESH)
            got = dsts[a].at[peer]
            arrival = pltpu.make_async_remote_copy(
                src_ref=got, dst_ref=got, send_sem=send.at[k], recv_sem=recv.at[k],
                device_id=(px, py, c), device_id_type=MESH)
            pairs.append((out, arrival))
    return pairs


def _sum_chips(ri, ro, rs, pi, po, ps, where):
    def body(w_ref, ri_ref, ro_ref, rs_ref, pi_ref, po_ref, ps_ref, gi_ref, go_ref, gs_ref, g5_ref, loss_ref,
             acc_i, acc_o, acc_s):
        k = pl.program_id(0)
        accs = (acc_i, acc_o, acc_s)

        @pl.when(k == 0)
        def _():
            for acc in accs:
                acc[...] = jnp.zeros_like(acc)

        @pl.when(k == w_ref[0])
        def _():
            for acc, val in zip(accs, (pi_ref[0], po_ref[0], ps_ref[...])):
                acc[...] += val.astype(F32)

        @pl.when(k != w_ref[0])
        def _():
            for acc, ref in zip(accs, (ri_ref, ro_ref, rs_ref)):
                acc[...] += ref[0].astype(F32)

        @pl.when(k == NCHIP - 1)
        def _():
            gi_ref[0] = acc_i[...]
            go_ref[0] = acc_o[...]
            gs_ref[...] = acc_s[...]
            g5_ref[...] = jnp.zeros_like(g5_ref)
            for i, row in enumerate((ROW_CONV_B, ROW_LN_G, ROW_LN_B, ROW_FINAL_G)):
                g5_ref[i + 1:i + 2, :] = acc_s[row:row + 1, :]
            loss = jnp.sum(acc_s[ROW_LOSS:ROW_LOSS + 1, :], axis=1, keepdims=True)
            loss_ref[...] = jnp.broadcast_to(loss, loss_ref.shape)

    def sent(k, w):
        return jnp.where(k == w[0], (k + 1) % NCHIP, k)

    hi, ho = D // 2, WOUT_SHARD // 2
    const = lambda shape: pl.BlockSpec(shape, lambda k, w: (0,) * len(shape))
    grid_spec = pltpu.PrefetchScalarGridSpec(
        num_scalar_prefetch=1, grid=(NCHIP,),
        in_specs=[pl.BlockSpec((1, hi, CHUNK), lambda k, w: (sent(k, w), 0, 0)),
                  pl.BlockSpec((1, ho, D), lambda k, w: (sent(k, w), 0, 0)),
                  pl.BlockSpec((1, SMALL_ROWS, D), lambda k, w: (sent(k, w), 0, 0)),
                  pl.BlockSpec((1, hi, CHUNK), lambda k, w: (w[0], 0, 0)),
                  pl.BlockSpec((1, ho, D), lambda k, w: (w[0], 0, 0)),
                  const((SMALL_ROWS, D))],
        out_specs=[pl.BlockSpec((1, hi, CHUNK), lambda k, w: (w[1], 0, 0)),
                   pl.BlockSpec((1, ho, D), lambda k, w: (w[1], 0, 0)),
                   const((SMALL_ROWS, D)), const((8, D)), const((8, LANES))],
        scratch_shapes=[pltpu.VMEM((hi, CHUNK), F32), pltpu.VMEM((ho, D), F32), pltpu.VMEM((SMALL_ROWS, D), F32)])
    return pl.pallas_call(
        body, grid_spec=grid_spec, name="sum_chips",
        out_shape=[jax.ShapeDtypeStruct((2, hi, CHUNK), F32), jax.ShapeDtypeStruct((2, ho, D), F32),
                   jax.ShapeDtypeStruct((SMALL_ROWS, D), F32), jax.ShapeDtypeStruct((8, D), F32),
                   jax.ShapeDtypeStruct((8, LANES), F32)],
        compiler_params=_params(("arbitrary",)),
    )(where, ri, ro, rs, pi, po, ps)


def _exchange_results(gi2, go2, st):
    flips = [(fx, fy, fc) for fx in (0, 1) for fy in (0, 1) for fc in (0, 1)][1:]

    def body(_gi, _go, st_ref, gi_ref, go_ref, all_ref, send, recv, lsem):
        x, y, c = _pos()
        sib = (x, y, 1 - c)

        def half(k, ref, slot):
            return pltpu.make_async_remote_copy(src_ref=ref.at[slot], dst_ref=ref.at[slot], send_sem=send.at[k],
                                                recv_sem=recv.at[k], device_id=sib, device_id_type=MESH)

        def stat(k, src, slot, dev):
            return pltpu.make_async_remote_copy(src_ref=src, dst_ref=all_ref.at[slot], send_sem=send.at[k],
                                                recv_sem=recv.at[k], device_id=dev, device_id_type=MESH)

        mine = pltpu.make_async_copy(st_ref, all_ref.at[4 * x + 2 * y + c], lsem)
        mine.start()
        sends = [half(k, ref, c) for k, ref in enumerate((gi_ref, go_ref))]
        peers = [(_flip(x, fx), _flip(y, fy), _flip(c, fc)) for fx, fy, fc in flips]
        sends += [stat(2 + k, st_ref, 4 * x + 2 * y + c, dev) for k, dev in enumerate(peers)]
        for cp in sends:
            cp.start()
        for k, ref in enumerate((gi_ref, go_ref)):
            half(k, ref, 1 - c).wait_recv()
        for k, (px, py, pc) in enumerate(peers):
            slot = 4 * px + 2 * py + pc
            stat(2 + k, all_ref.at[slot], slot, (px, py, pc)).wait_recv()
        for cp in sends:
            cp.wait_send()
        mine.wait()

    n = 2 + len(flips)
    return pl.pallas_call(
        body, name="exchange_results",
        in_specs=[ANY, ANY, ANY], out_specs=[ANY, ANY, ANY], input_output_aliases={0: 0, 1: 1},
        out_shape=[jax.ShapeDtypeStruct((2, D // 2, CHUNK), F32), jax.ShapeDtypeStruct((2, WOUT_SHARD // 2, D), F32),
                   jax.ShapeDtypeStruct((NDEV, 8, D), F32)],
        scratch_shapes=[pltpu.SemaphoreType.DMA((n,)), pltpu.SemaphoreType.DMA((n,)), pltpu.SemaphoreType.DMA],
    )(gi2, go2, st)


def _adamw_math(w, g, m, v):
    m2 = ADAM_B1 * m + (1.0 - ADAM_B1) * g
    v2 = ADAM_B2 * v + (1.0 - ADAM_B2) * (g * g)
    m_hat = m2 / (1.0 - ADAM_B1 ** ADAM_STEP)
    v_hat = v2 / (1.0 - ADAM_B2 ** ADAM_STEP)
    delta = -ADAM_LR * (m_hat / (jnp.sqrt(v_hat) + ADAM_EPS) + ADAM_WD * w)
    return delta, m2, v2


def _adamw(w, g, m, v, name):
    rows, cols = w.shape
    tm = 256 if rows % 256 == 0 else rows

    def body(w_ref, g_ref, m_ref, v_ref, d_ref, m2_ref, v2_ref):
        d_ref[...], m2_ref[...], v2_ref[...] = _adamw_math(w_ref[...], g_ref[...], m_ref[...], v_ref[...])

    shape = jax.ShapeDtypeStruct(w.shape, F32)
    return pl.pallas_call(
        body, grid=(rows // tm,), name=name,
        in_specs=[_rows(tm, cols)] * 4, out_specs=[_rows(tm, cols)] * 3, out_shape=[shape] * 3,
        compiler_params=_params(("arbitrary",)),
    )(w, g, m, v)


def _adamw_vectors(g5, first_parts, ws, ms, vs):
    n = len(ws)

    def body(g_ref, parts_ref, *refs):
        ins, g0_ref, outs = refs[:3 * n], refs[3 * n], refs[3 * n + 1:]
        g0 = parts_ref[0, 0:1, :]
        for dev in range(1, NDEV):
            g0 = g0 + parts_ref[dev, 0:1, :]
        g0_ref[...] = g0
        for i in range(n):
            g = g0 if i == 0 else g_ref[i:i + 1, :]
            res = _adamw_math(ins[i][...], g, ins[n + i][...], ins[2 * n + i][...])
            for kind in range(3):
                outs[kind * n + i][...] = res[kind]

    shape = jax.ShapeDtypeStruct((1, D), F32)
    return pl.pallas_call(body, name="adamw_vectors", out_shape=[shape] * (1 + 3 * n), compiler_params=_params())(
        g5, first_parts, *ws, *ms, *vs)


def kernel(x, norm_g, w_in, conv_w, conv_b, conv_ln_g, conv_ln_b, w_out, final_norm_g, loss_target, m_norm_g, m_w_in, m_conv_w, m_conv_b, m_conv_ln_g, m_conv_ln_b, m_w_out, m_final_norm_g, v_norm_g, v_w_in, v_conv_w, v_conv_b, v_conv_ln_g, v_conv_ln_b, v_w_out, v_final_norm_g):
    chip = 2 * lax.axis_index("x") + lax.axis_index("y")
    where = jnp.stack([chip, lax.axis_index("c")]).astype(jnp.int32)
    taps_shard = jnp.pad(conv_w[0], ((0, HALO - CONV_K), (0, 0)))
    wi_full, wo_full, cw_full = _gather_weights(*_place_shards(w_in[0], w_out[0], taps_shard, where))

    gf = final_norm_g[None]
    xb = x[0]
    h, q, k, v, a_gate, c_val, c_glu, c_gate = _inproj_fwd(xb, norm_g, wi_full)
    tables = [_bias_table(d) for d in PATTERNS]
    o, lse, y_att = _attn_fwd(q, k, v, tables, a_gate)
    u, y_conv = _conv_fwd(c_val, c_glu, c_gate, cw_full, conv_b, conv_ln_g, conv_ln_b)
    dx2, dy_att, dy_conv, dw_out, st_out = _outproj_loss(y_att, y_conv, wo_full, xb, loss_target[0], gf)
    du, dc_gate, st_conv = _conv_bwd_rows(u, c_gate, dy_conv, conv_ln_g, conv_ln_b)
    dc_val, dc_glu, dconv_w = _conv_bwd_taps(du, c_val, c_glu, cw_full)

    early = [dw_out.reshape(NCHIP, WOUT_SHARD, D), jnp.concatenate([st_conv, st_out, dconv_w], axis=0)]
    po, ps = _add_halves(early, _exchange_halves(early, "exchange_halves_early"), "add_halves_early")
    do, da_gate, delta, ro, rs = _attn_gate_bwd(dy_att, o, a_gate, _head_sum_selectors(), [po, ps])
    dqs, dkvs = zip(*[_attn_bwd(q, k, v, do, lse, delta, t, d) for t, d in zip(tables, PATTERNS)])

    dproj_pieces = (dqs, dkvs, (da_gate, dc_val, dc_glu, dc_gate))
    late = [_inproj_bwd_w(h, *dproj_pieces)]
    (pi,) = _add_halves(late, _exchange_halves(late, "exchange_halves"), "add_halves")
    grad_x, st_in, ri = _inproj_bwd_x(*dproj_pieces, wi_full, xb, norm_g, dx2, pi)
    gi2, go2, g_small, g5, loss8 = _sum_chips(ri, ro, rs, pi, po, ps, where)
    gi2, go2, norm_g_parts = _exchange_results(gi2, go2, st_in)
    g_w_in = gi2.reshape(D, CHUNK)
    g_w_out = go2.reshape(WOUT_SHARD, D)
    g_taps = lax.dynamic_slice(g_small, (ROW_TAPS, chip * CONVW_SHARD), (CONV_K, CONVW_SHARD))

    d_w_in, m2_w_in, v2_w_in = _adamw(w_in[0], g_w_in, m_w_in[0], v_w_in[0], "adamw_w_in")
    d_w_out, m2_w_out, v2_w_out = _adamw(w_out[0], g_w_out, m_w_out[0], v_w_out[0], "adamw_w_out")
    d_taps, m2_taps, v2_taps = _adamw(conv_w[0], g_taps, m_conv_w[0], v_conv_w[0], "adamw_conv_w")
    g_norm, *vec = _adamw_vectors(
        g5, norm_g_parts,
        (norm_g, conv_b, conv_ln_g, conv_ln_b, gf),
        (m_norm_g, m_conv_b, m_conv_ln_g, m_conv_ln_b, m_final_norm_g[None]),
        (v_norm_g, v_conv_b, v_conv_ln_g, v_conv_ln_b, v_final_norm_g[None]))
    d_vec, m2_vec, v2_vec = vec[0:5], vec[5:10], vec[10:15]

    def weight_order(ng, wi, cw, cb, lg, lb, wo, fg):
        return (ng, wi[None], cw[None], cb, lg, lb, wo[None], fg[0])

    grads = weight_order(g_norm, g_w_in, g_taps, g5[1:2], g5[2:3], g5[3:4], g_w_out, g5[4:5])
    deltas = weight_order(d_vec[0], d_w_in, d_taps, d_vec[1], d_vec[2], d_vec[3], d_w_out, d_vec[4])
    new_m = weight_order(m2_vec[0], m2_w_in, m2_taps, m2_vec[1], m2_vec[2], m2_vec[3], m2_w_out, m2_vec[4])
    new_v = weight_order(v2_vec[0], v2_w_in, v2_taps, v2_vec[1], v2_vec[2], v2_vec[3], v2_w_out, v2_vec[4])
    return (loss8[0, 0], grad_x[None], *grads, *deltas, *new_m, *new_v)
```

```python
import jax
import jax.numpy as jnp
from jax import lax
from jax.experimental import pallas as pl
from jax.experimental.pallas import tpu as pltpu

F32 = jnp.float32
BF16 = jnp.bfloat16

S = 4096
D = 1024
LANES = 128
HD = 64
NKV = 4
GQ = 4
KVW = NKV * HD
NCOL = 5632
CONV_K = 31
HALO = 32
BLK = 128
PATTERNS = (1, 4, 16)
NORM_EPS = 1e-6
LN_EPS = 1e-5
NEG = -1e30
OFF_Q, OFF_K, OFF_V, OFF_AG, OFF_CV, OFF_CG, OFF_CGATE = 0, 1024, 1280, 1536, 2560, 3584, 4608
NCHIP = 4
CHUNK = NCOL // NCHIP
WOUT_ROWS = 2 * D
WOUT_SHARD = WOUT_ROWS // NCHIP
CONVW_SHARD = D // NCHIP

ADAM_LR, ADAM_B1, ADAM_B2, ADAM_EPS, ADAM_WD, ADAM_STEP = 0.001, 0.9, 0.999, 1e-08, 0.01, 10

VMEM_LIMIT = 56 * 1024 * 1024


def _params(sem=None, vmem=VMEM_LIMIT):
    return pltpu.CompilerParams(dimension_semantics=sem, vmem_limit_bytes=vmem)


def _sigmoid(a):
    return 0.5 * jnp.tanh(0.5 * a) + 0.5


def _rows(tm, width):
    return pl.BlockSpec((tm, width), lambda i: (i, 0))


def _slabs(n):
    return jax.ShapeDtypeStruct((n, S, LANES), F32)


def _slab_rows(n, tm):
    return pl.BlockSpec((n, tm, LANES), lambda i: (0, i, 0))


def _resident(shape):
    return pl.BlockSpec(shape, lambda *_: (0,) * len(shape), pipeline_mode=pl.Buffered(1))


def _dot(a, b):
    return jnp.dot(a, b, preferred_element_type=F32)


def _dot_nt(a, b):
    return lax.dot_general(a, b, (((1,), (1,)), ((), ())), preferred_element_type=F32)


def _dot_tn(a, b):
    return lax.dot_general(a, b, (((0,), (0,)), ((), ())), preferred_element_type=F32)


def _inproj_fwd(x, g1, w_bf, wo_full, cw_full):
    tm = 512
    steps = S // tm

    def body(x_ref, g_ref, w_ref, _wo, _cw, h_ref, q_ref, k_ref, v_ref, ag_ref, cv_ref, cg_ref, cgate_ref,
             wo_ref, cw_ref, send, recv):
        i = pl.program_id(0)
        stages = _gather_stages([(W_OUT, wo_ref), (TAPS, cw_ref)], send, recv)
        for stage, step in zip(stages[:3], (0, steps // 2 - 1, steps - 2)):
            pl.when(i == step)(stage)
        xt = x_ref[...]
        r = lax.rsqrt(jnp.mean(xt * xt, axis=-1, keepdims=True) + NORM_EPS)
        h = (xt * r * g_ref[...]).astype(BF16)
        h_ref[...] = h
        q = _dot(h, w_ref[:, OFF_Q:OFF_Q + D]) * (HD ** -0.5)
        kv = _dot(h, w_ref[:, OFF_K:OFF_K + 2 * KVW])
        for sl in range(D // LANES):
            q_ref[sl] = q[:, sl * LANES:(sl + 1) * LANES]
        for sl in range(KVW // LANES):
            k_ref[sl] = kv[:, sl * LANES:(sl + 1) * LANES]
            v_ref[sl] = kv[:, KVW + sl * LANES:KVW + (sl + 1) * LANES]
        ag_ref[...] = _dot(h, w_ref[:, OFF_AG:OFF_AG + D])
        cv_ref[...] = _dot(h, w_ref[:, OFF_CV:OFF_CV + D])
        cg_ref[...] = _dot(h, w_ref[:, OFF_CG:OFF_CG + D])
        cgate_ref[...] = _dot(h, w_ref[:, OFF_CGATE:OFF_CGATE + D])
        pl.when(i == steps - 1)(stages[3])

    big = jax.ShapeDtypeStruct((S, D), F32)
    return pl.pallas_call(
        body, grid=(steps,), name="inproj_fwd",
        in_specs=[_rows(tm, D), _resident((1, D)), _resident((D, NCOL)), ANY, ANY],
        out_specs=[_rows(tm, D), _slab_rows(D // LANES, tm), _slab_rows(KVW // LANES, tm), _slab_rows(KVW // LANES, tm),
                   _rows(tm, D), _rows(tm, D), _rows(tm, D), _rows(tm, D), ANY, ANY],
        out_shape=[jax.ShapeDtypeStruct((S, D), BF16), _slabs(D // LANES), _slabs(KVW // LANES), _slabs(KVW // LANES),
                   big, big, big, big,
                   jax.ShapeDtypeStruct((WOUT_ROWS, D), BF16), jax.ShapeDtypeStruct((HALO, D), F32)],
        input_output_aliases={3: 8, 4: 9},
        scratch_shapes=_gather_sems(2),
        compiler_params=_params(("arbitrary",)),
    )(x, g1, w_bf, wo_full, cw_full)


def _bias_table(d):
    h = jnp.arange(NKV * GQ, dtype=F32)
    slopes = jnp.exp2(-8.0 * (h + 1.0) / (NKV * GQ))
    qi = jnp.arange(BLK)[:, None]
    kj = jnp.arange(2 * BLK)[None, :]
    dist = BLK + qi - kj
    window = (dist >= 0) & (dist <= BLK)
    bias = -slopes[:, None, None] * (dist * d).astype(F32)[None]
    has_prev = jnp.stack([jnp.broadcast_to(kj >= BLK, (BLK, 2 * BLK)), jnp.ones((BLK, 2 * BLK), bool)])
    valid = window[None] & has_prev
    tab = jnp.where(valid[:, None], bias[None], NEG)
    return tab.reshape(2, NKV, GQ * BLK, 2 * BLK)


def _sub_rows(start, d):
    if d == 1:
        return pl.ds(pl.multiple_of(start, BLK), BLK)
    return pl.ds(start, BLK, stride=d)


NHEAD = NKV * GQ
CHUNK_ROWS = 2048
BLOCKS_PER_CHUNK = CHUNK_ROWS // BLK


def _low_lanes(rows=BLK):
    return lax.broadcasted_iota(jnp.int32, (rows, LANES), 1) < HD


def _block_start(idx, d):
    shift = d.bit_length() - 1
    b, r = lax.shift_right_logical(idx, shift), lax.bitwise_and(idx, d - 1)
    start = b * (BLK * d) + r
    return b, start, jnp.maximum(start - BLK * d, r)


def _stack_heads(ref, rows):
    low = _low_lanes()
    t0, t1 = ref[0, rows, :], ref[1, rows, :]
    return jnp.concatenate([jnp.where(low, t0, 0.0), jnp.where(low, 0.0, t0),
                            jnp.where(low, t1, 0.0), jnp.where(low, 0.0, t1)], axis=0).astype(BF16)


def _unstack_heads(dup):
    low = _low_lanes()
    return (jnp.where(low, dup[0:BLK], dup[BLK:2 * BLK]), jnp.where(low, dup[2 * BLK:3 * BLK], dup[3 * BLK:4 * BLK]))


def _kv_dup(ref, prow, rows, odd):
    t = jnp.concatenate([ref[0, prow, :], ref[0, rows, :]], axis=0)
    swapped = pltpu.roll(t, HD, axis=1)
    keep = jnp.logical_xor(_low_lanes(2 * BLK), odd)
    return jnp.where(keep, t, swapped).astype(BF16)


PIECES = 3


def _by_head(tiles):
    lane = lax.broadcasted_iota(jnp.int32, tiles[0].shape, 1)
    out = tiles[0]
    for g in range(1, GQ):
        out = jnp.where(lax.bitwise_and(lane, GQ - 1) == g, tiles[g], out)
    return out


def _minus_in_pieces(x):
    lane = lax.broadcasted_iota(jnp.int32, x.shape, 1)
    hi = (-x).astype(BF16).astype(F32)
    rest = -x - hi
    mid = rest.astype(BF16).astype(F32)
    lo = (rest - mid).astype(BF16).astype(F32)
    return jnp.where(lane < GQ, hi, jnp.where(lane < 2 * GQ, mid, jnp.where(lane < PIECES * GQ, lo, 0.0)))


def _attn_fwd(q, k, v, tables, a_gate):
    tm = 256
    width = GQ * HD

    lane_out = jnp.arange(LANES)[None, :] // HD
    spread_sel = jnp.stack([jnp.arange(LANES)[:, None] == 2 * half + lane_out for half in range(2)]).astype(BF16)

    def body(q_ref, k_ref, v_ref, b1_ref, b2_ref, b3_ref, ag_ref, sel_ref, o_ref, lse_ref, y_ref, op, lp):
        odd = pl.program_id(0) % 2 == 1
        chunk = pl.program_id(1)
        ones = jnp.ones((2 * BLK, LANES), BF16)

        for pat, (d, b_ref) in enumerate(zip(PATTERNS, (b1_ref, b2_ref, b3_ref))):
            def block(idx, carry, pat=pat, d=d, b_ref=b_ref):
                b, start, pstart = _block_start(chunk * BLOCKS_PER_CHUNK + idx, d)
                rows, prow = _sub_rows(start, d), _sub_rows(pstart, d)
                mine = _sub_rows(start - chunk * CHUNK_ROWS, d)
                qs = _stack_heads(q_ref, mine)
                kw = _kv_dup(k_ref, prow, rows, odd)
                vw = _kv_dup(v_ref, prow, rows, odd)
                s = _dot_nt(qs, kw) + b_ref[jnp.minimum(b, 1), 0]
                m = jnp.max(s, axis=1, keepdims=True)
                p = jnp.exp(s - m).astype(BF16)
                ol = _dot(p, jnp.concatenate([vw, ones], axis=1))
                l = ol[:, LANES:]
                op[pat, 0, mine, :], op[pat, 1, mine, :] = _unstack_heads(ol[:, :LANES] / l)
                lp[pat, mine, :] = _by_head([(m + jnp.log(l))[g * BLK:(g + 1) * BLK] for g in range(GQ)])
                return carry

            lax.fori_loop(0, BLOCKS_PER_CHUNK, block, 0, unroll=2)

        def mix(t, carry):
            r = pl.ds(pl.multiple_of(t * tm, tm), tm)
            low = _low_lanes(tm)
            a, b, c = lp[0, r, :], lp[1, r, :], lp[2, r, :]
            m = jnp.maximum(jnp.maximum(a, b), c)
            ea, eb, ec = jnp.exp(a - m), jnp.exp(b - m), jnp.exp(c - m)
            den = ea + eb + ec
            lse_ref[0, r, :] = _minus_in_pieces(m + jnp.log(den))
            inv = 1.0 / den
            for half in range(2):
                def spread(w):
                    hi = w.astype(BF16)
                    lo = (w - hi.astype(F32)).astype(BF16)
                    return _dot(hi, sel_ref[half]) + _dot(lo, sel_ref[half])

                o = (spread(ea * inv) * op[0, half, r, :] + spread(eb * inv) * op[1, half, r, :]
                     + spread(ec * inv) * op[2, half, r, :])
                o_ref[half, r, :] = o
                cols = slice(half * LANES, (half + 1) * LANES)
                ag = ag_ref[r, cols]
                y_ref[r, cols] = (o * (ag * _sigmoid(ag))).astype(BF16)
            return carry

        lax.fori_loop(0, CHUNK_ROWS // tm, mix, 0)

    q_like = pl.BlockSpec((2, CHUNK_ROWS, LANES), lambda j, c: (j, c, 0))
    per_kv = pl.BlockSpec((1, CHUNK_ROWS, LANES), lambda j, c: (j, c, 0))
    kv = pl.BlockSpec((1, S, LANES), lambda j, c: (j // 2, 0, 0))
    bias_spec = pl.BlockSpec((2, 1, GQ * BLK, 2 * BLK), lambda j, c: (0, j, 0, 0))
    group_cols = pl.BlockSpec((CHUNK_ROWS, width), lambda j, c: (c, j))
    return pl.pallas_call(
        body, grid=(NKV, S // CHUNK_ROWS), name="attn_fwd",
        in_specs=[q_like, kv, kv, bias_spec, bias_spec, bias_spec, group_cols,
                  pl.BlockSpec((2, LANES, LANES), lambda j, c: (0, 0, 0))],
        out_specs=[q_like, per_kv, group_cols],
        out_shape=[_slabs(D // LANES), _slabs(NKV), jax.ShapeDtypeStruct((S, D), BF16)],
        scratch_shapes=[pltpu.VMEM((len(PATTERNS), 2, CHUNK_ROWS, LANES), F32),
                        pltpu.VMEM((len(PATTERNS), CHUNK_ROWS, LANES), F32)],
        compiler_params=_params(("arbitrary", "arbitrary")),
    )(q, k, v, *tables, a_gate, spread_sel)


def _head_sum_selectors():
    lane_in = jnp.arange(LANES)[:, None] // HD
    return jnp.stack([jnp.broadcast_to(lane_in == h, (LANES, LANES)) for h in range(2)]).astype(BF16)


def _attn_gate_bwd(dy_att, o, a_gate, selectors, chip_sums):
    tm = 256
    last = S // tm - 1
    landing, sems = _exchange_results_of(chip_sums)
    n_sums = len(chip_sums)

    def body(dy_ref, o_ref, ag_ref, e_ref, *refs):
        sums, (do_ref, dag_ref, delta_ref), refs = refs[:n_sums], refs[n_sums:n_sums + 3], refs[n_sums + 3:]
        landed, (send, recv) = refs[:n_sums], refs[n_sums:]
        i = pl.program_id(0)
        copies = _chip_exchange_copies(sums, landed, send, recv)
        _start_exchange(copies, i == 0)
        for j in range(NKV):
            deltas = []
            for sl in (2 * j, 2 * j + 1):
                cols = slice(sl * LANES, (sl + 1) * LANES)
                dy, ag, o_ = dy_ref[:, cols], ag_ref[:, cols], o_ref[sl]
                sg = _sigmoid(ag)
                do = dy * (ag * sg)
                do_ref[sl] = do
                dag_ref[:, cols] = (dy * o_ * (sg * (1.0 + ag * (1.0 - sg)))).astype(BF16)
                prod = do * o_
                hi = prod.astype(BF16)
                lo = (prod - hi.astype(F32)).astype(BF16)
                deltas += [_dot(hi, e_ref[h]) + _dot(lo, e_ref[h]) for h in range(2)]
            delta_ref[j] = _minus_in_pieces(_by_head(deltas))
        _finish_exchange(copies, i == last)

    return pl.pallas_call(
        body, grid=(S // tm,), name="attn_gate_bwd",
        in_specs=[_rows(tm, D), _slab_rows(D // LANES, tm), _rows(tm, D), _resident((2, LANES, LANES))] + [ANY] * n_sums,
        out_specs=[_slab_rows(D // LANES, tm), _rows(tm, D), _slab_rows(NKV, tm)] + [ANY] * n_sums,
        out_shape=[_slabs(D // LANES), jax.ShapeDtypeStruct((S, D), BF16), _slabs(NKV)] + landing,
        scratch_shapes=sems,
        compiler_params=_params(("arbitrary",)),
    )(dy_att, o, a_gate, selectors, *chip_sums)


def _own_pieces(tile):
    lane = lax.broadcasted_iota(jnp.int32, tile.shape, 1)
    head = jnp.where(lane < PIECES * GQ, lax.bitwise_and(lane, GQ - 1), -1)
    return jnp.concatenate([jnp.where(head == g, tile, 0.0) for g in range(GQ)], axis=0).astype(BF16)


def _attn_bwd(q, k, v, do, lse, delta, bias, d):
    def body(q_ref, do_ref, l_ref, dl_ref, k_ref, v_ref, b_ref, dq_ref, dkv_ref, acc):
        odd = pl.program_id(0) % 2 == 1
        chunk = pl.program_id(1)
        ones = (lax.broadcasted_iota(jnp.int32, (2 * BLK, LANES), 1) < PIECES * GQ).astype(BF16)

        def in_acc(block_idx):
            return pl.ds(pl.multiple_of(block_idx * BLK, BLK), BLK)

        @pl.when(chunk == 0)
        def _():
            acc[...] = jnp.zeros_like(acc)

        def block(idx, carry):
            idx = chunk * BLOCKS_PER_CHUNK + idx
            b, start, pstart = _block_start(idx, d)
            rows, prow = _sub_rows(start, d), _sub_rows(pstart, d)
            mine = _sub_rows(start - chunk * CHUNK_ROWS, d)
            qs = _stack_heads(q_ref, mine)
            dos = _stack_heads(do_ref, mine)
            kw = _kv_dup(k_ref, prow, rows, odd)
            vw = _kv_dup(v_ref, prow, rows, odd)
            s = _dot_nt(jnp.concatenate([qs, _own_pieces(l_ref[0, mine, :])], axis=1),
                        jnp.concatenate([kw, ones], axis=1)) + b_ref[jnp.minimum(b, 1), 0]
            p = jnp.exp(s)
            dv2 = _dot_tn(p.astype(BF16), dos)
            dp = _dot_nt(jnp.concatenate([dos, _own_pieces(dl_ref[0, mine, :])], axis=1),
                         jnp.concatenate([vw, ones], axis=1))
            ds = (p * dp).astype(BF16)
            dq_ref[0, mine, :], dq_ref[1, mine, :] = _unstack_heads(_dot(ds, kw))
            dk2 = _dot_tn(ds, qs)
            dkv = jnp.where(_low_lanes(2 * BLK), dk2 + pltpu.roll(dk2, HD, axis=1), dv2 + pltpu.roll(dv2, HD, axis=1))
            acc[in_acc(idx), :] = acc[in_acc(idx), :] + dkv[BLK:]
            before = jnp.where(b >= 1, idx - d, idx)
            acc[in_acc(before), :] = acc[in_acc(before), :] + dkv[:BLK]
            return carry

        lax.fori_loop(0, BLOCKS_PER_CHUNK, block, 0, unroll=8)

        @pl.when(chunk == S // CHUNK_ROWS - 1)
        def _():
            def place(idx, carry):
                _, start, _ = _block_start(idx, d)
                dkv_ref[0, _sub_rows(start, d), :] = acc[in_acc(idx), :]
                return carry

            lax.fori_loop(0, S // BLK, place, 0, unroll=4)

    q_like = pl.BlockSpec((2, CHUNK_ROWS, LANES), lambda j, c: (j, c, 0))
    pieces = pl.BlockSpec((1, CHUNK_ROWS, LANES), lambda j, c: (j, c, 0))
    kv = pl.BlockSpec((1, S, LANES), lambda j, c: (j // 2, 0, 0))
    per_kv = pl.BlockSpec((1, S, LANES), lambda j, c: (j, 0, 0))
    bias_spec = pl.BlockSpec((2, 1, GQ * BLK, 2 * BLK), lambda j, c: (0, j, 0, 0))
    return pl.pallas_call(
        body, grid=(NKV, S // CHUNK_ROWS), name=f"attn_bwd_d{d}",
        in_specs=[q_like, q_like, pieces, pieces, kv, kv, bias_spec],
        out_specs=[q_like, per_kv],
        out_shape=[_slabs(D // LANES), _slabs(NKV)],
        scratch_shapes=[pltpu.VMEM((S, LANES), F32)],
        compiler_params=_params(("arbitrary", "arbitrary")),
    )(q, do, lse, delta, k, v, bias)


CONV_T = 256


def _halo_before(i):
    return (jnp.maximum(i * (CONV_T // HALO) - 1, 0), 0)


def _halo_after(i):
    return (jnp.minimum((i + 1) * (CONV_T // HALO), S // HALO - 1), 0)


SUBLANES = 8
NCH = D // LANES
GROUP = SUBLANES * SUBLANES


def _comb(ref, cb, base):
    return ref[cb, pl.ds(base, SUBLANES, stride=SUBLANES), :]


def _taps(w_ref, cols):
    return [jnp.broadcast_to(w_ref[j:j + 1, cols], (SUBLANES, LANES)) for j in range(CONV_K)]


def _conv_fwd(c_val, c_glu, c_gate, conv_w, conv_b, ln_g, ln_b):
    T = CONV_T

    def body(cv_ref, cg_ref, cvh_ref, cgh_ref, gate_ref, w_ref, b_ref, lg_ref, lb_ref, u_ref, y_ref, win, us):
        i = pl.program_id(0)
        for cb in range(NCH):
            cols = slice(cb * LANES, (cb + 1) * LANES)
            win[cb, HALO:HALO + T, :] = cv_ref[:, cols] * _sigmoid(cg_ref[:, cols])
            win[cb, 0:HALO, :] = jnp.where(i > 0, cvh_ref[:, cols] * _sigmoid(cgh_ref[:, cols]), 0.0)
        for cb in range(NCH):
            cols = slice(cb * LANES, (cb + 1) * LANES)
            taps = _taps(w_ref, cols)
            bias = jnp.broadcast_to(b_ref[:, cols], (SUBLANES, LANES))

            def group(g, carry):
                for b in range(SUBLANES):
                    base = g * GROUP + b
                    acc = bias
                    for j in range(CONV_K):
                        acc = acc + taps[j] * _comb(win, cb, base + (HALO - (CONV_K - 1) + j))
                    us[cb, pl.ds(base, SUBLANES, stride=SUBLANES), :] = acc
                return carry

            lax.fori_loop(0, T // GROUP, group, 0)
        total = us[0]
        for cb in range(1, NCH):
            total = total + us[cb]
        mu = jnp.sum(total, axis=-1, keepdims=True) * (1.0 / D)
        sq = jnp.zeros((T, LANES), F32)
        for cb in range(NCH):
            uc = us[cb] - mu
            sq = sq + uc * uc
        rstd = lax.rsqrt(jnp.sum(sq, axis=-1, keepdims=True) * (1.0 / D) + LN_EPS)
        for cb in range(NCH):
            cols = slice(cb * LANES, (cb + 1) * LANES)
            u = us[cb]
            u_ref[:, cols] = u
            nrm = (u - mu) * rstd * lg_ref[:, cols] + lb_ref[:, cols]
            gate = gate_ref[:, cols]
            y_ref[:, cols] = (nrm * _sigmoid(nrm) * (gate * _sigmoid(gate))).astype(BF16)

    halo = pl.BlockSpec((HALO, D), _halo_before)
    return pl.pallas_call(
        body, grid=(S // T,), name="conv_fwd",
        in_specs=[_rows(T, D), _rows(T, D), halo, halo, _rows(T, D),
                  _resident((HALO, D)), _resident((1, D)), _resident((1, D)), _resident((1, D))],
        out_specs=[_rows(T, D), _rows(T, D)],
        out_shape=[jax.ShapeDtypeStruct((S, D), F32), jax.ShapeDtypeStruct((S, D), BF16)],
        scratch_shapes=[pltpu.VMEM((NCH, T + HALO, LANES), F32), pltpu.VMEM((NCH, T, LANES), F32)],
        compiler_params=_params(("arbitrary",)),
    )(c_val, c_glu, c_val, c_glu, c_gate, conv_w, conv_b, ln_g, ln_b)


def _conv_bwd_rows(u, c_gate, dy_conv, ln_g, ln_b):
    tm = 256

    def body(u_ref, gate_ref, dy_ref, lg_ref, lb_ref, du_ref, dgate_ref, st_ref):
        @pl.when(pl.program_id(0) == 0)
        def _():
            st_ref[...] = jnp.zeros_like(st_ref)

        u, gate, dy = u_ref[...], gate_ref[...], dy_ref[...]
        mu = jnp.mean(u, axis=-1, keepdims=True)
        uc = u - mu
        rstd = lax.rsqrt(jnp.mean(uc * uc, axis=-1, keepdims=True) + LN_EPS)
        z = uc * rstd
        nrm = z * lg_ref[...] + lb_ref[...]
        sn, sg = _sigmoid(nrm), _sigmoid(gate)
        dgate_ref[...] = (dy * (nrm * sn) * (sg * (1.0 + gate * (1.0 - sg)))).astype(BF16)
        dn = dy * (gate * sg) * (sn * (1.0 + nrm * (1.0 - sn)))
        dz = dn * lg_ref[...]
        du = rstd * (dz - jnp.mean(dz, axis=-1, keepdims=True) - z * jnp.mean(dz * z, axis=-1, keepdims=True))
        du_ref[...] = du
        st_ref[0:1, :] += jnp.sum(dn * z, axis=0, keepdims=True)
        st_ref[1:2, :] += jnp.sum(dn, axis=0, keepdims=True)
        st_ref[2:3, :] += jnp.sum(du, axis=0, keepdims=True)

    big = jax.ShapeDtypeStruct((S, D), F32)
    return pl.pallas_call(
        body, grid=(S // tm,), name="conv_bwd_rows",
        in_specs=[_rows(tm, D)] * 3 + [_resident((1, D)), _resident((1, D))],
        out_specs=[_rows(tm, D), _rows(tm, D), pl.BlockSpec((8, D), lambda i: (0, 0))],
        out_shape=[big, jax.ShapeDtypeStruct((S, D), BF16), jax.ShapeDtypeStruct((8, D), F32)],
        compiler_params=_params(("arbitrary",)),
    )(u, c_gate, dy_conv, ln_g, ln_b)


def _conv_bwd_taps(du, c_val, c_glu, conv_w):
    T = CONV_T
    last = S // T - 1

    def body(du_ref, dua_ref, cv_ref, cg_ref, cvh_ref, cgh_ref, w_ref, dcv_ref, dcg_ref, dw_ref,
             hwin, dwin, dhs, dw_acc):
        i = pl.program_id(0)

        @pl.when(i == 0)
        def _():
            dw_acc[...] = jnp.zeros_like(dw_acc)

        for cb in range(NCH):
            cols = slice(cb * LANES, (cb + 1) * LANES)
            hwin[cb, HALO:HALO + T, :] = cv_ref[:, cols] * _sigmoid(cg_ref[:, cols])
            hwin[cb, 0:HALO, :] = jnp.where(i > 0, cvh_ref[:, cols] * _sigmoid(cgh_ref[:, cols]), 0.0)
            dwin[cb, 0:T, :] = du_ref[:, cols]
            dwin[cb, T:T + HALO, :] = jnp.where(i < last, dua_ref[:, cols], 0.0)
        for cb in range(NCH):
            cols = slice(cb * LANES, (cb + 1) * LANES)
            taps = _taps(w_ref, cols)

            def group_dh(g, carry):
                for b in range(SUBLANES):
                    base = g * GROUP + b
                    acc = jnp.zeros((SUBLANES, LANES), F32)
                    for j in range(CONV_K):
                        acc = acc + taps[j] * _comb(dwin, cb, base + (CONV_K - 1 - j))
                    dhs[cb, pl.ds(base, SUBLANES, stride=SUBLANES), :] = acc
                return carry

            lax.fori_loop(0, T // GROUP, group_dh, 0)

            def group_dw(g, sums):
                for b in range(SUBLANES):
                    base = g * GROUP + b
                    d = _comb(dwin, cb, base)
                    sums = tuple(sums[j] + d * _comb(hwin, cb, base + (HALO - (CONV_K - 1) + j))
                                 for j in range(CONV_K))
                return sums

            sums = lax.fori_loop(0, T // GROUP, group_dw, tuple(dw_acc[j, :, cols] for j in range(CONV_K)))
            for j in range(CONV_K):
                dw_acc[j, :, cols] = sums[j]
            dh = dhs[cb]
            cv, sg = cv_ref[:, cols], _sigmoid(cg_ref[:, cols])
            dcv_ref[:, cols] = (dh * sg).astype(BF16)
            dcg_ref[:, cols] = (dh * cv * (sg * (1.0 - sg))).astype(BF16)

        @pl.when(i == last)
        def _():
            dw_ref[...] = jnp.zeros_like(dw_ref)
            for j in range(CONV_K):
                dw_ref[j:j + 1, :] = jnp.sum(dw_acc[j], axis=0, keepdims=True)

    before = pl.BlockSpec((HALO, D), _halo_before)
    after = pl.BlockSpec((HALO, D), _halo_after)
    big = jax.ShapeDtypeStruct((S, D), BF16)
    return pl.pallas_call(
        body, grid=(S // T,), name="conv_bwd_taps",
        in_specs=[_rows(T, D), after, _rows(T, D), _rows(T, D), before, before, _resident((HALO, D))],
        out_specs=[_rows(T, D), _rows(T, D), pl.BlockSpec((HALO, D), lambda i: (0, 0))],
        out_shape=[big, big, jax.ShapeDtypeStruct((HALO, D), F32)],
        scratch_shapes=[pltpu.VMEM((NCH, T + HALO, LANES), F32), pltpu.VMEM((NCH, T + HALO, LANES), F32),
                        pltpu.VMEM((NCH, T, LANES), F32), pltpu.VMEM((CONV_K, SUBLANES, D), F32)],
        compiler_params=_params(("arbitrary",)),
    )(du, du, c_val, c_glu, c_val, c_glu, conv_w)


def _outproj_loss(y_att, y_conv, w_out_bf, x, target, gf):
    tm = 512

    def body(ya_ref, yc_ref, w_ref, x_ref, t_ref, gf_ref, dx2_ref, dya_ref, dyc_ref, dw_ref, st_ref, acc):
        @pl.when(pl.program_id(0) == 0)
        def _():
            acc[...] = jnp.zeros_like(acc)
            st_ref[...] = jnp.zeros_like(st_ref)

        ya, yc = ya_ref[...], yc_ref[...]
        x2 = x_ref[...] + _dot(ya, w_ref[0:D, :]) + _dot(yc, w_ref[D:2 * D, :])
        r = lax.rsqrt(jnp.mean(x2 * x2, axis=-1, keepdims=True) + NORM_EPS)
        xn = x2 * r
        err = xn * gf_ref[...] - t_ref[...]
        dout = err * (1.0 / D)
        dxn = dout * gf_ref[...]
        dx2 = r * (dxn - xn * jnp.mean(dxn * xn, axis=-1, keepdims=True))
        dx2_ref[...] = dx2
        dx2b = dx2.astype(BF16)
        dya_ref[...] = _dot_nt(dx2b, w_ref[0:D, :])
        dyc_ref[...] = _dot_nt(dx2b, w_ref[D:2 * D, :])
        acc[0:D, :] += _dot_tn(ya, dx2b)
        acc[D:2 * D, :] += _dot_tn(yc, dx2b)
        st_ref[0:1, :] += jnp.sum(dout * xn, axis=0, keepdims=True)
        st_ref[1:2, :] += jnp.sum(err * err, axis=0, keepdims=True) * (0.5 / D)

        @pl.when(pl.program_id(0) == S // tm - 1)
        def _():
            dw_ref[...] = acc[...].astype(BF16)

    big = jax.ShapeDtypeStruct((S, D), F32)
    return pl.pallas_call(
        body, grid=(S // tm,), name="outproj_loss",
        in_specs=[_rows(tm, D), _rows(tm, D), _resident((WOUT_ROWS, D)), _rows(tm, D), _rows(tm, D), _resident((1, D))],
        out_specs=[_rows(tm, D), _rows(tm, D), _rows(tm, D),
                   pl.BlockSpec((WOUT_ROWS, D), lambda i: (0, 0)), pl.BlockSpec((8, D), lambda i: (0, 0))],
        out_shape=[big, big, big, jax.ShapeDtypeStruct((WOUT_ROWS, D), BF16), jax.ShapeDtypeStruct((8, D), F32)],
        scratch_shapes=[pltpu.VMEM((WOUT_ROWS, D), F32)],
        compiler_params=_params(("arbitrary",)),
    )(y_att, y_conv, w_out_bf, x, target, gf)


UNITS_PER_CHUNK = CHUNK // LANES


def _dproj_unit(u, dqs, dkvs, gates, rows):
    if u < OFF_K // LANES:
        return ((dqs[0][u] + dqs[1][u] + dqs[2][u]) * (HD ** -0.5)).astype(BF16)
    if u < OFF_AG // LANES:
        w = u - OFF_K // LANES
        ta, tb = (dkvs[0][j] + dkvs[1][j] + dkvs[2][j] for j in (2 * (w % 2), 2 * (w % 2) + 1))
        low = _low_lanes(rows)
        if w < 2:
            return jnp.where(low, ta, pltpu.roll(tb, HD, axis=1)).astype(BF16)
        return jnp.where(low, pltpu.roll(ta, HD, axis=1), tb).astype(BF16)
    g, sl = divmod(u - OFF_AG // LANES, D // LANES)
    return gates[g][:, sl * LANES:(sl + 1) * LANES]


def _dproj_sources(units, dqs, dkvs, gates, rows):
    use_q = any(u < OFF_K // LANES for u in units)
    use_kv = any(OFF_K // LANES <= u < OFF_AG // LANES for u in units)
    use_g = sorted({(u - OFF_AG // LANES) // (D // LANES) for u in units if u >= OFF_AG // LANES})
    args = (list(dqs) if use_q else []) + (list(dkvs) if use_kv else []) + [gates[g] for g in use_g]
    specs = ([_slab_rows(D // LANES, rows)] * 3 if use_q else []) + ([_slab_rows(NKV, rows)] * 3 if use_kv else []) \
        + [_rows(rows, D)] * len(use_g)

    def pick(refs):
        refs = list(refs)
        q_refs = [refs.pop(0) for _ in range(3)] if use_q else None
        kv_refs = [refs.pop(0) for _ in range(3)] if use_kv else None
        return q_refs, kv_refs, {g: refs.pop(0) for g in use_g}

    return args, specs, pick


def _exchange_results_of(chip_sums):
    n = len(chip_sums) * len(CHIP_FLIPS)
    shapes = [jax.ShapeDtypeStruct((NCHIP,) + tuple(a.shape[1:] if a.ndim == 3 else a.shape), a.dtype)
              for a in chip_sums]
    return shapes, [pltpu.SemaphoreType.DMA((n,)), pltpu.SemaphoreType.DMA((n,))]


def _start_exchange(copies, first_step):
    @pl.when(first_step)
    def _():
        for out, _ in copies:
            out.start()


def _finish_exchange(copies, last_step):
    @pl.when(last_step)
    def _():
        for _, arrival in copies:
            arrival.wait_recv()
        for out, _ in copies:
            out.wait_send()


def _inproj_bwd_x(dqs, dkvs, gates, w_bf, x, g1, dx2, pi):
    tm = 256
    last = S // tm - 1
    units = range(NCOL // LANES)
    pieces, piece_specs, pick = _dproj_sources(units, dqs, dkvs, gates, tm)
    landing, sems = _exchange_results_of([pi])

    def body(*refs):
        piece_refs, refs = refs[:len(pieces)], refs[len(pieces):]
        w_ref, x_ref, g_ref, dx2_ref, pi_ref, gx_ref, st_ref, ri_ref, dp_ref, send, recv = refs
        i = pl.program_id(0)
        copies = _chip_exchange_copies([pi_ref], [ri_ref], send, recv)
        _start_exchange(copies, i == 0)

        @pl.when(i == 0)
        def _():
            st_ref[...] = jnp.zeros_like(st_ref)

        sources = pick(piece_refs)
        for u in units:
            dp_ref[:, u * LANES:(u + 1) * LANES] = _dproj_unit(u, *sources, tm)
        dh = _dot_nt(dp_ref[...], w_ref[...])
        xt = x_ref[...]
        r = lax.rsqrt(jnp.mean(xt * xt, axis=-1, keepdims=True) + NORM_EPS)
        xn = xt * r
        dxn = dh * g_ref[...]
        gx_ref[...] = dx2_ref[...] + r * (dxn - xn * jnp.mean(dxn * xn, axis=-1, keepdims=True))
        st_ref[0:1, :] += jnp.sum(dh * xn, axis=0, keepdims=True)
        _finish_exchange(copies, i == last)

    return pl.pallas_call(
        body, grid=(S // tm,), name="inproj_bwd_x",
        in_specs=piece_specs + [_resident((D, NCOL)), _rows(tm, D), _resident((1, D)), _rows(tm, D), ANY],
        out_specs=[_rows(tm, D), pl.BlockSpec((8, D), lambda i: (0, 0)), ANY],
        out_shape=[jax.ShapeDtypeStruct((S, D), F32), jax.ShapeDtypeStruct((8, D), F32)] + landing,
        scratch_shapes=[pltpu.VMEM((tm, NCOL), BF16)] + sems,
        compiler_params=_params(("arbitrary",)),
    )(*pieces, w_bf, x, g1, dx2, pi)


def _inproj_bwd_w(h, dqs, dkvs, gates):
    out = None
    for k in range(NCHIP):
        units = range(k * UNITS_PER_CHUNK, (k + 1) * UNITS_PER_CHUNK)
        tk = 512 if units[0] < OFF_K // LANES else 1024
        nk = S // tk
        pieces, piece_specs, pick = _dproj_sources(units, dqs, dkvs, gates, tk)
        handed_on = [] if out is None else [out]

        def body(*refs, units=units, pick=pick, n_pieces=len(pieces), n_in=1 + len(pieces) + len(handed_on)):
            h_ref, piece_refs = refs[0], refs[1:1 + n_pieces]
            o_ref, tile, acc = refs[n_in:]
            i = pl.program_id(0)

            @pl.when(i == 0)
            def _():
                acc[...] = jnp.zeros_like(acc)

            sources = pick(piece_refs)
            for n, u in enumerate(units):
                tile[:, n * LANES:(n + 1) * LANES] = _dproj_unit(u, *sources, tk)
            acc[...] += _dot_tn(h_ref[...], tile[...])

            @pl.when(i == nk - 1)
            def _():
                o_ref[0] = acc[...].astype(BF16)

        out = pl.pallas_call(
            body, grid=(nk,), name=f"inproj_bwd_w{k}",
            in_specs=[_rows(tk, D)] + piece_specs + [ANY] * len(handed_on),
            out_specs=pl.BlockSpec((1, D, CHUNK), lambda i, k=k: (k, 0, 0)),
            out_shape=jax.ShapeDtypeStruct((NCHIP, D, CHUNK), BF16),
            input_output_aliases={1 + len(pieces): 0} if handed_on else {},
            scratch_shapes=[pltpu.VMEM((tk, CHUNK), BF16), pltpu.VMEM((D, CHUNK), F32)],
            compiler_params=_params(("arbitrary",)),
        )(h, *pieces, *handed_on)
    return out


ROW_LN_G, ROW_LN_B, ROW_CONV_B, ROW_FINAL_G, ROW_LOSS, ROW_TAPS = 0, 1, 2, 8, 9, 16
SMALL_ROWS = 16 + HALO
NDEV = 8


MESH = pl.DeviceIdType.MESH
ANY = pl.BlockSpec(memory_space=pl.ANY)
CHIP_FLIPS = ((1, 0), (0, 1), (1, 1))


def _pos():
    return lax.axis_index("x"), lax.axis_index("y"), lax.axis_index("c")


def _flip(v, f):
    return 1 - v if f else v


def _ds(start, size, align=None):
    return pl.ds(pl.multiple_of(start, align or size), size)


def _place_shards(wi, wo, cw, where):
    steps = 4

    def body(where_ref, wi_ref, wo_ref, cw_ref, wi_full, wo_full, cw_full):
        wi_full[...] = wi_ref[...].astype(BF16)
        wo_full[...] = wo_ref[...].astype(BF16)
        cw_full[...] = cw_ref[...]

    grid_spec = pltpu.PrefetchScalarGridSpec(
        num_scalar_prefetch=1, grid=(steps,),
        in_specs=[pl.BlockSpec((D // steps, CHUNK), lambda i, w: (i, 0)),
                  pl.BlockSpec((WOUT_SHARD // steps, D), lambda i, w: (i, 0)),
                  pl.BlockSpec((HALO, CONVW_SHARD), lambda i, w: (0, 0))],
        out_specs=[pl.BlockSpec((D // steps, CHUNK), lambda i, w: (i, w[0])),
                   pl.BlockSpec((WOUT_SHARD // steps, D), lambda i, w: (w[0] * steps + i, 0)),
                   pl.BlockSpec((HALO, CONVW_SHARD), lambda i, w: (0, w[0]))])
    return pl.pallas_call(
        body, grid_spec=grid_spec, name="place_shards",
        out_shape=[jax.ShapeDtypeStruct((D, NCOL), BF16), jax.ShapeDtypeStruct((WOUT_ROWS, D), BF16),
                   jax.ShapeDtypeStruct((HALO, D), F32)],
        compiler_params=_params(("arbitrary",)),
    )(where, wi, wo, cw)


W_IN, W_OUT, TAPS = range(3)
GATHER_SEMS = 8


def _gather_stages(fulls, send, recv):
    halves = {W_IN: D // 2, W_OUT: WOUT_SHARD // 2, TAPS: HALO // 2}
    OWN_X, OWN_Y, VIA_Y, VIA_X = range(4)
    x, y, c = _pos()
    x_nbr, y_nbr, diag = (1 - x, y), (x, 1 - y), (1 - x, 1 - y)

    def region(n_th, chip_xy, half, part=None):
        a, full = fulls[n_th]
        chip = 2 * chip_xy[0] + chip_xy[1]
        n, row = halves[a], half * halves[a]
        if part is not None:
            n = n // 2
            row = row + part * n
        if a == W_IN:
            return full.at[_ds(row, n), _ds(chip * CHUNK, CHUNK, 128)]
        if a == W_OUT:
            return full.at[_ds(chip * WOUT_SHARD + row, n), :]
        return full.at[_ds(row, n), _ds(chip * CONVW_SHARD, CONVW_SHARD, 128)]

    def copy(n_th, kind, piece, dev):
        k = GATHER_SEMS * n_th + kind
        return pltpu.make_async_remote_copy(src_ref=piece, dst_ref=piece, send_sem=send.at[k], recv_sem=recv.at[k],
                                            device_id=dev, device_id_type=MESH)

    def sent(a):
        sib = (x, y, 1 - c)
        return {
            OWN_X: copy(a, OWN_X, region(a, (x, y), c), (*x_nbr, c)),
            OWN_Y: copy(a, OWN_Y, region(a, (x, y), c), (*y_nbr, c)),
            VIA_Y: copy(a, VIA_Y, region(a, x_nbr, c, 0), (*y_nbr, c)),
            VIA_X: copy(a, VIA_X, region(a, y_nbr, c, 1), (*x_nbr, c)),
            4 + OWN_X: copy(a, 4 + OWN_X, region(a, x_nbr, c), sib),
            4 + OWN_Y: copy(a, 4 + OWN_Y, region(a, y_nbr, c), sib),
            4 + VIA_Y: copy(a, 4 + VIA_Y, region(a, diag, c, 0), sib),
            4 + VIA_X: copy(a, 4 + VIA_X, region(a, diag, c, 1), sib),
        }

    def arrival(a, kind, piece, dev):
        copy(a, kind, piece, dev).wait_recv()

    arrays = range(len(fulls))

    def own_to_neighbours():
        for a in arrays:
            mine = sent(a)
            mine[OWN_X].start()
            mine[OWN_Y].start()

    def pass_on_neighbours():
        for a in arrays:
            mine = sent(a)
            arrival(a, OWN_X, region(a, x_nbr, c), (*x_nbr, c))
            mine[VIA_Y].start()
            mine[4 + OWN_X].start()
            arrival(a, OWN_Y, region(a, y_nbr, c), (*y_nbr, c))
            mine[VIA_X].start()
            mine[4 + OWN_Y].start()

    def pass_on_diagonal():
        for a in arrays:
            mine = sent(a)
            arrival(a, VIA_Y, region(a, diag, c, 0), (*y_nbr, c))
            mine[4 + VIA_Y].start()
            arrival(a, VIA_X, region(a, diag, c, 1), (*x_nbr, c))
            mine[4 + VIA_X].start()

    def finish():
        for a in arrays:
            for kind, piece in ((OWN_X, region(a, x_nbr, 1 - c)), (OWN_Y, region(a, y_nbr, 1 - c)),
                                (VIA_Y, region(a, diag, 1 - c, 0)), (VIA_X, region(a, diag, 1 - c, 1))):
                arrival(a, 4 + kind, piece, (x, y, 1 - c))
            for cp in sent(a).values():
                cp.wait_send()

    return own_to_neighbours, pass_on_neighbours, pass_on_diagonal, finish


def _gather_sems(n_arrays):
    return [pltpu.SemaphoreType.DMA((GATHER_SEMS * n_arrays,)), pltpu.SemaphoreType.DMA((GATHER_SEMS * n_arrays,))]


def _gather_w_in(wi_full):
    def body(_wi, full, send, recv):
        for stage in _gather_stages([(W_IN, full)], send, recv):
            stage()

    return pl.pallas_call(
        body, name="gather_w_in", in_specs=[ANY], out_specs=ANY, input_output_aliases={0: 0},
        out_shape=jax.ShapeDtypeStruct((D, NCOL), BF16), scratch_shapes=_gather_sems(1),
    )(wi_full)


def _half_shape(a):
    return jax.ShapeDtypeStruct((NCHIP, a.shape[1] // 2, a.shape[2]) if a.ndim == 3 else a.shape, a.dtype)


def _exchange_halves(arrays, name):
    n = len(arrays)

    def body(*refs):
        srcs, dsts, (send, recv) = refs[:n], refs[n:2 * n], refs[2 * n:]
        x, y, c = _pos()
        cps = []
        for k, (s_, d_) in enumerate(zip(srcs, dsts)):
            if len(s_.shape) == 3:
                h = s_.shape[1] // 2
                s_ = s_.at[:, _ds((1 - c) * h, h), :]
            cps.append(pltpu.make_async_remote_copy(src_ref=s_, dst_ref=d_, send_sem=send.at[k], recv_sem=recv.at[k],
                                                    device_id=(x, y, 1 - c), device_id_type=MESH))
        for cp in cps:
            cp.start()
        for cp in cps:
            cp.wait()

    return pl.pallas_call(
        body, name=name, in_specs=[ANY] * n, out_specs=[ANY] * n, out_shape=[_half_shape(a) for a in arrays],
        scratch_shapes=[pltpu.SemaphoreType.DMA((n,)), pltpu.SemaphoreType.DMA((n,))],
    )(*arrays)


def _add_halves(arrays, received, name):
    n = len(arrays)

    def body(*refs):
        mine, theirs, outs = refs[:n], refs[n:2 * n], refs[2 * n:]
        c = lax.axis_index("c")
        for m_, t_, o_ in zip(mine, theirs, outs):
            if len(m_.shape) == 3:
                h = m_.shape[1] // 2
                o_[0] = (m_[0, _ds(c * h, h), :].astype(F32) + t_[0].astype(F32)).astype(o_.dtype)
            else:
                o_[...] = m_[...] + t_[...]

    def spec(shape):
        if len(shape) == 3:
            return pl.BlockSpec((1,) + tuple(shape[1:]), lambda k: (k, 0, 0))
        return pl.BlockSpec(tuple(shape), lambda k: (0, 0))

    halves = [_half_shape(a) for a in arrays]
    return pl.pallas_call(
        body, grid=(NCHIP,), name=name,
        in_specs=[spec(a.shape) for a in arrays] + [spec(h.shape) for h in halves],
        out_specs=[spec(h.shape) for h in halves], out_shape=halves,
        compiler_params=_params(("arbitrary",)),
    )(*arrays, *received)


def _chip_exchange_copies(srcs, dsts, send, recv):
    x, y, c = _pos()
    me = 2 * x + y
    pairs = []
    for a in range(len(srcs)):
        for j, (fx, fy) in enumerate(CHIP_FLIPS):
            px, py = _flip(x, fx), _flip(y, fy)
            peer = 2 * px + py
            k = len(CHIP_FLIPS) * a + j
            out = pltpu.make_async_remote_copy(
                src_ref=srcs[a].at[peer] if len(srcs[a].shape) == 3 else srcs[a], dst_ref=dsts[a].at[me],
                send_sem=send.at[k], recv_sem=recv.at[k], device_id=(px, py, c), device_id_type=MESH)
            got = dsts[a].at[peer]
            arrival = pltpu.make_async_remote_copy(
                src_ref=got, dst_ref=got, send_sem=send.at[k], recv_sem=recv.at[k],
                device_id=(px, py, c), device_id_type=MESH)
            pairs.append((out, arrival))
    return pairs


def _sum_chips(ri, ro, rs, pi, po, ps, where):
    def body(w_ref, ri_ref, ro_ref, rs_ref, pi_ref, po_ref, ps_ref, gi_ref, go_ref, gs_ref, g5_ref, loss_ref,
             acc_i, acc_o, acc_s):
        k = pl.program_id(0)
        accs = (acc_i, acc_o, acc_s)

        @pl.when(k == 0)
        def _():
            for acc in accs:
                acc[...] = jnp.zeros_like(acc)

        @pl.when(k == w_ref[0])
        def _():
            for acc, val in zip(accs, (pi_ref[0], po_ref[0], ps_ref[...])):
                acc[...] += val.astype(F32)

        @pl.when(k != w_ref[0])
        def _():
            for acc, ref in zip(accs, (ri_ref, ro_ref, rs_ref)):
                acc[...] += ref[0].astype(F32)

        @pl.when(k == NCHIP - 1)
        def _():
            gi_ref[0] = acc_i[...]
            go_ref[0] = acc_o[...]
            gs_ref[...] = acc_s[...]
            g5_ref[...] = jnp.zeros_like(g5_ref)
            for i, row in enumerate((ROW_CONV_B, ROW_LN_G, ROW_LN_B, ROW_FINAL_G)):
                g5_ref[i + 1:i + 2, :] = acc_s[row:row + 1, :]
            loss = jnp.sum(acc_s[ROW_LOSS:ROW_LOSS + 1, :], axis=1, keepdims=True)
            loss_ref[...] = jnp.broadcast_to(loss, loss_ref.shape)

    def sent(k, w):
        return jnp.where(k == w[0], (k + 1) % NCHIP, k)

    hi, ho = D // 2, WOUT_SHARD // 2
    const = lambda shape: pl.BlockSpec(shape, lambda k, w: (0,) * len(shape))
    grid_spec = pltpu.PrefetchScalarGridSpec(
        num_scalar_prefetch=1, grid=(NCHIP,),
        in_specs=[pl.BlockSpec((1, hi, CHUNK), lambda k, w: (sent(k, w), 0, 0)),
                  pl.BlockSpec((1, ho, D), lambda k, w: (sent(k, w), 0, 0)),
                  pl.BlockSpec((1, SMALL_ROWS, D), lambda k, w: (sent(k, w), 0, 0)),
                  pl.BlockSpec((1, hi, CHUNK), lambda k, w: (w[0], 0, 0)),
                  pl.BlockSpec((1, ho, D), lambda k, w: (w[0], 0, 0)),
                  const((SMALL_ROWS, D))],
        out_specs=[pl.BlockSpec((1, hi, CHUNK), lambda k, w: (w[1], 0, 0)),
                   pl.BlockSpec((1, ho, D), lambda k, w: (w[1], 0, 0)),
                   const((SMALL_ROWS, D)), const((8, D)), const((8, LANES))],
        scratch_shapes=[pltpu.VMEM((hi, CHUNK), F32), pltpu.VMEM((ho, D), F32), pltpu.VMEM((SMALL_ROWS, D), F32)])
    return pl.pallas_call(
        body, grid_spec=grid_spec, name="sum_chips",
        out_shape=[jax.ShapeDtypeStruct((2, hi, CHUNK), F32), jax.ShapeDtypeStruct((2, ho, D), F32),
                   jax.ShapeDtypeStruct((SMALL_ROWS, D), F32), jax.ShapeDtypeStruct((8, D), F32),
                   jax.ShapeDtypeStruct((8, LANES), F32)],
        compiler_params=_params(("arbitrary",)),
    )(where, ri, ro, rs, pi, po, ps)


def _exchange_results(gi2, go2, st):
    flips = [(fx, fy, fc) for fx in (0, 1) for fy in (0, 1) for fc in (0, 1)][1:]

    def body(_gi, _go, st_ref, gi_ref, go_ref, all_ref, send, recv, lsem):
        x, y, c = _pos()
        sib = (x, y, 1 - c)

        def half(k, ref, slot):
            return pltpu.make_async_remote_copy(src_ref=ref.at[slot], dst_ref=ref.at[slot], send_sem=send.at[k],
                                                recv_sem=recv.at[k], device_id=sib, device_id_type=MESH)

        def stat(k, src, slot, dev):
            return pltpu.make_async_remote_copy(src_ref=src, dst_ref=all_ref.at[slot], send_sem=send.at[k],
                                                recv_sem=recv.at[k], device_id=dev, device_id_type=MESH)

        mine = pltpu.make_async_copy(st_ref, all_ref.at[4 * x + 2 * y + c], lsem)
        mine.start()
        sends = [half(k, ref, c) for k, ref in enumerate((gi_ref, go_ref))]
        peers = [(_flip(x, fx), _flip(y, fy), _flip(c, fc)) for fx, fy, fc in flips]
        sends += [stat(2 + k, st_ref, 4 * x + 2 * y + c, dev) for k, dev in enumerate(peers)]
        for cp in sends:
            cp.start()
        for k, ref in enumerate((gi_ref, go_ref)):
            half(k, ref, 1 - c).wait_recv()
        for k, (px, py, pc) in enumerate(peers):
            slot = 4 * px + 2 * py + pc
            stat(2 + k, all_ref.at[slot], slot, (px, py, pc)).wait_recv()
        for cp in sends:
            cp.wait_send()
        mine.wait()

    n = 2 + len(flips)
    return pl.pallas_call(
        body, name="exchange_results",
        in_specs=[ANY, ANY, ANY], out_specs=[ANY, ANY, ANY], input_output_aliases={0: 0, 1: 1},
        out_shape=[jax.ShapeDtypeStruct((2, D // 2, CHUNK), F32), jax.ShapeDtypeStruct((2, WOUT_SHARD // 2, D), F32),
                   jax.ShapeDtypeStruct((NDEV, 8, D), F32)],
        scratch_shapes=[pltpu.SemaphoreType.DMA((n,)), pltpu.SemaphoreType.DMA((n,)), pltpu.SemaphoreType.DMA],
    )(gi2, go2, st)


def _adamw_math(w, g, m, v):
    m2 = ADAM_B1 * m + (1.0 - ADAM_B1) * g
    v2 = ADAM_B2 * v + (1.0 - ADAM_B2) * (g * g)
    m_hat = m2 / (1.0 - ADAM_B1 ** ADAM_STEP)
    v_hat = v2 / (1.0 - ADAM_B2 ** ADAM_STEP)
    delta = -ADAM_LR * (m_hat / (jnp.sqrt(v_hat) + ADAM_EPS) + ADAM_WD * w)
    return delta, m2, v2


def _adamw(w, g, m, v, name):
    rows, cols = w.shape
    tm = 256 if rows % 256 == 0 else rows

    def body(w_ref, g_ref, m_ref, v_ref, d_ref, m2_ref, v2_ref):
        d_ref[...], m2_ref[...], v2_ref[...] = _adamw_math(w_ref[...], g_ref[...], m_ref[...], v_ref[...])

    shape = jax.ShapeDtypeStruct(w.shape, F32)
    return pl.pallas_call(
        body, grid=(rows // tm,), name=name,
        in_specs=[_rows(tm, cols)] * 4, out_specs=[_rows(tm, cols)] * 3, out_shape=[shape] * 3,
        compiler_params=_params(("arbitrary",)),
    )(w, g, m, v)


def _adamw_vectors(g5, first_parts, ws, ms, vs):
    n = len(ws)

    def body(g_ref, parts_ref, *refs):
        ins, g0_ref, outs = refs[:3 * n], refs[3 * n], refs[3 * n + 1:]
        g0 = parts_ref[0, 0:1, :]
        for dev in range(1, NDEV):
            g0 = g0 + parts_ref[dev, 0:1, :]
        g0_ref[...] = g0
        for i in range(n):
            g = g0 if i == 0 else g_ref[i:i + 1, :]
            res = _adamw_math(ins[i][...], g, ins[n + i][...], ins[2 * n + i][...])
            for kind in range(3):
                outs[kind * n + i][...] = res[kind]

    shape = jax.ShapeDtypeStruct((1, D), F32)
    return pl.pallas_call(body, name="adamw_vectors", out_shape=[shape] * (1 + 3 * n), compiler_params=_params())(
        g5, first_parts, *ws, *ms, *vs)


def kernel(x, norm_g, w_in, conv_w, conv_b, conv_ln_g, conv_ln_b, w_out, final_norm_g, loss_target, m_norm_g, m_w_in, m_conv_w, m_conv_b, m_conv_ln_g, m_conv_ln_b, m_w_out, m_final_norm_g, v_norm_g, v_w_in, v_conv_w, v_conv_b, v_conv_ln_g, v_conv_ln_b, v_w_out, v_final_norm_g):
    chip = 2 * lax.axis_index("x") + lax.axis_index("y")
    where = jnp.stack([chip, lax.axis_index("c")]).astype(jnp.int32)
    taps_shard = jnp.pad(conv_w[0], ((0, HALO - CONV_K), (0, 0)))
    wi_full, wo_full, cw_full = _place_shards(w_in[0], w_out[0], taps_shard, where)
    wi_full = _gather_w_in(wi_full)

    gf = final_norm_g[None]
    xb = x[0]
    h, q, k, v, a_gate, c_val, c_glu, c_gate, wo_full, cw_full = _inproj_fwd(xb, norm_g, wi_full, wo_full, cw_full)
    tables = [_bias_table(d) for d in PATTERNS]
    o, lse, y_att = _attn_fwd(q, k, v, tables, a_gate)
    u, y_conv = _conv_fwd(c_val, c_glu, c_gate, cw_full, conv_b, conv_ln_g, conv_ln_b)
    dx2, dy_att, dy_conv, dw_out, st_out = _outproj_loss(y_att, y_conv, wo_full, xb, loss_target[0], gf)
    du, dc_gate, st_conv = _conv_bwd_rows(u, c_gate, dy_conv, conv_ln_g, conv_ln_b)
    dc_val, dc_glu, dconv_w = _conv_bwd_taps(du, c_val, c_glu, cw_full)

    early = [dw_out.reshape(NCHIP, WOUT_SHARD, D), jnp.concatenate([st_conv, st_out, dconv_w], axis=0)]
    po, ps = _add_halves(early, _exchange_halves(early, "exchange_halves_early"), "add_halves_early")
    do, da_gate, delta, ro, rs = _attn_gate_bwd(dy_att, o, a_gate, _head_sum_selectors(), [po, ps])
    dqs, dkvs = zip(*[_attn_bwd(q, k, v, do, lse, delta, t, d) for t, d in zip(tables, PATTERNS)])

    dproj_pieces = (dqs, dkvs, (da_gate, dc_val, dc_glu, dc_gate))
    late = [_inproj_bwd_w(h, *dproj_pieces)]
    (pi,) = _add_halves(late, _exchange_halves(late, "exchange_halves"), "add_halves")
    grad_x, st_in, ri = _inproj_bwd_x(*dproj_pieces, wi_full, xb, norm_g, dx2, pi)
    gi2, go2, g_small, g5, loss8 = _sum_chips(ri, ro, rs, pi, po, ps, where)
    gi2, go2, norm_g_parts = _exchange_results(gi2, go2, st_in)
    g_w_in = gi2.reshape(D, CHUNK)
    g_w_out = go2.reshape(WOUT_SHARD, D)
    g_taps = lax.dynamic_slice(g_small, (ROW_TAPS, chip * CONVW_SHARD), (CONV_K, CONVW_SHARD))

    d_w_in, m2_w_in, v2_w_in = _adamw(w_in[0], g_w_in, m_w_in[0], v_w_in[0], "adamw_w_in")
    d_w_out, m2_w_out, v2_w_out = _adamw(w_out[0], g_w_out, m_w_out[0], v_w_out[0], "adamw_w_out")
    d_taps, m2_taps, v2_taps = _adamw(conv_w[0], g_taps, m_conv_w[0], v_conv_w[0], "adamw_conv_w")
    g_norm, *vec = _adamw_vectors(
        g5, norm_g_parts,
        (norm_g, conv_b, conv_ln_g, conv_ln_b, gf),
        (m_norm_g, m_conv_b, m_conv_ln_g, m_conv_ln_b, m_final_norm_g[None]),
        (v_norm_g, v_conv_b, v_conv_ln_g, v_conv_ln_b, v_final_norm_g[None]))
    d_vec, m2_vec, v2_vec = vec[0:5], vec[5:10], vec[10:15]

    def weight_order(ng, wi, cw, cb, lg, lb, wo, fg):
        return (ng, wi[None], cw[None], cb, lg, lb, wo[None], fg[0])

    grads = weight_order(g_norm, g_w_in, g_taps, g5[1:2], g5[2:3], g5[3:4], g_w_out, g5[4:5])
    deltas = weight_order(d_vec[0], d_w_in, d_taps, d_vec[1], d_vec[2], d_vec[3], d_w_out, d_vec[4])
    new_m = weight_order(m2_vec[0], m2_w_in, m2_taps, m2_vec[1], m2_vec[2], m2_vec[3], m2_w_out, m2_vec[4])
    new_v = weight_order(v2_vec[0], v2_w_in, v2_taps, v2_vec[1], v2_vec[2], v2_vec[3], v2_w_out, v2_vec[4])
    return (loss8[0, 0], grad_x[None], *grads, *deltas, *new_m, *new_v)
```

```python
import jax
import jax.numpy as jnp
from jax import lax
from jax.experimental import pallas as pl
from jax.experimental.pallas import tpu as pltpu

F32 = jnp.float32
BF16 = jnp.bfloat16

S = 4096
D = 1024
LANES = 128
HD = 64
NKV = 4
GQ = 4
KVW = NKV * HD
NCOL = 5632
CONV_K = 31
HALO = 32
BLK = 128
PATTERNS = (1, 4, 16)
NORM_EPS = 1e-6
LN_EPS = 1e-5
NEG = -1e30
OFF_Q, OFF_K, OFF_AG, OFF_CV, OFF_CG, OFF_CGATE = 0, 1024, 1536, 2560, 3584, 4608
NCHIP = 4
CHUNK = NCOL // NCHIP
WOUT_ROWS = 2 * D
WOUT_SHARD = WOUT_ROWS // NCHIP
CONVW_SHARD = D // NCHIP

ADAM_LR, ADAM_B1, ADAM_B2, ADAM_EPS, ADAM_WD, ADAM_STEP = 0.001, 0.9, 0.999, 1e-08, 0.01, 10

VMEM_LIMIT = 56 * 1024 * 1024


def _params(sem=None, vmem=VMEM_LIMIT):
    return pltpu.CompilerParams(dimension_semantics=sem, vmem_limit_bytes=vmem)


def _sigmoid(a):
    return 0.5 * jnp.tanh(0.5 * a) + 0.5


def _rows(tm, width):
    return pl.BlockSpec((tm, width), lambda i: (i, 0))


def _slabs(n):
    return jax.ShapeDtypeStruct((n, S, LANES), F32)


def _slab_rows(n, tm):
    return pl.BlockSpec((n, tm, LANES), lambda i: (0, i, 0))


def _resident(shape):
    return pl.BlockSpec(shape, lambda *_: (0,) * len(shape), pipeline_mode=pl.Buffered(1))


def _dot(a, b):
    return jnp.dot(a, b, preferred_element_type=F32)


def _dot_nt(a, b):
    return lax.dot_general(a, b, (((1,), (1,)), ((), ())), preferred_element_type=F32)


def _dot_tn(a, b):
    return lax.dot_general(a, b, (((0,), (0,)), ((), ())), preferred_element_type=F32)


def _inproj_fwd(x, g1, w_bf, wo_full, cw_full):
    tm = 512
    steps = S // tm

    def body(x_ref, g_ref, w_ref, _wo, _cw, h_ref, q_ref, k_ref, v_ref, ag_ref, cv_ref, cg_ref, cgate_ref,
             wo_ref, cw_ref, send, recv):
        i = pl.program_id(0)
        stages = _gather_stages([(W_OUT, wo_ref), (TAPS, cw_ref)], send, recv)
        for stage, step in zip(stages[:3], (0, steps // 2 - 1, steps - 2)):
            pl.when(i == step)(stage)
        xt = x_ref[...]
        r = lax.rsqrt(jnp.mean(xt * xt, axis=-1, keepdims=True) + NORM_EPS)
        h = (xt * r * g_ref[...]).astype(BF16)
        h_ref[...] = h
        q = _dot(h, w_ref[:, OFF_Q:OFF_Q + D]) * (HD ** -0.5)
        kv = _dot(h, w_ref[:, OFF_K:OFF_K + 2 * KVW])
        for sl in range(D // LANES):
            q_ref[sl] = q[:, sl * LANES:(sl + 1) * LANES]
        for sl in range(KVW // LANES):
            k_ref[sl] = kv[:, sl * LANES:(sl + 1) * LANES]
            v_ref[sl] = kv[:, KVW + sl * LANES:KVW + (sl + 1) * LANES]
        ag_ref[...] = _dot(h, w_ref[:, OFF_AG:OFF_AG + D])
        cv_ref[...] = _dot(h, w_ref[:, OFF_CV:OFF_CV + D])
        cg_ref[...] = _dot(h, w_ref[:, OFF_CG:OFF_CG + D])
        cgate_ref[...] = _dot(h, w_ref[:, OFF_CGATE:OFF_CGATE + D])
        pl.when(i == steps - 1)(stages[3])

    big = jax.ShapeDtypeStruct((S, D), F32)
    return pl.pallas_call(
        body, grid=(steps,), name="inproj_fwd",
        in_specs=[_rows(tm, D), _resident((1, D)), _resident((D, NCOL)), ANY, ANY],
        out_specs=[_rows(tm, D), _slab_rows(D // LANES, tm), _slab_rows(KVW // LANES, tm), _slab_rows(KVW // LANES, tm),
                   _rows(tm, D), _rows(tm, D), _rows(tm, D), _rows(tm, D), ANY, ANY],
        out_shape=[jax.ShapeDtypeStruct((S, D), BF16), _slabs(D // LANES), _slabs(KVW // LANES), _slabs(KVW // LANES),
                   big, big, big, big,
                   jax.ShapeDtypeStruct((WOUT_ROWS, D), BF16), jax.ShapeDtypeStruct((HALO, D), F32)],
        input_output_aliases={3: 8, 4: 9},
        scratch_shapes=_gather_sems(2),
        compiler_params=_params(("arbitrary",)),
    )(x, g1, w_bf, wo_full, cw_full)


def _bias_table(d):
    h = jnp.arange(NKV * GQ, dtype=F32)
    slopes = jnp.exp2(-8.0 * (h + 1.0) / (NKV * GQ))
    qi = jnp.arange(BLK)[:, None]
    kj = jnp.arange(2 * BLK)[None, :]
    dist = BLK + qi - kj
    window = (dist >= 0) & (dist <= BLK)
    bias = -slopes[:, None, None] * (dist * d).astype(F32)[None]
    has_prev = jnp.stack([jnp.broadcast_to(kj >= BLK, (BLK, 2 * BLK)), jnp.ones((BLK, 2 * BLK), bool)])
    valid = window[None] & has_prev
    tab = jnp.where(valid[:, None], bias[None], NEG)
    return tab.reshape(2, NKV, GQ * BLK, 2 * BLK)


def _sub_rows(start, d):
    if d == 1:
        return pl.ds(pl.multiple_of(start, BLK), BLK)
    return pl.ds(start, BLK, stride=d)


CHUNK_ROWS = 2048
BLOCKS_PER_CHUNK = CHUNK_ROWS // BLK


def _low_lanes(rows=BLK):
    return lax.broadcasted_iota(jnp.int32, (rows, LANES), 1) < HD


def _block_start(idx, d):
    shift = d.bit_length() - 1
    b, r = lax.shift_right_logical(idx, shift), lax.bitwise_and(idx, d - 1)
    start = b * (BLK * d) + r
    return b, start, jnp.maximum(start - BLK * d, r)


def _stack_heads(ref, rows):
    low = _low_lanes()
    t0, t1 = ref[0, rows, :], ref[1, rows, :]
    return jnp.concatenate([jnp.where(low, t0, 0.0), jnp.where(low, 0.0, t0),
                            jnp.where(low, t1, 0.0), jnp.where(low, 0.0, t1)], axis=0).astype(BF16)


def _unstack_heads(dup):
    low = _low_lanes()
    return (jnp.where(low, dup[0:BLK], dup[BLK:2 * BLK]), jnp.where(low, dup[2 * BLK:3 * BLK], dup[3 * BLK:4 * BLK]))


def _kv_dup(ref, prow, rows, odd):
    t = jnp.concatenate([ref[0, prow, :], ref[0, rows, :]], axis=0)
    swapped = pltpu.roll(t, HD, axis=1)
    keep = jnp.logical_xor(_low_lanes(2 * BLK), odd)
    return jnp.where(keep, t, swapped).astype(BF16)


PIECES = 3


def _by_head(tiles):
    lane = lax.broadcasted_iota(jnp.int32, tiles[0].shape, 1)
    out = tiles[0]
    for g in range(1, GQ):
        out = jnp.where(lax.bitwise_and(lane, GQ - 1) == g, tiles[g], out)
    return out


def _minus_in_pieces(x):
    lane = lax.broadcasted_iota(jnp.int32, x.shape, 1)
    hi = (-x).astype(BF16).astype(F32)
    rest = -x - hi
    mid = rest.astype(BF16).astype(F32)
    lo = (rest - mid).astype(BF16).astype(F32)
    return jnp.where(lane < GQ, hi, jnp.where(lane < 2 * GQ, mid, jnp.where(lane < PIECES * GQ, lo, 0.0)))


def _attn_fwd(q, k, v, tables, a_gate):
    tm = 256
    width = GQ * HD

    lane_out = jnp.arange(LANES)[None, :] // HD
    spread_sel = jnp.stack([jnp.arange(LANES)[:, None] == 2 * half + lane_out for half in range(2)]).astype(BF16)

    def body(q_ref, k_ref, v_ref, b1_ref, b2_ref, b3_ref, ag_ref, sel_ref, o_ref, lse_ref, y_ref, op, lp):
        odd = pl.program_id(0) % 2 == 1
        chunk = pl.program_id(1)
        ones = jnp.ones((2 * BLK, LANES), BF16)

        for pat, (d, b_ref) in enumerate(zip(PATTERNS, (b1_ref, b2_ref, b3_ref))):
            def block(idx, carry, pat=pat, d=d, b_ref=b_ref):
                b, start, pstart = _block_start(chunk * BLOCKS_PER_CHUNK + idx, d)
                rows, prow = _sub_rows(start, d), _sub_rows(pstart, d)
                mine = _sub_rows(start - chunk * CHUNK_ROWS, d)
                qs = _stack_heads(q_ref, mine)
                kw = _kv_dup(k_ref, prow, rows, odd)
                vw = _kv_dup(v_ref, prow, rows, odd)
                s = _dot_nt(qs, kw) + b_ref[jnp.minimum(b, 1), 0]
                m = jnp.max(s, axis=1, keepdims=True)
                p = jnp.exp(s - m).astype(BF16)
                ol = _dot(p, jnp.concatenate([vw, ones], axis=1))
                l = ol[:, LANES:]
                op[pat, 0, mine, :], op[pat, 1, mine, :] = _unstack_heads(ol[:, :LANES] / l)
                lp[pat, mine, :] = _by_head([(m + jnp.log(l))[g * BLK:(g + 1) * BLK] for g in range(GQ)])
                return carry

            lax.fori_loop(0, BLOCKS_PER_CHUNK, block, 0, unroll=2)

        def mix(t, carry):
            r = pl.ds(pl.multiple_of(t * tm, tm), tm)
            a, b, c = lp[0, r, :], lp[1, r, :], lp[2, r, :]
            m = jnp.maximum(jnp.maximum(a, b), c)
            ea, eb, ec = jnp.exp(a - m), jnp.exp(b - m), jnp.exp(c - m)
            den = ea + eb + ec
            lse_ref[0, r, :] = _minus_in_pieces(m + jnp.log(den))
            inv = 1.0 / den
            for half in range(2):
                def spread(w):
                    hi = w.astype(BF16)
                    lo = (w - hi.astype(F32)).astype(BF16)
                    return _dot(hi, sel_ref[half]) + _dot(lo, sel_ref[half])

                o = (spread(ea * inv) * op[0, half, r, :] + spread(eb * inv) * op[1, half, r, :]
                     + spread(ec * inv) * op[2, half, r, :])
                o_ref[half, r, :] = o
                cols = slice(half * LANES, (half + 1) * LANES)
                ag = ag_ref[r, cols]
                y_ref[r, cols] = (o * (ag * _sigmoid(ag))).astype(BF16)
            return carry

        lax.fori_loop(0, CHUNK_ROWS // tm, mix, 0)

    q_like = pl.BlockSpec((2, CHUNK_ROWS, LANES), lambda j, c: (j, c, 0))
    per_kv = pl.BlockSpec((1, CHUNK_ROWS, LANES), lambda j, c: (j, c, 0))
    kv = pl.BlockSpec((1, S, LANES), lambda j, c: (j // 2, 0, 0))
    bias_spec = pl.BlockSpec((2, 1, GQ * BLK, 2 * BLK), lambda j, c: (0, j, 0, 0))
    group_cols = pl.BlockSpec((CHUNK_ROWS, width), lambda j, c: (c, j))
    return pl.pallas_call(
        body, grid=(NKV, S // CHUNK_ROWS), name="attn_fwd",
        in_specs=[q_like, kv, kv, bias_spec, bias_spec, bias_spec, group_cols,
                  pl.BlockSpec((2, LANES, LANES), lambda j, c: (0, 0, 0))],
        out_specs=[q_like, per_kv, group_cols],
        out_shape=[_slabs(D // LANES), _slabs(NKV), jax.ShapeDtypeStruct((S, D), BF16)],
        scratch_shapes=[pltpu.VMEM((len(PATTERNS), 2, CHUNK_ROWS, LANES), F32),
                        pltpu.VMEM((len(PATTERNS), CHUNK_ROWS, LANES), F32)],
        compiler_params=_params(("arbitrary", "arbitrary")),
    )(q, k, v, *tables, a_gate, spread_sel)


def _head_sum_selectors():
    lane_in = jnp.arange(LANES)[:, None] // HD
    return jnp.stack([jnp.broadcast_to(lane_in == h, (LANES, LANES)) for h in range(2)]).astype(BF16)


def _attn_gate_bwd(dy_att, o, a_gate, selectors, chip_sums):
    tm = 256
    last = S // tm - 1
    landing, sems = _exchange_results_of(chip_sums)
    n_sums = len(chip_sums)

    def body(dy_ref, o_ref, ag_ref, e_ref, *refs):
        sums, (do_ref, dag_ref, delta_ref), refs = refs[:n_sums], refs[n_sums:n_sums + 3], refs[n_sums + 3:]
        landed, (send, recv) = refs[:n_sums], refs[n_sums:]
        i = pl.program_id(0)
        copies = _chip_exchange_copies(sums, landed, send, recv)
        _start_exchange(copies, i == 0)
        for j in range(NKV):
            deltas = []
            for sl in (2 * j, 2 * j + 1):
                cols = slice(sl * LANES, (sl + 1) * LANES)
                dy, ag, o_ = dy_ref[:, cols], ag_ref[:, cols], o_ref[sl]
                sg = _sigmoid(ag)
                do = dy * (ag * sg)
                do_ref[sl] = do
                dag_ref[:, cols] = (dy * o_ * (sg * (1.0 + ag * (1.0 - sg)))).astype(BF16)
                prod = do * o_
                hi = prod.astype(BF16)
                lo = (prod - hi.astype(F32)).astype(BF16)
                deltas += [_dot(hi, e_ref[h]) + _dot(lo, e_ref[h]) for h in range(2)]
            delta_ref[j] = _minus_in_pieces(_by_head(deltas))
        _finish_exchange(copies, i == last)

    return pl.pallas_call(
        body, grid=(S // tm,), name="attn_gate_bwd",
        in_specs=[_rows(tm, D), _slab_rows(D // LANES, tm), _rows(tm, D), _resident((2, LANES, LANES))] + [ANY] * n_sums,
        out_specs=[_slab_rows(D // LANES, tm), _rows(tm, D), _slab_rows(NKV, tm)] + [ANY] * n_sums,
        out_shape=[_slabs(D // LANES), jax.ShapeDtypeStruct((S, D), BF16), _slabs(NKV)] + landing,
        scratch_shapes=sems,
        compiler_params=_params(("arbitrary",)),
    )(dy_att, o, a_gate, selectors, *chip_sums)


def _own_pieces(tile):
    lane = lax.broadcasted_iota(jnp.int32, tile.shape, 1)
    head = jnp.where(lane < PIECES * GQ, lax.bitwise_and(lane, GQ - 1), -1)
    return jnp.concatenate([jnp.where(head == g, tile, 0.0) for g in range(GQ)], axis=0).astype(BF16)


def _attn_bwd(q, k, v, do, lse, delta, bias, d):
    def body(q_ref, do_ref, l_ref, dl_ref, k_ref, v_ref, b_ref, dq_ref, dkv_ref, acc):
        odd = pl.program_id(0) % 2 == 1
        chunk = pl.program_id(1)
        ones = (lax.broadcasted_iota(jnp.int32, (2 * BLK, LANES), 1) < PIECES * GQ).astype(BF16)

        def in_acc(block_idx):
            return pl.ds(pl.multiple_of(block_idx * BLK, BLK), BLK)

        @pl.when(chunk == 0)
        def _():
            acc[...] = jnp.zeros_like(acc)

        def block(idx, carry):
            idx = chunk * BLOCKS_PER_CHUNK + idx
            b, start, pstart = _block_start(idx, d)
            rows, prow = _sub_rows(start, d), _sub_rows(pstart, d)
            mine = _sub_rows(start - chunk * CHUNK_ROWS, d)
            qs = _stack_heads(q_ref, mine)
            dos = _stack_heads(do_ref, mine)
            kw = _kv_dup(k_ref, prow, rows, odd)
            vw = _kv_dup(v_ref, prow, rows, odd)
            s = _dot_nt(jnp.concatenate([qs, _own_pieces(l_ref[0, mine, :])], axis=1),
                        jnp.concatenate([kw, ones], axis=1)) + b_ref[jnp.minimum(b, 1), 0]
            p = jnp.exp(s)
            dv2 = _dot_tn(p.astype(BF16), dos)
            dp = _dot_nt(jnp.concatenate([dos, _own_pieces(dl_ref[0, mine, :])], axis=1),
                         jnp.concatenate([vw, ones], axis=1))
            ds = (p * dp).astype(BF16)
            dq_ref[0, mine, :], dq_ref[1, mine, :] = _unstack_heads(_dot(ds, kw))
            dk2 = _dot_tn(ds, qs)
            dkv = jnp.where(_low_lanes(2 * BLK), dk2 + pltpu.roll(dk2, HD, axis=1), dv2 + pltpu.roll(dv2, HD, axis=1))
            acc[in_acc(idx), :] = acc[in_acc(idx), :] + dkv[BLK:]
            before = jnp.where(b >= 1, idx - d, idx)
            acc[in_acc(before), :] = acc[in_acc(before), :] + dkv[:BLK]
            return carry

        lax.fori_loop(0, BLOCKS_PER_CHUNK, block, 0, unroll=8)

        @pl.when(chunk == S // CHUNK_ROWS - 1)
        def _():
            def place(idx, carry):
                _, start, _ = _block_start(idx, d)
                dkv_ref[0, _sub_rows(start, d), :] = acc[in_acc(idx), :]
                return carry

            lax.fori_loop(0, S // BLK, place, 0, unroll=4)

    q_like = pl.BlockSpec((2, CHUNK_ROWS, LANES), lambda j, c: (j, c, 0))
    pieces = pl.BlockSpec((1, CHUNK_ROWS, LANES), lambda j, c: (j, c, 0))
    kv = pl.BlockSpec((1, S, LANES), lambda j, c: (j // 2, 0, 0))
    per_kv = pl.BlockSpec((1, S, LANES), lambda j, c: (j, 0, 0))
    bias_spec = pl.BlockSpec((2, 1, GQ * BLK, 2 * BLK), lambda j, c: (0, j, 0, 0))
    return pl.pallas_call(
        body, grid=(NKV, S // CHUNK_ROWS), name=f"attn_bwd_d{d}",
        in_specs=[q_like, q_like, pieces, pieces, kv, kv, bias_spec],
        out_specs=[q_like, per_kv],
        out_shape=[_slabs(D // LANES), _slabs(NKV)],
        scratch_shapes=[pltpu.VMEM((S, LANES), F32)],
        compiler_params=_params(("arbitrary", "arbitrary")),
    )(q, do, lse, delta, k, v, bias)


CONV_T = 256


def _halo_before(i):
    return (jnp.maximum(i * (CONV_T // HALO) - 1, 0), 0)


def _halo_after(i):
    return (jnp.minimum((i + 1) * (CONV_T // HALO), S // HALO - 1), 0)


SUBLANES = 8
NCH = D // LANES
GROUP = SUBLANES * SUBLANES


def _comb(ref, cb, base):
    return ref[cb, pl.ds(base, SUBLANES, stride=SUBLANES), :]


def _taps(w_ref, cols):
    return [jnp.broadcast_to(w_ref[j:j + 1, cols], (SUBLANES, LANES)) for j in range(CONV_K)]


def _conv_fwd(c_val, c_glu, c_gate, conv_w, conv_b, ln_g, ln_b):
    T = CONV_T

    def body(cv_ref, cg_ref, cvh_ref, cgh_ref, gate_ref, w_ref, b_ref, lg_ref, lb_ref, u_ref, y_ref, win, us):
        i = pl.program_id(0)
        for cb in range(NCH):
            cols = slice(cb * LANES, (cb + 1) * LANES)
            win[cb, HALO:HALO + T, :] = cv_ref[:, cols] * _sigmoid(cg_ref[:, cols])
            win[cb, 0:HALO, :] = jnp.where(i > 0, cvh_ref[:, cols] * _sigmoid(cgh_ref[:, cols]), 0.0)
        for cb in range(NCH):
            cols = slice(cb * LANES, (cb + 1) * LANES)
            taps = _taps(w_ref, cols)
            bias = jnp.broadcast_to(b_ref[:, cols], (SUBLANES, LANES))

            def group(g, carry):
                for b in range(SUBLANES):
                    base = g * GROUP + b
                    acc = bias
                    for j in range(CONV_K):
                        acc = acc + taps[j] * _comb(win, cb, base + (HALO - (CONV_K - 1) + j))
                    us[cb, pl.ds(base, SUBLANES, stride=SUBLANES), :] = acc
                return carry

            lax.fori_loop(0, T // GROUP, group, 0)
        total = us[0]
        for cb in range(1, NCH):
            total = total + us[cb]
        mu = jnp.sum(total, axis=-1, keepdims=True) * (1.0 / D)
        sq = jnp.zeros((T, LANES), F32)
        for cb in range(NCH):
            uc = us[cb] - mu
            sq = sq + uc * uc
        rstd = lax.rsqrt(jnp.sum(sq, axis=-1, keepdims=True) * (1.0 / D) + LN_EPS)
        for cb in range(NCH):
            cols = slice(cb * LANES, (cb + 1) * LANES)
            u = us[cb]
            u_ref[:, cols] = u
            nrm = (u - mu) * rstd * lg_ref[:, cols] + lb_ref[:, cols]
            gate = gate_ref[:, cols]
            y_ref[:, cols] = (nrm * _sigmoid(nrm) * (gate * _sigmoid(gate))).astype(BF16)

    halo = pl.BlockSpec((HALO, D), _halo_before)
    return pl.pallas_call(
        body, grid=(S // T,), name="conv_fwd",
        in_specs=[_rows(T, D), _rows(T, D), halo, halo, _rows(T, D),
                  _resident((HALO, D)), _resident((1, D)), _resident((1, D)), _resident((1, D))],
        out_specs=[_rows(T, D), _rows(T, D)],
        out_shape=[jax.ShapeDtypeStruct((S, D), F32), jax.ShapeDtypeStruct((S, D), BF16)],
        scratch_shapes=[pltpu.VMEM((NCH, T + HALO, LANES), F32), pltpu.VMEM((NCH, T, LANES), F32)],
        compiler_params=_params(("arbitrary",)),
    )(c_val, c_glu, c_val, c_glu, c_gate, conv_w, conv_b, ln_g, ln_b)


def _conv_bwd_rows(u, c_gate, dy_conv, ln_g, ln_b):
    tm = 256

    def body(u_ref, gate_ref, dy_ref, lg_ref, lb_ref, du_ref, dgate_ref, st_ref):
        @pl.when(pl.program_id(0) == 0)
        def _():
            st_ref[...] = jnp.zeros_like(st_ref)

        u, gate, dy = u_ref[...], gate_ref[...], dy_ref[...]
        mu = jnp.mean(u, axis=-1, keepdims=True)
        uc = u - mu
        rstd = lax.rsqrt(jnp.mean(uc * uc, axis=-1, keepdims=True) + LN_EPS)
        z = uc * rstd
        nrm = z * lg_ref[...] + lb_ref[...]
        sn, sg = _sigmoid(nrm), _sigmoid(gate)
        dgate_ref[...] = (dy * (nrm * sn) * (sg * (1.0 + gate * (1.0 - sg)))).astype(BF16)
        dn = dy * (gate * sg) * (sn * (1.0 + nrm * (1.0 - sn)))
        dz = dn * lg_ref[...]
        du = rstd * (dz - jnp.mean(dz, axis=-1, keepdims=True) - z * jnp.mean(dz * z, axis=-1, keepdims=True))
        du_ref[...] = du
        st_ref[0:1, :] += jnp.sum(dn * z, axis=0, keepdims=True)
        st_ref[1:2, :] += jnp.sum(dn, axis=0, keepdims=True)
        st_ref[2:3, :] += jnp.sum(du, axis=0, keepdims=True)

    big = jax.ShapeDtypeStruct((S, D), F32)
    return pl.pallas_call(
        body, grid=(S // tm,), name="conv_bwd_rows",
        in_specs=[_rows(tm, D)] * 3 + [_resident((1, D)), _resident((1, D))],
        out_specs=[_rows(tm, D), _rows(tm, D), pl.BlockSpec((8, D), lambda i: (0, 0))],
        out_shape=[big, jax.ShapeDtypeStruct((S, D), BF16), jax.ShapeDtypeStruct((8, D), F32)],
        compiler_params=_params(("arbitrary",)),
    )(u, c_gate, dy_conv, ln_g, ln_b)


def _conv_bwd_taps(du, c_val, c_glu, conv_w):
    T = CONV_T
    last = S // T - 1

    def body(du_ref, dua_ref, cv_ref, cg_ref, cvh_ref, cgh_ref, w_ref, dcv_ref, dcg_ref, dw_ref,
             hwin, dwin, dhs, dw_acc):
        i = pl.program_id(0)

        @pl.when(i == 0)
        def _():
            dw_acc[...] = jnp.zeros_like(dw_acc)

        for cb in range(NCH):
            cols = slice(cb * LANES, (cb + 1) * LANES)
            hwin[cb, HALO:HALO + T, :] = cv_ref[:, cols] * _sigmoid(cg_ref[:, cols])
            hwin[cb, 0:HALO, :] = jnp.where(i > 0, cvh_ref[:, cols] * _sigmoid(cgh_ref[:, cols]), 0.0)
            dwin[cb, 0:T, :] = du_ref[:, cols]
            dwin[cb, T:T + HALO, :] = jnp.where(i < last, dua_ref[:, cols], 0.0)
        for cb in range(NCH):
            cols = slice(cb * LANES, (cb + 1) * LANES)
            taps = _taps(w_ref, cols)

            def group_dh(g, carry):
                for b in range(SUBLANES):
                    base = g * GROUP + b
                    acc = jnp.zeros((SUBLANES, LANES), F32)
                    for j in range(CONV_K):
                        acc = acc + taps[j] * _comb(dwin, cb, base + (CONV_K - 1 - j))
                    dhs[cb, pl.ds(base, SUBLANES, stride=SUBLANES), :] = acc
                return carry

            lax.fori_loop(0, T // GROUP, group_dh, 0)

            def group_dw(g, sums):
                for b in range(SUBLANES):
                    base = g * GROUP + b
                    d = _comb(dwin, cb, base)
                    sums = tuple(sums[j] + d * _comb(hwin, cb, base + (HALO - (CONV_K - 1) + j))
                                 for j in range(CONV_K))
                return sums

            sums = lax.fori_loop(0, T // GROUP, group_dw, tuple(dw_acc[j, :, cols] for j in range(CONV_K)))
            for j in range(CONV_K):
                dw_acc[j, :, cols] = sums[j]
            dh = dhs[cb]
            cv, sg = cv_ref[:, cols], _sigmoid(cg_ref[:, cols])
            dcv_ref[:, cols] = (dh * sg).astype(BF16)
            dcg_ref[:, cols] = (dh * cv * (sg * (1.0 - sg))).astype(BF16)

        @pl.when(i == last)
        def _():
            dw_ref[...] = jnp.zeros_like(dw_ref)
            for j in range(CONV_K):
                dw_ref[j:j + 1, :] = jnp.sum(dw_acc[j], axis=0, keepdims=True)

    before = pl.BlockSpec((HALO, D), _halo_before)
    after = pl.BlockSpec((HALO, D), _halo_after)
    big = jax.ShapeDtypeStruct((S, D), BF16)
    return pl.pallas_call(
        body, grid=(S // T,), name="conv_bwd_taps",
        in_specs=[_rows(T, D), after, _rows(T, D), _rows(T, D), before, before, _resident((HALO, D))],
        out_specs=[_rows(T, D), _rows(T, D), pl.BlockSpec((HALO, D), lambda i: (0, 0))],
        out_shape=[big, big, jax.ShapeDtypeStruct((HALO, D), F32)],
        scratch_shapes=[pltpu.VMEM((NCH, T + HALO, LANES), F32), pltpu.VMEM((NCH, T + HALO, LANES), F32),
                        pltpu.VMEM((NCH, T, LANES), F32), pltpu.VMEM((CONV_K, SUBLANES, D), F32)],
        compiler_params=_params(("arbitrary",)),
    )(du, du, c_val, c_glu, c_val, c_glu, conv_w)


def _outproj_loss(y_att, y_conv, w_out_bf, x, target, gf):
    tm = 512

    def body(ya_ref, yc_ref, w_ref, x_ref, t_ref, gf_ref, dx2_ref, dya_ref, dyc_ref, dw_ref, st_ref, acc):
        @pl.when(pl.program_id(0) == 0)
        def _():
            acc[...] = jnp.zeros_like(acc)
            st_ref[...] = jnp.zeros_like(st_ref)

        ya, yc = ya_ref[...], yc_ref[...]
        x2 = x_ref[...] + _dot(ya, w_ref[0:D, :]) + _dot(yc, w_ref[D:2 * D, :])
        r = lax.rsqrt(jnp.mean(x2 * x2, axis=-1, keepdims=True) + NORM_EPS)
        xn = x2 * r
        err = xn * gf_ref[...] - t_ref[...]
        dout = err * (1.0 / D)
        dxn = dout * gf_ref[...]
        dx2 = r * (dxn - xn * jnp.mean(dxn * xn, axis=-1, keepdims=True))
        dx2_ref[...] = dx2
        dx2b = dx2.astype(BF16)
        dya_ref[...] = _dot_nt(dx2b, w_ref[0:D, :])
        dyc_ref[...] = _dot_nt(dx2b, w_ref[D:2 * D, :])
        acc[0:D, :] += _dot_tn(ya, dx2b)
        acc[D:2 * D, :] += _dot_tn(yc, dx2b)
        st_ref[0:1, :] += jnp.sum(dout * xn, axis=0, keepdims=True)
        st_ref[1:2, :] += jnp.sum(err * err, axis=0, keepdims=True) * (0.5 / D)

        @pl.when(pl.program_id(0) == S // tm - 1)
        def _():
            dw_ref[...] = acc[...].astype(BF16)

    big = jax.ShapeDtypeStruct((S, D), F32)
    return pl.pallas_call(
        body, grid=(S // tm,), name="outproj_loss",
        in_specs=[_rows(tm, D), _rows(tm, D), _resident((WOUT_ROWS, D)), _rows(tm, D), _rows(tm, D), _resident((1, D))],
        out_specs=[_rows(tm, D), _rows(tm, D), _rows(tm, D),
                   pl.BlockSpec((WOUT_ROWS, D), lambda i: (0, 0)), pl.BlockSpec((8, D), lambda i: (0, 0))],
        out_shape=[big, big, big, jax.ShapeDtypeStruct((WOUT_ROWS, D), BF16), jax.ShapeDtypeStruct((8, D), F32)],
        scratch_shapes=[pltpu.VMEM((WOUT_ROWS, D), F32)],
        compiler_params=_params(("arbitrary",)),
    )(y_att, y_conv, w_out_bf, x, target, gf)


UNITS_PER_CHUNK = CHUNK // LANES


def _dproj_unit(u, dqs, dkvs, gates, rows):
    if u < OFF_K // LANES:
        return ((dqs[0][u] + dqs[1][u] + dqs[2][u]) * (HD ** -0.5)).astype(BF16)
    if u < OFF_AG // LANES:
        w = u - OFF_K // LANES
        ta, tb = (dkvs[0][j] + dkvs[1][j] + dkvs[2][j] for j in (2 * (w % 2), 2 * (w % 2) + 1))
        low = _low_lanes(rows)
        if w < 2:
            return jnp.where(low, ta, pltpu.roll(tb, HD, axis=1)).astype(BF16)
        return jnp.where(low, pltpu.roll(ta, HD, axis=1), tb).astype(BF16)
    g, sl = divmod(u - OFF_AG // LANES, D // LANES)
    return gates[g][:, sl * LANES:(sl + 1) * LANES]


def _dproj_sources(units, dqs, dkvs, gates, rows):
    use_q = any(u < OFF_K // LANES for u in units)
    use_kv = any(OFF_K // LANES <= u < OFF_AG // LANES for u in units)
    use_g = sorted({(u - OFF_AG // LANES) // (D // LANES) for u in units if u >= OFF_AG // LANES})
    args = (list(dqs) if use_q else []) + (list(dkvs) if use_kv else []) + [gates[g] for g in use_g]
    specs = ([_slab_rows(D // LANES, rows)] * 3 if use_q else []) + ([_slab_rows(NKV, rows)] * 3 if use_kv else []) \
        + [_rows(rows, D)] * len(use_g)

    def pick(refs):
        refs = list(refs)
        q_refs = [refs.pop(0) for _ in range(3)] if use_q else None
        kv_refs = [refs.pop(0) for _ in range(3)] if use_kv else None
        return q_refs, kv_refs, {g: refs.pop(0) for g in use_g}

    return args, specs, pick


def _exchange_results_of(chip_sums):
    n = len(chip_sums) * len(CHIP_FLIPS)
    shapes = [jax.ShapeDtypeStruct((NCHIP,) + tuple(a.shape[1:] if a.ndim == 3 else a.shape), a.dtype)
              for a in chip_sums]
    return shapes, [pltpu.SemaphoreType.DMA((n,)), pltpu.SemaphoreType.DMA((n,))]


def _start_exchange(copies, first_step):
    @pl.when(first_step)
    def _():
        for out, _ in copies:
            out.start()


def _finish_exchange(copies, last_step):
    @pl.when(last_step)
    def _():
        for _, arrival in copies:
            arrival.wait_recv()
        for out, _ in copies:
            out.wait_send()


def _inproj_bwd_x(dqs, dkvs, gates, w_bf, x, g1, dx2, pi):
    tm = 256
    last = S // tm - 1
    units = range(NCOL // LANES)
    pieces, piece_specs, pick = _dproj_sources(units, dqs, dkvs, gates, tm)
    landing, sems = _exchange_results_of([pi])

    def body(*refs):
        piece_refs, refs = refs[:len(pieces)], refs[len(pieces):]
        w_ref, x_ref, g_ref, dx2_ref, pi_ref, gx_ref, st_ref, ri_ref, dp_ref, send, recv = refs
        i = pl.program_id(0)
        copies = _chip_exchange_copies([pi_ref], [ri_ref], send, recv)
        _start_exchange(copies, i == 0)

        @pl.when(i == 0)
        def _():
            st_ref[...] = jnp.zeros_like(st_ref)

        sources = pick(piece_refs)
        for u in units:
            dp_ref[:, u * LANES:(u + 1) * LANES] = _dproj_unit(u, *sources, tm)
        dh = _dot_nt(dp_ref[...], w_ref[...])
        xt = x_ref[...]
        r = lax.rsqrt(jnp.mean(xt * xt, axis=-1, keepdims=True) + NORM_EPS)
        xn = xt * r
        dxn = dh * g_ref[...]
        gx_ref[...] = dx2_ref[...] + r * (dxn - xn * jnp.mean(dxn * xn, axis=-1, keepdims=True))
        st_ref[0:1, :] += jnp.sum(dh * xn, axis=0, keepdims=True)
        _finish_exchange(copies, i == last)

    return pl.pallas_call(
        body, grid=(S // tm,), name="inproj_bwd_x",
        in_specs=piece_specs + [_resident((D, NCOL)), _rows(tm, D), _resident((1, D)), _rows(tm, D), ANY],
        out_specs=[_rows(tm, D), pl.BlockSpec((8, D), lambda i: (0, 0)), ANY],
        out_shape=[jax.ShapeDtypeStruct((S, D), F32), jax.ShapeDtypeStruct((8, D), F32)] + landing,
        scratch_shapes=[pltpu.VMEM((tm, NCOL), BF16)] + sems,
        compiler_params=_params(("arbitrary",)),
    )(*pieces, w_bf, x, g1, dx2, pi)


def _inproj_bwd_w(h, dqs, dkvs, gates):
    out = None
    for k in range(NCHIP):
        units = range(k * UNITS_PER_CHUNK, (k + 1) * UNITS_PER_CHUNK)
        tk = 512 if units[0] < OFF_K // LANES else 1024
        nk = S // tk
        pieces, piece_specs, pick = _dproj_sources(units, dqs, dkvs, gates, tk)
        handed_on = [] if out is None else [out]

        def body(*refs, units=units, pick=pick, n_pieces=len(pieces), n_in=1 + len(pieces) + len(handed_on)):
            h_ref, piece_refs = refs[0], refs[1:1 + n_pieces]
            o_ref, tile, acc = refs[n_in:]
            i = pl.program_id(0)

            @pl.when(i == 0)
            def _():
                acc[...] = jnp.zeros_like(acc)

            sources = pick(piece_refs)
            for n, u in enumerate(units):
                tile[:, n * LANES:(n + 1) * LANES] = _dproj_unit(u, *sources, tk)
            acc[...] += _dot_tn(h_ref[...], tile[...])

            @pl.when(i == nk - 1)
            def _():
                o_ref[0] = acc[...].astype(BF16)

        out = pl.pallas_call(
            body, grid=(nk,), name=f"inproj_bwd_w{k}",
            in_specs=[_rows(tk, D)] + piece_specs + [ANY] * len(handed_on),
            out_specs=pl.BlockSpec((1, D, CHUNK), lambda i, k=k: (k, 0, 0)),
            out_shape=jax.ShapeDtypeStruct((NCHIP, D, CHUNK), BF16),
            input_output_aliases={1 + len(pieces): 0} if handed_on else {},
            scratch_shapes=[pltpu.VMEM((tk, CHUNK), BF16), pltpu.VMEM((D, CHUNK), F32)],
            compiler_params=_params(("arbitrary",)),
        )(h, *pieces, *handed_on)
    return out


ROW_LN_G, ROW_LN_B, ROW_CONV_B, ROW_FINAL_G, ROW_LOSS, ROW_TAPS = 0, 1, 2, 8, 9, 16
SMALL_ROWS = 16 + HALO
NDEV = 8


MESH = pl.DeviceIdType.MESH
ANY = pl.BlockSpec(memory_space=pl.ANY)
CHIP_FLIPS = ((1, 0), (0, 1), (1, 1))


def _pos():
    return lax.axis_index("x"), lax.axis_index("y"), lax.axis_index("c")


def _flip(v, f):
    return 1 - v if f else v


def _ds(start, size, align=None):
    return pl.ds(pl.multiple_of(start, align or size), size)


def _place_shards(wi, wo, cw, where):
    steps = 4

    def body(where_ref, wi_ref, wo_ref, cw_ref, wi_full, wo_full, cw_full):
        wi_full[...] = wi_ref[...].astype(BF16)
        wo_full[...] = wo_ref[...].astype(BF16)
        cw_full[...] = cw_ref[...]

    grid_spec = pltpu.PrefetchScalarGridSpec(
        num_scalar_prefetch=1, grid=(steps,),
        in_specs=[pl.BlockSpec((D // steps, CHUNK), lambda i, w: (i, 0)),
                  pl.BlockSpec((WOUT_SHARD // steps, D), lambda i, w: (i, 0)),
                  pl.BlockSpec((HALO, CONVW_SHARD), lambda i, w: (0, 0))],
        out_specs=[pl.BlockSpec((D // steps, CHUNK), lambda i, w: (i, w[0])),
                   pl.BlockSpec((WOUT_SHARD // steps, D), lambda i, w: (w[0] * steps + i, 0)),
                   pl.BlockSpec((HALO, CONVW_SHARD), lambda i, w: (0, w[0]))])
    return pl.pallas_call(
        body, grid_spec=grid_spec, name="place_shards",
        out_shape=[jax.ShapeDtypeStruct((D, NCOL), BF16), jax.ShapeDtypeStruct((WOUT_ROWS, D), BF16),
                   jax.ShapeDtypeStruct((HALO, D), F32)],
        compiler_params=_params(("arbitrary",)),
    )(where, wi, wo, cw)


W_IN, W_OUT, TAPS = range(3)
GATHER_SEMS = 12


def _gather_stages(fulls, send, recv):
    halves = {W_IN: D // 2, W_OUT: WOUT_SHARD // 2, TAPS: HALO // 2}
    x, y, c = _pos()
    chips = {"me": (x, y), "x": (1 - x, y), "y": (x, 1 - y), "diag": (1 - x, 1 - y)}
    SENT = ((("me", 0), "x"), (("me", 1), "x"), (("me", 1), "y"), (("me", 0), "y"), (("x", 0), "y"), (("y", 1), "x"))
    LANDS = ((("x", 0), "x"), (("x", 1), "x"), (("y", 1), "y"), (("y", 0), "y"), (("diag", 0), "y"), (("diag", 1), "x"))
    N_ICI = len(SENT)

    def region(n_th, whose, half, part):
        a, full = fulls[n_th]
        chip = 2 * chips[whose][0] + chips[whose][1]
        n = halves[a] // 2
        row = half * halves[a] + part * n
        if a == W_IN:
            return full.at[_ds(row, n), _ds(chip * CHUNK, CHUNK, 128)]
        if a == W_OUT:
            return full.at[_ds(chip * WOUT_SHARD + row, n), :]
        return full.at[_ds(row, n), _ds(chip * CONVW_SHARD, CONVW_SHARD, 128)]

    def copy(n_th, kind, piece, dev):
        k = GATHER_SEMS * n_th + kind
        return pltpu.make_async_remote_copy(src_ref=piece, dst_ref=piece, send_sem=send.at[k], recv_sem=recv.at[k],
                                            device_id=dev, device_id_type=MESH)

    def sent(a, k):
        if k < N_ICI:
            (whose, part), to = SENT[k]
            return copy(a, k, region(a, whose, c, part), (*chips[to], c))
        (whose, part), _ = LANDS[k - N_ICI]
        return copy(a, k, region(a, whose, c, part), (x, y, 1 - c))

    def wait_arrival(a, k):
        if k < N_ICI:
            (whose, part), frm = LANDS[k]
            copy(a, k, region(a, whose, c, part), (*chips[frm], c)).wait_recv()
        else:
            (whose, part), _ = LANDS[k - N_ICI]
            copy(a, k, region(a, whose, 1 - c, part), (x, y, 1 - c)).wait_recv()

    arrays = range(len(fulls))

    def own_to_neighbours():
        for a in arrays:
            for k in (0, 2, 1, 3):
                sent(a, k).start()

    def pass_on_neighbours():
        for a in arrays:
            for k, onward in ((0, 4), (2, 5), (1, None), (3, None)):
                wait_arrival(a, k)
                if onward is not None:
                    sent(a, onward).start()
                sent(a, k + N_ICI).start()

    def pass_on_diagonal():
        for a in arrays:
            for k in (4, 5):
                wait_arrival(a, k)
                sent(a, k + N_ICI).start()

    def finish():
        for a in arrays:
            for k in range(N_ICI, 2 * N_ICI):
                wait_arrival(a, k)
            for k in range(2 * N_ICI):
                sent(a, k).wait_send()

    return own_to_neighbours, pass_on_neighbours, pass_on_diagonal, finish


def _gather_sems(n_arrays):
    return [pltpu.SemaphoreType.DMA((GATHER_SEMS * n_arrays,)), pltpu.SemaphoreType.DMA((GATHER_SEMS * n_arrays,))]


def _gather_w_in(wi_full):
    def body(_wi, full, send, recv):
        for stage in _gather_stages([(W_IN, full)], send, recv):
            stage()

    return pl.pallas_call(
        body, name="gather_w_in", in_specs=[ANY], out_specs=ANY, input_output_aliases={0: 0},
        out_shape=jax.ShapeDtypeStruct((D, NCOL), BF16), scratch_shapes=_gather_sems(1),
    )(wi_full)


def _half_shape(a):
    return jax.ShapeDtypeStruct((NCHIP, a.shape[1] // 2, a.shape[2]) if a.ndim == 3 else a.shape, a.dtype)


def _exchange_halves(arrays, name):
    n = len(arrays)

    def body(*refs):
        srcs, dsts, (send, recv) = refs[:n], refs[n:2 * n], refs[2 * n:]
        x, y, c = _pos()
        cps = []
        for k, (s_, d_) in enumerate(zip(srcs, dsts)):
            if len(s_.shape) == 3:
                h = s_.shape[1] // 2
                s_ = s_.at[:, _ds((1 - c) * h, h), :]
            cps.append(pltpu.make_async_remote_copy(src_ref=s_, dst_ref=d_, send_sem=send.at[k], recv_sem=recv.at[k],
                                                    device_id=(x, y, 1 - c), device_id_type=MESH))
        for cp in cps:
            cp.start()
        for cp in cps:
            cp.wait()

    return pl.pallas_call(
        body, name=name, in_specs=[ANY] * n, out_specs=[ANY] * n, out_shape=[_half_shape(a) for a in arrays],
        scratch_shapes=[pltpu.SemaphoreType.DMA((n,)), pltpu.SemaphoreType.DMA((n,))],
    )(*arrays)


def _add_halves(arrays, received, name):
    n = len(arrays)

    def body(*refs):
        mine, theirs, outs = refs[:n], refs[n:2 * n], refs[2 * n:]
        c = lax.axis_index("c")
        for m_, t_, o_ in zip(mine, theirs, outs):
            if len(m_.shape) == 3:
                h = m_.shape[1] // 2
                o_[0] = (m_[0, _ds(c * h, h), :].astype(F32) + t_[0].astype(F32)).astype(o_.dtype)
            else:
                o_[...] = m_[...] + t_[...]

    def spec(shape):
        if len(shape) == 3:
            return pl.BlockSpec((1,) + tuple(shape[1:]), lambda k: (k, 0, 0))
        return pl.BlockSpec(tuple(shape), lambda k: (0, 0))

    halves = [_half_shape(a) for a in arrays]
    return pl.pallas_call(
        body, grid=(NCHIP,), name=name,
        in_specs=[spec(a.shape) for a in arrays] + [spec(h.shape) for h in halves],
        out_specs=[spec(h.shape) for h in halves], out_shape=halves,
        compiler_params=_params(("arbitrary",)),
    )(*arrays, *received)


def _chip_exchange_copies(srcs, dsts, send, recv):
    x, y, c = _pos()
    me = 2 * x + y
    pairs = []
    for a in range(len(srcs)):
        for j, (fx, fy) in enumerate(CHIP_FLIPS):
            px, py = _flip(x, fx), _flip(y, fy)
            peer = 2 * px + py
            k = len(CHIP_FLIPS) * a + j
            out = pltpu.make_async_remote_copy(
                src_ref=srcs[a].at[peer] if len(srcs[a].shape) == 3 else srcs[a], dst_ref=dsts[a].at[me],
                send_sem=send.at[k], recv_sem=recv.at[k], device_id=(px, py, c), device_id_type=MESH)
            got = dsts[a].at[peer]
            arrival = pltpu.make_async_remote_copy(
                src_ref=got, dst_ref=got, send_sem=send.at[k], recv_sem=recv.at[k],
                device_id=(px, py, c), device_id_type=MESH)
            pairs.append((out, arrival))
    return pairs


def _sum_chips(ri, ro, rs, pi, po, ps, where):
    def body(w_ref, ri_ref, ro_ref, rs_ref, pi_ref, po_ref, ps_ref, gi_ref, go_ref, gs_ref, g5_ref, loss_ref,
             acc_i, acc_o, acc_s):
        k = pl.program_id(0)
        accs = (acc_i, acc_o, acc_s)

        @pl.when(k == 0)
        def _():
            for acc in accs:
                acc[...] = jnp.zeros_like(acc)

        @pl.when(k == w_ref[0])
        def _():
            for acc, val in zip(accs, (pi_ref[0], po_ref[0], ps_ref[...])):
                acc[...] += val.astype(F32)

        @pl.when(k != w_ref[0])
        def _():
            for acc, ref in zip(accs, (ri_ref, ro_ref, rs_ref)):
                acc[...] += ref[0].astype(F32)

        @pl.when(k == NCHIP - 1)
        def _():
            gi_ref[0] = acc_i[...]
            go_ref[0] = acc_o[...]
            gs_ref[...] = acc_s[...]
            g5_ref[...] = jnp.zeros_like(g5_ref)
            for i, row in enumerate((ROW_CONV_B, ROW_LN_G, ROW_LN_B, ROW_FINAL_G)):
                g5_ref[i + 1:i + 2, :] = acc_s[row:row + 1, :]
            loss = jnp.sum(acc_s[ROW_LOSS:ROW_LOSS + 1, :], axis=1, keepdims=True)
            loss_ref[...] = jnp.broadcast_to(loss, loss_ref.shape)

    def sent(k, w):
        return jnp.where(k == w[0], (k + 1) % NCHIP, k)

    hi, ho = D // 2, WOUT_SHARD // 2
    const = lambda shape: pl.BlockSpec(shape, lambda k, w: (0,) * len(shape))
    grid_spec = pltpu.PrefetchScalarGridSpec(
        num_scalar_prefetch=1, grid=(NCHIP,),
        in_specs=[pl.BlockSpec((1, hi, CHUNK), lambda k, w: (sent(k, w), 0, 0)),
                  pl.BlockSpec((1, ho, D), lambda k, w: (sent(k, w), 0, 0)),
                  pl.BlockSpec((1, SMALL_ROWS, D), lambda k, w: (sent(k, w), 0, 0)),
                  pl.BlockSpec((1, hi, CHUNK), lambda k, w: (w[0], 0, 0)),
                  pl.BlockSpec((1, ho, D), lambda k, w: (w[0], 0, 0)),
                  const((SMALL_ROWS, D))],
        out_specs=[pl.BlockSpec((1, hi, CHUNK), lambda k, w: (w[1], 0, 0)),
                   pl.BlockSpec((1, ho, D), lambda k, w: (w[1], 0, 0)),
                   const((SMALL_ROWS, D)), const((8, D)), const((8, LANES))],
        scratch_shapes=[pltpu.VMEM((hi, CHUNK), F32), pltpu.VMEM((ho, D), F32), pltpu.VMEM((SMALL_ROWS, D), F32)])
    return pl.pallas_call(
        body, grid_spec=grid_spec, name="sum_chips",
        out_shape=[jax.ShapeDtypeStruct((2, hi, CHUNK), F32), jax.ShapeDtypeStruct((2, ho, D), F32),
                   jax.ShapeDtypeStruct((SMALL_ROWS, D), F32), jax.ShapeDtypeStruct((8, D), F32),
                   jax.ShapeDtypeStruct((8, LANES), F32)],
        compiler_params=_params(("arbitrary",)),
    )(where, ri, ro, rs, pi, po, ps)


def _exchange_results(gi2, go2, st):
    flips = [(fx, fy, fc) for fx in (0, 1) for fy in (0, 1) for fc in (0, 1)][1:]

    def body(_gi, _go, st_ref, gi_ref, go_ref, all_ref, send, recv, lsem):
        x, y, c = _pos()
        sib = (x, y, 1 - c)

        def half(k, ref, slot):
            return pltpu.make_async_remote_copy(src_ref=ref.at[slot], dst_ref=ref.at[slot], send_sem=send.at[k],
                                                recv_sem=recv.at[k], device_id=sib, device_id_type=MESH)

        def stat(k, src, slot, dev):
            return pltpu.make_async_remote_copy(src_ref=src, dst_ref=all_ref.at[slot], send_sem=send.at[k],
                                                recv_sem=recv.at[k], device_id=dev, device_id_type=MESH)

        mine = pltpu.make_async_copy(st_ref, all_ref.at[4 * x + 2 * y + c], lsem)
        mine.start()
        sends = [half(k, ref, c) for k, ref in enumerate((gi_ref, go_ref))]
        peers = [(_flip(x, fx), _flip(y, fy), _flip(c, fc)) for fx, fy, fc in flips]
        sends += [stat(2 + k, st_ref, 4 * x + 2 * y + c, dev) for k, dev in enumerate(peers)]
        for cp in sends:
            cp.start()
        for k, ref in enumerate((gi_ref, go_ref)):
            half(k, ref, 1 - c).wait_recv()
        for k, (px, py, pc) in enumerate(peers):
            slot = 4 * px + 2 * py + pc
            stat(2 + k, all_ref.at[slot], slot, (px, py, pc)).wait_recv()
        for cp in sends:
            cp.wait_send()
        mine.wait()

    n = 2 + len(flips)
    return pl.pallas_call(
        body, name="exchange_results",
        in_specs=[ANY, ANY, ANY], out_specs=[ANY, ANY, ANY], input_output_aliases={0: 0, 1: 1},
        out_shape=[jax.ShapeDtypeStruct((2, D // 2, CHUNK), F32), jax.ShapeDtypeStruct((2, WOUT_SHARD // 2, D), F32),
                   jax.ShapeDtypeStruct((NDEV, 8, D), F32)],
        scratch_shapes=[pltpu.SemaphoreType.DMA((n,)), pltpu.SemaphoreType.DMA((n,)), pltpu.SemaphoreType.DMA],
    )(gi2, go2, st)


def _adamw_math(w, g, m, v):
    m2 = ADAM_B1 * m + (1.0 - ADAM_B1) * g
    v2 = ADAM_B2 * v + (1.0 - ADAM_B2) * (g * g)
    m_hat = m2 / (1.0 - ADAM_B1 ** ADAM_STEP)
    v_hat = v2 / (1.0 - ADAM_B2 ** ADAM_STEP)
    delta = -ADAM_LR * (m_hat / (jnp.sqrt(v_hat) + ADAM_EPS) + ADAM_WD * w)
    return delta, m2, v2


def _adamw(w, g, m, v, name):
    rows, cols = w.shape
    tm = 256 if rows % 256 == 0 else rows

    def body(w_ref, g_ref, m_ref, v_ref, d_ref, m2_ref, v2_ref):
        d_ref[...], m2_ref[...], v2_ref[...] = _adamw_math(w_ref[...], g_ref[...], m_ref[...], v_ref[...])

    shape = jax.ShapeDtypeStruct(w.shape, F32)
    return pl.pallas_call(
        body, grid=(rows // tm,), name=name,
        in_specs=[_rows(tm, cols)] * 4, out_specs=[_rows(tm, cols)] * 3, out_shape=[shape] * 3,
        compiler_params=_params(("arbitrary",)),
    )(w, g, m, v)


def _adamw_vectors(g5, first_parts, ws, ms, vs):
    n = len(ws)

    def body(g_ref, parts_ref, *refs):
        ins, g0_ref, outs = refs[:3 * n], refs[3 * n], refs[3 * n + 1:]
        g0 = parts_ref[0, 0:1, :]
        for dev in range(1, NDEV):
            g0 = g0 + parts_ref[dev, 0:1, :]
        g0_ref[...] = g0
        for i in range(n):
            g = g0 if i == 0 else g_ref[i:i + 1, :]
            res = _adamw_math(ins[i][...], g, ins[n + i][...], ins[2 * n + i][...])
            for kind in range(3):
                outs[kind * n + i][...] = res[kind]

    shape = jax.ShapeDtypeStruct((1, D), F32)
    return pl.pallas_call(body, name="adamw_vectors", out_shape=[shape] * (1 + 3 * n), compiler_params=_params())(
        g5, first_parts, *ws, *ms, *vs)


def kernel(x, norm_g, w_in, conv_w, conv_b, conv_ln_g, conv_ln_b, w_out, final_norm_g, loss_target, m_norm_g, m_w_in, m_conv_w, m_conv_b, m_conv_ln_g, m_conv_ln_b, m_w_out, m_final_norm_g, v_norm_g, v_w_in, v_conv_w, v_conv_b, v_conv_ln_g, v_conv_ln_b, v_w_out, v_final_norm_g):
    chip = 2 * lax.axis_index("x") + lax.axis_index("y")
    where = jnp.stack([chip, lax.axis_index("c")]).astype(jnp.int32)
    taps_shard = jnp.pad(conv_w[0], ((0, HALO - CONV_K), (0, 0)))
    wi_full, wo_full, cw_full = _place_shards(w_in[0], w_out[0], taps_shard, where)
    wi_full = _gather_w_in(wi_full)

    gf = final_norm_g[None]
    xb = x[0]
    h, q, k, v, a_gate, c_val, c_glu, c_gate, wo_full, cw_full = _inproj_fwd(xb, norm_g, wi_full, wo_full, cw_full)
    tables = [_bias_table(d) for d in PATTERNS]
    o, lse, y_att = _attn_fwd(q, k, v, tables, a_gate)
    u, y_conv = _conv_fwd(c_val, c_glu, c_gate, cw_full, conv_b, conv_ln_g, conv_ln_b)
    dx2, dy_att, dy_conv, dw_out, st_out = _outproj_loss(y_att, y_conv, wo_full, xb, loss_target[0], gf)
    du, dc_gate, st_conv = _conv_bwd_rows(u, c_gate, dy_conv, conv_ln_g, conv_ln_b)
    dc_val, dc_glu, dconv_w = _conv_bwd_taps(du, c_val, c_glu, cw_full)

    early = [dw_out.reshape(NCHIP, WOUT_SHARD, D), jnp.concatenate([st_conv, st_out, dconv_w], axis=0)]
    po, ps = _add_halves(early, _exchange_halves(early, "exchange_halves_early"), "add_halves_early")
    do, da_gate, delta, ro, rs = _attn_gate_bwd(dy_att, o, a_gate, _head_sum_selectors(), [po, ps])
    dqs, dkvs = zip(*[_attn_bwd(q, k, v, do, lse, delta, t, d) for t, d in zip(tables, PATTERNS)])

    dproj_pieces = (dqs, dkvs, (da_gate, dc_val, dc_glu, dc_gate))
    late = [_inproj_bwd_w(h, *dproj_pieces)]
    (pi,) = _add_halves(late, _exchange_halves(late, "exchange_halves"), "add_halves")
    grad_x, st_in, ri = _inproj_bwd_x(*dproj_pieces, wi_full, xb, norm_g, dx2, pi)
    gi2, go2, g_small, g5, loss8 = _sum_chips(ri, ro, rs, pi, po, ps, where)
    gi2, go2, norm_g_parts = _exchange_results(gi2, go2, st_in)
    g_w_in = gi2.reshape(D, CHUNK)
    g_w_out = go2.reshape(WOUT_SHARD, D)
    g_taps = lax.dynamic_slice(g_small, (ROW_TAPS, chip * CONVW_SHARD), (CONV_K, CONVW_SHARD))

    d_w_in, m2_w_in, v2_w_in = _adamw(w_in[0], g_w_in, m_w_in[0], v_w_in[0], "adamw_w_in")
    d_w_out, m2_w_out, v2_w_out = _adamw(w_out[0], g_w_out, m_w_out[0], v_w_out[0], "adamw_w_out")
    d_taps, m2_taps, v2_taps = _adamw(conv_w[0], g_taps, m_conv_w[0], v_conv_w[0], "adamw_conv_w")
    g_norm, *vec = _adamw_vectors(
        g5, norm_g_parts,
        (norm_g, conv_b, conv_ln_g, conv_ln_b, gf),
        (m_norm_g, m_conv_b, m_conv_ln_g, m_conv_ln_b, m_final_norm_g[None]),
        (v_norm_g, v_conv_b, v_conv_ln_g, v_conv_ln_b, v_final_norm_g[None]))
    d_vec, m2_vec, v2_vec = vec[0:5], vec[5:10], vec[10:15]

    def weight_order(ng, wi, cw, cb, lg, lb, wo, fg):
        return (ng, wi[None], cw[None], cb, lg, lb, wo[None], fg[0])

    grads = weight_order(g_norm, g_w_in, g_taps, g5[1:2], g5[2:3], g5[3:4], g_w_out, g5[4:5])
    deltas = weight_order(d_vec[0], d_w_in, d_taps, d_vec[1], d_vec[2], d_vec[3], d_w_out, d_vec[4])
    new_m = weight_order(m2_vec[0], m2_w_in, m2_taps, m2_vec[1], m2_vec[2], m2_vec[3], m2_w_out, m2_vec[4])
    new_v = weight_order(v2_vec[0], v2_w_in, v2_taps, v2_vec[1], v2_vec[2], v2_vec[3], v2_w_out, v2_vec[4])
    return (loss8[0, 0], grad_x[None], *grads, *deltas, *new_m, *new_v)
```

```python
import jax
import jax.numpy as jnp
from jax import lax
from jax.experimental import pallas as pl
from jax.experimental.pallas import tpu as pltpu

F32 = jnp.float32
BF16 = jnp.bfloat16

S = 4096
D = 1024
LANES = 128
HD = 64
NKV = 4
GQ = 4
KVW = NKV * HD
NCOL = 5632
CONV_K = 31
HALO = 32
BLK = 128
PATTERNS = (1, 4, 16)
NORM_EPS = 1e-6
LN_EPS = 1e-5
NEG = -1e30
OFF_Q, OFF_K, OFF_AG, OFF_CV, OFF_CG, OFF_CGATE = 0, 1024, 1536, 2560, 3584, 4608
NCHIP = 4
CHUNK = NCOL // NCHIP
WOUT_ROWS = 2 * D
WOUT_SHARD = WOUT_ROWS // NCHIP
CONVW_SHARD = D // NCHIP

ADAM_LR, ADAM_B1, ADAM_B2, ADAM_EPS, ADAM_WD, ADAM_STEP = 0.001, 0.9, 0.999, 1e-08, 0.01, 10

VMEM_LIMIT = 56 * 1024 * 1024


def _params(sem=None, vmem=VMEM_LIMIT):
    return pltpu.CompilerParams(dimension_semantics=sem, vmem_limit_bytes=vmem)


def _sigmoid(a):
    return 0.5 * jnp.tanh(0.5 * a) + 0.5


def _rows(tm, width):
    return pl.BlockSpec((tm, width), lambda i: (i, 0))


def _slabs(n):
    return jax.ShapeDtypeStruct((n, S, LANES), F32)


def _slab_rows(n, tm):
    return pl.BlockSpec((n, tm, LANES), lambda i: (0, i, 0))


def _resident(shape):
    return pl.BlockSpec(shape, lambda *_: (0,) * len(shape), pipeline_mode=pl.Buffered(1))


def _dot(a, b):
    return jnp.dot(a, b, preferred_element_type=F32)


def _dot_nt(a, b):
    return lax.dot_general(a, b, (((1,), (1,)), ((), ())), preferred_element_type=F32)


def _dot_tn(a, b):
    return lax.dot_general(a, b, (((0,), (0,)), ((), ())), preferred_element_type=F32)


def _inproj_fwd(x, g1, w_bf, wo_full, cw_full):
    tm = 512
    steps = S // tm

    def body(x_ref, g_ref, w_ref, _wo, _cw, h_ref, q_ref, k_ref, v_ref, ag_ref, cv_ref, cg_ref, cgate_ref,
             wo_ref, cw_ref, send, recv):
        i = pl.program_id(0)
        stages = _gather_stages([(W_OUT, wo_ref), (TAPS, cw_ref)], send, recv)
        for stage, step in zip(stages[:3], (0, steps // 2 - 1, steps - 2)):
            pl.when(i == step)(stage)
        xt = x_ref[...]
        r = lax.rsqrt(jnp.mean(xt * xt, axis=-1, keepdims=True) + NORM_EPS)
        h = (xt * r * g_ref[...]).astype(BF16)
        h_ref[...] = h
        q = _dot(h, w_ref[:, OFF_Q:OFF_Q + D]) * (HD ** -0.5)
        kv = _dot(h, w_ref[:, OFF_K:OFF_K + 2 * KVW])
        for sl in range(D // LANES):
            q_ref[sl] = q[:, sl * LANES:(sl + 1) * LANES]
        for sl in range(KVW // LANES):
            k_ref[sl] = kv[:, sl * LANES:(sl + 1) * LANES]
            v_ref[sl] = kv[:, KVW + sl * LANES:KVW + (sl + 1) * LANES]
        ag_ref[...] = _dot(h, w_ref[:, OFF_AG:OFF_AG + D])
        cv_ref[...] = _dot(h, w_ref[:, OFF_CV:OFF_CV + D])
        cg_ref[...] = _dot(h, w_ref[:, OFF_CG:OFF_CG + D])
        cgate_ref[...] = _dot(h, w_ref[:, OFF_CGATE:OFF_CGATE + D])
        pl.when(i == steps - 1)(stages[3])

    big = jax.ShapeDtypeStruct((S, D), F32)
    return pl.pallas_call(
        body, grid=(steps,), name="inproj_fwd",
        in_specs=[_rows(tm, D), _resident((1, D)), _resident((D, NCOL)), ANY, ANY],
        out_specs=[_rows(tm, D), _slab_rows(D // LANES, tm), _slab_rows(KVW // LANES, tm), _slab_rows(KVW // LANES, tm),
                   _rows(tm, D), _rows(tm, D), _rows(tm, D), _rows(tm, D), ANY, ANY],
        out_shape=[jax.ShapeDtypeStruct((S, D), BF16), _slabs(D // LANES), _slabs(KVW // LANES), _slabs(KVW // LANES),
                   big, big, big, big,
                   jax.ShapeDtypeStruct((WOUT_ROWS, D), BF16), jax.ShapeDtypeStruct((HALO, D), F32)],
        input_output_aliases={3: 8, 4: 9},
        scratch_shapes=_gather_sems(2),
        compiler_params=_params(("arbitrary",)),
    )(x, g1, w_bf, wo_full, cw_full)


def _bias_table(d):
    h = jnp.arange(NKV * GQ, dtype=F32)
    slopes = jnp.exp2(-8.0 * (h + 1.0) / (NKV * GQ))
    qi = jnp.arange(BLK)[:, None]
    kj = jnp.arange(2 * BLK)[None, :]
    dist = BLK + qi - kj
    window = (dist >= 0) & (dist <= BLK)
    bias = -slopes[:, None, None] * (dist * d).astype(F32)[None]
    has_prev = jnp.stack([jnp.broadcast_to(kj >= BLK, (BLK, 2 * BLK)), jnp.ones((BLK, 2 * BLK), bool)])
    valid = window[None] & has_prev
    tab = jnp.where(valid[:, None], bias[None], NEG)
    return tab.reshape(2, NKV, GQ * BLK, 2 * BLK)


def _sub_rows(start, d):
    if d == 1:
        return pl.ds(pl.multiple_of(start, BLK), BLK)
    return pl.ds(start, BLK, stride=d)


CHUNK_ROWS = 2048
BLOCKS_PER_CHUNK = CHUNK_ROWS // BLK


def _low_lanes(rows=BLK):
    return lax.broadcasted_iota(jnp.int32, (rows, LANES), 1) < HD


def _block_start(idx, d):
    shift = d.bit_length() - 1
    b, r = lax.shift_right_logical(idx, shift), lax.bitwise_and(idx, d - 1)
    start = b * (BLK * d) + r
    return b, start, jnp.maximum(start - BLK * d, r)


def _stack_heads(ref, rows):
    low = _low_lanes()
    t0, t1 = ref[0, rows, :], ref[1, rows, :]
    return jnp.concatenate([jnp.where(low, t0, 0.0), jnp.where(low, 0.0, t0),
                            jnp.where(low, t1, 0.0), jnp.where(low, 0.0, t1)], axis=0).astype(BF16)


def _unstack_heads(dup):
    low = _low_lanes()
    return (jnp.where(low, dup[0:BLK], dup[BLK:2 * BLK]), jnp.where(low, dup[2 * BLK:3 * BLK], dup[3 * BLK:4 * BLK]))


def _kv_dup(ref, prow, rows, odd):
    t = jnp.concatenate([ref[0, prow, :], ref[0, rows, :]], axis=0)
    swapped = pltpu.roll(t, HD, axis=1)
    keep = jnp.logical_xor(_low_lanes(2 * BLK), odd)
    return jnp.where(keep, t, swapped).astype(BF16)


PIECES = 3


def _by_head(tiles):
    lane = lax.broadcasted_iota(jnp.int32, tiles[0].shape, 1)
    out = tiles[0]
    for g in range(1, GQ):
        out = jnp.where(lax.bitwise_and(lane, GQ - 1) == g, tiles[g], out)
    return out


def _minus_in_pieces(x):
    lane = lax.broadcasted_iota(jnp.int32, x.shape, 1)
    hi = (-x).astype(BF16).astype(F32)
    rest = -x - hi
    mid = rest.astype(BF16).astype(F32)
    lo = (rest - mid).astype(BF16).astype(F32)
    return jnp.where(lane < GQ, hi, jnp.where(lane < 2 * GQ, mid, jnp.where(lane < PIECES * GQ, lo, 0.0)))


def _attn_fwd(q, k, v, tables, a_gate):
    tm = 256
    width = GQ * HD

    lane_out = jnp.arange(LANES)[None, :] // HD
    spread_sel = jnp.stack([jnp.arange(LANES)[:, None] == 2 * half + lane_out for half in range(2)]).astype(BF16)

    def body(q_ref, k_ref, v_ref, b1_ref, b2_ref, b3_ref, ag_ref, sel_ref, o_ref, lse_ref, y_ref, op, lp):
        odd = pl.program_id(0) % 2 == 1
        chunk = pl.program_id(1)
        ones = jnp.ones((2 * BLK, LANES), BF16)

        for pat, (d, b_ref) in enumerate(zip(PATTERNS, (b1_ref, b2_ref, b3_ref))):
            def block(idx, carry, pat=pat, d=d, b_ref=b_ref):
                b, start, pstart = _block_start(chunk * BLOCKS_PER_CHUNK + idx, d)
                rows, prow = _sub_rows(start, d), _sub_rows(pstart, d)
                mine = _sub_rows(start - chunk * CHUNK_ROWS, d)
                qs = _stack_heads(q_ref, mine)
                kw = _kv_dup(k_ref, prow, rows, odd)
                vw = _kv_dup(v_ref, prow, rows, odd)
                s = _dot_nt(qs, kw) + b_ref[jnp.minimum(b, 1), 0]
                m = jnp.max(s, axis=1, keepdims=True)
                p = jnp.exp(s - m).astype(BF16)
                ol = _dot(p, jnp.concatenate([vw, ones], axis=1))
                l = ol[:, LANES:]
                op[pat, 0, mine, :], op[pat, 1, mine, :] = _unstack_heads(ol[:, :LANES] / l)
                lp[pat, mine, :] = _by_head([(m + jnp.log(l))[g * BLK:(g + 1) * BLK] for g in range(GQ)])
                return carry

            lax.fori_loop(0, BLOCKS_PER_CHUNK, block, 0, unroll=2)

        def mix(t, carry):
            r = pl.ds(pl.multiple_of(t * tm, tm), tm)
            a, b, c = lp[0, r, :], lp[1, r, :], lp[2, r, :]
            m = jnp.maximum(jnp.maximum(a, b), c)
            ea, eb, ec = jnp.exp(a - m), jnp.exp(b - m), jnp.exp(c - m)
            den = ea + eb + ec
            lse_ref[0, r, :] = _minus_in_pieces(m + jnp.log(den))
            inv = 1.0 / den
            for half in range(2):
                def spread(w):
                    hi = w.astype(BF16)
                    lo = (w - hi.astype(F32)).astype(BF16)
                    return _dot(hi, sel_ref[half]) + _dot(lo, sel_ref[half])

                o = (spread(ea * inv) * op[0, half, r, :] + spread(eb * inv) * op[1, half, r, :]
                     + spread(ec * inv) * op[2, half, r, :])
                o_ref[half, r, :] = o
                cols = slice(half * LANES, (half + 1) * LANES)
                ag = ag_ref[r, cols]
                y_ref[r, cols] = (o * (ag * _sigmoid(ag))).astype(BF16)
            return carry

        lax.fori_loop(0, CHUNK_ROWS // tm, mix, 0)

    q_like = pl.BlockSpec((2, CHUNK_ROWS, LANES), lambda j, c: (j, c, 0))
    per_kv = pl.BlockSpec((1, CHUNK_ROWS, LANES), lambda j, c: (j, c, 0))
    kv = pl.BlockSpec((1, S, LANES), lambda j, c: (j // 2, 0, 0))
    bias_spec = pl.BlockSpec((2, 1, GQ * BLK, 2 * BLK), lambda j, c: (0, j, 0, 0))
    group_cols = pl.BlockSpec((CHUNK_ROWS, width), lambda j, c: (c, j))
    return pl.pallas_call(
        body, grid=(NKV, S // CHUNK_ROWS), name="attn_fwd",
        in_specs=[q_like, kv, kv, bias_spec, bias_spec, bias_spec, group_cols,
                  pl.BlockSpec((2, LANES, LANES), lambda j, c: (0, 0, 0))],
        out_specs=[q_like, per_kv, group_cols],
        out_shape=[_slabs(D // LANES), _slabs(NKV), jax.ShapeDtypeStruct((S, D), BF16)],
        scratch_shapes=[pltpu.VMEM((len(PATTERNS), 2, CHUNK_ROWS, LANES), F32),
                        pltpu.VMEM((len(PATTERNS), CHUNK_ROWS, LANES), F32)],
        compiler_params=_params(("arbitrary", "arbitrary")),
    )(q, k, v, *tables, a_gate, spread_sel)


def _head_sum_selectors():
    lane_in = jnp.arange(LANES)[:, None] // HD
    return jnp.stack([jnp.broadcast_to(lane_in == h, (LANES, LANES)) for h in range(2)]).astype(BF16)


def _attn_gate_bwd(dy_att, o, a_gate, selectors, chip_sums):
    tm = 256
    last = S // tm - 1
    landing, sems = _exchange_results_of(chip_sums)
    n_sums = len(chip_sums)

    def body(dy_ref, o_ref, ag_ref, e_ref, *refs):
        sums, (do_ref, dag_ref, delta_ref), refs = refs[:n_sums], refs[n_sums:n_sums + 3], refs[n_sums + 3:]
        landed, (send, recv) = refs[:n_sums], refs[n_sums:]
        i = pl.program_id(0)
        copies = _chip_exchange_copies(sums, landed, send, recv)
        _start_exchange(copies, i == 0)
        for j in range(NKV):
            deltas = []
            for sl in (2 * j, 2 * j + 1):
                cols = slice(sl * LANES, (sl + 1) * LANES)
                dy, ag, o_ = dy_ref[:, cols], ag_ref[:, cols], o_ref[sl]
                sg = _sigmoid(ag)
                do = dy * (ag * sg)
                do_ref[sl] = do
                dag_ref[:, cols] = (dy * o_ * (sg * (1.0 + ag * (1.0 - sg)))).astype(BF16)
                prod = do * o_
                hi = prod.astype(BF16)
                lo = (prod - hi.astype(F32)).astype(BF16)
                deltas += [_dot(hi, e_ref[h]) + _dot(lo, e_ref[h]) for h in range(2)]
            delta_ref[j] = _minus_in_pieces(_by_head(deltas))
        _finish_exchange(copies, i == last)

    return pl.pallas_call(
        body, grid=(S // tm,), name="attn_gate_bwd",
        in_specs=[_rows(tm, D), _slab_rows(D // LANES, tm), _rows(tm, D), _resident((2, LANES, LANES))] + [ANY] * n_sums,
        out_specs=[_slab_rows(D // LANES, tm), _rows(tm, D), _slab_rows(NKV, tm)] + [ANY] * n_sums,
        out_shape=[_slabs(D // LANES), jax.ShapeDtypeStruct((S, D), BF16), _slabs(NKV)] + landing,
        scratch_shapes=sems,
        compiler_params=_params(("arbitrary",)),
    )(dy_att, o, a_gate, selectors, *chip_sums)


def _own_pieces(tile):
    lane = lax.broadcasted_iota(jnp.int32, tile.shape, 1)
    head = jnp.where(lane < PIECES * GQ, lax.bitwise_and(lane, GQ - 1), -1)
    return jnp.concatenate([jnp.where(head == g, tile, 0.0) for g in range(GQ)], axis=0).astype(BF16)


def _attn_bwd(q, k, v, do, lse, delta, bias, d):
    def body(q_ref, do_ref, l_ref, dl_ref, k_ref, v_ref, b_ref, dq_ref, dkv_ref, acc):
        odd = pl.program_id(0) % 2 == 1
        chunk = pl.program_id(1)
        ones = (lax.broadcasted_iota(jnp.int32, (2 * BLK, LANES), 1) < PIECES * GQ).astype(BF16)

        def in_acc(block_idx):
            return pl.ds(pl.multiple_of(block_idx * BLK, BLK), BLK)

        @pl.when(chunk == 0)
        def _():
            acc[...] = jnp.zeros_like(acc)

        def block(idx, carry):
            idx = chunk * BLOCKS_PER_CHUNK + idx
            b, start, pstart = _block_start(idx, d)
            rows, prow = _sub_rows(start, d), _sub_rows(pstart, d)
            mine = _sub_rows(start - chunk * CHUNK_ROWS, d)
            qs = _stack_heads(q_ref, mine)
            dos = _stack_heads(do_ref, mine)
            kw = _kv_dup(k_ref, prow, rows, odd)
            vw = _kv_dup(v_ref, prow, rows, odd)
            s = _dot_nt(jnp.concatenate([qs, _own_pieces(l_ref[0, mine, :])], axis=1),
                        jnp.concatenate([kw, ones], axis=1)) + b_ref[jnp.minimum(b, 1), 0]
            p = jnp.exp(s)
            dv2 = _dot_tn(p.astype(BF16), dos)
            dp = _dot_nt(jnp.concatenate([dos, _own_pieces(dl_ref[0, mine, :])], axis=1),
                         jnp.concatenate([vw, ones], axis=1))
            ds = (p * dp).astype(BF16)
            dq_ref[0, mine, :], dq_ref[1, mine, :] = _unstack_heads(_dot(ds, kw))
            dk2 = _dot_tn(ds, qs)
            dkv = jnp.where(_low_lanes(2 * BLK), dk2 + pltpu.roll(dk2, HD, axis=1), dv2 + pltpu.roll(dv2, HD, axis=1))
            acc[in_acc(idx), :] = acc[in_acc(idx), :] + dkv[BLK:]
            before = jnp.where(b >= 1, idx - d, idx)
            acc[in_acc(before), :] = acc[in_acc(before), :] + dkv[:BLK]
            return carry

        lax.fori_loop(0, BLOCKS_PER_CHUNK, block, 0, unroll=8)

        @pl.when(chunk == S // CHUNK_ROWS - 1)
        def _():
            def place(idx, carry):
                _, start, _ = _block_start(idx, d)
                dkv_ref[0, _sub_rows(start, d), :] = acc[in_acc(idx), :]
                return carry

            lax.fori_loop(0, S // BLK, place, 0, unroll=4)

    q_like = pl.BlockSpec((2, CHUNK_ROWS, LANES), lambda j, c: (j, c, 0))
    pieces = pl.BlockSpec((1, CHUNK_ROWS, LANES), lambda j, c: (j, c, 0))
    kv = pl.BlockSpec((1, S, LANES), lambda j, c: (j // 2, 0, 0))
    per_kv = pl.BlockSpec((1, S, LANES), lambda j, c: (j, 0, 0))
    bias_spec = pl.BlockSpec((2, 1, GQ * BLK, 2 * BLK), lambda j, c: (0, j, 0, 0))
    return pl.pallas_call(
        body, grid=(NKV, S // CHUNK_ROWS), name=f"attn_bwd_d{d}",
        in_specs=[q_like, q_like, pieces, pieces, kv, kv, bias_spec],
        out_specs=[q_like, per_kv],
        out_shape=[_slabs(D // LANES), _slabs(NKV)],
        scratch_shapes=[pltpu.VMEM((S, LANES), F32)],
        compiler_params=_params(("arbitrary", "arbitrary")),
    )(q, do, lse, delta, k, v, bias)


CONV_T = 256


def _halo_before(i):
    return (jnp.maximum(i * (CONV_T // HALO) - 1, 0), 0)


def _halo_after(i):
    return (jnp.minimum((i + 1) * (CONV_T // HALO), S // HALO - 1), 0)


SUBLANES = 8
NCH = D // LANES
GROUP = SUBLANES * SUBLANES


def _comb(ref, cb, base):
    return ref[cb, pl.ds(base, SUBLANES, stride=SUBLANES), :]


def _taps(w_ref, cols):
    return [jnp.broadcast_to(w_ref[j:j + 1, cols], (SUBLANES, LANES)) for j in range(CONV_K)]


def _conv_fwd(c_val, c_glu, c_gate, conv_w, conv_b, ln_g, ln_b):
    T = CONV_T

    def body(cv_ref, cg_ref, cvh_ref, cgh_ref, gate_ref, w_ref, b_ref, lg_ref, lb_ref, u_ref, y_ref, win, us):
        i = pl.program_id(0)
        for cb in range(NCH):
            cols = slice(cb * LANES, (cb + 1) * LANES)
            win[cb, HALO:HALO + T, :] = cv_ref[:, cols] * _sigmoid(cg_ref[:, cols])
            win[cb, 0:HALO, :] = jnp.where(i > 0, cvh_ref[:, cols] * _sigmoid(cgh_ref[:, cols]), 0.0)
        for cb in range(NCH):
            cols = slice(cb * LANES, (cb + 1) * LANES)
            taps = _taps(w_ref, cols)
            bias = jnp.broadcast_to(b_ref[:, cols], (SUBLANES, LANES))

            def group(g, carry):
                for b in range(SUBLANES):
                    base = g * GROUP + b
                    acc = bias
                    for j in range(CONV_K):
                        acc = acc + taps[j] * _comb(win, cb, base + (HALO - (CONV_K - 1) + j))
                    us[cb, pl.ds(base, SUBLANES, stride=SUBLANES), :] = acc
                return carry

            lax.fori_loop(0, T // GROUP, group, 0)
        total = us[0]
        for cb in range(1, NCH):
            total = total + us[cb]
        mu = jnp.sum(total, axis=-1, keepdims=True) * (1.0 / D)
        sq = jnp.zeros((T, LANES), F32)
        for cb in range(NCH):
            uc = us[cb] - mu
            sq = sq + uc * uc
        rstd = lax.rsqrt(jnp.sum(sq, axis=-1, keepdims=True) * (1.0 / D) + LN_EPS)
        for cb in range(NCH):
            cols = slice(cb * LANES, (cb + 1) * LANES)
            u = us[cb]
            u_ref[:, cols] = u
            nrm = (u - mu) * rstd * lg_ref[:, cols] + lb_ref[:, cols]
            gate = gate_ref[:, cols]
            y_ref[:, cols] = (nrm * _sigmoid(nrm) * (gate * _sigmoid(gate))).astype(BF16)

    halo = pl.BlockSpec((HALO, D), _halo_before)
    return pl.pallas_call(
        body, grid=(S // T,), name="conv_fwd",
        in_specs=[_rows(T, D), _rows(T, D), halo, halo, _rows(T, D),
                  _resident((HALO, D)), _resident((1, D)), _resident((1, D)), _resident((1, D))],
        out_specs=[_rows(T, D), _rows(T, D)],
        out_shape=[jax.ShapeDtypeStruct((S, D), F32), jax.ShapeDtypeStruct((S, D), BF16)],
        scratch_shapes=[pltpu.VMEM((NCH, T + HALO, LANES), F32), pltpu.VMEM((NCH, T, LANES), F32)],
        compiler_params=_params(("arbitrary",)),
    )(c_val, c_glu, c_val, c_glu, c_gate, conv_w, conv_b, ln_g, ln_b)


def _conv_bwd_taps(du, c_val, c_glu, conv_w):
    T = CONV_T
    last = S // T - 1

    def body(du_ref, dua_ref, cv_ref, cg_ref, cvh_ref, cgh_ref, w_ref, dcv_ref, dcg_ref, dw_ref,
             hwin, dwin, dhs, dw_acc):
        i = pl.program_id(0)

        @pl.when(i == 0)
        def _():
            dw_acc[...] = jnp.zeros_like(dw_acc)

        for cb in range(NCH):
            cols = slice(cb * LANES, (cb + 1) * LANES)
            hwin[cb, HALO:HALO + T, :] = cv_ref[:, cols] * _sigmoid(cg_ref[:, cols])
            hwin[cb, 0:HALO, :] = jnp.where(i > 0, cvh_ref[:, cols] * _sigmoid(cgh_ref[:, cols]), 0.0)
            dwin[cb, 0:T, :] = du_ref[:, cols]
            dwin[cb, T:T + HALO, :] = jnp.where(i < last, dua_ref[:, cols], 0.0)
        for cb in range(NCH):
            cols = slice(cb * LANES, (cb + 1) * LANES)
            taps = _taps(w_ref, cols)

            def group_dh(g, carry):
                for b in range(SUBLANES):
                    base = g * GROUP + b
                    acc = jnp.zeros((SUBLANES, LANES), F32)
                    for j in range(CONV_K):
                        acc = acc + taps[j] * _comb(dwin, cb, base + (CONV_K - 1 - j))
                    dhs[cb, pl.ds(base, SUBLANES, stride=SUBLANES), :] = acc
                return carry

            lax.fori_loop(0, T // GROUP, group_dh, 0)

            def group_dw(g, sums):
                for b in range(SUBLANES):
                    base = g * GROUP + b
                    d = _comb(dwin, cb, base)
                    sums = tuple(sums[j] + d * _comb(hwin, cb, base + (HALO - (CONV_K - 1) + j))
                                 for j in range(CONV_K))
                return sums

            sums = lax.fori_loop(0, T // GROUP, group_dw, tuple(dw_acc[j, :, cols] for j in range(CONV_K)))
            for j in range(CONV_K):
                dw_acc[j, :, cols] = sums[j]
            dh = dhs[cb]
            cv, sg = cv_ref[:, cols], _sigmoid(cg_ref[:, cols])
            dcv_ref[:, cols] = (dh * sg).astype(BF16)
            dcg_ref[:, cols] = (dh * cv * (sg * (1.0 - sg))).astype(BF16)

        @pl.when(i == last)
        def _():
            dw_ref[...] = jnp.zeros_like(dw_ref)
            for j in range(CONV_K):
                dw_ref[j:j + 1, :] = jnp.sum(dw_acc[j], axis=0, keepdims=True)

    before = pl.BlockSpec((HALO, D), _halo_before)
    after = pl.BlockSpec((HALO, D), _halo_after)
    big = jax.ShapeDtypeStruct((S, D), BF16)
    return pl.pallas_call(
        body, grid=(S // T,), name="conv_bwd_taps",
        in_specs=[_rows(T, D), after, _rows(T, D), _rows(T, D), before, before, _resident((HALO, D))],
        out_specs=[_rows(T, D), _rows(T, D), pl.BlockSpec((HALO, D), lambda i: (0, 0))],
        out_shape=[big, big, jax.ShapeDtypeStruct((HALO, D), F32)],
        scratch_shapes=[pltpu.VMEM((NCH, T + HALO, LANES), F32), pltpu.VMEM((NCH, T + HALO, LANES), F32),
                        pltpu.VMEM((NCH, T, LANES), F32), pltpu.VMEM((CONV_K, SUBLANES, D), F32)],
        compiler_params=_params(("arbitrary",)),
    )(du, du, c_val, c_glu, c_val, c_glu, conv_w)


def _outproj_loss(y_att, y_conv, w_out_bf, x, target, gf, u, c_gate, ln_g, ln_b):
    tm = 256

    def body(ya_ref, yc_ref, w_ref, x_ref, t_ref, gf_ref, u_ref, gate_ref, lg_ref, lb_ref,
             dx2_ref, dya_ref, du_ref, dgate_ref, dw_ref, st_ref, acc):
        @pl.when(pl.program_id(0) == 0)
        def _():
            acc[...] = jnp.zeros_like(acc)
            st_ref[...] = jnp.zeros_like(st_ref)

        ya, yc = ya_ref[...], yc_ref[...]
        x2 = x_ref[...] + _dot(ya, w_ref[0:D, :]) + _dot(yc, w_ref[D:2 * D, :])
        r = lax.rsqrt(jnp.mean(x2 * x2, axis=-1, keepdims=True) + NORM_EPS)
        xn = x2 * r
        err = xn * gf_ref[...] - t_ref[...]
        dout = err * (1.0 / D)
        dxn = dout * gf_ref[...]
        dx2 = r * (dxn - xn * jnp.mean(dxn * xn, axis=-1, keepdims=True))
        dx2_ref[...] = dx2
        dx2b = dx2.astype(BF16)
        dya_ref[...] = _dot_nt(dx2b, w_ref[0:D, :])
        dy = _dot_nt(dx2b, w_ref[D:2 * D, :])
        acc[0:D, :] += _dot_tn(ya, dx2b)
        acc[D:2 * D, :] += _dot_tn(yc, dx2b)
        st_ref[ROW_FINAL_G:ROW_FINAL_G + 1, :] += jnp.sum(dout * xn, axis=0, keepdims=True)
        st_ref[ROW_LOSS:ROW_LOSS + 1, :] += jnp.sum(err * err, axis=0, keepdims=True) * (0.5 / D)

        u, gate = u_ref[...], gate_ref[...]
        mu = jnp.mean(u, axis=-1, keepdims=True)
        uc = u - mu
        rstd = lax.rsqrt(jnp.mean(uc * uc, axis=-1, keepdims=True) + LN_EPS)
        z = uc * rstd
        nrm = z * lg_ref[...] + lb_ref[...]
        sn, sg = _sigmoid(nrm), _sigmoid(gate)
        dgate_ref[...] = (dy * (nrm * sn) * (sg * (1.0 + gate * (1.0 - sg)))).astype(BF16)
        dn = dy * (gate * sg) * (sn * (1.0 + nrm * (1.0 - sn)))
        dz = dn * lg_ref[...]
        du = rstd * (dz - jnp.mean(dz, axis=-1, keepdims=True) - z * jnp.mean(dz * z, axis=-1, keepdims=True))
        du_ref[...] = du
        st_ref[ROW_LN_G:ROW_LN_G + 1, :] += jnp.sum(dn * z, axis=0, keepdims=True)
        st_ref[ROW_LN_B:ROW_LN_B + 1, :] += jnp.sum(dn, axis=0, keepdims=True)
        st_ref[ROW_CONV_B:ROW_CONV_B + 1, :] += jnp.sum(du, axis=0, keepdims=True)

        @pl.when(pl.program_id(0) == S // tm - 1)
        def _():
            dw_ref[...] = acc[...].astype(BF16)

    big = jax.ShapeDtypeStruct((S, D), F32)
    vec = _resident((1, D))
    return pl.pallas_call(
        body, grid=(S // tm,), name="outproj_loss",
        in_specs=[_rows(tm, D), _rows(tm, D), _resident((WOUT_ROWS, D)), _rows(tm, D), _rows(tm, D), vec,
                  _rows(tm, D), _rows(tm, D), vec, vec],
        out_specs=[_rows(tm, D), _rows(tm, D), _rows(tm, D), _rows(tm, D),
                   pl.BlockSpec((WOUT_ROWS, D), lambda i: (0, 0)), pl.BlockSpec((8, D), lambda i: (0, 0))],
        out_shape=[big, big, big, jax.ShapeDtypeStruct((S, D), BF16),
                   jax.ShapeDtypeStruct((WOUT_ROWS, D), BF16), jax.ShapeDtypeStruct((8, D), F32)],
        scratch_shapes=[pltpu.VMEM((WOUT_ROWS, D), F32)],
        compiler_params=_params(("arbitrary",)),
    )(y_att, y_conv, w_out_bf, x, target, gf, u, c_gate, ln_g, ln_b)


UNITS_PER_CHUNK = CHUNK // LANES


def _dproj_unit(u, dqs, dkvs, gates, rows):
    if u < OFF_K // LANES:
        return ((dqs[0][u] + dqs[1][u] + dqs[2][u]) * (HD ** -0.5)).astype(BF16)
    if u < OFF_AG // LANES:
        w = u - OFF_K // LANES
        ta, tb = (dkvs[0][j] + dkvs[1][j] + dkvs[2][j] for j in (2 * (w % 2), 2 * (w % 2) + 1))
        low = _low_lanes(rows)
        if w < 2:
            return jnp.where(low, ta, pltpu.roll(tb, HD, axis=1)).astype(BF16)
        return jnp.where(low, pltpu.roll(ta, HD, axis=1), tb).astype(BF16)
    g, sl = divmod(u - OFF_AG // LANES, D // LANES)
    return gates[g][:, sl * LANES:(sl + 1) * LANES]


def _dproj_sources(units, dqs, dkvs, gates, rows):
    use_q = any(u < OFF_K // LANES for u in units)
    use_kv = any(OFF_K // LANES <= u < OFF_AG // LANES for u in units)
    use_g = sorted({(u - OFF_AG // LANES) // (D // LANES) for u in units if u >= OFF_AG // LANES})
    args = (list(dqs) if use_q else []) + (list(dkvs) if use_kv else []) + [gates[g] for g in use_g]
    specs = ([_slab_rows(D // LANES, rows)] * 3 if use_q else []) + ([_slab_rows(NKV, rows)] * 3 if use_kv else []) \
        + [_rows(rows, D)] * len(use_g)

    def pick(refs):
        refs = list(refs)
        q_refs = [refs.pop(0) for _ in range(3)] if use_q else None
        kv_refs = [refs.pop(0) for _ in range(3)] if use_kv else None
        return q_refs, kv_refs, {g: refs.pop(0) for g in use_g}

    return args, specs, pick


def _exchange_results_of(chip_sums):
    n = len(chip_sums) * len(CHIP_FLIPS)
    shapes = [jax.ShapeDtypeStruct((NCHIP,) + tuple(a.shape[1:] if a.ndim == 3 else a.shape), a.dtype)
              for a in chip_sums]
    return shapes, [pltpu.SemaphoreType.DMA((n,)), pltpu.SemaphoreType.DMA((n,))]


def _start_exchange(copies, first_step):
    @pl.when(first_step)
    def _():
        for out, _ in copies:
            out.start()


def _finish_exchange(copies, last_step):
    @pl.when(last_step)
    def _():
        for _, arrival in copies:
            arrival.wait_recv()
        for out, _ in copies:
            out.wait_send()


def _inproj_bwd_x(dqs, dkvs, gates, w_bf, x, g1, dx2, pi):
    tm = 256
    last = S // tm - 1
    units = range(NCOL // LANES)
    pieces, piece_specs, pick = _dproj_sources(units, dqs, dkvs, gates, tm)
    landing, sems = _exchange_results_of([pi])

    def body(*refs):
        piece_refs, refs = refs[:len(pieces)], refs[len(pieces):]
        w_ref, x_ref, g_ref, dx2_ref, pi_ref, gx_ref, st_ref, ri_ref, dp_ref, send, recv = refs
        i = pl.program_id(0)
        copies = _chip_exchange_copies([pi_ref], [ri_ref], send, recv)
        _start_exchange(copies, i == 0)

        @pl.when(i == 0)
        def _():
            st_ref[...] = jnp.zeros_like(st_ref)

        sources = pick(piece_refs)
        for u in units:
            dp_ref[:, u * LANES:(u + 1) * LANES] = _dproj_unit(u, *sources, tm)
        dh = _dot_nt(dp_ref[...], w_ref[...])
        xt = x_ref[...]
        r = lax.rsqrt(jnp.mean(xt * xt, axis=-1, keepdims=True) + NORM_EPS)
        xn = xt * r
        dxn = dh * g_ref[...]
        gx_ref[...] = dx2_ref[...] + r * (dxn - xn * jnp.mean(dxn * xn, axis=-1, keepdims=True))
        st_ref[0:1, :] += jnp.sum(dh * xn, axis=0, keepdims=True)
        _finish_exchange(copies, i == last)

    return pl.pallas_call(
        body, grid=(S // tm,), name="inproj_bwd_x",
        in_specs=piece_specs + [_resident((D, NCOL)), _rows(tm, D), _resident((1, D)), _rows(tm, D), ANY],
        out_specs=[_rows(tm, D), pl.BlockSpec((8, D), lambda i: (0, 0)), ANY],
        out_shape=[jax.ShapeDtypeStruct((S, D), F32), jax.ShapeDtypeStruct((8, D), F32)] + landing,
        scratch_shapes=[pltpu.VMEM((tm, NCOL), BF16)] + sems,
        compiler_params=_params(("arbitrary",)),
    )(*pieces, w_bf, x, g1, dx2, pi)


def _inproj_bwd_w(h, dqs, dkvs, gates):
    out = None
    for k in range(NCHIP):
        units = range(k * UNITS_PER_CHUNK, (k + 1) * UNITS_PER_CHUNK)
        tk = 512 if units[0] < OFF_K // LANES else 1024
        nk = S // tk
        pieces, piece_specs, pick = _dproj_sources(units, dqs, dkvs, gates, tk)
        handed_on = [] if out is None else [out]

        def body(*refs, units=units, pick=pick, n_pieces=len(pieces), n_in=1 + len(pieces) + len(handed_on)):
            h_ref, piece_refs = refs[0], refs[1:1 + n_pieces]
            o_ref, tile, acc = refs[n_in:]
            i = pl.program_id(0)

            @pl.when(i == 0)
            def _():
                acc[...] = jnp.zeros_like(acc)

            sources = pick(piece_refs)
            for n, u in enumerate(units):
                tile[:, n * LANES:(n + 1) * LANES] = _dproj_unit(u, *sources, tk)
            acc[...] += _dot_tn(h_ref[...], tile[...])

            @pl.when(i == nk - 1)
            def _():
                o_ref[0] = acc[...].astype(BF16)

        out = pl.pallas_call(
            body, grid=(nk,), name=f"inproj_bwd_w{k}",
            in_specs=[_rows(tk, D)] + piece_specs + [ANY] * len(handed_on),
            out_specs=pl.BlockSpec((1, D, CHUNK), lambda i, k=k: (k, 0, 0)),
            out_shape=jax.ShapeDtypeStruct((NCHIP, D, CHUNK), BF16),
            input_output_aliases={1 + len(pieces): 0} if handed_on else {},
            scratch_shapes=[pltpu.VMEM((tk, CHUNK), BF16), pltpu.VMEM((D, CHUNK), F32)],
            compiler_params=_params(("arbitrary",)),
        )(h, *pieces, *handed_on)
    return out


ROW_FINAL_G, ROW_LOSS, ROW_LN_G, ROW_LN_B, ROW_CONV_B, ROW_TAPS = 0, 1, 2, 3, 4, 8
SMALL_ROWS = 8 + HALO
NDEV = 8


MESH = pl.DeviceIdType.MESH
ANY = pl.BlockSpec(memory_space=pl.ANY)
CHIP_FLIPS = ((1, 0), (0, 1), (1, 1))


def _pos():
    return lax.axis_index("x"), lax.axis_index("y"), lax.axis_index("c")


def _flip(v, f):
    return 1 - v if f else v


def _ds(start, size, align=None):
    return pl.ds(pl.multiple_of(start, align or size), size)


def _place_shards(wi, wo, cw, where):
    steps = 4

    def body(where_ref, wi_ref, wo_ref, cw_ref, wi_full, wo_full, cw_full):
        wi_full[...] = wi_ref[...].astype(BF16)
        wo_full[...] = wo_ref[...].astype(BF16)
        cw_full[...] = cw_ref[...]

    grid_spec = pltpu.PrefetchScalarGridSpec(
        num_scalar_prefetch=1, grid=(steps,),
        in_specs=[pl.BlockSpec((D // steps, CHUNK), lambda i, w: (i, 0)),
                  pl.BlockSpec((WOUT_SHARD // steps, D), lambda i, w: (i, 0)),
                  pl.BlockSpec((HALO, CONVW_SHARD), lambda i, w: (0, 0))],
        out_specs=[pl.BlockSpec((D // steps, CHUNK), lambda i, w: (i, w[0])),
                   pl.BlockSpec((WOUT_SHARD // steps, D), lambda i, w: (w[0] * steps + i, 0)),
                   pl.BlockSpec((HALO, CONVW_SHARD), lambda i, w: (0, w[0]))])
    return pl.pallas_call(
        body, grid_spec=grid_spec, name="place_shards",
        out_shape=[jax.ShapeDtypeStruct((D, NCOL), BF16), jax.ShapeDtypeStruct((WOUT_ROWS, D), BF16),
                   jax.ShapeDtypeStruct((HALO, D), F32)],
        compiler_params=_params(("arbitrary",)),
    )(where, wi, wo, cw)


W_IN, W_OUT, TAPS = range(3)
GATHER_SEMS = 12


def _gather_stages(fulls, send, recv):
    halves = {W_IN: D // 2, W_OUT: WOUT_SHARD // 2, TAPS: HALO // 2}
    x, y, c = _pos()
    chips = {"me": (x, y), "x": (1 - x, y), "y": (x, 1 - y), "diag": (1 - x, 1 - y)}
    SENT = ((("me", 0), "x"), (("me", 1), "x"), (("me", 1), "y"), (("me", 0), "y"), (("x", 0), "y"), (("y", 1), "x"))
    LANDS = ((("x", 0), "x"), (("x", 1), "x"), (("y", 1), "y"), (("y", 0), "y"), (("diag", 0), "y"), (("diag", 1), "x"))
    N_ICI = len(SENT)

    def region(n_th, whose, half, part):
        a, full = fulls[n_th]
        chip = 2 * chips[whose][0] + chips[whose][1]
        n = halves[a] // 2
        row = half * halves[a] + part * n
        if a == W_IN:
            return full.at[_ds(row, n), _ds(chip * CHUNK, CHUNK, 128)]
        if a == W_OUT:
            return full.at[_ds(chip * WOUT_SHARD + row, n), :]
        return full.at[_ds(row, n), _ds(chip * CONVW_SHARD, CONVW_SHARD, 128)]

    def copy(n_th, kind, piece, dev):
        k = GATHER_SEMS * n_th + kind
        return pltpu.make_async_remote_copy(src_ref=piece, dst_ref=piece, send_sem=send.at[k], recv_sem=recv.at[k],
                                            device_id=dev, device_id_type=MESH)

    def sent(a, k):
        if k < N_ICI:
            (whose, part), to = SENT[k]
            return copy(a, k, region(a, whose, c, part), (*chips[to], c))
        (whose, part), _ = LANDS[k - N_ICI]
        return copy(a, k, region(a, whose, c, part), (x, y, 1 - c))

    def wait_arrival(a, k):
        if k < N_ICI:
            (whose, part), frm = LANDS[k]
            copy(a, k, region(a, whose, c, part), (*chips[frm], c)).wait_recv()
        else:
            (whose, part), _ = LANDS[k - N_ICI]
            copy(a, k, region(a, whose, 1 - c, part), (x, y, 1 - c)).wait_recv()

    arrays = range(len(fulls))

    def own_to_neighbours():
        for a in arrays:
            for k in (0, 2, 1, 3):
                sent(a, k).start()

    def pass_on_neighbours():
        for a in arrays:
            for k, onward in ((0, 4), (2, 5), (1, None), (3, None)):
                wait_arrival(a, k)
                if onward is not None:
                    sent(a, onward).start()
                sent(a, k + N_ICI).start()

    def pass_on_diagonal():
        for a in arrays:
            for k in (4, 5):
                wait_arrival(a, k)
                sent(a, k + N_ICI).start()

    def finish():
        for a in arrays:
            for k in range(N_ICI, 2 * N_ICI):
                wait_arrival(a, k)
            for k in range(2 * N_ICI):
                sent(a, k).wait_send()

    return own_to_neighbours, pass_on_neighbours, pass_on_diagonal, finish


def _gather_sems(n_arrays):
    return [pltpu.SemaphoreType.DMA((GATHER_SEMS * n_arrays,)), pltpu.SemaphoreType.DMA((GATHER_SEMS * n_arrays,))]


def _gather_w_in(wi_full):
    def body(_wi, full, send, recv):
        for stage in _gather_stages([(W_IN, full)], send, recv):
            stage()

    return pl.pallas_call(
        body, name="gather_w_in", in_specs=[ANY], out_specs=ANY, input_output_aliases={0: 0},
        out_shape=jax.ShapeDtypeStruct((D, NCOL), BF16), scratch_shapes=_gather_sems(1),
    )(wi_full)


def _half_shape(a):
    return jax.ShapeDtypeStruct((NCHIP, a.shape[1] // 2, a.shape[2]) if a.ndim == 3 else a.shape, a.dtype)


def _exchange_halves(arrays, name):
    n = len(arrays)

    def body(*refs):
        srcs, dsts, (send, recv) = refs[:n], refs[n:2 * n], refs[2 * n:]
        x, y, c = _pos()
        cps = []
        for k, (s_, d_) in enumerate(zip(srcs, dsts)):
            if len(s_.shape) == 3:
                h = s_.shape[1] // 2
                s_ = s_.at[:, _ds((1 - c) * h, h), :]
            cps.append(pltpu.make_async_remote_copy(src_ref=s_, dst_ref=d_, send_sem=send.at[k], recv_sem=recv.at[k],
                                                    device_id=(x, y, 1 - c), device_id_type=MESH))
        for cp in cps:
            cp.start()
        for cp in cps:
            cp.wait()

    return pl.pallas_call(
        body, name=name, in_specs=[ANY] * n, out_specs=[ANY] * n, out_shape=[_half_shape(a) for a in arrays],
        scratch_shapes=[pltpu.SemaphoreType.DMA((n,)), pltpu.SemaphoreType.DMA((n,))],
    )(*arrays)


def _add_halves(arrays, received, name):
    n = len(arrays)

    def body(*refs):
        mine, theirs, outs = refs[:n], refs[n:2 * n], refs[2 * n:]
        c = lax.axis_index("c")
        for m_, t_, o_ in zip(mine, theirs, outs):
            if len(m_.shape) == 3:
                h = m_.shape[1] // 2
                o_[0] = (m_[0, _ds(c * h, h), :].astype(F32) + t_[0].astype(F32)).astype(o_.dtype)
            else:
                o_[...] = m_[...] + t_[...]

    def spec(shape):
        if len(shape) == 3:
            return pl.BlockSpec((1,) + tuple(shape[1:]), lambda k: (k, 0, 0))
        return pl.BlockSpec(tuple(shape), lambda k: (0, 0))

    halves = [_half_shape(a) for a in arrays]
    return pl.pallas_call(
        body, grid=(NCHIP,), name=name,
        in_specs=[spec(a.shape) for a in arrays] + [spec(h.shape) for h in halves],
        out_specs=[spec(h.shape) for h in halves], out_shape=halves,
        compiler_params=_params(("arbitrary",)),
    )(*arrays, *received)


def _chip_exchange_copies(srcs, dsts, send, recv):
    x, y, c = _pos()
    me = 2 * x + y
    pairs = []
    for a in range(len(srcs)):
        for j, (fx, fy) in enumerate(CHIP_FLIPS):
            px, py = _flip(x, fx), _flip(y, fy)
            peer = 2 * px + py
            k = len(CHIP_FLIPS) * a + j
            out = pltpu.make_async_remote_copy(
                src_ref=srcs[a].at[peer] if len(srcs[a].shape) == 3 else srcs[a], dst_ref=dsts[a].at[me],
                send_sem=send.at[k], recv_sem=recv.at[k], device_id=(px, py, c), device_id_type=MESH)
            got = dsts[a].at[peer]
            arrival = pltpu.make_async_remote_copy(
                src_ref=got, dst_ref=got, send_sem=send.at[k], recv_sem=recv.at[k],
                device_id=(px, py, c), device_id_type=MESH)
            pairs.append((out, arrival))
    return pairs


def _sum_chips(ri, ro, rs, pi, po, ps, where):
    def body(w_ref, ri_ref, ro_ref, rs_ref, pi_ref, po_ref, ps_ref, gi_ref, go_ref, gs_ref, g5_ref, loss_ref,
             acc_i, acc_o, acc_s):
        k = pl.program_id(0)
        accs = (acc_i, acc_o, acc_s)

        @pl.when(k == 0)
        def _():
            for acc in accs:
                acc[...] = jnp.zeros_like(acc)

        @pl.when(k == w_ref[0])
        def _():
            for acc, val in zip(accs, (pi_ref[0], po_ref[0], ps_ref[...])):
                acc[...] += val.astype(F32)

        @pl.when(k != w_ref[0])
        def _():
            for acc, ref in zip(accs, (ri_ref, ro_ref, rs_ref)):
                acc[...] += ref[0].astype(F32)

        @pl.when(k == NCHIP - 1)
        def _():
            gi_ref[0] = acc_i[...]
            go_ref[0] = acc_o[...]
            gs_ref[...] = acc_s[...]
            g5_ref[...] = jnp.zeros_like(g5_ref)
            for i, row in enumerate((ROW_CONV_B, ROW_LN_G, ROW_LN_B, ROW_FINAL_G)):
                g5_ref[i + 1:i + 2, :] = acc_s[row:row + 1, :]
            loss = jnp.sum(acc_s[ROW_LOSS:ROW_LOSS + 1, :], axis=1, keepdims=True)
            loss_ref[...] = jnp.broadcast_to(loss, loss_ref.shape)

    def sent(k, w):
        return jnp.where(k == w[0], (k + 1) % NCHIP, k)

    hi, ho = D // 2, WOUT_SHARD // 2
    const = lambda shape: pl.BlockSpec(shape, lambda k, w: (0,) * len(shape))
    grid_spec = pltpu.PrefetchScalarGridSpec(
        num_scalar_prefetch=1, grid=(NCHIP,),
        in_specs=[pl.BlockSpec((1, hi, CHUNK), lambda k, w: (sent(k, w), 0, 0)),
                  pl.BlockSpec((1, ho, D), lambda k, w: (sent(k, w), 0, 0)),
                  pl.BlockSpec((1, SMALL_ROWS, D), lambda k, w: (sent(k, w), 0, 0)),
                  pl.BlockSpec((1, hi, CHUNK), lambda k, w: (w[0], 0, 0)),
                  pl.BlockSpec((1, ho, D), lambda k, w: (w[0], 0, 0)),
                  const((SMALL_ROWS, D))],
        out_specs=[pl.BlockSpec((1, hi, CHUNK), lambda k, w: (w[1], 0, 0)),
                   pl.BlockSpec((1, ho, D), lambda k, w: (w[1], 0, 0)),
                   const((SMALL_ROWS, D)), const((8, D)), const((8, LANES))],
        scratch_shapes=[pltpu.VMEM((hi, CHUNK), F32), pltpu.VMEM((ho, D), F32), pltpu.VMEM((SMALL_ROWS, D), F32)])
    return pl.pallas_call(
        body, grid_spec=grid_spec, name="sum_chips",
        out_shape=[jax.ShapeDtypeStruct((2, hi, CHUNK), F32), jax.ShapeDtypeStruct((2, ho, D), F32),
                   jax.ShapeDtypeStruct((SMALL_ROWS, D), F32), jax.ShapeDtypeStruct((8, D), F32),
                   jax.ShapeDtypeStruct((8, LANES), F32)],
        compiler_params=_params(("arbitrary",)),
    )(where, ri, ro, rs, pi, po, ps)


def _exchange_results(gi2, go2, st):
    flips = [(fx, fy, fc) for fx in (0, 1) for fy in (0, 1) for fc in (0, 1)][1:]

    def body(_gi, _go, st_ref, gi_ref, go_ref, all_ref, send, recv, lsem):
        x, y, c = _pos()
        sib = (x, y, 1 - c)

        def half(k, ref, slot):
            return pltpu.make_async_remote_copy(src_ref=ref.at[slot], dst_ref=ref.at[slot], send_sem=send.at[k],
                                                recv_sem=recv.at[k], device_id=sib, device_id_type=MESH)

        def stat(k, src, slot, dev):
            return pltpu.make_async_remote_copy(src_ref=src, dst_ref=all_ref.at[slot], send_sem=send.at[k],
                                                recv_sem=recv.at[k], device_id=dev, device_id_type=MESH)

        mine = pltpu.make_async_copy(st_ref, all_ref.at[4 * x + 2 * y + c], lsem)
        mine.start()
        sends = [half(k, ref, c) for k, ref in enumerate((gi_ref, go_ref))]
        peers = [(_flip(x, fx), _flip(y, fy), _flip(c, fc)) for fx, fy, fc in flips]
        sends += [stat(2 + k, st_ref, 4 * x + 2 * y + c, dev) for k, dev in enumerate(peers)]
        for cp in sends:
            cp.start()
        for k, ref in enumerate((gi_ref, go_ref)):
            half(k, ref, 1 - c).wait_recv()
        for k, (px, py, pc) in enumerate(peers):
            slot = 4 * px + 2 * py + pc
            stat(2 + k, all_ref.at[slot], slot, (px, py, pc)).wait_recv()
        for cp in sends:
            cp.wait_send()
        mine.wait()

    n = 2 + len(flips)
    return pl.pallas_call(
        body, name="exchange_results",
        in_specs=[ANY, ANY, ANY], out_specs=[ANY, ANY, ANY], input_output_aliases={0: 0, 1: 1},
        out_shape=[jax.ShapeDtypeStruct((2, D // 2, CHUNK), F32), jax.ShapeDtypeStruct((2, WOUT_SHARD // 2, D), F32),
                   jax.ShapeDtypeStruct((NDEV, 8, D), F32)],
        scratch_shapes=[pltpu.SemaphoreType.DMA((n,)), pltpu.SemaphoreType.DMA((n,)), pltpu.SemaphoreType.DMA],
    )(gi2, go2, st)


def _adamw_math(w, g, m, v):
    m2 = ADAM_B1 * m + (1.0 - ADAM_B1) * g
    v2 = ADAM_B2 * v + (1.0 - ADAM_B2) * (g * g)
    m_hat = m2 / (1.0 - ADAM_B1 ** ADAM_STEP)
    v_hat = v2 / (1.0 - ADAM_B2 ** ADAM_STEP)
    delta = -ADAM_LR * (m_hat / (jnp.sqrt(v_hat) + ADAM_EPS) + ADAM_WD * w)
    return delta, m2, v2


def _adamw(w, g, m, v, name):
    rows, cols = w.shape
    tm = 256 if rows % 256 == 0 else rows

    def body(w_ref, g_ref, m_ref, v_ref, d_ref, m2_ref, v2_ref):
        d_ref[...], m2_ref[...], v2_ref[...] = _adamw_math(w_ref[...], g_ref[...], m_ref[...], v_ref[...])

    shape = jax.ShapeDtypeStruct(w.shape, F32)
    return pl.pallas_call(
        body, grid=(rows // tm,), name=name,
        in_specs=[_rows(tm, cols)] * 4, out_specs=[_rows(tm, cols)] * 3, out_shape=[shape] * 3,
        compiler_params=_params(("arbitrary",)),
    )(w, g, m, v)


def _adamw_vectors(g5, first_parts, ws, ms, vs):
    n = len(ws)

    def body(g_ref, parts_ref, *refs):
        ins, g0_ref, outs = refs[:3 * n], refs[3 * n], refs[3 * n + 1:]
        g0 = parts_ref[0, 0:1, :]
        for dev in range(1, NDEV):
            g0 = g0 + parts_ref[dev, 0:1, :]
        g0_ref[...] = g0
        for i in range(n):
            g = g0 if i == 0 else g_ref[i:i + 1, :]
            res = _adamw_math(ins[i][...], g, ins[n + i][...], ins[2 * n + i][...])
            for kind in range(3):
                outs[kind * n + i][...] = res[kind]

    shape = jax.ShapeDtypeStruct((1, D), F32)
    return pl.pallas_call(body, name="adamw_vectors", out_shape=[shape] * (1 + 3 * n), compiler_params=_params())(
        g5, first_parts, *ws, *ms, *vs)


def kernel(x, norm_g, w_in, conv_w, conv_b, conv_ln_g, conv_ln_b, w_out, final_norm_g, loss_target, m_norm_g, m_w_in, m_conv_w, m_conv_b, m_conv_ln_g, m_conv_ln_b, m_w_out, m_final_norm_g, v_norm_g, v_w_in, v_conv_w, v_conv_b, v_conv_ln_g, v_conv_ln_b, v_w_out, v_final_norm_g):
    chip = 2 * lax.axis_index("x") + lax.axis_index("y")
    where = jnp.stack([chip, lax.axis_index("c")]).astype(jnp.int32)
    taps_shard = jnp.pad(conv_w[0], ((0, HALO - CONV_K), (0, 0)))
    wi_full, wo_full, cw_full = _place_shards(w_in[0], w_out[0], taps_shard, where)
    wi_full = _gather_w_in(wi_full)

    gf = final_norm_g[None]
    xb = x[0]
    h, q, k, v, a_gate, c_val, c_glu, c_gate, wo_full, cw_full = _inproj_fwd(xb, norm_g, wi_full, wo_full, cw_full)
    tables = [_bias_table(d) for d in PATTERNS]
    o, lse, y_att = _attn_fwd(q, k, v, tables, a_gate)
    u, y_conv = _conv_fwd(c_val, c_glu, c_gate, cw_full, conv_b, conv_ln_g, conv_ln_b)
    dx2, dy_att, du, dc_gate, dw_out, st_out = _outproj_loss(
        y_att, y_conv, wo_full, xb, loss_target[0], gf, u, c_gate, conv_ln_g, conv_ln_b)
    dc_val, dc_glu, dconv_w = _conv_bwd_taps(du, c_val, c_glu, cw_full)

    early = [dw_out.reshape(NCHIP, WOUT_SHARD, D), jnp.concatenate([st_out, dconv_w], axis=0)]
    po, ps = _add_halves(early, _exchange_halves(early, "exchange_halves_early"), "add_halves_early")
    do, da_gate, delta, ro, rs = _attn_gate_bwd(dy_att, o, a_gate, _head_sum_selectors(), [po, ps])
    dqs, dkvs = zip(*[_attn_bwd(q, k, v, do, lse, delta, t, d) for t, d in zip(tables, PATTERNS)])

    dproj_pieces = (dqs, dkvs, (da_gate, dc_val, dc_glu, dc_gate))
    late = [_inproj_bwd_w(h, *dproj_pieces)]
    (pi,) = _add_halves(late, _exchange_halves(late, "exchange_halves"), "add_halves")
    grad_x, st_in, ri = _inproj_bwd_x(*dproj_pieces, wi_full, xb, norm_g, dx2, pi)
    gi2, go2, g_small, g5, loss8 = _sum_chips(ri, ro, rs, pi, po, ps, where)
    gi2, go2, norm_g_parts = _exchange_results(gi2, go2, st_in)
    g_w_in = gi2.reshape(D, CHUNK)
    g_w_out = go2.reshape(WOUT_SHARD, D)
    g_taps = lax.dynamic_slice(g_small, (ROW_TAPS, chip * CONVW_SHARD), (CONV_K, CONVW_SHARD))

    d_w_in, m2_w_in, v2_w_in = _adamw(w_in[0], g_w_in, m_w_in[0], v_w_in[0], "adamw_w_in")
    d_w_out, m2_w_out, v2_w_out = _adamw(w_out[0], g_w_out, m_w_out[0], v_w_out[0], "adamw_w_out")
    d_taps, m2_taps, v2_taps = _adamw(conv_w[0], g_taps, m_conv_w[0], v_conv_w[0], "adamw_conv_w")
    g_norm, *vec = _adamw_vectors(
        g5, norm_g_parts,
        (norm_g, conv_b, conv_ln_g, conv_ln_b, gf),
        (m_norm_g, m_conv_b, m_conv_ln_g, m_conv_ln_b, m_final_norm_g[None]),
        (v_norm_g, v_conv_b, v_conv_ln_g, v_conv_ln_b, v_final_norm_g[None]))
    d_vec, m2_vec, v2_vec = vec[0:5], vec[5:10], vec[10:15]

    def weight_order(ng, wi, cw, cb, lg, lb, wo, fg):
        return (ng, wi[None], cw[None], cb, lg, lb, wo[None], fg[0])

    grads = weight_order(g_norm, g_w_in, g_taps, g5[1:2], g5[2:3], g5[3:4], g_w_out, g5[4:5])
    deltas = weight_order(d_vec[0], d_w_in, d_taps, d_vec[1], d_vec[2], d_vec[3], d_w_out, d_vec[4])
    new_m = weight_order(m2_vec[0], m2_w_in, m2_taps, m2_vec[1], m2_vec[2], m2_vec[3], m2_w_out, m2_vec[4])
    new_v = weight_order(v2_vec[0], v2_w_in, v2_taps, v2_vec[1], v2_vec[2], v2_vec[3], v2_w_out, v2_vec[4])
    return (loss8[0, 0], grad_x[None], *grads, *deltas, *new_m, *new_v)
```

```python
import jax
import jax.numpy as jnp
from jax import lax
from jax.experimental import pallas as pl
from jax.experimental.pallas import tpu as pltpu

F32 = jnp.float32
BF16 = jnp.bfloat16

S = 4096
D = 1024
LANES = 128
HD = 64
NKV = 4
GQ = 4
KVW = NKV * HD
NCOL = 5632
CONV_K = 31
HALO = 32
BLK = 128
PATTERNS = (1, 4, 16)
NORM_EPS = 1e-6
LN_EPS = 1e-5
NEG = -1e30
OFF_Q, OFF_K, OFF_AG, OFF_CV, OFF_CG, OFF_CGATE = 0, 1024, 1536, 2560, 3584, 4608
NCHIP = 4
CHUNK = NCOL // NCHIP
WOUT_ROWS = 2 * D
WOUT_SHARD = WOUT_ROWS // NCHIP
CONVW_SHARD = D // NCHIP

ADAM_LR, ADAM_B1, ADAM_B2, ADAM_EPS, ADAM_WD, ADAM_STEP = 0.001, 0.9, 0.999, 1e-08, 0.01, 10

VMEM_LIMIT = 56 * 1024 * 1024


def _params(sem=None, vmem=VMEM_LIMIT):
    return pltpu.CompilerParams(dimension_semantics=sem, vmem_limit_bytes=vmem)


def _sigmoid(a):
    return 0.5 * jnp.tanh(0.5 * a) + 0.5


def _rows(tm, width):
    return pl.BlockSpec((tm, width), lambda i: (i, 0))


def _slabs(n):
    return jax.ShapeDtypeStruct((n, S, LANES), F32)


def _slab_rows(n, tm):
    return pl.BlockSpec((n, tm, LANES), lambda i: (0, i, 0))


def _resident(shape):
    return pl.BlockSpec(shape, lambda *_: (0,) * len(shape), pipeline_mode=pl.Buffered(1))


def _dot(a, b):
    return jnp.dot(a, b, preferred_element_type=F32)


def _dot_nt(a, b):
    return lax.dot_general(a, b, (((1,), (1,)), ((), ())), preferred_element_type=F32)


def _dot_tn(a, b):
    return lax.dot_general(a, b, (((0,), (0,)), ((), ())), preferred_element_type=F32)


def _inproj_fwd(x, g1, w_bf, wo_full, cw_full):
    tm = 512
    steps = S // tm

    def body(x_ref, g_ref, w_ref, _wo, _cw, h_ref, q_ref, k_ref, v_ref, ag_ref, cv_ref, cg_ref, cgate_ref,
             wo_ref, cw_ref, send, recv):
        i = pl.program_id(0)
        stages = _gather_stages([(W_OUT, wo_ref), (TAPS, cw_ref)], send, recv)
        for stage, step in zip(stages[:3], (0, steps // 2 - 1, steps - 2)):
            pl.when(i == step)(stage)
        xt = x_ref[...]
        r = lax.rsqrt(jnp.mean(xt * xt, axis=-1, keepdims=True) + NORM_EPS)
        h = (xt * r * g_ref[...]).astype(BF16)
        h_ref[...] = h
        q = _dot(h, w_ref[:, OFF_Q:OFF_Q + D]) * (HD ** -0.5)
        kv = _dot(h, w_ref[:, OFF_K:OFF_K + 2 * KVW])
        for sl in range(D // LANES):
            q_ref[sl] = q[:, sl * LANES:(sl + 1) * LANES]
        for sl in range(KVW // LANES):
            k_ref[sl] = kv[:, sl * LANES:(sl + 1) * LANES]
            v_ref[sl] = kv[:, KVW + sl * LANES:KVW + (sl + 1) * LANES]
        ag_ref[...] = _dot(h, w_ref[:, OFF_AG:OFF_AG + D])
        cv_ref[...] = _dot(h, w_ref[:, OFF_CV:OFF_CV + D])
        cg_ref[...] = _dot(h, w_ref[:, OFF_CG:OFF_CG + D])
        cgate_ref[...] = _dot(h, w_ref[:, OFF_CGATE:OFF_CGATE + D])
        pl.when(i == steps - 1)(stages[3])

    big = jax.ShapeDtypeStruct((S, D), F32)
    return pl.pallas_call(
        body, grid=(steps,), name="inproj_fwd",
        in_specs=[_rows(tm, D), _resident((1, D)), _resident((D, NCOL)), ANY, ANY],
        out_specs=[_rows(tm, D), _slab_rows(D // LANES, tm), _slab_rows(KVW // LANES, tm), _slab_rows(KVW // LANES, tm),
                   _rows(tm, D), _rows(tm, D), _rows(tm, D), _rows(tm, D), ANY, ANY],
        out_shape=[jax.ShapeDtypeStruct((S, D), BF16), _slabs(D // LANES), _slabs(KVW // LANES), _slabs(KVW // LANES),
                   big, big, big, big,
                   jax.ShapeDtypeStruct((WOUT_ROWS, D), BF16), jax.ShapeDtypeStruct((HALO, D), F32)],
        input_output_aliases={3: 8, 4: 9},
        scratch_shapes=_gather_sems(2),
        compiler_params=_params(("arbitrary",)),
    )(x, g1, w_bf, wo_full, cw_full)


def _bias_table(d):
    h = jnp.arange(NKV * GQ, dtype=F32)
    slopes = jnp.exp2(-8.0 * (h + 1.0) / (NKV * GQ))
    qi = jnp.arange(BLK)[:, None]
    kj = jnp.arange(2 * BLK)[None, :]
    dist = BLK + qi - kj
    window = (dist >= 0) & (dist <= BLK)
    bias = -slopes[:, None, None] * (dist * d).astype(F32)[None]
    has_prev = jnp.stack([jnp.broadcast_to(kj >= BLK, (BLK, 2 * BLK)), jnp.ones((BLK, 2 * BLK), bool)])
    valid = window[None] & has_prev
    tab = jnp.where(valid[:, None], bias[None], NEG)
    tab = tab.reshape(2, NKV, GQ, BLK, 2 * BLK)[:, :, jnp.array(STACK_ORDER)]
    return tab.reshape(2, NKV, GQ * BLK, 2 * BLK)


def _sub_rows(start, d):
    if d == 1:
        return pl.ds(pl.multiple_of(start, BLK), BLK)
    return pl.ds(start, BLK, stride=d)


CHUNK_ROWS = 2048
BLOCKS_PER_CHUNK = CHUNK_ROWS // BLK


def _low_lanes(rows=BLK):
    return lax.broadcasted_iota(jnp.int32, (rows, LANES), 1) < HD


def _block_start(idx, d):
    shift = d.bit_length() - 1
    b, r = lax.shift_right_logical(idx, shift), lax.bitwise_and(idx, d - 1)
    start = b * (BLK * d) + r
    return b, start, jnp.maximum(start - BLK * d, r)


STACK_ORDER = (0, 2, 1, 3)


def _stack_heads(tiles, low_fill=(0.0, 0.0), high_fill=(0.0, 0.0)):
    low = _low_lanes()
    t0, t1 = tiles
    return jnp.concatenate([jnp.where(low, t0, high_fill[0]), jnp.where(low, t1, high_fill[1]),
                            jnp.where(low, low_fill[0], t0), jnp.where(low, low_fill[1], t1)], axis=0).astype(BF16)


def _slab_tiles(ref, rows):
    return ref[0, rows, :], ref[1, rows, :]


def _unstack_heads(dup):
    low = _low_lanes()
    return (jnp.where(low, dup[0:BLK], dup[2 * BLK:3 * BLK]), jnp.where(low, dup[BLK:2 * BLK], dup[3 * BLK:4 * BLK]))


def _kv_dup(ref, prow, rows, odd):
    t = jnp.concatenate([ref[0, prow, :], ref[0, rows, :]], axis=0)
    swapped = pltpu.roll(t, HD, axis=1)
    keep = jnp.logical_xor(_low_lanes(2 * BLK), odd)
    return jnp.where(keep, t, swapped).astype(BF16)


PIECES = 3


def _by_head(tiles):
    lane = lax.broadcasted_iota(jnp.int32, tiles[0].shape, 1)
    out = tiles[0]
    for g in range(1, GQ):
        out = jnp.where(lax.bitwise_and(lane, GQ - 1) == g, tiles[g], out)
    return out


def _minus_in_pieces(x):
    lane = lax.bitwise_and(lax.broadcasted_iota(jnp.int32, x.shape, 1), HD - 1)
    hi = (-x).astype(BF16).astype(F32)
    rest = -x - hi
    mid = rest.astype(BF16).astype(F32)
    lo = (rest - mid).astype(BF16).astype(F32)
    return jnp.where(lane < GQ, hi, jnp.where(lane < 2 * GQ, mid, jnp.where(lane < PIECES * GQ, lo, 0.0)))


def _attn_fwd(q, k, v, tables, a_gate):
    tm = 256
    width = GQ * HD

    lane_out = jnp.arange(LANES)[None, :] // HD
    spread_sel = jnp.stack([jnp.arange(LANES)[:, None] == 2 * half + lane_out for half in range(2)]).astype(BF16)

    def body(q_ref, k_ref, v_ref, b1_ref, b2_ref, b3_ref, ag_ref, sel_ref, o_ref, lse_ref, y_ref, op, lp):
        odd = pl.program_id(0) % 2 == 1
        chunk = pl.program_id(1)
        ones = jnp.ones((2 * BLK, LANES), BF16)

        for pat, (d, b_ref) in enumerate(zip(PATTERNS, (b1_ref, b2_ref, b3_ref))):
            def block(idx, carry, pat=pat, d=d, b_ref=b_ref):
                b, start, pstart = _block_start(chunk * BLOCKS_PER_CHUNK + idx, d)
                rows, prow = _sub_rows(start, d), _sub_rows(pstart, d)
                mine = _sub_rows(start - chunk * CHUNK_ROWS, d)
                qs = _stack_heads(_slab_tiles(q_ref, mine))
                kw = _kv_dup(k_ref, prow, rows, odd)
                vw = _kv_dup(v_ref, prow, rows, odd)
                s = _dot_nt(qs, kw) + b_ref[jnp.minimum(b, 1), 0]
                m = jnp.max(s, axis=1, keepdims=True)
                p = jnp.exp(s - m).astype(BF16)
                ol = _dot(p, jnp.concatenate([vw, ones], axis=1))
                l = ol[:, LANES:]
                op[pat, 0, mine, :], op[pat, 1, mine, :] = _unstack_heads(ol[:, :LANES] / l)
                lse = m + jnp.log(l)
                lp[pat, mine, :] = _by_head([lse[b_ * BLK:(b_ + 1) * BLK] for b_ in STACK_ORDER])
                return carry

            lax.fori_loop(0, BLOCKS_PER_CHUNK, block, 0, unroll=2)

        def mix(t, carry):
            r = pl.ds(pl.multiple_of(t * tm, tm), tm)
            a, b, c = lp[0, r, :], lp[1, r, :], lp[2, r, :]
            m = jnp.maximum(jnp.maximum(a, b), c)
            ea, eb, ec = jnp.exp(a - m), jnp.exp(b - m), jnp.exp(c - m)
            den = ea + eb + ec
            lse_ref[0, r, :] = _minus_in_pieces(m + jnp.log(den))
            inv = 1.0 / den
            for half in range(2):
                def spread(w):
                    hi = w.astype(BF16)
                    lo = (w - hi.astype(F32)).astype(BF16)
                    return _dot(hi, sel_ref[half]) + _dot(lo, sel_ref[half])

                o = (spread(ea * inv) * op[0, half, r, :] + spread(eb * inv) * op[1, half, r, :]
                     + spread(ec * inv) * op[2, half, r, :])
                o_ref[half, r, :] = o
                cols = slice(half * LANES, (half + 1) * LANES)
                ag = ag_ref[r, cols]
                y_ref[r, cols] = (o * (ag * _sigmoid(ag))).astype(BF16)
            return carry

        lax.fori_loop(0, CHUNK_ROWS // tm, mix, 0)

    q_like = pl.BlockSpec((2, CHUNK_ROWS, LANES), lambda j, c: (j, c, 0))
    per_kv = pl.BlockSpec((1, CHUNK_ROWS, LANES), lambda j, c: (j, c, 0))
    kv = pl.BlockSpec((1, S, LANES), lambda j, c: (j // 2, 0, 0))
    bias_spec = pl.BlockSpec((2, 1, GQ * BLK, 2 * BLK), lambda j, c: (0, j, 0, 0))
    group_cols = pl.BlockSpec((CHUNK_ROWS, width), lambda j, c: (c, j))
    return pl.pallas_call(
        body, grid=(NKV, S // CHUNK_ROWS), name="attn_fwd",
        in_specs=[q_like, kv, kv, bias_spec, bias_spec, bias_spec, group_cols,
                  pl.BlockSpec((2, LANES, LANES), lambda j, c: (0, 0, 0))],
        out_specs=[q_like, per_kv, group_cols],
        out_shape=[_slabs(D // LANES), _slabs(NKV), jax.ShapeDtypeStruct((S, D), BF16)],
        scratch_shapes=[pltpu.VMEM((len(PATTERNS), 2, CHUNK_ROWS, LANES), F32),
                        pltpu.VMEM((len(PATTERNS), CHUNK_ROWS, LANES), F32)],
        compiler_params=_params(("arbitrary", "arbitrary")),
    )(q, k, v, *tables, a_gate, spread_sel)


def _head_sum_selectors():
    lane_in = jnp.arange(LANES)[:, None] // HD
    return jnp.stack([jnp.broadcast_to(lane_in == h, (LANES, LANES)) for h in range(2)]).astype(BF16)


def _attn_gate_bwd(dy_att, o, a_gate, selectors, chip_sums):
    tm = 256
    last = S // tm - 1
    landing, sems = _exchange_results_of(chip_sums)
    n_sums = len(chip_sums)

    def body(dy_ref, o_ref, ag_ref, e_ref, *refs):
        sums, (do_ref, dag_ref, delta_ref), refs = refs[:n_sums], refs[n_sums:n_sums + 3], refs[n_sums + 3:]
        landed, (send, recv) = refs[:n_sums], refs[n_sums:]
        i = pl.program_id(0)
        copies = _chip_exchange_copies(sums, landed, send, recv)
        _start_exchange(copies, i == 0)
        for j in range(NKV):
            deltas = []
            for sl in (2 * j, 2 * j + 1):
                cols = slice(sl * LANES, (sl + 1) * LANES)
                dy, ag, o_ = dy_ref[:, cols], ag_ref[:, cols], o_ref[sl]
                sg = _sigmoid(ag)
                do = dy * (ag * sg)
                do_ref[sl] = do
                dag_ref[:, cols] = (dy * o_ * (sg * (1.0 + ag * (1.0 - sg)))).astype(BF16)
                prod = do * o_
                hi = prod.astype(BF16)
                lo = (prod - hi.astype(F32)).astype(BF16)
                deltas += [_dot(hi, e_ref[h]) + _dot(lo, e_ref[h]) for h in range(2)]
            delta_ref[j] = _minus_in_pieces(_by_head(deltas))
        _finish_exchange(copies, i == last)

    return pl.pallas_call(
        body, grid=(S // tm,), name="attn_gate_bwd",
        in_specs=[_rows(tm, D), _slab_rows(D // LANES, tm), _rows(tm, D), _resident((2, LANES, LANES))] + [ANY] * n_sums,
        out_specs=[_slab_rows(D // LANES, tm), _rows(tm, D), _slab_rows(NKV, tm)] + [ANY] * n_sums,
        out_shape=[_slabs(D // LANES), jax.ShapeDtypeStruct((S, D), BF16), _slabs(NKV)] + landing,
        scratch_shapes=sems,
        compiler_params=_params(("arbitrary",)),
    )(dy_att, o, a_gate, selectors, *chip_sums)


def _own_pieces(tile):
    lane = lax.bitwise_and(lax.broadcasted_iota(jnp.int32, tile.shape, 1), HD - 1)
    head = jnp.where(lane < PIECES * GQ, lax.bitwise_and(lane, GQ - 1), -1)
    return [jnp.where(head == g, tile, 0.0) for g in range(GQ)]


def _minus_scalars_matmul(tiles, pieces, dup, ones_low, ones_high):
    own = _own_pieces(pieces)
    stacked = _stack_heads(tiles, low_fill=(own[1], own[3]), high_fill=(own[0], own[2]))
    low = _low_lanes(2 * BLK)
    return jnp.concatenate([_dot_nt(stacked[:2 * BLK], jnp.where(low, dup, ones_high)),
                            _dot_nt(stacked[2 * BLK:], jnp.where(low, ones_low, dup))], axis=0)


def _attn_bwd(q, k, v, do, lse, delta, bias, d):
    def body(q_ref, do_ref, l_ref, dl_ref, k_ref, v_ref, b_ref, dq_ref, dkv_ref, acc):
        odd = pl.program_id(0) % 2 == 1
        chunk = pl.program_id(1)
        lane = lax.broadcasted_iota(jnp.int32, (2 * BLK, LANES), 1)
        ones_low = (lane < PIECES * GQ).astype(BF16)
        ones_high = ((lane >= HD) & (lane < HD + PIECES * GQ)).astype(BF16)

        def in_acc(block_idx):
            return pl.ds(pl.multiple_of(block_idx * BLK, BLK), BLK)

        @pl.when(chunk == 0)
        def _():
            acc[...] = jnp.zeros_like(acc)

        def block(idx, carry):
            idx = chunk * BLOCKS_PER_CHUNK + idx
            b, start, pstart = _block_start(idx, d)
            rows, prow = _sub_rows(start, d), _sub_rows(pstart, d)
            mine = _sub_rows(start - chunk * CHUNK_ROWS, d)
            q_tiles, do_tiles = _slab_tiles(q_ref, mine), _slab_tiles(do_ref, mine)
            qs, dos = _stack_heads(q_tiles), _stack_heads(do_tiles)
            kw = _kv_dup(k_ref, prow, rows, odd)
            vw = _kv_dup(v_ref, prow, rows, odd)
            s = _minus_scalars_matmul(q_tiles, l_ref[0, mine, :], kw, ones_low, ones_high)
            p = jnp.exp(s + b_ref[jnp.minimum(b, 1), 0])
            dv2 = _dot_tn(p.astype(BF16), dos)
            dp = _minus_scalars_matmul(do_tiles, dl_ref[0, mine, :], vw, ones_low, ones_high)
            ds = (p * dp).astype(BF16)
            dq_ref[0, mine, :], dq_ref[1, mine, :] = _unstack_heads(_dot(ds, kw))
            dk2 = _dot_tn(ds, qs)
            dkv = jnp.where(_low_lanes(2 * BLK), dk2 + pltpu.roll(dk2, HD, axis=1), dv2 + pltpu.roll(dv2, HD, axis=1))
            acc[in_acc(idx), :] = acc[in_acc(idx), :] + dkv[BLK:]
            before = jnp.where(b >= 1, idx - d, idx)
            acc[in_acc(before), :] = acc[in_acc(before), :] + dkv[:BLK]
            return carry

        lax.fori_loop(0, BLOCKS_PER_CHUNK, block, 0, unroll=8)

        @pl.when(chunk == S // CHUNK_ROWS - 1)
        def _():
            def place(idx, carry):
                _, start, _ = _block_start(idx, d)
                dkv_ref[0, _sub_rows(start, d), :] = acc[in_acc(idx), :]
                return carry

            lax.fori_loop(0, S // BLK, place, 0, unroll=4)

    q_like = pl.BlockSpec((2, CHUNK_ROWS, LANES), lambda j, c: (j, c, 0))
    pieces = pl.BlockSpec((1, CHUNK_ROWS, LANES), lambda j, c: (j, c, 0))
    kv = pl.BlockSpec((1, S, LANES), lambda j, c: (j // 2, 0, 0))
    per_kv = pl.BlockSpec((1, S, LANES), lambda j, c: (j, 0, 0))
    bias_spec = pl.BlockSpec((2, 1, GQ * BLK, 2 * BLK), lambda j, c: (0, j, 0, 0))
    return pl.pallas_call(
        body, grid=(NKV, S // CHUNK_ROWS), name=f"attn_bwd_d{d}",
        in_specs=[q_like, q_like, pieces, pieces, kv, kv, bias_spec],
        out_specs=[q_like, per_kv],
        out_shape=[_slabs(D // LANES), _slabs(NKV)],
        scratch_shapes=[pltpu.VMEM((S, LANES), F32)],
        compiler_params=_params(("arbitrary", "arbitrary")),
    )(q, do, lse, delta, k, v, bias)


CONV_T = 256


def _halo_before(i):
    return (jnp.maximum(i * (CONV_T // HALO) - 1, 0), 0)


def _halo_after(i):
    return (jnp.minimum((i + 1) * (CONV_T // HALO), S // HALO - 1), 0)


SUBLANES = 8
NCH = D // LANES
GROUP = SUBLANES * SUBLANES


def _comb(ref, cb, base):
    return ref[cb, pl.ds(base, SUBLANES, stride=SUBLANES), :]


def _taps(w_ref, cols):
    return [jnp.broadcast_to(w_ref[j:j + 1, cols], (SUBLANES, LANES)) for j in range(CONV_K)]


def _conv_fwd(c_val, c_glu, c_gate, conv_w, conv_b, ln_g, ln_b):
    T = CONV_T

    def body(cv_ref, cg_ref, cvh_ref, cgh_ref, gate_ref, w_ref, b_ref, lg_ref, lb_ref, u_ref, y_ref, win, us):
        i = pl.program_id(0)
        for cb in range(NCH):
            cols = slice(cb * LANES, (cb + 1) * LANES)
            win[cb, HALO:HALO + T, :] = cv_ref[:, cols] * _sigmoid(cg_ref[:, cols])
            win[cb, 0:HALO, :] = jnp.where(i > 0, cvh_ref[:, cols] * _sigmoid(cgh_ref[:, cols]), 0.0)
        for cb in range(NCH):
            cols = slice(cb * LANES, (cb + 1) * LANES)
            taps = _taps(w_ref, cols)
            bias = jnp.broadcast_to(b_ref[:, cols], (SUBLANES, LANES))

            def group(g, carry):
                for b in range(SUBLANES):
                    base = g * GROUP + b
                    acc = bias
                    for j in range(CONV_K):
                        acc = acc + taps[j] * _comb(win, cb, base + (HALO - (CONV_K - 1) + j))
                    us[cb, pl.ds(base, SUBLANES, stride=SUBLANES), :] = acc
                return carry

            lax.fori_loop(0, T // GROUP, group, 0)
        total = us[0]
        for cb in range(1, NCH):
            total = total + us[cb]
        mu = jnp.sum(total, axis=-1, keepdims=True) * (1.0 / D)
        sq = jnp.zeros((T, LANES), F32)
        for cb in range(NCH):
            uc = us[cb] - mu
            sq = sq + uc * uc
        rstd = lax.rsqrt(jnp.sum(sq, axis=-1, keepdims=True) * (1.0 / D) + LN_EPS)
        for cb in range(NCH):
            cols = slice(cb * LANES, (cb + 1) * LANES)
            u = us[cb]
            u_ref[:, cols] = u
            nrm = (u - mu) * rstd * lg_ref[:, cols] + lb_ref[:, cols]
            gate = gate_ref[:, cols]
            y_ref[:, cols] = (nrm * _sigmoid(nrm) * (gate * _sigmoid(gate))).astype(BF16)

    halo = pl.BlockSpec((HALO, D), _halo_before)
    return pl.pallas_call(
        body, grid=(S // T,), name="conv_fwd",
        in_specs=[_rows(T, D), _rows(T, D), halo, halo, _rows(T, D),
                  _resident((HALO, D)), _resident((1, D)), _resident((1, D)), _resident((1, D))],
        out_specs=[_rows(T, D), _rows(T, D)],
        out_shape=[jax.ShapeDtypeStruct((S, D), F32), jax.ShapeDtypeStruct((S, D), BF16)],
        scratch_shapes=[pltpu.VMEM((NCH, T + HALO, LANES), F32), pltpu.VMEM((NCH, T, LANES), F32)],
        compiler_params=_params(("arbitrary",)),
    )(c_val, c_glu, c_val, c_glu, c_gate, conv_w, conv_b, ln_g, ln_b)


def _conv_bwd_taps(du, c_val, c_glu, conv_w):
    T = CONV_T
    last = S // T - 1

    def body(du_ref, dua_ref, cv_ref, cg_ref, cvh_ref, cgh_ref, w_ref, dcv_ref, dcg_ref, dw_ref,
             hwin, dwin, dhs, dw_acc):
        i = pl.program_id(0)

        @pl.when(i == 0)
        def _():
            dw_acc[...] = jnp.zeros_like(dw_acc)

        for cb in range(NCH):
            cols = slice(cb * LANES, (cb + 1) * LANES)
            hwin[cb, HALO:HALO + T, :] = cv_ref[:, cols] * _sigmoid(cg_ref[:, cols])
            hwin[cb, 0:HALO, :] = jnp.where(i > 0, cvh_ref[:, cols] * _sigmoid(cgh_ref[:, cols]), 0.0)
            dwin[cb, 0:T, :] = du_ref[:, cols]
            dwin[cb, T:T + HALO, :] = jnp.where(i < last, dua_ref[:, cols], 0.0)
        for cb in range(NCH):
            cols = slice(cb * LANES, (cb + 1) * LANES)
            taps = _taps(w_ref, cols)

            def group_dh(g, carry):
                for b in range(SUBLANES):
                    base = g * GROUP + b
                    acc = jnp.zeros((SUBLANES, LANES), F32)
                    for j in range(CONV_K):
                        acc = acc + taps[j] * _comb(dwin, cb, base + (CONV_K - 1 - j))
                    dhs[cb, pl.ds(base, SUBLANES, stride=SUBLANES), :] = acc
                return carry

            lax.fori_loop(0, T // GROUP, group_dh, 0)

            def group_dw(g, sums):
                for b in range(SUBLANES):
                    base = g * GROUP + b
                    d = _comb(dwin, cb, base)
                    sums = tuple(sums[j] + d * _comb(hwin, cb, base + (HALO - (CONV_K - 1) + j))
                                 for j in range(CONV_K))
                return sums

            sums = lax.fori_loop(0, T // GROUP, group_dw, tuple(dw_acc[j, :, cols] for j in range(CONV_K)))
            for j in range(CONV_K):
                dw_acc[j, :, cols] = sums[j]
            dh = dhs[cb]
            cv, sg = cv_ref[:, cols], _sigmoid(cg_ref[:, cols])
            dcv_ref[:, cols] = (dh * sg).astype(BF16)
            dcg_ref[:, cols] = (dh * cv * (sg * (1.0 - sg))).astype(BF16)

        @pl.when(i == last)
        def _():
            dw_ref[...] = jnp.zeros_like(dw_ref)
            for j in range(CONV_K):
                dw_ref[j:j + 1, :] = jnp.sum(dw_acc[j], axis=0, keepdims=True)

    before = pl.BlockSpec((HALO, D), _halo_before)
    after = pl.BlockSpec((HALO, D), _halo_after)
    big = jax.ShapeDtypeStruct((S, D), BF16)
    return pl.pallas_call(
        body, grid=(S // T,), name="conv_bwd_taps",
        in_specs=[_rows(T, D), after, _rows(T, D), _rows(T, D), before, before, _resident((HALO, D))],
        out_specs=[_rows(T, D), _rows(T, D), pl.BlockSpec((HALO, D), lambda i: (0, 0))],
        out_shape=[big, big, jax.ShapeDtypeStruct((HALO, D), F32)],
        scratch_shapes=[pltpu.VMEM((NCH, T + HALO, LANES), F32), pltpu.VMEM((NCH, T + HALO, LANES), F32),
                        pltpu.VMEM((NCH, T, LANES), F32), pltpu.VMEM((CONV_K, SUBLANES, D), F32)],
        compiler_params=_params(("arbitrary",)),
    )(du, du, c_val, c_glu, c_val, c_glu, conv_w)


def _outproj_loss(y_att, y_conv, w_out_bf, x, target, gf, u, c_gate, ln_g, ln_b):
    tm = 256

    def body(ya_ref, yc_ref, w_ref, x_ref, t_ref, gf_ref, u_ref, gate_ref, lg_ref, lb_ref,
             dx2_ref, dya_ref, du_ref, dgate_ref, dw_ref, st_ref, acc):
        @pl.when(pl.program_id(0) == 0)
        def _():
            acc[...] = jnp.zeros_like(acc)
            st_ref[...] = jnp.zeros_like(st_ref)

        ya, yc = ya_ref[...], yc_ref[...]
        x2 = x_ref[...] + _dot(ya, w_ref[0:D, :]) + _dot(yc, w_ref[D:2 * D, :])
        r = lax.rsqrt(jnp.mean(x2 * x2, axis=-1, keepdims=True) + NORM_EPS)
        xn = x2 * r
        err = xn * gf_ref[...] - t_ref[...]
        dout = err * (1.0 / D)
        dxn = dout * gf_ref[...]
        dx2 = r * (dxn - xn * jnp.mean(dxn * xn, axis=-1, keepdims=True))
        dx2_ref[...] = dx2
        dx2b = dx2.astype(BF16)
        dya_ref[...] = _dot_nt(dx2b, w_ref[0:D, :])
        dy = _dot_nt(dx2b, w_ref[D:2 * D, :])
        acc[0:D, :] += _dot_tn(ya, dx2b)
        acc[D:2 * D, :] += _dot_tn(yc, dx2b)
        st_ref[ROW_FINAL_G:ROW_FINAL_G + 1, :] += jnp.sum(dout * xn, axis=0, keepdims=True)
        st_ref[ROW_LOSS:ROW_LOSS + 1, :] += jnp.sum(err * err, axis=0, keepdims=True) * (0.5 / D)

        u, gate = u_ref[...], gate_ref[...]
        mu = jnp.mean(u, axis=-1, keepdims=True)
        uc = u - mu
        rstd = lax.rsqrt(jnp.mean(uc * uc, axis=-1, keepdims=True) + LN_EPS)
        z = uc * rstd
        nrm = z * lg_ref[...] + lb_ref[...]
        sn, sg = _sigmoid(nrm), _sigmoid(gate)
        dgate_ref[...] = (dy * (nrm * sn) * (sg * (1.0 + gate * (1.0 - sg)))).astype(BF16)
        dn = dy * (gate * sg) * (sn * (1.0 + nrm * (1.0 - sn)))
        dz = dn * lg_ref[...]
        du = rstd * (dz - jnp.mean(dz, axis=-1, keepdims=True) - z * jnp.mean(dz * z, axis=-1, keepdims=True))
        du_ref[...] = du
        st_ref[ROW_LN_G:ROW_LN_G + 1, :] += jnp.sum(dn * z, axis=0, keepdims=True)
        st_ref[ROW_LN_B:ROW_LN_B + 1, :] += jnp.sum(dn, axis=0, keepdims=True)
        st_ref[ROW_CONV_B:ROW_CONV_B + 1, :] += jnp.sum(du, axis=0, keepdims=True)

        @pl.when(pl.program_id(0) == S // tm - 1)
        def _():
            dw_ref[...] = acc[...].astype(BF16)

    big = jax.ShapeDtypeStruct((S, D), F32)
    vec = _resident((1, D))
    return pl.pallas_call(
        body, grid=(S // tm,), name="outproj_loss",
        in_specs=[_rows(tm, D), _rows(tm, D), _resident((WOUT_ROWS, D)), _rows(tm, D), _rows(tm, D), vec,
                  _rows(tm, D), _rows(tm, D), vec, vec],
        out_specs=[_rows(tm, D), _rows(tm, D), _rows(tm, D), _rows(tm, D),
                   pl.BlockSpec((WOUT_ROWS, D), lambda i: (0, 0)), pl.BlockSpec((8, D), lambda i: (0, 0))],
        out_shape=[big, big, big, jax.ShapeDtypeStruct((S, D), BF16),
                   jax.ShapeDtypeStruct((WOUT_ROWS, D), BF16), jax.ShapeDtypeStruct((8, D), F32)],
        scratch_shapes=[pltpu.VMEM((WOUT_ROWS, D), F32)],
        compiler_params=_params(("arbitrary",)),
    )(y_att, y_conv, w_out_bf, x, target, gf, u, c_gate, ln_g, ln_b)


UNITS_PER_CHUNK = CHUNK // LANES


def _dproj_unit(u, dqs, dkvs, gates, rows):
    if u < OFF_K // LANES:
        return ((dqs[0][u] + dqs[1][u] + dqs[2][u]) * (HD ** -0.5)).astype(BF16)
    if u < OFF_AG // LANES:
        w = u - OFF_K // LANES
        ta, tb = (dkvs[0][j] + dkvs[1][j] + dkvs[2][j] for j in (2 * (w % 2), 2 * (w % 2) + 1))
        low = _low_lanes(rows)
        if w < 2:
            return jnp.where(low, ta, pltpu.roll(tb, HD, axis=1)).astype(BF16)
        return jnp.where(low, pltpu.roll(ta, HD, axis=1), tb).astype(BF16)
    g, sl = divmod(u - OFF_AG // LANES, D // LANES)
    return gates[g][:, sl * LANES:(sl + 1) * LANES]


def _dproj_sources(units, dqs, dkvs, gates, rows):
    use_q = any(u < OFF_K // LANES for u in units)
    use_kv = any(OFF_K // LANES <= u < OFF_AG // LANES for u in units)
    use_g = sorted({(u - OFF_AG // LANES) // (D // LANES) for u in units if u >= OFF_AG // LANES})
    args = (list(dqs) if use_q else []) + (list(dkvs) if use_kv else []) + [gates[g] for g in use_g]
    specs = ([_slab_rows(D // LANES, rows)] * 3 if use_q else []) + ([_slab_rows(NKV, rows)] * 3 if use_kv else []) \
        + [_rows(rows, D)] * len(use_g)

    def pick(refs):
        refs = list(refs)
        q_refs = [refs.pop(0) for _ in range(3)] if use_q else None
        kv_refs = [refs.pop(0) for _ in range(3)] if use_kv else None
        return q_refs, kv_refs, {g: refs.pop(0) for g in use_g}

    return args, specs, pick


def _exchange_results_of(chip_sums):
    n = len(chip_sums) * len(CHIP_FLIPS)
    shapes = [jax.ShapeDtypeStruct((NCHIP,) + tuple(a.shape[1:] if a.ndim == 3 else a.shape), a.dtype)
              for a in chip_sums]
    return shapes, [pltpu.SemaphoreType.DMA((n,)), pltpu.SemaphoreType.DMA((n,))]


def _start_exchange(copies, first_step):
    @pl.when(first_step)
    def _():
        for out, _ in copies:
            out.start()


def _finish_exchange(copies, last_step):
    @pl.when(last_step)
    def _():
        for _, arrival in copies:
            arrival.wait_recv()
        for out, _ in copies:
            out.wait_send()


def _inproj_bwd_x(dqs, dkvs, gates, w_bf, x, g1, dx2, pi):
    tm = 256
    last = S // tm - 1
    units = range(NCOL // LANES)
    pieces, piece_specs, pick = _dproj_sources(units, dqs, dkvs, gates, tm)
    landing, sems = _exchange_results_of([pi])

    def body(*refs):
        piece_refs, refs = refs[:len(pieces)], refs[len(pieces):]
        w_ref, x_ref, g_ref, dx2_ref, pi_ref, gx_ref, st_ref, ri_ref, dp_ref, send, recv = refs
        i = pl.program_id(0)
        copies = _chip_exchange_copies([pi_ref], [ri_ref], send, recv)
        _start_exchange(copies, i == 0)

        @pl.when(i == 0)
        def _():
            st_ref[...] = jnp.zeros_like(st_ref)

        sources = pick(piece_refs)
        for u in units:
            dp_ref[:, u * LANES:(u + 1) * LANES] = _dproj_unit(u, *sources, tm)
        dh = _dot_nt(dp_ref[...], w_ref[...])
        xt = x_ref[...]
        r = lax.rsqrt(jnp.mean(xt * xt, axis=-1, keepdims=True) + NORM_EPS)
        xn = xt * r
        dxn = dh * g_ref[...]
        gx_ref[...] = dx2_ref[...] + r * (dxn - xn * jnp.mean(dxn * xn, axis=-1, keepdims=True))
        st_ref[0:1, :] += jnp.sum(dh * xn, axis=0, keepdims=True)
        _finish_exchange(copies, i == last)

    return pl.pallas_call(
        body, grid=(S // tm,), name="inproj_bwd_x",
        in_specs=piece_specs + [_resident((D, NCOL)), _rows(tm, D), _resident((1, D)), _rows(tm, D), ANY],
        out_specs=[_rows(tm, D), pl.BlockSpec((8, D), lambda i: (0, 0)), ANY],
        out_shape=[jax.ShapeDtypeStruct((S, D), F32), jax.ShapeDtypeStruct((8, D), F32)] + landing,
        scratch_shapes=[pltpu.VMEM((tm, NCOL), BF16)] + sems,
        compiler_params=_params(("arbitrary",)),
    )(*pieces, w_bf, x, g1, dx2, pi)


def _inproj_bwd_w(h, dqs, dkvs, gates):
    out = None
    for k in range(NCHIP):
        units = range(k * UNITS_PER_CHUNK, (k + 1) * UNITS_PER_CHUNK)
        tk = 512 if units[0] < OFF_K // LANES else 1024
        nk = S // tk
        pieces, piece_specs, pick = _dproj_sources(units, dqs, dkvs, gates, tk)
        handed_on = [] if out is None else [out]

        def body(*refs, units=units, pick=pick, n_pieces=len(pieces), n_in=1 + len(pieces) + len(handed_on)):
            h_ref, piece_refs = refs[0], refs[1:1 + n_pieces]
            o_ref, tile, acc = refs[n_in:]
            i = pl.program_id(0)

            @pl.when(i == 0)
            def _():
                acc[...] = jnp.zeros_like(acc)

            sources = pick(piece_refs)
            for n, u in enumerate(units):
                tile[:, n * LANES:(n + 1) * LANES] = _dproj_unit(u, *sources, tk)
            acc[...] += _dot_tn(h_ref[...], tile[...])

            @pl.when(i == nk - 1)
            def _():
                o_ref[0] = acc[...].astype(BF16)

        out = pl.pallas_call(
            body, grid=(nk,), name=f"inproj_bwd_w{k}",
            in_specs=[_rows(tk, D)] + piece_specs + [ANY] * len(handed_on),
            out_specs=pl.BlockSpec((1, D, CHUNK), lambda i, k=k: (k, 0, 0)),
            out_shape=jax.ShapeDtypeStruct((NCHIP, D, CHUNK), BF16),
            input_output_aliases={1 + len(pieces): 0} if handed_on else {},
            scratch_shapes=[pltpu.VMEM((tk, CHUNK), BF16), pltpu.VMEM((D, CHUNK), F32)],
            compiler_params=_params(("arbitrary",)),
        )(h, *pieces, *handed_on)
    return out


ROW_FINAL_G, ROW_LOSS, ROW_LN_G, ROW_LN_B, ROW_CONV_B, ROW_TAPS = 0, 1, 2, 3, 4, 8
SMALL_ROWS = 8 + HALO
NDEV = 8


MESH = pl.DeviceIdType.MESH
ANY = pl.BlockSpec(memory_space=pl.ANY)
CHIP_FLIPS = ((1, 0), (0, 1), (1, 1))


def _pos():
    return lax.axis_index("x"), lax.axis_index("y"), lax.axis_index("c")


def _flip(v, f):
    return 1 - v if f else v


def _ds(start, size, align=None):
    return pl.ds(pl.multiple_of(start, align or size), size)


def _place_shards(wi, wo, cw, where):
    steps = 4

    def body(where_ref, wi_ref, wo_ref, cw_ref, wi_full, wo_full, cw_full):
        wi_full[...] = wi_ref[...].astype(BF16)
        wo_full[...] = wo_ref[...].astype(BF16)
        cw_full[...] = cw_ref[...]

    grid_spec = pltpu.PrefetchScalarGridSpec(
        num_scalar_prefetch=1, grid=(steps,),
        in_specs=[pl.BlockSpec((D // steps, CHUNK), lambda i, w: (i, 0)),
                  pl.BlockSpec((WOUT_SHARD // steps, D), lambda i, w: (i, 0)),
                  pl.BlockSpec((HALO, CONVW_SHARD), lambda i, w: (0, 0))],
        out_specs=[pl.BlockSpec((D // steps, CHUNK), lambda i, w: (i, w[0])),
                   pl.BlockSpec((WOUT_SHARD // steps, D), lambda i, w: (w[0] * steps + i, 0)),
                   pl.BlockSpec((HALO, CONVW_SHARD), lambda i, w: (0, w[0]))])
    return pl.pallas_call(
        body, grid_spec=grid_spec, name="place_shards",
        out_shape=[jax.ShapeDtypeStruct((D, NCOL), BF16), jax.ShapeDtypeStruct((WOUT_ROWS, D), BF16),
                   jax.ShapeDtypeStruct((HALO, D), F32)],
        compiler_params=_params(("arbitrary",)),
    )(where, wi, wo, cw)


W_IN, W_OUT, TAPS = range(3)
GATHER_SEMS = 12


def _gather_stages(fulls, send, recv):
    halves = {W_IN: D // 2, W_OUT: WOUT_SHARD // 2, TAPS: HALO // 2}
    x, y, c = _pos()
    chips = {"me": (x, y), "x": (1 - x, y), "y": (x, 1 - y), "diag": (1 - x, 1 - y)}
    SENT = ((("me", 0), "x"), (("me", 1), "x"), (("me", 1), "y"), (("me", 0), "y"), (("x", 0), "y"), (("y", 1), "x"))
    LANDS = ((("x", 0), "x"), (("x", 1), "x"), (("y", 1), "y"), (("y", 0), "y"), (("diag", 0), "y"), (("diag", 1), "x"))
    N_ICI = len(SENT)

    def region(n_th, whose, half, part):
        a, full = fulls[n_th]
        chip = 2 * chips[whose][0] + chips[whose][1]
        n = halves[a] // 2
        row = half * halves[a] + part * n
        if a == W_IN:
            return full.at[_ds(row, n), _ds(chip * CHUNK, CHUNK, 128)]
        if a == W_OUT:
            return full.at[_ds(chip * WOUT_SHARD + row, n), :]
        return full.at[_ds(row, n), _ds(chip * CONVW_SHARD, CONVW_SHARD, 128)]

    def copy(n_th, kind, piece, dev):
        k = GATHER_SEMS * n_th + kind
        return pltpu.make_async_remote_copy(src_ref=piece, dst_ref=piece, send_sem=send.at[k], recv_sem=recv.at[k],
                                            device_id=dev, device_id_type=MESH)

    def sent(a, k):
        if k < N_ICI:
            (whose, part), to = SENT[k]
            return copy(a, k, region(a, whose, c, part), (*chips[to], c))
        (whose, part), _ = LANDS[k - N_ICI]
        return copy(a, k, region(a, whose, c, part), (x, y, 1 - c))

    def wait_arrival(a, k):
        if k < N_ICI:
            (whose, part), frm = LANDS[k]
            copy(a, k, region(a, whose, c, part), (*chips[frm], c)).wait_recv()
        else:
            (whose, part), _ = LANDS[k - N_ICI]
            copy(a, k, region(a, whose, 1 - c, part), (x, y, 1 - c)).wait_recv()

    arrays = range(len(fulls))

    def own_to_neighbours():
        for a in arrays:
            for k in (0, 2, 1, 3):
                sent(a, k).start()

    def pass_on_neighbours():
        for a in arrays:
            for k, onward in ((0, 4), (2, 5), (1, None), (3, None)):
                wait_arrival(a, k)
                if onward is not None:
                    sent(a, onward).start()
                sent(a, k + N_ICI).start()

    def pass_on_diagonal():
        for a in arrays:
            for k in (4, 5):
                wait_arrival(a, k)
                sent(a, k + N_ICI).start()

    def finish():
        for a in arrays:
            for k in range(N_ICI, 2 * N_ICI):
                wait_arrival(a, k)
            for k in range(2 * N_ICI):
                sent(a, k).wait_send()

    return own_to_neighbours, pass_on_neighbours, pass_on_diagonal, finish


def _gather_sems(n_arrays):
    return [pltpu.SemaphoreType.DMA((GATHER_SEMS * n_arrays,)), pltpu.SemaphoreType.DMA((GATHER_SEMS * n_arrays,))]


def _gather_w_in(wi_full):
    def body(_wi, full, send, recv):
        for stage in _gather_stages([(W_IN, full)], send, recv):
            stage()

    return pl.pallas_call(
        body, name="gather_w_in", in_specs=[ANY], out_specs=ANY, input_output_aliases={0: 0},
        out_shape=jax.ShapeDtypeStruct((D, NCOL), BF16), scratch_shapes=_gather_sems(1),
    )(wi_full)


def _half_shape(a):
    return jax.ShapeDtypeStruct((NCHIP, a.shape[1] // 2, a.shape[2]) if a.ndim == 3 else a.shape, a.dtype)


def _exchange_halves(arrays, name):
    n = len(arrays)

    def body(*refs):
        srcs, dsts, (send, recv) = refs[:n], refs[n:2 * n], refs[2 * n:]
        x, y, c = _pos()
        cps = []
        for k, (s_, d_) in enumerate(zip(srcs, dsts)):
            if len(s_.shape) == 3:
                h = s_.shape[1] // 2
                s_ = s_.at[:, _ds((1 - c) * h, h), :]
            cps.append(pltpu.make_async_remote_copy(src_ref=s_, dst_ref=d_, send_sem=send.at[k], recv_sem=recv.at[k],
                                                    device_id=(x, y, 1 - c), device_id_type=MESH))
        for cp in cps:
            cp.start()
        for cp in cps:
            cp.wait()

    return pl.pallas_call(
        body, name=name, in_specs=[ANY] * n, out_specs=[ANY] * n, out_shape=[_half_shape(a) for a in arrays],
        scratch_shapes=[pltpu.SemaphoreType.DMA((n,)), pltpu.SemaphoreType.DMA((n,))],
    )(*arrays)


def _add_halves(arrays, received, name):
    n = len(arrays)

    def body(*refs):
        mine, theirs, outs = refs[:n], refs[n:2 * n], refs[2 * n:]
        c = lax.axis_index("c")
        for m_, t_, o_ in zip(mine, theirs, outs):
            if len(m_.shape) == 3:
                h = m_.shape[1] // 2
                o_[0] = (m_[0, _ds(c * h, h), :].astype(F32) + t_[0].astype(F32)).astype(o_.dtype)
            else:
                o_[...] = m_[...] + t_[...]

    def spec(shape):
        if len(shape) == 3:
            return pl.BlockSpec((1,) + tuple(shape[1:]), lambda k: (k, 0, 0))
        return pl.BlockSpec(tuple(shape), lambda k: (0, 0))

    halves = [_half_shape(a) for a in arrays]
    return pl.pallas_call(
        body, grid=(NCHIP,), name=name,
        in_specs=[spec(a.shape) for a in arrays] + [spec(h.shape) for h in halves],
        out_specs=[spec(h.shape) for h in halves], out_shape=halves,
        compiler_params=_params(("arbitrary",)),
    )(*arrays, *received)


def _chip_exchange_copies(srcs, dsts, send, recv):
    x, y, c = _pos()
    me = 2 * x + y
    pairs = []
    for a in range(len(srcs)):
        for j, (fx, fy) in enumerate(CHIP_FLIPS):
            px, py = _flip(x, fx), _flip(y, fy)
            peer = 2 * px + py
            k = len(CHIP_FLIPS) * a + j
            out = pltpu.make_async_remote_copy(
                src_ref=srcs[a].at[peer] if len(srcs[a].shape) == 3 else srcs[a], dst_ref=dsts[a].at[me],
                send_sem=send.at[k], recv_sem=recv.at[k], device_id=(px, py, c), device_id_type=MESH)
            got = dsts[a].at[peer]
            arrival = pltpu.make_async_remote_copy(
                src_ref=got, dst_ref=got, send_sem=send.at[k], recv_sem=recv.at[k],
                device_id=(px, py, c), device_id_type=MESH)
            pairs.append((out, arrival))
    return pairs


def _sum_chips(ri, ro, rs, pi, po, ps, where):
    def body(w_ref, ri_ref, ro_ref, rs_ref, pi_ref, po_ref, ps_ref, gi_ref, go_ref, gs_ref, g5_ref, loss_ref,
             acc_i, acc_o, acc_s):
        k = pl.program_id(0)
        accs = (acc_i, acc_o, acc_s)

        @pl.when(k == 0)
        def _():
            for acc in accs:
                acc[...] = jnp.zeros_like(acc)

        @pl.when(k == w_ref[0])
        def _():
            for acc, val in zip(accs, (pi_ref[0], po_ref[0], ps_ref[...])):
                acc[...] += val.astype(F32)

        @pl.when(k != w_ref[0])
        def _():
            for acc, ref in zip(accs, (ri_ref, ro_ref, rs_ref)):
                acc[...] += ref[0].astype(F32)

        @pl.when(k == NCHIP - 1)
        def _():
            gi_ref[0] = acc_i[...]
            go_ref[0] = acc_o[...]
            gs_ref[...] = acc_s[...]
            g5_ref[...] = jnp.zeros_like(g5_ref)
            for i, row in enumerate((ROW_CONV_B, ROW_LN_G, ROW_LN_B, ROW_FINAL_G)):
                g5_ref[i + 1:i + 2, :] = acc_s[row:row + 1, :]
            loss = jnp.sum(acc_s[ROW_LOSS:ROW_LOSS + 1, :], axis=1, keepdims=True)
            loss_ref[...] = jnp.broadcast_to(loss, loss_ref.shape)

    def sent(k, w):
        return jnp.where(k == w[0], (k + 1) % NCHIP, k)

    hi, ho = D // 2, WOUT_SHARD // 2
    const = lambda shape: pl.BlockSpec(shape, lambda k, w: (0,) * len(shape))
    grid_spec = pltpu.PrefetchScalarGridSpec(
        num_scalar_prefetch=1, grid=(NCHIP,),
        in_specs=[pl.BlockSpec((1, hi, CHUNK), lambda k, w: (sent(k, w), 0, 0)),
                  pl.BlockSpec((1, ho, D), lambda k, w: (sent(k, w), 0, 0)),
                  pl.BlockSpec((1, SMALL_ROWS, D), lambda k, w: (sent(k, w), 0, 0)),
                  pl.BlockSpec((1, hi, CHUNK), lambda k, w: (w[0], 0, 0)),
                  pl.BlockSpec((1, ho, D), lambda k, w: (w[0], 0, 0)),
                  const((SMALL_ROWS, D))],
        out_specs=[pl.BlockSpec((1, hi, CHUNK), lambda k, w: (w[1], 0, 0)),
                   pl.BlockSpec((1, ho, D), lambda k, w: (w[1], 0, 0)),
                   const((SMALL_ROWS, D)), const((8, D)), const((8, LANES))],
        scratch_shapes=[pltpu.VMEM((hi, CHUNK), F32), pltpu.VMEM((ho, D), F32), pltpu.VMEM((SMALL_ROWS, D), F32)])
    return pl.pallas_call(
        body, grid_spec=grid_spec, name="sum_chips",
        out_shape=[jax.ShapeDtypeStruct((2, hi, CHUNK), F32), jax.ShapeDtypeStruct((2, ho, D), F32),
                   jax.ShapeDtypeStruct((SMALL_ROWS, D), F32), jax.ShapeDtypeStruct((8, D), F32),
                   jax.ShapeDtypeStruct((8, LANES), F32)],
        compiler_params=_params(("arbitrary",)),
    )(where, ri, ro, rs, pi, po, ps)


def _exchange_results(gi2, go2, st):
    flips = [(fx, fy, fc) for fx in (0, 1) for fy in (0, 1) for fc in (0, 1)][1:]

    def body(_gi, _go, st_ref, gi_ref, go_ref, all_ref, send, recv, lsem):
        x, y, c = _pos()
        sib = (x, y, 1 - c)

        def half(k, ref, slot):
            return pltpu.make_async_remote_copy(src_ref=ref.at[slot], dst_ref=ref.at[slot], send_sem=send.at[k],
                                                recv_sem=recv.at[k], device_id=sib, device_id_type=MESH)

        def stat(k, src, slot, dev):
            return pltpu.make_async_remote_copy(src_ref=src, dst_ref=all_ref.at[slot], send_sem=send.at[k],
                                                recv_sem=recv.at[k], device_id=dev, device_id_type=MESH)

        mine = pltpu.make_async_copy(st_ref, all_ref.at[4 * x + 2 * y + c], lsem)
        mine.start()
        sends = [half(k, ref, c) for k, ref in enumerate((gi_ref, go_ref))]
        peers = [(_flip(x, fx), _flip(y, fy), _flip(c, fc)) for fx, fy, fc in flips]
        sends += [stat(2 + k, st_ref, 4 * x + 2 * y + c, dev) for k, dev in enumerate(peers)]
        for cp in sends:
            cp.start()
        for k, ref in enumerate((gi_ref, go_ref)):
            half(k, ref, 1 - c).wait_recv()
        for k, (px, py, pc) in enumerate(peers):
            slot = 4 * px + 2 * py + pc
            stat(2 + k, all_ref.at[slot], slot, (px, py, pc)).wait_recv()
        for cp in sends:
            cp.wait_send()
        mine.wait()

    n = 2 + len(flips)
    return pl.pallas_call(
        body, name="exchange_results",
        in_specs=[ANY, ANY, ANY], out_specs=[ANY, ANY, ANY], input_output_aliases={0: 0, 1: 1},
        out_shape=[jax.ShapeDtypeStruct((2, D // 2, CHUNK), F32), jax.ShapeDtypeStruct((2, WOUT_SHARD // 2, D), F32),
                   jax.ShapeDtypeStruct((NDEV, 8, D), F32)],
        scratch_shapes=[pltpu.SemaphoreType.DMA((n,)), pltpu.SemaphoreType.DMA((n,)), pltpu.SemaphoreType.DMA],
    )(gi2, go2, st)


def _adamw_math(w, g, m, v):
    m2 = ADAM_B1 * m + (1.0 - ADAM_B1) * g
    v2 = ADAM_B2 * v + (1.0 - ADAM_B2) * (g * g)
    m_hat = m2 / (1.0 - ADAM_B1 ** ADAM_STEP)
    v_hat = v2 / (1.0 - ADAM_B2 ** ADAM_STEP)
    delta = -ADAM_LR * (m_hat / (jnp.sqrt(v_hat) + ADAM_EPS) + ADAM_WD * w)
    return delta, m2, v2


def _adamw(w, g, m, v, name):
    rows, cols = w.shape
    tm = 256 if rows % 256 == 0 else rows

    def body(w_ref, g_ref, m_ref, v_ref, d_ref, m2_ref, v2_ref):
        d_ref[...], m2_ref[...], v2_ref[...] = _adamw_math(w_ref[...], g_ref[...], m_ref[...], v_ref[...])

    shape = jax.ShapeDtypeStruct(w.shape, F32)
    return pl.pallas_call(
        body, grid=(rows // tm,), name=name,
        in_specs=[_rows(tm, cols)] * 4, out_specs=[_rows(tm, cols)] * 3, out_shape=[shape] * 3,
        compiler_params=_params(("arbitrary",)),
    )(w, g, m, v)


def _adamw_vectors(g5, first_parts, ws, ms, vs):
    n = len(ws)

    def body(g_ref, parts_ref, *refs):
        ins, g0_ref, outs = refs[:3 * n], refs[3 * n], refs[3 * n + 1:]
        g0 = parts_ref[0, 0:1, :]
        for dev in range(1, NDEV):
            g0 = g0 + parts_ref[dev, 0:1, :]
        g0_ref[...] = g0
        for i in range(n):
            g = g0 if i == 0 else g_ref[i:i + 1, :]
            res = _adamw_math(ins[i][...], g, ins[n + i][...], ins[2 * n + i][...])
            for kind in range(3):
                outs[kind * n + i][...] = res[kind]

    shape = jax.ShapeDtypeStruct((1, D), F32)
    return pl.pallas_call(body, name="adamw_vectors", out_shape=[shape] * (1 + 3 * n), compiler_params=_params())(
        g5, first_parts, *ws, *ms, *vs)


def kernel(x, norm_g, w_in, conv_w, conv_b, conv_ln_g, conv_ln_b, w_out, final_norm_g, loss_target, m_norm_g, m_w_in, m_conv_w, m_conv_b, m_conv_ln_g, m_conv_ln_b, m_w_out, m_final_norm_g, v_norm_g, v_w_in, v_conv_w, v_conv_b, v_conv_ln_g, v_conv_ln_b, v_w_out, v_final_norm_g):
    chip = 2 * lax.axis_index("x") + lax.axis_index("y")
    where = jnp.stack([chip, lax.axis_index("c")]).astype(jnp.int32)
    taps_shard = jnp.pad(conv_w[0], ((0, HALO - CONV_K), (0, 0)))
    wi_full, wo_full, cw_full = _place_shards(w_in[0], w_out[0], taps_shard, where)
    wi_full = _gather_w_in(wi_full)

    gf = final_norm_g[None]
    xb = x[0]
    h, q, k, v, a_gate, c_val, c_glu, c_gate, wo_full, cw_full = _inproj_fwd(xb, norm_g, wi_full, wo_full, cw_full)
    tables = [_bias_table(d) for d in PATTERNS]
    o, lse, y_att = _attn_fwd(q, k, v, tables, a_gate)
    u, y_conv = _conv_fwd(c_val, c_glu, c_gate, cw_full, conv_b, conv_ln_g, conv_ln_b)
    dx2, dy_att, du, dc_gate, dw_out, st_out = _outproj_loss(
        y_att, y_conv, wo_full, xb, loss_target[0], gf, u, c_gate, conv_ln_g, conv_ln_b)
    dc_val, dc_glu, dconv_w = _conv_bwd_taps(du, c_val, c_glu, cw_full)

    early = [dw_out.reshape(NCHIP, WOUT_SHARD, D), jnp.concatenate([st_out, dconv_w], axis=0)]
    po, ps = _add_halves(early, _exchange_halves(early, "exchange_halves_early"), "add_halves_early")
    do, da_gate, delta, ro, rs = _attn_gate_bwd(dy_att, o, a_gate, _head_sum_selectors(), [po, ps])
    dqs, dkvs = zip(*[_attn_bwd(q, k, v, do, lse, delta, t, d) for t, d in zip(tables, PATTERNS)])

    dproj_pieces = (dqs, dkvs, (da_gate, dc_val, dc_glu, dc_gate))
    late = [_inproj_bwd_w(h, *dproj_pieces)]
    (pi,) = _add_halves(late, _exchange_halves(late, "exchange_halves"), "add_halves")
    grad_x, st_in, ri = _inproj_bwd_x(*dproj_pieces, wi_full, xb, norm_g, dx2, pi)
    gi2, go2, g_small, g5, loss8 = _sum_chips(ri, ro, rs, pi, po, ps, where)
    gi2, go2, norm_g_parts = _exchange_results(gi2, go2, st_in)
    g_w_in = gi2.reshape(D, CHUNK)
    g_w_out = go2.reshape(WOUT_SHARD, D)
    g_taps = lax.dynamic_slice(g_small, (ROW_TAPS, chip * CONVW_SHARD), (CONV_K, CONVW_SHARD))

    d_w_in, m2_w_in, v2_w_in = _adamw(w_in[0], g_w_in, m_w_in[0], v_w_in[0], "adamw_w_in")
    d_w_out, m2_w_out, v2_w_out = _adamw(w_out[0], g_w_out, m_w_out[0], v_w_out[0], "adamw_w_out")
    d_taps, m2_taps, v2_taps = _adamw(conv_w[0], g_taps, m_conv_w[0], v_conv_w[0], "adamw_conv_w")
    g_norm, *vec = _adamw_vectors(
        g5, norm_g_parts,
        (norm_g, conv_b, conv_ln_g, conv_ln_b, gf),
        (m_norm_g, m_conv_b, m_conv_ln_g, m_conv_ln_b, m_final_norm_g[None]),
        (v_norm_g, v_conv_b, v_conv_ln_g, v_conv_ln_b, v_final_norm_g[None]))
    d_vec, m2_vec, v2_vec = vec[0:5], vec[5:10], vec[10:15]

    def weight_order(ng, wi, cw, cb, lg, lb, wo, fg):
        return (ng, wi[None], cw[None], cb, lg, lb, wo[None], fg[0])

    grads = weight_order(g_norm, g_w_in, g_taps, g5[1:2], g5[2:3], g5[3:4], g_w_out, g5[4:5])
    deltas = weight_order(d_vec[0], d_w_in, d_taps, d_vec[1], d_vec[2], d_vec[3], d_w_out, d_vec[4])
    new_m = weight_order(m2_vec[0], m2_w_in, m2_taps, m2_vec[1], m2_vec[2], m2_vec[3], m2_w_out, m2_vec[4])
    new_v = weight_order(v2_vec[0], v2_w_in, v2_taps, v2_vec[1], v2_vec[2], v2_vec[3], v2_w_out, v2_vec[4])
    return (loss8[0, 0], grad_x[None], *grads, *deltas, *new_m, *new_v)
```

```python
import jax
import jax.numpy as jnp
from jax import lax
from jax.experimental import pallas as pl
from jax.experimental.pallas import tpu as pltpu

F32 = jnp.float32
BF16 = jnp.bfloat16

S = 4096
D = 1024
LANES = 128
HD = 64
NKV = 4
GQ = 4
KVW = NKV * HD
NCOL = 5632
CONV_K = 31
HALO = 32
BLK = 128
PATTERNS = (1, 4, 16)
NORM_EPS = 1e-6
LN_EPS = 1e-5
NEG = -1e30
OFF_Q, OFF_K, OFF_AG, OFF_CV, OFF_CG, OFF_CGATE = 0, 1024, 1536, 2560, 3584, 4608
NCHIP = 4
CHUNK = NCOL // NCHIP
WOUT_ROWS = 2 * D
WOUT_SHARD = WOUT_ROWS // NCHIP
CONVW_SHARD = D // NCHIP

ADAM_LR, ADAM_B1, ADAM_B2, ADAM_EPS, ADAM_WD, ADAM_STEP = 0.001, 0.9, 0.999, 1e-08, 0.01, 10

VMEM_LIMIT = 56 * 1024 * 1024


def _params(sem=None, vmem=VMEM_LIMIT):
    return pltpu.CompilerParams(dimension_semantics=sem, vmem_limit_bytes=vmem)


def _sigmoid(a):
    return 0.5 * jnp.tanh(0.5 * a) + 0.5


def _rows(tm, width):
    return pl.BlockSpec((tm, width), lambda i: (i, 0))


def _slabs(n):
    return jax.ShapeDtypeStruct((n, S, LANES), F32)


def _slab_rows(n, tm):
    return pl.BlockSpec((n, tm, LANES), lambda i: (0, i, 0))


def _resident(shape):
    return pl.BlockSpec(shape, lambda *_: (0,) * len(shape), pipeline_mode=pl.Buffered(1))


def _dot(a, b):
    return jnp.dot(a, b, preferred_element_type=F32)


def _dot_nt(a, b):
    return lax.dot_general(a, b, (((1,), (1,)), ((), ())), preferred_element_type=F32)


def _dot_tn(a, b):
    return lax.dot_general(a, b, (((0,), (0,)), ((), ())), preferred_element_type=F32)


def _inproj_fwd(x, g1, w_bf, wo_full, cw_full):
    tm = 512
    steps = S // tm

    def body(x_ref, g_ref, w_ref, _wo, _cw, h_ref, q_ref, k_ref, v_ref, ag_ref, cv_ref, cg_ref, cgate_ref,
             wo_ref, cw_ref, send, recv):
        i = pl.program_id(0)
        stages = _gather_stages([(W_OUT, wo_ref), (TAPS, cw_ref)], send, recv)
        for stage, step in zip(stages[:3], (0, steps // 2 - 1, steps - 2)):
            pl.when(i == step)(stage)
        xt = x_ref[...]
        r = lax.rsqrt(jnp.mean(xt * xt, axis=-1, keepdims=True) + NORM_EPS)
        h = (xt * r * g_ref[...]).astype(BF16)
        h_ref[...] = h
        q = _dot(h, w_ref[:, OFF_Q:OFF_Q + D]) * (HD ** -0.5)
        kv = _dot(h, w_ref[:, OFF_K:OFF_K + 2 * KVW])
        for sl in range(D // LANES):
            q_ref[sl] = q[:, sl * LANES:(sl + 1) * LANES]
        for sl in range(KVW // LANES):
            k_ref[sl] = kv[:, sl * LANES:(sl + 1) * LANES]
            v_ref[sl] = kv[:, KVW + sl * LANES:KVW + (sl + 1) * LANES]
        ag_ref[...] = _dot(h, w_ref[:, OFF_AG:OFF_AG + D])
        cv_ref[...] = _dot(h, w_ref[:, OFF_CV:OFF_CV + D])
        cg_ref[...] = _dot(h, w_ref[:, OFF_CG:OFF_CG + D])
        cgate_ref[...] = _dot(h, w_ref[:, OFF_CGATE:OFF_CGATE + D])
        pl.when(i == steps - 1)(stages[3])

    big = jax.ShapeDtypeStruct((S, D), F32)
    return pl.pallas_call(
        body, grid=(steps,), name="inproj_fwd",
        in_specs=[_rows(tm, D), _resident((1, D)), _resident((D, NCOL)), ANY, ANY],
        out_specs=[_rows(tm, D), _slab_rows(D // LANES, tm), _slab_rows(KVW // LANES, tm), _slab_rows(KVW // LANES, tm),
                   _rows(tm, D), _rows(tm, D), _rows(tm, D), _rows(tm, D), ANY, ANY],
        out_shape=[jax.ShapeDtypeStruct((S, D), BF16), _slabs(D // LANES), _slabs(KVW // LANES), _slabs(KVW // LANES),
                   big, big, big, big,
                   jax.ShapeDtypeStruct((WOUT_ROWS, D), BF16), jax.ShapeDtypeStruct((HALO, D), F32)],
        input_output_aliases={3: 8, 4: 9},
        scratch_shapes=_gather_sems(2),
        compiler_params=_params(("arbitrary",)),
    )(x, g1, w_bf, wo_full, cw_full)


def _bias_table(d):
    h = jnp.arange(NKV * GQ, dtype=F32)
    slopes = jnp.exp2(-8.0 * (h + 1.0) / (NKV * GQ))
    qi = jnp.arange(BLK)[:, None]
    kj = jnp.arange(2 * BLK)[None, :]
    dist = BLK + qi - kj
    window = (dist >= 0) & (dist <= BLK)
    bias = -slopes[:, None, None] * (dist * d).astype(F32)[None]
    has_prev = jnp.stack([jnp.broadcast_to(kj >= BLK, (BLK, 2 * BLK)), jnp.ones((BLK, 2 * BLK), bool)])
    valid = window[None] & has_prev
    tab = jnp.where(valid[:, None], bias[None], NEG)
    return tab.reshape(2, NKV, GQ * BLK, 2 * BLK)


def _sub_rows(start, d):
    if d == 1:
        return pl.ds(pl.multiple_of(start, BLK), BLK)
    return pl.ds(start, BLK, stride=d)


CHUNK_ROWS = 2048
BLOCKS_PER_CHUNK = CHUNK_ROWS // BLK


def _low_lanes(rows=BLK):
    return lax.broadcasted_iota(jnp.int32, (rows, LANES), 1) < HD


def _block_start(idx, d):
    shift = d.bit_length() - 1
    b, r = lax.shift_right_logical(idx, shift), lax.bitwise_and(idx, d - 1)
    start = b * (BLK * d) + r
    return b, start, jnp.maximum(start - BLK * d, r)


def _stack_heads(ref, rows):
    low = _low_lanes()
    t0, t1 = ref[0, rows, :], ref[1, rows, :]
    return jnp.concatenate([jnp.where(low, t0, 0.0), jnp.where(low, 0.0, t0),
                            jnp.where(low, t1, 0.0), jnp.where(low, 0.0, t1)], axis=0).astype(BF16)


def _unstack_heads(dup):
    low = _low_lanes()
    return (jnp.where(low, dup[0:BLK], dup[BLK:2 * BLK]), jnp.where(low, dup[2 * BLK:3 * BLK], dup[3 * BLK:4 * BLK]))


def _kv_dup(ref, prow, rows, odd):
    t = jnp.concatenate([ref[0, prow, :], ref[0, rows, :]], axis=0)
    swapped = pltpu.roll(t, HD, axis=1)
    keep = jnp.logical_xor(_low_lanes(2 * BLK), odd)
    return jnp.where(keep, t, swapped).astype(BF16)


PIECES = 3


def _by_head(tiles):
    lane = lax.broadcasted_iota(jnp.int32, tiles[0].shape, 1)
    out = tiles[0]
    for g in range(1, GQ):
        out = jnp.where(lax.bitwise_and(lane, GQ - 1) == g, tiles[g], out)
    return out


def _minus_in_pieces(x):
    lane = lax.broadcasted_iota(jnp.int32, x.shape, 1)
    hi = (-x).astype(BF16).astype(F32)
    rest = -x - hi
    mid = rest.astype(BF16).astype(F32)
    lo = (rest - mid).astype(BF16).astype(F32)
    return jnp.where(lane < GQ, hi, jnp.where(lane < 2 * GQ, mid, jnp.where(lane < PIECES * GQ, lo, 0.0)))


def _attn_fwd(q, k, v, tables, a_gate):
    tm = 256
    width = GQ * HD

    lane_out = jnp.arange(LANES)[None, :] // HD
    spread_sel = jnp.stack([jnp.arange(LANES)[:, None] == 2 * half + lane_out for half in range(2)]).astype(BF16)

    def body(q_ref, k_ref, v_ref, b1_ref, b2_ref, b3_ref, ag_ref, sel_ref, o_ref, lse_ref, y_ref, op, lp):
        odd = pl.program_id(0) % 2 == 1
        chunk = pl.program_id(1)
        ones = jnp.ones((2 * BLK, LANES), BF16)

        for pat, (d, b_ref) in enumerate(zip(PATTERNS, (b1_ref, b2_ref, b3_ref))):
            def block(idx, carry, pat=pat, d=d, b_ref=b_ref):
                b, start, pstart = _block_start(chunk * BLOCKS_PER_CHUNK + idx, d)
                rows, prow = _sub_rows(start, d), _sub_rows(pstart, d)
                mine = _sub_rows(start - chunk * CHUNK_ROWS, d)
                qs = _stack_heads(q_ref, mine)
                kw = _kv_dup(k_ref, prow, rows, odd)
                vw = _kv_dup(v_ref, prow, rows, odd)
                s = _dot_nt(qs, kw) + b_ref[jnp.minimum(b, 1), 0]
                m = jnp.max(s, axis=1, keepdims=True)
                p = jnp.exp(s - m).astype(BF16)
                ol = _dot(p, jnp.concatenate([vw, ones], axis=1))
                l = ol[:, LANES:]
                op[pat, 0, mine, :], op[pat, 1, mine, :] = _unstack_heads(ol[:, :LANES] / l)
                lp[pat, mine, :] = _by_head([(m + jnp.log(l))[g * BLK:(g + 1) * BLK] for g in range(GQ)])
                return carry

            lax.fori_loop(0, BLOCKS_PER_CHUNK, block, 0, unroll=2)

        def mix(t, carry):
            r = pl.ds(pl.multiple_of(t * tm, tm), tm)
            a, b, c = lp[0, r, :], lp[1, r, :], lp[2, r, :]
            m = jnp.maximum(jnp.maximum(a, b), c)
            ea, eb, ec = jnp.exp(a - m), jnp.exp(b - m), jnp.exp(c - m)
            den = ea + eb + ec
            lse_ref[0, r, :] = _minus_in_pieces(m + jnp.log(den))
            inv = 1.0 / den
            for half in range(2):
                def spread(w):
                    hi = w.astype(BF16)
                    lo = (w - hi.astype(F32)).astype(BF16)
                    return _dot(hi, sel_ref[half]) + _dot(lo, sel_ref[half])

                o = (spread(ea * inv) * op[0, half, r, :] + spread(eb * inv) * op[1, half, r, :]
                     + spread(ec * inv) * op[2, half, r, :])
                o_ref[half, r, :] = o
                cols = slice(half * LANES, (half + 1) * LANES)
                ag = ag_ref[r, cols]
                y_ref[r, cols] = (o * (ag * _sigmoid(ag))).astype(BF16)
            return carry

        lax.fori_loop(0, CHUNK_ROWS // tm, mix, 0)

    q_like = pl.BlockSpec((2, CHUNK_ROWS, LANES), lambda j, c: (j, c, 0))
    per_kv = pl.BlockSpec((1, CHUNK_ROWS, LANES), lambda j, c: (j, c, 0))
    kv = pl.BlockSpec((1, S, LANES), lambda j, c: (j // 2, 0, 0))
    bias_spec = pl.BlockSpec((2, 1, GQ * BLK, 2 * BLK), lambda j, c: (0, j, 0, 0))
    group_cols = pl.BlockSpec((CHUNK_ROWS, width), lambda j, c: (c, j))
    return pl.pallas_call(
        body, grid=(NKV, S // CHUNK_ROWS), name="attn_fwd",
        in_specs=[q_like, kv, kv, bias_spec, bias_spec, bias_spec, group_cols,
                  pl.BlockSpec((2, LANES, LANES), lambda j, c: (0, 0, 0))],
        out_specs=[q_like, per_kv, group_cols],
        out_shape=[_slabs(D // LANES), _slabs(NKV), jax.ShapeDtypeStruct((S, D), BF16)],
        scratch_shapes=[pltpu.VMEM((len(PATTERNS), 2, CHUNK_ROWS, LANES), F32),
                        pltpu.VMEM((len(PATTERNS), CHUNK_ROWS, LANES), F32)],
        compiler_params=_params(("arbitrary", "arbitrary")),
    )(q, k, v, *tables, a_gate, spread_sel)


def _head_sum_selectors():
    lane_in = jnp.arange(LANES)[:, None] // HD
    return jnp.stack([jnp.broadcast_to(lane_in == h, (LANES, LANES)) for h in range(2)]).astype(BF16)


def _attn_gate_bwd(dy_att, o, a_gate, selectors, chip_sums):
    tm = 256
    last = S // tm - 1
    landing, sems = _exchange_results_of(chip_sums)
    n_sums = len(chip_sums)

    def body(dy_ref, o_ref, ag_ref, e_ref, *refs):
        sums, (do_ref, dag_ref, delta_ref), refs = refs[:n_sums], refs[n_sums:n_sums + 3], refs[n_sums + 3:]
        landed, (send, recv) = refs[:n_sums], refs[n_sums:]
        i = pl.program_id(0)
        copies = _chip_exchange_copies(sums, landed, send, recv)
        _start_exchange(copies, i == 0)
        for j in range(NKV):
            deltas = []
            for sl in (2 * j, 2 * j + 1):
                cols = slice(sl * LANES, (sl + 1) * LANES)
                dy, ag, o_ = dy_ref[:, cols], ag_ref[:, cols], o_ref[sl]
                sg = _sigmoid(ag)
                do = dy * (ag * sg)
                do_ref[sl] = do
                dag_ref[:, cols] = (dy * o_ * (sg * (1.0 + ag * (1.0 - sg)))).astype(BF16)
                prod = do * o_
                hi = prod.astype(BF16)
                lo = (prod - hi.astype(F32)).astype(BF16)
                deltas += [_dot(hi, e_ref[h]) + _dot(lo, e_ref[h]) for h in range(2)]
            delta_ref[j] = _minus_in_pieces(_by_head(deltas))
        _finish_exchange(copies, i == last)

    return pl.pallas_call(
        body, grid=(S // tm,), name="attn_gate_bwd",
        in_specs=[_rows(tm, D), _slab_rows(D // LANES, tm), _rows(tm, D), _resident((2, LANES, LANES))] + [ANY] * n_sums,
        out_specs=[_slab_rows(D // LANES, tm), _rows(tm, D), _slab_rows(NKV, tm)] + [ANY] * n_sums,
        out_shape=[_slabs(D // LANES), jax.ShapeDtypeStruct((S, D), BF16), _slabs(NKV)] + landing,
        scratch_shapes=sems,
        compiler_params=_params(("arbitrary",)),
    )(dy_att, o, a_gate, selectors, *chip_sums)


def _own_pieces(tile):
    lane = lax.broadcasted_iota(jnp.int32, tile.shape, 1)
    head = jnp.where(lane < PIECES * GQ, lax.bitwise_and(lane, GQ - 1), -1)
    return jnp.concatenate([jnp.where(head == g, tile, 0.0) for g in range(GQ)], axis=0).astype(BF16)


def _attn_bwd(q, k, v, do, lse, delta, bias, d):
    def body(q_ref, do_ref, l_ref, dl_ref, k_ref, v_ref, b_ref, dq_ref, dkv_ref, acc):
        odd = pl.program_id(0) % 2 == 1
        chunk = pl.program_id(1)
        ones = (lax.broadcasted_iota(jnp.int32, (2 * BLK, LANES), 1) < PIECES * GQ).astype(BF16)

        def in_acc(block_idx):
            return pl.ds(pl.multiple_of(block_idx * BLK, BLK), BLK)

        @pl.when(chunk == 0)
        def _():
            acc[...] = jnp.zeros_like(acc)

        def block(idx, carry):
            idx = chunk * BLOCKS_PER_CHUNK + idx
            b, start, pstart = _block_start(idx, d)
            rows, prow = _sub_rows(start, d), _sub_rows(pstart, d)
            mine = _sub_rows(start - chunk * CHUNK_ROWS, d)
            qs = _stack_heads(q_ref, mine)
            dos = _stack_heads(do_ref, mine)
            kw = _kv_dup(k_ref, prow, rows, odd)
            vw = _kv_dup(v_ref, prow, rows, odd)
            s = _dot_nt(jnp.concatenate([qs, _own_pieces(l_ref[0, mine, :])], axis=1),
                        jnp.concatenate([kw, ones], axis=1)) + b_ref[jnp.minimum(b, 1), 0]
            p = jnp.exp(s)
            dv2 = _dot_tn(p.astype(BF16), dos)
            dp = _dot_nt(jnp.concatenate([dos, _own_pieces(dl_ref[0, mine, :])], axis=1),
                         jnp.concatenate([vw, ones], axis=1))
            ds = (p * dp).astype(BF16)
            dq_ref[0, mine, :], dq_ref[1, mine, :] = _unstack_heads(_dot(ds, kw))
            dk2 = _dot_tn(ds, qs)
            dkv = jnp.where(_low_lanes(2 * BLK), dk2 + pltpu.roll(dk2, HD, axis=1), dv2 + pltpu.roll(dv2, HD, axis=1))
            acc[in_acc(idx), :] = acc[in_acc(idx), :] + dkv[BLK:]
            before = jnp.where(b >= 1, idx - d, idx)
            acc[in_acc(before), :] = acc[in_acc(before), :] + dkv[:BLK]
            return carry

        lax.fori_loop(0, BLOCKS_PER_CHUNK, block, 0, unroll=16)

        @pl.when(chunk == S // CHUNK_ROWS - 1)
        def _():
            def place(idx, carry):
                _, start, _ = _block_start(idx, d)
                dkv_ref[0, _sub_rows(start, d), :] = acc[in_acc(idx), :]
                return carry

            lax.fori_loop(0, S // BLK, place, 0, unroll=4)

    q_like = pl.BlockSpec((2, CHUNK_ROWS, LANES), lambda j, c: (j, c, 0))
    pieces = pl.BlockSpec((1, CHUNK_ROWS, LANES), lambda j, c: (j, c, 0))
    kv = pl.BlockSpec((1, S, LANES), lambda j, c: (j // 2, 0, 0))
    per_kv = pl.BlockSpec((1, S, LANES), lambda j, c: (j, 0, 0))
    bias_spec = pl.BlockSpec((2, 1, GQ * BLK, 2 * BLK), lambda j, c: (0, j, 0, 0))
    return pl.pallas_call(
        body, grid=(NKV, S // CHUNK_ROWS), name=f"attn_bwd_d{d}",
        in_specs=[q_like, q_like, pieces, pieces, kv, kv, bias_spec],
        out_specs=[q_like, per_kv],
        out_shape=[_slabs(D // LANES), _slabs(NKV)],
        scratch_shapes=[pltpu.VMEM((S, LANES), F32)],
        compiler_params=_params(("arbitrary", "arbitrary")),
    )(q, do, lse, delta, k, v, bias)


CONV_T = 256


def _halo_before(i):
    return (jnp.maximum(i * (CONV_T // HALO) - 1, 0), 0)


def _halo_after(i):
    return (jnp.minimum((i + 1) * (CONV_T // HALO), S // HALO - 1), 0)


SUBLANES = 8
NCH = D // LANES
GROUP = SUBLANES * SUBLANES


def _comb(ref, cb, base):
    return ref[cb, pl.ds(base, SUBLANES, stride=SUBLANES), :]


def _taps(w_ref, cols):
    return [jnp.broadcast_to(w_ref[j:j + 1, cols], (SUBLANES, LANES)) for j in range(CONV_K)]


def _conv_fwd(c_val, c_glu, c_gate, conv_w, conv_b, ln_g, ln_b):
    T = CONV_T

    def body(cv_ref, cg_ref, cvh_ref, cgh_ref, gate_ref, w_ref, b_ref, lg_ref, lb_ref, u_ref, y_ref, win, us):
        i = pl.program_id(0)
        for cb in range(NCH):
            cols = slice(cb * LANES, (cb + 1) * LANES)
            win[cb, HALO:HALO + T, :] = cv_ref[:, cols] * _sigmoid(cg_ref[:, cols])
            win[cb, 0:HALO, :] = jnp.where(i > 0, cvh_ref[:, cols] * _sigmoid(cgh_ref[:, cols]), 0.0)
        for cb in range(NCH):
            cols = slice(cb * LANES, (cb + 1) * LANES)
            taps = _taps(w_ref, cols)
            bias = jnp.broadcast_to(b_ref[:, cols], (SUBLANES, LANES))

            def group(g, carry):
                for b in range(SUBLANES):
                    base = g * GROUP + b
                    acc = bias
                    for j in range(CONV_K):
                        acc = acc + taps[j] * _comb(win, cb, base + (HALO - (CONV_K - 1) + j))
                    us[cb, pl.ds(base, SUBLANES, stride=SUBLANES), :] = acc
                return carry

            lax.fori_loop(0, T // GROUP, group, 0)
        total = us[0]
        for cb in range(1, NCH):
            total = total + us[cb]
        mu = jnp.sum(total, axis=-1, keepdims=True) * (1.0 / D)
        sq = jnp.zeros((T, LANES), F32)
        for cb in range(NCH):
            uc = us[cb] - mu
            sq = sq + uc * uc
        rstd = lax.rsqrt(jnp.sum(sq, axis=-1, keepdims=True) * (1.0 / D) + LN_EPS)
        for cb in range(NCH):
            cols = slice(cb * LANES, (cb + 1) * LANES)
            u = us[cb]
            u_ref[:, cols] = u
            nrm = (u - mu) * rstd * lg_ref[:, cols] + lb_ref[:, cols]
            gate = gate_ref[:, cols]
            y_ref[:, cols] = (nrm * _sigmoid(nrm) * (gate * _sigmoid(gate))).astype(BF16)

    halo = pl.BlockSpec((HALO, D), _halo_before)
    return pl.pallas_call(
        body, grid=(S // T,), name="conv_fwd",
        in_specs=[_rows(T, D), _rows(T, D), halo, halo, _rows(T, D),
                  _resident((HALO, D)), _resident((1, D)), _resident((1, D)), _resident((1, D))],
        out_specs=[_rows(T, D), _rows(T, D)],
        out_shape=[jax.ShapeDtypeStruct((S, D), F32), jax.ShapeDtypeStruct((S, D), BF16)],
        scratch_shapes=[pltpu.VMEM((NCH, T + HALO, LANES), F32), pltpu.VMEM((NCH, T, LANES), F32)],
        compiler_params=_params(("arbitrary",)),
    )(c_val, c_glu, c_val, c_glu, c_gate, conv_w, conv_b, ln_g, ln_b)


def _conv_bwd_taps(du, c_val, c_glu, conv_w):
    T = CONV_T
    last = S // T - 1

    def body(du_ref, dua_ref, cv_ref, cg_ref, cvh_ref, cgh_ref, w_ref, dcv_ref, dcg_ref, dw_ref,
             hwin, dwin, dhs, dw_acc):
        i = pl.program_id(0)

        @pl.when(i == 0)
        def _():
            dw_acc[...] = jnp.zeros_like(dw_acc)

        for cb in range(NCH):
            cols = slice(cb * LANES, (cb + 1) * LANES)
            hwin[cb, HALO:HALO + T, :] = cv_ref[:, cols] * _sigmoid(cg_ref[:, cols])
            hwin[cb, 0:HALO, :] = jnp.where(i > 0, cvh_ref[:, cols] * _sigmoid(cgh_ref[:, cols]), 0.0)
            dwin[cb, 0:T, :] = du_ref[:, cols]
            dwin[cb, T:T + HALO, :] = jnp.where(i < last, dua_ref[:, cols], 0.0)
        for cb in range(NCH):
            cols = slice(cb * LANES, (cb + 1) * LANES)
            taps = _taps(w_ref, cols)

            def group_dh(g, carry):
                for b in range(SUBLANES):
                    base = g * GROUP + b
                    acc = jnp.zeros((SUBLANES, LANES), F32)
                    for j in range(CONV_K):
                        acc = acc + taps[j] * _comb(dwin, cb, base + (CONV_K - 1 - j))
                    dhs[cb, pl.ds(base, SUBLANES, stride=SUBLANES), :] = acc
                return carry

            lax.fori_loop(0, T // GROUP, group_dh, 0)

            def group_dw(g, sums):
                for b in range(SUBLANES):
                    base = g * GROUP + b
                    d = _comb(dwin, cb, base)
                    sums = tuple(sums[j] + d * _comb(hwin, cb, base + (HALO - (CONV_K - 1) + j))
                                 for j in range(CONV_K))
                return sums

            sums = lax.fori_loop(0, T // GROUP, group_dw, tuple(dw_acc[j, :, cols] for j in range(CONV_K)))
            for j in range(CONV_K):
                dw_acc[j, :, cols] = sums[j]
            dh = dhs[cb]
            cv, sg = cv_ref[:, cols], _sigmoid(cg_ref[:, cols])
            dcv_ref[:, cols] = (dh * sg).astype(BF16)
            dcg_ref[:, cols] = (dh * cv * (sg * (1.0 - sg))).astype(BF16)

        @pl.when(i == last)
        def _():
            dw_ref[...] = jnp.zeros_like(dw_ref)
            for j in range(CONV_K):
                dw_ref[j:j + 1, :] = jnp.sum(dw_acc[j], axis=0, keepdims=True)

    before = pl.BlockSpec((HALO, D), _halo_before)
    after = pl.BlockSpec((HALO, D), _halo_after)
    big = jax.ShapeDtypeStruct((S, D), BF16)
    return pl.pallas_call(
        body, grid=(S // T,), name="conv_bwd_taps",
        in_specs=[_rows(T, D), after, _rows(T, D), _rows(T, D), before, before, _resident((HALO, D))],
        out_specs=[_rows(T, D), _rows(T, D), pl.BlockSpec((HALO, D), lambda i: (0, 0))],
        out_shape=[big, big, jax.ShapeDtypeStruct((HALO, D), F32)],
        scratch_shapes=[pltpu.VMEM((NCH, T + HALO, LANES), F32), pltpu.VMEM((NCH, T + HALO, LANES), F32),
                        pltpu.VMEM((NCH, T, LANES), F32), pltpu.VMEM((CONV_K, SUBLANES, D), F32)],
        compiler_params=_params(("arbitrary",)),
    )(du, du, c_val, c_glu, c_val, c_glu, conv_w)


def _outproj_loss(y_att, y_conv, w_out_bf, x, target, gf, u, c_gate, ln_g, ln_b):
    tm = 256

    def body(ya_ref, yc_ref, w_ref, x_ref, t_ref, gf_ref, u_ref, gate_ref, lg_ref, lb_ref,
             dx2_ref, dya_ref, du_ref, dgate_ref, dw_ref, st_ref, acc):
        @pl.when(pl.program_id(0) == 0)
        def _():
            acc[...] = jnp.zeros_like(acc)
            st_ref[...] = jnp.zeros_like(st_ref)

        ya, yc = ya_ref[...], yc_ref[...]
        x2 = x_ref[...] + _dot(ya, w_ref[0:D, :]) + _dot(yc, w_ref[D:2 * D, :])
        r = lax.rsqrt(jnp.mean(x2 * x2, axis=-1, keepdims=True) + NORM_EPS)
        xn = x2 * r
        err = xn * gf_ref[...] - t_ref[...]
        dout = err * (1.0 / D)
        dxn = dout * gf_ref[...]
        dx2 = r * (dxn - xn * jnp.mean(dxn * xn, axis=-1, keepdims=True))
        dx2_ref[...] = dx2
        dx2b = dx2.astype(BF16)
        dya_ref[...] = _dot_nt(dx2b, w_ref[0:D, :])
        dy = _dot_nt(dx2b, w_ref[D:2 * D, :])
        acc[0:D, :] += _dot_tn(ya, dx2b)
        acc[D:2 * D, :] += _dot_tn(yc, dx2b)
        st_ref[ROW_FINAL_G:ROW_FINAL_G + 1, :] += jnp.sum(dout * xn, axis=0, keepdims=True)
        st_ref[ROW_LOSS:ROW_LOSS + 1, :] += jnp.sum(err * err, axis=0, keepdims=True) * (0.5 / D)

        u, gate = u_ref[...], gate_ref[...]
        mu = jnp.mean(u, axis=-1, keepdims=True)
        uc = u - mu
        rstd = lax.rsqrt(jnp.mean(uc * uc, axis=-1, keepdims=True) + LN_EPS)
        z = uc * rstd
        nrm = z * lg_ref[...] + lb_ref[...]
        sn, sg = _sigmoid(nrm), _sigmoid(gate)
        dgate_ref[...] = (dy * (nrm * sn) * (sg * (1.0 + gate * (1.0 - sg)))).astype(BF16)
        dn = dy * (gate * sg) * (sn * (1.0 + nrm * (1.0 - sn)))
        dz = dn * lg_ref[...]
        du = rstd * (dz - jnp.mean(dz, axis=-1, keepdims=True) - z * jnp.mean(dz * z, axis=-1, keepdims=True))
        du_ref[...] = du
        st_ref[ROW_LN_G:ROW_LN_G + 1, :] += jnp.sum(dn * z, axis=0, keepdims=True)
        st_ref[ROW_LN_B:ROW_LN_B + 1, :] += jnp.sum(dn, axis=0, keepdims=True)
        st_ref[ROW_CONV_B:ROW_CONV_B + 1, :] += jnp.sum(du, axis=0, keepdims=True)

        @pl.when(pl.program_id(0) == S // tm - 1)
        def _():
            dw_ref[...] = acc[...].astype(BF16)

    big = jax.ShapeDtypeStruct((S, D), F32)
    vec = _resident((1, D))
    return pl.pallas_call(
        body, grid=(S // tm,), name="outproj_loss",
        in_specs=[_rows(tm, D), _rows(tm, D), _resident((WOUT_ROWS, D)), _rows(tm, D), _rows(tm, D), vec,
                  _rows(tm, D), _rows(tm, D), vec, vec],
        out_specs=[_rows(tm, D), _rows(tm, D), _rows(tm, D), _rows(tm, D),
                   pl.BlockSpec((WOUT_ROWS, D), lambda i: (0, 0)), pl.BlockSpec((8, D), lambda i: (0, 0))],
        out_shape=[big, big, big, jax.ShapeDtypeStruct((S, D), BF16),
                   jax.ShapeDtypeStruct((WOUT_ROWS, D), BF16), jax.ShapeDtypeStruct((8, D), F32)],
        scratch_shapes=[pltpu.VMEM((WOUT_ROWS, D), F32)],
        compiler_params=_params(("arbitrary",)),
    )(y_att, y_conv, w_out_bf, x, target, gf, u, c_gate, ln_g, ln_b)


UNITS_PER_CHUNK = CHUNK // LANES


def _dproj_unit(u, dqs, dkvs, gates, rows):
    if u < OFF_K // LANES:
        return ((dqs[0][u] + dqs[1][u] + dqs[2][u]) * (HD ** -0.5)).astype(BF16)
    if u < OFF_AG // LANES:
        w = u - OFF_K // LANES
        ta, tb = (dkvs[0][j] + dkvs[1][j] + dkvs[2][j] for j in (2 * (w % 2), 2 * (w % 2) + 1))
        low = _low_lanes(rows)
        if w < 2:
            return jnp.where(low, ta, pltpu.roll(tb, HD, axis=1)).astype(BF16)
        return jnp.where(low, pltpu.roll(ta, HD, axis=1), tb).astype(BF16)
    g, sl = divmod(u - OFF_AG // LANES, D // LANES)
    return gates[g][:, sl * LANES:(sl + 1) * LANES]


def _dproj_sources(units, dqs, dkvs, gates, rows):
    use_q = any(u < OFF_K // LANES for u in units)
    use_kv = any(OFF_K // LANES <= u < OFF_AG // LANES for u in units)
    use_g = sorted({(u - OFF_AG // LANES) // (D // LANES) for u in units if u >= OFF_AG // LANES})
    args = (list(dqs) if use_q else []) + (list(dkvs) if use_kv else []) + [gates[g] for g in use_g]
    specs = ([_slab_rows(D // LANES, rows)] * 3 if use_q else []) + ([_slab_rows(NKV, rows)] * 3 if use_kv else []) \
        + [_rows(rows, D)] * len(use_g)

    def pick(refs):
        refs = list(refs)
        q_refs = [refs.pop(0) for _ in range(3)] if use_q else None
        kv_refs = [refs.pop(0) for _ in range(3)] if use_kv else None
        return q_refs, kv_refs, {g: refs.pop(0) for g in use_g}

    return args, specs, pick


def _exchange_results_of(chip_sums):
    n = len(chip_sums) * len(CHIP_FLIPS)
    shapes = [jax.ShapeDtypeStruct((NCHIP,) + tuple(a.shape[1:] if a.ndim == 3 else a.shape), a.dtype)
              for a in chip_sums]
    return shapes, [pltpu.SemaphoreType.DMA((n,)), pltpu.SemaphoreType.DMA((n,))]


def _start_exchange(copies, first_step):
    @pl.when(first_step)
    def _():
        for out, _ in copies:
            out.start()


def _finish_exchange(copies, last_step):
    @pl.when(last_step)
    def _():
        for _, arrival in copies:
            arrival.wait_recv()
        for out, _ in copies:
            out.wait_send()


def _inproj_bwd_x(dqs, dkvs, gates, w_bf, x, g1, dx2, pi):
    tm = 256
    last = S // tm - 1
    units = range(NCOL // LANES)
    pieces, piece_specs, pick = _dproj_sources(units, dqs, dkvs, gates, tm)
    landing, sems = _exchange_results_of([pi])

    def body(*refs):
        piece_refs, refs = refs[:len(pieces)], refs[len(pieces):]
        w_ref, x_ref, g_ref, dx2_ref, pi_ref, gx_ref, st_ref, ri_ref, dp_ref, send, recv = refs
        i = pl.program_id(0)
        copies = _chip_exchange_copies([pi_ref], [ri_ref], send, recv)
        _start_exchange(copies, i == 0)

        @pl.when(i == 0)
        def _():
            st_ref[...] = jnp.zeros_like(st_ref)

        sources = pick(piece_refs)
        for u in units:
            dp_ref[:, u * LANES:(u + 1) * LANES] = _dproj_unit(u, *sources, tm)
        dh = _dot_nt(dp_ref[...], w_ref[...])
        xt = x_ref[...]
        r = lax.rsqrt(jnp.mean(xt * xt, axis=-1, keepdims=True) + NORM_EPS)
        xn = xt * r
        dxn = dh * g_ref[...]
        gx_ref[...] = dx2_ref[...] + r * (dxn - xn * jnp.mean(dxn * xn, axis=-1, keepdims=True))
        st_ref[0:1, :] += jnp.sum(dh * xn, axis=0, keepdims=True)
        _finish_exchange(copies, i == last)

    return pl.pallas_call(
        body, grid=(S // tm,), name="inproj_bwd_x",
        in_specs=piece_specs + [_resident((D, NCOL)), _rows(tm, D), _resident((1, D)), _rows(tm, D), ANY],
        out_specs=[_rows(tm, D), pl.BlockSpec((8, D), lambda i: (0, 0)), ANY],
        out_shape=[jax.ShapeDtypeStruct((S, D), F32), jax.ShapeDtypeStruct((8, D), F32)] + landing,
        scratch_shapes=[pltpu.VMEM((tm, NCOL), BF16)] + sems,
        compiler_params=_params(("arbitrary",)),
    )(*pieces, w_bf, x, g1, dx2, pi)


def _inproj_bwd_w(h, dqs, dkvs, gates):
    out = None
    for k in range(NCHIP):
        units = range(k * UNITS_PER_CHUNK, (k + 1) * UNITS_PER_CHUNK)
        tk = 512 if units[0] < OFF_K // LANES else 1024
        nk = S // tk
        pieces, piece_specs, pick = _dproj_sources(units, dqs, dkvs, gates, tk)
        handed_on = [] if out is None else [out]

        def body(*refs, units=units, pick=pick, n_pieces=len(pieces), n_in=1 + len(pieces) + len(handed_on)):
            h_ref, piece_refs = refs[0], refs[1:1 + n_pieces]
            o_ref, tile, acc = refs[n_in:]
            i = pl.program_id(0)

            @pl.when(i == 0)
            def _():
                acc[...] = jnp.zeros_like(acc)

            sources = pick(piece_refs)
            for n, u in enumerate(units):
                tile[:, n * LANES:(n + 1) * LANES] = _dproj_unit(u, *sources, tk)
            acc[...] += _dot_tn(h_ref[...], tile[...])

            @pl.when(i == nk - 1)
            def _():
                o_ref[0] = acc[...].astype(BF16)

        out = pl.pallas_call(
            body, grid=(nk,), name=f"inproj_bwd_w{k}",
            in_specs=[_rows(tk, D)] + piece_specs + [ANY] * len(handed_on),
            out_specs=pl.BlockSpec((1, D, CHUNK), lambda i, k=k: (k, 0, 0)),
            out_shape=jax.ShapeDtypeStruct((NCHIP, D, CHUNK), BF16),
            input_output_aliases={1 + len(pieces): 0} if handed_on else {},
            scratch_shapes=[pltpu.VMEM((tk, CHUNK), BF16), pltpu.VMEM((D, CHUNK), F32)],
            compiler_params=_params(("arbitrary",)),
        )(h, *pieces, *handed_on)
    return out


ROW_FINAL_G, ROW_LOSS, ROW_LN_G, ROW_LN_B, ROW_CONV_B, ROW_TAPS = 0, 1, 2, 3, 4, 8
SMALL_ROWS = 8 + HALO
NDEV = 8


MESH = pl.DeviceIdType.MESH
ANY = pl.BlockSpec(memory_space=pl.ANY)
CHIP_FLIPS = ((1, 0), (0, 1), (1, 1))


def _pos():
    return lax.axis_index("x"), lax.axis_index("y"), lax.axis_index("c")


def _flip(v, f):
    return 1 - v if f else v


def _ds(start, size, align=None):
    return pl.ds(pl.multiple_of(start, align or size), size)


def _place_shards(wi, wo, cw, where):
    steps = 4

    def body(where_ref, wi_ref, wo_ref, cw_ref, wi_full, wo_full, cw_full):
        wi_full[...] = wi_ref[...].astype(BF16)
        wo_full[...] = wo_ref[...].astype(BF16)
        cw_full[...] = cw_ref[...]

    grid_spec = pltpu.PrefetchScalarGridSpec(
        num_scalar_prefetch=1, grid=(steps,),
        in_specs=[pl.BlockSpec((D // steps, CHUNK), lambda i, w: (i, 0)),
                  pl.BlockSpec((WOUT_SHARD // steps, D), lambda i, w: (i, 0)),
                  pl.BlockSpec((HALO, CONVW_SHARD), lambda i, w: (0, 0))],
        out_specs=[pl.BlockSpec((D // steps, CHUNK), lambda i, w: (i, w[0])),
                   pl.BlockSpec((WOUT_SHARD // steps, D), lambda i, w: (w[0] * steps + i, 0)),
                   pl.BlockSpec((HALO, CONVW_SHARD), lambda i, w: (0, w[0]))])
    return pl.pallas_call(
        body, grid_spec=grid_spec, name="place_shards",
        out_shape=[jax.ShapeDtypeStruct((D, NCOL), BF16), jax.ShapeDtypeStruct((WOUT_ROWS, D), BF16),
                   jax.ShapeDtypeStruct((HALO, D), F32)],
        compiler_params=_params(("arbitrary",)),
    )(where, wi, wo, cw)


W_IN, W_OUT, TAPS = range(3)
GATHER_SEMS = 12


def _gather_stages(fulls, send, recv):
    halves = {W_IN: D // 2, W_OUT: WOUT_SHARD // 2, TAPS: HALO // 2}
    x, y, c = _pos()
    chips = {"me": (x, y), "x": (1 - x, y), "y": (x, 1 - y), "diag": (1 - x, 1 - y)}
    SENT = ((("me", 0), "x"), (("me", 1), "x"), (("me", 1), "y"), (("me", 0), "y"), (("x", 0), "y"), (("y", 1), "x"))
    LANDS = ((("x", 0), "x"), (("x", 1), "x"), (("y", 1), "y"), (("y", 0), "y"), (("diag", 0), "y"), (("diag", 1), "x"))
    N_ICI = len(SENT)

    def region(n_th, whose, half, part):
        a, full = fulls[n_th]
        chip = 2 * chips[whose][0] + chips[whose][1]
        n = halves[a] // 2
        row = half * halves[a] + part * n
        if a == W_IN:
            return full.at[_ds(row, n), _ds(chip * CHUNK, CHUNK, 128)]
        if a == W_OUT:
            return full.at[_ds(chip * WOUT_SHARD + row, n), :]
        return full.at[_ds(row, n), _ds(chip * CONVW_SHARD, CONVW_SHARD, 128)]

    def copy(n_th, kind, piece, dev):
        k = GATHER_SEMS * n_th + kind
        return pltpu.make_async_remote_copy(src_ref=piece, dst_ref=piece, send_sem=send.at[k], recv_sem=recv.at[k],
                                            device_id=dev, device_id_type=MESH)

    def sent(a, k):
        if k < N_ICI:
            (whose, part), to = SENT[k]
            return copy(a, k, region(a, whose, c, part), (*chips[to], c))
        (whose, part), _ = LANDS[k - N_ICI]
        return copy(a, k, region(a, whose, c, part), (x, y, 1 - c))

    def wait_arrival(a, k):
        if k < N_ICI:
            (whose, part), frm = LANDS[k]
            copy(a, k, region(a, whose, c, part), (*chips[frm], c)).wait_recv()
        else:
            (whose, part), _ = LANDS[k - N_ICI]
            copy(a, k, region(a, whose, 1 - c, part), (x, y, 1 - c)).wait_recv()

    arrays = range(len(fulls))

    def own_to_neighbours():
        for a in arrays:
            for k in (0, 2, 1, 3):
                sent(a, k).start()

    def pass_on_neighbours():
        for a in arrays:
            for k, onward in ((0, 4), (2, 5), (1, None), (3, None)):
                wait_arrival(a, k)
                if onward is not None:
                    sent(a, onward).start()
                sent(a, k + N_ICI).start()

    def pass_on_diagonal():
        for a in arrays:
            for k in (4, 5):
                wait_arrival(a, k)
                sent(a, k + N_ICI).start()

    def finish():
        for a in arrays:
            for k in range(N_ICI, 2 * N_ICI):
                wait_arrival(a, k)
            for k in range(2 * N_ICI):
                sent(a, k).wait_send()

    return own_to_neighbours, pass_on_neighbours, pass_on_diagonal, finish


def _gather_sems(n_arrays):
    return [pltpu.SemaphoreType.DMA((GATHER_SEMS * n_arrays,)), pltpu.SemaphoreType.DMA((GATHER_SEMS * n_arrays,))]


def _gather_w_in(wi_full):
    def body(_wi, full, send, recv):
        for stage in _gather_stages([(W_IN, full)], send, recv):
            stage()

    return pl.pallas_call(
        body, name="gather_w_in", in_specs=[ANY], out_specs=ANY, input_output_aliases={0: 0},
        out_shape=jax.ShapeDtypeStruct((D, NCOL), BF16), scratch_shapes=_gather_sems(1),
    )(wi_full)


def _half_shape(a):
    return jax.ShapeDtypeStruct((NCHIP, a.shape[1] // 2, a.shape[2]) if a.ndim == 3 else a.shape, a.dtype)


def _exchange_halves(arrays, name):
    n = len(arrays)

    def body(*refs):
        srcs, dsts, (send, recv) = refs[:n], refs[n:2 * n], refs[2 * n:]
        x, y, c = _pos()
        cps = []
        for k, (s_, d_) in enumerate(zip(srcs, dsts)):
            if len(s_.shape) == 3:
                h = s_.shape[1] // 2
                s_ = s_.at[:, _ds((1 - c) * h, h), :]
            cps.append(pltpu.make_async_remote_copy(src_ref=s_, dst_ref=d_, send_sem=send.at[k], recv_sem=recv.at[k],
                                                    device_id=(x, y, 1 - c), device_id_type=MESH))
        for cp in cps:
            cp.start()
        for cp in cps:
            cp.wait()

    return pl.pallas_call(
        body, name=name, in_specs=[ANY] * n, out_specs=[ANY] * n, out_shape=[_half_shape(a) for a in arrays],
        scratch_shapes=[pltpu.SemaphoreType.DMA((n,)), pltpu.SemaphoreType.DMA((n,))],
    )(*arrays)


def _add_halves(arrays, received, name):
    n = len(arrays)

    def body(*refs):
        mine, theirs, outs = refs[:n], refs[n:2 * n], refs[2 * n:]
        c = lax.axis_index("c")
        for m_, t_, o_ in zip(mine, theirs, outs):
            if len(m_.shape) == 3:
                h = m_.shape[1] // 2
                o_[0] = (m_[0, _ds(c * h, h), :].astype(F32) + t_[0].astype(F32)).astype(o_.dtype)
            else:
                o_[...] = m_[...] + t_[...]

    def spec(shape):
        if len(shape) == 3:
            return pl.BlockSpec((1,) + tuple(shape[1:]), lambda k: (k, 0, 0))
        return pl.BlockSpec(tuple(shape), lambda k: (0, 0))

    halves = [_half_shape(a) for a in arrays]
    return pl.pallas_call(
        body, grid=(NCHIP,), name=name,
        in_specs=[spec(a.shape) for a in arrays] + [spec(h.shape) for h in halves],
        out_specs=[spec(h.shape) for h in halves], out_shape=halves,
        compiler_params=_params(("arbitrary",)),
    )(*arrays, *received)


def _chip_exchange_copies(srcs, dsts, send, recv):
    x, y, c = _pos()
    me = 2 * x + y
    pairs = []
    for a in range(len(srcs)):
        for j, (fx, fy) in enumerate(CHIP_FLIPS):
            px, py = _flip(x, fx), _flip(y, fy)
            peer = 2 * px + py
            k = len(CHIP_FLIPS) * a + j
            out = pltpu.make_async_remote_copy(
                src_ref=srcs[a].at[peer] if len(srcs[a].shape) == 3 else srcs[a], dst_ref=dsts[a].at[me],
                send_sem=send.at[k], recv_sem=recv.at[k], device_id=(px, py, c), device_id_type=MESH)
            got = dsts[a].at[peer]
            arrival = pltpu.make_async_remote_copy(
                src_ref=got, dst_ref=got, send_sem=send.at[k], recv_sem=recv.at[k],
                device_id=(px, py, c), device_id_type=MESH)
            pairs.append((out, arrival))
    return pairs


def _sum_chips(ri, ro, rs, pi, po, ps, where):
    def body(w_ref, ri_ref, ro_ref, rs_ref, pi_ref, po_ref, ps_ref, gi_ref, go_ref, gs_ref, g5_ref, loss_ref,
             acc_i, acc_o, acc_s):
        k = pl.program_id(0)
        accs = (acc_i, acc_o, acc_s)

        @pl.when(k == 0)
        def _():
            for acc in accs:
                acc[...] = jnp.zeros_like(acc)

        @pl.when(k == w_ref[0])
        def _():
            for acc, val in zip(accs, (pi_ref[0], po_ref[0], ps_ref[...])):
                acc[...] += val.astype(F32)

        @pl.when(k != w_ref[0])
        def _():
            for acc, ref in zip(accs, (ri_ref, ro_ref, rs_ref)):
                acc[...] += ref[0].astype(F32)

        @pl.when(k == NCHIP - 1)
        def _():
            gi_ref[0] = acc_i[...]
            go_ref[0] = acc_o[...]
            gs_ref[...] = acc_s[...]
            g5_ref[...] = jnp.zeros_like(g5_ref)
            for i, row in enumerate((ROW_CONV_B, ROW_LN_G, ROW_LN_B, ROW_FINAL_G)):
                g5_ref[i + 1:i + 2, :] = acc_s[row:row + 1, :]
            loss = jnp.sum(acc_s[ROW_LOSS:ROW_LOSS + 1, :], axis=1, keepdims=True)
            loss_ref[...] = jnp.broadcast_to(loss, loss_ref.shape)

    def sent(k, w):
        return jnp.where(k == w[0], (k + 1) % NCHIP, k)

    hi, ho = D // 2, WOUT_SHARD // 2
    const = lambda shape: pl.BlockSpec(shape, lambda k, w: (0,) * len(shape))
    grid_spec = pltpu.PrefetchScalarGridSpec(
        num_scalar_prefetch=1, grid=(NCHIP,),
        in_specs=[pl.BlockSpec((1, hi, CHUNK), lambda k, w: (sent(k, w), 0, 0)),
                  pl.BlockSpec((1, ho, D), lambda k, w: (sent(k, w), 0, 0)),
                  pl.BlockSpec((1, SMALL_ROWS, D), lambda k, w: (sent(k, w), 0, 0)),
                  pl.BlockSpec((1, hi, CHUNK), lambda k, w: (w[0], 0, 0)),
                  pl.BlockSpec((1, ho, D), lambda k, w: (w[0], 0, 0)),
                  const((SMALL_ROWS, D))],
        out_specs=[pl.BlockSpec((1, hi, CHUNK), lambda k, w: (w[1], 0, 0)),
                   pl.BlockSpec((1, ho, D), lambda k, w: (w[1], 0, 0)),
                   const((SMALL_ROWS, D)), const((8, D)), const((8, LANES))],
        scratch_shapes=[pltpu.VMEM((hi, CHUNK), F32), pltpu.VMEM((ho, D), F32), pltpu.VMEM((SMALL_ROWS, D), F32)])
    return pl.pallas_call(
        body, grid_spec=grid_spec, name="sum_chips",
        out_shape=[jax.ShapeDtypeStruct((2, hi, CHUNK), F32), jax.ShapeDtypeStruct((2, ho, D), F32),
                   jax.ShapeDtypeStruct((SMALL_ROWS, D), F32), jax.ShapeDtypeStruct((8, D), F32),
                   jax.ShapeDtypeStruct((8, LANES), F32)],
        compiler_params=_params(("arbitrary",)),
    )(where, ri, ro, rs, pi, po, ps)


def _exchange_results(gi2, go2, st):
    flips = [(fx, fy, fc) for fx in (0, 1) for fy in (0, 1) for fc in (0, 1)][1:]

    def body(_gi, _go, st_ref, gi_ref, go_ref, all_ref, send, recv, lsem):
        x, y, c = _pos()
        sib = (x, y, 1 - c)

        def half(k, ref, slot):
            return pltpu.make_async_remote_copy(src_ref=ref.at[slot], dst_ref=ref.at[slot], send_sem=send.at[k],
                                                recv_sem=recv.at[k], device_id=sib, device_id_type=MESH)

        def stat(k, src, slot, dev):
            return pltpu.make_async_remote_copy(src_ref=src, dst_ref=all_ref.at[slot], send_sem=send.at[k],
                                                recv_sem=recv.at[k], device_id=dev, device_id_type=MESH)

        mine = pltpu.make_async_copy(st_ref, all_ref.at[4 * x + 2 * y + c], lsem)
        mine.start()
        sends = [half(k, ref, c) for k, ref in enumerate((gi_ref, go_ref))]
        peers = [(_flip(x, fx), _flip(y, fy), _flip(c, fc)) for fx, fy, fc in flips]
        sends += [stat(2 + k, st_ref, 4 * x + 2 * y + c, dev) for k, dev in enumerate(peers)]
        for cp in sends:
            cp.start()
        for k, ref in enumerate((gi_ref, go_ref)):
            half(k, ref, 1 - c).wait_recv()
        for k, (px, py, pc) in enumerate(peers):
            slot = 4 * px + 2 * py + pc
            stat(2 + k, all_ref.at[slot], slot, (px, py, pc)).wait_recv()
        for cp in sends:
            cp.wait_send()
        mine.wait()

    n = 2 + len(flips)
    return pl.pallas_call(
        body, name="exchange_results",
        in_specs=[ANY, ANY, ANY], out_specs=[ANY, ANY, ANY], input_output_aliases={0: 0, 1: 1},
        out_shape=[jax.ShapeDtypeStruct((2, D // 2, CHUNK), F32), jax.ShapeDtypeStruct((2, WOUT_SHARD // 2, D), F32),
                   jax.ShapeDtypeStruct((NDEV, 8, D), F32)],
        scratch_shapes=[pltpu.SemaphoreType.DMA((n,)), pltpu.SemaphoreType.DMA((n,)), pltpu.SemaphoreType.DMA],
    )(gi2, go2, st)


def _adamw_math(w, g, m, v):
    m2 = ADAM_B1 * m + (1.0 - ADAM_B1) * g
    v2 = ADAM_B2 * v + (1.0 - ADAM_B2) * (g * g)
    m_hat = m2 / (1.0 - ADAM_B1 ** ADAM_STEP)
    v_hat = v2 / (1.0 - ADAM_B2 ** ADAM_STEP)
    delta = -ADAM_LR * (m_hat / (jnp.sqrt(v_hat) + ADAM_EPS) + ADAM_WD * w)
    return delta, m2, v2


def _adamw(w, g, m, v, name):
    rows, cols = w.shape
    tm = 256 if rows % 256 == 0 else rows

    def body(w_ref, g_ref, m_ref, v_ref, d_ref, m2_ref, v2_ref):
        d_ref[...], m2_ref[...], v2_ref[...] = _adamw_math(w_ref[...], g_ref[...], m_ref[...], v_ref[...])

    shape = jax.ShapeDtypeStruct(w.shape, F32)
    return pl.pallas_call(
        body, grid=(rows // tm,), name=name,
        in_specs=[_rows(tm, cols)] * 4, out_specs=[_rows(tm, cols)] * 3, out_shape=[shape] * 3,
        compiler_params=_params(("arbitrary",)),
    )(w, g, m, v)


def _adamw_vectors(g5, first_parts, ws, ms, vs):
    n = len(ws)

    def body(g_ref, parts_ref, *refs):
        ins, g0_ref, outs = refs[:3 * n], refs[3 * n], refs[3 * n + 1:]
        g0 = parts_ref[0, 0:1, :]
        for dev in range(1, NDEV):
            g0 = g0 + parts_ref[dev, 0:1, :]
        g0_ref[...] = g0
        for i in range(n):
            g = g0 if i == 0 else g_ref[i:i + 1, :]
            res = _adamw_math(ins[i][...], g, ins[n + i][...], ins[2 * n + i][...])
            for kind in range(3):
                outs[kind * n + i][...] = res[kind]

    shape = jax.ShapeDtypeStruct((1, D), F32)
    return pl.pallas_call(body, name="adamw_vectors", out_shape=[shape] * (1 + 3 * n), compiler_params=_params())(
        g5, first_parts, *ws, *ms, *vs)


def kernel(x, norm_g, w_in, conv_w, conv_b, conv_ln_g, conv_ln_b, w_out, final_norm_g, loss_target, m_norm_g, m_w_in, m_conv_w, m_conv_b, m_conv_ln_g, m_conv_ln_b, m_w_out, m_final_norm_g, v_norm_g, v_w_in, v_conv_w, v_conv_b, v_conv_ln_g, v_conv_ln_b, v_w_out, v_final_norm_g):
    chip = 2 * lax.axis_index("x") + lax.axis_index("y")
    where = jnp.stack([chip, lax.axis_index("c")]).astype(jnp.int32)
    taps_shard = jnp.pad(conv_w[0], ((0, HALO - CONV_K), (0, 0)))
    wi_full, wo_full, cw_full = _place_shards(w_in[0], w_out[0], taps_shard, where)
    wi_full = _gather_w_in(wi_full)

    gf = final_norm_g[None]
    xb = x[0]
    h, q, k, v, a_gate, c_val, c_glu, c_gate, wo_full, cw_full = _inproj_fwd(xb, norm_g, wi_full, wo_full, cw_full)
    tables = [_bias_table(d) for d in PATTERNS]
    o, lse, y_att = _attn_fwd(q, k, v, tables, a_gate)
    u, y_conv = _conv_fwd(c_val, c_glu, c_gate, cw_full, conv_b, conv_ln_g, conv_ln_b)
    dx2, dy_att, du, dc_gate, dw_out, st_out = _outproj_loss(
        y_att, y_conv, wo_full, xb, loss_target[0], gf, u, c_gate, conv_ln_g, conv_ln_b)
    dc_val, dc_glu, dconv_w = _conv_bwd_taps(du, c_val, c_glu, cw_full)

    early = [dw_out.reshape(NCHIP, WOUT_SHARD, D), jnp.concatenate([st_out, dconv_w], axis=0)]
    po, ps = _add_halves(early, _exchange_halves(early, "exchange_halves_early"), "add_halves_early")
    do, da_gate, delta, ro, rs = _attn_gate_bwd(dy_att, o, a_gate, _head_sum_selectors(), [po, ps])
    dqs, dkvs = zip(*[_attn_bwd(q, k, v, do, lse, delta, t, d) for t, d in zip(tables, PATTERNS)])

    dproj_pieces = (dqs, dkvs, (da_gate, dc_val, dc_glu, dc_gate))
    late = [_inproj_bwd_w(h, *dproj_pieces)]
    (pi,) = _add_halves(late, _exchange_halves(late, "exchange_halves"), "add_halves")
    grad_x, st_in, ri = _inproj_bwd_x(*dproj_pieces, wi_full, xb, norm_g, dx2, pi)
    gi2, go2, g_small, g5, loss8 = _sum_chips(ri, ro, rs, pi, po, ps, where)
    gi2, go2, norm_g_parts = _exchange_results(gi2, go2, st_in)
    g_w_in = gi2.reshape(D, CHUNK)
    g_w_out = go2.reshape(WOUT_SHARD, D)
    g_taps = lax.dynamic_slice(g_small, (ROW_TAPS, chip * CONVW_SHARD), (CONV_K, CONVW_SHARD))

    d_w_in, m2_w_in, v2_w_in = _adamw(w_in[0], g_w_in, m_w_in[0], v_w_in[0], "adamw_w_in")
    d_w_out, m2_w_out, v2_w_out = _adamw(w_out[0], g_w_out, m_w_out[0], v_w_out[0], "adamw_w_out")
    d_taps, m2_taps, v2_taps = _adamw(conv_w[0], g_taps, m_conv_w[0], v_conv_w[0], "adamw_conv_w")
    g_norm, *vec = _adamw_vectors(
        g5, norm_g_parts,
        (norm_g, conv_b, conv_ln_g, conv_ln_b, gf),
        (m_norm_g, m_conv_b, m_conv_ln_g, m_conv_ln_b, m_final_norm_g[None]),
        (v_norm_g, v_conv_b, v_conv_ln_g, v_conv_ln_b, v_final_norm_g[None]))
    d_vec, m2_vec, v2_vec = vec[0:5], vec[5:10], vec[10:15]

    def weight_order(ng, wi, cw, cb, lg, lb, wo, fg):
        return (ng, wi[None], cw[None], cb, lg, lb, wo[None], fg[0])

    grads = weight_order(g_norm, g_w_in, g_taps, g5[1:2], g5[2:3], g5[3:4], g_w_out, g5[4:5])
    deltas = weight_order(d_vec[0], d_w_in, d_taps, d_vec[1], d_vec[2], d_vec[3], d_w_out, d_vec[4])
    new_m = weight_order(m2_vec[0], m2_w_in, m2_taps, m2_vec[1], m2_vec[2], m2_vec[3], m2_w_out, m2_vec[4])
    new_v = weight_order(v2_vec[0], v2_w_in, v2_taps, v2_vec[1], v2_vec[2], v2_vec[3], v2_w_out, v2_vec[4])
    return (loss8[0, 0], grad_x[None], *grads, *deltas, *new_m, *new_v)
```

```python
import jax
import jax.numpy as jnp
from jax import lax
from jax.experimental import pallas as pl
from jax.experimental.pallas import tpu as pltpu

F32 = jnp.float32
BF16 = jnp.bfloat16

S = 4096
D = 1024
LANES = 128
HD = 64
NKV = 4
GQ = 4
KVW = NKV * HD
NCOL = 5632
CONV_K = 31
HALO = 32
BLK = 128
PATTERNS = (1, 4, 16)
NORM_EPS = 1e-6
LN_EPS = 1e-5
NEG = -1e30
OFF_Q, OFF_K, OFF_AG, OFF_CV, OFF_CG, OFF_CGATE = 0, 1024, 1536, 2560, 3584, 4608
NCHIP = 4
CHUNK = NCOL // NCHIP
WOUT_ROWS = 2 * D
WOUT_SHARD = WOUT_ROWS // NCHIP
CONVW_SHARD = D // NCHIP

ADAM_LR, ADAM_B1, ADAM_B2, ADAM_EPS, ADAM_WD, ADAM_STEP = 0.001, 0.9, 0.999, 1e-08, 0.01, 10

VMEM_LIMIT = 56 * 1024 * 1024


def _params(sem=None, vmem=VMEM_LIMIT):
    return pltpu.CompilerParams(dimension_semantics=sem, vmem_limit_bytes=vmem)


def _sigmoid(a):
    return 0.5 * jnp.tanh(0.5 * a) + 0.5


def _rows(tm, width):
    return pl.BlockSpec((tm, width), lambda i: (i, 0))


def _slabs(n):
    return jax.ShapeDtypeStruct((n, S, LANES), F32)


def _slab_rows(n, tm):
    return pl.BlockSpec((n, tm, LANES), lambda i: (0, i, 0))


def _resident(shape):
    return pl.BlockSpec(shape, lambda *_: (0,) * len(shape), pipeline_mode=pl.Buffered(1))


def _dot(a, b):
    return jnp.dot(a, b, preferred_element_type=F32)


def _dot_nt(a, b):
    return lax.dot_general(a, b, (((1,), (1,)), ((), ())), preferred_element_type=F32)


def _dot_tn(a, b):
    return lax.dot_general(a, b, (((0,), (0,)), ((), ())), preferred_element_type=F32)


def _inproj_fwd(x, g1, w_bf, wo_full, cw_full):
    tm = 512
    steps = S // tm

    def body(x_ref, g_ref, w_ref, _wo, _cw, h_ref, q_ref, k_ref, v_ref, ag_ref, cv_ref, cg_ref, cgate_ref,
             wo_ref, cw_ref, send, recv):
        i = pl.program_id(0)
        stages = _gather_stages([(W_OUT, wo_ref), (TAPS, cw_ref)], send, recv)
        for stage, step in zip(stages[:3], (0, steps // 2 - 1, steps - 2)):
            pl.when(i == step)(stage)
        xt = x_ref[...]
        r = lax.rsqrt(jnp.mean(xt * xt, axis=-1, keepdims=True) + NORM_EPS)
        h = (xt * r * g_ref[...]).astype(BF16)
        h_ref[...] = h
        q = _dot(h, w_ref[:, OFF_Q:OFF_Q + D]) * (HD ** -0.5)
        kv = _dot(h, w_ref[:, OFF_K:OFF_K + 2 * KVW])
        for sl in range(D // LANES):
            q_ref[sl] = q[:, sl * LANES:(sl + 1) * LANES]
        for sl in range(KVW // LANES):
            k_ref[sl] = kv[:, sl * LANES:(sl + 1) * LANES]
            v_ref[sl] = kv[:, KVW + sl * LANES:KVW + (sl + 1) * LANES]
        ag_ref[...] = _dot(h, w_ref[:, OFF_AG:OFF_AG + D])
        cv_ref[...] = _dot(h, w_ref[:, OFF_CV:OFF_CV + D])
        cg_ref[...] = _dot(h, w_ref[:, OFF_CG:OFF_CG + D])
        cgate_ref[...] = _dot(h, w_ref[:, OFF_CGATE:OFF_CGATE + D])
        pl.when(i == steps - 1)(stages[3])

    big = jax.ShapeDtypeStruct((S, D), F32)
    return pl.pallas_call(
        body, grid=(steps,), name="inproj_fwd",
        in_specs=[_rows(tm, D), _resident((1, D)), _resident((D, NCOL)), ANY, ANY],
        out_specs=[_rows(tm, D), _slab_rows(D // LANES, tm), _slab_rows(KVW // LANES, tm), _slab_rows(KVW // LANES, tm),
                   _rows(tm, D), _rows(tm, D), _rows(tm, D), _rows(tm, D), ANY, ANY],
        out_shape=[jax.ShapeDtypeStruct((S, D), BF16), _slabs(D // LANES), _slabs(KVW // LANES), _slabs(KVW // LANES),
                   big, big, big, big,
                   jax.ShapeDtypeStruct((WOUT_ROWS, D), BF16), jax.ShapeDtypeStruct((HALO, D), F32)],
        input_output_aliases={3: 8, 4: 9},
        scratch_shapes=_gather_sems(2),
        compiler_params=_params(("arbitrary",)),
    )(x, g1, w_bf, wo_full, cw_full)


def _bias_table(d):
    h = jnp.arange(NKV * GQ, dtype=F32)
    slopes = jnp.exp2(-8.0 * (h + 1.0) / (NKV * GQ))
    qi = jnp.arange(BLK)[:, None]
    kj = jnp.arange(2 * BLK)[None, :]
    dist = BLK + qi - kj
    window = (dist >= 0) & (dist <= BLK)
    bias = -slopes[:, None, None] * (dist * d).astype(F32)[None]
    has_prev = jnp.stack([jnp.broadcast_to(kj >= BLK, (BLK, 2 * BLK)), jnp.ones((BLK, 2 * BLK), bool)])
    valid = window[None] & has_prev
    tab = jnp.where(valid[:, None], bias[None], NEG)
    return tab.reshape(2, NKV, GQ * BLK, 2 * BLK)


def _sub_rows(start, d):
    if d == 1:
        return pl.ds(pl.multiple_of(start, BLK), BLK)
    return pl.ds(start, BLK, stride=d)


CHUNK_ROWS = 2048
BLOCKS_PER_CHUNK = CHUNK_ROWS // BLK


def _low_lanes(rows=BLK):
    return lax.broadcasted_iota(jnp.int32, (rows, LANES), 1) < HD


def _block_start(idx, d):
    shift = d.bit_length() - 1
    b, r = lax.shift_right_logical(idx, shift), lax.bitwise_and(idx, d - 1)
    start = b * (BLK * d) + r
    return b, start, jnp.maximum(start - BLK * d, r)


def _stack_heads(ref, rows):
    low = _low_lanes()
    t0, t1 = ref[0, rows, :], ref[1, rows, :]
    return jnp.concatenate([jnp.where(low, t0, 0.0), jnp.where(low, 0.0, t0),
                            jnp.where(low, t1, 0.0), jnp.where(low, 0.0, t1)], axis=0).astype(BF16)


def _unstack_heads(dup):
    low = _low_lanes()
    return (jnp.where(low, dup[0:BLK], dup[BLK:2 * BLK]), jnp.where(low, dup[2 * BLK:3 * BLK], dup[3 * BLK:4 * BLK]))


def _kv_dup(ref, prow, rows, odd):
    t = jnp.concatenate([ref[0, prow, :], ref[0, rows, :]], axis=0)
    swapped = pltpu.roll(t, HD, axis=1)
    keep = jnp.logical_xor(_low_lanes(2 * BLK), odd)
    return jnp.where(keep, t, swapped).astype(BF16)


PIECES = 3


def _by_head(tiles):
    lane = lax.broadcasted_iota(jnp.int32, tiles[0].shape, 1)
    out = tiles[0]
    for g in range(1, GQ):
        out = jnp.where(lax.bitwise_and(lane, GQ - 1) == g, tiles[g], out)
    return out


def _minus_in_pieces(x):
    lane = lax.broadcasted_iota(jnp.int32, x.shape, 1)
    hi = (-x).astype(BF16).astype(F32)
    rest = -x - hi
    mid = rest.astype(BF16).astype(F32)
    lo = (rest - mid).astype(BF16).astype(F32)
    return jnp.where(lane < GQ, hi, jnp.where(lane < 2 * GQ, mid, jnp.where(lane < PIECES * GQ, lo, 0.0)))


def _attn_fwd(q, k, v, tables, a_gate):
    tm = 256
    width = GQ * HD

    lane_out = jnp.arange(LANES)[None, :] // HD
    spread_sel = jnp.stack([jnp.arange(LANES)[:, None] == 2 * half + lane_out for half in range(2)]).astype(BF16)

    def body(q_ref, k_ref, v_ref, b1_ref, b2_ref, b3_ref, ag_ref, sel_ref, o_ref, lse_ref, y_ref, op, lp):
        odd = pl.program_id(0) % 2 == 1
        chunk = pl.program_id(1)
        ones = jnp.ones((2 * BLK, LANES), BF16)

        for pat, (d, b_ref) in enumerate(zip(PATTERNS, (b1_ref, b2_ref, b3_ref))):
            def block(idx, carry, pat=pat, d=d, b_ref=b_ref):
                b, start, pstart = _block_start(chunk * BLOCKS_PER_CHUNK + idx, d)
                rows, prow = _sub_rows(start, d), _sub_rows(pstart, d)
                mine = _sub_rows(start - chunk * CHUNK_ROWS, d)
                qs = _stack_heads(q_ref, mine)
                kw = _kv_dup(k_ref, prow, rows, odd)
                vw = _kv_dup(v_ref, prow, rows, odd)
                s = _dot_nt(qs, kw) + b_ref[jnp.minimum(b, 1), 0]
                m = jnp.max(s, axis=1, keepdims=True)
                p = jnp.exp(s - m).astype(BF16)
                ol = _dot(p, jnp.concatenate([vw, ones], axis=1))
                l = ol[:, LANES:]
                op[pat, 0, mine, :], op[pat, 1, mine, :] = _unstack_heads(ol[:, :LANES] / l)
                lp[pat, mine, :] = _by_head([(m + jnp.log(l))[g * BLK:(g + 1) * BLK] for g in range(GQ)])
                return carry

            lax.fori_loop(0, BLOCKS_PER_CHUNK, block, 0, unroll=2)

        def mix(t, carry):
            r = pl.ds(pl.multiple_of(t * tm, tm), tm)
            a, b, c = lp[0, r, :], lp[1, r, :], lp[2, r, :]
            m = jnp.maximum(jnp.maximum(a, b), c)
            ea, eb, ec = jnp.exp(a - m), jnp.exp(b - m), jnp.exp(c - m)
            den = ea + eb + ec
            lse_ref[0, r, :] = _minus_in_pieces(m + jnp.log(den))
            inv = 1.0 / den
            for half in range(2):
                def spread(w):
                    hi = w.astype(BF16)
                    lo = (w - hi.astype(F32)).astype(BF16)
                    return _dot(hi, sel_ref[half]) + _dot(lo, sel_ref[half])

                o = (spread(ea * inv) * op[0, half, r, :] + spread(eb * inv) * op[1, half, r, :]
                     + spread(ec * inv) * op[2, half, r, :])
                o_ref[half, r, :] = o
                cols = slice(half * LANES, (half + 1) * LANES)
                ag = ag_ref[r, cols]
                y_ref[r, cols] = (o * (ag * _sigmoid(ag))).astype(BF16)
            return carry

        lax.fori_loop(0, CHUNK_ROWS // tm, mix, 0, unroll=2)

    q_like = pl.BlockSpec((2, CHUNK_ROWS, LANES), lambda j, c: (j, c, 0))
    per_kv = pl.BlockSpec((1, CHUNK_ROWS, LANES), lambda j, c: (j, c, 0))
    kv = pl.BlockSpec((1, S, LANES), lambda j, c: (j // 2, 0, 0))
    bias_spec = pl.BlockSpec((2, 1, GQ * BLK, 2 * BLK), lambda j, c: (0, j, 0, 0))
    group_cols = pl.BlockSpec((CHUNK_ROWS, width), lambda j, c: (c, j))
    return pl.pallas_call(
        body, grid=(NKV, S // CHUNK_ROWS), name="attn_fwd",
        in_specs=[q_like, kv, kv, bias_spec, bias_spec, bias_spec, group_cols,
                  pl.BlockSpec((2, LANES, LANES), lambda j, c: (0, 0, 0))],
        out_specs=[q_like, per_kv, group_cols],
        out_shape=[_slabs(D // LANES), _slabs(NKV), jax.ShapeDtypeStruct((S, D), BF16)],
        scratch_shapes=[pltpu.VMEM((len(PATTERNS), 2, CHUNK_ROWS, LANES), F32),
                        pltpu.VMEM((len(PATTERNS), CHUNK_ROWS, LANES), F32)],
        compiler_params=_params(("arbitrary", "arbitrary")),
    )(q, k, v, *tables, a_gate, spread_sel)


def _head_sum_selectors():
    lane_in = jnp.arange(LANES)[:, None] // HD
    return jnp.stack([jnp.broadcast_to(lane_in == h, (LANES, LANES)) for h in range(2)]).astype(BF16)


def _attn_gate_bwd(dy_att, o, a_gate, selectors, chip_sums):
    tm = 256
    last = S // tm - 1
    landing, sems = _exchange_results_of(chip_sums)
    n_sums = len(chip_sums)

    def body(dy_ref, o_ref, ag_ref, e_ref, *refs):
        sums, (do_ref, dag_ref, delta_ref), refs = refs[:n_sums], refs[n_sums:n_sums + 3], refs[n_sums + 3:]
        landed, (send, recv) = refs[:n_sums], refs[n_sums:]
        i = pl.program_id(0)
        copies = _chip_exchange_copies(sums, landed, send, recv)
        _start_exchange(copies, i == 0)
        for j in range(NKV):
            deltas = []
            for sl in (2 * j, 2 * j + 1):
                cols = slice(sl * LANES, (sl + 1) * LANES)
                dy, ag, o_ = dy_ref[:, cols], ag_ref[:, cols], o_ref[sl]
                sg = _sigmoid(ag)
                do = dy * (ag * sg)
                do_ref[sl] = do
                dag_ref[:, cols] = (dy * o_ * (sg * (1.0 + ag * (1.0 - sg)))).astype(BF16)
                prod = do * o_
                hi = prod.astype(BF16)
                lo = (prod - hi.astype(F32)).astype(BF16)
                deltas += [_dot(hi, e_ref[h]) + _dot(lo, e_ref[h]) for h in range(2)]
            delta_ref[j] = _minus_in_pieces(_by_head(deltas))
        _finish_exchange(copies, i == last)

    return pl.pallas_call(
        body, grid=(S // tm,), name="attn_gate_bwd",
        in_specs=[_rows(tm, D), _slab_rows(D // LANES, tm), _rows(tm, D), _resident((2, LANES, LANES))] + [ANY] * n_sums,
        out_specs=[_slab_rows(D // LANES, tm), _rows(tm, D), _slab_rows(NKV, tm)] + [ANY] * n_sums,
        out_shape=[_slabs(D // LANES), jax.ShapeDtypeStruct((S, D), BF16), _slabs(NKV)] + landing,
        scratch_shapes=sems,
        compiler_params=_params(("arbitrary",)),
    )(dy_att, o, a_gate, selectors, *chip_sums)


def _own_pieces(tile):
    lane = lax.broadcasted_iota(jnp.int32, tile.shape, 1)
    head = jnp.where(lane < PIECES * GQ, lax.bitwise_and(lane, GQ - 1), -1)
    return jnp.concatenate([jnp.where(head == g, tile, 0.0) for g in range(GQ)], axis=0).astype(BF16)


def _attn_bwd(q, k, v, do, lse, delta, bias, d):
    def body(q_ref, do_ref, l_ref, dl_ref, k_ref, v_ref, b_ref, dq_ref, dkv_ref, acc):
        odd = pl.program_id(0) % 2 == 1
        chunk = pl.program_id(1)
        ones = (lax.broadcasted_iota(jnp.int32, (2 * BLK, LANES), 1) < PIECES * GQ).astype(BF16)

        def in_acc(block_idx):
            return pl.ds(pl.multiple_of(block_idx * BLK, BLK), BLK)

        @pl.when(chunk == 0)
        def _():
            acc[...] = jnp.zeros_like(acc)

        def block(idx, carry):
            idx = chunk * BLOCKS_PER_CHUNK + idx
            b, start, pstart = _block_start(idx, d)
            rows, prow = _sub_rows(start, d), _sub_rows(pstart, d)
            mine = _sub_rows(start - chunk * CHUNK_ROWS, d)
            qs = _stack_heads(q_ref, mine)
            dos = _stack_heads(do_ref, mine)
            kw = _kv_dup(k_ref, prow, rows, odd)
            vw = _kv_dup(v_ref, prow, rows, odd)
            s = _dot_nt(jnp.concatenate([qs, _own_pieces(l_ref[0, mine, :])], axis=1),
                        jnp.concatenate([kw, ones], axis=1)) + b_ref[jnp.minimum(b, 1), 0]
            p = jnp.exp(s)
            dv2 = _dot_tn(p.astype(BF16), dos)
            dp = _dot_nt(jnp.concatenate([dos, _own_pieces(dl_ref[0, mine, :])], axis=1),
                         jnp.concatenate([vw, ones], axis=1))
            ds = (p * dp).astype(BF16)
            dq_ref[0, mine, :], dq_ref[1, mine, :] = _unstack_heads(_dot(ds, kw))
            dk2 = _dot_tn(ds, qs)
            dkv = jnp.where(_low_lanes(2 * BLK), dk2 + pltpu.roll(dk2, HD, axis=1), dv2 + pltpu.roll(dv2, HD, axis=1))
            acc[in_acc(idx), :] = acc[in_acc(idx), :] + dkv[BLK:]
            before = jnp.where(b >= 1, idx - d, idx)
            acc[in_acc(before), :] = acc[in_acc(before), :] + dkv[:BLK]
            return carry

        lax.fori_loop(0, BLOCKS_PER_CHUNK, block, 0, unroll=16)

        @pl.when(chunk == S // CHUNK_ROWS - 1)
        def _():
            def place(idx, carry):
                _, start, _ = _block_start(idx, d)
                dkv_ref[0, _sub_rows(start, d), :] = acc[in_acc(idx), :]
                return carry

            lax.fori_loop(0, S // BLK, place, 0, unroll=4)

    q_like = pl.BlockSpec((2, CHUNK_ROWS, LANES), lambda j, c: (j, c, 0))
    pieces = pl.BlockSpec((1, CHUNK_ROWS, LANES), lambda j, c: (j, c, 0))
    kv = pl.BlockSpec((1, S, LANES), lambda j, c: (j // 2, 0, 0))
    per_kv = pl.BlockSpec((1, S, LANES), lambda j, c: (j, 0, 0))
    bias_spec = pl.BlockSpec((2, 1, GQ * BLK, 2 * BLK), lambda j, c: (0, j, 0, 0))
    return pl.pallas_call(
        body, grid=(NKV, S // CHUNK_ROWS), name=f"attn_bwd_d{d}",
        in_specs=[q_like, q_like, pieces, pieces, kv, kv, bias_spec],
        out_specs=[q_like, per_kv],
        out_shape=[_slabs(D // LANES), _slabs(NKV)],
        scratch_shapes=[pltpu.VMEM((S, LANES), F32)],
        compiler_params=_params(("arbitrary", "arbitrary")),
    )(q, do, lse, delta, k, v, bias)


CONV_T = 256


def _halo_before(i):
    return (jnp.maximum(i * (CONV_T // HALO) - 1, 0), 0)


def _halo_after(i):
    return (jnp.minimum((i + 1) * (CONV_T // HALO), S // HALO - 1), 0)


SUBLANES = 8
NCH = D // LANES
GROUP = SUBLANES * SUBLANES


def _comb(ref, cb, base):
    return ref[cb, pl.ds(base, SUBLANES, stride=SUBLANES), :]


def _taps(w_ref, cols):
    return [jnp.broadcast_to(w_ref[j:j + 1, cols], (SUBLANES, LANES)) for j in range(CONV_K)]


def _conv_fwd(c_val, c_glu, c_gate, conv_w, conv_b, ln_g, ln_b):
    T = CONV_T

    def body(cv_ref, cg_ref, cvh_ref, cgh_ref, gate_ref, w_ref, b_ref, lg_ref, lb_ref, u_ref, y_ref, win, us):
        i = pl.program_id(0)
        for cb in range(NCH):
            cols = slice(cb * LANES, (cb + 1) * LANES)
            win[cb, HALO:HALO + T, :] = cv_ref[:, cols] * _sigmoid(cg_ref[:, cols])
            win[cb, 0:HALO, :] = jnp.where(i > 0, cvh_ref[:, cols] * _sigmoid(cgh_ref[:, cols]), 0.0)
        for cb in range(NCH):
            cols = slice(cb * LANES, (cb + 1) * LANES)
            taps = _taps(w_ref, cols)
            bias = jnp.broadcast_to(b_ref[:, cols], (SUBLANES, LANES))

            def group(g, carry):
                for b in range(SUBLANES):
                    base = g * GROUP + b
                    acc = bias
                    for j in range(CONV_K):
                        acc = acc + taps[j] * _comb(win, cb, base + (HALO - (CONV_K - 1) + j))
                    us[cb, pl.ds(base, SUBLANES, stride=SUBLANES), :] = acc
                return carry

            lax.fori_loop(0, T // GROUP, group, 0, unroll=2)
        total = us[0]
        for cb in range(1, NCH):
            total = total + us[cb]
        mu = jnp.sum(total, axis=-1, keepdims=True) * (1.0 / D)
        sq = jnp.zeros((T, LANES), F32)
        for cb in range(NCH):
            uc = us[cb] - mu
            sq = sq + uc * uc
        rstd = lax.rsqrt(jnp.sum(sq, axis=-1, keepdims=True) * (1.0 / D) + LN_EPS)
        for cb in range(NCH):
            cols = slice(cb * LANES, (cb + 1) * LANES)
            u = us[cb]
            u_ref[:, cols] = u
            nrm = (u - mu) * rstd * lg_ref[:, cols] + lb_ref[:, cols]
            gate = gate_ref[:, cols]
            y_ref[:, cols] = (nrm * _sigmoid(nrm) * (gate * _sigmoid(gate))).astype(BF16)

    halo = pl.BlockSpec((HALO, D), _halo_before)
    return pl.pallas_call(
        body, grid=(S // T,), name="conv_fwd",
        in_specs=[_rows(T, D), _rows(T, D), halo, halo, _rows(T, D),
                  _resident((HALO, D)), _resident((1, D)), _resident((1, D)), _resident((1, D))],
        out_specs=[_rows(T, D), _rows(T, D)],
        out_shape=[jax.ShapeDtypeStruct((S, D), F32), jax.ShapeDtypeStruct((S, D), BF16)],
        scratch_shapes=[pltpu.VMEM((NCH, T + HALO, LANES), F32), pltpu.VMEM((NCH, T, LANES), F32)],
        compiler_params=_params(("arbitrary",)),
    )(c_val, c_glu, c_val, c_glu, c_gate, conv_w, conv_b, ln_g, ln_b)


def _conv_bwd_taps(du, c_val, c_glu, conv_w):
    T = CONV_T
    last = S // T - 1

    def body(du_ref, dua_ref, cv_ref, cg_ref, cvh_ref, cgh_ref, w_ref, dcv_ref, dcg_ref, dw_ref,
             hwin, dwin, dhs, dw_acc):
        i = pl.program_id(0)

        @pl.when(i == 0)
        def _():
            dw_acc[...] = jnp.zeros_like(dw_acc)

        for cb in range(NCH):
            cols = slice(cb * LANES, (cb + 1) * LANES)
            hwin[cb, HALO:HALO + T, :] = cv_ref[:, cols] * _sigmoid(cg_ref[:, cols])
            hwin[cb, 0:HALO, :] = jnp.where(i > 0, cvh_ref[:, cols] * _sigmoid(cgh_ref[:, cols]), 0.0)
            dwin[cb, 0:T, :] = du_ref[:, cols]
            dwin[cb, T:T + HALO, :] = jnp.where(i < last, dua_ref[:, cols], 0.0)
        for cb in range(NCH):
            cols = slice(cb * LANES, (cb + 1) * LANES)
            taps = _taps(w_ref, cols)

            def group_dh(g, carry):
                for b in range(SUBLANES):
                    base = g * GROUP + b
                    acc = jnp.zeros((SUBLANES, LANES), F32)
                    for j in range(CONV_K):
                        acc = acc + taps[j] * _comb(dwin, cb, base + (CONV_K - 1 - j))
                    dhs[cb, pl.ds(base, SUBLANES, stride=SUBLANES), :] = acc
                return carry

            lax.fori_loop(0, T // GROUP, group_dh, 0, unroll=2)

            def group_dw(g, sums):
                for b in range(SUBLANES):
                    base = g * GROUP + b
                    d = _comb(dwin, cb, base)
                    sums = tuple(sums[j] + d * _comb(hwin, cb, base + (HALO - (CONV_K - 1) + j))
                                 for j in range(CONV_K))
                return sums

            sums = lax.fori_loop(0, T // GROUP, group_dw, tuple(dw_acc[j, :, cols] for j in range(CONV_K)))
            for j in range(CONV_K):
                dw_acc[j, :, cols] = sums[j]
            dh = dhs[cb]
            cv, sg = cv_ref[:, cols], _sigmoid(cg_ref[:, cols])
            dcv_ref[:, cols] = (dh * sg).astype(BF16)
            dcg_ref[:, cols] = (dh * cv * (sg * (1.0 - sg))).astype(BF16)

        @pl.when(i == last)
        def _():
            dw_ref[...] = jnp.zeros_like(dw_ref)
            for j in range(CONV_K):
                dw_ref[j:j + 1, :] = jnp.sum(dw_acc[j], axis=0, keepdims=True)

    before = pl.BlockSpec((HALO, D), _halo_before)
    after = pl.BlockSpec((HALO, D), _halo_after)
    big = jax.ShapeDtypeStruct((S, D), BF16)
    return pl.pallas_call(
        body, grid=(S // T,), name="conv_bwd_taps",
        in_specs=[_rows(T, D), after, _rows(T, D), _rows(T, D), before, before, _resident((HALO, D))],
        out_specs=[_rows(T, D), _rows(T, D), pl.BlockSpec((HALO, D), lambda i: (0, 0))],
        out_shape=[big, big, jax.ShapeDtypeStruct((HALO, D), F32)],
        scratch_shapes=[pltpu.VMEM((NCH, T + HALO, LANES), F32), pltpu.VMEM((NCH, T + HALO, LANES), F32),
                        pltpu.VMEM((NCH, T, LANES), F32), pltpu.VMEM((CONV_K, SUBLANES, D), F32)],
        compiler_params=_params(("arbitrary",)),
    )(du, du, c_val, c_glu, c_val, c_glu, conv_w)


def _outproj_loss(y_att, y_conv, w_out_bf, x, target, gf, u, c_gate, ln_g, ln_b):
    tm = 256

    def body(ya_ref, yc_ref, w_ref, x_ref, t_ref, gf_ref, u_ref, gate_ref, lg_ref, lb_ref,
             dx2_ref, dya_ref, du_ref, dgate_ref, dw_ref, st_ref, acc):
        @pl.when(pl.program_id(0) == 0)
        def _():
            acc[...] = jnp.zeros_like(acc)
            st_ref[...] = jnp.zeros_like(st_ref)

        ya, yc = ya_ref[...], yc_ref[...]
        x2 = x_ref[...] + _dot(ya, w_ref[0:D, :]) + _dot(yc, w_ref[D:2 * D, :])
        r = lax.rsqrt(jnp.mean(x2 * x2, axis=-1, keepdims=True) + NORM_EPS)
        xn = x2 * r
        err = xn * gf_ref[...] - t_ref[...]
        dout = err * (1.0 / D)
        dxn = dout * gf_ref[...]
        dx2 = r * (dxn - xn * jnp.mean(dxn * xn, axis=-1, keepdims=True))
        dx2_ref[...] = dx2
        dx2b = dx2.astype(BF16)
        dya_ref[...] = _dot_nt(dx2b, w_ref[0:D, :])
        dy = _dot_nt(dx2b, w_ref[D:2 * D, :])
        acc[0:D, :] += _dot_tn(ya, dx2b)
        acc[D:2 * D, :] += _dot_tn(yc, dx2b)
        st_ref[ROW_FINAL_G:ROW_FINAL_G + 1, :] += jnp.sum(dout * xn, axis=0, keepdims=True)
        st_ref[ROW_LOSS:ROW_LOSS + 1, :] += jnp.sum(err * err, axis=0, keepdims=True) * (0.5 / D)

        u, gate = u_ref[...], gate_ref[...]
        mu = jnp.mean(u, axis=-1, keepdims=True)
        uc = u - mu
        rstd = lax.rsqrt(jnp.mean(uc * uc, axis=-1, keepdims=True) + LN_EPS)
        z = uc * rstd
        nrm = z * lg_ref[...] + lb_ref[...]
        sn, sg = _sigmoid(nrm), _sigmoid(gate)
        dgate_ref[...] = (dy * (nrm * sn) * (sg * (1.0 + gate * (1.0 - sg)))).astype(BF16)
        dn = dy * (gate * sg) * (sn * (1.0 + nrm * (1.0 - sn)))
        dz = dn * lg_ref[...]
        du = rstd * (dz - jnp.mean(dz, axis=-1, keepdims=True) - z * jnp.mean(dz * z, axis=-1, keepdims=True))
        du_ref[...] = du
        st_ref[ROW_LN_G:ROW_LN_G + 1, :] += jnp.sum(dn * z, axis=0, keepdims=True)
        st_ref[ROW_LN_B:ROW_LN_B + 1, :] += jnp.sum(dn, axis=0, keepdims=True)
        st_ref[ROW_CONV_B:ROW_CONV_B + 1, :] += jnp.sum(du, axis=0, keepdims=True)

        @pl.when(pl.program_id(0) == S // tm - 1)
        def _():
            dw_ref[...] = acc[...].astype(BF16)

    big = jax.ShapeDtypeStruct((S, D), F32)
    vec = _resident((1, D))
    return pl.pallas_call(
        body, grid=(S // tm,), name="outproj_loss",
        in_specs=[_rows(tm, D), _rows(tm, D), _resident((WOUT_ROWS, D)), _rows(tm, D), _rows(tm, D), vec,
                  _rows(tm, D), _rows(tm, D), vec, vec],
        out_specs=[_rows(tm, D), _rows(tm, D), _rows(tm, D), _rows(tm, D),
                   pl.BlockSpec((WOUT_ROWS, D), lambda i: (0, 0)), pl.BlockSpec((8, D), lambda i: (0, 0))],
        out_shape=[big, big, big, jax.ShapeDtypeStruct((S, D), BF16),
                   jax.ShapeDtypeStruct((WOUT_ROWS, D), BF16), jax.ShapeDtypeStruct((8, D), F32)],
        scratch_shapes=[pltpu.VMEM((WOUT_ROWS, D), F32)],
        compiler_params=_params(("arbitrary",)),
    )(y_att, y_conv, w_out_bf, x, target, gf, u, c_gate, ln_g, ln_b)


UNITS_PER_CHUNK = CHUNK // LANES


def _dproj_unit(u, dqs, dkvs, gates, rows):
    if u < OFF_K // LANES:
        return ((dqs[0][u] + dqs[1][u] + dqs[2][u]) * (HD ** -0.5)).astype(BF16)
    if u < OFF_AG // LANES:
        w = u - OFF_K // LANES
        ta, tb = (dkvs[0][j] + dkvs[1][j] + dkvs[2][j] for j in (2 * (w % 2), 2 * (w % 2) + 1))
        low = _low_lanes(rows)
        if w < 2:
            return jnp.where(low, ta, pltpu.roll(tb, HD, axis=1)).astype(BF16)
        return jnp.where(low, pltpu.roll(ta, HD, axis=1), tb).astype(BF16)
    g, sl = divmod(u - OFF_AG // LANES, D // LANES)
    return gates[g][:, sl * LANES:(sl + 1) * LANES]


def _dproj_sources(units, dqs, dkvs, gates, rows):
    use_q = any(u < OFF_K // LANES for u in units)
    use_kv = any(OFF_K // LANES <= u < OFF_AG // LANES for u in units)
    use_g = sorted({(u - OFF_AG // LANES) // (D // LANES) for u in units if u >= OFF_AG // LANES})
    args = (list(dqs) if use_q else []) + (list(dkvs) if use_kv else []) + [gates[g] for g in use_g]
    specs = ([_slab_rows(D // LANES, rows)] * 3 if use_q else []) + ([_slab_rows(NKV, rows)] * 3 if use_kv else []) \
        + [_rows(rows, D)] * len(use_g)

    def pick(refs):
        refs = list(refs)
        q_refs = [refs.pop(0) for _ in range(3)] if use_q else None
        kv_refs = [refs.pop(0) for _ in range(3)] if use_kv else None
        return q_refs, kv_refs, {g: refs.pop(0) for g in use_g}

    return args, specs, pick


def _exchange_results_of(chip_sums):
    n = len(chip_sums) * len(CHIP_FLIPS)
    shapes = [jax.ShapeDtypeStruct((NCHIP,) + tuple(a.shape[1:] if a.ndim == 3 else a.shape), a.dtype)
              for a in chip_sums]
    return shapes, [pltpu.SemaphoreType.DMA((n,)), pltpu.SemaphoreType.DMA((n,))]


def _start_exchange(copies, first_step):
    @pl.when(first_step)
    def _():
        for out, _ in copies:
            out.start()


def _finish_exchange(copies, last_step):
    @pl.when(last_step)
    def _():
        for _, arrival in copies:
            arrival.wait_recv()
        for out, _ in copies:
            out.wait_send()


def _inproj_bwd_x(dqs, dkvs, gates, w_bf, x, g1, dx2, pi):
    tm = 256
    last = S // tm - 1
    units = range(NCOL // LANES)
    pieces, piece_specs, pick = _dproj_sources(units, dqs, dkvs, gates, tm)
    landing, sems = _exchange_results_of([pi])

    def body(*refs):
        piece_refs, refs = refs[:len(pieces)], refs[len(pieces):]
        w_ref, x_ref, g_ref, dx2_ref, pi_ref, gx_ref, st_ref, ri_ref, dp_ref, send, recv = refs
        i = pl.program_id(0)
        copies = _chip_exchange_copies([pi_ref], [ri_ref], send, recv)
        _start_exchange(copies, i == 0)

        @pl.when(i == 0)
        def _():
            st_ref[...] = jnp.zeros_like(st_ref)

        sources = pick(piece_refs)
        for u in units:
            dp_ref[:, u * LANES:(u + 1) * LANES] = _dproj_unit(u, *sources, tm)
        dh = _dot_nt(dp_ref[...], w_ref[...])
        xt = x_ref[...]
        r = lax.rsqrt(jnp.mean(xt * xt, axis=-1, keepdims=True) + NORM_EPS)
        xn = xt * r
        dxn = dh * g_ref[...]
        gx_ref[...] = dx2_ref[...] + r * (dxn - xn * jnp.mean(dxn * xn, axis=-1, keepdims=True))
        st_ref[0:1, :] += jnp.sum(dh * xn, axis=0, keepdims=True)
        _finish_exchange(copies, i == last)

    return pl.pallas_call(
        body, grid=(S // tm,), name="inproj_bwd_x",
        in_specs=piece_specs + [_resident((D, NCOL)), _rows(tm, D), _resident((1, D)), _rows(tm, D), ANY],
        out_specs=[_rows(tm, D), pl.BlockSpec((8, D), lambda i: (0, 0)), ANY],
        out_shape=[jax.ShapeDtypeStruct((S, D), F32), jax.ShapeDtypeStruct((8, D), F32)] + landing,
        scratch_shapes=[pltpu.VMEM((tm, NCOL), BF16)] + sems,
        compiler_params=_params(("arbitrary",)),
    )(*pieces, w_bf, x, g1, dx2, pi)


def _inproj_bwd_w(h, dqs, dkvs, gates):
    out = None
    for k in range(NCHIP):
        units = range(k * UNITS_PER_CHUNK, (k + 1) * UNITS_PER_CHUNK)
        tk = 512 if units[0] < OFF_K // LANES else 1024
        nk = S // tk
        pieces, piece_specs, pick = _dproj_sources(units, dqs, dkvs, gates, tk)
        handed_on = [] if out is None else [out]

        def body(*refs, units=units, pick=pick, n_pieces=len(pieces), n_in=1 + len(pieces) + len(handed_on)):
            h_ref, piece_refs = refs[0], refs[1:1 + n_pieces]
            o_ref, tile, acc = refs[n_in:]
            i = pl.program_id(0)

            @pl.when(i == 0)
            def _():
                acc[...] = jnp.zeros_like(acc)

            sources = pick(piece_refs)
            for n, u in enumerate(units):
                tile[:, n * LANES:(n + 1) * LANES] = _dproj_unit(u, *sources, tk)
            acc[...] += _dot_tn(h_ref[...], tile[...])

            @pl.when(i == nk - 1)
            def _():
                o_ref[0] = acc[...].astype(BF16)

        out = pl.pallas_call(
            body, grid=(nk,), name=f"inproj_bwd_w{k}",
            in_specs=[_rows(tk, D)] + piece_specs + [ANY] * len(handed_on),
            out_specs=pl.BlockSpec((1, D, CHUNK), lambda i, k=k: (k, 0, 0)),
            out_shape=jax.ShapeDtypeStruct((NCHIP, D, CHUNK), BF16),
            input_output_aliases={1 + len(pieces): 0} if handed_on else {},
            scratch_shapes=[pltpu.VMEM((tk, CHUNK), BF16), pltpu.VMEM((D, CHUNK), F32)],
            compiler_params=_params(("arbitrary",)),
        )(h, *pieces, *handed_on)
    return out


ROW_FINAL_G, ROW_LOSS, ROW_LN_G, ROW_LN_B, ROW_CONV_B, ROW_TAPS = 0, 1, 2, 3, 4, 8
SMALL_ROWS = 8 + HALO
NDEV = 8


MESH = pl.DeviceIdType.MESH
ANY = pl.BlockSpec(memory_space=pl.ANY)
CHIP_FLIPS = ((1, 0), (0, 1), (1, 1))


def _pos():
    return lax.axis_index("x"), lax.axis_index("y"), lax.axis_index("c")


def _flip(v, f):
    return 1 - v if f else v


def _ds(start, size, align=None):
    return pl.ds(pl.multiple_of(start, align or size), size)


def _place_shards(wi, wo, cw, where):
    steps = 4

    def body(where_ref, wi_ref, wo_ref, cw_ref, wi_full, wo_full, cw_full):
        wi_full[...] = wi_ref[...].astype(BF16)
        wo_full[...] = wo_ref[...].astype(BF16)
        cw_full[...] = cw_ref[...]

    grid_spec = pltpu.PrefetchScalarGridSpec(
        num_scalar_prefetch=1, grid=(steps,),
        in_specs=[pl.BlockSpec((D // steps, CHUNK), lambda i, w: (i, 0)),
                  pl.BlockSpec((WOUT_SHARD // steps, D), lambda i, w: (i, 0)),
                  pl.BlockSpec((HALO, CONVW_SHARD), lambda i, w: (0, 0))],
        out_specs=[pl.BlockSpec((D // steps, CHUNK), lambda i, w: (i, w[0])),
                   pl.BlockSpec((WOUT_SHARD // steps, D), lambda i, w: (w[0] * steps + i, 0)),
                   pl.BlockSpec((HALO, CONVW_SHARD), lambda i, w: (0, w[0]))])
    return pl.pallas_call(
        body, grid_spec=grid_spec, name="place_shards",
        out_shape=[jax.ShapeDtypeStruct((D, NCOL), BF16), jax.ShapeDtypeStruct((WOUT_ROWS, D), BF16),
                   jax.ShapeDtypeStruct((HALO, D), F32)],
        compiler_params=_params(("arbitrary",)),
    )(where, wi, wo, cw)


W_IN, W_OUT, TAPS = range(3)
GATHER_SEMS = 12


def _gather_stages(fulls, send, recv):
    halves = {W_IN: D // 2, W_OUT: WOUT_SHARD // 2, TAPS: HALO // 2}
    x, y, c = _pos()
    chips = {"me": (x, y), "x": (1 - x, y), "y": (x, 1 - y), "diag": (1 - x, 1 - y)}
    SENT = ((("me", 0), "x"), (("me", 1), "x"), (("me", 1), "y"), (("me", 0), "y"), (("x", 0), "y"), (("y", 1), "x"))
    LANDS = ((("x", 0), "x"), (("x", 1), "x"), (("y", 1), "y"), (("y", 0), "y"), (("diag", 0), "y"), (("diag", 1), "x"))
    N_ICI = len(SENT)

    def region(n_th, whose, half, part):
        a, full = fulls[n_th]
        chip = 2 * chips[whose][0] + chips[whose][1]
        n = halves[a] // 2
        row = half * halves[a] + part * n
        if a == W_IN:
            return full.at[_ds(row, n), _ds(chip * CHUNK, CHUNK, 128)]
        if a == W_OUT:
            return full.at[_ds(chip * WOUT_SHARD + row, n), :]
        return full.at[_ds(row, n), _ds(chip * CONVW_SHARD, CONVW_SHARD, 128)]

    def copy(n_th, kind, piece, dev):
        k = GATHER_SEMS * n_th + kind
        return pltpu.make_async_remote_copy(src_ref=piece, dst_ref=piece, send_sem=send.at[k], recv_sem=recv.at[k],
                                            device_id=dev, device_id_type=MESH)

    def sent(a, k):
        if k < N_ICI:
            (whose, part), to = SENT[k]
            return copy(a, k, region(a, whose, c, part), (*chips[to], c))
        (whose, part), _ = LANDS[k - N_ICI]
        return copy(a, k, region(a, whose, c, part), (x, y, 1 - c))

    def wait_arrival(a, k):
        if k < N_ICI:
            (whose, part), frm = LANDS[k]
            copy(a, k, region(a, whose, c, part), (*chips[frm], c)).wait_recv()
        else:
            (whose, part), _ = LANDS[k - N_ICI]
            copy(a, k, region(a, whose, 1 - c, part), (x, y, 1 - c)).wait_recv()

    arrays = range(len(fulls))

    def own_to_neighbours():
        for a in arrays:
            for k in (0, 2, 1, 3):
                sent(a, k).start()

    def pass_on_neighbours():
        for a in arrays:
            for k, onward in ((0, 4), (2, 5), (1, None), (3, None)):
                wait_arrival(a, k)
                if onward is not None:
                    sent(a, onward).start()
                sent(a, k + N_ICI).start()

    def pass_on_diagonal():
        for a in arrays:
            for k in (4, 5):
                wait_arrival(a, k)
                sent(a, k + N_ICI).start()

    def finish():
        for a in arrays:
            for k in range(N_ICI, 2 * N_ICI):
                wait_arrival(a, k)
            for k in range(2 * N_ICI):
                sent(a, k).wait_send()

    return own_to_neighbours, pass_on_neighbours, pass_on_diagonal, finish


def _gather_sems(n_arrays):
    return [pltpu.SemaphoreType.DMA((GATHER_SEMS * n_arrays,)), pltpu.SemaphoreType.DMA((GATHER_SEMS * n_arrays,))]


def _gather_w_in(wi_full):
    def body(_wi, full, send, recv):
        for stage in _gather_stages([(W_IN, full)], send, recv):
            stage()

    return pl.pallas_call(
        body, name="gather_w_in", in_specs=[ANY], out_specs=ANY, input_output_aliases={0: 0},
        out_shape=jax.ShapeDtypeStruct((D, NCOL), BF16), scratch_shapes=_gather_sems(1),
    )(wi_full)


def _half_shape(a):
    return jax.ShapeDtypeStruct((NCHIP, a.shape[1] // 2, a.shape[2]) if a.ndim == 3 else a.shape, a.dtype)


def _exchange_halves(arrays, name):
    n = len(arrays)

    def body(*refs):
        srcs, dsts, (send, recv) = refs[:n], refs[n:2 * n], refs[2 * n:]
        x, y, c = _pos()
        cps = []
        for k, (s_, d_) in enumerate(zip(srcs, dsts)):
            if len(s_.shape) == 3:
                h = s_.shape[1] // 2
                s_ = s_.at[:, _ds((1 - c) * h, h), :]
            cps.append(pltpu.make_async_remote_copy(src_ref=s_, dst_ref=d_, send_sem=send.at[k], recv_sem=recv.at[k],
                                                    device_id=(x, y, 1 - c), device_id_type=MESH))
        for cp in cps:
            cp.start()
        for cp in cps:
            cp.wait()

    return pl.pallas_call(
        body, name=name, in_specs=[ANY] * n, out_specs=[ANY] * n, out_shape=[_half_shape(a) for a in arrays],
        scratch_shapes=[pltpu.SemaphoreType.DMA((n,)), pltpu.SemaphoreType.DMA((n,))],
    )(*arrays)


def _add_halves(arrays, received, name):
    n = len(arrays)

    def body(*refs):
        mine, theirs, outs = refs[:n], refs[n:2 * n], refs[2 * n:]
        c = lax.axis_index("c")
        for m_, t_, o_ in zip(mine, theirs, outs):
            if len(m_.shape) == 3:
                h = m_.shape[1] // 2
                o_[0] = (m_[0, _ds(c * h, h), :].astype(F32) + t_[0].astype(F32)).astype(o_.dtype)
            else:
                o_[...] = m_[...] + t_[...]

    def spec(shape):
        if len(shape) == 3:
            return pl.BlockSpec((1,) + tuple(shape[1:]), lambda k: (k, 0, 0))
        return pl.BlockSpec(tuple(shape), lambda k: (0, 0))

    halves = [_half_shape(a) for a in arrays]
    return pl.pallas_call(
        body, grid=(NCHIP,), name=name,
        in_specs=[spec(a.shape) for a in arrays] + [spec(h.shape) for h in halves],
        out_specs=[spec(h.shape) for h in halves], out_shape=halves,
        compiler_params=_params(("arbitrary",)),
    )(*arrays, *received)


def _chip_exchange_copies(srcs, dsts, send, recv):
    x, y, c = _pos()
    me = 2 * x + y
    pairs = []
    for a in range(len(srcs)):
        for j, (fx, fy) in enumerate(CHIP_FLIPS):
            px, py = _flip(x, fx), _flip(y, fy)
            peer = 2 * px + py
            k = len(CHIP_FLIPS) * a + j
            out = pltpu.make_async_remote_copy(
                src_ref=srcs[a].at[peer] if len(srcs[a].shape) == 3 else srcs[a], dst_ref=dsts[a].at[me],
                send_sem=send.at[k], recv_sem=recv.at[k], device_id=(px, py, c), device_id_type=MESH)
            got = dsts[a].at[peer]
            arrival = pltpu.make_async_remote_copy(
                src_ref=got, dst_ref=got, send_sem=send.at[k], recv_sem=recv.at[k],
                device_id=(px, py, c), device_id_type=MESH)
            pairs.append((out, arrival))
    return pairs


def _sum_chips(ri, ro, rs, pi, po, ps, where):
    def body(w_ref, ri_ref, ro_ref, rs_ref, pi_ref, po_ref, ps_ref, gi_ref, go_ref, gs_ref, g5_ref, loss_ref,
             acc_i, acc_o, acc_s):
        k = pl.program_id(0)
        accs = (acc_i, acc_o, acc_s)

        @pl.when(k == 0)
        def _():
            for acc in accs:
                acc[...] = jnp.zeros_like(acc)

        @pl.when(k == w_ref[0])
        def _():
            for acc, val in zip(accs, (pi_ref[0], po_ref[0], ps_ref[...])):
                acc[...] += val.astype(F32)

        @pl.when(k != w_ref[0])
        def _():
            for acc, ref in zip(accs, (ri_ref, ro_ref, rs_ref)):
                acc[...] += ref[0].astype(F32)

        @pl.when(k == NCHIP - 1)
        def _():
            gi_ref[0] = acc_i[...]
            go_ref[0] = acc_o[...]
            gs_ref[...] = acc_s[...]
            g5_ref[...] = jnp.zeros_like(g5_ref)
            for i, row in enumerate((ROW_CONV_B, ROW_LN_G, ROW_LN_B, ROW_FINAL_G)):
                g5_ref[i + 1:i + 2, :] = acc_s[row:row + 1, :]
            loss = jnp.sum(acc_s[ROW_LOSS:ROW_LOSS + 1, :], axis=1, keepdims=True)
            loss_ref[...] = jnp.broadcast_to(loss, loss_ref.shape)

    def sent(k, w):
        return jnp.where(k == w[0], (k + 1) % NCHIP, k)

    hi, ho = D // 2, WOUT_SHARD // 2
    const = lambda shape: pl.BlockSpec(shape, lambda k, w: (0,) * len(shape))
    grid_spec = pltpu.PrefetchScalarGridSpec(
        num_scalar_prefetch=1, grid=(NCHIP,),
        in_specs=[pl.BlockSpec((1, hi, CHUNK), lambda k, w: (sent(k, w), 0, 0)),
                  pl.BlockSpec((1, ho, D), lambda k, w: (sent(k, w), 0, 0)),
                  pl.BlockSpec((1, SMALL_ROWS, D), lambda k, w: (sent(k, w), 0, 0)),
                  pl.BlockSpec((1, hi, CHUNK), lambda k, w: (w[0], 0, 0)),
                  pl.BlockSpec((1, ho, D), lambda k, w: (w[0], 0, 0)),
                  const((SMALL_ROWS, D))],
        out_specs=[pl.BlockSpec((1, hi, CHUNK), lambda k, w: (w[1], 0, 0)),
                   pl.BlockSpec((1, ho, D), lambda k, w: (w[1], 0, 0)),
                   const((SMALL_ROWS, D)), const((8, D)), const((8, LANES))],
        scratch_shapes=[pltpu.VMEM((hi, CHUNK), F32), pltpu.VMEM((ho, D), F32), pltpu.VMEM((SMALL_ROWS, D), F32)])
    return pl.pallas_call(
        body, grid_spec=grid_spec, name="sum_chips",
        out_shape=[jax.ShapeDtypeStruct((2, hi, CHUNK), F32), jax.ShapeDtypeStruct((2, ho, D), F32),
                   jax.ShapeDtypeStruct((SMALL_ROWS, D), F32), jax.ShapeDtypeStruct((8, D), F32),
                   jax.ShapeDtypeStruct((8, LANES), F32)],
        compiler_params=_params(("arbitrary",)),
    )(where, ri, ro, rs, pi, po, ps)


def _exchange_results(gi2, go2, st):
    flips = [(fx, fy, fc) for fx in (0, 1) for fy in (0, 1) for fc in (0, 1)][1:]

    def body(_gi, _go, st_ref, gi_ref, go_ref, all_ref, send, recv, lsem):
        x, y, c = _pos()
        sib = (x, y, 1 - c)

        def half(k, ref, slot):
            return pltpu.make_async_remote_copy(src_ref=ref.at[slot], dst_ref=ref.at[slot], send_sem=send.at[k],
                                                recv_sem=recv.at[k], device_id=sib, device_id_type=MESH)

        def stat(k, src, slot, dev):
            return pltpu.make_async_remote_copy(src_ref=src, dst_ref=all_ref.at[slot], send_sem=send.at[k],
                                                recv_sem=recv.at[k], device_id=dev, device_id_type=MESH)

        mine = pltpu.make_async_copy(st_ref, all_ref.at[4 * x + 2 * y + c], lsem)
        mine.start()
        sends = [half(k, ref, c) for k, ref in enumerate((gi_ref, go_ref))]
        peers = [(_flip(x, fx), _flip(y, fy), _flip(c, fc)) for fx, fy, fc in flips]
        sends += [stat(2 + k, st_ref, 4 * x + 2 * y + c, dev) for k, dev in enumerate(peers)]
        for cp in sends:
            cp.start()
        for k, ref in enumerate((gi_ref, go_ref)):
            half(k, ref, 1 - c).wait_recv()
        for k, (px, py, pc) in enumerate(peers):
            slot = 4 * px + 2 * py + pc
            stat(2 + k, all_ref.at[slot], slot, (px, py, pc)).wait_recv()
        for cp in sends:
            cp.wait_send()
        mine.wait()

    n = 2 + len(flips)
    return pl.pallas_call(
        body, name="exchange_results",
        in_specs=[ANY, ANY, ANY], out_specs=[ANY, ANY, ANY], input_output_aliases={0: 0, 1: 1},
        out_shape=[jax.ShapeDtypeStruct((2, D // 2, CHUNK), F32), jax.ShapeDtypeStruct((2, WOUT_SHARD // 2, D), F32),
                   jax.ShapeDtypeStruct((NDEV, 8, D), F32)],
        scratch_shapes=[pltpu.SemaphoreType.DMA((n,)), pltpu.SemaphoreType.DMA((n,)), pltpu.SemaphoreType.DMA],
    )(gi2, go2, st)


def _adamw_math(w, g, m, v):
    m2 = ADAM_B1 * m + (1.0 - ADAM_B1) * g
    v2 = ADAM_B2 * v + (1.0 - ADAM_B2) * (g * g)
    m_hat = m2 / (1.0 - ADAM_B1 ** ADAM_STEP)
    v_hat = v2 / (1.0 - ADAM_B2 ** ADAM_STEP)
    delta = -ADAM_LR * (m_hat / (jnp.sqrt(v_hat) + ADAM_EPS) + ADAM_WD * w)
    return delta, m2, v2


def _adamw(w, g, m, v, name):
    rows, cols = w.shape
    tm = 256 if rows % 256 == 0 else rows

    def body(w_ref, g_ref, m_ref, v_ref, d_ref, m2_ref, v2_ref):
        d_ref[...], m2_ref[...], v2_ref[...] = _adamw_math(w_ref[...], g_ref[...], m_ref[...], v_ref[...])

    shape = jax.ShapeDtypeStruct(w.shape, F32)
    return pl.pallas_call(
        body, grid=(rows // tm,), name=name,
        in_specs=[_rows(tm, cols)] * 4, out_specs=[_rows(tm, cols)] * 3, out_shape=[shape] * 3,
        compiler_params=_params(("arbitrary",)),
    )(w, g, m, v)


def _adamw_vectors(g5, first_parts, ws, ms, vs):
    n = len(ws)

    def body(g_ref, parts_ref, *refs):
        ins, g0_ref, outs = refs[:3 * n], refs[3 * n], refs[3 * n + 1:]
        g0 = parts_ref[0, 0:1, :]
        for dev in range(1, NDEV):
            g0 = g0 + parts_ref[dev, 0:1, :]
        g0_ref[...] = g0
        for i in range(n):
            g = g0 if i == 0 else g_ref[i:i + 1, :]
            res = _adamw_math(ins[i][...], g, ins[n + i][...], ins[2 * n + i][...])
            for kind in range(3):
                outs[kind * n + i][...] = res[kind]

    shape = jax.ShapeDtypeStruct((1, D), F32)
    return pl.pallas_call(body, name="adamw_vectors", out_shape=[shape] * (1 + 3 * n), compiler_params=_params())(
        g5, first_parts, *ws, *ms, *vs)


def kernel(x, norm_g, w_in, conv_w, conv_b, conv_ln_g, conv_ln_b, w_out, final_norm_g, loss_target, m_norm_g, m_w_in, m_conv_w, m_conv_b, m_conv_ln_g, m_conv_ln_b, m_w_out, m_final_norm_g, v_norm_g, v_w_in, v_conv_w, v_conv_b, v_conv_ln_g, v_conv_ln_b, v_w_out, v_final_norm_g):
    chip = 2 * lax.axis_index("x") + lax.axis_index("y")
    where = jnp.stack([chip, lax.axis_index("c")]).astype(jnp.int32)
    taps_shard = jnp.pad(conv_w[0], ((0, HALO - CONV_K), (0, 0)))
    wi_full, wo_full, cw_full = _place_shards(w_in[0], w_out[0], taps_shard, where)
    wi_full = _gather_w_in(wi_full)

    gf = final_norm_g[None]
    xb = x[0]
    h, q, k, v, a_gate, c_val, c_glu, c_gate, wo_full, cw_full = _inproj_fwd(xb, norm_g, wi_full, wo_full, cw_full)
    tables = [_bias_table(d) for d in PATTERNS]
    o, lse, y_att = _attn_fwd(q, k, v, tables, a_gate)
    u, y_conv = _conv_fwd(c_val, c_glu, c_gate, cw_full, conv_b, conv_ln_g, conv_ln_b)
    dx2, dy_att, du, dc_gate, dw_out, st_out = _outproj_loss(
        y_att, y_conv, wo_full, xb, loss_target[0], gf, u, c_gate, conv_ln_g, conv_ln_b)
    dc_val, dc_glu, dconv_w = _conv_bwd_taps(du, c_val, c_glu, cw_full)

    early = [dw_out.reshape(NCHIP, WOUT_SHARD, D), jnp.concatenate([st_out, dconv_w], axis=0)]
    po, ps = _add_halves(early, _exchange_halves(early, "exchange_halves_early"), "add_halves_early")
    do, da_gate, delta, ro, rs = _attn_gate_bwd(dy_att, o, a_gate, _head_sum_selectors(), [po, ps])
    dqs, dkvs = zip(*[_attn_bwd(q, k, v, do, lse, delta, t, d) for t, d in zip(tables, PATTERNS)])

    dproj_pieces = (dqs, dkvs, (da_gate, dc_val, dc_glu, dc_gate))
    late = [_inproj_bwd_w(h, *dproj_pieces)]
    (pi,) = _add_halves(late, _exchange_halves(late, "exchange_halves"), "add_halves")
    grad_x, st_in, ri = _inproj_bwd_x(*dproj_pieces, wi_full, xb, norm_g, dx2, pi)
    gi2, go2, g_small, g5, loss8 = _sum_chips(ri, ro, rs, pi, po, ps, where)
    gi2, go2, norm_g_parts = _exchange_results(gi2, go2, st_in)
    g_w_in = gi2.reshape(D, CHUNK)
    g_w_out = go2.reshape(WOUT_SHARD, D)
    g_taps = lax.dynamic_slice(g_small, (ROW_TAPS, chip * CONVW_SHARD), (CONV_K, CONVW_SHARD))

    d_w_in, m2_w_in, v2_w_in = _adamw(w_in[0], g_w_in, m_w_in[0], v_w_in[0], "adamw_w_in")
    d_w_out, m2_w_out, v2_w_out = _adamw(w_out[0], g_w_out, m_w_out[0], v_w_out[0], "adamw_w_out")
    d_taps, m2_taps, v2_taps = _adamw(conv_w[0], g_taps, m_conv_w[0], v_conv_w[0], "adamw_conv_w")
    g_norm, *vec = _adamw_vectors(
        g5, norm_g_parts,
        (norm_g, conv_b, conv_ln_g, conv_ln_b, gf),
        (m_norm_g, m_conv_b, m_conv_ln_g, m_conv_ln_b, m_final_norm_g[None]),
        (v_norm_g, v_conv_b, v_conv_ln_g, v_conv_ln_b, v_final_norm_g[None]))
    d_vec, m2_vec, v2_vec = vec[0:5], vec[5:10], vec[10:15]

    def weight_order(ng, wi, cw, cb, lg, lb, wo, fg):
        return (ng, wi[None], cw[None], cb, lg, lb, wo[None], fg[0])

    grads = weight_order(g_norm, g_w_in, g_taps, g5[1:2], g5[2:3], g5[3:4], g_w_out, g5[4:5])
    deltas = weight_order(d_vec[0], d_w_in, d_taps, d_vec[1], d_vec[2], d_vec[3], d_w_out, d_vec[4])
    new_m = weight_order(m2_vec[0], m2_w_in, m2_taps, m2_vec[1], m2_vec[2], m2_vec[3], m2_w_out, m2_vec[4])
    new_v = weight_order(v2_vec[0], v2_w_in, v2_taps, v2_vec[1], v2_vec[2], v2_vec[3], v2_w_out, v2_vec[4])
    return (loss8[0, 0], grad_x[None], *grads, *deltas, *new_m, *new_v)
```

```python
import jax
import jax.numpy as jnp
from jax import lax
from jax.experimental import pallas as pl
from jax.experimental.pallas import tpu as pltpu

F32 = jnp.float32
BF16 = jnp.bfloat16

S = 4096
D = 1024
LANES = 128
HD = 64
NKV = 4
GQ = 4
KVW = NKV * HD
NCOL = 5632
CONV_K = 31
HALO = 32
BLK = 128
PATTERNS = (1, 4, 16)
NORM_EPS = 1e-6
LN_EPS = 1e-5
NEG = -1e30
OFF_Q, OFF_K, OFF_AG, OFF_CV, OFF_CG, OFF_CGATE = 0, 1024, 1536, 2560, 3584, 4608
NCHIP = 4
CHUNK = NCOL // NCHIP
WOUT_ROWS = 2 * D
WOUT_SHARD = WOUT_ROWS // NCHIP
CONVW_SHARD = D // NCHIP

ADAM_LR, ADAM_B1, ADAM_B2, ADAM_EPS, ADAM_WD, ADAM_STEP = 0.001, 0.9, 0.999, 1e-08, 0.01, 10

VMEM_LIMIT = 56 * 1024 * 1024


def _params(sem=None, vmem=VMEM_LIMIT):
    return pltpu.CompilerParams(dimension_semantics=sem, vmem_limit_bytes=vmem)


def _sigmoid(a):
    return 0.5 * jnp.tanh(0.5 * a) + 0.5


def _rows(tm, width):
    return pl.BlockSpec((tm, width), lambda i: (i, 0))


def _slabs(n):
    return jax.ShapeDtypeStruct((n, S, LANES), F32)


def _slab_rows(n, tm):
    return pl.BlockSpec((n, tm, LANES), lambda i: (0, i, 0))


def _resident(shape):
    return pl.BlockSpec(shape, lambda *_: (0,) * len(shape), pipeline_mode=pl.Buffered(1))


def _dot(a, b):
    return jnp.dot(a, b, preferred_element_type=F32)


def _dot_nt(a, b):
    return lax.dot_general(a, b, (((1,), (1,)), ((), ())), preferred_element_type=F32)


def _dot_tn(a, b):
    return lax.dot_general(a, b, (((0,), (0,)), ((), ())), preferred_element_type=F32)


def _inproj_fwd(x, g1, w_bf, wo_full, cw_full):
    tm = 512
    steps = S // tm

    def body(x_ref, g_ref, w_ref, _wo, _cw, h_ref, q_ref, k_ref, v_ref, ag_ref, cv_ref, cg_ref, cgate_ref,
             wo_ref, cw_ref, send, recv):
        i = pl.program_id(0)
        stages = _gather_stages([(W_OUT, wo_ref), (TAPS, cw_ref)], send, recv)
        for stage, step in zip(stages[:3], (0, steps // 2 - 1, steps - 2)):
            pl.when(i == step)(stage)
        xt = x_ref[...]
        r = lax.rsqrt(jnp.mean(xt * xt, axis=-1, keepdims=True) + NORM_EPS)
        h = (xt * r * g_ref[...]).astype(BF16)
        h_ref[...] = h
        q = _dot(h, w_ref[:, OFF_Q:OFF_Q + D]) * (HD ** -0.5)
        kv = _dot(h, w_ref[:, OFF_K:OFF_K + 2 * KVW])
        for sl in range(D // LANES):
            q_ref[sl] = q[:, sl * LANES:(sl + 1) * LANES]
        for sl in range(KVW // LANES):
            k_ref[sl] = kv[:, sl * LANES:(sl + 1) * LANES]
            v_ref[sl] = kv[:, KVW + sl * LANES:KVW + (sl + 1) * LANES]
        ag_ref[...] = _dot(h, w_ref[:, OFF_AG:OFF_AG + D])
        cv_ref[...] = _dot(h, w_ref[:, OFF_CV:OFF_CV + D])
        cg_ref[...] = _dot(h, w_ref[:, OFF_CG:OFF_CG + D])
        cgate_ref[...] = _dot(h, w_ref[:, OFF_CGATE:OFF_CGATE + D])
        pl.when(i == steps - 1)(stages[3])

    big = jax.ShapeDtypeStruct((S, D), F32)
    return pl.pallas_call(
        body, grid=(steps,), name="inproj_fwd",
        in_specs=[_rows(tm, D), _resident((1, D)), _resident((D, NCOL)), ANY, ANY],
        out_specs=[_rows(tm, D), _slab_rows(D // LANES, tm), _slab_rows(KVW // LANES, tm), _slab_rows(KVW // LANES, tm),
                   _rows(tm, D), _rows(tm, D), _rows(tm, D), _rows(tm, D), ANY, ANY],
        out_shape=[jax.ShapeDtypeStruct((S, D), BF16), _slabs(D // LANES), _slabs(KVW // LANES), _slabs(KVW // LANES),
                   big, big, big, big,
                   jax.ShapeDtypeStruct((WOUT_ROWS, D), BF16), jax.ShapeDtypeStruct((HALO, D), F32)],
        input_output_aliases={3: 8, 4: 9},
        scratch_shapes=_gather_sems(2),
        compiler_params=_params(("arbitrary",)),
    )(x, g1, w_bf, wo_full, cw_full)


def _bias_table(d):
    h = jnp.arange(NKV * GQ, dtype=F32)
    slopes = jnp.exp2(-8.0 * (h + 1.0) / (NKV * GQ))
    qi = jnp.arange(BLK)[:, None]
    kj = jnp.arange(2 * BLK)[None, :]
    dist = BLK + qi - kj
    window = (dist >= 0) & (dist <= BLK)
    bias = -slopes[:, None, None] * (dist * d).astype(F32)[None]
    has_prev = jnp.stack([jnp.broadcast_to(kj >= BLK, (BLK, 2 * BLK)), jnp.ones((BLK, 2 * BLK), bool)])
    valid = window[None] & has_prev
    tab = jnp.where(valid[:, None], bias[None], NEG)
    return tab.reshape(2, NKV, GQ * BLK, 2 * BLK)


def _sub_rows(start, d):
    if d == 1:
        return pl.ds(pl.multiple_of(start, BLK), BLK)
    return pl.ds(start, BLK, stride=d)


CHUNK_ROWS = 2048
BLOCKS_PER_CHUNK = CHUNK_ROWS // BLK


def _low_lanes(rows=BLK):
    return lax.broadcasted_iota(jnp.int32, (rows, LANES), 1) < HD


def _block_start(idx, d):
    shift = d.bit_length() - 1
    b, r = lax.shift_right_logical(idx, shift), lax.bitwise_and(idx, d - 1)
    start = b * (BLK * d) + r
    return b, start, jnp.maximum(start - BLK * d, r)


def _stack_heads(ref, rows):
    low = _low_lanes()
    t0, t1 = ref[0, rows, :], ref[1, rows, :]
    return jnp.concatenate([jnp.where(low, t0, 0.0), jnp.where(low, 0.0, t0),
                            jnp.where(low, t1, 0.0), jnp.where(low, 0.0, t1)], axis=0).astype(BF16)


def _unstack_heads(dup):
    low = _low_lanes()
    return (jnp.where(low, dup[0:BLK], dup[BLK:2 * BLK]), jnp.where(low, dup[2 * BLK:3 * BLK], dup[3 * BLK:4 * BLK]))


def _kv_dup(ref, prow, rows, odd):
    t = jnp.concatenate([ref[0, prow, :], ref[0, rows, :]], axis=0)
    swapped = pltpu.roll(t, HD, axis=1)
    keep = jnp.logical_xor(_low_lanes(2 * BLK), odd)
    return jnp.where(keep, t, swapped).astype(BF16)


PIECES = 3


def _by_head(tiles):
    lane = lax.broadcasted_iota(jnp.int32, tiles[0].shape, 1)
    out = tiles[0]
    for g in range(1, GQ):
        out = jnp.where(lax.bitwise_and(lane, GQ - 1) == g, tiles[g], out)
    return out


def _minus_in_pieces(x):
    lane = lax.broadcasted_iota(jnp.int32, x.shape, 1)
    hi = (-x).astype(BF16).astype(F32)
    rest = -x - hi
    mid = rest.astype(BF16).astype(F32)
    lo = (rest - mid).astype(BF16).astype(F32)
    return jnp.where(lane < GQ, hi, jnp.where(lane < 2 * GQ, mid, jnp.where(lane < PIECES * GQ, lo, 0.0)))


def _attn_fwd(q, k, v, tables, a_gate):
    tm = 256
    width = GQ * HD

    lane_out = jnp.arange(LANES)[None, :] // HD
    spread_sel = jnp.stack([jnp.arange(LANES)[:, None] == 2 * half + lane_out for half in range(2)]).astype(BF16)

    def body(q_ref, k_ref, v_ref, b1_ref, b2_ref, b3_ref, ag_ref, sel_ref, o_ref, lse_ref, y_ref, op, lp):
        odd = pl.program_id(0) % 2 == 1
        chunk = pl.program_id(1)
        ones = jnp.ones((2 * BLK, LANES), BF16)

        for pat, (d, b_ref) in enumerate(zip(PATTERNS, (b1_ref, b2_ref, b3_ref))):
            def block(idx, carry, pat=pat, d=d, b_ref=b_ref):
                b, start, pstart = _block_start(chunk * BLOCKS_PER_CHUNK + idx, d)
                rows, prow = _sub_rows(start, d), _sub_rows(pstart, d)
                mine = _sub_rows(start - chunk * CHUNK_ROWS, d)
                qs = _stack_heads(q_ref, mine)
                kw = _kv_dup(k_ref, prow, rows, odd)
                vw = _kv_dup(v_ref, prow, rows, odd)
                s = _dot_nt(qs, kw) + b_ref[jnp.minimum(b, 1), 0]
                m = jnp.max(s, axis=1, keepdims=True)
                p = jnp.exp(s - m).astype(BF16)
                ol = _dot(p, jnp.concatenate([vw, ones], axis=1))
                l = ol[:, LANES:]
                op[pat, 0, mine, :], op[pat, 1, mine, :] = _unstack_heads(ol[:, :LANES] / l)
                lp[pat, mine, :] = _by_head([(m + jnp.log(l))[g * BLK:(g + 1) * BLK] for g in range(GQ)])
                return carry

            lax.fori_loop(0, BLOCKS_PER_CHUNK, block, 0, unroll=2)

        def mix(t, carry):
            r = pl.ds(pl.multiple_of(t * tm, tm), tm)
            a, b, c = lp[0, r, :], lp[1, r, :], lp[2, r, :]
            m = jnp.maximum(jnp.maximum(a, b), c)
            ea, eb, ec = jnp.exp(a - m), jnp.exp(b - m), jnp.exp(c - m)
            den = ea + eb + ec
            lse_ref[0, r, :] = _minus_in_pieces(m + jnp.log(den))
            inv = 1.0 / den
            for half in range(2):
                def spread(w):
                    hi = w.astype(BF16)
                    lo = (w - hi.astype(F32)).astype(BF16)
                    return _dot(hi, sel_ref[half]) + _dot(lo, sel_ref[half])

                o = (spread(ea * inv) * op[0, half, r, :] + spread(eb * inv) * op[1, half, r, :]
                     + spread(ec * inv) * op[2, half, r, :])
                o_ref[half, r, :] = o
                cols = slice(half * LANES, (half + 1) * LANES)
                ag = ag_ref[r, cols]
                y_ref[r, cols] = (o * (ag * _sigmoid(ag))).astype(BF16)
            return carry

        lax.fori_loop(0, CHUNK_ROWS // tm, mix, 0, unroll=2)

    q_like = pl.BlockSpec((2, CHUNK_ROWS, LANES), lambda j, c: (j, c, 0))
    per_kv = pl.BlockSpec((1, CHUNK_ROWS, LANES), lambda j, c: (j, c, 0))
    kv = pl.BlockSpec((1, S, LANES), lambda j, c: (j // 2, 0, 0))
    bias_spec = pl.BlockSpec((2, 1, GQ * BLK, 2 * BLK), lambda j, c: (0, j, 0, 0))
    group_cols = pl.BlockSpec((CHUNK_ROWS, width), lambda j, c: (c, j))
    return pl.pallas_call(
        body, grid=(NKV, S // CHUNK_ROWS), name="attn_fwd",
        in_specs=[q_like, kv, kv, bias_spec, bias_spec, bias_spec, group_cols,
                  pl.BlockSpec((2, LANES, LANES), lambda j, c: (0, 0, 0))],
        out_specs=[q_like, per_kv, group_cols],
        out_shape=[_slabs(D // LANES), _slabs(NKV), jax.ShapeDtypeStruct((S, D), BF16)],
        scratch_shapes=[pltpu.VMEM((len(PATTERNS), 2, CHUNK_ROWS, LANES), F32),
                        pltpu.VMEM((len(PATTERNS), CHUNK_ROWS, LANES), F32)],
        compiler_params=_params(("arbitrary", "arbitrary")),
    )(q, k, v, *tables, a_gate, spread_sel)


def _head_sum_selectors():
    lane_in = jnp.arange(LANES)[:, None] // HD
    return jnp.stack([jnp.broadcast_to(lane_in == h, (LANES, LANES)) for h in range(2)]).astype(BF16)


def _attn_gate_bwd(dy_att, o, a_gate, selectors, chip_sums):
    tm = 256
    last = S // tm - 1
    landing, sems = _exchange_results_of(chip_sums)
    n_sums = len(chip_sums)

    def body(dy_ref, o_ref, ag_ref, e_ref, *refs):
        sums, (do_ref, dag_ref, delta_ref), refs = refs[:n_sums], refs[n_sums:n_sums + 3], refs[n_sums + 3:]
        landed, (send, recv) = refs[:n_sums], refs[n_sums:]
        i = pl.program_id(0)
        copies = _chip_exchange_copies(sums, landed, send, recv)
        _start_exchange(copies, i == 0)
        for j in range(NKV):
            deltas = []
            for sl in (2 * j, 2 * j + 1):
                cols = slice(sl * LANES, (sl + 1) * LANES)
                dy, ag, o_ = dy_ref[:, cols], ag_ref[:, cols], o_ref[sl]
                sg = _sigmoid(ag)
                do = dy * (ag * sg)
                do_ref[sl] = do
                dag_ref[:, cols] = (dy * o_ * (sg * (1.0 + ag * (1.0 - sg)))).astype(BF16)
                prod = do * o_
                hi = prod.astype(BF16)
                lo = (prod - hi.astype(F32)).astype(BF16)
                deltas += [_dot(hi, e_ref[h]) + _dot(lo, e_ref[h]) for h in range(2)]
            delta_ref[j] = _minus_in_pieces(_by_head(deltas))
        _finish_exchange(copies, i == last)

    return pl.pallas_call(
        body, grid=(S // tm,), name="attn_gate_bwd",
        in_specs=[_rows(tm, D), _slab_rows(D // LANES, tm), _rows(tm, D), _resident((2, LANES, LANES))] + [ANY] * n_sums,
        out_specs=[_slab_rows(D // LANES, tm), _rows(tm, D), _slab_rows(NKV, tm)] + [ANY] * n_sums,
        out_shape=[_slabs(D // LANES), jax.ShapeDtypeStruct((S, D), BF16), _slabs(NKV)] + landing,
        scratch_shapes=sems,
        compiler_params=_params(("arbitrary",)),
    )(dy_att, o, a_gate, selectors, *chip_sums)


def _own_pieces(tile):
    lane = lax.broadcasted_iota(jnp.int32, tile.shape, 1)
    head = jnp.where(lane < PIECES * GQ, lax.bitwise_and(lane, GQ - 1), -1)
    return jnp.concatenate([jnp.where(head == g, tile, 0.0) for g in range(GQ)], axis=0).astype(BF16)


def _attn_bwd(q, k, v, do, lse, delta, bias, d):
    def body(q_ref, do_ref, l_ref, dl_ref, k_ref, v_ref, b_ref, dq_ref, dkv_ref, acc):
        odd = pl.program_id(0) % 2 == 1
        chunk = pl.program_id(1)
        ones = (lax.broadcasted_iota(jnp.int32, (2 * BLK, LANES), 1) < PIECES * GQ).astype(BF16)

        def in_acc(block_idx):
            return pl.ds(pl.multiple_of(block_idx * BLK, BLK), BLK)

        @pl.when(chunk == 0)
        def _():
            acc[...] = jnp.zeros_like(acc)

        def block(idx, carry):
            idx = chunk * BLOCKS_PER_CHUNK + idx
            b, start, pstart = _block_start(idx, d)
            rows, prow = _sub_rows(start, d), _sub_rows(pstart, d)
            mine = _sub_rows(start - chunk * CHUNK_ROWS, d)
            qs = _stack_heads(q_ref, mine)
            dos = _stack_heads(do_ref, mine)
            kw = _kv_dup(k_ref, prow, rows, odd)
            vw = _kv_dup(v_ref, prow, rows, odd)
            s = _dot_nt(jnp.concatenate([qs, _own_pieces(l_ref[0, mine, :])], axis=1),
                        jnp.concatenate([kw, ones], axis=1)) + b_ref[jnp.minimum(b, 1), 0]
            p = jnp.exp(s)
            dv2 = _dot_tn(p.astype(BF16), dos)
            dp = _dot_nt(jnp.concatenate([dos, _own_pieces(dl_ref[0, mine, :])], axis=1),
                         jnp.concatenate([vw, ones], axis=1))
            ds = (p * dp).astype(BF16)
            dq_ref[0, mine, :], dq_ref[1, mine, :] = _unstack_heads(_dot(ds, kw))
            dk2 = _dot_tn(ds, qs)
            dkv = jnp.where(_low_lanes(2 * BLK), dk2 + pltpu.roll(dk2, HD, axis=1), dv2 + pltpu.roll(dv2, HD, axis=1))
            acc[in_acc(idx), :] = acc[in_acc(idx), :] + dkv[BLK:]
            before = jnp.where(b >= 1, idx - d, idx)
            acc[in_acc(before), :] = acc[in_acc(before), :] + dkv[:BLK]
            return carry

        lax.fori_loop(0, BLOCKS_PER_CHUNK, block, 0, unroll=16)

        @pl.when(chunk == S // CHUNK_ROWS - 1)
        def _():
            def place(idx, carry):
                _, start, _ = _block_start(idx, d)
                dkv_ref[0, _sub_rows(start, d), :] = acc[in_acc(idx), :]
                return carry

            lax.fori_loop(0, S // BLK, place, 0, unroll=4)

    q_like = pl.BlockSpec((2, CHUNK_ROWS, LANES), lambda j, c: (j, c, 0))
    pieces = pl.BlockSpec((1, CHUNK_ROWS, LANES), lambda j, c: (j, c, 0))
    kv = pl.BlockSpec((1, S, LANES), lambda j, c: (j // 2, 0, 0))
    per_kv = pl.BlockSpec((1, S, LANES), lambda j, c: (j, 0, 0))
    bias_spec = pl.BlockSpec((2, 1, GQ * BLK, 2 * BLK), lambda j, c: (0, j, 0, 0))
    return pl.pallas_call(
        body, grid=(NKV, S // CHUNK_ROWS), name=f"attn_bwd_d{d}",
        in_specs=[q_like, q_like, pieces, pieces, kv, kv, bias_spec],
        out_specs=[q_like, per_kv],
        out_shape=[_slabs(D // LANES), _slabs(NKV)],
        scratch_shapes=[pltpu.VMEM((S, LANES), F32)],
        compiler_params=_params(("arbitrary", "arbitrary")),
    )(q, do, lse, delta, k, v, bias)


CONV_T = 256


def _halo_before(i):
    return (jnp.maximum(i * (CONV_T // HALO) - 1, 0), 0)


def _halo_after(i):
    return (jnp.minimum((i + 1) * (CONV_T // HALO), S // HALO - 1), 0)


SUBLANES = 8
NCH = D // LANES
GROUP = SUBLANES * SUBLANES


def _comb(ref, cb, base):
    return ref[cb, pl.ds(base, SUBLANES, stride=SUBLANES), :]


def _taps(w_ref, cols):
    return [jnp.broadcast_to(w_ref[j:j + 1, cols], (SUBLANES, LANES)) for j in range(CONV_K)]


def _conv_fwd(c_val, c_glu, c_gate, conv_w, conv_b, ln_g, ln_b):
    T = CONV_T

    def body(cv_ref, cg_ref, cvh_ref, cgh_ref, gate_ref, w_ref, b_ref, lg_ref, lb_ref, u_ref, y_ref, win, us):
        i = pl.program_id(0)
        for cb in range(NCH):
            cols = slice(cb * LANES, (cb + 1) * LANES)
            win[cb, HALO:HALO + T, :] = cv_ref[:, cols] * _sigmoid(cg_ref[:, cols])
            win[cb, 0:HALO, :] = jnp.where(i > 0, cvh_ref[:, cols] * _sigmoid(cgh_ref[:, cols]), 0.0)
        for cb in range(NCH):
            cols = slice(cb * LANES, (cb + 1) * LANES)
            taps = _taps(w_ref, cols)
            bias = jnp.broadcast_to(b_ref[:, cols], (SUBLANES, LANES))

            def group(g, carry):
                for b in range(SUBLANES):
                    base = g * GROUP + b
                    acc = bias
                    for j in range(CONV_K):
                        acc = acc + taps[j] * _comb(win, cb, base + (HALO - (CONV_K - 1) + j))
                    us[cb, pl.ds(base, SUBLANES, stride=SUBLANES), :] = acc
                return carry

            lax.fori_loop(0, T // GROUP, group, 0, unroll=2)
        total = us[0]
        for cb in range(1, NCH):
            total = total + us[cb]
        mu = jnp.sum(total, axis=-1, keepdims=True) * (1.0 / D)
        sq = jnp.zeros((T, LANES), F32)
        for cb in range(NCH):
            uc = us[cb] - mu
            sq = sq + uc * uc
        rstd = lax.rsqrt(jnp.sum(sq, axis=-1, keepdims=True) * (1.0 / D) + LN_EPS)
        for cb in range(NCH):
            cols = slice(cb * LANES, (cb + 1) * LANES)
            u = us[cb]
            u_ref[:, cols] = u
            nrm = (u - mu) * rstd * lg_ref[:, cols] + lb_ref[:, cols]
            gate = gate_ref[:, cols]
            y_ref[:, cols] = (nrm * _sigmoid(nrm) * (gate * _sigmoid(gate))).astype(BF16)

    halo = pl.BlockSpec((HALO, D), _halo_before)
    return pl.pallas_call(
        body, grid=(S // T,), name="conv_fwd",
        in_specs=[_rows(T, D), _rows(T, D), halo, halo, _rows(T, D),
                  _resident((HALO, D)), _resident((1, D)), _resident((1, D)), _resident((1, D))],
        out_specs=[_rows(T, D), _rows(T, D)],
        out_shape=[jax.ShapeDtypeStruct((S, D), F32), jax.ShapeDtypeStruct((S, D), BF16)],
        scratch_shapes=[pltpu.VMEM((NCH, T + HALO, LANES), F32), pltpu.VMEM((NCH, T, LANES), F32)],
        compiler_params=_params(("arbitrary",)),
    )(c_val, c_glu, c_val, c_glu, c_gate, conv_w, conv_b, ln_g, ln_b)


def _conv_bwd_taps(du, c_val, c_glu, conv_w, finished):
    T = CONV_T
    last = S // T - 1

    def body(du_ref, dua_ref, cv_ref, cg_ref, cvh_ref, cgh_ref, w_ref, mine_ref, dcv_ref, dcg_ref, dw_ref, theirs_ref,
             hwin, dwin, dhs, dw_acc, send, recv):
        i = pl.program_id(0)
        copies = _halves_copies([mine_ref], [theirs_ref], send, recv)
        _start_exchange(copies, i == 0)

        @pl.when(i == 0)
        def _():
            dw_acc[...] = jnp.zeros_like(dw_acc)

        for cb in range(NCH):
            cols = slice(cb * LANES, (cb + 1) * LANES)
            hwin[cb, HALO:HALO + T, :] = cv_ref[:, cols] * _sigmoid(cg_ref[:, cols])
            hwin[cb, 0:HALO, :] = jnp.where(i > 0, cvh_ref[:, cols] * _sigmoid(cgh_ref[:, cols]), 0.0)
            dwin[cb, 0:T, :] = du_ref[:, cols]
            dwin[cb, T:T + HALO, :] = jnp.where(i < last, dua_ref[:, cols], 0.0)
        for cb in range(NCH):
            cols = slice(cb * LANES, (cb + 1) * LANES)
            taps = _taps(w_ref, cols)

            def group_dh(g, carry):
                for b in range(SUBLANES):
                    base = g * GROUP + b
                    acc = jnp.zeros((SUBLANES, LANES), F32)
                    for j in range(CONV_K):
                        acc = acc + taps[j] * _comb(dwin, cb, base + (CONV_K - 1 - j))
                    dhs[cb, pl.ds(base, SUBLANES, stride=SUBLANES), :] = acc
                return carry

            lax.fori_loop(0, T // GROUP, group_dh, 0, unroll=2)

            def group_dw(g, sums):
                for b in range(SUBLANES):
                    base = g * GROUP + b
                    d = _comb(dwin, cb, base)
                    sums = tuple(sums[j] + d * _comb(hwin, cb, base + (HALO - (CONV_K - 1) + j))
                                 for j in range(CONV_K))
                return sums

            sums = lax.fori_loop(0, T // GROUP, group_dw, tuple(dw_acc[j, :, cols] for j in range(CONV_K)))
            for j in range(CONV_K):
                dw_acc[j, :, cols] = sums[j]
            dh = dhs[cb]
            cv, sg = cv_ref[:, cols], _sigmoid(cg_ref[:, cols])
            dcv_ref[:, cols] = (dh * sg).astype(BF16)
            dcg_ref[:, cols] = (dh * cv * (sg * (1.0 - sg))).astype(BF16)

        @pl.when(i == last)
        def _():
            dw_ref[...] = jnp.zeros_like(dw_ref)
            for j in range(CONV_K):
                dw_ref[j:j + 1, :] = jnp.sum(dw_acc[j], axis=0, keepdims=True)

        _finish_exchange(copies, i == last)

    before = pl.BlockSpec((HALO, D), _halo_before)
    after = pl.BlockSpec((HALO, D), _halo_after)
    big = jax.ShapeDtypeStruct((S, D), BF16)
    return pl.pallas_call(
        body, grid=(S // T,), name="conv_bwd_taps",
        in_specs=[_rows(T, D), after, _rows(T, D), _rows(T, D), before, before, _resident((HALO, D)), ANY],
        out_specs=[_rows(T, D), _rows(T, D), pl.BlockSpec((HALO, D), lambda i: (0, 0)), ANY],
        out_shape=[big, big, jax.ShapeDtypeStruct((HALO, D), F32), _half_shape(finished)],
        scratch_shapes=[pltpu.VMEM((NCH, T + HALO, LANES), F32), pltpu.VMEM((NCH, T + HALO, LANES), F32),
                        pltpu.VMEM((NCH, T, LANES), F32), pltpu.VMEM((CONV_K, SUBLANES, D), F32)] + _halves_sems(1),
        compiler_params=_params(("arbitrary",)),
    )(du, du, c_val, c_glu, c_val, c_glu, conv_w, finished)


def _outproj_loss(y_att, y_conv, w_out_bf, x, target, gf, u, c_gate, ln_g, ln_b):
    tm = 256

    def body(ya_ref, yc_ref, w_ref, x_ref, t_ref, gf_ref, u_ref, gate_ref, lg_ref, lb_ref,
             dx2_ref, dya_ref, du_ref, dgate_ref, dw_ref, st_ref, acc):
        @pl.when(pl.program_id(0) == 0)
        def _():
            acc[...] = jnp.zeros_like(acc)
            st_ref[...] = jnp.zeros_like(st_ref)

        ya, yc = ya_ref[...], yc_ref[...]
        x2 = x_ref[...] + _dot(ya, w_ref[0:D, :]) + _dot(yc, w_ref[D:2 * D, :])
        r = lax.rsqrt(jnp.mean(x2 * x2, axis=-1, keepdims=True) + NORM_EPS)
        xn = x2 * r
        err = xn * gf_ref[...] - t_ref[...]
        dout = err * (1.0 / D)
        dxn = dout * gf_ref[...]
        dx2 = r * (dxn - xn * jnp.mean(dxn * xn, axis=-1, keepdims=True))
        dx2_ref[...] = dx2
        dx2b = dx2.astype(BF16)
        dya_ref[...] = _dot_nt(dx2b, w_ref[0:D, :])
        dy = _dot_nt(dx2b, w_ref[D:2 * D, :])
        acc[0:D, :] += _dot_tn(ya, dx2b)
        acc[D:2 * D, :] += _dot_tn(yc, dx2b)
        st_ref[ROW_FINAL_G:ROW_FINAL_G + 1, :] += jnp.sum(dout * xn, axis=0, keepdims=True)
        st_ref[ROW_LOSS:ROW_LOSS + 1, :] += jnp.sum(err * err, axis=0, keepdims=True) * (0.5 / D)

        u, gate = u_ref[...], gate_ref[...]
        mu = jnp.mean(u, axis=-1, keepdims=True)
        uc = u - mu
        rstd = lax.rsqrt(jnp.mean(uc * uc, axis=-1, keepdims=True) + LN_EPS)
        z = uc * rstd
        nrm = z * lg_ref[...] + lb_ref[...]
        sn, sg = _sigmoid(nrm), _sigmoid(gate)
        dgate_ref[...] = (dy * (nrm * sn) * (sg * (1.0 + gate * (1.0 - sg)))).astype(BF16)
        dn = dy * (gate * sg) * (sn * (1.0 + nrm * (1.0 - sn)))
        dz = dn * lg_ref[...]
        du = rstd * (dz - jnp.mean(dz, axis=-1, keepdims=True) - z * jnp.mean(dz * z, axis=-1, keepdims=True))
        du_ref[...] = du
        st_ref[ROW_LN_G:ROW_LN_G + 1, :] += jnp.sum(dn * z, axis=0, keepdims=True)
        st_ref[ROW_LN_B:ROW_LN_B + 1, :] += jnp.sum(dn, axis=0, keepdims=True)
        st_ref[ROW_CONV_B:ROW_CONV_B + 1, :] += jnp.sum(du, axis=0, keepdims=True)

        @pl.when(pl.program_id(0) == S // tm - 1)
        def _():
            dw_ref[...] = acc[...].astype(BF16)

    big = jax.ShapeDtypeStruct((S, D), F32)
    vec = _resident((1, D))
    return pl.pallas_call(
        body, grid=(S // tm,), name="outproj_loss",
        in_specs=[_rows(tm, D), _rows(tm, D), _resident((WOUT_ROWS, D)), _rows(tm, D), _rows(tm, D), vec,
                  _rows(tm, D), _rows(tm, D), vec, vec],
        out_specs=[_rows(tm, D), _rows(tm, D), _rows(tm, D), _rows(tm, D),
                   pl.BlockSpec((WOUT_ROWS, D), lambda i: (0, 0)), pl.BlockSpec((8, D), lambda i: (0, 0))],
        out_shape=[big, big, big, jax.ShapeDtypeStruct((S, D), BF16),
                   jax.ShapeDtypeStruct((WOUT_ROWS, D), BF16), jax.ShapeDtypeStruct((8, D), F32)],
        scratch_shapes=[pltpu.VMEM((WOUT_ROWS, D), F32)],
        compiler_params=_params(("arbitrary",)),
    )(y_att, y_conv, w_out_bf, x, target, gf, u, c_gate, ln_g, ln_b)


UNITS_PER_CHUNK = CHUNK // LANES


def _dproj_unit(u, dqs, dkvs, gates, rows):
    if u < OFF_K // LANES:
        return ((dqs[0][u] + dqs[1][u] + dqs[2][u]) * (HD ** -0.5)).astype(BF16)
    if u < OFF_AG // LANES:
        w = u - OFF_K // LANES
        ta, tb = (dkvs[0][j] + dkvs[1][j] + dkvs[2][j] for j in (2 * (w % 2), 2 * (w % 2) + 1))
        low = _low_lanes(rows)
        if w < 2:
            return jnp.where(low, ta, pltpu.roll(tb, HD, axis=1)).astype(BF16)
        return jnp.where(low, pltpu.roll(ta, HD, axis=1), tb).astype(BF16)
    g, sl = divmod(u - OFF_AG // LANES, D // LANES)
    return gates[g][:, sl * LANES:(sl + 1) * LANES]


def _dproj_sources(units, dqs, dkvs, gates, rows):
    use_q = any(u < OFF_K // LANES for u in units)
    use_kv = any(OFF_K // LANES <= u < OFF_AG // LANES for u in units)
    use_g = sorted({(u - OFF_AG // LANES) // (D // LANES) for u in units if u >= OFF_AG // LANES})
    args = (list(dqs) if use_q else []) + (list(dkvs) if use_kv else []) + [gates[g] for g in use_g]
    specs = ([_slab_rows(D // LANES, rows)] * 3 if use_q else []) + ([_slab_rows(NKV, rows)] * 3 if use_kv else []) \
        + [_rows(rows, D)] * len(use_g)

    def pick(refs):
        refs = list(refs)
        q_refs = [refs.pop(0) for _ in range(3)] if use_q else None
        kv_refs = [refs.pop(0) for _ in range(3)] if use_kv else None
        return q_refs, kv_refs, {g: refs.pop(0) for g in use_g}

    return args, specs, pick


def _exchange_results_of(chip_sums):
    n = len(chip_sums) * len(CHIP_FLIPS)
    shapes = [jax.ShapeDtypeStruct((NCHIP,) + tuple(a.shape[1:] if a.ndim == 3 else a.shape), a.dtype)
              for a in chip_sums]
    return shapes, [pltpu.SemaphoreType.DMA((n,)), pltpu.SemaphoreType.DMA((n,))]


def _start_exchange(copies, first_step):
    @pl.when(first_step)
    def _():
        for out, _ in copies:
            out.start()


def _finish_exchange(copies, last_step):
    @pl.when(last_step)
    def _():
        for _, arrival in copies:
            arrival.wait_recv()
        for out, _ in copies:
            out.wait_send()


def _inproj_bwd_x(dqs, dkvs, gates, w_bf, x, g1, dx2, chip_sums):
    tm = 256
    last = S // tm - 1
    units = range(NCOL // LANES)
    pieces, piece_specs, pick = _dproj_sources(units, dqs, dkvs, gates, tm)
    landing, sems = _exchange_results_of(chip_sums)
    n_sums = len(chip_sums)

    def body(*refs):
        piece_refs, refs = refs[:len(pieces)], refs[len(pieces):]
        (w_ref, x_ref, g_ref, dx2_ref), sums, refs = refs[:4], refs[4:4 + n_sums], refs[4 + n_sums:]
        (gx_ref, st_ref), landed, (dp_ref, send, recv) = refs[:2], refs[2:2 + n_sums], refs[2 + n_sums:]
        i = pl.program_id(0)
        copies = _chip_exchange_copies(sums, landed, send, recv)
        _start_exchange(copies, i == 0)

        @pl.when(i == 0)
        def _():
            st_ref[...] = jnp.zeros_like(st_ref)

        sources = pick(piece_refs)
        for u in units:
            dp_ref[:, u * LANES:(u + 1) * LANES] = _dproj_unit(u, *sources, tm)
        dh = _dot_nt(dp_ref[...], w_ref[...])
        xt = x_ref[...]
        r = lax.rsqrt(jnp.mean(xt * xt, axis=-1, keepdims=True) + NORM_EPS)
        xn = xt * r
        dxn = dh * g_ref[...]
        gx_ref[...] = dx2_ref[...] + r * (dxn - xn * jnp.mean(dxn * xn, axis=-1, keepdims=True))
        st_ref[0:1, :] += jnp.sum(dh * xn, axis=0, keepdims=True)
        _finish_exchange(copies, i == last)

    return pl.pallas_call(
        body, grid=(S // tm,), name="inproj_bwd_x",
        in_specs=piece_specs + [_resident((D, NCOL)), _rows(tm, D), _resident((1, D)), _rows(tm, D)] + [ANY] * n_sums,
        out_specs=[_rows(tm, D), pl.BlockSpec((8, D), lambda i: (0, 0))] + [ANY] * n_sums,
        out_shape=[jax.ShapeDtypeStruct((S, D), F32), jax.ShapeDtypeStruct((8, D), F32)] + landing,
        scratch_shapes=[pltpu.VMEM((tm, NCOL), BF16)] + sems,
        compiler_params=_params(("arbitrary",)),
    )(*pieces, w_bf, x, g1, dx2, *chip_sums)


def _inproj_bwd_w(h, dqs, dkvs, gates):
    whole = jax.ShapeDtypeStruct((NCHIP, D, CHUNK), BF16)
    out = None
    for k in range(NCHIP):
        units = range(k * UNITS_PER_CHUNK, (k + 1) * UNITS_PER_CHUNK)
        tk = 512 if units[0] < OFF_K // LANES else 1024
        nk = S // tk
        pieces, piece_specs, pick = _dproj_sources(units, dqs, dkvs, gates, tk)
        handed_on = [] if out is None else [out]
        carries = k == NCHIP - 1

        def body(*refs, units=units, pick=pick, n_pieces=len(pieces), n_in=1 + len(pieces) + len(handed_on),
                 carries=carries, nk=nk, tk=tk):
            h_ref, piece_refs = refs[0], refs[1:1 + n_pieces]
            i = pl.program_id(0)
            if carries:
                o_ref, theirs_ref, tile, acc, send, recv = refs[n_in:]
                copies = _halves_copies([refs[n_in - 1]], [theirs_ref], send, recv, [(0, NCHIP - 1)])
                _start_exchange(copies, i == 0)
            else:
                o_ref, tile, acc = refs[n_in:]

            @pl.when(i == 0)
            def _():
                acc[...] = jnp.zeros_like(acc)

            sources = pick(piece_refs)
            for n, u in enumerate(units):
                tile[:, n * LANES:(n + 1) * LANES] = _dproj_unit(u, *sources, tk)
            acc[...] += _dot_tn(h_ref[...], tile[...])

            @pl.when(i == nk - 1)
            def _():
                o_ref[0] = acc[...].astype(BF16)

            if carries:
                _finish_exchange(copies, i == nk - 1)

        out = pl.pallas_call(
            body, grid=(nk,), name=f"inproj_bwd_w{k}",
            in_specs=[_rows(tk, D)] + piece_specs + [ANY] * len(handed_on),
            out_specs=[pl.BlockSpec((1, D, CHUNK), lambda i, k=k: (k, 0, 0))] + [ANY] * carries,
            out_shape=[whole] + [_half_shape(whole)] * carries,
            input_output_aliases={1 + len(pieces): 0} if handed_on else {},
            scratch_shapes=[pltpu.VMEM((tk, CHUNK), BF16), pltpu.VMEM((D, CHUNK), F32)] + _halves_sems(1) * carries,
            compiler_params=_params(("arbitrary",)),
        )(h, *pieces, *handed_on)
        if not carries:
            (out,) = out
    return out


ROW_FINAL_G, ROW_LOSS, ROW_LN_G, ROW_LN_B, ROW_CONV_B, ROW_TAPS = 0, 1, 2, 3, 4, 8
SMALL_ROWS = 8 + HALO
NDEV = 8


MESH = pl.DeviceIdType.MESH
ANY = pl.BlockSpec(memory_space=pl.ANY)
CHIP_FLIPS = ((1, 0), (0, 1), (1, 1))


def _pos():
    return lax.axis_index("x"), lax.axis_index("y"), lax.axis_index("c")


def _flip(v, f):
    return 1 - v if f else v


def _ds(start, size, align=None):
    return pl.ds(pl.multiple_of(start, align or size), size)


def _place_shards(wi, wo, cw, where):
    steps = 4

    def body(where_ref, wi_ref, wo_ref, cw_ref, wi_full, wo_full, cw_full):
        wi_full[...] = wi_ref[...].astype(BF16)
        wo_full[...] = wo_ref[...].astype(BF16)
        cw_full[...] = cw_ref[...]

    grid_spec = pltpu.PrefetchScalarGridSpec(
        num_scalar_prefetch=1, grid=(steps,),
        in_specs=[pl.BlockSpec((D // steps, CHUNK), lambda i, w: (i, 0)),
                  pl.BlockSpec((WOUT_SHARD // steps, D), lambda i, w: (i, 0)),
                  pl.BlockSpec((HALO, CONVW_SHARD), lambda i, w: (0, 0))],
        out_specs=[pl.BlockSpec((D // steps, CHUNK), lambda i, w: (i, w[0])),
                   pl.BlockSpec((WOUT_SHARD // steps, D), lambda i, w: (w[0] * steps + i, 0)),
                   pl.BlockSpec((HALO, CONVW_SHARD), lambda i, w: (0, w[0]))])
    return pl.pallas_call(
        body, grid_spec=grid_spec, name="place_shards",
        out_shape=[jax.ShapeDtypeStruct((D, NCOL), BF16), jax.ShapeDtypeStruct((WOUT_ROWS, D), BF16),
                   jax.ShapeDtypeStruct((HALO, D), F32)],
        compiler_params=_params(("arbitrary",)),
    )(where, wi, wo, cw)


W_IN, W_OUT, TAPS = range(3)
GATHER_SEMS = 12


def _gather_stages(fulls, send, recv):
    halves = {W_IN: D // 2, W_OUT: WOUT_SHARD // 2, TAPS: HALO // 2}
    x, y, c = _pos()
    chips = {"me": (x, y), "x": (1 - x, y), "y": (x, 1 - y), "diag": (1 - x, 1 - y)}
    SENT = ((("me", 0), "x"), (("me", 1), "x"), (("me", 1), "y"), (("me", 0), "y"), (("x", 0), "y"), (("y", 1), "x"))
    LANDS = ((("x", 0), "x"), (("x", 1), "x"), (("y", 1), "y"), (("y", 0), "y"), (("diag", 0), "y"), (("diag", 1), "x"))
    N_ICI = len(SENT)

    def region(n_th, whose, half, part):
        a, full = fulls[n_th]
        chip = 2 * chips[whose][0] + chips[whose][1]
        n = halves[a] // 2
        row = half * halves[a] + part * n
        if a == W_IN:
            return full.at[_ds(row, n), _ds(chip * CHUNK, CHUNK, 128)]
        if a == W_OUT:
            return full.at[_ds(chip * WOUT_SHARD + row, n), :]
        return full.at[_ds(row, n), _ds(chip * CONVW_SHARD, CONVW_SHARD, 128)]

    def copy(n_th, kind, piece, dev):
        k = GATHER_SEMS * n_th + kind
        return pltpu.make_async_remote_copy(src_ref=piece, dst_ref=piece, send_sem=send.at[k], recv_sem=recv.at[k],
                                            device_id=dev, device_id_type=MESH)

    def sent(a, k):
        if k < N_ICI:
            (whose, part), to = SENT[k]
            return copy(a, k, region(a, whose, c, part), (*chips[to], c))
        (whose, part), _ = LANDS[k - N_ICI]
        return copy(a, k, region(a, whose, c, part), (x, y, 1 - c))

    def wait_arrival(a, k):
        if k < N_ICI:
            (whose, part), frm = LANDS[k]
            copy(a, k, region(a, whose, c, part), (*chips[frm], c)).wait_recv()
        else:
            (whose, part), _ = LANDS[k - N_ICI]
            copy(a, k, region(a, whose, 1 - c, part), (x, y, 1 - c)).wait_recv()

    arrays = range(len(fulls))

    def own_to_neighbours():
        for a in arrays:
            for k in (0, 2, 1, 3):
                sent(a, k).start()

    def pass_on_neighbours():
        for a in arrays:
            for k, onward in ((0, 4), (2, 5), (1, None), (3, None)):
                wait_arrival(a, k)
                if onward is not None:
                    sent(a, onward).start()
                sent(a, k + N_ICI).start()

    def pass_on_diagonal():
        for a in arrays:
            for k in (4, 5):
                wait_arrival(a, k)
                sent(a, k + N_ICI).start()

    def finish():
        for a in arrays:
            for k in range(N_ICI, 2 * N_ICI):
                wait_arrival(a, k)
            for k in range(2 * N_ICI):
                sent(a, k).wait_send()

    return own_to_neighbours, pass_on_neighbours, pass_on_diagonal, finish


def _gather_sems(n_arrays):
    return [pltpu.SemaphoreType.DMA((GATHER_SEMS * n_arrays,)), pltpu.SemaphoreType.DMA((GATHER_SEMS * n_arrays,))]


def _gather_w_in(wi_full):
    def body(_wi, full, send, recv):
        for stage in _gather_stages([(W_IN, full)], send, recv):
            stage()

    return pl.pallas_call(
        body, name="gather_w_in", in_specs=[ANY], out_specs=ANY, input_output_aliases={0: 0},
        out_shape=jax.ShapeDtypeStruct((D, NCOL), BF16), scratch_shapes=_gather_sems(1),
    )(wi_full)


def _half_shape(a):
    return jax.ShapeDtypeStruct((NCHIP, a.shape[1] // 2, a.shape[2]) if a.ndim == 3 else a.shape, a.dtype)


def _halves_sems(n):
    return [pltpu.SemaphoreType.DMA((n,)), pltpu.SemaphoreType.DMA((n,))]


def _halves_copies(srcs, dsts, send, recv, chunks=None):
    x, y, c = _pos()
    pairs = []
    for k, (s_, d_) in enumerate(zip(srcs, dsts)):
        if len(s_.shape) == 3:
            h = s_.shape[1] // 2
            first, count = chunks[k] if chunks else (0, NCHIP)
            s_ = s_.at[pl.ds(first, count), _ds((1 - c) * h, h), :]
            d_ = d_.at[pl.ds(first, count)]
        cp = pltpu.make_async_remote_copy(src_ref=s_, dst_ref=d_, send_sem=send.at[k], recv_sem=recv.at[k],
                                          device_id=(x, y, 1 - c), device_id_type=MESH)
        pairs.append((cp, cp))
    return pairs


def _exchange_halves(arrays, name, started=()):
    n, n_started = len(arrays), len(started)
    chunks = [(NCHIP - 1, 1) if k < n_started else (0, NCHIP) for k in range(n)]

    def body(*refs):
        srcs, dsts, (send, recv) = refs[:n], refs[n + n_started:2 * n + n_started], refs[2 * n + n_started:]
        copies = _halves_copies(srcs, dsts, send, recv, chunks)
        for out, _ in copies:
            out.start()
        for out, _ in copies:
            out.wait()

    return pl.pallas_call(
        body, name=name, in_specs=[ANY] * (n + n_started), out_specs=[ANY] * n,
        out_shape=[_half_shape(a) for a in arrays], input_output_aliases={n + k: k for k in range(n_started)},
        scratch_shapes=_halves_sems(n),
    )(*arrays, *started)


def _add_halves(arrays, received, name):
    n = len(arrays)

    def body(*refs):
        mine, theirs, outs = refs[:n], refs[n:2 * n], refs[2 * n:]
        c = lax.axis_index("c")
        for m_, t_, o_ in zip(mine, theirs, outs):
            if len(m_.shape) == 3:
                h = m_.shape[1] // 2
                o_[0] = (m_[0, _ds(c * h, h), :].astype(F32) + t_[0].astype(F32)).astype(o_.dtype)
            else:
                o_[...] = m_[...] + t_[...]

    def spec(shape):
        if len(shape) == 3:
            return pl.BlockSpec((1,) + tuple(shape[1:]), lambda k: (k, 0, 0))
        return pl.BlockSpec(tuple(shape), lambda k: (0, 0))

    halves = [_half_shape(a) for a in arrays]
    return pl.pallas_call(
        body, grid=(NCHIP,), name=name,
        in_specs=[spec(a.shape) for a in arrays] + [spec(h.shape) for h in halves],
        out_specs=[spec(h.shape) for h in halves], out_shape=halves,
        compiler_params=_params(("arbitrary",)),
    )(*arrays, *received)


def _chip_exchange_copies(srcs, dsts, send, recv):
    x, y, c = _pos()
    me = 2 * x + y
    pairs = []
    for a in range(len(srcs)):
        for j, (fx, fy) in enumerate(CHIP_FLIPS):
            px, py = _flip(x, fx), _flip(y, fy)
            peer = 2 * px + py
            k = len(CHIP_FLIPS) * a + j
            out = pltpu.make_async_remote_copy(
                src_ref=srcs[a].at[peer] if len(srcs[a].shape) == 3 else srcs[a], dst_ref=dsts[a].at[me],
                send_sem=send.at[k], recv_sem=recv.at[k], device_id=(px, py, c), device_id_type=MESH)
            got = dsts[a].at[peer]
            arrival = pltpu.make_async_remote_copy(
                src_ref=got, dst_ref=got, send_sem=send.at[k], recv_sem=recv.at[k],
                device_id=(px, py, c), device_id_type=MESH)
            pairs.append((out, arrival))
    return pairs


def _sum_chips(ri, ro, rs, pi, po, ps, where):
    def body(w_ref, ri_ref, ro_ref, rs_ref, pi_ref, po_ref, ps_ref, gi_ref, go_ref, gs_ref, g5_ref, loss_ref,
             acc_i, acc_o, acc_s):
        k = pl.program_id(0)
        accs = (acc_i, acc_o, acc_s)

        @pl.when(k == 0)
        def _():
            for acc in accs:
                acc[...] = jnp.zeros_like(acc)

        @pl.when(k == w_ref[0])
        def _():
            for acc, val in zip(accs, (pi_ref[0], po_ref[0], ps_ref[...])):
                acc[...] += val.astype(F32)

        @pl.when(k != w_ref[0])
        def _():
            for acc, ref in zip(accs, (ri_ref, ro_ref, rs_ref)):
                acc[...] += ref[0].astype(F32)

        @pl.when(k == NCHIP - 1)
        def _():
            gi_ref[0] = acc_i[...]
            go_ref[0] = acc_o[...]
            gs_ref[...] = acc_s[...]
            g5_ref[...] = jnp.zeros_like(g5_ref)
            for i, row in enumerate((ROW_CONV_B, ROW_LN_G, ROW_LN_B, ROW_FINAL_G)):
                g5_ref[i + 1:i + 2, :] = acc_s[row:row + 1, :]
            loss = jnp.sum(acc_s[ROW_LOSS:ROW_LOSS + 1, :], axis=1, keepdims=True)
            loss_ref[...] = jnp.broadcast_to(loss, loss_ref.shape)

    def sent(k, w):
        return jnp.where(k == w[0], (k + 1) % NCHIP, k)

    hi, ho = D // 2, WOUT_SHARD // 2
    const = lambda shape: pl.BlockSpec(shape, lambda k, w: (0,) * len(shape))
    grid_spec = pltpu.PrefetchScalarGridSpec(
        num_scalar_prefetch=1, grid=(NCHIP,),
        in_specs=[pl.BlockSpec((1, hi, CHUNK), lambda k, w: (sent(k, w), 0, 0)),
                  pl.BlockSpec((1, ho, D), lambda k, w: (sent(k, w), 0, 0)),
                  pl.BlockSpec((1, SMALL_ROWS, D), lambda k, w: (sent(k, w), 0, 0)),
                  pl.BlockSpec((1, hi, CHUNK), lambda k, w: (w[0], 0, 0)),
                  pl.BlockSpec((1, ho, D), lambda k, w: (w[0], 0, 0)),
                  const((SMALL_ROWS, D))],
        out_specs=[pl.BlockSpec((1, hi, CHUNK), lambda k, w: (w[1], 0, 0)),
                   pl.BlockSpec((1, ho, D), lambda k, w: (w[1], 0, 0)),
                   const((SMALL_ROWS, D)), const((8, D)), const((8, LANES))],
        scratch_shapes=[pltpu.VMEM((hi, CHUNK), F32), pltpu.VMEM((ho, D), F32), pltpu.VMEM((SMALL_ROWS, D), F32)])
    return pl.pallas_call(
        body, grid_spec=grid_spec, name="sum_chips",
        out_shape=[jax.ShapeDtypeStruct((2, hi, CHUNK), F32), jax.ShapeDtypeStruct((2, ho, D), F32),
                   jax.ShapeDtypeStruct((SMALL_ROWS, D), F32), jax.ShapeDtypeStruct((8, D), F32),
                   jax.ShapeDtypeStruct((8, LANES), F32)],
        compiler_params=_params(("arbitrary",)),
    )(where, ri, ro, rs, pi, po, ps)


def _exchange_results(gi2, go2, st):
    flips = [(fx, fy, fc) for fx in (0, 1) for fy in (0, 1) for fc in (0, 1)][1:]

    def body(_gi, _go, st_ref, gi_ref, go_ref, all_ref, send, recv, lsem):
        x, y, c = _pos()
        sib = (x, y, 1 - c)

        def half(k, ref, slot):
            return pltpu.make_async_remote_copy(src_ref=ref.at[slot], dst_ref=ref.at[slot], send_sem=send.at[k],
                                                recv_sem=recv.at[k], device_id=sib, device_id_type=MESH)

        def stat(k, src, slot, dev):
            return pltpu.make_async_remote_copy(src_ref=src, dst_ref=all_ref.at[slot], send_sem=send.at[k],
                                                recv_sem=recv.at[k], device_id=dev, device_id_type=MESH)

        mine = pltpu.make_async_copy(st_ref, all_ref.at[4 * x + 2 * y + c], lsem)
        mine.start()
        sends = [half(k, ref, c) for k, ref in enumerate((gi_ref, go_ref))]
        peers = [(_flip(x, fx), _flip(y, fy), _flip(c, fc)) for fx, fy, fc in flips]
        sends += [stat(2 + k, st_ref, 4 * x + 2 * y + c, dev) for k, dev in enumerate(peers)]
        for cp in sends:
            cp.start()
        for k, ref in enumerate((gi_ref, go_ref)):
            half(k, ref, 1 - c).wait_recv()
        for k, (px, py, pc) in enumerate(peers):
            slot = 4 * px + 2 * py + pc
            stat(2 + k, all_ref.at[slot], slot, (px, py, pc)).wait_recv()
        for cp in sends:
            cp.wait_send()
        mine.wait()

    n = 2 + len(flips)
    return pl.pallas_call(
        body, name="exchange_results",
        in_specs=[ANY, ANY, ANY], out_specs=[ANY, ANY, ANY], input_output_aliases={0: 0, 1: 1},
        out_shape=[jax.ShapeDtypeStruct((2, D // 2, CHUNK), F32), jax.ShapeDtypeStruct((2, WOUT_SHARD // 2, D), F32),
                   jax.ShapeDtypeStruct((NDEV, 8, D), F32)],
        scratch_shapes=[pltpu.SemaphoreType.DMA((n,)), pltpu.SemaphoreType.DMA((n,)), pltpu.SemaphoreType.DMA],
    )(gi2, go2, st)


def _adamw_math(w, g, m, v):
    m2 = ADAM_B1 * m + (1.0 - ADAM_B1) * g
    v2 = ADAM_B2 * v + (1.0 - ADAM_B2) * (g * g)
    m_hat = m2 / (1.0 - ADAM_B1 ** ADAM_STEP)
    v_hat = v2 / (1.0 - ADAM_B2 ** ADAM_STEP)
    delta = -ADAM_LR * (m_hat / (jnp.sqrt(v_hat) + ADAM_EPS) + ADAM_WD * w)
    return delta, m2, v2


def _adamw(w, g, m, v, name):
    rows, cols = w.shape
    tm = 256 if rows % 256 == 0 else rows

    def body(w_ref, g_ref, m_ref, v_ref, d_ref, m2_ref, v2_ref):
        d_ref[...], m2_ref[...], v2_ref[...] = _adamw_math(w_ref[...], g_ref[...], m_ref[...], v_ref[...])

    shape = jax.ShapeDtypeStruct(w.shape, F32)
    return pl.pallas_call(
        body, grid=(rows // tm,), name=name,
        in_specs=[_rows(tm, cols)] * 4, out_specs=[_rows(tm, cols)] * 3, out_shape=[shape] * 3,
        compiler_params=_params(("arbitrary",)),
    )(w, g, m, v)


def _adamw_vectors(g5, first_parts, ws, ms, vs):
    n = len(ws)

    def body(g_ref, parts_ref, *refs):
        ins, g0_ref, outs = refs[:3 * n], refs[3 * n], refs[3 * n + 1:]
        g0 = parts_ref[0, 0:1, :]
        for dev in range(1, NDEV):
            g0 = g0 + parts_ref[dev, 0:1, :]
        g0_ref[...] = g0
        for i in range(n):
            g = g0 if i == 0 else g_ref[i:i + 1, :]
            res = _adamw_math(ins[i][...], g, ins[n + i][...], ins[2 * n + i][...])
            for kind in range(3):
                outs[kind * n + i][...] = res[kind]

    shape = jax.ShapeDtypeStruct((1, D), F32)
    return pl.pallas_call(body, name="adamw_vectors", out_shape=[shape] * (1 + 3 * n), compiler_params=_params())(
        g5, first_parts, *ws, *ms, *vs)


def kernel(x, norm_g, w_in, conv_w, conv_b, conv_ln_g, conv_ln_b, w_out, final_norm_g, loss_target, m_norm_g, m_w_in, m_conv_w, m_conv_b, m_conv_ln_g, m_conv_ln_b, m_w_out, m_final_norm_g, v_norm_g, v_w_in, v_conv_w, v_conv_b, v_conv_ln_g, v_conv_ln_b, v_w_out, v_final_norm_g):
    chip = 2 * lax.axis_index("x") + lax.axis_index("y")
    where = jnp.stack([chip, lax.axis_index("c")]).astype(jnp.int32)
    taps_shard = jnp.pad(conv_w[0], ((0, HALO - CONV_K), (0, 0)))
    wi_full, wo_full, cw_full = _place_shards(w_in[0], w_out[0], taps_shard, where)
    wi_full = _gather_w_in(wi_full)

    gf = final_norm_g[None]
    xb = x[0]
    h, q, k, v, a_gate, c_val, c_glu, c_gate, wo_full, cw_full = _inproj_fwd(xb, norm_g, wi_full, wo_full, cw_full)
    tables = [_bias_table(d) for d in PATTERNS]
    o, lse, y_att = _attn_fwd(q, k, v, tables, a_gate)
    u, y_conv = _conv_fwd(c_val, c_glu, c_gate, cw_full, conv_b, conv_ln_g, conv_ln_b)
    dx2, dy_att, du, dc_gate, dw_out, st_out = _outproj_loss(
        y_att, y_conv, wo_full, xb, loss_target[0], gf, u, c_gate, conv_ln_g, conv_ln_b)
    dw_out = dw_out.reshape(NCHIP, WOUT_SHARD, D)
    dc_val, dc_glu, dconv_w, dw_out_theirs = _conv_bwd_taps(du, c_val, c_glu, cw_full, dw_out)
    (po,) = _add_halves([dw_out], [dw_out_theirs], "add_halves_early")
    do, da_gate, delta, ro = _attn_gate_bwd(dy_att, o, a_gate, _head_sum_selectors(), [po])
    dqs, dkvs = zip(*[_attn_bwd(q, k, v, do, lse, delta, t, d) for t, d in zip(tables, PATTERNS)])

    dproj_pieces = (dqs, dkvs, (da_gate, dc_val, dc_glu, dc_gate))
    dw_in, dw_in_theirs = _inproj_bwd_w(h, *dproj_pieces)
    late = [dw_in, jnp.concatenate([st_out, dconv_w], axis=0)]
    pi, ps = _add_halves(late, _exchange_halves(late, "exchange_halves", [dw_in_theirs]), "add_halves")
    grad_x, st_in, ri, rs = _inproj_bwd_x(*dproj_pieces, wi_full, xb, norm_g, dx2, [pi, ps])
    gi2, go2, g_small, g5, loss8 = _sum_chips(ri, ro, rs, pi, po, ps, where)
    gi2, go2, norm_g_parts = _exchange_results(gi2, go2, st_in)
    g_w_in = gi2.reshape(D, CHUNK)
    g_w_out = go2.reshape(WOUT_SHARD, D)
    g_taps = lax.dynamic_slice(g_small, (ROW_TAPS, chip * CONVW_SHARD), (CONV_K, CONVW_SHARD))

    d_w_in, m2_w_in, v2_w_in = _adamw(w_in[0], g_w_in, m_w_in[0], v_w_in[0], "adamw_w_in")
    d_w_out, m2_w_out, v2_w_out = _adamw(w_out[0], g_w_out, m_w_out[0], v_w_out[0], "adamw_w_out")
    d_taps, m2_taps, v2_taps = _adamw(conv_w[0], g_taps, m_conv_w[0], v_conv_w[0], "adamw_conv_w")
    g_norm, *vec = _adamw_vectors(
        g5, norm_g_parts,
        (norm_g, conv_b, conv_ln_g, conv_ln_b, gf),
        (m_norm_g, m_conv_b, m_conv_ln_g, m_conv_ln_b, m_final_norm_g[None]),
        (v_norm_g, v_conv_b, v_conv_ln_g, v_conv_ln_b, v_final_norm_g[None]))
    d_vec, m2_vec, v2_vec = vec[0:5], vec[5:10], vec[10:15]

    def weight_order(ng, wi, cw, cb, lg, lb, wo, fg):
        return (ng, wi[None], cw[None], cb, lg, lb, wo[None], fg[0])

    grads = weight_order(g_norm, g_w_in, g_taps, g5[1:2], g5[2:3], g5[3:4], g_w_out, g5[4:5])
    deltas = weight_order(d_vec[0], d_w_in, d_taps, d_vec[1], d_vec[2], d_vec[3], d_w_out, d_vec[4])
    new_m = weight_order(m2_vec[0], m2_w_in, m2_taps, m2_vec[1], m2_vec[2], m2_vec[3], m2_w_out, m2_vec[4])
    new_v = weight_order(v2_vec[0], v2_w_in, v2_taps, v2_vec[1], v2_vec[2], v2_vec[3], v2_w_out, v2_vec[4])
    return (loss8[0, 0], grad_x[None], *grads, *deltas, *new_m, *new_v)
```

```python
import jax
import jax.numpy as jnp
from jax import lax
from jax.experimental import pallas as pl
from jax.experimental.pallas import tpu as pltpu

F32 = jnp.float32
BF16 = jnp.bfloat16

S = 4096
D = 1024
LANES = 128
HD = 64
NKV = 4
GQ = 4
KVW = NKV * HD
NCOL = 5632
CONV_K = 31
HALO = 32
BLK = 128
PATTERNS = (1, 4, 16)
NORM_EPS = 1e-6
LN_EPS = 1e-5
NEG = -1e30
OFF_Q, OFF_K, OFF_AG, OFF_CV, OFF_CG, OFF_CGATE = 0, 1024, 1536, 2560, 3584, 4608
NCHIP = 4
CHUNK = NCOL // NCHIP
WOUT_ROWS = 2 * D
WOUT_SHARD = WOUT_ROWS // NCHIP
CONVW_SHARD = D // NCHIP

ADAM_LR, ADAM_B1, ADAM_B2, ADAM_EPS, ADAM_WD, ADAM_STEP = 0.001, 0.9, 0.999, 1e-08, 0.01, 10

VMEM_LIMIT = 56 * 1024 * 1024


def _params(sem=None, vmem=VMEM_LIMIT):
    return pltpu.CompilerParams(dimension_semantics=sem, vmem_limit_bytes=vmem)


def _sigmoid(a):
    return 0.5 * jnp.tanh(0.5 * a) + 0.5


def _rows(tm, width):
    return pl.BlockSpec((tm, width), lambda i: (i, 0))


def _slabs(n):
    return jax.ShapeDtypeStruct((n, S, LANES), F32)


def _slab_rows(n, tm):
    return pl.BlockSpec((n, tm, LANES), lambda i: (0, i, 0))


def _resident(shape):
    return pl.BlockSpec(shape, lambda *_: (0,) * len(shape), pipeline_mode=pl.Buffered(1))


def _dot(a, b):
    return jnp.dot(a, b, preferred_element_type=F32)


def _dot_nt(a, b):
    return lax.dot_general(a, b, (((1,), (1,)), ((), ())), preferred_element_type=F32)


def _dot_tn(a, b):
    return lax.dot_general(a, b, (((0,), (0,)), ((), ())), preferred_element_type=F32)


def _inproj_fwd(x, g1, w_bf, wo_full, cw_full):
    tm = 512
    steps = S // tm

    def body(x_ref, g_ref, w_ref, _wo, _cw, h_ref, q_ref, k_ref, v_ref, ag_ref, cv_ref, cg_ref, cgate_ref,
             wo_ref, cw_ref, send, recv):
        i = pl.program_id(0)
        stages = _gather_stages([(W_OUT, wo_ref), (TAPS, cw_ref)], send, recv)
        for stage, step in zip(stages[:3], (0, steps // 2 - 1, steps - 2)):
            pl.when(i == step)(stage)
        xt = x_ref[...]
        r = lax.rsqrt(jnp.mean(xt * xt, axis=-1, keepdims=True) + NORM_EPS)
        h = (xt * r * g_ref[...]).astype(BF16)
        h_ref[...] = h
        q = _dot(h, w_ref[:, OFF_Q:OFF_Q + D]) * (HD ** -0.5)
        kv = _dot(h, w_ref[:, OFF_K:OFF_K + 2 * KVW])
        for sl in range(D // LANES):
            q_ref[sl] = q[:, sl * LANES:(sl + 1) * LANES]
        for sl in range(KVW // LANES):
            k_ref[sl] = kv[:, sl * LANES:(sl + 1) * LANES]
            v_ref[sl] = kv[:, KVW + sl * LANES:KVW + (sl + 1) * LANES]
        ag_ref[...] = _dot(h, w_ref[:, OFF_AG:OFF_AG + D])
        cv_ref[...] = _dot(h, w_ref[:, OFF_CV:OFF_CV + D])
        cg_ref[...] = _dot(h, w_ref[:, OFF_CG:OFF_CG + D])
        cgate_ref[...] = _dot(h, w_ref[:, OFF_CGATE:OFF_CGATE + D])
        pl.when(i == steps - 1)(stages[3])

    big = jax.ShapeDtypeStruct((S, D), F32)
    return pl.pallas_call(
        body, grid=(steps,), name="inproj_fwd",
        in_specs=[_rows(tm, D), _resident((1, D)), _resident((D, NCOL)), ANY, ANY],
        out_specs=[_rows(tm, D), _slab_rows(D // LANES, tm), _slab_rows(KVW // LANES, tm), _slab_rows(KVW // LANES, tm),
                   _rows(tm, D), _rows(tm, D), _rows(tm, D), _rows(tm, D), ANY, ANY],
        out_shape=[jax.ShapeDtypeStruct((S, D), BF16), _slabs(D // LANES), _slabs(KVW // LANES), _slabs(KVW // LANES),
                   big, big, big, big,
                   jax.ShapeDtypeStruct((WOUT_ROWS, D), BF16), jax.ShapeDtypeStruct((HALO, D), F32)],
        input_output_aliases={3: 8, 4: 9},
        scratch_shapes=_gather_sems(2),
        compiler_params=_params(("arbitrary",)),
    )(x, g1, w_bf, wo_full, cw_full)


def _bias_table(d):
    h = jnp.arange(NKV * GQ, dtype=F32)
    slopes = jnp.exp2(-8.0 * (h + 1.0) / (NKV * GQ))
    qi = jnp.arange(BLK)[:, None]
    kj = jnp.arange(2 * BLK)[None, :]
    dist = BLK + qi - kj
    window = (dist >= 0) & (dist <= BLK)
    bias = -slopes[:, None, None] * (dist * d).astype(F32)[None]
    has_prev = jnp.stack([jnp.broadcast_to(kj >= BLK, (BLK, 2 * BLK)), jnp.ones((BLK, 2 * BLK), bool)])
    valid = window[None] & has_prev
    tab = jnp.where(valid[:, None], bias[None], NEG)
    return tab.reshape(2, NKV, GQ * BLK, 2 * BLK)


def _sub_rows(start, d):
    if d == 1:
        return pl.ds(pl.multiple_of(start, BLK), BLK)
    return pl.ds(start, BLK, stride=d)


CHUNK_ROWS = 2048
BLOCKS_PER_CHUNK = CHUNK_ROWS // BLK


def _low_lanes(rows=BLK):
    return lax.broadcasted_iota(jnp.int32, (rows, LANES), 1) < HD


def _block_start(idx, d):
    shift = d.bit_length() - 1
    b, r = lax.shift_right_logical(idx, shift), lax.bitwise_and(idx, d - 1)
    start = b * (BLK * d) + r
    return b, start, jnp.maximum(start - BLK * d, r)


def _stack_heads(ref, rows):
    low = _low_lanes()
    t0, t1 = ref[0, rows, :], ref[1, rows, :]
    return jnp.concatenate([jnp.where(low, t0, 0.0), jnp.where(low, 0.0, t0),
                            jnp.where(low, t1, 0.0), jnp.where(low, 0.0, t1)], axis=0).astype(BF16)


def _unstack_heads(dup):
    low = _low_lanes()
    return (jnp.where(low, dup[0:BLK], dup[BLK:2 * BLK]), jnp.where(low, dup[2 * BLK:3 * BLK], dup[3 * BLK:4 * BLK]))


def _kv_dup(ref, prow, rows, odd):
    t = jnp.concatenate([ref[0, prow, :], ref[0, rows, :]], axis=0)
    swapped = pltpu.roll(t, HD, axis=1)
    keep = jnp.logical_xor(_low_lanes(2 * BLK), odd)
    return jnp.where(keep, t, swapped).astype(BF16)


PIECES = 3


def _by_head(tiles):
    lane = lax.broadcasted_iota(jnp.int32, tiles[0].shape, 1)
    out = tiles[0]
    for g in range(1, GQ):
        out = jnp.where(lax.bitwise_and(lane, GQ - 1) == g, tiles[g], out)
    return out


def _minus_in_pieces(x):
    lane = lax.broadcasted_iota(jnp.int32, x.shape, 1)
    hi = (-x).astype(BF16).astype(F32)
    rest = -x - hi
    mid = rest.astype(BF16).astype(F32)
    lo = (rest - mid).astype(BF16).astype(F32)
    return jnp.where(lane < GQ, hi, jnp.where(lane < 2 * GQ, mid, jnp.where(lane < PIECES * GQ, lo, 0.0)))


def _attn_fwd(q, k, v, tables, a_gate):
    tm = 256
    width = GQ * HD

    lane_out = jnp.arange(LANES)[None, :] // HD
    spread_sel = jnp.stack([jnp.arange(LANES)[:, None] == 2 * half + lane_out for half in range(2)]).astype(BF16)

    def body(q_ref, k_ref, v_ref, b1_ref, b2_ref, b3_ref, ag_ref, sel_ref, o_ref, lse_ref, y_ref, op, lp):
        odd = pl.program_id(0) % 2 == 1
        chunk = pl.program_id(1)
        ones = jnp.ones((2 * BLK, LANES), BF16)

        for pat, (d, b_ref) in enumerate(zip(PATTERNS, (b1_ref, b2_ref, b3_ref))):
            def block(idx, carry, pat=pat, d=d, b_ref=b_ref):
                b, start, pstart = _block_start(chunk * BLOCKS_PER_CHUNK + idx, d)
                rows, prow = _sub_rows(start, d), _sub_rows(pstart, d)
                mine = _sub_rows(start - chunk * CHUNK_ROWS, d)
                qs = _stack_heads(q_ref, mine)
                kw = _kv_dup(k_ref, prow, rows, odd)
                vw = _kv_dup(v_ref, prow, rows, odd)
                s = _dot_nt(qs, kw) + b_ref[jnp.minimum(b, 1), 0]
                m = jnp.max(s, axis=1, keepdims=True)
                p = jnp.exp(s - m).astype(BF16)
                ol = _dot(p, jnp.concatenate([vw, ones], axis=1))
                l = ol[:, LANES:]
                op[pat, 0, mine, :], op[pat, 1, mine, :] = _unstack_heads(ol[:, :LANES] / l)
                lp[pat, mine, :] = _by_head([(m + jnp.log(l))[g * BLK:(g + 1) * BLK] for g in range(GQ)])
                return carry

            lax.fori_loop(0, BLOCKS_PER_CHUNK, block, 0, unroll=2)

        def mix(t, carry):
            r = pl.ds(pl.multiple_of(t * tm, tm), tm)
            a, b, c = lp[0, r, :], lp[1, r, :], lp[2, r, :]
            m = jnp.maximum(jnp.maximum(a, b), c)
            ea, eb, ec = jnp.exp(a - m), jnp.exp(b - m), jnp.exp(c - m)
            den = ea + eb + ec
            lse_ref[0, r, :] = _minus_in_pieces(m + jnp.log(den))
            inv = 1.0 / den
            for half in range(2):
                def spread(w):
                    hi = w.astype(BF16)
                    lo = (w - hi.astype(F32)).astype(BF16)
                    return _dot(hi, sel_ref[half]) + _dot(lo, sel_ref[half])

                o = (spread(ea * inv) * op[0, half, r, :] + spread(eb * inv) * op[1, half, r, :]
                     + spread(ec * inv) * op[2, half, r, :])
                o_ref[half, r, :] = o
                cols = slice(half * LANES, (half + 1) * LANES)
                ag = ag_ref[r, cols]
                y_ref[r, cols] = (o * (ag * _sigmoid(ag))).astype(BF16)
            return carry

        lax.fori_loop(0, CHUNK_ROWS // tm, mix, 0, unroll=2)

    q_like = pl.BlockSpec((2, CHUNK_ROWS, LANES), lambda j, c: (j, c, 0))
    per_kv = pl.BlockSpec((1, CHUNK_ROWS, LANES), lambda j, c: (j, c, 0))
    kv = pl.BlockSpec((1, S, LANES), lambda j, c: (j // 2, 0, 0))
    bias_spec = pl.BlockSpec((2, 1, GQ * BLK, 2 * BLK), lambda j, c: (0, j, 0, 0))
    group_cols = pl.BlockSpec((CHUNK_ROWS, width), lambda j, c: (c, j))
    return pl.pallas_call(
        body, grid=(NKV, S // CHUNK_ROWS), name="attn_fwd",
        in_specs=[q_like, kv, kv, bias_spec, bias_spec, bias_spec, group_cols,
                  pl.BlockSpec((2, LANES, LANES), lambda j, c: (0, 0, 0))],
        out_specs=[q_like, per_kv, group_cols],
        out_shape=[_slabs(D // LANES), _slabs(NKV), jax.ShapeDtypeStruct((S, D), BF16)],
        scratch_shapes=[pltpu.VMEM((len(PATTERNS), 2, CHUNK_ROWS, LANES), F32),
                        pltpu.VMEM((len(PATTERNS), CHUNK_ROWS, LANES), F32)],
        compiler_params=_params(("arbitrary", "arbitrary")),
    )(q, k, v, *tables, a_gate, spread_sel)


def _head_sum_selectors():
    lane_in = jnp.arange(LANES)[:, None] // HD
    return jnp.stack([jnp.broadcast_to(lane_in == h, (LANES, LANES)) for h in range(2)]).astype(BF16)


def _attn_gate_bwd(dy_att, o, a_gate, selectors, chip_sums):
    tm = 256
    last = S // tm - 1
    landing, sems = _exchange_results_of(chip_sums)
    n_sums = len(chip_sums)

    def body(dy_ref, o_ref, ag_ref, e_ref, *refs):
        sums, (do_ref, dag_ref, delta_ref), refs = refs[:n_sums], refs[n_sums:n_sums + 3], refs[n_sums + 3:]
        landed, (send, recv) = refs[:n_sums], refs[n_sums:]
        i = pl.program_id(0)
        copies = _chip_exchange_copies(sums, landed, send, recv)
        _start_exchange(copies, i == 0)
        for j in range(NKV):
            deltas = []
            for sl in (2 * j, 2 * j + 1):
                cols = slice(sl * LANES, (sl + 1) * LANES)
                dy, ag, o_ = dy_ref[:, cols], ag_ref[:, cols], o_ref[sl]
                sg = _sigmoid(ag)
                do = dy * (ag * sg)
                do_ref[sl] = do
                dag_ref[:, cols] = (dy * o_ * (sg * (1.0 + ag * (1.0 - sg)))).astype(BF16)
                prod = do * o_
                hi = prod.astype(BF16)
                lo = (prod - hi.astype(F32)).astype(BF16)
                deltas += [_dot(hi, e_ref[h]) + _dot(lo, e_ref[h]) for h in range(2)]
            delta_ref[j] = _minus_in_pieces(_by_head(deltas))
        _finish_exchange(copies, i == last)

    return pl.pallas_call(
        body, grid=(S // tm,), name="attn_gate_bwd",
        in_specs=[_rows(tm, D), _slab_rows(D // LANES, tm), _rows(tm, D), _resident((2, LANES, LANES))] + [ANY] * n_sums,
        out_specs=[_slab_rows(D // LANES, tm), _rows(tm, D), _slab_rows(NKV, tm)] + [ANY] * n_sums,
        out_shape=[_slabs(D // LANES), jax.ShapeDtypeStruct((S, D), BF16), _slabs(NKV)] + landing,
        scratch_shapes=sems,
        compiler_params=_params(("arbitrary",)),
    )(dy_att, o, a_gate, selectors, *chip_sums)


def _own_pieces(tile):
    lane = lax.broadcasted_iota(jnp.int32, tile.shape, 1)
    head = jnp.where(lane < PIECES * GQ, lax.bitwise_and(lane, GQ - 1), -1)
    return jnp.concatenate([jnp.where(head == g, tile, 0.0) for g in range(GQ)], axis=0).astype(BF16)


def _attn_bwd(q, k, v, do, lse, delta, bias, d):
    def body(q_ref, do_ref, l_ref, dl_ref, k_ref, v_ref, b_ref, dq_ref, dkv_ref, acc):
        odd = pl.program_id(0) % 2 == 1
        chunk = pl.program_id(1)
        ones = (lax.broadcasted_iota(jnp.int32, (2 * BLK, LANES), 1) < PIECES * GQ).astype(BF16)

        def in_acc(block_idx):
            return pl.ds(pl.multiple_of(block_idx * BLK, BLK), BLK)

        @pl.when(chunk == 0)
        def _():
            acc[...] = jnp.zeros_like(acc)

        def block(idx, carry):
            idx = chunk * BLOCKS_PER_CHUNK + idx
            b, start, pstart = _block_start(idx, d)
            rows, prow = _sub_rows(start, d), _sub_rows(pstart, d)
            mine = _sub_rows(start - chunk * CHUNK_ROWS, d)
            qs = _stack_heads(q_ref, mine)
            dos = _stack_heads(do_ref, mine)
            kw = _kv_dup(k_ref, prow, rows, odd)
            vw = _kv_dup(v_ref, prow, rows, odd)
            s = _dot_nt(jnp.concatenate([qs, _own_pieces(l_ref[0, mine, :])], axis=1),
                        jnp.concatenate([kw, ones], axis=1)) + b_ref[jnp.minimum(b, 1), 0]
            p = jnp.exp(s)
            dv2 = _dot_tn(p.astype(BF16), dos)
            dp = _dot_nt(jnp.concatenate([dos, _own_pieces(dl_ref[0, mine, :])], axis=1),
                         jnp.concatenate([vw, ones], axis=1))
            ds = (p * dp).astype(BF16)
            dq_ref[0, mine, :], dq_ref[1, mine, :] = _unstack_heads(_dot(ds, kw))
            dk2 = _dot_tn(ds, qs)
            dkv = jnp.where(_low_lanes(2 * BLK), dk2 + pltpu.roll(dk2, HD, axis=1), dv2 + pltpu.roll(dv2, HD, axis=1))
            acc[in_acc(idx), :] = acc[in_acc(idx), :] + dkv[BLK:]
            before = jnp.where(b >= 1, idx - d, idx)
            acc[in_acc(before), :] = acc[in_acc(before), :] + dkv[:BLK]
            return carry

        lax.fori_loop(0, BLOCKS_PER_CHUNK, block, 0, unroll=16)

        @pl.when(chunk == S // CHUNK_ROWS - 1)
        def _():
            def place(idx, carry):
                _, start, _ = _block_start(idx, d)
                dkv_ref[0, _sub_rows(start, d), :] = acc[in_acc(idx), :]
                return carry

            lax.fori_loop(0, S // BLK, place, 0, unroll=4)

    q_like = pl.BlockSpec((2, CHUNK_ROWS, LANES), lambda j, c: (j, c, 0))
    pieces = pl.BlockSpec((1, CHUNK_ROWS, LANES), lambda j, c: (j, c, 0))
    kv = pl.BlockSpec((1, S, LANES), lambda j, c: (j // 2, 0, 0))
    per_kv = pl.BlockSpec((1, S, LANES), lambda j, c: (j, 0, 0))
    bias_spec = pl.BlockSpec((2, 1, GQ * BLK, 2 * BLK), lambda j, c: (0, j, 0, 0))
    return pl.pallas_call(
        body, grid=(NKV, S // CHUNK_ROWS), name=f"attn_bwd_d{d}",
        in_specs=[q_like, q_like, pieces, pieces, kv, kv, bias_spec],
        out_specs=[q_like, per_kv],
        out_shape=[_slabs(D // LANES), _slabs(NKV)],
        scratch_shapes=[pltpu.VMEM((S, LANES), F32)],
        compiler_params=_params(("arbitrary", "arbitrary")),
    )(q, do, lse, delta, k, v, bias)


CONV_T = 256


def _halo_before(i):
    return (jnp.maximum(i * (CONV_T // HALO) - 1, 0), 0)


def _halo_after(i):
    return (jnp.minimum((i + 1) * (CONV_T // HALO), S // HALO - 1), 0)


SUBLANES = 8
NCH = D // LANES
GROUP = SUBLANES * SUBLANES


def _comb(ref, cb, base):
    return ref[cb, pl.ds(base, SUBLANES, stride=SUBLANES), :]


def _taps(w_ref, cols):
    return [jnp.broadcast_to(w_ref[j:j + 1, cols], (SUBLANES, LANES)) for j in range(CONV_K)]


def _conv_fwd(c_val, c_glu, c_gate, conv_w, conv_b, ln_g, ln_b):
    T = CONV_T

    def body(cv_ref, cg_ref, cvh_ref, cgh_ref, gate_ref, w_ref, b_ref, lg_ref, lb_ref, u_ref, y_ref, win, us):
        i = pl.program_id(0)
        for cb in range(NCH):
            cols = slice(cb * LANES, (cb + 1) * LANES)
            win[cb, HALO:HALO + T, :] = cv_ref[:, cols] * _sigmoid(cg_ref[:, cols])
            win[cb, 0:HALO, :] = jnp.where(i > 0, cvh_ref[:, cols] * _sigmoid(cgh_ref[:, cols]), 0.0)
        for cb in range(NCH):
            cols = slice(cb * LANES, (cb + 1) * LANES)
            taps = _taps(w_ref, cols)
            bias = jnp.broadcast_to(b_ref[:, cols], (SUBLANES, LANES))

            def group(g, carry):
                for b in range(SUBLANES):
                    base = g * GROUP + b
                    acc = bias
                    for j in range(CONV_K):
                        acc = acc + taps[j] * _comb(win, cb, base + (HALO - (CONV_K - 1) + j))
                    us[cb, pl.ds(base, SUBLANES, stride=SUBLANES), :] = acc
                return carry

            lax.fori_loop(0, T // GROUP, group, 0, unroll=2)
        total = us[0]
        for cb in range(1, NCH):
            total = total + us[cb]
        mu = jnp.sum(total, axis=-1, keepdims=True) * (1.0 / D)
        sq = jnp.zeros((T, LANES), F32)
        for cb in range(NCH):
            uc = us[cb] - mu
            sq = sq + uc * uc
        rstd = lax.rsqrt(jnp.sum(sq, axis=-1, keepdims=True) * (1.0 / D) + LN_EPS)
        for cb in range(NCH):
            cols = slice(cb * LANES, (cb + 1) * LANES)
            u = us[cb]
            u_ref[:, cols] = u
            nrm = (u - mu) * rstd * lg_ref[:, cols] + lb_ref[:, cols]
            gate = gate_ref[:, cols]
            y_ref[:, cols] = (nrm * _sigmoid(nrm) * (gate * _sigmoid(gate))).astype(BF16)

    halo = pl.BlockSpec((HALO, D), _halo_before)
    return pl.pallas_call(
        body, grid=(S // T,), name="conv_fwd",
        in_specs=[_rows(T, D), _rows(T, D), halo, halo, _rows(T, D),
                  _resident((HALO, D)), _resident((1, D)), _resident((1, D)), _resident((1, D))],
        out_specs=[_rows(T, D), _rows(T, D)],
        out_shape=[jax.ShapeDtypeStruct((S, D), F32), jax.ShapeDtypeStruct((S, D), BF16)],
        scratch_shapes=[pltpu.VMEM((NCH, T + HALO, LANES), F32), pltpu.VMEM((NCH, T, LANES), F32)],
        compiler_params=_params(("arbitrary",)),
    )(c_val, c_glu, c_val, c_glu, c_gate, conv_w, conv_b, ln_g, ln_b)


def _conv_bwd_taps(du, c_val, c_glu, conv_w):
    T = CONV_T
    last = S // T - 1

    def body(du_ref, dua_ref, cv_ref, cg_ref, cvh_ref, cgh_ref, w_ref, dcv_ref, dcg_ref, dw_ref,
             hwin, dwin, dhs, dw_acc):
        i = pl.program_id(0)

        @pl.when(i == 0)
        def _():
            dw_acc[...] = jnp.zeros_like(dw_acc)

        for cb in range(NCH):
            cols = slice(cb * LANES, (cb + 1) * LANES)
            hwin[cb, HALO:HALO + T, :] = cv_ref[:, cols] * _sigmoid(cg_ref[:, cols])
            hwin[cb, 0:HALO, :] = jnp.where(i > 0, cvh_ref[:, cols] * _sigmoid(cgh_ref[:, cols]), 0.0)
            dwin[cb, 0:T, :] = du_ref[:, cols]
            dwin[cb, T:T + HALO, :] = jnp.where(i < last, dua_ref[:, cols], 0.0)
        for cb in range(NCH):
            cols = slice(cb * LANES, (cb + 1) * LANES)
            taps = _taps(w_ref, cols)

            def group_dh(g, carry):
                for b in range(SUBLANES):
                    base = g * GROUP + b
                    acc = jnp.zeros((SUBLANES, LANES), F32)
                    for j in range(CONV_K):
                        acc = acc + taps[j] * _comb(dwin, cb, base + (CONV_K - 1 - j))
                    dhs[cb, pl.ds(base, SUBLANES, stride=SUBLANES), :] = acc
                return carry

            lax.fori_loop(0, T // GROUP, group_dh, 0, unroll=2)

            def group_dw(g, sums):
                for b in range(SUBLANES):
                    base = g * GROUP + b
                    d = _comb(dwin, cb, base)
                    sums = tuple(sums[j] + d * _comb(hwin, cb, base + (HALO - (CONV_K - 1) + j))
                                 for j in range(CONV_K))
                return sums

            sums = lax.fori_loop(0, T // GROUP, group_dw, tuple(dw_acc[j, :, cols] for j in range(CONV_K)))
            for j in range(CONV_K):
                dw_acc[j, :, cols] = sums[j]
            dh = dhs[cb]
            cv, sg = cv_ref[:, cols], _sigmoid(cg_ref[:, cols])
            dcv_ref[:, cols] = (dh * sg).astype(BF16)
            dcg_ref[:, cols] = (dh * cv * (sg * (1.0 - sg))).astype(BF16)

        @pl.when(i == last)
        def _():
            dw_ref[...] = jnp.zeros_like(dw_ref)
            for j in range(CONV_K):
                dw_ref[j:j + 1, :] = jnp.sum(dw_acc[j], axis=0, keepdims=True)

    before = pl.BlockSpec((HALO, D), _halo_before)
    after = pl.BlockSpec((HALO, D), _halo_after)
    big = jax.ShapeDtypeStruct((S, D), BF16)
    return pl.pallas_call(
        body, grid=(S // T,), name="conv_bwd_taps",
        in_specs=[_rows(T, D), after, _rows(T, D), _rows(T, D), before, before, _resident((HALO, D))],
        out_specs=[_rows(T, D), _rows(T, D), pl.BlockSpec((HALO, D), lambda i: (0, 0))],
        out_shape=[big, big, jax.ShapeDtypeStruct((HALO, D), F32)],
        scratch_shapes=[pltpu.VMEM((NCH, T + HALO, LANES), F32), pltpu.VMEM((NCH, T + HALO, LANES), F32),
                        pltpu.VMEM((NCH, T, LANES), F32), pltpu.VMEM((CONV_K, SUBLANES, D), F32)],
        compiler_params=_params(("arbitrary",)),
    )(du, du, c_val, c_glu, c_val, c_glu, conv_w)


def _outproj_loss(y_att, y_conv, w_out_bf, x, target, gf, u, c_gate, ln_g, ln_b):
    tm = 256

    def body(ya_ref, yc_ref, w_ref, x_ref, t_ref, gf_ref, u_ref, gate_ref, lg_ref, lb_ref,
             dx2_ref, dya_ref, du_ref, dgate_ref, dw_ref, st_ref, acc):
        @pl.when(pl.program_id(0) == 0)
        def _():
            acc[...] = jnp.zeros_like(acc)
            st_ref[...] = jnp.zeros_like(st_ref)

        ya, yc = ya_ref[...], yc_ref[...]
        x2 = x_ref[...] + _dot(ya, w_ref[0:D, :]) + _dot(yc, w_ref[D:2 * D, :])
        r = lax.rsqrt(jnp.mean(x2 * x2, axis=-1, keepdims=True) + NORM_EPS)
        xn = x2 * r
        err = xn * gf_ref[...] - t_ref[...]
        dout = err * (1.0 / D)
        dxn = dout * gf_ref[...]
        dx2 = r * (dxn - xn * jnp.mean(dxn * xn, axis=-1, keepdims=True))
        dx2_ref[...] = dx2
        dx2b = dx2.astype(BF16)
        dya_ref[...] = _dot_nt(dx2b, w_ref[0:D, :])
        dy = _dot_nt(dx2b, w_ref[D:2 * D, :])
        acc[0:D, :] += _dot_tn(ya, dx2b)
        acc[D:2 * D, :] += _dot_tn(yc, dx2b)
        st_ref[ROW_FINAL_G:ROW_FINAL_G + 1, :] += jnp.sum(dout * xn, axis=0, keepdims=True)
        st_ref[ROW_LOSS:ROW_LOSS + 1, :] += jnp.sum(err * err, axis=0, keepdims=True) * (0.5 / D)

        u, gate = u_ref[...], gate_ref[...]
        mu = jnp.mean(u, axis=-1, keepdims=True)
        uc = u - mu
        rstd = lax.rsqrt(jnp.mean(uc * uc, axis=-1, keepdims=True) + LN_EPS)
        z = uc * rstd
        nrm = z * lg_ref[...] + lb_ref[...]
        sn, sg = _sigmoid(nrm), _sigmoid(gate)
        dgate_ref[...] = (dy * (nrm * sn) * (sg * (1.0 + gate * (1.0 - sg)))).astype(BF16)
        dn = dy * (gate * sg) * (sn * (1.0 + nrm * (1.0 - sn)))
        dz = dn * lg_ref[...]
        du = rstd * (dz - jnp.mean(dz, axis=-1, keepdims=True) - z * jnp.mean(dz * z, axis=-1, keepdims=True))
        du_ref[...] = du
        st_ref[ROW_LN_G:ROW_LN_G + 1, :] += jnp.sum(dn * z, axis=0, keepdims=True)
        st_ref[ROW_LN_B:ROW_LN_B + 1, :] += jnp.sum(dn, axis=0, keepdims=True)
        st_ref[ROW_CONV_B:ROW_CONV_B + 1, :] += jnp.sum(du, axis=0, keepdims=True)

        @pl.when(pl.program_id(0) == S // tm - 1)
        def _():
            dw_ref[...] = acc[...].astype(BF16)

    big = jax.ShapeDtypeStruct((S, D), F32)
    vec = _resident((1, D))
    return pl.pallas_call(
        body, grid=(S // tm,), name="outproj_loss",
        in_specs=[_rows(tm, D), _rows(tm, D), _resident((WOUT_ROWS, D)), _rows(tm, D), _rows(tm, D), vec,
                  _rows(tm, D), _rows(tm, D), vec, vec],
        out_specs=[_rows(tm, D), _rows(tm, D), _rows(tm, D), _rows(tm, D),
                   pl.BlockSpec((WOUT_ROWS, D), lambda i: (0, 0)), pl.BlockSpec((8, D), lambda i: (0, 0))],
        out_shape=[big, big, big, jax.ShapeDtypeStruct((S, D), BF16),
                   jax.ShapeDtypeStruct((WOUT_ROWS, D), BF16), jax.ShapeDtypeStruct((8, D), F32)],
        scratch_shapes=[pltpu.VMEM((WOUT_ROWS, D), F32)],
        compiler_params=_params(("arbitrary",)),
    )(y_att, y_conv, w_out_bf, x, target, gf, u, c_gate, ln_g, ln_b)


UNITS_PER_CHUNK = CHUNK // LANES


def _dproj_unit(u, dqs, dkvs, gates, rows):
    if u < OFF_K // LANES:
        return ((dqs[0][u] + dqs[1][u] + dqs[2][u]) * (HD ** -0.5)).astype(BF16)
    if u < OFF_AG // LANES:
        w = u - OFF_K // LANES
        ta, tb = (dkvs[0][j] + dkvs[1][j] + dkvs[2][j] for j in (2 * (w % 2), 2 * (w % 2) + 1))
        low = _low_lanes(rows)
        if w < 2:
            return jnp.where(low, ta, pltpu.roll(tb, HD, axis=1)).astype(BF16)
        return jnp.where(low, pltpu.roll(ta, HD, axis=1), tb).astype(BF16)
    g, sl = divmod(u - OFF_AG // LANES, D // LANES)
    return gates[g][:, sl * LANES:(sl + 1) * LANES]


def _dproj_sources(units, dqs, dkvs, gates, rows):
    use_q = any(u < OFF_K // LANES for u in units)
    use_kv = any(OFF_K // LANES <= u < OFF_AG // LANES for u in units)
    use_g = sorted({(u - OFF_AG // LANES) // (D // LANES) for u in units if u >= OFF_AG // LANES})
    args = (list(dqs) if use_q else []) + (list(dkvs) if use_kv else []) + [gates[g] for g in use_g]
    specs = ([_slab_rows(D // LANES, rows)] * 3 if use_q else []) + ([_slab_rows(NKV, rows)] * 3 if use_kv else []) \
        + [_rows(rows, D)] * len(use_g)

    def pick(refs):
        refs = list(refs)
        q_refs = [refs.pop(0) for _ in range(3)] if use_q else None
        kv_refs = [refs.pop(0) for _ in range(3)] if use_kv else None
        return q_refs, kv_refs, {g: refs.pop(0) for g in use_g}

    return args, specs, pick


def _exchange_results_of(chip_sums):
    n = len(chip_sums) * len(CHIP_FLIPS)
    shapes = [jax.ShapeDtypeStruct((NCHIP,) + tuple(a.shape[1:] if a.ndim == 3 else a.shape), a.dtype)
              for a in chip_sums]
    return shapes, [pltpu.SemaphoreType.DMA((n,)), pltpu.SemaphoreType.DMA((n,))]


def _start_exchange(copies, first_step):
    @pl.when(first_step)
    def _():
        for out, _ in copies:
            out.start()


def _finish_exchange(copies, last_step):
    @pl.when(last_step)
    def _():
        for _, arrival in copies:
            arrival.wait_recv()
        for out, _ in copies:
            out.wait_send()


FORWARD_SEMS = 6


def _forwarded_exchange(pi_ref, ri_ref, passing, mine, summed, send, recv, local):
    x, y, c = _pos()
    me = 2 * x + y
    half_rows = pi_ref.shape[1] // 2
    halves = [pl.ds(n * half_rows, half_rows) for n in range(2)]
    peers = [(_flip(x, 1), y), (x, _flip(y, 1))]
    chunk_of = [2 * px + py for px, py in peers]
    diagonal = 2 * _flip(x, 1) + _flip(y, 1)

    def remote(src, dst, k, peer):
        return pltpu.make_async_remote_copy(src_ref=src, dst_ref=dst, send_sem=send.at[k], recv_sem=recv.at[k],
                                            device_id=(peer[0], peer[1], c), device_id_type=MESH)

    to_pass = [remote(pi_ref.at[diagonal, halves[n]], passing.at[n], n, peers[n]) for n in range(2)]
    direct = [remote(pi_ref.at[chunk_of[n], halves[n]], ri_ref.at[me, halves[n]], 2 + n, peers[n]) for n in range(2)]
    sums = [remote(summed.at[n], ri_ref.at[me, halves[n]], 4 + n, peers[1 - n]) for n in range(2)]
    loads = [pltpu.make_async_copy(pi_ref.at[chunk_of[1 - n], halves[n]], mine.at[n], local.at[n]) for n in range(2)]

    def start():
        for cp in to_pass + direct + loads:
            cp.start()

    def forward():
        for n in range(2):
            to_pass[n].wait_recv()
            loads[n].wait()
            summed[n] = (mine[n].astype(F32) + passing[n].astype(F32)).astype(BF16)
            sums[n].start()

    def finish():
        for cp in direct + sums:
            cp.wait_recv()
        for cp in to_pass + direct + sums:
            cp.wait_send()

    return start, forward, finish


def _inproj_bwd_x(dqs, dkvs, gates, w_bf, x, g1, dx2, pi):
    tm = 256
    last = S // tm - 1
    units = range(NCOL // LANES)
    pieces, piece_specs, pick = _dproj_sources(units, dqs, dkvs, gates, tm)
    landing = jax.ShapeDtypeStruct(pi.shape, pi.dtype)
    staged = pltpu.VMEM((2, pi.shape[1] // 2, pi.shape[2]), pi.dtype)

    def body(*refs):
        piece_refs, refs = refs[:len(pieces)], refs[len(pieces):]
        w_ref, x_ref, g_ref, dx2_ref, pi_ref, gx_ref, st_ref, ri_ref, dp_ref, passing, mine, summed, send, recv, local = refs
        i = pl.program_id(0)
        start, forward, finish = _forwarded_exchange(pi_ref, ri_ref, passing, mine, summed, send, recv, local)
        pl.when(i == 0)(start)
        pl.when(i == last // 2)(forward)

        @pl.when(i == 0)
        def _():
            st_ref[...] = jnp.zeros_like(st_ref)

        sources = pick(piece_refs)
        for u in units:
            dp_ref[:, u * LANES:(u + 1) * LANES] = _dproj_unit(u, *sources, tm)
        dh = _dot_nt(dp_ref[...], w_ref[...])
        xt = x_ref[...]
        r = lax.rsqrt(jnp.mean(xt * xt, axis=-1, keepdims=True) + NORM_EPS)
        xn = xt * r
        dxn = dh * g_ref[...]
        gx_ref[...] = dx2_ref[...] + r * (dxn - xn * jnp.mean(dxn * xn, axis=-1, keepdims=True))
        st_ref[0:1, :] += jnp.sum(dh * xn, axis=0, keepdims=True)
        pl.when(i == last)(finish)

    return pl.pallas_call(
        body, grid=(S // tm,), name="inproj_bwd_x",
        in_specs=piece_specs + [_resident((D, NCOL)), _rows(tm, D), _resident((1, D)), _rows(tm, D), ANY],
        out_specs=[_rows(tm, D), pl.BlockSpec((8, D), lambda i: (0, 0)), ANY],
        out_shape=[jax.ShapeDtypeStruct((S, D), F32), jax.ShapeDtypeStruct((8, D), F32), landing],
        scratch_shapes=[pltpu.VMEM((tm, NCOL), BF16), staged, staged, staged,
                        pltpu.SemaphoreType.DMA((FORWARD_SEMS,)), pltpu.SemaphoreType.DMA((FORWARD_SEMS,)),
                        pltpu.SemaphoreType.DMA((2,))],
        compiler_params=_params(("arbitrary",)),
    )(*pieces, w_bf, x, g1, dx2, pi)


def _inproj_bwd_w(h, dqs, dkvs, gates):
    out = None
    for k in range(NCHIP):
        units = range(k * UNITS_PER_CHUNK, (k + 1) * UNITS_PER_CHUNK)
        tk = 512 if units[0] < OFF_K // LANES else 1024
        nk = S // tk
        pieces, piece_specs, pick = _dproj_sources(units, dqs, dkvs, gates, tk)
        handed_on = [] if out is None else [out]

        def body(*refs, units=units, pick=pick, n_pieces=len(pieces), n_in=1 + len(pieces) + len(handed_on)):
            h_ref, piece_refs = refs[0], refs[1:1 + n_pieces]
            o_ref, tile, acc = refs[n_in:]
            i = pl.program_id(0)

            @pl.when(i == 0)
            def _():
                acc[...] = jnp.zeros_like(acc)

            sources = pick(piece_refs)
            for n, u in enumerate(units):
                tile[:, n * LANES:(n + 1) * LANES] = _dproj_unit(u, *sources, tk)
            acc[...] += _dot_tn(h_ref[...], tile[...])

            @pl.when(i == nk - 1)
            def _():
                o_ref[0] = acc[...].astype(BF16)

        out = pl.pallas_call(
            body, grid=(nk,), name=f"inproj_bwd_w{k}",
            in_specs=[_rows(tk, D)] + piece_specs + [ANY] * len(handed_on),
            out_specs=pl.BlockSpec((1, D, CHUNK), lambda i, k=k: (k, 0, 0)),
            out_shape=jax.ShapeDtypeStruct((NCHIP, D, CHUNK), BF16),
            input_output_aliases={1 + len(pieces): 0} if handed_on else {},
            scratch_shapes=[pltpu.VMEM((tk, CHUNK), BF16), pltpu.VMEM((D, CHUNK), F32)],
            compiler_params=_params(("arbitrary",)),
        )(h, *pieces, *handed_on)
    return out


ROW_FINAL_G, ROW_LOSS, ROW_LN_G, ROW_LN_B, ROW_CONV_B, ROW_TAPS = 0, 1, 2, 3, 4, 8
SMALL_ROWS = 8 + HALO
NDEV = 8


MESH = pl.DeviceIdType.MESH
ANY = pl.BlockSpec(memory_space=pl.ANY)
CHIP_FLIPS = ((1, 0), (0, 1), (1, 1))


def _pos():
    return lax.axis_index("x"), lax.axis_index("y"), lax.axis_index("c")


def _flip(v, f):
    return 1 - v if f else v


def _ds(start, size, align=None):
    return pl.ds(pl.multiple_of(start, align or size), size)


def _place_shards(wi, wo, cw, where):
    steps = 4

    def body(where_ref, wi_ref, wo_ref, cw_ref, wi_full, wo_full, cw_full):
        wi_full[...] = wi_ref[...].astype(BF16)
        wo_full[...] = wo_ref[...].astype(BF16)
        cw_full[...] = cw_ref[...]

    grid_spec = pltpu.PrefetchScalarGridSpec(
        num_scalar_prefetch=1, grid=(steps,),
        in_specs=[pl.BlockSpec((D // steps, CHUNK), lambda i, w: (i, 0)),
                  pl.BlockSpec((WOUT_SHARD // steps, D), lambda i, w: (i, 0)),
                  pl.BlockSpec((HALO, CONVW_SHARD), lambda i, w: (0, 0))],
        out_specs=[pl.BlockSpec((D // steps, CHUNK), lambda i, w: (i, w[0])),
                   pl.BlockSpec((WOUT_SHARD // steps, D), lambda i, w: (w[0] * steps + i, 0)),
                   pl.BlockSpec((HALO, CONVW_SHARD), lambda i, w: (0, w[0]))])
    return pl.pallas_call(
        body, grid_spec=grid_spec, name="place_shards",
        out_shape=[jax.ShapeDtypeStruct((D, NCOL), BF16), jax.ShapeDtypeStruct((WOUT_ROWS, D), BF16),
                   jax.ShapeDtypeStruct((HALO, D), F32)],
        compiler_params=_params(("arbitrary",)),
    )(where, wi, wo, cw)


W_IN, W_OUT, TAPS = range(3)
GATHER_SEMS = 12


def _gather_stages(fulls, send, recv):
    halves = {W_IN: D // 2, W_OUT: WOUT_SHARD // 2, TAPS: HALO // 2}
    x, y, c = _pos()
    chips = {"me": (x, y), "x": (1 - x, y), "y": (x, 1 - y), "diag": (1 - x, 1 - y)}
    SENT = ((("me", 0), "x"), (("me", 1), "x"), (("me", 1), "y"), (("me", 0), "y"), (("x", 0), "y"), (("y", 1), "x"))
    LANDS = ((("x", 0), "x"), (("x", 1), "x"), (("y", 1), "y"), (("y", 0), "y"), (("diag", 0), "y"), (("diag", 1), "x"))
    N_ICI = len(SENT)

    def region(n_th, whose, half, part):
        a, full = fulls[n_th]
        chip = 2 * chips[whose][0] + chips[whose][1]
        n = halves[a] // 2
        row = half * halves[a] + part * n
        if a == W_IN:
            return full.at[_ds(row, n), _ds(chip * CHUNK, CHUNK, 128)]
        if a == W_OUT:
            return full.at[_ds(chip * WOUT_SHARD + row, n), :]
        return full.at[_ds(row, n), _ds(chip * CONVW_SHARD, CONVW_SHARD, 128)]

    def copy(n_th, kind, piece, dev):
        k = GATHER_SEMS * n_th + kind
        return pltpu.make_async_remote_copy(src_ref=piece, dst_ref=piece, send_sem=send.at[k], recv_sem=recv.at[k],
                                            device_id=dev, device_id_type=MESH)

    def sent(a, k):
        if k < N_ICI:
            (whose, part), to = SENT[k]
            return copy(a, k, region(a, whose, c, part), (*chips[to], c))
        (whose, part), _ = LANDS[k - N_ICI]
        return copy(a, k, region(a, whose, c, part), (x, y, 1 - c))

    def wait_arrival(a, k):
        if k < N_ICI:
            (whose, part), frm = LANDS[k]
            copy(a, k, region(a, whose, c, part), (*chips[frm], c)).wait_recv()
        else:
            (whose, part), _ = LANDS[k - N_ICI]
            copy(a, k, region(a, whose, 1 - c, part), (x, y, 1 - c)).wait_recv()

    arrays = range(len(fulls))

    def own_to_neighbours():
        for a in arrays:
            for k in (0, 2, 1, 3):
                sent(a, k).start()

    def pass_on_neighbours():
        for a in arrays:
            for k, onward in ((0, 4), (2, 5), (1, None), (3, None)):
                wait_arrival(a, k)
                if onward is not None:
                    sent(a, onward).start()
                sent(a, k + N_ICI).start()

    def pass_on_diagonal():
        for a in arrays:
            for k in (4, 5):
                wait_arrival(a, k)
                sent(a, k + N_ICI).start()

    def finish():
        for a in arrays:
            for k in range(N_ICI, 2 * N_ICI):
                wait_arrival(a, k)
            for k in range(2 * N_ICI):
                sent(a, k).wait_send()

    return own_to_neighbours, pass_on_neighbours, pass_on_diagonal, finish


def _gather_sems(n_arrays):
    return [pltpu.SemaphoreType.DMA((GATHER_SEMS * n_arrays,)), pltpu.SemaphoreType.DMA((GATHER_SEMS * n_arrays,))]


def _gather_w_in(wi_full):
    def body(_wi, full, send, recv):
        for stage in _gather_stages([(W_IN, full)], send, recv):
            stage()

    return pl.pallas_call(
        body, name="gather_w_in", in_specs=[ANY], out_specs=ANY, input_output_aliases={0: 0},
        out_shape=jax.ShapeDtypeStruct((D, NCOL), BF16), scratch_shapes=_gather_sems(1),
    )(wi_full)


def _half_shape(a):
    return jax.ShapeDtypeStruct((NCHIP, a.shape[1] // 2, a.shape[2]) if a.ndim == 3 else a.shape, a.dtype)


def _exchange_halves(arrays, name):
    n = len(arrays)

    def body(*refs):
        srcs, dsts, (send, recv) = refs[:n], refs[n:2 * n], refs[2 * n:]
        x, y, c = _pos()
        cps = []
        for k, (s_, d_) in enumerate(zip(srcs, dsts)):
            if len(s_.shape) == 3:
                h = s_.shape[1] // 2
                s_ = s_.at[:, _ds((1 - c) * h, h), :]
            cps.append(pltpu.make_async_remote_copy(src_ref=s_, dst_ref=d_, send_sem=send.at[k], recv_sem=recv.at[k],
                                                    device_id=(x, y, 1 - c), device_id_type=MESH))
        for cp in cps:
            cp.start()
        for cp in cps:
            cp.wait()

    return pl.pallas_call(
        body, name=name, in_specs=[ANY] * n, out_specs=[ANY] * n, out_shape=[_half_shape(a) for a in arrays],
        scratch_shapes=[pltpu.SemaphoreType.DMA((n,)), pltpu.SemaphoreType.DMA((n,))],
    )(*arrays)


def _add_halves(arrays, received, name):
    n = len(arrays)

    def body(*refs):
        mine, theirs, outs = refs[:n], refs[n:2 * n], refs[2 * n:]
        c = lax.axis_index("c")
        for m_, t_, o_ in zip(mine, theirs, outs):
            if len(m_.shape) == 3:
                h = m_.shape[1] // 2
                o_[0] = (m_[0, _ds(c * h, h), :].astype(F32) + t_[0].astype(F32)).astype(o_.dtype)
            else:
                o_[...] = m_[...] + t_[...]

    def spec(shape):
        if len(shape) == 3:
            return pl.BlockSpec((1,) + tuple(shape[1:]), lambda k: (k, 0, 0))
        return pl.BlockSpec(tuple(shape), lambda k: (0, 0))

    halves = [_half_shape(a) for a in arrays]
    return pl.pallas_call(
        body, grid=(NCHIP,), name=name,
        in_specs=[spec(a.shape) for a in arrays] + [spec(h.shape) for h in halves],
        out_specs=[spec(h.shape) for h in halves], out_shape=halves,
        compiler_params=_params(("arbitrary",)),
    )(*arrays, *received)


def _chip_exchange_copies(srcs, dsts, send, recv):
    x, y, c = _pos()
    me = 2 * x + y
    pairs = []
    for a in range(len(srcs)):
        for j, (fx, fy) in enumerate(CHIP_FLIPS):
            px, py = _flip(x, fx), _flip(y, fy)
            peer = 2 * px + py
            k = len(CHIP_FLIPS) * a + j
            out = pltpu.make_async_remote_copy(
                src_ref=srcs[a].at[peer] if len(srcs[a].shape) == 3 else srcs[a], dst_ref=dsts[a].at[me],
                send_sem=send.at[k], recv_sem=recv.at[k], device_id=(px, py, c), device_id_type=MESH)
            got = dsts[a].at[peer]
            arrival = pltpu.make_async_remote_copy(
                src_ref=got, dst_ref=got, send_sem=send.at[k], recv_sem=recv.at[k],
                device_id=(px, py, c), device_id_type=MESH)
            pairs.append((out, arrival))
    return pairs


def _sum_chips(ri, ro, rs, pi, po, ps, where):
    def body(w_ref, ri_ref, ro_ref, rs_ref, pi_ref, po_ref, ps_ref, gi_ref, go_ref, gs_ref, g5_ref, loss_ref,
             acc_i, acc_o, acc_s):
        k = pl.program_id(0)
        accs = (acc_i, acc_o, acc_s)

        @pl.when(k == 0)
        def _():
            for acc in accs:
                acc[...] = jnp.zeros_like(acc)

        @pl.when(k == w_ref[0])
        def _():
            for acc, val in zip(accs, (pi_ref[0], po_ref[0], ps_ref[...])):
                acc[...] += val.astype(F32)

        @pl.when(k != w_ref[0])
        def _():
            for acc, ref in zip(accs[1:], (ro_ref, rs_ref)):
                acc[...] += ref[0].astype(F32)

        @pl.when(jnp.logical_and(k != w_ref[0], k != NCHIP - 1 - w_ref[0]))
        def _():
            acc_i[...] += ri_ref[0].astype(F32)

        @pl.when(k == NCHIP - 1)
        def _():
            gi_ref[0] = acc_i[...]
            go_ref[0] = acc_o[...]
            gs_ref[...] = acc_s[...]
            g5_ref[...] = jnp.zeros_like(g5_ref)
            for i, row in enumerate((ROW_CONV_B, ROW_LN_G, ROW_LN_B, ROW_FINAL_G)):
                g5_ref[i + 1:i + 2, :] = acc_s[row:row + 1, :]
            loss = jnp.sum(acc_s[ROW_LOSS:ROW_LOSS + 1, :], axis=1, keepdims=True)
            loss_ref[...] = jnp.broadcast_to(loss, loss_ref.shape)

    def sent(k, w):
        return jnp.where(k == w[0], (k + 1) % NCHIP, k)

    def sent_by_neighbours(k, w):
        return jnp.where(jnp.logical_or(k == w[0], k == NCHIP - 1 - w[0]), lax.bitwise_xor(w[0], 2), k)

    hi, ho = D // 2, WOUT_SHARD // 2
    const = lambda shape: pl.BlockSpec(shape, lambda k, w: (0,) * len(shape))
    grid_spec = pltpu.PrefetchScalarGridSpec(
        num_scalar_prefetch=1, grid=(NCHIP,),
        in_specs=[pl.BlockSpec((1, hi, CHUNK), lambda k, w: (sent_by_neighbours(k, w), 0, 0)),
                  pl.BlockSpec((1, ho, D), lambda k, w: (sent(k, w), 0, 0)),
                  pl.BlockSpec((1, SMALL_ROWS, D), lambda k, w: (sent(k, w), 0, 0)),
                  pl.BlockSpec((1, hi, CHUNK), lambda k, w: (w[0], 0, 0)),
                  pl.BlockSpec((1, ho, D), lambda k, w: (w[0], 0, 0)),
                  const((SMALL_ROWS, D))],
        out_specs=[pl.BlockSpec((1, hi, CHUNK), lambda k, w: (w[1], 0, 0)),
                   pl.BlockSpec((1, ho, D), lambda k, w: (w[1], 0, 0)),
                   const((SMALL_ROWS, D)), const((8, D)), const((8, LANES))],
        scratch_shapes=[pltpu.VMEM((hi, CHUNK), F32), pltpu.VMEM((ho, D), F32), pltpu.VMEM((SMALL_ROWS, D), F32)])
    return pl.pallas_call(
        body, grid_spec=grid_spec, name="sum_chips",
        out_shape=[jax.ShapeDtypeStruct((2, hi, CHUNK), F32), jax.ShapeDtypeStruct((2, ho, D), F32),
                   jax.ShapeDtypeStruct((SMALL_ROWS, D), F32), jax.ShapeDtypeStruct((8, D), F32),
                   jax.ShapeDtypeStruct((8, LANES), F32)],
        compiler_params=_params(("arbitrary",)),
    )(where, ri, ro, rs, pi, po, ps)


def _exchange_results(gi2, go2, st):
    flips = [(fx, fy, fc) for fx in (0, 1) for fy in (0, 1) for fc in (0, 1)][1:]

    def body(_gi, _go, st_ref, gi_ref, go_ref, all_ref, send, recv, lsem):
        x, y, c = _pos()
        sib = (x, y, 1 - c)

        def half(k, ref, slot):
            return pltpu.make_async_remote_copy(src_ref=ref.at[slot], dst_ref=ref.at[slot], send_sem=send.at[k],
                                                recv_sem=recv.at[k], device_id=sib, device_id_type=MESH)

        def stat(k, src, slot, dev):
            return pltpu.make_async_remote_copy(src_ref=src, dst_ref=all_ref.at[slot], send_sem=send.at[k],
                                                recv_sem=recv.at[k], device_id=dev, device_id_type=MESH)

        mine = pltpu.make_async_copy(st_ref, all_ref.at[4 * x + 2 * y + c], lsem)
        mine.start()
        sends = [half(k, ref, c) for k, ref in enumerate((gi_ref, go_ref))]
        peers = [(_flip(x, fx), _flip(y, fy), _flip(c, fc)) for fx, fy, fc in flips]
        sends += [stat(2 + k, st_ref, 4 * x + 2 * y + c, dev) for k, dev in enumerate(peers)]
        for cp in sends:
            cp.start()
        for k, ref in enumerate((gi_ref, go_ref)):
            half(k, ref, 1 - c).wait_recv()
        for k, (px, py, pc) in enumerate(peers):
            slot = 4 * px + 2 * py + pc
            stat(2 + k, all_ref.at[slot], slot, (px, py, pc)).wait_recv()
        for cp in sends:
            cp.wait_send()
        mine.wait()

    n = 2 + len(flips)
    return pl.pallas_call(
        body, name="exchange_results",
        in_specs=[ANY, ANY, ANY], out_specs=[ANY, ANY, ANY], input_output_aliases={0: 0, 1: 1},
        out_shape=[jax.ShapeDtypeStruct((2, D // 2, CHUNK), F32), jax.ShapeDtypeStruct((2, WOUT_SHARD // 2, D), F32),
                   jax.ShapeDtypeStruct((NDEV, 8, D), F32)],
        scratch_shapes=[pltpu.SemaphoreType.DMA((n,)), pltpu.SemaphoreType.DMA((n,)), pltpu.SemaphoreType.DMA],
    )(gi2, go2, st)


def _adamw_math(w, g, m, v):
    m2 = ADAM_B1 * m + (1.0 - ADAM_B1) * g
    v2 = ADAM_B2 * v + (1.0 - ADAM_B2) * (g * g)
    m_hat = m2 / (1.0 - ADAM_B1 ** ADAM_STEP)
    v_hat = v2 / (1.0 - ADAM_B2 ** ADAM_STEP)
    delta = -ADAM_LR * (m_hat / (jnp.sqrt(v_hat) + ADAM_EPS) + ADAM_WD * w)
    return delta, m2, v2


def _adamw(w, g, m, v, name):
    rows, cols = w.shape
    tm = 256 if rows % 256 == 0 else rows

    def body(w_ref, g_ref, m_ref, v_ref, d_ref, m2_ref, v2_ref):
        d_ref[...], m2_ref[...], v2_ref[...] = _adamw_math(w_ref[...], g_ref[...], m_ref[...], v_ref[...])

    shape = jax.ShapeDtypeStruct(w.shape, F32)
    return pl.pallas_call(
        body, grid=(rows // tm,), name=name,
        in_specs=[_rows(tm, cols)] * 4, out_specs=[_rows(tm, cols)] * 3, out_shape=[shape] * 3,
        compiler_params=_params(("arbitrary",)),
    )(w, g, m, v)


def _adamw_vectors(g5, first_parts, ws, ms, vs):
    n = len(ws)

    def body(g_ref, parts_ref, *refs):
        ins, g0_ref, outs = refs[:3 * n], refs[3 * n], refs[3 * n + 1:]
        g0 = parts_ref[0, 0:1, :]
        for dev in range(1, NDEV):
            g0 = g0 + parts_ref[dev, 0:1, :]
        g0_ref[...] = g0
        for i in range(n):
            g = g0 if i == 0 else g_ref[i:i + 1, :]
            res = _adamw_math(ins[i][...], g, ins[n + i][...], ins[2 * n + i][...])
            for kind in range(3):
                outs[kind * n + i][...] = res[kind]

    shape = jax.ShapeDtypeStruct((1, D), F32)
    return pl.pallas_call(body, name="adamw_vectors", out_shape=[shape] * (1 + 3 * n), compiler_params=_params())(
        g5, first_parts, *ws, *ms, *vs)


def kernel(x, norm_g, w_in, conv_w, conv_b, conv_ln_g, conv_ln_b, w_out, final_norm_g, loss_target, m_norm_g, m_w_in, m_conv_w, m_conv_b, m_conv_ln_g, m_conv_ln_b, m_w_out, m_final_norm_g, v_norm_g, v_w_in, v_conv_w, v_conv_b, v_conv_ln_g, v_conv_ln_b, v_w_out, v_final_norm_g):
    chip = 2 * lax.axis_index("x") + lax.axis_index("y")
    where = jnp.stack([chip, lax.axis_index("c")]).astype(jnp.int32)
    taps_shard = jnp.pad(conv_w[0], ((0, HALO - CONV_K), (0, 0)))
    wi_full, wo_full, cw_full = _place_shards(w_in[0], w_out[0], taps_shard, where)
    wi_full = _gather_w_in(wi_full)

    gf = final_norm_g[None]
    xb = x[0]
    h, q, k, v, a_gate, c_val, c_glu, c_gate, wo_full, cw_full = _inproj_fwd(xb, norm_g, wi_full, wo_full, cw_full)
    tables = [_bias_table(d) for d in PATTERNS]
    o, lse, y_att = _attn_fwd(q, k, v, tables, a_gate)
    u, y_conv = _conv_fwd(c_val, c_glu, c_gate, cw_full, conv_b, conv_ln_g, conv_ln_b)
    dx2, dy_att, du, dc_gate, dw_out, st_out = _outproj_loss(
        y_att, y_conv, wo_full, xb, loss_target[0], gf, u, c_gate, conv_ln_g, conv_ln_b)
    dc_val, dc_glu, dconv_w = _conv_bwd_taps(du, c_val, c_glu, cw_full)

    early = [dw_out.reshape(NCHIP, WOUT_SHARD, D), jnp.concatenate([st_out, dconv_w], axis=0)]
    po, ps = _add_halves(early, _exchange_halves(early, "exchange_halves_early"), "add_halves_early")
    do, da_gate, delta, ro, rs = _attn_gate_bwd(dy_att, o, a_gate, _head_sum_selectors(), [po, ps])
    dqs, dkvs = zip(*[_attn_bwd(q, k, v, do, lse, delta, t, d) for t, d in zip(tables, PATTERNS)])

    dproj_pieces = (dqs, dkvs, (da_gate, dc_val, dc_glu, dc_gate))
    late = [_inproj_bwd_w(h, *dproj_pieces)]
    (pi,) = _add_halves(late, _exchange_halves(late, "exchange_halves"), "add_halves")
    grad_x, st_in, ri = _inproj_bwd_x(*dproj_pieces, wi_full, xb, norm_g, dx2, pi)
    gi2, go2, g_small, g5, loss8 = _sum_chips(ri, ro, rs, pi, po, ps, where)
    gi2, go2, norm_g_parts = _exchange_results(gi2, go2, st_in)
    g_w_in = gi2.reshape(D, CHUNK)
    g_w_out = go2.reshape(WOUT_SHARD, D)
    g_taps = lax.dynamic_slice(g_small, (ROW_TAPS, chip * CONVW_SHARD), (CONV_K, CONVW_SHARD))

    d_w_in, m2_w_in, v2_w_in = _adamw(w_in[0], g_w_in, m_w_in[0], v_w_in[0], "adamw_w_in")
    d_w_out, m2_w_out, v2_w_out = _adamw(w_out[0], g_w_out, m_w_out[0], v_w_out[0], "adamw_w_out")
    d_taps, m2_taps, v2_taps = _adamw(conv_w[0], g_taps, m_conv_w[0], v_conv_w[0], "adamw_conv_w")
    g_norm, *vec = _adamw_vectors(
        g5, norm_g_parts,
        (norm_g, conv_b, conv_ln_g, conv_ln_b, gf),
        (m_norm_g, m_conv_b, m_conv_ln_g, m_conv_ln_b, m_final_norm_g[None]),
        (v_norm_g, v_conv_b, v_conv_ln_g, v_conv_ln_b, v_final_norm_g[None]))
    d_vec, m2_vec, v2_vec = vec[0:5], vec[5:10], vec[10:15]

    def weight_order(ng, wi, cw, cb, lg, lb, wo, fg):
        return (ng, wi[None], cw[None], cb, lg, lb, wo[None], fg[0])

    grads = weight_order(g_norm, g_w_in, g_taps, g5[1:2], g5[2:3], g5[3:4], g_w_out, g5[4:5])
    deltas = weight_order(d_vec[0], d_w_in, d_taps, d_vec[1], d_vec[2], d_vec[3], d_w_out, d_vec[4])
    new_m = weight_order(m2_vec[0], m2_w_in, m2_taps, m2_vec[1], m2_vec[2], m2_vec[3], m2_w_out, m2_vec[4])
    new_v = weight_order(v2_vec[0], v2_w_in, v2_taps, v2_vec[1], v2_vec[2], v2_vec[3], v2_w_out, v2_vec[4])
    return (loss8[0, 0], grad_x[None], *grads, *deltas, *new_m, *new_v)
```

```python
import jax
import jax.numpy as jnp
from jax import lax
from jax.experimental import pallas as pl
from jax.experimental.pallas import tpu as pltpu

F32 = jnp.float32
BF16 = jnp.bfloat16

S = 4096
D = 1024
LANES = 128
HD = 64
NKV = 4
GQ = 4
KVW = NKV * HD
NCOL = 5632
CONV_K = 31
HALO = 32
BLK = 128
PATTERNS = (1, 4, 16)
NORM_EPS = 1e-6
LN_EPS = 1e-5
NEG = -1e30
OFF_Q, OFF_K, OFF_AG, OFF_CV, OFF_CG, OFF_CGATE = 0, 1024, 1536, 2560, 3584, 4608
NCHIP = 4
CHUNK = NCOL // NCHIP
WOUT_ROWS = 2 * D
WOUT_SHARD = WOUT_ROWS // NCHIP
CONVW_SHARD = D // NCHIP

ADAM_LR, ADAM_B1, ADAM_B2, ADAM_EPS, ADAM_WD, ADAM_STEP = 0.001, 0.9, 0.999, 1e-08, 0.01, 10

VMEM_LIMIT = 56 * 1024 * 1024


def _params(sem=None, vmem=VMEM_LIMIT):
    return pltpu.CompilerParams(dimension_semantics=sem, vmem_limit_bytes=vmem)


def _sigmoid(a):
    return 0.5 * jnp.tanh(0.5 * a) + 0.5


def _rows(tm, width):
    return pl.BlockSpec((tm, width), lambda i: (i, 0))


def _slabs(n):
    return jax.ShapeDtypeStruct((n, S, LANES), F32)


def _slab_rows(n, tm):
    return pl.BlockSpec((n, tm, LANES), lambda i: (0, i, 0))


def _resident(shape):
    return pl.BlockSpec(shape, lambda *_: (0,) * len(shape), pipeline_mode=pl.Buffered(1))


def _dot(a, b):
    return jnp.dot(a, b, preferred_element_type=F32)


def _dot_nt(a, b):
    return lax.dot_general(a, b, (((1,), (1,)), ((), ())), preferred_element_type=F32)


def _dot_tn(a, b):
    return lax.dot_general(a, b, (((0,), (0,)), ((), ())), preferred_element_type=F32)


def _inproj_fwd(x, g1, w_bf, wo_full, cw_full):
    tm = 512
    steps = S // tm

    def body(x_ref, g_ref, w_ref, _wo, _cw, h_ref, q_ref, k_ref, v_ref, ag_ref, cv_ref, cg_ref, cgate_ref,
             wo_ref, cw_ref, send, recv):
        i = pl.program_id(0)
        stages = _gather_stages([(W_OUT, wo_ref), (TAPS, cw_ref)], send, recv)
        for stage, step in zip(stages[:3], (0, steps // 2 - 1, steps - 2)):
            pl.when(i == step)(stage)
        xt = x_ref[...]
        r = lax.rsqrt(jnp.mean(xt * xt, axis=-1, keepdims=True) + NORM_EPS)
        h = (xt * r * g_ref[...]).astype(BF16)
        h_ref[...] = h
        q = _dot(h, w_ref[:, OFF_Q:OFF_Q + D]) * (HD ** -0.5)
        kv = _dot(h, w_ref[:, OFF_K:OFF_K + 2 * KVW])
        for sl in range(D // LANES):
            q_ref[sl] = q[:, sl * LANES:(sl + 1) * LANES]
        for sl in range(KVW // LANES):
            k_ref[sl] = kv[:, sl * LANES:(sl + 1) * LANES]
            v_ref[sl] = kv[:, KVW + sl * LANES:KVW + (sl + 1) * LANES]
        ag_ref[...] = _dot(h, w_ref[:, OFF_AG:OFF_AG + D])
        cv_ref[...] = _dot(h, w_ref[:, OFF_CV:OFF_CV + D])
        cg_ref[...] = _dot(h, w_ref[:, OFF_CG:OFF_CG + D])
        cgate_ref[...] = _dot(h, w_ref[:, OFF_CGATE:OFF_CGATE + D])
        pl.when(i == steps - 1)(stages[3])

    big = jax.ShapeDtypeStruct((S, D), F32)
    return pl.pallas_call(
        body, grid=(steps,), name="inproj_fwd",
        in_specs=[_rows(tm, D), _resident((1, D)), _resident((D, NCOL)), ANY, ANY],
        out_specs=[_rows(tm, D), _slab_rows(D // LANES, tm), _slab_rows(KVW // LANES, tm), _slab_rows(KVW // LANES, tm),
                   _rows(tm, D), _rows(tm, D), _rows(tm, D), _rows(tm, D), ANY, ANY],
        out_shape=[jax.ShapeDtypeStruct((S, D), BF16), _slabs(D // LANES), _slabs(KVW // LANES), _slabs(KVW // LANES),
                   big, big, big, big,
                   jax.ShapeDtypeStruct((WOUT_ROWS, D), BF16), jax.ShapeDtypeStruct((HALO, D), F32)],
        input_output_aliases={3: 8, 4: 9},
        scratch_shapes=_gather_sems(2),
        compiler_params=_params(("arbitrary",)),
    )(x, g1, w_bf, wo_full, cw_full)


def _bias_table(d):
    h = jnp.arange(NKV * GQ, dtype=F32)
    slopes = jnp.exp2(-8.0 * (h + 1.0) / (NKV * GQ))
    qi = jnp.arange(BLK)[:, None]
    kj = jnp.arange(2 * BLK)[None, :]
    dist = BLK + qi - kj
    window = (dist >= 0) & (dist <= BLK)
    bias = -slopes[:, None, None] * (dist * d).astype(F32)[None]
    has_prev = jnp.stack([jnp.broadcast_to(kj >= BLK, (BLK, 2 * BLK)), jnp.ones((BLK, 2 * BLK), bool)])
    valid = window[None] & has_prev
    tab = jnp.where(valid[:, None], bias[None], NEG)
    return tab.reshape(2, NKV, GQ * BLK, 2 * BLK)


def _sub_rows(start, d):
    if d == 1:
        return pl.ds(pl.multiple_of(start, BLK), BLK)
    return pl.ds(start, BLK, stride=d)


CHUNK_ROWS = 2048
BLOCKS_PER_CHUNK = CHUNK_ROWS // BLK


def _low_lanes(rows=BLK):
    return lax.broadcasted_iota(jnp.int32, (rows, LANES), 1) < HD


def _block_start(idx, d):
    shift = d.bit_length() - 1
    b, r = lax.shift_right_logical(idx, shift), lax.bitwise_and(idx, d - 1)
    start = b * (BLK * d) + r
    return b, start, jnp.maximum(start - BLK * d, r)


def _stack_heads(ref, rows):
    low = _low_lanes()
    t0, t1 = ref[0, rows, :], ref[1, rows, :]
    return jnp.concatenate([jnp.where(low, t0, 0.0), jnp.where(low, 0.0, t0),
                            jnp.where(low, t1, 0.0), jnp.where(low, 0.0, t1)], axis=0).astype(BF16)


def _unstack_heads(dup):
    low = _low_lanes()
    return (jnp.where(low, dup[0:BLK], dup[BLK:2 * BLK]), jnp.where(low, dup[2 * BLK:3 * BLK], dup[3 * BLK:4 * BLK]))


def _kv_dup(ref, prow, rows, odd):
    t = jnp.concatenate([ref[0, prow, :], ref[0, rows, :]], axis=0)
    swapped = pltpu.roll(t, HD, axis=1)
    keep = jnp.logical_xor(_low_lanes(2 * BLK), odd)
    return jnp.where(keep, t, swapped).astype(BF16)


PIECES = 3


def _by_head(tiles):
    lane = lax.broadcasted_iota(jnp.int32, tiles[0].shape, 1)
    out = tiles[0]
    for g in range(1, GQ):
        out = jnp.where(lax.bitwise_and(lane, GQ - 1) == g, tiles[g], out)
    return out


def _minus_in_pieces(x):
    lane = lax.broadcasted_iota(jnp.int32, x.shape, 1)
    hi = (-x).astype(BF16).astype(F32)
    rest = -x - hi
    mid = rest.astype(BF16).astype(F32)
    lo = (rest - mid).astype(BF16).astype(F32)
    return jnp.where(lane < GQ, hi, jnp.where(lane < 2 * GQ, mid, jnp.where(lane < PIECES * GQ, lo, 0.0)))


def _attn_fwd(q, k, v, tables, a_gate):
    tm = 256
    width = GQ * HD

    lane_out = jnp.arange(LANES)[None, :] // HD
    spread_sel = jnp.stack([jnp.arange(LANES)[:, None] == 2 * half + lane_out for half in range(2)]).astype(BF16)

    def body(q_ref, k_ref, v_ref, b1_ref, b2_ref, b3_ref, ag_ref, sel_ref, o_ref, lse_ref, y_ref, op, lp):
        odd = pl.program_id(0) % 2 == 1
        chunk = pl.program_id(1)
        ones = jnp.ones((2 * BLK, LANES), BF16)

        for pat, (d, b_ref) in enumerate(zip(PATTERNS, (b1_ref, b2_ref, b3_ref))):
            def block(idx, carry, pat=pat, d=d, b_ref=b_ref):
                b, start, pstart = _block_start(chunk * BLOCKS_PER_CHUNK + idx, d)
                rows, prow = _sub_rows(start, d), _sub_rows(pstart, d)
                mine = _sub_rows(start - chunk * CHUNK_ROWS, d)
                qs = _stack_heads(q_ref, mine)
                kw = _kv_dup(k_ref, prow, rows, odd)
                vw = _kv_dup(v_ref, prow, rows, odd)
                s = _dot_nt(qs, kw) + b_ref[jnp.minimum(b, 1), 0]
                m = jnp.max(s, axis=1, keepdims=True)
                p = jnp.exp(s - m).astype(BF16)
                ol = _dot(p, jnp.concatenate([vw, ones], axis=1))
                l = ol[:, LANES:]
                op[pat, 0, mine, :], op[pat, 1, mine, :] = _unstack_heads(ol[:, :LANES] / l)
                lp[pat, mine, :] = _by_head([(m + jnp.log(l))[g * BLK:(g + 1) * BLK] for g in range(GQ)])
                return carry

            lax.fori_loop(0, BLOCKS_PER_CHUNK, block, 0, unroll=2)

        def mix(t, carry):
            r = pl.ds(pl.multiple_of(t * tm, tm), tm)
            a, b, c = lp[0, r, :], lp[1, r, :], lp[2, r, :]
            m = jnp.maximum(jnp.maximum(a, b), c)
            ea, eb, ec = jnp.exp(a - m), jnp.exp(b - m), jnp.exp(c - m)
            den = ea + eb + ec
            lse_ref[0, r, :] = _minus_in_pieces(m + jnp.log(den))
            inv = 1.0 / den
            for half in range(2):
                def spread(w):
                    hi = w.astype(BF16)
                    lo = (w - hi.astype(F32)).astype(BF16)
                    return _dot(hi, sel_ref[half]) + _dot(lo, sel_ref[half])

                o = (spread(ea * inv) * op[0, half, r, :] + spread(eb * inv) * op[1, half, r, :]
                     + spread(ec * inv) * op[2, half, r, :])
                o_ref[half, r, :] = o
                cols = slice(half * LANES, (half + 1) * LANES)
                ag = ag_ref[r, cols]
                y_ref[r, cols] = (o * (ag * _sigmoid(ag))).astype(BF16)
            return carry

        lax.fori_loop(0, CHUNK_ROWS // tm, mix, 0, unroll=2)

    q_like = pl.BlockSpec((2, CHUNK_ROWS, LANES), lambda j, c: (j, c, 0))
    per_kv = pl.BlockSpec((1, CHUNK_ROWS, LANES), lambda j, c: (j, c, 0))
    kv = pl.BlockSpec((1, S, LANES), lambda j, c: (j // 2, 0, 0))
    bias_spec = pl.BlockSpec((2, 1, GQ * BLK, 2 * BLK), lambda j, c: (0, j, 0, 0))
    group_cols = pl.BlockSpec((CHUNK_ROWS, width), lambda j, c: (c, j))
    return pl.pallas_call(
        body, grid=(NKV, S // CHUNK_ROWS), name="attn_fwd",
        in_specs=[q_like, kv, kv, bias_spec, bias_spec, bias_spec, group_cols,
                  pl.BlockSpec((2, LANES, LANES), lambda j, c: (0, 0, 0))],
        out_specs=[q_like, per_kv, group_cols],
        out_shape=[_slabs(D // LANES), _slabs(NKV), jax.ShapeDtypeStruct((S, D), BF16)],
        scratch_shapes=[pltpu.VMEM((len(PATTERNS), 2, CHUNK_ROWS, LANES), F32),
                        pltpu.VMEM((len(PATTERNS), CHUNK_ROWS, LANES), F32)],
        compiler_params=_params(("arbitrary", "arbitrary")),
    )(q, k, v, *tables, a_gate, spread_sel)


def _head_sum_selectors():
    lane_in = jnp.arange(LANES)[:, None] // HD
    return jnp.stack([jnp.broadcast_to(lane_in == h, (LANES, LANES)) for h in range(2)]).astype(BF16)


def _attn_gate_bwd(dy_att, o, a_gate, selectors, chip_sums):
    tm = 512
    last = S // tm - 1
    landing, sems = _exchange_results_of(chip_sums)
    n_sums = len(chip_sums)

    def body(dy_ref, o_ref, ag_ref, e_ref, *refs):
        sums, (do_ref, dag_ref, delta_ref), refs = refs[:n_sums], refs[n_sums:n_sums + 3], refs[n_sums + 3:]
        landed, (send, recv) = refs[:n_sums], refs[n_sums:]
        i = pl.program_id(0)
        copies = _chip_exchange_copies(sums, landed, send, recv)
        _start_exchange(copies, i == 0)
        for j in range(NKV):
            deltas = []
            for sl in (2 * j, 2 * j + 1):
                cols = slice(sl * LANES, (sl + 1) * LANES)
                dy, ag, o_ = dy_ref[:, cols], ag_ref[:, cols], o_ref[sl]
                sg = _sigmoid(ag)
                do = dy * (ag * sg)
                do_ref[sl] = do
                dag_ref[:, cols] = (dy * o_ * (sg * (1.0 + ag * (1.0 - sg)))).astype(BF16)
                prod = do * o_
                hi = prod.astype(BF16)
                lo = (prod - hi.astype(F32)).astype(BF16)
                deltas += [_dot(hi, e_ref[h]) + _dot(lo, e_ref[h]) for h in range(2)]
            delta_ref[j] = _minus_in_pieces(_by_head(deltas))
        _finish_exchange(copies, i == last)

    return pl.pallas_call(
        body, grid=(S // tm,), name="attn_gate_bwd",
        in_specs=[_rows(tm, D), _slab_rows(D // LANES, tm), _rows(tm, D), _resident((2, LANES, LANES))] + [ANY] * n_sums,
        out_specs=[_slab_rows(D // LANES, tm), _rows(tm, D), _slab_rows(NKV, tm)] + [ANY] * n_sums,
        out_shape=[_slabs(D // LANES), jax.ShapeDtypeStruct((S, D), BF16), _slabs(NKV)] + landing,
        scratch_shapes=sems,
        compiler_params=_params(("arbitrary",)),
    )(dy_att, o, a_gate, selectors, *chip_sums)


def _own_pieces(tile):
    lane = lax.broadcasted_iota(jnp.int32, tile.shape, 1)
    head = jnp.where(lane < PIECES * GQ, lax.bitwise_and(lane, GQ - 1), -1)
    return jnp.concatenate([jnp.where(head == g, tile, 0.0) for g in range(GQ)], axis=0).astype(BF16)


def _attn_bwd(q, k, v, do, lse, delta, bias, d):
    def body(q_ref, do_ref, l_ref, dl_ref, k_ref, v_ref, b_ref, dq_ref, dkv_ref, acc):
        odd = pl.program_id(0) % 2 == 1
        chunk = pl.program_id(1)
        ones = (lax.broadcasted_iota(jnp.int32, (2 * BLK, LANES), 1) < PIECES * GQ).astype(BF16)

        def in_acc(block_idx):
            return pl.ds(pl.multiple_of(block_idx * BLK, BLK), BLK)

        @pl.when(chunk == 0)
        def _():
            acc[...] = jnp.zeros_like(acc)

        def block(idx, carry):
            idx = chunk * BLOCKS_PER_CHUNK + idx
            b, start, pstart = _block_start(idx, d)
            rows, prow = _sub_rows(start, d), _sub_rows(pstart, d)
            mine = _sub_rows(start - chunk * CHUNK_ROWS, d)
            qs = _stack_heads(q_ref, mine)
            dos = _stack_heads(do_ref, mine)
            kw = _kv_dup(k_ref, prow, rows, odd)
            vw = _kv_dup(v_ref, prow, rows, odd)
            s = _dot_nt(jnp.concatenate([qs, _own_pieces(l_ref[0, mine, :])], axis=1),
                        jnp.concatenate([kw, ones], axis=1)) + b_ref[jnp.minimum(b, 1), 0]
            p = jnp.exp(s)
            dv2 = _dot_tn(p.astype(BF16), dos)
            dp = _dot_nt(jnp.concatenate([dos, _own_pieces(dl_ref[0, mine, :])], axis=1),
                         jnp.concatenate([vw, ones], axis=1))
            ds = (p * dp).astype(BF16)
            dq_ref[0, mine, :], dq_ref[1, mine, :] = _unstack_heads(_dot(ds, kw))
            dk2 = _dot_tn(ds, qs)
            dkv = jnp.where(_low_lanes(2 * BLK), dk2 + pltpu.roll(dk2, HD, axis=1), dv2 + pltpu.roll(dv2, HD, axis=1))
            acc[in_acc(idx), :] = acc[in_acc(idx), :] + dkv[BLK:]
            before = jnp.where(b >= 1, idx - d, idx)
            acc[in_acc(before), :] = acc[in_acc(before), :] + dkv[:BLK]
            return carry

        lax.fori_loop(0, BLOCKS_PER_CHUNK, block, 0, unroll=16)

        @pl.when(chunk == S // CHUNK_ROWS - 1)
        def _():
            def place(idx, carry):
                _, start, _ = _block_start(idx, d)
                dkv_ref[0, _sub_rows(start, d), :] = acc[in_acc(idx), :]
                return carry

            lax.fori_loop(0, S // BLK, place, 0, unroll=4)

    q_like = pl.BlockSpec((2, CHUNK_ROWS, LANES), lambda j, c: (j, c, 0))
    pieces = pl.BlockSpec((1, CHUNK_ROWS, LANES), lambda j, c: (j, c, 0))
    kv = pl.BlockSpec((1, S, LANES), lambda j, c: (j // 2, 0, 0))
    per_kv = pl.BlockSpec((1, S, LANES), lambda j, c: (j, 0, 0))
    bias_spec = pl.BlockSpec((2, 1, GQ * BLK, 2 * BLK), lambda j, c: (0, j, 0, 0))
    return pl.pallas_call(
        body, grid=(NKV, S // CHUNK_ROWS), name=f"attn_bwd_d{d}",
        in_specs=[q_like, q_like, pieces, pieces, kv, kv, bias_spec],
        out_specs=[q_like, per_kv],
        out_shape=[_slabs(D // LANES), _slabs(NKV)],
        scratch_shapes=[pltpu.VMEM((S, LANES), F32)],
        compiler_params=_params(("arbitrary", "arbitrary")),
    )(q, do, lse, delta, k, v, bias)


CONV_T = 256


def _halo_before(i):
    return (jnp.maximum(i * (CONV_T // HALO) - 1, 0), 0)


def _halo_after(i):
    return (jnp.minimum((i + 1) * (CONV_T // HALO), S // HALO - 1), 0)


SUBLANES = 8
NCH = D // LANES
GROUP = SUBLANES * SUBLANES


def _comb(ref, cb, base):
    return ref[cb, pl.ds(base, SUBLANES, stride=SUBLANES), :]


def _taps(w_ref, cols):
    return [jnp.broadcast_to(w_ref[j:j + 1, cols], (SUBLANES, LANES)) for j in range(CONV_K)]


def _conv_fwd(c_val, c_glu, c_gate, conv_w, conv_b, ln_g, ln_b):
    T = CONV_T

    def body(cv_ref, cg_ref, cvh_ref, cgh_ref, gate_ref, w_ref, b_ref, lg_ref, lb_ref, u_ref, y_ref, win, us):
        i = pl.program_id(0)
        for cb in range(NCH):
            cols = slice(cb * LANES, (cb + 1) * LANES)
            win[cb, HALO:HALO + T, :] = cv_ref[:, cols] * _sigmoid(cg_ref[:, cols])
            win[cb, 0:HALO, :] = jnp.where(i > 0, cvh_ref[:, cols] * _sigmoid(cgh_ref[:, cols]), 0.0)
        for cb in range(NCH):
            cols = slice(cb * LANES, (cb + 1) * LANES)
            taps = _taps(w_ref, cols)
            bias = jnp.broadcast_to(b_ref[:, cols], (SUBLANES, LANES))

            def group(g, carry):
                for b in range(SUBLANES):
                    base = g * GROUP + b
                    acc = bias
                    for j in range(CONV_K):
                        acc = acc + taps[j] * _comb(win, cb, base + (HALO - (CONV_K - 1) + j))
                    us[cb, pl.ds(base, SUBLANES, stride=SUBLANES), :] = acc
                return carry

            lax.fori_loop(0, T // GROUP, group, 0, unroll=2)
        total = us[0]
        for cb in range(1, NCH):
            total = total + us[cb]
        mu = jnp.sum(total, axis=-1, keepdims=True) * (1.0 / D)
        sq = jnp.zeros((T, LANES), F32)
        for cb in range(NCH):
            uc = us[cb] - mu
            sq = sq + uc * uc
        rstd = lax.rsqrt(jnp.sum(sq, axis=-1, keepdims=True) * (1.0 / D) + LN_EPS)
        for cb in range(NCH):
            cols = slice(cb * LANES, (cb + 1) * LANES)
            u = us[cb]
            u_ref[:, cols] = u
            nrm = (u - mu) * rstd * lg_ref[:, cols] + lb_ref[:, cols]
            gate = gate_ref[:, cols]
            y_ref[:, cols] = (nrm * _sigmoid(nrm) * (gate * _sigmoid(gate))).astype(BF16)

    halo = pl.BlockSpec((HALO, D), _halo_before)
    return pl.pallas_call(
        body, grid=(S // T,), name="conv_fwd",
        in_specs=[_rows(T, D), _rows(T, D), halo, halo, _rows(T, D),
                  _resident((HALO, D)), _resident((1, D)), _resident((1, D)), _resident((1, D))],
        out_specs=[_rows(T, D), _rows(T, D)],
        out_shape=[jax.ShapeDtypeStruct((S, D), F32), jax.ShapeDtypeStruct((S, D), BF16)],
        scratch_shapes=[pltpu.VMEM((NCH, T + HALO, LANES), F32), pltpu.VMEM((NCH, T, LANES), F32)],
        compiler_params=_params(("arbitrary",)),
    )(c_val, c_glu, c_val, c_glu, c_gate, conv_w, conv_b, ln_g, ln_b)


def _conv_bwd_taps(du, c_val, c_glu, conv_w):
    T = CONV_T
    last = S // T - 1

    def body(du_ref, dua_ref, cv_ref, cg_ref, cvh_ref, cgh_ref, w_ref, dcv_ref, dcg_ref, dw_ref,
             hwin, dwin, dhs, dw_acc):
        i = pl.program_id(0)

        @pl.when(i == 0)
        def _():
            dw_acc[...] = jnp.zeros_like(dw_acc)

        for cb in range(NCH):
            cols = slice(cb * LANES, (cb + 1) * LANES)
            hwin[cb, HALO:HALO + T, :] = cv_ref[:, cols] * _sigmoid(cg_ref[:, cols])
            hwin[cb, 0:HALO, :] = jnp.where(i > 0, cvh_ref[:, cols] * _sigmoid(cgh_ref[:, cols]), 0.0)
            dwin[cb, 0:T, :] = du_ref[:, cols]
            dwin[cb, T:T + HALO, :] = jnp.where(i < last, dua_ref[:, cols], 0.0)
        for cb in range(NCH):
            cols = slice(cb * LANES, (cb + 1) * LANES)
            taps = _taps(w_ref, cols)

            def group_dh(g, carry):
                for b in range(SUBLANES):
                    base = g * GROUP + b
                    acc = jnp.zeros((SUBLANES, LANES), F32)
                    for j in range(CONV_K):
                        acc = acc + taps[j] * _comb(dwin, cb, base + (CONV_K - 1 - j))
                    dhs[cb, pl.ds(base, SUBLANES, stride=SUBLANES), :] = acc
                return carry

            lax.fori_loop(0, T // GROUP, group_dh, 0, unroll=2)

            def group_dw(g, sums):
                for b in range(SUBLANES):
                    base = g * GROUP + b
                    d = _comb(dwin, cb, base)
                    sums = tuple(sums[j] + d * _comb(hwin, cb, base + (HALO - (CONV_K - 1) + j))
                                 for j in range(CONV_K))
                return sums

            sums = lax.fori_loop(0, T // GROUP, group_dw, tuple(dw_acc[j, :, cols] for j in range(CONV_K)))
            for j in range(CONV_K):
                dw_acc[j, :, cols] = sums[j]
            dh = dhs[cb]
            cv, sg = cv_ref[:, cols], _sigmoid(cg_ref[:, cols])
            dcv_ref[:, cols] = (dh * sg).astype(BF16)
            dcg_ref[:, cols] = (dh * cv * (sg * (1.0 - sg))).astype(BF16)

        @pl.when(i == last)
        def _():
            dw_ref[...] = jnp.zeros_like(dw_ref)
            for j in range(CONV_K):
                dw_ref[j:j + 1, :] = jnp.sum(dw_acc[j], axis=0, keepdims=True)

    before = pl.BlockSpec((HALO, D), _halo_before)
    after = pl.BlockSpec((HALO, D), _halo_after)
    big = jax.ShapeDtypeStruct((S, D), BF16)
    return pl.pallas_call(
        body, grid=(S // T,), name="conv_bwd_taps",
        in_specs=[_rows(T, D), after, _rows(T, D), _rows(T, D), before, before, _resident((HALO, D))],
        out_specs=[_rows(T, D), _rows(T, D), pl.BlockSpec((HALO, D), lambda i: (0, 0))],
        out_shape=[big, big, jax.ShapeDtypeStruct((HALO, D), F32)],
        scratch_shapes=[pltpu.VMEM((NCH, T + HALO, LANES), F32), pltpu.VMEM((NCH, T + HALO, LANES), F32),
                        pltpu.VMEM((NCH, T, LANES), F32), pltpu.VMEM((CONV_K, SUBLANES, D), F32)],
        compiler_params=_params(("arbitrary",)),
    )(du, du, c_val, c_glu, c_val, c_glu, conv_w)


def _outproj_loss(y_att, y_conv, w_out_bf, x, target, gf, u, c_gate, ln_g, ln_b):
    tm = 256

    def body(ya_ref, yc_ref, w_ref, x_ref, t_ref, gf_ref, u_ref, gate_ref, lg_ref, lb_ref,
             dx2_ref, dya_ref, du_ref, dgate_ref, dw_ref, st_ref, acc):
        @pl.when(pl.program_id(0) == 0)
        def _():
            acc[...] = jnp.zeros_like(acc)
            st_ref[...] = jnp.zeros_like(st_ref)

        ya, yc = ya_ref[...], yc_ref[...]
        x2 = x_ref[...] + _dot(ya, w_ref[0:D, :]) + _dot(yc, w_ref[D:2 * D, :])
        r = lax.rsqrt(jnp.mean(x2 * x2, axis=-1, keepdims=True) + NORM_EPS)
        xn = x2 * r
        err = xn * gf_ref[...] - t_ref[...]
        dout = err * (1.0 / D)
        dxn = dout * gf_ref[...]
        dx2 = r * (dxn - xn * jnp.mean(dxn * xn, axis=-1, keepdims=True))
        dx2_ref[...] = dx2
        dx2b = dx2.astype(BF16)
        dya_ref[...] = _dot_nt(dx2b, w_ref[0:D, :])
        dy = _dot_nt(dx2b, w_ref[D:2 * D, :])
        acc[0:D, :] += _dot_tn(ya, dx2b)
        acc[D:2 * D, :] += _dot_tn(yc, dx2b)
        st_ref[ROW_FINAL_G:ROW_FINAL_G + 1, :] += jnp.sum(dout * xn, axis=0, keepdims=True)
        st_ref[ROW_LOSS:ROW_LOSS + 1, :] += jnp.sum(err * err, axis=0, keepdims=True) * (0.5 / D)

        u, gate = u_ref[...], gate_ref[...]
        mu = jnp.mean(u, axis=-1, keepdims=True)
        uc = u - mu
        rstd = lax.rsqrt(jnp.mean(uc * uc, axis=-1, keepdims=True) + LN_EPS)
        z = uc * rstd
        nrm = z * lg_ref[...] + lb_ref[...]
        sn, sg = _sigmoid(nrm), _sigmoid(gate)
        dgate_ref[...] = (dy * (nrm * sn) * (sg * (1.0 + gate * (1.0 - sg)))).astype(BF16)
        dn = dy * (gate * sg) * (sn * (1.0 + nrm * (1.0 - sn)))
        dz = dn * lg_ref[...]
        du = rstd * (dz - jnp.mean(dz, axis=-1, keepdims=True) - z * jnp.mean(dz * z, axis=-1, keepdims=True))
        du_ref[...] = du
        st_ref[ROW_LN_G:ROW_LN_G + 1, :] += jnp.sum(dn * z, axis=0, keepdims=True)
        st_ref[ROW_LN_B:ROW_LN_B + 1, :] += jnp.sum(dn, axis=0, keepdims=True)
        st_ref[ROW_CONV_B:ROW_CONV_B + 1, :] += jnp.sum(du, axis=0, keepdims=True)

        @pl.when(pl.program_id(0) == S // tm - 1)
        def _():
            dw_ref[...] = acc[...].astype(BF16)

    big = jax.ShapeDtypeStruct((S, D), F32)
    vec = _resident((1, D))
    return pl.pallas_call(
        body, grid=(S // tm,), name="outproj_loss",
        in_specs=[_rows(tm, D), _rows(tm, D), _resident((WOUT_ROWS, D)), _rows(tm, D), _rows(tm, D), vec,
                  _rows(tm, D), _rows(tm, D), vec, vec],
        out_specs=[_rows(tm, D), _rows(tm, D), _rows(tm, D), _rows(tm, D),
                   pl.BlockSpec((WOUT_ROWS, D), lambda i: (0, 0)), pl.BlockSpec((8, D), lambda i: (0, 0))],
        out_shape=[big, big, big, jax.ShapeDtypeStruct((S, D), BF16),
                   jax.ShapeDtypeStruct((WOUT_ROWS, D), BF16), jax.ShapeDtypeStruct((8, D), F32)],
        scratch_shapes=[pltpu.VMEM((WOUT_ROWS, D), F32)],
        compiler_params=_params(("arbitrary",)),
    )(y_att, y_conv, w_out_bf, x, target, gf, u, c_gate, ln_g, ln_b)


UNITS_PER_CHUNK = CHUNK // LANES


def _dproj_unit(u, dqs, dkvs, gates, rows):
    if u < OFF_K // LANES:
        return ((dqs[0][u] + dqs[1][u] + dqs[2][u]) * (HD ** -0.5)).astype(BF16)
    if u < OFF_AG // LANES:
        w = u - OFF_K // LANES
        ta, tb = (dkvs[0][j] + dkvs[1][j] + dkvs[2][j] for j in (2 * (w % 2), 2 * (w % 2) + 1))
        low = _low_lanes(rows)
        if w < 2:
            return jnp.where(low, ta, pltpu.roll(tb, HD, axis=1)).astype(BF16)
        return jnp.where(low, pltpu.roll(ta, HD, axis=1), tb).astype(BF16)
    g, sl = divmod(u - OFF_AG // LANES, D // LANES)
    return gates[g][:, sl * LANES:(sl + 1) * LANES]


def _dproj_sources(units, dqs, dkvs, gates, rows):
    use_q = any(u < OFF_K // LANES for u in units)
    use_kv = any(OFF_K // LANES <= u < OFF_AG // LANES for u in units)
    use_g = sorted({(u - OFF_AG // LANES) // (D // LANES) for u in units if u >= OFF_AG // LANES})
    args = (list(dqs) if use_q else []) + (list(dkvs) if use_kv else []) + [gates[g] for g in use_g]
    specs = ([_slab_rows(D // LANES, rows)] * 3 if use_q else []) + ([_slab_rows(NKV, rows)] * 3 if use_kv else []) \
        + [_rows(rows, D)] * len(use_g)

    def pick(refs):
        refs = list(refs)
        q_refs = [refs.pop(0) for _ in range(3)] if use_q else None
        kv_refs = [refs.pop(0) for _ in range(3)] if use_kv else None
        return q_refs, kv_refs, {g: refs.pop(0) for g in use_g}

    return args, specs, pick


def _exchange_results_of(chip_sums):
    n = len(chip_sums) * len(CHIP_FLIPS)
    shapes = [jax.ShapeDtypeStruct((NCHIP,) + tuple(a.shape[1:] if a.ndim == 3 else a.shape), a.dtype)
              for a in chip_sums]
    return shapes, [pltpu.SemaphoreType.DMA((n,)), pltpu.SemaphoreType.DMA((n,))]


def _start_exchange(copies, first_step):
    @pl.when(first_step)
    def _():
        for out, _ in copies:
            out.start()


def _finish_exchange(copies, last_step):
    @pl.when(last_step)
    def _():
        for _, arrival in copies:
            arrival.wait_recv()
        for out, _ in copies:
            out.wait_send()


def _inproj_bwd_x(dqs, dkvs, gates, w_bf, x, g1, dx2, pi):
    tm = 256
    last = S // tm - 1
    units = range(NCOL // LANES)
    pieces, piece_specs, pick = _dproj_sources(units, dqs, dkvs, gates, tm)
    landing, sems = _exchange_results_of([pi])

    def body(*refs):
        piece_refs, refs = refs[:len(pieces)], refs[len(pieces):]
        w_ref, x_ref, g_ref, dx2_ref, pi_ref, gx_ref, st_ref, ri_ref, dp_ref, send, recv = refs
        i = pl.program_id(0)
        copies = _chip_exchange_copies([pi_ref], [ri_ref], send, recv)
        _start_exchange(copies, i == 0)

        @pl.when(i == 0)
        def _():
            st_ref[...] = jnp.zeros_like(st_ref)

        sources = pick(piece_refs)
        for u in units:
            dp_ref[:, u * LANES:(u + 1) * LANES] = _dproj_unit(u, *sources, tm)
        dh = _dot_nt(dp_ref[...], w_ref[...])
        xt = x_ref[...]
        r = lax.rsqrt(jnp.mean(xt * xt, axis=-1, keepdims=True) + NORM_EPS)
        xn = xt * r
        dxn = dh * g_ref[...]
        gx_ref[...] = dx2_ref[...] + r * (dxn - xn * jnp.mean(dxn * xn, axis=-1, keepdims=True))
        st_ref[0:1, :] += jnp.sum(dh * xn, axis=0, keepdims=True)
        _finish_exchange(copies, i == last)

    return pl.pallas_call(
        body, grid=(S // tm,), name="inproj_bwd_x",
        in_specs=piece_specs + [_resident((D, NCOL)), _rows(tm, D), _resident((1, D)), _rows(tm, D), ANY],
        out_specs=[_rows(tm, D), pl.BlockSpec((8, D), lambda i: (0, 0)), ANY],
        out_shape=[jax.ShapeDtypeStruct((S, D), F32), jax.ShapeDtypeStruct((8, D), F32)] + landing,
        scratch_shapes=[pltpu.VMEM((tm, NCOL), BF16)] + sems,
        compiler_params=_params(("arbitrary",)),
    )(*pieces, w_bf, x, g1, dx2, pi)


def _inproj_bwd_w(h, dqs, dkvs, gates):
    out = None
    for k in range(NCHIP):
        units = range(k * UNITS_PER_CHUNK, (k + 1) * UNITS_PER_CHUNK)
        tk = 512 if units[0] < OFF_K // LANES else 1024
        nk = S // tk
        pieces, piece_specs, pick = _dproj_sources(units, dqs, dkvs, gates, tk)
        handed_on = [] if out is None else [out]

        def body(*refs, units=units, pick=pick, n_pieces=len(pieces), n_in=1 + len(pieces) + len(handed_on)):
            h_ref, piece_refs = refs[0], refs[1:1 + n_pieces]
            o_ref, tile, acc = refs[n_in:]
            i = pl.program_id(0)

            @pl.when(i == 0)
            def _():
                acc[...] = jnp.zeros_like(acc)

            sources = pick(piece_refs)
            for n, u in enumerate(units):
                tile[:, n * LANES:(n + 1) * LANES] = _dproj_unit(u, *sources, tk)
            acc[...] += _dot_tn(h_ref[...], tile[...])

            @pl.when(i == nk - 1)
            def _():
                o_ref[0] = acc[...].astype(BF16)

        out = pl.pallas_call(
            body, grid=(nk,), name=f"inproj_bwd_w{k}",
            in_specs=[_rows(tk, D)] + piece_specs + [ANY] * len(handed_on),
            out_specs=pl.BlockSpec((1, D, CHUNK), lambda i, k=k: (k, 0, 0)),
            out_shape=jax.ShapeDtypeStruct((NCHIP, D, CHUNK), BF16),
            input_output_aliases={1 + len(pieces): 0} if handed_on else {},
            scratch_shapes=[pltpu.VMEM((tk, CHUNK), BF16), pltpu.VMEM((D, CHUNK), F32)],
            compiler_params=_params(("arbitrary",)),
        )(h, *pieces, *handed_on)
    return out


ROW_FINAL_G, ROW_LOSS, ROW_LN_G, ROW_LN_B, ROW_CONV_B, ROW_TAPS = 0, 1, 2, 3, 4, 8
SMALL_ROWS = 8 + HALO
NDEV = 8


MESH = pl.DeviceIdType.MESH
ANY = pl.BlockSpec(memory_space=pl.ANY)
CHIP_FLIPS = ((1, 0), (0, 1), (1, 1))


def _pos():
    return lax.axis_index("x"), lax.axis_index("y"), lax.axis_index("c")


def _flip(v, f):
    return 1 - v if f else v


def _ds(start, size, align=None):
    return pl.ds(pl.multiple_of(start, align or size), size)


def _place_shards(wi, wo, cw, where):
    steps = 4

    def body(where_ref, wi_ref, wo_ref, cw_ref, wi_full, wo_full, cw_full):
        wi_full[...] = wi_ref[...].astype(BF16)
        wo_full[...] = wo_ref[...].astype(BF16)
        cw_full[...] = cw_ref[...]

    grid_spec = pltpu.PrefetchScalarGridSpec(
        num_scalar_prefetch=1, grid=(steps,),
        in_specs=[pl.BlockSpec((D // steps, CHUNK), lambda i, w: (i, 0)),
                  pl.BlockSpec((WOUT_SHARD // steps, D), lambda i, w: (i, 0)),
                  pl.BlockSpec((HALO, CONVW_SHARD), lambda i, w: (0, 0))],
        out_specs=[pl.BlockSpec((D // steps, CHUNK), lambda i, w: (i, w[0])),
                   pl.BlockSpec((WOUT_SHARD // steps, D), lambda i, w: (w[0] * steps + i, 0)),
                   pl.BlockSpec((HALO, CONVW_SHARD), lambda i, w: (0, w[0]))])
    return pl.pallas_call(
        body, grid_spec=grid_spec, name="place_shards",
        out_shape=[jax.ShapeDtypeStruct((D, NCOL), BF16), jax.ShapeDtypeStruct((WOUT_ROWS, D), BF16),
                   jax.ShapeDtypeStruct((HALO, D), F32)],
        compiler_params=_params(("arbitrary",)),
    )(where, wi, wo, cw)


W_IN, W_OUT, TAPS = range(3)
GATHER_SEMS = 12


def _gather_stages(fulls, send, recv):
    halves = {W_IN: D // 2, W_OUT: WOUT_SHARD // 2, TAPS: HALO // 2}
    x, y, c = _pos()
    chips = {"me": (x, y), "x": (1 - x, y), "y": (x, 1 - y), "diag": (1 - x, 1 - y)}
    SENT = ((("me", 0), "x"), (("me", 1), "x"), (("me", 1), "y"), (("me", 0), "y"), (("x", 0), "y"), (("y", 1), "x"))
    LANDS = ((("x", 0), "x"), (("x", 1), "x"), (("y", 1), "y"), (("y", 0), "y"), (("diag", 0), "y"), (("diag", 1), "x"))
    N_ICI = len(SENT)

    def region(n_th, whose, half, part):
        a, full = fulls[n_th]
        chip = 2 * chips[whose][0] + chips[whose][1]
        n = halves[a] // 2
        row = half * halves[a] + part * n
        if a == W_IN:
            return full.at[_ds(row, n), _ds(chip * CHUNK, CHUNK, 128)]
        if a == W_OUT:
            return full.at[_ds(chip * WOUT_SHARD + row, n), :]
        return full.at[_ds(row, n), _ds(chip * CONVW_SHARD, CONVW_SHARD, 128)]

    def copy(n_th, kind, piece, dev):
        k = GATHER_SEMS * n_th + kind
        return pltpu.make_async_remote_copy(src_ref=piece, dst_ref=piece, send_sem=send.at[k], recv_sem=recv.at[k],
                                            device_id=dev, device_id_type=MESH)

    def sent(a, k):
        if k < N_ICI:
            (whose, part), to = SENT[k]
            return copy(a, k, region(a, whose, c, part), (*chips[to], c))
        (whose, part), _ = LANDS[k - N_ICI]
        return copy(a, k, region(a, whose, c, part), (x, y, 1 - c))

    def wait_arrival(a, k):
        if k < N_ICI:
            (whose, part), frm = LANDS[k]
            copy(a, k, region(a, whose, c, part), (*chips[frm], c)).wait_recv()
        else:
            (whose, part), _ = LANDS[k - N_ICI]
            copy(a, k, region(a, whose, 1 - c, part), (x, y, 1 - c)).wait_recv()

    arrays = range(len(fulls))

    def own_to_neighbours():
        for a in arrays:
            for k in (0, 2, 1, 3):
                sent(a, k).start()

    def pass_on_neighbours():
        for a in arrays:
            for k, onward in ((0, 4), (2, 5), (1, None), (3, None)):
                wait_arrival(a, k)
                if onward is not None:
                    sent(a, onward).start()
                sent(a, k + N_ICI).start()

    def pass_on_diagonal():
        for a in arrays:
            for k in (4, 5):
                wait_arrival(a, k)
                sent(a, k + N_ICI).start()

    def finish():
        for a in arrays:
            for k in range(N_ICI, 2 * N_ICI):
                wait_arrival(a, k)
            for k in range(2 * N_ICI):
                sent(a, k).wait_send()

    return own_to_neighbours, pass_on_neighbours, pass_on_diagonal, finish


def _gather_sems(n_arrays):
    return [pltpu.SemaphoreType.DMA((GATHER_SEMS * n_arrays,)), pltpu.SemaphoreType.DMA((GATHER_SEMS * n_arrays,))]


def _gather_w_in(wi_full):
    def body(_wi, full, send, recv):
        for stage in _gather_stages([(W_IN, full)], send, recv):
            stage()

    return pl.pallas_call(
        body, name="gather_w_in", in_specs=[ANY], out_specs=ANY, input_output_aliases={0: 0},
        out_shape=jax.ShapeDtypeStruct((D, NCOL), BF16), scratch_shapes=_gather_sems(1),
    )(wi_full)


def _half_shape(a):
    return jax.ShapeDtypeStruct((NCHIP, a.shape[1] // 2, a.shape[2]) if a.ndim == 3 else a.shape, a.dtype)


def _exchange_halves(arrays, name):
    n = len(arrays)

    def body(*refs):
        srcs, dsts, (send, recv) = refs[:n], refs[n:2 * n], refs[2 * n:]
        x, y, c = _pos()
        cps = []
        for k, (s_, d_) in enumerate(zip(srcs, dsts)):
            if len(s_.shape) == 3:
                h = s_.shape[1] // 2
                s_ = s_.at[:, _ds((1 - c) * h, h), :]
            cps.append(pltpu.make_async_remote_copy(src_ref=s_, dst_ref=d_, send_sem=send.at[k], recv_sem=recv.at[k],
                                                    device_id=(x, y, 1 - c), device_id_type=MESH))
        for cp in cps:
            cp.start()
        for cp in cps:
            cp.wait()

    return pl.pallas_call(
        body, name=name, in_specs=[ANY] * n, out_specs=[ANY] * n, out_shape=[_half_shape(a) for a in arrays],
        scratch_shapes=[pltpu.SemaphoreType.DMA((n,)), pltpu.SemaphoreType.DMA((n,))],
    )(*arrays)


def _add_halves(arrays, received, name):
    n = len(arrays)

    def body(*refs):
        mine, theirs, outs = refs[:n], refs[n:2 * n], refs[2 * n:]
        c = lax.axis_index("c")
        for m_, t_, o_ in zip(mine, theirs, outs):
            if len(m_.shape) == 3:
                h = m_.shape[1] // 2
                o_[0] = (m_[0, _ds(c * h, h), :].astype(F32) + t_[0].astype(F32)).astype(o_.dtype)
            else:
                o_[...] = m_[...] + t_[...]

    def spec(shape):
        if len(shape) == 3:
            return pl.BlockSpec((1,) + tuple(shape[1:]), lambda k: (k, 0, 0))
        return pl.BlockSpec(tuple(shape), lambda k: (0, 0))

    halves = [_half_shape(a) for a in arrays]
    return pl.pallas_call(
        body, grid=(NCHIP,), name=name,
        in_specs=[spec(a.shape) for a in arrays] + [spec(h.shape) for h in halves],
        out_specs=[spec(h.shape) for h in halves], out_shape=halves,
        compiler_params=_params(("arbitrary",)),
    )(*arrays, *received)


def _chip_exchange_copies(srcs, dsts, send, recv):
    x, y, c = _pos()
    me = 2 * x + y
    pairs = []
    for a in range(len(srcs)):
        for j, (fx, fy) in enumerate(CHIP_FLIPS):
            px, py = _flip(x, fx), _flip(y, fy)
            peer = 2 * px + py
            k = len(CHIP_FLIPS) * a + j
            out = pltpu.make_async_remote_copy(
                src_ref=srcs[a].at[peer] if len(srcs[a].shape) == 3 else srcs[a], dst_ref=dsts[a].at[me],
                send_sem=send.at[k], recv_sem=recv.at[k], device_id=(px, py, c), device_id_type=MESH)
            got = dsts[a].at[peer]
            arrival = pltpu.make_async_remote_copy(
                src_ref=got, dst_ref=got, send_sem=send.at[k], recv_sem=recv.at[k],
                device_id=(px, py, c), device_id_type=MESH)
            pairs.append((out, arrival))
    return pairs


def _sum_chips(ri, ro, rs, pi, po, ps, where):
    def body(w_ref, ri_ref, ro_ref, rs_ref, pi_ref, po_ref, ps_ref, gi_ref, go_ref, gs_ref, g5_ref, loss_ref,
             acc_i, acc_o, acc_s):
        k = pl.program_id(0)
        accs = (acc_i, acc_o, acc_s)

        @pl.when(k == 0)
        def _():
            for acc in accs:
                acc[...] = jnp.zeros_like(acc)

        @pl.when(k == w_ref[0])
        def _():
            for acc, val in zip(accs, (pi_ref[0], po_ref[0], ps_ref[...])):
                acc[...] += val.astype(F32)

        @pl.when(k != w_ref[0])
        def _():
            for acc, ref in zip(accs, (ri_ref, ro_ref, rs_ref)):
                acc[...] += ref[0].astype(F32)

        @pl.when(k == NCHIP - 1)
        def _():
            gi_ref[0] = acc_i[...]
            go_ref[0] = acc_o[...]
            gs_ref[...] = acc_s[...]
            g5_ref[...] = jnp.zeros_like(g5_ref)
            for i, row in enumerate((ROW_CONV_B, ROW_LN_G, ROW_LN_B, ROW_FINAL_G)):
                g5_ref[i + 1:i + 2, :] = acc_s[row:row + 1, :]
            loss = jnp.sum(acc_s[ROW_LOSS:ROW_LOSS + 1, :], axis=1, keepdims=True)
            loss_ref[...] = jnp.broadcast_to(loss, loss_ref.shape)

    def sent(k, w):
        return jnp.where(k == w[0], (k + 1) % NCHIP, k)

    hi, ho = D // 2, WOUT_SHARD // 2
    const = lambda shape: pl.BlockSpec(shape, lambda k, w: (0,) * len(shape))
    grid_spec = pltpu.PrefetchScalarGridSpec(
        num_scalar_prefetch=1, grid=(NCHIP,),
        in_specs=[pl.BlockSpec((1, hi, CHUNK), lambda k, w: (sent(k, w), 0, 0)),
                  pl.BlockSpec((1, ho, D), lambda k, w: (sent(k, w), 0, 0)),
                  pl.BlockSpec((1, SMALL_ROWS, D), lambda k, w: (sent(k, w), 0, 0)),
                  pl.BlockSpec((1, hi, CHUNK), lambda k, w: (w[0], 0, 0)),
                  pl.BlockSpec((1, ho, D), lambda k, w: (w[0], 0, 0)),
                  const((SMALL_ROWS, D))],
        out_specs=[pl.BlockSpec((1, hi, CHUNK), lambda k, w: (w[1], 0, 0)),
                   pl.BlockSpec((1, ho, D), lambda k, w: (w[1], 0, 0)),
                   const((SMALL_ROWS, D)), const((8, D)), const((8, LANES))],
        scratch_shapes=[pltpu.VMEM((hi, CHUNK), F32), pltpu.VMEM((ho, D), F32), pltpu.VMEM((SMALL_ROWS, D), F32)])
    return pl.pallas_call(
        body, grid_spec=grid_spec, name="sum_chips",
        out_shape=[jax.ShapeDtypeStruct((2, hi, CHUNK), F32), jax.ShapeDtypeStruct((2, ho, D), F32),
                   jax.ShapeDtypeStruct((SMALL_ROWS, D), F32), jax.ShapeDtypeStruct((8, D), F32),
                   jax.ShapeDtypeStruct((8, LANES), F32)],
        compiler_params=_params(("arbitrary",)),
    )(where, ri, ro, rs, pi, po, ps)


def _exchange_results(gi2, go2, st):
    flips = [(fx, fy, fc) for fx in (0, 1) for fy in (0, 1) for fc in (0, 1)][1:]

    def body(_gi, _go, st_ref, gi_ref, go_ref, all_ref, send, recv, lsem):
        x, y, c = _pos()
        sib = (x, y, 1 - c)

        def half(k, ref, slot):
            return pltpu.make_async_remote_copy(src_ref=ref.at[slot], dst_ref=ref.at[slot], send_sem=send.at[k],
                                                recv_sem=recv.at[k], device_id=sib, device_id_type=MESH)

        def stat(k, src, slot, dev):
            return pltpu.make_async_remote_copy(src_ref=src, dst_ref=all_ref.at[slot], send_sem=send.at[k],
                                                recv_sem=recv.at[k], device_id=dev, device_id_type=MESH)

        mine = pltpu.make_async_copy(st_ref, all_ref.at[4 * x + 2 * y + c], lsem)
        mine.start()
        sends = [half(k, ref, c) for k, ref in enumerate((gi_ref, go_ref))]
        peers = [(_flip(x, fx), _flip(y, fy), _flip(c, fc)) for fx, fy, fc in flips]
        sends += [stat(2 + k, st_ref, 4 * x + 2 * y + c, dev) for k, dev in enumerate(peers)]
        for cp in sends:
            cp.start()
        for k, ref in enumerate((gi_ref, go_ref)):
            half(k, ref, 1 - c).wait_recv()
        for k, (px, py, pc) in enumerate(peers):
            slot = 4 * px + 2 * py + pc
            stat(2 + k, all_ref.at[slot], slot, (px, py, pc)).wait_recv()
        for cp in sends:
            cp.wait_send()
        mine.wait()

    n = 2 + len(flips)
    return pl.pallas_call(
        body, name="exchange_results",
        in_specs=[ANY, ANY, ANY], out_specs=[ANY, ANY, ANY], input_output_aliases={0: 0, 1: 1},
        out_shape=[jax.ShapeDtypeStruct((2, D // 2, CHUNK), F32), jax.ShapeDtypeStruct((2, WOUT_SHARD // 2, D), F32),
                   jax.ShapeDtypeStruct((NDEV, 8, D), F32)],
        scratch_shapes=[pltpu.SemaphoreType.DMA((n,)), pltpu.SemaphoreType.DMA((n,)), pltpu.SemaphoreType.DMA],
    )(gi2, go2, st)


def _adamw_math(w, g, m, v):
    m2 = ADAM_B1 * m + (1.0 - ADAM_B1) * g
    v2 = ADAM_B2 * v + (1.0 - ADAM_B2) * (g * g)
    m_hat = m2 / (1.0 - ADAM_B1 ** ADAM_STEP)
    v_hat = v2 / (1.0 - ADAM_B2 ** ADAM_STEP)
    delta = -ADAM_LR * (m_hat / (jnp.sqrt(v_hat) + ADAM_EPS) + ADAM_WD * w)
    return delta, m2, v2


def _adamw(w, g, m, v, name):
    rows, cols = w.shape
    tm = 256 if rows % 256 == 0 else rows

    def body(w_ref, g_ref, m_ref, v_ref, d_ref, m2_ref, v2_ref):
        d_ref[...], m2_ref[...], v2_ref[...] = _adamw_math(w_ref[...], g_ref[...], m_ref[...], v_ref[...])

    shape = jax.ShapeDtypeStruct(w.shape, F32)
    return pl.pallas_call(
        body, grid=(rows // tm,), name=name,
        in_specs=[_rows(tm, cols)] * 4, out_specs=[_rows(tm, cols)] * 3, out_shape=[shape] * 3,
        compiler_params=_params(("arbitrary",)),
    )(w, g, m, v)


def _adamw_vectors(g5, first_parts, ws, ms, vs):
    n = len(ws)

    def body(g_ref, parts_ref, *refs):
        ins, g0_ref, outs = refs[:3 * n], refs[3 * n], refs[3 * n + 1:]
        g0 = parts_ref[0, 0:1, :]
        for dev in range(1, NDEV):
            g0 = g0 + parts_ref[dev, 0:1, :]
        g0_ref[...] = g0
        for i in range(n):
            g = g0 if i == 0 else g_ref[i:i + 1, :]
            res = _adamw_math(ins[i][...], g, ins[n + i][...], ins[2 * n + i][...])
            for kind in range(3):
                outs[kind * n + i][...] = res[kind]

    shape = jax.ShapeDtypeStruct((1, D), F32)
    return pl.pallas_call(body, name="adamw_vectors", out_shape=[shape] * (1 + 3 * n), compiler_params=_params())(
        g5, first_parts, *ws, *ms, *vs)


def kernel(x, norm_g, w_in, conv_w, conv_b, conv_ln_g, conv_ln_b, w_out, final_norm_g, loss_target, m_norm_g, m_w_in, m_conv_w, m_conv_b, m_conv_ln_g, m_conv_ln_b, m_w_out, m_final_norm_g, v_norm_g, v_w_in, v_conv_w, v_conv_b, v_conv_ln_g, v_conv_ln_b, v_w_out, v_final_norm_g):
    chip = 2 * lax.axis_index("x") + lax.axis_index("y")
    where = jnp.stack([chip, lax.axis_index("c")]).astype(jnp.int32)
    taps_shard = jnp.pad(conv_w[0], ((0, HALO - CONV_K), (0, 0)))
    wi_full, wo_full, cw_full = _place_shards(w_in[0], w_out[0], taps_shard, where)
    wi_full = _gather_w_in(wi_full)

    gf = final_norm_g[None]
    xb = x[0]
    h, q, k, v, a_gate, c_val, c_glu, c_gate, wo_full, cw_full = _inproj_fwd(xb, norm_g, wi_full, wo_full, cw_full)
    tables = [_bias_table(d) for d in PATTERNS]
    o, lse, y_att = _attn_fwd(q, k, v, tables, a_gate)
    u, y_conv = _conv_fwd(c_val, c_glu, c_gate, cw_full, conv_b, conv_ln_g, conv_ln_b)
    dx2, dy_att, du, dc_gate, dw_out, st_out = _outproj_loss(
        y_att, y_conv, wo_full, xb, loss_target[0], gf, u, c_gate, conv_ln_g, conv_ln_b)
    dc_val, dc_glu, dconv_w = _conv_bwd_taps(du, c_val, c_glu, cw_full)

    early = [dw_out.reshape(NCHIP, WOUT_SHARD, D), jnp.concatenate([st_out, dconv_w], axis=0)]
    po, ps = _add_halves(early, _exchange_halves(early, "exchange_halves_early"), "add_halves_early")
    do, da_gate, delta, ro, rs = _attn_gate_bwd(dy_att, o, a_gate, _head_sum_selectors(), [po, ps])
    dqs, dkvs = zip(*[_attn_bwd(q, k, v, do, lse, delta, t, d) for t, d in zip(tables, PATTERNS)])

    dproj_pieces = (dqs, dkvs, (da_gate, dc_val, dc_glu, dc_gate))
    late = [_inproj_bwd_w(h, *dproj_pieces)]
    (pi,) = _add_halves(late, _exchange_halves(late, "exchange_halves"), "add_halves")
    grad_x, st_in, ri = _inproj_bwd_x(*dproj_pieces, wi_full, xb, norm_g, dx2, pi)
    gi2, go2, g_small, g5, loss8 = _sum_chips(ri, ro, rs, pi, po, ps, where)
    gi2, go2, norm_g_parts = _exchange_results(gi2, go2, st_in)
    g_w_in = gi2.reshape(D, CHUNK)
    g_w_out = go2.reshape(WOUT_SHARD, D)
    g_taps = lax.dynamic_slice(g_small, (ROW_TAPS, chip * CONVW_SHARD), (CONV_K, CONVW_SHARD))

    d_w_in, m2_w_in, v2_w_in = _adamw(w_in[0], g_w_in, m_w_in[0], v_w_in[0], "adamw_w_in")
    d_w_out, m2_w_out, v2_w_out = _adamw(w_out[0], g_w_out, m_w_out[0], v_w_out[0], "adamw_w_out")
    d_taps, m2_taps, v2_taps = _adamw(conv_w[0], g_taps, m_conv_w[0], v_conv_w[0], "adamw_conv_w")
    g_norm, *vec = _adamw_vectors(
        g5, norm_g_parts,
        (norm_g, conv_b, conv_ln_g, conv_ln_b, gf),
        (m_norm_g, m_conv_b, m_conv_ln_g, m_conv_ln_b, m_final_norm_g[None]),
        (v_norm_g, v_conv_b, v_conv_ln_g, v_conv_ln_b, v_final_norm_g[None]))
    d_vec, m2_vec, v2_vec = vec[0:5], vec[5:10], vec[10:15]

    def weight_order(ng, wi, cw, cb, lg, lb, wo, fg):
        return (ng, wi[None], cw[None], cb, lg, lb, wo[None], fg[0])

    grads = weight_order(g_norm, g_w_in, g_taps, g5[1:2], g5[2:3], g5[3:4], g_w_out, g5[4:5])
    deltas = weight_order(d_vec[0], d_w_in, d_taps, d_vec[1], d_vec[2], d_vec[3], d_w_out, d_vec[4])
    new_m = weight_order(m2_vec[0], m2_w_in, m2_taps, m2_vec[1], m2_vec[2], m2_vec[3], m2_w_out, m2_vec[4])
    new_v = weight_order(v2_vec[0], v2_w_in, v2_taps, v2_vec[1], v2_vec[2], v2_vec[3], v2_w_out, v2_vec[4])
    return (loss8[0, 0], grad_x[None], *grads, *deltas, *new_m, *new_v)
```

```python
import jax
import jax.numpy as jnp
from jax import lax
from jax.experimental import pallas as pl
from jax.experimental.pallas import tpu as pltpu

F32 = jnp.float32
BF16 = jnp.bfloat16

S = 4096
D = 1024
LANES = 128
HD = 64
NKV = 4
GQ = 4
KVW = NKV * HD
NCOL = 5632
CONV_K = 31
HALO = 32
BLK = 128
PATTERNS = (1, 4, 16)
NORM_EPS = 1e-6
LN_EPS = 1e-5
NEG = -1e30
OFF_Q, OFF_K, OFF_AG, OFF_CV, OFF_CG, OFF_CGATE = 0, 1024, 1536, 2560, 3584, 4608
NCHIP = 4
CHUNK = NCOL // NCHIP
WOUT_ROWS = 2 * D
WOUT_SHARD = WOUT_ROWS // NCHIP
CONVW_SHARD = D // NCHIP

ADAM_LR, ADAM_B1, ADAM_B2, ADAM_EPS, ADAM_WD, ADAM_STEP = 0.001, 0.9, 0.999, 1e-08, 0.01, 10

VMEM_LIMIT = 56 * 1024 * 1024


def _params(sem=None, vmem=VMEM_LIMIT):
    return pltpu.CompilerParams(dimension_semantics=sem, vmem_limit_bytes=vmem)


def _sigmoid(a):
    return 0.5 * jnp.tanh(0.5 * a) + 0.5


def _rows(tm, width):
    return pl.BlockSpec((tm, width), lambda i: (i, 0))


def _slabs(n):
    return jax.ShapeDtypeStruct((n, S, LANES), F32)


def _slab_rows(n, tm):
    return pl.BlockSpec((n, tm, LANES), lambda i: (0, i, 0))


def _resident(shape):
    return pl.BlockSpec(shape, lambda *_: (0,) * len(shape), pipeline_mode=pl.Buffered(1))


def _dot(a, b):
    return jnp.dot(a, b, preferred_element_type=F32)


def _dot_nt(a, b):
    return lax.dot_general(a, b, (((1,), (1,)), ((), ())), preferred_element_type=F32)


def _dot_tn(a, b):
    return lax.dot_general(a, b, (((0,), (0,)), ((), ())), preferred_element_type=F32)


def _inproj_fwd(x, g1, w_bf, wo_full, cw_full):
    tm = 512
    steps = S // tm

    def body(x_ref, g_ref, w_ref, _wo, _cw, h_ref, q_ref, k_ref, v_ref, ag_ref, cv_ref, cg_ref, cgate_ref,
             wo_ref, cw_ref, send, recv):
        i = pl.program_id(0)
        stages = _gather_stages([(W_OUT, wo_ref), (TAPS, cw_ref)], send, recv)
        for stage, step in zip(stages[:3], (0, steps // 2 - 1, steps - 2)):
            pl.when(i == step)(stage)
        xt = x_ref[...]
        r = lax.rsqrt(jnp.mean(xt * xt, axis=-1, keepdims=True) + NORM_EPS)
        h = (xt * r * g_ref[...]).astype(BF16)
        h_ref[...] = h
        q = _dot(h, w_ref[:, OFF_Q:OFF_Q + D]) * (HD ** -0.5)
        kv = _dot(h, w_ref[:, OFF_K:OFF_K + 2 * KVW])
        for sl in range(D // LANES):
            q_ref[sl] = q[:, sl * LANES:(sl + 1) * LANES]
        for sl in range(KVW // LANES):
            k_ref[sl] = kv[:, sl * LANES:(sl + 1) * LANES]
            v_ref[sl] = kv[:, KVW + sl * LANES:KVW + (sl + 1) * LANES]
        ag_ref[...] = _dot(h, w_ref[:, OFF_AG:OFF_AG + D])
        cv_ref[...] = _dot(h, w_ref[:, OFF_CV:OFF_CV + D])
        cg_ref[...] = _dot(h, w_ref[:, OFF_CG:OFF_CG + D])
        cgate_ref[...] = _dot(h, w_ref[:, OFF_CGATE:OFF_CGATE + D])
        pl.when(i == steps - 1)(stages[3])

    big = jax.ShapeDtypeStruct((S, D), F32)
    return pl.pallas_call(
        body, grid=(steps,), name="inproj_fwd",
        in_specs=[_rows(tm, D), _resident((1, D)), _resident((D, NCOL)), ANY, ANY],
        out_specs=[_rows(tm, D), _slab_rows(D // LANES, tm), _slab_rows(KVW // LANES, tm), _slab_rows(KVW // LANES, tm),
                   _rows(tm, D), _rows(tm, D), _rows(tm, D), _rows(tm, D), ANY, ANY],
        out_shape=[jax.ShapeDtypeStruct((S, D), BF16), _slabs(D // LANES), _slabs(KVW // LANES), _slabs(KVW // LANES),
                   big, big, big, big,
                   jax.ShapeDtypeStruct((WOUT_ROWS, D), BF16), jax.ShapeDtypeStruct((HALO, D), F32)],
        input_output_aliases={3: 8, 4: 9},
        scratch_shapes=_gather_sems(2),
        compiler_params=_params(("arbitrary",)),
    )(x, g1, w_bf, wo_full, cw_full)


def _bias_table(d):
    h = jnp.arange(NKV * GQ, dtype=F32)
    slopes = jnp.exp2(-8.0 * (h + 1.0) / (NKV * GQ))
    qi = jnp.arange(BLK)[:, None]
    kj = jnp.arange(2 * BLK)[None, :]
    dist = BLK + qi - kj
    window = (dist >= 0) & (dist <= BLK)
    bias = -slopes[:, None, None] * (dist * d).astype(F32)[None]
    has_prev = jnp.stack([jnp.broadcast_to(kj >= BLK, (BLK, 2 * BLK)), jnp.ones((BLK, 2 * BLK), bool)])
    valid = window[None] & has_prev
    tab = jnp.where(valid[:, None], bias[None], NEG)
    return tab.reshape(2, NKV, GQ * BLK, 2 * BLK)


def _sub_rows(start, d, align=BLK):
    if d == 1:
        return pl.ds(pl.multiple_of(start, align), BLK)
    return pl.ds(start, BLK, stride=d)


CHUNK_ROWS = 2048
BLOCKS_PER_CHUNK = CHUNK_ROWS // BLK


def _low_lanes(rows=BLK):
    return lax.broadcasted_iota(jnp.int32, (rows, LANES), 1) < HD


def _block_start(idx, d):
    shift = d.bit_length() - 1
    b, r = lax.shift_right_logical(idx, shift), lax.bitwise_and(idx, d - 1)
    start = b * (BLK * d) + r
    return b, start, jnp.maximum(start - BLK * d, r)


def _stack_heads(ref, rows):
    low = _low_lanes()
    t0, t1 = ref[0, rows, :], ref[1, rows, :]
    return jnp.concatenate([jnp.where(low, t0, 0.0), jnp.where(low, 0.0, t0),
                            jnp.where(low, t1, 0.0), jnp.where(low, 0.0, t1)], axis=0).astype(BF16)


def _unstack_heads(dup):
    low = _low_lanes()
    return (jnp.where(low, dup[0:BLK], dup[BLK:2 * BLK]), jnp.where(low, dup[2 * BLK:3 * BLK], dup[3 * BLK:4 * BLK]))


def _kv_dup(ref, prow, rows, odd):
    t = jnp.concatenate([ref[0, prow, :], ref[0, rows, :]], axis=0)
    swapped = pltpu.roll(t, HD, axis=1)
    keep = jnp.logical_xor(_low_lanes(2 * BLK), odd)
    return jnp.where(keep, t, swapped).astype(BF16)


PIECES = 3


def _by_head(tiles):
    lane = lax.broadcasted_iota(jnp.int32, tiles[0].shape, 1)
    out = tiles[0]
    for g in range(1, GQ):
        out = jnp.where(lax.bitwise_and(lane, GQ - 1) == g, tiles[g], out)
    return out


def _minus_in_pieces(x):
    lane = lax.broadcasted_iota(jnp.int32, x.shape, 1)
    hi = (-x).astype(BF16).astype(F32)
    rest = -x - hi
    mid = rest.astype(BF16).astype(F32)
    lo = (rest - mid).astype(BF16).astype(F32)
    return jnp.where(lane < GQ, hi, jnp.where(lane < 2 * GQ, mid, jnp.where(lane < PIECES * GQ, lo, 0.0)))


def _attn_fwd(q, k, v, tables, a_gate):
    tm = 256
    width = GQ * HD

    lane_out = jnp.arange(LANES)[None, :] // HD
    spread_sel = jnp.stack([jnp.arange(LANES)[:, None] == 2 * half + lane_out for half in range(2)]).astype(BF16)

    def body(q_ref, k_ref, v_ref, b1_ref, b2_ref, b3_ref, ag_ref, sel_ref, o_ref, lse_ref, y_ref, op, lp):
        odd = pl.program_id(0) % 2 == 1
        chunk = pl.program_id(1)
        ones = jnp.ones((2 * BLK, LANES), BF16)

        for pat, (d, b_ref) in enumerate(zip(PATTERNS, (b1_ref, b2_ref, b3_ref))):
            def block(idx, carry, pat=pat, d=d, b_ref=b_ref):
                b, start, pstart = _block_start(chunk * BLOCKS_PER_CHUNK + idx, d)
                rows, prow = _sub_rows(start, d), _sub_rows(pstart, d)
                mine = _sub_rows(start - chunk * CHUNK_ROWS, d)
                qs = _stack_heads(q_ref, mine)
                kw = _kv_dup(k_ref, prow, rows, odd)
                vw = _kv_dup(v_ref, prow, rows, odd)
                s = _dot_nt(qs, kw) + b_ref[jnp.minimum(b, 1), 0]
                m = jnp.max(s, axis=1, keepdims=True)
                p = jnp.exp(s - m).astype(BF16)
                ol = _dot(p, jnp.concatenate([vw, ones], axis=1))
                l = ol[:, LANES:]
                op[pat, 0, mine, :], op[pat, 1, mine, :] = _unstack_heads(ol[:, :LANES] / l)
                lp[pat, mine, :] = _by_head([(m + jnp.log(l))[g * BLK:(g + 1) * BLK] for g in range(GQ)])
                return carry

            lax.fori_loop(0, BLOCKS_PER_CHUNK, block, 0, unroll=2)

        def mix(t, carry):
            r = pl.ds(pl.multiple_of(t * tm, tm), tm)
            a, b, c = lp[0, r, :], lp[1, r, :], lp[2, r, :]
            m = jnp.maximum(jnp.maximum(a, b), c)
            ea, eb, ec = jnp.exp(a - m), jnp.exp(b - m), jnp.exp(c - m)
            den = ea + eb + ec
            lse_ref[0, r, :] = _minus_in_pieces(m + jnp.log(den))
            inv = 1.0 / den
            for half in range(2):
                def spread(w):
                    hi = w.astype(BF16)
                    lo = (w - hi.astype(F32)).astype(BF16)
                    return _dot(hi, sel_ref[half]) + _dot(lo, sel_ref[half])

                o = (spread(ea * inv) * op[0, half, r, :] + spread(eb * inv) * op[1, half, r, :]
                     + spread(ec * inv) * op[2, half, r, :])
                o_ref[half, r, :] = o
                cols = slice(half * LANES, (half + 1) * LANES)
                ag = ag_ref[r, cols]
                y_ref[r, cols] = (o * (ag * _sigmoid(ag))).astype(BF16)
            return carry

        lax.fori_loop(0, CHUNK_ROWS // tm, mix, 0, unroll=2)

    q_like = pl.BlockSpec((2, CHUNK_ROWS, LANES), lambda j, c: (j, c, 0))
    per_kv = pl.BlockSpec((1, CHUNK_ROWS, LANES), lambda j, c: (j, c, 0))
    kv = pl.BlockSpec((1, S, LANES), lambda j, c: (j // 2, 0, 0))
    bias_spec = pl.BlockSpec((2, 1, GQ * BLK, 2 * BLK), lambda j, c: (0, j, 0, 0))
    group_cols = pl.BlockSpec((CHUNK_ROWS, width), lambda j, c: (c, j))
    return pl.pallas_call(
        body, grid=(NKV, S // CHUNK_ROWS), name="attn_fwd",
        in_specs=[q_like, kv, kv, bias_spec, bias_spec, bias_spec, group_cols,
                  pl.BlockSpec((2, LANES, LANES), lambda j, c: (0, 0, 0))],
        out_specs=[q_like, per_kv, group_cols],
        out_shape=[_slabs(D // LANES), _slabs(NKV), jax.ShapeDtypeStruct((S, D), BF16)],
        scratch_shapes=[pltpu.VMEM((len(PATTERNS), 2, CHUNK_ROWS, LANES), F32),
                        pltpu.VMEM((len(PATTERNS), CHUNK_ROWS, LANES), F32)],
        compiler_params=_params(("arbitrary", "arbitrary")),
    )(q, k, v, *tables, a_gate, spread_sel)


def _head_sum_selectors():
    lane_in = jnp.arange(LANES)[:, None] // HD
    return jnp.stack([jnp.broadcast_to(lane_in == h, (LANES, LANES)) for h in range(2)]).astype(BF16)


def _attn_gate_bwd(dy_att, o, a_gate, selectors, chip_sums):
    tm = 256
    last = S // tm - 1
    landing, sems = _exchange_results_of(chip_sums)
    n_sums = len(chip_sums)

    def body(dy_ref, o_ref, ag_ref, e_ref, *refs):
        sums, (do_ref, dag_ref, delta_ref), refs = refs[:n_sums], refs[n_sums:n_sums + 3], refs[n_sums + 3:]
        landed, (send, recv) = refs[:n_sums], refs[n_sums:]
        i = pl.program_id(0)
        copies = _chip_exchange_copies(sums, landed, send, recv)
        _start_exchange(copies, i == 0)
        for j in range(NKV):
            deltas = []
            for sl in (2 * j, 2 * j + 1):
                cols = slice(sl * LANES, (sl + 1) * LANES)
                dy, ag, o_ = dy_ref[:, cols], ag_ref[:, cols], o_ref[sl]
                sg = _sigmoid(ag)
                do = dy * (ag * sg)
                do_ref[sl] = do
                dag_ref[:, cols] = (dy * o_ * (sg * (1.0 + ag * (1.0 - sg)))).astype(BF16)
                prod = do * o_
                hi = prod.astype(BF16)
                lo = (prod - hi.astype(F32)).astype(BF16)
                deltas += [_dot(hi, e_ref[h]) + _dot(lo, e_ref[h]) for h in range(2)]
            delta_ref[j] = _minus_in_pieces(_by_head(deltas))
        _finish_exchange(copies, i == last)

    return pl.pallas_call(
        body, grid=(S // tm,), name="attn_gate_bwd",
        in_specs=[_rows(tm, D), _slab_rows(D // LANES, tm), _rows(tm, D), _resident((2, LANES, LANES))] + [ANY] * n_sums,
        out_specs=[_slab_rows(D // LANES, tm), _rows(tm, D), _slab_rows(NKV, tm)] + [ANY] * n_sums,
        out_shape=[_slabs(D // LANES), jax.ShapeDtypeStruct((S, D), BF16), _slabs(NKV)] + landing,
        scratch_shapes=sems,
        compiler_params=_params(("arbitrary",)),
    )(dy_att, o, a_gate, selectors, *chip_sums)


def _own_pieces(tile):
    lane = lax.broadcasted_iota(jnp.int32, tile.shape, 1)
    head = jnp.where(lane < PIECES * GQ, lax.bitwise_and(lane, GQ - 1), -1)
    return jnp.concatenate([jnp.where(head == g, tile, 0.0) for g in range(GQ)], axis=0).astype(BF16)


def _attn_bwd(q, k, v, do, lse, delta, bias, d):
    apart = d * BLK == CHUNK_ROWS
    pitch, kv_pitch = BLK + SUBLANES, S // d + SUBLANES

    def pull_apart(src, dst, groups, pitch, back=False):
        def group(g, carry):
            for n in range(src.shape[0]):
                for half in range(d // SUBLANES):
                    together = (n, pl.ds(pl.multiple_of(g * d + half * SUBLANES, SUBLANES), SUBLANES), slice(None))
                    spread = (n, pl.ds(half * SUBLANES * pitch + g, SUBLANES, stride=pitch), slice(None))
                    if back:
                        src[together] = dst[spread]
                    else:
                        dst[spread] = src[together]
            return carry

        lax.fori_loop(0, groups, group, 0, unroll=8)

    def body(q_ref, do_ref, l_ref, dl_ref, k_ref, v_ref, b_ref, dq_ref, dkv_ref, acc, *copies):
        odd = pl.program_id(0) % 2 == 1
        chunk = pl.program_id(1)
        dq_out = dq_ref
        if apart:
            qd, dod, ld, dld, kd, vd, dq_out = copies

            @pl.when(jnp.logical_and(chunk == 0, jnp.logical_not(odd)))
            def _():
                pull_apart(k_ref, kd, S // d, kv_pitch)
                pull_apart(v_ref, vd, S // d, kv_pitch)

            for src, dst in ((q_ref, qd), (do_ref, dod), (l_ref, ld), (dl_ref, dld)):
                pull_apart(src, dst, BLK, pitch)
            q_ref, do_ref, l_ref, dl_ref, k_ref, v_ref = qd, dod, ld, dld, kd, vd
        ones = (lax.broadcasted_iota(jnp.int32, (2 * BLK, LANES), 1) < PIECES * GQ).astype(BF16)

        def in_acc(block_idx):
            return pl.ds(pl.multiple_of(block_idx * BLK, BLK), BLK)

        @pl.when(chunk == 0)
        def _():
            acc[...] = jnp.zeros_like(acc)

        def block(idx, carry):
            idx = chunk * BLOCKS_PER_CHUNK + idx
            b, start, pstart = _block_start(idx, d)
            rows, prow = _sub_rows(start, d), _sub_rows(pstart, d)
            mine = _sub_rows(start - chunk * CHUNK_ROWS, d)
            if apart:
                r = idx - b * d
                rows = _sub_rows(r * kv_pitch + b * BLK, 1, SUBLANES)
                prow = _sub_rows(r * kv_pitch + jnp.maximum(b - 1, 0) * BLK, 1, SUBLANES)
                mine = _sub_rows(r * pitch, 1, SUBLANES)
            qs = _stack_heads(q_ref, mine)
            dos = _stack_heads(do_ref, mine)
            kw = _kv_dup(k_ref, prow, rows, odd)
            vw = _kv_dup(v_ref, prow, rows, odd)
            s = _dot_nt(jnp.concatenate([qs, _own_pieces(l_ref[0, mine, :])], axis=1),
                        jnp.concatenate([kw, ones], axis=1)) + b_ref[jnp.minimum(b, 1), 0]
            p = jnp.exp(s)
            dv2 = _dot_tn(p.astype(BF16), dos)
            dp = _dot_nt(jnp.concatenate([dos, _own_pieces(dl_ref[0, mine, :])], axis=1),
                         jnp.concatenate([vw, ones], axis=1))
            ds = (p * dp).astype(BF16)
            dq_out[0, mine, :], dq_out[1, mine, :] = _unstack_heads(_dot(ds, kw))
            dk2 = _dot_tn(ds, qs)
            dkv = jnp.where(_low_lanes(2 * BLK), dk2 + pltpu.roll(dk2, HD, axis=1), dv2 + pltpu.roll(dv2, HD, axis=1))
            acc[in_acc(idx), :] = acc[in_acc(idx), :] + dkv[BLK:]
            before = jnp.where(b >= 1, idx - d, idx)
            acc[in_acc(before), :] = acc[in_acc(before), :] + dkv[:BLK]
            return carry

        lax.fori_loop(0, BLOCKS_PER_CHUNK, block, 0, unroll=16)
        if apart:
            pull_apart(dq_ref, dq_out, BLK, pitch, back=True)

        @pl.when(chunk == S // CHUNK_ROWS - 1)
        def _():
            def place(idx, carry):
                _, start, _ = _block_start(idx, d)
                dkv_ref[0, _sub_rows(start, d), :] = acc[in_acc(idx), :]
                return carry

            lax.fori_loop(0, S // BLK, place, 0, unroll=4)

    q_like = pl.BlockSpec((2, CHUNK_ROWS, LANES), lambda j, c: (j, c, 0))
    pieces = pl.BlockSpec((1, CHUNK_ROWS, LANES), lambda j, c: (j, c, 0))
    kv = pl.BlockSpec((1, S, LANES), lambda j, c: (j // 2, 0, 0))
    per_kv = pl.BlockSpec((1, S, LANES), lambda j, c: (j, 0, 0))
    bias_spec = pl.BlockSpec((2, 1, GQ * BLK, 2 * BLK), lambda j, c: (0, j, 0, 0))
    return pl.pallas_call(
        body, grid=(NKV, S // CHUNK_ROWS), name=f"attn_bwd_d{d}",
        in_specs=[q_like, q_like, pieces, pieces, kv, kv, bias_spec],
        out_specs=[q_like, per_kv],
        out_shape=[_slabs(D // LANES), _slabs(NKV)],
        scratch_shapes=[pltpu.VMEM((S, LANES), F32)] + apart * [
            pltpu.VMEM((n, d * rows, LANES), F32)
            for n, rows in ((2, pitch), (2, pitch), (1, pitch), (1, pitch), (1, kv_pitch), (1, kv_pitch), (2, pitch))],
        compiler_params=_params(("arbitrary", "arbitrary")),
    )(q, do, lse, delta, k, v, bias)


CONV_T = 256


def _halo_before(i):
    return (jnp.maximum(i * (CONV_T // HALO) - 1, 0), 0)


def _halo_after(i):
    return (jnp.minimum((i + 1) * (CONV_T // HALO), S // HALO - 1), 0)


SUBLANES = 8
NCH = D // LANES
GROUP = SUBLANES * SUBLANES


def _comb(ref, cb, base):
    return ref[cb, pl.ds(base, SUBLANES, stride=SUBLANES), :]


def _taps(w_ref, cols):
    return [jnp.broadcast_to(w_ref[j:j + 1, cols], (SUBLANES, LANES)) for j in range(CONV_K)]


def _conv_fwd(c_val, c_glu, c_gate, conv_w, conv_b, ln_g, ln_b):
    T = CONV_T

    def body(cv_ref, cg_ref, cvh_ref, cgh_ref, gate_ref, w_ref, b_ref, lg_ref, lb_ref, u_ref, y_ref, win, us):
        i = pl.program_id(0)
        for cb in range(NCH):
            cols = slice(cb * LANES, (cb + 1) * LANES)
            win[cb, HALO:HALO + T, :] = cv_ref[:, cols] * _sigmoid(cg_ref[:, cols])
            win[cb, 0:HALO, :] = jnp.where(i > 0, cvh_ref[:, cols] * _sigmoid(cgh_ref[:, cols]), 0.0)
        for cb in range(NCH):
            cols = slice(cb * LANES, (cb + 1) * LANES)
            taps = _taps(w_ref, cols)
            bias = jnp.broadcast_to(b_ref[:, cols], (SUBLANES, LANES))

            def group(g, carry):
                for b in range(SUBLANES):
                    base = g * GROUP + b
                    acc = bias
                    for j in range(CONV_K):
                        acc = acc + taps[j] * _comb(win, cb, base + (HALO - (CONV_K - 1) + j))
                    us[cb, pl.ds(base, SUBLANES, stride=SUBLANES), :] = acc
                return carry

            lax.fori_loop(0, T // GROUP, group, 0, unroll=2)
        total = us[0]
        for cb in range(1, NCH):
            total = total + us[cb]
        mu = jnp.sum(total, axis=-1, keepdims=True) * (1.0 / D)
        sq = jnp.zeros((T, LANES), F32)
        for cb in range(NCH):
            uc = us[cb] - mu
            sq = sq + uc * uc
        rstd = lax.rsqrt(jnp.sum(sq, axis=-1, keepdims=True) * (1.0 / D) + LN_EPS)
        for cb in range(NCH):
            cols = slice(cb * LANES, (cb + 1) * LANES)
            u = us[cb]
            u_ref[:, cols] = u
            nrm = (u - mu) * rstd * lg_ref[:, cols] + lb_ref[:, cols]
            gate = gate_ref[:, cols]
            y_ref[:, cols] = (nrm * _sigmoid(nrm) * (gate * _sigmoid(gate))).astype(BF16)

    halo = pl.BlockSpec((HALO, D), _halo_before)
    return pl.pallas_call(
        body, grid=(S // T,), name="conv_fwd",
        in_specs=[_rows(T, D), _rows(T, D), halo, halo, _rows(T, D),
                  _resident((HALO, D)), _resident((1, D)), _resident((1, D)), _resident((1, D))],
        out_specs=[_rows(T, D), _rows(T, D)],
        out_shape=[jax.ShapeDtypeStruct((S, D), F32), jax.ShapeDtypeStruct((S, D), BF16)],
        scratch_shapes=[pltpu.VMEM((NCH, T + HALO, LANES), F32), pltpu.VMEM((NCH, T, LANES), F32)],
        compiler_params=_params(("arbitrary",)),
    )(c_val, c_glu, c_val, c_glu, c_gate, conv_w, conv_b, ln_g, ln_b)


def _conv_bwd_taps(du, c_val, c_glu, conv_w):
    T = CONV_T
    last = S // T - 1

    def body(du_ref, dua_ref, cv_ref, cg_ref, cvh_ref, cgh_ref, w_ref, dcv_ref, dcg_ref, dw_ref,
             hwin, dwin, dhs, dw_acc):
        i = pl.program_id(0)

        @pl.when(i == 0)
        def _():
            dw_acc[...] = jnp.zeros_like(dw_acc)

        for cb in range(NCH):
            cols = slice(cb * LANES, (cb + 1) * LANES)
            hwin[cb, HALO:HALO + T, :] = cv_ref[:, cols] * _sigmoid(cg_ref[:, cols])
            hwin[cb, 0:HALO, :] = jnp.where(i > 0, cvh_ref[:, cols] * _sigmoid(cgh_ref[:, cols]), 0.0)
            dwin[cb, 0:T, :] = du_ref[:, cols]
            dwin[cb, T:T + HALO, :] = jnp.where(i < last, dua_ref[:, cols], 0.0)
        for cb in range(NCH):
            cols = slice(cb * LANES, (cb + 1) * LANES)
            taps = _taps(w_ref, cols)

            def group_dh(g, carry):
                for b in range(SUBLANES):
                    base = g * GROUP + b
                    acc = jnp.zeros((SUBLANES, LANES), F32)
                    for j in range(CONV_K):
                        acc = acc + taps[j] * _comb(dwin, cb, base + (CONV_K - 1 - j))
                    dhs[cb, pl.ds(base, SUBLANES, stride=SUBLANES), :] = acc
                return carry

            lax.fori_loop(0, T // GROUP, group_dh, 0, unroll=2)

            def group_dw(g, sums):
                for b in range(SUBLANES):
                    base = g * GROUP + b
                    d = _comb(dwin, cb, base)
                    sums = tuple(sums[j] + d * _comb(hwin, cb, base + (HALO - (CONV_K - 1) + j))
                                 for j in range(CONV_K))
                return sums

            sums = lax.fori_loop(0, T // GROUP, group_dw, tuple(dw_acc[j, :, cols] for j in range(CONV_K)))
            for j in range(CONV_K):
                dw_acc[j, :, cols] = sums[j]
            dh = dhs[cb]
            cv, sg = cv_ref[:, cols], _sigmoid(cg_ref[:, cols])
            dcv_ref[:, cols] = (dh * sg).astype(BF16)
            dcg_ref[:, cols] = (dh * cv * (sg * (1.0 - sg))).astype(BF16)

        @pl.when(i == last)
        def _():
            dw_ref[...] = jnp.zeros_like(dw_ref)
            for j in range(CONV_K):
                dw_ref[j:j + 1, :] = jnp.sum(dw_acc[j], axis=0, keepdims=True)

    before = pl.BlockSpec((HALO, D), _halo_before)
    after = pl.BlockSpec((HALO, D), _halo_after)
    big = jax.ShapeDtypeStruct((S, D), BF16)
    return pl.pallas_call(
        body, grid=(S // T,), name="conv_bwd_taps",
        in_specs=[_rows(T, D), after, _rows(T, D), _rows(T, D), before, before, _resident((HALO, D))],
        out_specs=[_rows(T, D), _rows(T, D), pl.BlockSpec((HALO, D), lambda i: (0, 0))],
        out_shape=[big, big, jax.ShapeDtypeStruct((HALO, D), F32)],
        scratch_shapes=[pltpu.VMEM((NCH, T + HALO, LANES), F32), pltpu.VMEM((NCH, T + HALO, LANES), F32),
                        pltpu.VMEM((NCH, T, LANES), F32), pltpu.VMEM((CONV_K, SUBLANES, D), F32)],
        compiler_params=_params(("arbitrary",)),
    )(du, du, c_val, c_glu, c_val, c_glu, conv_w)


def _outproj_loss(y_att, y_conv, w_out_bf, x, target, gf, u, c_gate, ln_g, ln_b):
    tm = 256

    def body(ya_ref, yc_ref, w_ref, x_ref, t_ref, gf_ref, u_ref, gate_ref, lg_ref, lb_ref,
             dx2_ref, dya_ref, du_ref, dgate_ref, dw_ref, st_ref, acc):
        @pl.when(pl.program_id(0) == 0)
        def _():
            acc[...] = jnp.zeros_like(acc)
            st_ref[...] = jnp.zeros_like(st_ref)

        ya, yc = ya_ref[...], yc_ref[...]
        x2 = x_ref[...] + _dot(ya, w_ref[0:D, :]) + _dot(yc, w_ref[D:2 * D, :])
        r = lax.rsqrt(jnp.mean(x2 * x2, axis=-1, keepdims=True) + NORM_EPS)
        xn = x2 * r
        err = xn * gf_ref[...] - t_ref[...]
        dout = err * (1.0 / D)
        dxn = dout * gf_ref[...]
        dx2 = r * (dxn - xn * jnp.mean(dxn * xn, axis=-1, keepdims=True))
        dx2_ref[...] = dx2
        dx2b = dx2.astype(BF16)
        dya_ref[...] = _dot_nt(dx2b, w_ref[0:D, :])
        dy = _dot_nt(dx2b, w_ref[D:2 * D, :])
        acc[0:D, :] += _dot_tn(ya, dx2b)
        acc[D:2 * D, :] += _dot_tn(yc, dx2b)
        st_ref[ROW_FINAL_G:ROW_FINAL_G + 1, :] += jnp.sum(dout * xn, axis=0, keepdims=True)
        st_ref[ROW_LOSS:ROW_LOSS + 1, :] += jnp.sum(err * err, axis=0, keepdims=True) * (0.5 / D)

        u, gate = u_ref[...], gate_ref[...]
        mu = jnp.mean(u, axis=-1, keepdims=True)
        uc = u - mu
        rstd = lax.rsqrt(jnp.mean(uc * uc, axis=-1, keepdims=True) + LN_EPS)
        z = uc * rstd
        nrm = z * lg_ref[...] + lb_ref[...]
        sn, sg = _sigmoid(nrm), _sigmoid(gate)
        dgate_ref[...] = (dy * (nrm * sn) * (sg * (1.0 + gate * (1.0 - sg)))).astype(BF16)
        dn = dy * (gate * sg) * (sn * (1.0 + nrm * (1.0 - sn)))
        dz = dn * lg_ref[...]
        du = rstd * (dz - jnp.mean(dz, axis=-1, keepdims=True) - z * jnp.mean(dz * z, axis=-1, keepdims=True))
        du_ref[...] = du
        st_ref[ROW_LN_G:ROW_LN_G + 1, :] += jnp.sum(dn * z, axis=0, keepdims=True)
        st_ref[ROW_LN_B:ROW_LN_B + 1, :] += jnp.sum(dn, axis=0, keepdims=True)
        st_ref[ROW_CONV_B:ROW_CONV_B + 1, :] += jnp.sum(du, axis=0, keepdims=True)

        @pl.when(pl.program_id(0) == S // tm - 1)
        def _():
            dw_ref[...] = acc[...].astype(BF16)

    big = jax.ShapeDtypeStruct((S, D), F32)
    vec = _resident((1, D))
    return pl.pallas_call(
        body, grid=(S // tm,), name="outproj_loss",
        in_specs=[_rows(tm, D), _rows(tm, D), _resident((WOUT_ROWS, D)), _rows(tm, D), _rows(tm, D), vec,
                  _rows(tm, D), _rows(tm, D), vec, vec],
        out_specs=[_rows(tm, D), _rows(tm, D), _rows(tm, D), _rows(tm, D),
                   pl.BlockSpec((WOUT_ROWS, D), lambda i: (0, 0)), pl.BlockSpec((8, D), lambda i: (0, 0))],
        out_shape=[big, big, big, jax.ShapeDtypeStruct((S, D), BF16),
                   jax.ShapeDtypeStruct((WOUT_ROWS, D), BF16), jax.ShapeDtypeStruct((8, D), F32)],
        scratch_shapes=[pltpu.VMEM((WOUT_ROWS, D), F32)],
        compiler_params=_params(("arbitrary",)),
    )(y_att, y_conv, w_out_bf, x, target, gf, u, c_gate, ln_g, ln_b)


UNITS_PER_CHUNK = CHUNK // LANES


def _dproj_unit(u, dqs, dkvs, gates, rows):
    if u < OFF_K // LANES:
        return ((dqs[0][u] + dqs[1][u] + dqs[2][u]) * (HD ** -0.5)).astype(BF16)
    if u < OFF_AG // LANES:
        w = u - OFF_K // LANES
        ta, tb = (dkvs[0][j] + dkvs[1][j] + dkvs[2][j] for j in (2 * (w % 2), 2 * (w % 2) + 1))
        low = _low_lanes(rows)
        if w < 2:
            return jnp.where(low, ta, pltpu.roll(tb, HD, axis=1)).astype(BF16)
        return jnp.where(low, pltpu.roll(ta, HD, axis=1), tb).astype(BF16)
    g, sl = divmod(u - OFF_AG // LANES, D // LANES)
    return gates[g][:, sl * LANES:(sl + 1) * LANES]


def _dproj_sources(units, dqs, dkvs, gates, rows):
    use_q = any(u < OFF_K // LANES for u in units)
    use_kv = any(OFF_K // LANES <= u < OFF_AG // LANES for u in units)
    use_g = sorted({(u - OFF_AG // LANES) // (D // LANES) for u in units if u >= OFF_AG // LANES})
    args = (list(dqs) if use_q else []) + (list(dkvs) if use_kv else []) + [gates[g] for g in use_g]
    specs = ([_slab_rows(D // LANES, rows)] * 3 if use_q else []) + ([_slab_rows(NKV, rows)] * 3 if use_kv else []) \
        + [_rows(rows, D)] * len(use_g)

    def pick(refs):
        refs = list(refs)
        q_refs = [refs.pop(0) for _ in range(3)] if use_q else None
        kv_refs = [refs.pop(0) for _ in range(3)] if use_kv else None
        return q_refs, kv_refs, {g: refs.pop(0) for g in use_g}

    return args, specs, pick


def _exchange_results_of(chip_sums):
    n = len(chip_sums) * len(CHIP_FLIPS)
    shapes = [jax.ShapeDtypeStruct((NCHIP,) + tuple(a.shape[1:] if a.ndim == 3 else a.shape), a.dtype)
              for a in chip_sums]
    return shapes, [pltpu.SemaphoreType.DMA((n,)), pltpu.SemaphoreType.DMA((n,))]


def _start_exchange(copies, first_step):
    @pl.when(first_step)
    def _():
        for out, _ in copies:
            out.start()


def _finish_exchange(copies, last_step):
    @pl.when(last_step)
    def _():
        for _, arrival in copies:
            arrival.wait_recv()
        for out, _ in copies:
            out.wait_send()


def _inproj_bwd_x(dqs, dkvs, gates, w_bf, x, g1, dx2, pi):
    tm = 256
    last = S // tm - 1
    units = range(NCOL // LANES)
    pieces, piece_specs, pick = _dproj_sources(units, dqs, dkvs, gates, tm)
    landing, sems = _exchange_results_of([pi])

    def body(*refs):
        piece_refs, refs = refs[:len(pieces)], refs[len(pieces):]
        w_ref, x_ref, g_ref, dx2_ref, pi_ref, gx_ref, st_ref, ri_ref, dp_ref, send, recv = refs
        i = pl.program_id(0)
        copies = _chip_exchange_copies([pi_ref], [ri_ref], send, recv)
        _start_exchange(copies, i == 0)

        @pl.when(i == 0)
        def _():
            st_ref[...] = jnp.zeros_like(st_ref)

        sources = pick(piece_refs)
        for u in units:
            dp_ref[:, u * LANES:(u + 1) * LANES] = _dproj_unit(u, *sources, tm)
        dh = _dot_nt(dp_ref[...], w_ref[...])
        xt = x_ref[...]
        r = lax.rsqrt(jnp.mean(xt * xt, axis=-1, keepdims=True) + NORM_EPS)
        xn = xt * r
        dxn = dh * g_ref[...]
        gx_ref[...] = dx2_ref[...] + r * (dxn - xn * jnp.mean(dxn * xn, axis=-1, keepdims=True))
        st_ref[0:1, :] += jnp.sum(dh * xn, axis=0, keepdims=True)
        _finish_exchange(copies, i == last)

    return pl.pallas_call(
        body, grid=(S // tm,), name="inproj_bwd_x",
        in_specs=piece_specs + [_resident((D, NCOL)), _rows(tm, D), _resident((1, D)), _rows(tm, D), ANY],
        out_specs=[_rows(tm, D), pl.BlockSpec((8, D), lambda i: (0, 0)), ANY],
        out_shape=[jax.ShapeDtypeStruct((S, D), F32), jax.ShapeDtypeStruct((8, D), F32)] + landing,
        scratch_shapes=[pltpu.VMEM((tm, NCOL), BF16)] + sems,
        compiler_params=_params(("arbitrary",)),
    )(*pieces, w_bf, x, g1, dx2, pi)


def _inproj_bwd_w(h, dqs, dkvs, gates):
    out = None
    for k in range(NCHIP):
        units = range(k * UNITS_PER_CHUNK, (k + 1) * UNITS_PER_CHUNK)
        tk = 512 if units[0] < OFF_K // LANES else 1024
        nk = S // tk
        pieces, piece_specs, pick = _dproj_sources(units, dqs, dkvs, gates, tk)
        handed_on = [] if out is None else [out]

        def body(*refs, units=units, pick=pick, n_pieces=len(pieces), n_in=1 + len(pieces) + len(handed_on)):
            h_ref, piece_refs = refs[0], refs[1:1 + n_pieces]
            o_ref, tile, acc = refs[n_in:]
            i = pl.program_id(0)

            @pl.when(i == 0)
            def _():
                acc[...] = jnp.zeros_like(acc)

            sources = pick(piece_refs)
            for n, u in enumerate(units):
                tile[:, n * LANES:(n + 1) * LANES] = _dproj_unit(u, *sources, tk)
            acc[...] += _dot_tn(h_ref[...], tile[...])

            @pl.when(i == nk - 1)
            def _():
                o_ref[0] = acc[...].astype(BF16)

        out = pl.pallas_call(
            body, grid=(nk,), name=f"inproj_bwd_w{k}",
            in_specs=[_rows(tk, D)] + piece_specs + [ANY] * len(handed_on),
            out_specs=pl.BlockSpec((1, D, CHUNK), lambda i, k=k: (k, 0, 0)),
            out_shape=jax.ShapeDtypeStruct((NCHIP, D, CHUNK), BF16),
            input_output_aliases={1 + len(pieces): 0} if handed_on else {},
            scratch_shapes=[pltpu.VMEM((tk, CHUNK), BF16), pltpu.VMEM((D, CHUNK), F32)],
            compiler_params=_params(("arbitrary",)),
        )(h, *pieces, *handed_on)
    return out


ROW_FINAL_G, ROW_LOSS, ROW_LN_G, ROW_LN_B, ROW_CONV_B, ROW_TAPS = 0, 1, 2, 3, 4, 8
SMALL_ROWS = 8 + HALO
NDEV = 8


MESH = pl.DeviceIdType.MESH
ANY = pl.BlockSpec(memory_space=pl.ANY)
CHIP_FLIPS = ((1, 0), (0, 1), (1, 1))


def _pos():
    return lax.axis_index("x"), lax.axis_index("y"), lax.axis_index("c")


def _flip(v, f):
    return 1 - v if f else v


def _ds(start, size, align=None):
    return pl.ds(pl.multiple_of(start, align or size), size)


def _place_shards(wi, wo, cw, where):
    steps = 4

    def body(where_ref, wi_ref, wo_ref, cw_ref, wi_full, wo_full, cw_full):
        wi_full[...] = wi_ref[...].astype(BF16)
        wo_full[...] = wo_ref[...].astype(BF16)
        cw_full[...] = cw_ref[...]

    grid_spec = pltpu.PrefetchScalarGridSpec(
        num_scalar_prefetch=1, grid=(steps,),
        in_specs=[pl.BlockSpec((D // steps, CHUNK), lambda i, w: (i, 0)),
                  pl.BlockSpec((WOUT_SHARD // steps, D), lambda i, w: (i, 0)),
                  pl.BlockSpec((HALO, CONVW_SHARD), lambda i, w: (0, 0))],
        out_specs=[pl.BlockSpec((D // steps, CHUNK), lambda i, w: (i, w[0])),
                   pl.BlockSpec((WOUT_SHARD // steps, D), lambda i, w: (w[0] * steps + i, 0)),
                   pl.BlockSpec((HALO, CONVW_SHARD), lambda i, w: (0, w[0]))])
    return pl.pallas_call(
        body, grid_spec=grid_spec, name="place_shards",
        out_shape=[jax.ShapeDtypeStruct((D, NCOL), BF16), jax.ShapeDtypeStruct((WOUT_ROWS, D), BF16),
                   jax.ShapeDtypeStruct((HALO, D), F32)],
        compiler_params=_params(("arbitrary",)),
    )(where, wi, wo, cw)


W_IN, W_OUT, TAPS = range(3)
GATHER_SEMS = 12


def _gather_stages(fulls, send, recv):
    halves = {W_IN: D // 2, W_OUT: WOUT_SHARD // 2, TAPS: HALO // 2}
    x, y, c = _pos()
    chips = {"me": (x, y), "x": (1 - x, y), "y": (x, 1 - y), "diag": (1 - x, 1 - y)}
    SENT = ((("me", 0), "x"), (("me", 1), "x"), (("me", 1), "y"), (("me", 0), "y"), (("x", 0), "y"), (("y", 1), "x"))
    LANDS = ((("x", 0), "x"), (("x", 1), "x"), (("y", 1), "y"), (("y", 0), "y"), (("diag", 0), "y"), (("diag", 1), "x"))
    N_ICI = len(SENT)

    def region(n_th, whose, half, part):
        a, full = fulls[n_th]
        chip = 2 * chips[whose][0] + chips[whose][1]
        n = halves[a] // 2
        row = half * halves[a] + part * n
        if a == W_IN:
            return full.at[_ds(row, n), _ds(chip * CHUNK, CHUNK, 128)]
        if a == W_OUT:
            return full.at[_ds(chip * WOUT_SHARD + row, n), :]
        return full.at[_ds(row, n), _ds(chip * CONVW_SHARD, CONVW_SHARD, 128)]

    def copy(n_th, kind, piece, dev):
        k = GATHER_SEMS * n_th + kind
        return pltpu.make_async_remote_copy(src_ref=piece, dst_ref=piece, send_sem=send.at[k], recv_sem=recv.at[k],
                                            device_id=dev, device_id_type=MESH)

    def sent(a, k):
        if k < N_ICI:
            (whose, part), to = SENT[k]
            return copy(a, k, region(a, whose, c, part), (*chips[to], c))
        (whose, part), _ = LANDS[k - N_ICI]
        return copy(a, k, region(a, whose, c, part), (x, y, 1 - c))

    def wait_arrival(a, k):
        if k < N_ICI:
            (whose, part), frm = LANDS[k]
            copy(a, k, region(a, whose, c, part), (*chips[frm], c)).wait_recv()
        else:
            (whose, part), _ = LANDS[k - N_ICI]
            copy(a, k, region(a, whose, 1 - c, part), (x, y, 1 - c)).wait_recv()

    arrays = range(len(fulls))

    def own_to_neighbours():
        for a in arrays:
            for k in (0, 2, 1, 3):
                sent(a, k).start()

    def pass_on_neighbours():
        for a in arrays:
            for k, onward in ((0, 4), (2, 5), (1, None), (3, None)):
                wait_arrival(a, k)
                if onward is not None:
                    sent(a, onward).start()
                sent(a, k + N_ICI).start()

    def pass_on_diagonal():
        for a in arrays:
            for k in (4, 5):
                wait_arrival(a, k)
                sent(a, k + N_ICI).start()

    def finish():
        for a in arrays:
            for k in range(N_ICI, 2 * N_ICI):
                wait_arrival(a, k)
            for k in range(2 * N_ICI):
                sent(a, k).wait_send()

    return own_to_neighbours, pass_on_neighbours, pass_on_diagonal, finish


def _gather_sems(n_arrays):
    return [pltpu.SemaphoreType.DMA((GATHER_SEMS * n_arrays,)), pltpu.SemaphoreType.DMA((GATHER_SEMS * n_arrays,))]


def _gather_w_in(wi_full):
    def body(_wi, full, send, recv):
        for stage in _gather_stages([(W_IN, full)], send, recv):
            stage()

    return pl.pallas_call(
        body, name="gather_w_in", in_specs=[ANY], out_specs=ANY, input_output_aliases={0: 0},
        out_shape=jax.ShapeDtypeStruct((D, NCOL), BF16), scratch_shapes=_gather_sems(1),
    )(wi_full)


def _half_shape(a):
    return jax.ShapeDtypeStruct((NCHIP, a.shape[1] // 2, a.shape[2]) if a.ndim == 3 else a.shape, a.dtype)


def _exchange_halves(arrays, name):
    n = len(arrays)

    def body(*refs):
        srcs, dsts, (send, recv) = refs[:n], refs[n:2 * n], refs[2 * n:]
        x, y, c = _pos()
        cps = []
        for k, (s_, d_) in enumerate(zip(srcs, dsts)):
            if len(s_.shape) == 3:
                h = s_.shape[1] // 2
                s_ = s_.at[:, _ds((1 - c) * h, h), :]
            cps.append(pltpu.make_async_remote_copy(src_ref=s_, dst_ref=d_, send_sem=send.at[k], recv_sem=recv.at[k],
                                                    device_id=(x, y, 1 - c), device_id_type=MESH))
        for cp in cps:
            cp.start()
        for cp in cps:
            cp.wait()

    return pl.pallas_call(
        body, name=name, in_specs=[ANY] * n, out_specs=[ANY] * n, out_shape=[_half_shape(a) for a in arrays],
        scratch_shapes=[pltpu.SemaphoreType.DMA((n,)), pltpu.SemaphoreType.DMA((n,))],
    )(*arrays)


def _add_halves(arrays, received, name):
    n = len(arrays)

    def body(*refs):
        mine, theirs, outs = refs[:n], refs[n:2 * n], refs[2 * n:]
        c = lax.axis_index("c")
        for m_, t_, o_ in zip(mine, theirs, outs):
            if len(m_.shape) == 3:
                h = m_.shape[1] // 2
                o_[0] = (m_[0, _ds(c * h, h), :].astype(F32) + t_[0].astype(F32)).astype(o_.dtype)
            else:
                o_[...] = m_[...] + t_[...]

    def spec(shape):
        if len(shape) == 3:
            return pl.BlockSpec((1,) + tuple(shape[1:]), lambda k: (k, 0, 0))
        return pl.BlockSpec(tuple(shape), lambda k: (0, 0))

    halves = [_half_shape(a) for a in arrays]
    return pl.pallas_call(
        body, grid=(NCHIP,), name=name,
        in_specs=[spec(a.shape) for a in arrays] + [spec(h.shape) for h in halves],
        out_specs=[spec(h.shape) for h in halves], out_shape=halves,
        compiler_params=_params(("arbitrary",)),
    )(*arrays, *received)


def _chip_exchange_copies(srcs, dsts, send, recv):
    x, y, c = _pos()
    me = 2 * x + y
    pairs = []
    for a in range(len(srcs)):
        for j, (fx, fy) in enumerate(CHIP_FLIPS):
            px, py = _flip(x, fx), _flip(y, fy)
            peer = 2 * px + py
            k = len(CHIP_FLIPS) * a + j
            out = pltpu.make_async_remote_copy(
                src_ref=srcs[a].at[peer] if len(srcs[a].shape) == 3 else srcs[a], dst_ref=dsts[a].at[me],
                send_sem=send.at[k], recv_sem=recv.at[k], device_id=(px, py, c), device_id_type=MESH)
            got = dsts[a].at[peer]
            arrival = pltpu.make_async_remote_copy(
                src_ref=got, dst_ref=got, send_sem=send.at[k], recv_sem=recv.at[k],
                device_id=(px, py, c), device_id_type=MESH)
            pairs.append((out, arrival))
    return pairs


def _sum_chips(ri, ro, rs, pi, po, ps, where):
    def body(w_ref, ri_ref, ro_ref, rs_ref, pi_ref, po_ref, ps_ref, gi_ref, go_ref, gs_ref, g5_ref, loss_ref,
             acc_i, acc_o, acc_s):
        k = pl.program_id(0)
        accs = (acc_i, acc_o, acc_s)

        @pl.when(k == 0)
        def _():
            for acc in accs:
                acc[...] = jnp.zeros_like(acc)

        @pl.when(k == w_ref[0])
        def _():
            for acc, val in zip(accs, (pi_ref[0], po_ref[0], ps_ref[...])):
                acc[...] += val.astype(F32)

        @pl.when(k != w_ref[0])
        def _():
            for acc, ref in zip(accs, (ri_ref, ro_ref, rs_ref)):
                acc[...] += ref[0].astype(F32)

        @pl.when(k == NCHIP - 1)
        def _():
            gi_ref[0] = acc_i[...]
            go_ref[0] = acc_o[...]
            gs_ref[...] = acc_s[...]
            g5_ref[...] = jnp.zeros_like(g5_ref)
            for i, row in enumerate((ROW_CONV_B, ROW_LN_G, ROW_LN_B, ROW_FINAL_G)):
                g5_ref[i + 1:i + 2, :] = acc_s[row:row + 1, :]
            loss = jnp.sum(acc_s[ROW_LOSS:ROW_LOSS + 1, :], axis=1, keepdims=True)
            loss_ref[...] = jnp.broadcast_to(loss, loss_ref.shape)

    def sent(k, w):
        return jnp.where(k == w[0], (k + 1) % NCHIP, k)

    hi, ho = D // 2, WOUT_SHARD // 2
    const = lambda shape: pl.BlockSpec(shape, lambda k, w: (0,) * len(shape))
    grid_spec = pltpu.PrefetchScalarGridSpec(
        num_scalar_prefetch=1, grid=(NCHIP,),
        in_specs=[pl.BlockSpec((1, hi, CHUNK), lambda k, w: (sent(k, w), 0, 0)),
                  pl.BlockSpec((1, ho, D), lambda k, w: (sent(k, w), 0, 0)),
                  pl.BlockSpec((1, SMALL_ROWS, D), lambda k, w: (sent(k, w), 0, 0)),
                  pl.BlockSpec((1, hi, CHUNK), lambda k, w: (w[0], 0, 0)),
                  pl.BlockSpec((1, ho, D), lambda k, w: (w[0], 0, 0)),
                  const((SMALL_ROWS, D))],
        out_specs=[pl.BlockSpec((1, hi, CHUNK), lambda k, w: (w[1], 0, 0)),
                   pl.BlockSpec((1, ho, D), lambda k, w: (w[1], 0, 0)),
                   const((SMALL_ROWS, D)), const((8, D)), const((8, LANES))],
        scratch_shapes=[pltpu.VMEM((hi, CHUNK), F32), pltpu.VMEM((ho, D), F32), pltpu.VMEM((SMALL_ROWS, D), F32)])
    return pl.pallas_call(
        body, grid_spec=grid_spec, name="sum_chips",
        out_shape=[jax.ShapeDtypeStruct((2, hi, CHUNK), F32), jax.ShapeDtypeStruct((2, ho, D), F32),
                   jax.ShapeDtypeStruct((SMALL_ROWS, D), F32), jax.ShapeDtypeStruct((8, D), F32),
                   jax.ShapeDtypeStruct((8, LANES), F32)],
        compiler_params=_params(("arbitrary",)),
    )(where, ri, ro, rs, pi, po, ps)


def _exchange_results(gi2, go2, st):
    flips = [(fx, fy, fc) for fx in (0, 1) for fy in (0, 1) for fc in (0, 1)][1:]

    def body(_gi, _go, st_ref, gi_ref, go_ref, all_ref, send, recv, lsem):
        x, y, c = _pos()
        sib = (x, y, 1 - c)

        def half(k, ref, slot):
            return pltpu.make_async_remote_copy(src_ref=ref.at[slot], dst_ref=ref.at[slot], send_sem=send.at[k],
                                                recv_sem=recv.at[k], device_id=sib, device_id_type=MESH)

        def stat(k, src, slot, dev):
            return pltpu.make_async_remote_copy(src_ref=src, dst_ref=all_ref.at[slot], send_sem=send.at[k],
                                                recv_sem=recv.at[k], device_id=dev, device_id_type=MESH)

        mine = pltpu.make_async_copy(st_ref, all_ref.at[4 * x + 2 * y + c], lsem)
        mine.start()
        sends = [half(k, ref, c) for k, ref in enumerate((gi_ref, go_ref))]
        peers = [(_flip(x, fx), _flip(y, fy), _flip(c, fc)) for fx, fy, fc in flips]
        sends += [stat(2 + k, st_ref, 4 * x + 2 * y + c, dev) for k, dev in enumerate(peers)]
        for cp in sends:
            cp.start()
        for k, ref in enumerate((gi_ref, go_ref)):
            half(k, ref, 1 - c).wait_recv()
        for k, (px, py, pc) in enumerate(peers):
            slot = 4 * px + 2 * py + pc
            stat(2 + k, all_ref.at[slot], slot, (px, py, pc)).wait_recv()
        for cp in sends:
            cp.wait_send()
        mine.wait()

    n = 2 + len(flips)
    return pl.pallas_call(
        body, name="exchange_results",
        in_specs=[ANY, ANY, ANY], out_specs=[ANY, ANY, ANY], input_output_aliases={0: 0, 1: 1},
        out_shape=[jax.ShapeDtypeStruct((2, D // 2, CHUNK), F32), jax.ShapeDtypeStruct((2, WOUT_SHARD // 2, D), F32),
                   jax.ShapeDtypeStruct((NDEV, 8, D), F32)],
        scratch_shapes=[pltpu.SemaphoreType.DMA((n,)), pltpu.SemaphoreType.DMA((n,)), pltpu.SemaphoreType.DMA],
    )(gi2, go2, st)


def _adamw_math(w, g, m, v):
    m2 = ADAM_B1 * m + (1.0 - ADAM_B1) * g
    v2 = ADAM_B2 * v + (1.0 - ADAM_B2) * (g * g)
    m_hat = m2 / (1.0 - ADAM_B1 ** ADAM_STEP)
    v_hat = v2 / (1.0 - ADAM_B2 ** ADAM_STEP)
    delta = -ADAM_LR * (m_hat / (jnp.sqrt(v_hat) + ADAM_EPS) + ADAM_WD * w)
    return delta, m2, v2


def _adamw(w, g, m, v, name):
    rows, cols = w.shape
    tm = 256 if rows % 256 == 0 else rows

    def body(w_ref, g_ref, m_ref, v_ref, d_ref, m2_ref, v2_ref):
        d_ref[...], m2_ref[...], v2_ref[...] = _adamw_math(w_ref[...], g_ref[...], m_ref[...], v_ref[...])

    shape = jax.ShapeDtypeStruct(w.shape, F32)
    return pl.pallas_call(
        body, grid=(rows // tm,), name=name,
        in_specs=[_rows(tm, cols)] * 4, out_specs=[_rows(tm, cols)] * 3, out_shape=[shape] * 3,
        compiler_params=_params(("arbitrary",)),
    )(w, g, m, v)


def _adamw_vectors(g5, first_parts, ws, ms, vs):
    n = len(ws)

    def body(g_ref, parts_ref, *refs):
        ins, g0_ref, outs = refs[:3 * n], refs[3 * n], refs[3 * n + 1:]
        g0 = parts_ref[0, 0:1, :]
        for dev in range(1, NDEV):
            g0 = g0 + parts_ref[dev, 0:1, :]
        g0_ref[...] = g0
        for i in range(n):
            g = g0 if i == 0 else g_ref[i:i + 1, :]
            res = _adamw_math(ins[i][...], g, ins[n + i][...], ins[2 * n + i][...])
            for kind in range(3):
                outs[kind * n + i][...] = res[kind]

    shape = jax.ShapeDtypeStruct((1, D), F32)
    return pl.pallas_call(body, name="adamw_vectors", out_shape=[shape] * (1 + 3 * n), compiler_params=_params())(
        g5, first_parts, *ws, *ms, *vs)


def kernel(x, norm_g, w_in, conv_w, conv_b, conv_ln_g, conv_ln_b, w_out, final_norm_g, loss_target, m_norm_g, m_w_in, m_conv_w, m_conv_b, m_conv_ln_g, m_conv_ln_b, m_w_out, m_final_norm_g, v_norm_g, v_w_in, v_conv_w, v_conv_b, v_conv_ln_g, v_conv_ln_b, v_w_out, v_final_norm_g):
    chip = 2 * lax.axis_index("x") + lax.axis_index("y")
    where = jnp.stack([chip, lax.axis_index("c")]).astype(jnp.int32)
    taps_shard = jnp.pad(conv_w[0], ((0, HALO - CONV_K), (0, 0)))
    wi_full, wo_full, cw_full = _place_shards(w_in[0], w_out[0], taps_shard, where)
    wi_full = _gather_w_in(wi_full)

    gf = final_norm_g[None]
    xb = x[0]
    h, q, k, v, a_gate, c_val, c_glu, c_gate, wo_full, cw_full = _inproj_fwd(xb, norm_g, wi_full, wo_full, cw_full)
    tables = [_bias_table(d) for d in PATTERNS]
    o, lse, y_att = _attn_fwd(q, k, v, tables, a_gate)
    u, y_conv = _conv_fwd(c_val, c_glu, c_gate, cw_full, conv_b, conv_ln_g, conv_ln_b)
    dx2, dy_att, du, dc_gate, dw_out, st_out = _outproj_loss(
        y_att, y_conv, wo_full, xb, loss_target[0], gf, u, c_gate, conv_ln_g, conv_ln_b)
    dc_val, dc_glu, dconv_w = _conv_bwd_taps(du, c_val, c_glu, cw_full)

    early = [dw_out.reshape(NCHIP, WOUT_SHARD, D), jnp.concatenate([st_out, dconv_w], axis=0)]
    po, ps = _add_halves(early, _exchange_halves(early, "exchange_halves_early"), "add_halves_early")
    do, da_gate, delta, ro, rs = _attn_gate_bwd(dy_att, o, a_gate, _head_sum_selectors(), [po, ps])
    dqs, dkvs = zip(*[_attn_bwd(q, k, v, do, lse, delta, t, d) for t, d in zip(tables, PATTERNS)])

    dproj_pieces = (dqs, dkvs, (da_gate, dc_val, dc_glu, dc_gate))
    late = [_inproj_bwd_w(h, *dproj_pieces)]
    (pi,) = _add_halves(late, _exchange_halves(late, "exchange_halves"), "add_halves")
    grad_x, st_in, ri = _inproj_bwd_x(*dproj_pieces, wi_full, xb, norm_g, dx2, pi)
    gi2, go2, g_small, g5, loss8 = _sum_chips(ri, ro, rs, pi, po, ps, where)
    gi2, go2, norm_g_parts = _exchange_results(gi2, go2, st_in)
    g_w_in = gi2.reshape(D, CHUNK)
    g_w_out = go2.reshape(WOUT_SHARD, D)
    g_taps = lax.dynamic_slice(g_small, (ROW_TAPS, chip * CONVW_SHARD), (CONV_K, CONVW_SHARD))

    d_w_in, m2_w_in, v2_w_in = _adamw(w_in[0], g_w_in, m_w_in[0], v_w_in[0], "adamw_w_in")
    d_w_out, m2_w_out, v2_w_out = _adamw(w_out[0], g_w_out, m_w_out[0], v_w_out[0], "adamw_w_out")
    d_taps, m2_taps, v2_taps = _adamw(conv_w[0], g_taps, m_conv_w[0], v_conv_w[0], "adamw_conv_w")
    g_norm, *vec = _adamw_vectors(
        g5, norm_g_parts,
        (norm_g, conv_b, conv_ln_g, conv_ln_b, gf),
        (m_norm_g, m_conv_b, m_conv_ln_g, m_conv_ln_b, m_final_norm_g[None]),
        (v_norm_g, v_conv_b, v_conv_ln_g, v_conv_ln_b, v_final_norm_g[None]))
    d_vec, m2_vec, v2_vec = vec[0:5], vec[5:10], vec[10:15]

    def weight_order(ng, wi, cw, cb, lg, lb, wo, fg):
        return (ng, wi[None], cw[None], cb, lg, lb, wo[None], fg[0])

    grads = weight_order(g_norm, g_w_in, g_taps, g5[1:2], g5[2:3], g5[3:4], g_w_out, g5[4:5])
    deltas = weight_order(d_vec[0], d_w_in, d_taps, d_vec[1], d_vec[2], d_vec[3], d_w_out, d_vec[4])
    new_m = weight_order(m2_vec[0], m2_w_in, m2_taps, m2_vec[1], m2_vec[2], m2_vec[3], m2_w_out, m2_vec[4])
    new_v = weight_order(v2_vec[0], v2_w_in, v2_taps, v2_vec[1], v2_vec[2], v2_vec[3], v2_w_out, v2_vec[4])
    return (loss8[0, 0], grad_x[None], *grads, *deltas, *new_m, *new_v)
```

```python
import jax
import jax.numpy as jnp
from jax import lax
from jax.experimental import pallas as pl
from jax.experimental.pallas import tpu as pltpu

F32 = jnp.float32
BF16 = jnp.bfloat16

S = 4096
D = 1024
LANES = 128
HD = 64
NKV = 4
GQ = 4
KVW = NKV * HD
NCOL = 5632
CONV_K = 31
HALO = 32
BLK = 128
PATTERNS = (1, 4, 16)
NORM_EPS = 1e-6
LN_EPS = 1e-5
NEG = -1e30
OFF_Q, OFF_K, OFF_AG, OFF_CV, OFF_CG, OFF_CGATE = 0, 1024, 1536, 2560, 3584, 4608
NCHIP = 4
CHUNK = NCOL // NCHIP
WOUT_ROWS = 2 * D
WOUT_SHARD = WOUT_ROWS // NCHIP
CONVW_SHARD = D // NCHIP

ADAM_LR, ADAM_B1, ADAM_B2, ADAM_EPS, ADAM_WD, ADAM_STEP = 0.001, 0.9, 0.999, 1e-08, 0.01, 10

VMEM_LIMIT = 56 * 1024 * 1024


def _params(sem=None, vmem=VMEM_LIMIT):
    return pltpu.CompilerParams(dimension_semantics=sem, vmem_limit_bytes=vmem)


def _sigmoid(a):
    return 0.5 * jnp.tanh(0.5 * a) + 0.5


def _rows(tm, width):
    return pl.BlockSpec((tm, width), lambda i: (i, 0))


def _slabs(n):
    return jax.ShapeDtypeStruct((n, S, LANES), F32)


def _slab_rows(n, tm):
    return pl.BlockSpec((n, tm, LANES), lambda i: (0, i, 0))


def _resident(shape):
    return pl.BlockSpec(shape, lambda *_: (0,) * len(shape), pipeline_mode=pl.Buffered(1))


def _dot(a, b):
    return jnp.dot(a, b, preferred_element_type=F32)


def _dot_nt(a, b):
    return lax.dot_general(a, b, (((1,), (1,)), ((), ())), preferred_element_type=F32)


def _dot_tn(a, b):
    return lax.dot_general(a, b, (((0,), (0,)), ((), ())), preferred_element_type=F32)


def _inproj_fwd(x, g1, w_bf, wo_full, cw_full):
    tm = 512
    steps = S // tm

    def body(x_ref, g_ref, w_ref, _wo, _cw, h_ref, q_ref, k_ref, v_ref, ag_ref, cv_ref, cg_ref, cgate_ref,
             wo_ref, cw_ref, send, recv):
        i = pl.program_id(0)
        stages = _gather_stages([(W_OUT, wo_ref), (TAPS, cw_ref)], send, recv)
        for stage, step in zip(stages[:3], (0, steps // 2 - 1, steps - 2)):
            pl.when(i == step)(stage)
        xt = x_ref[...]
        r = lax.rsqrt(jnp.mean(xt * xt, axis=-1, keepdims=True) + NORM_EPS)
        h = (xt * r * g_ref[...]).astype(BF16)
        h_ref[...] = h
        q = _dot(h, w_ref[:, OFF_Q:OFF_Q + D]) * (HD ** -0.5)
        kv = _dot(h, w_ref[:, OFF_K:OFF_K + 2 * KVW])
        for sl in range(D // LANES):
            q_ref[sl] = q[:, sl * LANES:(sl + 1) * LANES]
        for sl in range(KVW // LANES):
            k_ref[sl] = kv[:, sl * LANES:(sl + 1) * LANES]
            v_ref[sl] = kv[:, KVW + sl * LANES:KVW + (sl + 1) * LANES]
        ag_ref[...] = _dot(h, w_ref[:, OFF_AG:OFF_AG + D])
        cv_ref[...] = _dot(h, w_ref[:, OFF_CV:OFF_CV + D])
        cg_ref[...] = _dot(h, w_ref[:, OFF_CG:OFF_CG + D])
        cgate_ref[...] = _dot(h, w_ref[:, OFF_CGATE:OFF_CGATE + D])
        pl.when(i == steps - 1)(stages[3])

    big = jax.ShapeDtypeStruct((S, D), F32)
    return pl.pallas_call(
        body, grid=(steps,), name="inproj_fwd",
        in_specs=[_rows(tm, D), _resident((1, D)), _resident((D, NCOL)), ANY, ANY],
        out_specs=[_rows(tm, D), _slab_rows(D // LANES, tm), _slab_rows(KVW // LANES, tm), _slab_rows(KVW // LANES, tm),
                   _rows(tm, D), _rows(tm, D), _rows(tm, D), _rows(tm, D), ANY, ANY],
        out_shape=[jax.ShapeDtypeStruct((S, D), BF16), _slabs(D // LANES), _slabs(KVW // LANES), _slabs(KVW // LANES),
                   big, big, big, big,
                   jax.ShapeDtypeStruct((WOUT_ROWS, D), BF16), jax.ShapeDtypeStruct((HALO, D), F32)],
        input_output_aliases={3: 8, 4: 9},
        scratch_shapes=_gather_sems(2),
        compiler_params=_params(("arbitrary",)),
    )(x, g1, w_bf, wo_full, cw_full)


def _bias_table(d):
    h = jnp.arange(NKV * GQ, dtype=F32)
    slopes = jnp.exp2(-8.0 * (h + 1.0) / (NKV * GQ))
    qi = jnp.arange(BLK)[:, None]
    kj = jnp.arange(2 * BLK)[None, :]
    dist = BLK + qi - kj
    window = (dist >= 0) & (dist <= BLK)
    bias = -slopes[:, None, None] * (dist * d).astype(F32)[None]
    has_prev = jnp.stack([jnp.broadcast_to(kj >= BLK, (BLK, 2 * BLK)), jnp.ones((BLK, 2 * BLK), bool)])
    valid = window[None] & has_prev
    tab = jnp.where(valid[:, None], bias[None], NEG)
    return tab.reshape(2, NKV, GQ * BLK, 2 * BLK)


def _sub_rows(start, d, align=BLK):
    if d == 1:
        return pl.ds(pl.multiple_of(start, align), BLK)
    return pl.ds(start, BLK, stride=d)


CHUNK_ROWS = 2048
BLOCKS_PER_CHUNK = CHUNK_ROWS // BLK


def _low_lanes(rows=BLK):
    return lax.broadcasted_iota(jnp.int32, (rows, LANES), 1) < HD


def _block_start(idx, d):
    shift = d.bit_length() - 1
    b, r = lax.shift_right_logical(idx, shift), lax.bitwise_and(idx, d - 1)
    start = b * (BLK * d) + r
    return b, start, jnp.maximum(start - BLK * d, r)


def _stack_heads(ref, rows):
    low = _low_lanes()
    t0, t1 = ref[0, rows, :], ref[1, rows, :]
    return jnp.concatenate([jnp.where(low, t0, 0.0), jnp.where(low, 0.0, t0),
                            jnp.where(low, t1, 0.0), jnp.where(low, 0.0, t1)], axis=0).astype(BF16)


def _unstack_heads(dup):
    low = _low_lanes()
    return (jnp.where(low, dup[0:BLK], dup[BLK:2 * BLK]), jnp.where(low, dup[2 * BLK:3 * BLK], dup[3 * BLK:4 * BLK]))


def _kv_dup(ref, prow, rows, odd):
    t = jnp.concatenate([ref[0, prow, :], ref[0, rows, :]], axis=0)
    swapped = pltpu.roll(t, HD, axis=1)
    keep = jnp.logical_xor(_low_lanes(2 * BLK), odd)
    return jnp.where(keep, t, swapped).astype(BF16)


PIECES = 3


def _by_head(tiles):
    lane = lax.broadcasted_iota(jnp.int32, tiles[0].shape, 1)
    out = tiles[0]
    for g in range(1, GQ):
        out = jnp.where(lax.bitwise_and(lane, GQ - 1) == g, tiles[g], out)
    return out


def _minus_in_pieces(x):
    lane = lax.broadcasted_iota(jnp.int32, x.shape, 1)
    hi = (-x).astype(BF16).astype(F32)
    rest = -x - hi
    mid = rest.astype(BF16).astype(F32)
    lo = (rest - mid).astype(BF16).astype(F32)
    return jnp.where(lane < GQ, hi, jnp.where(lane < 2 * GQ, mid, jnp.where(lane < PIECES * GQ, lo, 0.0)))


PITCH_PAD = 4


def _pitches(d):
    return BLK + PITCH_PAD, S // d + PITCH_PAD


def _pull_apart(pairs, d, groups, pitch, back=False):
    def group(g, carry):
        for src, dst in pairs:
            for n in range(src.shape[0]):
                for half in range(d // SUBLANES):
                    together = (n, pl.ds(pl.multiple_of(g * d + half * SUBLANES, SUBLANES), SUBLANES), slice(None))
                    spread = (n, pl.ds(half * SUBLANES * pitch + g, SUBLANES, stride=pitch), slice(None))
                    if back:
                        src[together] = dst[spread]
                    else:
                        dst[spread] = src[together]
        return carry

    lax.fori_loop(0, groups, group, 0, unroll=8)


def _attn_fwd(q, k, v, tables, a_gate):
    tm = 256
    width = GQ * HD

    lane_out = jnp.arange(LANES)[None, :] // HD
    spread_sel = jnp.stack([jnp.arange(LANES)[:, None] == 2 * half + lane_out for half in range(2)]).astype(BF16)

    def body(q_ref, k_ref, v_ref, b1_ref, b2_ref, b3_ref, ag_ref, sel_ref, o_ref, lse_ref, y_ref, op, lp):
        odd = pl.program_id(0) % 2 == 1
        chunk = pl.program_id(1)
        ones = jnp.ones((2 * BLK, LANES), BF16)

        for pat, (d, b_ref) in enumerate(zip(PATTERNS, (b1_ref, b2_ref, b3_ref))):
            def block(idx, carry, pat=pat, d=d, b_ref=b_ref):
                b, start, pstart = _block_start(chunk * BLOCKS_PER_CHUNK + idx, d)
                rows, prow = _sub_rows(start, d), _sub_rows(pstart, d)
                mine = _sub_rows(start - chunk * CHUNK_ROWS, d)
                qs = _stack_heads(q_ref, mine)
                kw = _kv_dup(k_ref, prow, rows, odd)
                vw = _kv_dup(v_ref, prow, rows, odd)
                s = _dot_nt(qs, kw) + b_ref[jnp.minimum(b, 1), 0]
                m = jnp.max(s, axis=1, keepdims=True)
                p = jnp.exp(s - m).astype(BF16)
                ol = _dot(p, jnp.concatenate([vw, ones], axis=1))
                l = ol[:, LANES:]
                op[pat, 0, mine, :], op[pat, 1, mine, :] = _unstack_heads(ol[:, :LANES] / l)
                lp[pat, mine, :] = _by_head([(m + jnp.log(l))[g * BLK:(g + 1) * BLK] for g in range(GQ)])
                return carry

            lax.fori_loop(0, BLOCKS_PER_CHUNK, block, 0, unroll=2)

        def mix(t, carry):
            r = pl.ds(pl.multiple_of(t * tm, tm), tm)
            a, b, c = lp[0, r, :], lp[1, r, :], lp[2, r, :]
            m = jnp.maximum(jnp.maximum(a, b), c)
            ea, eb, ec = jnp.exp(a - m), jnp.exp(b - m), jnp.exp(c - m)
            den = ea + eb + ec
            lse_ref[0, r, :] = _minus_in_pieces(m + jnp.log(den))
            inv = 1.0 / den
            for half in range(2):
                def spread(w):
                    hi = w.astype(BF16)
                    lo = (w - hi.astype(F32)).astype(BF16)
                    return _dot(hi, sel_ref[half]) + _dot(lo, sel_ref[half])

                o = (spread(ea * inv) * op[0, half, r, :] + spread(eb * inv) * op[1, half, r, :]
                     + spread(ec * inv) * op[2, half, r, :])
                o_ref[half, r, :] = o
                cols = slice(half * LANES, (half + 1) * LANES)
                ag = ag_ref[r, cols]
                y_ref[r, cols] = (o * (ag * _sigmoid(ag))).astype(BF16)
            return carry

        lax.fori_loop(0, CHUNK_ROWS // tm, mix, 0, unroll=2)

    q_like = pl.BlockSpec((2, CHUNK_ROWS, LANES), lambda j, c: (j, c, 0))
    per_kv = pl.BlockSpec((1, CHUNK_ROWS, LANES), lambda j, c: (j, c, 0))
    kv = pl.BlockSpec((1, S, LANES), lambda j, c: (j // 2, 0, 0))
    bias_spec = pl.BlockSpec((2, 1, GQ * BLK, 2 * BLK), lambda j, c: (0, j, 0, 0))
    group_cols = pl.BlockSpec((CHUNK_ROWS, width), lambda j, c: (c, j))
    return pl.pallas_call(
        body, grid=(NKV, S // CHUNK_ROWS), name="attn_fwd",
        in_specs=[q_like, kv, kv, bias_spec, bias_spec, bias_spec, group_cols,
                  pl.BlockSpec((2, LANES, LANES), lambda j, c: (0, 0, 0))],
        out_specs=[q_like, per_kv, group_cols],
        out_shape=[_slabs(D // LANES), _slabs(NKV), jax.ShapeDtypeStruct((S, D), BF16)],
        scratch_shapes=[pltpu.VMEM((len(PATTERNS), 2, CHUNK_ROWS, LANES), F32),
                        pltpu.VMEM((len(PATTERNS), CHUNK_ROWS, LANES), F32)],
        compiler_params=_params(("arbitrary", "arbitrary")),
    )(q, k, v, *tables, a_gate, spread_sel)


def _head_sum_selectors():
    lane_in = jnp.arange(LANES)[:, None] // HD
    return jnp.stack([jnp.broadcast_to(lane_in == h, (LANES, LANES)) for h in range(2)]).astype(BF16)


def _attn_gate_bwd(dy_att, o, a_gate, selectors, chip_sums):
    tm = 256
    last = S // tm - 1
    landing, sems = _exchange_results_of(chip_sums)
    n_sums = len(chip_sums)

    def body(dy_ref, o_ref, ag_ref, e_ref, *refs):
        sums, (do_ref, dag_ref, delta_ref), refs = refs[:n_sums], refs[n_sums:n_sums + 3], refs[n_sums + 3:]
        landed, (send, recv) = refs[:n_sums], refs[n_sums:]
        i = pl.program_id(0)
        copies = _chip_exchange_copies(sums, landed, send, recv)
        _start_exchange(copies, i == 0)
        for j in range(NKV):
            deltas = []
            for sl in (2 * j, 2 * j + 1):
                cols = slice(sl * LANES, (sl + 1) * LANES)
                dy, ag, o_ = dy_ref[:, cols], ag_ref[:, cols], o_ref[sl]
                sg = _sigmoid(ag)
                do = dy * (ag * sg)
                do_ref[sl] = do
                dag_ref[:, cols] = (dy * o_ * (sg * (1.0 + ag * (1.0 - sg)))).astype(BF16)
                prod = do * o_
                hi = prod.astype(BF16)
                lo = (prod - hi.astype(F32)).astype(BF16)
                deltas += [_dot(hi, e_ref[h]) + _dot(lo, e_ref[h]) for h in range(2)]
            delta_ref[j] = _minus_in_pieces(_by_head(deltas))
        _finish_exchange(copies, i == last)

    return pl.pallas_call(
        body, grid=(S // tm,), name="attn_gate_bwd",
        in_specs=[_rows(tm, D), _slab_rows(D // LANES, tm), _rows(tm, D), _resident((2, LANES, LANES))] + [ANY] * n_sums,
        out_specs=[_slab_rows(D // LANES, tm), _rows(tm, D), _slab_rows(NKV, tm)] + [ANY] * n_sums,
        out_shape=[_slabs(D // LANES), jax.ShapeDtypeStruct((S, D), BF16), _slabs(NKV)] + landing,
        scratch_shapes=sems,
        compiler_params=_params(("arbitrary",)),
    )(dy_att, o, a_gate, selectors, *chip_sums)


def _own_pieces(tile):
    lane = lax.broadcasted_iota(jnp.int32, tile.shape, 1)
    head = jnp.where(lane < PIECES * GQ, lax.bitwise_and(lane, GQ - 1), -1)
    return jnp.concatenate([jnp.where(head == g, tile, 0.0) for g in range(GQ)], axis=0).astype(BF16)


def _attn_bwd(q, k, v, do, lse, delta, bias, d):
    apart = d * BLK == CHUNK_ROWS
    pitch, kv_pitch = _pitches(d)

    def body(q_ref, do_ref, l_ref, dl_ref, k_ref, v_ref, b_ref, dq_ref, dkv_ref, acc, *copies):
        odd = pl.program_id(0) % 2 == 1
        chunk = pl.program_id(1)
        dq_out = dq_ref
        if apart:
            qd, dod, ld, dld, kd, vd, dq_out = copies

            @pl.when(jnp.logical_and(chunk == 0, jnp.logical_not(odd)))
            def _():
                _pull_apart([(k_ref, kd), (v_ref, vd)], d, S // d, kv_pitch)

            _pull_apart([(q_ref, qd), (do_ref, dod), (l_ref, ld), (dl_ref, dld)], d, BLK, pitch)
            q_ref, do_ref, l_ref, dl_ref, k_ref, v_ref = qd, dod, ld, dld, kd, vd
        ones = (lax.broadcasted_iota(jnp.int32, (2 * BLK, LANES), 1) < PIECES * GQ).astype(BF16)

        def in_acc(block_idx):
            return pl.ds(pl.multiple_of(block_idx * BLK, BLK), BLK)

        @pl.when(chunk == 0)
        def _():
            acc[...] = jnp.zeros_like(acc)

        def block(idx, carry):
            idx = chunk * BLOCKS_PER_CHUNK + idx
            b, start, pstart = _block_start(idx, d)
            rows, prow = _sub_rows(start, d), _sub_rows(pstart, d)
            mine = _sub_rows(start - chunk * CHUNK_ROWS, d)
            if apart:
                r = idx - b * d
                rows = _sub_rows(r * kv_pitch + b * BLK, 1, PITCH_PAD)
                prow = _sub_rows(r * kv_pitch + jnp.maximum(b - 1, 0) * BLK, 1, PITCH_PAD)
                mine = _sub_rows(r * pitch, 1, PITCH_PAD)
            qs = _stack_heads(q_ref, mine)
            dos = _stack_heads(do_ref, mine)
            kw = _kv_dup(k_ref, prow, rows, odd)
            vw = _kv_dup(v_ref, prow, rows, odd)
            s = _dot_nt(jnp.concatenate([qs, _own_pieces(l_ref[0, mine, :])], axis=1),
                        jnp.concatenate([kw, ones], axis=1)) + b_ref[jnp.minimum(b, 1), 0]
            p = jnp.exp(s)
            dv2 = _dot_tn(p.astype(BF16), dos)
            dp = _dot_nt(jnp.concatenate([dos, _own_pieces(dl_ref[0, mine, :])], axis=1),
                         jnp.concatenate([vw, ones], axis=1))
            ds = (p * dp).astype(BF16)
            dq_out[0, mine, :], dq_out[1, mine, :] = _unstack_heads(_dot(ds, kw))
            dk2 = _dot_tn(ds, qs)
            dkv = jnp.where(_low_lanes(2 * BLK), dk2 + pltpu.roll(dk2, HD, axis=1), dv2 + pltpu.roll(dv2, HD, axis=1))
            acc[in_acc(idx), :] = acc[in_acc(idx), :] + dkv[BLK:]
            before = jnp.where(b >= 1, idx - d, idx)
            acc[in_acc(before), :] = acc[in_acc(before), :] + dkv[:BLK]
            return carry

        lax.fori_loop(0, BLOCKS_PER_CHUNK, block, 0, unroll=16)
        if apart:
            _pull_apart([(dq_ref, dq_out)], d, BLK, pitch, back=True)

        @pl.when(chunk == S // CHUNK_ROWS - 1)
        def _():
            def place(idx, carry):
                _, start, _ = _block_start(idx, d)
                dkv_ref[0, _sub_rows(start, d), :] = acc[in_acc(idx), :]
                return carry

            lax.fori_loop(0, S // BLK, place, 0, unroll=4)

    q_like = pl.BlockSpec((2, CHUNK_ROWS, LANES), lambda j, c: (j, c, 0))
    pieces = pl.BlockSpec((1, CHUNK_ROWS, LANES), lambda j, c: (j, c, 0))
    kv = pl.BlockSpec((1, S, LANES), lambda j, c: (j // 2, 0, 0))
    per_kv = pl.BlockSpec((1, S, LANES), lambda j, c: (j, 0, 0))
    bias_spec = pl.BlockSpec((2, 1, GQ * BLK, 2 * BLK), lambda j, c: (0, j, 0, 0))
    return pl.pallas_call(
        body, grid=(NKV, S // CHUNK_ROWS), name=f"attn_bwd_d{d}",
        in_specs=[q_like, q_like, pieces, pieces, kv, kv, bias_spec],
        out_specs=[q_like, per_kv],
        out_shape=[_slabs(D // LANES), _slabs(NKV)],
        scratch_shapes=[pltpu.VMEM((S, LANES), F32)] + apart * [
            pltpu.VMEM((n, d * rows, LANES), F32)
            for n, rows in ((2, pitch), (2, pitch), (1, pitch), (1, pitch), (1, kv_pitch), (1, kv_pitch), (2, pitch))],
        compiler_params=_params(("arbitrary", "arbitrary")),
    )(q, do, lse, delta, k, v, bias)


CONV_T = 256


def _halo_before(i):
    return (jnp.maximum(i * (CONV_T // HALO) - 1, 0), 0)


def _halo_after(i):
    return (jnp.minimum((i + 1) * (CONV_T // HALO), S // HALO - 1), 0)


SUBLANES = 8
NCH = D // LANES
GROUP = SUBLANES * SUBLANES


def _comb(ref, cb, base):
    return ref[cb, pl.ds(base, SUBLANES, stride=SUBLANES), :]


def _taps(w_ref, cols):
    return [jnp.broadcast_to(w_ref[j:j + 1, cols], (SUBLANES, LANES)) for j in range(CONV_K)]


def _conv_fwd(c_val, c_glu, c_gate, conv_w, conv_b, ln_g, ln_b):
    T = CONV_T

    def body(cv_ref, cg_ref, cvh_ref, cgh_ref, gate_ref, w_ref, b_ref, lg_ref, lb_ref, u_ref, y_ref, win, us):
        i = pl.program_id(0)
        for cb in range(NCH):
            cols = slice(cb * LANES, (cb + 1) * LANES)
            win[cb, HALO:HALO + T, :] = cv_ref[:, cols] * _sigmoid(cg_ref[:, cols])
            win[cb, 0:HALO, :] = jnp.where(i > 0, cvh_ref[:, cols] * _sigmoid(cgh_ref[:, cols]), 0.0)
        for cb in range(NCH):
            cols = slice(cb * LANES, (cb + 1) * LANES)
            taps = _taps(w_ref, cols)
            bias = jnp.broadcast_to(b_ref[:, cols], (SUBLANES, LANES))

            def group(g, carry):
                for b in range(SUBLANES):
                    base = g * GROUP + b
                    acc = bias
                    for j in range(CONV_K):
                        acc = acc + taps[j] * _comb(win, cb, base + (HALO - (CONV_K - 1) + j))
                    us[cb, pl.ds(base, SUBLANES, stride=SUBLANES), :] = acc
                return carry

            lax.fori_loop(0, T // GROUP, group, 0, unroll=2)
        total = us[0]
        for cb in range(1, NCH):
            total = total + us[cb]
        mu = jnp.sum(total, axis=-1, keepdims=True) * (1.0 / D)
        sq = jnp.zeros((T, LANES), F32)
        for cb in range(NCH):
            uc = us[cb] - mu
            sq = sq + uc * uc
        rstd = lax.rsqrt(jnp.sum(sq, axis=-1, keepdims=True) * (1.0 / D) + LN_EPS)
        for cb in range(NCH):
            cols = slice(cb * LANES, (cb + 1) * LANES)
            u = us[cb]
            u_ref[:, cols] = u
            nrm = (u - mu) * rstd * lg_ref[:, cols] + lb_ref[:, cols]
            gate = gate_ref[:, cols]
            y_ref[:, cols] = (nrm * _sigmoid(nrm) * (gate * _sigmoid(gate))).astype(BF16)

    halo = pl.BlockSpec((HALO, D), _halo_before)
    return pl.pallas_call(
        body, grid=(S // T,), name="conv_fwd",
        in_specs=[_rows(T, D), _rows(T, D), halo, halo, _rows(T, D),
                  _resident((HALO, D)), _resident((1, D)), _resident((1, D)), _resident((1, D))],
        out_specs=[_rows(T, D), _rows(T, D)],
        out_shape=[jax.ShapeDtypeStruct((S, D), F32), jax.ShapeDtypeStruct((S, D), BF16)],
        scratch_shapes=[pltpu.VMEM((NCH, T + HALO, LANES), F32), pltpu.VMEM((NCH, T, LANES), F32)],
        compiler_params=_params(("arbitrary",)),
    )(c_val, c_glu, c_val, c_glu, c_gate, conv_w, conv_b, ln_g, ln_b)


def _conv_bwd_taps(du, c_val, c_glu, conv_w):
    T = CONV_T
    last = S // T - 1

    def body(du_ref, dua_ref, cv_ref, cg_ref, cvh_ref, cgh_ref, w_ref, dcv_ref, dcg_ref, dw_ref,
             hwin, dwin, dhs, dw_acc):
        i = pl.program_id(0)

        @pl.when(i == 0)
        def _():
            dw_acc[...] = jnp.zeros_like(dw_acc)

        for cb in range(NCH):
            cols = slice(cb * LANES, (cb + 1) * LANES)
            hwin[cb, HALO:HALO + T, :] = cv_ref[:, cols] * _sigmoid(cg_ref[:, cols])
            hwin[cb, 0:HALO, :] = jnp.where(i > 0, cvh_ref[:, cols] * _sigmoid(cgh_ref[:, cols]), 0.0)
            dwin[cb, 0:T, :] = du_ref[:, cols]
            dwin[cb, T:T + HALO, :] = jnp.where(i < last, dua_ref[:, cols], 0.0)
        for cb in range(NCH):
            cols = slice(cb * LANES, (cb + 1) * LANES)
            taps = _taps(w_ref, cols)

            def group_dh(g, carry):
                for b in range(SUBLANES):
                    base = g * GROUP + b
                    acc = jnp.zeros((SUBLANES, LANES), F32)
                    for j in range(CONV_K):
                        acc = acc + taps[j] * _comb(dwin, cb, base + (CONV_K - 1 - j))
                    dhs[cb, pl.ds(base, SUBLANES, stride=SUBLANES), :] = acc
                return carry

            lax.fori_loop(0, T // GROUP, group_dh, 0, unroll=2)

            def group_dw(g, sums):
                for b in range(SUBLANES):
                    base = g * GROUP + b
                    d = _comb(dwin, cb, base)
                    sums = tuple(sums[j] + d * _comb(hwin, cb, base + (HALO - (CONV_K - 1) + j))
                                 for j in range(CONV_K))
                return sums

            sums = lax.fori_loop(0, T // GROUP, group_dw, tuple(dw_acc[j, :, cols] for j in range(CONV_K)))
            for j in range(CONV_K):
                dw_acc[j, :, cols] = sums[j]
            dh = dhs[cb]
            cv, sg = cv_ref[:, cols], _sigmoid(cg_ref[:, cols])
            dcv_ref[:, cols] = (dh * sg).astype(BF16)
            dcg_ref[:, cols] = (dh * cv * (sg * (1.0 - sg))).astype(BF16)

        @pl.when(i == last)
        def _():
            dw_ref[...] = jnp.zeros_like(dw_ref)
            for j in range(CONV_K):
                dw_ref[j:j + 1, :] = jnp.sum(dw_acc[j], axis=0, keepdims=True)

    before = pl.BlockSpec((HALO, D), _halo_before)
    after = pl.BlockSpec((HALO, D), _halo_after)
    big = jax.ShapeDtypeStruct((S, D), BF16)
    return pl.pallas_call(
        body, grid=(S // T,), name="conv_bwd_taps",
        in_specs=[_rows(T, D), after, _rows(T, D), _rows(T, D), before, before, _resident((HALO, D))],
        out_specs=[_rows(T, D), _rows(T, D), pl.BlockSpec((HALO, D), lambda i: (0, 0))],
        out_shape=[big, big, jax.ShapeDtypeStruct((HALO, D), F32)],
        scratch_shapes=[pltpu.VMEM((NCH, T + HALO, LANES), F32), pltpu.VMEM((NCH, T + HALO, LANES), F32),
                        pltpu.VMEM((NCH, T, LANES), F32), pltpu.VMEM((CONV_K, SUBLANES, D), F32)],
        compiler_params=_params(("arbitrary",)),
    )(du, du, c_val, c_glu, c_val, c_glu, conv_w)


def _outproj_loss(y_att, y_conv, w_out_bf, x, target, gf, u, c_gate, ln_g, ln_b):
    tm = 256

    def body(ya_ref, yc_ref, w_ref, x_ref, t_ref, gf_ref, u_ref, gate_ref, lg_ref, lb_ref,
             dx2_ref, dya_ref, du_ref, dgate_ref, dw_ref, st_ref, acc):
        @pl.when(pl.program_id(0) == 0)
        def _():
            acc[...] = jnp.zeros_like(acc)
            st_ref[...] = jnp.zeros_like(st_ref)

        ya, yc = ya_ref[...], yc_ref[...]
        x2 = x_ref[...] + _dot(ya, w_ref[0:D, :]) + _dot(yc, w_ref[D:2 * D, :])
        r = lax.rsqrt(jnp.mean(x2 * x2, axis=-1, keepdims=True) + NORM_EPS)
        xn = x2 * r
        err = xn * gf_ref[...] - t_ref[...]
        dout = err * (1.0 / D)
        dxn = dout * gf_ref[...]
        dx2 = r * (dxn - xn * jnp.mean(dxn * xn, axis=-1, keepdims=True))
        dx2_ref[...] = dx2
        dx2b = dx2.astype(BF16)
        dya_ref[...] = _dot_nt(dx2b, w_ref[0:D, :])
        dy = _dot_nt(dx2b, w_ref[D:2 * D, :])
        acc[0:D, :] += _dot_tn(ya, dx2b)
        acc[D:2 * D, :] += _dot_tn(yc, dx2b)
        st_ref[ROW_FINAL_G:ROW_FINAL_G + 1, :] += jnp.sum(dout * xn, axis=0, keepdims=True)
        st_ref[ROW_LOSS:ROW_LOSS + 1, :] += jnp.sum(err * err, axis=0, keepdims=True) * (0.5 / D)

        u, gate = u_ref[...], gate_ref[...]
        mu = jnp.mean(u, axis=-1, keepdims=True)
        uc = u - mu
        rstd = lax.rsqrt(jnp.mean(uc * uc, axis=-1, keepdims=True) + LN_EPS)
        z = uc * rstd
        nrm = z * lg_ref[...] + lb_ref[...]
        sn, sg = _sigmoid(nrm), _sigmoid(gate)
        dgate_ref[...] = (dy * (nrm * sn) * (sg * (1.0 + gate * (1.0 - sg)))).astype(BF16)
        dn = dy * (gate * sg) * (sn * (1.0 + nrm * (1.0 - sn)))
        dz = dn * lg_ref[...]
        du = rstd * (dz - jnp.mean(dz, axis=-1, keepdims=True) - z * jnp.mean(dz * z, axis=-1, keepdims=True))
        du_ref[...] = du
        st_ref[ROW_LN_G:ROW_LN_G + 1, :] += jnp.sum(dn * z, axis=0, keepdims=True)
        st_ref[ROW_LN_B:ROW_LN_B + 1, :] += jnp.sum(dn, axis=0, keepdims=True)
        st_ref[ROW_CONV_B:ROW_CONV_B + 1, :] += jnp.sum(du, axis=0, keepdims=True)

        @pl.when(pl.program_id(0) == S // tm - 1)
        def _():
            dw_ref[...] = acc[...].astype(BF16)

    big = jax.ShapeDtypeStruct((S, D), F32)
    vec = _resident((1, D))
    return pl.pallas_call(
        body, grid=(S // tm,), name="outproj_loss",
        in_specs=[_rows(tm, D), _rows(tm, D), _resident((WOUT_ROWS, D)), _rows(tm, D), _rows(tm, D), vec,
                  _rows(tm, D), _rows(tm, D), vec, vec],
        out_specs=[_rows(tm, D), _rows(tm, D), _rows(tm, D), _rows(tm, D),
                   pl.BlockSpec((WOUT_ROWS, D), lambda i: (0, 0)), pl.BlockSpec((8, D), lambda i: (0, 0))],
        out_shape=[big, big, big, jax.ShapeDtypeStruct((S, D), BF16),
                   jax.ShapeDtypeStruct((WOUT_ROWS, D), BF16), jax.ShapeDtypeStruct((8, D), F32)],
        scratch_shapes=[pltpu.VMEM((WOUT_ROWS, D), F32)],
        compiler_params=_params(("arbitrary",)),
    )(y_att, y_conv, w_out_bf, x, target, gf, u, c_gate, ln_g, ln_b)


UNITS_PER_CHUNK = CHUNK // LANES


def _dproj_unit(u, dqs, dkvs, gates, rows):
    if u < OFF_K // LANES:
        return ((dqs[0][u] + dqs[1][u] + dqs[2][u]) * (HD ** -0.5)).astype(BF16)
    if u < OFF_AG // LANES:
        w = u - OFF_K // LANES
        ta, tb = (dkvs[0][j] + dkvs[1][j] + dkvs[2][j] for j in (2 * (w % 2), 2 * (w % 2) + 1))
        low = _low_lanes(rows)
        if w < 2:
            return jnp.where(low, ta, pltpu.roll(tb, HD, axis=1)).astype(BF16)
        return jnp.where(low, pltpu.roll(ta, HD, axis=1), tb).astype(BF16)
    g, sl = divmod(u - OFF_AG // LANES, D // LANES)
    return gates[g][:, sl * LANES:(sl + 1) * LANES]


def _dproj_sources(units, dqs, dkvs, gates, rows):
    use_q = any(u < OFF_K // LANES for u in units)
    use_kv = any(OFF_K // LANES <= u < OFF_AG // LANES for u in units)
    use_g = sorted({(u - OFF_AG // LANES) // (D // LANES) for u in units if u >= OFF_AG // LANES})
    args = (list(dqs) if use_q else []) + (list(dkvs) if use_kv else []) + [gates[g] for g in use_g]
    specs = ([_slab_rows(D // LANES, rows)] * 3 if use_q else []) + ([_slab_rows(NKV, rows)] * 3 if use_kv else []) \
        + [_rows(rows, D)] * len(use_g)

    def pick(refs):
        refs = list(refs)
        q_refs = [refs.pop(0) for _ in range(3)] if use_q else None
        kv_refs = [refs.pop(0) for _ in range(3)] if use_kv else None
        return q_refs, kv_refs, {g: refs.pop(0) for g in use_g}

    return args, specs, pick


def _exchange_results_of(chip_sums):
    n = len(chip_sums) * len(CHIP_FLIPS)
    shapes = [jax.ShapeDtypeStruct((NCHIP,) + tuple(a.shape[1:] if a.ndim == 3 else a.shape), a.dtype)
              for a in chip_sums]
    return shapes, [pltpu.SemaphoreType.DMA((n,)), pltpu.SemaphoreType.DMA((n,))]


def _start_exchange(copies, first_step):
    @pl.when(first_step)
    def _():
        for out, _ in copies:
            out.start()


def _finish_exchange(copies, last_step):
    @pl.when(last_step)
    def _():
        for _, arrival in copies:
            arrival.wait_recv()
        for out, _ in copies:
            out.wait_send()


def _inproj_bwd_x(dqs, dkvs, gates, w_bf, x, g1, dx2, pi):
    tm = 256
    last = S // tm - 1
    units = range(NCOL // LANES)
    pieces, piece_specs, pick = _dproj_sources(units, dqs, dkvs, gates, tm)
    landing, sems = _exchange_results_of([pi])

    def body(*refs):
        piece_refs, refs = refs[:len(pieces)], refs[len(pieces):]
        w_ref, x_ref, g_ref, dx2_ref, pi_ref, gx_ref, st_ref, ri_ref, dp_ref, send, recv = refs
        i = pl.program_id(0)
        copies = _chip_exchange_copies([pi_ref], [ri_ref], send, recv)
        _start_exchange(copies, i == 0)

        @pl.when(i == 0)
        def _():
            st_ref[...] = jnp.zeros_like(st_ref)

        sources = pick(piece_refs)
        for u in units:
            dp_ref[:, u * LANES:(u + 1) * LANES] = _dproj_unit(u, *sources, tm)
        dh = _dot_nt(dp_ref[...], w_ref[...])
        xt = x_ref[...]
        r = lax.rsqrt(jnp.mean(xt * xt, axis=-1, keepdims=True) + NORM_EPS)
        xn = xt * r
        dxn = dh * g_ref[...]
        gx_ref[...] = dx2_ref[...] + r * (dxn - xn * jnp.mean(dxn * xn, axis=-1, keepdims=True))
        st_ref[0:1, :] += jnp.sum(dh * xn, axis=0, keepdims=True)
        _finish_exchange(copies, i == last)

    return pl.pallas_call(
        body, grid=(S // tm,), name="inproj_bwd_x",
        in_specs=piece_specs + [_resident((D, NCOL)), _rows(tm, D), _resident((1, D)), _rows(tm, D), ANY],
        out_specs=[_rows(tm, D), pl.BlockSpec((8, D), lambda i: (0, 0)), ANY],
        out_shape=[jax.ShapeDtypeStruct((S, D), F32), jax.ShapeDtypeStruct((8, D), F32)] + landing,
        scratch_shapes=[pltpu.VMEM((tm, NCOL), BF16)] + sems,
        compiler_params=_params(("arbitrary",)),
    )(*pieces, w_bf, x, g1, dx2, pi)


def _inproj_bwd_w(h, dqs, dkvs, gates):
    out = None
    for k in range(NCHIP):
        units = range(k * UNITS_PER_CHUNK, (k + 1) * UNITS_PER_CHUNK)
        tk = 512 if units[0] < OFF_K // LANES else 1024
        nk = S // tk
        pieces, piece_specs, pick = _dproj_sources(units, dqs, dkvs, gates, tk)
        handed_on = [] if out is None else [out]

        def body(*refs, units=units, pick=pick, n_pieces=len(pieces), n_in=1 + len(pieces) + len(handed_on)):
            h_ref, piece_refs = refs[0], refs[1:1 + n_pieces]
            o_ref, tile, acc = refs[n_in:]
            i = pl.program_id(0)

            @pl.when(i == 0)
            def _():
                acc[...] = jnp.zeros_like(acc)

            sources = pick(piece_refs)
            for n, u in enumerate(units):
                tile[:, n * LANES:(n + 1) * LANES] = _dproj_unit(u, *sources, tk)
            acc[...] += _dot_tn(h_ref[...], tile[...])

            @pl.when(i == nk - 1)
            def _():
                o_ref[0] = acc[...].astype(BF16)

        out = pl.pallas_call(
            body, grid=(nk,), name=f"inproj_bwd_w{k}",
            in_specs=[_rows(tk, D)] + piece_specs + [ANY] * len(handed_on),
            out_specs=pl.BlockSpec((1, D, CHUNK), lambda i, k=k: (k, 0, 0)),
            out_shape=jax.ShapeDtypeStruct((NCHIP, D, CHUNK), BF16),
            input_output_aliases={1 + len(pieces): 0} if handed_on else {},
            scratch_shapes=[pltpu.VMEM((tk, CHUNK), BF16), pltpu.VMEM((D, CHUNK), F32)],
            compiler_params=_params(("arbitrary",)),
        )(h, *pieces, *handed_on)
    return out


ROW_FINAL_G, ROW_LOSS, ROW_LN_G, ROW_LN_B, ROW_CONV_B, ROW_TAPS = 0, 1, 2, 3, 4, 8
SMALL_ROWS = 8 + HALO
NDEV = 8


MESH = pl.DeviceIdType.MESH
ANY = pl.BlockSpec(memory_space=pl.ANY)
CHIP_FLIPS = ((1, 0), (0, 1), (1, 1))


def _pos():
    return lax.axis_index("x"), lax.axis_index("y"), lax.axis_index("c")


def _flip(v, f):
    return 1 - v if f else v


def _ds(start, size, align=None):
    return pl.ds(pl.multiple_of(start, align or size), size)


def _place_shards(wi, wo, cw, where):
    steps = 4

    def body(where_ref, wi_ref, wo_ref, cw_ref, wi_full, wo_full, cw_full):
        wi_full[...] = wi_ref[...].astype(BF16)
        wo_full[...] = wo_ref[...].astype(BF16)
        cw_full[...] = cw_ref[...]

    grid_spec = pltpu.PrefetchScalarGridSpec(
        num_scalar_prefetch=1, grid=(steps,),
        in_specs=[pl.BlockSpec((D // steps, CHUNK), lambda i, w: (i, 0)),
                  pl.BlockSpec((WOUT_SHARD // steps, D), lambda i, w: (i, 0)),
                  pl.BlockSpec((HALO, CONVW_SHARD), lambda i, w: (0, 0))],
        out_specs=[pl.BlockSpec((D // steps, CHUNK), lambda i, w: (i, w[0])),
                   pl.BlockSpec((WOUT_SHARD // steps, D), lambda i, w: (w[0] * steps + i, 0)),
                   pl.BlockSpec((HALO, CONVW_SHARD), lambda i, w: (0, w[0]))])
    return pl.pallas_call(
        body, grid_spec=grid_spec, name="place_shards",
        out_shape=[jax.ShapeDtypeStruct((D, NCOL), BF16), jax.ShapeDtypeStruct((WOUT_ROWS, D), BF16),
                   jax.ShapeDtypeStruct((HALO, D), F32)],
        compiler_params=_params(("arbitrary",)),
    )(where, wi, wo, cw)


W_IN, W_OUT, TAPS = range(3)
GATHER_SEMS = 12


def _gather_stages(fulls, send, recv):
    halves = {W_IN: D // 2, W_OUT: WOUT_SHARD // 2, TAPS: HALO // 2}
    x, y, c = _pos()
    chips = {"me": (x, y), "x": (1 - x, y), "y": (x, 1 - y), "diag": (1 - x, 1 - y)}
    SENT = ((("me", 0), "x"), (("me", 1), "x"), (("me", 1), "y"), (("me", 0), "y"), (("x", 0), "y"), (("y", 1), "x"))
    LANDS = ((("x", 0), "x"), (("x", 1), "x"), (("y", 1), "y"), (("y", 0), "y"), (("diag", 0), "y"), (("diag", 1), "x"))
    N_ICI = len(SENT)

    def region(n_th, whose, half, part):
        a, full = fulls[n_th]
        chip = 2 * chips[whose][0] + chips[whose][1]
        n = halves[a] // 2
        row = half * halves[a] + part * n
        if a == W_IN:
            return full.at[_ds(row, n), _ds(chip * CHUNK, CHUNK, 128)]
        if a == W_OUT:
            return full.at[_ds(chip * WOUT_SHARD + row, n), :]
        return full.at[_ds(row, n), _ds(chip * CONVW_SHARD, CONVW_SHARD, 128)]

    def copy(n_th, kind, piece, dev):
        k = GATHER_SEMS * n_th + kind
        return pltpu.make_async_remote_copy(src_ref=piece, dst_ref=piece, send_sem=send.at[k], recv_sem=recv.at[k],
                                            device_id=dev, device_id_type=MESH)

    def sent(a, k):
        if k < N_ICI:
            (whose, part), to = SENT[k]
            return copy(a, k, region(a, whose, c, part), (*chips[to], c))
        (whose, part), _ = LANDS[k - N_ICI]
        return copy(a, k, region(a, whose, c, part), (x, y, 1 - c))

    def wait_arrival(a, k):
        if k < N_ICI:
            (whose, part), frm = LANDS[k]
            copy(a, k, region(a, whose, c, part), (*chips[frm], c)).wait_recv()
        else:
            (whose, part), _ = LANDS[k - N_ICI]
            copy(a, k, region(a, whose, 1 - c, part), (x, y, 1 - c)).wait_recv()

    arrays = range(len(fulls))

    def own_to_neighbours():
        for a in arrays:
            for k in (0, 2, 1, 3):
                sent(a, k).start()

    def pass_on_neighbours():
        for a in arrays:
            for k, onward in ((0, 4), (2, 5), (1, None), (3, None)):
                wait_arrival(a, k)
                if onward is not None:
                    sent(a, onward).start()
                sent(a, k + N_ICI).start()

    def pass_on_diagonal():
        for a in arrays:
            for k in (4, 5):
                wait_arrival(a, k)
                sent(a, k + N_ICI).start()

    def finish():
        for a in arrays:
            for k in range(N_ICI, 2 * N_ICI):
                wait_arrival(a, k)
            for k in range(2 * N_ICI):
                sent(a, k).wait_send()

    return own_to_neighbours, pass_on_neighbours, pass_on_diagonal, finish


def _gather_sems(n_arrays):
    return [pltpu.SemaphoreType.DMA((GATHER_SEMS * n_arrays,)), pltpu.SemaphoreType.DMA((GATHER_SEMS * n_arrays,))]


def _gather_w_in(wi_full):
    def body(_wi, full, send, recv):
        for stage in _gather_stages([(W_IN, full)], send, recv):
            stage()

    return pl.pallas_call(
        body, name="gather_w_in", in_specs=[ANY], out_specs=ANY, input_output_aliases={0: 0},
        out_shape=jax.ShapeDtypeStruct((D, NCOL), BF16), scratch_shapes=_gather_sems(1),
    )(wi_full)


def _half_shape(a):
    return jax.ShapeDtypeStruct((NCHIP, a.shape[1] // 2, a.shape[2]) if a.ndim == 3 else a.shape, a.dtype)


def _exchange_halves(arrays, name):
    n = len(arrays)

    def body(*refs):
        srcs, dsts, (send, recv) = refs[:n], refs[n:2 * n], refs[2 * n:]
        x, y, c = _pos()
        cps = []
        for k, (s_, d_) in enumerate(zip(srcs, dsts)):
            if len(s_.shape) == 3:
                h = s_.shape[1] // 2
                s_ = s_.at[:, _ds((1 - c) * h, h), :]
            cps.append(pltpu.make_async_remote_copy(src_ref=s_, dst_ref=d_, send_sem=send.at[k], recv_sem=recv.at[k],
                                                    device_id=(x, y, 1 - c), device_id_type=MESH))
        for cp in cps:
            cp.start()
        for cp in cps:
            cp.wait()

    return pl.pallas_call(
        body, name=name, in_specs=[ANY] * n, out_specs=[ANY] * n, out_shape=[_half_shape(a) for a in arrays],
        scratch_shapes=[pltpu.SemaphoreType.DMA((n,)), pltpu.SemaphoreType.DMA((n,))],
    )(*arrays)


def _add_halves(arrays, received, name):
    n = len(arrays)

    def body(*refs):
        mine, theirs, outs = refs[:n], refs[n:2 * n], refs[2 * n:]
        c = lax.axis_index("c")
        for m_, t_, o_ in zip(mine, theirs, outs):
            if len(m_.shape) == 3:
                h = m_.shape[1] // 2
                o_[0] = (m_[0, _ds(c * h, h), :].astype(F32) + t_[0].astype(F32)).astype(o_.dtype)
            else:
                o_[...] = m_[...] + t_[...]

    def spec(shape):
        if len(shape) == 3:
            return pl.BlockSpec((1,) + tuple(shape[1:]), lambda k: (k, 0, 0))
        return pl.BlockSpec(tuple(shape), lambda k: (0, 0))

    halves = [_half_shape(a) for a in arrays]
    return pl.pallas_call(
        body, grid=(NCHIP,), name=name,
        in_specs=[spec(a.shape) for a in arrays] + [spec(h.shape) for h in halves],
        out_specs=[spec(h.shape) for h in halves], out_shape=halves,
        compiler_params=_params(("arbitrary",)),
    )(*arrays, *received)


def _chip_exchange_copies(srcs, dsts, send, recv):
    x, y, c = _pos()
    me = 2 * x + y
    pairs = []
    for a in range(len(srcs)):
        for j, (fx, fy) in enumerate(CHIP_FLIPS):
            px, py = _flip(x, fx), _flip(y, fy)
            peer = 2 * px + py
            k = len(CHIP_FLIPS) * a + j
            out = pltpu.make_async_remote_copy(
                src_ref=srcs[a].at[peer] if len(srcs[a].shape) == 3 else srcs[a], dst_ref=dsts[a].at[me],
                send_sem=send.at[k], recv_sem=recv.at[k], device_id=(px, py, c), device_id_type=MESH)
            got = dsts[a].at[peer]
            arrival = pltpu.make_async_remote_copy(
                src_ref=got, dst_ref=got, send_sem=send.at[k], recv_sem=recv.at[k],
                device_id=(px, py, c), device_id_type=MESH)
            pairs.append((out, arrival))
    return pairs


def _sum_chips(ri, ro, rs, pi, po, ps, where):
    def body(w_ref, ri_ref, ro_ref, rs_ref, pi_ref, po_ref, ps_ref, gi_ref, go_ref, gs_ref, g5_ref, loss_ref,
             acc_i, acc_o, acc_s):
        k = pl.program_id(0)
        accs = (acc_i, acc_o, acc_s)

        @pl.when(k == 0)
        def _():
            for acc in accs:
                acc[...] = jnp.zeros_like(acc)

        @pl.when(k == w_ref[0])
        def _():
            for acc, val in zip(accs, (pi_ref[0], po_ref[0], ps_ref[...])):
                acc[...] += val.astype(F32)

        @pl.when(k != w_ref[0])
        def _():
            for acc, ref in zip(accs, (ri_ref, ro_ref, rs_ref)):
                acc[...] += ref[0].astype(F32)

        @pl.when(k == NCHIP - 1)
        def _():
            gi_ref[0] = acc_i[...]
            go_ref[0] = acc_o[...]
            gs_ref[...] = acc_s[...]
            g5_ref[...] = jnp.zeros_like(g5_ref)
            for i, row in enumerate((ROW_CONV_B, ROW_LN_G, ROW_LN_B, ROW_FINAL_G)):
                g5_ref[i + 1:i + 2, :] = acc_s[row:row + 1, :]
            loss = jnp.sum(acc_s[ROW_LOSS:ROW_LOSS + 1, :], axis=1, keepdims=True)
            loss_ref[...] = jnp.broadcast_to(loss, loss_ref.shape)

    def sent(k, w):
        return jnp.where(k == w[0], (k + 1) % NCHIP, k)

    hi, ho = D // 2, WOUT_SHARD // 2
    const = lambda shape: pl.BlockSpec(shape, lambda k, w: (0,) * len(shape))
    grid_spec = pltpu.PrefetchScalarGridSpec(
        num_scalar_prefetch=1, grid=(NCHIP,),
        in_specs=[pl.BlockSpec((1, hi, CHUNK), lambda k, w: (sent(k, w), 0, 0)),
                  pl.BlockSpec((1, ho, D), lambda k, w: (sent(k, w), 0, 0)),
                  pl.BlockSpec((1, SMALL_ROWS, D), lambda k, w: (sent(k, w), 0, 0)),
                  pl.BlockSpec((1, hi, CHUNK), lambda k, w: (w[0], 0, 0)),
                  pl.BlockSpec((1, ho, D), lambda k, w: (w[0], 0, 0)),
                  const((SMALL_ROWS, D))],
        out_specs=[pl.BlockSpec((1, hi, CHUNK), lambda k, w: (w[1], 0, 0)),
                   pl.BlockSpec((1, ho, D), lambda k, w: (w[1], 0, 0)),
                   const((SMALL_ROWS, D)), const((8, D)), const((8, LANES))],
        scratch_shapes=[pltpu.VMEM((hi, CHUNK), F32), pltpu.VMEM((ho, D), F32), pltpu.VMEM((SMALL_ROWS, D), F32)])
    return pl.pallas_call(
        body, grid_spec=grid_spec, name="sum_chips",
        out_shape=[jax.ShapeDtypeStruct((2, hi, CHUNK), F32), jax.ShapeDtypeStruct((2, ho, D), F32),
                   jax.ShapeDtypeStruct((SMALL_ROWS, D), F32), jax.ShapeDtypeStruct((8, D), F32),
                   jax.ShapeDtypeStruct((8, LANES), F32)],
        compiler_params=_params(("arbitrary",)),
    )(where, ri, ro, rs, pi, po, ps)


def _exchange_results(gi2, go2, st):
    flips = [(fx, fy, fc) for fx in (0, 1) for fy in (0, 1) for fc in (0, 1)][1:]

    def body(_gi, _go, st_ref, gi_ref, go_ref, all_ref, send, recv, lsem):
        x, y, c = _pos()
        sib = (x, y, 1 - c)

        def half(k, ref, slot):
            return pltpu.make_async_remote_copy(src_ref=ref.at[slot], dst_ref=ref.at[slot], send_sem=send.at[k],
                                                recv_sem=recv.at[k], device_id=sib, device_id_type=MESH)

        def stat(k, src, slot, dev):
            return pltpu.make_async_remote_copy(src_ref=src, dst_ref=all_ref.at[slot], send_sem=send.at[k],
                                                recv_sem=recv.at[k], device_id=dev, device_id_type=MESH)

        mine = pltpu.make_async_copy(st_ref, all_ref.at[4 * x + 2 * y + c], lsem)
        mine.start()
        sends = [half(k, ref, c) for k, ref in enumerate((gi_ref, go_ref))]
        peers = [(_flip(x, fx), _flip(y, fy), _flip(c, fc)) for fx, fy, fc in flips]
        sends += [stat(2 + k, st_ref, 4 * x + 2 * y + c, dev) for k, dev in enumerate(peers)]
        for cp in sends:
            cp.start()
        for k, ref in enumerate((gi_ref, go_ref)):
            half(k, ref, 1 - c).wait_recv()
        for k, (px, py, pc) in enumerate(peers):
            slot = 4 * px + 2 * py + pc
            stat(2 + k, all_ref.at[slot], slot, (px, py, pc)).wait_recv()
        for cp in sends:
            cp.wait_send()
        mine.wait()

    n = 2 + len(flips)
    return pl.pallas_call(
        body, name="exchange_results",
        in_specs=[ANY, ANY, ANY], out_specs=[ANY, ANY, ANY], input_output_aliases={0: 0, 1: 1},
        out_shape=[jax.ShapeDtypeStruct((2, D // 2, CHUNK), F32), jax.ShapeDtypeStruct((2, WOUT_SHARD // 2, D), F32),
                   jax.ShapeDtypeStruct((NDEV, 8, D), F32)],
        scratch_shapes=[pltpu.SemaphoreType.DMA((n,)), pltpu.SemaphoreType.DMA((n,)), pltpu.SemaphoreType.DMA],
    )(gi2, go2, st)


def _adamw_math(w, g, m, v):
    m2 = ADAM_B1 * m + (1.0 - ADAM_B1) * g
    v2 = ADAM_B2 * v + (1.0 - ADAM_B2) * (g * g)
    m_hat = m2 / (1.0 - ADAM_B1 ** ADAM_STEP)
    v_hat = v2 / (1.0 - ADAM_B2 ** ADAM_STEP)
    delta = -ADAM_LR * (m_hat / (jnp.sqrt(v_hat) + ADAM_EPS) + ADAM_WD * w)
    return delta, m2, v2


def _adamw(w, g, m, v, name):
    rows, cols = w.shape
    tm = 256 if rows % 256 == 0 else rows

    def body(w_ref, g_ref, m_ref, v_ref, d_ref, m2_ref, v2_ref):
        d_ref[...], m2_ref[...], v2_ref[...] = _adamw_math(w_ref[...], g_ref[...], m_ref[...], v_ref[...])

    shape = jax.ShapeDtypeStruct(w.shape, F32)
    return pl.pallas_call(
        body, grid=(rows // tm,), name=name,
        in_specs=[_rows(tm, cols)] * 4, out_specs=[_rows(tm, cols)] * 3, out_shape=[shape] * 3,
        compiler_params=_params(("arbitrary",)),
    )(w, g, m, v)


def _adamw_vectors(g5, first_parts, ws, ms, vs):
    n = len(ws)

    def body(g_ref, parts_ref, *refs):
        ins, g0_ref, outs = refs[:3 * n], refs[3 * n], refs[3 * n + 1:]
        g0 = parts_ref[0, 0:1, :]
        for dev in range(1, NDEV):
            g0 = g0 + parts_ref[dev, 0:1, :]
        g0_ref[...] = g0
        for i in range(n):
            g = g0 if i == 0 else g_ref[i:i + 1, :]
            res = _adamw_math(ins[i][...], g, ins[n + i][...], ins[2 * n + i][...])
            for kind in range(3):
                outs[kind * n + i][...] = res[kind]

    shape = jax.ShapeDtypeStruct((1, D), F32)
    return pl.pallas_call(body, name="adamw_vectors", out_shape=[shape] * (1 + 3 * n), compiler_params=_params())(
        g5, first_parts, *ws, *ms, *vs)


def kernel(x, norm_g, w_in, conv_w, conv_b, conv_ln_g, conv_ln_b, w_out, final_norm_g, loss_target, m_norm_g, m_w_in, m_conv_w, m_conv_b, m_conv_ln_g, m_conv_ln_b, m_w_out, m_final_norm_g, v_norm_g, v_w_in, v_conv_w, v_conv_b, v_conv_ln_g, v_conv_ln_b, v_w_out, v_final_norm_g):
    chip = 2 * lax.axis_index("x") + lax.axis_index("y")
    where = jnp.stack([chip, lax.axis_index("c")]).astype(jnp.int32)
    taps_shard = jnp.pad(conv_w[0], ((0, HALO - CONV_K), (0, 0)))
    wi_full, wo_full, cw_full = _place_shards(w_in[0], w_out[0], taps_shard, where)
    wi_full = _gather_w_in(wi_full)

    gf = final_norm_g[None]
    xb = x[0]
    h, q, k, v, a_gate, c_val, c_glu, c_gate, wo_full, cw_full = _inproj_fwd(xb, norm_g, wi_full, wo_full, cw_full)
    tables = [_bias_table(d) for d in PATTERNS]
    o, lse, y_att = _attn_fwd(q, k, v, tables, a_gate)
    u, y_conv = _conv_fwd(c_val, c_glu, c_gate, cw_full, conv_b, conv_ln_g, conv_ln_b)
    dx2, dy_att, du, dc_gate, dw_out, st_out = _outproj_loss(
        y_att, y_conv, wo_full, xb, loss_target[0], gf, u, c_gate, conv_ln_g, conv_ln_b)
    dc_val, dc_glu, dconv_w = _conv_bwd_taps(du, c_val, c_glu, cw_full)

    early = [dw_out.reshape(NCHIP, WOUT_SHARD, D), jnp.concatenate([st_out, dconv_w], axis=0)]
    po, ps = _add_halves(early, _exchange_halves(early, "exchange_halves_early"), "add_halves_early")
    do, da_gate, delta, ro, rs = _attn_gate_bwd(dy_att, o, a_gate, _head_sum_selectors(), [po, ps])
    dqs, dkvs = zip(*[_attn_bwd(q, k, v, do, lse, delta, t, d) for t, d in zip(tables, PATTERNS)])

    dproj_pieces = (dqs, dkvs, (da_gate, dc_val, dc_glu, dc_gate))
    late = [_inproj_bwd_w(h, *dproj_pieces)]
    (pi,) = _add_halves(late, _exchange_halves(late, "exchange_halves"), "add_halves")
    grad_x, st_in, ri = _inproj_bwd_x(*dproj_pieces, wi_full, xb, norm_g, dx2, pi)
    gi2, go2, g_small, g5, loss8 = _sum_chips(ri, ro, rs, pi, po, ps, where)
    gi2, go2, norm_g_parts = _exchange_results(gi2, go2, st_in)
    g_w_in = gi2.reshape(D, CHUNK)
    g_w_out = go2.reshape(WOUT_SHARD, D)
    g_taps = lax.dynamic_slice(g_small, (ROW_TAPS, chip * CONVW_SHARD), (CONV_K, CONVW_SHARD))

    d_w_in, m2_w_in, v2_w_in = _adamw(w_in[0], g_w_in, m_w_in[0], v_w_in[0], "adamw_w_in")
    d_w_out, m2_w_out, v2_w_out = _adamw(w_out[0], g_w_out, m_w_out[0], v_w_out[0], "adamw_w_out")
    d_taps, m2_taps, v2_taps = _adamw(conv_w[0], g_taps, m_conv_w[0], v_conv_w[0], "adamw_conv_w")
    g_norm, *vec = _adamw_vectors(
        g5, norm_g_parts,
        (norm_g, conv_b, conv_ln_g, conv_ln_b, gf),
        (m_norm_g, m_conv_b, m_conv_ln_g, m_conv_ln_b, m_final_norm_g[None]),
        (v_norm_g, v_conv_b, v_conv_ln_g, v_conv_ln_b, v_final_norm_g[None]))
    d_vec, m2_vec, v2_vec = vec[0:5], vec[5:10], vec[10:15]

    def weight_order(ng, wi, cw, cb, lg, lb, wo, fg):
        return (ng, wi[None], cw[None], cb, lg, lb, wo[None], fg[0])

    grads = weight_order(g_norm, g_w_in, g_taps, g5[1:2], g5[2:3], g5[3:4], g_w_out, g5[4:5])
    deltas = weight_order(d_vec[0], d_w_in, d_taps, d_vec[1], d_vec[2], d_vec[3], d_w_out, d_vec[4])
    new_m = weight_order(m2_vec[0], m2_w_in, m2_taps, m2_vec[1], m2_vec[2], m2_vec[3], m2_w_out, m2_vec[4])
    new_v = weight_order(v2_vec[0], v2_w_in, v2_taps, v2_vec[1], v2_vec[2], v2_vec[3], v2_w_out, v2_vec[4])
    return (loss8[0, 0], grad_x[None], *grads, *deltas, *new_m, *new_v)
```

```python
import jax
import jax.numpy as jnp
from jax import lax
from jax.experimental import pallas as pl
from jax.experimental.pallas import tpu as pltpu

F32 = jnp.float32
BF16 = jnp.bfloat16

S = 4096
D = 1024
LANES = 128
HD = 64
NKV = 4
GQ = 4
KVW = NKV * HD
NCOL = 5632
CONV_K = 31
HALO = 32
BLK = 128
PATTERNS = (1, 4, 16)
NORM_EPS = 1e-6
LN_EPS = 1e-5
NEG = -1e30
OFF_Q, OFF_K, OFF_AG, OFF_CV, OFF_CG, OFF_CGATE = 0, 1024, 1536, 2560, 3584, 4608
NCHIP = 4
CHUNK = NCOL // NCHIP
WOUT_ROWS = 2 * D
WOUT_SHARD = WOUT_ROWS // NCHIP
CONVW_SHARD = D // NCHIP

ADAM_LR, ADAM_B1, ADAM_B2, ADAM_EPS, ADAM_WD, ADAM_STEP = 0.001, 0.9, 0.999, 1e-08, 0.01, 10

VMEM_LIMIT = 56 * 1024 * 1024


def _params(sem=None, vmem=VMEM_LIMIT):
    return pltpu.CompilerParams(dimension_semantics=sem, vmem_limit_bytes=vmem)


def _sigmoid(a):
    return 0.5 * jnp.tanh(0.5 * a) + 0.5


def _rows(tm, width):
    return pl.BlockSpec((tm, width), lambda i: (i, 0))


def _slabs(n):
    return jax.ShapeDtypeStruct((n, S, LANES), F32)


def _slab_rows(n, tm):
    return pl.BlockSpec((n, tm, LANES), lambda i: (0, i, 0))


def _resident(shape):
    return pl.BlockSpec(shape, lambda *_: (0,) * len(shape), pipeline_mode=pl.Buffered(1))


def _dot(a, b):
    return jnp.dot(a, b, preferred_element_type=F32)


def _dot_nt(a, b):
    return lax.dot_general(a, b, (((1,), (1,)), ((), ())), preferred_element_type=F32)


def _dot_tn(a, b):
    return lax.dot_general(a, b, (((0,), (0,)), ((), ())), preferred_element_type=F32)


def _inproj_fwd(x, g1, w_bf, wo_full, cw_full):
    tm = 512
    steps = S // tm

    def body(x_ref, g_ref, w_ref, _wo, _cw, h_ref, q_ref, k_ref, v_ref, ag_ref, cv_ref, cg_ref, cgate_ref,
             wo_ref, cw_ref, send, recv):
        i = pl.program_id(0)
        stages = _gather_stages([(W_OUT, wo_ref), (TAPS, cw_ref)], send, recv)
        for stage, step in zip(stages[:3], (0, steps // 2 - 1, steps - 2)):
            pl.when(i == step)(stage)
        xt = x_ref[...]
        r = lax.rsqrt(jnp.mean(xt * xt, axis=-1, keepdims=True) + NORM_EPS)
        h = (xt * r * g_ref[...]).astype(BF16)
        h_ref[...] = h
        q = _dot(h, w_ref[:, OFF_Q:OFF_Q + D]) * (HD ** -0.5)
        kv = _dot(h, w_ref[:, OFF_K:OFF_K + 2 * KVW])
        for sl in range(D // LANES):
            q_ref[sl] = q[:, sl * LANES:(sl + 1) * LANES]
        for sl in range(KVW // LANES):
            k_ref[sl] = kv[:, sl * LANES:(sl + 1) * LANES]
            v_ref[sl] = kv[:, KVW + sl * LANES:KVW + (sl + 1) * LANES]
        ag_ref[...] = _dot(h, w_ref[:, OFF_AG:OFF_AG + D])
        cv_ref[...] = _dot(h, w_ref[:, OFF_CV:OFF_CV + D])
        cg_ref[...] = _dot(h, w_ref[:, OFF_CG:OFF_CG + D])
        cgate_ref[...] = _dot(h, w_ref[:, OFF_CGATE:OFF_CGATE + D])
        pl.when(i == steps - 1)(stages[3])

    big = jax.ShapeDtypeStruct((S, D), F32)
    return pl.pallas_call(
        body, grid=(steps,), name="inproj_fwd",
        in_specs=[_rows(tm, D), _resident((1, D)), _resident((D, NCOL)), ANY, ANY],
        out_specs=[_rows(tm, D), _slab_rows(D // LANES, tm), _slab_rows(KVW // LANES, tm), _slab_rows(KVW // LANES, tm),
                   _rows(tm, D), _rows(tm, D), _rows(tm, D), _rows(tm, D), ANY, ANY],
        out_shape=[jax.ShapeDtypeStruct((S, D), BF16), _slabs(D // LANES), _slabs(KVW // LANES), _slabs(KVW // LANES),
                   big, big, big, big,
                   jax.ShapeDtypeStruct((WOUT_ROWS, D), BF16), jax.ShapeDtypeStruct((HALO, D), F32)],
        input_output_aliases={3: 8, 4: 9},
        scratch_shapes=_gather_sems(2),
        compiler_params=_params(("arbitrary",)),
    )(x, g1, w_bf, wo_full, cw_full)


def _bias_table(d):
    h = jnp.arange(NKV * GQ, dtype=F32)
    slopes = jnp.exp2(-8.0 * (h + 1.0) / (NKV * GQ))
    qi = jnp.arange(BLK)[:, None]
    kj = jnp.arange(2 * BLK)[None, :]
    dist = BLK + qi - kj
    window = (dist >= 0) & (dist <= BLK)
    bias = -slopes[:, None, None] * (dist * d).astype(F32)[None]
    has_prev = jnp.stack([jnp.broadcast_to(kj >= BLK, (BLK, 2 * BLK)), jnp.ones((BLK, 2 * BLK), bool)])
    valid = window[None] & has_prev
    tab = jnp.where(valid[:, None], bias[None], NEG)
    return tab.reshape(2, NKV, GQ * BLK, 2 * BLK)


def _sub_rows(start, d, align=BLK):
    if d == 1:
        return pl.ds(pl.multiple_of(start, align), BLK)
    return pl.ds(start, BLK, stride=d)


CHUNK_ROWS = 2048
BLOCKS_PER_CHUNK = CHUNK_ROWS // BLK


def _low_lanes(rows=BLK):
    return lax.broadcasted_iota(jnp.int32, (rows, LANES), 1) < HD


def _block_start(idx, d):
    shift = d.bit_length() - 1
    b, r = lax.shift_right_logical(idx, shift), lax.bitwise_and(idx, d - 1)
    start = b * (BLK * d) + r
    return b, start, jnp.maximum(start - BLK * d, r)


def _stack_heads(ref, rows):
    low = _low_lanes()
    t0, t1 = ref[0, rows, :], ref[1, rows, :]
    return jnp.concatenate([jnp.where(low, t0, 0.0), jnp.where(low, 0.0, t0),
                            jnp.where(low, t1, 0.0), jnp.where(low, 0.0, t1)], axis=0).astype(BF16)


def _unstack_heads(dup):
    low = _low_lanes()
    return (jnp.where(low, dup[0:BLK], dup[BLK:2 * BLK]), jnp.where(low, dup[2 * BLK:3 * BLK], dup[3 * BLK:4 * BLK]))


def _kv_dup(ref, prow, rows, odd):
    t = jnp.concatenate([ref[0, prow, :], ref[0, rows, :]], axis=0)
    swapped = pltpu.roll(t, HD, axis=1)
    keep = jnp.logical_xor(_low_lanes(2 * BLK), odd)
    return jnp.where(keep, t, swapped).astype(BF16)


PIECES = 3


def _by_head(tiles):
    lane = lax.broadcasted_iota(jnp.int32, tiles[0].shape, 1)
    out = tiles[0]
    for g in range(1, GQ):
        out = jnp.where(lax.bitwise_and(lane, GQ - 1) == g, tiles[g], out)
    return out


def _minus_in_pieces(x):
    lane = lax.broadcasted_iota(jnp.int32, x.shape, 1)
    hi = (-x).astype(BF16).astype(F32)
    rest = -x - hi
    mid = rest.astype(BF16).astype(F32)
    lo = (rest - mid).astype(BF16).astype(F32)
    return jnp.where(lane < GQ, hi, jnp.where(lane < 2 * GQ, mid, jnp.where(lane < PIECES * GQ, lo, 0.0)))


PITCH_PAD = 4


def _pitches(d):
    return BLK + PITCH_PAD, S // d + PITCH_PAD


def _pull_apart(pairs, d, groups, pitch, back=False):
    def group(g, carry):
        for src, dst in pairs:
            for n in range(src.shape[0]):
                for half in range(d // SUBLANES):
                    together = (n, pl.ds(pl.multiple_of(g * d + half * SUBLANES, SUBLANES), SUBLANES), slice(None))
                    spread = (n, pl.ds(half * SUBLANES * pitch + g, SUBLANES, stride=pitch), slice(None))
                    if back:
                        src[together] = dst[spread]
                    else:
                        dst[spread] = src[together]
        return carry

    lax.fori_loop(0, groups, group, 0, unroll=8)


def _attn_fwd(q, k, v, tables, a_gate):
    tm = 256
    width = GQ * HD

    lane_out = jnp.arange(LANES)[None, :] // HD
    spread_sel = jnp.stack([jnp.arange(LANES)[:, None] == 2 * half + lane_out for half in range(2)]).astype(BF16)

    def body(q_ref, k_ref, v_ref, b1_ref, b2_ref, b3_ref, ag_ref, sel_ref, o_ref, lse_ref, y_ref, op, lp):
        odd = pl.program_id(0) % 2 == 1
        chunk = pl.program_id(1)
        ones = jnp.ones((2 * BLK, LANES), BF16)

        for pat, (d, b_ref) in enumerate(zip(PATTERNS, (b1_ref, b2_ref, b3_ref))):
            def block(idx, carry, pat=pat, d=d, b_ref=b_ref):
                b, start, pstart = _block_start(chunk * BLOCKS_PER_CHUNK + idx, d)
                rows, prow = _sub_rows(start, d), _sub_rows(pstart, d)
                mine = _sub_rows(start - chunk * CHUNK_ROWS, d)
                qs = _stack_heads(q_ref, mine)
                kw = _kv_dup(k_ref, prow, rows, odd)
                vw = _kv_dup(v_ref, prow, rows, odd)
                s = _dot_nt(qs, kw) + b_ref[jnp.minimum(b, 1), 0]
                m = jnp.max(s, axis=1, keepdims=True)
                p = jnp.exp(s - m).astype(BF16)
                ol = _dot(p, jnp.concatenate([vw, ones], axis=1))
                l = ol[:, LANES:]
                op[pat, 0, mine, :], op[pat, 1, mine, :] = _unstack_heads(ol[:, :LANES] / l)
                lp[pat, mine, :] = _by_head([(m + jnp.log(l))[g * BLK:(g + 1) * BLK] for g in range(GQ)])
                return carry

            lax.fori_loop(0, BLOCKS_PER_CHUNK, block, 0, unroll=2)

        def mix(t, carry):
            r = pl.ds(pl.multiple_of(t * tm, tm), tm)
            a, b, c = lp[0, r, :], lp[1, r, :], lp[2, r, :]
            m = jnp.maximum(jnp.maximum(a, b), c)
            ea, eb, ec = jnp.exp(a - m), jnp.exp(b - m), jnp.exp(c - m)
            den = ea + eb + ec
            lse_ref[0, r, :] = _minus_in_pieces(m + jnp.log(den))
            inv = 1.0 / den
            for half in range(2):
                def spread(w):
                    hi = w.astype(BF16)
                    lo = (w - hi.astype(F32)).astype(BF16)
                    return _dot(hi, sel_ref[half]) + _dot(lo, sel_ref[half])

                o = (spread(ea * inv) * op[0, half, r, :] + spread(eb * inv) * op[1, half, r, :]
                     + spread(ec * inv) * op[2, half, r, :])
                o_ref[half, r, :] = o
                cols = slice(half * LANES, (half + 1) * LANES)
                ag = ag_ref[r, cols]
                y_ref[r, cols] = (o * (ag * _sigmoid(ag))).astype(BF16)
            return carry

        lax.fori_loop(0, CHUNK_ROWS // tm, mix, 0, unroll=2)

    q_like = pl.BlockSpec((2, CHUNK_ROWS, LANES), lambda j, c: (j, c, 0))
    per_kv = pl.BlockSpec((1, CHUNK_ROWS, LANES), lambda j, c: (j, c, 0))
    kv = pl.BlockSpec((1, S, LANES), lambda j, c: (j // 2, 0, 0))
    bias_spec = pl.BlockSpec((2, 1, GQ * BLK, 2 * BLK), lambda j, c: (0, j, 0, 0))
    group_cols = pl.BlockSpec((CHUNK_ROWS, width), lambda j, c: (c, j))
    return pl.pallas_call(
        body, grid=(NKV, S // CHUNK_ROWS), name="attn_fwd",
        in_specs=[q_like, kv, kv, bias_spec, bias_spec, bias_spec, group_cols,
                  pl.BlockSpec((2, LANES, LANES), lambda j, c: (0, 0, 0))],
        out_specs=[q_like, per_kv, group_cols],
        out_shape=[_slabs(D // LANES), _slabs(NKV), jax.ShapeDtypeStruct((S, D), BF16)],
        scratch_shapes=[pltpu.VMEM((len(PATTERNS), 2, CHUNK_ROWS, LANES), F32),
                        pltpu.VMEM((len(PATTERNS), CHUNK_ROWS, LANES), F32)],
        compiler_params=_params(("arbitrary", "arbitrary")),
    )(q, k, v, *tables, a_gate, spread_sel)


def _head_sum_selectors():
    lane_in = jnp.arange(LANES)[:, None] // HD
    return jnp.stack([jnp.broadcast_to(lane_in == h, (LANES, LANES)) for h in range(2)]).astype(BF16)


def _attn_gate_bwd(dy_att, o, a_gate, selectors, chip_sums):
    tm = 256
    last = S // tm - 1
    landing, sems = _exchange_results_of(chip_sums)
    n_sums = len(chip_sums)

    def body(dy_ref, o_ref, ag_ref, e_ref, *refs):
        sums, (do_ref, dag_ref, delta_ref), refs = refs[:n_sums], refs[n_sums:n_sums + 3], refs[n_sums + 3:]
        landed, (send, recv) = refs[:n_sums], refs[n_sums:]
        i = pl.program_id(0)
        copies = _chip_exchange_copies(sums, landed, send, recv)
        _start_exchange(copies, i == 0)
        for j in range(NKV):
            deltas = []
            for sl in (2 * j, 2 * j + 1):
                cols = slice(sl * LANES, (sl + 1) * LANES)
                dy, ag, o_ = dy_ref[:, cols], ag_ref[:, cols], o_ref[sl]
                sg = _sigmoid(ag)
                do = dy * (ag * sg)
                do_ref[sl] = do
                dag_ref[:, cols] = (dy * o_ * (sg * (1.0 + ag * (1.0 - sg)))).astype(BF16)
                prod = do * o_
                hi = prod.astype(BF16)
                lo = (prod - hi.astype(F32)).astype(BF16)
                deltas += [_dot(hi, e_ref[h]) + _dot(lo, e_ref[h]) for h in range(2)]
            delta_ref[j] = _minus_in_pieces(_by_head(deltas))
        _finish_exchange(copies, i == last)

    return pl.pallas_call(
        body, grid=(S // tm,), name="attn_gate_bwd",
        in_specs=[_rows(tm, D), _slab_rows(D // LANES, tm), _rows(tm, D), _resident((2, LANES, LANES))] + [ANY] * n_sums,
        out_specs=[_slab_rows(D // LANES, tm), _rows(tm, D), _slab_rows(NKV, tm)] + [ANY] * n_sums,
        out_shape=[_slabs(D // LANES), jax.ShapeDtypeStruct((S, D), BF16), _slabs(NKV)] + landing,
        scratch_shapes=sems,
        compiler_params=_params(("arbitrary",)),
    )(dy_att, o, a_gate, selectors, *chip_sums)


def _own_pieces(tile):
    lane = lax.broadcasted_iota(jnp.int32, tile.shape, 1)
    head = jnp.where(lane < PIECES * GQ, lax.bitwise_and(lane, GQ - 1), -1)
    return jnp.concatenate([jnp.where(head == g, tile, 0.0) for g in range(GQ)], axis=0).astype(BF16)


def _attn_bwd(q, k, v, do, lse, delta, bias, d):
    apart = d * BLK == CHUNK_ROWS
    pitch, kv_pitch = _pitches(d)

    def body(q_ref, do_ref, l_ref, dl_ref, k_ref, v_ref, b_ref, dq_ref, dkv_ref, acc, *copies):
        odd = pl.program_id(0) % 2 == 1
        chunk = pl.program_id(1)
        dq_out = dq_ref
        if apart:
            qd, dod, ld, dld, kd, vd, dq_out = copies

            @pl.when(jnp.logical_and(chunk == 0, jnp.logical_not(odd)))
            def _():
                _pull_apart([(k_ref, kd), (v_ref, vd)], d, S // d, kv_pitch)

            _pull_apart([(q_ref, qd), (do_ref, dod), (l_ref, ld), (dl_ref, dld)], d, BLK, pitch)
            q_ref, do_ref, l_ref, dl_ref, k_ref, v_ref = qd, dod, ld, dld, kd, vd
        ones = (lax.broadcasted_iota(jnp.int32, (2 * BLK, LANES), 1) < PIECES * GQ).astype(BF16)

        def in_acc(block_idx):
            return pl.ds(pl.multiple_of(block_idx * BLK, BLK), BLK)

        @pl.when(chunk == 0)
        def _():
            acc[...] = jnp.zeros_like(acc)

        def block(idx, carry):
            idx = chunk * BLOCKS_PER_CHUNK + idx
            b, start, pstart = _block_start(idx, d)
            rows, prow = _sub_rows(start, d), _sub_rows(pstart, d)
            mine = _sub_rows(start - chunk * CHUNK_ROWS, d)
            if apart:
                r = idx - b * d
                rows = _sub_rows(r * kv_pitch + b * BLK, 1, PITCH_PAD)
                prow = _sub_rows(r * kv_pitch + jnp.maximum(b - 1, 0) * BLK, 1, PITCH_PAD)
                mine = _sub_rows(r * pitch, 1, PITCH_PAD)
            qs = _stack_heads(q_ref, mine)
            dos = _stack_heads(do_ref, mine)
            kw = _kv_dup(k_ref, prow, rows, odd)
            vw = _kv_dup(v_ref, prow, rows, odd)
            s = _dot_nt(jnp.concatenate([qs, _own_pieces(l_ref[0, mine, :])], axis=1),
                        jnp.concatenate([kw, ones], axis=1)) + b_ref[jnp.minimum(b, 1), 0]
            p = jnp.exp(s)
            dv2 = _dot_tn(p.astype(BF16), dos)
            dp = _dot_nt(jnp.concatenate([dos, _own_pieces(dl_ref[0, mine, :])], axis=1),
                         jnp.concatenate([vw, ones], axis=1))
            ds = (p * dp).astype(BF16)
            dq_out[0, mine, :], dq_out[1, mine, :] = _unstack_heads(_dot(ds, kw))
            dk2 = _dot_tn(ds, qs)
            dkv = jnp.where(_low_lanes(2 * BLK), dk2 + pltpu.roll(dk2, HD, axis=1), dv2 + pltpu.roll(dv2, HD, axis=1))
            acc[in_acc(idx), :] = acc[in_acc(idx), :] + dkv[BLK:]
            before = jnp.where(b >= 1, idx - d, idx)
            acc[in_acc(before), :] = acc[in_acc(before), :] + dkv[:BLK]
            return carry

        lax.fori_loop(0, BLOCKS_PER_CHUNK, block, 0, unroll=16)
        if apart:
            _pull_apart([(dq_ref, dq_out)], d, BLK, pitch, back=True)

        @pl.when(chunk == S // CHUNK_ROWS - 1)
        def _():
            def place(idx, carry):
                _, start, _ = _block_start(idx, d)
                dkv_ref[0, _sub_rows(start, d), :] = acc[in_acc(idx), :]
                return carry

            lax.fori_loop(0, S // BLK, place, 0, unroll=4)

    q_like = pl.BlockSpec((2, CHUNK_ROWS, LANES), lambda j, c: (j, c, 0))
    pieces = pl.BlockSpec((1, CHUNK_ROWS, LANES), lambda j, c: (j, c, 0))
    kv = pl.BlockSpec((1, S, LANES), lambda j, c: (j // 2, 0, 0))
    per_kv = pl.BlockSpec((1, S, LANES), lambda j, c: (j, 0, 0))
    bias_spec = pl.BlockSpec((2, 1, GQ * BLK, 2 * BLK), lambda j, c: (0, j, 0, 0))
    return pl.pallas_call(
        body, grid=(NKV, S // CHUNK_ROWS), name=f"attn_bwd_d{d}",
        in_specs=[q_like, q_like, pieces, pieces, kv, kv, bias_spec],
        out_specs=[q_like, per_kv],
        out_shape=[_slabs(D // LANES), _slabs(NKV)],
        scratch_shapes=[pltpu.VMEM((S, LANES), F32)] + apart * [
            pltpu.VMEM((n, d * rows, LANES), F32)
            for n, rows in ((2, pitch), (2, pitch), (1, pitch), (1, pitch), (1, kv_pitch), (1, kv_pitch), (2, pitch))],
        compiler_params=_params(("arbitrary", "arbitrary")),
    )(q, do, lse, delta, k, v, bias)


CONV_T = 256


def _halo_before(i):
    return (jnp.maximum(i * (CONV_T // HALO) - 1, 0), 0)


def _halo_after(i):
    return (jnp.minimum((i + 1) * (CONV_T // HALO), S // HALO - 1), 0)


SUBLANES = 8
NCH = D // LANES
GROUP = SUBLANES * SUBLANES
COMB_STRIDE = 4


def _comb_base(g, b):
    return g * GROUP + (b // COMB_STRIDE) * (SUBLANES * COMB_STRIDE) + b % COMB_STRIDE


def _comb(ref, cb, base):
    return ref[cb, pl.ds(base, SUBLANES, stride=COMB_STRIDE), :]


def _taps(w_ref, cols):
    return [jnp.broadcast_to(w_ref[j:j + 1, cols], (SUBLANES, LANES)) for j in range(CONV_K)]


def _conv_fwd(c_val, c_glu, c_gate, conv_w, conv_b, ln_g, ln_b):
    T = CONV_T

    def body(cv_ref, cg_ref, cvh_ref, cgh_ref, gate_ref, w_ref, b_ref, lg_ref, lb_ref, u_ref, y_ref, win, us):
        i = pl.program_id(0)
        for cb in range(NCH):
            cols = slice(cb * LANES, (cb + 1) * LANES)
            win[cb, HALO:HALO + T, :] = cv_ref[:, cols] * _sigmoid(cg_ref[:, cols])
            win[cb, 0:HALO, :] = jnp.where(i > 0, cvh_ref[:, cols] * _sigmoid(cgh_ref[:, cols]), 0.0)
        for cb in range(NCH):
            cols = slice(cb * LANES, (cb + 1) * LANES)
            taps = _taps(w_ref, cols)
            bias = jnp.broadcast_to(b_ref[:, cols], (SUBLANES, LANES))

            def group(g, carry):
                for b in range(SUBLANES):
                    base = _comb_base(g, b)
                    acc = bias
                    for j in range(CONV_K):
                        acc = acc + taps[j] * _comb(win, cb, base + (HALO - (CONV_K - 1) + j))
                    us[cb, pl.ds(base, SUBLANES, stride=COMB_STRIDE), :] = acc
                return carry

            lax.fori_loop(0, T // GROUP, group, 0, unroll=2)
        total = us[0]
        for cb in range(1, NCH):
            total = total + us[cb]
        mu = jnp.sum(total, axis=-1, keepdims=True) * (1.0 / D)
        sq = jnp.zeros((T, LANES), F32)
        for cb in range(NCH):
            uc = us[cb] - mu
            sq = sq + uc * uc
        rstd = lax.rsqrt(jnp.sum(sq, axis=-1, keepdims=True) * (1.0 / D) + LN_EPS)
        for cb in range(NCH):
            cols = slice(cb * LANES, (cb + 1) * LANES)
            u = us[cb]
            u_ref[:, cols] = u
            nrm = (u - mu) * rstd * lg_ref[:, cols] + lb_ref[:, cols]
            gate = gate_ref[:, cols]
            y_ref[:, cols] = (nrm * _sigmoid(nrm) * (gate * _sigmoid(gate))).astype(BF16)

    halo = pl.BlockSpec((HALO, D), _halo_before)
    return pl.pallas_call(
        body, grid=(S // T,), name="conv_fwd",
        in_specs=[_rows(T, D), _rows(T, D), halo, halo, _rows(T, D),
                  _resident((HALO, D)), _resident((1, D)), _resident((1, D)), _resident((1, D))],
        out_specs=[_rows(T, D), _rows(T, D)],
        out_shape=[jax.ShapeDtypeStruct((S, D), F32), jax.ShapeDtypeStruct((S, D), BF16)],
        scratch_shapes=[pltpu.VMEM((NCH, T + HALO, LANES), F32), pltpu.VMEM((NCH, T, LANES), F32)],
        compiler_params=_params(("arbitrary",)),
    )(c_val, c_glu, c_val, c_glu, c_gate, conv_w, conv_b, ln_g, ln_b)


def _conv_bwd_taps(du, c_val, c_glu, conv_w):
    T = CONV_T
    last = S // T - 1

    def body(du_ref, dua_ref, cv_ref, cg_ref, cvh_ref, cgh_ref, w_ref, dcv_ref, dcg_ref, dw_ref,
             hwin, dwin, dhs, dw_acc):
        i = pl.program_id(0)

        @pl.when(i == 0)
        def _():
            dw_acc[...] = jnp.zeros_like(dw_acc)

        for cb in range(NCH):
            cols = slice(cb * LANES, (cb + 1) * LANES)
            hwin[cb, HALO:HALO + T, :] = cv_ref[:, cols] * _sigmoid(cg_ref[:, cols])
            hwin[cb, 0:HALO, :] = jnp.where(i > 0, cvh_ref[:, cols] * _sigmoid(cgh_ref[:, cols]), 0.0)
            dwin[cb, 0:T, :] = du_ref[:, cols]
            dwin[cb, T:T + HALO, :] = jnp.where(i < last, dua_ref[:, cols], 0.0)
        for cb in range(NCH):
            cols = slice(cb * LANES, (cb + 1) * LANES)
            taps = _taps(w_ref, cols)

            def group_dh(g, carry):
                for b in range(SUBLANES):
                    base = _comb_base(g, b)
                    acc = jnp.zeros((SUBLANES, LANES), F32)
                    for j in range(CONV_K):
                        acc = acc + taps[j] * _comb(dwin, cb, base + (CONV_K - 1 - j))
                    dhs[cb, pl.ds(base, SUBLANES, stride=COMB_STRIDE), :] = acc
                return carry

            lax.fori_loop(0, T // GROUP, group_dh, 0, unroll=2)

            def group_dw(g, sums):
                for b in range(SUBLANES):
                    base = _comb_base(g, b)
                    d = _comb(dwin, cb, base)
                    sums = tuple(sums[j] + d * _comb(hwin, cb, base + (HALO - (CONV_K - 1) + j))
                                 for j in range(CONV_K))
                return sums

            sums = lax.fori_loop(0, T // GROUP, group_dw, tuple(dw_acc[j, :, cols] for j in range(CONV_K)))
            for j in range(CONV_K):
                dw_acc[j, :, cols] = sums[j]
            dh = dhs[cb]
            cv, sg = cv_ref[:, cols], _sigmoid(cg_ref[:, cols])
            dcv_ref[:, cols] = (dh * sg).astype(BF16)
            dcg_ref[:, cols] = (dh * cv * (sg * (1.0 - sg))).astype(BF16)

        @pl.when(i == last)
        def _():
            dw_ref[...] = jnp.zeros_like(dw_ref)
            for j in range(CONV_K):
                dw_ref[j:j + 1, :] = jnp.sum(dw_acc[j], axis=0, keepdims=True)

    before = pl.BlockSpec((HALO, D), _halo_before)
    after = pl.BlockSpec((HALO, D), _halo_after)
    big = jax.ShapeDtypeStruct((S, D), BF16)
    return pl.pallas_call(
        body, grid=(S // T,), name="conv_bwd_taps",
        in_specs=[_rows(T, D), after, _rows(T, D), _rows(T, D), before, before, _resident((HALO, D))],
        out_specs=[_rows(T, D), _rows(T, D), pl.BlockSpec((HALO, D), lambda i: (0, 0))],
        out_shape=[big, big, jax.ShapeDtypeStruct((HALO, D), F32)],
        scratch_shapes=[pltpu.VMEM((NCH, T + HALO, LANES), F32), pltpu.VMEM((NCH, T + HALO, LANES), F32),
                        pltpu.VMEM((NCH, T, LANES), F32), pltpu.VMEM((CONV_K, SUBLANES, D), F32)],
        compiler_params=_params(("arbitrary",)),
    )(du, du, c_val, c_glu, c_val, c_glu, conv_w)


def _outproj_loss(y_att, y_conv, w_out_bf, x, target, gf, u, c_gate, ln_g, ln_b):
    tm = 256

    def body(ya_ref, yc_ref, w_ref, x_ref, t_ref, gf_ref, u_ref, gate_ref, lg_ref, lb_ref,
             dx2_ref, dya_ref, du_ref, dgate_ref, dw_ref, st_ref, acc):
        @pl.when(pl.program_id(0) == 0)
        def _():
            acc[...] = jnp.zeros_like(acc)
            st_ref[...] = jnp.zeros_like(st_ref)

        ya, yc = ya_ref[...], yc_ref[...]
        x2 = x_ref[...] + _dot(ya, w_ref[0:D, :]) + _dot(yc, w_ref[D:2 * D, :])
        r = lax.rsqrt(jnp.mean(x2 * x2, axis=-1, keepdims=True) + NORM_EPS)
        xn = x2 * r
        err = xn * gf_ref[...] - t_ref[...]
        dout = err * (1.0 / D)
        dxn = dout * gf_ref[...]
        dx2 = r * (dxn - xn * jnp.mean(dxn * xn, axis=-1, keepdims=True))
        dx2_ref[...] = dx2
        dx2b = dx2.astype(BF16)
        dya_ref[...] = _dot_nt(dx2b, w_ref[0:D, :])
        dy = _dot_nt(dx2b, w_ref[D:2 * D, :])
        acc[0:D, :] += _dot_tn(ya, dx2b)
        acc[D:2 * D, :] += _dot_tn(yc, dx2b)
        st_ref[ROW_FINAL_G:ROW_FINAL_G + 1, :] += jnp.sum(dout * xn, axis=0, keepdims=True)
        st_ref[ROW_LOSS:ROW_LOSS + 1, :] += jnp.sum(err * err, axis=0, keepdims=True) * (0.5 / D)

        u, gate = u_ref[...], gate_ref[...]
        mu = jnp.mean(u, axis=-1, keepdims=True)
        uc = u - mu
        rstd = lax.rsqrt(jnp.mean(uc * uc, axis=-1, keepdims=True) + LN_EPS)
        z = uc * rstd
        nrm = z * lg_ref[...] + lb_ref[...]
        sn, sg = _sigmoid(nrm), _sigmoid(gate)
        dgate_ref[...] = (dy * (nrm * sn) * (sg * (1.0 + gate * (1.0 - sg)))).astype(BF16)
        dn = dy * (gate * sg) * (sn * (1.0 + nrm * (1.0 - sn)))
        dz = dn * lg_ref[...]
        du = rstd * (dz - jnp.mean(dz, axis=-1, keepdims=True) - z * jnp.mean(dz * z, axis=-1, keepdims=True))
        du_ref[...] = du
        st_ref[ROW_LN_G:ROW_LN_G + 1, :] += jnp.sum(dn * z, axis=0, keepdims=True)
        st_ref[ROW_LN_B:ROW_LN_B + 1, :] += jnp.sum(dn, axis=0, keepdims=True)
        st_ref[ROW_CONV_B:ROW_CONV_B + 1, :] += jnp.sum(du, axis=0, keepdims=True)

        @pl.when(pl.program_id(0) == S // tm - 1)
        def _():
            dw_ref[...] = acc[...].astype(BF16)

    big = jax.ShapeDtypeStruct((S, D), F32)
    vec = _resident((1, D))
    return pl.pallas_call(
        body, grid=(S // tm,), name="outproj_loss",
        in_specs=[_rows(tm, D), _rows(tm, D), _resident((WOUT_ROWS, D)), _rows(tm, D), _rows(tm, D), vec,
                  _rows(tm, D), _rows(tm, D), vec, vec],
        out_specs=[_rows(tm, D), _rows(tm, D), _rows(tm, D), _rows(tm, D),
                   pl.BlockSpec((WOUT_ROWS, D), lambda i: (0, 0)), pl.BlockSpec((8, D), lambda i: (0, 0))],
        out_shape=[big, big, big, jax.ShapeDtypeStruct((S, D), BF16),
                   jax.ShapeDtypeStruct((WOUT_ROWS, D), BF16), jax.ShapeDtypeStruct((8, D), F32)],
        scratch_shapes=[pltpu.VMEM((WOUT_ROWS, D), F32)],
        compiler_params=_params(("arbitrary",)),
    )(y_att, y_conv, w_out_bf, x, target, gf, u, c_gate, ln_g, ln_b)


UNITS_PER_CHUNK = CHUNK // LANES


def _dproj_unit(u, dqs, dkvs, gates, rows):
    if u < OFF_K // LANES:
        return ((dqs[0][u] + dqs[1][u] + dqs[2][u]) * (HD ** -0.5)).astype(BF16)
    if u < OFF_AG // LANES:
        w = u - OFF_K // LANES
        ta, tb = (dkvs[0][j] + dkvs[1][j] + dkvs[2][j] for j in (2 * (w % 2), 2 * (w % 2) + 1))
        low = _low_lanes(rows)
        if w < 2:
            return jnp.where(low, ta, pltpu.roll(tb, HD, axis=1)).astype(BF16)
        return jnp.where(low, pltpu.roll(ta, HD, axis=1), tb).astype(BF16)
    g, sl = divmod(u - OFF_AG // LANES, D // LANES)
    return gates[g][:, sl * LANES:(sl + 1) * LANES]


def _dproj_sources(units, dqs, dkvs, gates, rows):
    use_q = any(u < OFF_K // LANES for u in units)
    use_kv = any(OFF_K // LANES <= u < OFF_AG // LANES for u in units)
    use_g = sorted({(u - OFF_AG // LANES) // (D // LANES) for u in units if u >= OFF_AG // LANES})
    args = (list(dqs) if use_q else []) + (list(dkvs) if use_kv else []) + [gates[g] for g in use_g]
    specs = ([_slab_rows(D // LANES, rows)] * 3 if use_q else []) + ([_slab_rows(NKV, rows)] * 3 if use_kv else []) \
        + [_rows(rows, D)] * len(use_g)

    def pick(refs):
        refs = list(refs)
        q_refs = [refs.pop(0) for _ in range(3)] if use_q else None
        kv_refs = [refs.pop(0) for _ in range(3)] if use_kv else None
        return q_refs, kv_refs, {g: refs.pop(0) for g in use_g}

    return args, specs, pick


def _exchange_results_of(chip_sums):
    n = len(chip_sums) * len(CHIP_FLIPS)
    shapes = [jax.ShapeDtypeStruct((NCHIP,) + tuple(a.shape[1:] if a.ndim == 3 else a.shape), a.dtype)
              for a in chip_sums]
    return shapes, [pltpu.SemaphoreType.DMA((n,)), pltpu.SemaphoreType.DMA((n,))]


def _start_exchange(copies, first_step):
    @pl.when(first_step)
    def _():
        for out, _ in copies:
            out.start()


def _finish_exchange(copies, last_step):
    @pl.when(last_step)
    def _():
        for _, arrival in copies:
            arrival.wait_recv()
        for out, _ in copies:
            out.wait_send()


def _inproj_bwd_x(dqs, dkvs, gates, w_bf, x, g1, dx2, pi):
    tm = 256
    last = S // tm - 1
    units = range(NCOL // LANES)
    pieces, piece_specs, pick = _dproj_sources(units, dqs, dkvs, gates, tm)
    landing, sems = _exchange_results_of([pi])

    def body(*refs):
        piece_refs, refs = refs[:len(pieces)], refs[len(pieces):]
        w_ref, x_ref, g_ref, dx2_ref, pi_ref, gx_ref, st_ref, ri_ref, dp_ref, send, recv = refs
        i = pl.program_id(0)
        copies = _chip_exchange_copies([pi_ref], [ri_ref], send, recv)
        _start_exchange(copies, i == 0)

        @pl.when(i == 0)
        def _():
            st_ref[...] = jnp.zeros_like(st_ref)

        sources = pick(piece_refs)
        for u in units:
            dp_ref[:, u * LANES:(u + 1) * LANES] = _dproj_unit(u, *sources, tm)
        dh = _dot_nt(dp_ref[...], w_ref[...])
        xt = x_ref[...]
        r = lax.rsqrt(jnp.mean(xt * xt, axis=-1, keepdims=True) + NORM_EPS)
        xn = xt * r
        dxn = dh * g_ref[...]
        gx_ref[...] = dx2_ref[...] + r * (dxn - xn * jnp.mean(dxn * xn, axis=-1, keepdims=True))
        st_ref[0:1, :] += jnp.sum(dh * xn, axis=0, keepdims=True)
        _finish_exchange(copies, i == last)

    return pl.pallas_call(
        body, grid=(S // tm,), name="inproj_bwd_x",
        in_specs=piece_specs + [_resident((D, NCOL)), _rows(tm, D), _resident((1, D)), _rows(tm, D), ANY],
        out_specs=[_rows(tm, D), pl.BlockSpec((8, D), lambda i: (0, 0)), ANY],
        out_shape=[jax.ShapeDtypeStruct((S, D), F32), jax.ShapeDtypeStruct((8, D), F32)] + landing,
        scratch_shapes=[pltpu.VMEM((tm, NCOL), BF16)] + sems,
        compiler_params=_params(("arbitrary",)),
    )(*pieces, w_bf, x, g1, dx2, pi)


def _inproj_bwd_w(h, dqs, dkvs, gates):
    out = None
    for k in range(NCHIP):
        units = range(k * UNITS_PER_CHUNK, (k + 1) * UNITS_PER_CHUNK)
        tk = 512 if units[0] < OFF_K // LANES else 1024
        nk = S // tk
        pieces, piece_specs, pick = _dproj_sources(units, dqs, dkvs, gates, tk)
        handed_on = [] if out is None else [out]

        def body(*refs, units=units, pick=pick, n_pieces=len(pieces), n_in=1 + len(pieces) + len(handed_on)):
            h_ref, piece_refs = refs[0], refs[1:1 + n_pieces]
            o_ref, tile, acc = refs[n_in:]
            i = pl.program_id(0)

            @pl.when(i == 0)
            def _():
                acc[...] = jnp.zeros_like(acc)

            sources = pick(piece_refs)
            for n, u in enumerate(units):
                tile[:, n * LANES:(n + 1) * LANES] = _dproj_unit(u, *sources, tk)
            acc[...] += _dot_tn(h_ref[...], tile[...])

            @pl.when(i == nk - 1)
            def _():
                o_ref[0] = acc[...].astype(BF16)

        out = pl.pallas_call(
            body, grid=(nk,), name=f"inproj_bwd_w{k}",
            in_specs=[_rows(tk, D)] + piece_specs + [ANY] * len(handed_on),
            out_specs=pl.BlockSpec((1, D, CHUNK), lambda i, k=k: (k, 0, 0)),
            out_shape=jax.ShapeDtypeStruct((NCHIP, D, CHUNK), BF16),
            input_output_aliases={1 + len(pieces): 0} if handed_on else {},
            scratch_shapes=[pltpu.VMEM((tk, CHUNK), BF16), pltpu.VMEM((D, CHUNK), F32)],
            compiler_params=_params(("arbitrary",)),
        )(h, *pieces, *handed_on)
    return out


ROW_FINAL_G, ROW_LOSS, ROW_LN_G, ROW_LN_B, ROW_CONV_B, ROW_TAPS = 0, 1, 2, 3, 4, 8
SMALL_ROWS = 8 + HALO
NDEV = 8


MESH = pl.DeviceIdType.MESH
ANY = pl.BlockSpec(memory_space=pl.ANY)
CHIP_FLIPS = ((1, 0), (0, 1), (1, 1))


def _pos():
    return lax.axis_index("x"), lax.axis_index("y"), lax.axis_index("c")


def _flip(v, f):
    return 1 - v if f else v


def _ds(start, size, align=None):
    return pl.ds(pl.multiple_of(start, align or size), size)


def _place_shards(wi, wo, cw, where):
    steps = 4

    def body(where_ref, wi_ref, wo_ref, cw_ref, wi_full, wo_full, cw_full):
        wi_full[...] = wi_ref[...].astype(BF16)
        wo_full[...] = wo_ref[...].astype(BF16)
        cw_full[...] = cw_ref[...]

    grid_spec = pltpu.PrefetchScalarGridSpec(
        num_scalar_prefetch=1, grid=(steps,),
        in_specs=[pl.BlockSpec((D // steps, CHUNK), lambda i, w: (i, 0)),
                  pl.BlockSpec((WOUT_SHARD // steps, D), lambda i, w: (i, 0)),
                  pl.BlockSpec((HALO, CONVW_SHARD), lambda i, w: (0, 0))],
        out_specs=[pl.BlockSpec((D // steps, CHUNK), lambda i, w: (i, w[0])),
                   pl.BlockSpec((WOUT_SHARD // steps, D), lambda i, w: (w[0] * steps + i, 0)),
                   pl.BlockSpec((HALO, CONVW_SHARD), lambda i, w: (0, w[0]))])
    return pl.pallas_call(
        body, grid_spec=grid_spec, name="place_shards",
        out_shape=[jax.ShapeDtypeStruct((D, NCOL), BF16), jax.ShapeDtypeStruct((WOUT_ROWS, D), BF16),
                   jax.ShapeDtypeStruct((HALO, D), F32)],
        compiler_params=_params(("arbitrary",)),
    )(where, wi, wo, cw)


W_IN, W_OUT, TAPS = range(3)
GATHER_SEMS = 12


def _gather_stages(fulls, send, recv):
    halves = {W_IN: D // 2, W_OUT: WOUT_SHARD // 2, TAPS: HALO // 2}
    x, y, c = _pos()
    chips = {"me": (x, y), "x": (1 - x, y), "y": (x, 1 - y), "diag": (1 - x, 1 - y)}
    SENT = ((("me", 0), "x"), (("me", 1), "x"), (("me", 1), "y"), (("me", 0), "y"), (("x", 0), "y"), (("y", 1), "x"))
    LANDS = ((("x", 0), "x"), (("x", 1), "x"), (("y", 1), "y"), (("y", 0), "y"), (("diag", 0), "y"), (("diag", 1), "x"))
    N_ICI = len(SENT)

    def region(n_th, whose, half, part):
        a, full = fulls[n_th]
        chip = 2 * chips[whose][0] + chips[whose][1]
        n = halves[a] // 2
        row = half * halves[a] + part * n
        if a == W_IN:
            return full.at[_ds(row, n), _ds(chip * CHUNK, CHUNK, 128)]
        if a == W_OUT:
            return full.at[_ds(chip * WOUT_SHARD + row, n), :]
        return full.at[_ds(row, n), _ds(chip * CONVW_SHARD, CONVW_SHARD, 128)]

    def copy(n_th, kind, piece, dev):
        k = GATHER_SEMS * n_th + kind
        return pltpu.make_async_remote_copy(src_ref=piece, dst_ref=piece, send_sem=send.at[k], recv_sem=recv.at[k],
                                            device_id=dev, device_id_type=MESH)

    def sent(a, k):
        if k < N_ICI:
            (whose, part), to = SENT[k]
            return copy(a, k, region(a, whose, c, part), (*chips[to], c))
        (whose, part), _ = LANDS[k - N_ICI]
        return copy(a, k, region(a, whose, c, part), (x, y, 1 - c))

    def wait_arrival(a, k):
        if k < N_ICI:
            (whose, part), frm = LANDS[k]
            copy(a, k, region(a, whose, c, part), (*chips[frm], c)).wait_recv()
        else:
            (whose, part), _ = LANDS[k - N_ICI]
            copy(a, k, region(a, whose, 1 - c, part), (x, y, 1 - c)).wait_recv()

    arrays = range(len(fulls))

    def own_to_neighbours():
        for a in arrays:
            for k in (0, 2, 1, 3):
                sent(a, k).start()

    def pass_on_neighbours():
        for a in arrays:
            for k, onward in ((0, 4), (2, 5), (1, None), (3, None)):
                wait_arrival(a, k)
                if onward is not None:
                    sent(a, onward).start()
                sent(a, k + N_ICI).start()

    def pass_on_diagonal():
        for a in arrays:
            for k in (4, 5):
                wait_arrival(a, k)
                sent(a, k + N_ICI).start()

    def finish():
        for a in arrays:
            for k in range(N_ICI, 2 * N_ICI):
                wait_arrival(a, k)
            for k in range(2 * N_ICI):
                sent(a, k).wait_send()

    return own_to_neighbours, pass_on_neighbours, pass_on_diagonal, finish


def _gather_sems(n_arrays):
    return [pltpu.SemaphoreType.DMA((GATHER_SEMS * n_arrays,)), pltpu.SemaphoreType.DMA((GATHER_SEMS * n_arrays,))]


def _gather_w_in(wi_full):
    def body(_wi, full, send, recv):
        for stage in _gather_stages([(W_IN, full)], send, recv):
            stage()

    return pl.pallas_call(
        body, name="gather_w_in", in_specs=[ANY], out_specs=ANY, input_output_aliases={0: 0},
        out_shape=jax.ShapeDtypeStruct((D, NCOL), BF16), scratch_shapes=_gather_sems(1),
    )(wi_full)


def _half_shape(a):
    return jax.ShapeDtypeStruct((NCHIP, a.shape[1] // 2, a.shape[2]) if a.ndim == 3 else a.shape, a.dtype)


def _exchange_halves(arrays, name):
    n = len(arrays)

    def body(*refs):
        srcs, dsts, (send, recv) = refs[:n], refs[n:2 * n], refs[2 * n:]
        x, y, c = _pos()
        cps = []
        for k, (s_, d_) in enumerate(zip(srcs, dsts)):
            if len(s_.shape) == 3:
                h = s_.shape[1] // 2
                s_ = s_.at[:, _ds((1 - c) * h, h), :]
            cps.append(pltpu.make_async_remote_copy(src_ref=s_, dst_ref=d_, send_sem=send.at[k], recv_sem=recv.at[k],
                                                    device_id=(x, y, 1 - c), device_id_type=MESH))
        for cp in cps:
            cp.start()
        for cp in cps:
            cp.wait()

    return pl.pallas_call(
        body, name=name, in_specs=[ANY] * n, out_specs=[ANY] * n, out_shape=[_half_shape(a) for a in arrays],
        scratch_shapes=[pltpu.SemaphoreType.DMA((n,)), pltpu.SemaphoreType.DMA((n,))],
    )(*arrays)


def _add_halves(arrays, received, name):
    n = len(arrays)

    def body(*refs):
        mine, theirs, outs = refs[:n], refs[n:2 * n], refs[2 * n:]
        c = lax.axis_index("c")
        for m_, t_, o_ in zip(mine, theirs, outs):
            if len(m_.shape) == 3:
                h = m_.shape[1] // 2
                o_[0] = (m_[0, _ds(c * h, h), :].astype(F32) + t_[0].astype(F32)).astype(o_.dtype)
            else:
                o_[...] = m_[...] + t_[...]

    def spec(shape):
        if len(shape) == 3:
            return pl.BlockSpec((1,) + tuple(shape[1:]), lambda k: (k, 0, 0))
        return pl.BlockSpec(tuple(shape), lambda k: (0, 0))

    halves = [_half_shape(a) for a in arrays]
    return pl.pallas_call(
        body, grid=(NCHIP,), name=name,
        in_specs=[spec(a.shape) for a in arrays] + [spec(h.shape) for h in halves],
        out_specs=[spec(h.shape) for h in halves], out_shape=halves,
        compiler_params=_params(("arbitrary",)),
    )(*arrays, *received)


def _chip_exchange_copies(srcs, dsts, send, recv):
    x, y, c = _pos()
    me = 2 * x + y
    pairs = []
    for a in range(len(srcs)):
        for j, (fx, fy) in enumerate(CHIP_FLIPS):
            px, py = _flip(x, fx), _flip(y, fy)
            peer = 2 * px + py
            k = len(CHIP_FLIPS) * a + j
            out = pltpu.make_async_remote_copy(
                src_ref=srcs[a].at[peer] if len(srcs[a].shape) == 3 else srcs[a], dst_ref=dsts[a].at[me],
                send_sem=send.at[k], recv_sem=recv.at[k], device_id=(px, py, c), device_id_type=MESH)
            got = dsts[a].at[peer]
            arrival = pltpu.make_async_remote_copy(
                src_ref=got, dst_ref=got, send_sem=send.at[k], recv_sem=recv.at[k],
                device_id=(px, py, c), device_id_type=MESH)
            pairs.append((out, arrival))
    return pairs


def _sum_chips(ri, ro, rs, pi, po, ps, where):
    def body(w_ref, ri_ref, ro_ref, rs_ref, pi_ref, po_ref, ps_ref, gi_ref, go_ref, gs_ref, g5_ref, loss_ref,
             acc_i, acc_o, acc_s):
        k = pl.program_id(0)
        accs = (acc_i, acc_o, acc_s)

        @pl.when(k == 0)
        def _():
            for acc in accs:
                acc[...] = jnp.zeros_like(acc)

        @pl.when(k == w_ref[0])
        def _():
            for acc, val in zip(accs, (pi_ref[0], po_ref[0], ps_ref[...])):
                acc[...] += val.astype(F32)

        @pl.when(k != w_ref[0])
        def _():
            for acc, ref in zip(accs, (ri_ref, ro_ref, rs_ref)):
                acc[...] += ref[0].astype(F32)

        @pl.when(k == NCHIP - 1)
        def _():
            gi_ref[0] = acc_i[...]
            go_ref[0] = acc_o[...]
            gs_ref[...] = acc_s[...]
            g5_ref[...] = jnp.zeros_like(g5_ref)
            for i, row in enumerate((ROW_CONV_B, ROW_LN_G, ROW_LN_B, ROW_FINAL_G)):
                g5_ref[i + 1:i + 2, :] = acc_s[row:row + 1, :]
            loss = jnp.sum(acc_s[ROW_LOSS:ROW_LOSS + 1, :], axis=1, keepdims=True)
            loss_ref[...] = jnp.broadcast_to(loss, loss_ref.shape)

    def sent(k, w):
        return jnp.where(k == w[0], (k + 1) % NCHIP, k)

    hi, ho = D // 2, WOUT_SHARD // 2
    const = lambda shape: pl.BlockSpec(shape, lambda k, w: (0,) * len(shape))
    grid_spec = pltpu.PrefetchScalarGridSpec(
        num_scalar_prefetch=1, grid=(NCHIP,),
        in_specs=[pl.BlockSpec((1, hi, CHUNK), lambda k, w: (sent(k, w), 0, 0)),
                  pl.BlockSpec((1, ho, D), lambda k, w: (sent(k, w), 0, 0)),
                  pl.BlockSpec((1, SMALL_ROWS, D), lambda k, w: (sent(k, w), 0, 0)),
                  pl.BlockSpec((1, hi, CHUNK), lambda k, w: (w[0], 0, 0)),
                  pl.BlockSpec((1, ho, D), lambda k, w: (w[0], 0, 0)),
                  const((SMALL_ROWS, D))],
        out_specs=[pl.BlockSpec((1, hi, CHUNK), lambda k, w: (w[1], 0, 0)),
                   pl.BlockSpec((1, ho, D), lambda k, w: (w[1], 0, 0)),
                   const((SMALL_ROWS, D)), const((8, D)), const((8, LANES))],
        scratch_shapes=[pltpu.VMEM((hi, CHUNK), F32), pltpu.VMEM((ho, D), F32), pltpu.VMEM((SMALL_ROWS, D), F32)])
    return pl.pallas_call(
        body, grid_spec=grid_spec, name="sum_chips",
        out_shape=[jax.ShapeDtypeStruct((2, hi, CHUNK), F32), jax.ShapeDtypeStruct((2, ho, D), F32),
                   jax.ShapeDtypeStruct((SMALL_ROWS, D), F32), jax.ShapeDtypeStruct((8, D), F32),
                   jax.ShapeDtypeStruct((8, LANES), F32)],
        compiler_params=_params(("arbitrary",)),
    )(where, ri, ro, rs, pi, po, ps)


def _exchange_results(gi2, go2, st):
    flips = [(fx, fy, fc) for fx in (0, 1) for fy in (0, 1) for fc in (0, 1)][1:]

    def body(_gi, _go, st_ref, gi_ref, go_ref, all_ref, send, recv, lsem):
        x, y, c = _pos()
        sib = (x, y, 1 - c)

        def half(k, ref, slot):
            return pltpu.make_async_remote_copy(src_ref=ref.at[slot], dst_ref=ref.at[slot], send_sem=send.at[k],
                                                recv_sem=recv.at[k], device_id=sib, device_id_type=MESH)

        def stat(k, src, slot, dev):
            return pltpu.make_async_remote_copy(src_ref=src, dst_ref=all_ref.at[slot], send_sem=send.at[k],
                                                recv_sem=recv.at[k], device_id=dev, device_id_type=MESH)

        mine = pltpu.make_async_copy(st_ref, all_ref.at[4 * x + 2 * y + c], lsem)
        mine.start()
        sends = [half(k, ref, c) for k, ref in enumerate((gi_ref, go_ref))]
        peers = [(_flip(x, fx), _flip(y, fy), _flip(c, fc)) for fx, fy, fc in flips]
        sends += [stat(2 + k, st_ref, 4 * x + 2 * y + c, dev) for k, dev in enumerate(peers)]
        for cp in sends:
            cp.start()
        for k, ref in enumerate((gi_ref, go_ref)):
            half(k, ref, 1 - c).wait_recv()
        for k, (px, py, pc) in enumerate(peers):
            slot = 4 * px + 2 * py + pc
            stat(2 + k, all_ref.at[slot], slot, (px, py, pc)).wait_recv()
        for cp in sends:
            cp.wait_send()
        mine.wait()

    n = 2 + len(flips)
    return pl.pallas_call(
        body, name="exchange_results",
        in_specs=[ANY, ANY, ANY], out_specs=[ANY, ANY, ANY], input_output_aliases={0: 0, 1: 1},
        out_shape=[jax.ShapeDtypeStruct((2, D // 2, CHUNK), F32), jax.ShapeDtypeStruct((2, WOUT_SHARD // 2, D), F32),
                   jax.ShapeDtypeStruct((NDEV, 8, D), F32)],
        scratch_shapes=[pltpu.SemaphoreType.DMA((n,)), pltpu.SemaphoreType.DMA((n,)), pltpu.SemaphoreType.DMA],
    )(gi2, go2, st)


def _adamw_math(w, g, m, v):
    m2 = ADAM_B1 * m + (1.0 - ADAM_B1) * g
    v2 = ADAM_B2 * v + (1.0 - ADAM_B2) * (g * g)
    m_hat = m2 / (1.0 - ADAM_B1 ** ADAM_STEP)
    v_hat = v2 / (1.0 - ADAM_B2 ** ADAM_STEP)
    delta = -ADAM_LR * (m_hat / (jnp.sqrt(v_hat) + ADAM_EPS) + ADAM_WD * w)
    return delta, m2, v2


def _adamw(w, g, m, v, name):
    rows, cols = w.shape
    tm = 256 if rows % 256 == 0 else rows

    def body(w_ref, g_ref, m_ref, v_ref, d_ref, m2_ref, v2_ref):
        d_ref[...], m2_ref[...], v2_ref[...] = _adamw_math(w_ref[...], g_ref[...], m_ref[...], v_ref[...])

    shape = jax.ShapeDtypeStruct(w.shape, F32)
    return pl.pallas_call(
        body, grid=(rows // tm,), name=name,
        in_specs=[_rows(tm, cols)] * 4, out_specs=[_rows(tm, cols)] * 3, out_shape=[shape] * 3,
        compiler_params=_params(("arbitrary",)),
    )(w, g, m, v)


def _adamw_vectors(g5, first_parts, ws, ms, vs):
    n = len(ws)

    def body(g_ref, parts_ref, *refs):
        ins, g0_ref, outs = refs[:3 * n], refs[3 * n], refs[3 * n + 1:]
        g0 = parts_ref[0, 0:1, :]
        for dev in range(1, NDEV):
            g0 = g0 + parts_ref[dev, 0:1, :]
        g0_ref[...] = g0
        for i in range(n):
            g = g0 if i == 0 else g_ref[i:i + 1, :]
            res = _adamw_math(ins[i][...], g, ins[n + i][...], ins[2 * n + i][...])
            for kind in range(3):
                outs[kind * n + i][...] = res[kind]

    shape = jax.ShapeDtypeStruct((1, D), F32)
    return pl.pallas_call(body, name="adamw_vectors", out_shape=[shape] * (1 + 3 * n), compiler_params=_params())(
        g5, first_parts, *ws, *ms, *vs)


def kernel(x, norm_g, w_in, conv_w, conv_b, conv_ln_g, conv_ln_b, w_out, final_norm_g, loss_target, m_norm_g, m_w_in, m_conv_w, m_conv_b, m_conv_ln_g, m_conv_ln_b, m_w_out, m_final_norm_g, v_norm_g, v_w_in, v_conv_w, v_conv_b, v_conv_ln_g, v_conv_ln_b, v_w_out, v_final_norm_g):
    chip = 2 * lax.axis_index("x") + lax.axis_index("y")
    where = jnp.stack([chip, lax.axis_index("c")]).astype(jnp.int32)
    taps_shard = jnp.pad(conv_w[0], ((0, HALO - CONV_K), (0, 0)))
    wi_full, wo_full, cw_full = _place_shards(w_in[0], w_out[0], taps_shard, where)
    wi_full = _gather_w_in(wi_full)

    gf = final_norm_g[None]
    xb = x[0]
    h, q, k, v, a_gate, c_val, c_glu, c_gate, wo_full, cw_full = _inproj_fwd(xb, norm_g, wi_full, wo_full, cw_full)
    tables = [_bias_table(d) for d in PATTERNS]
    o, lse, y_att = _attn_fwd(q, k, v, tables, a_gate)
    u, y_conv = _conv_fwd(c_val, c_glu, c_gate, cw_full, conv_b, conv_ln_g, conv_ln_b)
    dx2, dy_att, du, dc_gate, dw_out, st_out = _outproj_loss(
        y_att, y_conv, wo_full, xb, loss_target[0], gf, u, c_gate, conv_ln_g, conv_ln_b)
    dc_val, dc_glu, dconv_w = _conv_bwd_taps(du, c_val, c_glu, cw_full)

    early = [dw_out.reshape(NCHIP, WOUT_SHARD, D), jnp.concatenate([st_out, dconv_w], axis=0)]
    po, ps = _add_halves(early, _exchange_halves(early, "exchange_halves_early"), "add_halves_early")
    do, da_gate, delta, ro, rs = _attn_gate_bwd(dy_att, o, a_gate, _head_sum_selectors(), [po, ps])
    dqs, dkvs = zip(*[_attn_bwd(q, k, v, do, lse, delta, t, d) for t, d in zip(tables, PATTERNS)])

    dproj_pieces = (dqs, dkvs, (da_gate, dc_val, dc_glu, dc_gate))
    late = [_inproj_bwd_w(h, *dproj_pieces)]
    (pi,) = _add_halves(late, _exchange_halves(late, "exchange_halves"), "add_halves")
    grad_x, st_in, ri = _inproj_bwd_x(*dproj_pieces, wi_full, xb, norm_g, dx2, pi)
    gi2, go2, g_small, g5, loss8 = _sum_chips(ri, ro, rs, pi, po, ps, where)
    gi2, go2, norm_g_parts = _exchange_results(gi2, go2, st_in)
    g_w_in = gi2.reshape(D, CHUNK)
    g_w_out = go2.reshape(WOUT_SHARD, D)
    g_taps = lax.dynamic_slice(g_small, (ROW_TAPS, chip * CONVW_SHARD), (CONV_K, CONVW_SHARD))

    d_w_in, m2_w_in, v2_w_in = _adamw(w_in[0], g_w_in, m_w_in[0], v_w_in[0], "adamw_w_in")
    d_w_out, m2_w_out, v2_w_out = _adamw(w_out[0], g_w_out, m_w_out[0], v_w_out[0], "adamw_w_out")
    d_taps, m2_taps, v2_taps = _adamw(conv_w[0], g_taps, m_conv_w[0], v_conv_w[0], "adamw_conv_w")
    g_norm, *vec = _adamw_vectors(
        g5, norm_g_parts,
        (norm_g, conv_b, conv_ln_g, conv_ln_b, gf),
        (m_norm_g, m_conv_b, m_conv_ln_g, m_conv_ln_b, m_final_norm_g[None]),
        (v_norm_g, v_conv_b, v_conv_ln_g, v_conv_ln_b, v_final_norm_g[None]))
    d_vec, m2_vec, v2_vec = vec[0:5], vec[5:10], vec[10:15]

    def weight_order(ng, wi, cw, cb, lg, lb, wo, fg):
        return (ng, wi[None], cw[None], cb, lg, lb, wo[None], fg[0])

    grads = weight_order(g_norm, g_w_in, g_taps, g5[1:2], g5[2:3], g5[3:4], g_w_out, g5[4:5])
    deltas = weight_order(d_vec[0], d_w_in, d_taps, d_vec[1], d_vec[2], d_vec[3], d_w_out, d_vec[4])
    new_m = weight_order(m2_vec[0], m2_w_in, m2_taps, m2_vec[1], m2_vec[2], m2_vec[3], m2_w_out, m2_vec[4])
    new_v = weight_order(v2_vec[0], v2_w_in, v2_taps, v2_vec[1], v2_vec[2], v2_vec[3], v2_w_out, v2_vec[4])
    return (loss8[0, 0], grad_x[None], *grads, *deltas, *new_m, *new_v)
```

```python
import jax
import jax.numpy as jnp
from jax import lax
from jax.experimental import pallas as pl
from jax.experimental.pallas import tpu as pltpu

F32 = jnp.float32
BF16 = jnp.bfloat16

S = 4096
D = 1024
LANES = 128
HD = 64
NKV = 4
GQ = 4
KVW = NKV * HD
NCOL = 5632
CONV_K = 31
HALO = 32
BLK = 128
PATTERNS = (1, 4, 16)
NORM_EPS = 1e-6
LN_EPS = 1e-5
NEG = -1e30
OFF_Q, OFF_K, OFF_AG, OFF_CV, OFF_CG, OFF_CGATE = 0, 1024, 1536, 2560, 3584, 4608
NCHIP = 4
CHUNK = NCOL // NCHIP
WOUT_ROWS = 2 * D
WOUT_SHARD = WOUT_ROWS // NCHIP
CONVW_SHARD = D // NCHIP

ADAM_LR, ADAM_B1, ADAM_B2, ADAM_EPS, ADAM_WD, ADAM_STEP = 0.001, 0.9, 0.999, 1e-08, 0.01, 10

VMEM_LIMIT = 56 * 1024 * 1024


def _params(sem=None, vmem=VMEM_LIMIT):
    return pltpu.CompilerParams(dimension_semantics=sem, vmem_limit_bytes=vmem)


def _sigmoid(a):
    return 0.5 * jnp.tanh(0.5 * a) + 0.5


def _rows(tm, width):
    return pl.BlockSpec((tm, width), lambda i: (i, 0))


def _slabs(n):
    return jax.ShapeDtypeStruct((n, S, LANES), F32)


def _slab_rows(n, tm):
    return pl.BlockSpec((n, tm, LANES), lambda i: (0, i, 0))


def _resident(shape):
    return pl.BlockSpec(shape, lambda *_: (0,) * len(shape), pipeline_mode=pl.Buffered(1))


def _dot(a, b):
    return jnp.dot(a, b, preferred_element_type=F32)


def _dot_nt(a, b):
    return lax.dot_general(a, b, (((1,), (1,)), ((), ())), preferred_element_type=F32)


def _dot_tn(a, b):
    return lax.dot_general(a, b, (((0,), (0,)), ((), ())), preferred_element_type=F32)


def _inproj_fwd(x, g1, w_bf, wo_full, cw_full):
    tm = 512
    steps = S // tm

    def body(x_ref, g_ref, w_ref, _wo, _cw, h_ref, q_ref, k_ref, v_ref, ag_ref, cv_ref, cg_ref, cgate_ref,
             wo_ref, cw_ref, send, recv):
        i = pl.program_id(0)
        stages = _gather_stages([(W_OUT, wo_ref), (TAPS, cw_ref)], send, recv)
        for stage, step in zip(stages[:3], (0, steps // 2 - 1, steps - 2)):
            pl.when(i == step)(stage)
        xt = x_ref[...]
        r = lax.rsqrt(jnp.mean(xt * xt, axis=-1, keepdims=True) + NORM_EPS)
        h = (xt * r * g_ref[...]).astype(BF16)
        h_ref[...] = h
        q = _dot(h, w_ref[:, OFF_Q:OFF_Q + D]) * (HD ** -0.5)
        kv = _dot(h, w_ref[:, OFF_K:OFF_K + 2 * KVW])
        for sl in range(D // LANES):
            q_ref[sl] = q[:, sl * LANES:(sl + 1) * LANES]
        for sl in range(KVW // LANES):
            k_ref[sl] = kv[:, sl * LANES:(sl + 1) * LANES]
            v_ref[sl] = kv[:, KVW + sl * LANES:KVW + (sl + 1) * LANES]
        ag_ref[...] = _dot(h, w_ref[:, OFF_AG:OFF_AG + D])
        cv_ref[...] = _dot(h, w_ref[:, OFF_CV:OFF_CV + D])
        cg_ref[...] = _dot(h, w_ref[:, OFF_CG:OFF_CG + D])
        cgate_ref[...] = _dot(h, w_ref[:, OFF_CGATE:OFF_CGATE + D])
        pl.when(i == steps - 1)(stages[3])

    big = jax.ShapeDtypeStruct((S, D), F32)
    return pl.pallas_call(
        body, grid=(steps,), name="inproj_fwd",
        in_specs=[_rows(tm, D), _resident((1, D)), _resident((D, NCOL)), ANY, ANY],
        out_specs=[_rows(tm, D), _slab_rows(D // LANES, tm), _slab_rows(KVW // LANES, tm), _slab_rows(KVW // LANES, tm),
                   _rows(tm, D), _rows(tm, D), _rows(tm, D), _rows(tm, D), ANY, ANY],
        out_shape=[jax.ShapeDtypeStruct((S, D), BF16), _slabs(D // LANES), _slabs(KVW // LANES), _slabs(KVW // LANES),
                   big, big, big, big,
                   jax.ShapeDtypeStruct((WOUT_ROWS, D), BF16), jax.ShapeDtypeStruct((HALO, D), F32)],
        input_output_aliases={3: 8, 4: 9},
        scratch_shapes=_gather_sems(2),
        compiler_params=_params(("arbitrary",)),
    )(x, g1, w_bf, wo_full, cw_full)


def _bias_table(d):
    h = jnp.arange(NKV * GQ, dtype=F32)
    slopes = jnp.exp2(-8.0 * (h + 1.0) / (NKV * GQ))
    qi = jnp.arange(BLK)[:, None]
    kj = jnp.arange(2 * BLK)[None, :]
    dist = BLK + qi - kj
    window = (dist >= 0) & (dist <= BLK)
    bias = -slopes[:, None, None] * (dist * d).astype(F32)[None]
    has_prev = jnp.stack([jnp.broadcast_to(kj >= BLK, (BLK, 2 * BLK)), jnp.ones((BLK, 2 * BLK), bool)])
    valid = window[None] & has_prev
    tab = jnp.where(valid[:, None], bias[None], NEG)
    return tab.reshape(2, NKV, GQ * BLK, 2 * BLK)


def _sub_rows(start, d, align=BLK):
    if d == 1:
        return pl.ds(pl.multiple_of(start, align), BLK)
    return pl.ds(start, BLK, stride=d)


CHUNK_ROWS = 2048
BLOCKS_PER_CHUNK = CHUNK_ROWS // BLK


def _low_lanes(rows=BLK):
    return lax.broadcasted_iota(jnp.int32, (rows, LANES), 1) < HD


def _block_start(idx, d):
    shift = d.bit_length() - 1
    b, r = lax.shift_right_logical(idx, shift), lax.bitwise_and(idx, d - 1)
    start = b * (BLK * d) + r
    return b, start, jnp.maximum(start - BLK * d, r)


def _stack_heads(ref, rows):
    low = _low_lanes()
    t0, t1 = ref[0, rows, :], ref[1, rows, :]
    return jnp.concatenate([jnp.where(low, t0, 0.0), jnp.where(low, 0.0, t0),
                            jnp.where(low, t1, 0.0), jnp.where(low, 0.0, t1)], axis=0).astype(BF16)


def _unstack_heads(dup):
    low = _low_lanes()
    return (jnp.where(low, dup[0:BLK], dup[BLK:2 * BLK]), jnp.where(low, dup[2 * BLK:3 * BLK], dup[3 * BLK:4 * BLK]))


def _kv_dup(ref, prow, rows, odd):
    t = jnp.concatenate([ref[0, prow, :], ref[0, rows, :]], axis=0)
    swapped = pltpu.roll(t, HD, axis=1)
    keep = jnp.logical_xor(_low_lanes(2 * BLK), odd)
    return jnp.where(keep, t, swapped).astype(BF16)


PIECES = 3


def _by_head(tiles):
    lane = lax.broadcasted_iota(jnp.int32, tiles[0].shape, 1)
    out = tiles[0]
    for g in range(1, GQ):
        out = jnp.where(lax.bitwise_and(lane, GQ - 1) == g, tiles[g], out)
    return out


def _minus_in_pieces(x):
    lane = lax.broadcasted_iota(jnp.int32, x.shape, 1)
    hi = (-x).astype(BF16).astype(F32)
    rest = -x - hi
    mid = rest.astype(BF16).astype(F32)
    lo = (rest - mid).astype(BF16).astype(F32)
    return jnp.where(lane < GQ, hi, jnp.where(lane < 2 * GQ, mid, jnp.where(lane < PIECES * GQ, lo, 0.0)))


PITCH_PAD = 4


def _pitches(d):
    return BLK + PITCH_PAD, S // d + PITCH_PAD


def _pull_apart(pairs, d, groups, pitch, back=False):
    def group(g, carry):
        for src, dst in pairs:
            for n in range(src.shape[0]):
                for half in range(d // SUBLANES):
                    together = (n, pl.ds(pl.multiple_of(g * d + half * SUBLANES, SUBLANES), SUBLANES), slice(None))
                    spread = (n, pl.ds(half * SUBLANES * pitch + g, SUBLANES, stride=pitch), slice(None))
                    if back:
                        src[together] = dst[spread]
                    else:
                        dst[spread] = src[together]
        return carry

    lax.fori_loop(0, groups, group, 0, unroll=8)


def _attn_fwd(q, k, v, tables, a_gate):
    tm = 256
    width = GQ * HD

    lane_out = jnp.arange(LANES)[None, :] // HD
    spread_sel = jnp.stack([jnp.arange(LANES)[:, None] == 2 * half + lane_out for half in range(2)]).astype(BF16)

    def body(q_ref, k_ref, v_ref, b1_ref, b2_ref, b3_ref, ag_ref, sel_ref, o_ref, lse_ref, y_ref, op, lp,
             qd, kd, vd, opd, lpd):
        odd = pl.program_id(0) % 2 == 1
        chunk = pl.program_id(1)
        ones = jnp.ones((2 * BLK, LANES), BF16)

        for pat, (d, b_ref) in enumerate(zip(PATTERNS, (b1_ref, b2_ref, b3_ref))):
            apart = d * BLK == CHUNK_ROWS
            pitch, kv_pitch = _pitches(d)
            if apart:
                @pl.when(jnp.logical_and(chunk == 0, jnp.logical_not(odd)))
                def _(d=d, kv_pitch=kv_pitch):
                    _pull_apart([(k_ref, kd), (v_ref, vd)], d, S // d, kv_pitch)

                _pull_apart([(q_ref, qd)], d, BLK, pitch)
            q_in, k_in, v_in = (qd, kd, vd) if apart else (q_ref, k_ref, v_ref)
            o_out, l_out = (opd, lpd) if apart else (op.at[pat], lp.at[pl.ds(pat, 1)])

            def block(idx, carry, d=d, b_ref=b_ref, apart=apart, pitch=pitch, kv_pitch=kv_pitch,
                      q_in=q_in, k_in=k_in, v_in=v_in, o_out=o_out, l_out=l_out):
                b, start, pstart = _block_start(chunk * BLOCKS_PER_CHUNK + idx, d)
                rows, prow = _sub_rows(start, d), _sub_rows(pstart, d)
                mine = _sub_rows(start - chunk * CHUNK_ROWS, d)
                if apart:
                    rows = _sub_rows(idx * kv_pitch + b * BLK, 1, PITCH_PAD)
                    prow = _sub_rows(idx * kv_pitch + jnp.maximum(b - 1, 0) * BLK, 1, PITCH_PAD)
                    mine = _sub_rows(idx * pitch, 1, PITCH_PAD)
                qs = _stack_heads(q_in, mine)
                kw = _kv_dup(k_in, prow, rows, odd)
                vw = _kv_dup(v_in, prow, rows, odd)
                s = _dot_nt(qs, kw) + b_ref[jnp.minimum(b, 1), 0]
                m = jnp.max(s, axis=1, keepdims=True)
                p = jnp.exp(s - m).astype(BF16)
                ol = _dot(p, jnp.concatenate([vw, ones], axis=1))
                l = ol[:, LANES:]
                o_out[0, mine, :], o_out[1, mine, :] = _unstack_heads(ol[:, :LANES] / l)
                l_out[0, mine, :] = _by_head([(m + jnp.log(l))[g * BLK:(g + 1) * BLK] for g in range(GQ)])
                return carry

            lax.fori_loop(0, BLOCKS_PER_CHUNK, block, 0, unroll=2)
            if apart:
                _pull_apart([(op.at[pat], opd), (lp.at[pl.ds(pat, 1)], lpd)], d, BLK, pitch, back=True)

        def mix(t, carry):
            r = pl.ds(pl.multiple_of(t * tm, tm), tm)
            a, b, c = lp[0, r, :], lp[1, r, :], lp[2, r, :]
            m = jnp.maximum(jnp.maximum(a, b), c)
            ea, eb, ec = jnp.exp(a - m), jnp.exp(b - m), jnp.exp(c - m)
            den = ea + eb + ec
            lse_ref[0, r, :] = _minus_in_pieces(m + jnp.log(den))
            inv = 1.0 / den
            for half in range(2):
                def spread(w):
                    hi = w.astype(BF16)
                    lo = (w - hi.astype(F32)).astype(BF16)
                    return _dot(hi, sel_ref[half]) + _dot(lo, sel_ref[half])

                o = (spread(ea * inv) * op[0, half, r, :] + spread(eb * inv) * op[1, half, r, :]
                     + spread(ec * inv) * op[2, half, r, :])
                o_ref[half, r, :] = o
                cols = slice(half * LANES, (half + 1) * LANES)
                ag = ag_ref[r, cols]
                y_ref[r, cols] = (o * (ag * _sigmoid(ag))).astype(BF16)
            return carry

        lax.fori_loop(0, CHUNK_ROWS // tm, mix, 0, unroll=2)

    q_like = pl.BlockSpec((2, CHUNK_ROWS, LANES), lambda j, c: (j, c, 0))
    per_kv = pl.BlockSpec((1, CHUNK_ROWS, LANES), lambda j, c: (j, c, 0))
    kv = pl.BlockSpec((1, S, LANES), lambda j, c: (j // 2, 0, 0))
    bias_spec = pl.BlockSpec((2, 1, GQ * BLK, 2 * BLK), lambda j, c: (0, j, 0, 0))
    group_cols = pl.BlockSpec((CHUNK_ROWS, width), lambda j, c: (c, j))
    return pl.pallas_call(
        body, grid=(NKV, S // CHUNK_ROWS), name="attn_fwd",
        in_specs=[q_like, kv, kv, bias_spec, bias_spec, bias_spec, group_cols,
                  pl.BlockSpec((2, LANES, LANES), lambda j, c: (0, 0, 0))],
        out_specs=[q_like, per_kv, group_cols],
        out_shape=[_slabs(D // LANES), _slabs(NKV), jax.ShapeDtypeStruct((S, D), BF16)],
        scratch_shapes=[pltpu.VMEM((len(PATTERNS), 2, CHUNK_ROWS, LANES), F32),
                        pltpu.VMEM((len(PATTERNS), CHUNK_ROWS, LANES), F32)] + [
            pltpu.VMEM((n, BLOCKS_PER_CHUNK * _pitches(BLOCKS_PER_CHUNK)[whole], LANES), F32)
            for n, whole in ((2, 0), (1, 1), (1, 1), (2, 0), (1, 0))],
        compiler_params=_params(("arbitrary", "arbitrary")),
    )(q, k, v, *tables, a_gate, spread_sel)


def _head_sum_selectors():
    lane_in = jnp.arange(LANES)[:, None] // HD
    return jnp.stack([jnp.broadcast_to(lane_in == h, (LANES, LANES)) for h in range(2)]).astype(BF16)


def _attn_gate_bwd(dy_att, o, a_gate, selectors, chip_sums):
    tm = 256
    last = S // tm - 1
    landing, sems = _exchange_results_of(chip_sums)
    n_sums = len(chip_sums)

    def body(dy_ref, o_ref, ag_ref, e_ref, *refs):
        sums, (do_ref, dag_ref, delta_ref), refs = refs[:n_sums], refs[n_sums:n_sums + 3], refs[n_sums + 3:]
        landed, (send, recv) = refs[:n_sums], refs[n_sums:]
        i = pl.program_id(0)
        copies = _chip_exchange_copies(sums, landed, send, recv)
        _start_exchange(copies, i == 0)
        for j in range(NKV):
            deltas = []
            for sl in (2 * j, 2 * j + 1):
                cols = slice(sl * LANES, (sl + 1) * LANES)
                dy, ag, o_ = dy_ref[:, cols], ag_ref[:, cols], o_ref[sl]
                sg = _sigmoid(ag)
                do = dy * (ag * sg)
                do_ref[sl] = do
                dag_ref[:, cols] = (dy * o_ * (sg * (1.0 + ag * (1.0 - sg)))).astype(BF16)
                prod = do * o_
                hi = prod.astype(BF16)
                lo = (prod - hi.astype(F32)).astype(BF16)
                deltas += [_dot(hi, e_ref[h]) + _dot(lo, e_ref[h]) for h in range(2)]
            delta_ref[j] = _minus_in_pieces(_by_head(deltas))
        _finish_exchange(copies, i == last)

    return pl.pallas_call(
        body, grid=(S // tm,), name="attn_gate_bwd",
        in_specs=[_rows(tm, D), _slab_rows(D // LANES, tm), _rows(tm, D), _resident((2, LANES, LANES))] + [ANY] * n_sums,
        out_specs=[_slab_rows(D // LANES, tm), _rows(tm, D), _slab_rows(NKV, tm)] + [ANY] * n_sums,
        out_shape=[_slabs(D // LANES), jax.ShapeDtypeStruct((S, D), BF16), _slabs(NKV)] + landing,
        scratch_shapes=sems,
        compiler_params=_params(("arbitrary",)),
    )(dy_att, o, a_gate, selectors, *chip_sums)


def _own_pieces(tile):
    lane = lax.broadcasted_iota(jnp.int32, tile.shape, 1)
    head = jnp.where(lane < PIECES * GQ, lax.bitwise_and(lane, GQ - 1), -1)
    return jnp.concatenate([jnp.where(head == g, tile, 0.0) for g in range(GQ)], axis=0).astype(BF16)


def _attn_bwd(q, k, v, do, lse, delta, bias, d):
    apart = d * BLK == CHUNK_ROWS
    pitch, kv_pitch = _pitches(d)

    def body(q_ref, do_ref, l_ref, dl_ref, k_ref, v_ref, b_ref, dq_ref, dkv_ref, acc, *copies):
        odd = pl.program_id(0) % 2 == 1
        chunk = pl.program_id(1)
        dq_out = dq_ref
        if apart:
            qd, dod, ld, dld, kd, vd, dq_out = copies

            @pl.when(jnp.logical_and(chunk == 0, jnp.logical_not(odd)))
            def _():
                _pull_apart([(k_ref, kd), (v_ref, vd)], d, S // d, kv_pitch)

            _pull_apart([(q_ref, qd), (do_ref, dod), (l_ref, ld), (dl_ref, dld)], d, BLK, pitch)
            q_ref, do_ref, l_ref, dl_ref, k_ref, v_ref = qd, dod, ld, dld, kd, vd
        ones = (lax.broadcasted_iota(jnp.int32, (2 * BLK, LANES), 1) < PIECES * GQ).astype(BF16)

        def in_acc(block_idx):
            return pl.ds(pl.multiple_of(block_idx * BLK, BLK), BLK)

        @pl.when(chunk == 0)
        def _():
            acc[...] = jnp.zeros_like(acc)

        def block(idx, carry):
            idx = chunk * BLOCKS_PER_CHUNK + idx
            b, start, pstart = _block_start(idx, d)
            rows, prow = _sub_rows(start, d), _sub_rows(pstart, d)
            mine = _sub_rows(start - chunk * CHUNK_ROWS, d)
            if apart:
                r = idx - b * d
                rows = _sub_rows(r * kv_pitch + b * BLK, 1, PITCH_PAD)
                prow = _sub_rows(r * kv_pitch + jnp.maximum(b - 1, 0) * BLK, 1, PITCH_PAD)
                mine = _sub_rows(r * pitch, 1, PITCH_PAD)
            qs = _stack_heads(q_ref, mine)
            dos = _stack_heads(do_ref, mine)
            kw = _kv_dup(k_ref, prow, rows, odd)
            vw = _kv_dup(v_ref, prow, rows, odd)
            s = _dot_nt(jnp.concatenate([qs, _own_pieces(l_ref[0, mine, :])], axis=1),
                        jnp.concatenate([kw, ones], axis=1)) + b_ref[jnp.minimum(b, 1), 0]
            p = jnp.exp(s)
            dv2 = _dot_tn(p.astype(BF16), dos)
            dp = _dot_nt(jnp.concatenate([dos, _own_pieces(dl_ref[0, mine, :])], axis=1),
                         jnp.concatenate([vw, ones], axis=1))
            ds = (p * dp).astype(BF16)
            dq_out[0, mine, :], dq_out[1, mine, :] = _unstack_heads(_dot(ds, kw))
            dk2 = _dot_tn(ds, qs)
            dkv = jnp.where(_low_lanes(2 * BLK), dk2 + pltpu.roll(dk2, HD, axis=1), dv2 + pltpu.roll(dv2, HD, axis=1))
            acc[in_acc(idx), :] = acc[in_acc(idx), :] + dkv[BLK:]
            before = jnp.where(b >= 1, idx - d, idx)
            acc[in_acc(before), :] = acc[in_acc(before), :] + dkv[:BLK]
            return carry

        lax.fori_loop(0, BLOCKS_PER_CHUNK, block, 0, unroll=16)
        if apart:
            _pull_apart([(dq_ref, dq_out)], d, BLK, pitch, back=True)

        @pl.when(chunk == S // CHUNK_ROWS - 1)
        def _():
            def place(idx, carry):
                _, start, _ = _block_start(idx, d)
                dkv_ref[0, _sub_rows(start, d), :] = acc[in_acc(idx), :]
                return carry

            lax.fori_loop(0, S // BLK, place, 0, unroll=4)

    q_like = pl.BlockSpec((2, CHUNK_ROWS, LANES), lambda j, c: (j, c, 0))
    pieces = pl.BlockSpec((1, CHUNK_ROWS, LANES), lambda j, c: (j, c, 0))
    kv = pl.BlockSpec((1, S, LANES), lambda j, c: (j // 2, 0, 0))
    per_kv = pl.BlockSpec((1, S, LANES), lambda j, c: (j, 0, 0))
    bias_spec = pl.BlockSpec((2, 1, GQ * BLK, 2 * BLK), lambda j, c: (0, j, 0, 0))
    return pl.pallas_call(
        body, grid=(NKV, S // CHUNK_ROWS), name=f"attn_bwd_d{d}",
        in_specs=[q_like, q_like, pieces, pieces, kv, kv, bias_spec],
        out_specs=[q_like, per_kv],
        out_shape=[_slabs(D // LANES), _slabs(NKV)],
        scratch_shapes=[pltpu.VMEM((S, LANES), F32)] + apart * [
            pltpu.VMEM((n, d * rows, LANES), F32)
            for n, rows in ((2, pitch), (2, pitch), (1, pitch), (1, pitch), (1, kv_pitch), (1, kv_pitch), (2, pitch))],
        compiler_params=_params(("arbitrary", "arbitrary")),
    )(q, do, lse, delta, k, v, bias)


CONV_T = 256


def _halo_before(i):
    return (jnp.maximum(i * (CONV_T // HALO) - 1, 0), 0)


def _halo_after(i):
    return (jnp.minimum((i + 1) * (CONV_T // HALO), S // HALO - 1), 0)


SUBLANES = 8
NCH = D // LANES
GROUP = SUBLANES * SUBLANES
COMB_STRIDE = 4


def _comb_base(g, b):
    return g * GROUP + (b // COMB_STRIDE) * (SUBLANES * COMB_STRIDE) + b % COMB_STRIDE


def _comb(ref, cb, base):
    return ref[cb, pl.ds(base, SUBLANES, stride=COMB_STRIDE), :]


def _taps(w_ref, cols):
    return [jnp.broadcast_to(w_ref[j:j + 1, cols], (SUBLANES, LANES)) for j in range(CONV_K)]


def _conv_fwd(c_val, c_glu, c_gate, conv_w, conv_b, ln_g, ln_b):
    T = CONV_T

    def body(cv_ref, cg_ref, cvh_ref, cgh_ref, gate_ref, w_ref, b_ref, lg_ref, lb_ref, u_ref, y_ref, win, us):
        i = pl.program_id(0)
        for cb in range(NCH):
            cols = slice(cb * LANES, (cb + 1) * LANES)
            win[cb, HALO:HALO + T, :] = cv_ref[:, cols] * _sigmoid(cg_ref[:, cols])
            win[cb, 0:HALO, :] = jnp.where(i > 0, cvh_ref[:, cols] * _sigmoid(cgh_ref[:, cols]), 0.0)
        for cb in range(NCH):
            cols = slice(cb * LANES, (cb + 1) * LANES)
            taps = _taps(w_ref, cols)
            bias = jnp.broadcast_to(b_ref[:, cols], (SUBLANES, LANES))

            def group(g, carry):
                for b in range(SUBLANES):
                    base = _comb_base(g, b)
                    acc = bias
                    for j in range(CONV_K):
                        acc = acc + taps[j] * _comb(win, cb, base + (HALO - (CONV_K - 1) + j))
                    us[cb, pl.ds(base, SUBLANES, stride=COMB_STRIDE), :] = acc
                return carry

            lax.fori_loop(0, T // GROUP, group, 0, unroll=2)
        total = us[0]
        for cb in range(1, NCH):
            total = total + us[cb]
        mu = jnp.sum(total, axis=-1, keepdims=True) * (1.0 / D)
        sq = jnp.zeros((T, LANES), F32)
        for cb in range(NCH):
            uc = us[cb] - mu
            sq = sq + uc * uc
        rstd = lax.rsqrt(jnp.sum(sq, axis=-1, keepdims=True) * (1.0 / D) + LN_EPS)
        for cb in range(NCH):
            cols = slice(cb * LANES, (cb + 1) * LANES)
            u = us[cb]
            u_ref[:, cols] = u
            nrm = (u - mu) * rstd * lg_ref[:, cols] + lb_ref[:, cols]
            gate = gate_ref[:, cols]
            y_ref[:, cols] = (nrm * _sigmoid(nrm) * (gate * _sigmoid(gate))).astype(BF16)

    halo = pl.BlockSpec((HALO, D), _halo_before)
    return pl.pallas_call(
        body, grid=(S // T,), name="conv_fwd",
        in_specs=[_rows(T, D), _rows(T, D), halo, halo, _rows(T, D),
                  _resident((HALO, D)), _resident((1, D)), _resident((1, D)), _resident((1, D))],
        out_specs=[_rows(T, D), _rows(T, D)],
        out_shape=[jax.ShapeDtypeStruct((S, D), F32), jax.ShapeDtypeStruct((S, D), BF16)],
        scratch_shapes=[pltpu.VMEM((NCH, T + HALO, LANES), F32), pltpu.VMEM((NCH, T, LANES), F32)],
        compiler_params=_params(("arbitrary",)),
    )(c_val, c_glu, c_val, c_glu, c_gate, conv_w, conv_b, ln_g, ln_b)


def _conv_bwd_taps(du, c_val, c_glu, conv_w):
    T = CONV_T
    last = S // T - 1

    def body(du_ref, dua_ref, cv_ref, cg_ref, cvh_ref, cgh_ref, w_ref, dcv_ref, dcg_ref, dw_ref,
             hwin, dwin, dhs, dw_acc):
        i = pl.program_id(0)

        @pl.when(i == 0)
        def _():
            dw_acc[...] = jnp.zeros_like(dw_acc)

        for cb in range(NCH):
            cols = slice(cb * LANES, (cb + 1) * LANES)
            hwin[cb, HALO:HALO + T, :] = cv_ref[:, cols] * _sigmoid(cg_ref[:, cols])
            hwin[cb, 0:HALO, :] = jnp.where(i > 0, cvh_ref[:, cols] * _sigmoid(cgh_ref[:, cols]), 0.0)
            dwin[cb, 0:T, :] = du_ref[:, cols]
            dwin[cb, T:T + HALO, :] = jnp.where(i < last, dua_ref[:, cols], 0.0)
        for cb in range(NCH):
            cols = slice(cb * LANES, (cb + 1) * LANES)
            taps = _taps(w_ref, cols)

            def group_dh(g, carry):
                for b in range(SUBLANES):
                    base = _comb_base(g, b)
                    acc = jnp.zeros((SUBLANES, LANES), F32)
                    for j in range(CONV_K):
                        acc = acc + taps[j] * _comb(dwin, cb, base + (CONV_K - 1 - j))
                    dhs[cb, pl.ds(base, SUBLANES, stride=COMB_STRIDE), :] = acc
                return carry

            lax.fori_loop(0, T // GROUP, group_dh, 0, unroll=2)

            def group_dw(g, sums):
                for b in range(SUBLANES):
                    base = _comb_base(g, b)
                    d = _comb(dwin, cb, base)
                    sums = tuple(sums[j] + d * _comb(hwin, cb, base + (HALO - (CONV_K - 1) + j))
                                 for j in range(CONV_K))
                return sums

            sums = lax.fori_loop(0, T // GROUP, group_dw, tuple(dw_acc[j, :, cols] for j in range(CONV_K)))
            for j in range(CONV_K):
                dw_acc[j, :, cols] = sums[j]
            dh = dhs[cb]
            cv, sg = cv_ref[:, cols], _sigmoid(cg_ref[:, cols])
            dcv_ref[:, cols] = (dh * sg).astype(BF16)
            dcg_ref[:, cols] = (dh * cv * (sg * (1.0 - sg))).astype(BF16)

        @pl.when(i == last)
        def _():
            dw_ref[...] = jnp.zeros_like(dw_ref)
            for j in range(CONV_K):
                dw_ref[j:j + 1, :] = jnp.sum(dw_acc[j], axis=0, keepdims=True)

    before = pl.BlockSpec((HALO, D), _halo_before)
    after = pl.BlockSpec((HALO, D), _halo_after)
    big = jax.ShapeDtypeStruct((S, D), BF16)
    return pl.pallas_call(
        body, grid=(S // T,), name="conv_bwd_taps",
        in_specs=[_rows(T, D), after, _rows(T, D), _rows(T, D), before, before, _resident((HALO, D))],
        out_specs=[_rows(T, D), _rows(T, D), pl.BlockSpec((HALO, D), lambda i: (0, 0))],
        out_shape=[big, big, jax.ShapeDtypeStruct((HALO, D), F32)],
        scratch_shapes=[pltpu.VMEM((NCH, T + HALO, LANES), F32), pltpu.VMEM((NCH, T + HALO, LANES), F32),
                        pltpu.VMEM((NCH, T, LANES), F32), pltpu.VMEM((CONV_K, SUBLANES, D), F32)],
        compiler_params=_params(("arbitrary",)),
    )(du, du, c_val, c_glu, c_val, c_glu, conv_w)


def _outproj_loss(y_att, y_conv, w_out_bf, x, target, gf, u, c_gate, ln_g, ln_b):
    tm = 256

    def body(ya_ref, yc_ref, w_ref, x_ref, t_ref, gf_ref, u_ref, gate_ref, lg_ref, lb_ref,
             dx2_ref, dya_ref, du_ref, dgate_ref, dw_ref, st_ref, acc):
        @pl.when(pl.program_id(0) == 0)
        def _():
            acc[...] = jnp.zeros_like(acc)
            st_ref[...] = jnp.zeros_like(st_ref)

        ya, yc = ya_ref[...], yc_ref[...]
        x2 = x_ref[...] + _dot(ya, w_ref[0:D, :]) + _dot(yc, w_ref[D:2 * D, :])
        r = lax.rsqrt(jnp.mean(x2 * x2, axis=-1, keepdims=True) + NORM_EPS)
        xn = x2 * r
        err = xn * gf_ref[...] - t_ref[...]
        dout = err * (1.0 / D)
        dxn = dout * gf_ref[...]
        dx2 = r * (dxn - xn * jnp.mean(dxn * xn, axis=-1, keepdims=True))
        dx2_ref[...] = dx2
        dx2b = dx2.astype(BF16)
        dya_ref[...] = _dot_nt(dx2b, w_ref[0:D, :])
        dy = _dot_nt(dx2b, w_ref[D:2 * D, :])
        acc[0:D, :] += _dot_tn(ya, dx2b)
        acc[D:2 * D, :] += _dot_tn(yc, dx2b)
        st_ref[ROW_FINAL_G:ROW_FINAL_G + 1, :] += jnp.sum(dout * xn, axis=0, keepdims=True)
        st_ref[ROW_LOSS:ROW_LOSS + 1, :] += jnp.sum(err * err, axis=0, keepdims=True) * (0.5 / D)

        u, gate = u_ref[...], gate_ref[...]
        mu = jnp.mean(u, axis=-1, keepdims=True)
        uc = u - mu
        rstd = lax.rsqrt(jnp.mean(uc * uc, axis=-1, keepdims=True) + LN_EPS)
        z = uc * rstd
        nrm = z * lg_ref[...] + lb_ref[...]
        sn, sg = _sigmoid(nrm), _sigmoid(gate)
        dgate_ref[...] = (dy * (nrm * sn) * (sg * (1.0 + gate * (1.0 - sg)))).astype(BF16)
        dn = dy * (gate * sg) * (sn * (1.0 + nrm * (1.0 - sn)))
        dz = dn * lg_ref[...]
        du = rstd * (dz - jnp.mean(dz, axis=-1, keepdims=True) - z * jnp.mean(dz * z, axis=-1, keepdims=True))
        du_ref[...] = du
        st_ref[ROW_LN_G:ROW_LN_G + 1, :] += jnp.sum(dn * z, axis=0, keepdims=True)
        st_ref[ROW_LN_B:ROW_LN_B + 1, :] += jnp.sum(dn, axis=0, keepdims=True)
        st_ref[ROW_CONV_B:ROW_CONV_B + 1, :] += jnp.sum(du, axis=0, keepdims=True)

        @pl.when(pl.program_id(0) == S // tm - 1)
        def _():
            dw_ref[...] = acc[...].astype(BF16)

    big = jax.ShapeDtypeStruct((S, D), F32)
    vec = _resident((1, D))
    return pl.pallas_call(
        body, grid=(S // tm,), name="outproj_loss",
        in_specs=[_rows(tm, D), _rows(tm, D), _resident((WOUT_ROWS, D)), _rows(tm, D), _rows(tm, D), vec,
                  _rows(tm, D), _rows(tm, D), vec, vec],
        out_specs=[_rows(tm, D), _rows(tm, D), _rows(tm, D), _rows(tm, D),
                   pl.BlockSpec((WOUT_ROWS, D), lambda i: (0, 0)), pl.BlockSpec((8, D), lambda i: (0, 0))],
        out_shape=[big, big, big, jax.ShapeDtypeStruct((S, D), BF16),
                   jax.ShapeDtypeStruct((WOUT_ROWS, D), BF16), jax.ShapeDtypeStruct((8, D), F32)],
        scratch_shapes=[pltpu.VMEM((WOUT_ROWS, D), F32)],
        compiler_params=_params(("arbitrary",)),
    )(y_att, y_conv, w_out_bf, x, target, gf, u, c_gate, ln_g, ln_b)


UNITS_PER_CHUNK = CHUNK // LANES


def _dproj_unit(u, dqs, dkvs, gates, rows):
    if u < OFF_K // LANES:
        return ((dqs[0][u] + dqs[1][u] + dqs[2][u]) * (HD ** -0.5)).astype(BF16)
    if u < OFF_AG // LANES:
        w = u - OFF_K // LANES
        ta, tb = (dkvs[0][j] + dkvs[1][j] + dkvs[2][j] for j in (2 * (w % 2), 2 * (w % 2) + 1))
        low = _low_lanes(rows)
        if w < 2:
            return jnp.where(low, ta, pltpu.roll(tb, HD, axis=1)).astype(BF16)
        return jnp.where(low, pltpu.roll(ta, HD, axis=1), tb).astype(BF16)
    g, sl = divmod(u - OFF_AG // LANES, D // LANES)
    return gates[g][:, sl * LANES:(sl + 1) * LANES]


def _dproj_sources(units, dqs, dkvs, gates, rows):
    use_q = any(u < OFF_K // LANES for u in units)
    use_kv = any(OFF_K // LANES <= u < OFF_AG // LANES for u in units)
    use_g = sorted({(u - OFF_AG // LANES) // (D // LANES) for u in units if u >= OFF_AG // LANES})
    args = (list(dqs) if use_q else []) + (list(dkvs) if use_kv else []) + [gates[g] for g in use_g]
    specs = ([_slab_rows(D // LANES, rows)] * 3 if use_q else []) + ([_slab_rows(NKV, rows)] * 3 if use_kv else []) \
        + [_rows(rows, D)] * len(use_g)

    def pick(refs):
        refs = list(refs)
        q_refs = [refs.pop(0) for _ in range(3)] if use_q else None
        kv_refs = [refs.pop(0) for _ in range(3)] if use_kv else None
        return q_refs, kv_refs, {g: refs.pop(0) for g in use_g}

    return args, specs, pick


def _exchange_results_of(chip_sums):
    n = len(chip_sums) * len(CHIP_FLIPS)
    shapes = [jax.ShapeDtypeStruct((NCHIP,) + tuple(a.shape[1:] if a.ndim == 3 else a.shape), a.dtype)
              for a in chip_sums]
    return shapes, [pltpu.SemaphoreType.DMA((n,)), pltpu.SemaphoreType.DMA((n,))]


def _start_exchange(copies, first_step):
    @pl.when(first_step)
    def _():
        for out, _ in copies:
            out.start()


def _finish_exchange(copies, last_step):
    @pl.when(last_step)
    def _():
        for _, arrival in copies:
            arrival.wait_recv()
        for out, _ in copies:
            out.wait_send()


def _inproj_bwd_x(dqs, dkvs, gates, w_bf, x, g1, dx2, pi):
    tm = 256
    last = S // tm - 1
    units = range(NCOL // LANES)
    pieces, piece_specs, pick = _dproj_sources(units, dqs, dkvs, gates, tm)
    landing, sems = _exchange_results_of([pi])

    def body(*refs):
        piece_refs, refs = refs[:len(pieces)], refs[len(pieces):]
        w_ref, x_ref, g_ref, dx2_ref, pi_ref, gx_ref, st_ref, ri_ref, dp_ref, send, recv = refs
        i = pl.program_id(0)
        copies = _chip_exchange_copies([pi_ref], [ri_ref], send, recv)
        _start_exchange(copies, i == 0)

        @pl.when(i == 0)
        def _():
            st_ref[...] = jnp.zeros_like(st_ref)

        sources = pick(piece_refs)
        for u in units:
            dp_ref[:, u * LANES:(u + 1) * LANES] = _dproj_unit(u, *sources, tm)
        dh = _dot_nt(dp_ref[...], w_ref[...])
        xt = x_ref[...]
        r = lax.rsqrt(jnp.mean(xt * xt, axis=-1, keepdims=True) + NORM_EPS)
        xn = xt * r
        dxn = dh * g_ref[...]
        gx_ref[...] = dx2_ref[...] + r * (dxn - xn * jnp.mean(dxn * xn, axis=-1, keepdims=True))
        st_ref[0:1, :] += jnp.sum(dh * xn, axis=0, keepdims=True)
        _finish_exchange(copies, i == last)

    return pl.pallas_call(
        body, grid=(S // tm,), name="inproj_bwd_x",
        in_specs=piece_specs + [_resident((D, NCOL)), _rows(tm, D), _resident((1, D)), _rows(tm, D), ANY],
        out_specs=[_rows(tm, D), pl.BlockSpec((8, D), lambda i: (0, 0)), ANY],
        out_shape=[jax.ShapeDtypeStruct((S, D), F32), jax.ShapeDtypeStruct((8, D), F32)] + landing,
        scratch_shapes=[pltpu.VMEM((tm, NCOL), BF16)] + sems,
        compiler_params=_params(("arbitrary",)),
    )(*pieces, w_bf, x, g1, dx2, pi)


def _inproj_bwd_w(h, dqs, dkvs, gates):
    out = None
    for k in range(NCHIP):
        units = range(k * UNITS_PER_CHUNK, (k + 1) * UNITS_PER_CHUNK)
        tk = 512 if units[0] < OFF_K // LANES else 1024
        nk = S // tk
        pieces, piece_specs, pick = _dproj_sources(units, dqs, dkvs, gates, tk)
        handed_on = [] if out is None else [out]

        def body(*refs, units=units, pick=pick, n_pieces=len(pieces), n_in=1 + len(pieces) + len(handed_on)):
            h_ref, piece_refs = refs[0], refs[1:1 + n_pieces]
            o_ref, tile, acc = refs[n_in:]
            i = pl.program_id(0)

            @pl.when(i == 0)
            def _():
                acc[...] = jnp.zeros_like(acc)

            sources = pick(piece_refs)
            for n, u in enumerate(units):
                tile[:, n * LANES:(n + 1) * LANES] = _dproj_unit(u, *sources, tk)
            acc[...] += _dot_tn(h_ref[...], tile[...])

            @pl.when(i == nk - 1)
            def _():
                o_ref[0] = acc[...].astype(BF16)

        out = pl.pallas_call(
            body, grid=(nk,), name=f"inproj_bwd_w{k}",
            in_specs=[_rows(tk, D)] + piece_specs + [ANY] * len(handed_on),
            out_specs=pl.BlockSpec((1, D, CHUNK), lambda i, k=k: (k, 0, 0)),
            out_shape=jax.ShapeDtypeStruct((NCHIP, D, CHUNK), BF16),
            input_output_aliases={1 + len(pieces): 0} if handed_on else {},
            scratch_shapes=[pltpu.VMEM((tk, CHUNK), BF16), pltpu.VMEM((D, CHUNK), F32)],
            compiler_params=_params(("arbitrary",)),
        )(h, *pieces, *handed_on)
    return out


ROW_FINAL_G, ROW_LOSS, ROW_LN_G, ROW_LN_B, ROW_CONV_B, ROW_TAPS = 0, 1, 2, 3, 4, 8
SMALL_ROWS = 8 + HALO
NDEV = 8


MESH = pl.DeviceIdType.MESH
ANY = pl.BlockSpec(memory_space=pl.ANY)
CHIP_FLIPS = ((1, 0), (0, 1), (1, 1))


def _pos():
    return lax.axis_index("x"), lax.axis_index("y"), lax.axis_index("c")


def _flip(v, f):
    return 1 - v if f else v


def _ds(start, size, align=None):
    return pl.ds(pl.multiple_of(start, align or size), size)


def _place_shards(wi, wo, cw, where):
    steps = 4

    def body(where_ref, wi_ref, wo_ref, cw_ref, wi_full, wo_full, cw_full):
        wi_full[...] = wi_ref[...].astype(BF16)
        wo_full[...] = wo_ref[...].astype(BF16)
        cw_full[...] = cw_ref[...]

    grid_spec = pltpu.PrefetchScalarGridSpec(
        num_scalar_prefetch=1, grid=(steps,),
        in_specs=[pl.BlockSpec((D // steps, CHUNK), lambda i, w: (i, 0)),
                  pl.BlockSpec((WOUT_SHARD // steps, D), lambda i, w: (i, 0)),
                  pl.BlockSpec((HALO, CONVW_SHARD), lambda i, w: (0, 0))],
        out_specs=[pl.BlockSpec((D // steps, CHUNK), lambda i, w: (i, w[0])),
                   pl.BlockSpec((WOUT_SHARD // steps, D), lambda i, w: (w[0] * steps + i, 0)),
                   pl.BlockSpec((HALO, CONVW_SHARD), lambda i, w: (0, w[0]))])
    return pl.pallas_call(
        body, grid_spec=grid_spec, name="place_shards",
        out_shape=[jax.ShapeDtypeStruct((D, NCOL), BF16), jax.ShapeDtypeStruct((WOUT_ROWS, D), BF16),
                   jax.ShapeDtypeStruct((HALO, D), F32)],
        compiler_params=_params(("arbitrary",)),
    )(where, wi, wo, cw)


W_IN, W_OUT, TAPS = range(3)
GATHER_SEMS = 12


def _gather_stages(fulls, send, recv):
    halves = {W_IN: D // 2, W_OUT: WOUT_SHARD // 2, TAPS: HALO // 2}
    x, y, c = _pos()
    chips = {"me": (x, y), "x": (1 - x, y), "y": (x, 1 - y), "diag": (1 - x, 1 - y)}
    SENT = ((("me", 0), "x"), (("me", 1), "x"), (("me", 1), "y"), (("me", 0), "y"), (("x", 0), "y"), (("y", 1), "x"))
    LANDS = ((("x", 0), "x"), (("x", 1), "x"), (("y", 1), "y"), (("y", 0), "y"), (("diag", 0), "y"), (("diag", 1), "x"))
    N_ICI = len(SENT)

    def region(n_th, whose, half, part):
        a, full = fulls[n_th]
        chip = 2 * chips[whose][0] + chips[whose][1]
        n = halves[a] // 2
        row = half * halves[a] + part * n
        if a == W_IN:
            return full.at[_ds(row, n), _ds(chip * CHUNK, CHUNK, 128)]
        if a == W_OUT:
            return full.at[_ds(chip * WOUT_SHARD + row, n), :]
        return full.at[_ds(row, n), _ds(chip * CONVW_SHARD, CONVW_SHARD, 128)]

    def copy(n_th, kind, piece, dev):
        k = GATHER_SEMS * n_th + kind
        return pltpu.make_async_remote_copy(src_ref=piece, dst_ref=piece, send_sem=send.at[k], recv_sem=recv.at[k],
                                            device_id=dev, device_id_type=MESH)

    def sent(a, k):
        if k < N_ICI:
            (whose, part), to = SENT[k]
            return copy(a, k, region(a, whose, c, part), (*chips[to], c))
        (whose, part), _ = LANDS[k - N_ICI]
        return copy(a, k, region(a, whose, c, part), (x, y, 1 - c))

    def wait_arrival(a, k):
        if k < N_ICI:
            (whose, part), frm = LANDS[k]
            copy(a, k, region(a, whose, c, part), (*chips[frm], c)).wait_recv()
        else:
            (whose, part), _ = LANDS[k - N_ICI]
            copy(a, k, region(a, whose, 1 - c, part), (x, y, 1 - c)).wait_recv()

    arrays = range(len(fulls))

    def own_to_neighbours():
        for a in arrays:
            for k in (0, 2, 1, 3):
                sent(a, k).start()

    def pass_on_neighbours():
        for a in arrays:
            for k, onward in ((0, 4), (2, 5), (1, None), (3, None)):
                wait_arrival(a, k)
                if onward is not None:
                    sent(a, onward).start()
                sent(a, k + N_ICI).start()

    def pass_on_diagonal():
        for a in arrays:
            for k in (4, 5):
                wait_arrival(a, k)
                sent(a, k + N_ICI).start()

    def finish():
        for a in arrays:
            for k in range(N_ICI, 2 * N_ICI):
                wait_arrival(a, k)
            for k in range(2 * N_ICI):
                sent(a, k).wait_send()

    return own_to_neighbours, pass_on_neighbours, pass_on_diagonal, finish


def _gather_sems(n_arrays):
    return [pltpu.SemaphoreType.DMA((GATHER_SEMS * n_arrays,)), pltpu.SemaphoreType.DMA((GATHER_SEMS * n_arrays,))]


def _gather_w_in(wi_full):
    def body(_wi, full, send, recv):
        for stage in _gather_stages([(W_IN, full)], send, recv):
            stage()

    return pl.pallas_call(
        body, name="gather_w_in", in_specs=[ANY], out_specs=ANY, input_output_aliases={0: 0},
        out_shape=jax.ShapeDtypeStruct((D, NCOL), BF16), scratch_shapes=_gather_sems(1),
    )(wi_full)


def _half_shape(a):
    return jax.ShapeDtypeStruct((NCHIP, a.shape[1] // 2, a.shape[2]) if a.ndim == 3 else a.shape, a.dtype)


def _exchange_halves(arrays, name):
    n = len(arrays)

    def body(*refs):
        srcs, dsts, (send, recv) = refs[:n], refs[n:2 * n], refs[2 * n:]
        x, y, c = _pos()
        cps = []
        for k, (s_, d_) in enumerate(zip(srcs, dsts)):
            if len(s_.shape) == 3:
                h = s_.shape[1] // 2
                s_ = s_.at[:, _ds((1 - c) * h, h), :]
            cps.append(pltpu.make_async_remote_copy(src_ref=s_, dst_ref=d_, send_sem=send.at[k], recv_sem=recv.at[k],
                                                    device_id=(x, y, 1 - c), device_id_type=MESH))
        for cp in cps:
            cp.start()
        for cp in cps:
            cp.wait()

    return pl.pallas_call(
        body, name=name, in_specs=[ANY] * n, out_specs=[ANY] * n, out_shape=[_half_shape(a) for a in arrays],
        scratch_shapes=[pltpu.SemaphoreType.DMA((n,)), pltpu.SemaphoreType.DMA((n,))],
    )(*arrays)


def _add_halves(arrays, received, name):
    n = len(arrays)

    def body(*refs):
        mine, theirs, outs = refs[:n], refs[n:2 * n], refs[2 * n:]
        c = lax.axis_index("c")
        for m_, t_, o_ in zip(mine, theirs, outs):
            if len(m_.shape) == 3:
                h = m_.shape[1] // 2
                o_[0] = (m_[0, _ds(c * h, h), :].astype(F32) + t_[0].astype(F32)).astype(o_.dtype)
            else:
                o_[...] = m_[...] + t_[...]

    def spec(shape):
        if len(shape) == 3:
            return pl.BlockSpec((1,) + tuple(shape[1:]), lambda k: (k, 0, 0))
        return pl.BlockSpec(tuple(shape), lambda k: (0, 0))

    halves = [_half_shape(a) for a in arrays]
    return pl.pallas_call(
        body, grid=(NCHIP,), name=name,
        in_specs=[spec(a.shape) for a in arrays] + [spec(h.shape) for h in halves],
        out_specs=[spec(h.shape) for h in halves], out_shape=halves,
        compiler_params=_params(("arbitrary",)),
    )(*arrays, *received)


def _chip_exchange_copies(srcs, dsts, send, recv):
    x, y, c = _pos()
    me = 2 * x + y
    pairs = []
    for a in range(len(srcs)):
        for j, (fx, fy) in enumerate(CHIP_FLIPS):
            px, py = _flip(x, fx), _flip(y, fy)
            peer = 2 * px + py
            k = len(CHIP_FLIPS) * a + j
            out = pltpu.make_async_remote_copy(
                src_ref=srcs[a].at[peer] if len(srcs[a].shape) == 3 else srcs[a], dst_ref=dsts[a].at[me],
                send_sem=send.at[k], recv_sem=recv.at[k], device_id=(px, py, c), device_id_type=MESH)
            got = dsts[a].at[peer]
            arrival = pltpu.make_async_remote_copy(
                src_ref=got, dst_ref=got, send_sem=send.at[k], recv_sem=recv.at[k],
                device_id=(px, py, c), device_id_type=MESH)
            pairs.append((out, arrival))
    return pairs


def _sum_chips(ri, ro, rs, pi, po, ps, where):
    def body(w_ref, ri_ref, ro_ref, rs_ref, pi_ref, po_ref, ps_ref, gi_ref, go_ref, gs_ref, g5_ref, loss_ref,
             acc_i, acc_o, acc_s):
        k = pl.program_id(0)
        accs = (acc_i, acc_o, acc_s)

        @pl.when(k == 0)
        def _():
            for acc in accs:
                acc[...] = jnp.zeros_like(acc)

        @pl.when(k == w_ref[0])
        def _():
            for acc, val in zip(accs, (pi_ref[0], po_ref[0], ps_ref[...])):
                acc[...] += val.astype(F32)

        @pl.when(k != w_ref[0])
        def _():
            for acc, ref in zip(accs, (ri_ref, ro_ref, rs_ref)):
                acc[...] += ref[0].astype(F32)

        @pl.when(k == NCHIP - 1)
        def _():
            gi_ref[0] = acc_i[...]
            go_ref[0] = acc_o[...]
            gs_ref[...] = acc_s[...]
            g5_ref[...] = jnp.zeros_like(g5_ref)
            for i, row in enumerate((ROW_CONV_B, ROW_LN_G, ROW_LN_B, ROW_FINAL_G)):
                g5_ref[i + 1:i + 2, :] = acc_s[row:row + 1, :]
            loss = jnp.sum(acc_s[ROW_LOSS:ROW_LOSS + 1, :], axis=1, keepdims=True)
            loss_ref[...] = jnp.broadcast_to(loss, loss_ref.shape)

    def sent(k, w):
        return jnp.where(k == w[0], (k + 1) % NCHIP, k)

    hi, ho = D // 2, WOUT_SHARD // 2
    const = lambda shape: pl.BlockSpec(shape, lambda k, w: (0,) * len(shape))
    grid_spec = pltpu.PrefetchScalarGridSpec(
        num_scalar_prefetch=1, grid=(NCHIP,),
        in_specs=[pl.BlockSpec((1, hi, CHUNK), lambda k, w: (sent(k, w), 0, 0)),
                  pl.BlockSpec((1, ho, D), lambda k, w: (sent(k, w), 0, 0)),
                  pl.BlockSpec((1, SMALL_ROWS, D), lambda k, w: (sent(k, w), 0, 0)),
                  pl.BlockSpec((1, hi, CHUNK), lambda k, w: (w[0], 0, 0)),
                  pl.BlockSpec((1, ho, D), lambda k, w: (w[0], 0, 0)),
                  const((SMALL_ROWS, D))],
        out_specs=[pl.BlockSpec((1, hi, CHUNK), lambda k, w: (w[1], 0, 0)),
                   pl.BlockSpec((1, ho, D), lambda k, w: (w[1], 0, 0)),
                   const((SMALL_ROWS, D)), const((8, D)), const((8, LANES))],
        scratch_shapes=[pltpu.VMEM((hi, CHUNK), F32), pltpu.VMEM((ho, D), F32), pltpu.VMEM((SMALL_ROWS, D), F32)])
    return pl.pallas_call(
        body, grid_spec=grid_spec, name="sum_chips",
        out_shape=[jax.ShapeDtypeStruct((2, hi, CHUNK), F32), jax.ShapeDtypeStruct((2, ho, D), F32),
                   jax.ShapeDtypeStruct((SMALL_ROWS, D), F32), jax.ShapeDtypeStruct((8, D), F32),
                   jax.ShapeDtypeStruct((8, LANES), F32)],
        compiler_params=_params(("arbitrary",)),
    )(where, ri, ro, rs, pi, po, ps)


def _exchange_results(gi2, go2, st):
    flips = [(fx, fy, fc) for fx in (0, 1) for fy in (0, 1) for fc in (0, 1)][1:]

    def body(_gi, _go, st_ref, gi_ref, go_ref, all_ref, send, recv, lsem):
        x, y, c = _pos()
        sib = (x, y, 1 - c)

        def half(k, ref, slot):
            return pltpu.make_async_remote_copy(src_ref=ref.at[slot], dst_ref=ref.at[slot], send_sem=send.at[k],
                                                recv_sem=recv.at[k], device_id=sib, device_id_type=MESH)

        def stat(k, src, slot, dev):
            return pltpu.make_async_remote_copy(src_ref=src, dst_ref=all_ref.at[slot], send_sem=send.at[k],
                                                recv_sem=recv.at[k], device_id=dev, device_id_type=MESH)

        mine = pltpu.make_async_copy(st_ref, all_ref.at[4 * x + 2 * y + c], lsem)
        mine.start()
        sends = [half(k, ref, c) for k, ref in enumerate((gi_ref, go_ref))]
        peers = [(_flip(x, fx), _flip(y, fy), _flip(c, fc)) for fx, fy, fc in flips]
        sends += [stat(2 + k, st_ref, 4 * x + 2 * y + c, dev) for k, dev in enumerate(peers)]
        for cp in sends:
            cp.start()
        for k, ref in enumerate((gi_ref, go_ref)):
            half(k, ref, 1 - c).wait_recv()
        for k, (px, py, pc) in enumerate(peers):
            slot = 4 * px + 2 * py + pc
            stat(2 + k, all_ref.at[slot], slot, (px, py, pc)).wait_recv()
        for cp in sends:
            cp.wait_send()
        mine.wait()

    n = 2 + len(flips)
    return pl.pallas_call(
        body, name="exchange_results",
        in_specs=[ANY, ANY, ANY], out_specs=[ANY, ANY, ANY], input_output_aliases={0: 0, 1: 1},
        out_shape=[jax.ShapeDtypeStruct((2, D // 2, CHUNK), F32), jax.ShapeDtypeStruct((2, WOUT_SHARD // 2, D), F32),
                   jax.ShapeDtypeStruct((NDEV, 8, D), F32)],
        scratch_shapes=[pltpu.SemaphoreType.DMA((n,)), pltpu.SemaphoreType.DMA((n,)), pltpu.SemaphoreType.DMA],
    )(gi2, go2, st)


def _adamw_math(w, g, m, v):
    m2 = ADAM_B1 * m + (1.0 - ADAM_B1) * g
    v2 = ADAM_B2 * v + (1.0 - ADAM_B2) * (g * g)
    m_hat = m2 / (1.0 - ADAM_B1 ** ADAM_STEP)
    v_hat = v2 / (1.0 - ADAM_B2 ** ADAM_STEP)
    delta = -ADAM_LR * (m_hat / (jnp.sqrt(v_hat) + ADAM_EPS) + ADAM_WD * w)
    return delta, m2, v2


def _adamw(w, g, m, v, name):
    rows, cols = w.shape
    tm = 256 if rows % 256 == 0 else rows

    def body(w_ref, g_ref, m_ref, v_ref, d_ref, m2_ref, v2_ref):
        d_ref[...], m2_ref[...], v2_ref[...] = _adamw_math(w_ref[...], g_ref[...], m_ref[...], v_ref[...])

    shape = jax.ShapeDtypeStruct(w.shape, F32)
    return pl.pallas_call(
        body, grid=(rows // tm,), name=name,
        in_specs=[_rows(tm, cols)] * 4, out_specs=[_rows(tm, cols)] * 3, out_shape=[shape] * 3,
        compiler_params=_params(("arbitrary",)),
    )(w, g, m, v)


def _adamw_vectors(g5, first_parts, ws, ms, vs):
    n = len(ws)

    def body(g_ref, parts_ref, *refs):
        ins, g0_ref, outs = refs[:3 * n], refs[3 * n], refs[3 * n + 1:]
        g0 = parts_ref[0, 0:1, :]
        for dev in range(1, NDEV):
            g0 = g0 + parts_ref[dev, 0:1, :]
        g0_ref[...] = g0
        for i in range(n):
            g = g0 if i == 0 else g_ref[i:i + 1, :]
            res = _adamw_math(ins[i][...], g, ins[n + i][...], ins[2 * n + i][...])
            for kind in range(3):
                outs[kind * n + i][...] = res[kind]

    shape = jax.ShapeDtypeStruct((1, D), F32)
    return pl.pallas_call(body, name="adamw_vectors", out_shape=[shape] * (1 + 3 * n), compiler_params=_params())(
        g5, first_parts, *ws, *ms, *vs)


def kernel(x, norm_g, w_in, conv_w, conv_b, conv_ln_g, conv_ln_b, w_out, final_norm_g, loss_target, m_norm_g, m_w_in, m_conv_w, m_conv_b, m_conv_ln_g, m_conv_ln_b, m_w_out, m_final_norm_g, v_norm_g, v_w_in, v_conv_w, v_conv_b, v_conv_ln_g, v_conv_ln_b, v_w_out, v_final_norm_g):
    chip = 2 * lax.axis_index("x") + lax.axis_index("y")
    where = jnp.stack([chip, lax.axis_index("c")]).astype(jnp.int32)
    taps_shard = jnp.pad(conv_w[0], ((0, HALO - CONV_K), (0, 0)))
    wi_full, wo_full, cw_full = _place_shards(w_in[0], w_out[0], taps_shard, where)
    wi_full = _gather_w_in(wi_full)

    gf = final_norm_g[None]
    xb = x[0]
    h, q, k, v, a_gate, c_val, c_glu, c_gate, wo_full, cw_full = _inproj_fwd(xb, norm_g, wi_full, wo_full, cw_full)
    tables = [_bias_table(d) for d in PATTERNS]
    o, lse, y_att = _attn_fwd(q, k, v, tables, a_gate)
    u, y_conv = _conv_fwd(c_val, c_glu, c_gate, cw_full, conv_b, conv_ln_g, conv_ln_b)
    dx2, dy_att, du, dc_gate, dw_out, st_out = _outproj_loss(
        y_att, y_conv, wo_full, xb, loss_target[0], gf, u, c_gate, conv_ln_g, conv_ln_b)
    dc_val, dc_glu, dconv_w = _conv_bwd_taps(du, c_val, c_glu, cw_full)

    early = [dw_out.reshape(NCHIP, WOUT_SHARD, D), jnp.concatenate([st_out, dconv_w], axis=0)]
    po, ps = _add_halves(early, _exchange_halves(early, "exchange_halves_early"), "add_halves_early")
    do, da_gate, delta, ro, rs = _attn_gate_bwd(dy_att, o, a_gate, _head_sum_selectors(), [po, ps])
    dqs, dkvs = zip(*[_attn_bwd(q, k, v, do, lse, delta, t, d) for t, d in zip(tables, PATTERNS)])

    dproj_pieces = (dqs, dkvs, (da_gate, dc_val, dc_glu, dc_gate))
    late = [_inproj_bwd_w(h, *dproj_pieces)]
    (pi,) = _add_halves(late, _exchange_halves(late, "exchange_halves"), "add_halves")
    grad_x, st_in, ri = _inproj_bwd_x(*dproj_pieces, wi_full, xb, norm_g, dx2, pi)
    gi2, go2, g_small, g5, loss8 = _sum_chips(ri, ro, rs, pi, po, ps, where)
    gi2, go2, norm_g_parts = _exchange_results(gi2, go2, st_in)
    g_w_in = gi2.reshape(D, CHUNK)
    g_w_out = go2.reshape(WOUT_SHARD, D)
    g_taps = lax.dynamic_slice(g_small, (ROW_TAPS, chip * CONVW_SHARD), (CONV_K, CONVW_SHARD))

    d_w_in, m2_w_in, v2_w_in = _adamw(w_in[0], g_w_in, m_w_in[0], v_w_in[0], "adamw_w_in")
    d_w_out, m2_w_out, v2_w_out = _adamw(w_out[0], g_w_out, m_w_out[0], v_w_out[0], "adamw_w_out")
    d_taps, m2_taps, v2_taps = _adamw(conv_w[0], g_taps, m_conv_w[0], v_conv_w[0], "adamw_conv_w")
    g_norm, *vec = _adamw_vectors(
        g5, norm_g_parts,
        (norm_g, conv_b, conv_ln_g, conv_ln_b, gf),
        (m_norm_g, m_conv_b, m_conv_ln_g, m_conv_ln_b, m_final_norm_g[None]),
        (v_norm_g, v_conv_b, v_conv_ln_g, v_conv_ln_b, v_final_norm_g[None]))
    d_vec, m2_vec, v2_vec = vec[0:5], vec[5:10], vec[10:15]

    def weight_order(ng, wi, cw, cb, lg, lb, wo, fg):
        return (ng, wi[None], cw[None], cb, lg, lb, wo[None], fg[0])

    grads = weight_order(g_norm, g_w_in, g_taps, g5[1:2], g5[2:3], g5[3:4], g_w_out, g5[4:5])
    deltas = weight_order(d_vec[0], d_w_in, d_taps, d_vec[1], d_vec[2], d_vec[3], d_w_out, d_vec[4])
    new_m = weight_order(m2_vec[0], m2_w_in, m2_taps, m2_vec[1], m2_vec[2], m2_vec[3], m2_w_out, m2_vec[4])
    new_v = weight_order(v2_vec[0], v2_w_in, v2_taps, v2_vec[1], v2_vec[2], v2_vec[3], v2_w_out, v2_vec[4])
    return (loss8[0, 0], grad_x[None], *grads, *deltas, *new_m, *new_v)
```

```python
import jax
import jax.numpy as jnp
from jax import lax
from jax.experimental import pallas as pl
from jax.experimental.pallas import tpu as pltpu

F32 = jnp.float32
BF16 = jnp.bfloat16

S = 4096
D = 1024
LANES = 128
HD = 64
NKV = 4
GQ = 4
KVW = NKV * HD
NCOL = 5632
CONV_K = 31
HALO = 32
BLK = 128
PATTERNS = (1, 4, 16)
NORM_EPS = 1e-6
LN_EPS = 1e-5
NEG = -1e30
OFF_Q, OFF_K, OFF_AG, OFF_CV, OFF_CG, OFF_CGATE = 0, 1024, 1536, 2560, 3584, 4608
NCHIP = 4
CHUNK = NCOL // NCHIP
WOUT_ROWS = 2 * D
WOUT_SHARD = WOUT_ROWS // NCHIP
CONVW_SHARD = D // NCHIP

ADAM_LR, ADAM_B1, ADAM_B2, ADAM_EPS, ADAM_WD, ADAM_STEP = 0.001, 0.9, 0.999, 1e-08, 0.01, 10

VMEM_LIMIT = 56 * 1024 * 1024


def _params(sem=None, vmem=VMEM_LIMIT):
    return pltpu.CompilerParams(dimension_semantics=sem, vmem_limit_bytes=vmem)


def _sigmoid(a):
    return 0.5 * jnp.tanh(0.5 * a) + 0.5


def _rows(tm, width):
    return pl.BlockSpec((tm, width), lambda i: (i, 0))


def _slabs(n):
    return jax.ShapeDtypeStruct((n, S, LANES), F32)


def _slab_rows(n, tm):
    return pl.BlockSpec((n, tm, LANES), lambda i: (0, i, 0))


def _resident(shape):
    return pl.BlockSpec(shape, lambda *_: (0,) * len(shape), pipeline_mode=pl.Buffered(1))


def _dot(a, b):
    return jnp.dot(a, b, preferred_element_type=F32)


def _dot_nt(a, b):
    return lax.dot_general(a, b, (((1,), (1,)), ((), ())), preferred_element_type=F32)


def _dot_tn(a, b):
    return lax.dot_general(a, b, (((0,), (0,)), ((), ())), preferred_element_type=F32)


def _inproj_fwd(x, g1, w_bf, wo_full, cw_full):
    tm = 512
    steps = S // tm

    def body(x_ref, g_ref, w_ref, _wo, _cw, h_ref, q_ref, k_ref, v_ref, ag_ref, cv_ref, cg_ref, cgate_ref,
             wo_ref, cw_ref, send, recv):
        i = pl.program_id(0)
        stages = _gather_stages([(W_OUT, wo_ref), (TAPS, cw_ref)], send, recv)
        for stage, step in zip(stages[:3], (0, steps // 2 - 1, steps - 2)):
            pl.when(i == step)(stage)
        xt = x_ref[...]
        r = lax.rsqrt(jnp.mean(xt * xt, axis=-1, keepdims=True) + NORM_EPS)
        h = (xt * r * g_ref[...]).astype(BF16)
        h_ref[...] = h
        q = _dot(h, w_ref[:, OFF_Q:OFF_Q + D]) * (HD ** -0.5)
        kv = _dot(h, w_ref[:, OFF_K:OFF_K + 2 * KVW])
        for sl in range(D // LANES):
            q_ref[sl] = q[:, sl * LANES:(sl + 1) * LANES]
        for sl in range(KVW // LANES):
            k_ref[sl] = kv[:, sl * LANES:(sl + 1) * LANES]
            v_ref[sl] = kv[:, KVW + sl * LANES:KVW + (sl + 1) * LANES]
        ag_ref[...] = _dot(h, w_ref[:, OFF_AG:OFF_AG + D])
        cv_ref[...] = _dot(h, w_ref[:, OFF_CV:OFF_CV + D])
        cg_ref[...] = _dot(h, w_ref[:, OFF_CG:OFF_CG + D])
        cgate_ref[...] = _dot(h, w_ref[:, OFF_CGATE:OFF_CGATE + D])
        pl.when(i == steps - 1)(stages[3])

    big = jax.ShapeDtypeStruct((S, D), F32)
    return pl.pallas_call(
        body, grid=(steps,), name="inproj_fwd",
        in_specs=[_rows(tm, D), _resident((1, D)), _resident((D, NCOL)), ANY, ANY],
        out_specs=[_rows(tm, D), _slab_rows(D // LANES, tm), _slab_rows(KVW // LANES, tm), _slab_rows(KVW // LANES, tm),
                   _rows(tm, D), _rows(tm, D), _rows(tm, D), _rows(tm, D), ANY, ANY],
        out_shape=[jax.ShapeDtypeStruct((S, D), BF16), _slabs(D // LANES), _slabs(KVW // LANES), _slabs(KVW // LANES),
                   big, big, big, big,
                   jax.ShapeDtypeStruct((WOUT_ROWS, D), BF16), jax.ShapeDtypeStruct((HALO, D), F32)],
        input_output_aliases={3: 8, 4: 9},
        scratch_shapes=_gather_sems(2),
        compiler_params=_params(("arbitrary",)),
    )(x, g1, w_bf, wo_full, cw_full)


def _bias_table(d):
    h = jnp.arange(NKV * GQ, dtype=F32)
    slopes = jnp.exp2(-8.0 * (h + 1.0) / (NKV * GQ))
    qi = jnp.arange(BLK)[:, None]
    kj = jnp.arange(2 * BLK)[None, :]
    dist = BLK + qi - kj
    window = (dist >= 0) & (dist <= BLK)
    bias = -slopes[:, None, None] * (dist * d).astype(F32)[None]
    has_prev = jnp.stack([jnp.broadcast_to(kj >= BLK, (BLK, 2 * BLK)), jnp.ones((BLK, 2 * BLK), bool)])
    valid = window[None] & has_prev
    tab = jnp.where(valid[:, None], bias[None], NEG)
    return tab.reshape(2, NKV, GQ * BLK, 2 * BLK)


def _sub_rows(start, d, align=BLK):
    if d == 1:
        return pl.ds(pl.multiple_of(start, align), BLK)
    return pl.ds(start, BLK, stride=d)


CHUNK_ROWS = 2048
BLOCKS_PER_CHUNK = CHUNK_ROWS // BLK


def _low_lanes(rows=BLK):
    return lax.broadcasted_iota(jnp.int32, (rows, LANES), 1) < HD


def _block_start(idx, d):
    shift = d.bit_length() - 1
    b, r = lax.shift_right_logical(idx, shift), lax.bitwise_and(idx, d - 1)
    start = b * (BLK * d) + r
    return b, start, jnp.maximum(start - BLK * d, r)


def _stack_heads(ref, rows):
    low = _low_lanes()
    t0, t1 = ref[0, rows, :], ref[1, rows, :]
    return jnp.concatenate([jnp.where(low, t0, 0.0), jnp.where(low, 0.0, t0),
                            jnp.where(low, t1, 0.0), jnp.where(low, 0.0, t1)], axis=0).astype(BF16)


def _unstack_heads(dup):
    low = _low_lanes()
    return (jnp.where(low, dup[0:BLK], dup[BLK:2 * BLK]), jnp.where(low, dup[2 * BLK:3 * BLK], dup[3 * BLK:4 * BLK]))


def _kv_dup(ref, prow, rows, odd):
    t = jnp.concatenate([ref[0, prow, :], ref[0, rows, :]], axis=0)
    swapped = pltpu.roll(t, HD, axis=1)
    keep = jnp.logical_xor(_low_lanes(2 * BLK), odd)
    return jnp.where(keep, t, swapped).astype(BF16)


PIECES = 3


def _by_head(tiles):
    lane = lax.broadcasted_iota(jnp.int32, tiles[0].shape, 1)
    out = tiles[0]
    for g in range(1, GQ):
        out = jnp.where(lax.bitwise_and(lane, GQ - 1) == g, tiles[g], out)
    return out


def _minus_in_pieces(x):
    lane = lax.broadcasted_iota(jnp.int32, x.shape, 1)
    hi = (-x).astype(BF16).astype(F32)
    rest = -x - hi
    mid = rest.astype(BF16).astype(F32)
    lo = (rest - mid).astype(BF16).astype(F32)
    return jnp.where(lane < GQ, hi, jnp.where(lane < 2 * GQ, mid, jnp.where(lane < PIECES * GQ, lo, 0.0)))


PITCH_PAD = 4


def _pitches(d):
    return BLK + PITCH_PAD, S // d + PITCH_PAD


def _pull_apart(pairs, d, groups, pitch, back=False):
    def group(g, carry):
        for src, dst in pairs:
            for n in range(src.shape[0]):
                for half in range(d // SUBLANES):
                    together = (n, pl.ds(pl.multiple_of(g * d + half * SUBLANES, SUBLANES), SUBLANES), slice(None))
                    spread = (n, pl.ds(half * SUBLANES * pitch + g, SUBLANES, stride=pitch), slice(None))
                    if back:
                        src[together] = dst[spread]
                    else:
                        dst[spread] = src[together]
        return carry

    lax.fori_loop(0, groups, group, 0, unroll=8)


def _attn_fwd(q, k, v, tables, a_gate):
    tm = 256
    width = GQ * HD

    lane_out = jnp.arange(LANES)[None, :] // HD
    spread_sel = jnp.stack([jnp.arange(LANES)[:, None] == 2 * half + lane_out for half in range(2)]).astype(BF16)

    def body(q_ref, k_ref, v_ref, b1_ref, b2_ref, b3_ref, ag_ref, sel_ref, o_ref, lse_ref, y_ref, op, lp,
             qd, kd, vd, opd, lpd):
        odd = pl.program_id(0) % 2 == 1
        chunk = pl.program_id(1)
        ones = jnp.ones((2 * BLK, LANES), BF16)

        for pat, (d, b_ref) in enumerate(zip(PATTERNS, (b1_ref, b2_ref, b3_ref))):
            apart = d * BLK == CHUNK_ROWS
            pitch, kv_pitch = _pitches(d)
            if apart:
                @pl.when(jnp.logical_and(chunk == 0, jnp.logical_not(odd)))
                def _(d=d, kv_pitch=kv_pitch):
                    _pull_apart([(k_ref, kd), (v_ref, vd)], d, S // d, kv_pitch)

                _pull_apart([(q_ref, qd)], d, BLK, pitch)
            q_in, k_in, v_in = (qd, kd, vd) if apart else (q_ref, k_ref, v_ref)
            o_out, l_out = (opd, lpd) if apart else (op.at[pat], lp.at[pl.ds(pat, 1)])

            def block(idx, carry, d=d, b_ref=b_ref, apart=apart, pitch=pitch, kv_pitch=kv_pitch,
                      q_in=q_in, k_in=k_in, v_in=v_in, o_out=o_out, l_out=l_out):
                b, start, pstart = _block_start(chunk * BLOCKS_PER_CHUNK + idx, d)
                rows, prow = _sub_rows(start, d), _sub_rows(pstart, d)
                mine = _sub_rows(start - chunk * CHUNK_ROWS, d)
                if apart:
                    rows = _sub_rows(idx * kv_pitch + b * BLK, 1, PITCH_PAD)
                    prow = _sub_rows(idx * kv_pitch + jnp.maximum(b - 1, 0) * BLK, 1, PITCH_PAD)
                    mine = _sub_rows(idx * pitch, 1, PITCH_PAD)
                qs = _stack_heads(q_in, mine)
                kw = _kv_dup(k_in, prow, rows, odd)
                vw = _kv_dup(v_in, prow, rows, odd)
                s = _dot_nt(qs, kw) + b_ref[jnp.minimum(b, 1), 0]
                m = jnp.max(s, axis=1, keepdims=True)
                p = jnp.exp(s - m).astype(BF16)
                ol = _dot(p, jnp.concatenate([vw, ones], axis=1))
                l = ol[:, LANES:]
                o_out[0, mine, :], o_out[1, mine, :] = _unstack_heads(ol[:, :LANES] / l)
                l_out[0, mine, :] = _by_head([(m + jnp.log(l))[g * BLK:(g + 1) * BLK] for g in range(GQ)])
                return carry

            lax.fori_loop(0, BLOCKS_PER_CHUNK, block, 0, unroll=2)
            if apart:
                _pull_apart([(op.at[pat], opd), (lp.at[pl.ds(pat, 1)], lpd)], d, BLK, pitch, back=True)

        def mix(t, carry):
            r = pl.ds(pl.multiple_of(t * tm, tm), tm)
            a, b, c = lp[0, r, :], lp[1, r, :], lp[2, r, :]
            m = jnp.maximum(jnp.maximum(a, b), c)
            ea, eb, ec = jnp.exp(a - m), jnp.exp(b - m), jnp.exp(c - m)
            den = ea + eb + ec
            lse_ref[0, r, :] = _minus_in_pieces(m + jnp.log(den))
            inv = 1.0 / den
            for half in range(2):
                def spread(w):
                    hi = w.astype(BF16)
                    lo = (w - hi.astype(F32)).astype(BF16)
                    return _dot(hi, sel_ref[half]) + _dot(lo, sel_ref[half])

                o = (spread(ea * inv) * op[0, half, r, :] + spread(eb * inv) * op[1, half, r, :]
                     + spread(ec * inv) * op[2, half, r, :])
                o_ref[half, r, :] = o
                cols = slice(half * LANES, (half + 1) * LANES)
                ag = ag_ref[r, cols]
                y_ref[r, cols] = (o * (ag * _sigmoid(ag))).astype(BF16)
            return carry

        lax.fori_loop(0, CHUNK_ROWS // tm, mix, 0, unroll=4)

    q_like = pl.BlockSpec((2, CHUNK_ROWS, LANES), lambda j, c: (j, c, 0))
    per_kv = pl.BlockSpec((1, CHUNK_ROWS, LANES), lambda j, c: (j, c, 0))
    kv = pl.BlockSpec((1, S, LANES), lambda j, c: (j // 2, 0, 0))
    bias_spec = pl.BlockSpec((2, 1, GQ * BLK, 2 * BLK), lambda j, c: (0, j, 0, 0))
    group_cols = pl.BlockSpec((CHUNK_ROWS, width), lambda j, c: (c, j))
    return pl.pallas_call(
        body, grid=(NKV, S // CHUNK_ROWS), name="attn_fwd",
        in_specs=[q_like, kv, kv, bias_spec, bias_spec, bias_spec, group_cols,
                  pl.BlockSpec((2, LANES, LANES), lambda j, c: (0, 0, 0))],
        out_specs=[q_like, per_kv, group_cols],
        out_shape=[_slabs(D // LANES), _slabs(NKV), jax.ShapeDtypeStruct((S, D), BF16)],
        scratch_shapes=[pltpu.VMEM((len(PATTERNS), 2, CHUNK_ROWS, LANES), F32),
                        pltpu.VMEM((len(PATTERNS), CHUNK_ROWS, LANES), F32)] + [
            pltpu.VMEM((n, BLOCKS_PER_CHUNK * _pitches(BLOCKS_PER_CHUNK)[whole], LANES), F32)
            for n, whole in ((2, 0), (1, 1), (1, 1), (2, 0), (1, 0))],
        compiler_params=_params(("arbitrary", "arbitrary")),
    )(q, k, v, *tables, a_gate, spread_sel)


def _head_sum_selectors():
    lane_in = jnp.arange(LANES)[:, None] // HD
    return jnp.stack([jnp.broadcast_to(lane_in == h, (LANES, LANES)) for h in range(2)]).astype(BF16)


def _attn_gate_bwd(dy_att, o, a_gate, selectors, chip_sums):
    tm = 256
    last = S // tm - 1
    landing, sems = _exchange_results_of(chip_sums)
    n_sums = len(chip_sums)

    def body(dy_ref, o_ref, ag_ref, e_ref, *refs):
        sums, (do_ref, dag_ref, delta_ref), refs = refs[:n_sums], refs[n_sums:n_sums + 3], refs[n_sums + 3:]
        landed, (send, recv) = refs[:n_sums], refs[n_sums:]
        i = pl.program_id(0)
        copies = _chip_exchange_copies(sums, landed, send, recv)
        _start_exchange(copies, i == 0)
        for j in range(NKV):
            deltas = []
            for sl in (2 * j, 2 * j + 1):
                cols = slice(sl * LANES, (sl + 1) * LANES)
                dy, ag, o_ = dy_ref[:, cols], ag_ref[:, cols], o_ref[sl]
                sg = _sigmoid(ag)
                do = dy * (ag * sg)
                do_ref[sl] = do
                dag_ref[:, cols] = (dy * o_ * (sg * (1.0 + ag * (1.0 - sg)))).astype(BF16)
                prod = do * o_
                hi = prod.astype(BF16)
                lo = (prod - hi.astype(F32)).astype(BF16)
                deltas += [_dot(hi, e_ref[h]) + _dot(lo, e_ref[h]) for h in range(2)]
            delta_ref[j] = _minus_in_pieces(_by_head(deltas))
        _finish_exchange(copies, i == last)

    return pl.pallas_call(
        body, grid=(S // tm,), name="attn_gate_bwd",
        in_specs=[_rows(tm, D), _slab_rows(D // LANES, tm), _rows(tm, D), _resident((2, LANES, LANES))] + [ANY] * n_sums,
        out_specs=[_slab_rows(D // LANES, tm), _rows(tm, D), _slab_rows(NKV, tm)] + [ANY] * n_sums,
        out_shape=[_slabs(D // LANES), jax.ShapeDtypeStruct((S, D), BF16), _slabs(NKV)] + landing,
        scratch_shapes=sems,
        compiler_params=_params(("arbitrary",)),
    )(dy_att, o, a_gate, selectors, *chip_sums)


def _own_pieces(tile):
    lane = lax.broadcasted_iota(jnp.int32, tile.shape, 1)
    head = jnp.where(lane < PIECES * GQ, lax.bitwise_and(lane, GQ - 1), -1)
    return jnp.concatenate([jnp.where(head == g, tile, 0.0) for g in range(GQ)], axis=0).astype(BF16)


def _attn_bwd(q, k, v, do, lse, delta, bias, d):
    apart = d * BLK == CHUNK_ROWS
    pitch, kv_pitch = _pitches(d)

    def body(q_ref, do_ref, l_ref, dl_ref, k_ref, v_ref, b_ref, dq_ref, dkv_ref, acc, *copies):
        odd = pl.program_id(0) % 2 == 1
        chunk = pl.program_id(1)
        dq_out = dq_ref
        if apart:
            qd, dod, ld, dld, kd, vd, dq_out = copies

            @pl.when(jnp.logical_and(chunk == 0, jnp.logical_not(odd)))
            def _():
                _pull_apart([(k_ref, kd), (v_ref, vd)], d, S // d, kv_pitch)

            _pull_apart([(q_ref, qd), (do_ref, dod), (l_ref, ld), (dl_ref, dld)], d, BLK, pitch)
            q_ref, do_ref, l_ref, dl_ref, k_ref, v_ref = qd, dod, ld, dld, kd, vd
        ones = (lax.broadcasted_iota(jnp.int32, (2 * BLK, LANES), 1) < PIECES * GQ).astype(BF16)

        def in_acc(block_idx):
            if apart:
                return (lax.shift_right_logical(block_idx, d.bit_length() - 1),
                        _sub_rows(lax.bitwise_and(block_idx, d - 1) * pitch, 1, PITCH_PAD))
            return (pl.ds(pl.multiple_of(block_idx * BLK, BLK), BLK),)

        @pl.when(chunk == 0)
        def _():
            acc[...] = jnp.zeros_like(acc)

        def block(idx, carry):
            idx = chunk * BLOCKS_PER_CHUNK + idx
            b, start, pstart = _block_start(idx, d)
            rows, prow = _sub_rows(start, d), _sub_rows(pstart, d)
            mine = _sub_rows(start - chunk * CHUNK_ROWS, d)
            if apart:
                r = idx - b * d
                rows = _sub_rows(r * kv_pitch + b * BLK, 1, PITCH_PAD)
                prow = _sub_rows(r * kv_pitch + jnp.maximum(b - 1, 0) * BLK, 1, PITCH_PAD)
                mine = _sub_rows(r * pitch, 1, PITCH_PAD)
            qs = _stack_heads(q_ref, mine)
            dos = _stack_heads(do_ref, mine)
            kw = _kv_dup(k_ref, prow, rows, odd)
            vw = _kv_dup(v_ref, prow, rows, odd)
            s = _dot_nt(jnp.concatenate([qs, _own_pieces(l_ref[0, mine, :])], axis=1),
                        jnp.concatenate([kw, ones], axis=1)) + b_ref[jnp.minimum(b, 1), 0]
            p = jnp.exp(s)
            dv2 = _dot_tn(p.astype(BF16), dos)
            dp = _dot_nt(jnp.concatenate([dos, _own_pieces(dl_ref[0, mine, :])], axis=1),
                         jnp.concatenate([vw, ones], axis=1))
            ds = (p * dp).astype(BF16)
            dq_out[0, mine, :], dq_out[1, mine, :] = _unstack_heads(_dot(ds, kw))
            dk2 = _dot_tn(ds, qs)
            dkv = jnp.where(_low_lanes(2 * BLK), dk2 + pltpu.roll(dk2, HD, axis=1), dv2 + pltpu.roll(dv2, HD, axis=1))
            acc[in_acc(idx)] = acc[in_acc(idx)] + dkv[BLK:]
            before = jnp.where(b >= 1, idx - d, idx)
            acc[in_acc(before)] = acc[in_acc(before)] + dkv[:BLK]
            return carry

        lax.fori_loop(0, BLOCKS_PER_CHUNK, block, 0, unroll=16)
        if apart:
            _pull_apart([(dq_ref, dq_out)], d, BLK, pitch, back=True)

        @pl.when(chunk == S // CHUNK_ROWS - 1)
        def _():
            def place(idx, carry):
                _, start, _ = _block_start(idx, d)
                dkv_ref[0, _sub_rows(start, d), :] = acc[in_acc(idx)]
                return carry

            if apart:
                for b in range(S // CHUNK_ROWS):
                    chunk_rows = pl.ds(b * CHUNK_ROWS, CHUNK_ROWS)
                    _pull_apart([(dkv_ref.at[:, chunk_rows], acc.at[pl.ds(b, 1)])], d, BLK, pitch, back=True)
            else:
                lax.fori_loop(0, S // BLK, place, 0, unroll=4)

    q_like = pl.BlockSpec((2, CHUNK_ROWS, LANES), lambda j, c: (j, c, 0))
    pieces = pl.BlockSpec((1, CHUNK_ROWS, LANES), lambda j, c: (j, c, 0))
    kv = pl.BlockSpec((1, S, LANES), lambda j, c: (j // 2, 0, 0))
    per_kv = pl.BlockSpec((1, S, LANES), lambda j, c: (j, 0, 0))
    bias_spec = pl.BlockSpec((2, 1, GQ * BLK, 2 * BLK), lambda j, c: (0, j, 0, 0))
    return pl.pallas_call(
        body, grid=(NKV, S // CHUNK_ROWS), name=f"attn_bwd_d{d}",
        in_specs=[q_like, q_like, pieces, pieces, kv, kv, bias_spec],
        out_specs=[q_like, per_kv],
        out_shape=[_slabs(D // LANES), _slabs(NKV)],
        scratch_shapes=[pltpu.VMEM((S // CHUNK_ROWS, d * pitch, LANES) if apart else (S, LANES), F32)] + apart * [
            pltpu.VMEM((n, d * rows, LANES), F32)
            for n, rows in ((2, pitch), (2, pitch), (1, pitch), (1, pitch), (1, kv_pitch), (1, kv_pitch), (2, pitch))],
        compiler_params=_params(("arbitrary", "arbitrary")),
    )(q, do, lse, delta, k, v, bias)


CONV_T = 256


def _halo_before(i):
    return (jnp.maximum(i * (CONV_T // HALO) - 1, 0), 0)


def _halo_after(i):
    return (jnp.minimum((i + 1) * (CONV_T // HALO), S // HALO - 1), 0)


SUBLANES = 8
NCH = D // LANES
GROUP = SUBLANES * SUBLANES
COMB_STRIDE = 4


def _comb_base(g, b):
    return g * GROUP + (b // COMB_STRIDE) * (SUBLANES * COMB_STRIDE) + b % COMB_STRIDE


def _comb(ref, cb, base):
    return ref[cb, pl.ds(base, SUBLANES, stride=COMB_STRIDE), :]


def _taps(w_ref, cols):
    return [jnp.broadcast_to(w_ref[j:j + 1, cols], (SUBLANES, LANES)) for j in range(CONV_K)]


def _conv_fwd(c_val, c_glu, c_gate, conv_w, conv_b, ln_g, ln_b):
    T = CONV_T

    def body(cv_ref, cg_ref, cvh_ref, cgh_ref, gate_ref, w_ref, b_ref, lg_ref, lb_ref, u_ref, y_ref, win, us):
        i = pl.program_id(0)
        for cb in range(NCH):
            cols = slice(cb * LANES, (cb + 1) * LANES)
            win[cb, HALO:HALO + T, :] = cv_ref[:, cols] * _sigmoid(cg_ref[:, cols])
            win[cb, 0:HALO, :] = jnp.where(i > 0, cvh_ref[:, cols] * _sigmoid(cgh_ref[:, cols]), 0.0)
        for cb in range(NCH):
            cols = slice(cb * LANES, (cb + 1) * LANES)
            taps = _taps(w_ref, cols)
            bias = jnp.broadcast_to(b_ref[:, cols], (SUBLANES, LANES))

            def group(g, carry):
                for b in range(SUBLANES):
                    base = _comb_base(g, b)
                    acc = bias
                    for j in range(CONV_K):
                        acc = acc + taps[j] * _comb(win, cb, base + (HALO - (CONV_K - 1) + j))
                    us[cb, pl.ds(base, SUBLANES, stride=COMB_STRIDE), :] = acc
                return carry

            lax.fori_loop(0, T // GROUP, group, 0, unroll=2)
        total = us[0]
        for cb in range(1, NCH):
            total = total + us[cb]
        mu = jnp.sum(total, axis=-1, keepdims=True) * (1.0 / D)
        sq = jnp.zeros((T, LANES), F32)
        for cb in range(NCH):
            uc = us[cb] - mu
            sq = sq + uc * uc
        rstd = lax.rsqrt(jnp.sum(sq, axis=-1, keepdims=True) * (1.0 / D) + LN_EPS)
        for cb in range(NCH):
            cols = slice(cb * LANES, (cb + 1) * LANES)
            u = us[cb]
            u_ref[:, cols] = u
            nrm = (u - mu) * rstd * lg_ref[:, cols] + lb_ref[:, cols]
            gate = gate_ref[:, cols]
            y_ref[:, cols] = (nrm * _sigmoid(nrm) * (gate * _sigmoid(gate))).astype(BF16)

    halo = pl.BlockSpec((HALO, D), _halo_before)
    return pl.pallas_call(
        body, grid=(S // T,), name="conv_fwd",
        in_specs=[_rows(T, D), _rows(T, D), halo, halo, _rows(T, D),
                  _resident((HALO, D)), _resident((1, D)), _resident((1, D)), _resident((1, D))],
        out_specs=[_rows(T, D), _rows(T, D)],
        out_shape=[jax.ShapeDtypeStruct((S, D), F32), jax.ShapeDtypeStruct((S, D), BF16)],
        scratch_shapes=[pltpu.VMEM((NCH, T + HALO, LANES), F32), pltpu.VMEM((NCH, T, LANES), F32)],
        compiler_params=_params(("arbitrary",)),
    )(c_val, c_glu, c_val, c_glu, c_gate, conv_w, conv_b, ln_g, ln_b)


def _conv_bwd_taps(du, c_val, c_glu, conv_w):
    T = CONV_T
    last = S // T - 1

    def body(du_ref, dua_ref, cv_ref, cg_ref, cvh_ref, cgh_ref, w_ref, dcv_ref, dcg_ref, dw_ref,
             hwin, dwin, dhs, dw_acc):
        i = pl.program_id(0)

        @pl.when(i == 0)
        def _():
            dw_acc[...] = jnp.zeros_like(dw_acc)

        for cb in range(NCH):
            cols = slice(cb * LANES, (cb + 1) * LANES)
            hwin[cb, HALO:HALO + T, :] = cv_ref[:, cols] * _sigmoid(cg_ref[:, cols])
            hwin[cb, 0:HALO, :] = jnp.where(i > 0, cvh_ref[:, cols] * _sigmoid(cgh_ref[:, cols]), 0.0)
            dwin[cb, 0:T, :] = du_ref[:, cols]
            dwin[cb, T:T + HALO, :] = jnp.where(i < last, dua_ref[:, cols], 0.0)
        for cb in range(NCH):
            cols = slice(cb * LANES, (cb + 1) * LANES)
            taps = _taps(w_ref, cols)

            def group_dh(g, carry):
                for b in range(SUBLANES):
                    base = _comb_base(g, b)
                    acc = jnp.zeros((SUBLANES, LANES), F32)
                    for j in range(CONV_K):
                        acc = acc + taps[j] * _comb(dwin, cb, base + (CONV_K - 1 - j))
                    dhs[cb, pl.ds(base, SUBLANES, stride=COMB_STRIDE), :] = acc
                return carry

            lax.fori_loop(0, T // GROUP, group_dh, 0, unroll=2)

            def group_dw(g, sums):
                for b in range(SUBLANES):
                    base = _comb_base(g, b)
                    d = _comb(dwin, cb, base)
                    sums = tuple(sums[j] + d * _comb(hwin, cb, base + (HALO - (CONV_K - 1) + j))
                                 for j in range(CONV_K))
                return sums

            sums = lax.fori_loop(0, T // GROUP, group_dw, tuple(dw_acc[j, :, cols] for j in range(CONV_K)))
            for j in range(CONV_K):
                dw_acc[j, :, cols] = sums[j]
            dh = dhs[cb]
            cv, sg = cv_ref[:, cols], _sigmoid(cg_ref[:, cols])
            dcv_ref[:, cols] = (dh * sg).astype(BF16)
            dcg_ref[:, cols] = (dh * cv * (sg * (1.0 - sg))).astype(BF16)

        @pl.when(i == last)
        def _():
            dw_ref[...] = jnp.zeros_like(dw_ref)
            for j in range(CONV_K):
                dw_ref[j:j + 1, :] = jnp.sum(dw_acc[j], axis=0, keepdims=True)

    before = pl.BlockSpec((HALO, D), _halo_before)
    after = pl.BlockSpec((HALO, D), _halo_after)
    big = jax.ShapeDtypeStruct((S, D), BF16)
    return pl.pallas_call(
        body, grid=(S // T,), name="conv_bwd_taps",
        in_specs=[_rows(T, D), after, _rows(T, D), _rows(T, D), before, before, _resident((HALO, D))],
        out_specs=[_rows(T, D), _rows(T, D), pl.BlockSpec((HALO, D), lambda i: (0, 0))],
        out_shape=[big, big, jax.ShapeDtypeStruct((HALO, D), F32)],
        scratch_shapes=[pltpu.VMEM((NCH, T + HALO, LANES), F32), pltpu.VMEM((NCH, T + HALO, LANES), F32),
                        pltpu.VMEM((NCH, T, LANES), F32), pltpu.VMEM((CONV_K, SUBLANES, D), F32)],
        compiler_params=_params(("arbitrary",)),
    )(du, du, c_val, c_glu, c_val, c_glu, conv_w)


def _outproj_loss(y_att, y_conv, w_out_bf, x, target, gf, u, c_gate, ln_g, ln_b):
    tm = 256

    def body(ya_ref, yc_ref, w_ref, x_ref, t_ref, gf_ref, u_ref, gate_ref, lg_ref, lb_ref,
             dx2_ref, dya_ref, du_ref, dgate_ref, dw_ref, st_ref, acc):
        @pl.when(pl.program_id(0) == 0)
        def _():
            acc[...] = jnp.zeros_like(acc)
            st_ref[...] = jnp.zeros_like(st_ref)

        ya, yc = ya_ref[...], yc_ref[...]
        x2 = x_ref[...] + _dot(ya, w_ref[0:D, :]) + _dot(yc, w_ref[D:2 * D, :])
        r = lax.rsqrt(jnp.mean(x2 * x2, axis=-1, keepdims=True) + NORM_EPS)
        xn = x2 * r
        err = xn * gf_ref[...] - t_ref[...]
        dout = err * (1.0 / D)
        dxn = dout * gf_ref[...]
        dx2 = r * (dxn - xn * jnp.mean(dxn * xn, axis=-1, keepdims=True))
        dx2_ref[...] = dx2
        dx2b = dx2.astype(BF16)
        dya_ref[...] = _dot_nt(dx2b, w_ref[0:D, :])
        dy = _dot_nt(dx2b, w_ref[D:2 * D, :])
        acc[0:D, :] += _dot_tn(ya, dx2b)
        acc[D:2 * D, :] += _dot_tn(yc, dx2b)
        st_ref[ROW_FINAL_G:ROW_FINAL_G + 1, :] += jnp.sum(dout * xn, axis=0, keepdims=True)
        st_ref[ROW_LOSS:ROW_LOSS + 1, :] += jnp.sum(err * err, axis=0, keepdims=True) * (0.5 / D)

        u, gate = u_ref[...], gate_ref[...]
        mu = jnp.mean(u, axis=-1, keepdims=True)
        uc = u - mu
        rstd = lax.rsqrt(jnp.mean(uc * uc, axis=-1, keepdims=True) + LN_EPS)
        z = uc * rstd
        nrm = z * lg_ref[...] + lb_ref[...]
        sn, sg = _sigmoid(nrm), _sigmoid(gate)
        dgate_ref[...] = (dy * (nrm * sn) * (sg * (1.0 + gate * (1.0 - sg)))).astype(BF16)
        dn = dy * (gate * sg) * (sn * (1.0 + nrm * (1.0 - sn)))
        dz = dn * lg_ref[...]
        du = rstd * (dz - jnp.mean(dz, axis=-1, keepdims=True) - z * jnp.mean(dz * z, axis=-1, keepdims=True))
        du_ref[...] = du
        st_ref[ROW_LN_G:ROW_LN_G + 1, :] += jnp.sum(dn * z, axis=0, keepdims=True)
        st_ref[ROW_LN_B:ROW_LN_B + 1, :] += jnp.sum(dn, axis=0, keepdims=True)
        st_ref[ROW_CONV_B:ROW_CONV_B + 1, :] += jnp.sum(du, axis=0, keepdims=True)

        @pl.when(pl.program_id(0) == S // tm - 1)
        def _():
            dw_ref[...] = acc[...].astype(BF16)

    big = jax.ShapeDtypeStruct((S, D), F32)
    vec = _resident((1, D))
    return pl.pallas_call(
        body, grid=(S // tm,), name="outproj_loss",
        in_specs=[_rows(tm, D), _rows(tm, D), _resident((WOUT_ROWS, D)), _rows(tm, D), _rows(tm, D), vec,
                  _rows(tm, D), _rows(tm, D), vec, vec],
        out_specs=[_rows(tm, D), _rows(tm, D), _rows(tm, D), _rows(tm, D),
                   pl.BlockSpec((WOUT_ROWS, D), lambda i: (0, 0)), pl.BlockSpec((8, D), lambda i: (0, 0))],
        out_shape=[big, big, big, jax.ShapeDtypeStruct((S, D), BF16),
                   jax.ShapeDtypeStruct((WOUT_ROWS, D), BF16), jax.ShapeDtypeStruct((8, D), F32)],
        scratch_shapes=[pltpu.VMEM((WOUT_ROWS, D), F32)],
        compiler_params=_params(("arbitrary",)),
    )(y_att, y_conv, w_out_bf, x, target, gf, u, c_gate, ln_g, ln_b)


UNITS_PER_CHUNK = CHUNK // LANES


def _dproj_unit(u, dqs, dkvs, gates, rows):
    if u < OFF_K // LANES:
        return ((dqs[0][u] + dqs[1][u] + dqs[2][u]) * (HD ** -0.5)).astype(BF16)
    if u < OFF_AG // LANES:
        w = u - OFF_K // LANES
        ta, tb = (dkvs[0][j] + dkvs[1][j] + dkvs[2][j] for j in (2 * (w % 2), 2 * (w % 2) + 1))
        low = _low_lanes(rows)
        if w < 2:
            return jnp.where(low, ta, pltpu.roll(tb, HD, axis=1)).astype(BF16)
        return jnp.where(low, pltpu.roll(ta, HD, axis=1), tb).astype(BF16)
    g, sl = divmod(u - OFF_AG // LANES, D // LANES)
    return gates[g][:, sl * LANES:(sl + 1) * LANES]


def _dproj_sources(units, dqs, dkvs, gates, rows):
    use_q = any(u < OFF_K // LANES for u in units)
    use_kv = any(OFF_K // LANES <= u < OFF_AG // LANES for u in units)
    use_g = sorted({(u - OFF_AG // LANES) // (D // LANES) for u in units if u >= OFF_AG // LANES})
    args = (list(dqs) if use_q else []) + (list(dkvs) if use_kv else []) + [gates[g] for g in use_g]
    specs = ([_slab_rows(D // LANES, rows)] * 3 if use_q else []) + ([_slab_rows(NKV, rows)] * 3 if use_kv else []) \
        + [_rows(rows, D)] * len(use_g)

    def pick(refs):
        refs = list(refs)
        q_refs = [refs.pop(0) for _ in range(3)] if use_q else None
        kv_refs = [refs.pop(0) for _ in range(3)] if use_kv else None
        return q_refs, kv_refs, {g: refs.pop(0) for g in use_g}

    return args, specs, pick


def _exchange_results_of(chip_sums):
    n = len(chip_sums) * len(CHIP_FLIPS)
    shapes = [jax.ShapeDtypeStruct((NCHIP,) + tuple(a.shape[1:] if a.ndim == 3 else a.shape), a.dtype)
              for a in chip_sums]
    return shapes, [pltpu.SemaphoreType.DMA((n,)), pltpu.SemaphoreType.DMA((n,))]


def _start_exchange(copies, first_step):
    @pl.when(first_step)
    def _():
        for out, _ in copies:
            out.start()


def _finish_exchange(copies, last_step):
    @pl.when(last_step)
    def _():
        for _, arrival in copies:
            arrival.wait_recv()
        for out, _ in copies:
            out.wait_send()


def _inproj_bwd_x(dqs, dkvs, gates, w_bf, x, g1, dx2, pi):
    tm = 256
    last = S // tm - 1
    units = range(NCOL // LANES)
    pieces, piece_specs, pick = _dproj_sources(units, dqs, dkvs, gates, tm)
    landing, sems = _exchange_results_of([pi])

    def body(*refs):
        piece_refs, refs = refs[:len(pieces)], refs[len(pieces):]
        w_ref, x_ref, g_ref, dx2_ref, pi_ref, gx_ref, st_ref, ri_ref, dp_ref, send, recv = refs
        i = pl.program_id(0)
        copies = _chip_exchange_copies([pi_ref], [ri_ref], send, recv)
        _start_exchange(copies, i == 0)

        @pl.when(i == 0)
        def _():
            st_ref[...] = jnp.zeros_like(st_ref)

        sources = pick(piece_refs)
        for u in units:
            dp_ref[:, u * LANES:(u + 1) * LANES] = _dproj_unit(u, *sources, tm)
        dh = _dot_nt(dp_ref[...], w_ref[...])
        xt = x_ref[...]
        r = lax.rsqrt(jnp.mean(xt * xt, axis=-1, keepdims=True) + NORM_EPS)
        xn = xt * r
        dxn = dh * g_ref[...]
        gx_ref[...] = dx2_ref[...] + r * (dxn - xn * jnp.mean(dxn * xn, axis=-1, keepdims=True))
        st_ref[0:1, :] += jnp.sum(dh * xn, axis=0, keepdims=True)
        _finish_exchange(copies, i == last)

    return pl.pallas_call(
        body, grid=(S // tm,), name="inproj_bwd_x",
        in_specs=piece_specs + [_resident((D, NCOL)), _rows(tm, D), _resident((1, D)), _rows(tm, D), ANY],
        out_specs=[_rows(tm, D), pl.BlockSpec((8, D), lambda i: (0, 0)), ANY],
        out_shape=[jax.ShapeDtypeStruct((S, D), F32), jax.ShapeDtypeStruct((8, D), F32)] + landing,
        scratch_shapes=[pltpu.VMEM((tm, NCOL), BF16)] + sems,
        compiler_params=_params(("arbitrary",)),
    )(*pieces, w_bf, x, g1, dx2, pi)


def _inproj_bwd_w(h, dqs, dkvs, gates):
    out = None
    for k in range(NCHIP):
        units = range(k * UNITS_PER_CHUNK, (k + 1) * UNITS_PER_CHUNK)
        tk = 512 if units[0] < OFF_K // LANES else 1024
        nk = S // tk
        pieces, piece_specs, pick = _dproj_sources(units, dqs, dkvs, gates, tk)
        handed_on = [] if out is None else [out]

        def body(*refs, units=units, pick=pick, n_pieces=len(pieces), n_in=1 + len(pieces) + len(handed_on)):
            h_ref, piece_refs = refs[0], refs[1:1 + n_pieces]
            o_ref, tile, acc = refs[n_in:]
            i = pl.program_id(0)

            @pl.when(i == 0)
            def _():
                acc[...] = jnp.zeros_like(acc)

            sources = pick(piece_refs)
            for n, u in enumerate(units):
                tile[:, n * LANES:(n + 1) * LANES] = _dproj_unit(u, *sources, tk)
            acc[...] += _dot_tn(h_ref[...], tile[...])

            @pl.when(i == nk - 1)
            def _():
                o_ref[0] = acc[...].astype(BF16)

        out = pl.pallas_call(
            body, grid=(nk,), name=f"inproj_bwd_w{k}",
            in_specs=[_rows(tk, D)] + piece_specs + [ANY] * len(handed_on),
            out_specs=pl.BlockSpec((1, D, CHUNK), lambda i, k=k: (k, 0, 0)),
            out_shape=jax.ShapeDtypeStruct((NCHIP, D, CHUNK), BF16),
            input_output_aliases={1 + len(pieces): 0} if handed_on else {},
            scratch_shapes=[pltpu.VMEM((tk, CHUNK), BF16), pltpu.VMEM((D, CHUNK), F32)],
            compiler_params=_params(("arbitrary",)),
        )(h, *pieces, *handed_on)
    return out


ROW_FINAL_G, ROW_LOSS, ROW_LN_G, ROW_LN_B, ROW_CONV_B, ROW_TAPS = 0, 1, 2, 3, 4, 8
SMALL_ROWS = 8 + HALO
NDEV = 8


MESH = pl.DeviceIdType.MESH
ANY = pl.BlockSpec(memory_space=pl.ANY)
CHIP_FLIPS = ((1, 0), (0, 1), (1, 1))


def _pos():
    return lax.axis_index("x"), lax.axis_index("y"), lax.axis_index("c")


def _flip(v, f):
    return 1 - v if f else v


def _ds(start, size, align=None):
    return pl.ds(pl.multiple_of(start, align or size), size)


def _place_shards(wi, wo, cw, where):
    steps = 4

    def body(where_ref, wi_ref, wo_ref, cw_ref, wi_full, wo_full, cw_full):
        wi_full[...] = wi_ref[...].astype(BF16)
        wo_full[...] = wo_ref[...].astype(BF16)
        cw_full[...] = cw_ref[...]

    grid_spec = pltpu.PrefetchScalarGridSpec(
        num_scalar_prefetch=1, grid=(steps,),
        in_specs=[pl.BlockSpec((D // steps, CHUNK), lambda i, w: (i, 0)),
                  pl.BlockSpec((WOUT_SHARD // steps, D), lambda i, w: (i, 0)),
                  pl.BlockSpec((HALO, CONVW_SHARD), lambda i, w: (0, 0))],
        out_specs=[pl.BlockSpec((D // steps, CHUNK), lambda i, w: (i, w[0])),
                   pl.BlockSpec((WOUT_SHARD // steps, D), lambda i, w: (w[0] * steps + i, 0)),
                   pl.BlockSpec((HALO, CONVW_SHARD), lambda i, w: (0, w[0]))])
    return pl.pallas_call(
        body, grid_spec=grid_spec, name="place_shards",
        out_shape=[jax.ShapeDtypeStruct((D, NCOL), BF16), jax.ShapeDtypeStruct((WOUT_ROWS, D), BF16),
                   jax.ShapeDtypeStruct((HALO, D), F32)],
        compiler_params=_params(("arbitrary",)),
    )(where, wi, wo, cw)


W_IN, W_OUT, TAPS = range(3)
GATHER_SEMS = 12


def _gather_stages(fulls, send, recv):
    halves = {W_IN: D // 2, W_OUT: WOUT_SHARD // 2, TAPS: HALO // 2}
    x, y, c = _pos()
    chips = {"me": (x, y), "x": (1 - x, y), "y": (x, 1 - y), "diag": (1 - x, 1 - y)}
    SENT = ((("me", 0), "x"), (("me", 1), "x"), (("me", 1), "y"), (("me", 0), "y"), (("x", 0), "y"), (("y", 1), "x"))
    LANDS = ((("x", 0), "x"), (("x", 1), "x"), (("y", 1), "y"), (("y", 0), "y"), (("diag", 0), "y"), (("diag", 1), "x"))
    N_ICI = len(SENT)

    def region(n_th, whose, half, part):
        a, full = fulls[n_th]
        chip = 2 * chips[whose][0] + chips[whose][1]
        n = halves[a] // 2
        row = half * halves[a] + part * n
        if a == W_IN:
            return full.at[_ds(row, n), _ds(chip * CHUNK, CHUNK, 128)]
        if a == W_OUT:
            return full.at[_ds(chip * WOUT_SHARD + row, n), :]
        return full.at[_ds(row, n), _ds(chip * CONVW_SHARD, CONVW_SHARD, 128)]

    def copy(n_th, kind, piece, dev):
        k = GATHER_SEMS * n_th + kind
        return pltpu.make_async_remote_copy(src_ref=piece, dst_ref=piece, send_sem=send.at[k], recv_sem=recv.at[k],
                                            device_id=dev, device_id_type=MESH)

    def sent(a, k):
        if k < N_ICI:
            (whose, part), to = SENT[k]
            return copy(a, k, region(a, whose, c, part), (*chips[to], c))
        (whose, part), _ = LANDS[k - N_ICI]
        return copy(a, k, region(a, whose, c, part), (x, y, 1 - c))

    def wait_arrival(a, k):
        if k < N_ICI:
            (whose, part), frm = LANDS[k]
            copy(a, k, region(a, whose, c, part), (*chips[frm], c)).wait_recv()
        else:
            (whose, part), _ = LANDS[k - N_ICI]
            copy(a, k, region(a, whose, 1 - c, part), (x, y, 1 - c)).wait_recv()

    arrays = range(len(fulls))

    def own_to_neighbours():
        for a in arrays:
            for k in (0, 2, 1, 3):
                sent(a, k).start()

    def pass_on_neighbours():
        for a in arrays:
            for k, onward in ((0, 4), (2, 5), (1, None), (3, None)):
                wait_arrival(a, k)
                if onward is not None:
                    sent(a, onward).start()
                sent(a, k + N_ICI).start()

    def pass_on_diagonal():
        for a in arrays:
            for k in (4, 5):
                wait_arrival(a, k)
                sent(a, k + N_ICI).start()

    def finish():
        for a in arrays:
            for k in range(N_ICI, 2 * N_ICI):
                wait_arrival(a, k)
            for k in range(2 * N_ICI):
                sent(a, k).wait_send()

    return own_to_neighbours, pass_on_neighbours, pass_on_diagonal, finish


def _gather_sems(n_arrays):
    return [pltpu.SemaphoreType.DMA((GATHER_SEMS * n_arrays,)), pltpu.SemaphoreType.DMA((GATHER_SEMS * n_arrays,))]


def _gather_w_in(wi_full):
    def body(_wi, full, send, recv):
        for stage in _gather_stages([(W_IN, full)], send, recv):
            stage()

    return pl.pallas_call(
        body, name="gather_w_in", in_specs=[ANY], out_specs=ANY, input_output_aliases={0: 0},
        out_shape=jax.ShapeDtypeStruct((D, NCOL), BF16), scratch_shapes=_gather_sems(1),
    )(wi_full)


def _half_shape(a):
    return jax.ShapeDtypeStruct((NCHIP, a.shape[1] // 2, a.shape[2]) if a.ndim == 3 else a.shape, a.dtype)


def _exchange_halves(arrays, name):
    n = len(arrays)

    def body(*refs):
        srcs, dsts, (send, recv) = refs[:n], refs[n:2 * n], refs[2 * n:]
        x, y, c = _pos()
        cps = []
        for k, (s_, d_) in enumerate(zip(srcs, dsts)):
            if len(s_.shape) == 3:
                h = s_.shape[1] // 2
                s_ = s_.at[:, _ds((1 - c) * h, h), :]
            cps.append(pltpu.make_async_remote_copy(src_ref=s_, dst_ref=d_, send_sem=send.at[k], recv_sem=recv.at[k],
                                                    device_id=(x, y, 1 - c), device_id_type=MESH))
        for cp in cps:
            cp.start()
        for cp in cps:
            cp.wait()

    return pl.pallas_call(
        body, name=name, in_specs=[ANY] * n, out_specs=[ANY] * n, out_shape=[_half_shape(a) for a in arrays],
        scratch_shapes=[pltpu.SemaphoreType.DMA((n,)), pltpu.SemaphoreType.DMA((n,))],
    )(*arrays)


def _add_halves(arrays, received, name):
    n = len(arrays)

    def body(*refs):
        mine, theirs, outs = refs[:n], refs[n:2 * n], refs[2 * n:]
        c = lax.axis_index("c")
        for m_, t_, o_ in zip(mine, theirs, outs):
            if len(m_.shape) == 3:
                h = m_.shape[1] // 2
                o_[0] = (m_[0, _ds(c * h, h), :].astype(F32) + t_[0].astype(F32)).astype(o_.dtype)
            else:
                o_[...] = m_[...] + t_[...]

    def spec(shape):
        if len(shape) == 3:
            return pl.BlockSpec((1,) + tuple(shape[1:]), lambda k: (k, 0, 0))
        return pl.BlockSpec(tuple(shape), lambda k: (0, 0))

    halves = [_half_shape(a) for a in arrays]
    return pl.pallas_call(
        body, grid=(NCHIP,), name=name,
        in_specs=[spec(a.shape) for a in arrays] + [spec(h.shape) for h in halves],
        out_specs=[spec(h.shape) for h in halves], out_shape=halves,
        compiler_params=_params(("arbitrary",)),
    )(*arrays, *received)


def _chip_exchange_copies(srcs, dsts, send, recv):
    x, y, c = _pos()
    me = 2 * x + y
    pairs = []
    for a in range(len(srcs)):
        for j, (fx, fy) in enumerate(CHIP_FLIPS):
            px, py = _flip(x, fx), _flip(y, fy)
            peer = 2 * px + py
            k = len(CHIP_FLIPS) * a + j
            out = pltpu.make_async_remote_copy(
                src_ref=srcs[a].at[peer] if len(srcs[a].shape) == 3 else srcs[a], dst_ref=dsts[a].at[me],
                send_sem=send.at[k], recv_sem=recv.at[k], device_id=(px, py, c), device_id_type=MESH)
            got = dsts[a].at[peer]
            arrival = pltpu.make_async_remote_copy(
                src_ref=got, dst_ref=got, send_sem=send.at[k], recv_sem=recv.at[k],
                device_id=(px, py, c), device_id_type=MESH)
            pairs.append((out, arrival))
    return pairs


def _sum_chips(ri, ro, rs, pi, po, ps, where):
    def body(w_ref, ri_ref, ro_ref, rs_ref, pi_ref, po_ref, ps_ref, gi_ref, go_ref, gs_ref, g5_ref, loss_ref,
             acc_i, acc_o, acc_s):
        k = pl.program_id(0)
        accs = (acc_i, acc_o, acc_s)

        @pl.when(k == 0)
        def _():
            for acc in accs:
                acc[...] = jnp.zeros_like(acc)

        @pl.when(k == w_ref[0])
        def _():
            for acc, val in zip(accs, (pi_ref[0], po_ref[0], ps_ref[...])):
                acc[...] += val.astype(F32)

        @pl.when(k != w_ref[0])
        def _():
            for acc, ref in zip(accs, (ri_ref, ro_ref, rs_ref)):
                acc[...] += ref[0].astype(F32)

        @pl.when(k == NCHIP - 1)
        def _():
            gi_ref[0] = acc_i[...]
            go_ref[0] = acc_o[...]
            gs_ref[...] = acc_s[...]
            g5_ref[...] = jnp.zeros_like(g5_ref)
            for i, row in enumerate((ROW_CONV_B, ROW_LN_G, ROW_LN_B, ROW_FINAL_G)):
                g5_ref[i + 1:i + 2, :] = acc_s[row:row + 1, :]
            loss = jnp.sum(acc_s[ROW_LOSS:ROW_LOSS + 1, :], axis=1, keepdims=True)
            loss_ref[...] = jnp.broadcast_to(loss, loss_ref.shape)

    def sent(k, w):
        return jnp.where(k == w[0], (k + 1) % NCHIP, k)

    hi, ho = D // 2, WOUT_SHARD // 2
    const = lambda shape: pl.BlockSpec(shape, lambda k, w: (0,) * len(shape))
    grid_spec = pltpu.PrefetchScalarGridSpec(
        num_scalar_prefetch=1, grid=(NCHIP,),
        in_specs=[pl.BlockSpec((1, hi, CHUNK), lambda k, w: (sent(k, w), 0, 0)),
                  pl.BlockSpec((1, ho, D), lambda k, w: (sent(k, w), 0, 0)),
                  pl.BlockSpec((1, SMALL_ROWS, D), lambda k, w: (sent(k, w), 0, 0)),
                  pl.BlockSpec((1, hi, CHUNK), lambda k, w: (w[0], 0, 0)),
                  pl.BlockSpec((1, ho, D), lambda k, w: (w[0], 0, 0)),
                  const((SMALL_ROWS, D))],
        out_specs=[pl.BlockSpec((1, hi, CHUNK), lambda k, w: (w[1], 0, 0)),
                   pl.BlockSpec((1, ho, D), lambda k, w: (w[1], 0, 0)),
                   const((SMALL_ROWS, D)), const((8, D)), const((8, LANES))],
        scratch_shapes=[pltpu.VMEM((hi, CHUNK), F32), pltpu.VMEM((ho, D), F32), pltpu.VMEM((SMALL_ROWS, D), F32)])
    return pl.pallas_call(
        body, grid_spec=grid_spec, name="sum_chips",
        out_shape=[jax.ShapeDtypeStruct((2, hi, CHUNK), F32), jax.ShapeDtypeStruct((2, ho, D), F32),
                   jax.ShapeDtypeStruct((SMALL_ROWS, D), F32), jax.ShapeDtypeStruct((8, D), F32),
                   jax.ShapeDtypeStruct((8, LANES), F32)],
        compiler_params=_params(("arbitrary",)),
    )(where, ri, ro, rs, pi, po, ps)


def _exchange_results(gi2, go2, st):
    flips = [(fx, fy, fc) for fx in (0, 1) for fy in (0, 1) for fc in (0, 1)][1:]

    def body(_gi, _go, st_ref, gi_ref, go_ref, all_ref, send, recv, lsem):
        x, y, c = _pos()
        sib = (x, y, 1 - c)

        def half(k, ref, slot):
            return pltpu.make_async_remote_copy(src_ref=ref.at[slot], dst_ref=ref.at[slot], send_sem=send.at[k],
                                                recv_sem=recv.at[k], device_id=sib, device_id_type=MESH)

        def stat(k, src, slot, dev):
            return pltpu.make_async_remote_copy(src_ref=src, dst_ref=all_ref.at[slot], send_sem=send.at[k],
                                                recv_sem=recv.at[k], device_id=dev, device_id_type=MESH)

        mine = pltpu.make_async_copy(st_ref, all_ref.at[4 * x + 2 * y + c], lsem)
        mine.start()
        sends = [half(k, ref, c) for k, ref in enumerate((gi_ref, go_ref))]
        peers = [(_flip(x, fx), _flip(y, fy), _flip(c, fc)) for fx, fy, fc in flips]
        sends += [stat(2 + k, st_ref, 4 * x + 2 * y + c, dev) for k, dev in enumerate(peers)]
        for cp in sends:
            cp.start()
        for k, ref in enumerate((gi_ref, go_ref)):
            half(k, ref, 1 - c).wait_recv()
        for k, (px, py, pc) in enumerate(peers):
            slot = 4 * px + 2 * py + pc
            stat(2 + k, all_ref.at[slot], slot, (px, py, pc)).wait_recv()
        for cp in sends:
            cp.wait_send()
        mine.wait()

    n = 2 + len(flips)
    return pl.pallas_call(
        body, name="exchange_results",
        in_specs=[ANY, ANY, ANY], out_specs=[ANY, ANY, ANY], input_output_aliases={0: 0, 1: 1},
        out_shape=[jax.ShapeDtypeStruct((2, D // 2, CHUNK), F32), jax.ShapeDtypeStruct((2, WOUT_SHARD // 2, D), F32),
                   jax.ShapeDtypeStruct((NDEV, 8, D), F32)],
        scratch_shapes=[pltpu.SemaphoreType.DMA((n,)), pltpu.SemaphoreType.DMA((n,)), pltpu.SemaphoreType.DMA],
    )(gi2, go2, st)


def _adamw_math(w, g, m, v):
    m2 = ADAM_B1 * m + (1.0 - ADAM_B1) * g
    v2 = ADAM_B2 * v + (1.0 - ADAM_B2) * (g * g)
    m_hat = m2 / (1.0 - ADAM_B1 ** ADAM_STEP)
    v_hat = v2 / (1.0 - ADAM_B2 ** ADAM_STEP)
    delta = -ADAM_LR * (m_hat / (jnp.sqrt(v_hat) + ADAM_EPS) + ADAM_WD * w)
    return delta, m2, v2


def _adamw(w, g, m, v, name):
    rows, cols = w.shape
    tm = 256 if rows % 256 == 0 else rows

    def body(w_ref, g_ref, m_ref, v_ref, d_ref, m2_ref, v2_ref):
        d_ref[...], m2_ref[...], v2_ref[...] = _adamw_math(w_ref[...], g_ref[...], m_ref[...], v_ref[...])

    shape = jax.ShapeDtypeStruct(w.shape, F32)
    return pl.pallas_call(
        body, grid=(rows // tm,), name=name,
        in_specs=[_rows(tm, cols)] * 4, out_specs=[_rows(tm, cols)] * 3, out_shape=[shape] * 3,
        compiler_params=_params(("arbitrary",)),
    )(w, g, m, v)


def _adamw_vectors(g5, first_parts, ws, ms, vs):
    n = len(ws)

    def body(g_ref, parts_ref, *refs):
        ins, g0_ref, outs = refs[:3 * n], refs[3 * n], refs[3 * n + 1:]
        g0 = parts_ref[0, 0:1, :]
        for dev in range(1, NDEV):
            g0 = g0 + parts_ref[dev, 0:1, :]
        g0_ref[...] = g0
        for i in range(n):
            g = g0 if i == 0 else g_ref[i:i + 1, :]
            res = _adamw_math(ins[i][...], g, ins[n + i][...], ins[2 * n + i][...])
            for kind in range(3):
                outs[kind * n + i][...] = res[kind]

    shape = jax.ShapeDtypeStruct((1, D), F32)
    return pl.pallas_call(body, name="adamw_vectors", out_shape=[shape] * (1 + 3 * n), compiler_params=_params())(
        g5, first_parts, *ws, *ms, *vs)


def kernel(x, norm_g, w_in, conv_w, conv_b, conv_ln_g, conv_ln_b, w_out, final_norm_g, loss_target, m_norm_g, m_w_in, m_conv_w, m_conv_b, m_conv_ln_g, m_conv_ln_b, m_w_out, m_final_norm_g, v_norm_g, v_w_in, v_conv_w, v_conv_b, v_conv_ln_g, v_conv_ln_b, v_w_out, v_final_norm_g):
    chip = 2 * lax.axis_index("x") + lax.axis_index("y")
    where = jnp.stack([chip, lax.axis_index("c")]).astype(jnp.int32)
    taps_shard = jnp.pad(conv_w[0], ((0, HALO - CONV_K), (0, 0)))
    wi_full, wo_full, cw_full = _place_shards(w_in[0], w_out[0], taps_shard, where)
    wi_full = _gather_w_in(wi_full)

    gf = final_norm_g[None]
    xb = x[0]
    h, q, k, v, a_gate, c_val, c_glu, c_gate, wo_full, cw_full = _inproj_fwd(xb, norm_g, wi_full, wo_full, cw_full)
    tables = [_bias_table(d) for d in PATTERNS]
    o, lse, y_att = _attn_fwd(q, k, v, tables, a_gate)
    u, y_conv = _conv_fwd(c_val, c_glu, c_gate, cw_full, conv_b, conv_ln_g, conv_ln_b)
    dx2, dy_att, du, dc_gate, dw_out, st_out = _outproj_loss(
        y_att, y_conv, wo_full, xb, loss_target[0], gf, u, c_gate, conv_ln_g, conv_ln_b)
    dc_val, dc_glu, dconv_w = _conv_bwd_taps(du, c_val, c_glu, cw_full)

    early = [dw_out.reshape(NCHIP, WOUT_SHARD, D), jnp.concatenate([st_out, dconv_w], axis=0)]
    po, ps = _add_halves(early, _exchange_halves(early, "exchange_halves_early"), "add_halves_early")
    do, da_gate, delta, ro, rs = _attn_gate_bwd(dy_att, o, a_gate, _head_sum_selectors(), [po, ps])
    dqs, dkvs = zip(*[_attn_bwd(q, k, v, do, lse, delta, t, d) for t, d in zip(tables, PATTERNS)])

    dproj_pieces = (dqs, dkvs, (da_gate, dc_val, dc_glu, dc_gate))
    late = [_inproj_bwd_w(h, *dproj_pieces)]
    (pi,) = _add_halves(late, _exchange_halves(late, "exchange_halves"), "add_halves")
    grad_x, st_in, ri = _inproj_bwd_x(*dproj_pieces, wi_full, xb, norm_g, dx2, pi)
    gi2, go2, g_small, g5, loss8 = _sum_chips(ri, ro, rs, pi, po, ps, where)
    gi2, go2, norm_g_parts = _exchange_results(gi2, go2, st_in)
    g_w_in = gi2.reshape(D, CHUNK)
    g_w_out = go2.reshape(WOUT_SHARD, D)
    g_taps = lax.dynamic_slice(g_small, (ROW_TAPS, chip * CONVW_SHARD), (CONV_K, CONVW_SHARD))

    d_w_in, m2_w_in, v2_w_in = _adamw(w_in[0], g_w_in, m_w_in[0], v_w_in[0], "adamw_w_in")
    d_w_out, m2_w_out, v2_w_out = _adamw(w_out[0], g_w_out, m_w_out[0], v_w_out[0], "adamw_w_out")
    d_taps, m2_taps, v2_taps = _adamw(conv_w[0], g_taps, m_conv_w[0], v_conv_w[0], "adamw_conv_w")
    g_norm, *vec = _adamw_vectors(
        g5, norm_g_parts,
        (norm_g, conv_b, conv_ln_g, conv_ln_b, gf),
        (m_norm_g, m_conv_b, m_conv_ln_g, m_conv_ln_b, m_final_norm_g[None]),
        (v_norm_g, v_conv_b, v_conv_ln_g, v_conv_ln_b, v_final_norm_g[None]))
    d_vec, m2_vec, v2_vec = vec[0:5], vec[5:10], vec[10:15]

    def weight_order(ng, wi, cw, cb, lg, lb, wo, fg):
        return (ng, wi[None], cw[None], cb, lg, lb, wo[None], fg[0])

    grads = weight_order(g_norm, g_w_in, g_taps, g5[1:2], g5[2:3], g5[3:4], g_w_out, g5[4:5])
    deltas = weight_order(d_vec[0], d_w_in, d_taps, d_vec[1], d_vec[2], d_vec[3], d_w_out, d_vec[4])
    new_m = weight_order(m2_vec[0], m2_w_in, m2_taps, m2_vec[1], m2_vec[2], m2_vec[3], m2_w_out, m2_vec[4])
    new_v = weight_order(v2_vec[0], v2_w_in, v2_taps, v2_vec[1], v2_vec[2], v2_vec[3], v2_w_out, v2_vec[4])
    return (loss8[0, 0], grad_x[None], *grads, *deltas, *new_m, *new_v)
```

```python
import jax
import jax.numpy as jnp
from jax import lax
from jax.experimental import pallas as pl
from jax.experimental.pallas import tpu as pltpu

F32 = jnp.float32
BF16 = jnp.bfloat16

S = 4096
D = 1024
LANES = 128
HD = 64
NKV = 4
GQ = 4
KVW = NKV * HD
NCOL = 5632
CONV_K = 31
HALO = 32
BLK = 128
PATTERNS = (1, 4, 16)
NORM_EPS = 1e-6
LN_EPS = 1e-5
NEG = -1e30
OFF_Q, OFF_K, OFF_AG, OFF_CV, OFF_CG, OFF_CGATE = 0, 1024, 1536, 2560, 3584, 4608
NCHIP = 4
CHUNK = NCOL // NCHIP
WOUT_ROWS = 2 * D
WOUT_SHARD = WOUT_ROWS // NCHIP
CONVW_SHARD = D // NCHIP

ADAM_LR, ADAM_B1, ADAM_B2, ADAM_EPS, ADAM_WD, ADAM_STEP = 0.001, 0.9, 0.999, 1e-08, 0.01, 10

VMEM_LIMIT = 56 * 1024 * 1024


def _params(sem=None, vmem=VMEM_LIMIT):
    return pltpu.CompilerParams(dimension_semantics=sem, vmem_limit_bytes=vmem)


def _sigmoid(a):
    return 0.5 * jnp.tanh(0.5 * a) + 0.5


def _rows(tm, width):
    return pl.BlockSpec((tm, width), lambda i: (i, 0))


def _slabs(n):
    return jax.ShapeDtypeStruct((n, S, LANES), F32)


def _slab_rows(n, tm):
    return pl.BlockSpec((n, tm, LANES), lambda i: (0, i, 0))


def _resident(shape):
    return pl.BlockSpec(shape, lambda *_: (0,) * len(shape), pipeline_mode=pl.Buffered(1))


def _dot(a, b):
    return jnp.dot(a, b, preferred_element_type=F32)


def _dot_nt(a, b):
    return lax.dot_general(a, b, (((1,), (1,)), ((), ())), preferred_element_type=F32)


def _dot_tn(a, b):
    return lax.dot_general(a, b, (((0,), (0,)), ((), ())), preferred_element_type=F32)


def _inproj_fwd(x, g1, w_bf, wo_full, cw_full):
    tm = 512
    steps = S // tm

    def body(x_ref, g_ref, w_ref, _wo, _cw, h_ref, q_ref, k_ref, v_ref, ag_ref, cv_ref, cg_ref, cgate_ref,
             wo_ref, cw_ref, send, recv):
        i = pl.program_id(0)
        stages = _gather_stages([(W_OUT, wo_ref), (TAPS, cw_ref)], send, recv)
        for stage, step in zip(stages[:3], (0, steps // 2 - 1, steps - 2)):
            pl.when(i == step)(stage)
        xt = x_ref[...]
        r = lax.rsqrt(jnp.mean(xt * xt, axis=-1, keepdims=True) + NORM_EPS)
        h = (xt * r * g_ref[...]).astype(BF16)
        h_ref[...] = h
        q = _dot(h, w_ref[:, OFF_Q:OFF_Q + D]) * (HD ** -0.5)
        kv = _dot(h, w_ref[:, OFF_K:OFF_K + 2 * KVW])
        for sl in range(D // LANES):
            q_ref[sl] = q[:, sl * LANES:(sl + 1) * LANES]
        for sl in range(KVW // LANES):
            k_ref[sl] = kv[:, sl * LANES:(sl + 1) * LANES]
            v_ref[sl] = kv[:, KVW + sl * LANES:KVW + (sl + 1) * LANES]
        ag_ref[...] = _dot(h, w_ref[:, OFF_AG:OFF_AG + D])
        cv_ref[...] = _dot(h, w_ref[:, OFF_CV:OFF_CV + D])
        cg_ref[...] = _dot(h, w_ref[:, OFF_CG:OFF_CG + D])
        cgate_ref[...] = _dot(h, w_ref[:, OFF_CGATE:OFF_CGATE + D])
        pl.when(i == steps - 1)(stages[3])

    big = jax.ShapeDtypeStruct((S, D), F32)
    return pl.pallas_call(
        body, grid=(steps,), name="inproj_fwd",
        in_specs=[_rows(tm, D), _resident((1, D)), _resident((D, NCOL)), ANY, ANY],
        out_specs=[_rows(tm, D), _slab_rows(D // LANES, tm), _slab_rows(KVW // LANES, tm), _slab_rows(KVW // LANES, tm),
                   _rows(tm, D), _rows(tm, D), _rows(tm, D), _rows(tm, D), ANY, ANY],
        out_shape=[jax.ShapeDtypeStruct((S, D), BF16), _slabs(D // LANES), _slabs(KVW // LANES), _slabs(KVW // LANES),
                   big, big, big, big,
                   jax.ShapeDtypeStruct((WOUT_ROWS, D), BF16), jax.ShapeDtypeStruct((HALO, D), F32)],
        input_output_aliases={3: 8, 4: 9},
        scratch_shapes=_gather_sems(2),
        compiler_params=_params(("arbitrary",)),
    )(x, g1, w_bf, wo_full, cw_full)


def _bias_table(d):
    h = jnp.arange(NKV * GQ, dtype=F32)
    slopes = jnp.exp2(-8.0 * (h + 1.0) / (NKV * GQ))
    qi = jnp.arange(BLK)[:, None]
    kj = jnp.arange(2 * BLK)[None, :]
    dist = BLK + qi - kj
    window = (dist >= 0) & (dist <= BLK)
    bias = -slopes[:, None, None] * (dist * d).astype(F32)[None]
    has_prev = jnp.stack([jnp.broadcast_to(kj >= BLK, (BLK, 2 * BLK)), jnp.ones((BLK, 2 * BLK), bool)])
    valid = window[None] & has_prev
    tab = jnp.where(valid[:, None], bias[None], NEG)
    return tab.reshape(2, NKV, GQ * BLK, 2 * BLK)


def _sub_rows(start, d, align=BLK):
    if d == 1:
        return pl.ds(pl.multiple_of(start, align), BLK)
    return pl.ds(start, BLK, stride=d)


CHUNK_ROWS = 2048
BLOCKS_PER_CHUNK = CHUNK_ROWS // BLK


def _low_lanes(rows=BLK):
    return lax.broadcasted_iota(jnp.int32, (rows, LANES), 1) < HD


def _block_start(idx, d):
    shift = d.bit_length() - 1
    b, r = lax.shift_right_logical(idx, shift), lax.bitwise_and(idx, d - 1)
    start = b * (BLK * d) + r
    return b, start, jnp.maximum(start - BLK * d, r)


def _stack_heads(ref, rows):
    low = _low_lanes()
    t0, t1 = ref[0, rows, :], ref[1, rows, :]
    return jnp.concatenate([jnp.where(low, t0, 0.0), jnp.where(low, 0.0, t0),
                            jnp.where(low, t1, 0.0), jnp.where(low, 0.0, t1)], axis=0).astype(BF16)


def _unstack_heads(dup):
    low = _low_lanes()
    return (jnp.where(low, dup[0:BLK], dup[BLK:2 * BLK]), jnp.where(low, dup[2 * BLK:3 * BLK], dup[3 * BLK:4 * BLK]))


def _kv_dup(ref, prow, rows, odd):
    t = jnp.concatenate([ref[0, prow, :], ref[0, rows, :]], axis=0)
    swapped = pltpu.roll(t, HD, axis=1)
    keep = jnp.logical_xor(_low_lanes(2 * BLK), odd)
    return jnp.where(keep, t, swapped).astype(BF16)


PIECES = 3


def _by_head(tiles):
    lane = lax.broadcasted_iota(jnp.int32, tiles[0].shape, 1)
    out = tiles[0]
    for g in range(1, GQ):
        out = jnp.where(lax.bitwise_and(lane, GQ - 1) == g, tiles[g], out)
    return out


def _minus_in_pieces(x):
    lane = lax.broadcasted_iota(jnp.int32, x.shape, 1)
    hi = (-x).astype(BF16).astype(F32)
    rest = -x - hi
    mid = rest.astype(BF16).astype(F32)
    lo = (rest - mid).astype(BF16).astype(F32)
    return jnp.where(lane < GQ, hi, jnp.where(lane < 2 * GQ, mid, jnp.where(lane < PIECES * GQ, lo, 0.0)))


PITCH_PAD = 4


def _pitches(d):
    return BLK + PITCH_PAD, S // d + PITCH_PAD


def _pull_apart(pairs, d, groups, pitch, back=False):
    def group(g, carry):
        for src, dst in pairs:
            for n in range(src.shape[0]):
                for half in range(d // SUBLANES):
                    together = (n, pl.ds(pl.multiple_of(g * d + half * SUBLANES, SUBLANES), SUBLANES), slice(None))
                    spread = (n, pl.ds(half * SUBLANES * pitch + g, SUBLANES, stride=pitch), slice(None))
                    if back:
                        src[together] = dst[spread]
                    else:
                        dst[spread] = src[together]
        return carry

    lax.fori_loop(0, groups, group, 0, unroll=8)


def _attn_fwd(q, k, v, tables, a_gate):
    tm = 256
    width = GQ * HD

    lane_out = jnp.arange(LANES)[None, :] // HD
    spread_sel = jnp.stack([jnp.arange(LANES)[:, None] == 2 * half + lane_out for half in range(2)]).astype(BF16)

    def body(q_ref, k_ref, v_ref, b1_ref, b2_ref, b3_ref, ag_ref, sel_ref, o_ref, lse_ref, y_ref, op, lp,
             qd, kd, vd, opd, lpd):
        odd = pl.program_id(0) % 2 == 1
        chunk = pl.program_id(1)
        ones = jnp.ones((2 * BLK, LANES), BF16)

        for pat, (d, b_ref) in enumerate(zip(PATTERNS, (b1_ref, b2_ref, b3_ref))):
            apart = d * BLK == CHUNK_ROWS
            pitch, kv_pitch = _pitches(d)
            if apart:
                @pl.when(jnp.logical_and(chunk == 0, jnp.logical_not(odd)))
                def _(d=d, kv_pitch=kv_pitch):
                    _pull_apart([(k_ref, kd), (v_ref, vd)], d, S // d, kv_pitch)

                _pull_apart([(q_ref, qd)], d, BLK, pitch)
            q_in, k_in, v_in = (qd, kd, vd) if apart else (q_ref, k_ref, v_ref)
            o_out, l_out = (opd, lpd) if apart else (op.at[pat], lp.at[pl.ds(pat, 1)])

            def block(idx, carry, d=d, b_ref=b_ref, apart=apart, pitch=pitch, kv_pitch=kv_pitch,
                      q_in=q_in, k_in=k_in, v_in=v_in, o_out=o_out, l_out=l_out):
                b, start, pstart = _block_start(chunk * BLOCKS_PER_CHUNK + idx, d)
                rows, prow = _sub_rows(start, d), _sub_rows(pstart, d)
                mine = _sub_rows(start - chunk * CHUNK_ROWS, d)
                if apart:
                    rows = _sub_rows(idx * kv_pitch + b * BLK, 1, PITCH_PAD)
                    prow = _sub_rows(idx * kv_pitch + jnp.maximum(b - 1, 0) * BLK, 1, PITCH_PAD)
                    mine = _sub_rows(idx * pitch, 1, PITCH_PAD)
                qs = _stack_heads(q_in, mine)
                kw = _kv_dup(k_in, prow, rows, odd)
                vw = _kv_dup(v_in, prow, rows, odd)
                s = _dot_nt(qs, kw) + b_ref[jnp.minimum(b, 1), 0]
                m = jnp.max(s, axis=1, keepdims=True)
                p = jnp.exp(s - m).astype(BF16)
                ol = _dot(p, jnp.concatenate([vw, ones], axis=1))
                l = ol[:, LANES:]
                o_out[0, mine, :], o_out[1, mine, :] = _unstack_heads(ol[:, :LANES] / l)
                l_out[0, mine, :] = _by_head([(m + jnp.log(l))[g * BLK:(g + 1) * BLK] for g in range(GQ)])
                return carry

            lax.fori_loop(0, BLOCKS_PER_CHUNK, block, 0, unroll=2)
            if apart:
                _pull_apart([(op.at[pat], opd), (lp.at[pl.ds(pat, 1)], lpd)], d, BLK, pitch, back=True)

        def mix(t, carry):
            r = pl.ds(pl.multiple_of(t * tm, tm), tm)
            a, b, c = lp[0, r, :], lp[1, r, :], lp[2, r, :]
            m = jnp.maximum(jnp.maximum(a, b), c)
            ea, eb, ec = jnp.exp(a - m), jnp.exp(b - m), jnp.exp(c - m)
            den = ea + eb + ec
            lse_ref[0, r, :] = _minus_in_pieces(m + jnp.log(den))
            inv = 1.0 / den
            for half in range(2):
                def spread(w):
                    hi = w.astype(BF16)
                    lo = (w - hi.astype(F32)).astype(BF16)
                    return _dot(hi, sel_ref[half]) + _dot(lo, sel_ref[half])

                o = (spread(ea * inv) * op[0, half, r, :] + spread(eb * inv) * op[1, half, r, :]
                     + spread(ec * inv) * op[2, half, r, :])
                o_ref[half, r, :] = o
                cols = slice(half * LANES, (half + 1) * LANES)
                ag = ag_ref[r, cols]
                y_ref[r, cols] = (o * (ag * _sigmoid(ag))).astype(BF16)
            return carry

        lax.fori_loop(0, CHUNK_ROWS // tm, mix, 0, unroll=4)

    q_like = pl.BlockSpec((2, CHUNK_ROWS, LANES), lambda j, c: (j, c, 0))
    per_kv = pl.BlockSpec((1, CHUNK_ROWS, LANES), lambda j, c: (j, c, 0))
    kv = pl.BlockSpec((1, S, LANES), lambda j, c: (j // 2, 0, 0))
    bias_spec = pl.BlockSpec((2, 1, GQ * BLK, 2 * BLK), lambda j, c: (0, j, 0, 0))
    group_cols = pl.BlockSpec((CHUNK_ROWS, width), lambda j, c: (c, j))
    return pl.pallas_call(
        body, grid=(NKV, S // CHUNK_ROWS), name="attn_fwd",
        in_specs=[q_like, kv, kv, bias_spec, bias_spec, bias_spec, group_cols,
                  pl.BlockSpec((2, LANES, LANES), lambda j, c: (0, 0, 0))],
        out_specs=[q_like, per_kv, group_cols],
        out_shape=[_slabs(D // LANES), _slabs(NKV), jax.ShapeDtypeStruct((S, D), BF16)],
        scratch_shapes=[pltpu.VMEM((len(PATTERNS), 2, CHUNK_ROWS, LANES), F32),
                        pltpu.VMEM((len(PATTERNS), CHUNK_ROWS, LANES), F32)] + [
            pltpu.VMEM((n, BLOCKS_PER_CHUNK * _pitches(BLOCKS_PER_CHUNK)[whole], LANES), F32)
            for n, whole in ((2, 0), (1, 1), (1, 1), (2, 0), (1, 0))],
        compiler_params=_params(("arbitrary", "arbitrary")),
    )(q, k, v, *tables, a_gate, spread_sel)


def _head_sum_selectors():
    lane_in = jnp.arange(LANES)[:, None] // HD
    return jnp.stack([jnp.broadcast_to(lane_in == h, (LANES, LANES)) for h in range(2)]).astype(BF16)


def _attn_gate_bwd(dy_att, o, a_gate, selectors, chip_sums):
    tm = 256
    last = S // tm - 1
    landing, sems = _exchange_results_of(chip_sums)
    n_sums = len(chip_sums)

    def body(dy_ref, o_ref, ag_ref, e_ref, *refs):
        sums, (do_ref, dag_ref, delta_ref), refs = refs[:n_sums], refs[n_sums:n_sums + 3], refs[n_sums + 3:]
        landed, (send, recv) = refs[:n_sums], refs[n_sums:]
        i = pl.program_id(0)
        copies = _chip_exchange_copies(sums, landed, send, recv)
        _start_exchange(copies, i == 0)
        for j in range(NKV):
            deltas = []
            for sl in (2 * j, 2 * j + 1):
                cols = slice(sl * LANES, (sl + 1) * LANES)
                dy, ag, o_ = dy_ref[:, cols], ag_ref[:, cols], o_ref[sl]
                sg = _sigmoid(ag)
                do = dy * (ag * sg)
                do_ref[sl] = do
                dag_ref[:, cols] = (dy * o_ * (sg * (1.0 + ag * (1.0 - sg)))).astype(BF16)
                prod = do * o_
                hi = prod.astype(BF16)
                lo = (prod - hi.astype(F32)).astype(BF16)
                deltas += [_dot(hi, e_ref[h]) + _dot(lo, e_ref[h]) for h in range(2)]
            delta_ref[j] = _minus_in_pieces(_by_head(deltas))
        _finish_exchange(copies, i == last)

    return pl.pallas_call(
        body, grid=(S // tm,), name="attn_gate_bwd",
        in_specs=[_rows(tm, D), _slab_rows(D // LANES, tm), _rows(tm, D), _resident((2, LANES, LANES))] + [ANY] * n_sums,
        out_specs=[_slab_rows(D // LANES, tm), _rows(tm, D), _slab_rows(NKV, tm)] + [ANY] * n_sums,
        out_shape=[_slabs(D // LANES), jax.ShapeDtypeStruct((S, D), BF16), _slabs(NKV)] + landing,
        scratch_shapes=sems,
        compiler_params=_params(("arbitrary",)),
    )(dy_att, o, a_gate, selectors, *chip_sums)


def _own_pieces(tile):
    lane = lax.broadcasted_iota(jnp.int32, tile.shape, 1)
    head = jnp.where(lane < PIECES * GQ, lax.bitwise_and(lane, GQ - 1), -1)
    return jnp.concatenate([jnp.where(head == g, tile, 0.0) for g in range(GQ)], axis=0).astype(BF16)


def _attn_bwd(q, k, v, do, lse, delta, bias, d):
    apart = d * BLK == CHUNK_ROWS
    pitch, kv_pitch = _pitches(d)

    def body(q_ref, do_ref, l_ref, dl_ref, k_ref, v_ref, b_ref, dq_ref, dkv_ref, acc, *copies):
        odd = pl.program_id(0) % 2 == 1
        chunk = pl.program_id(1)
        dq_out = dq_ref
        if apart:
            qd, dod, ld, dld, kd, vd, dq_out = copies

            @pl.when(jnp.logical_and(chunk == 0, jnp.logical_not(odd)))
            def _():
                _pull_apart([(k_ref, kd), (v_ref, vd)], d, S // d, kv_pitch)

            _pull_apart([(q_ref, qd), (do_ref, dod), (l_ref, ld), (dl_ref, dld)], d, BLK, pitch)
            q_ref, do_ref, l_ref, dl_ref, k_ref, v_ref = qd, dod, ld, dld, kd, vd
        ones = (lax.broadcasted_iota(jnp.int32, (2 * BLK, LANES), 1) < PIECES * GQ).astype(BF16)

        def in_acc(block_idx):
            if apart:
                return (lax.shift_right_logical(block_idx, d.bit_length() - 1),
                        _sub_rows(lax.bitwise_and(block_idx, d - 1) * pitch, 1, PITCH_PAD))
            return (pl.ds(pl.multiple_of(block_idx * BLK, BLK), BLK),)

        @pl.when(chunk == 0)
        def _():
            acc[...] = jnp.zeros_like(acc)

        def block(idx, carry):
            idx = chunk * BLOCKS_PER_CHUNK + idx
            b, start, pstart = _block_start(idx, d)
            rows, prow = _sub_rows(start, d), _sub_rows(pstart, d)
            mine = _sub_rows(start - chunk * CHUNK_ROWS, d)
            if apart:
                r = idx - b * d
                rows = _sub_rows(r * kv_pitch + b * BLK, 1, PITCH_PAD)
                prow = _sub_rows(r * kv_pitch + jnp.maximum(b - 1, 0) * BLK, 1, PITCH_PAD)
                mine = _sub_rows(r * pitch, 1, PITCH_PAD)
            qs = _stack_heads(q_ref, mine)
            dos = _stack_heads(do_ref, mine)
            kw = _kv_dup(k_ref, prow, rows, odd)
            vw = _kv_dup(v_ref, prow, rows, odd)
            s = _dot_nt(jnp.concatenate([qs, _own_pieces(l_ref[0, mine, :])], axis=1),
                        jnp.concatenate([kw, ones], axis=1)) + b_ref[jnp.minimum(b, 1), 0]
            p = jnp.exp(s)
            dv2 = _dot_tn(p.astype(BF16), dos)
            dp = _dot_nt(jnp.concatenate([dos, _own_pieces(dl_ref[0, mine, :])], axis=1),
                         jnp.concatenate([vw, ones], axis=1))
            ds = (p * dp).astype(BF16)
            dq_out[0, mine, :], dq_out[1, mine, :] = _unstack_heads(_dot(ds, kw))
            dk2 = _dot_tn(ds, qs)
            dkv = jnp.where(_low_lanes(2 * BLK), dk2 + pltpu.roll(dk2, HD, axis=1), dv2 + pltpu.roll(dv2, HD, axis=1))
            acc[in_acc(idx)] = acc[in_acc(idx)] + dkv[BLK:]
            before = jnp.where(b >= 1, idx - d, idx)
            acc[in_acc(before)] = acc[in_acc(before)] + dkv[:BLK]
            return carry

        lax.fori_loop(0, BLOCKS_PER_CHUNK, block, 0, unroll=16)
        if apart:
            _pull_apart([(dq_ref, dq_out)], d, BLK, pitch, back=True)

        @pl.when(chunk == S // CHUNK_ROWS - 1)
        def _():
            def place(idx, carry):
                _, start, _ = _block_start(idx, d)
                dkv_ref[0, _sub_rows(start, d), :] = acc[in_acc(idx)]
                return carry

            if apart:
                for b in range(S // CHUNK_ROWS):
                    chunk_rows = pl.ds(b * CHUNK_ROWS, CHUNK_ROWS)
                    _pull_apart([(dkv_ref.at[:, chunk_rows], acc.at[pl.ds(b, 1)])], d, BLK, pitch, back=True)
            else:
                lax.fori_loop(0, S // BLK, place, 0, unroll=4)

    q_like = pl.BlockSpec((2, CHUNK_ROWS, LANES), lambda j, c: (j, c, 0))
    pieces = pl.BlockSpec((1, CHUNK_ROWS, LANES), lambda j, c: (j, c, 0))
    kv = pl.BlockSpec((1, S, LANES), lambda j, c: (j // 2, 0, 0))
    per_kv = pl.BlockSpec((1, S, LANES), lambda j, c: (j, 0, 0))
    bias_spec = pl.BlockSpec((2, 1, GQ * BLK, 2 * BLK), lambda j, c: (0, j, 0, 0))
    return pl.pallas_call(
        body, grid=(NKV, S // CHUNK_ROWS), name=f"attn_bwd_d{d}",
        in_specs=[q_like, q_like, pieces, pieces, kv, kv, bias_spec],
        out_specs=[q_like, per_kv],
        out_shape=[_slabs(D // LANES), _slabs(NKV)],
        scratch_shapes=[pltpu.VMEM((S // CHUNK_ROWS, d * pitch, LANES) if apart else (S, LANES), F32)] + apart * [
            pltpu.VMEM((n, d * rows, LANES), F32)
            for n, rows in ((2, pitch), (2, pitch), (1, pitch), (1, pitch), (1, kv_pitch), (1, kv_pitch), (2, pitch))],
        compiler_params=_params(("arbitrary", "arbitrary")),
    )(q, do, lse, delta, k, v, bias)


CONV_T = 512


def _halo_before(i):
    return (jnp.maximum(i * (CONV_T // HALO) - 1, 0), 0)


def _halo_after(i):
    return (jnp.minimum((i + 1) * (CONV_T // HALO), S // HALO - 1), 0)


SUBLANES = 8
NCH = D // LANES
GROUP = SUBLANES * SUBLANES
COMB_STRIDE = 4


def _comb_base(g, b):
    return g * GROUP + (b // COMB_STRIDE) * (SUBLANES * COMB_STRIDE) + b % COMB_STRIDE


def _comb(ref, cb, base):
    return ref[cb, pl.ds(base, SUBLANES, stride=COMB_STRIDE), :]


def _taps(w_ref, cols):
    return [jnp.broadcast_to(w_ref[j:j + 1, cols], (SUBLANES, LANES)) for j in range(CONV_K)]


def _conv_fwd(c_val, c_glu, c_gate, conv_w, conv_b, ln_g, ln_b):
    T = CONV_T

    def body(cv_ref, cg_ref, cvh_ref, cgh_ref, gate_ref, w_ref, b_ref, lg_ref, lb_ref, u_ref, y_ref, win, us):
        i = pl.program_id(0)
        for cb in range(NCH):
            cols = slice(cb * LANES, (cb + 1) * LANES)
            win[cb, HALO:HALO + T, :] = cv_ref[:, cols] * _sigmoid(cg_ref[:, cols])
            win[cb, 0:HALO, :] = jnp.where(i > 0, cvh_ref[:, cols] * _sigmoid(cgh_ref[:, cols]), 0.0)
        for cb in range(NCH):
            cols = slice(cb * LANES, (cb + 1) * LANES)
            taps = _taps(w_ref, cols)
            bias = jnp.broadcast_to(b_ref[:, cols], (SUBLANES, LANES))

            def group(g, carry):
                for b in range(SUBLANES):
                    base = _comb_base(g, b)
                    acc = bias
                    for j in range(CONV_K):
                        acc = acc + taps[j] * _comb(win, cb, base + (HALO - (CONV_K - 1) + j))
                    us[cb, pl.ds(base, SUBLANES, stride=COMB_STRIDE), :] = acc
                return carry

            lax.fori_loop(0, T // GROUP, group, 0, unroll=2)
        total = us[0]
        for cb in range(1, NCH):
            total = total + us[cb]
        mu = jnp.sum(total, axis=-1, keepdims=True) * (1.0 / D)
        sq = jnp.zeros((T, LANES), F32)
        for cb in range(NCH):
            uc = us[cb] - mu
            sq = sq + uc * uc
        rstd = lax.rsqrt(jnp.sum(sq, axis=-1, keepdims=True) * (1.0 / D) + LN_EPS)
        for cb in range(NCH):
            cols = slice(cb * LANES, (cb + 1) * LANES)
            u = us[cb]
            u_ref[:, cols] = u
            nrm = (u - mu) * rstd * lg_ref[:, cols] + lb_ref[:, cols]
            gate = gate_ref[:, cols]
            y_ref[:, cols] = (nrm * _sigmoid(nrm) * (gate * _sigmoid(gate))).astype(BF16)

    halo = pl.BlockSpec((HALO, D), _halo_before)
    return pl.pallas_call(
        body, grid=(S // T,), name="conv_fwd",
        in_specs=[_rows(T, D), _rows(T, D), halo, halo, _rows(T, D),
                  _resident((HALO, D)), _resident((1, D)), _resident((1, D)), _resident((1, D))],
        out_specs=[_rows(T, D), _rows(T, D)],
        out_shape=[jax.ShapeDtypeStruct((S, D), F32), jax.ShapeDtypeStruct((S, D), BF16)],
        scratch_shapes=[pltpu.VMEM((NCH, T + HALO, LANES), F32), pltpu.VMEM((NCH, T, LANES), F32)],
        compiler_params=_params(("arbitrary",)),
    )(c_val, c_glu, c_val, c_glu, c_gate, conv_w, conv_b, ln_g, ln_b)


def _conv_bwd_taps(du, c_val, c_glu, conv_w):
    T = CONV_T
    last = S // T - 1

    def body(du_ref, dua_ref, cv_ref, cg_ref, cvh_ref, cgh_ref, w_ref, dcv_ref, dcg_ref, dw_ref,
             hwin, dwin, dhs, dw_acc):
        i = pl.program_id(0)

        @pl.when(i == 0)
        def _():
            dw_acc[...] = jnp.zeros_like(dw_acc)

        for cb in range(NCH):
            cols = slice(cb * LANES, (cb + 1) * LANES)
            hwin[cb, HALO:HALO + T, :] = cv_ref[:, cols] * _sigmoid(cg_ref[:, cols])
            hwin[cb, 0:HALO, :] = jnp.where(i > 0, cvh_ref[:, cols] * _sigmoid(cgh_ref[:, cols]), 0.0)
            dwin[cb, 0:T, :] = du_ref[:, cols]
            dwin[cb, T:T + HALO, :] = jnp.where(i < last, dua_ref[:, cols], 0.0)
        for cb in range(NCH):
            cols = slice(cb * LANES, (cb + 1) * LANES)
            taps = _taps(w_ref, cols)

            def group_dh(g, carry):
                for b in range(SUBLANES):
                    base = _comb_base(g, b)
                    acc = jnp.zeros((SUBLANES, LANES), F32)
                    for j in range(CONV_K):
                        acc = acc + taps[j] * _comb(dwin, cb, base + (CONV_K - 1 - j))
                    dhs[cb, pl.ds(base, SUBLANES, stride=COMB_STRIDE), :] = acc
                return carry

            lax.fori_loop(0, T // GROUP, group_dh, 0, unroll=2)

            def group_dw(g, sums):
                for b in range(SUBLANES):
                    base = _comb_base(g, b)
                    d = _comb(dwin, cb, base)
                    sums = tuple(sums[j] + d * _comb(hwin, cb, base + (HALO - (CONV_K - 1) + j))
                                 for j in range(CONV_K))
                return sums

            sums = lax.fori_loop(0, T // GROUP, group_dw, tuple(dw_acc[j, :, cols] for j in range(CONV_K)))
            for j in range(CONV_K):
                dw_acc[j, :, cols] = sums[j]
            dh = dhs[cb]
            cv, sg = cv_ref[:, cols], _sigmoid(cg_ref[:, cols])
            dcv_ref[:, cols] = (dh * sg).astype(BF16)
            dcg_ref[:, cols] = (dh * cv * (sg * (1.0 - sg))).astype(BF16)

        @pl.when(i == last)
        def _():
            dw_ref[...] = jnp.zeros_like(dw_ref)
            for j in range(CONV_K):
                dw_ref[j:j + 1, :] = jnp.sum(dw_acc[j], axis=0, keepdims=True)

    before = pl.BlockSpec((HALO, D), _halo_before)
    after = pl.BlockSpec((HALO, D), _halo_after)
    big = jax.ShapeDtypeStruct((S, D), BF16)
    return pl.pallas_call(
        body, grid=(S // T,), name="conv_bwd_taps",
        in_specs=[_rows(T, D), after, _rows(T, D), _rows(T, D), before, before, _resident((HALO, D))],
        out_specs=[_rows(T, D), _rows(T, D), pl.BlockSpec((HALO, D), lambda i: (0, 0))],
        out_shape=[big, big, jax.ShapeDtypeStruct((HALO, D), F32)],
        scratch_shapes=[pltpu.VMEM((NCH, T + HALO, LANES), F32), pltpu.VMEM((NCH, T + HALO, LANES), F32),
                        pltpu.VMEM((NCH, T, LANES), F32), pltpu.VMEM((CONV_K, SUBLANES, D), F32)],
        compiler_params=_params(("arbitrary",)),
    )(du, du, c_val, c_glu, c_val, c_glu, conv_w)


def _outproj_loss(y_att, y_conv, w_out_bf, x, target, gf, u, c_gate, ln_g, ln_b):
    tm = 256

    def body(ya_ref, yc_ref, w_ref, x_ref, t_ref, gf_ref, u_ref, gate_ref, lg_ref, lb_ref,
             dx2_ref, dya_ref, du_ref, dgate_ref, dw_ref, st_ref, acc):
        @pl.when(pl.program_id(0) == 0)
        def _():
            acc[...] = jnp.zeros_like(acc)
            st_ref[...] = jnp.zeros_like(st_ref)

        ya, yc = ya_ref[...], yc_ref[...]
        x2 = x_ref[...] + _dot(ya, w_ref[0:D, :]) + _dot(yc, w_ref[D:2 * D, :])
        r = lax.rsqrt(jnp.mean(x2 * x2, axis=-1, keepdims=True) + NORM_EPS)
        xn = x2 * r
        err = xn * gf_ref[...] - t_ref[...]
        dout = err * (1.0 / D)
        dxn = dout * gf_ref[...]
        dx2 = r * (dxn - xn * jnp.mean(dxn * xn, axis=-1, keepdims=True))
        dx2_ref[...] = dx2
        dx2b = dx2.astype(BF16)
        dya_ref[...] = _dot_nt(dx2b, w_ref[0:D, :])
        dy = _dot_nt(dx2b, w_ref[D:2 * D, :])
        acc[0:D, :] += _dot_tn(ya, dx2b)
        acc[D:2 * D, :] += _dot_tn(yc, dx2b)
        st_ref[ROW_FINAL_G:ROW_FINAL_G + 1, :] += jnp.sum(dout * xn, axis=0, keepdims=True)
        st_ref[ROW_LOSS:ROW_LOSS + 1, :] += jnp.sum(err * err, axis=0, keepdims=True) * (0.5 / D)

        u, gate = u_ref[...], gate_ref[...]
        mu = jnp.mean(u, axis=-1, keepdims=True)
        uc = u - mu
        rstd = lax.rsqrt(jnp.mean(uc * uc, axis=-1, keepdims=True) + LN_EPS)
        z = uc * rstd
        nrm = z * lg_ref[...] + lb_ref[...]
        sn, sg = _sigmoid(nrm), _sigmoid(gate)
        dgate_ref[...] = (dy * (nrm * sn) * (sg * (1.0 + gate * (1.0 - sg)))).astype(BF16)
        dn = dy * (gate * sg) * (sn * (1.0 + nrm * (1.0 - sn)))
        dz = dn * lg_ref[...]
        du = rstd * (dz - jnp.mean(dz, axis=-1, keepdims=True) - z * jnp.mean(dz * z, axis=-1, keepdims=True))
        du_ref[...] = du
        st_ref[ROW_LN_G:ROW_LN_G + 1, :] += jnp.sum(dn * z, axis=0, keepdims=True)
        st_ref[ROW_LN_B:ROW_LN_B + 1, :] += jnp.sum(dn, axis=0, keepdims=True)
        st_ref[ROW_CONV_B:ROW_CONV_B + 1, :] += jnp.sum(du, axis=0, keepdims=True)

        @pl.when(pl.program_id(0) == S // tm - 1)
        def _():
            dw_ref[...] = acc[...].astype(BF16)

    big = jax.ShapeDtypeStruct((S, D), F32)
    vec = _resident((1, D))
    return pl.pallas_call(
        body, grid=(S // tm,), name="outproj_loss",
        in_specs=[_rows(tm, D), _rows(tm, D), _resident((WOUT_ROWS, D)), _rows(tm, D), _rows(tm, D), vec,
                  _rows(tm, D), _rows(tm, D), vec, vec],
        out_specs=[_rows(tm, D), _rows(tm, D), _rows(tm, D), _rows(tm, D),
                   pl.BlockSpec((WOUT_ROWS, D), lambda i: (0, 0)), pl.BlockSpec((8, D), lambda i: (0, 0))],
        out_shape=[big, big, big, jax.ShapeDtypeStruct((S, D), BF16),
                   jax.ShapeDtypeStruct((WOUT_ROWS, D), BF16), jax.ShapeDtypeStruct((8, D), F32)],
        scratch_shapes=[pltpu.VMEM((WOUT_ROWS, D), F32)],
        compiler_params=_params(("arbitrary",)),
    )(y_att, y_conv, w_out_bf, x, target, gf, u, c_gate, ln_g, ln_b)


UNITS_PER_CHUNK = CHUNK // LANES


def _dproj_unit(u, dqs, dkvs, gates, rows):
    if u < OFF_K // LANES:
        return ((dqs[0][u] + dqs[1][u] + dqs[2][u]) * (HD ** -0.5)).astype(BF16)
    if u < OFF_AG // LANES:
        w = u - OFF_K // LANES
        ta, tb = (dkvs[0][j] + dkvs[1][j] + dkvs[2][j] for j in (2 * (w % 2), 2 * (w % 2) + 1))
        low = _low_lanes(rows)
        if w < 2:
            return jnp.where(low, ta, pltpu.roll(tb, HD, axis=1)).astype(BF16)
        return jnp.where(low, pltpu.roll(ta, HD, axis=1), tb).astype(BF16)
    g, sl = divmod(u - OFF_AG // LANES, D // LANES)
    return gates[g][:, sl * LANES:(sl + 1) * LANES]


def _dproj_sources(units, dqs, dkvs, gates, rows):
    use_q = any(u < OFF_K // LANES for u in units)
    use_kv = any(OFF_K // LANES <= u < OFF_AG // LANES for u in units)
    use_g = sorted({(u - OFF_AG // LANES) // (D // LANES) for u in units if u >= OFF_AG // LANES})
    args = (list(dqs) if use_q else []) + (list(dkvs) if use_kv else []) + [gates[g] for g in use_g]
    specs = ([_slab_rows(D // LANES, rows)] * 3 if use_q else []) + ([_slab_rows(NKV, rows)] * 3 if use_kv else []) \
        + [_rows(rows, D)] * len(use_g)

    def pick(refs):
        refs = list(refs)
        q_refs = [refs.pop(0) for _ in range(3)] if use_q else None
        kv_refs = [refs.pop(0) for _ in range(3)] if use_kv else None
        return q_refs, kv_refs, {g: refs.pop(0) for g in use_g}

    return args, specs, pick


def _exchange_results_of(chip_sums):
    n = len(chip_sums) * len(CHIP_FLIPS)
    shapes = [jax.ShapeDtypeStruct((NCHIP,) + tuple(a.shape[1:] if a.ndim == 3 else a.shape), a.dtype)
              for a in chip_sums]
    return shapes, [pltpu.SemaphoreType.DMA((n,)), pltpu.SemaphoreType.DMA((n,))]


def _start_exchange(copies, first_step):
    @pl.when(first_step)
    def _():
        for out, _ in copies:
            out.start()


def _finish_exchange(copies, last_step):
    @pl.when(last_step)
    def _():
        for _, arrival in copies:
            arrival.wait_recv()
        for out, _ in copies:
            out.wait_send()


def _inproj_bwd_x(dqs, dkvs, gates, w_bf, x, g1, dx2, pi):
    tm = 256
    last = S // tm - 1
    units = range(NCOL // LANES)
    pieces, piece_specs, pick = _dproj_sources(units, dqs, dkvs, gates, tm)
    landing, sems = _exchange_results_of([pi])

    def body(*refs):
        piece_refs, refs = refs[:len(pieces)], refs[len(pieces):]
        w_ref, x_ref, g_ref, dx2_ref, pi_ref, gx_ref, st_ref, ri_ref, dp_ref, send, recv = refs
        i = pl.program_id(0)
        copies = _chip_exchange_copies([pi_ref], [ri_ref], send, recv)
        _start_exchange(copies, i == 0)

        @pl.when(i == 0)
        def _():
            st_ref[...] = jnp.zeros_like(st_ref)

        sources = pick(piece_refs)
        for u in units:
            dp_ref[:, u * LANES:(u + 1) * LANES] = _dproj_unit(u, *sources, tm)
        dh = _dot_nt(dp_ref[...], w_ref[...])
        xt = x_ref[...]
        r = lax.rsqrt(jnp.mean(xt * xt, axis=-1, keepdims=True) + NORM_EPS)
        xn = xt * r
        dxn = dh * g_ref[...]
        gx_ref[...] = dx2_ref[...] + r * (dxn - xn * jnp.mean(dxn * xn, axis=-1, keepdims=True))
        st_ref[0:1, :] += jnp.sum(dh * xn, axis=0, keepdims=True)
        _finish_exchange(copies, i == last)

    return pl.pallas_call(
        body, grid=(S // tm,), name="inproj_bwd_x",
        in_specs=piece_specs + [_resident((D, NCOL)), _rows(tm, D), _resident((1, D)), _rows(tm, D), ANY],
        out_specs=[_rows(tm, D), pl.BlockSpec((8, D), lambda i: (0, 0)), ANY],
        out_shape=[jax.ShapeDtypeStruct((S, D), F32), jax.ShapeDtypeStruct((8, D), F32)] + landing,
        scratch_shapes=[pltpu.VMEM((tm, NCOL), BF16)] + sems,
        compiler_params=_params(("arbitrary",)),
    )(*pieces, w_bf, x, g1, dx2, pi)


def _inproj_bwd_w(h, dqs, dkvs, gates):
    out = None
    for k in range(NCHIP):
        units = range(k * UNITS_PER_CHUNK, (k + 1) * UNITS_PER_CHUNK)
        tk = 512 if units[0] < OFF_K // LANES else 1024
        nk = S // tk
        pieces, piece_specs, pick = _dproj_sources(units, dqs, dkvs, gates, tk)
        handed_on = [] if out is None else [out]

        def body(*refs, units=units, pick=pick, n_pieces=len(pieces), n_in=1 + len(pieces) + len(handed_on)):
            h_ref, piece_refs = refs[0], refs[1:1 + n_pieces]
            o_ref, tile, acc = refs[n_in:]
            i = pl.program_id(0)

            @pl.when(i == 0)
            def _():
                acc[...] = jnp.zeros_like(acc)

            sources = pick(piece_refs)
            for n, u in enumerate(units):
                tile[:, n * LANES:(n + 1) * LANES] = _dproj_unit(u, *sources, tk)
            acc[...] += _dot_tn(h_ref[...], tile[...])

            @pl.when(i == nk - 1)
            def _():
                o_ref[0] = acc[...].astype(BF16)

        out = pl.pallas_call(
            body, grid=(nk,), name=f"inproj_bwd_w{k}",
            in_specs=[_rows(tk, D)] + piece_specs + [ANY] * len(handed_on),
            out_specs=pl.BlockSpec((1, D, CHUNK), lambda i, k=k: (k, 0, 0)),
            out_shape=jax.ShapeDtypeStruct((NCHIP, D, CHUNK), BF16),
            input_output_aliases={1 + len(pieces): 0} if handed_on else {},
            scratch_shapes=[pltpu.VMEM((tk, CHUNK), BF16), pltpu.VMEM((D, CHUNK), F32)],
            compiler_params=_params(("arbitrary",)),
        )(h, *pieces, *handed_on)
    return out


ROW_FINAL_G, ROW_LOSS, ROW_LN_G, ROW_LN_B, ROW_CONV_B, ROW_TAPS = 0, 1, 2, 3, 4, 8
SMALL_ROWS = 8 + HALO
NDEV = 8


MESH = pl.DeviceIdType.MESH
ANY = pl.BlockSpec(memory_space=pl.ANY)
CHIP_FLIPS = ((1, 0), (0, 1), (1, 1))


def _pos():
    return lax.axis_index("x"), lax.axis_index("y"), lax.axis_index("c")


def _flip(v, f):
    return 1 - v if f else v


def _ds(start, size, align=None):
    return pl.ds(pl.multiple_of(start, align or size), size)


def _place_shards(wi, wo, cw, where):
    steps = 4

    def body(where_ref, wi_ref, wo_ref, cw_ref, wi_full, wo_full, cw_full):
        wi_full[...] = wi_ref[...].astype(BF16)
        wo_full[...] = wo_ref[...].astype(BF16)
        cw_full[...] = cw_ref[...]

    grid_spec = pltpu.PrefetchScalarGridSpec(
        num_scalar_prefetch=1, grid=(steps,),
        in_specs=[pl.BlockSpec((D // steps, CHUNK), lambda i, w: (i, 0)),
                  pl.BlockSpec((WOUT_SHARD // steps, D), lambda i, w: (i, 0)),
                  pl.BlockSpec((HALO, CONVW_SHARD), lambda i, w: (0, 0))],
        out_specs=[pl.BlockSpec((D // steps, CHUNK), lambda i, w: (i, w[0])),
                   pl.BlockSpec((WOUT_SHARD // steps, D), lambda i, w: (w[0] * steps + i, 0)),
                   pl.BlockSpec((HALO, CONVW_SHARD), lambda i, w: (0, w[0]))])
    return pl.pallas_call(
        body, grid_spec=grid_spec, name="place_shards",
        out_shape=[jax.ShapeDtypeStruct((D, NCOL), BF16), jax.ShapeDtypeStruct((WOUT_ROWS, D), BF16),
                   jax.ShapeDtypeStruct((HALO, D), F32)],
        compiler_params=_params(("arbitrary",)),
    )(where, wi, wo, cw)


W_IN, W_OUT, TAPS = range(3)
GATHER_SEMS = 12


def _gather_stages(fulls, send, recv):
    halves = {W_IN: D // 2, W_OUT: WOUT_SHARD // 2, TAPS: HALO // 2}
    x, y, c = _pos()
    chips = {"me": (x, y), "x": (1 - x, y), "y": (x, 1 - y), "diag": (1 - x, 1 - y)}
    SENT = ((("me", 0), "x"), (("me", 1), "x"), (("me", 1), "y"), (("me", 0), "y"), (("x", 0), "y"), (("y", 1), "x"))
    LANDS = ((("x", 0), "x"), (("x", 1), "x"), (("y", 1), "y"), (("y", 0), "y"), (("diag", 0), "y"), (("diag", 1), "x"))
    N_ICI = len(SENT)

    def region(n_th, whose, half, part):
        a, full = fulls[n_th]
        chip = 2 * chips[whose][0] + chips[whose][1]
        n = halves[a] // 2
        row = half * halves[a] + part * n
        if a == W_IN:
            return full.at[_ds(row, n), _ds(chip * CHUNK, CHUNK, 128)]
        if a == W_OUT:
            return full.at[_ds(chip * WOUT_SHARD + row, n), :]
        return full.at[_ds(row, n), _ds(chip * CONVW_SHARD, CONVW_SHARD, 128)]

    def copy(n_th, kind, piece, dev):
        k = GATHER_SEMS * n_th + kind
        return pltpu.make_async_remote_copy(src_ref=piece, dst_ref=piece, send_sem=send.at[k], recv_sem=recv.at[k],
                                            device_id=dev, device_id_type=MESH)

    def sent(a, k):
        if k < N_ICI:
            (whose, part), to = SENT[k]
            return copy(a, k, region(a, whose, c, part), (*chips[to], c))
        (whose, part), _ = LANDS[k - N_ICI]
        return copy(a, k, region(a, whose, c, part), (x, y, 1 - c))

    def wait_arrival(a, k):
        if k < N_ICI:
            (whose, part), frm = LANDS[k]
            copy(a, k, region(a, whose, c, part), (*chips[frm], c)).wait_recv()
        else:
            (whose, part), _ = LANDS[k - N_ICI]
            copy(a, k, region(a, whose, 1 - c, part), (x, y, 1 - c)).wait_recv()

    arrays = range(len(fulls))

    def own_to_neighbours():
        for a in arrays:
            for k in (0, 2, 1, 3):
                sent(a, k).start()

    def pass_on_neighbours():
        for a in arrays:
            for k, onward in ((0, 4), (2, 5), (1, None), (3, None)):
                wait_arrival(a, k)
                if onward is not None:
                    sent(a, onward).start()
                sent(a, k + N_ICI).start()

    def pass_on_diagonal():
        for a in arrays:
            for k in (4, 5):
                wait_arrival(a, k)
                sent(a, k + N_ICI).start()

    def finish():
        for a in arrays:
            for k in range(N_ICI, 2 * N_ICI):
                wait_arrival(a, k)
            for k in range(2 * N_ICI):
                sent(a, k).wait_send()

    return own_to_neighbours, pass_on_neighbours, pass_on_diagonal, finish


def _gather_sems(n_arrays):
    return [pltpu.SemaphoreType.DMA((GATHER_SEMS * n_arrays,)), pltpu.SemaphoreType.DMA((GATHER_SEMS * n_arrays,))]


def _gather_w_in(wi_full):
    def body(_wi, full, send, recv):
        for stage in _gather_stages([(W_IN, full)], send, recv):
            stage()

    return pl.pallas_call(
        body, name="gather_w_in", in_specs=[ANY], out_specs=ANY, input_output_aliases={0: 0},
        out_shape=jax.ShapeDtypeStruct((D, NCOL), BF16), scratch_shapes=_gather_sems(1),
    )(wi_full)


def _half_shape(a):
    return jax.ShapeDtypeStruct((NCHIP, a.shape[1] // 2, a.shape[2]) if a.ndim == 3 else a.shape, a.dtype)


def _exchange_halves(arrays, name):
    n = len(arrays)

    def body(*refs):
        srcs, dsts, (send, recv) = refs[:n], refs[n:2 * n], refs[2 * n:]
        x, y, c = _pos()
        cps = []
        for k, (s_, d_) in enumerate(zip(srcs, dsts)):
            if len(s_.shape) == 3:
                h = s_.shape[1] // 2
                s_ = s_.at[:, _ds((1 - c) * h, h), :]
            cps.append(pltpu.make_async_remote_copy(src_ref=s_, dst_ref=d_, send_sem=send.at[k], recv_sem=recv.at[k],
                                                    device_id=(x, y, 1 - c), device_id_type=MESH))
        for cp in cps:
            cp.start()
        for cp in cps:
            cp.wait()

    return pl.pallas_call(
        body, name=name, in_specs=[ANY] * n, out_specs=[ANY] * n, out_shape=[_half_shape(a) for a in arrays],
        scratch_shapes=[pltpu.SemaphoreType.DMA((n,)), pltpu.SemaphoreType.DMA((n,))],
    )(*arrays)


def _add_halves(arrays, received, name):
    n = len(arrays)

    def body(*refs):
        mine, theirs, outs = refs[:n], refs[n:2 * n], refs[2 * n:]
        c = lax.axis_index("c")
        for m_, t_, o_ in zip(mine, theirs, outs):
            if len(m_.shape) == 3:
                h = m_.shape[1] // 2
                o_[0] = (m_[0, _ds(c * h, h), :].astype(F32) + t_[0].astype(F32)).astype(o_.dtype)
            else:
                o_[...] = m_[...] + t_[...]

    def spec(shape):
        if len(shape) == 3:
            return pl.BlockSpec((1,) + tuple(shape[1:]), lambda k: (k, 0, 0))
        return pl.BlockSpec(tuple(shape), lambda k: (0, 0))

    halves = [_half_shape(a) for a in arrays]
    return pl.pallas_call(
        body, grid=(NCHIP,), name=name,
        in_specs=[spec(a.shape) for a in arrays] + [spec(h.shape) for h in halves],
        out_specs=[spec(h.shape) for h in halves], out_shape=halves,
        compiler_params=_params(("arbitrary",)),
    )(*arrays, *received)


def _chip_exchange_copies(srcs, dsts, send, recv):
    x, y, c = _pos()
    me = 2 * x + y
    pairs = []
    for a in range(len(srcs)):
        for j, (fx, fy) in enumerate(CHIP_FLIPS):
            px, py = _flip(x, fx), _flip(y, fy)
            peer = 2 * px + py
            k = len(CHIP_FLIPS) * a + j
            out = pltpu.make_async_remote_copy(
                src_ref=srcs[a].at[peer] if len(srcs[a].shape) == 3 else srcs[a], dst_ref=dsts[a].at[me],
                send_sem=send.at[k], recv_sem=recv.at[k], device_id=(px, py, c), device_id_type=MESH)
            got = dsts[a].at[peer]
            arrival = pltpu.make_async_remote_copy(
                src_ref=got, dst_ref=got, send_sem=send.at[k], recv_sem=recv.at[k],
                device_id=(px, py, c), device_id_type=MESH)
            pairs.append((out, arrival))
    return pairs


def _sum_chips(ri, ro, rs, pi, po, ps, where):
    def body(w_ref, ri_ref, ro_ref, rs_ref, pi_ref, po_ref, ps_ref, gi_ref, go_ref, gs_ref, g5_ref, loss_ref,
             acc_i, acc_o, acc_s):
        k = pl.program_id(0)
        accs = (acc_i, acc_o, acc_s)

        @pl.when(k == 0)
        def _():
            for acc in accs:
                acc[...] = jnp.zeros_like(acc)

        @pl.when(k == w_ref[0])
        def _():
            for acc, val in zip(accs, (pi_ref[0], po_ref[0], ps_ref[...])):
                acc[...] += val.astype(F32)

        @pl.when(k != w_ref[0])
        def _():
            for acc, ref in zip(accs, (ri_ref, ro_ref, rs_ref)):
                acc[...] += ref[0].astype(F32)

        @pl.when(k == NCHIP - 1)
        def _():
            gi_ref[0] = acc_i[...]
            go_ref[0] = acc_o[...]
            gs_ref[...] = acc_s[...]
            g5_ref[...] = jnp.zeros_like(g5_ref)
            for i, row in enumerate((ROW_CONV_B, ROW_LN_G, ROW_LN_B, ROW_FINAL_G)):
                g5_ref[i + 1:i + 2, :] = acc_s[row:row + 1, :]
            loss = jnp.sum(acc_s[ROW_LOSS:ROW_LOSS + 1, :], axis=1, keepdims=True)
            loss_ref[...] = jnp.broadcast_to(loss, loss_ref.shape)

    def sent(k, w):
        return jnp.where(k == w[0], (k + 1) % NCHIP, k)

    hi, ho = D // 2, WOUT_SHARD // 2
    const = lambda shape: pl.BlockSpec(shape, lambda k, w: (0,) * len(shape))
    grid_spec = pltpu.PrefetchScalarGridSpec(
        num_scalar_prefetch=1, grid=(NCHIP,),
        in_specs=[pl.BlockSpec((1, hi, CHUNK), lambda k, w: (sent(k, w), 0, 0)),
                  pl.BlockSpec((1, ho, D), lambda k, w: (sent(k, w), 0, 0)),
                  pl.BlockSpec((1, SMALL_ROWS, D), lambda k, w: (sent(k, w), 0, 0)),
                  pl.BlockSpec((1, hi, CHUNK), lambda k, w: (w[0], 0, 0)),
                  pl.BlockSpec((1, ho, D), lambda k, w: (w[0], 0, 0)),
                  const((SMALL_ROWS, D))],
        out_specs=[pl.BlockSpec((1, hi, CHUNK), lambda k, w: (w[1], 0, 0)),
                   pl.BlockSpec((1, ho, D), lambda k, w: (w[1], 0, 0)),
                   const((SMALL_ROWS, D)), const((8, D)), const((8, LANES))],
        scratch_shapes=[pltpu.VMEM((hi, CHUNK), F32), pltpu.VMEM((ho, D), F32), pltpu.VMEM((SMALL_ROWS, D), F32)])
    return pl.pallas_call(
        body, grid_spec=grid_spec, name="sum_chips",
        out_shape=[jax.ShapeDtypeStruct((2, hi, CHUNK), F32), jax.ShapeDtypeStruct((2, ho, D), F32),
                   jax.ShapeDtypeStruct((SMALL_ROWS, D), F32), jax.ShapeDtypeStruct((8, D), F32),
                   jax.ShapeDtypeStruct((8, LANES), F32)],
        compiler_params=_params(("arbitrary",)),
    )(where, ri, ro, rs, pi, po, ps)


def _exchange_results(gi2, go2, st):
    flips = [(fx, fy, fc) for fx in (0, 1) for fy in (0, 1) for fc in (0, 1)][1:]

    def body(_gi, _go, st_ref, gi_ref, go_ref, all_ref, send, recv, lsem):
        x, y, c = _pos()
        sib = (x, y, 1 - c)

        def half(k, ref, slot):
            return pltpu.make_async_remote_copy(src_ref=ref.at[slot], dst_ref=ref.at[slot], send_sem=send.at[k],
                                                recv_sem=recv.at[k], device_id=sib, device_id_type=MESH)

        def stat(k, src, slot, dev):
            return pltpu.make_async_remote_copy(src_ref=src, dst_ref=all_ref.at[slot], send_sem=send.at[k],
                                                recv_sem=recv.at[k], device_id=dev, device_id_type=MESH)

        mine = pltpu.make_async_copy(st_ref, all_ref.at[4 * x + 2 * y + c], lsem)
        mine.start()
        sends = [half(k, ref, c) for k, ref in enumerate((gi_ref, go_ref))]
        peers = [(_flip(x, fx), _flip(y, fy), _flip(c, fc)) for fx, fy, fc in flips]
        sends += [stat(2 + k, st_ref, 4 * x + 2 * y + c, dev) for k, dev in enumerate(peers)]
        for cp in sends:
            cp.start()
        for k, ref in enumerate((gi_ref, go_ref)):
            half(k, ref, 1 - c).wait_recv()
        for k, (px, py, pc) in enumerate(peers):
            slot = 4 * px + 2 * py + pc
            stat(2 + k, all_ref.at[slot], slot, (px, py, pc)).wait_recv()
        for cp in sends:
            cp.wait_send()
        mine.wait()

    n = 2 + len(flips)
    return pl.pallas_call(
        body, name="exchange_results",
        in_specs=[ANY, ANY, ANY], out_specs=[ANY, ANY, ANY], input_output_aliases={0: 0, 1: 1},
        out_shape=[jax.ShapeDtypeStruct((2, D // 2, CHUNK), F32), jax.ShapeDtypeStruct((2, WOUT_SHARD // 2, D), F32),
                   jax.ShapeDtypeStruct((NDEV, 8, D), F32)],
        scratch_shapes=[pltpu.SemaphoreType.DMA((n,)), pltpu.SemaphoreType.DMA((n,)), pltpu.SemaphoreType.DMA],
    )(gi2, go2, st)


def _adamw_math(w, g, m, v):
    m2 = ADAM_B1 * m + (1.0 - ADAM_B1) * g
    v2 = ADAM_B2 * v + (1.0 - ADAM_B2) * (g * g)
    m_hat = m2 / (1.0 - ADAM_B1 ** ADAM_STEP)
    v_hat = v2 / (1.0 - ADAM_B2 ** ADAM_STEP)
    delta = -ADAM_LR * (m_hat / (jnp.sqrt(v_hat) + ADAM_EPS) + ADAM_WD * w)
    return delta, m2, v2


def _adamw(w, g, m, v, name):
    rows, cols = w.shape
    tm = 256 if rows % 256 == 0 else rows

    def body(w_ref, g_ref, m_ref, v_ref, d_ref, m2_ref, v2_ref):
        d_ref[...], m2_ref[...], v2_ref[...] = _adamw_math(w_ref[...], g_ref[...], m_ref[...], v_ref[...])

    shape = jax.ShapeDtypeStruct(w.shape, F32)
    return pl.pallas_call(
        body, grid=(rows // tm,), name=name,
        in_specs=[_rows(tm, cols)] * 4, out_specs=[_rows(tm, cols)] * 3, out_shape=[shape] * 3,
        compiler_params=_params(("arbitrary",)),
    )(w, g, m, v)


def _adamw_vectors(g5, first_parts, ws, ms, vs):
    n = len(ws)

    def body(g_ref, parts_ref, *refs):
        ins, g0_ref, outs = refs[:3 * n], refs[3 * n], refs[3 * n + 1:]
        g0 = parts_ref[0, 0:1, :]
        for dev in range(1, NDEV):
            g0 = g0 + parts_ref[dev, 0:1, :]
        g0_ref[...] = g0
        for i in range(n):
            g = g0 if i == 0 else g_ref[i:i + 1, :]
            res = _adamw_math(ins[i][...], g, ins[n + i][...], ins[2 * n + i][...])
            for kind in range(3):
                outs[kind * n + i][...] = res[kind]

    shape = jax.ShapeDtypeStruct((1, D), F32)
    return pl.pallas_call(body, name="adamw_vectors", out_shape=[shape] * (1 + 3 * n), compiler_params=_params())(
        g5, first_parts, *ws, *ms, *vs)


def kernel(x, norm_g, w_in, conv_w, conv_b, conv_ln_g, conv_ln_b, w_out, final_norm_g, loss_target, m_norm_g, m_w_in, m_conv_w, m_conv_b, m_conv_ln_g, m_conv_ln_b, m_w_out, m_final_norm_g, v_norm_g, v_w_in, v_conv_w, v_conv_b, v_conv_ln_g, v_conv_ln_b, v_w_out, v_final_norm_g):
    chip = 2 * lax.axis_index("x") + lax.axis_index("y")
    where = jnp.stack([chip, lax.axis_index("c")]).astype(jnp.int32)
    taps_shard = jnp.pad(conv_w[0], ((0, HALO - CONV_K), (0, 0)))
    wi_full, wo_full, cw_full = _place_shards(w_in[0], w_out[0], taps_shard, where)
    wi_full = _gather_w_in(wi_full)

    gf = final_norm_g[None]
    xb = x[0]
    h, q, k, v, a_gate, c_val, c_glu, c_gate, wo_full, cw_full = _inproj_fwd(xb, norm_g, wi_full, wo_full, cw_full)
    tables = [_bias_table(d) for d in PATTERNS]
    o, lse, y_att = _attn_fwd(q, k, v, tables, a_gate)
    u, y_conv = _conv_fwd(c_val, c_glu, c_gate, cw_full, conv_b, conv_ln_g, conv_ln_b)
    dx2, dy_att, du, dc_gate, dw_out, st_out = _outproj_loss(
        y_att, y_conv, wo_full, xb, loss_target[0], gf, u, c_gate, conv_ln_g, conv_ln_b)
    dc_val, dc_glu, dconv_w = _conv_bwd_taps(du, c_val, c_glu, cw_full)

    early = [dw_out.reshape(NCHIP, WOUT_SHARD, D), jnp.concatenate([st_out, dconv_w], axis=0)]
    po, ps = _add_halves(early, _exchange_halves(early, "exchange_halves_early"), "add_halves_early")
    do, da_gate, delta, ro, rs = _attn_gate_bwd(dy_att, o, a_gate, _head_sum_selectors(), [po, ps])
    dqs, dkvs = zip(*[_attn_bwd(q, k, v, do, lse, delta, t, d) for t, d in zip(tables, PATTERNS)])

    dproj_pieces = (dqs, dkvs, (da_gate, dc_val, dc_glu, dc_gate))
    late = [_inproj_bwd_w(h, *dproj_pieces)]
    (pi,) = _add_halves(late, _exchange_halves(late, "exchange_halves"), "add_halves")
    grad_x, st_in, ri = _inproj_bwd_x(*dproj_pieces, wi_full, xb, norm_g, dx2, pi)
    gi2, go2, g_small, g5, loss8 = _sum_chips(ri, ro, rs, pi, po, ps, where)
    gi2, go2, norm_g_parts = _exchange_results(gi2, go2, st_in)
    g_w_in = gi2.reshape(D, CHUNK)
    g_w_out = go2.reshape(WOUT_SHARD, D)
    g_taps = lax.dynamic_slice(g_small, (ROW_TAPS, chip * CONVW_SHARD), (CONV_K, CONVW_SHARD))

    d_w_in, m2_w_in, v2_w_in = _adamw(w_in[0], g_w_in, m_w_in[0], v_w_in[0], "adamw_w_in")
    d_w_out, m2_w_out, v2_w_out = _adamw(w_out[0], g_w_out, m_w_out[0], v_w_out[0], "adamw_w_out")
    d_taps, m2_taps, v2_taps = _adamw(conv_w[0], g_taps, m_conv_w[0], v_conv_w[0], "adamw_conv_w")
    g_norm, *vec = _adamw_vectors(
        g5, norm_g_parts,
        (norm_g, conv_b, conv_ln_g, conv_ln_b, gf),
        (m_norm_g, m_conv_b, m_conv_ln_g, m_conv_ln_b, m_final_norm_g[None]),
        (v_norm_g, v_conv_b, v_conv_ln_g, v_conv_ln_b, v_final_norm_g[None]))
    d_vec, m2_vec, v2_vec = vec[0:5], vec[5:10], vec[10:15]

    def weight_order(ng, wi, cw, cb, lg, lb, wo, fg):
        return (ng, wi[None], cw[None], cb, lg, lb, wo[None], fg[0])

    grads = weight_order(g_norm, g_w_in, g_taps, g5[1:2], g5[2:3], g5[3:4], g_w_out, g5[4:5])
    deltas = weight_order(d_vec[0], d_w_in, d_taps, d_vec[1], d_vec[2], d_vec[3], d_w_out, d_vec[4])
    new_m = weight_order(m2_vec[0], m2_w_in, m2_taps, m2_vec[1], m2_vec[2], m2_vec[3], m2_w_out, m2_vec[4])
    new_v = weight_order(v2_vec[0], v2_w_in, v2_taps, v2_vec[1], v2_vec[2], v2_vec[3], v2_w_out, v2_vec[4])
    return (loss8[0, 0], grad_x[None], *grads, *deltas, *new_m, *new_v)
```

```python
import jax
import jax.numpy as jnp
from jax import lax
from jax.experimental import pallas as pl
from jax.experimental.pallas import tpu as pltpu

F32 = jnp.float32
BF16 = jnp.bfloat16

S = 4096
D = 1024
LANES = 128
HD = 64
NKV = 4
GQ = 4
KVW = NKV * HD
NCOL = 5632
CONV_K = 31
HALO = 32
BLK = 128
PATTERNS = (1, 4, 16)
NORM_EPS = 1e-6
LN_EPS = 1e-5
NEG = -1e30
OFF_Q, OFF_K, OFF_AG, OFF_CV, OFF_CG, OFF_CGATE = 0, 1024, 1536, 2560, 3584, 4608
NCHIP = 4
CHUNK = NCOL // NCHIP
WOUT_ROWS = 2 * D
WOUT_SHARD = WOUT_ROWS // NCHIP
CONVW_SHARD = D // NCHIP

ADAM_LR, ADAM_B1, ADAM_B2, ADAM_EPS, ADAM_WD, ADAM_STEP = 0.001, 0.9, 0.999, 1e-08, 0.01, 10

VMEM_LIMIT = 56 * 1024 * 1024


def _params(sem=None, vmem=VMEM_LIMIT):
    return pltpu.CompilerParams(dimension_semantics=sem, vmem_limit_bytes=vmem)


def _sigmoid(a):
    return 0.5 * jnp.tanh(0.5 * a) + 0.5


def _rows(tm, width):
    return pl.BlockSpec((tm, width), lambda i: (i, 0))


def _slabs(n):
    return jax.ShapeDtypeStruct((n, S, LANES), F32)


def _slab_rows(n, tm):
    return pl.BlockSpec((n, tm, LANES), lambda i: (0, i, 0))


def _resident(shape):
    return pl.BlockSpec(shape, lambda *_: (0,) * len(shape), pipeline_mode=pl.Buffered(1))


def _dot(a, b):
    return jnp.dot(a, b, preferred_element_type=F32)


def _dot_nt(a, b):
    return lax.dot_general(a, b, (((1,), (1,)), ((), ())), preferred_element_type=F32)


def _dot_tn(a, b):
    return lax.dot_general(a, b, (((0,), (0,)), ((), ())), preferred_element_type=F32)


def _inproj_fwd(x, g1, w_bf, wo_full, cw_full):
    tm = 512
    steps = S // tm

    def body(x_ref, g_ref, w_ref, _wo, _cw, h_ref, q_ref, k_ref, v_ref, ag_ref, cv_ref, cg_ref, cgate_ref,
             wo_ref, cw_ref, send, recv):
        i = pl.program_id(0)
        stages = _gather_stages([(W_OUT, wo_ref), (TAPS, cw_ref)], send, recv)
        for stage, step in zip(stages[:3], (0, steps // 2 - 1, steps - 2)):
            pl.when(i == step)(stage)
        xt = x_ref[...]
        r = lax.rsqrt(jnp.mean(xt * xt, axis=-1, keepdims=True) + NORM_EPS)
        h = (xt * r * g_ref[...]).astype(BF16)
        h_ref[...] = h
        q = _dot(h, w_ref[:, OFF_Q:OFF_Q + D]) * (HD ** -0.5)
        kv = _dot(h, w_ref[:, OFF_K:OFF_K + 2 * KVW])
        for sl in range(D // LANES):
            q_ref[sl] = q[:, sl * LANES:(sl + 1) * LANES]
        for sl in range(KVW // LANES):
            k_ref[sl] = kv[:, sl * LANES:(sl + 1) * LANES]
            v_ref[sl] = kv[:, KVW + sl * LANES:KVW + (sl + 1) * LANES]
        ag_ref[...] = _dot(h, w_ref[:, OFF_AG:OFF_AG + D])
        cv_ref[...] = _dot(h, w_ref[:, OFF_CV:OFF_CV + D])
        cg_ref[...] = _dot(h, w_ref[:, OFF_CG:OFF_CG + D])
        cgate_ref[...] = _dot(h, w_ref[:, OFF_CGATE:OFF_CGATE + D])
        pl.when(i == steps - 1)(stages[3])

    big = jax.ShapeDtypeStruct((S, D), F32)
    return pl.pallas_call(
        body, grid=(steps,), name="inproj_fwd",
        in_specs=[_rows(tm, D), _resident((1, D)), _resident((D, NCOL)), ANY, ANY],
        out_specs=[_rows(tm, D), _slab_rows(D // LANES, tm), _slab_rows(KVW // LANES, tm), _slab_rows(KVW // LANES, tm),
                   _rows(tm, D), _rows(tm, D), _rows(tm, D), _rows(tm, D), ANY, ANY],
        out_shape=[jax.ShapeDtypeStruct((S, D), BF16), _slabs(D // LANES), _slabs(KVW // LANES), _slabs(KVW // LANES),
                   big, big, big, big,
                   jax.ShapeDtypeStruct((WOUT_ROWS, D), BF16), jax.ShapeDtypeStruct((HALO, D), F32)],
        input_output_aliases={3: 8, 4: 9},
        scratch_shapes=_gather_sems(2),
        compiler_params=_params(("arbitrary",)),
    )(x, g1, w_bf, wo_full, cw_full)


def _bias_table(d):
    h = jnp.arange(NKV * GQ, dtype=F32)
    slopes = jnp.exp2(-8.0 * (h + 1.0) / (NKV * GQ))
    qi = jnp.arange(BLK)[:, None]
    kj = jnp.arange(2 * BLK)[None, :]
    dist = BLK + qi - kj
    window = (dist >= 0) & (dist <= BLK)
    bias = -slopes[:, None, None] * (dist * d).astype(F32)[None]
    has_prev = jnp.stack([jnp.broadcast_to(kj >= BLK, (BLK, 2 * BLK)), jnp.ones((BLK, 2 * BLK), bool)])
    valid = window[None] & has_prev
    tab = jnp.where(valid[:, None], bias[None], NEG)
    return tab.reshape(2, NKV, GQ * BLK, 2 * BLK)


def _sub_rows(start, d, align=BLK):
    if d == 1:
        return pl.ds(pl.multiple_of(start, align), BLK)
    return pl.ds(start, BLK, stride=d)


CHUNK_ROWS = 2048
BLOCKS_PER_CHUNK = CHUNK_ROWS // BLK


def _low_lanes(rows=BLK):
    return lax.broadcasted_iota(jnp.int32, (rows, LANES), 1) < HD


def _block_start(idx, d):
    shift = d.bit_length() - 1
    b, r = lax.shift_right_logical(idx, shift), lax.bitwise_and(idx, d - 1)
    start = b * (BLK * d) + r
    return b, start, jnp.maximum(start - BLK * d, r)


def _stack_heads(ref, rows):
    low = _low_lanes()
    t0, t1 = ref[0, rows, :], ref[1, rows, :]
    return jnp.concatenate([jnp.where(low, t0, 0.0), jnp.where(low, 0.0, t0),
                            jnp.where(low, t1, 0.0), jnp.where(low, 0.0, t1)], axis=0).astype(BF16)


def _unstack_heads(dup):
    low = _low_lanes()
    return (jnp.where(low, dup[0:BLK], dup[BLK:2 * BLK]), jnp.where(low, dup[2 * BLK:3 * BLK], dup[3 * BLK:4 * BLK]))


def _kv_dup(ref, prow, rows, odd):
    t = jnp.concatenate([ref[0, prow, :], ref[0, rows, :]], axis=0)
    swapped = pltpu.roll(t, HD, axis=1)
    keep = jnp.logical_xor(_low_lanes(2 * BLK), odd)
    return jnp.where(keep, t, swapped).astype(BF16)


PIECES = 3


def _by_head(tiles):
    lane = lax.broadcasted_iota(jnp.int32, tiles[0].shape, 1)
    out = tiles[0]
    for g in range(1, GQ):
        out = jnp.where(lax.bitwise_and(lane, GQ - 1) == g, tiles[g], out)
    return out


def _minus_in_pieces(x):
    lane = lax.broadcasted_iota(jnp.int32, x.shape, 1)
    hi = (-x).astype(BF16).astype(F32)
    rest = -x - hi
    mid = rest.astype(BF16).astype(F32)
    lo = (rest - mid).astype(BF16).astype(F32)
    return jnp.where(lane < GQ, hi, jnp.where(lane < 2 * GQ, mid, jnp.where(lane < PIECES * GQ, lo, 0.0)))


PITCH_PAD = 4


def _pitches(d):
    return BLK + PITCH_PAD, S // d + PITCH_PAD


def _pull_apart(pairs, d, groups, pitch, back=False):
    def group(g, carry):
        for src, dst in pairs:
            for n in range(src.shape[0]):
                for half in range(d // SUBLANES):
                    together = (n, pl.ds(pl.multiple_of(g * d + half * SUBLANES, SUBLANES), SUBLANES), slice(None))
                    spread = (n, pl.ds(half * SUBLANES * pitch + g, SUBLANES, stride=pitch), slice(None))
                    if back:
                        src[together] = dst[spread]
                    else:
                        dst[spread] = src[together]
        return carry

    lax.fori_loop(0, groups, group, 0, unroll=8)


def _attn_fwd(q, k, v, tables, a_gate):
    tm = 256
    width = GQ * HD

    lane_out = jnp.arange(LANES)[None, :] // HD
    spread_sel = jnp.stack([jnp.arange(LANES)[:, None] == 2 * half + lane_out for half in range(2)]).astype(BF16)

    def body(q_ref, k_ref, v_ref, b1_ref, b2_ref, b3_ref, ag_ref, sel_ref, o_ref, lse_ref, y_ref, op, lp,
             qd, kd, vd, opd, lpd):
        odd = pl.program_id(0) % 2 == 1
        chunk = pl.program_id(1)
        ones = jnp.ones((2 * BLK, LANES), BF16)

        for pat, (d, b_ref) in enumerate(zip(PATTERNS, (b1_ref, b2_ref, b3_ref))):
            apart = d * BLK == CHUNK_ROWS
            pitch, kv_pitch = _pitches(d)
            if apart:
                @pl.when(jnp.logical_and(chunk == 0, jnp.logical_not(odd)))
                def _(d=d, kv_pitch=kv_pitch):
                    _pull_apart([(k_ref, kd), (v_ref, vd)], d, S // d, kv_pitch)

                _pull_apart([(q_ref, qd)], d, BLK, pitch)
            q_in, k_in, v_in = (qd, kd, vd) if apart else (q_ref, k_ref, v_ref)
            o_out, l_out = (opd, lpd) if apart else (op.at[pat], lp.at[pl.ds(pat, 1)])

            def block(idx, carry, d=d, b_ref=b_ref, apart=apart, pitch=pitch, kv_pitch=kv_pitch,
                      q_in=q_in, k_in=k_in, v_in=v_in, o_out=o_out, l_out=l_out):
                b, start, pstart = _block_start(chunk * BLOCKS_PER_CHUNK + idx, d)
                rows, prow = _sub_rows(start, d), _sub_rows(pstart, d)
                mine = _sub_rows(start - chunk * CHUNK_ROWS, d)
                if apart:
                    rows = _sub_rows(idx * kv_pitch + b * BLK, 1, PITCH_PAD)
                    prow = _sub_rows(idx * kv_pitch + jnp.maximum(b - 1, 0) * BLK, 1, PITCH_PAD)
                    mine = _sub_rows(idx * pitch, 1, PITCH_PAD)
                qs = _stack_heads(q_in, mine)
                kw = _kv_dup(k_in, prow, rows, odd)
                vw = _kv_dup(v_in, prow, rows, odd)
                s = _dot_nt(qs, kw) + b_ref[jnp.minimum(b, 1), 0]
                m = jnp.max(s, axis=1, keepdims=True)
                p = jnp.exp(s - m).astype(BF16)
                ol = _dot(p, jnp.concatenate([vw, ones], axis=1))
                l = ol[:, LANES:]
                o_out[0, mine, :], o_out[1, mine, :] = _unstack_heads(ol[:, :LANES] / l)
                l_out[0, mine, :] = _by_head([(m + jnp.log(l))[g * BLK:(g + 1) * BLK] for g in range(GQ)])
                return carry

            lax.fori_loop(0, BLOCKS_PER_CHUNK, block, 0, unroll=2)
            if apart:
                _pull_apart([(op.at[pat], opd), (lp.at[pl.ds(pat, 1)], lpd)], d, BLK, pitch, back=True)

        def mix(t, carry):
            r = pl.ds(pl.multiple_of(t * tm, tm), tm)
            a, b, c = lp[0, r, :], lp[1, r, :], lp[2, r, :]
            m = jnp.maximum(jnp.maximum(a, b), c)
            ea, eb, ec = jnp.exp(a - m), jnp.exp(b - m), jnp.exp(c - m)
            den = ea + eb + ec
            lse_ref[0, r, :] = _minus_in_pieces(m + jnp.log(den))
            inv = 1.0 / den
            for half in range(2):
                def spread(w):
                    hi = w.astype(BF16)
                    lo = (w - hi.astype(F32)).astype(BF16)
                    return _dot(hi, sel_ref[half]) + _dot(lo, sel_ref[half])

                o = (spread(ea * inv) * op[0, half, r, :] + spread(eb * inv) * op[1, half, r, :]
                     + spread(ec * inv) * op[2, half, r, :])
                o_ref[half, r, :] = o
                cols = slice(half * LANES, (half + 1) * LANES)
                ag = ag_ref[r, cols]
                y_ref[r, cols] = (o * (ag * _sigmoid(ag))).astype(BF16)
            return carry

        lax.fori_loop(0, CHUNK_ROWS // tm, mix, 0, unroll=4)

    q_like = pl.BlockSpec((2, CHUNK_ROWS, LANES), lambda j, c: (j, c, 0))
    per_kv = pl.BlockSpec((1, CHUNK_ROWS, LANES), lambda j, c: (j, c, 0))
    kv = pl.BlockSpec((1, S, LANES), lambda j, c: (j // 2, 0, 0))
    bias_spec = pl.BlockSpec((2, 1, GQ * BLK, 2 * BLK), lambda j, c: (0, j, 0, 0))
    group_cols = pl.BlockSpec((CHUNK_ROWS, width), lambda j, c: (c, j))
    return pl.pallas_call(
        body, grid=(NKV, S // CHUNK_ROWS), name="attn_fwd",
        in_specs=[q_like, kv, kv, bias_spec, bias_spec, bias_spec, group_cols,
                  pl.BlockSpec((2, LANES, LANES), lambda j, c: (0, 0, 0))],
        out_specs=[q_like, per_kv, group_cols],
        out_shape=[_slabs(D // LANES), _slabs(NKV), jax.ShapeDtypeStruct((S, D), BF16)],
        scratch_shapes=[pltpu.VMEM((len(PATTERNS), 2, CHUNK_ROWS, LANES), F32),
                        pltpu.VMEM((len(PATTERNS), CHUNK_ROWS, LANES), F32)] + [
            pltpu.VMEM((n, BLOCKS_PER_CHUNK * _pitches(BLOCKS_PER_CHUNK)[whole], LANES), F32)
            for n, whole in ((2, 0), (1, 1), (1, 1), (2, 0), (1, 0))],
        compiler_params=_params(("arbitrary", "arbitrary")),
    )(q, k, v, *tables, a_gate, spread_sel)


def _head_sum_selectors():
    lane_in = jnp.arange(LANES)[:, None] // HD
    return jnp.stack([jnp.broadcast_to(lane_in == h, (LANES, LANES)) for h in range(2)]).astype(BF16)


def _attn_gate_bwd(dy_att, o, a_gate, selectors, chip_sums):
    tm = 256
    last = S // tm - 1
    landing, sems = _exchange_results_of(chip_sums)
    n_sums = len(chip_sums)

    def body(dy_ref, o_ref, ag_ref, e_ref, *refs):
        sums, (do_ref, dag_ref, delta_ref), refs = refs[:n_sums], refs[n_sums:n_sums + 3], refs[n_sums + 3:]
        landed, (send, recv) = refs[:n_sums], refs[n_sums:]
        i = pl.program_id(0)
        copies = _chip_exchange_copies(sums, landed, send, recv)
        _start_exchange(copies, i == 0)
        for j in range(NKV):
            deltas = []
            for sl in (2 * j, 2 * j + 1):
                cols = slice(sl * LANES, (sl + 1) * LANES)
                dy, ag, o_ = dy_ref[:, cols], ag_ref[:, cols], o_ref[sl]
                sg = _sigmoid(ag)
                do = dy * (ag * sg)
                do_ref[sl] = do
                dag_ref[:, cols] = (dy * o_ * (sg * (1.0 + ag * (1.0 - sg)))).astype(BF16)
                prod = do * o_
                hi = prod.astype(BF16)
                lo = (prod - hi.astype(F32)).astype(BF16)
                deltas += [_dot(hi, e_ref[h]) + _dot(lo, e_ref[h]) for h in range(2)]
            delta_ref[j] = _minus_in_pieces(_by_head(deltas))
        _finish_exchange(copies, i == last)

    return pl.pallas_call(
        body, grid=(S // tm,), name="attn_gate_bwd",
        in_specs=[_rows(tm, D), _slab_rows(D // LANES, tm), _rows(tm, D), _resident((2, LANES, LANES))] + [ANY] * n_sums,
        out_specs=[_slab_rows(D // LANES, tm), _rows(tm, D), _slab_rows(NKV, tm)] + [ANY] * n_sums,
        out_shape=[_slabs(D // LANES), jax.ShapeDtypeStruct((S, D), BF16), _slabs(NKV)] + landing,
        scratch_shapes=sems,
        compiler_params=_params(("arbitrary",)),
    )(dy_att, o, a_gate, selectors, *chip_sums)


def _own_pieces(tile):
    lane = lax.broadcasted_iota(jnp.int32, tile.shape, 1)
    head = jnp.where(lane < PIECES * GQ, lax.bitwise_and(lane, GQ - 1), -1)
    return jnp.concatenate([jnp.where(head == g, tile, 0.0) for g in range(GQ)], axis=0).astype(BF16)


def _attn_bwd(q, k, v, do, lse, delta, bias, d):
    apart = d * BLK == CHUNK_ROWS
    pitch, kv_pitch = _pitches(d)
    step_rows = CHUNK_ROWS if apart else S
    step_blocks = step_rows // BLK

    def body(q_ref, do_ref, l_ref, dl_ref, k_ref, v_ref, b_ref, dq_ref, dkv_ref, acc, *copies):
        odd = pl.program_id(0) % 2 == 1
        chunk = pl.program_id(1)
        dq_out = dq_ref
        if apart:
            qd, dod, ld, dld, kd, vd, dq_out = copies

            @pl.when(jnp.logical_and(chunk == 0, jnp.logical_not(odd)))
            def _():
                _pull_apart([(k_ref, kd), (v_ref, vd)], d, S // d, kv_pitch)

            _pull_apart([(q_ref, qd), (do_ref, dod), (l_ref, ld), (dl_ref, dld)], d, BLK, pitch)
            q_ref, do_ref, l_ref, dl_ref, k_ref, v_ref = qd, dod, ld, dld, kd, vd
        ones = (lax.broadcasted_iota(jnp.int32, (2 * BLK, LANES), 1) < PIECES * GQ).astype(BF16)

        def in_acc(block_idx):
            if apart:
                return (lax.shift_right_logical(block_idx, d.bit_length() - 1),
                        _sub_rows(lax.bitwise_and(block_idx, d - 1) * pitch, 1, PITCH_PAD))
            return (pl.ds(pl.multiple_of(block_idx * BLK, BLK), BLK),)

        @pl.when(chunk == 0)
        def _():
            acc[...] = jnp.zeros_like(acc)

        def block(idx, carry):
            idx = chunk * step_blocks + idx
            b, start, pstart = _block_start(idx, d)
            rows, prow = _sub_rows(start, d), _sub_rows(pstart, d)
            mine = _sub_rows(start - chunk * step_rows, d)
            if apart:
                r = idx - b * d
                rows = _sub_rows(r * kv_pitch + b * BLK, 1, PITCH_PAD)
                prow = _sub_rows(r * kv_pitch + jnp.maximum(b - 1, 0) * BLK, 1, PITCH_PAD)
                mine = _sub_rows(r * pitch, 1, PITCH_PAD)
            qs = _stack_heads(q_ref, mine)
            dos = _stack_heads(do_ref, mine)
            kw = _kv_dup(k_ref, prow, rows, odd)
            vw = _kv_dup(v_ref, prow, rows, odd)
            s = _dot_nt(jnp.concatenate([qs, _own_pieces(l_ref[0, mine, :])], axis=1),
                        jnp.concatenate([kw, ones], axis=1)) + b_ref[jnp.minimum(b, 1), 0]
            p = jnp.exp(s)
            dv2 = _dot_tn(p.astype(BF16), dos)
            dp = _dot_nt(jnp.concatenate([dos, _own_pieces(dl_ref[0, mine, :])], axis=1),
                         jnp.concatenate([vw, ones], axis=1))
            ds = (p * dp).astype(BF16)
            dq_out[0, mine, :], dq_out[1, mine, :] = _unstack_heads(_dot(ds, kw))
            dk2 = _dot_tn(ds, qs)
            dkv = jnp.where(_low_lanes(2 * BLK), dk2 + pltpu.roll(dk2, HD, axis=1), dv2 + pltpu.roll(dv2, HD, axis=1))
            acc[in_acc(idx)] = acc[in_acc(idx)] + dkv[BLK:]
            before = jnp.where(b >= 1, idx - d, idx)
            acc[in_acc(before)] = acc[in_acc(before)] + dkv[:BLK]
            return carry

        lax.fori_loop(0, step_blocks, block, 0, unroll=step_blocks)
        if apart:
            _pull_apart([(dq_ref, dq_out)], d, BLK, pitch, back=True)

        @pl.when(chunk == S // step_rows - 1)
        def _():
            def place(idx, carry):
                _, start, _ = _block_start(idx, d)
                dkv_ref[0, _sub_rows(start, d), :] = acc[in_acc(idx)]
                return carry

            if apart:
                for b in range(S // CHUNK_ROWS):
                    chunk_rows = pl.ds(b * CHUNK_ROWS, CHUNK_ROWS)
                    _pull_apart([(dkv_ref.at[:, chunk_rows], acc.at[pl.ds(b, 1)])], d, BLK, pitch, back=True)
            else:
                lax.fori_loop(0, S // BLK, place, 0, unroll=4)

    q_like = pl.BlockSpec((2, step_rows, LANES), lambda j, c: (j, c, 0))
    pieces = pl.BlockSpec((1, step_rows, LANES), lambda j, c: (j, c, 0))
    kv = pl.BlockSpec((1, S, LANES), lambda j, c: (j // 2, 0, 0))
    per_kv = pl.BlockSpec((1, S, LANES), lambda j, c: (j, 0, 0))
    bias_spec = pl.BlockSpec((2, 1, GQ * BLK, 2 * BLK), lambda j, c: (0, j, 0, 0))
    return pl.pallas_call(
        body, grid=(NKV, S // step_rows), name=f"attn_bwd_d{d}",
        in_specs=[q_like, q_like, pieces, pieces, kv, kv, bias_spec],
        out_specs=[q_like, per_kv],
        out_shape=[_slabs(D // LANES), _slabs(NKV)],
        scratch_shapes=[pltpu.VMEM((S // CHUNK_ROWS, d * pitch, LANES) if apart else (S, LANES), F32)] + apart * [
            pltpu.VMEM((n, d * rows, LANES), F32)
            for n, rows in ((2, pitch), (2, pitch), (1, pitch), (1, pitch), (1, kv_pitch), (1, kv_pitch), (2, pitch))],
        compiler_params=_params(("arbitrary", "arbitrary")),
    )(q, do, lse, delta, k, v, bias)


CONV_T = 256


def _halo_before(i):
    return (jnp.maximum(i * (CONV_T // HALO) - 1, 0), 0)


def _halo_after(i):
    return (jnp.minimum((i + 1) * (CONV_T // HALO), S // HALO - 1), 0)


SUBLANES = 8
NCH = D // LANES
GROUP = SUBLANES * SUBLANES
COMB_STRIDE = 4


def _comb_base(g, b):
    return g * GROUP + (b // COMB_STRIDE) * (SUBLANES * COMB_STRIDE) + b % COMB_STRIDE


def _comb(ref, cb, base):
    return ref[cb, pl.ds(base, SUBLANES, stride=COMB_STRIDE), :]


def _taps(w_ref, cols):
    return [jnp.broadcast_to(w_ref[j:j + 1, cols], (SUBLANES, LANES)) for j in range(CONV_K)]


def _conv_fwd(c_val, c_glu, c_gate, conv_w, conv_b, ln_g, ln_b):
    T = CONV_T

    def body(cv_ref, cg_ref, cvh_ref, cgh_ref, gate_ref, w_ref, b_ref, lg_ref, lb_ref, u_ref, y_ref, win, us):
        i = pl.program_id(0)
        for cb in range(NCH):
            cols = slice(cb * LANES, (cb + 1) * LANES)
            win[cb, HALO:HALO + T, :] = cv_ref[:, cols] * _sigmoid(cg_ref[:, cols])
            win[cb, 0:HALO, :] = jnp.where(i > 0, cvh_ref[:, cols] * _sigmoid(cgh_ref[:, cols]), 0.0)
        for cb in range(NCH):
            cols = slice(cb * LANES, (cb + 1) * LANES)
            taps = _taps(w_ref, cols)
            bias = jnp.broadcast_to(b_ref[:, cols], (SUBLANES, LANES))

            def group(g, carry):
                for b in range(SUBLANES):
                    base = _comb_base(g, b)
                    acc = bias
                    for j in range(CONV_K):
                        acc = acc + taps[j] * _comb(win, cb, base + (HALO - (CONV_K - 1) + j))
                    us[cb, pl.ds(base, SUBLANES, stride=COMB_STRIDE), :] = acc
                return carry

            lax.fori_loop(0, T // GROUP, group, 0, unroll=2)
        total = us[0]
        for cb in range(1, NCH):
            total = total + us[cb]
        mu = jnp.sum(total, axis=-1, keepdims=True) * (1.0 / D)
        sq = jnp.zeros((T, LANES), F32)
        for cb in range(NCH):
            uc = us[cb] - mu
            sq = sq + uc * uc
        rstd = lax.rsqrt(jnp.sum(sq, axis=-1, keepdims=True) * (1.0 / D) + LN_EPS)
        for cb in range(NCH):
            cols = slice(cb * LANES, (cb + 1) * LANES)
            u = us[cb]
            u_ref[:, cols] = u
            nrm = (u - mu) * rstd * lg_ref[:, cols] + lb_ref[:, cols]
            gate = gate_ref[:, cols]
            y_ref[:, cols] = (nrm * _sigmoid(nrm) * (gate * _sigmoid(gate))).astype(BF16)

    halo = pl.BlockSpec((HALO, D), _halo_before)
    return pl.pallas_call(
        body, grid=(S // T,), name="conv_fwd",
        in_specs=[_rows(T, D), _rows(T, D), halo, halo, _rows(T, D),
                  _resident((HALO, D)), _resident((1, D)), _resident((1, D)), _resident((1, D))],
        out_specs=[_rows(T, D), _rows(T, D)],
        out_shape=[jax.ShapeDtypeStruct((S, D), F32), jax.ShapeDtypeStruct((S, D), BF16)],
        scratch_shapes=[pltpu.VMEM((NCH, T + HALO, LANES), F32), pltpu.VMEM((NCH, T, LANES), F32)],
        compiler_params=_params(("arbitrary",)),
    )(c_val, c_glu, c_val, c_glu, c_gate, conv_w, conv_b, ln_g, ln_b)


def _conv_bwd_taps(du, c_val, c_glu, conv_w):
    T = CONV_T
    last = S // T - 1

    def body(du_ref, dua_ref, cv_ref, cg_ref, cvh_ref, cgh_ref, w_ref, dcv_ref, dcg_ref, dw_ref,
             hwin, dwin, dhs, dw_acc):
        i = pl.program_id(0)

        @pl.when(i == 0)
        def _():
            dw_acc[...] = jnp.zeros_like(dw_acc)

        for cb in range(NCH):
            cols = slice(cb * LANES, (cb + 1) * LANES)
            hwin[cb, HALO:HALO + T, :] = cv_ref[:, cols] * _sigmoid(cg_ref[:, cols])
            hwin[cb, 0:HALO, :] = jnp.where(i > 0, cvh_ref[:, cols] * _sigmoid(cgh_ref[:, cols]), 0.0)
            dwin[cb, 0:T, :] = du_ref[:, cols]
            dwin[cb, T:T + HALO, :] = jnp.where(i < last, dua_ref[:, cols], 0.0)
        for cb in range(NCH):
            cols = slice(cb * LANES, (cb + 1) * LANES)
            taps = _taps(w_ref, cols)

            def group_dh(g, carry):
                for b in range(SUBLANES):
                    base = _comb_base(g, b)
                    acc = jnp.zeros((SUBLANES, LANES), F32)
                    for j in range(CONV_K):
                        acc = acc + taps[j] * _comb(dwin, cb, base + (CONV_K - 1 - j))
                    dhs[cb, pl.ds(base, SUBLANES, stride=COMB_STRIDE), :] = acc
                return carry

            lax.fori_loop(0, T // GROUP, group_dh, 0, unroll=2)

            def group_dw(g, sums):
                for b in range(SUBLANES):
                    base = _comb_base(g, b)
                    d = _comb(dwin, cb, base)
                    sums = tuple(sums[j] + d * _comb(hwin, cb, base + (HALO - (CONV_K - 1) + j))
                                 for j in range(CONV_K))
                return sums

            sums = lax.fori_loop(0, T // GROUP, group_dw, tuple(dw_acc[j, :, cols] for j in range(CONV_K)))
            for j in range(CONV_K):
                dw_acc[j, :, cols] = sums[j]
            dh = dhs[cb]
            cv, sg = cv_ref[:, cols], _sigmoid(cg_ref[:, cols])
            dcv_ref[:, cols] = (dh * sg).astype(BF16)
            dcg_ref[:, cols] = (dh * cv * (sg * (1.0 - sg))).astype(BF16)

        @pl.when(i == last)
        def _():
            dw_ref[...] = jnp.zeros_like(dw_ref)
            for j in range(CONV_K):
                dw_ref[j:j + 1, :] = jnp.sum(dw_acc[j], axis=0, keepdims=True)

    before = pl.BlockSpec((HALO, D), _halo_before)
    after = pl.BlockSpec((HALO, D), _halo_after)
    big = jax.ShapeDtypeStruct((S, D), BF16)
    return pl.pallas_call(
        body, grid=(S // T,), name="conv_bwd_taps",
        in_specs=[_rows(T, D), after, _rows(T, D), _rows(T, D), before, before, _resident((HALO, D))],
        out_specs=[_rows(T, D), _rows(T, D), pl.BlockSpec((HALO, D), lambda i: (0, 0))],
        out_shape=[big, big, jax.ShapeDtypeStruct((HALO, D), F32)],
        scratch_shapes=[pltpu.VMEM((NCH, T + HALO, LANES), F32), pltpu.VMEM((NCH, T + HALO, LANES), F32),
                        pltpu.VMEM((NCH, T, LANES), F32), pltpu.VMEM((CONV_K, SUBLANES, D), F32)],
        compiler_params=_params(("arbitrary",)),
    )(du, du, c_val, c_glu, c_val, c_glu, conv_w)


def _outproj_loss(y_att, y_conv, w_out_bf, x, target, gf, u, c_gate, ln_g, ln_b):
    tm = 256

    def body(ya_ref, yc_ref, w_ref, x_ref, t_ref, gf_ref, u_ref, gate_ref, lg_ref, lb_ref,
             dx2_ref, dya_ref, du_ref, dgate_ref, dw_ref, st_ref, acc):
        @pl.when(pl.program_id(0) == 0)
        def _():
            acc[...] = jnp.zeros_like(acc)
            st_ref[...] = jnp.zeros_like(st_ref)

        ya, yc = ya_ref[...], yc_ref[...]
        x2 = x_ref[...] + _dot(ya, w_ref[0:D, :]) + _dot(yc, w_ref[D:2 * D, :])
        r = lax.rsqrt(jnp.mean(x2 * x2, axis=-1, keepdims=True) + NORM_EPS)
        xn = x2 * r
        err = xn * gf_ref[...] - t_ref[...]
        dout = err * (1.0 / D)
        dxn = dout * gf_ref[...]
        dx2 = r * (dxn - xn * jnp.mean(dxn * xn, axis=-1, keepdims=True))
        dx2_ref[...] = dx2
        dx2b = dx2.astype(BF16)
        dya_ref[...] = _dot_nt(dx2b, w_ref[0:D, :])
        dy = _dot_nt(dx2b, w_ref[D:2 * D, :])
        acc[0:D, :] += _dot_tn(ya, dx2b)
        acc[D:2 * D, :] += _dot_tn(yc, dx2b)
        st_ref[ROW_FINAL_G:ROW_FINAL_G + 1, :] += jnp.sum(dout * xn, axis=0, keepdims=True)
        st_ref[ROW_LOSS:ROW_LOSS + 1, :] += jnp.sum(err * err, axis=0, keepdims=True) * (0.5 / D)

        u, gate = u_ref[...], gate_ref[...]
        mu = jnp.mean(u, axis=-1, keepdims=True)
        uc = u - mu
        rstd = lax.rsqrt(jnp.mean(uc * uc, axis=-1, keepdims=True) + LN_EPS)
        z = uc * rstd
        nrm = z * lg_ref[...] + lb_ref[...]
        sn, sg = _sigmoid(nrm), _sigmoid(gate)
        dgate_ref[...] = (dy * (nrm * sn) * (sg * (1.0 + gate * (1.0 - sg)))).astype(BF16)
        dn = dy * (gate * sg) * (sn * (1.0 + nrm * (1.0 - sn)))
        dz = dn * lg_ref[...]
        du = rstd * (dz - jnp.mean(dz, axis=-1, keepdims=True) - z * jnp.mean(dz * z, axis=-1, keepdims=True))
        du_ref[...] = du
        st_ref[ROW_LN_G:ROW_LN_G + 1, :] += jnp.sum(dn * z, axis=0, keepdims=True)
        st_ref[ROW_LN_B:ROW_LN_B + 1, :] += jnp.sum(dn, axis=0, keepdims=True)
        st_ref[ROW_CONV_B:ROW_CONV_B + 1, :] += jnp.sum(du, axis=0, keepdims=True)

        @pl.when(pl.program_id(0) == S // tm - 1)
        def _():
            dw_ref[...] = acc[...].astype(BF16)

    big = jax.ShapeDtypeStruct((S, D), F32)
    vec = _resident((1, D))
    return pl.pallas_call(
        body, grid=(S // tm,), name="outproj_loss",
        in_specs=[_rows(tm, D), _rows(tm, D), _resident((WOUT_ROWS, D)), _rows(tm, D), _rows(tm, D), vec,
                  _rows(tm, D), _rows(tm, D), vec, vec],
        out_specs=[_rows(tm, D), _rows(tm, D), _rows(tm, D), _rows(tm, D),
                   pl.BlockSpec((WOUT_ROWS, D), lambda i: (0, 0)), pl.BlockSpec((8, D), lambda i: (0, 0))],
        out_shape=[big, big, big, jax.ShapeDtypeStruct((S, D), BF16),
                   jax.ShapeDtypeStruct((WOUT_ROWS, D), BF16), jax.ShapeDtypeStruct((8, D), F32)],
        scratch_shapes=[pltpu.VMEM((WOUT_ROWS, D), F32)],
        compiler_params=_params(("arbitrary",)),
    )(y_att, y_conv, w_out_bf, x, target, gf, u, c_gate, ln_g, ln_b)


UNITS_PER_CHUNK = CHUNK // LANES


def _dproj_unit(u, dqs, dkvs, gates, rows):
    if u < OFF_K // LANES:
        return ((dqs[0][u] + dqs[1][u] + dqs[2][u]) * (HD ** -0.5)).astype(BF16)
    if u < OFF_AG // LANES:
        w = u - OFF_K // LANES
        ta, tb = (dkvs[0][j] + dkvs[1][j] + dkvs[2][j] for j in (2 * (w % 2), 2 * (w % 2) + 1))
        low = _low_lanes(rows)
        if w < 2:
            return jnp.where(low, ta, pltpu.roll(tb, HD, axis=1)).astype(BF16)
        return jnp.where(low, pltpu.roll(ta, HD, axis=1), tb).astype(BF16)
    g, sl = divmod(u - OFF_AG // LANES, D // LANES)
    return gates[g][:, sl * LANES:(sl + 1) * LANES]


def _dproj_sources(units, dqs, dkvs, gates, rows):
    use_q = any(u < OFF_K // LANES for u in units)
    use_kv = any(OFF_K // LANES <= u < OFF_AG // LANES for u in units)
    use_g = sorted({(u - OFF_AG // LANES) // (D // LANES) for u in units if u >= OFF_AG // LANES})
    args = (list(dqs) if use_q else []) + (list(dkvs) if use_kv else []) + [gates[g] for g in use_g]
    specs = ([_slab_rows(D // LANES, rows)] * 3 if use_q else []) + ([_slab_rows(NKV, rows)] * 3 if use_kv else []) \
        + [_rows(rows, D)] * len(use_g)

    def pick(refs):
        refs = list(refs)
        q_refs = [refs.pop(0) for _ in range(3)] if use_q else None
        kv_refs = [refs.pop(0) for _ in range(3)] if use_kv else None
        return q_refs, kv_refs, {g: refs.pop(0) for g in use_g}

    return args, specs, pick


def _exchange_results_of(chip_sums):
    n = len(chip_sums) * len(CHIP_FLIPS)
    shapes = [jax.ShapeDtypeStruct((NCHIP,) + tuple(a.shape[1:] if a.ndim == 3 else a.shape), a.dtype)
              for a in chip_sums]
    return shapes, [pltpu.SemaphoreType.DMA((n,)), pltpu.SemaphoreType.DMA((n,))]


def _start_exchange(copies, first_step):
    @pl.when(first_step)
    def _():
        for out, _ in copies:
            out.start()


def _finish_exchange(copies, last_step):
    @pl.when(last_step)
    def _():
        for _, arrival in copies:
            arrival.wait_recv()
        for out, _ in copies:
            out.wait_send()


def _inproj_bwd_x(dqs, dkvs, gates, w_bf, x, g1, dx2, pi):
    tm = 256
    last = S // tm - 1
    units = range(NCOL // LANES)
    pieces, piece_specs, pick = _dproj_sources(units, dqs, dkvs, gates, tm)
    landing, sems = _exchange_results_of([pi])

    def body(*refs):
        piece_refs, refs = refs[:len(pieces)], refs[len(pieces):]
        w_ref, x_ref, g_ref, dx2_ref, pi_ref, gx_ref, st_ref, ri_ref, dp_ref, send, recv = refs
        i = pl.program_id(0)
        copies = _chip_exchange_copies([pi_ref], [ri_ref], send, recv)
        _start_exchange(copies, i == 0)

        @pl.when(i == 0)
        def _():
            st_ref[...] = jnp.zeros_like(st_ref)

        sources = pick(piece_refs)
        for u in units:
            dp_ref[:, u * LANES:(u + 1) * LANES] = _dproj_unit(u, *sources, tm)
        dh = _dot_nt(dp_ref[...], w_ref[...])
        xt = x_ref[...]
        r = lax.rsqrt(jnp.mean(xt * xt, axis=-1, keepdims=True) + NORM_EPS)
        xn = xt * r
        dxn = dh * g_ref[...]
        gx_ref[...] = dx2_ref[...] + r * (dxn - xn * jnp.mean(dxn * xn, axis=-1, keepdims=True))
        st_ref[0:1, :] += jnp.sum(dh * xn, axis=0, keepdims=True)
        _finish_exchange(copies, i == last)

    return pl.pallas_call(
        body, grid=(S // tm,), name="inproj_bwd_x",
        in_specs=piece_specs + [_resident((D, NCOL)), _rows(tm, D), _resident((1, D)), _rows(tm, D), ANY],
        out_specs=[_rows(tm, D), pl.BlockSpec((8, D), lambda i: (0, 0)), ANY],
        out_shape=[jax.ShapeDtypeStruct((S, D), F32), jax.ShapeDtypeStruct((8, D), F32)] + landing,
        scratch_shapes=[pltpu.VMEM((tm, NCOL), BF16)] + sems,
        compiler_params=_params(("arbitrary",)),
    )(*pieces, w_bf, x, g1, dx2, pi)


def _inproj_bwd_w(h, dqs, dkvs, gates):
    out = None
    for k in range(NCHIP):
        units = range(k * UNITS_PER_CHUNK, (k + 1) * UNITS_PER_CHUNK)
        tk = 512 if units[0] < OFF_K // LANES else 1024
        nk = S // tk
        pieces, piece_specs, pick = _dproj_sources(units, dqs, dkvs, gates, tk)
        handed_on = [] if out is None else [out]

        def body(*refs, units=units, pick=pick, n_pieces=len(pieces), n_in=1 + len(pieces) + len(handed_on)):
            h_ref, piece_refs = refs[0], refs[1:1 + n_pieces]
            o_ref, tile, acc = refs[n_in:]
            i = pl.program_id(0)

            @pl.when(i == 0)
            def _():
                acc[...] = jnp.zeros_like(acc)

            sources = pick(piece_refs)
            for n, u in enumerate(units):
                tile[:, n * LANES:(n + 1) * LANES] = _dproj_unit(u, *sources, tk)
            acc[...] += _dot_tn(h_ref[...], tile[...])

            @pl.when(i == nk - 1)
            def _():
                o_ref[0] = acc[...].astype(BF16)

        out = pl.pallas_call(
            body, grid=(nk,), name=f"inproj_bwd_w{k}",
            in_specs=[_rows(tk, D)] + piece_specs + [ANY] * len(handed_on),
            out_specs=pl.BlockSpec((1, D, CHUNK), lambda i, k=k: (k, 0, 0)),
            out_shape=jax.ShapeDtypeStruct((NCHIP, D, CHUNK), BF16),
            input_output_aliases={1 + len(pieces): 0} if handed_on else {},
            scratch_shapes=[pltpu.VMEM((tk, CHUNK), BF16), pltpu.VMEM((D, CHUNK), F32)],
            compiler_params=_params(("arbitrary",)),
        )(h, *pieces, *handed_on)
    return out


ROW_FINAL_G, ROW_LOSS, ROW_LN_G, ROW_LN_B, ROW_CONV_B, ROW_TAPS = 0, 1, 2, 3, 4, 8
SMALL_ROWS = 8 + HALO
NDEV = 8


MESH = pl.DeviceIdType.MESH
ANY = pl.BlockSpec(memory_space=pl.ANY)
CHIP_FLIPS = ((1, 0), (0, 1), (1, 1))


def _pos():
    return lax.axis_index("x"), lax.axis_index("y"), lax.axis_index("c")


def _flip(v, f):
    return 1 - v if f else v


def _ds(start, size, align=None):
    return pl.ds(pl.multiple_of(start, align or size), size)


def _place_shards(wi, wo, cw, where):
    steps = 4

    def body(where_ref, wi_ref, wo_ref, cw_ref, wi_full, wo_full, cw_full):
        wi_full[...] = wi_ref[...].astype(BF16)
        wo_full[...] = wo_ref[...].astype(BF16)
        cw_full[...] = cw_ref[...]

    grid_spec = pltpu.PrefetchScalarGridSpec(
        num_scalar_prefetch=1, grid=(steps,),
        in_specs=[pl.BlockSpec((D // steps, CHUNK), lambda i, w: (i, 0)),
                  pl.BlockSpec((WOUT_SHARD // steps, D), lambda i, w: (i, 0)),
                  pl.BlockSpec((HALO, CONVW_SHARD), lambda i, w: (0, 0))],
        out_specs=[pl.BlockSpec((D // steps, CHUNK), lambda i, w: (i, w[0])),
                   pl.BlockSpec((WOUT_SHARD // steps, D), lambda i, w: (w[0] * steps + i, 0)),
                   pl.BlockSpec((HALO, CONVW_SHARD), lambda i, w: (0, w[0]))])
    return pl.pallas_call(
        body, grid_spec=grid_spec, name="place_shards",
        out_shape=[jax.ShapeDtypeStruct((D, NCOL), BF16), jax.ShapeDtypeStruct((WOUT_ROWS, D), BF16),
                   jax.ShapeDtypeStruct((HALO, D), F32)],
        compiler_params=_params(("arbitrary",)),
    )(where, wi, wo, cw)


W_IN, W_OUT, TAPS = range(3)
GATHER_SEMS = 12


def _gather_stages(fulls, send, recv):
    halves = {W_IN: D // 2, W_OUT: WOUT_SHARD // 2, TAPS: HALO // 2}
    x, y, c = _pos()
    chips = {"me": (x, y), "x": (1 - x, y), "y": (x, 1 - y), "diag": (1 - x, 1 - y)}
    SENT = ((("me", 0), "x"), (("me", 1), "x"), (("me", 1), "y"), (("me", 0), "y"), (("x", 0), "y"), (("y", 1), "x"))
    LANDS = ((("x", 0), "x"), (("x", 1), "x"), (("y", 1), "y"), (("y", 0), "y"), (("diag", 0), "y"), (("diag", 1), "x"))
    N_ICI = len(SENT)

    def region(n_th, whose, half, part):
        a, full = fulls[n_th]
        chip = 2 * chips[whose][0] + chips[whose][1]
        n = halves[a] // 2
        row = half * halves[a] + part * n
        if a == W_IN:
            return full.at[_ds(row, n), _ds(chip * CHUNK, CHUNK, 128)]
        if a == W_OUT:
            return full.at[_ds(chip * WOUT_SHARD + row, n), :]
        return full.at[_ds(row, n), _ds(chip * CONVW_SHARD, CONVW_SHARD, 128)]

    def copy(n_th, kind, piece, dev):
        k = GATHER_SEMS * n_th + kind
        return pltpu.make_async_remote_copy(src_ref=piece, dst_ref=piece, send_sem=send.at[k], recv_sem=recv.at[k],
                                            device_id=dev, device_id_type=MESH)

    def sent(a, k):
        if k < N_ICI:
            (whose, part), to = SENT[k]
            return copy(a, k, region(a, whose, c, part), (*chips[to], c))
        (whose, part), _ = LANDS[k - N_ICI]
        return copy(a, k, region(a, whose, c, part), (x, y, 1 - c))

    def wait_arrival(a, k):
        if k < N_ICI:
            (whose, part), frm = LANDS[k]
            copy(a, k, region(a, whose, c, part), (*chips[frm], c)).wait_recv()
        else:
            (whose, part), _ = LANDS[k - N_ICI]
            copy(a, k, region(a, whose, 1 - c, part), (x, y, 1 - c)).wait_recv()

    arrays = range(len(fulls))

    def own_to_neighbours():
        for a in arrays:
            for k in (0, 2, 1, 3):
                sent(a, k).start()

    def pass_on_neighbours():
        for a in arrays:
            for k, onward in ((0, 4), (2, 5), (1, None), (3, None)):
                wait_arrival(a, k)
                if onward is not None:
                    sent(a, onward).start()
                sent(a, k + N_ICI).start()

    def pass_on_diagonal():
        for a in arrays:
            for k in (4, 5):
                wait_arrival(a, k)
                sent(a, k + N_ICI).start()

    def finish():
        for a in arrays:
            for k in range(N_ICI, 2 * N_ICI):
                wait_arrival(a, k)
            for k in range(2 * N_ICI):
                sent(a, k).wait_send()

    return own_to_neighbours, pass_on_neighbours, pass_on_diagonal, finish


def _gather_sems(n_arrays):
    return [pltpu.SemaphoreType.DMA((GATHER_SEMS * n_arrays,)), pltpu.SemaphoreType.DMA((GATHER_SEMS * n_arrays,))]


def _gather_w_in(wi_full):
    def body(_wi, full, send, recv):
        for stage in _gather_stages([(W_IN, full)], send, recv):
            stage()

    return pl.pallas_call(
        body, name="gather_w_in", in_specs=[ANY], out_specs=ANY, input_output_aliases={0: 0},
        out_shape=jax.ShapeDtypeStruct((D, NCOL), BF16), scratch_shapes=_gather_sems(1),
    )(wi_full)


def _half_shape(a):
    return jax.ShapeDtypeStruct((NCHIP, a.shape[1] // 2, a.shape[2]) if a.ndim == 3 else a.shape, a.dtype)


def _exchange_halves(arrays, name):
    n = len(arrays)

    def body(*refs):
        srcs, dsts, (send, recv) = refs[:n], refs[n:2 * n], refs[2 * n:]
        x, y, c = _pos()
        cps = []
        for k, (s_, d_) in enumerate(zip(srcs, dsts)):
            if len(s_.shape) == 3:
                h = s_.shape[1] // 2
                s_ = s_.at[:, _ds((1 - c) * h, h), :]
            cps.append(pltpu.make_async_remote_copy(src_ref=s_, dst_ref=d_, send_sem=send.at[k], recv_sem=recv.at[k],
                                                    device_id=(x, y, 1 - c), device_id_type=MESH))
        for cp in cps:
            cp.start()
        for cp in cps:
            cp.wait()

    return pl.pallas_call(
        body, name=name, in_specs=[ANY] * n, out_specs=[ANY] * n, out_shape=[_half_shape(a) for a in arrays],
        scratch_shapes=[pltpu.SemaphoreType.DMA((n,)), pltpu.SemaphoreType.DMA((n,))],
    )(*arrays)


def _add_halves(arrays, received, name):
    n = len(arrays)

    def body(*refs):
        mine, theirs, outs = refs[:n], refs[n:2 * n], refs[2 * n:]
        c = lax.axis_index("c")
        for m_, t_, o_ in zip(mine, theirs, outs):
            if len(m_.shape) == 3:
                h = m_.shape[1] // 2
                o_[0] = (m_[0, _ds(c * h, h), :].astype(F32) + t_[0].astype(F32)).astype(o_.dtype)
            else:
                o_[...] = m_[...] + t_[...]

    def spec(shape):
        if len(shape) == 3:
            return pl.BlockSpec((1,) + tuple(shape[1:]), lambda k: (k, 0, 0))
        return pl.BlockSpec(tuple(shape), lambda k: (0, 0))

    halves = [_half_shape(a) for a in arrays]
    return pl.pallas_call(
        body, grid=(NCHIP,), name=name,
        in_specs=[spec(a.shape) for a in arrays] + [spec(h.shape) for h in halves],
        out_specs=[spec(h.shape) for h in halves], out_shape=halves,
        compiler_params=_params(("arbitrary",)),
    )(*arrays, *received)


def _chip_exchange_copies(srcs, dsts, send, recv):
    x, y, c = _pos()
    me = 2 * x + y
    pairs = []
    for a in range(len(srcs)):
        for j, (fx, fy) in enumerate(CHIP_FLIPS):
            px, py = _flip(x, fx), _flip(y, fy)
            peer = 2 * px + py
            k = len(CHIP_FLIPS) * a + j
            out = pltpu.make_async_remote_copy(
                src_ref=srcs[a].at[peer] if len(srcs[a].shape) == 3 else srcs[a], dst_ref=dsts[a].at[me],
                send_sem=send.at[k], recv_sem=recv.at[k], device_id=(px, py, c), device_id_type=MESH)
            got = dsts[a].at[peer]
            arrival = pltpu.make_async_remote_copy(
                src_ref=got, dst_ref=got, send_sem=send.at[k], recv_sem=recv.at[k],
                device_id=(px, py, c), device_id_type=MESH)
            pairs.append((out, arrival))
    return pairs


def _sum_chips(ri, ro, rs, pi, po, ps, where):
    def body(w_ref, ri_ref, ro_ref, rs_ref, pi_ref, po_ref, ps_ref, gi_ref, go_ref, gs_ref, g5_ref, loss_ref,
             acc_i, acc_o, acc_s):
        k = pl.program_id(0)
        accs = (acc_i, acc_o, acc_s)

        @pl.when(k == 0)
        def _():
            for acc in accs:
                acc[...] = jnp.zeros_like(acc)

        @pl.when(k == w_ref[0])
        def _():
            for acc, val in zip(accs, (pi_ref[0], po_ref[0], ps_ref[...])):
                acc[...] += val.astype(F32)

        @pl.when(k != w_ref[0])
        def _():
            for acc, ref in zip(accs, (ri_ref, ro_ref, rs_ref)):
                acc[...] += ref[0].astype(F32)

        @pl.when(k == NCHIP - 1)
        def _():
            gi_ref[0] = acc_i[...]
            go_ref[0] = acc_o[...]
            gs_ref[...] = acc_s[...]
            g5_ref[...] = jnp.zeros_like(g5_ref)
            for i, row in enumerate((ROW_CONV_B, ROW_LN_G, ROW_LN_B, ROW_FINAL_G)):
                g5_ref[i + 1:i + 2, :] = acc_s[row:row + 1, :]
            loss = jnp.sum(acc_s[ROW_LOSS:ROW_LOSS + 1, :], axis=1, keepdims=True)
            loss_ref[...] = jnp.broadcast_to(loss, loss_ref.shape)

    def sent(k, w):
        return jnp.where(k == w[0], (k + 1) % NCHIP, k)

    hi, ho = D // 2, WOUT_SHARD // 2
    const = lambda shape: pl.BlockSpec(shape, lambda k, w: (0,) * len(shape))
    grid_spec = pltpu.PrefetchScalarGridSpec(
        num_scalar_prefetch=1, grid=(NCHIP,),
        in_specs=[pl.BlockSpec((1, hi, CHUNK), lambda k, w: (sent(k, w), 0, 0)),
                  pl.BlockSpec((1, ho, D), lambda k, w: (sent(k, w), 0, 0)),
                  pl.BlockSpec((1, SMALL_ROWS, D), lambda k, w: (sent(k, w), 0, 0)),
                  pl.BlockSpec((1, hi, CHUNK), lambda k, w: (w[0], 0, 0)),
                  pl.BlockSpec((1, ho, D), lambda k, w: (w[0], 0, 0)),
                  const((SMALL_ROWS, D))],
        out_specs=[pl.BlockSpec((1, hi, CHUNK), lambda k, w: (w[1], 0, 0)),
                   pl.BlockSpec((1, ho, D), lambda k, w: (w[1], 0, 0)),
                   const((SMALL_ROWS, D)), const((8, D)), const((8, LANES))],
        scratch_shapes=[pltpu.VMEM((hi, CHUNK), F32), pltpu.VMEM((ho, D), F32), pltpu.VMEM((SMALL_ROWS, D), F32)])
    return pl.pallas_call(
        body, grid_spec=grid_spec, name="sum_chips",
        out_shape=[jax.ShapeDtypeStruct((2, hi, CHUNK), F32), jax.ShapeDtypeStruct((2, ho, D), F32),
                   jax.ShapeDtypeStruct((SMALL_ROWS, D), F32), jax.ShapeDtypeStruct((8, D), F32),
                   jax.ShapeDtypeStruct((8, LANES), F32)],
        compiler_params=_params(("arbitrary",)),
    )(where, ri, ro, rs, pi, po, ps)


def _exchange_results(gi2, go2, st):
    flips = [(fx, fy, fc) for fx in (0, 1) for fy in (0, 1) for fc in (0, 1)][1:]

    def body(_gi, _go, st_ref, gi_ref, go_ref, all_ref, send, recv, lsem):
        x, y, c = _pos()
        sib = (x, y, 1 - c)

        def half(k, ref, slot):
            return pltpu.make_async_remote_copy(src_ref=ref.at[slot], dst_ref=ref.at[slot], send_sem=send.at[k],
                                                recv_sem=recv.at[k], device_id=sib, device_id_type=MESH)

        def stat(k, src, slot, dev):
            return pltpu.make_async_remote_copy(src_ref=src, dst_ref=all_ref.at[slot], send_sem=send.at[k],
                                                recv_sem=recv.at[k], device_id=dev, device_id_type=MESH)

        mine = pltpu.make_async_copy(st_ref, all_ref.at[4 * x + 2 * y + c], lsem)
        mine.start()
        sends = [half(k, ref, c) for k, ref in enumerate((gi_ref, go_ref))]
        peers = [(_flip(x, fx), _flip(y, fy), _flip(c, fc)) for fx, fy, fc in flips]
        sends += [stat(2 + k, st_ref, 4 * x + 2 * y + c, dev) for k, dev in enumerate(peers)]
        for cp in sends:
            cp.start()
        for k, ref in enumerate((gi_ref, go_ref)):
            half(k, ref, 1 - c).wait_recv()
        for k, (px, py, pc) in enumerate(peers):
            slot = 4 * px + 2 * py + pc
            stat(2 + k, all_ref.at[slot], slot, (px, py, pc)).wait_recv()
        for cp in sends:
            cp.wait_send()
        mine.wait()

    n = 2 + len(flips)
    return pl.pallas_call(
        body, name="exchange_results",
        in_specs=[ANY, ANY, ANY], out_specs=[ANY, ANY, ANY], input_output_aliases={0: 0, 1: 1},
        out_shape=[jax.ShapeDtypeStruct((2, D // 2, CHUNK), F32), jax.ShapeDtypeStruct((2, WOUT_SHARD // 2, D), F32),
                   jax.ShapeDtypeStruct((NDEV, 8, D), F32)],
        scratch_shapes=[pltpu.SemaphoreType.DMA((n,)), pltpu.SemaphoreType.DMA((n,)), pltpu.SemaphoreType.DMA],
    )(gi2, go2, st)


def _adamw_math(w, g, m, v):
    m2 = ADAM_B1 * m + (1.0 - ADAM_B1) * g
    v2 = ADAM_B2 * v + (1.0 - ADAM_B2) * (g * g)
    m_hat = m2 / (1.0 - ADAM_B1 ** ADAM_STEP)
    v_hat = v2 / (1.0 - ADAM_B2 ** ADAM_STEP)
    delta = -ADAM_LR * (m_hat / (jnp.sqrt(v_hat) + ADAM_EPS) + ADAM_WD * w)
    return delta, m2, v2


def _adamw(w, g, m, v, name):
    rows, cols = w.shape
    tm = 256 if rows % 256 == 0 else rows

    def body(w_ref, g_ref, m_ref, v_ref, d_ref, m2_ref, v2_ref):
        d_ref[...], m2_ref[...], v2_ref[...] = _adamw_math(w_ref[...], g_ref[...], m_ref[...], v_ref[...])

    shape = jax.ShapeDtypeStruct(w.shape, F32)
    return pl.pallas_call(
        body, grid=(rows // tm,), name=name,
        in_specs=[_rows(tm, cols)] * 4, out_specs=[_rows(tm, cols)] * 3, out_shape=[shape] * 3,
        compiler_params=_params(("arbitrary",)),
    )(w, g, m, v)


def _adamw_vectors(g5, first_parts, ws, ms, vs):
    n = len(ws)

    def body(g_ref, parts_ref, *refs):
        ins, g0_ref, outs = refs[:3 * n], refs[3 * n], refs[3 * n + 1:]
        g0 = parts_ref[0, 0:1, :]
        for dev in range(1, NDEV):
            g0 = g0 + parts_ref[dev, 0:1, :]
        g0_ref[...] = g0
        for i in range(n):
            g = g0 if i == 0 else g_ref[i:i + 1, :]
            res = _adamw_math(ins[i][...], g, ins[n + i][...], ins[2 * n + i][...])
            for kind in range(3):
                outs[kind * n + i][...] = res[kind]

    shape = jax.ShapeDtypeStruct((1, D), F32)
    return pl.pallas_call(body, name="adamw_vectors", out_shape=[shape] * (1 + 3 * n), compiler_params=_params())(
        g5, first_parts, *ws, *ms, *vs)


def kernel(x, norm_g, w_in, conv_w, conv_b, conv_ln_g, conv_ln_b, w_out, final_norm_g, loss_target, m_norm_g, m_w_in, m_conv_w, m_conv_b, m_conv_ln_g, m_conv_ln_b, m_w_out, m_final_norm_g, v_norm_g, v_w_in, v_conv_w, v_conv_b, v_conv_ln_g, v_conv_ln_b, v_w_out, v_final_norm_g):
    chip = 2 * lax.axis_index("x") + lax.axis_index("y")
    where = jnp.stack([chip, lax.axis_index("c")]).astype(jnp.int32)
    taps_shard = jnp.pad(conv_w[0], ((0, HALO - CONV_K), (0, 0)))
    wi_full, wo_full, cw_full = _place_shards(w_in[0], w_out[0], taps_shard, where)
    wi_full = _gather_w_in(wi_full)

    gf = final_norm_g[None]
    xb = x[0]
    h, q, k, v, a_gate, c_val, c_glu, c_gate, wo_full, cw_full = _inproj_fwd(xb, norm_g, wi_full, wo_full, cw_full)
    tables = [_bias_table(d) for d in PATTERNS]
    o, lse, y_att = _attn_fwd(q, k, v, tables, a_gate)
    u, y_conv = _conv_fwd(c_val, c_glu, c_gate, cw_full, conv_b, conv_ln_g, conv_ln_b)
    dx2, dy_att, du, dc_gate, dw_out, st_out = _outproj_loss(
        y_att, y_conv, wo_full, xb, loss_target[0], gf, u, c_gate, conv_ln_g, conv_ln_b)
    dc_val, dc_glu, dconv_w = _conv_bwd_taps(du, c_val, c_glu, cw_full)

    early = [dw_out.reshape(NCHIP, WOUT_SHARD, D), jnp.concatenate([st_out, dconv_w], axis=0)]
    po, ps = _add_halves(early, _exchange_halves(early, "exchange_halves_early"), "add_halves_early")
    do, da_gate, delta, ro, rs = _attn_gate_bwd(dy_att, o, a_gate, _head_sum_selectors(), [po, ps])
    dqs, dkvs = zip(*[_attn_bwd(q, k, v, do, lse, delta, t, d) for t, d in zip(tables, PATTERNS)])

    dproj_pieces = (dqs, dkvs, (da_gate, dc_val, dc_glu, dc_gate))
    late = [_inproj_bwd_w(h, *dproj_pieces)]
    (pi,) = _add_halves(late, _exchange_halves(late, "exchange_halves"), "add_halves")
    grad_x, st_in, ri = _inproj_bwd_x(*dproj_pieces, wi_full, xb, norm_g, dx2, pi)
    gi2, go2, g_small, g5, loss8 = _sum_chips(ri, ro, rs, pi, po, ps, where)
    gi2, go2, norm_g_parts = _exchange_results(gi2, go2, st_in)
    g_w_in = gi2.reshape(D, CHUNK)
    g_w_out = go2.reshape(WOUT_SHARD, D)
    g_taps = lax.dynamic_slice(g_small, (ROW_TAPS, chip * CONVW_SHARD), (CONV_K, CONVW_SHARD))

    d_w_in, m2_w_in, v2_w_in = _adamw(w_in[0], g_w_in, m_w_in[0], v_w_in[0], "adamw_w_in")
    d_w_out, m2_w_out, v2_w_out = _adamw(w_out[0], g_w_out, m_w_out[0], v_w_out[0], "adamw_w_out")
    d_taps, m2_taps, v2_taps = _adamw(conv_w[0], g_taps, m_conv_w[0], v_conv_w[0], "adamw_conv_w")
    g_norm, *vec = _adamw_vectors(
        g5, norm_g_parts,
        (norm_g, conv_b, conv_ln_g, conv_ln_b, gf),
        (m_norm_g, m_conv_b, m_conv_ln_g, m_conv_ln_b, m_final_norm_g[None]),
        (v_norm_g, v_conv_b, v_conv_ln_g, v_conv_ln_b, v_final_norm_g[None]))
    d_vec, m2_vec, v2_vec = vec[0:5], vec[5:10], vec[10:15]

    def weight_order(ng, wi, cw, cb, lg, lb, wo, fg):
        return (ng, wi[None], cw[None], cb, lg, lb, wo[None], fg[0])

    grads = weight_order(g_norm, g_w_in, g_taps, g5[1:2], g5[2:3], g5[3:4], g_w_out, g5[4:5])
    deltas = weight_order(d_vec[0], d_w_in, d_taps, d_vec[1], d_vec[2], d_vec[3], d_w_out, d_vec[4])
    new_m = weight_order(m2_vec[0], m2_w_in, m2_taps, m2_vec[1], m2_vec[2], m2_vec[3], m2_w_out, m2_vec[4])
    new_v = weight_order(v2_vec[0], v2_w_in, v2_taps, v2_vec[1], v2_vec[2], v2_vec[3], v2_w_out, v2_vec[4])
    return (loss8[0, 0], grad_x[None], *grads, *deltas, *new_m, *new_v)
```
